```python
import math
import jax, jax.numpy as jnp
from jax import lax
import numpy as np

D_MODEL = 1024
BATCH = 8
SEQ = 2048
DEPTH = 2

N_A_LAYERS = DEPTH // 2
N_B_LAYERS = DEPTH - N_A_LAYERS

SSM_EXPAND = 2
D_INNER = SSM_EXPAND * D_MODEL
SSM_HEAD_DIM = 64
SSM_HEADS = D_INNER // SSM_HEAD_DIM
SSM_GROUPS = 4
SSM_STATE = 128
CONV_WIDTH = 4
CHUNK = 128
GN = SSM_GROUPS * SSM_STATE
CONV_DIM = D_INNER + 2 * GN
IN_PROJ_DIM = D_INNER + CONV_DIM + SSM_HEADS

ATT_HEAD_DIM = 64
N_Q_HEADS = D_MODEL // ATT_HEAD_DIM
N_KV_HEADS = 4
WINDOW = 128
ROPE_THETA = 10000.0

D_FF = 2816
FFN_RES_WEIGHT = 0.5
EPS = 1e-5

kernel_name = 'yoco_ssd_swa_sink_macaron'


def rmsnorm(x, w):
    xf = x.astype(jnp.float32)
    xf = xf * lax.rsqrt(jnp.mean(xf * xf, axis=-1, keepdims=True) + EPS)
    return (xf * w.astype(jnp.float32)).astype(x.dtype)


def swiglu(h, w_gate, w_up, w_down):
    return (jax.nn.silu(h @ w_gate) * (h @ w_up)) @ w_down


def rope_tables(seqlen):
    pos = jnp.arange(seqlen, dtype=jnp.float32)
    inv = 1.0 / (ROPE_THETA ** (jnp.arange(0, ATT_HEAD_DIM, 2, dtype=jnp.float32) / ATT_HEAD_DIM))
    ang = pos[:, None] * inv[None, :]
    return jnp.cos(ang), jnp.sin(ang)


def apply_rope(t, cos, sin):
    tf = t.astype(jnp.float32)
    t1, t2 = jnp.split(tf, 2, axis=-1)
    c = cos[:, None, :]
    s = sin[:, None, :]
    return jnp.concatenate([t1 * c - t2 * s, t2 * c + t1 * s], axis=-1).astype(t.dtype)


def causal_depthwise_conv(u, w, b):
    out = lax.conv_general_dilated(
        u, w[:, None, :].astype(u.dtype), window_strides=(1,),
        padding=[(CONV_WIDTH - 1, 0)],
        dimension_numbers=('NWC', 'WIO', 'NWC'),
        feature_group_count=u.shape[-1])
    return out + b


def segsum(a):
    cs = jnp.cumsum(a, axis=-1)
    diff = cs[..., :, None] - cs[..., None, :]
    t = a.shape[-1]
    mask = jnp.tril(jnp.ones((t, t), dtype=bool))
    return jnp.where(mask, diff, -jnp.inf)


def ssd_chunked(xdt, a, b_ssm, c_ssm):
    bsz, seqlen, _, _ = xdt.shape
    nc = seqlen // CHUNK
    r = SSM_HEADS // SSM_GROUPS
    x = xdt.reshape(bsz, nc, CHUNK, SSM_GROUPS, r, SSM_HEAD_DIM)
    a = a.reshape(bsz, nc, CHUNK, SSM_GROUPS, r).transpose(0, 3, 4, 1, 2)
    bc = b_ssm.reshape(bsz, nc, CHUNK, SSM_GROUPS, SSM_STATE)
    cc = c_ssm.reshape(bsz, nc, CHUNK, SSM_GROUPS, SSM_STATE)
    a_cs = jnp.cumsum(a, axis=-1)
    decay_in = jnp.exp(segsum(a))
    cb = jnp.einsum('bclgn,bcsgn->bcgls', cc, bc)
    y_diag = jnp.einsum('bcgls,bgrcls,bcsgrp->bclgrp', cb, decay_in, x)
    decay_states = jnp.exp(a_cs[..., -1:] - a_cs)
    states = jnp.einsum('bclgn,bgrcl,bclgrp->bcgrpn', bc, decay_states, x)
    chunk_decay = jnp.exp(a_cs[..., -1])
    states_c = jnp.moveaxis(states, 1, 0)
    decay_c = jnp.moveaxis(chunk_decay, 3, 0)

    def step(carry, inp):
        s, d = inp
        return carry * d[..., None, None] + s, carry

    _, prev = lax.scan(step, jnp.zeros_like(states_c[0]), (states_c, decay_c))
    prev = jnp.moveaxis(prev, 0, 1)
    decay_out = jnp.exp(a_cs)
    y_off = jnp.einsum('bclgn,bcgrpn,bgrcl->bclgrp', cc, prev, decay_out)
    return (y_diag + y_off).reshape(bsz, seqlen, SSM_HEADS, SSM_HEAD_DIM)


def mamba2_mixer(h, w_in, conv_w, conv_b, dt_bias, a_log, d_skip, norm_w, w_out):
    bsz, seqlen, _ = h.shape
    zxbcdt = h @ w_in
    z = zxbcdt[..., :D_INNER]
    xbc = zxbcdt[..., D_INNER:D_INNER + CONV_DIM]
    dt_raw = zxbcdt[..., D_INNER + CONV_DIM:]
    xbc = jax.nn.silu(causal_depthwise_conv(xbc, conv_w, conv_b))
    xs = xbc[..., :D_INNER].reshape(bsz, seqlen, SSM_HEADS, SSM_HEAD_DIM).astype(jnp.float32)
    b_ssm = xbc[..., D_INNER:D_INNER + GN].reshape(bsz, seqlen, SSM_GROUPS, SSM_STATE).astype(jnp.float32)
    c_ssm = xbc[..., D_INNER + GN:].reshape(bsz, seqlen, SSM_GROUPS, SSM_STATE).astype(jnp.float32)
    dt = jax.nn.softplus(dt_raw.astype(jnp.float32) + dt_bias.astype(jnp.float32))
    a = -jnp.exp(a_log.astype(jnp.float32))
    y = ssd_chunked(xs * dt[..., None], dt * a, b_ssm, c_ssm)
    y = y + xs * d_skip.astype(jnp.float32)[:, None]
    y = y.reshape(bsz, seqlen, D_INNER) * jax.nn.silu(z.astype(jnp.float32))
    yg = y.reshape(bsz, seqlen, SSM_GROUPS, D_INNER // SSM_GROUPS)
    yg = yg * lax.rsqrt(jnp.mean(yg * yg, axis=-1, keepdims=True) + EPS)
    y = (yg.reshape(bsz, seqlen, D_INNER) * norm_w.astype(jnp.float32)).astype(h.dtype)
    return y @ w_out


def shared_kv(x, kv_norm_w, w_k, b_k, w_v, b_v, cos, sin):
    bsz, seqlen, _ = x.shape
    hkv = rmsnorm(x, kv_norm_w)
    k = (hkv @ w_k + b_k).reshape(bsz, seqlen, N_KV_HEADS, ATT_HEAD_DIM)
    v = (hkv @ w_v + b_v).reshape(bsz, seqlen, N_KV_HEADS, ATT_HEAD_DIM)
    return apply_rope(k, cos, sin), v


def band_blocks(t):
    prev = jnp.pad(t[:, :-1], ((0, 0), (1, 0), (0, 0), (0, 0), (0, 0)))
    return jnp.concatenate([prev, t], axis=2)


def swa_sink_attention(h, k_rot, v, w_q, b_q, sinks, w_o, b_o, cos, sin):
    bsz, seqlen, _ = h.shape
    nb = seqlen // WINDOW
    grp = N_Q_HEADS // N_KV_HEADS
    q = (h @ w_q + b_q).reshape(bsz, seqlen, N_Q_HEADS, ATT_HEAD_DIM)
    q = apply_rope(q, cos, sin).reshape(bsz, nb, WINDOW, N_KV_HEADS, grp, ATT_HEAD_DIM)
    k_band = band_blocks(k_rot.reshape(bsz, nb, WINDOW, N_KV_HEADS, ATT_HEAD_DIM))
    v_band = band_blocks(v.reshape(bsz, nb, WINDOW, N_KV_HEADS, ATT_HEAD_DIM))
    scale = 1.0 / math.sqrt(ATT_HEAD_DIM)
    scores = jnp.einsum('bnqkgd,bnskd->bnkgqs', q, k_band,
                        preferred_element_type=jnp.float32) * scale
    qpos = jnp.arange(WINDOW)[:, None] + WINDOW
    kpos = jnp.arange(2 * WINDOW)[None, :]
    in_window = (kpos <= qpos) & (kpos > qpos - WINDOW)
    has_prev = (jnp.arange(nb) > 0)[:, None, None] | (kpos >= WINDOW)[None]
    mask = in_window[None] & has_prev
    scores = jnp.where(mask[None, :, None, None], scores, -jnp.inf)
    sink = sinks.astype(jnp.float32).reshape(N_KV_HEADS, grp)[None, None, :, :, None]
    m = jnp.maximum(scores.max(axis=-1), sink)
    p = jnp.exp(scores - m[..., None])
    probs = p / (p.sum(axis=-1) + jnp.exp(sink - m))[..., None]
    out = jnp.einsum('bnkgqs,bnskd->bnqkgd', probs.astype(v.dtype), v_band)
    out = out.reshape(bsz, seqlen, N_Q_HEADS * ATT_HEAD_DIM)
    return out @ w_o + b_o


def _fwd_setup_inputs(seed: int = 0) -> dict:
    key = jax.random.key(seed)
    ks = jax.random.split(key, 24)
    f32 = jnp.float32

    def nrm(k, shape, scale):
        return jax.random.normal(k, shape, f32) * scale

    dt0 = jnp.exp(jax.random.uniform(ks[8], (N_A_LAYERS, SSM_HEADS), f32,
                                     math.log(1e-3), math.log(1e-1)))
    return {
        'x': nrm(ks[0], (BATCH, SEQ, D_MODEL), 1.0),
        'norm_w': 1.0 + nrm(ks[1], (DEPTH, 3, D_MODEL), 0.01),
        'ffn_w_gate': nrm(ks[2], (DEPTH, 2, D_MODEL, D_FF), D_MODEL ** -0.5),
        'ffn_w_up': nrm(ks[3], (DEPTH, 2, D_MODEL, D_FF), D_MODEL ** -0.5),
        'ffn_w_down': nrm(ks[4], (DEPTH, 2, D_FF, D_MODEL), D_FF ** -0.5),
        'ssm_w_in': nrm(ks[5], (N_A_LAYERS, D_MODEL, IN_PROJ_DIM), D_MODEL ** -0.5),
        'ssm_conv_w': nrm(ks[6], (N_A_LAYERS, CONV_WIDTH, CONV_DIM), CONV_WIDTH ** -0.5),
        'ssm_conv_b': nrm(ks[7], (N_A_LAYERS, CONV_DIM), 0.01),
        'ssm_dt_bias': dt0 + jnp.log(-jnp.expm1(-dt0)),
        'ssm_a_log': jnp.log(jax.random.uniform(ks[9], (N_A_LAYERS, SSM_HEADS), f32, 1.0, 16.0)),
        'ssm_d': 1.0 + nrm(ks[10], (N_A_LAYERS, SSM_HEADS), 0.01),
        'ssm_norm_w': 1.0 + nrm(ks[11], (N_A_LAYERS, D_INNER), 0.01),
        'ssm_w_out': nrm(ks[12], (N_A_LAYERS, D_INNER, D_MODEL), D_INNER ** -0.5),
        'kv_norm_w': 1.0 + nrm(ks[13], (D_MODEL,), 0.01),
        'w_k': nrm(ks[14], (D_MODEL, N_KV_HEADS * ATT_HEAD_DIM), D_MODEL ** -0.5),
        'b_k': nrm(ks[15], (N_KV_HEADS * ATT_HEAD_DIM,), 0.01),
        'w_v': nrm(ks[16], (D_MODEL, N_KV_HEADS * ATT_HEAD_DIM), D_MODEL ** -0.5),
        'b_v': nrm(ks[17], (N_KV_HEADS * ATT_HEAD_DIM,), 0.01),
        'attn_w_q': nrm(ks[18], (N_B_LAYERS, D_MODEL, N_Q_HEADS * ATT_HEAD_DIM), D_MODEL ** -0.5),
        'attn_b_q': nrm(ks[19], (N_B_LAYERS, N_Q_HEADS * ATT_HEAD_DIM), 0.01),
        'attn_sinks': nrm(ks[20], (N_B_LAYERS, N_Q_HEADS), 0.5),
        'attn_w_o': nrm(ks[21], (N_B_LAYERS, N_Q_HEADS * ATT_HEAD_DIM, D_MODEL),
                        (N_Q_HEADS * ATT_HEAD_DIM) ** -0.5),
        'attn_b_o': nrm(ks[22], (N_B_LAYERS, D_MODEL), 0.01),
        'final_norm_w': 1.0 + nrm(ks[23], (D_MODEL,), 0.01),
    }


def _fwd_reference(x, norm_w, ffn_w_gate, ffn_w_up, ffn_w_down,
              ssm_w_in, ssm_conv_w, ssm_conv_b, ssm_dt_bias, ssm_a_log, ssm_d, ssm_norm_w, ssm_w_out,
              kv_norm_w, w_k, b_k, w_v, b_v,
              attn_w_q, attn_b_q, attn_sinks, attn_w_o, attn_b_o,
              final_norm_w):
    cos, sin = rope_tables(x.shape[1])
    k_shared = None
    v_shared = None
    for layer in range(DEPTH):
        if layer == N_A_LAYERS:
            k_shared, v_shared = shared_kv(x, kv_norm_w, w_k, b_k, w_v, b_v, cos, sin)
        x = x + FFN_RES_WEIGHT * swiglu(rmsnorm(x, norm_w[layer, 0]), ffn_w_gate[layer, 0],
                                        ffn_w_up[layer, 0], ffn_w_down[layer, 0])
        h = rmsnorm(x, norm_w[layer, 1])
        if layer < N_A_LAYERS:
            i = layer
            x = x + mamba2_mixer(h, ssm_w_in[i], ssm_conv_w[i], ssm_conv_b[i], ssm_dt_bias[i],
                                 ssm_a_log[i], ssm_d[i], ssm_norm_w[i], ssm_w_out[i])
        else:
            j = layer - N_A_LAYERS
            x = x + swa_sink_attention(h, k_shared, v_shared, attn_w_q[j], attn_b_q[j],
                                       attn_sinks[j], attn_w_o[j], attn_b_o[j], cos, sin)
        x = x + FFN_RES_WEIGHT * swiglu(rmsnorm(x, norm_w[layer, 2]), ffn_w_gate[layer, 1],
                                        ffn_w_up[layer, 1], ffn_w_down[layer, 1])
    return rmsnorm(x, final_norm_w)


import jax as _jax
import jax.numpy as _jnp

TWIN_FORMAT = 'train_step'
FWD_PARAMS = ['x', 'norm_w', 'ffn_w_gate', 'ffn_w_up', 'ffn_w_down', 'ssm_w_in', 'ssm_conv_w', 'ssm_conv_b', 'ssm_dt_bias', 'ssm_a_log', 'ssm_d', 'ssm_norm_w', 'ssm_w_out', 'kv_norm_w', 'w_k', 'b_k', 'w_v', 'b_v', 'attn_w_q', 'attn_b_q', 'attn_sinks', 'attn_w_o', 'attn_b_o', 'final_norm_w']
TWIN_WEIGHTS = ['norm_w', 'ffn_w_gate', 'ffn_w_up', 'ffn_w_down', 'ssm_w_in', 'ssm_conv_w', 'ssm_conv_b', 'ssm_dt_bias', 'ssm_a_log', 'ssm_d', 'ssm_norm_w', 'ssm_w_out', 'kv_norm_w', 'w_k', 'b_k', 'w_v', 'b_v', 'attn_w_q', 'attn_b_q', 'attn_sinks', 'attn_w_o', 'attn_b_o', 'final_norm_w']
TWIN_DIFF_INPUT = 'x'
TWIN_INPUTS = ['x', 'norm_w', 'ffn_w_gate', 'ffn_w_up', 'ffn_w_down', 'ssm_w_in', 'ssm_conv_w', 'ssm_conv_b', 'ssm_dt_bias', 'ssm_a_log', 'ssm_d', 'ssm_norm_w', 'ssm_w_out', 'kv_norm_w', 'w_k', 'b_k', 'w_v', 'b_v', 'attn_w_q', 'attn_b_q', 'attn_sinks', 'attn_w_o', 'attn_b_o', 'final_norm_w', 'loss_target', 'm_norm_w', 'm_ffn_w_gate', 'm_ffn_w_up', 'm_ffn_w_down', 'm_ssm_w_in', 'm_ssm_conv_w', 'm_ssm_conv_b', 'm_ssm_dt_bias', 'm_ssm_a_log', 'm_ssm_d', 'm_ssm_norm_w', 'm_ssm_w_out', 'm_kv_norm_w', 'm_w_k', 'm_b_k', 'm_w_v', 'm_b_v', 'm_attn_w_q', 'm_attn_b_q', 'm_attn_sinks', 'm_attn_w_o', 'm_attn_b_o', 'm_final_norm_w', 'v_norm_w', 'v_ffn_w_gate', 'v_ffn_w_up', 'v_ffn_w_down', 'v_ssm_w_in', 'v_ssm_conv_w', 'v_ssm_conv_b', 'v_ssm_dt_bias', 'v_ssm_a_log', 'v_ssm_d', 'v_ssm_norm_w', 'v_ssm_w_out', 'v_kv_norm_w', 'v_w_k', 'v_b_k', 'v_w_v', 'v_b_v', 'v_attn_w_q', 'v_attn_b_q', 'v_attn_sinks', 'v_attn_w_o', 'v_attn_b_o', 'v_final_norm_w']
TWIN_OUTPUTS = ['loss', 'grad_x', 'grad_norm_w', 'grad_ffn_w_gate', 'grad_ffn_w_up', 'grad_ffn_w_down', 'grad_ssm_w_in', 'grad_ssm_conv_w', 'grad_ssm_conv_b', 'grad_ssm_dt_bias', 'grad_ssm_a_log', 'grad_ssm_d', 'grad_ssm_norm_w', 'grad_ssm_w_out', 'grad_kv_norm_w', 'grad_w_k', 'grad_b_k', 'grad_w_v', 'grad_b_v', 'grad_attn_w_q', 'grad_attn_b_q', 'grad_attn_sinks', 'grad_attn_w_o', 'grad_attn_b_o', 'grad_final_norm_w', 'delta_norm_w', 'delta_ffn_w_gate', 'delta_ffn_w_up', 'delta_ffn_w_down', 'delta_ssm_w_in', 'delta_ssm_conv_w', 'delta_ssm_conv_b', 'delta_ssm_dt_bias', 'delta_ssm_a_log', 'delta_ssm_d', 'delta_ssm_norm_w', 'delta_ssm_w_out', 'delta_kv_norm_w', 'delta_w_k', 'delta_b_k', 'delta_w_v', 'delta_b_v', 'delta_attn_w_q', 'delta_attn_b_q', 'delta_attn_sinks', 'delta_attn_w_o', 'delta_attn_b_o', 'delta_final_norm_w', 'new_m_norm_w', 'new_m_ffn_w_gate', 'new_m_ffn_w_up', 'new_m_ffn_w_down', 'new_m_ssm_w_in', 'new_m_ssm_conv_w', 'new_m_ssm_conv_b', 'new_m_ssm_dt_bias', 'new_m_ssm_a_log', 'new_m_ssm_d', 'new_m_ssm_norm_w', 'new_m_ssm_w_out', 'new_m_kv_norm_w', 'new_m_w_k', 'new_m_b_k', 'new_m_w_v', 'new_m_b_v', 'new_m_attn_w_q', 'new_m_attn_b_q', 'new_m_attn_sinks', 'new_m_attn_w_o', 'new_m_attn_b_o', 'new_m_final_norm_w', 'new_v_norm_w', 'new_v_ffn_w_gate', 'new_v_ffn_w_up', 'new_v_ffn_w_down', 'new_v_ssm_w_in', 'new_v_ssm_conv_w', 'new_v_ssm_conv_b', 'new_v_ssm_dt_bias', 'new_v_ssm_a_log', 'new_v_ssm_d', 'new_v_ssm_norm_w', 'new_v_ssm_w_out', 'new_v_kv_norm_w', 'new_v_w_k', 'new_v_b_k', 'new_v_w_v', 'new_v_b_v', 'new_v_attn_w_q', 'new_v_attn_b_q', 'new_v_attn_sinks', 'new_v_attn_w_o', 'new_v_attn_b_o', 'new_v_final_norm_w']
TWIN_LEAF_KINDS = {'loss': 'loss', 'grad_x': 'grad_x', 'grad_norm_w': 'grad_w', 'grad_ffn_w_gate': 'grad_w', 'grad_ffn_w_up': 'grad_w', 'grad_ffn_w_down': 'grad_w', 'grad_ssm_w_in': 'grad_w', 'grad_ssm_conv_w': 'grad_w', 'grad_ssm_conv_b': 'grad_w', 'grad_ssm_dt_bias': 'grad_w', 'grad_ssm_a_log': 'grad_w', 'grad_ssm_d': 'grad_w', 'grad_ssm_norm_w': 'grad_w', 'grad_ssm_w_out': 'grad_w', 'grad_kv_norm_w': 'grad_w', 'grad_w_k': 'grad_w', 'grad_b_k': 'grad_w', 'grad_w_v': 'grad_w', 'grad_b_v': 'grad_w', 'grad_attn_w_q': 'grad_w', 'grad_attn_b_q': 'grad_w', 'grad_attn_sinks': 'grad_w', 'grad_attn_w_o': 'grad_w', 'grad_attn_b_o': 'grad_w', 'grad_final_norm_w': 'grad_w', 'delta_norm_w': 'delta_w', 'delta_ffn_w_gate': 'delta_w', 'delta_ffn_w_up': 'delta_w', 'delta_ffn_w_down': 'delta_w', 'delta_ssm_w_in': 'delta_w', 'delta_ssm_conv_w': 'delta_w', 'delta_ssm_conv_b': 'delta_w', 'delta_ssm_dt_bias': 'delta_w', 'delta_ssm_a_log': 'delta_w', 'delta_ssm_d': 'delta_w', 'delta_ssm_norm_w': 'delta_w', 'delta_ssm_w_out': 'delta_w', 'delta_kv_norm_w': 'delta_w', 'delta_w_k': 'delta_w', 'delta_b_k': 'delta_w', 'delta_w_v': 'delta_w', 'delta_b_v': 'delta_w', 'delta_attn_w_q': 'delta_w', 'delta_attn_b_q': 'delta_w', 'delta_attn_sinks': 'delta_w', 'delta_attn_w_o': 'delta_w', 'delta_attn_b_o': 'delta_w', 'delta_final_norm_w': 'delta_w', 'new_m_norm_w': 'new_m', 'new_m_ffn_w_gate': 'new_m', 'new_m_ffn_w_up': 'new_m', 'new_m_ffn_w_down': 'new_m', 'new_m_ssm_w_in': 'new_m', 'new_m_ssm_conv_w': 'new_m', 'new_m_ssm_conv_b': 'new_m', 'new_m_ssm_dt_bias': 'new_m', 'new_m_ssm_a_log': 'new_m', 'new_m_ssm_d': 'new_m', 'new_m_ssm_norm_w': 'new_m', 'new_m_ssm_w_out': 'new_m', 'new_m_kv_norm_w': 'new_m', 'new_m_w_k': 'new_m', 'new_m_b_k': 'new_m', 'new_m_w_v': 'new_m', 'new_m_b_v': 'new_m', 'new_m_attn_w_q': 'new_m', 'new_m_attn_b_q': 'new_m', 'new_m_attn_sinks': 'new_m', 'new_m_attn_w_o': 'new_m', 'new_m_attn_b_o': 'new_m', 'new_m_final_norm_w': 'new_m', 'new_v_norm_w': 'new_v', 'new_v_ffn_w_gate': 'new_v', 'new_v_ffn_w_up': 'new_v', 'new_v_ffn_w_down': 'new_v', 'new_v_ssm_w_in': 'new_v', 'new_v_ssm_conv_w': 'new_v', 'new_v_ssm_conv_b': 'new_v', 'new_v_ssm_dt_bias': 'new_v', 'new_v_ssm_a_log': 'new_v', 'new_v_ssm_d': 'new_v', 'new_v_ssm_norm_w': 'new_v', 'new_v_ssm_w_out': 'new_v', 'new_v_kv_norm_w': 'new_v', 'new_v_w_k': 'new_v', 'new_v_b_k': 'new_v', 'new_v_w_v': 'new_v', 'new_v_b_v': 'new_v', 'new_v_attn_w_q': 'new_v', 'new_v_attn_b_q': 'new_v', 'new_v_attn_sinks': 'new_v', 'new_v_attn_w_o': 'new_v', 'new_v_attn_b_o': 'new_v', 'new_v_final_norm_w': 'new_v'}


def _forward(args):
    return _fwd_reference(*[args[k] for k in FWD_PARAMS])


def _output_shape():
    out = _jax.eval_shape(lambda: _forward(_fwd_setup_inputs(0)))
    return out.shape, out.dtype

N_MICROBATCH = 1
ADAM_LR = 0.001
ADAM_B1 = 0.9
ADAM_B2 = 0.999
ADAM_EPS = 1e-08
ADAM_WD = 0.01
ADAM_STEP = 10
PER_EXAMPLE_BATCH_AXIS = {'x': 0, 'loss_target': 0}
SHARED_INPUTS = []
_WEIGHT_DTYPES = {'norm_w': _jnp.float32, 'ffn_w_gate': _jnp.float32, 'ffn_w_up': _jnp.float32, 'ffn_w_down': _jnp.float32, 'ssm_w_in': _jnp.float32, 'ssm_conv_w': _jnp.float32, 'ssm_conv_b': _jnp.float32, 'ssm_dt_bias': _jnp.float32, 'ssm_a_log': _jnp.float32, 'ssm_d': _jnp.float32, 'ssm_norm_w': _jnp.float32, 'ssm_w_out': _jnp.float32, 'kv_norm_w': _jnp.float32, 'w_k': _jnp.float32, 'b_k': _jnp.float32, 'w_v': _jnp.float32, 'b_v': _jnp.float32, 'attn_w_q': _jnp.float32, 'attn_b_q': _jnp.float32, 'attn_sinks': _jnp.float32, 'attn_w_o': _jnp.float32, 'attn_b_o': _jnp.float32, 'final_norm_w': _jnp.float32}
MOMENT_SCALE = {'norm_w': 6.775106e-02, 'ffn_w_gate': 2.028834e-02, 'ffn_w_up': 1.962828e-02, 'ffn_w_down': 3.254392e-02, 'ssm_w_in': 6.145126e-02, 'ssm_conv_w': 5.694841e-02, 'ssm_conv_b': 8.149558e-02, 'ssm_dt_bias': 9.528646e-02, 'ssm_a_log': 1.252531e-01, 'ssm_d': 2.903996e-01, 'ssm_norm_w': 6.890129e-02, 'ssm_w_out': 9.348863e-02, 'kv_norm_w': 2.418079e-02, 'w_k': 2.933212e-02, 'b_k': 1.838725e-02, 'w_v': 3.941541e-02, 'b_v': 1.925188e-01, 'attn_w_q': 1.451793e-02, 'attn_b_q': 1.486749e-02, 'attn_sinks': 1.558161e-02, 'attn_w_o': 2.001218e-02, 'attn_b_o': 8.563918e-02, 'final_norm_w': 1.601776e+01}


def _to_microbatches(a, axis):
    t = _jnp.moveaxis(a, axis, 0)
    t = t.reshape((N_MICROBATCH, t.shape[0] // N_MICROBATCH) + t.shape[1:])
    return _jnp.moveaxis(t, 1, axis + 1)


def setup_inputs(seed: int = 0) -> dict:
    inp = _fwd_setup_inputs(seed)
    key = _jax.random.fold_in(_jax.random.key(seed), 7919)
    shape, _ = _output_shape()
    out = dict(inp)
    out["loss_target"] = _jax.random.normal(_jax.random.fold_in(key, 0), shape, _jnp.float32)
    for i, name in enumerate(TWIN_WEIGHTS):
        w = inp[name].astype(_jnp.float32)
        if MOMENT_SCALE is None:
            s = _jnp.sqrt(_jnp.mean(_jnp.square(w)) + 1e-30)
        else:
            s = MOMENT_SCALE[name]
        km, kv = _jax.random.split(_jax.random.fold_in(key, i + 1))
        out[name] = w
        out["m_" + name] = s * _jax.random.normal(km, w.shape, _jnp.float32)
        out["v_" + name] = (s * s) * _jax.random.uniform(kv, w.shape, _jnp.float32, 0.5, 1.5)
    if N_MICROBATCH > 1:
        for name, axis in PER_EXAMPLE_BATCH_AXIS.items():
            out[name] = _to_microbatches(out[name], axis)
    return {'x': out['x'], 'norm_w': out['norm_w'], 'ffn_w_gate': out['ffn_w_gate'], 'ffn_w_up': out['ffn_w_up'], 'ffn_w_down': out['ffn_w_down'], 'ssm_w_in': out['ssm_w_in'], 'ssm_conv_w': out['ssm_conv_w'], 'ssm_conv_b': out['ssm_conv_b'], 'ssm_dt_bias': out['ssm_dt_bias'], 'ssm_a_log': out['ssm_a_log'], 'ssm_d': out['ssm_d'], 'ssm_norm_w': out['ssm_norm_w'], 'ssm_w_out': out['ssm_w_out'], 'kv_norm_w': out['kv_norm_w'], 'w_k': out['w_k'], 'b_k': out['b_k'], 'w_v': out['w_v'], 'b_v': out['b_v'], 'attn_w_q': out['attn_w_q'], 'attn_b_q': out['attn_b_q'], 'attn_sinks': out['attn_sinks'], 'attn_w_o': out['attn_w_o'], 'attn_b_o': out['attn_b_o'], 'final_norm_w': out['final_norm_w'], 'loss_target': out['loss_target'], 'm_norm_w': out['m_norm_w'], 'm_ffn_w_gate': out['m_ffn_w_gate'], 'm_ffn_w_up': out['m_ffn_w_up'], 'm_ffn_w_down': out['m_ffn_w_down'], 'm_ssm_w_in': out['m_ssm_w_in'], 'm_ssm_conv_w': out['m_ssm_conv_w'], 'm_ssm_conv_b': out['m_ssm_conv_b'], 'm_ssm_dt_bias': out['m_ssm_dt_bias'], 'm_ssm_a_log': out['m_ssm_a_log'], 'm_ssm_d': out['m_ssm_d'], 'm_ssm_norm_w': out['m_ssm_norm_w'], 'm_ssm_w_out': out['m_ssm_w_out'], 'm_kv_norm_w': out['m_kv_norm_w'], 'm_w_k': out['m_w_k'], 'm_b_k': out['m_b_k'], 'm_w_v': out['m_w_v'], 'm_b_v': out['m_b_v'], 'm_attn_w_q': out['m_attn_w_q'], 'm_attn_b_q': out['m_attn_b_q'], 'm_attn_sinks': out['m_attn_sinks'], 'm_attn_w_o': out['m_attn_w_o'], 'm_attn_b_o': out['m_attn_b_o'], 'm_final_norm_w': out['m_final_norm_w'], 'v_norm_w': out['v_norm_w'], 'v_ffn_w_gate': out['v_ffn_w_gate'], 'v_ffn_w_up': out['v_ffn_w_up'], 'v_ffn_w_down': out['v_ffn_w_down'], 'v_ssm_w_in': out['v_ssm_w_in'], 'v_ssm_conv_w': out['v_ssm_conv_w'], 'v_ssm_conv_b': out['v_ssm_conv_b'], 'v_ssm_dt_bias': out['v_ssm_dt_bias'], 'v_ssm_a_log': out['v_ssm_a_log'], 'v_ssm_d': out['v_ssm_d'], 'v_ssm_norm_w': out['v_ssm_norm_w'], 'v_ssm_w_out': out['v_ssm_w_out'], 'v_kv_norm_w': out['v_kv_norm_w'], 'v_w_k': out['v_w_k'], 'v_b_k': out['v_b_k'], 'v_w_v': out['v_w_v'], 'v_b_v': out['v_b_v'], 'v_attn_w_q': out['v_attn_w_q'], 'v_attn_b_q': out['v_attn_b_q'], 'v_attn_sinks': out['v_attn_sinks'], 'v_attn_w_o': out['v_attn_w_o'], 'v_attn_b_o': out['v_attn_b_o'], 'v_final_norm_w': out['v_final_norm_w']}


def _loss(weights, diff, rest, loss_target):
    with _jax.named_scope("forward"):
        args = {**rest, TWIN_DIFF_INPUT: diff, **{k: w.astype(_WEIGHT_DTYPES[k]) for k, w in weights.items()}}
        y = _forward(args)
    with _jax.named_scope("loss_head"):
        err = _jnp.square(y.astype(_jnp.float32) - loss_target)
        return 0.5 * _jnp.sum(_jnp.mean(err, axis=-1)) if err.ndim else 0.5 * err


def _adamw(w, g, m, v):
    m = ADAM_B1 * m + (1.0 - ADAM_B1) * g
    v = ADAM_B2 * v + (1.0 - ADAM_B2) * _jnp.square(g)
    m_hat = m / (1.0 - ADAM_B1 ** ADAM_STEP)
    v_hat = v / (1.0 - ADAM_B2 ** ADAM_STEP)
    delta = -ADAM_LR * (m_hat / (_jnp.sqrt(v_hat) + ADAM_EPS) + ADAM_WD * w)
    return delta, m, v


def reference(x, norm_w, ffn_w_gate, ffn_w_up, ffn_w_down, ssm_w_in, ssm_conv_w, ssm_conv_b, ssm_dt_bias, ssm_a_log, ssm_d, ssm_norm_w, ssm_w_out, kv_norm_w, w_k, b_k, w_v, b_v, attn_w_q, attn_b_q, attn_sinks, attn_w_o, attn_b_o, final_norm_w, loss_target, m_norm_w, m_ffn_w_gate, m_ffn_w_up, m_ffn_w_down, m_ssm_w_in, m_ssm_conv_w, m_ssm_conv_b, m_ssm_dt_bias, m_ssm_a_log, m_ssm_d, m_ssm_norm_w, m_ssm_w_out, m_kv_norm_w, m_w_k, m_b_k, m_w_v, m_b_v, m_attn_w_q, m_attn_b_q, m_attn_sinks, m_attn_w_o, m_attn_b_o, m_final_norm_w, v_norm_w, v_ffn_w_gate, v_ffn_w_up, v_ffn_w_down, v_ssm_w_in, v_ssm_conv_w, v_ssm_conv_b, v_ssm_dt_bias, v_ssm_a_log, v_ssm_d, v_ssm_norm_w, v_ssm_w_out, v_kv_norm_w, v_w_k, v_b_k, v_w_v, v_b_v, v_attn_w_q, v_attn_b_q, v_attn_sinks, v_attn_w_o, v_attn_b_o, v_final_norm_w):
    given = dict(x=x, norm_w=norm_w, ffn_w_gate=ffn_w_gate, ffn_w_up=ffn_w_up, ffn_w_down=ffn_w_down, ssm_w_in=ssm_w_in, ssm_conv_w=ssm_conv_w, ssm_conv_b=ssm_conv_b, ssm_dt_bias=ssm_dt_bias, ssm_a_log=ssm_a_log, ssm_d=ssm_d, ssm_norm_w=ssm_norm_w, ssm_w_out=ssm_w_out, kv_norm_w=kv_norm_w, w_k=w_k, b_k=b_k, w_v=w_v, b_v=b_v, attn_w_q=attn_w_q, attn_b_q=attn_b_q, attn_sinks=attn_sinks, attn_w_o=attn_w_o, attn_b_o=attn_b_o, final_norm_w=final_norm_w, loss_target=loss_target, m_norm_w=m_norm_w, m_ffn_w_gate=m_ffn_w_gate, m_ffn_w_up=m_ffn_w_up, m_ffn_w_down=m_ffn_w_down, m_ssm_w_in=m_ssm_w_in, m_ssm_conv_w=m_ssm_conv_w, m_ssm_conv_b=m_ssm_conv_b, m_ssm_dt_bias=m_ssm_dt_bias, m_ssm_a_log=m_ssm_a_log, m_ssm_d=m_ssm_d, m_ssm_norm_w=m_ssm_norm_w, m_ssm_w_out=m_ssm_w_out, m_kv_norm_w=m_kv_norm_w, m_w_k=m_w_k, m_b_k=m_b_k, m_w_v=m_w_v, m_b_v=m_b_v, m_attn_w_q=m_attn_w_q, m_attn_b_q=m_attn_b_q, m_attn_sinks=m_attn_sinks, m_attn_w_o=m_attn_w_o, m_attn_b_o=m_attn_b_o, m_final_norm_w=m_final_norm_w, v_norm_w=v_norm_w, v_ffn_w_gate=v_ffn_w_gate, v_ffn_w_up=v_ffn_w_up, v_ffn_w_down=v_ffn_w_down, v_ssm_w_in=v_ssm_w_in, v_ssm_conv_w=v_ssm_conv_w, v_ssm_conv_b=v_ssm_conv_b, v_ssm_dt_bias=v_ssm_dt_bias, v_ssm_a_log=v_ssm_a_log, v_ssm_d=v_ssm_d, v_ssm_norm_w=v_ssm_norm_w, v_ssm_w_out=v_ssm_w_out, v_kv_norm_w=v_kv_norm_w, v_w_k=v_w_k, v_b_k=v_b_k, v_w_v=v_w_v, v_b_v=v_b_v, v_attn_w_q=v_attn_w_q, v_attn_b_q=v_attn_b_q, v_attn_sinks=v_attn_sinks, v_attn_w_o=v_attn_w_o, v_attn_b_o=v_attn_b_o, v_final_norm_w=v_final_norm_w)
    weights = {n: given[n] for n in TWIN_WEIGHTS}
    shared = {n: given[n] for n in SHARED_INPUTS}
    per_example = {n: given[n] for n in ['x']}
    grad_fn = _jax.value_and_grad(_loss, argnums=(0, 1))

    def one_microbatch(ex, loss_target):
        ex = dict(ex)
        diff = ex.pop(TWIN_DIFF_INPUT)
        return grad_fn(weights, diff, {**shared, **ex}, loss_target)

    if N_MICROBATCH == 1:
        loss, (grad_w, grad_x) = one_microbatch(per_example, given["loss_target"])
    else:
        def body(carry, xs):
            loss_sum, grad_sum = carry
            l_k, (gw_k, gx_k) = one_microbatch(xs[0], xs[1])
            with _jax.named_scope("update"):
                return (loss_sum + l_k, _jax.tree.map(_jnp.add, grad_sum, gw_k)), gx_k

        init = (_jnp.zeros((), _jnp.float32), _jax.tree.map(_jnp.zeros_like, weights))
        (loss, grad_w), grad_x = _jax.lax.scan(body, init, (per_example, given["loss_target"]))
    with _jax.named_scope("update"):
        delta_w, new_m, new_v = {}, {}, {}
        for n in TWIN_WEIGHTS:
            delta_w[n], new_m[n], new_v[n] = _adamw(weights[n], grad_w[n], given["m_" + n], given["v_" + n])
    return (loss, grad_x, *[grad_w[n] for n in TWIN_WEIGHTS], *[delta_w[n] for n in TWIN_WEIGHTS],
            *[new_m[n] for n in TWIN_WEIGHTS], *[new_v[n] for n in TWIN_WEIGHTS])
```

```python
import functools
import math

import jax
import jax.numpy as jnp
from jax import lax
from jax.experimental import pallas as pl
from jax.experimental.pallas import tpu as pltpu

F32 = jnp.float32
BF16 = jnp.bfloat16
HI = lax.Precision.HIGHEST

D_MODEL = 1024
D_INNER = 2048
SSM_HEADS = 32
SSM_GROUPS = 4
HEADS_PER_GROUP = SSM_HEADS // SSM_GROUPS
SSM_HEAD_DIM = 64
SSM_STATE = 128
GROUP_DIM = D_INNER // SSM_GROUPS
CONV_DIM = D_INNER + 2 * SSM_GROUPS * SSM_STATE
CONV_WIDTH = 4
CHUNK = 128
ATT_HEAD_DIM = 64
N_Q_HEADS = 16
N_KV_HEADS = 4
Q_PER_KV = N_Q_HEADS // N_KV_HEADS
WINDOW = 128
ROPE_THETA = 10000.0
D_FF = 2816
N_CHIPS = 4
FF_SHARD = D_FF // N_CHIPS
IN_PROJ_DIM = D_INNER + CONV_DIM + SSM_HEADS
IN_SHARD = IN_PROJ_DIM // N_CHIPS
EPS = 1e-5
NEG = -1e30
LANES = 128
VMEM_LIMIT = 56 * 1024 * 1024

ADAM_LR = 0.001
ADAM_B1 = 0.9
ADAM_B2 = 0.999
ADAM_EPS = 1e-08
ADAM_WD = 0.01
ADAM_STEP = 10

NN = ((1,), (0,))
NT = ((1,), (1,))
TN = ((0,), (0,))


def _dot(a, b, dims=NN, precision=None):
    return lax.dot_general(a, b, (dims, ((), ())), preferred_element_type=F32, precision=precision)


def _cp(n_grid, **kw):
    return pltpu.CompilerParams(dimension_semantics=("arbitrary",) * n_grid,
                                vmem_limit_bytes=VMEM_LIMIT, **kw)


def _sigmoid(x):
    return 1.0 / (1.0 + jnp.exp(-x))


def _rms_fwd(xf, w):
    r = lax.rsqrt(jnp.mean(xf * xf, axis=-1, keepdims=True) + EPS)
    return xf * r * w


def _rms_bwd(dh, xf, w):
    r = lax.rsqrt(jnp.mean(xf * xf, axis=-1, keepdims=True) + EPS)
    xhat = xf * r
    dxhat = dh * w
    dx = r * (dxhat - xhat * jnp.mean(dxhat * xhat, axis=-1, keepdims=True))
    return dx, dh * xhat


def _row_tile(s, pref):
    return pref if s % pref == 0 else s


def rmsnorm_fwd(x, w, name):
    s, d = x.shape
    tm = _row_tile(s, 512)

    def body(x_ref, w_ref, o_ref):
        o_ref[...] = _rms_fwd(x_ref[...], w_ref[...]).astype(BF16)

    return pl.pallas_call(
        body, out_shape=jax.ShapeDtypeStruct((s, d), BF16), grid=(s // tm,),
        in_specs=[pl.BlockSpec((tm, d), lambda i: (i, 0)), pl.BlockSpec((1, d), lambda i: (0, 0))],
        out_specs=pl.BlockSpec((tm, d), lambda i: (i, 0)),
        name=name, compiler_params=_cp(1))(x, w.reshape(1, d))


def ffn_up(h, wg, wu, name):
    s, d = h.shape
    ng, _, f = wg.shape
    tm = _row_tile(s, 512)

    def body(h_ref, wg_ref, wu_ref, g_ref, u_ref, a_ref):
        hb = h_ref[...]
        g = _dot(hb, wg_ref[0])
        u = _dot(hb, wu_ref[0])
        g_ref[0] = g.astype(BF16)
        u_ref[0] = u.astype(BF16)
        a_ref[0] = (g * _sigmoid(g) * u).astype(BF16)

    out = jax.ShapeDtypeStruct((ng, s, f), BF16)
    w_spec = pl.BlockSpec((1, d, f), lambda i, k: (k, 0, 0))
    o_spec = pl.BlockSpec((1, tm, f), lambda i, k: (k, i, 0))
    return pl.pallas_call(
        body, out_shape=(out, out, out), grid=(s // tm, ng),
        in_specs=[pl.BlockSpec((tm, d), lambda i, k: (i, 0)), w_spec, w_spec],
        out_specs=(o_spec, o_spec, o_spec),
        name=name, compiler_params=_cp(2))(h, wg, wu)


def mm_res(a, w, x, scale, name, bias=None, norm_ws=()):
    ng, s, k = a.shape
    n = w.shape[2]
    tm = _row_tile(s, 256)
    has_bias = bias is not None
    n_norm = len(norm_ws)

    def body(*refs):
        a_ref, w_ref, x_ref = refs[:3]
        pos = 3
        b_ref = None
        if has_bias:
            b_ref = refs[pos]
            pos += 1
        nw_refs = refs[pos:pos + n_norm]
        pos += n_norm
        o_ref = refs[pos]
        h_refs = refs[pos + 1:pos + 1 + n_norm]
        acc = refs[-1]
        g = pl.program_id(1)
        part = _dot(a_ref[0], w_ref[0])

        @pl.when(g == 0)
        def _():
            acc[...] = part

        @pl.when(g > 0)
        def _():
            acc[...] += part

        @pl.when(g == ng - 1)
        def _():
            t = acc[...]
            if has_bias:
                t = t + b_ref[...]
            xn = x_ref[...] + scale * t
            o_ref[...] = xn
            for nw_ref, h_ref in zip(nw_refs, h_refs):
                h_ref[...] = _rms_fwd(xn, nw_ref[...]).astype(BF16)

    row = pl.BlockSpec((tm, n), lambda i, g: (i, 0))
    vec = pl.BlockSpec((1, n), lambda i, g: (0, 0))
    in_specs = [pl.BlockSpec((1, tm, k), lambda i, g: (g, i, 0)),
                pl.BlockSpec((1, k, n), lambda i, g: (g, 0, 0)), row]
    args = [a, w, x]
    if has_bias:
        in_specs.append(vec)
        args.append(bias.reshape(1, n))
    for nw in norm_ws:
        in_specs.append(vec)
        args.append(nw.reshape(1, n))
    out_shape = [jax.ShapeDtypeStruct((s, n), F32)] + [jax.ShapeDtypeStruct((s, n), BF16)] * n_norm
    res = pl.pallas_call(
        body, out_shape=tuple(out_shape), grid=(s // tm, ng),
        in_specs=in_specs, out_specs=tuple([row] * (1 + n_norm)),
        scratch_shapes=[pltpu.VMEM((tm, n), F32)],
        name=name, compiler_params=_cp(2))(*args)
    return res


def _col_tile(n):
    for t in (1024, 768, 512, 256, 128):
        if n % t == 0:
            return t
    return n


def mm_nn(a, w, name, bias=None, out_dtype=F32):
    s, k = a.shape
    n = w.shape[1]
    tm = _row_tile(s, 512)
    tn = _col_tile(n)
    has_bias = bias is not None

    def body(*refs):
        a_ref, w_ref = refs[:2]
        o_ref = refs[-1]
        t = _dot(a_ref[...], w_ref[...])
        if has_bias:
            t = t + refs[2][...]
        o_ref[...] = t.astype(out_dtype)

    in_specs = [pl.BlockSpec((tm, k), lambda j, i: (i, 0)), pl.BlockSpec((k, tn), lambda j, i: (0, j))]
    args = [a, w]
    if has_bias:
        in_specs.append(pl.BlockSpec((1, tn), lambda j, i: (0, j)))
        args.append(bias.reshape(1, n))
    return pl.pallas_call(
        body, out_shape=jax.ShapeDtypeStruct((s, n), out_dtype), grid=(n // tn, s // tm),
        in_specs=in_specs, out_specs=pl.BlockSpec((tm, tn), lambda j, i: (i, j)),
        name=name, compiler_params=_cp(2))(*args)


def mm_nt(a, w, name, scale=1.0, out_dtype=F32):
    s, k = a.shape
    n = w.shape[0]
    tm = _row_tile(s, 512)
    tn = _col_tile(n)

    def body(a_ref, w_ref, o_ref):
        t = _dot(a_ref[...].astype(BF16), w_ref[...], NT)
        o_ref[...] = (scale * t).astype(out_dtype)

    return pl.pallas_call(
        body, out_shape=jax.ShapeDtypeStruct((s, n), out_dtype), grid=(n // tn, s // tm),
        in_specs=[pl.BlockSpec((tm, k), lambda j, i: (i, 0)), pl.BlockSpec((tn, k), lambda j, i: (j, 0))],
        out_specs=pl.BlockSpec((tm, tn), lambda j, i: (i, j)),
        name=name, compiler_params=_cp(2))(a, w)


def mm_tn(a, b, name, scale=1.0):
    ga, s, m = a.shape
    gb, _, n = b.shape
    ng = max(ga, gb)
    tm = _col_tile(m)
    tn = _col_tile(n) if n > 1024 else n

    def body(a_ref, b_ref, o_ref):
        t = _dot(a_ref[0].astype(BF16), b_ref[0].astype(BF16), TN)
        o_ref[0] = (scale * t).astype(BF16)

    a_map = (lambda g, i, j: (g, 0, i)) if ga > 1 else (lambda g, i, j: (0, 0, i))
    b_map = (lambda g, i, j: (g, 0, j)) if gb > 1 else (lambda g, i, j: (0, 0, j))
    return pl.pallas_call(
        body, out_shape=jax.ShapeDtypeStruct((ng, m, n), BF16), grid=(ng, m // tm, n // tn),
        in_specs=[pl.BlockSpec((1, s, tm), a_map), pl.BlockSpec((1, s, tn), b_map)],
        out_specs=pl.BlockSpec((1, tm, tn), lambda g, i, j: (g, i, j)),
        name=name, compiler_params=_cp(3))(a, b)


def mm_nt_rms(pairs, dxn, x, nw, name):
    s, n = x.shape
    ng = pairs[0][0].shape[0]
    np_ = len(pairs)
    tm = _row_tile(s, 256)
    nrow = s // tm

    def body(*refs):
        a_refs = refs[0:2 * np_:2]
        w_refs = refs[1:2 * np_:2]
        dxn_ref, x_ref, nw_ref, dx_ref, dnw_ref, acc = refs[2 * np_:]
        i = pl.program_id(0)
        g = pl.program_id(1)
        part = _dot(a_refs[0][0].astype(BF16), w_refs[0][0], NT)
        for a_ref, w_ref in zip(a_refs[1:], w_refs[1:]):
            part += _dot(a_ref[0].astype(BF16), w_ref[0], NT)

        @pl.when(g == 0)
        def _():
            acc[...] = part

        @pl.when(g > 0)
        def _():
            acc[...] += part

        @pl.when(g == ng - 1)
        def _():
            dx, dnw = _rms_bwd(acc[...], x_ref[...], nw_ref[...])
            dx_ref[...] = dxn_ref[...] + dx
            col = jnp.sum(dnw, axis=0, keepdims=True)

            @pl.when(i == 0)
            def _():
                dnw_ref[...] = col

            @pl.when(i > 0)
            def _():
                dnw_ref[...] += col

    in_specs, args = [], []
    for a, w in pairs:
        k = a.shape[2]
        in_specs.append(pl.BlockSpec((1, tm, k), lambda i, g: (g, i, 0)))
        in_specs.append(pl.BlockSpec((1, n, k), lambda i, g: (g, 0, 0)))
        args += [a, w]
    row = pl.BlockSpec((tm, n), lambda i, g: (i, 0))
    vec = pl.BlockSpec((1, n), lambda i, g: (0, 0))
    in_specs += [row, row, vec]
    args += [dxn, x, nw.reshape(1, n)]
    dx, dnw = pl.pallas_call(
        body, out_shape=(jax.ShapeDtypeStruct((s, n), F32), jax.ShapeDtypeStruct((1, n), F32)),
        grid=(nrow, ng), in_specs=in_specs, out_specs=(row, vec),
        scratch_shapes=[pltpu.VMEM((tm, n), F32)],
        name=name, compiler_params=_cp(2))(*args)
    return dx, dnw[0]


def ffn_bwd_act(dxn, wd, g, u, name):
    s, d = dxn.shape
    ng, f, _ = wd.shape
    tm = _row_tile(s, 512)

    def body(dx_ref, wd_ref, g_ref, u_ref, dg_ref, du_ref):
        da = 0.5 * _dot(dx_ref[...].astype(BF16), wd_ref[0], NT)
        gf = g_ref[0].astype(F32)
        uf = u_ref[0].astype(F32)
        sg = _sigmoid(gf)
        dg_ref[0] = (da * uf * (sg * (1.0 + gf * (1.0 - sg)))).astype(BF16)
        du_ref[0] = (da * gf * sg).astype(BF16)

    blk = pl.BlockSpec((1, tm, f), lambda i, k: (k, i, 0))
    out = jax.ShapeDtypeStruct((ng, s, f), BF16)
    return pl.pallas_call(
        body, out_shape=(out, out), grid=(s // tm, ng),
        in_specs=[pl.BlockSpec((tm, d), lambda i, k: (i, 0)),
                  pl.BlockSpec((1, f, d), lambda i, k: (k, 0, 0)), blk, blk],
        out_specs=(blk, blk), name=name, compiler_params=_cp(2))(dxn, wd, g, u)


def colsum(a, name):
    s, n = a.shape
    tm = _row_tile(s, 512)

    def body(a_ref, o_ref):
        col = jnp.sum(a_ref[...].astype(F32), axis=0, keepdims=True)

        @pl.when(pl.program_id(0) == 0)
        def _():
            o_ref[...] = col

        @pl.when(pl.program_id(0) > 0)
        def _():
            o_ref[...] += col

    return pl.pallas_call(
        body, out_shape=jax.ShapeDtypeStruct((1, n), F32), grid=(s // tm,),
        in_specs=[pl.BlockSpec((tm, n), lambda i: (i, 0))],
        out_specs=pl.BlockSpec((1, n), lambda i: (0, 0)),
        name=name, compiler_params=_cp(1))(a)[0]


def rope_tables(s):
    pos = jnp.arange(s, dtype=F32)
    inv = 1.0 / (ROPE_THETA ** (jnp.arange(0, ATT_HEAD_DIM, 2, dtype=F32) / ATT_HEAD_DIM))
    ang = pos[:, None] * inv[None, :]
    cos = jnp.tile(jnp.cos(ang), (1, 2 * LANES // ATT_HEAD_DIM))
    sin = jnp.tile(jnp.sin(ang), (1, 2 * LANES // ATT_HEAD_DIM))
    return cos, sin


def rope_apply(t, cos, sin, name, inverse=False, scale=1.0, out_dtype=BF16):
    s, n = t.shape
    tm = _row_tile(s, 512)
    half = ATT_HEAD_DIM // 2
    reps = n // LANES

    def body(t_ref, c_ref, s_ref, o_ref):
        tf = t_ref[...].astype(F32)
        c = jnp.tile(c_ref[...], (1, reps))
        sn = jnp.tile(s_ref[...], (1, reps))
        lane = lax.broadcasted_iota(jnp.int32, tf.shape, 1)
        first = (lane & (ATT_HEAD_DIM - 1)) < half
        rot = jnp.where(first, -pltpu.roll(tf, n - half, 1), pltpu.roll(tf, half, 1))
        sign = -1.0 if inverse else 1.0
        o_ref[...] = (scale * (tf * c + sign * rot * sn)).astype(out_dtype)

    tab = pl.BlockSpec((tm, LANES), lambda i: (i, 0))
    return pl.pallas_call(
        body, out_shape=jax.ShapeDtypeStruct((s, n), out_dtype), grid=(s // tm,),
        in_specs=[pl.BlockSpec((tm, n), lambda i: (i, 0)), tab, tab],
        out_specs=pl.BlockSpec((tm, n), lambda i: (i, 0)),
        name=name, compiler_params=_cp(1))(t, cos, sin)


def _shift_down(u, k):
    if k == 0:
        return u
    row = lax.broadcasted_iota(jnp.int32, u.shape, 0)
    return jnp.where(row >= k, pltpu.roll(u, k, 0), 0.0)


def _shift_up(u, k):
    if k == 0:
        return u
    s = u.shape[0]
    row = lax.broadcasted_iota(jnp.int32, u.shape, 0)
    return jnp.where(row < s - k, pltpu.roll(u, s - k, 0), 0.0)


def _conv_pre(u, w_ref, b_ref):
    pre = b_ref[...] + w_ref[CONV_WIDTH - 1:CONV_WIDTH, :] * u
    for k in range(CONV_WIDTH - 1):
        pre += w_ref[k:k + 1, :] * _shift_down(u, CONV_WIDTH - 1 - k)
    return pre


def conv_fwd(u, w, b, name):
    s, c = u.shape
    tc = 256

    def body(u_ref, w_ref, b_ref, o_ref):
        pre = _conv_pre(u_ref[...], w_ref, b_ref)
        o_ref[...] = pre * _sigmoid(pre)

    col = pl.BlockSpec((s, tc), lambda j: (0, j))
    return pl.pallas_call(
        body, out_shape=jax.ShapeDtypeStruct((s, c), F32), grid=(c // tc,),
        in_specs=[col, pl.BlockSpec((CONV_WIDTH, tc), lambda j: (0, j)), pl.BlockSpec((1, tc), lambda j: (0, j))],
        out_specs=col, name=name, compiler_params=_cp(1))(u, w, b.reshape(1, c))


def conv_bwd(dact, u, w, b, name):
    s, c = u.shape
    tc = 256

    def body(da_ref, u_ref, w_ref, b_ref, du_ref, dw_ref, db_ref):
        uf = u_ref[...]
        pre = _conv_pre(uf, w_ref, b_ref)
        sg = _sigmoid(pre)
        dpre = da_ref[...] * (sg * (1.0 + pre * (1.0 - sg)))
        du = w_ref[CONV_WIDTH - 1:CONV_WIDTH, :] * dpre
        for k in range(CONV_WIDTH - 1):
            du += w_ref[k:k + 1, :] * _shift_up(dpre, CONV_WIDTH - 1 - k)
        du_ref[...] = du
        db_ref[...] = jnp.sum(dpre, axis=0, keepdims=True)
        for k in range(CONV_WIDTH):
            dw_ref[k:k + 1, :] = jnp.sum(dpre * _shift_down(uf, CONV_WIDTH - 1 - k), axis=0, keepdims=True)

    col = pl.BlockSpec((s, tc), lambda j: (0, j))
    wsp = pl.BlockSpec((CONV_WIDTH, tc), lambda j: (0, j))
    bsp = pl.BlockSpec((1, tc), lambda j: (0, j))
    du, dw, db = pl.pallas_call(
        body, out_shape=(jax.ShapeDtypeStruct((s, c), F32), jax.ShapeDtypeStruct((CONV_WIDTH, c), F32),
                         jax.ShapeDtypeStruct((1, c), F32)),
        grid=(c // tc,), in_specs=[col, col, wsp, bsp], out_specs=(col, wsp, bsp),
        name=name, compiler_params=_cp(1))(dact, u, w, b.reshape(1, c))
    return du, dw, db[0]


def _lane_pick(mat, idx):
    lane = lax.broadcasted_iota(jnp.int32, mat.shape, 1)
    return jnp.sum(jnp.where(lane == idx, mat, 0.0), axis=1, keepdims=True)


def _sub_pick(mat, idx):
    sub = lax.broadcasted_iota(jnp.int32, mat.shape, 0)
    return jnp.sum(jnp.where(sub == idx, mat, 0.0), axis=0, keepdims=True)


def _expand_heads(cols):
    rows = cols[0].shape[0]
    left = lax.broadcasted_iota(jnp.int32, (rows, LANES), 1) < SSM_HEAD_DIM
    return jnp.concatenate(
        [jnp.where(left, cols[2 * p], cols[2 * p + 1]) for p in range(HEADS_PER_GROUP // 2)], axis=1)


def _heads_to_lanes(mat, g):
    jj = lax.broadcasted_iota(jnp.int32, (GROUP_DIM, LANES), 0)
    ll = lax.broadcasted_iota(jnp.int32, (GROUP_DIM, LANES), 1)
    sel = (ll == HEADS_PER_GROUP * g + (jj >> 6)).astype(F32)
    return _dot(mat, sel, NN, HI)


def _softplus(x):
    return jnp.maximum(x, 0.0) + jnp.log1p(jnp.exp(-jnp.abs(x)))


def _ssd_scalars(dt_ref, bias_ref, a_ref, dtall, csall, cst):
    dta = _softplus(dt_ref[...] + bias_ref[...])
    row = lax.broadcasted_iota(jnp.int32, (CHUNK, CHUNK), 0)
    col = lax.broadcasted_iota(jnp.int32, (CHUNK, CHUNK), 1)
    tri = (row >= col).astype(F32)
    cs = _dot(tri, dta * a_ref[...], NN, HI)
    dtall[...] = dta
    csall[...] = cs
    cst[...] = cs.T


def _decay_mat(cs_col, cs_row):
    row = lax.broadcasted_iota(jnp.int32, (CHUNK, CHUNK), 0)
    col = lax.broadcasted_iota(jnp.int32, (CHUNK, CHUNK), 1)
    return jnp.exp(jnp.where(row >= col, cs_col - cs_row, NEG))


def _head_mask(xpair, right):
    lane = lax.broadcasted_iota(jnp.int32, xpair.shape, 1)
    keep = (lane >= SSM_HEAD_DIM) if right else (lane < SSM_HEAD_DIM)
    return jnp.where(keep, xpair, 0.0)


def _chunk_cols(x_all, g):
    return [_lane_pick(x_all, HEADS_PER_GROUP * g + r) for r in range(HEADS_PER_GROUP)]


def _decay_col(cs_cols):
    return jnp.concatenate(
        [jnp.broadcast_to(jnp.exp(cc[CHUNK - 1:CHUNK, :]), (SSM_HEAD_DIM, 1)) for cc in cs_cols], axis=0)


def ssd_fwd(act, z, dtp, bias_p, a_p, d_p, normw, name):
    s = act.shape[0]
    nc = s // CHUNK
    xs_blocks = D_INNER // GROUP_DIM
    b_off = D_INNER // SSM_STATE
    c_off = b_off + SSM_GROUPS

    def body(xs_ref, b_ref, c_ref, z_ref, dt_ref, bias_ref, a_ref, d_ref, nw_ref,
             yn_ref, y_ref, st_ref, state, dtall, csall, cst):
        c = pl.program_id(0)
        g = pl.program_id(1)

        @pl.when(g == 0)
        def _():
            _ssd_scalars(dt_ref, bias_ref, a_ref, dtall, csall, cst)

        @pl.when(c == 0)
        def _():
            state[g] = jnp.zeros((GROUP_DIM, SSM_STATE), F32)

        cs_cols = _chunk_cols(csall[...], g)
        dt_cols = _chunk_cols(dtall[...], g)
        cs_rows = [_sub_pick(cst[...], HEADS_PER_GROUP * g + r) for r in range(HEADS_PER_GROUP)]
        d_cols = _chunk_cols(d_ref[...], g)
        cs_exp = _expand_heads(cs_cols)
        dt_exp = _expand_heads(dt_cols)
        d_exp = _expand_heads(d_cols)
        xs = xs_ref[...]
        bb = b_ref[...].astype(BF16)
        cb16 = c_ref[...].astype(BF16)
        xdt = xs * dt_exp
        s_prev = state[g]
        st_ref[0, 0] = s_prev
        y_off = _dot(cb16, s_prev.astype(BF16), NT) * jnp.exp(cs_exp)
        decay_st = jnp.exp(cs_exp[CHUNK - 1:CHUNK, :] - cs_exp)
        contrib = _dot((xdt * decay_st).astype(BF16), bb, TN)
        state[g] = _decay_col(cs_cols) * s_prev + contrib
        cbm = _dot(cb16, bb, NT)
        pairs = []
        for p in range(HEADS_PER_GROUP // 2):
            xpair = xdt[:, LANES * p:LANES * (p + 1)]
            m0 = (cbm * _decay_mat(cs_cols[2 * p], cs_rows[2 * p])).astype(BF16)
            m1 = (cbm * _decay_mat(cs_cols[2 * p + 1], cs_rows[2 * p + 1])).astype(BF16)
            pairs.append(_dot(m0, _head_mask(xpair, False).astype(BF16))
                         + _dot(m1, _head_mask(xpair, True).astype(BF16)))
        y = jnp.concatenate(pairs, axis=1) + y_off + xs * d_exp
        y_ref[...] = y
        zf = z_ref[...]
        yg = y * (zf * _sigmoid(zf))
        yn_ref[...] = _rms_fwd(yg, nw_ref[...]).astype(BF16)

    grp = pl.BlockSpec((CHUNK, GROUP_DIM), lambda c, g: (c, g))
    par = pl.BlockSpec((1, LANES), lambda c, g: (0, 0))
    return pl.pallas_call(
        body,
        out_shape=(jax.ShapeDtypeStruct((s, D_INNER), BF16), jax.ShapeDtypeStruct((s, D_INNER), F32),
                   jax.ShapeDtypeStruct((nc, SSM_GROUPS, GROUP_DIM, SSM_STATE), F32)),
        grid=(nc, SSM_GROUPS),
        in_specs=[grp,
                  pl.BlockSpec((CHUNK, SSM_STATE), lambda c, g: (c, b_off + g)),
                  pl.BlockSpec((CHUNK, SSM_STATE), lambda c, g: (c, c_off + g)),
                  grp,
                  pl.BlockSpec((CHUNK, LANES), lambda c, g: (c, 0)),
                  par, par, par,
                  pl.BlockSpec((1, GROUP_DIM), lambda c, g: (0, g))],
        out_specs=(grp, grp, pl.BlockSpec((1, 1, GROUP_DIM, SSM_STATE), lambda c, g: (c, g, 0, 0))),
        scratch_shapes=[pltpu.VMEM((SSM_GROUPS, GROUP_DIM, SSM_STATE), F32),
                        pltpu.VMEM((CHUNK, LANES), F32), pltpu.VMEM((CHUNK, LANES), F32),
                        pltpu.VMEM((LANES, CHUNK), F32)],
        name=name, compiler_params=_cp(2))(act, act, act, z, dtp, bias_p, a_p, d_p, normw)


def ssd_bwd(dyn, act, z, y_pre, states, dtp, bias_p, a_p, d_p, normw, name):
    s = act.shape[0]
    nc = s // CHUNK
    b_off = D_INNER // SSM_STATE
    c_off = b_off + SSM_GROUPS

    def body(dyn_ref, xs_ref, b_ref, c_ref, z_ref, y_ref, st_ref, dt_ref, bias_ref, a_ref, d_ref, nw_ref,
             dxs_ref, db_ref, dc_ref, dz_ref, ddt_ref, dnw_ref, dbias_ref, da_ref, dd_ref,
             dstate, dtall, csall, cst):
        c = pl.program_id(0)
        g = pl.program_id(1)

        @pl.when(g == 0)
        def _():
            _ssd_scalars(dt_ref, bias_ref, a_ref, dtall, csall, cst)
            ddt_ref[...] = jnp.zeros((CHUNK, LANES), F32)

        @pl.when(c == 0)
        def _():
            dstate[g] = jnp.zeros((GROUP_DIM, SSM_STATE), F32)

        @pl.when(jnp.logical_and(c == 0, g == 0))
        def _():
            dnw_ref[...] = jnp.zeros(dnw_ref.shape, F32)
            dbias_ref[...] = jnp.zeros((1, LANES), F32)
            da_ref[...] = jnp.zeros((1, LANES), F32)
            dd_ref[...] = jnp.zeros((1, LANES), F32)

        cs_cols = _chunk_cols(csall[...], g)
        dt_cols = _chunk_cols(dtall[...], g)
        cs_rows = [_sub_pick(cst[...], HEADS_PER_GROUP * g + r) for r in range(HEADS_PER_GROUP)]
        d_cols = _chunk_cols(d_ref[...], g)
        cs_exp = _expand_heads(cs_cols)
        dt_exp = _expand_heads(dt_cols)
        d_exp = _expand_heads(d_cols)
        xs = xs_ref[...]
        bb = b_ref[...].astype(BF16)
        cb16 = c_ref[...].astype(BF16)
        xdt = xs * dt_exp
        s_prev = st_ref[0, 0]
        s_prev16 = s_prev.astype(BF16)
        ds_next = dstate[g]
        ds16 = ds_next.astype(BF16)

        zf = z_ref[...]
        sz = _sigmoid(zf)
        silu_z = zf * sz
        y = y_ref[...]
        yg = y * silu_z
        dout = dyn_ref[...]
        dyg, dnw = _rms_bwd(dout, yg, nw_ref[...])
        dnw_ref[pl.ds(g, 1), :] += jnp.sum(dnw, axis=0, keepdims=True)
        dy = dyg * silu_z
        dz_ref[...] = dyg * y * (sz * (1.0 + zf * (1.0 - sz)))
        dd_ref[...] += jnp.sum(_heads_to_lanes(dy * xs, g), axis=0, keepdims=True)

        exp_cs = jnp.exp(cs_exp)
        decay_st = jnp.exp(cs_exp[CHUNK - 1:CHUNK, :] - cs_exp)
        cs_t = _dot(cb16, s_prev16, NT)
        dyo = dy * exp_cs
        dc_acc = _dot(dyo.astype(BF16), s_prev16, NN)
        g1 = _dot(bb, ds16, NT)
        xds = xdt * decay_st
        db_acc = _dot(xds.astype(BF16), ds16, NN)
        dxdt_off = g1 * decay_st
        t_exp = g1 * xds
        dcs_exp = dy * cs_t * exp_cs - t_exp
        decay_c = _decay_col(cs_cols)
        dstate[g] = decay_c * ds_next + _dot(dyo.astype(BF16), cb16, TN)
        dlast_col = jnp.sum(ds_next * s_prev, axis=1, keepdims=True) * decay_c
        jj = lax.broadcasted_iota(jnp.int32, (GROUP_DIM, LANES), 0)
        ll = lax.broadcasted_iota(jnp.int32, (GROUP_DIM, LANES), 1)
        sel = ll == HEADS_PER_GROUP * g + (jj >> 6)
        dlast = jnp.sum(jnp.where(sel, dlast_col, 0.0), axis=0, keepdims=True)
        t_all = _heads_to_lanes(t_exp, g)
        dlast += jnp.sum(t_all, axis=0, keepdims=True)
        dcs_all = _heads_to_lanes(dcs_exp, g)

        cbm = _dot(cb16, bb, NT)
        dcb = jnp.zeros((CHUNK, CHUNK), F32)
        dcs_rows = jnp.zeros((LANES, CHUNK), F32)
        lane_l = lax.broadcasted_iota(jnp.int32, (CHUNK, LANES), 1)
        sub_l = lax.broadcasted_iota(jnp.int32, (LANES, CHUNK), 0)
        dxdt_pairs = []
        for p in range(HEADS_PER_GROUP // 2):
            xpair16 = xdt[:, LANES * p:LANES * (p + 1)].astype(BF16)
            dypair = dy[:, LANES * p:LANES * (p + 1)]
            acc = None
            for r in (2 * p, 2 * p + 1):
                lm = _decay_mat(cs_cols[r], cs_rows[r])
                m = cbm * lm
                dyh = _head_mask(dypair, r % 2 == 1).astype(BF16)
                dm = _dot(dyh, xpair16, NT)
                dcb += dm * lm
                q = dm * m
                idx = HEADS_PER_GROUP * g + r
                dcs_all += jnp.where(lane_l == idx, jnp.sum(q, axis=1, keepdims=True), 0.0)
                dcs_rows -= jnp.where(sub_l == idx, jnp.sum(q, axis=0, keepdims=True), 0.0)
                part = _dot(m.astype(BF16), dyh, TN)
                acc = part if acc is None else acc + part
            dxdt_pairs.append(acc)
        dxdt = jnp.concatenate(dxdt_pairs, axis=1) + dxdt_off
        dcb16 = dcb.astype(BF16)
        dc_ref[...] = dc_acc + _dot(dcb16, bb, NN)
        db_ref[...] = db_acc + _dot(dcb16, cb16, TN)
        dxs_ref[...] = dxdt * dt_exp + dy * d_exp

        dcs_all += dcs_rows.T
        row = lax.broadcasted_iota(jnp.int32, (CHUNK, CHUNK), 0)
        col = lax.broadcasted_iota(jnp.int32, (CHUNK, CHUNK), 1)
        last_row = lax.broadcasted_iota(jnp.int32, (CHUNK, LANES), 0) == CHUNK - 1
        dcs_all += jnp.where(last_row, dlast, 0.0)
        da_all = _dot((col >= row).astype(F32), dcs_all, NN, HI)
        dta = dtall[...]
        in_group = jnp.logical_and(lane_l >= HEADS_PER_GROUP * g, lane_l < HEADS_PER_GROUP * (g + 1))
        ddt = jnp.where(in_group, da_all * a_ref[...] + _heads_to_lanes(dxdt * xs, g), 0.0)
        da_ref[...] += jnp.sum(jnp.where(in_group, da_all * dta, 0.0), axis=0, keepdims=True)
        ddt_raw = ddt * _sigmoid(dt_ref[...] + bias_ref[...])
        ddt_ref[...] += ddt_raw
        dbias_ref[...] += jnp.sum(ddt_raw, axis=0, keepdims=True)

    rev = lambda c, g: (nc - 1 - c, g)
    grp = pl.BlockSpec((CHUNK, GROUP_DIM), rev)
    st = pl.BlockSpec((CHUNK, SSM_STATE), rev)
    par = pl.BlockSpec((1, LANES), lambda c, g: (0, 0))
    dtb = pl.BlockSpec((CHUNK, LANES), lambda c, g: (nc - 1 - c, 0))
    f = lambda shape: jax.ShapeDtypeStruct(shape, F32)
    return pl.pallas_call(
        body,
        out_shape=(f((s, D_INNER)), f((s, SSM_GROUPS * SSM_STATE)), f((s, SSM_GROUPS * SSM_STATE)),
                   f((s, D_INNER)), f((s, LANES)), f((8, GROUP_DIM)), f((1, LANES)), f((1, LANES)), f((1, LANES))),
        grid=(nc, SSM_GROUPS),
        in_specs=[grp, grp,
                  pl.BlockSpec((CHUNK, SSM_STATE), lambda c, g: (nc - 1 - c, b_off + g)),
                  pl.BlockSpec((CHUNK, SSM_STATE), lambda c, g: (nc - 1 - c, c_off + g)),
                  grp, grp,
                  pl.BlockSpec((1, 1, GROUP_DIM, SSM_STATE), lambda c, g: (nc - 1 - c, g, 0, 0)),
                  dtb, par, par, par,
                  pl.BlockSpec((1, GROUP_DIM), lambda c, g: (0, g))],
        out_specs=(grp, st, st, grp, dtb,
                   pl.BlockSpec((8, GROUP_DIM), lambda c, g: (0, 0)), par, par, par),
        scratch_shapes=[pltpu.VMEM((SSM_GROUPS, GROUP_DIM, SSM_STATE), F32),
                        pltpu.VMEM((CHUNK, LANES), F32), pltpu.VMEM((CHUNK, LANES), F32),
                        pltpu.VMEM((LANES, CHUNK), F32)],
        name=name, compiler_params=_cp(2))(dyn, act, act, act, z, y_pre, states, dtp, bias_p, a_p, d_p, normw)


def _attn_probs(q, kp, kc, sink, n):
    sp = _dot(q, kp, NT)
    sc = _dot(q, kc, NT)
    i = lax.broadcasted_iota(jnp.int32, sp.shape, 0) & (WINDOW - 1)
    j = lax.broadcasted_iota(jnp.int32, sp.shape, 1)
    sp = jnp.where(jnp.logical_and(j > i, n > 0), sp, NEG)
    sc = jnp.where(j <= i, sc, NEG)
    m = jnp.maximum(jnp.maximum(jnp.max(sp, axis=1, keepdims=True), jnp.max(sc, axis=1, keepdims=True)), sink)
    pp = jnp.exp(sp - m)
    pc = jnp.exp(sc - m)
    ps = jnp.exp(sink - m)
    inv = 1.0 / (jnp.sum(pp, axis=1, keepdims=True) + jnp.sum(pc, axis=1, keepdims=True) + ps)
    return pp * inv, pc * inv, ps * inv


def attn_fwd(qt, kt, vt, sink_rows, name):
    s = qt.shape[1]
    nb = s // WINDOW
    rows = Q_PER_KV * WINDOW

    def body(q_ref, kp_ref, kc_ref, vp_ref, vc_ref, sk_ref, o_ref):
        n = pl.program_id(1)
        q = q_ref[...].reshape(rows, ATT_HEAD_DIM)
        pp, pc, _ = _attn_probs(q, kp_ref[0], kc_ref[0], sk_ref[0], n)
        o = _dot(pp.astype(BF16), vp_ref[0]) + _dot(pc.astype(BF16), vc_ref[0])
        o_ref[...] = o.reshape(Q_PER_KV, WINDOW, ATT_HEAD_DIM).astype(BF16)

    qsp = pl.BlockSpec((Q_PER_KV, WINDOW, ATT_HEAD_DIM), lambda h, n: (h, n, 0))
    prev = pl.BlockSpec((1, WINDOW, ATT_HEAD_DIM), lambda h, n: (h, jnp.maximum(n - 1, 0), 0))
    cur = pl.BlockSpec((1, WINDOW, ATT_HEAD_DIM), lambda h, n: (h, n, 0))
    return pl.pallas_call(
        body, out_shape=jax.ShapeDtypeStruct(qt.shape, BF16), grid=(N_KV_HEADS, nb),
        in_specs=[qsp, prev, cur, prev, cur, pl.BlockSpec((1, rows, 1), lambda h, n: (h, 0, 0))],
        out_specs=qsp, name=name, compiler_params=_cp(2))(qt, kt, kt, vt, vt, sink_rows)


def attn_bwd(qt, kt, vt, sink_rows, dot_, name):
    s = qt.shape[1]
    nb = s // WINDOW
    rows = Q_PER_KV * WINDOW

    def body(q_ref, kp_ref, kc_ref, vp_ref, vc_ref, sk_ref, do_ref, dq_ref, dk_ref, dv_ref, ds_ref, kacc, vacc):
        n = pl.program_id(1)

        @pl.when(n < nb)
        def _():
            q = q_ref[...].reshape(rows, ATT_HEAD_DIM)
            do = do_ref[...].reshape(rows, ATT_HEAD_DIM)
            kp, kc, vp, vc = kp_ref[0], kc_ref[0], vp_ref[0], vc_ref[0]
            pp, pc, ps = _attn_probs(q, kp, kc, sk_ref[0], n)
            dpp = _dot(do, vp, NT)
            dpc = _dot(do, vc, NT)
            delta = jnp.sum(pp * dpp, axis=1, keepdims=True) + jnp.sum(pc * dpc, axis=1, keepdims=True)
            dsp = (pp * (dpp - delta)).astype(BF16)
            dsc = (pc * (dpc - delta)).astype(BF16)
            dq = _dot(dsp, kp) + _dot(dsc, kc)
            dq_ref[...] = dq.reshape(Q_PER_KV, WINDOW, ATT_HEAD_DIM)
            dk_prev = _dot(dsp, q, TN)
            dv_prev = _dot(pp.astype(BF16), do, TN)
            @pl.when(n == 0)
            def _():
                dk_ref[0] = dk_prev
                dv_ref[0] = dv_prev

            @pl.when(n > 0)
            def _():
                dk_ref[0] = kacc[...] + dk_prev
                dv_ref[0] = vacc[...] + dv_prev

            kacc[...] = _dot(dsc, q, TN)
            vacc[...] = _dot(pc.astype(BF16), do, TN)
            dsk = -ps * delta
            sub = lax.broadcasted_iota(jnp.int32, (8, LANES), 0)
            tile = jnp.zeros((8, LANES), F32)
            for h in range(Q_PER_KV):
                tile += jnp.where(sub == h, jnp.sum(dsk[h * WINDOW:(h + 1) * WINDOW, :], axis=0, keepdims=True), 0.0)
            ds_ref[0, 0] = tile

        @pl.when(n == nb)
        def _():
            dk_ref[0] = kacc[...]
            dv_ref[0] = vacc[...]
            ds_ref[0, 0] = jnp.zeros((8, LANES), F32)

    last = nb - 1
    qsp = pl.BlockSpec((Q_PER_KV, WINDOW, ATT_HEAD_DIM), lambda h, n: (h, jnp.minimum(n, last), 0))
    prev = pl.BlockSpec((1, WINDOW, ATT_HEAD_DIM), lambda h, n: (h, jnp.clip(n - 1, 0, last), 0))
    cur = pl.BlockSpec((1, WINDOW, ATT_HEAD_DIM), lambda h, n: (h, jnp.minimum(n, last), 0))
    dkv = pl.BlockSpec((1, WINDOW, ATT_HEAD_DIM), lambda h, n: (h, jnp.maximum(n - 1, 0), 0))
    f = lambda shape: jax.ShapeDtypeStruct(shape, F32)
    return pl.pallas_call(
        body, out_shape=(f(qt.shape), f(kt.shape), f(vt.shape), f((N_KV_HEADS, nb + 1, 8, LANES))),
        grid=(N_KV_HEADS, nb + 1),
        in_specs=[qsp, prev, cur, prev, cur, pl.BlockSpec((1, rows, 1), lambda h, n: (h, 0, 0)), qsp],
        out_specs=(qsp, dkv, dkv, pl.BlockSpec((1, 1, 8, LANES), lambda h, n: (h, n, 0, 0))),
        scratch_shapes=[pltpu.VMEM((WINDOW, ATT_HEAD_DIM), F32), pltpu.VMEM((WINDOW, ATT_HEAD_DIM), F32)],
        name=name, compiler_params=_cp(2))(qt, kt, kt, vt, vt, sink_rows, dot_)


def loss_head(x, w, tgt, name):
    s, d = x.shape
    tm = _row_tile(s, 256)

    def body(x_ref, w_ref, t_ref, loss_ref, dx_ref, dw_ref):
        i = pl.program_id(0)
        xf = x_ref[...]
        wv = w_ref[...]
        r = lax.rsqrt(jnp.mean(xf * xf, axis=-1, keepdims=True) + EPS)
        xhat = xf * r
        e = xhat * wv - t_ref[...]
        part = 0.5 * jnp.sum(jnp.mean(e * e, axis=-1, keepdims=True), axis=0, keepdims=True)
        dy = e * (1.0 / d)
        dxhat = dy * wv
        dx_ref[...] = r * (dxhat - xhat * jnp.mean(dxhat * xhat, axis=-1, keepdims=True))
        col = jnp.sum(dy * xhat, axis=0, keepdims=True)

        @pl.when(i == 0)
        def _():
            loss_ref[...] = jnp.broadcast_to(part, (1, LANES))
            dw_ref[...] = col

        @pl.when(i > 0)
        def _():
            loss_ref[...] += jnp.broadcast_to(part, (1, LANES))
            dw_ref[...] += col

    row = pl.BlockSpec((tm, d), lambda i: (i, 0))
    vec = pl.BlockSpec((1, d), lambda i: (0, 0))
    return pl.pallas_call(
        body, out_shape=(jax.ShapeDtypeStruct((1, LANES), F32), jax.ShapeDtypeStruct((s, d), F32),
                         jax.ShapeDtypeStruct((1, d), F32)),
        grid=(s // tm,), in_specs=[row, vec, row],
        out_specs=(pl.BlockSpec((1, LANES), lambda i: (0, 0)), row, vec),
        name=name, compiler_params=_cp(1))(x, w.reshape(1, d), tgt)


def _heads_major(t, n_heads):
    s = t.shape[0]
    return t.reshape(s, n_heads, ATT_HEAD_DIM).transpose(1, 0, 2)


def _tokens_major(t):
    h, s, dh = t.shape
    return t.transpose(1, 0, 2).reshape(s, h * dh)


def _pad_lanes(v):
    return jnp.pad(v.reshape(1, -1), ((0, 0), (0, LANES - v.size)))


def _ffn_fwd(h, wts, x, tag, norm_ws):
    wg, wu, wd = wts
    g, u, a = ffn_up(h, wg, wu, "ffn_up_" + tag)
    outs = mm_res(a, wd, x, 0.5, "ffn_down_" + tag, norm_ws=norm_ws)
    return outs, (g, u, a)


def _ffn_bwd(dxn, wts, saved, h, x_in, nw, tag):
    wg, wu, wd = wts
    g, u, a = saved
    dg, du = ffn_bwd_act(dxn, wd, g, u, "ffn_bwd_act_" + tag)
    d_wd = mm_tn(a, dxn[None], "ffn_dwd_" + tag, scale=0.5)
    dx, dnw = mm_nt_rms([(dg, wg), (du, wu)], dxn, x_in, nw, "ffn_bwd_dh_" + tag)
    d_wg = mm_tn(h[None], dg, "ffn_dwg_" + tag)
    d_wu = mm_tn(h[None], du, "ffn_dwu_" + tag)
    return dx, dnw, (d_wg, d_wu, d_wd)


def local_step(x, tgt, p):
    s = x.shape[0]
    nw = p["norm_w"]
    cos, sin = rope_tables(s)
    ffn = lambda l, i: (p["ffn_w_gate"][l][i], p["ffn_w_up"][l][i], p["ffn_w_down"][l][i])
    grads = {}

    h00 = rmsnorm_fwd(x, nw[0, 0], "norm_in")
    (x1, h01), sv00 = _ffn_fwd(h00, ffn(0, 0), x, "00", [nw[0, 1]])
    zz = mm_nn(h01, p["w_z"], "ssm_in_z")
    xbc = mm_nn(h01, p["w_xbc"], "ssm_in_xbc")
    dtp = mm_nn(h01, p["w_dt"], "ssm_in_dt")
    act = conv_fwd(xbc, p["ssm_conv_w"], p["ssm_conv_b"], "ssm_conv")
    bias_p = _pad_lanes(p["ssm_dt_bias"])
    a_p = _pad_lanes(-jnp.exp(p["ssm_a_log"]))
    d_p = _pad_lanes(p["ssm_d"])
    ssm_nw = p["ssm_norm_w"].reshape(1, D_INNER)
    yn, y_pre, states = ssd_fwd(act, zz, dtp, bias_p, a_p, d_p, ssm_nw, "ssd_fwd")
    x2, h02 = mm_res(yn[None], p["ssm_w_out"][None], x1, 1.0, "ssm_out", norm_ws=[nw[0, 2]])
    (x3, hkv, h10), sv01 = _ffn_fwd(h02, ffn(0, 1), x2, "01", [p["kv_norm_w"], nw[1, 0]])

    k_rot = rope_apply(mm_nn(hkv, p["w_k"], "kv_k", bias=p["b_k"]), cos, sin, "rope_k")
    v = mm_nn(hkv, p["w_v"], "kv_v", bias=p["b_v"], out_dtype=BF16)
    kt = _heads_major(k_rot, N_KV_HEADS)
    vt = _heads_major(v, N_KV_HEADS)

    (x4, h11), sv10 = _ffn_fwd(h10, ffn(1, 0), x3, "10", [nw[1, 1]])
    scale = 1.0 / math.sqrt(ATT_HEAD_DIM)
    q_rot = rope_apply(mm_nn(h11, p["attn_w_q"], "attn_q", bias=p["attn_b_q"]), cos, sin, "rope_q", scale=scale)
    qt = _heads_major(q_rot, N_Q_HEADS)
    sink_rows = jnp.repeat(p["attn_sinks"].reshape(N_KV_HEADS, Q_PER_KV), WINDOW, axis=1).reshape(
        N_KV_HEADS, Q_PER_KV * WINDOW, 1)
    o = _tokens_major(attn_fwd(qt, kt, vt, sink_rows, "attn_fwd"))
    x5, h12 = mm_res(o[None], p["attn_w_o"][None], x4, 1.0, "attn_out", bias=p["attn_b_o"], norm_ws=[nw[1, 2]])
    (x6,), sv11 = _ffn_fwd(h12, ffn(1, 1), x5, "11", [])

    loss_v, dx6, d_final = loss_head(x6, p["final_norm_w"], tgt, "loss_head")
    grads["final_norm_w"] = d_final[0]

    dnw = [[None] * 3 for _ in range(2)]
    dffn = {}
    dx5, dnw[1][2], dffn[(1, 1)] = _ffn_bwd(dx6, ffn(1, 1), sv11, h12, x5, nw[1, 2], "11")
    grads["attn_b_o"] = colsum(dx5, "attn_dbo")
    grads["attn_w_o"] = mm_tn(o[None], dx5[None], "attn_dwo")[0]
    do = mm_nt(dx5, p["attn_w_o"], "attn_do", out_dtype=BF16)
    dqt, dkt, dvt, dsink = attn_bwd(qt, kt, vt, sink_rows, _heads_major(do, N_Q_HEADS), "attn_bwd")
    grads["attn_sinks"] = jnp.sum(dsink[:, :, :Q_PER_KV, 0], axis=1).reshape(N_Q_HEADS)
    dq_pre = rope_apply(_tokens_major(dqt), cos, sin, "rope_dq", inverse=True, scale=scale, out_dtype=F32)
    grads["attn_b_q"] = colsum(dq_pre, "attn_dbq")
    grads["attn_w_q"] = mm_tn(h11[None], dq_pre[None], "attn_dwq")[0]
    dx4, dnw[1][1] = mm_nt_rms([(dq_pre[None], p["attn_w_q"][None])], dx5, x4, nw[1, 1], "attn_bwd_dh")
    dx3, dnw[1][0], dffn[(1, 0)] = _ffn_bwd(dx4, ffn(1, 0), sv10, h10, x3, nw[1, 0], "10")

    dk_pre = rope_apply(_tokens_major(dkt), cos, sin, "rope_dk", inverse=True, out_dtype=F32)
    dv = _tokens_major(dvt)
    grads["b_k"] = colsum(dk_pre, "kv_dbk")
    grads["b_v"] = colsum(dv, "kv_dbv")
    grads["w_k"] = mm_tn(hkv[None], dk_pre[None], "kv_dwk")[0]
    grads["w_v"] = mm_tn(hkv[None], dv[None], "kv_dwv")[0]
    dx3, grads["kv_norm_w"] = mm_nt_rms([(dk_pre[None], p["w_k"][None]), (dv[None], p["w_v"][None])],
                                        dx3, x3, p["kv_norm_w"], "kv_bwd_dh")

    dx2, dnw[0][2], dffn[(0, 1)] = _ffn_bwd(dx3, ffn(0, 1), sv01, h02, x2, nw[0, 2], "01")
    grads["ssm_w_out"] = mm_tn(yn[None], dx2[None], "ssm_dwout")[0]
    dyn = mm_nt(dx2, p["ssm_w_out"], "ssm_dyn")
    dxs, db_, dc_, dz, ddt, d_ssm_nw, d_bias, d_a, d_d = ssd_bwd(
        dyn, act, zz, y_pre, states, dtp, bias_p, a_p, d_p, ssm_nw, "ssd_bwd")
    grads["ssm_norm_w"] = d_ssm_nw[:SSM_GROUPS].reshape(D_INNER)
    grads["ssm_dt_bias"] = d_bias[0, :SSM_HEADS]
    grads["ssm_a_log"] = d_a[0, :SSM_HEADS] * a_p[0, :SSM_HEADS]
    grads["ssm_d"] = d_d[0, :SSM_HEADS]
    dact = jnp.concatenate([dxs, db_, dc_], axis=1)
    dxbc, grads["ssm_conv_w"], grads["ssm_conv_b"] = conv_bwd(
        dact, xbc, p["ssm_conv_w"], p["ssm_conv_b"], "ssm_conv_bwd")
    grads["w_z"] = mm_tn(h01[None], dz[None], "ssm_dwz")[0]
    grads["w_xbc"] = mm_tn(h01[None], dxbc[None], "ssm_dwxbc")[0]
    grads["w_dt"] = mm_tn(h01[None], ddt[None], "ssm_dwdt")[0]
    dx1, dnw[0][1] = mm_nt_rms(
        [(dz[None], p["w_z"][None]), (dxbc[None], p["w_xbc"][None]), (ddt[None], p["w_dt"][None])],
        dx2, x1, nw[0, 1], "ssm_bwd_dh")
    grad_x, dnw[0][0], dffn[(0, 0)] = _ffn_bwd(dx1, ffn(0, 0), sv00, h00, x, nw[0, 0], "00")

    grads["norm_w"] = jnp.stack([jnp.stack(r) for r in dnw])
    grads["ffn"] = dffn
    return loss_v[0, 0], grad_x, grads


MESH = pl.DeviceIdType.MESH
ANY = pl.BlockSpec(memory_space=pl.ANY)


def _place():
    x, y, c = lax.axis_index("x"), lax.axis_index("y"), lax.axis_index("c")
    others = [(1 - x, y), (x, 1 - y), (1 - x, 1 - y)]
    return x, y, c, 2 * x + y, others


def _hbm_like(shape, dtype):
    return jax.ShapeDtypeStruct(shape, dtype)


def all_gather_chips(arrs, name):
    n = len(arrs)

    def body(*refs):
        ins, outs = refs[:n], refs[n:2 * n]
        send_sems, recv_sems, local_sems = refs[2 * n:]
        x, y, c, k, others = _place()
        sibling = (x, y, 1 - c)
        started = []
        for a in range(n):
            lc = pltpu.make_async_copy(ins[a], outs[a].at[k], local_sems.at[a])
            lc.start()
            started.append(lc)
        sends = []
        for a in range(n):
            for j, (px, py) in enumerate(others):
                cp = pltpu.make_async_remote_copy(
                    src_ref=ins[a].at[c], dst_ref=outs[a].at[k, c], send_sem=send_sems.at[a, j],
                    recv_sem=recv_sems.at[a, j], device_id=(px, py, c), device_id_type=MESH)
                cp.start()
                sends.append(cp)
        for a in range(n):
            for j, (px, py) in enumerate(others):
                blk = outs[a].at[2 * px + py, c]
                pltpu.make_async_remote_copy(
                    src_ref=blk, dst_ref=blk, send_sem=send_sems.at[a, j], recv_sem=recv_sems.at[a, j],
                    device_id=(px, py, c), device_id_type=MESH).wait_recv()
                fw = pltpu.make_async_remote_copy(
                    src_ref=blk, dst_ref=blk, send_sem=send_sems.at[a, 3 + j], recv_sem=recv_sems.at[a, 3 + j],
                    device_id=sibling, device_id_type=MESH)
                fw.start()
                sends.append(fw)
        for a in range(n):
            for j, (px, py) in enumerate(others):
                blk = outs[a].at[2 * px + py, 1 - c]
                pltpu.make_async_remote_copy(
                    src_ref=blk, dst_ref=blk, send_sem=send_sems.at[a, 3 + j], recv_sem=recv_sems.at[a, 3 + j],
                    device_id=sibling, device_id_type=MESH).wait_recv()
        for cp in sends:
            cp.wait_send()
        for lc in started:
            lc.wait()

    return pl.pallas_call(
        body, out_shape=tuple(_hbm_like((N_CHIPS,) + a.shape, a.dtype) for a in arrs),
        in_specs=[ANY] * n, out_specs=tuple([ANY] * n),
        scratch_shapes=[pltpu.SemaphoreType.DMA((n, 6)), pltpu.SemaphoreType.DMA((n, 6)),
                        pltpu.SemaphoreType.DMA((n,))],
        name=name)(*arrs)


def pair_swap(arrs, name):
    n = len(arrs)

    def body(*refs):
        ins, outs = refs[:n], refs[n:2 * n]
        send_sems, recv_sems = refs[2 * n:]
        x, y, c, _, _ = _place()
        cps = []
        for a in range(n):
            cp = pltpu.make_async_remote_copy(
                src_ref=ins[a].at[1 - c], dst_ref=outs[a], send_sem=send_sems.at[a], recv_sem=recv_sems.at[a],
                device_id=(x, y, 1 - c), device_id_type=MESH)
            cp.start()
            cps.append(cp)
        for cp in cps:
            cp.wait()

    return pl.pallas_call(
        body, out_shape=tuple(_hbm_like(a.shape[1:], a.dtype) for a in arrs),
        in_specs=[ANY] * n, out_specs=tuple([ANY] * n),
        scratch_shapes=[pltpu.SemaphoreType.DMA((n,)), pltpu.SemaphoreType.DMA((n,))],
        name=name)(*arrs)


def chip_scatter(arrs, name):
    n = len(arrs)

    def body(*refs):
        ins, outs = refs[:n], refs[n:2 * n]
        send_sems, recv_sems, local_sems = refs[2 * n:]
        x, y, c, k, others = _place()
        started, sends = [], []
        for a in range(n):
            lc = pltpu.make_async_copy(ins[a].at[k], outs[a].at[k], local_sems.at[a])
            lc.start()
            started.append(lc)
            for j, (px, py) in enumerate(others):
                cp = pltpu.make_async_remote_copy(
                    src_ref=ins[a].at[2 * px + py], dst_ref=outs[a].at[k], send_sem=send_sems.at[a, j],
                    recv_sem=recv_sems.at[a, j], device_id=(px, py, c), device_id_type=MESH)
                cp.start()
                sends.append(cp)
        for a in range(n):
            for j, (px, py) in enumerate(others):
                blk = outs[a].at[2 * px + py]
                pltpu.make_async_remote_copy(
                    src_ref=blk, dst_ref=blk, send_sem=send_sems.at[a, j], recv_sem=recv_sems.at[a, j],
                    device_id=(px, py, c), device_id_type=MESH).wait_recv()
        for cp in sends:
            cp.wait_send()
        for lc in started:
            lc.wait()

    return pl.pallas_call(
        body, out_shape=tuple(_hbm_like(a.shape, a.dtype) for a in arrs),
        in_specs=[ANY] * n, out_specs=tuple([ANY] * n),
        scratch_shapes=[pltpu.SemaphoreType.DMA((n, 3)), pltpu.SemaphoreType.DMA((n, 3)),
                        pltpu.SemaphoreType.DMA((n,))],
        name=name)(*arrs)


def pair_gather(arrs, name):
    n = len(arrs)

    def body(*refs):
        ins, outs = refs[:n], refs[n:2 * n]
        send_sems, recv_sems, local_sems = refs[2 * n:]
        x, y, c, _, _ = _place()
        started, sends = [], []
        for a in range(n):
            lc = pltpu.make_async_copy(ins[a], outs[a].at[c], local_sems.at[a])
            lc.start()
            started.append(lc)
            cp = pltpu.make_async_remote_copy(
                src_ref=ins[a], dst_ref=outs[a].at[c], send_sem=send_sems.at[a], recv_sem=recv_sems.at[a],
                device_id=(x, y, 1 - c), device_id_type=MESH)
            cp.start()
            sends.append(cp)
        for a in range(n):
            blk = outs[a].at[1 - c]
            pltpu.make_async_remote_copy(
                src_ref=blk, dst_ref=blk, send_sem=send_sems.at[a], recv_sem=recv_sems.at[a],
                device_id=(x, y, 1 - c), device_id_type=MESH).wait_recv()
        for cp in sends:
            cp.wait_send()
        for lc in started:
            lc.wait()

    return pl.pallas_call(
        body, out_shape=tuple(_hbm_like((2,) + a.shape, a.dtype) for a in arrs),
        in_specs=[ANY] * n, out_specs=tuple([ANY] * n),
        scratch_shapes=[pltpu.SemaphoreType.DMA((n,)), pltpu.SemaphoreType.DMA((n,)),
                        pltpu.SemaphoreType.DMA((n,))],
        name=name)(*arrs)


def all_reduce_small(buf, name):
    r = buf.shape[0]
    n_dev = 8

    def body(in_ref, o_ref, land, send_sems, recv_sems):
        x, y, c, _, _ = _place()
        me = 4 * x + 2 * y + c
        land[me] = in_ref[...]
        sends = []
        for d in range(1, n_dev):
            peer = (x ^ (d >> 2), y ^ ((d >> 1) & 1), c ^ (d & 1))
            cp = pltpu.make_async_remote_copy(
                src_ref=in_ref, dst_ref=land.at[me], send_sem=send_sems.at[d], recv_sem=recv_sems.at[d],
                device_id=peer, device_id_type=MESH)
            cp.start()
            sends.append(cp)
        for d in range(1, n_dev):
            blk = land.at[me ^ d]
            pltpu.make_async_remote_copy(
                src_ref=blk, dst_ref=blk, send_sem=send_sems.at[d], recv_sem=recv_sems.at[d],
                device_id=(x, y, c), device_id_type=MESH).wait_recv()
        for cp in sends:
            cp.wait_send()
        tot = land[0]
        for d in range(1, n_dev):
            tot = tot + land[d]
        o_ref[...] = tot

    vm = pl.BlockSpec(memory_space=pltpu.VMEM)
    return pl.pallas_call(
        body, out_shape=jax.ShapeDtypeStruct(buf.shape, F32), in_specs=[vm], out_specs=vm,
        scratch_shapes=[pltpu.VMEM((n_dev, r, LANES), F32), pltpu.SemaphoreType.DMA((n_dev,)),
                        pltpu.SemaphoreType.DMA((n_dev,))],
        name=name)(buf)


def _tile_rows(r, c, max_elems=262144, mult=16):
    best = None
    for t in range(mult, r + 1, mult):
        if r % t == 0 and t * c <= max_elems:
            best = t
    return best or r


def add_pair(xh, p, c_idx, name):
    _, r, c = xh.shape
    tr = _tile_rows(r, c)

    def body(c_ref, x_ref, p_ref, o_ref):
        o_ref[...] = (x_ref[0].astype(F32) + p_ref[...].astype(F32)).astype(BF16)

    blk = pl.BlockSpec((tr, c), lambda i, cr: (i, 0))
    return pl.pallas_call(
        body, out_shape=jax.ShapeDtypeStruct((r, c), BF16),
        grid_spec=pltpu.PrefetchScalarGridSpec(
            num_scalar_prefetch=1, grid=(r // tr,),
            in_specs=[pl.BlockSpec((1, tr, c), lambda i, cr: (cr[0], i, 0)), blk], out_specs=blk),
        name=name, compiler_params=_cp(1))(c_idx, xh, p)


def sum_chips(q, name):
    _, r, c = q.shape
    tr = _tile_rows(r, c)

    def body(q_ref, o_ref):
        t = q_ref[0].astype(F32) + q_ref[1].astype(F32)
        t = t + q_ref[2].astype(F32)
        o_ref[...] = t + q_ref[3].astype(F32)

    return pl.pallas_call(
        body, out_shape=jax.ShapeDtypeStruct((r, c), F32), grid=(r // tr,),
        in_specs=[pl.BlockSpec((N_CHIPS, tr, c), lambda i: (0, i, 0))],
        out_specs=pl.BlockSpec((tr, c), lambda i: (i, 0)),
        name=name, compiler_params=_cp(1))(q)


def adamw(w, g, m, v, name):
    r, c = w.shape
    tr = _tile_rows(r, c, max_elems=131072, mult=8)
    c1 = 1.0 / (1.0 - ADAM_B1 ** ADAM_STEP)
    c2 = 1.0 / (1.0 - ADAM_B2 ** ADAM_STEP)

    def body(w_ref, g_ref, m_ref, v_ref, d_ref, mo_ref, vo_ref):
        gf = g_ref[...]
        mn = ADAM_B1 * m_ref[...] + (1.0 - ADAM_B1) * gf
        vn = ADAM_B2 * v_ref[...] + (1.0 - ADAM_B2) * (gf * gf)
        mo_ref[...] = mn
        vo_ref[...] = vn
        d_ref[...] = -ADAM_LR * ((mn * c1) / (jnp.sqrt(vn * c2) + ADAM_EPS) + ADAM_WD * w_ref[...])

    blk = pl.BlockSpec((tr, c), lambda i: (i, 0))
    out = jax.ShapeDtypeStruct((r, c), F32)
    return pl.pallas_call(
        body, out_shape=(out, out, out), grid=(r // tr,), in_specs=[blk] * 4, out_specs=(blk, blk, blk),
        name=name, compiler_params=_cp(1))(w, g, m, v)


WEIGHTS = ['norm_w', 'ffn_w_gate', 'ffn_w_up', 'ffn_w_down', 'ssm_w_in', 'ssm_conv_w', 'ssm_conv_b', 'ssm_dt_bias',
           'ssm_a_log', 'ssm_d', 'ssm_norm_w', 'ssm_w_out', 'kv_norm_w', 'w_k', 'b_k', 'w_v', 'b_v', 'attn_w_q',
           'attn_b_q', 'attn_sinks', 'attn_w_o', 'attn_b_o', 'final_norm_w']
BIG = ['ffn_w_gate', 'ffn_w_up', 'ffn_w_down', 'ssm_w_in', 'ssm_w_out', 'w_k', 'w_v', 'attn_w_q', 'attn_w_o']
SMALL = [n for n in WEIGHTS if n not in BIG]
SMALL_SHARDED = {'norm_w': 2, 'ssm_conv_w': 2, 'ssm_conv_b': 1, 'ssm_norm_w': 1}
ROW_ALIGN = 8 * LANES


def _pack_rows(parts):
    flat = jnp.concatenate([p.reshape(-1).astype(F32) for p in parts])
    pad = (-flat.size) % ROW_ALIGN
    return jnp.pad(flat, (0, pad)).reshape(-1, LANES)


def _unpack_rows(buf, shapes):
    flat = buf.reshape(-1)
    out, pos = [], 0
    for shp in shapes:
        size = math.prod(shp)
        out.append(flat[pos:pos + size].reshape(shp))
        pos += size
    return out


def _as2d(a):
    return a.reshape(-1, a.shape[-1])


def kernel(*args):
    names = (['x'] + WEIGHTS + ['loss_target'] + ['m_' + n for n in WEIGHTS] + ['v_' + n for n in WEIGHTS])
    a = dict(zip(names, args))
    xi, yi, ci = lax.axis_index("x"), lax.axis_index("y"), lax.axis_index("c")
    chip = 2 * xi + yi
    c_idx = jnp.reshape(ci, (1,)).astype(jnp.int32)
    s = a['x'].shape[1]

    ga = jnp.concatenate([a['ffn_w_gate'].reshape(4, D_MODEL, FF_SHARD), a['ffn_w_up'].reshape(4, D_MODEL, FF_SHARD)],
                         0).astype(BF16).reshape(2, 4, D_MODEL, FF_SHARD)
    gb = jnp.concatenate([a['ffn_w_down'].reshape(4 * FF_SHARD, D_MODEL), a['ssm_w_out'][0], a['attn_w_q'][0],
                          a['attn_w_o'][0]], 0).astype(BF16)
    rows_b = gb.shape[0]
    gb = gb.reshape(2, rows_b // 2, D_MODEL)
    gc = a['ssm_w_in'][0].astype(BF16).reshape(2, D_MODEL // 2, IN_SHARD)
    gd = jnp.concatenate([a['w_k'], a['w_v']], 0).astype(BF16).reshape(2, 256, 256)
    small_shard_names = list(SMALL_SHARDED)
    gs = _pack_rows([a[n] for n in small_shard_names])
    gs = gs.reshape(2, gs.shape[0] // 2, LANES)
    fa, fb, fc, fd, fs = all_gather_chips([ga, gb, gc, gd, gs], "gather_weights")

    p = {}
    fa = fa.reshape(N_CHIPS, 2, 2, 2, D_MODEL, FF_SHARD)
    p['ffn_w_gate'] = [[fa[:, 0, l, i] for i in range(2)] for l in range(2)]
    p['ffn_w_up'] = [[fa[:, 1, l, i] for i in range(2)] for l in range(2)]
    fb = fb.reshape(N_CHIPS, rows_b, D_MODEL)
    down = fb[:, :4 * FF_SHARD].reshape(N_CHIPS, 2, 2, FF_SHARD, D_MODEL)
    p['ffn_w_down'] = [[down[:, l, i] for i in range(2)] for l in range(2)]
    r0 = 4 * FF_SHARD
    p['ssm_w_out'] = fb[:, r0:r0 + 512].reshape(D_INNER, D_MODEL)
    p['attn_w_q'] = fb[:, r0 + 512:r0 + 768].reshape(D_MODEL, D_MODEL)
    p['attn_w_o'] = fb[:, r0 + 768:r0 + 1024].reshape(D_MODEL, D_MODEL)
    w_in = fc.reshape(N_CHIPS, D_MODEL, IN_SHARD).transpose(1, 0, 2).reshape(D_MODEL, IN_PROJ_DIM)
    p['w_z'] = w_in[:, :D_INNER]
    p['w_xbc'] = w_in[:, D_INNER:D_INNER + CONV_DIM]
    p['w_dt'] = jnp.pad(w_in[:, D_INNER + CONV_DIM:], ((0, 0), (0, LANES - SSM_HEADS)))
    fd = fd.reshape(N_CHIPS, 512, 256)
    p['w_k'] = fd[:, :256].reshape(D_MODEL, 256)
    p['w_v'] = fd[:, 256:].reshape(D_MODEL, 256)
    per_chip = [_unpack_rows(fs[k], [a[n].shape for n in small_shard_names]) for k in range(N_CHIPS)]
    for idx, n in enumerate(small_shard_names):
        p[n] = jnp.concatenate([per_chip[k][idx] for k in range(N_CHIPS)], axis=SMALL_SHARDED[n])
    p['ssm_conv_w'], p['ssm_conv_b'], p['ssm_norm_w'] = p['ssm_conv_w'][0], p['ssm_conv_b'][0], p['ssm_norm_w'][0]
    for n in ('ssm_dt_bias', 'ssm_a_log', 'ssm_d', 'attn_b_q', 'attn_sinks', 'attn_b_o'):
        p[n] = a[n][0]
    for n in ('kv_norm_w', 'b_k', 'b_v', 'final_norm_w'):
        p[n] = a[n]

    loss_local, grad_x, g = local_step(a['x'][0], a['loss_target'][0], p)
    loss = lax.psum(loss_local, ("x", "y", "c"))

    dffn = g['ffn']
    ra = jnp.stack([dffn[(l, i)][w] for w in range(2) for l in range(2) for i in range(2)], axis=1)
    ra = ra.reshape(N_CHIPS, 2, 4 * D_MODEL, FF_SHARD).transpose(1, 0, 2, 3)
    rb = jnp.concatenate(
        [jnp.concatenate([dffn[(l, i)][2] for l in range(2) for i in range(2)], axis=1),
         g['ssm_w_out'].reshape(N_CHIPS, 512, D_MODEL), g['attn_w_q'].reshape(N_CHIPS, 256, D_MODEL),
         g['attn_w_o'].reshape(N_CHIPS, 256, D_MODEL)], axis=1)
    rb = rb.reshape(N_CHIPS, 2, rows_b // 2, D_MODEL).transpose(1, 0, 2, 3)
    d_in = jnp.concatenate([g['w_z'], g['w_xbc'], g['w_dt'][:, :SSM_HEADS]], axis=1)
    rc = d_in.reshape(D_MODEL, N_CHIPS, IN_SHARD).transpose(1, 0, 2).reshape(
        N_CHIPS, 2, D_MODEL // 2, IN_SHARD).transpose(1, 0, 2, 3)
    rd = jnp.concatenate([g['w_k'].reshape(N_CHIPS, 256, 256), g['w_v'].reshape(N_CHIPS, 256, 256)], axis=1)
    rd = rd.reshape(N_CHIPS, 2, 256, 256).transpose(1, 0, 2, 3)
    groups = [ra, rb, rc, rd]
    from_sibling = pair_swap(groups, "rs_pair_swap")
    chip_parts = []
    for idx, (grp, sib) in enumerate(zip(groups, from_sibling)):
        t = add_pair(grp.reshape(2, -1, grp.shape[-1]), _as2d(sib), c_idx, "rs_add_pair_%d" % idx)
        chip_parts.append(t.reshape(grp.shape[1:]))
    landed = chip_scatter(chip_parts, "rs_chip_scatter")
    halves = []
    for idx, q in enumerate(landed):
        t = sum_chips(q.reshape(N_CHIPS, -1, q.shape[-1]), "rs_sum_chips_%d" % idx)
        halves.append(t.reshape(q.shape[1:]))
    sa, sb, sc, sd = pair_gather(halves, "rs_pair_gather")

    gw = {}
    sa = sa.reshape(2, 2, 2, D_MODEL, FF_SHARD)
    gw['ffn_w_gate'], gw['ffn_w_up'] = sa[0], sa[1]
    sb = sb.reshape(rows_b, D_MODEL)
    gw['ffn_w_down'] = sb[:r0].reshape(2, 2, FF_SHARD, D_MODEL)
    gw['ssm_w_out'] = sb[r0:r0 + 512][None]
    gw['attn_w_q'] = sb[r0 + 512:r0 + 768][None]
    gw['attn_w_o'] = sb[r0 + 768:r0 + 1024][None]
    gw['ssm_w_in'] = sc.reshape(1, D_MODEL, IN_SHARD)
    sd = sd.reshape(512, 256)
    gw['w_k'], gw['w_v'] = sd[:256], sd[256:]

    g_small = {n: g[n] for n in SMALL}
    full_shapes = {n: g_small[n].shape for n in SMALL}
    red = all_reduce_small(_pack_rows([g_small[n] for n in SMALL]), "reduce_vectors")
    for n, t in zip(SMALL, _unpack_rows(red, [full_shapes[n] for n in SMALL])):
        if n in SMALL_SHARDED:
            ax = SMALL_SHARDED[n] - (a[n].ndim - t.ndim)
            width = a[n].shape[SMALL_SHARDED[n]]
            t = lax.dynamic_slice_in_dim(t, chip * width, width, axis=ax)
        gw[n] = t.reshape(a[n].shape)

    delta, new_m, new_v = {}, {}, {}
    for n in BIG:
        d, mo, vo = adamw(_as2d(a[n]), _as2d(gw[n]), _as2d(a['m_' + n]), _as2d(a['v_' + n]), "adamw_" + n)
        delta[n], new_m[n], new_v[n] = d.reshape(a[n].shape), mo.reshape(a[n].shape), vo.reshape(a[n].shape)
    shapes = [a[n].shape for n in SMALL]
    packed = [_pack_rows([src[n] for n in SMALL]) for src in
              (a, gw, {n: a['m_' + n] for n in SMALL}, {n: a['v_' + n] for n in SMALL})]
    outs = adamw(*packed, "adamw_vectors")
    for dst, buf in zip((delta, new_m, new_v), outs):
        for n, t in zip(SMALL, _unpack_rows(buf, shapes)):
            dst[n] = t

    return (loss, grad_x[None], *[gw[n] for n in WEIGHTS], *[delta[n] for n in WEIGHTS],
            *[new_m[n] for n in WEIGHTS], *[new_v[n] for n in WEIGHTS])
```

```python
import functools
import math

import jax
import jax.numpy as jnp
from jax import lax
from jax.experimental import pallas as pl
from jax.experimental.pallas import tpu as pltpu

F32 = jnp.float32
BF16 = jnp.bfloat16
HI = lax.Precision.HIGHEST

D_MODEL = 1024
D_INNER = 2048
SSM_HEADS = 32
SSM_GROUPS = 4
HEADS_PER_GROUP = SSM_HEADS // SSM_GROUPS
SSM_HEAD_DIM = 64
SSM_STATE = 128
GROUP_DIM = D_INNER // SSM_GROUPS
CONV_DIM = D_INNER + 2 * SSM_GROUPS * SSM_STATE
CONV_WIDTH = 4
CHUNK = 128
ATT_HEAD_DIM = 64
N_Q_HEADS = 16
N_KV_HEADS = 4
Q_PER_KV = N_Q_HEADS // N_KV_HEADS
WINDOW = 128
ROPE_THETA = 10000.0
D_FF = 2816
N_CHIPS = 4
FF_SHARD = D_FF // N_CHIPS
IN_PROJ_DIM = D_INNER + CONV_DIM + SSM_HEADS
IN_SHARD = IN_PROJ_DIM // N_CHIPS
EPS = 1e-5
NEG = -1e30
LANES = 128
VMEM_LIMIT = 56 * 1024 * 1024

ADAM_LR = 0.001
ADAM_B1 = 0.9
ADAM_B2 = 0.999
ADAM_EPS = 1e-08
ADAM_WD = 0.01
ADAM_STEP = 10

NN = ((1,), (0,))
NT = ((1,), (1,))
TN = ((0,), (0,))


def _dot(a, b, dims=NN, precision=None):
    return lax.dot_general(a, b, (dims, ((), ())), preferred_element_type=F32, precision=precision)


def _cp(n_grid, **kw):
    return pltpu.CompilerParams(dimension_semantics=("arbitrary",) * n_grid,
                                vmem_limit_bytes=VMEM_LIMIT, **kw)


def _sigmoid(x):
    return 1.0 / (1.0 + jnp.exp(-x))


def _rms_fwd(xf, w):
    r = lax.rsqrt(jnp.mean(xf * xf, axis=-1, keepdims=True) + EPS)
    return xf * r * w


def _rms_bwd(dh, xf, w):
    r = lax.rsqrt(jnp.mean(xf * xf, axis=-1, keepdims=True) + EPS)
    xhat = xf * r
    dxhat = dh * w
    dx = r * (dxhat - xhat * jnp.mean(dxhat * xhat, axis=-1, keepdims=True))
    return dx, dh * xhat


def _row_tile(s, pref):
    return pref if s % pref == 0 else s


def rmsnorm_fwd(x, w, name):
    s, d = x.shape
    tm = _row_tile(s, 512)

    def body(x_ref, w_ref, o_ref):
        o_ref[...] = _rms_fwd(x_ref[...], w_ref[...]).astype(BF16)

    return pl.pallas_call(
        body, out_shape=jax.ShapeDtypeStruct((s, d), BF16), grid=(s // tm,),
        in_specs=[pl.BlockSpec((tm, d), lambda i: (i, 0)), pl.BlockSpec((1, d), lambda i: (0, 0))],
        out_specs=pl.BlockSpec((tm, d), lambda i: (i, 0)),
        name=name, compiler_params=_cp(1))(x, w.reshape(1, d))


def ffn_up(h, wg, wu, name):
    s, d = h.shape
    ng, _, f = wg.shape
    tm = _row_tile(s, 512)

    def body(h_ref, wg_ref, wu_ref, g_ref, u_ref, a_ref):
        hb = h_ref[...]
        g = _dot(hb, wg_ref[0])
        u = _dot(hb, wu_ref[0])
        g_ref[0] = g.astype(BF16)
        u_ref[0] = u.astype(BF16)
        a_ref[0] = (g * _sigmoid(g) * u).astype(BF16)

    out = jax.ShapeDtypeStruct((ng, s, f), BF16)
    w_spec = pl.BlockSpec((1, d, f), lambda i, k: (k, 0, 0))
    o_spec = pl.BlockSpec((1, tm, f), lambda i, k: (k, i, 0))
    return pl.pallas_call(
        body, out_shape=(out, out, out), grid=(s // tm, ng),
        in_specs=[pl.BlockSpec((tm, d), lambda i, k: (i, 0)), w_spec, w_spec],
        out_specs=(o_spec, o_spec, o_spec),
        name=name, compiler_params=_cp(2))(h, wg, wu)


def mm_res(a, w, x, scale, name, bias=None, norm_ws=()):
    ng, s, k = a.shape
    n = w.shape[2]
    tm = _row_tile(s, 256)
    has_bias = bias is not None
    n_norm = len(norm_ws)

    def body(*refs):
        a_ref, w_ref, x_ref = refs[:3]
        pos = 3
        b_ref = None
        if has_bias:
            b_ref = refs[pos]
            pos += 1
        nw_refs = refs[pos:pos + n_norm]
        pos += n_norm
        o_ref = refs[pos]
        h_refs = refs[pos + 1:pos + 1 + n_norm]
        acc = refs[-1]
        g = pl.program_id(1)
        part = _dot(a_ref[0], w_ref[0])

        @pl.when(g == 0)
        def _():
            acc[...] = part

        @pl.when(g > 0)
        def _():
            acc[...] += part

        @pl.when(g == ng - 1)
        def _():
            t = acc[...]
            if has_bias:
                t = t + b_ref[...]
            xn = x_ref[...] + scale * t
            o_ref[...] = xn
            for nw_ref, h_ref in zip(nw_refs, h_refs):
                h_ref[...] = _rms_fwd(xn, nw_ref[...]).astype(BF16)

    row = pl.BlockSpec((tm, n), lambda i, g: (i, 0))
    vec = pl.BlockSpec((1, n), lambda i, g: (0, 0))
    in_specs = [pl.BlockSpec((1, tm, k), lambda i, g: (g, i, 0)),
                pl.BlockSpec((1, k, n), lambda i, g: (g, 0, 0)), row]
    args = [a, w, x]
    if has_bias:
        in_specs.append(vec)
        args.append(bias.reshape(1, n))
    for nw in norm_ws:
        in_specs.append(vec)
        args.append(nw.reshape(1, n))
    out_shape = [jax.ShapeDtypeStruct((s, n), F32)] + [jax.ShapeDtypeStruct((s, n), BF16)] * n_norm
    res = pl.pallas_call(
        body, out_shape=tuple(out_shape), grid=(s // tm, ng),
        in_specs=in_specs, out_specs=tuple([row] * (1 + n_norm)),
        scratch_shapes=[pltpu.VMEM((tm, n), F32)],
        name=name, compiler_params=_cp(2))(*args)
    return res


def _col_tile(n):
    for t in (1024, 768, 512, 256, 128):
        if n % t == 0:
            return t
    return n


def mm_nn(a, w, name, bias=None, out_dtype=F32):
    s, k = a.shape
    n = w.shape[1]
    tm = _row_tile(s, 512)
    tn = _col_tile(n)
    has_bias = bias is not None

    def body(*refs):
        a_ref, w_ref = refs[:2]
        o_ref = refs[-1]
        t = _dot(a_ref[...], w_ref[...])
        if has_bias:
            t = t + refs[2][...]
        o_ref[...] = t.astype(out_dtype)

    in_specs = [pl.BlockSpec((tm, k), lambda j, i: (i, 0)), pl.BlockSpec((k, tn), lambda j, i: (0, j))]
    args = [a, w]
    if has_bias:
        in_specs.append(pl.BlockSpec((1, tn), lambda j, i: (0, j)))
        args.append(bias.reshape(1, n))
    return pl.pallas_call(
        body, out_shape=jax.ShapeDtypeStruct((s, n), out_dtype), grid=(n // tn, s // tm),
        in_specs=in_specs, out_specs=pl.BlockSpec((tm, tn), lambda j, i: (i, j)),
        name=name, compiler_params=_cp(2))(*args)


def mm_nt(a, w, name, scale=1.0, out_dtype=F32):
    s, k = a.shape
    n = w.shape[0]
    tm = _row_tile(s, 512)
    tn = _col_tile(n)

    def body(a_ref, w_ref, o_ref):
        t = _dot(a_ref[...].astype(BF16), w_ref[...], NT)
        o_ref[...] = (scale * t).astype(out_dtype)

    return pl.pallas_call(
        body, out_shape=jax.ShapeDtypeStruct((s, n), out_dtype), grid=(n // tn, s // tm),
        in_specs=[pl.BlockSpec((tm, k), lambda j, i: (i, 0)), pl.BlockSpec((tn, k), lambda j, i: (j, 0))],
        out_specs=pl.BlockSpec((tm, tn), lambda j, i: (i, j)),
        name=name, compiler_params=_cp(2))(a, w)


def mm_tn(a, b, name, scale=1.0):
    ga, s, m = a.shape
    gb, _, n = b.shape
    ng = max(ga, gb)
    tm = _col_tile(m)
    tn = _col_tile(n) if n > 1024 else n

    def body(a_ref, b_ref, o_ref):
        t = _dot(a_ref[0].astype(BF16), b_ref[0].astype(BF16), TN)
        o_ref[0] = (scale * t).astype(BF16)

    a_map = (lambda g, i, j: (g, 0, i)) if ga > 1 else (lambda g, i, j: (0, 0, i))
    b_map = (lambda g, i, j: (g, 0, j)) if gb > 1 else (lambda g, i, j: (0, 0, j))
    return pl.pallas_call(
        body, out_shape=jax.ShapeDtypeStruct((ng, m, n), BF16), grid=(ng, m // tm, n // tn),
        in_specs=[pl.BlockSpec((1, s, tm), a_map), pl.BlockSpec((1, s, tn), b_map)],
        out_specs=pl.BlockSpec((1, tm, tn), lambda g, i, j: (g, i, j)),
        name=name, compiler_params=_cp(3))(a, b)


def mm_nt_rms(pairs, dxn, x, nw, name):
    s, n = x.shape
    ng = pairs[0][0].shape[0]
    np_ = len(pairs)
    tm = _row_tile(s, 256)
    nrow = s // tm

    def body(*refs):
        a_refs = refs[0:2 * np_:2]
        w_refs = refs[1:2 * np_:2]
        dxn_ref, x_ref, nw_ref, dx_ref, dnw_ref, acc = refs[2 * np_:]
        i = pl.program_id(0)
        g = pl.program_id(1)
        part = _dot(a_refs[0][0].astype(BF16), w_refs[0][0], NT)
        for a_ref, w_ref in zip(a_refs[1:], w_refs[1:]):
            part += _dot(a_ref[0].astype(BF16), w_ref[0], NT)

        @pl.when(g == 0)
        def _():
            acc[...] = part

        @pl.when(g > 0)
        def _():
            acc[...] += part

        @pl.when(g == ng - 1)
        def _():
            dx, dnw = _rms_bwd(acc[...], x_ref[...], nw_ref[...])
            dx_ref[...] = dxn_ref[...] + dx
            col = jnp.sum(dnw, axis=0, keepdims=True)

            @pl.when(i == 0)
            def _():
                dnw_ref[...] = col

            @pl.when(i > 0)
            def _():
                dnw_ref[...] += col

    in_specs, args = [], []
    for a, w in pairs:
        k = a.shape[2]
        in_specs.append(pl.BlockSpec((1, tm, k), lambda i, g: (g, i, 0)))
        in_specs.append(pl.BlockSpec((1, n, k), lambda i, g: (g, 0, 0)))
        args += [a, w]
    row = pl.BlockSpec((tm, n), lambda i, g: (i, 0))
    vec = pl.BlockSpec((1, n), lambda i, g: (0, 0))
    in_specs += [row, row, vec]
    args += [dxn, x, nw.reshape(1, n)]
    dx, dnw = pl.pallas_call(
        body, out_shape=(jax.ShapeDtypeStruct((s, n), F32), jax.ShapeDtypeStruct((1, n), F32)),
        grid=(nrow, ng), in_specs=in_specs, out_specs=(row, vec),
        scratch_shapes=[pltpu.VMEM((tm, n), F32)],
        name=name, compiler_params=_cp(2))(*args)
    return dx, dnw[0]


def ffn_bwd_act(dxn, wd, g, u, name):
    s, d = dxn.shape
    ng, f, _ = wd.shape
    tm = _row_tile(s, 512)

    def body(dx_ref, wd_ref, g_ref, u_ref, dg_ref, du_ref):
        da = 0.5 * _dot(dx_ref[...].astype(BF16), wd_ref[0], NT)
        gf = g_ref[0].astype(F32)
        uf = u_ref[0].astype(F32)
        sg = _sigmoid(gf)
        dg_ref[0] = (da * uf * (sg * (1.0 + gf * (1.0 - sg)))).astype(BF16)
        du_ref[0] = (da * gf * sg).astype(BF16)

    blk = pl.BlockSpec((1, tm, f), lambda i, k: (k, i, 0))
    out = jax.ShapeDtypeStruct((ng, s, f), BF16)
    return pl.pallas_call(
        body, out_shape=(out, out), grid=(s // tm, ng),
        in_specs=[pl.BlockSpec((tm, d), lambda i, k: (i, 0)),
                  pl.BlockSpec((1, f, d), lambda i, k: (k, 0, 0)), blk, blk],
        out_specs=(blk, blk), name=name, compiler_params=_cp(2))(dxn, wd, g, u)


def colsum(a, name):
    s, n = a.shape
    tm = _row_tile(s, 512)

    def body(a_ref, o_ref):
        col = jnp.sum(a_ref[...].astype(F32), axis=0, keepdims=True)

        @pl.when(pl.program_id(0) == 0)
        def _():
            o_ref[...] = col

        @pl.when(pl.program_id(0) > 0)
        def _():
            o_ref[...] += col

    return pl.pallas_call(
        body, out_shape=jax.ShapeDtypeStruct((1, n), F32), grid=(s // tm,),
        in_specs=[pl.BlockSpec((tm, n), lambda i: (i, 0))],
        out_specs=pl.BlockSpec((1, n), lambda i: (0, 0)),
        name=name, compiler_params=_cp(1))(a)[0]


def rope_tables(s):
    pos = jnp.arange(s, dtype=F32)
    inv = 1.0 / (ROPE_THETA ** (jnp.arange(0, ATT_HEAD_DIM, 2, dtype=F32) / ATT_HEAD_DIM))
    ang = pos[:, None] * inv[None, :]
    cos = jnp.tile(jnp.cos(ang), (1, 2 * LANES // ATT_HEAD_DIM))
    sin = jnp.tile(jnp.sin(ang), (1, 2 * LANES // ATT_HEAD_DIM))
    return cos, sin


def rope_apply(t, cos, sin, name, inverse=False, scale=1.0, out_dtype=BF16):
    s, n = t.shape
    tm = _row_tile(s, 512)
    half = ATT_HEAD_DIM // 2
    reps = n // LANES

    def body(t_ref, c_ref, s_ref, o_ref):
        tf = t_ref[...].astype(F32)
        c = jnp.tile(c_ref[...], (1, reps))
        sn = jnp.tile(s_ref[...], (1, reps))
        lane = lax.broadcasted_iota(jnp.int32, tf.shape, 1)
        first = (lane & (ATT_HEAD_DIM - 1)) < half
        rot = jnp.where(first, -pltpu.roll(tf, n - half, 1), pltpu.roll(tf, half, 1))
        sign = -1.0 if inverse else 1.0
        o_ref[...] = (scale * (tf * c + sign * rot * sn)).astype(out_dtype)

    tab = pl.BlockSpec((tm, LANES), lambda i: (i, 0))
    return pl.pallas_call(
        body, out_shape=jax.ShapeDtypeStruct((s, n), out_dtype), grid=(s // tm,),
        in_specs=[pl.BlockSpec((tm, n), lambda i: (i, 0)), tab, tab],
        out_specs=pl.BlockSpec((tm, n), lambda i: (i, 0)),
        name=name, compiler_params=_cp(1))(t, cos, sin)


def _shift_down(u, k):
    if k == 0:
        return u
    row = lax.broadcasted_iota(jnp.int32, u.shape, 0)
    return jnp.where(row >= k, pltpu.roll(u, k, 0), 0.0)


def _shift_up(u, k):
    if k == 0:
        return u
    s = u.shape[0]
    row = lax.broadcasted_iota(jnp.int32, u.shape, 0)
    return jnp.where(row < s - k, pltpu.roll(u, s - k, 0), 0.0)


def _conv_pre(u, w_ref, b_ref):
    pre = b_ref[...] + w_ref[CONV_WIDTH - 1:CONV_WIDTH, :] * u
    for k in range(CONV_WIDTH - 1):
        pre += w_ref[k:k + 1, :] * _shift_down(u, CONV_WIDTH - 1 - k)
    return pre


def conv_fwd(u, w, b, name):
    s, c = u.shape
    tc = 256

    def body(u_ref, w_ref, b_ref, o_ref):
        pre = _conv_pre(u_ref[...], w_ref, b_ref)
        o_ref[...] = pre * _sigmoid(pre)

    col = pl.BlockSpec((s, tc), lambda j: (0, j))
    return pl.pallas_call(
        body, out_shape=jax.ShapeDtypeStruct((s, c), F32), grid=(c // tc,),
        in_specs=[col, pl.BlockSpec((CONV_WIDTH, tc), lambda j: (0, j)), pl.BlockSpec((1, tc), lambda j: (0, j))],
        out_specs=col, name=name, compiler_params=_cp(1))(u, w, b.reshape(1, c))


def conv_bwd(dact, u, w, b, name):
    s, c = u.shape
    tc = 256

    def body(da_ref, u_ref, w_ref, b_ref, du_ref, dw_ref, db_ref):
        uf = u_ref[...]
        pre = _conv_pre(uf, w_ref, b_ref)
        sg = _sigmoid(pre)
        dpre = da_ref[...] * (sg * (1.0 + pre * (1.0 - sg)))
        du = w_ref[CONV_WIDTH - 1:CONV_WIDTH, :] * dpre
        for k in range(CONV_WIDTH - 1):
            du += w_ref[k:k + 1, :] * _shift_up(dpre, CONV_WIDTH - 1 - k)
        du_ref[...] = du
        db_ref[...] = jnp.sum(dpre, axis=0, keepdims=True)
        for k in range(CONV_WIDTH):
            dw_ref[k:k + 1, :] = jnp.sum(dpre * _shift_down(uf, CONV_WIDTH - 1 - k), axis=0, keepdims=True)

    col = pl.BlockSpec((s, tc), lambda j: (0, j))
    wsp = pl.BlockSpec((CONV_WIDTH, tc), lambda j: (0, j))
    bsp = pl.BlockSpec((1, tc), lambda j: (0, j))
    du, dw, db = pl.pallas_call(
        body, out_shape=(jax.ShapeDtypeStruct((s, c), F32), jax.ShapeDtypeStruct((CONV_WIDTH, c), F32),
                         jax.ShapeDtypeStruct((1, c), F32)),
        grid=(c // tc,), in_specs=[col, col, wsp, bsp], out_specs=(col, wsp, bsp),
        name=name, compiler_params=_cp(1))(dact, u, w, b.reshape(1, c))
    return du, dw, db[0]


def _lane_pick(mat, idx):
    lane = lax.broadcasted_iota(jnp.int32, mat.shape, 1)
    return jnp.sum(jnp.where(lane == idx, mat, 0.0), axis=1, keepdims=True)


def _sub_pick(mat, idx):
    sub = lax.broadcasted_iota(jnp.int32, mat.shape, 0)
    return jnp.sum(jnp.where(sub == idx, mat, 0.0), axis=0, keepdims=True)


def _expand_heads(cols):
    rows = cols[0].shape[0]
    left = lax.broadcasted_iota(jnp.int32, (rows, LANES), 1) < SSM_HEAD_DIM
    return jnp.concatenate(
        [jnp.where(left, cols[2 * p], cols[2 * p + 1]) for p in range(HEADS_PER_GROUP // 2)], axis=1)


def _heads_to_lanes(mat, g):
    jj = lax.broadcasted_iota(jnp.int32, (GROUP_DIM, LANES), 0)
    ll = lax.broadcasted_iota(jnp.int32, (GROUP_DIM, LANES), 1)
    sel = (ll == HEADS_PER_GROUP * g + (jj >> 6)).astype(F32)
    return _dot(mat, sel, NN, HI)


def _softplus(x):
    return jnp.maximum(x, 0.0) + jnp.log1p(jnp.exp(-jnp.abs(x)))


def _ssd_scalars(dt_ref, bias_ref, a_ref, dtall, csall, cst):
    dta = _softplus(dt_ref[...] + bias_ref[...])
    row = lax.broadcasted_iota(jnp.int32, (CHUNK, CHUNK), 0)
    col = lax.broadcasted_iota(jnp.int32, (CHUNK, CHUNK), 1)
    tri = (row >= col).astype(F32)
    cs = _dot(tri, dta * a_ref[...], NN, HI)
    dtall[...] = dta
    csall[...] = cs
    cst[...] = cs.T


def _decay_mat(cs_col, cs_row):
    row = lax.broadcasted_iota(jnp.int32, (CHUNK, CHUNK), 0)
    col = lax.broadcasted_iota(jnp.int32, (CHUNK, CHUNK), 1)
    return jnp.exp(jnp.where(row >= col, cs_col - cs_row, NEG))


def _head_mask(xpair, right):
    lane = lax.broadcasted_iota(jnp.int32, xpair.shape, 1)
    keep = (lane >= SSM_HEAD_DIM) if right else (lane < SSM_HEAD_DIM)
    return jnp.where(keep, xpair, 0.0)


def _chunk_cols(x_all, g):
    return [_lane_pick(x_all, HEADS_PER_GROUP * g + r) for r in range(HEADS_PER_GROUP)]


def _decay_col(cs_cols):
    return jnp.concatenate(
        [jnp.broadcast_to(jnp.exp(cc[CHUNK - 1:CHUNK, :]), (SSM_HEAD_DIM, 1)) for cc in cs_cols], axis=0)


def ssd_fwd(act, z, dtp, bias_p, a_p, d_p, normw, name):
    s = act.shape[0]
    nc = s // CHUNK
    xs_blocks = D_INNER // GROUP_DIM
    b_off = D_INNER // SSM_STATE
    c_off = b_off + SSM_GROUPS

    def body(xs_ref, b_ref, c_ref, z_ref, dt_ref, bias_ref, a_ref, d_ref, nw_ref,
             yn_ref, y_ref, st_ref, state, dtall, csall, cst):
        c = pl.program_id(0)
        g = pl.program_id(1)

        @pl.when(g == 0)
        def _():
            _ssd_scalars(dt_ref, bias_ref, a_ref, dtall, csall, cst)

        @pl.when(c == 0)
        def _():
            state[g] = jnp.zeros((GROUP_DIM, SSM_STATE), F32)

        cs_cols = _chunk_cols(csall[...], g)
        dt_cols = _chunk_cols(dtall[...], g)
        cs_rows = [_sub_pick(cst[...], HEADS_PER_GROUP * g + r) for r in range(HEADS_PER_GROUP)]
        d_cols = _chunk_cols(d_ref[...], g)
        cs_exp = _expand_heads(cs_cols)
        dt_exp = _expand_heads(dt_cols)
        d_exp = _expand_heads(d_cols)
        xs = xs_ref[...]
        bb = b_ref[...].astype(BF16)
        cb16 = c_ref[...].astype(BF16)
        xdt = xs * dt_exp
        s_prev = state[g]
        st_ref[0, 0] = s_prev
        y_off = _dot(cb16, s_prev.astype(BF16), NT) * jnp.exp(cs_exp)
        decay_st = jnp.exp(cs_exp[CHUNK - 1:CHUNK, :] - cs_exp)
        contrib = _dot((xdt * decay_st).astype(BF16), bb, TN)
        state[g] = _decay_col(cs_cols) * s_prev + contrib
        cbm = _dot(cb16, bb, NT)
        pairs = []
        for p in range(HEADS_PER_GROUP // 2):
            xpair = xdt[:, LANES * p:LANES * (p + 1)]
            m0 = (cbm * _decay_mat(cs_cols[2 * p], cs_rows[2 * p])).astype(BF16)
            m1 = (cbm * _decay_mat(cs_cols[2 * p + 1], cs_rows[2 * p + 1])).astype(BF16)
            pairs.append(_dot(m0, _head_mask(xpair, False).astype(BF16))
                         + _dot(m1, _head_mask(xpair, True).astype(BF16)))
        y = jnp.concatenate(pairs, axis=1) + y_off + xs * d_exp
        y_ref[...] = y
        zf = z_ref[...]
        yg = y * (zf * _sigmoid(zf))
        yn_ref[...] = _rms_fwd(yg, nw_ref[...]).astype(BF16)

    grp = pl.BlockSpec((CHUNK, GROUP_DIM), lambda c, g: (c, g))
    par = pl.BlockSpec((1, LANES), lambda c, g: (0, 0))
    return pl.pallas_call(
        body,
        out_shape=(jax.ShapeDtypeStruct((s, D_INNER), BF16), jax.ShapeDtypeStruct((s, D_INNER), F32),
                   jax.ShapeDtypeStruct((nc, SSM_GROUPS, GROUP_DIM, SSM_STATE), F32)),
        grid=(nc, SSM_GROUPS),
        in_specs=[grp,
                  pl.BlockSpec((CHUNK, SSM_STATE), lambda c, g: (c, b_off + g)),
                  pl.BlockSpec((CHUNK, SSM_STATE), lambda c, g: (c, c_off + g)),
                  grp,
                  pl.BlockSpec((CHUNK, LANES), lambda c, g: (c, 0)),
                  par, par, par,
                  pl.BlockSpec((1, GROUP_DIM), lambda c, g: (0, g))],
        out_specs=(grp, grp, pl.BlockSpec((1, 1, GROUP_DIM, SSM_STATE), lambda c, g: (c, g, 0, 0))),
        scratch_shapes=[pltpu.VMEM((SSM_GROUPS, GROUP_DIM, SSM_STATE), F32),
                        pltpu.VMEM((CHUNK, LANES), F32), pltpu.VMEM((CHUNK, LANES), F32),
                        pltpu.VMEM((LANES, CHUNK), F32)],
        name=name, compiler_params=_cp(2))(act, act, act, z, dtp, bias_p, a_p, d_p, normw)


def ssd_bwd(dyn, act, z, y_pre, states, dtp, bias_p, a_p, d_p, normw, name):
    s = act.shape[0]
    nc = s // CHUNK
    b_off = D_INNER // SSM_STATE
    c_off = b_off + SSM_GROUPS

    def body(dyn_ref, xs_ref, b_ref, c_ref, z_ref, y_ref, st_ref, dt_ref, bias_ref, a_ref, d_ref, nw_ref,
             dxs_ref, db_ref, dc_ref, dz_ref, ddt_ref, dnw_ref, dbias_ref, da_ref, dd_ref,
             dstate, dtall, csall, cst):
        c = pl.program_id(0)
        g = pl.program_id(1)

        @pl.when(g == 0)
        def _():
            _ssd_scalars(dt_ref, bias_ref, a_ref, dtall, csall, cst)
            ddt_ref[...] = jnp.zeros((CHUNK, LANES), F32)

        @pl.when(c == 0)
        def _():
            dstate[g] = jnp.zeros((GROUP_DIM, SSM_STATE), F32)

        @pl.when(jnp.logical_and(c == 0, g == 0))
        def _():
            dnw_ref[...] = jnp.zeros(dnw_ref.shape, F32)
            dbias_ref[...] = jnp.zeros((1, LANES), F32)
            da_ref[...] = jnp.zeros((1, LANES), F32)
            dd_ref[...] = jnp.zeros((1, LANES), F32)

        cs_cols = _chunk_cols(csall[...], g)
        dt_cols = _chunk_cols(dtall[...], g)
        cs_rows = [_sub_pick(cst[...], HEADS_PER_GROUP * g + r) for r in range(HEADS_PER_GROUP)]
        d_cols = _chunk_cols(d_ref[...], g)
        cs_exp = _expand_heads(cs_cols)
        dt_exp = _expand_heads(dt_cols)
        d_exp = _expand_heads(d_cols)
        xs = xs_ref[...]
        bb = b_ref[...].astype(BF16)
        cb16 = c_ref[...].astype(BF16)
        xdt = xs * dt_exp
        s_prev = st_ref[0, 0]
        s_prev16 = s_prev.astype(BF16)
        ds_next = dstate[g]
        ds16 = ds_next.astype(BF16)

        zf = z_ref[...]
        sz = _sigmoid(zf)
        silu_z = zf * sz
        y = y_ref[...]
        yg = y * silu_z
        dout = dyn_ref[...]
        dyg, dnw = _rms_bwd(dout, yg, nw_ref[...])
        dnw_ref[pl.ds(g, 1), :] += jnp.sum(dnw, axis=0, keepdims=True)
        dy = dyg * silu_z
        dz_ref[...] = dyg * y * (sz * (1.0 + zf * (1.0 - sz)))
        dd_ref[...] += jnp.sum(_heads_to_lanes(dy * xs, g), axis=0, keepdims=True)

        exp_cs = jnp.exp(cs_exp)
        decay_st = jnp.exp(cs_exp[CHUNK - 1:CHUNK, :] - cs_exp)
        cs_t = _dot(cb16, s_prev16, NT)
        dyo = dy * exp_cs
        dc_acc = _dot(dyo.astype(BF16), s_prev16, NN)
        g1 = _dot(bb, ds16, NT)
        xds = xdt * decay_st
        db_acc = _dot(xds.astype(BF16), ds16, NN)
        dxdt_off = g1 * decay_st
        t_exp = g1 * xds
        dcs_exp = dy * cs_t * exp_cs - t_exp
        decay_c = _decay_col(cs_cols)
        dstate[g] = decay_c * ds_next + _dot(dyo.astype(BF16), cb16, TN)
        dlast_col = jnp.sum(ds_next * s_prev, axis=1, keepdims=True) * decay_c
        jj = lax.broadcasted_iota(jnp.int32, (GROUP_DIM, LANES), 0)
        ll = lax.broadcasted_iota(jnp.int32, (GROUP_DIM, LANES), 1)
        sel = ll == HEADS_PER_GROUP * g + (jj >> 6)
        dlast = jnp.sum(jnp.where(sel, dlast_col, 0.0), axis=0, keepdims=True)
        t_all = _heads_to_lanes(t_exp, g)
        dlast += jnp.sum(t_all, axis=0, keepdims=True)
        dcs_all = _heads_to_lanes(dcs_exp, g)

        cbm = _dot(cb16, bb, NT)
        dcb = jnp.zeros((CHUNK, CHUNK), F32)
        dcs_rows = jnp.zeros((LANES, CHUNK), F32)
        lane_l = lax.broadcasted_iota(jnp.int32, (CHUNK, LANES), 1)
        sub_l = lax.broadcasted_iota(jnp.int32, (LANES, CHUNK), 0)
        dxdt_pairs = []
        for p in range(HEADS_PER_GROUP // 2):
            xpair16 = xdt[:, LANES * p:LANES * (p + 1)].astype(BF16)
            dypair = dy[:, LANES * p:LANES * (p + 1)]
            acc = None
            for r in (2 * p, 2 * p + 1):
                lm = _decay_mat(cs_cols[r], cs_rows[r])
                m = cbm * lm
                dyh = _head_mask(dypair, r % 2 == 1).astype(BF16)
                dm = _dot(dyh, xpair16, NT)
                dcb += dm * lm
                q = dm * m
                idx = HEADS_PER_GROUP * g + r
                dcs_all += jnp.where(lane_l == idx, jnp.sum(q, axis=1, keepdims=True), 0.0)
                dcs_rows -= jnp.where(sub_l == idx, jnp.sum(q, axis=0, keepdims=True), 0.0)
                part = _dot(m.astype(BF16), dyh, TN)
                acc = part if acc is None else acc + part
            dxdt_pairs.append(acc)
        dxdt = jnp.concatenate(dxdt_pairs, axis=1) + dxdt_off
        dcb16 = dcb.astype(BF16)
        dc_ref[...] = dc_acc + _dot(dcb16, bb, NN)
        db_ref[...] = db_acc + _dot(dcb16, cb16, TN)
        dxs_ref[...] = dxdt * dt_exp + dy * d_exp

        dcs_all += dcs_rows.T
        row = lax.broadcasted_iota(jnp.int32, (CHUNK, CHUNK), 0)
        col = lax.broadcasted_iota(jnp.int32, (CHUNK, CHUNK), 1)
        last_row = lax.broadcasted_iota(jnp.int32, (CHUNK, LANES), 0) == CHUNK - 1
        dcs_all += jnp.where(last_row, dlast, 0.0)
        da_all = _dot((col >= row).astype(F32), dcs_all, NN, HI)
        dta = dtall[...]
        in_group = jnp.logical_and(lane_l >= HEADS_PER_GROUP * g, lane_l < HEADS_PER_GROUP * (g + 1))
        ddt = jnp.where(in_group, da_all * a_ref[...] + _heads_to_lanes(dxdt * xs, g), 0.0)
        da_ref[...] += jnp.sum(jnp.where(in_group, da_all * dta, 0.0), axis=0, keepdims=True)
        ddt_raw = ddt * _sigmoid(dt_ref[...] + bias_ref[...])
        ddt_ref[...] += ddt_raw
        dbias_ref[...] += jnp.sum(ddt_raw, axis=0, keepdims=True)

    rev = lambda c, g: (nc - 1 - c, g)
    grp = pl.BlockSpec((CHUNK, GROUP_DIM), rev)
    st = pl.BlockSpec((CHUNK, SSM_STATE), rev)
    par = pl.BlockSpec((1, LANES), lambda c, g: (0, 0))
    dtb = pl.BlockSpec((CHUNK, LANES), lambda c, g: (nc - 1 - c, 0))
    f = lambda shape: jax.ShapeDtypeStruct(shape, F32)
    return pl.pallas_call(
        body,
        out_shape=(f((s, D_INNER)), f((s, SSM_GROUPS * SSM_STATE)), f((s, SSM_GROUPS * SSM_STATE)),
                   f((s, D_INNER)), f((s, LANES)), f((8, GROUP_DIM)), f((1, LANES)), f((1, LANES)), f((1, LANES))),
        grid=(nc, SSM_GROUPS),
        in_specs=[grp, grp,
                  pl.BlockSpec((CHUNK, SSM_STATE), lambda c, g: (nc - 1 - c, b_off + g)),
                  pl.BlockSpec((CHUNK, SSM_STATE), lambda c, g: (nc - 1 - c, c_off + g)),
                  grp, grp,
                  pl.BlockSpec((1, 1, GROUP_DIM, SSM_STATE), lambda c, g: (nc - 1 - c, g, 0, 0)),
                  dtb, par, par, par,
                  pl.BlockSpec((1, GROUP_DIM), lambda c, g: (0, g))],
        out_specs=(grp, st, st, grp, dtb,
                   pl.BlockSpec((8, GROUP_DIM), lambda c, g: (0, 0)), par, par, par),
        scratch_shapes=[pltpu.VMEM((SSM_GROUPS, GROUP_DIM, SSM_STATE), F32),
                        pltpu.VMEM((CHUNK, LANES), F32), pltpu.VMEM((CHUNK, LANES), F32),
                        pltpu.VMEM((LANES, CHUNK), F32)],
        name=name, compiler_params=_cp(2))(dyn, act, act, act, z, y_pre, states, dtp, bias_p, a_p, d_p, normw)


def _attn_probs(q, kp, kc, sink, n):
    sp = _dot(q, kp, NT)
    sc = _dot(q, kc, NT)
    i = lax.broadcasted_iota(jnp.int32, sp.shape, 0) & (WINDOW - 1)
    j = lax.broadcasted_iota(jnp.int32, sp.shape, 1)
    sp = jnp.where(jnp.logical_and(j > i, n > 0), sp, NEG)
    sc = jnp.where(j <= i, sc, NEG)
    m = jnp.maximum(jnp.maximum(jnp.max(sp, axis=1, keepdims=True), jnp.max(sc, axis=1, keepdims=True)), sink)
    pp = jnp.exp(sp - m)
    pc = jnp.exp(sc - m)
    ps = jnp.exp(sink - m)
    inv = 1.0 / (jnp.sum(pp, axis=1, keepdims=True) + jnp.sum(pc, axis=1, keepdims=True) + ps)
    return pp * inv, pc * inv, ps * inv


def attn_fwd(qt, kt, vt, sink_rows, name):
    s = qt.shape[1]
    nb = s // WINDOW
    rows = Q_PER_KV * WINDOW

    def body(q_ref, kp_ref, kc_ref, vp_ref, vc_ref, sk_ref, o_ref):
        n = pl.program_id(1)
        q = q_ref[...].reshape(rows, ATT_HEAD_DIM)
        pp, pc, _ = _attn_probs(q, kp_ref[0], kc_ref[0], sk_ref[0], n)
        o = _dot(pp.astype(BF16), vp_ref[0]) + _dot(pc.astype(BF16), vc_ref[0])
        o_ref[...] = o.reshape(Q_PER_KV, WINDOW, ATT_HEAD_DIM).astype(BF16)

    qsp = pl.BlockSpec((Q_PER_KV, WINDOW, ATT_HEAD_DIM), lambda h, n: (h, n, 0))
    prev = pl.BlockSpec((1, WINDOW, ATT_HEAD_DIM), lambda h, n: (h, jnp.maximum(n - 1, 0), 0))
    cur = pl.BlockSpec((1, WINDOW, ATT_HEAD_DIM), lambda h, n: (h, n, 0))
    return pl.pallas_call(
        body, out_shape=jax.ShapeDtypeStruct(qt.shape, BF16), grid=(N_KV_HEADS, nb),
        in_specs=[qsp, prev, cur, prev, cur, pl.BlockSpec((1, rows, 1), lambda h, n: (h, 0, 0))],
        out_specs=qsp, name=name, compiler_params=_cp(2))(qt, kt, kt, vt, vt, sink_rows)


def attn_bwd(qt, kt, vt, sink_rows, dot_, name):
    s = qt.shape[1]
    nb = s // WINDOW
    rows = Q_PER_KV * WINDOW

    def body(q_ref, kp_ref, kc_ref, vp_ref, vc_ref, sk_ref, do_ref, dq_ref, dk_ref, dv_ref, ds_ref, kacc, vacc):
        n = pl.program_id(1)

        @pl.when(n < nb)
        def _():
            q = q_ref[...].reshape(rows, ATT_HEAD_DIM)
            do = do_ref[...].reshape(rows, ATT_HEAD_DIM)
            kp, kc, vp, vc = kp_ref[0], kc_ref[0], vp_ref[0], vc_ref[0]
            pp, pc, ps = _attn_probs(q, kp, kc, sk_ref[0], n)
            dpp = _dot(do, vp, NT)
            dpc = _dot(do, vc, NT)
            delta = jnp.sum(pp * dpp, axis=1, keepdims=True) + jnp.sum(pc * dpc, axis=1, keepdims=True)
            dsp = (pp * (dpp - delta)).astype(BF16)
            dsc = (pc * (dpc - delta)).astype(BF16)
            dq = _dot(dsp, kp) + _dot(dsc, kc)
            dq_ref[...] = dq.reshape(Q_PER_KV, WINDOW, ATT_HEAD_DIM)
            dk_prev = _dot(dsp, q, TN)
            dv_prev = _dot(pp.astype(BF16), do, TN)
            @pl.when(n == 0)
            def _():
                dk_ref[0] = dk_prev
                dv_ref[0] = dv_prev

            @pl.when(n > 0)
            def _():
                dk_ref[0] = kacc[...] + dk_prev
                dv_ref[0] = vacc[...] + dv_prev

            kacc[...] = _dot(dsc, q, TN)
            vacc[...] = _dot(pc.astype(BF16), do, TN)
            dsk = -ps * delta
            sub = lax.broadcasted_iota(jnp.int32, (8, LANES), 0)
            tile = jnp.zeros((8, LANES), F32)
            for h in range(Q_PER_KV):
                tile += jnp.where(sub == h, jnp.sum(dsk[h * WINDOW:(h + 1) * WINDOW, :], axis=0, keepdims=True), 0.0)
            ds_ref[0, 0] = tile

        @pl.when(n == nb)
        def _():
            dk_ref[0] = kacc[...]
            dv_ref[0] = vacc[...]
            ds_ref[0, 0] = jnp.zeros((8, LANES), F32)

    last = nb - 1
    qsp = pl.BlockSpec((Q_PER_KV, WINDOW, ATT_HEAD_DIM), lambda h, n: (h, jnp.minimum(n, last), 0))
    prev = pl.BlockSpec((1, WINDOW, ATT_HEAD_DIM), lambda h, n: (h, jnp.clip(n - 1, 0, last), 0))
    cur = pl.BlockSpec((1, WINDOW, ATT_HEAD_DIM), lambda h, n: (h, jnp.minimum(n, last), 0))
    dkv = pl.BlockSpec((1, WINDOW, ATT_HEAD_DIM), lambda h, n: (h, jnp.maximum(n - 1, 0), 0))
    f = lambda shape: jax.ShapeDtypeStruct(shape, F32)
    return pl.pallas_call(
        body, out_shape=(f(qt.shape), f(kt.shape), f(vt.shape), f((N_KV_HEADS, nb + 1, 8, LANES))),
        grid=(N_KV_HEADS, nb + 1),
        in_specs=[qsp, prev, cur, prev, cur, pl.BlockSpec((1, rows, 1), lambda h, n: (h, 0, 0)), qsp],
        out_specs=(qsp, dkv, dkv, pl.BlockSpec((1, 1, 8, LANES), lambda h, n: (h, n, 0, 0))),
        scratch_shapes=[pltpu.VMEM((WINDOW, ATT_HEAD_DIM), F32), pltpu.VMEM((WINDOW, ATT_HEAD_DIM), F32)],
        name=name, compiler_params=_cp(2))(qt, kt, kt, vt, vt, sink_rows, dot_)


def loss_head(x, w, tgt, name):
    s, d = x.shape
    tm = _row_tile(s, 256)

    def body(x_ref, w_ref, t_ref, loss_ref, dx_ref, dw_ref):
        i = pl.program_id(0)
        xf = x_ref[...]
        wv = w_ref[...]
        r = lax.rsqrt(jnp.mean(xf * xf, axis=-1, keepdims=True) + EPS)
        xhat = xf * r
        e = xhat * wv - t_ref[...]
        part = 0.5 * jnp.sum(jnp.mean(e * e, axis=-1, keepdims=True), axis=0, keepdims=True)
        dy = e * (1.0 / d)
        dxhat = dy * wv
        dx_ref[...] = r * (dxhat - xhat * jnp.mean(dxhat * xhat, axis=-1, keepdims=True))
        col = jnp.sum(dy * xhat, axis=0, keepdims=True)

        @pl.when(i == 0)
        def _():
            loss_ref[...] = jnp.broadcast_to(part, (1, LANES))
            dw_ref[...] = col

        @pl.when(i > 0)
        def _():
            loss_ref[...] += jnp.broadcast_to(part, (1, LANES))
            dw_ref[...] += col

    row = pl.BlockSpec((tm, d), lambda i: (i, 0))
    vec = pl.BlockSpec((1, d), lambda i: (0, 0))
    return pl.pallas_call(
        body, out_shape=(jax.ShapeDtypeStruct((1, LANES), F32), jax.ShapeDtypeStruct((s, d), F32),
                         jax.ShapeDtypeStruct((1, d), F32)),
        grid=(s // tm,), in_specs=[row, vec, row],
        out_specs=(pl.BlockSpec((1, LANES), lambda i: (0, 0)), row, vec),
        name=name, compiler_params=_cp(1))(x, w.reshape(1, d), tgt)


def _heads_major(t, n_heads):
    s = t.shape[0]
    return t.reshape(s, n_heads, ATT_HEAD_DIM).transpose(1, 0, 2)


def _tokens_major(t):
    h, s, dh = t.shape
    return t.transpose(1, 0, 2).reshape(s, h * dh)


def _pad_lanes(v):
    return jnp.pad(v.reshape(1, -1), ((0, 0), (0, LANES - v.size)))


def _ffn_fwd(h, wts, x, tag, norm_ws):
    wg, wu, wd = wts
    g, u, a = ffn_up(h, wg, wu, "ffn_up_" + tag)
    outs = mm_res(a, wd, x, 0.5, "ffn_down_" + tag, norm_ws=norm_ws)
    return outs, (g, u, a)


def _ffn_bwd(dxn, wts, saved, h, x_in, nw, tag):
    wg, wu, wd = wts
    g, u, a = saved
    dg, du = ffn_bwd_act(dxn, wd, g, u, "ffn_bwd_act_" + tag)
    d_wd = mm_tn(a, dxn[None], "ffn_dwd_" + tag, scale=0.5)
    dx, dnw = mm_nt_rms([(dg, wg), (du, wu)], dxn, x_in, nw, "ffn_bwd_dh_" + tag)
    d_wg = mm_tn(h[None], dg, "ffn_dwg_" + tag)
    d_wu = mm_tn(h[None], du, "ffn_dwu_" + tag)
    return dx, dnw, (d_wg, d_wu, d_wd)


def local_step(x, tgt, p):
    s = x.shape[0]
    nw = p["norm_w"]
    cos, sin = rope_tables(s)
    ffn = lambda l, i: (p["ffn_w_gate"][l][i], p["ffn_w_up"][l][i], p["ffn_w_down"][l][i])
    grads = {}

    h00 = rmsnorm_fwd(x, nw[0, 0], "norm_in")
    (x1, h01), sv00 = _ffn_fwd(h00, ffn(0, 0), x, "00", [nw[0, 1]])
    zz = mm_nn(h01, p["w_z"], "ssm_in_z")
    xbc = mm_nn(h01, p["w_xbc"], "ssm_in_xbc")
    dtp = mm_nn(h01, p["w_dt"], "ssm_in_dt")
    act = conv_fwd(xbc, p["ssm_conv_w"], p["ssm_conv_b"], "ssm_conv")
    bias_p = _pad_lanes(p["ssm_dt_bias"])
    a_p = _pad_lanes(-jnp.exp(p["ssm_a_log"]))
    d_p = _pad_lanes(p["ssm_d"])
    ssm_nw = p["ssm_norm_w"].reshape(1, D_INNER)
    yn, y_pre, states = ssd_fwd(act, zz, dtp, bias_p, a_p, d_p, ssm_nw, "ssd_fwd")
    x2, h02 = mm_res(yn[None], p["ssm_w_out"][None], x1, 1.0, "ssm_out", norm_ws=[nw[0, 2]])
    (x3, hkv, h10), sv01 = _ffn_fwd(h02, ffn(0, 1), x2, "01", [p["kv_norm_w"], nw[1, 0]])

    k_rot = rope_apply(mm_nn(hkv, p["w_k"], "kv_k", bias=p["b_k"]), cos, sin, "rope_k")
    v = mm_nn(hkv, p["w_v"], "kv_v", bias=p["b_v"], out_dtype=BF16)
    kt = _heads_major(k_rot, N_KV_HEADS)
    vt = _heads_major(v, N_KV_HEADS)

    (x4, h11), sv10 = _ffn_fwd(h10, ffn(1, 0), x3, "10", [nw[1, 1]])
    scale = 1.0 / math.sqrt(ATT_HEAD_DIM)
    q_rot = rope_apply(mm_nn(h11, p["attn_w_q"], "attn_q", bias=p["attn_b_q"]), cos, sin, "rope_q", scale=scale)
    qt = _heads_major(q_rot, N_Q_HEADS)
    sink_rows = jnp.repeat(p["attn_sinks"].reshape(N_KV_HEADS, Q_PER_KV), WINDOW, axis=1).reshape(
        N_KV_HEADS, Q_PER_KV * WINDOW, 1)
    o = _tokens_major(attn_fwd(qt, kt, vt, sink_rows, "attn_fwd"))
    x5, h12 = mm_res(o[None], p["attn_w_o"][None], x4, 1.0, "attn_out", bias=p["attn_b_o"], norm_ws=[nw[1, 2]])
    (x6,), sv11 = _ffn_fwd(h12, ffn(1, 1), x5, "11", [])

    loss_v, dx6, d_final = loss_head(x6, p["final_norm_w"], tgt, "loss_head")
    grads["final_norm_w"] = d_final[0]

    dnw = [[None] * 3 for _ in range(2)]
    dffn = {}
    dx5, dnw[1][2], dffn[(1, 1)] = _ffn_bwd(dx6, ffn(1, 1), sv11, h12, x5, nw[1, 2], "11")
    grads["attn_b_o"] = colsum(dx5, "attn_dbo")
    grads["attn_w_o"] = mm_tn(o[None], dx5[None], "attn_dwo")[0]
    do = mm_nt(dx5, p["attn_w_o"], "attn_do", out_dtype=BF16)
    dqt, dkt, dvt, dsink = attn_bwd(qt, kt, vt, sink_rows, _heads_major(do, N_Q_HEADS), "attn_bwd")
    grads["attn_sinks"] = jnp.sum(dsink[:, :, :Q_PER_KV, 0], axis=1).reshape(N_Q_HEADS)
    dq_pre = rope_apply(_tokens_major(dqt), cos, sin, "rope_dq", inverse=True, scale=scale, out_dtype=F32)
    grads["attn_b_q"] = colsum(dq_pre, "attn_dbq")
    grads["attn_w_q"] = mm_tn(h11[None], dq_pre[None], "attn_dwq")[0]
    dx4, dnw[1][1] = mm_nt_rms([(dq_pre[None], p["attn_w_q"][None])], dx5, x4, nw[1, 1], "attn_bwd_dh")
    dx3, dnw[1][0], dffn[(1, 0)] = _ffn_bwd(dx4, ffn(1, 0), sv10, h10, x3, nw[1, 0], "10")

    dk_pre = rope_apply(_tokens_major(dkt), cos, sin, "rope_dk", inverse=True, out_dtype=F32)
    dv = _tokens_major(dvt)
    grads["b_k"] = colsum(dk_pre, "kv_dbk")
    grads["b_v"] = colsum(dv, "kv_dbv")
    grads["w_k"] = mm_tn(hkv[None], dk_pre[None], "kv_dwk")[0]
    grads["w_v"] = mm_tn(hkv[None], dv[None], "kv_dwv")[0]
    dx3, grads["kv_norm_w"] = mm_nt_rms([(dk_pre[None], p["w_k"][None]), (dv[None], p["w_v"][None])],
                                        dx3, x3, p["kv_norm_w"], "kv_bwd_dh")

    dx2, dnw[0][2], dffn[(0, 1)] = _ffn_bwd(dx3, ffn(0, 1), sv01, h02, x2, nw[0, 2], "01")
    grads["ssm_w_out"] = mm_tn(yn[None], dx2[None], "ssm_dwout")[0]
    dyn = mm_nt(dx2, p["ssm_w_out"], "ssm_dyn")
    dxs, db_, dc_, dz, ddt, d_ssm_nw, d_bias, d_a, d_d = ssd_bwd(
        dyn, act, zz, y_pre, states, dtp, bias_p, a_p, d_p, ssm_nw, "ssd_bwd")
    grads["ssm_norm_w"] = d_ssm_nw[:SSM_GROUPS].reshape(D_INNER)
    grads["ssm_dt_bias"] = d_bias[0, :SSM_HEADS]
    grads["ssm_a_log"] = d_a[0, :SSM_HEADS] * a_p[0, :SSM_HEADS]
    grads["ssm_d"] = d_d[0, :SSM_HEADS]
    dact = jnp.concatenate([dxs, db_, dc_], axis=1)
    dxbc, grads["ssm_conv_w"], grads["ssm_conv_b"] = conv_bwd(
        dact, xbc, p["ssm_conv_w"], p["ssm_conv_b"], "ssm_conv_bwd")
    grads["w_z"] = mm_tn(h01[None], dz[None], "ssm_dwz")[0]
    grads["w_xbc"] = mm_tn(h01[None], dxbc[None], "ssm_dwxbc")[0]
    grads["w_dt"] = mm_tn(h01[None], ddt[None], "ssm_dwdt")[0]
    dx1, dnw[0][1] = mm_nt_rms(
        [(dz[None], p["w_z"][None]), (dxbc[None], p["w_xbc"][None]), (ddt[None], p["w_dt"][None])],
        dx2, x1, nw[0, 1], "ssm_bwd_dh")
    grad_x, dnw[0][0], dffn[(0, 0)] = _ffn_bwd(dx1, ffn(0, 0), sv00, h00, x, nw[0, 0], "00")

    grads["norm_w"] = jnp.stack([jnp.stack(r) for r in dnw])
    grads["ffn"] = dffn
    return loss_v[0, 0], grad_x, grads


MESH = pl.DeviceIdType.MESH
ANY = pl.BlockSpec(memory_space=pl.ANY)


def _place():
    x, y, c = lax.axis_index("x"), lax.axis_index("y"), lax.axis_index("c")
    others = [(1 - x, y), (x, 1 - y), (1 - x, 1 - y)]
    return x, y, c, 2 * x + y, others


def _hbm_like(shape, dtype):
    return jax.ShapeDtypeStruct(shape, dtype)


def all_gather_chips(arrs, name):
    n = len(arrs)

    def body(*refs):
        ins, outs = refs[:n], refs[n:2 * n]
        send_sems, recv_sems = refs[2 * n:]
        x, y, c, k, others = _place()
        sibling = (x, y, 1 - c)
        sends = []
        for a in range(n):
            for j, (px, py) in enumerate(others):
                cp = pltpu.make_async_remote_copy(
                    src_ref=ins[a].at[c], dst_ref=outs[a].at[k, c], send_sem=send_sems.at[a, j],
                    recv_sem=recv_sems.at[a, j], device_id=(px, py, c), device_id_type=MESH)
                cp.start()
                sends.append(cp)
        for a in range(n):
            for j, (px, py) in enumerate(others):
                blk = outs[a].at[2 * px + py, c]
                pltpu.make_async_remote_copy(
                    src_ref=blk, dst_ref=blk, send_sem=send_sems.at[a, j], recv_sem=recv_sems.at[a, j],
                    device_id=(px, py, c), device_id_type=MESH).wait_recv()
                fw = pltpu.make_async_remote_copy(
                    src_ref=blk, dst_ref=blk, send_sem=send_sems.at[a, 3 + j], recv_sem=recv_sems.at[a, 3 + j],
                    device_id=sibling, device_id_type=MESH)
                fw.start()
                sends.append(fw)
        for a in range(n):
            for j, (px, py) in enumerate(others):
                blk = outs[a].at[2 * px + py, 1 - c]
                pltpu.make_async_remote_copy(
                    src_ref=blk, dst_ref=blk, send_sem=send_sems.at[a, 3 + j], recv_sem=recv_sems.at[a, 3 + j],
                    device_id=sibling, device_id_type=MESH).wait_recv()
        for cp in sends:
            cp.wait_send()

    return pl.pallas_call(
        body, out_shape=tuple(_hbm_like((N_CHIPS,) + a.shape, a.dtype) for a in arrs),
        in_specs=[ANY] * n, out_specs=tuple([ANY] * n),
        scratch_shapes=[pltpu.SemaphoreType.DMA((n, 6)), pltpu.SemaphoreType.DMA((n, 6))],
        name=name)(*arrs)


def pair_swap(arrs, name):
    n = len(arrs)

    def body(*refs):
        ins, outs = refs[:n], refs[n:2 * n]
        send_sems, recv_sems = refs[2 * n:]
        x, y, c, _, _ = _place()
        cps = []
        for a in range(n):
            cp = pltpu.make_async_remote_copy(
                src_ref=ins[a].at[1 - c], dst_ref=outs[a], send_sem=send_sems.at[a], recv_sem=recv_sems.at[a],
                device_id=(x, y, 1 - c), device_id_type=MESH)
            cp.start()
            cps.append(cp)
        for cp in cps:
            cp.wait()

    return pl.pallas_call(
        body, out_shape=tuple(_hbm_like(a.shape[1:], a.dtype) for a in arrs),
        in_specs=[ANY] * n, out_specs=tuple([ANY] * n),
        scratch_shapes=[pltpu.SemaphoreType.DMA((n,)), pltpu.SemaphoreType.DMA((n,))],
        name=name)(*arrs)


def chip_scatter(arrs, name):
    n = len(arrs)

    def body(*refs):
        ins, outs = refs[:n], refs[n:2 * n]
        send_sems, recv_sems = refs[2 * n:]
        x, y, c, k, others = _place()
        sends = []
        for a in range(n):
            for j, (px, py) in enumerate(others):
                cp = pltpu.make_async_remote_copy(
                    src_ref=ins[a].at[2 * px + py], dst_ref=outs[a].at[k], send_sem=send_sems.at[a, j],
                    recv_sem=recv_sems.at[a, j], device_id=(px, py, c), device_id_type=MESH)
                cp.start()
                sends.append(cp)
        for a in range(n):
            for j, (px, py) in enumerate(others):
                blk = outs[a].at[2 * px + py]
                pltpu.make_async_remote_copy(
                    src_ref=blk, dst_ref=blk, send_sem=send_sems.at[a, j], recv_sem=recv_sems.at[a, j],
                    device_id=(px, py, c), device_id_type=MESH).wait_recv()
        for cp in sends:
            cp.wait_send()

    return pl.pallas_call(
        body, out_shape=tuple(_hbm_like(a.shape, a.dtype) for a in arrs),
        in_specs=[ANY] * n, out_specs=tuple([ANY] * n),
        scratch_shapes=[pltpu.SemaphoreType.DMA((n, 3)), pltpu.SemaphoreType.DMA((n, 3))],
        name=name)(*arrs)


def pair_send(arrs, name):
    n = len(arrs)

    def body(*refs):
        ins, outs = refs[:n], refs[n:2 * n]
        send_sems, recv_sems = refs[2 * n:]
        x, y, c, _, _ = _place()
        cps = []
        for a in range(n):
            cp = pltpu.make_async_remote_copy(
                src_ref=ins[a], dst_ref=outs[a], send_sem=send_sems.at[a], recv_sem=recv_sems.at[a],
                device_id=(x, y, 1 - c), device_id_type=MESH)
            cp.start()
            cps.append(cp)
        for cp in cps:
            cp.wait()

    return pl.pallas_call(
        body, out_shape=tuple(_hbm_like(a.shape, a.dtype) for a in arrs),
        in_specs=[ANY] * n, out_specs=tuple([ANY] * n),
        scratch_shapes=[pltpu.SemaphoreType.DMA((n,)), pltpu.SemaphoreType.DMA((n,))],
        name=name)(*arrs)


def all_reduce_small(buf, name):
    r = buf.shape[0]
    n_dev = 8

    def body(in_ref, o_ref, land, send_sems, recv_sems):
        x, y, c, _, _ = _place()
        me = 4 * x + 2 * y + c
        land[me] = in_ref[...]
        sends = []
        for d in range(1, n_dev):
            peer = (x ^ (d >> 2), y ^ ((d >> 1) & 1), c ^ (d & 1))
            cp = pltpu.make_async_remote_copy(
                src_ref=in_ref, dst_ref=land.at[me], send_sem=send_sems.at[d], recv_sem=recv_sems.at[d],
                device_id=peer, device_id_type=MESH)
            cp.start()
            sends.append(cp)
        for d in range(1, n_dev):
            blk = land.at[me ^ d]
            pltpu.make_async_remote_copy(
                src_ref=blk, dst_ref=blk, send_sem=send_sems.at[d], recv_sem=recv_sems.at[d],
                device_id=(x, y, c), device_id_type=MESH).wait_recv()
        for cp in sends:
            cp.wait_send()
        tot = land[0]
        for d in range(1, n_dev):
            tot = tot + land[d]
        o_ref[...] = tot

    vm = pl.BlockSpec(memory_space=pltpu.VMEM)
    return pl.pallas_call(
        body, out_shape=jax.ShapeDtypeStruct(buf.shape, F32), in_specs=[vm], out_specs=vm,
        scratch_shapes=[pltpu.VMEM((n_dev, r, LANES), F32), pltpu.SemaphoreType.DMA((n_dev,)),
                        pltpu.SemaphoreType.DMA((n_dev,))],
        name=name)(buf)


def _tile_rows(r, c, max_elems=262144, mult=16):
    best = None
    for t in range(mult, r + 1, mult):
        if r % t == 0 and t * c <= max_elems:
            best = t
    return best or r


def add_pair(xh, p, c_idx, name):
    _, r, c = xh.shape
    tr = _tile_rows(r, c)

    def body(c_ref, x_ref, p_ref, o_ref):
        o_ref[...] = (x_ref[0].astype(F32) + p_ref[...].astype(F32)).astype(BF16)

    blk = pl.BlockSpec((tr, c), lambda i, cr: (i, 0))
    return pl.pallas_call(
        body, out_shape=jax.ShapeDtypeStruct((r, c), BF16),
        grid_spec=pltpu.PrefetchScalarGridSpec(
            num_scalar_prefetch=1, grid=(r // tr,),
            in_specs=[pl.BlockSpec((1, tr, c), lambda i, cr: (cr[0], i, 0)), blk], out_specs=blk),
        name=name, compiler_params=_cp(1))(c_idx, xh, p)


def sum_chips(q, own, chip_idx, name):
    _, r, c = q.shape
    tr = _tile_rows(r, c)

    def body(k_ref, q_ref, own_ref, o_ref):
        k = k_ref[0]
        mine = own_ref[0].astype(F32)
        tot = None
        for j in range(N_CHIPS):
            term = jnp.where(k == j, mine, q_ref[j].astype(F32))
            tot = term if tot is None else tot + term
        o_ref[...] = tot

    return pl.pallas_call(
        body, out_shape=jax.ShapeDtypeStruct((r, c), F32),
        grid_spec=pltpu.PrefetchScalarGridSpec(
            num_scalar_prefetch=1, grid=(r // tr,),
            in_specs=[pl.BlockSpec((N_CHIPS, tr, c), lambda i, kr: (0, i, 0)),
                      pl.BlockSpec((1, tr, c), lambda i, kr: (kr[0], i, 0))],
            out_specs=pl.BlockSpec((tr, c), lambda i, kr: (i, 0))),
        name=name, compiler_params=_cp(1))(chip_idx, q, own)


def adamw(w, g, m, v, name):
    r, c = w.shape
    tr = _tile_rows(r, c, max_elems=131072, mult=8)
    c1 = 1.0 / (1.0 - ADAM_B1 ** ADAM_STEP)
    c2 = 1.0 / (1.0 - ADAM_B2 ** ADAM_STEP)

    def body(w_ref, g_ref, m_ref, v_ref, d_ref, mo_ref, vo_ref):
        gf = g_ref[...]
        mn = ADAM_B1 * m_ref[...] + (1.0 - ADAM_B1) * gf
        vn = ADAM_B2 * v_ref[...] + (1.0 - ADAM_B2) * (gf * gf)
        mo_ref[...] = mn
        vo_ref[...] = vn
        d_ref[...] = -ADAM_LR * ((mn * c1) / (jnp.sqrt(vn * c2) + ADAM_EPS) + ADAM_WD * w_ref[...])

    blk = pl.BlockSpec((tr, c), lambda i: (i, 0))
    out = jax.ShapeDtypeStruct((r, c), F32)
    return pl.pallas_call(
        body, out_shape=(out, out, out), grid=(r // tr,), in_specs=[blk] * 4, out_specs=(blk, blk, blk),
        name=name, compiler_params=_cp(1))(w, g, m, v)


WEIGHTS = ['norm_w', 'ffn_w_gate', 'ffn_w_up', 'ffn_w_down', 'ssm_w_in', 'ssm_conv_w', 'ssm_conv_b', 'ssm_dt_bias',
           'ssm_a_log', 'ssm_d', 'ssm_norm_w', 'ssm_w_out', 'kv_norm_w', 'w_k', 'b_k', 'w_v', 'b_v', 'attn_w_q',
           'attn_b_q', 'attn_sinks', 'attn_w_o', 'attn_b_o', 'final_norm_w']
BIG = ['ffn_w_gate', 'ffn_w_up', 'ffn_w_down', 'ssm_w_in', 'ssm_w_out', 'w_k', 'w_v', 'attn_w_q', 'attn_w_o']
SMALL = [n for n in WEIGHTS if n not in BIG]
SMALL_SHARDED = {'norm_w': 2, 'ssm_conv_w': 2, 'ssm_conv_b': 1, 'ssm_norm_w': 1}
ROW_ALIGN = 8 * LANES


def _pack_rows(parts):
    flat = jnp.concatenate([p.reshape(-1).astype(F32) for p in parts])
    pad = (-flat.size) % ROW_ALIGN
    return jnp.pad(flat, (0, pad)).reshape(-1, LANES)


def _unpack_rows(buf, shapes):
    flat = buf.reshape(-1)
    out, pos = [], 0
    for shp in shapes:
        size = math.prod(shp)
        out.append(flat[pos:pos + size].reshape(shp))
        pos += size
    return out


def _as2d(a):
    return a.reshape(-1, a.shape[-1])


def kernel(*args):
    names = (['x'] + WEIGHTS + ['loss_target'] + ['m_' + n for n in WEIGHTS] + ['v_' + n for n in WEIGHTS])
    a = dict(zip(names, args))
    xi, yi, ci = lax.axis_index("x"), lax.axis_index("y"), lax.axis_index("c")
    chip = 2 * xi + yi
    c_idx = jnp.reshape(ci, (1,)).astype(jnp.int32)
    chip_idx = jnp.reshape(chip, (1,)).astype(jnp.int32)
    s = a['x'].shape[1]

    ga = jnp.concatenate([a['ffn_w_gate'].reshape(4, D_MODEL, FF_SHARD), a['ffn_w_up'].reshape(4, D_MODEL, FF_SHARD)],
                         0).astype(BF16).reshape(2, 4, D_MODEL, FF_SHARD)
    gb = jnp.concatenate([a['ffn_w_down'].reshape(4 * FF_SHARD, D_MODEL), a['ssm_w_out'][0], a['attn_w_q'][0],
                          a['attn_w_o'][0]], 0).astype(BF16)
    rows_b = gb.shape[0]
    gb = gb.reshape(2, rows_b // 2, D_MODEL)
    gc = a['ssm_w_in'][0].astype(BF16).reshape(2, D_MODEL // 2, IN_SHARD)
    gd = jnp.concatenate([a['w_k'], a['w_v']], 0).astype(BF16).reshape(2, 256, 256)
    small_shard_names = list(SMALL_SHARDED)
    gs = _pack_rows([a[n] for n in small_shard_names])
    gs = gs.reshape(2, gs.shape[0] // 2, LANES)
    gathered = all_gather_chips([ga, gb, gc, gd, gs], "gather_weights")
    fa, fb, fc, fd, fs = [lax.dynamic_update_index_in_dim(full, mine, chip, 0)
                          for full, mine in zip(gathered, [ga, gb, gc, gd, gs])]

    p = {}
    fa = fa.reshape(N_CHIPS, 2, 2, 2, D_MODEL, FF_SHARD)
    p['ffn_w_gate'] = [[fa[:, 0, l, i] for i in range(2)] for l in range(2)]
    p['ffn_w_up'] = [[fa[:, 1, l, i] for i in range(2)] for l in range(2)]
    fb = fb.reshape(N_CHIPS, rows_b, D_MODEL)
    down = fb[:, :4 * FF_SHARD].reshape(N_CHIPS, 2, 2, FF_SHARD, D_MODEL)
    p['ffn_w_down'] = [[down[:, l, i] for i in range(2)] for l in range(2)]
    r0 = 4 * FF_SHARD
    p['ssm_w_out'] = fb[:, r0:r0 + 512].reshape(D_INNER, D_MODEL)
    p['attn_w_q'] = fb[:, r0 + 512:r0 + 768].reshape(D_MODEL, D_MODEL)
    p['attn_w_o'] = fb[:, r0 + 768:r0 + 1024].reshape(D_MODEL, D_MODEL)
    w_in = fc.reshape(N_CHIPS, D_MODEL, IN_SHARD).transpose(1, 0, 2).reshape(D_MODEL, IN_PROJ_DIM)
    p['w_z'] = w_in[:, :D_INNER]
    p['w_xbc'] = w_in[:, D_INNER:D_INNER + CONV_DIM]
    p['w_dt'] = jnp.pad(w_in[:, D_INNER + CONV_DIM:], ((0, 0), (0, LANES - SSM_HEADS)))
    fd = fd.reshape(N_CHIPS, 512, 256)
    p['w_k'] = fd[:, :256].reshape(D_MODEL, 256)
    p['w_v'] = fd[:, 256:].reshape(D_MODEL, 256)
    per_chip = [_unpack_rows(fs[k], [a[n].shape for n in small_shard_names]) for k in range(N_CHIPS)]
    for idx, n in enumerate(small_shard_names):
        p[n] = jnp.concatenate([per_chip[k][idx] for k in range(N_CHIPS)], axis=SMALL_SHARDED[n])
    p['ssm_conv_w'], p['ssm_conv_b'], p['ssm_norm_w'] = p['ssm_conv_w'][0], p['ssm_conv_b'][0], p['ssm_norm_w'][0]
    for n in ('ssm_dt_bias', 'ssm_a_log', 'ssm_d', 'attn_b_q', 'attn_sinks', 'attn_b_o'):
        p[n] = a[n][0]
    for n in ('kv_norm_w', 'b_k', 'b_v', 'final_norm_w'):
        p[n] = a[n]

    loss_local, grad_x, g = local_step(a['x'][0], a['loss_target'][0], p)
    loss = lax.psum(loss_local, ("x", "y", "c"))

    dffn = g['ffn']
    ra = jnp.stack([dffn[(l, i)][w] for w in range(2) for l in range(2) for i in range(2)], axis=1)
    ra = ra.reshape(N_CHIPS, 2, 4 * D_MODEL, FF_SHARD).transpose(1, 0, 2, 3)
    rb = jnp.concatenate(
        [jnp.concatenate([dffn[(l, i)][2] for l in range(2) for i in range(2)], axis=1),
         g['ssm_w_out'].reshape(N_CHIPS, 512, D_MODEL), g['attn_w_q'].reshape(N_CHIPS, 256, D_MODEL),
         g['attn_w_o'].reshape(N_CHIPS, 256, D_MODEL)], axis=1)
    rb = rb.reshape(N_CHIPS, 2, rows_b // 2, D_MODEL).transpose(1, 0, 2, 3)
    d_in = jnp.concatenate([g['w_z'], g['w_xbc'], g['w_dt'][:, :SSM_HEADS]], axis=1)
    rc = d_in.reshape(D_MODEL, N_CHIPS, IN_SHARD).transpose(1, 0, 2).reshape(
        N_CHIPS, 2, D_MODEL // 2, IN_SHARD).transpose(1, 0, 2, 3)
    rd = jnp.concatenate([g['w_k'].reshape(N_CHIPS, 256, 256), g['w_v'].reshape(N_CHIPS, 256, 256)], axis=1)
    rd = rd.reshape(N_CHIPS, 2, 256, 256).transpose(1, 0, 2, 3)
    groups = [ra, rb, rc, rd]
    from_sibling = pair_swap(groups, "rs_pair_swap")
    chip_parts = []
    for idx, (grp, sib) in enumerate(zip(groups, from_sibling)):
        t = add_pair(grp.reshape(2, -1, grp.shape[-1]), _as2d(sib), c_idx, "rs_add_pair_%d" % idx)
        chip_parts.append(t.reshape(grp.shape[1:]))
    landed = chip_scatter(chip_parts, "rs_chip_scatter")
    halves = []
    for idx, (q, own) in enumerate(zip(landed, chip_parts)):
        t = sum_chips(q.reshape(N_CHIPS, -1, q.shape[-1]), own.reshape(N_CHIPS, -1, q.shape[-1]), chip_idx,
                      "rs_sum_chips_%d" % idx)
        halves.append(t.reshape(q.shape[1:]))
    theirs = pair_send(halves, "rs_pair_send")
    south = ci == 0
    sa, sb, sc, sd = [jnp.stack([jnp.where(south, h, t), jnp.where(south, t, h)])
                      for h, t in zip(halves, theirs)]

    gw = {}
    sa = sa.reshape(2, 2, 2, D_MODEL, FF_SHARD)
    gw['ffn_w_gate'], gw['ffn_w_up'] = sa[0], sa[1]
    sb = sb.reshape(rows_b, D_MODEL)
    gw['ffn_w_down'] = sb[:r0].reshape(2, 2, FF_SHARD, D_MODEL)
    gw['ssm_w_out'] = sb[r0:r0 + 512][None]
    gw['attn_w_q'] = sb[r0 + 512:r0 + 768][None]
    gw['attn_w_o'] = sb[r0 + 768:r0 + 1024][None]
    gw['ssm_w_in'] = sc.reshape(1, D_MODEL, IN_SHARD)
    sd = sd.reshape(512, 256)
    gw['w_k'], gw['w_v'] = sd[:256], sd[256:]

    g_small = {n: g[n] for n in SMALL}
    full_shapes = {n: g_small[n].shape for n in SMALL}
    red = all_reduce_small(_pack_rows([g_small[n] for n in SMALL]), "reduce_vectors")
    for n, t in zip(SMALL, _unpack_rows(red, [full_shapes[n] for n in SMALL])):
        if n in SMALL_SHARDED:
            ax = SMALL_SHARDED[n] - (a[n].ndim - t.ndim)
            width = a[n].shape[SMALL_SHARDED[n]]
            t = lax.dynamic_slice_in_dim(t, chip * width, width, axis=ax)
        gw[n] = t.reshape(a[n].shape)

    delta, new_m, new_v = {}, {}, {}
    for n in BIG:
        d, mo, vo = adamw(_as2d(a[n]), _as2d(gw[n]), _as2d(a['m_' + n]), _as2d(a['v_' + n]), "adamw_" + n)
        delta[n], new_m[n], new_v[n] = d.reshape(a[n].shape), mo.reshape(a[n].shape), vo.reshape(a[n].shape)
    shapes = [a[n].shape for n in SMALL]
    packed = [_pack_rows([src[n] for n in SMALL]) for src in
              (a, gw, {n: a['m_' + n] for n in SMALL}, {n: a['v_' + n] for n in SMALL})]
    outs = adamw(*packed, "adamw_vectors")
    for dst, buf in zip((delta, new_m, new_v), outs):
        for n, t in zip(SMALL, _unpack_rows(buf, shapes)):
            dst[n] = t

    return (loss, grad_x[None], *[gw[n] for n in WEIGHTS], *[delta[n] for n in WEIGHTS],
            *[new_m[n] for n in WEIGHTS], *[new_v[n] for n in WEIGHTS])
```

```python
import math

import jax
import jax.numpy as jnp
from jax import lax
from jax.experimental import pallas as pl
from jax.experimental.pallas import tpu as pltpu

F32 = jnp.float32
BF16 = jnp.bfloat16
HI = lax.Precision.HIGHEST

D_MODEL = 1024
D_INNER = 2048
SSM_HEADS = 32
SSM_GROUPS = 4
HEADS_PER_GROUP = SSM_HEADS // SSM_GROUPS
SSM_HEAD_DIM = 64
SSM_STATE = 128
GROUP_DIM = D_INNER // SSM_GROUPS
CONV_DIM = D_INNER + 2 * SSM_GROUPS * SSM_STATE
CONV_WIDTH = 4
CHUNK = 128
ATT_HEAD_DIM = 64
N_Q_HEADS = 16
N_KV_HEADS = 4
Q_PER_KV = N_Q_HEADS // N_KV_HEADS
KV_DIM = N_KV_HEADS * ATT_HEAD_DIM
WINDOW = 128
ROPE_THETA = 10000.0
D_FF = 2816
N_CHIPS = 4
N_CORES = 2
FF_SHARD = D_FF // N_CHIPS
FF_PART = FF_SHARD // N_CORES
N_PARTS = N_CHIPS * N_CORES
IN_PROJ_DIM = D_INNER + CONV_DIM + SSM_HEADS
IN_SHARD = IN_PROJ_DIM // N_CHIPS
IN_SHARD_PAD = 1312
EPS = 1e-5
NEG = -1e30
LANES = 128
VMEM_LIMIT = 56 * 1024 * 1024

ADAM_LR = 0.001
ADAM_B1 = 0.9
ADAM_B2 = 0.999
ADAM_EPS = 1e-08
ADAM_WD = 0.01
ADAM_STEP = 10

NN = ((1,), (0,))
NT = ((1,), (1,))
TN = ((0,), (0,))
MESH = pl.DeviceIdType.MESH
ANY = pl.BlockSpec(memory_space=pl.ANY)


def _dot(a, b, dims=NN, precision=None):
    return lax.dot_general(a, b, (dims, ((), ())), preferred_element_type=F32, precision=precision)


def _cp(n_grid):
    return pltpu.CompilerParams(dimension_semantics=("arbitrary",) * n_grid, vmem_limit_bytes=VMEM_LIMIT)


def _sigmoid(x):
    return 1.0 / (1.0 + jnp.exp(-x))


def _rms_fwd(xf, w):
    r = lax.rsqrt(jnp.mean(xf * xf, axis=-1, keepdims=True) + EPS)
    return xf * r * w


def _rms_bwd(dh, xf, w):
    r = lax.rsqrt(jnp.mean(xf * xf, axis=-1, keepdims=True) + EPS)
    xhat = xf * r
    dxhat = dh * w
    dx = r * (dxhat - xhat * jnp.mean(dxhat * xhat, axis=-1, keepdims=True))
    return dx, dh * xhat


def _row_tile(s, pref):
    return pref if s % pref == 0 else s


def _col_tile(n):
    for t in (1024, 768, 512, 256, 128):
        if n % t == 0:
            return t
    return n


def _sds(shape, dtype):
    return jax.ShapeDtypeStruct(tuple(shape), dtype)


class Exchange:
    def __init__(self, ins, out_shapes, sems, start, finish):
        self.ins, self.out_shapes, self.sems, self.start, self.finish = ins, out_shapes, sems, start, finish


def _place():
    x, y, c = lax.axis_index("x"), lax.axis_index("y"), lax.axis_index("c")
    others = [(1 - x, y), (x, 1 - y), (1 - x, 1 - y)]
    return x, y, c, 2 * x + y, others


def _rc(src, dst, send_sem, recv_sem, dev):
    return pltpu.make_async_remote_copy(src_ref=src, dst_ref=dst, send_sem=send_sem, recv_sem=recv_sem,
                                        device_id=dev, device_id_type=MESH)


def gather_chips(arrs):
    n = len(arrs)

    def copies(ins, outs, sems):
        send, recv = sems
        x, y, c, k, others = _place()
        ici, land, fwd, fland = [], [], [], []
        for a in range(n):
            for j, (px, py) in enumerate(others):
                ici.append(_rc(ins[a].at[c], outs[a].at[c, k], send.at[a, j], recv.at[a, j], (px, py, c)))
                blk = outs[a].at[c, 2 * px + py]
                land.append(_rc(blk, blk, send.at[a, j], recv.at[a, j], (px, py, c)))
                fwd.append(_rc(blk, blk, send.at[a, 3 + j], recv.at[a, 3 + j], (x, y, 1 - c)))
                blk2 = outs[a].at[1 - c, 2 * px + py]
                fland.append(_rc(blk2, blk2, send.at[a, 3 + j], recv.at[a, 3 + j], (x, y, 1 - c)))
        return ici, land, fwd, fland

    def start(ins, outs, sems):
        for cp in copies(ins, outs, sems)[0]:
            cp.start()

    def finish(ins, outs, sems):
        ici, land, fwd, fland = copies(ins, outs, sems)
        for arrived, onward in zip(land, fwd):
            arrived.wait_recv()
            onward.start()
        for arrived in fland:
            arrived.wait_recv()
        for cp in ici + fwd:
            cp.wait_send()

    return Exchange(list(arrs), [_sds((2, N_CHIPS) + a.shape[1:], a.dtype) for a in arrs],
                    [pltpu.SemaphoreType.DMA((n, 6)), pltpu.SemaphoreType.DMA((n, 6))], start, finish)


def scatter_chips(arrs):
    n = len(arrs)

    def copies(ins, outs, sems):
        send, recv = sems
        x, y, c, k, others = _place()
        out, land = [], []
        for a in range(n):
            for j, (px, py) in enumerate(others):
                out.append(_rc(ins[a].at[2 * px + py], outs[a].at[k], send.at[a, j], recv.at[a, j], (px, py, c)))
                blk = outs[a].at[2 * px + py]
                land.append(_rc(blk, blk, send.at[a, j], recv.at[a, j], (px, py, c)))
        return out, land

    def start(ins, outs, sems):
        for cp in copies(ins, outs, sems)[0]:
            cp.start()

    def finish(ins, outs, sems):
        out, land = copies(ins, outs, sems)
        for arrived in land:
            arrived.wait_recv()
        for cp in out:
            cp.wait_send()

    return Exchange(list(arrs), [_sds(a.shape, a.dtype) for a in arrs],
                    [pltpu.SemaphoreType.DMA((n, 3)), pltpu.SemaphoreType.DMA((n, 3))], start, finish)


def swap_cores(arrs, pick_other):
    n = len(arrs)

    def copies(ins, outs, sems):
        send, recv = sems
        x, y, c, _, _ = _place()
        return [_rc(ins[a].at[1 - c] if pick_other else ins[a], outs[a], send.at[a], recv.at[a], (x, y, 1 - c))
                for a in range(n)]

    def start(ins, outs, sems):
        for cp in copies(ins, outs, sems):
            cp.start()

    def finish(ins, outs, sems):
        for cp in copies(ins, outs, sems):
            cp.wait()

    shapes = [_sds(a.shape[1:] if pick_other else a.shape, a.dtype) for a in arrs]
    return Exchange(list(arrs), shapes, [pltpu.SemaphoreType.DMA((n,)), pltpu.SemaphoreType.DMA((n,))],
                    start, finish)


def join(*parts):
    parts = [p for p in parts if p is not None]
    if not parts:
        return None

    def split(refs, counts):
        out, pos = [], 0
        for cnt in counts:
            out.append(refs[pos:pos + cnt])
            pos += cnt
        return out

    n_in = [len(p.ins) for p in parts]
    n_out = [len(p.out_shapes) for p in parts]
    n_sem = [len(p.sems) for p in parts]

    def run(which):
        def go(ins, outs, sems):
            for p, i, o, s in zip(parts, split(ins, n_in), split(outs, n_out), split(sems, n_sem)):
                getattr(p, which)(i, o, s)
        return go

    return Exchange([a for p in parts for a in p.ins], [s for p in parts for s in p.out_shapes],
                    [s for p in parts for s in p.sems], run("start"), run("finish"))


def _pcall(body, *, out_shape, grid, in_specs, out_specs, args, name, scratch_shapes=(), ride=None, aliases=None):
    out_shape, out_specs, in_specs = tuple(out_shape), tuple(out_specs), list(in_specs)
    aliases = aliases or {}
    if ride is None:
        return pl.pallas_call(body, out_shape=out_shape, grid=grid, in_specs=in_specs, out_specs=out_specs,
                              scratch_shapes=list(scratch_shapes), input_output_aliases=aliases, name=name,
                              compiler_params=_cp(len(grid)))(*args)
    n_in, n_out, n_sc = len(args), len(out_shape), len(scratch_shapes)
    n_xi, n_xo = len(ride.ins), len(ride.out_shapes)

    def wrapped(*refs):
        pos = [0]

        def take(cnt):
            got = refs[pos[0]:pos[0] + cnt]
            pos[0] += cnt
            return got

        c_in, x_in, c_out, x_out, c_sc = take(n_in), take(n_xi), take(n_out), take(n_xo), take(n_sc)
        sems = refs[pos[0]:]
        first, last = True, True
        for d, size in enumerate(grid):
            first = jnp.logical_and(first, pl.program_id(d) == 0)
            last = jnp.logical_and(last, pl.program_id(d) == size - 1)

        @pl.when(first)
        def _():
            ride.start(x_in, x_out, sems)

        body(*c_in, *c_out, *c_sc)

        @pl.when(last)
        def _():
            ride.finish(x_in, x_out, sems)

    res = pl.pallas_call(
        wrapped, out_shape=out_shape + tuple(ride.out_shapes), grid=grid,
        in_specs=in_specs + [ANY] * n_xi, out_specs=out_specs + (ANY,) * n_xo,
        scratch_shapes=list(scratch_shapes) + list(ride.sems), input_output_aliases=aliases, name=name,
        compiler_params=_cp(len(grid)))(*args, *ride.ins)
    return res[:n_out], res[n_out:]


def run_exchange(ex, name):
    n_xi, n_xo = len(ex.ins), len(ex.out_shapes)

    def body(*refs):
        ins, outs, sems = refs[:n_xi], refs[n_xi:n_xi + n_xo], refs[n_xi + n_xo:]
        ex.start(ins, outs, sems)
        ex.finish(ins, outs, sems)

    return pl.pallas_call(body, out_shape=tuple(ex.out_shapes), in_specs=[ANY] * n_xi, out_specs=(ANY,) * n_xo,
                          scratch_shapes=list(ex.sems), name=name)(*ex.ins)


def all_reduce_small(buf, name):
    r = buf.shape[0]
    n_dev = 8

    def body(in_ref, o_ref, land, send_sems, recv_sems):
        x, y, c, _, _ = _place()
        me = 4 * x + 2 * y + c
        land[me] = in_ref[...]
        sends = []
        for d in range(1, n_dev):
            peer = (x ^ (d >> 2), y ^ ((d >> 1) & 1), c ^ (d & 1))
            cp = _rc(in_ref, land.at[me], send_sems.at[d], recv_sems.at[d], peer)
            cp.start()
            sends.append(cp)
        for d in range(1, n_dev):
            blk = land.at[me ^ d]
            _rc(blk, blk, send_sems.at[d], recv_sems.at[d], (x, y, c)).wait_recv()
        for cp in sends:
            cp.wait_send()
        tot = land[0]
        for d in range(1, n_dev):
            tot = tot + land[d]
        o_ref[...] = tot

    vm = pl.BlockSpec(memory_space=pltpu.VMEM)
    return pl.pallas_call(
        body, out_shape=_sds(buf.shape, F32), in_specs=[vm], out_specs=vm,
        scratch_shapes=[pltpu.VMEM((n_dev, r, LANES), F32), pltpu.SemaphoreType.DMA((n_dev,)),
                        pltpu.SemaphoreType.DMA((n_dev,))],
        name=name)(buf)


def rmsnorm_fwd(x, w, name):
    s, d = x.shape
    tm = _row_tile(s, 512)

    def body(x_ref, w_ref, o_ref):
        o_ref[...] = _rms_fwd(x_ref[...], w_ref[...]).astype(BF16)

    return _pcall(body, out_shape=[_sds((s, d), BF16)], grid=(s // tm,),
                  in_specs=[pl.BlockSpec((tm, d), lambda i: (i, 0)), pl.BlockSpec((1, d), lambda i: (0, 0))],
                  out_specs=[pl.BlockSpec((tm, d), lambda i: (i, 0))], args=[x, w.reshape(1, d)], name=name)[0]


def ffn_fwd(h, x, w, norm_ws, name, ride=None):
    s, d = h.shape
    n_norm = len(norm_ws)
    tm = _row_tile(s, 512)

    def body(*refs):
        h_ref, x_ref, w_ref = refs[:3]
        nw_refs = refs[3:3 + n_norm]
        o_ref = refs[3 + n_norm]
        h_refs = refs[4 + n_norm:4 + 2 * n_norm]
        gu_ref, acc = refs[4 + 2 * n_norm], refs[5 + 2 * n_norm]
        e = pl.program_id(1)
        hb = h_ref[...]
        g = _dot(hb, w_ref[0, 0], NT)
        u = _dot(hb, w_ref[0, 1], NT)
        gu_ref[0, 0] = g.astype(BF16)
        gu_ref[0, 1] = u.astype(BF16)
        part = _dot((g * _sigmoid(g) * u).astype(BF16), w_ref[0, 2])

        @pl.when(e == 0)
        def _():
            acc[...] = part

        @pl.when(e > 0)
        def _():
            acc[...] += part

        @pl.when(e == N_PARTS - 1)
        def _():
            xn = x_ref[...] + 0.5 * acc[...]
            o_ref[...] = xn
            for nw_ref, hn_ref in zip(nw_refs, h_refs):
                hn_ref[...] = _rms_fwd(xn, nw_ref[...]).astype(BF16)

    row = pl.BlockSpec((tm, d), lambda i, e: (i, 0))
    vec = pl.BlockSpec((1, d), lambda i, e: (0, 0))
    return _pcall(
        body, out_shape=[_sds((s, d), F32)] + [_sds((s, d), BF16)] * n_norm + [_sds((N_PARTS, 2, s, FF_PART), BF16)],
        grid=(s // tm, N_PARTS),
        in_specs=[row, row, pl.BlockSpec((1, 3, FF_PART, d), lambda i, e: (e, 0, 0, 0))] + [vec] * n_norm,
        out_specs=[row] * (1 + n_norm) + [pl.BlockSpec((1, 2, tm, FF_PART), lambda i, e: (e, 0, i, 0))],
        scratch_shapes=[pltpu.VMEM((tm, d), F32)],
        args=[h, x, w] + [nw.reshape(1, d) for nw in norm_ws], name=name, ride=ride)


def ffn_bwd(dxn, h, x_in, nw, gu, w, name, ride=None):
    s, d = h.shape
    tm = _row_tile(s, 512)
    ni = s // tm
    last_e = N_PARTS - 1

    def body(dxn_ref, h_ref, x_ref, nw_ref, gu_ref, w_ref, dx_ref, dnw_ref, dw_ref, dh, wacc):
        e = pl.program_id(0)
        i = pl.program_id(1)
        dxb = dxn_ref[...].astype(BF16)
        hb = h_ref[...]
        g = gu_ref[0, 0].astype(F32)
        u = gu_ref[0, 1].astype(F32)
        sg = _sigmoid(g)
        silu = g * sg
        da = 0.5 * _dot(dxb, w_ref[0, 2], NT)
        dg = (da * u * (sg * (1.0 + g * (1.0 - sg)))).astype(BF16)
        du = (da * silu).astype(BF16)
        grads = (_dot(dg, hb, TN), _dot(du, hb, TN), 0.5 * _dot((silu * u).astype(BF16), dxb, TN))

        @pl.when(i == 0)
        def _():
            for t, gt in enumerate(grads):
                wacc[t] = gt

        @pl.when(i > 0)
        def _():
            for t, gt in enumerate(grads):
                wacc[t] += gt

        @pl.when(i == ni - 1)
        def _():
            dw_ref[0] = wacc[...].astype(BF16)

        rows = pl.ds(pl.multiple_of(i * tm, tm), tm)
        dh_part = _dot(dg, w_ref[0, 0]) + _dot(du, w_ref[0, 1])

        @pl.when(e == 0)
        def _():
            dh[rows, :] = dh_part

        @pl.when(e > 0)
        def _():
            dh[rows, :] += dh_part

        @pl.when(e == last_e)
        def _():
            dx, dnw = _rms_bwd(dh[rows, :], x_ref[...], nw_ref[...])
            dx_ref[...] = dxn_ref[...] + dx
            col = jnp.sum(dnw, axis=0, keepdims=True)

            @pl.when(i == 0)
            def _():
                dnw_ref[...] = col

            @pl.when(i > 0)
            def _():
                dnw_ref[...] += col

    row = pl.BlockSpec((tm, d), lambda e, i: (i, 0))
    late = pl.BlockSpec((tm, d), lambda e, i: (jnp.where(e == last_e, i, 0), 0))
    vec = pl.BlockSpec((1, d), lambda e, i: (0, 0))
    wsp = pl.BlockSpec((1, 3, FF_PART, d), lambda e, i: (e, 0, 0, 0))
    return _pcall(
        body, out_shape=[_sds((s, d), F32), _sds((1, d), F32), _sds((N_PARTS, 3, FF_PART, d), BF16)],
        grid=(N_PARTS, ni),
        in_specs=[row, row, late, vec, pl.BlockSpec((1, 2, tm, FF_PART), lambda e, i: (e, 0, i, 0)), wsp],
        out_specs=[late, vec, wsp],
        scratch_shapes=[pltpu.VMEM((s, d), F32), pltpu.VMEM((3, FF_PART, d), F32)],
        args=[dxn, h, x_in, nw.reshape(1, d), gu, w], name=name, ride=ride)


def mm_res(a, w, x, name, bias=None, norm_ws=(), ride=None):
    s, k = a.shape
    n = w.shape[1]
    tm = _row_tile(s, 256)
    has_bias = bias is not None
    n_norm = len(norm_ws)

    def body(*refs):
        a_ref, w_ref, x_ref = refs[:3]
        pos = 3
        t = _dot(a_ref[...], w_ref[...])
        if has_bias:
            t = t + refs[pos][...]
            pos += 1
        nw_refs = refs[pos:pos + n_norm]
        o_ref = refs[pos + n_norm]
        h_refs = refs[pos + n_norm + 1:]
        xn = x_ref[...] + t
        o_ref[...] = xn
        for nw_ref, h_ref in zip(nw_refs, h_refs):
            h_ref[...] = _rms_fwd(xn, nw_ref[...]).astype(BF16)

    row = pl.BlockSpec((tm, n), lambda i: (i, 0))
    vec = pl.BlockSpec((1, n), lambda i: (0, 0))
    in_specs = [pl.BlockSpec((tm, k), lambda i: (i, 0)), pl.BlockSpec((k, n), lambda i: (0, 0)), row]
    args = [a, w, x]
    if has_bias:
        in_specs.append(vec)
        args.append(bias.reshape(1, n))
    for nw in norm_ws:
        in_specs.append(vec)
        args.append(nw.reshape(1, n))
    return _pcall(body, out_shape=[_sds((s, n), F32)] + [_sds((s, n), BF16)] * n_norm, grid=(s // tm,),
                  in_specs=in_specs, out_specs=[row] * (1 + n_norm), args=args, name=name, ride=ride)


def mm_nn(a, w, name, bias=None, out_dtype=F32):
    s, k = a.shape
    n = w.shape[1]
    tm = _row_tile(s, 512)
    tn = _col_tile(n)
    has_bias = bias is not None

    def body(*refs):
        a_ref, w_ref = refs[:2]
        o_ref = refs[-1]
        t = _dot(a_ref[...], w_ref[...])
        if has_bias:
            t = t + refs[2][...]
        o_ref[...] = t.astype(out_dtype)

    in_specs = [pl.BlockSpec((tm, k), lambda j, i: (i, 0)), pl.BlockSpec((k, tn), lambda j, i: (0, j))]
    args = [a, w]
    if has_bias:
        in_specs.append(pl.BlockSpec((1, tn), lambda j, i: (0, j)))
        args.append(bias.reshape(1, n))
    return _pcall(body, out_shape=[_sds((s, n), out_dtype)], grid=(n // tn, s // tm), in_specs=in_specs,
                  out_specs=[pl.BlockSpec((tm, tn), lambda j, i: (i, j))], args=args, name=name)[0]


def mm_nt(a, w, name, n=None, row0=0, out_dtype=F32, ride=None):
    s, k = a.shape
    n = w.shape[0] if n is None else n
    tm = _row_tile(s, 512)
    tn = _col_tile(n)
    base = row0 // tn
    assert row0 % tn == 0

    def body(a_ref, w_ref, o_ref):
        o_ref[...] = _dot(a_ref[...].astype(BF16), w_ref[...], NT).astype(out_dtype)

    res = _pcall(body, out_shape=[_sds((s, n), out_dtype)], grid=(n // tn, s // tm),
                 in_specs=[pl.BlockSpec((tm, k), lambda j, i: (i, 0)), pl.BlockSpec((tn, k), lambda j, i: (base + j, 0))],
                 out_specs=[pl.BlockSpec((tm, tn), lambda j, i: (i, j))], args=[a, w], name=name, ride=ride)
    return res[0] if ride is None else (res[0][0], res[1])


def mm_tn(a, b, name, into=None, rows=None, row0=0, m_valid=None, ride=None):
    s, m = a.shape
    n = b.shape[1]
    mv = m if m_valid is None else m_valid
    tm = _col_tile(m) if m_valid is None else mv
    tn = n if n <= 1024 else _col_tile(n)
    rows = mv if rows is None else rows
    assert row0 % tm == 0 and (m_valid is None or m == LANES)
    base = row0 // tm
    ta = m if m_valid is not None else tm

    def body(*refs):
        a_ref, b_ref, o_ref = refs[0], refs[1], refs[-1]
        t = _dot(a_ref[...].astype(BF16), b_ref[...].astype(BF16), TN)
        o_ref[...] = t[:tm].astype(BF16)

    in_specs = [pl.BlockSpec((s, ta), lambda i, j: (0, i)), pl.BlockSpec((s, tn), lambda i, j: (0, j))]
    args = [a, b]
    aliases = None
    if into is not None:
        in_specs.append(ANY)
        args.append(into)
        aliases = {2: 0}
    res = _pcall(body, out_shape=[_sds((rows, n), BF16)], grid=(mv // tm, n // tn), in_specs=in_specs,
                 out_specs=[pl.BlockSpec((tm, tn), lambda i, j: (base + i, j))], args=args, name=name, ride=ride,
                 aliases=aliases)
    return res[0] if ride is None else (res[0][0], res[1])


def mm_rms_bwd(terms, dxn, x, nw, name, ride=None):
    s, n = x.shape
    nt_ = len(terms)
    tm = _row_tile(s, 256)
    forms = [t[5] for t in terms]

    def body(*refs):
        dxn_ref, x_ref, nw_ref, dx_ref, dnw_ref = refs[2 * nt_:]
        i = pl.program_id(0)
        dh = None
        for t in range(nt_):
            part = _dot(refs[2 * t][...].astype(BF16), refs[2 * t + 1][...], NN if forms[t] == "nn" else NT)
            dh = part if dh is None else dh + part
        dx, dnw = _rms_bwd(dh, x_ref[...], nw_ref[...])
        dx_ref[...] = dxn_ref[...] + dx
        col = jnp.sum(dnw, axis=0, keepdims=True)

        @pl.when(i == 0)
        def _():
            dnw_ref[...] = col

        @pl.when(i > 0)
        def _():
            dnw_ref[...] += col

    in_specs, args = [], []
    for a, cb, w, rb, kb, form in terms:
        in_specs.append(pl.BlockSpec((tm, kb), lambda i, cb=cb: (i, cb)))
        if form == "nn":
            in_specs.append(pl.BlockSpec((kb, n), lambda i, rb=rb: (rb, 0)))
        else:
            in_specs.append(pl.BlockSpec((n, kb), lambda i, rb=rb: (0, rb)))
        args += [a, w]
    row = pl.BlockSpec((tm, n), lambda i: (i, 0))
    vec = pl.BlockSpec((1, n), lambda i: (0, 0))
    res = _pcall(body, out_shape=[_sds((s, n), F32), _sds((1, n), F32)], grid=(s // tm,),
                 in_specs=in_specs + [row, row, vec], out_specs=[row, vec],
                 args=args + [dxn, x, nw.reshape(1, n)], name=name, ride=ride)
    outs = res if ride is None else res[0]
    out = (outs[0], outs[1][0])
    return out if ride is None else (out, res[1])


def colsum(a, name):
    s, n = a.shape
    tm = _row_tile(s, 512)

    def body(a_ref, o_ref):
        col = jnp.sum(a_ref[...].astype(F32), axis=0, keepdims=True)

        @pl.when(pl.program_id(0) == 0)
        def _():
            o_ref[...] = col

        @pl.when(pl.program_id(0) > 0)
        def _():
            o_ref[...] += col

    return _pcall(body, out_shape=[_sds((1, n), F32)], grid=(s // tm,),
                  in_specs=[pl.BlockSpec((tm, n), lambda i: (i, 0))],
                  out_specs=[pl.BlockSpec((1, n), lambda i: (0, 0))], args=[a], name=name)[0][0]


def rope_tables(s):
    pos = jnp.arange(s, dtype=F32)
    inv = 1.0 / (ROPE_THETA ** (jnp.arange(0, ATT_HEAD_DIM, 2, dtype=F32) / ATT_HEAD_DIM))
    ang = pos[:, None] * inv[None, :]
    cos = jnp.tile(jnp.cos(ang), (1, 2 * LANES // ATT_HEAD_DIM))
    sin = jnp.tile(jnp.sin(ang), (1, 2 * LANES // ATT_HEAD_DIM))
    return cos, sin


def rope_apply(t, cos, sin, name, inverse=False, scale=1.0, out_dtype=BF16):
    s, n = t.shape
    tm = _row_tile(s, 512)
    half = ATT_HEAD_DIM // 2
    reps = n // LANES

    def body(t_ref, c_ref, s_ref, o_ref):
        tf = t_ref[...].astype(F32)
        c = jnp.tile(c_ref[...], (1, reps))
        sn = jnp.tile(s_ref[...], (1, reps))
        lane = lax.broadcasted_iota(jnp.int32, tf.shape, 1)
        first = (lane & (ATT_HEAD_DIM - 1)) < half
        rot = jnp.where(first, -pltpu.roll(tf, n - half, 1), pltpu.roll(tf, half, 1))
        sign = -1.0 if inverse else 1.0
        o_ref[...] = (scale * (tf * c + sign * rot * sn)).astype(out_dtype)

    tab = pl.BlockSpec((tm, LANES), lambda i: (i, 0))
    return _pcall(body, out_shape=[_sds((s, n), out_dtype)], grid=(s // tm,),
                  in_specs=[pl.BlockSpec((tm, n), lambda i: (i, 0)), tab, tab],
                  out_specs=[pl.BlockSpec((tm, n), lambda i: (i, 0))], args=[t, cos, sin], name=name)[0]


CONV_TILE = 256


def _shift_down(u, k):
    if k == 0:
        return u
    row = lax.broadcasted_iota(jnp.int32, u.shape, 0)
    return jnp.where(row >= k, pltpu.roll(u, k, 0), 0.0)


def _shift_up(u, k):
    if k == 0:
        return u
    s = u.shape[0]
    row = lax.broadcasted_iota(jnp.int32, u.shape, 0)
    return jnp.where(row < s - k, pltpu.roll(u, s - k, 0), 0.0)


def _conv_pre(u, w_ref, b_ref):
    pre = b_ref[...] + w_ref[CONV_WIDTH - 1:CONV_WIDTH, :] * u
    for k in range(CONV_WIDTH - 1):
        pre += w_ref[k:k + 1, :] * _shift_down(u, CONV_WIDTH - 1 - k)
    return pre


def conv_fwd(u, w, b, name):
    s, c = u.shape

    def body(u_ref, w_ref, b_ref, o_ref):
        pre = _conv_pre(u_ref[...], w_ref, b_ref)
        o_ref[...] = pre * _sigmoid(pre)

    col = pl.BlockSpec((s, CONV_TILE), lambda j: (0, j))
    return _pcall(body, out_shape=[_sds((s, c), F32)], grid=(c // CONV_TILE,),
                  in_specs=[col, pl.BlockSpec((CONV_WIDTH, CONV_TILE), lambda j: (0, j)),
                            pl.BlockSpec((1, CONV_TILE), lambda j: (0, j))],
                  out_specs=[col], args=[u, w, b.reshape(1, c)], name=name)[0]


def conv_bwd(dxs, db_, dc_, u, w, b, name):
    s, c = u.shape
    n_x = dxs.shape[1] // CONV_TILE
    n_b = db_.shape[1] // CONV_TILE

    def body(dx_ref, dbb_ref, dcc_ref, u_ref, w_ref, b_ref, du_ref, dw_ref, dbias_ref):
        j = pl.program_id(0)
        dact = jnp.where(j < n_x, dx_ref[...], jnp.where(j < n_x + n_b, dbb_ref[...], dcc_ref[...]))
        uf = u_ref[...]
        pre = _conv_pre(uf, w_ref, b_ref)
        sg = _sigmoid(pre)
        dpre = dact * (sg * (1.0 + pre * (1.0 - sg)))
        du = w_ref[CONV_WIDTH - 1:CONV_WIDTH, :] * dpre
        for k in range(CONV_WIDTH - 1):
            du += w_ref[k:k + 1, :] * _shift_up(dpre, CONV_WIDTH - 1 - k)
        du_ref[...] = du
        dbias_ref[...] = jnp.sum(dpre, axis=0, keepdims=True)
        for k in range(CONV_WIDTH):
            dw_ref[k:k + 1, :] = jnp.sum(dpre * _shift_down(uf, CONV_WIDTH - 1 - k), axis=0, keepdims=True)

    col = pl.BlockSpec((s, CONV_TILE), lambda j: (0, j))
    wsp = pl.BlockSpec((CONV_WIDTH, CONV_TILE), lambda j: (0, j))
    bsp = pl.BlockSpec((1, CONV_TILE), lambda j: (0, j))
    du, dw, db = _pcall(
        body, out_shape=[_sds((s, c), F32), _sds((CONV_WIDTH, c), F32), _sds((1, c), F32)], grid=(c // CONV_TILE,),
        in_specs=[pl.BlockSpec((s, CONV_TILE), lambda j: (0, jnp.minimum(j, n_x - 1))),
                  pl.BlockSpec((s, CONV_TILE), lambda j: (0, jnp.clip(j - n_x, 0, n_b - 1))),
                  pl.BlockSpec((s, CONV_TILE), lambda j: (0, jnp.clip(j - n_x - n_b, 0, n_b - 1))),
                  col, wsp, bsp],
        out_specs=[col, wsp, bsp], args=[dxs, db_, dc_, u, w, b.reshape(1, c)], name=name)
    return du, dw, db[0]


def _lane_pick(mat, idx):
    lane = lax.broadcasted_iota(jnp.int32, mat.shape, 1)
    return jnp.sum(jnp.where(lane == idx, mat, 0.0), axis=1, keepdims=True)


def _sub_pick(mat, idx):
    sub = lax.broadcasted_iota(jnp.int32, mat.shape, 0)
    return jnp.sum(jnp.where(sub == idx, mat, 0.0), axis=0, keepdims=True)


def _expand_heads(cols):
    rows = cols[0].shape[0]
    left = lax.broadcasted_iota(jnp.int32, (rows, LANES), 1) < SSM_HEAD_DIM
    return jnp.concatenate(
        [jnp.where(left, cols[2 * p], cols[2 * p + 1]) for p in range(HEADS_PER_GROUP // 2)], axis=1)


def _heads_to_lanes(mat, g):
    jj = lax.broadcasted_iota(jnp.int32, (GROUP_DIM, LANES), 0)
    ll = lax.broadcasted_iota(jnp.int32, (GROUP_DIM, LANES), 1)
    sel = (ll == HEADS_PER_GROUP * g + (jj >> 6)).astype(F32)
    return _dot(mat, sel, NN, HI)


def _softplus(x):
    return jnp.maximum(x, 0.0) + jnp.log1p(jnp.exp(-jnp.abs(x)))


def _ssd_scalars(dt_ref, bias_ref, a_ref, dtall, csall, cst):
    dta = _softplus(dt_ref[...] + bias_ref[...])
    row = lax.broadcasted_iota(jnp.int32, (CHUNK, CHUNK), 0)
    col = lax.broadcasted_iota(jnp.int32, (CHUNK, CHUNK), 1)
    tri = (row >= col).astype(F32)
    cs = _dot(tri, dta * a_ref[...], NN, HI)
    dtall[...] = dta
    csall[...] = cs
    cst[...] = cs.T


def _decay_mat(cs_col, cs_row):
    row = lax.broadcasted_iota(jnp.int32, (CHUNK, CHUNK), 0)
    col = lax.broadcasted_iota(jnp.int32, (CHUNK, CHUNK), 1)
    return jnp.exp(jnp.where(row >= col, cs_col - cs_row, NEG))


def _head_mask(xpair, right):
    lane = lax.broadcasted_iota(jnp.int32, xpair.shape, 1)
    keep = (lane >= SSM_HEAD_DIM) if right else (lane < SSM_HEAD_DIM)
    return jnp.where(keep, xpair, 0.0)


def _chunk_cols(x_all, g):
    return [_lane_pick(x_all, HEADS_PER_GROUP * g + r) for r in range(HEADS_PER_GROUP)]


def _decay_col(cs_cols):
    return jnp.concatenate(
        [jnp.broadcast_to(jnp.exp(cc[CHUNK - 1:CHUNK, :]), (SSM_HEAD_DIM, 1)) for cc in cs_cols], axis=0)


def ssd_fwd(act, z, dtp, bias_p, a_p, d_p, normw, name, ride=None):
    s = act.shape[0]
    nc = s // CHUNK
    b_off = D_INNER // SSM_STATE
    c_off = b_off + SSM_GROUPS

    def body(xs_ref, b_ref, c_ref, z_ref, dt_ref, bias_ref, a_ref, d_ref, nw_ref,
             yn_ref, y_ref, st_ref, state, dtall, csall, cst):
        c = pl.program_id(0)
        g = pl.program_id(1)

        @pl.when(g == 0)
        def _():
            _ssd_scalars(dt_ref, bias_ref, a_ref, dtall, csall, cst)

        @pl.when(c == 0)
        def _():
            state[g] = jnp.zeros((GROUP_DIM, SSM_STATE), F32)

        cs_cols = _chunk_cols(csall[...], g)
        dt_cols = _chunk_cols(dtall[...], g)
        cs_rows = [_sub_pick(cst[...], HEADS_PER_GROUP * g + r) for r in range(HEADS_PER_GROUP)]
        d_cols = _chunk_cols(d_ref[...], g)
        cs_exp = _expand_heads(cs_cols)
        dt_exp = _expand_heads(dt_cols)
        d_exp = _expand_heads(d_cols)
        xs = xs_ref[...]
        bb = b_ref[...].astype(BF16)
        cb16 = c_ref[...].astype(BF16)
        xdt = xs * dt_exp
        s_prev = state[g]
        st_ref[0, 0] = s_prev
        y_off = _dot(cb16, s_prev.astype(BF16), NT) * jnp.exp(cs_exp)
        decay_st = jnp.exp(cs_exp[CHUNK - 1:CHUNK, :] - cs_exp)
        contrib = _dot((xdt * decay_st).astype(BF16), bb, TN)
        state[g] = _decay_col(cs_cols) * s_prev + contrib
        cbm = _dot(cb16, bb, NT)
        pairs = []
        for p in range(HEADS_PER_GROUP // 2):
            xpair = xdt[:, LANES * p:LANES * (p + 1)]
            m0 = (cbm * _decay_mat(cs_cols[2 * p], cs_rows[2 * p])).astype(BF16)
            m1 = (cbm * _decay_mat(cs_cols[2 * p + 1], cs_rows[2 * p + 1])).astype(BF16)
            pairs.append(_dot(m0, _head_mask(xpair, False).astype(BF16))
                         + _dot(m1, _head_mask(xpair, True).astype(BF16)))
        y = jnp.concatenate(pairs, axis=1) + y_off + xs * d_exp
        y_ref[...] = y
        zf = z_ref[...]
        yg = y * (zf * _sigmoid(zf))
        yn_ref[...] = _rms_fwd(yg, nw_ref[...]).astype(BF16)

    grp = pl.BlockSpec((CHUNK, GROUP_DIM), lambda c, g: (c, g))
    par = pl.BlockSpec((1, LANES), lambda c, g: (0, 0))
    return _pcall(
        body,
        out_shape=[_sds((s, D_INNER), BF16), _sds((s, D_INNER), F32),
                   _sds((nc, SSM_GROUPS, GROUP_DIM, SSM_STATE), F32)],
        grid=(nc, SSM_GROUPS),
        in_specs=[grp,
                  pl.BlockSpec((CHUNK, SSM_STATE), lambda c, g: (c, b_off + g)),
                  pl.BlockSpec((CHUNK, SSM_STATE), lambda c, g: (c, c_off + g)),
                  grp,
                  pl.BlockSpec((CHUNK, LANES), lambda c, g: (c, 0)),
                  par, par, par,
                  pl.BlockSpec((1, GROUP_DIM), lambda c, g: (0, g))],
        out_specs=[grp, grp, pl.BlockSpec((1, 1, GROUP_DIM, SSM_STATE), lambda c, g: (c, g, 0, 0))],
        scratch_shapes=[pltpu.VMEM((SSM_GROUPS, GROUP_DIM, SSM_STATE), F32),
                        pltpu.VMEM((CHUNK, LANES), F32), pltpu.VMEM((CHUNK, LANES), F32),
                        pltpu.VMEM((LANES, CHUNK), F32)],
        args=[act, act, act, z, dtp, bias_p, a_p, d_p, normw], name=name, ride=ride)


def ssd_bwd(dyn, act, z, y_pre, states, dtp, bias_p, a_p, d_p, normw, name, ride=None):
    s = act.shape[0]
    nc = s // CHUNK
    b_off = D_INNER // SSM_STATE
    c_off = b_off + SSM_GROUPS

    def body(dyn_ref, xs_ref, b_ref, c_ref, z_ref, y_ref, st_ref, dt_ref, bias_ref, a_ref, d_ref, nw_ref,
             dxs_ref, db_ref, dc_ref, dz_ref, ddt_ref, dnw_ref, dbias_ref, da_ref, dd_ref,
             dstate, dtall, csall, cst):
        c = pl.program_id(0)
        g = pl.program_id(1)

        @pl.when(g == 0)
        def _():
            _ssd_scalars(dt_ref, bias_ref, a_ref, dtall, csall, cst)
            ddt_ref[...] = jnp.zeros((CHUNK, LANES), F32)

        @pl.when(c == 0)
        def _():
            dstate[g] = jnp.zeros((GROUP_DIM, SSM_STATE), F32)

        @pl.when(jnp.logical_and(c == 0, g == 0))
        def _():
            dnw_ref[...] = jnp.zeros(dnw_ref.shape, F32)
            dbias_ref[...] = jnp.zeros((1, LANES), F32)
            da_ref[...] = jnp.zeros((1, LANES), F32)
            dd_ref[...] = jnp.zeros((1, LANES), F32)

        cs_cols = _chunk_cols(csall[...], g)
        dt_cols = _chunk_cols(dtall[...], g)
        cs_rows = [_sub_pick(cst[...], HEADS_PER_GROUP * g + r) for r in range(HEADS_PER_GROUP)]
        d_cols = _chunk_cols(d_ref[...], g)
        cs_exp = _expand_heads(cs_cols)
        dt_exp = _expand_heads(dt_cols)
        d_exp = _expand_heads(d_cols)
        xs = xs_ref[...]
        bb = b_ref[...].astype(BF16)
        cb16 = c_ref[...].astype(BF16)
        xdt = xs * dt_exp
        s_prev = st_ref[0, 0]
        s_prev16 = s_prev.astype(BF16)
        ds_next = dstate[g]
        ds16 = ds_next.astype(BF16)

        zf = z_ref[...]
        sz = _sigmoid(zf)
        silu_z = zf * sz
        y = y_ref[...]
        yg = y * silu_z
        dout = dyn_ref[...]
        dyg, dnw = _rms_bwd(dout, yg, nw_ref[...])
        dnw_ref[pl.ds(g, 1), :] += jnp.sum(dnw, axis=0, keepdims=True)
        dy = dyg * silu_z
        dz_ref[...] = dyg * y * (sz * (1.0 + zf * (1.0 - sz)))
        dd_ref[...] += jnp.sum(_heads_to_lanes(dy * xs, g), axis=0, keepdims=True)

        exp_cs = jnp.exp(cs_exp)
        decay_st = jnp.exp(cs_exp[CHUNK - 1:CHUNK, :] - cs_exp)
        cs_t = _dot(cb16, s_prev16, NT)
        dyo = dy * exp_cs
        dc_acc = _dot(dyo.astype(BF16), s_prev16, NN)
        g1 = _dot(bb, ds16, NT)
        xds = xdt * decay_st
        db_acc = _dot(xds.astype(BF16), ds16, NN)
        dxdt_off = g1 * decay_st
        t_exp = g1 * xds
        dcs_exp = dy * cs_t * exp_cs - t_exp
        decay_c = _decay_col(cs_cols)
        dstate[g] = decay_c * ds_next + _dot(dyo.astype(BF16), cb16, TN)
        dlast_col = jnp.sum(ds_next * s_prev, axis=1, keepdims=True) * decay_c
        jj = lax.broadcasted_iota(jnp.int32, (GROUP_DIM, LANES), 0)
        ll = lax.broadcasted_iota(jnp.int32, (GROUP_DIM, LANES), 1)
        sel = ll == HEADS_PER_GROUP * g + (jj >> 6)
        dlast = jnp.sum(jnp.where(sel, dlast_col, 0.0), axis=0, keepdims=True)
        t_all = _heads_to_lanes(t_exp, g)
        dlast += jnp.sum(t_all, axis=0, keepdims=True)
        dcs_all = _heads_to_lanes(dcs_exp, g)

        cbm = _dot(cb16, bb, NT)
        dcb = jnp.zeros((CHUNK, CHUNK), F32)
        dcs_rows = jnp.zeros((LANES, CHUNK), F32)
        lane_l = lax.broadcasted_iota(jnp.int32, (CHUNK, LANES), 1)
        sub_l = lax.broadcasted_iota(jnp.int32, (LANES, CHUNK), 0)
        dxdt_pairs = []
        for p in range(HEADS_PER_GROUP // 2):
            xpair16 = xdt[:, LANES * p:LANES * (p + 1)].astype(BF16)
            dypair = dy[:, LANES * p:LANES * (p + 1)]
            acc = None
            for r in (2 * p, 2 * p + 1):
                lm = _decay_mat(cs_cols[r], cs_rows[r])
                m = cbm * lm
                dyh = _head_mask(dypair, r % 2 == 1).astype(BF16)
                dm = _dot(dyh, xpair16, NT)
                dcb += dm * lm
                q = dm * m
                idx = HEADS_PER_GROUP * g + r
                dcs_all += jnp.where(lane_l == idx, jnp.sum(q, axis=1, keepdims=True), 0.0)
                dcs_rows -= jnp.where(sub_l == idx, jnp.sum(q, axis=0, keepdims=True), 0.0)
                part = _dot(m.astype(BF16), dyh, TN)
                acc = part if acc is None else acc + part
            dxdt_pairs.append(acc)
        dxdt = jnp.concatenate(dxdt_pairs, axis=1) + dxdt_off
        dcb16 = dcb.astype(BF16)
        dc_ref[...] = dc_acc + _dot(dcb16, bb, NN)
        db_ref[...] = db_acc + _dot(dcb16, cb16, TN)
        dxs_ref[...] = dxdt * dt_exp + dy * d_exp

        dcs_all += dcs_rows.T
        row = lax.broadcasted_iota(jnp.int32, (CHUNK, CHUNK), 0)
        col = lax.broadcasted_iota(jnp.int32, (CHUNK, CHUNK), 1)
        last_row = lax.broadcasted_iota(jnp.int32, (CHUNK, LANES), 0) == CHUNK - 1
        dcs_all += jnp.where(last_row, dlast, 0.0)
        da_all = _dot((col >= row).astype(F32), dcs_all, NN, HI)
        dta = dtall[...]
        in_group = jnp.logical_and(lane_l >= HEADS_PER_GROUP * g, lane_l < HEADS_PER_GROUP * (g + 1))
        ddt = jnp.where(in_group, da_all * a_ref[...] + _heads_to_lanes(dxdt * xs, g), 0.0)
        da_ref[...] += jnp.sum(jnp.where(in_group, da_all * dta, 0.0), axis=0, keepdims=True)
        ddt_raw = ddt * _sigmoid(dt_ref[...] + bias_ref[...])
        ddt_ref[...] += ddt_raw
        dbias_ref[...] += jnp.sum(ddt_raw, axis=0, keepdims=True)

    rev = lambda c, g: (nc - 1 - c, g)
    grp = pl.BlockSpec((CHUNK, GROUP_DIM), rev)
    st = pl.BlockSpec((CHUNK, SSM_STATE), rev)
    par = pl.BlockSpec((1, LANES), lambda c, g: (0, 0))
    dtb = pl.BlockSpec((CHUNK, LANES), lambda c, g: (nc - 1 - c, 0))
    f = lambda shape: _sds(shape, F32)
    return _pcall(
        body,
        out_shape=[f((s, D_INNER)), f((s, SSM_GROUPS * SSM_STATE)), f((s, SSM_GROUPS * SSM_STATE)),
                   f((s, D_INNER)), f((s, LANES)), f((8, GROUP_DIM)), f((1, LANES)), f((1, LANES)), f((1, LANES))],
        grid=(nc, SSM_GROUPS),
        in_specs=[grp, grp,
                  pl.BlockSpec((CHUNK, SSM_STATE), lambda c, g: (nc - 1 - c, b_off + g)),
                  pl.BlockSpec((CHUNK, SSM_STATE), lambda c, g: (nc - 1 - c, c_off + g)),
                  grp, grp,
                  pl.BlockSpec((1, 1, GROUP_DIM, SSM_STATE), lambda c, g: (nc - 1 - c, g, 0, 0)),
                  dtb, par, par, par,
                  pl.BlockSpec((1, GROUP_DIM), lambda c, g: (0, g))],
        out_specs=[grp, st, st, grp, dtb, pl.BlockSpec((8, GROUP_DIM), lambda c, g: (0, 0)), par, par, par],
        scratch_shapes=[pltpu.VMEM((SSM_GROUPS, GROUP_DIM, SSM_STATE), F32),
                        pltpu.VMEM((CHUNK, LANES), F32), pltpu.VMEM((CHUNK, LANES), F32),
                        pltpu.VMEM((LANES, CHUNK), F32)],
        args=[dyn, act, act, act, z, y_pre, states, dtp, bias_p, a_p, d_p, normw], name=name, ride=ride)


def _attn_probs(q, kp, kc, sink, n):
    sp = _dot(q, kp, NT)
    sc = _dot(q, kc, NT)
    i = lax.broadcasted_iota(jnp.int32, sp.shape, 0) & (WINDOW - 1)
    j = lax.broadcasted_iota(jnp.int32, sp.shape, 1)
    sp = jnp.where(jnp.logical_and(j > i, n > 0), sp, NEG)
    sc = jnp.where(j <= i, sc, NEG)
    m = jnp.maximum(jnp.maximum(jnp.max(sp, axis=1, keepdims=True), jnp.max(sc, axis=1, keepdims=True)), sink)
    pp = jnp.exp(sp - m)
    pc = jnp.exp(sc - m)
    ps = jnp.exp(sink - m)
    inv = 1.0 / (jnp.sum(pp, axis=1, keepdims=True) + jnp.sum(pc, axis=1, keepdims=True) + ps)
    return pp * inv, pc * inv, ps * inv


def attn_fwd(qt, kt, vt, sink_rows, name, ride=None):
    s = qt.shape[1]
    nb = s // WINDOW
    rows = Q_PER_KV * WINDOW

    def body(q_ref, kp_ref, kc_ref, vp_ref, vc_ref, sk_ref, o_ref):
        n = pl.program_id(1)
        q = q_ref[...].reshape(rows, ATT_HEAD_DIM)
        pp, pc, _ = _attn_probs(q, kp_ref[0], kc_ref[0], sk_ref[0], n)
        o = _dot(pp.astype(BF16), vp_ref[0]) + _dot(pc.astype(BF16), vc_ref[0])
        o_ref[...] = o.reshape(Q_PER_KV, WINDOW, ATT_HEAD_DIM).astype(BF16)

    qsp = pl.BlockSpec((Q_PER_KV, WINDOW, ATT_HEAD_DIM), lambda h, n: (h, n, 0))
    prev = pl.BlockSpec((1, WINDOW, ATT_HEAD_DIM), lambda h, n: (h, jnp.maximum(n - 1, 0), 0))
    cur = pl.BlockSpec((1, WINDOW, ATT_HEAD_DIM), lambda h, n: (h, n, 0))
    return _pcall(body, out_shape=[_sds(qt.shape, BF16)], grid=(N_KV_HEADS, nb),
                  in_specs=[qsp, prev, cur, prev, cur, pl.BlockSpec((1, rows, 1), lambda h, n: (h, 0, 0))],
                  out_specs=[qsp], args=[qt, kt, kt, vt, vt, sink_rows], name=name, ride=ride)


def attn_bwd(qt, kt, vt, sink_rows, dot_, name, ride=None):
    s = qt.shape[1]
    nb = s // WINDOW
    rows = Q_PER_KV * WINDOW

    def body(q_ref, kp_ref, kc_ref, vp_ref, vc_ref, sk_ref, do_ref, dq_ref, dk_ref, dv_ref, ds_ref, kacc, vacc):
        n = pl.program_id(1)

        @pl.when(n < nb)
        def _():
            q = q_ref[...].reshape(rows, ATT_HEAD_DIM)
            do = do_ref[...].reshape(rows, ATT_HEAD_DIM)
            kp, kc, vp, vc = kp_ref[0], kc_ref[0], vp_ref[0], vc_ref[0]
            pp, pc, ps = _attn_probs(q, kp, kc, sk_ref[0], n)
            dpp = _dot(do, vp, NT)
            dpc = _dot(do, vc, NT)
            delta = jnp.sum(pp * dpp, axis=1, keepdims=True) + jnp.sum(pc * dpc, axis=1, keepdims=True)
            dsp = (pp * (dpp - delta)).astype(BF16)
            dsc = (pc * (dpc - delta)).astype(BF16)
            dq = _dot(dsp, kp) + _dot(dsc, kc)
            dq_ref[...] = dq.reshape(Q_PER_KV, WINDOW, ATT_HEAD_DIM)
            dk_prev = _dot(dsp, q, TN)
            dv_prev = _dot(pp.astype(BF16), do, TN)

            @pl.when(n == 0)
            def _():
                dk_ref[0] = dk_prev
                dv_ref[0] = dv_prev

            @pl.when(n > 0)
            def _():
                dk_ref[0] = kacc[...] + dk_prev
                dv_ref[0] = vacc[...] + dv_prev

            kacc[...] = _dot(dsc, q, TN)
            vacc[...] = _dot(pc.astype(BF16), do, TN)
            dsk = -ps * delta
            sub = lax.broadcasted_iota(jnp.int32, (8, LANES), 0)
            tile = jnp.zeros((8, LANES), F32)
            for h in range(Q_PER_KV):
                tile += jnp.where(sub == h, jnp.sum(dsk[h * WINDOW:(h + 1) * WINDOW, :], axis=0, keepdims=True), 0.0)
            ds_ref[0, 0] = tile

        @pl.when(n == nb)
        def _():
            dk_ref[0] = kacc[...]
            dv_ref[0] = vacc[...]
            ds_ref[0, 0] = jnp.zeros((8, LANES), F32)

    last = nb - 1
    qsp = pl.BlockSpec((Q_PER_KV, WINDOW, ATT_HEAD_DIM), lambda h, n: (h, jnp.minimum(n, last), 0))
    prev = pl.BlockSpec((1, WINDOW, ATT_HEAD_DIM), lambda h, n: (h, jnp.clip(n - 1, 0, last), 0))
    cur = pl.BlockSpec((1, WINDOW, ATT_HEAD_DIM), lambda h, n: (h, jnp.minimum(n, last), 0))
    dkv = pl.BlockSpec((1, WINDOW, ATT_HEAD_DIM), lambda h, n: (h, jnp.maximum(n - 1, 0), 0))
    f = lambda shape: _sds(shape, F32)
    return _pcall(
        body, out_shape=[f(qt.shape), f(kt.shape), f(vt.shape), f((N_KV_HEADS, nb + 1, 8, LANES))],
        grid=(N_KV_HEADS, nb + 1),
        in_specs=[qsp, prev, cur, prev, cur, pl.BlockSpec((1, rows, 1), lambda h, n: (h, 0, 0)), qsp],
        out_specs=[qsp, dkv, dkv, pl.BlockSpec((1, 1, 8, LANES), lambda h, n: (h, n, 0, 0))],
        scratch_shapes=[pltpu.VMEM((WINDOW, ATT_HEAD_DIM), F32), pltpu.VMEM((WINDOW, ATT_HEAD_DIM), F32)],
        args=[qt, kt, kt, vt, vt, sink_rows, dot_], name=name, ride=ride)


def loss_head(x, w, tgt, name):
    s, d = x.shape
    tm = _row_tile(s, 256)

    def body(x_ref, w_ref, t_ref, loss_ref, dx_ref, dw_ref):
        i = pl.program_id(0)
        xf = x_ref[...]
        wv = w_ref[...]
        r = lax.rsqrt(jnp.mean(xf * xf, axis=-1, keepdims=True) + EPS)
        xhat = xf * r
        e = xhat * wv - t_ref[...]
        part = 0.5 * jnp.sum(jnp.mean(e * e, axis=-1, keepdims=True), axis=0, keepdims=True)
        dy = e * (1.0 / d)
        dxhat = dy * wv
        dx_ref[...] = r * (dxhat - xhat * jnp.mean(dxhat * xhat, axis=-1, keepdims=True))
        col = jnp.sum(dy * xhat, axis=0, keepdims=True)

        @pl.when(i == 0)
        def _():
            loss_ref[...] = jnp.broadcast_to(part, (1, LANES))
            dw_ref[...] = col

        @pl.when(i > 0)
        def _():
            loss_ref[...] += jnp.broadcast_to(part, (1, LANES))
            dw_ref[...] += col

    row = pl.BlockSpec((tm, d), lambda i: (i, 0))
    vec = pl.BlockSpec((1, d), lambda i: (0, 0))
    return _pcall(body, out_shape=[_sds((1, LANES), F32), _sds((s, d), F32), _sds((1, d), F32)], grid=(s // tm,),
                  in_specs=[row, vec, row], out_specs=[pl.BlockSpec((1, LANES), lambda i: (0, 0)), row, vec],
                  args=[x, w.reshape(1, d), tgt], name=name)


def _tile_rows(r, c, max_elems=262144, mult=16):
    best = None
    for t in range(mult, r + 1, mult):
        if r % t == 0 and t * c <= max_elems:
            best = t
    return best or r


def add_pair(xh, p, c_idx, name):
    _, r, c = xh.shape
    tr = _tile_rows(r, c)

    def body(c_ref, x_ref, p_ref, o_ref):
        o_ref[...] = (x_ref[0].astype(F32) + p_ref[...].astype(F32)).astype(BF16)

    blk = pl.BlockSpec((tr, c), lambda i, cr: (i, 0))
    return pl.pallas_call(
        body, out_shape=_sds((r, c), BF16),
        grid_spec=pltpu.PrefetchScalarGridSpec(
            num_scalar_prefetch=1, grid=(r // tr,),
            in_specs=[pl.BlockSpec((1, tr, c), lambda i, cr: (cr[0], i, 0)), blk], out_specs=blk),
        name=name, compiler_params=_cp(1))(c_idx, xh, p)


def sum_chips(q, own, chip_idx, name):
    _, r, c = q.shape
    tr = _tile_rows(r, c)

    def body(k_ref, q_ref, own_ref, o_ref):
        k = k_ref[0]
        mine = own_ref[0].astype(F32)
        tot = None
        for j in range(N_CHIPS):
            term = jnp.where(k == j, mine, q_ref[j].astype(F32))
            tot = term if tot is None else tot + term
        o_ref[...] = tot

    return pl.pallas_call(
        body, out_shape=_sds((r, c), F32),
        grid_spec=pltpu.PrefetchScalarGridSpec(
            num_scalar_prefetch=1, grid=(r // tr,),
            in_specs=[pl.BlockSpec((N_CHIPS, tr, c), lambda i, kr: (0, i, 0)),
                      pl.BlockSpec((1, tr, c), lambda i, kr: (kr[0], i, 0))],
            out_specs=pl.BlockSpec((tr, c), lambda i, kr: (i, 0))),
        name=name, compiler_params=_cp(1))(chip_idx, q, own)


def adamw(w, g, m, v, name):
    r, c = w.shape
    tr = _tile_rows(r, c, max_elems=131072, mult=8)
    c1 = 1.0 / (1.0 - ADAM_B1 ** ADAM_STEP)
    c2 = 1.0 / (1.0 - ADAM_B2 ** ADAM_STEP)

    def body(w_ref, g_ref, m_ref, v_ref, d_ref, mo_ref, vo_ref):
        gf = g_ref[...]
        mn = ADAM_B1 * m_ref[...] + (1.0 - ADAM_B1) * gf
        vn = ADAM_B2 * v_ref[...] + (1.0 - ADAM_B2) * (gf * gf)
        mo_ref[...] = mn
        vo_ref[...] = vn
        d_ref[...] = -ADAM_LR * ((mn * c1) / (jnp.sqrt(vn * c2) + ADAM_EPS) + ADAM_WD * w_ref[...])

    blk = pl.BlockSpec((tr, c), lambda i: (i, 0))
    out = _sds((r, c), F32)
    return _pcall(body, out_shape=[out, out, out], grid=(r // tr,), in_specs=[blk] * 4, out_specs=[blk] * 3,
                  args=[w, g, m, v], name=name)


WEIGHTS = ['norm_w', 'ffn_w_gate', 'ffn_w_up', 'ffn_w_down', 'ssm_w_in', 'ssm_conv_w', 'ssm_conv_b', 'ssm_dt_bias',
           'ssm_a_log', 'ssm_d', 'ssm_norm_w', 'ssm_w_out', 'kv_norm_w', 'w_k', 'b_k', 'w_v', 'b_v', 'attn_w_q',
           'attn_b_q', 'attn_sinks', 'attn_w_o', 'attn_b_o', 'final_norm_w']
BIG = ['ffn_w_gate', 'ffn_w_up', 'ffn_w_down', 'ssm_w_in', 'ssm_w_out', 'w_k', 'w_v', 'attn_w_q', 'attn_w_o']
TRANSPOSED = ('ffn_w_gate', 'ffn_w_up', 'ssm_w_in')
SMALL = [n for n in WEIGHTS if n not in BIG]
SMALL_SHARDED = {'norm_w': 2, 'ssm_conv_w': 2, 'ssm_conv_b': 1, 'ssm_norm_w': 1}
ROW_ALIGN = 8 * LANES


def _pack_rows(parts):
    flat = jnp.concatenate([p.reshape(-1).astype(F32) for p in parts])
    pad = (-flat.size) % ROW_ALIGN
    return jnp.pad(flat, (0, pad)).reshape(-1, LANES)


def _unpack_rows(buf, shapes):
    flat = buf.reshape(-1)
    out, pos = [], 0
    for shp in shapes:
        size = math.prod(shp)
        out.append(flat[pos:pos + size].reshape(shp))
        pos += size
    return out


def _as2d(a):
    return a.reshape(-1, a.shape[-1])


def _heads_major(t, n_heads):
    s = t.shape[0]
    return t.reshape(s, n_heads, ATT_HEAD_DIM).transpose(1, 0, 2)


def _tokens_major(t):
    h, s, dh = t.shape
    return t.transpose(1, 0, 2).reshape(s, h * dh)


def _pad_lanes(v):
    return jnp.pad(v.reshape(1, -1), ((0, 0), (0, LANES - v.size)))


def _chips_first(t):
    return t.swapaxes(0, 1).reshape((-1,) + t.shape[3:])


def _parts_first(t, rows):
    return t.reshape((N_CHIPS, N_CORES, rows) + t.shape[1:]).swapaxes(0, 1)


def kernel(*args):
    names = (['x'] + WEIGHTS + ['loss_target'] + ['m_' + n for n in WEIGHTS] + ['v_' + n for n in WEIGHTS])
    a = dict(zip(names, args))
    for n in TRANSPOSED:
        for pre in ('', 'm_', 'v_'):
            a[pre + n] = a[pre + n].swapaxes(-1, -2)
    xi, yi, ci = lax.axis_index("x"), lax.axis_index("y"), lax.axis_index("c")
    chip = 2 * xi + yi
    south = ci == 0
    c_idx = jnp.reshape(ci, (1,)).astype(jnp.int32)
    chip_idx = jnp.reshape(chip, (1,)).astype(jnp.int32)
    x0 = a['x'][0]
    s = x0.shape[0]
    cos, sin = rope_tables(s)

    def own_slot(full, mine):
        return lax.dynamic_update_slice_in_dim(full, mine[:, None], chip, axis=1)

    def ffn_shard(l, i):
        w3 = jnp.stack([a['ffn_w_gate'][l, i], a['ffn_w_up'][l, i], a['ffn_w_down'][l, i]]).astype(BF16)
        return w3.reshape(3, N_CORES, FF_PART, D_MODEL).swapaxes(0, 1)
    w_in_sh = jnp.pad(a['ssm_w_in'][0], ((0, IN_SHARD_PAD - IN_SHARD), (0, 0))).astype(BF16).reshape(
        N_CORES, IN_SHARD_PAD // 2, D_MODEL)
    w_out_sh = a['ssm_w_out'][0].astype(BF16).reshape(N_CORES, 256, D_MODEL)
    attn_sh = jnp.stack([a['attn_w_q'][0], a['attn_w_o'][0]]).astype(BF16)
    kv_sh = jnp.stack([a['w_k'], a['w_v']]).astype(BF16)
    small_names = list(SMALL_SHARDED)
    small_sh = _pack_rows([a[n] for n in small_names])
    small_sh = small_sh.reshape(N_CORES, small_sh.shape[0] // 2, LANES)

    sh00 = ffn_shard(0, 0)
    w00, smalls = run_exchange(gather_chips([sh00, small_sh]), "gather_first")
    w00 = own_slot(w00, sh00).reshape(N_PARTS, 3, FF_PART, D_MODEL)
    smalls = own_slot(smalls, small_sh)
    p = {}
    per_chip = [_unpack_rows(smalls[:, k], [a[n].shape for n in small_names]) for k in range(N_CHIPS)]
    for idx, n in enumerate(small_names):
        p[n] = jnp.concatenate([per_chip[k][idx] for k in range(N_CHIPS)], axis=SMALL_SHARDED[n])
    nw = p['norm_w']
    conv_w, conv_b, ssm_nw = p['ssm_conv_w'][0], p['ssm_conv_b'][0], p['ssm_norm_w'][0].reshape(1, D_INNER)

    h00 = rmsnorm_fwd(x0, nw[0, 0], "norm_in")
    (x1, h01, gu00), (w_in_g, w_out_g) = ffn_fwd(h00, x0, w00, [nw[0, 1]], "ffn_fwd_00",
                                                 ride=gather_chips([w_in_sh, w_out_sh]))
    w_in_t = _chips_first(own_slot(w_in_g, w_in_sh)).reshape(N_CHIPS, IN_SHARD_PAD, D_MODEL)[:, :IN_SHARD].reshape(
        IN_PROJ_DIM, D_MODEL)
    w_dt_t = jnp.pad(w_in_t[D_INNER + CONV_DIM:], ((0, LANES - SSM_HEADS), (0, 0)))
    w_out = _chips_first(own_slot(w_out_g, w_out_sh))

    zz = mm_nt(h01, w_in_t, "ssm_in_z", n=D_INNER)
    xbc = mm_nt(h01, w_in_t, "ssm_in_xbc", n=CONV_DIM, row0=D_INNER)
    dtp = mm_nt(h01, w_dt_t, "ssm_in_dt")
    act = conv_fwd(xbc, conv_w, conv_b, "ssm_conv")
    bias_p = _pad_lanes(a['ssm_dt_bias'][0])
    a_p = _pad_lanes(-jnp.exp(a['ssm_a_log'][0]))
    d_p = _pad_lanes(a['ssm_d'][0])
    sh01 = ffn_shard(0, 1)
    (yn, y_pre, states), (w01,) = ssd_fwd(act, zz, dtp, bias_p, a_p, d_p, ssm_nw, "ssd_fwd",
                                          ride=gather_chips([sh01]))
    w01 = own_slot(w01, sh01).reshape(N_PARTS, 3, FF_PART, D_MODEL)
    x2, h02 = mm_res(yn, w_out, x1, "ssm_out", norm_ws=[nw[0, 2]])
    sh10 = ffn_shard(1, 0)
    (x3, hkv, h10, gu01), (kv_g, w10) = ffn_fwd(h02, x2, w01, [a['kv_norm_w'], nw[1, 0]], "ffn_fwd_01",
                                                ride=gather_chips([kv_sh, sh10]))
    kv_g = own_slot(kv_g, kv_sh)
    w_k, w_v = kv_g[0].reshape(D_MODEL, KV_DIM), kv_g[1].reshape(D_MODEL, KV_DIM)
    w10 = own_slot(w10, sh10).reshape(N_PARTS, 3, FF_PART, D_MODEL)

    k_rot = rope_apply(mm_nn(hkv, w_k, "kv_k", bias=a['b_k']), cos, sin, "rope_k")
    v = mm_nn(hkv, w_v, "kv_v", bias=a['b_v'], out_dtype=BF16)
    kt = _heads_major(k_rot, N_KV_HEADS)
    vt = _heads_major(v, N_KV_HEADS)

    (x4, h11, gu10), (attn_g,) = ffn_fwd(h10, x3, w10, [nw[1, 1]], "ffn_fwd_10", ride=gather_chips([attn_sh]))
    attn_g = own_slot(attn_g, attn_sh)
    w_q, w_o = attn_g[0].reshape(D_MODEL, D_MODEL), attn_g[1].reshape(D_MODEL, D_MODEL)
    scale = 1.0 / math.sqrt(ATT_HEAD_DIM)
    q_rot = rope_apply(mm_nn(h11, w_q, "attn_q", bias=a['attn_b_q'][0]), cos, sin, "rope_q", scale=scale)
    qt = _heads_major(q_rot, N_Q_HEADS)
    sink_rows = jnp.repeat(a['attn_sinks'][0].reshape(N_KV_HEADS, Q_PER_KV), WINDOW, axis=1).reshape(
        N_KV_HEADS, Q_PER_KV * WINDOW, 1)
    sh11 = ffn_shard(1, 1)
    (ot,), (w11,) = attn_fwd(qt, kt, vt, sink_rows, "attn_fwd", ride=gather_chips([sh11]))
    w11 = own_slot(w11, sh11).reshape(N_PARTS, 3, FF_PART, D_MODEL)
    o = _tokens_major(ot)
    x5, h12 = mm_res(o, w_o, x4, "attn_out", bias=a['attn_b_o'][0], norm_ws=[nw[1, 2]])
    x6, gu11 = ffn_fwd(h12, x5, w11, [], "ffn_fwd_11")

    loss_v, dx6, d_final = loss_head(x6, a['final_norm_w'], a['loss_target'][0], "loss_head")
    loss = lax.psum(loss_v[0, 0], ("x", "y", "c"))
    g = {'final_norm_w': d_final[0]}

    def pre_reduce(grads, sib, tag):
        out = []
        for idx, (gr, sb) in enumerate(zip(grads, sib)):
            t = add_pair(gr.reshape(2, -1, gr.shape[-1]), _as2d(sb), c_idx, "rs_add_%s_%d" % (tag, idx))
            out.append(t.reshape(gr.shape[1:]))
        return out

    def chip_sum(landed, parts, tag):
        out = []
        for idx, (q, own) in enumerate(zip(landed, parts)):
            t = sum_chips(q.reshape(N_CHIPS, -1, q.shape[-1]), own.reshape(N_CHIPS, -1, q.shape[-1]), chip_idx,
                          "rs_sum_%s_%d" % (tag, idx))
            out.append(t.reshape(q.shape[1:]))
        return out

    dnw = [[None] * 3 for _ in range(2)]
    sums = {}

    dx5, dnw12, dw11 = ffn_bwd(dx6, h12, x5, nw[1, 2], gu11, w11, "ffn_bwd_11")
    dnw[1][2] = dnw12[0]
    g11 = [dw11.reshape(N_CORES, N_CHIPS, 3, FF_PART, D_MODEL)]
    g['attn_b_o'] = colsum(dx5, "attn_dbo")
    d_wo, sib11 = mm_tn(o, dx5, "attn_dwo", ride=swap_cores(g11, True))
    t11 = pre_reduce(g11, sib11, "11")
    do = mm_nt(dx5, w_o, "attn_do", out_dtype=BF16)
    (dqt, dkt, dvt, dsink), land11 = attn_bwd(qt, kt, vt, sink_rows, _heads_major(do, N_Q_HEADS), "attn_bwd",
                                             ride=scatter_chips(t11))
    sums['11'] = chip_sum(land11, t11, "11")
    g['attn_sinks'] = jnp.sum(dsink[:, :, :Q_PER_KV, 0], axis=1).reshape(N_Q_HEADS)
    dq_pre = rope_apply(_tokens_major(dqt), cos, sin, "rope_dq", inverse=True, scale=scale, out_dtype=F32)
    g['attn_b_q'] = colsum(dq_pre, "attn_dbq")
    d_wq = mm_tn(h11, dq_pre, "attn_dwq")
    g_attn = [jnp.stack([d_wq.reshape(N_CHIPS, 256, D_MODEL), d_wo.reshape(N_CHIPS, 256, D_MODEL)])]
    (dx4, dnw[1][1]), sib_attn = mm_rms_bwd([(dq_pre, 0, w_q, 0, D_MODEL, "nt")], dx5, x4, nw[1, 1], "attn_bwd_dh",
                                            ride=swap_cores(g_attn, True))
    t_attn = pre_reduce(g_attn, sib_attn, "attn")
    (dx3, dnw10, dw10), land_attn = ffn_bwd(dx4, h10, x3, nw[1, 0], gu10, w10, "ffn_bwd_10",
                                           ride=scatter_chips(t_attn))
    dnw[1][0] = dnw10[0]
    sums['attn'] = chip_sum(land_attn, t_attn, "attn")
    g10 = [dw10.reshape(N_CORES, N_CHIPS, 3, FF_PART, D_MODEL)]
    dk_pre = rope_apply(_tokens_major(dkt), cos, sin, "rope_dk", inverse=True, out_dtype=F32)
    dv = _tokens_major(dvt)
    g['b_k'] = colsum(dk_pre, "kv_dbk")
    g['b_v'] = colsum(dv, "kv_dbv")
    d_wk, sib10 = mm_tn(hkv, dk_pre, "kv_dwk", ride=swap_cores(g10, True))
    t10 = pre_reduce(g10, sib10, "10")
    d_wv = mm_tn(hkv, dv, "kv_dwv")
    g_kv = [jnp.stack([d_wk.reshape(N_CHIPS, 256, KV_DIM), d_wv.reshape(N_CHIPS, 256, KV_DIM)])]
    (dx3, g['kv_norm_w']), sib_kv = mm_rms_bwd(
        [(dk_pre, 0, w_k, 0, KV_DIM, "nt"), (dv, 0, w_v, 0, KV_DIM, "nt")], dx3, x3, a['kv_norm_w'], "kv_bwd_dh",
        ride=swap_cores(g_kv, True))
    t_kv = pre_reduce(g_kv, sib_kv, "kv")
    (dx2, dnw02, dw01), land_mid = ffn_bwd(dx3, h02, x2, nw[0, 2], gu01, w01, "ffn_bwd_01",
                                          ride=scatter_chips(t10 + t_kv))
    dnw[0][2] = dnw02[0]
    sums['10'] = chip_sum(land_mid[:1], t10, "10")
    sums['kv'] = chip_sum(land_mid[1:], t_kv, "kv")
    g01 = [dw01.reshape(N_CORES, N_CHIPS, 3, FF_PART, D_MODEL)]
    d_wout, sib01 = mm_tn(yn, dx2, "ssm_dwout", ride=swap_cores(g01, True))
    t01 = pre_reduce(g01, sib01, "01")
    dyn = mm_nt(dx2, w_out, "ssm_dyn")
    (dxs, db_, dc_, dz, ddt, d_ssm_nw, d_bias, d_a, d_d), land01 = ssd_bwd(
        dyn, act, zz, y_pre, states, dtp, bias_p, a_p, d_p, ssm_nw, "ssd_bwd", ride=scatter_chips(t01))
    sums['01'] = chip_sum(land01, t01, "01")
    g['ssm_norm_w'] = d_ssm_nw[:SSM_GROUPS].reshape(D_INNER)
    g['ssm_dt_bias'] = d_bias[0, :SSM_HEADS]
    g['ssm_a_log'] = d_a[0, :SSM_HEADS] * a_p[0, :SSM_HEADS]
    g['ssm_d'] = d_d[0, :SSM_HEADS]
    dxbc, g['ssm_conv_w'], g['ssm_conv_b'] = conv_bwd(dxs, db_, dc_, xbc, conv_w, conv_b, "ssm_conv_bwd")
    d_win = mm_tn(dz, h01, "ssm_dwz", rows=IN_PROJ_DIM)
    d_win = mm_tn(dxbc, h01, "ssm_dwxbc", into=d_win, rows=IN_PROJ_DIM, row0=D_INNER)
    d_win = mm_tn(ddt, h01, "ssm_dwdt", into=d_win, rows=IN_PROJ_DIM, row0=D_INNER + CONV_DIM, m_valid=SSM_HEADS)
    d_win = jnp.pad(d_win.reshape(N_CHIPS, IN_SHARD, D_MODEL), ((0, 0), (0, IN_SHARD_PAD - IN_SHARD), (0, 0)))
    g_ssm = [_parts_first(d_win.reshape(-1, D_MODEL), IN_SHARD_PAD // 2), _parts_first(d_wout, 256)]
    kb = 1024
    terms = ([(dz, j, w_in_t, j, kb, "nn") for j in range(D_INNER // kb)]
             + [(dxbc, j, w_in_t, D_INNER // kb + j, kb, "nn") for j in range(CONV_DIM // kb)]
             + [(ddt, 0, w_dt_t, 0, LANES, "nn")])
    (dx1, dnw[0][1]), sib_ssm = mm_rms_bwd(terms, dx2, x1, nw[0, 1], "ssm_bwd_dh", ride=swap_cores(g_ssm, True))
    t_ssm = pre_reduce(g_ssm, sib_ssm, "ssm")
    (grad_x, dnw00, dw00), land_ssm = ffn_bwd(dx1, h00, x0, nw[0, 0], gu00, w00, "ffn_bwd_00",
                                             ride=scatter_chips(t_ssm))
    dnw[0][0] = dnw00[0]
    sums['ssm'] = chip_sum(land_ssm, t_ssm, "ssm")
    g00 = [dw00.reshape(N_CORES, N_CHIPS, 3, FF_PART, D_MODEL)]
    sib00 = run_exchange(swap_cores(g00, True), "rs_swap_00")
    t00 = pre_reduce(g00, sib00, "00")
    land00 = run_exchange(scatter_chips(t00), "rs_scatter_00")
    sums['00'] = chip_sum(land00, t00, "00")

    order = ['11', 'attn', '10', 'kv', '01', 'ssm', '00']
    mine = [t for key in order for t in sums[key]]
    theirs = run_exchange(swap_cores(mine, False), "rs_trade_parts")
    both = [(jnp.where(south, m_, t_), jnp.where(south, t_, m_)) for m_, t_ in zip(mine, theirs)]
    it = iter(both)
    full = {key: [next(it) for _ in sums[key]] for key in order}

    gw = {}
    ffn_g = {}
    for key, (l, i) in (('00', (0, 0)), ('01', (0, 1)), ('10', (1, 0)), ('11', (1, 1))):
        lo, hi = full[key][0]
        ffn_g[(l, i)] = jnp.concatenate([lo, hi], axis=1)
    for t, n in enumerate(('ffn_w_gate', 'ffn_w_up', 'ffn_w_down')):
        gw[n] = jnp.stack([jnp.stack([ffn_g[(l, i)][t] for i in range(2)]) for l in range(2)])
    lo, hi = full['attn'][0]
    gw['attn_w_q'], gw['attn_w_o'] = lo[None], hi[None]
    lo, hi = full['kv'][0]
    gw['w_k'], gw['w_v'] = lo, hi
    lo, hi = full['ssm'][0]
    gw['ssm_w_in'] = jnp.concatenate([lo, hi], axis=0)[:IN_SHARD][None]
    lo, hi = full['ssm'][1]
    gw['ssm_w_out'] = jnp.concatenate([lo, hi], axis=0)[None]

    g['norm_w'] = jnp.stack([jnp.stack(r) for r in dnw])
    red = all_reduce_small(_pack_rows([g[n] for n in SMALL]), "reduce_vectors")
    for n, t in zip(SMALL, _unpack_rows(red, [g[n].shape for n in SMALL])):
        if n in SMALL_SHARDED:
            ax = SMALL_SHARDED[n] - (a[n].ndim - t.ndim)
            width = a[n].shape[SMALL_SHARDED[n]]
            t = lax.dynamic_slice_in_dim(t, chip * width, width, axis=ax)
        gw[n] = t.reshape(a[n].shape)

    delta, new_m, new_v = {}, {}, {}
    for n in BIG:
        d, mo, vo = adamw(_as2d(a[n]), _as2d(gw[n]), _as2d(a['m_' + n]), _as2d(a['v_' + n]), "adamw_" + n)
        delta[n], new_m[n], new_v[n] = d.reshape(a[n].shape), mo.reshape(a[n].shape), vo.reshape(a[n].shape)
    shapes = [a[n].shape for n in SMALL]
    packed = [_pack_rows([src[n] for n in SMALL]) for src in
              (a, gw, {n: a['m_' + n] for n in SMALL}, {n: a['v_' + n] for n in SMALL})]
    outs = adamw(*packed, "adamw_vectors")
    for dst, buf in zip((delta, new_m, new_v), outs):
        for n, t in zip(SMALL, _unpack_rows(buf, shapes)):
            dst[n] = t
    for n in TRANSPOSED:
        for dst in (gw, delta, new_m, new_v):
            dst[n] = dst[n].swapaxes(-1, -2)

    return (loss, grad_x[None], *[gw[n] for n in WEIGHTS], *[delta[n] for n in WEIGHTS],
            *[new_m[n] for n in WEIGHTS], *[new_v[n] for n in WEIGHTS])
```

```python
import math

import jax
import jax.numpy as jnp
from jax import lax
from jax.experimental import pallas as pl
from jax.experimental.pallas import tpu as pltpu

F32 = jnp.float32
BF16 = jnp.bfloat16
HI = lax.Precision.HIGHEST

D_MODEL = 1024
D_INNER = 2048
SSM_HEADS = 32
SSM_GROUPS = 4
HEADS_PER_GROUP = SSM_HEADS // SSM_GROUPS
SSM_HEAD_DIM = 64
SSM_STATE = 128
GROUP_DIM = D_INNER // SSM_GROUPS
CONV_DIM = D_INNER + 2 * SSM_GROUPS * SSM_STATE
CONV_WIDTH = 4
CHUNK = 128
ATT_HEAD_DIM = 64
N_Q_HEADS = 16
N_KV_HEADS = 4
Q_PER_KV = N_Q_HEADS // N_KV_HEADS
KV_DIM = N_KV_HEADS * ATT_HEAD_DIM
WINDOW = 128
ROPE_THETA = 10000.0
D_FF = 2816
N_CHIPS = 4
N_CORES = 2
FF_SHARD = D_FF // N_CHIPS
FF_PART = FF_SHARD // N_CORES
N_PARTS = N_CHIPS * N_CORES
IN_PROJ_DIM = D_INNER + CONV_DIM + SSM_HEADS
IN_SHARD = IN_PROJ_DIM // N_CHIPS
IN_SHARD_PAD = 1312
EPS = 1e-5
NEG = -1e30
LANES = 128
VMEM_LIMIT = 56 * 1024 * 1024

ADAM_LR = 0.001
ADAM_B1 = 0.9
ADAM_B2 = 0.999
ADAM_EPS = 1e-08
ADAM_WD = 0.01
ADAM_STEP = 10

NN = ((1,), (0,))
NT = ((1,), (1,))
TN = ((0,), (0,))
MESH = pl.DeviceIdType.MESH
ANY = pl.BlockSpec(memory_space=pl.ANY)


def _dot(a, b, dims=NN, precision=None):
    return lax.dot_general(a, b, (dims, ((), ())), preferred_element_type=F32, precision=precision)


def _cp(n_grid):
    return pltpu.CompilerParams(dimension_semantics=("arbitrary",) * n_grid, vmem_limit_bytes=VMEM_LIMIT)


def _sigmoid(x):
    return 1.0 / (1.0 + jnp.exp(-x))


def _rms_fwd(xf, w):
    r = lax.rsqrt(jnp.mean(xf * xf, axis=-1, keepdims=True) + EPS)
    return xf * r * w


def _rms_bwd(dh, xf, w):
    r = lax.rsqrt(jnp.mean(xf * xf, axis=-1, keepdims=True) + EPS)
    xhat = xf * r
    dxhat = dh * w
    dx = r * (dxhat - xhat * jnp.mean(dxhat * xhat, axis=-1, keepdims=True))
    return dx, dh * xhat


def _row_tile(s, pref):
    return pref if s % pref == 0 else s


def _col_tile(n):
    for t in (1024, 768, 512, 256, 128):
        if n % t == 0:
            return t
    return n


def _sds(shape, dtype):
    return jax.ShapeDtypeStruct(tuple(shape), dtype)


class Exchange:
    def __init__(self, ins, out_shapes, sems, start, finish):
        self.ins, self.out_shapes, self.sems, self.start, self.finish = ins, out_shapes, sems, start, finish


def _place():
    x, y, c = lax.axis_index("x"), lax.axis_index("y"), lax.axis_index("c")
    others = [(1 - x, y), (x, 1 - y), (1 - x, 1 - y)]
    return x, y, c, 2 * x + y, others


def _rc(src, dst, send_sem, recv_sem, dev):
    return pltpu.make_async_remote_copy(src_ref=src, dst_ref=dst, send_sem=send_sem, recv_sem=recv_sem,
                                        device_id=dev, device_id_type=MESH)


def gather_chips(arrs):
    n = len(arrs)

    def copies(ins, outs, sems):
        send, recv = sems
        x, y, c, k, others = _place()
        ici, land, fwd, fland = [], [], [], []
        for a in range(n):
            for j, (px, py) in enumerate(others):
                ici.append(_rc(ins[a].at[c], outs[a].at[c, k], send.at[a, j], recv.at[a, j], (px, py, c)))
                blk = outs[a].at[c, 2 * px + py]
                land.append(_rc(blk, blk, send.at[a, j], recv.at[a, j], (px, py, c)))
                fwd.append(_rc(blk, blk, send.at[a, 3 + j], recv.at[a, 3 + j], (x, y, 1 - c)))
                blk2 = outs[a].at[1 - c, 2 * px + py]
                fland.append(_rc(blk2, blk2, send.at[a, 3 + j], recv.at[a, 3 + j], (x, y, 1 - c)))
        return ici, land, fwd, fland

    def start(ins, outs, sems):
        for cp in copies(ins, outs, sems)[0]:
            cp.start()

    def finish(ins, outs, sems):
        ici, land, fwd, fland = copies(ins, outs, sems)
        for arrived, onward in zip(land, fwd):
            arrived.wait_recv()
            onward.start()
        for arrived in fland:
            arrived.wait_recv()
        for cp in ici + fwd:
            cp.wait_send()

    return Exchange(list(arrs), [_sds((2, N_CHIPS) + a.shape[1:], a.dtype) for a in arrs],
                    [pltpu.SemaphoreType.DMA((n, 6)), pltpu.SemaphoreType.DMA((n, 6))], start, finish)


def scatter_chips(arrs):
    n = len(arrs)

    def copies(ins, outs, sems):
        send, recv = sems
        x, y, c, k, others = _place()
        out, land = [], []
        for a in range(n):
            for j, (px, py) in enumerate(others):
                out.append(_rc(ins[a].at[2 * px + py], outs[a].at[k], send.at[a, j], recv.at[a, j], (px, py, c)))
                blk = outs[a].at[2 * px + py]
                land.append(_rc(blk, blk, send.at[a, j], recv.at[a, j], (px, py, c)))
        return out, land

    def start(ins, outs, sems):
        for cp in copies(ins, outs, sems)[0]:
            cp.start()

    def finish(ins, outs, sems):
        out, land = copies(ins, outs, sems)
        for arrived in land:
            arrived.wait_recv()
        for cp in out:
            cp.wait_send()

    return Exchange(list(arrs), [_sds(a.shape, a.dtype) for a in arrs],
                    [pltpu.SemaphoreType.DMA((n, 3)), pltpu.SemaphoreType.DMA((n, 3))], start, finish)


def swap_cores(arrs, pick_other):
    n = len(arrs)

    def copies(ins, outs, sems):
        send, recv = sems
        x, y, c, _, _ = _place()
        return [_rc(ins[a].at[1 - c] if pick_other else ins[a], outs[a], send.at[a], recv.at[a], (x, y, 1 - c))
                for a in range(n)]

    def start(ins, outs, sems):
        for cp in copies(ins, outs, sems):
            cp.start()

    def finish(ins, outs, sems):
        for cp in copies(ins, outs, sems):
            cp.wait()

    shapes = [_sds(a.shape[1:] if pick_other else a.shape, a.dtype) for a in arrs]
    return Exchange(list(arrs), shapes, [pltpu.SemaphoreType.DMA((n,)), pltpu.SemaphoreType.DMA((n,))],
                    start, finish)


def join(*parts):
    parts = [p for p in parts if p is not None]
    if not parts:
        return None

    def split(refs, counts):
        out, pos = [], 0
        for cnt in counts:
            out.append(refs[pos:pos + cnt])
            pos += cnt
        return out

    n_in = [len(p.ins) for p in parts]
    n_out = [len(p.out_shapes) for p in parts]
    n_sem = [len(p.sems) for p in parts]

    def run(which):
        def go(ins, outs, sems):
            for p, i, o, s in zip(parts, split(ins, n_in), split(outs, n_out), split(sems, n_sem)):
                getattr(p, which)(i, o, s)
        return go

    return Exchange([a for p in parts for a in p.ins], [s for p in parts for s in p.out_shapes],
                    [s for p in parts for s in p.sems], run("start"), run("finish"))


def _pcall(body, *, out_shape, grid, in_specs, out_specs, args, name, scratch_shapes=(), ride=None, aliases=None):
    out_shape, out_specs, in_specs = tuple(out_shape), tuple(out_specs), list(in_specs)
    aliases = aliases or {}
    if ride is None:
        return pl.pallas_call(body, out_shape=out_shape, grid=grid, in_specs=in_specs, out_specs=out_specs,
                              scratch_shapes=list(scratch_shapes), input_output_aliases=aliases, name=name,
                              compiler_params=_cp(len(grid)))(*args)
    n_in, n_out, n_sc = len(args), len(out_shape), len(scratch_shapes)
    n_xi, n_xo = len(ride.ins), len(ride.out_shapes)

    def wrapped(*refs):
        pos = [0]

        def take(cnt):
            got = refs[pos[0]:pos[0] + cnt]
            pos[0] += cnt
            return got

        c_in, x_in, c_out, x_out, c_sc = take(n_in), take(n_xi), take(n_out), take(n_xo), take(n_sc)
        sems = refs[pos[0]:]
        first, last = True, True
        for d, size in enumerate(grid):
            first = jnp.logical_and(first, pl.program_id(d) == 0)
            last = jnp.logical_and(last, pl.program_id(d) == size - 1)

        @pl.when(first)
        def _():
            ride.start(x_in, x_out, sems)

        body(*c_in, *c_out, *c_sc)

        @pl.when(last)
        def _():
            ride.finish(x_in, x_out, sems)

    res = pl.pallas_call(
        wrapped, out_shape=out_shape + tuple(ride.out_shapes), grid=grid,
        in_specs=in_specs + [ANY] * n_xi, out_specs=out_specs + (ANY,) * n_xo,
        scratch_shapes=list(scratch_shapes) + list(ride.sems), input_output_aliases=aliases, name=name,
        compiler_params=_cp(len(grid)))(*args, *ride.ins)
    return res[:n_out], res[n_out:]


def run_exchange(ex, name):
    n_xi, n_xo = len(ex.ins), len(ex.out_shapes)

    def body(*refs):
        ins, outs, sems = refs[:n_xi], refs[n_xi:n_xi + n_xo], refs[n_xi + n_xo:]
        ex.start(ins, outs, sems)
        ex.finish(ins, outs, sems)

    return pl.pallas_call(body, out_shape=tuple(ex.out_shapes), in_specs=[ANY] * n_xi, out_specs=(ANY,) * n_xo,
                          scratch_shapes=list(ex.sems), name=name)(*ex.ins)


def all_reduce_small(buf, name):
    r = buf.shape[0]
    n_dev = 8

    def body(in_ref, o_ref, land, send_sems, recv_sems):
        x, y, c, _, _ = _place()
        me = 4 * x + 2 * y + c
        land[me] = in_ref[...]
        sends = []
        for d in range(1, n_dev):
            peer = (x ^ (d >> 2), y ^ ((d >> 1) & 1), c ^ (d & 1))
            cp = _rc(in_ref, land.at[me], send_sems.at[d], recv_sems.at[d], peer)
            cp.start()
            sends.append(cp)
        for d in range(1, n_dev):
            blk = land.at[me ^ d]
            _rc(blk, blk, send_sems.at[d], recv_sems.at[d], (x, y, c)).wait_recv()
        for cp in sends:
            cp.wait_send()
        tot = land[0]
        for d in range(1, n_dev):
            tot = tot + land[d]
        o_ref[...] = tot

    vm = pl.BlockSpec(memory_space=pltpu.VMEM)
    return pl.pallas_call(
        body, out_shape=_sds(buf.shape, F32), in_specs=[vm], out_specs=vm,
        scratch_shapes=[pltpu.VMEM((n_dev, r, LANES), F32), pltpu.SemaphoreType.DMA((n_dev,)),
                        pltpu.SemaphoreType.DMA((n_dev,))],
        name=name)(buf)


def rmsnorm_fwd(x, w, name):
    s, d = x.shape
    tm = _row_tile(s, 512)

    def body(x_ref, w_ref, o_ref):
        o_ref[...] = _rms_fwd(x_ref[...], w_ref[...]).astype(BF16)

    return _pcall(body, out_shape=[_sds((s, d), BF16)], grid=(s // tm,),
                  in_specs=[pl.BlockSpec((tm, d), lambda i: (i, 0)), pl.BlockSpec((1, d), lambda i: (0, 0))],
                  out_specs=[pl.BlockSpec((tm, d), lambda i: (i, 0))], args=[x, w.reshape(1, d)], name=name)[0]


def _ffn_w_spec(chip_of, single=False):
    mode = dict(pipeline_mode=pl.Buffered(1)) if single else {}
    return pl.BlockSpec((N_CORES, 1, FF_PART, D_MODEL), lambda *ids: (0, chip_of(*ids), 0, 0), **mode)


def ffn_fwd(h, x, wg, wu, wd, norm_ws, name, ride=None):
    s, d = h.shape
    n_norm = len(norm_ws)
    tm = _row_tile(s, 512)

    def body(*refs):
        h_ref, x_ref, wg_ref, wu_ref, wd_ref = refs[:5]
        nw_refs = refs[5:5 + n_norm]
        o_ref = refs[5 + n_norm]
        h_refs = refs[6 + n_norm:6 + 2 * n_norm]
        gu_ref, acc = refs[6 + 2 * n_norm], refs[7 + 2 * n_norm]
        k = pl.program_id(1)
        hb = h_ref[...]
        g = _dot(hb, wg_ref[...].reshape(FF_SHARD, d), NT)
        u = _dot(hb, wu_ref[...].reshape(FF_SHARD, d), NT)
        gu_ref[0, 0] = g.astype(BF16)
        gu_ref[0, 1] = u.astype(BF16)
        part = _dot((g * _sigmoid(g) * u).astype(BF16), wd_ref[...].reshape(FF_SHARD, d))

        @pl.when(k == 0)
        def _():
            acc[...] = part

        @pl.when(k > 0)
        def _():
            acc[...] += part

        @pl.when(k == N_CHIPS - 1)
        def _():
            xn = x_ref[...] + 0.5 * acc[...]
            o_ref[...] = xn
            for nw_ref, hn_ref in zip(nw_refs, h_refs):
                hn_ref[...] = _rms_fwd(xn, nw_ref[...]).astype(BF16)

    row = pl.BlockSpec((tm, d), lambda i, k: (i, 0))
    vec = pl.BlockSpec((1, d), lambda i, k: (0, 0))
    wsp = _ffn_w_spec(lambda i, k: k)
    return _pcall(
        body, out_shape=[_sds((s, d), F32)] + [_sds((s, d), BF16)] * n_norm + [_sds((N_CHIPS, 2, s, FF_SHARD), BF16)],
        grid=(s // tm, N_CHIPS),
        in_specs=[row, row, wsp, wsp, wsp] + [vec] * n_norm,
        out_specs=[row] * (1 + n_norm) + [pl.BlockSpec((1, 2, tm, FF_SHARD), lambda i, k: (k, 0, i, 0))],
        scratch_shapes=[pltpu.VMEM((tm, d), F32)],
        args=[h, x, wg, wu, wd] + [nw.reshape(1, d) for nw in norm_ws], name=name, ride=ride)


def ffn_bwd(dxn, h, x_in, nw, gu, wg, wu, wd, name, ride=None):
    s, d = h.shape
    tm = _row_tile(s, 512)
    ni = s // tm
    last_e = N_CHIPS - 1

    def body(dxn_ref, h_ref, x_ref, nw_ref, gu_ref, wg_ref, wu_ref, wd_ref,
             dx_ref, dnw_ref, dwg_ref, dwu_ref, dwd_ref, dh, wacc):
        e = pl.program_id(0)
        i = pl.program_id(1)
        dxb = dxn_ref[...].astype(BF16)
        hb = h_ref[...]
        g = gu_ref[0, 0].astype(F32)
        u = gu_ref[0, 1].astype(F32)
        sg = _sigmoid(g)
        silu = g * sg
        da = 0.5 * _dot(dxb, wd_ref[...].reshape(FF_SHARD, d), NT)
        dg = (da * u * (sg * (1.0 + g * (1.0 - sg)))).astype(BF16)
        du = (da * silu).astype(BF16)
        grads = (_dot(dg, hb, TN), _dot(du, hb, TN), 0.5 * _dot((silu * u).astype(BF16), dxb, TN))

        @pl.when(i == 0)
        def _():
            for t, gt in enumerate(grads):
                wacc[t] = gt

        @pl.when(i > 0)
        def _():
            for t, gt in enumerate(grads):
                wacc[t] += gt

        @pl.when(i == ni - 1)
        def _():
            for t, dw_ref in enumerate((dwg_ref, dwu_ref, dwd_ref)):
                dw_ref[...] = wacc[t].astype(BF16).reshape(N_CORES, 1, FF_PART, d)

        rows = pl.ds(pl.multiple_of(i * tm, tm), tm)
        dh_part = _dot(dg, wg_ref[...].reshape(FF_SHARD, d)) + _dot(du, wu_ref[...].reshape(FF_SHARD, d))

        @pl.when(e == 0)
        def _():
            dh[rows, :] = dh_part

        @pl.when(e > 0)
        def _():
            dh[rows, :] += dh_part

        @pl.when(e == last_e)
        def _():
            dx, dnw = _rms_bwd(dh[rows, :], x_ref[...], nw_ref[...])
            dx_ref[...] = dxn_ref[...] + dx
            col = jnp.sum(dnw, axis=0, keepdims=True)

            @pl.when(i == 0)
            def _():
                dnw_ref[...] = col

            @pl.when(i > 0)
            def _():
                dnw_ref[...] += col

    row = pl.BlockSpec((tm, d), lambda e, i: (i, 0))
    late = pl.BlockSpec((tm, d), lambda e, i: (jnp.where(e == last_e, i, 0), 0))
    vec = pl.BlockSpec((1, d), lambda e, i: (0, 0))
    wsp = _ffn_w_spec(lambda e, i: e, single=True)
    dwsp = _ffn_w_spec(lambda e, i: e, single=True)
    dw = _sds((N_CORES, N_CHIPS, FF_PART, d), BF16)
    return _pcall(
        body, out_shape=[_sds((s, d), F32), _sds((1, d), F32), dw, dw, dw],
        grid=(N_CHIPS, ni),
        in_specs=[row, row, late, vec, pl.BlockSpec((1, 2, tm, FF_SHARD), lambda e, i: (e, 0, i, 0)), wsp, wsp, wsp],
        out_specs=[late, vec, dwsp, dwsp, dwsp],
        scratch_shapes=[pltpu.VMEM((s, d), F32), pltpu.VMEM((3, FF_SHARD, d), F32)],
        args=[dxn, h, x_in, nw.reshape(1, d), gu, wg, wu, wd], name=name, ride=ride)


def mm_res(a, w, x, name, bias=None, norm_ws=(), ride=None):
    s, k = a.shape
    n = w.shape[1]
    tm = _row_tile(s, 256)
    has_bias = bias is not None
    n_norm = len(norm_ws)

    def body(*refs):
        a_ref, w_ref, x_ref = refs[:3]
        pos = 3
        t = _dot(a_ref[...], w_ref[...])
        if has_bias:
            t = t + refs[pos][...]
            pos += 1
        nw_refs = refs[pos:pos + n_norm]
        o_ref = refs[pos + n_norm]
        h_refs = refs[pos + n_norm + 1:]
        xn = x_ref[...] + t
        o_ref[...] = xn
        for nw_ref, h_ref in zip(nw_refs, h_refs):
            h_ref[...] = _rms_fwd(xn, nw_ref[...]).astype(BF16)

    row = pl.BlockSpec((tm, n), lambda i: (i, 0))
    vec = pl.BlockSpec((1, n), lambda i: (0, 0))
    in_specs = [pl.BlockSpec((tm, k), lambda i: (i, 0)), pl.BlockSpec((k, n), lambda i: (0, 0)), row]
    args = [a, w, x]
    if has_bias:
        in_specs.append(vec)
        args.append(bias.reshape(1, n))
    for nw in norm_ws:
        in_specs.append(vec)
        args.append(nw.reshape(1, n))
    return _pcall(body, out_shape=[_sds((s, n), F32)] + [_sds((s, n), BF16)] * n_norm, grid=(s // tm,),
                  in_specs=in_specs, out_specs=[row] * (1 + n_norm), args=args, name=name, ride=ride)


def mm_nn(a, w, name, bias=None, out_dtype=F32):
    s, k = a.shape
    n = w.shape[1]
    tm = _row_tile(s, 512)
    tn = _col_tile(n)
    has_bias = bias is not None

    def body(*refs):
        a_ref, w_ref = refs[:2]
        o_ref = refs[-1]
        t = _dot(a_ref[...], w_ref[...])
        if has_bias:
            t = t + refs[2][...]
        o_ref[...] = t.astype(out_dtype)

    in_specs = [pl.BlockSpec((tm, k), lambda j, i: (i, 0)), pl.BlockSpec((k, tn), lambda j, i: (0, j))]
    args = [a, w]
    if has_bias:
        in_specs.append(pl.BlockSpec((1, tn), lambda j, i: (0, j)))
        args.append(bias.reshape(1, n))
    return _pcall(body, out_shape=[_sds((s, n), out_dtype)], grid=(n // tn, s // tm), in_specs=in_specs,
                  out_specs=[pl.BlockSpec((tm, tn), lambda j, i: (i, j))], args=args, name=name)[0]


def mm_nt(a, w, name, n=None, row0=0, out_dtype=F32, ride=None):
    s, k = a.shape
    n = w.shape[0] if n is None else n
    tm = _row_tile(s, 512)
    tn = _col_tile(n)
    base = row0 // tn
    assert row0 % tn == 0

    def body(a_ref, w_ref, o_ref):
        o_ref[...] = _dot(a_ref[...].astype(BF16), w_ref[...], NT).astype(out_dtype)

    res = _pcall(body, out_shape=[_sds((s, n), out_dtype)], grid=(n // tn, s // tm),
                 in_specs=[pl.BlockSpec((tm, k), lambda j, i: (i, 0)), pl.BlockSpec((tn, k), lambda j, i: (base + j, 0))],
                 out_specs=[pl.BlockSpec((tm, tn), lambda j, i: (i, j))], args=[a, w], name=name, ride=ride)
    return res[0] if ride is None else (res[0][0], res[1])


def mm_tn(a, b, name, into=None, rows=None, row0=0, m_valid=None, ride=None):
    s, m = a.shape
    n = b.shape[1]
    mv = m if m_valid is None else m_valid
    tm = _col_tile(m) if m_valid is None else mv
    tn = n if n <= 1024 else _col_tile(n)
    rows = mv if rows is None else rows
    assert row0 % tm == 0 and (m_valid is None or m == LANES)
    base = row0 // tm
    ta = m if m_valid is not None else tm

    def body(*refs):
        a_ref, b_ref, o_ref = refs[0], refs[1], refs[-1]
        t = _dot(a_ref[...].astype(BF16), b_ref[...].astype(BF16), TN)
        o_ref[...] = t[:tm].astype(BF16)

    in_specs = [pl.BlockSpec((s, ta), lambda i, j: (0, i)), pl.BlockSpec((s, tn), lambda i, j: (0, j))]
    args = [a, b]
    aliases = None
    if into is not None:
        in_specs.append(ANY)
        args.append(into)
        aliases = {2: 0}
    res = _pcall(body, out_shape=[_sds((rows, n), BF16)], grid=(mv // tm, n // tn), in_specs=in_specs,
                 out_specs=[pl.BlockSpec((tm, tn), lambda i, j: (base + i, j))], args=args, name=name, ride=ride,
                 aliases=aliases)
    return res[0] if ride is None else (res[0][0], res[1])


def mm_rms_bwd(terms, dxn, x, nw, name, ride=None):
    s, n = x.shape
    nt_ = len(terms)
    tm = _row_tile(s, 256)
    forms = [t[5] for t in terms]

    def body(*refs):
        dxn_ref, x_ref, nw_ref, dx_ref, dnw_ref = refs[2 * nt_:]
        i = pl.program_id(0)
        dh = None
        for t in range(nt_):
            part = _dot(refs[2 * t][...].astype(BF16), refs[2 * t + 1][...], NN if forms[t] == "nn" else NT)
            dh = part if dh is None else dh + part
        dx, dnw = _rms_bwd(dh, x_ref[...], nw_ref[...])
        dx_ref[...] = dxn_ref[...] + dx
        col = jnp.sum(dnw, axis=0, keepdims=True)

        @pl.when(i == 0)
        def _():
            dnw_ref[...] = col

        @pl.when(i > 0)
        def _():
            dnw_ref[...] += col

    in_specs, args = [], []
    for a, cb, w, rb, kb, form in terms:
        in_specs.append(pl.BlockSpec((tm, kb), lambda i, cb=cb: (i, cb)))
        if form == "nn":
            in_specs.append(pl.BlockSpec((kb, n), lambda i, rb=rb: (rb, 0)))
        else:
            in_specs.append(pl.BlockSpec((n, kb), lambda i, rb=rb: (0, rb)))
        args += [a, w]
    row = pl.BlockSpec((tm, n), lambda i: (i, 0))
    vec = pl.BlockSpec((1, n), lambda i: (0, 0))
    res = _pcall(body, out_shape=[_sds((s, n), F32), _sds((1, n), F32)], grid=(s // tm,),
                 in_specs=in_specs + [row, row, vec], out_specs=[row, vec],
                 args=args + [dxn, x, nw.reshape(1, n)], name=name, ride=ride)
    outs = res if ride is None else res[0]
    out = (outs[0], outs[1][0])
    return out if ride is None else (out, res[1])


def colsum(a, name):
    s, n = a.shape
    tm = _row_tile(s, 512)

    def body(a_ref, o_ref):
        col = jnp.sum(a_ref[...].astype(F32), axis=0, keepdims=True)

        @pl.when(pl.program_id(0) == 0)
        def _():
            o_ref[...] = col

        @pl.when(pl.program_id(0) > 0)
        def _():
            o_ref[...] += col

    return _pcall(body, out_shape=[_sds((1, n), F32)], grid=(s // tm,),
                  in_specs=[pl.BlockSpec((tm, n), lambda i: (i, 0))],
                  out_specs=[pl.BlockSpec((1, n), lambda i: (0, 0))], args=[a], name=name)[0][0]


def rope_tables(s):
    pos = jnp.arange(s, dtype=F32)
    inv = 1.0 / (ROPE_THETA ** (jnp.arange(0, ATT_HEAD_DIM, 2, dtype=F32) / ATT_HEAD_DIM))
    ang = pos[:, None] * inv[None, :]
    cos = jnp.tile(jnp.cos(ang), (1, 2 * LANES // ATT_HEAD_DIM))
    sin = jnp.tile(jnp.sin(ang), (1, 2 * LANES // ATT_HEAD_DIM))
    return cos, sin


def rope_apply(t, cos, sin, name, inverse=False, scale=1.0, out_dtype=BF16):
    s, n = t.shape
    tm = _row_tile(s, 512)
    half = ATT_HEAD_DIM // 2
    reps = n // LANES

    def body(t_ref, c_ref, s_ref, o_ref):
        tf = t_ref[...].astype(F32)
        c = jnp.tile(c_ref[...], (1, reps))
        sn = jnp.tile(s_ref[...], (1, reps))
        lane = lax.broadcasted_iota(jnp.int32, tf.shape, 1)
        first = (lane & (ATT_HEAD_DIM - 1)) < half
        rot = jnp.where(first, -pltpu.roll(tf, n - half, 1), pltpu.roll(tf, half, 1))
        sign = -1.0 if inverse else 1.0
        o_ref[...] = (scale * (tf * c + sign * rot * sn)).astype(out_dtype)

    tab = pl.BlockSpec((tm, LANES), lambda i: (i, 0))
    return _pcall(body, out_shape=[_sds((s, n), out_dtype)], grid=(s // tm,),
                  in_specs=[pl.BlockSpec((tm, n), lambda i: (i, 0)), tab, tab],
                  out_specs=[pl.BlockSpec((tm, n), lambda i: (i, 0))], args=[t, cos, sin], name=name)[0]


CONV_TILE = 256


def _shift_down(u, k):
    if k == 0:
        return u
    row = lax.broadcasted_iota(jnp.int32, u.shape, 0)
    return jnp.where(row >= k, pltpu.roll(u, k, 0), 0.0)


def _shift_up(u, k):
    if k == 0:
        return u
    s = u.shape[0]
    row = lax.broadcasted_iota(jnp.int32, u.shape, 0)
    return jnp.where(row < s - k, pltpu.roll(u, s - k, 0), 0.0)


def _conv_pre(u, w_ref, b_ref):
    pre = b_ref[...] + w_ref[CONV_WIDTH - 1:CONV_WIDTH, :] * u
    for k in range(CONV_WIDTH - 1):
        pre += w_ref[k:k + 1, :] * _shift_down(u, CONV_WIDTH - 1 - k)
    return pre


def conv_fwd(u, w, b, name, ride=None):
    s, c = u.shape

    def body(u_ref, w_ref, b_ref, o_ref):
        pre = _conv_pre(u_ref[...], w_ref, b_ref)
        o_ref[...] = pre * _sigmoid(pre)

    col = pl.BlockSpec((s, CONV_TILE), lambda j: (0, j))
    res = _pcall(body, out_shape=[_sds((s, c), F32)], grid=(c // CONV_TILE,),
                 in_specs=[col, pl.BlockSpec((CONV_WIDTH, CONV_TILE), lambda j: (0, j)),
                           pl.BlockSpec((1, CONV_TILE), lambda j: (0, j))],
                 out_specs=[col], args=[u, w, b.reshape(1, c)], name=name, ride=ride)
    return res[0] if ride is None else (res[0][0], res[1])


def conv_bwd(dxs, db_, dc_, u, w, b, name):
    s, c = u.shape
    n_x = dxs.shape[1] // CONV_TILE
    n_b = db_.shape[1] // CONV_TILE

    def body(dx_ref, dbb_ref, dcc_ref, u_ref, w_ref, b_ref, du_ref, dw_ref, dbias_ref):
        j = pl.program_id(0)
        dact = jnp.where(j < n_x, dx_ref[...], jnp.where(j < n_x + n_b, dbb_ref[...], dcc_ref[...]))
        uf = u_ref[...]
        pre = _conv_pre(uf, w_ref, b_ref)
        sg = _sigmoid(pre)
        dpre = dact * (sg * (1.0 + pre * (1.0 - sg)))
        du = w_ref[CONV_WIDTH - 1:CONV_WIDTH, :] * dpre
        for k in range(CONV_WIDTH - 1):
            du += w_ref[k:k + 1, :] * _shift_up(dpre, CONV_WIDTH - 1 - k)
        du_ref[...] = du
        dbias_ref[...] = jnp.sum(dpre, axis=0, keepdims=True)
        for k in range(CONV_WIDTH):
            dw_ref[k:k + 1, :] = jnp.sum(dpre * _shift_down(uf, CONV_WIDTH - 1 - k), axis=0, keepdims=True)

    col = pl.BlockSpec((s, CONV_TILE), lambda j: (0, j))
    wsp = pl.BlockSpec((CONV_WIDTH, CONV_TILE), lambda j: (0, j))
    bsp = pl.BlockSpec((1, CONV_TILE), lambda j: (0, j))
    du, dw, db = _pcall(
        body, out_shape=[_sds((s, c), F32), _sds((CONV_WIDTH, c), F32), _sds((1, c), F32)], grid=(c // CONV_TILE,),
        in_specs=[pl.BlockSpec((s, CONV_TILE), lambda j: (0, jnp.minimum(j, n_x - 1))),
                  pl.BlockSpec((s, CONV_TILE), lambda j: (0, jnp.clip(j - n_x, 0, n_b - 1))),
                  pl.BlockSpec((s, CONV_TILE), lambda j: (0, jnp.clip(j - n_x - n_b, 0, n_b - 1))),
                  col, wsp, bsp],
        out_specs=[col, wsp, bsp], args=[dxs, db_, dc_, u, w, b.reshape(1, c)], name=name)
    return du, dw, db[0]


def _lane_pick(mat, idx):
    lane = lax.broadcasted_iota(jnp.int32, mat.shape, 1)
    return jnp.sum(jnp.where(lane == idx, mat, 0.0), axis=1, keepdims=True)


def _sub_pick(mat, idx):
    sub = lax.broadcasted_iota(jnp.int32, mat.shape, 0)
    return jnp.sum(jnp.where(sub == idx, mat, 0.0), axis=0, keepdims=True)


def _expand_heads(cols):
    rows = cols[0].shape[0]
    left = lax.broadcasted_iota(jnp.int32, (rows, LANES), 1) < SSM_HEAD_DIM
    return jnp.concatenate(
        [jnp.where(left, cols[2 * p], cols[2 * p + 1]) for p in range(HEADS_PER_GROUP // 2)], axis=1)


def _heads_to_lanes(mat, g):
    jj = lax.broadcasted_iota(jnp.int32, (GROUP_DIM, LANES), 0)
    ll = lax.broadcasted_iota(jnp.int32, (GROUP_DIM, LANES), 1)
    sel = (ll == HEADS_PER_GROUP * g + (jj >> 6)).astype(F32)
    return _dot(mat, sel, NN, HI)


def _softplus(x):
    return jnp.maximum(x, 0.0) + jnp.log1p(jnp.exp(-jnp.abs(x)))


def _ssd_scalars(dt_ref, bias_ref, a_ref, dtall, csall, cst):
    dta = _softplus(dt_ref[...] + bias_ref[...])
    row = lax.broadcasted_iota(jnp.int32, (CHUNK, CHUNK), 0)
    col = lax.broadcasted_iota(jnp.int32, (CHUNK, CHUNK), 1)
    tri = (row >= col).astype(F32)
    cs = _dot(tri, dta * a_ref[...], NN, HI)
    dtall[...] = dta
    csall[...] = cs
    cst[...] = cs.T


def _decay_mat(cs_col, cs_row):
    row = lax.broadcasted_iota(jnp.int32, (CHUNK, CHUNK), 0)
    col = lax.broadcasted_iota(jnp.int32, (CHUNK, CHUNK), 1)
    return jnp.exp(jnp.where(row >= col, cs_col - cs_row, NEG))


def _head_mask(xpair, right):
    lane = lax.broadcasted_iota(jnp.int32, xpair.shape, 1)
    keep = (lane >= SSM_HEAD_DIM) if right else (lane < SSM_HEAD_DIM)
    return jnp.where(keep, xpair, 0.0)


def _chunk_cols(x_all, g):
    return [_lane_pick(x_all, HEADS_PER_GROUP * g + r) for r in range(HEADS_PER_GROUP)]


def _decay_col(cs_cols):
    return jnp.concatenate(
        [jnp.broadcast_to(jnp.exp(cc[CHUNK - 1:CHUNK, :]), (SSM_HEAD_DIM, 1)) for cc in cs_cols], axis=0)


def ssd_fwd(act, z, dtp, bias_p, a_p, d_p, normw, name, ride=None):
    s = act.shape[0]
    nc = s // CHUNK
    b_off = D_INNER // SSM_STATE
    c_off = b_off + SSM_GROUPS

    def body(xs_ref, b_ref, c_ref, z_ref, dt_ref, bias_ref, a_ref, d_ref, nw_ref,
             yn_ref, y_ref, st_ref, state, dtall, csall, cst):
        c = pl.program_id(0)
        g = pl.program_id(1)

        @pl.when(g == 0)
        def _():
            _ssd_scalars(dt_ref, bias_ref, a_ref, dtall, csall, cst)

        @pl.when(c == 0)
        def _():
            state[g] = jnp.zeros((GROUP_DIM, SSM_STATE), F32)

        cs_cols = _chunk_cols(csall[...], g)
        dt_cols = _chunk_cols(dtall[...], g)
        cs_rows = [_sub_pick(cst[...], HEADS_PER_GROUP * g + r) for r in range(HEADS_PER_GROUP)]
        d_cols = _chunk_cols(d_ref[...], g)
        cs_exp = _expand_heads(cs_cols)
        dt_exp = _expand_heads(dt_cols)
        d_exp = _expand_heads(d_cols)
        xs = xs_ref[...]
        bb = b_ref[...].astype(BF16)
        cb16 = c_ref[...].astype(BF16)
        xdt = xs * dt_exp
        s_prev = state[g]
        st_ref[0, 0] = s_prev
        y_off = _dot(cb16, s_prev.astype(BF16), NT) * jnp.exp(cs_exp)
        decay_st = jnp.exp(cs_exp[CHUNK - 1:CHUNK, :] - cs_exp)
        contrib = _dot((xdt * decay_st).astype(BF16), bb, TN)
        state[g] = _decay_col(cs_cols) * s_prev + contrib
        cbm = _dot(cb16, bb, NT)
        pairs = []
        for p in range(HEADS_PER_GROUP // 2):
            xpair = xdt[:, LANES * p:LANES * (p + 1)]
            m0 = (cbm * _decay_mat(cs_cols[2 * p], cs_rows[2 * p])).astype(BF16)
            m1 = (cbm * _decay_mat(cs_cols[2 * p + 1], cs_rows[2 * p + 1])).astype(BF16)
            pairs.append(_dot(m0, _head_mask(xpair, False).astype(BF16))
                         + _dot(m1, _head_mask(xpair, True).astype(BF16)))
        y = jnp.concatenate(pairs, axis=1) + y_off + xs * d_exp
        y_ref[...] = y
        zf = z_ref[...]
        yg = y * (zf * _sigmoid(zf))
        yn_ref[...] = _rms_fwd(yg, nw_ref[...]).astype(BF16)

    grp = pl.BlockSpec((CHUNK, GROUP_DIM), lambda c, g: (c, g))
    par = pl.BlockSpec((1, LANES), lambda c, g: (0, 0))
    return _pcall(
        body,
        out_shape=[_sds((s, D_INNER), BF16), _sds((s, D_INNER), F32),
                   _sds((nc, SSM_GROUPS, GROUP_DIM, SSM_STATE), F32)],
        grid=(nc, SSM_GROUPS),
        in_specs=[grp,
                  pl.BlockSpec((CHUNK, SSM_STATE), lambda c, g: (c, b_off + g)),
                  pl.BlockSpec((CHUNK, SSM_STATE), lambda c, g: (c, c_off + g)),
                  grp,
                  pl.BlockSpec((CHUNK, LANES), lambda c, g: (c, 0)),
                  par, par, par,
                  pl.BlockSpec((1, GROUP_DIM), lambda c, g: (0, g))],
        out_specs=[grp, grp, pl.BlockSpec((1, 1, GROUP_DIM, SSM_STATE), lambda c, g: (c, g, 0, 0))],
        scratch_shapes=[pltpu.VMEM((SSM_GROUPS, GROUP_DIM, SSM_STATE), F32),
                        pltpu.VMEM((CHUNK, LANES), F32), pltpu.VMEM((CHUNK, LANES), F32),
                        pltpu.VMEM((LANES, CHUNK), F32)],
        args=[act, act, act, z, dtp, bias_p, a_p, d_p, normw], name=name, ride=ride)


def ssd_bwd(dyn, act, z, y_pre, states, dtp, bias_p, a_p, d_p, normw, name, ride=None):
    s = act.shape[0]
    nc = s // CHUNK
    b_off = D_INNER // SSM_STATE
    c_off = b_off + SSM_GROUPS

    def body(dyn_ref, xs_ref, b_ref, c_ref, z_ref, y_ref, st_ref, dt_ref, bias_ref, a_ref, d_ref, nw_ref,
             dxs_ref, db_ref, dc_ref, dz_ref, ddt_ref, dnw_ref, dbias_ref, da_ref, dd_ref,
             dstate, dtall, csall, cst):
        c = pl.program_id(0)
        g = pl.program_id(1)

        @pl.when(g == 0)
        def _():
            _ssd_scalars(dt_ref, bias_ref, a_ref, dtall, csall, cst)
            ddt_ref[...] = jnp.zeros((CHUNK, LANES), F32)

        @pl.when(c == 0)
        def _():
            dstate[g] = jnp.zeros((GROUP_DIM, SSM_STATE), F32)

        @pl.when(jnp.logical_and(c == 0, g == 0))
        def _():
            dnw_ref[...] = jnp.zeros(dnw_ref.shape, F32)
            dbias_ref[...] = jnp.zeros((1, LANES), F32)
            da_ref[...] = jnp.zeros((1, LANES), F32)
            dd_ref[...] = jnp.zeros((1, LANES), F32)

        cs_cols = _chunk_cols(csall[...], g)
        dt_cols = _chunk_cols(dtall[...], g)
        cs_rows = [_sub_pick(cst[...], HEADS_PER_GROUP * g + r) for r in range(HEADS_PER_GROUP)]
        d_cols = _chunk_cols(d_ref[...], g)
        cs_exp = _expand_heads(cs_cols)
        dt_exp = _expand_heads(dt_cols)
        d_exp = _expand_heads(d_cols)
        xs = xs_ref[...]
        bb = b_ref[...].astype(BF16)
        cb16 = c_ref[...].astype(BF16)
        xdt = xs * dt_exp
        s_prev = st_ref[0, 0]
        s_prev16 = s_prev.astype(BF16)
        ds_next = dstate[g]
        ds16 = ds_next.astype(BF16)

        zf = z_ref[...]
        sz = _sigmoid(zf)
        silu_z = zf * sz
        y = y_ref[...]
        yg = y * silu_z
        dout = dyn_ref[...]
        dyg, dnw = _rms_bwd(dout, yg, nw_ref[...])
        dnw_ref[pl.ds(g, 1), :] += jnp.sum(dnw, axis=0, keepdims=True)
        dy = dyg * silu_z
        dz_ref[...] = dyg * y * (sz * (1.0 + zf * (1.0 - sz)))
        dd_ref[...] += jnp.sum(_heads_to_lanes(dy * xs, g), axis=0, keepdims=True)

        exp_cs = jnp.exp(cs_exp)
        decay_st = jnp.exp(cs_exp[CHUNK - 1:CHUNK, :] - cs_exp)
        cs_t = _dot(cb16, s_prev16, NT)
        dyo = dy * exp_cs
        dc_acc = _dot(dyo.astype(BF16), s_prev16, NN)
        g1 = _dot(bb, ds16, NT)
        xds = xdt * decay_st
        db_acc = _dot(xds.astype(BF16), ds16, NN)
        dxdt_off = g1 * decay_st
        t_exp = g1 * xds
        dcs_exp = dy * cs_t * exp_cs - t_exp
        decay_c = _decay_col(cs_cols)
        dstate[g] = decay_c * ds_next + _dot(dyo.astype(BF16), cb16, TN)
        dlast_col = jnp.sum(ds_next * s_prev, axis=1, keepdims=True) * decay_c
        jj = lax.broadcasted_iota(jnp.int32, (GROUP_DIM, LANES), 0)
        ll = lax.broadcasted_iota(jnp.int32, (GROUP_DIM, LANES), 1)
        sel = ll == HEADS_PER_GROUP * g + (jj >> 6)
        dlast = jnp.sum(jnp.where(sel, dlast_col, 0.0), axis=0, keepdims=True)
        t_all = _heads_to_lanes(t_exp, g)
        dlast += jnp.sum(t_all, axis=0, keepdims=True)
        dcs_all = _heads_to_lanes(dcs_exp, g)

        cbm = _dot(cb16, bb, NT)
        dcb = jnp.zeros((CHUNK, CHUNK), F32)
        dcs_rows = jnp.zeros((LANES, CHUNK), F32)
        lane_l = lax.broadcasted_iota(jnp.int32, (CHUNK, LANES), 1)
        sub_l = lax.broadcasted_iota(jnp.int32, (LANES, CHUNK), 0)
        dxdt_pairs = []
        for p in range(HEADS_PER_GROUP // 2):
            xpair16 = xdt[:, LANES * p:LANES * (p + 1)].astype(BF16)
            dypair = dy[:, LANES * p:LANES * (p + 1)]
            acc = None
            for r in (2 * p, 2 * p + 1):
                lm = _decay_mat(cs_cols[r], cs_rows[r])
                m = cbm * lm
                dyh = _head_mask(dypair, r % 2 == 1).astype(BF16)
                dm = _dot(dyh, xpair16, NT)
                dcb += dm * lm
                q = dm * m
                idx = HEADS_PER_GROUP * g + r
                dcs_all += jnp.where(lane_l == idx, jnp.sum(q, axis=1, keepdims=True), 0.0)
                dcs_rows -= jnp.where(sub_l == idx, jnp.sum(q, axis=0, keepdims=True), 0.0)
                part = _dot(m.astype(BF16), dyh, TN)
                acc = part if acc is None else acc + part
            dxdt_pairs.append(acc)
        dxdt = jnp.concatenate(dxdt_pairs, axis=1) + dxdt_off
        dcb16 = dcb.astype(BF16)
        dc_ref[...] = dc_acc + _dot(dcb16, bb, NN)
        db_ref[...] = db_acc + _dot(dcb16, cb16, TN)
        dxs_ref[...] = dxdt * dt_exp + dy * d_exp

        dcs_all += dcs_rows.T
        row = lax.broadcasted_iota(jnp.int32, (CHUNK, CHUNK), 0)
        col = lax.broadcasted_iota(jnp.int32, (CHUNK, CHUNK), 1)
        last_row = lax.broadcasted_iota(jnp.int32, (CHUNK, LANES), 0) == CHUNK - 1
        dcs_all += jnp.where(last_row, dlast, 0.0)
        da_all = _dot((col >= row).astype(F32), dcs_all, NN, HI)
        dta = dtall[...]
        in_group = jnp.logical_and(lane_l >= HEADS_PER_GROUP * g, lane_l < HEADS_PER_GROUP * (g + 1))
        ddt = jnp.where(in_group, da_all * a_ref[...] + _heads_to_lanes(dxdt * xs, g), 0.0)
        da_ref[...] += jnp.sum(jnp.where(in_group, da_all * dta, 0.0), axis=0, keepdims=True)
        ddt_raw = ddt * _sigmoid(dt_ref[...] + bias_ref[...])
        ddt_ref[...] += ddt_raw
        dbias_ref[...] += jnp.sum(ddt_raw, axis=0, keepdims=True)

    rev = lambda c, g: (nc - 1 - c, g)
    grp = pl.BlockSpec((CHUNK, GROUP_DIM), rev)
    st = pl.BlockSpec((CHUNK, SSM_STATE), rev)
    par = pl.BlockSpec((1, LANES), lambda c, g: (0, 0))
    dtb = pl.BlockSpec((CHUNK, LANES), lambda c, g: (nc - 1 - c, 0))
    f = lambda shape: _sds(shape, F32)
    return _pcall(
        body,
        out_shape=[f((s, D_INNER)), f((s, SSM_GROUPS * SSM_STATE)), f((s, SSM_GROUPS * SSM_STATE)),
                   f((s, D_INNER)), f((s, LANES)), f((8, GROUP_DIM)), f((1, LANES)), f((1, LANES)), f((1, LANES))],
        grid=(nc, SSM_GROUPS),
        in_specs=[grp, grp,
                  pl.BlockSpec((CHUNK, SSM_STATE), lambda c, g: (nc - 1 - c, b_off + g)),
                  pl.BlockSpec((CHUNK, SSM_STATE), lambda c, g: (nc - 1 - c, c_off + g)),
                  grp, grp,
                  pl.BlockSpec((1, 1, GROUP_DIM, SSM_STATE), lambda c, g: (nc - 1 - c, g, 0, 0)),
                  dtb, par, par, par,
                  pl.BlockSpec((1, GROUP_DIM), lambda c, g: (0, g))],
        out_specs=[grp, st, st, grp, dtb, pl.BlockSpec((8, GROUP_DIM), lambda c, g: (0, 0)), par, par, par],
        scratch_shapes=[pltpu.VMEM((SSM_GROUPS, GROUP_DIM, SSM_STATE), F32),
                        pltpu.VMEM((CHUNK, LANES), F32), pltpu.VMEM((CHUNK, LANES), F32),
                        pltpu.VMEM((LANES, CHUNK), F32)],
        args=[dyn, act, act, act, z, y_pre, states, dtp, bias_p, a_p, d_p, normw], name=name, ride=ride)


def _attn_probs(q, kp, kc, sink, n):
    sp = _dot(q, kp, NT)
    sc = _dot(q, kc, NT)
    i = lax.broadcasted_iota(jnp.int32, sp.shape, 0) & (WINDOW - 1)
    j = lax.broadcasted_iota(jnp.int32, sp.shape, 1)
    sp = jnp.where(jnp.logical_and(j > i, n > 0), sp, NEG)
    sc = jnp.where(j <= i, sc, NEG)
    m = jnp.maximum(jnp.maximum(jnp.max(sp, axis=1, keepdims=True), jnp.max(sc, axis=1, keepdims=True)), sink)
    pp = jnp.exp(sp - m)
    pc = jnp.exp(sc - m)
    ps = jnp.exp(sink - m)
    inv = 1.0 / (jnp.sum(pp, axis=1, keepdims=True) + jnp.sum(pc, axis=1, keepdims=True) + ps)
    return pp * inv, pc * inv, ps * inv


def attn_fwd(qt, kt, vt, sink_rows, name, ride=None):
    s = qt.shape[1]
    nb = s // WINDOW
    rows = Q_PER_KV * WINDOW

    def body(q_ref, kp_ref, kc_ref, vp_ref, vc_ref, sk_ref, o_ref):
        n = pl.program_id(1)
        q = q_ref[...].reshape(rows, ATT_HEAD_DIM)
        pp, pc, _ = _attn_probs(q, kp_ref[0], kc_ref[0], sk_ref[0], n)
        o = _dot(pp.astype(BF16), vp_ref[0]) + _dot(pc.astype(BF16), vc_ref[0])
        o_ref[...] = o.reshape(Q_PER_KV, WINDOW, ATT_HEAD_DIM).astype(BF16)

    qsp = pl.BlockSpec((Q_PER_KV, WINDOW, ATT_HEAD_DIM), lambda h, n: (h, n, 0))
    prev = pl.BlockSpec((1, WINDOW, ATT_HEAD_DIM), lambda h, n: (h, jnp.maximum(n - 1, 0), 0))
    cur = pl.BlockSpec((1, WINDOW, ATT_HEAD_DIM), lambda h, n: (h, n, 0))
    return _pcall(body, out_shape=[_sds(qt.shape, BF16)], grid=(N_KV_HEADS, nb),
                  in_specs=[qsp, prev, cur, prev, cur, pl.BlockSpec((1, rows, 1), lambda h, n: (h, 0, 0))],
                  out_specs=[qsp], args=[qt, kt, kt, vt, vt, sink_rows], name=name, ride=ride)


def attn_bwd(qt, kt, vt, sink_rows, dot_, name, ride=None):
    s = qt.shape[1]
    nb = s // WINDOW
    rows = Q_PER_KV * WINDOW

    def body(q_ref, kp_ref, kc_ref, vp_ref, vc_ref, sk_ref, do_ref, dq_ref, dk_ref, dv_ref, ds_ref, kacc, vacc):
        n = pl.program_id(1)

        @pl.when(n < nb)
        def _():
            q = q_ref[...].reshape(rows, ATT_HEAD_DIM)
            do = do_ref[...].reshape(rows, ATT_HEAD_DIM)
            kp, kc, vp, vc = kp_ref[0], kc_ref[0], vp_ref[0], vc_ref[0]
            pp, pc, ps = _attn_probs(q, kp, kc, sk_ref[0], n)
            dpp = _dot(do, vp, NT)
            dpc = _dot(do, vc, NT)
            delta = jnp.sum(pp * dpp, axis=1, keepdims=True) + jnp.sum(pc * dpc, axis=1, keepdims=True)
            dsp = (pp * (dpp - delta)).astype(BF16)
            dsc = (pc * (dpc - delta)).astype(BF16)
            dq = _dot(dsp, kp) + _dot(dsc, kc)
            dq_ref[...] = dq.reshape(Q_PER_KV, WINDOW, ATT_HEAD_DIM)
            dk_prev = _dot(dsp, q, TN)
            dv_prev = _dot(pp.astype(BF16), do, TN)

            @pl.when(n == 0)
            def _():
                dk_ref[0] = dk_prev
                dv_ref[0] = dv_prev

            @pl.when(n > 0)
            def _():
                dk_ref[0] = kacc[...] + dk_prev
                dv_ref[0] = vacc[...] + dv_prev

            kacc[...] = _dot(dsc, q, TN)
            vacc[...] = _dot(pc.astype(BF16), do, TN)
            dsk = -ps * delta
            sub = lax.broadcasted_iota(jnp.int32, (8, LANES), 0)
            tile = jnp.zeros((8, LANES), F32)
            for h in range(Q_PER_KV):
                tile += jnp.where(sub == h, jnp.sum(dsk[h * WINDOW:(h + 1) * WINDOW, :], axis=0, keepdims=True), 0.0)
            ds_ref[0, 0] = tile

        @pl.when(n == nb)
        def _():
            dk_ref[0] = kacc[...]
            dv_ref[0] = vacc[...]
            ds_ref[0, 0] = jnp.zeros((8, LANES), F32)

    last = nb - 1
    qsp = pl.BlockSpec((Q_PER_KV, WINDOW, ATT_HEAD_DIM), lambda h, n: (h, jnp.minimum(n, last), 0))
    prev = pl.BlockSpec((1, WINDOW, ATT_HEAD_DIM), lambda h, n: (h, jnp.clip(n - 1, 0, last), 0))
    cur = pl.BlockSpec((1, WINDOW, ATT_HEAD_DIM), lambda h, n: (h, jnp.minimum(n, last), 0))
    dkv = pl.BlockSpec((1, WINDOW, ATT_HEAD_DIM), lambda h, n: (h, jnp.maximum(n - 1, 0), 0))
    f = lambda shape: _sds(shape, F32)
    return _pcall(
        body, out_shape=[f(qt.shape), f(kt.shape), f(vt.shape), f((N_KV_HEADS, nb + 1, 8, LANES))],
        grid=(N_KV_HEADS, nb + 1),
        in_specs=[qsp, prev, cur, prev, cur, pl.BlockSpec((1, rows, 1), lambda h, n: (h, 0, 0)), qsp],
        out_specs=[qsp, dkv, dkv, pl.BlockSpec((1, 1, 8, LANES), lambda h, n: (h, n, 0, 0))],
        scratch_shapes=[pltpu.VMEM((WINDOW, ATT_HEAD_DIM), F32), pltpu.VMEM((WINDOW, ATT_HEAD_DIM), F32)],
        args=[qt, kt, kt, vt, vt, sink_rows, dot_], name=name, ride=ride)


def loss_head(x, w, tgt, name):
    s, d = x.shape
    tm = _row_tile(s, 256)

    def body(x_ref, w_ref, t_ref, loss_ref, dx_ref, dw_ref):
        i = pl.program_id(0)
        xf = x_ref[...]
        wv = w_ref[...]
        r = lax.rsqrt(jnp.mean(xf * xf, axis=-1, keepdims=True) + EPS)
        xhat = xf * r
        e = xhat * wv - t_ref[...]
        part = 0.5 * jnp.sum(jnp.mean(e * e, axis=-1, keepdims=True), axis=0, keepdims=True)
        dy = e * (1.0 / d)
        dxhat = dy * wv
        dx_ref[...] = r * (dxhat - xhat * jnp.mean(dxhat * xhat, axis=-1, keepdims=True))
        col = jnp.sum(dy * xhat, axis=0, keepdims=True)

        @pl.when(i == 0)
        def _():
            loss_ref[...] = jnp.broadcast_to(part, (1, LANES))
            dw_ref[...] = col

        @pl.when(i > 0)
        def _():
            loss_ref[...] += jnp.broadcast_to(part, (1, LANES))
            dw_ref[...] += col

    row = pl.BlockSpec((tm, d), lambda i: (i, 0))
    vec = pl.BlockSpec((1, d), lambda i: (0, 0))
    return _pcall(body, out_shape=[_sds((1, LANES), F32), _sds((s, d), F32), _sds((1, d), F32)], grid=(s // tm,),
                  in_specs=[row, vec, row], out_specs=[pl.BlockSpec((1, LANES), lambda i: (0, 0)), row, vec],
                  args=[x, w.reshape(1, d), tgt], name=name)


def _tile_rows(r, c, max_elems=262144, mult=16):
    best = None
    for t in range(mult, r + 1, mult):
        if r % t == 0 and t * c <= max_elems:
            best = t
    return best or r


def add_pair(xh, p, c_idx, name):
    _, r, c = xh.shape
    tr = _tile_rows(r, c)

    def body(c_ref, x_ref, p_ref, o_ref):
        o_ref[...] = (x_ref[0].astype(F32) + p_ref[...].astype(F32)).astype(BF16)

    blk = pl.BlockSpec((tr, c), lambda i, cr: (i, 0))
    return pl.pallas_call(
        body, out_shape=_sds((r, c), BF16),
        grid_spec=pltpu.PrefetchScalarGridSpec(
            num_scalar_prefetch=1, grid=(r // tr,),
            in_specs=[pl.BlockSpec((1, tr, c), lambda i, cr: (cr[0], i, 0)), blk], out_specs=blk),
        name=name, compiler_params=_cp(1))(c_idx, xh, p)


def sum_chips(q, own, chip_idx, name):
    _, r, c = q.shape
    tr = _tile_rows(r, c)

    def body(k_ref, q_ref, own_ref, o_ref):
        k = k_ref[0]
        mine = own_ref[0].astype(F32)
        tot = None
        for j in range(N_CHIPS):
            term = jnp.where(k == j, mine, q_ref[j].astype(F32))
            tot = term if tot is None else tot + term
        o_ref[...] = tot

    return pl.pallas_call(
        body, out_shape=_sds((r, c), F32),
        grid_spec=pltpu.PrefetchScalarGridSpec(
            num_scalar_prefetch=1, grid=(r // tr,),
            in_specs=[pl.BlockSpec((N_CHIPS, tr, c), lambda i, kr: (0, i, 0)),
                      pl.BlockSpec((1, tr, c), lambda i, kr: (kr[0], i, 0))],
            out_specs=pl.BlockSpec((tr, c), lambda i, kr: (i, 0))),
        name=name, compiler_params=_cp(1))(chip_idx, q, own)


def adamw(w, g, m, v, name):
    r, c = w.shape
    tr = _tile_rows(r, c, max_elems=131072, mult=8)
    c1 = 1.0 / (1.0 - ADAM_B1 ** ADAM_STEP)
    c2 = 1.0 / (1.0 - ADAM_B2 ** ADAM_STEP)

    def body(w_ref, g_ref, m_ref, v_ref, d_ref, mo_ref, vo_ref):
        gf = g_ref[...]
        mn = ADAM_B1 * m_ref[...] + (1.0 - ADAM_B1) * gf
        vn = ADAM_B2 * v_ref[...] + (1.0 - ADAM_B2) * (gf * gf)
        mo_ref[...] = mn
        vo_ref[...] = vn
        d_ref[...] = -ADAM_LR * ((mn * c1) / (jnp.sqrt(vn * c2) + ADAM_EPS) + ADAM_WD * w_ref[...])

    blk = pl.BlockSpec((tr, c), lambda i: (i, 0))
    out = _sds((r, c), F32)
    return _pcall(body, out_shape=[out, out, out], grid=(r // tr,), in_specs=[blk] * 4, out_specs=[blk] * 3,
                  args=[w, g, m, v], name=name)


WEIGHTS = ['norm_w', 'ffn_w_gate', 'ffn_w_up', 'ffn_w_down', 'ssm_w_in', 'ssm_conv_w', 'ssm_conv_b', 'ssm_dt_bias',
           'ssm_a_log', 'ssm_d', 'ssm_norm_w', 'ssm_w_out', 'kv_norm_w', 'w_k', 'b_k', 'w_v', 'b_v', 'attn_w_q',
           'attn_b_q', 'attn_sinks', 'attn_w_o', 'attn_b_o', 'final_norm_w']
BIG = ['ffn_w_gate', 'ffn_w_up', 'ffn_w_down', 'ssm_w_in', 'ssm_w_out', 'w_k', 'w_v', 'attn_w_q', 'attn_w_o']
TRANSPOSED = ('ffn_w_gate', 'ffn_w_up', 'ssm_w_in')
SMALL = [n for n in WEIGHTS if n not in BIG]
SMALL_SHARDED = {'norm_w': 2, 'ssm_conv_w': 2, 'ssm_conv_b': 1, 'ssm_norm_w': 1}
ROW_ALIGN = 8 * LANES


def _pack_rows(parts):
    flat = jnp.concatenate([p.reshape(-1).astype(F32) for p in parts])
    pad = (-flat.size) % ROW_ALIGN
    return jnp.pad(flat, (0, pad)).reshape(-1, LANES)


def _unpack_rows(buf, shapes):
    flat = buf.reshape(-1)
    out, pos = [], 0
    for shp in shapes:
        size = math.prod(shp)
        out.append(flat[pos:pos + size].reshape(shp))
        pos += size
    return out


def _as2d(a):
    return a.reshape(-1, a.shape[-1])


def _heads_major(t, n_heads):
    s = t.shape[0]
    return t.reshape(s, n_heads, ATT_HEAD_DIM).transpose(1, 0, 2)


def _tokens_major(t):
    h, s, dh = t.shape
    return t.transpose(1, 0, 2).reshape(s, h * dh)


def _pad_lanes(v):
    return jnp.pad(v.reshape(1, -1), ((0, 0), (0, LANES - v.size)))


def _chips_first(t):
    return t.swapaxes(0, 1).reshape((-1,) + t.shape[3:])


def _parts_first(t, rows):
    return t.reshape((N_CHIPS, N_CORES, rows) + t.shape[1:]).swapaxes(0, 1)


def kernel(*args):
    names = (['x'] + WEIGHTS + ['loss_target'] + ['m_' + n for n in WEIGHTS] + ['v_' + n for n in WEIGHTS])
    a = dict(zip(names, args))
    for n in TRANSPOSED:
        for pre in ('', 'm_', 'v_'):
            a[pre + n] = a[pre + n].swapaxes(-1, -2)
    xi, yi, ci = lax.axis_index("x"), lax.axis_index("y"), lax.axis_index("c")
    chip = 2 * xi + yi
    south = ci == 0
    c_idx = jnp.reshape(ci, (1,)).astype(jnp.int32)
    chip_idx = jnp.reshape(chip, (1,)).astype(jnp.int32)
    x0 = a['x'][0]
    s = x0.shape[0]
    cos, sin = rope_tables(s)

    def own_slot(full, mine):
        return lax.dynamic_update_slice_in_dim(full, mine[:, None], chip, axis=1)

    def ffn_shard(l, i):
        return [a[n][l, i].astype(BF16).reshape(N_CORES, FF_PART, D_MODEL)
                for n in ('ffn_w_gate', 'ffn_w_up', 'ffn_w_down')]

    def own_slots(fulls, mines):
        return [own_slot(f, m) for f, m in zip(fulls, mines)]
    w_in_sh = jnp.pad(a['ssm_w_in'][0], ((0, IN_SHARD_PAD - IN_SHARD), (0, 0))).astype(BF16).reshape(
        N_CORES, IN_SHARD_PAD // 2, D_MODEL)
    w_out_sh = a['ssm_w_out'][0].astype(BF16).reshape(N_CORES, 256, D_MODEL)
    attn_sh = jnp.stack([a['attn_w_q'][0], a['attn_w_o'][0]]).astype(BF16)
    kv_sh = jnp.stack([a['w_k'], a['w_v']]).astype(BF16)
    small_names = list(SMALL_SHARDED)
    small_sh = _pack_rows([a[n] for n in small_names])
    small_sh = small_sh.reshape(N_CORES, small_sh.shape[0] // 2, LANES)

    sh00, sh01, sh10, sh11 = ffn_shard(0, 0), ffn_shard(0, 1), ffn_shard(1, 0), ffn_shard(1, 1)
    first = run_exchange(gather_chips(sh00 + [small_sh]), "gather_first")
    w00 = own_slots(first[:3], sh00)
    smalls = own_slot(first[3], small_sh)
    p = {}
    per_chip = [_unpack_rows(smalls[:, k], [a[n].shape for n in small_names]) for k in range(N_CHIPS)]
    for idx, n in enumerate(small_names):
        p[n] = jnp.concatenate([per_chip[k][idx] for k in range(N_CHIPS)], axis=SMALL_SHARDED[n])
    nw = p['norm_w']
    conv_w, conv_b, ssm_nw = p['ssm_conv_w'][0], p['ssm_conv_b'][0], p['ssm_norm_w'][0].reshape(1, D_INNER)

    h00 = rmsnorm_fwd(x0, nw[0, 0], "norm_in")
    (x1, h01, gu00), (w_in_g, kv_g) = ffn_fwd(h00, x0, *w00, [nw[0, 1]], "ffn_fwd_00",
                                              ride=gather_chips([w_in_sh, kv_sh]))
    w_in_t = _chips_first(own_slot(w_in_g, w_in_sh)).reshape(N_CHIPS, IN_SHARD_PAD, D_MODEL)[:, :IN_SHARD].reshape(
        IN_PROJ_DIM, D_MODEL)
    w_dt_t = jnp.pad(w_in_t[D_INNER + CONV_DIM:], ((0, LANES - SSM_HEADS), (0, 0)))
    kv_g = own_slot(kv_g, kv_sh)
    w_k, w_v = kv_g[0].reshape(D_MODEL, KV_DIM), kv_g[1].reshape(D_MODEL, KV_DIM)

    zz = mm_nt(h01, w_in_t, "ssm_in_z", n=D_INNER)
    xbc, (w_out_g,) = mm_nt(h01, w_in_t, "ssm_in_xbc", n=CONV_DIM, row0=D_INNER, ride=gather_chips([w_out_sh]))
    w_out = _chips_first(own_slot(w_out_g, w_out_sh))
    dtp = mm_nt(h01, w_dt_t, "ssm_in_dt")
    act, (wg01,) = conv_fwd(xbc, conv_w, conv_b, "ssm_conv", ride=gather_chips(sh01[:1]))
    bias_p = _pad_lanes(a['ssm_dt_bias'][0])
    a_p = _pad_lanes(-jnp.exp(a['ssm_a_log'][0]))
    d_p = _pad_lanes(a['ssm_d'][0])
    (yn, y_pre, states), (wu01, wd01) = ssd_fwd(act, zz, dtp, bias_p, a_p, d_p, ssm_nw, "ssd_fwd",
                                                ride=gather_chips(sh01[1:]))
    w01 = own_slots([wg01, wu01, wd01], sh01)
    (x2, h02), (wg10,) = mm_res(yn, w_out, x1, "ssm_out", norm_ws=[nw[0, 2]], ride=gather_chips(sh10[:1]))
    (x3, hkv, h10, gu01), (wu10, wd10) = ffn_fwd(h02, x2, *w01, [a['kv_norm_w'], nw[1, 0]], "ffn_fwd_01",
                                                 ride=gather_chips(sh10[1:]))
    w10 = own_slots([wg10, wu10, wd10], sh10)

    k_rot = rope_apply(mm_nn(hkv, w_k, "kv_k", bias=a['b_k']), cos, sin, "rope_k")
    v = mm_nn(hkv, w_v, "kv_v", bias=a['b_v'], out_dtype=BF16)
    kt = _heads_major(k_rot, N_KV_HEADS)
    vt = _heads_major(v, N_KV_HEADS)

    (x4, h11, gu10), (attn_g, wg11) = ffn_fwd(h10, x3, *w10, [nw[1, 1]], "ffn_fwd_10",
                                              ride=gather_chips([attn_sh, sh11[0]]))
    attn_g = own_slot(attn_g, attn_sh)
    w_q, w_o = attn_g[0].reshape(D_MODEL, D_MODEL), attn_g[1].reshape(D_MODEL, D_MODEL)
    scale = 1.0 / math.sqrt(ATT_HEAD_DIM)
    q_rot = rope_apply(mm_nn(h11, w_q, "attn_q", bias=a['attn_b_q'][0]), cos, sin, "rope_q", scale=scale)
    qt = _heads_major(q_rot, N_Q_HEADS)
    sink_rows = jnp.repeat(a['attn_sinks'][0].reshape(N_KV_HEADS, Q_PER_KV), WINDOW, axis=1).reshape(
        N_KV_HEADS, Q_PER_KV * WINDOW, 1)
    (ot,), (wu11, wd11) = attn_fwd(qt, kt, vt, sink_rows, "attn_fwd", ride=gather_chips(sh11[1:]))
    w11 = own_slots([wg11, wu11, wd11], sh11)
    o = _tokens_major(ot)
    x5, h12 = mm_res(o, w_o, x4, "attn_out", bias=a['attn_b_o'][0], norm_ws=[nw[1, 2]])
    x6, gu11 = ffn_fwd(h12, x5, *w11, [], "ffn_fwd_11")

    loss_v, dx6, d_final = loss_head(x6, a['final_norm_w'], a['loss_target'][0], "loss_head")
    loss = lax.psum(loss_v[0, 0], ("x", "y", "c"))
    g = {'final_norm_w': d_final[0]}

    def pre_reduce(grads, sib, tag):
        out = []
        for idx, (gr, sb) in enumerate(zip(grads, sib)):
            t = add_pair(gr.reshape(2, -1, gr.shape[-1]), _as2d(sb), c_idx, "rs_add_%s_%d" % (tag, idx))
            out.append(t.reshape(gr.shape[1:]))
        return out

    def chip_sum(landed, parts, tag):
        out = []
        for idx, (q, own) in enumerate(zip(landed, parts)):
            t = sum_chips(q.reshape(N_CHIPS, -1, q.shape[-1]), own.reshape(N_CHIPS, -1, q.shape[-1]), chip_idx,
                          "rs_sum_%s_%d" % (tag, idx))
            out.append(t.reshape(q.shape[1:]))
        return out

    dnw = [[None] * 3 for _ in range(2)]
    sums = {}

    def trade(key):
        return swap_cores(sums[key], False)

    dx5, dnw12, *g11 = ffn_bwd(dx6, h12, x5, nw[1, 2], gu11, *w11, "ffn_bwd_11")
    dnw[1][2] = dnw12[0]
    g['attn_b_o'] = colsum(dx5, "attn_dbo")
    d_wo, sib11 = mm_tn(o, dx5, "attn_dwo", ride=swap_cores(g11, True))
    t11 = pre_reduce(g11, sib11, "11")
    do = mm_nt(dx5, w_o, "attn_do", out_dtype=BF16)
    (dqt, dkt, dvt, dsink), land11 = attn_bwd(qt, kt, vt, sink_rows, _heads_major(do, N_Q_HEADS), "attn_bwd",
                                             ride=scatter_chips(t11))
    sums['11'] = chip_sum(land11, t11, "11")
    g['attn_sinks'] = jnp.sum(dsink[:, :, :Q_PER_KV, 0], axis=1).reshape(N_Q_HEADS)
    dq_pre = rope_apply(_tokens_major(dqt), cos, sin, "rope_dq", inverse=True, scale=scale, out_dtype=F32)
    g['attn_b_q'] = colsum(dq_pre, "attn_dbq")
    d_wq = mm_tn(h11, dq_pre, "attn_dwq")
    g_attn = [jnp.stack([d_wq.reshape(N_CHIPS, 256, D_MODEL), d_wo.reshape(N_CHIPS, 256, D_MODEL)])]
    (dx4, dnw[1][1]), sib_attn = mm_rms_bwd([(dq_pre, 0, w_q, 0, D_MODEL, "nt")], dx5, x4, nw[1, 1], "attn_bwd_dh",
                                            ride=swap_cores(g_attn, True))
    t_attn = pre_reduce(g_attn, sib_attn, "attn")
    (dx3, dnw10, *g10), landed = ffn_bwd(dx4, h10, x3, nw[1, 0], gu10, *w10, "ffn_bwd_10",
                                         ride=join(scatter_chips(t_attn), trade('11')))
    dnw[1][0] = dnw10[0]
    sums['attn'] = chip_sum(landed[:1], t_attn, "attn")
    theirs = {'11': landed[1:]}
    dk_pre = rope_apply(_tokens_major(dkt), cos, sin, "rope_dk", inverse=True, out_dtype=F32)
    dv = _tokens_major(dvt)
    g['b_k'] = colsum(dk_pre, "kv_dbk")
    g['b_v'] = colsum(dv, "kv_dbv")
    d_wk, sib10 = mm_tn(hkv, dk_pre, "kv_dwk", ride=swap_cores(g10, True))
    t10 = pre_reduce(g10, sib10, "10")
    d_wv = mm_tn(hkv, dv, "kv_dwv")
    g_kv = [jnp.stack([d_wk.reshape(N_CHIPS, 256, KV_DIM), d_wv.reshape(N_CHIPS, 256, KV_DIM)])]
    (dx3, g['kv_norm_w']), sib_kv = mm_rms_bwd(
        [(dk_pre, 0, w_k, 0, KV_DIM, "nt"), (dv, 0, w_v, 0, KV_DIM, "nt")], dx3, x3, a['kv_norm_w'], "kv_bwd_dh",
        ride=swap_cores(g_kv, True))
    t_kv = pre_reduce(g_kv, sib_kv, "kv")
    (dx2, dnw02, *g01), landed = ffn_bwd(dx3, h02, x2, nw[0, 2], gu01, *w01, "ffn_bwd_01",
                                         ride=join(scatter_chips(t10 + t_kv), trade('attn')))
    dnw[0][2] = dnw02[0]
    sums['10'] = chip_sum(landed[:3], t10, "10")
    sums['kv'] = chip_sum(landed[3:4], t_kv, "kv")
    theirs['attn'] = landed[4:]
    d_wout, sib01 = mm_tn(yn, dx2, "ssm_dwout", ride=swap_cores(g01, True))
    t01 = pre_reduce(g01, sib01, "01")
    dyn = mm_nt(dx2, w_out, "ssm_dyn")
    (dxs, db_, dc_, dz, ddt, d_ssm_nw, d_bias, d_a, d_d), landed = ssd_bwd(
        dyn, act, zz, y_pre, states, dtp, bias_p, a_p, d_p, ssm_nw, "ssd_bwd",
        ride=join(scatter_chips(t01), trade('10'), trade('kv')))
    sums['01'] = chip_sum(landed[:3], t01, "01")
    theirs['10'], theirs['kv'] = landed[3:6], landed[6:]
    g['ssm_norm_w'] = d_ssm_nw[:SSM_GROUPS].reshape(D_INNER)
    g['ssm_dt_bias'] = d_bias[0, :SSM_HEADS]
    g['ssm_a_log'] = d_a[0, :SSM_HEADS] * a_p[0, :SSM_HEADS]
    g['ssm_d'] = d_d[0, :SSM_HEADS]
    dxbc, g['ssm_conv_w'], g['ssm_conv_b'] = conv_bwd(dxs, db_, dc_, xbc, conv_w, conv_b, "ssm_conv_bwd")
    d_win = mm_tn(dz, h01, "ssm_dwz", rows=IN_PROJ_DIM)
    d_win = mm_tn(dxbc, h01, "ssm_dwxbc", into=d_win, rows=IN_PROJ_DIM, row0=D_INNER)
    d_win = mm_tn(ddt, h01, "ssm_dwdt", into=d_win, rows=IN_PROJ_DIM, row0=D_INNER + CONV_DIM, m_valid=SSM_HEADS)
    d_win = jnp.pad(d_win.reshape(N_CHIPS, IN_SHARD, D_MODEL), ((0, 0), (0, IN_SHARD_PAD - IN_SHARD), (0, 0)))
    g_ssm = [_parts_first(d_win.reshape(-1, D_MODEL), IN_SHARD_PAD // 2), _parts_first(d_wout, 256)]
    kb = 1024
    terms = ([(dz, j, w_in_t, j, kb, "nn") for j in range(D_INNER // kb)]
             + [(dxbc, j, w_in_t, D_INNER // kb + j, kb, "nn") for j in range(CONV_DIM // kb)]
             + [(ddt, 0, w_dt_t, 0, LANES, "nn")])
    (dx1, dnw[0][1]), sib_ssm = mm_rms_bwd(terms, dx2, x1, nw[0, 1], "ssm_bwd_dh", ride=swap_cores(g_ssm, True))
    t_ssm = pre_reduce(g_ssm, sib_ssm, "ssm")
    (grad_x, dnw00, *g00), landed = ffn_bwd(dx1, h00, x0, nw[0, 0], gu00, *w00, "ffn_bwd_00",
                                            ride=join(scatter_chips(t_ssm), trade('01')))
    dnw[0][0] = dnw00[0]
    sums['ssm'] = chip_sum(landed[:2], t_ssm, "ssm")
    theirs['01'] = landed[2:]
    landed = run_exchange(join(swap_cores(g00, True), trade('ssm')), "rs_swap_00")
    t00 = pre_reduce(g00, landed[:3], "00")
    theirs['ssm'] = landed[3:]
    land00 = run_exchange(scatter_chips(t00), "rs_scatter_00")
    sums['00'] = chip_sum(land00, t00, "00")
    theirs['00'] = run_exchange(trade('00'), "rs_trade_00")

    full = {key: [(jnp.where(south, m_, t_), jnp.where(south, t_, m_)) for m_, t_ in zip(sums[key], theirs[key])]
            for key in sums}

    gw = {}
    for t, n in enumerate(('ffn_w_gate', 'ffn_w_up', 'ffn_w_down')):
        gw[n] = jnp.stack([jnp.stack([jnp.concatenate(full['%d%d' % (l, i)][t], axis=0) for i in range(2)])
                           for l in range(2)])
    lo, hi = full['attn'][0]
    gw['attn_w_q'], gw['attn_w_o'] = lo[None], hi[None]
    lo, hi = full['kv'][0]
    gw['w_k'], gw['w_v'] = lo, hi
    lo, hi = full['ssm'][0]
    gw['ssm_w_in'] = jnp.concatenate([lo, hi], axis=0)[:IN_SHARD][None]
    lo, hi = full['ssm'][1]
    gw['ssm_w_out'] = jnp.concatenate([lo, hi], axis=0)[None]

    g['norm_w'] = jnp.stack([jnp.stack(r) for r in dnw])
    red = all_reduce_small(_pack_rows([g[n] for n in SMALL]), "reduce_vectors")
    for n, t in zip(SMALL, _unpack_rows(red, [g[n].shape for n in SMALL])):
        if n in SMALL_SHARDED:
            ax = SMALL_SHARDED[n] - (a[n].ndim - t.ndim)
            width = a[n].shape[SMALL_SHARDED[n]]
            t = lax.dynamic_slice_in_dim(t, chip * width, width, axis=ax)
        gw[n] = t.reshape(a[n].shape)

    delta, new_m, new_v = {}, {}, {}
    for n in BIG:
        d, mo, vo = adamw(_as2d(a[n]), _as2d(gw[n]), _as2d(a['m_' + n]), _as2d(a['v_' + n]), "adamw_" + n)
        delta[n], new_m[n], new_v[n] = d.reshape(a[n].shape), mo.reshape(a[n].shape), vo.reshape(a[n].shape)
    shapes = [a[n].shape for n in SMALL]
    packed = [_pack_rows([src[n] for n in SMALL]) for src in
              (a, gw, {n: a['m_' + n] for n in SMALL}, {n: a['v_' + n] for n in SMALL})]
    outs = adamw(*packed, "adamw_vectors")
    for dst, buf in zip((delta, new_m, new_v), outs):
        for n, t in zip(SMALL, _unpack_rows(buf, shapes)):
            dst[n] = t
    for n in TRANSPOSED:
        for dst in (gw, delta, new_m, new_v):
            dst[n] = dst[n].swapaxes(-1, -2)

    return (loss, grad_x[None], *[gw[n] for n in WEIGHTS], *[delta[n] for n in WEIGHTS],
            *[new_m[n] for n in WEIGHTS], *[new_v[n] for n in WEIGHTS])
```

```python
import math

import jax
import jax.numpy as jnp
from jax import lax
from jax.experimental import pallas as pl
from jax.experimental.pallas import tpu as pltpu

F32 = jnp.float32
BF16 = jnp.bfloat16

D_MODEL = 1024
D_INNER = 2048
SSM_HEADS = 32
SSM_GROUPS = 4
HEADS_PER_GROUP = SSM_HEADS // SSM_GROUPS
SSM_HEAD_DIM = 64
SSM_STATE = 128
GROUP_DIM = D_INNER // SSM_GROUPS
CONV_DIM = D_INNER + 2 * SSM_GROUPS * SSM_STATE
CONV_WIDTH = 4
CHUNK = 128
ATT_HEAD_DIM = 64
N_Q_HEADS = 16
N_KV_HEADS = 4
Q_PER_KV = N_Q_HEADS // N_KV_HEADS
KV_DIM = N_KV_HEADS * ATT_HEAD_DIM
WINDOW = 128
ROPE_THETA = 10000.0
D_FF = 2816
N_CHIPS = 4
N_CORES = 2
FF_SHARD = D_FF // N_CHIPS
FF_PART = FF_SHARD // N_CORES
IN_PROJ_DIM = D_INNER + CONV_DIM + SSM_HEADS
IN_SHARD = IN_PROJ_DIM // N_CHIPS
IN_SHARD_PAD = 1312
EPS = 1e-5
NEG = -1e30
LANES = 128
VMEM_LIMIT = 56 * 1024 * 1024

ADAM_LR = 0.001
ADAM_B1 = 0.9
ADAM_B2 = 0.999
ADAM_EPS = 1e-08
ADAM_WD = 0.01
ADAM_STEP = 10

NN = ((1,), (0,))
NT = ((1,), (1,))
TN = ((0,), (0,))
MESH = pl.DeviceIdType.MESH
ANY = pl.BlockSpec(memory_space=pl.ANY)


def _dot(a, b, dims=NN, precision=None):
    return lax.dot_general(a, b, (dims, ((), ())), preferred_element_type=F32, precision=precision)


def _cp(n_grid):
    return pltpu.CompilerParams(dimension_semantics=("arbitrary",) * n_grid, vmem_limit_bytes=VMEM_LIMIT)


def _sigmoid(x):
    return 1.0 / (1.0 + jnp.exp(-x))


def _rms_fwd(xf, w):
    r = lax.rsqrt(jnp.mean(xf * xf, axis=-1, keepdims=True) + EPS)
    return xf * r * w


def _rms_bwd(dh, xf, w):
    r = lax.rsqrt(jnp.mean(xf * xf, axis=-1, keepdims=True) + EPS)
    xhat = xf * r
    dxhat = dh * w
    dx = r * (dxhat - xhat * jnp.mean(dxhat * xhat, axis=-1, keepdims=True))
    return dx, dh * xhat


def _row_tile(s, pref):
    return pref if s % pref == 0 else s


def _col_tile(n):
    for t in (1024, 768, 512, 256, 128):
        if n % t == 0:
            return t
    return n


def _sds(shape, dtype):
    return jax.ShapeDtypeStruct(tuple(shape), dtype)


class Exchange:
    def __init__(self, ins, out_shapes, sems, start, finish):
        self.ins, self.out_shapes, self.sems, self.start, self.finish = ins, out_shapes, sems, start, finish


def _place():
    x, y, c = lax.axis_index("x"), lax.axis_index("y"), lax.axis_index("c")
    others = [(1 - x, y), (x, 1 - y), (1 - x, 1 - y)]
    return x, y, c, 2 * x + y, others


def _rc(src, dst, send_sem, recv_sem, dev):
    return pltpu.make_async_remote_copy(src_ref=src, dst_ref=dst, send_sem=send_sem, recv_sem=recv_sem,
                                        device_id=dev, device_id_type=MESH)


def gather_chips(arrs):
    n = len(arrs)

    def copies(ins, outs, sems):
        send, recv = sems
        x, y, c, k, others = _place()
        ici, land, fwd, fland = [], [], [], []
        for a in range(n):
            for j, (px, py) in enumerate(others):
                ici.append(_rc(ins[a].at[c], outs[a].at[c, k], send.at[a, j], recv.at[a, j], (px, py, c)))
                blk = outs[a].at[c, 2 * px + py]
                land.append(_rc(blk, blk, send.at[a, j], recv.at[a, j], (px, py, c)))
                fwd.append(_rc(blk, blk, send.at[a, 3 + j], recv.at[a, 3 + j], (x, y, 1 - c)))
                blk2 = outs[a].at[1 - c, 2 * px + py]
                fland.append(_rc(blk2, blk2, send.at[a, 3 + j], recv.at[a, 3 + j], (x, y, 1 - c)))
        return ici, land, fwd, fland

    def start(ins, outs, sems):
        for cp in copies(ins, outs, sems)[0]:
            cp.start()

    def finish(ins, outs, sems):
        ici, land, fwd, fland = copies(ins, outs, sems)
        for arrived, onward in zip(land, fwd):
            arrived.wait_recv()
            onward.start()
        for arrived in fland:
            arrived.wait_recv()
        for cp in ici + fwd:
            cp.wait_send()

    return Exchange(list(arrs), [_sds((2, N_CHIPS) + a.shape[1:], a.dtype) for a in arrs],
                    [pltpu.SemaphoreType.DMA((n, 6)), pltpu.SemaphoreType.DMA((n, 6))], start, finish)


def scatter_chips(arrs):
    n = len(arrs)

    def copies(ins, outs, sems):
        send, recv = sems
        x, y, c, k, others = _place()
        out, land = [], []
        for a in range(n):
            for j, (px, py) in enumerate(others):
                out.append(_rc(ins[a].at[2 * px + py], outs[a].at[k], send.at[a, j], recv.at[a, j], (px, py, c)))
                blk = outs[a].at[2 * px + py]
                land.append(_rc(blk, blk, send.at[a, j], recv.at[a, j], (px, py, c)))
        return out, land

    def start(ins, outs, sems):
        for cp in copies(ins, outs, sems)[0]:
            cp.start()

    def finish(ins, outs, sems):
        out, land = copies(ins, outs, sems)
        for arrived in land:
            arrived.wait_recv()
        for cp in out:
            cp.wait_send()

    return Exchange(list(arrs), [_sds(a.shape, a.dtype) for a in arrs],
                    [pltpu.SemaphoreType.DMA((n, 3)), pltpu.SemaphoreType.DMA((n, 3))], start, finish)


def swap_cores(arrs, pick_other):
    n = len(arrs)

    def copies(ins, outs, sems):
        send, recv = sems
        x, y, c, _, _ = _place()
        return [_rc(ins[a].at[1 - c] if pick_other else ins[a], outs[a], send.at[a], recv.at[a], (x, y, 1 - c))
                for a in range(n)]

    def start(ins, outs, sems):
        for cp in copies(ins, outs, sems):
            cp.start()

    def finish(ins, outs, sems):
        for cp in copies(ins, outs, sems):
            cp.wait()

    shapes = [_sds(a.shape[1:] if pick_other else a.shape, a.dtype) for a in arrs]
    return Exchange(list(arrs), shapes, [pltpu.SemaphoreType.DMA((n,)), pltpu.SemaphoreType.DMA((n,))],
                    start, finish)


def join(*parts):
    parts = [p for p in parts if p is not None]
    if not parts:
        return None

    def split(refs, counts):
        out, pos = [], 0
        for cnt in counts:
            out.append(refs[pos:pos + cnt])
            pos += cnt
        return out

    n_in = [len(p.ins) for p in parts]
    n_out = [len(p.out_shapes) for p in parts]
    n_sem = [len(p.sems) for p in parts]

    def run(which):
        def go(ins, outs, sems):
            for p, i, o, s in zip(parts, split(ins, n_in), split(outs, n_out), split(sems, n_sem)):
                getattr(p, which)(i, o, s)
        return go

    return Exchange([a for p in parts for a in p.ins], [s for p in parts for s in p.out_shapes],
                    [s for p in parts for s in p.sems], run("start"), run("finish"))


def _pcall(body, *, out_shape, grid, in_specs, out_specs, args, name, scratch_shapes=(), ride=None, aliases=None):
    out_shape, out_specs, in_specs = tuple(out_shape), tuple(out_specs), list(in_specs)
    aliases = aliases or {}
    if ride is None:
        return pl.pallas_call(body, out_shape=out_shape, grid=grid, in_specs=in_specs, out_specs=out_specs,
                              scratch_shapes=list(scratch_shapes), input_output_aliases=aliases, name=name,
                              compiler_params=_cp(len(grid)))(*args)
    n_in, n_out, n_sc = len(args), len(out_shape), len(scratch_shapes)
    n_xi, n_xo = len(ride.ins), len(ride.out_shapes)

    def wrapped(*refs):
        pos = [0]

        def take(cnt):
            got = refs[pos[0]:pos[0] + cnt]
            pos[0] += cnt
            return got

        c_in, x_in, c_out, x_out, c_sc = take(n_in), take(n_xi), take(n_out), take(n_xo), take(n_sc)
        sems = refs[pos[0]:]
        first, last = True, True
        for d, size in enumerate(grid):
            first = jnp.logical_and(first, pl.program_id(d) == 0)
            last = jnp.logical_and(last, pl.program_id(d) == size - 1)

        @pl.when(first)
        def _():
            ride.start(x_in, x_out, sems)

        body(*c_in, *c_out, *c_sc)

        @pl.when(last)
        def _():
            ride.finish(x_in, x_out, sems)

    res = pl.pallas_call(
        wrapped, out_shape=out_shape + tuple(ride.out_shapes), grid=grid,
        in_specs=in_specs + [ANY] * n_xi, out_specs=out_specs + (ANY,) * n_xo,
        scratch_shapes=list(scratch_shapes) + list(ride.sems), input_output_aliases=aliases, name=name,
        compiler_params=_cp(len(grid)))(*args, *ride.ins)
    return res[:n_out], res[n_out:]


def run_exchange(ex, name):
    n_xi, n_xo = len(ex.ins), len(ex.out_shapes)

    def body(*refs):
        ins, outs, sems = refs[:n_xi], refs[n_xi:n_xi + n_xo], refs[n_xi + n_xo:]
        ex.start(ins, outs, sems)
        ex.finish(ins, outs, sems)

    return pl.pallas_call(body, out_shape=tuple(ex.out_shapes), in_specs=[ANY] * n_xi, out_specs=(ANY,) * n_xo,
                          scratch_shapes=list(ex.sems), name=name)(*ex.ins)


def all_reduce_small(buf, name):
    r = buf.shape[0]
    n_dev = 8

    def body(in_ref, o_ref, land, send_sems, recv_sems):
        x, y, c, _, _ = _place()
        me = 4 * x + 2 * y + c
        land[me] = in_ref[...]
        sends = []
        for d in range(1, n_dev):
            peer = (x ^ (d >> 2), y ^ ((d >> 1) & 1), c ^ (d & 1))
            cp = _rc(in_ref, land.at[me], send_sems.at[d], recv_sems.at[d], peer)
            cp.start()
            sends.append(cp)
        for d in range(1, n_dev):
            blk = land.at[me ^ d]
            _rc(blk, blk, send_sems.at[d], recv_sems.at[d], (x, y, c)).wait_recv()
        for cp in sends:
            cp.wait_send()
        tot = land[0]
        for d in range(1, n_dev):
            tot = tot + land[d]
        o_ref[...] = tot

    vm = pl.BlockSpec(memory_space=pltpu.VMEM)
    return pl.pallas_call(
        body, out_shape=_sds(buf.shape, F32), in_specs=[vm], out_specs=vm,
        scratch_shapes=[pltpu.VMEM((n_dev, r, LANES), F32), pltpu.SemaphoreType.DMA((n_dev,)),
                        pltpu.SemaphoreType.DMA((n_dev,))],
        name=name)(buf)


def rmsnorm_fwd(x, w, name):
    s, d = x.shape
    tm = _row_tile(s, 512)

    def body(x_ref, w_ref, o_ref):
        o_ref[...] = _rms_fwd(x_ref[...], w_ref[...]).astype(BF16)

    return _pcall(body, out_shape=[_sds((s, d), BF16)], grid=(s // tm,),
                  in_specs=[pl.BlockSpec((tm, d), lambda i: (i, 0)), pl.BlockSpec((1, d), lambda i: (0, 0))],
                  out_specs=[pl.BlockSpec((tm, d), lambda i: (i, 0))], args=[x, w.reshape(1, d)], name=name)[0]


def _ffn_w_spec(chip_of, single=False):
    mode = dict(pipeline_mode=pl.Buffered(1)) if single else {}
    return pl.BlockSpec((N_CORES, 1, FF_PART, D_MODEL), lambda *ids: (0, chip_of(*ids), 0, 0), **mode)


def ffn_fwd(h, x, wg, wu, wd, norm_ws, name, ride=None):
    s, d = h.shape
    n_norm = len(norm_ws)
    tm = _row_tile(s, 1024)

    def body(*refs):
        h_ref, x_ref, wg_ref, wu_ref, wd_ref = refs[:5]
        nw_refs = refs[5:5 + n_norm]
        o_ref = refs[5 + n_norm]
        h_refs = refs[6 + n_norm:6 + 2 * n_norm]
        gu_ref, acc = refs[6 + 2 * n_norm], refs[7 + 2 * n_norm]
        k = pl.program_id(1)
        hb = h_ref[...]
        g = _dot(hb, wg_ref[...].reshape(FF_SHARD, d), NT)
        u = _dot(hb, wu_ref[...].reshape(FF_SHARD, d), NT)
        gu_ref[0, 0] = g.astype(BF16)
        gu_ref[0, 1] = u.astype(BF16)

        @pl.when(k == 0)
        def _():
            acc[...] = jnp.zeros(acc.shape, F32)

        acc[...] += _dot((g * _sigmoid(g) * u).astype(BF16), wd_ref[...].reshape(FF_SHARD, d))

        @pl.when(k == N_CHIPS - 1)
        def _():
            xn = x_ref[...] + 0.5 * acc[...]
            o_ref[...] = xn
            for nw_ref, hn_ref in zip(nw_refs, h_refs):
                hn_ref[...] = _rms_fwd(xn, nw_ref[...]).astype(BF16)

    row = pl.BlockSpec((tm, d), lambda i, k: (i, 0))
    vec = pl.BlockSpec((1, d), lambda i, k: (0, 0))
    wsp = _ffn_w_spec(lambda i, k: k)
    return _pcall(
        body, out_shape=[_sds((s, d), F32)] + [_sds((s, d), BF16)] * n_norm + [_sds((N_CHIPS, 2, s, FF_SHARD), BF16)],
        grid=(s // tm, N_CHIPS),
        in_specs=[row, row, wsp, wsp, wsp] + [vec] * n_norm,
        out_specs=[row] * (1 + n_norm) + [pl.BlockSpec((1, 2, tm, FF_SHARD), lambda i, k: (k, 0, i, 0))],
        scratch_shapes=[pltpu.VMEM((tm, d), F32)],
        args=[h, x, wg, wu, wd] + [nw.reshape(1, d) for nw in norm_ws], name=name, ride=ride)


def ffn_bwd(dxn, h, x_in, nw, gu, wg, wu, wd, name, ride=None):
    s, d = h.shape
    tm = _row_tile(s, 512)
    ni = s // tm
    last_e = N_CHIPS - 1

    def body(dxn_ref, h_ref, x_ref, nw_ref, gu_ref, wg_ref, wu_ref, wd_ref,
             dx_ref, dnw_ref, dwg_ref, dwu_ref, dwd_ref, dh, wacc):
        e = pl.program_id(0)
        i = pl.program_id(1)
        dxb = dxn_ref[...].astype(BF16)
        hb = h_ref[...]
        g = gu_ref[0, 0].astype(F32)
        u = gu_ref[0, 1].astype(F32)
        rows = pl.ds(pl.multiple_of(i * tm, tm), tm)

        @pl.when(i == 0)
        def _():
            wacc[...] = jnp.zeros(wacc.shape, F32)

        @pl.when(e == 0)
        def _():
            dh[rows, :] = jnp.zeros((tm, d), F32)

        sg = _sigmoid(g)
        silu = g * sg
        wacc[2] += _dot((0.5 * silu * u).astype(BF16), dxb, TN)
        da = 0.5 * _dot(dxb, wd_ref[...].reshape(FF_SHARD, d), NT)
        dg = (da * u * (sg * (1.0 + g * (1.0 - sg)))).astype(BF16)
        wacc[0] += _dot(dg, hb, TN)
        du = (da * silu).astype(BF16)
        dh[rows, :] += _dot(dg, wg_ref[...].reshape(FF_SHARD, d))
        wacc[1] += _dot(du, hb, TN)
        dh[rows, :] += _dot(du, wu_ref[...].reshape(FF_SHARD, d))

        @pl.when(i == ni - 1)
        def _():
            for t, dw_ref in enumerate((dwg_ref, dwu_ref, dwd_ref)):
                dw_ref[...] = wacc[t].astype(BF16).reshape(N_CORES, 1, FF_PART, d)

        @pl.when(e == last_e)
        def _():
            dx, dnw = _rms_bwd(dh[rows, :], x_ref[...], nw_ref[...])
            dx_ref[...] = dxn_ref[...] + dx
            col = jnp.sum(dnw, axis=0, keepdims=True)

            @pl.when(i == 0)
            def _():
                dnw_ref[...] = col

            @pl.when(i > 0)
            def _():
                dnw_ref[...] += col

    row = pl.BlockSpec((tm, d), lambda e, i: (i, 0))
    late = pl.BlockSpec((tm, d), lambda e, i: (jnp.where(e == last_e, i, 0), 0))
    vec = pl.BlockSpec((1, d), lambda e, i: (0, 0))
    wsp = _ffn_w_spec(lambda e, i: e, single=True)
    dwsp = _ffn_w_spec(lambda e, i: e, single=True)
    dw = _sds((N_CORES, N_CHIPS, FF_PART, d), BF16)
    return _pcall(
        body, out_shape=[_sds((s, d), F32), _sds((1, d), F32), dw, dw, dw],
        grid=(N_CHIPS, ni),
        in_specs=[row, row, late, vec, pl.BlockSpec((1, 2, tm, FF_SHARD), lambda e, i: (e, 0, i, 0)), wsp, wsp, wsp],
        out_specs=[late, vec, dwsp, dwsp, dwsp],
        scratch_shapes=[pltpu.VMEM((s, d), F32), pltpu.VMEM((3, FF_SHARD, d), F32)],
        args=[dxn, h, x_in, nw.reshape(1, d), gu, wg, wu, wd], name=name, ride=ride)


def mm_res(a, w, x, name, bias=None, norm_ws=(), ride=None):
    s, k = a.shape
    n = w.shape[1]
    tm = _row_tile(s, 256)
    has_bias = bias is not None
    n_norm = len(norm_ws)

    def body(*refs):
        a_ref, w_ref, x_ref = refs[:3]
        pos = 3
        t = _dot(a_ref[...], w_ref[...])
        if has_bias:
            t = t + refs[pos][...]
            pos += 1
        nw_refs = refs[pos:pos + n_norm]
        o_ref = refs[pos + n_norm]
        h_refs = refs[pos + n_norm + 1:]
        xn = x_ref[...] + t
        o_ref[...] = xn
        for nw_ref, h_ref in zip(nw_refs, h_refs):
            h_ref[...] = _rms_fwd(xn, nw_ref[...]).astype(BF16)

    row = pl.BlockSpec((tm, n), lambda i: (i, 0))
    vec = pl.BlockSpec((1, n), lambda i: (0, 0))
    in_specs = [pl.BlockSpec((tm, k), lambda i: (i, 0)), pl.BlockSpec((k, n), lambda i: (0, 0)), row]
    args = [a, w, x]
    if has_bias:
        in_specs.append(vec)
        args.append(bias.reshape(1, n))
    for nw in norm_ws:
        in_specs.append(vec)
        args.append(nw.reshape(1, n))
    return _pcall(body, out_shape=[_sds((s, n), F32)] + [_sds((s, n), BF16)] * n_norm, grid=(s // tm,),
                  in_specs=in_specs, out_specs=[row] * (1 + n_norm), args=args, name=name, ride=ride)


def mm_nn(a, w, name, bias=None, out_dtype=F32):
    s, k = a.shape
    n = w.shape[1]
    tm = _row_tile(s, 512)
    tn = _col_tile(n)
    has_bias = bias is not None

    def body(*refs):
        a_ref, w_ref = refs[:2]
        o_ref = refs[-1]
        t = _dot(a_ref[...], w_ref[...])
        if has_bias:
            t = t + refs[2][...]
        o_ref[...] = t.astype(out_dtype)

    in_specs = [pl.BlockSpec((tm, k), lambda j, i: (i, 0)), pl.BlockSpec((k, tn), lambda j, i: (0, j))]
    args = [a, w]
    if has_bias:
        in_specs.append(pl.BlockSpec((1, tn), lambda j, i: (0, j)))
        args.append(bias.reshape(1, n))
    return _pcall(body, out_shape=[_sds((s, n), out_dtype)], grid=(n // tn, s // tm), in_specs=in_specs,
                  out_specs=[pl.BlockSpec((tm, tn), lambda j, i: (i, j))], args=args, name=name)[0]


def mm_nt(a, w, name, n=None, row0=0, out_dtype=F32, ride=None):
    s, k = a.shape
    n = w.shape[0] if n is None else n
    tm = _row_tile(s, 512)
    tn = _col_tile(n)
    base = row0 // tn
    assert row0 % tn == 0

    def body(a_ref, w_ref, o_ref):
        o_ref[...] = _dot(a_ref[...].astype(BF16), w_ref[...], NT).astype(out_dtype)

    res = _pcall(body, out_shape=[_sds((s, n), out_dtype)], grid=(n // tn, s // tm),
                 in_specs=[pl.BlockSpec((tm, k), lambda j, i: (i, 0)), pl.BlockSpec((tn, k), lambda j, i: (base + j, 0))],
                 out_specs=[pl.BlockSpec((tm, tn), lambda j, i: (i, j))], args=[a, w], name=name, ride=ride)
    return res[0] if ride is None else (res[0][0], res[1])


def mm_tn(a, b, name, into=None, rows=None, row0=0, m_valid=None, col_sum=False, ride=None):
    s, m = a.shape
    n = b.shape[1]
    mv = m if m_valid is None else m_valid
    tm = _col_tile(m) if m_valid is None else mv
    tn = n if n <= 1024 else _col_tile(n)
    rows = mv if rows is None else rows
    assert row0 % tm == 0 and (m_valid is None or m == LANES)
    assert not col_sum or mv == tm
    base = row0 // tm
    ta = m if m_valid is not None else tm

    def body(*refs):
        a_ref, b_ref = refs[0], refs[1]
        o_ref = refs[-2] if col_sum else refs[-1]
        bf = b_ref[...]
        t = _dot(a_ref[...].astype(BF16), bf.astype(BF16), TN)
        o_ref[...] = t[:tm].astype(BF16)
        if col_sum:
            refs[-1][...] = jnp.sum(bf.astype(F32), axis=0, keepdims=True)

    in_specs = [pl.BlockSpec((s, ta), lambda i, j: (0, i)), pl.BlockSpec((s, tn), lambda i, j: (0, j))]
    args = [a, b]
    aliases = None
    if into is not None:
        in_specs.append(ANY)
        args.append(into)
        aliases = {2: 0}
    out_shape = [_sds((rows, n), BF16)]
    out_specs = [pl.BlockSpec((tm, tn), lambda i, j: (base + i, j))]
    if col_sum:
        out_shape.append(_sds((1, n), F32))
        out_specs.append(pl.BlockSpec((1, tn), lambda i, j: (0, j)))
    res = _pcall(body, out_shape=out_shape, grid=(mv // tm, n // tn), in_specs=in_specs, out_specs=out_specs,
                 args=args, name=name, ride=ride, aliases=aliases)
    outs = res if ride is None else res[0]
    out = (outs[0], outs[1][0]) if col_sum else outs[0]
    return out if ride is None else (out, res[1])


def mm_rms_bwd(terms, dxn, x, nw, name, ride=None):
    s, n = x.shape
    nt_ = len(terms)
    tm = _row_tile(s, 256)
    forms = [t[5] for t in terms]

    def body(*refs):
        dxn_ref, x_ref, nw_ref, dx_ref, dnw_ref = refs[2 * nt_:]
        i = pl.program_id(0)
        dh = None
        for t in range(nt_):
            part = _dot(refs[2 * t][...].astype(BF16), refs[2 * t + 1][...], NN if forms[t] == "nn" else NT)
            dh = part if dh is None else dh + part
        dx, dnw = _rms_bwd(dh, x_ref[...], nw_ref[...])
        dx_ref[...] = dxn_ref[...] + dx
        col = jnp.sum(dnw, axis=0, keepdims=True)

        @pl.when(i == 0)
        def _():
            dnw_ref[...] = col

        @pl.when(i > 0)
        def _():
            dnw_ref[...] += col

    in_specs, args = [], []
    for a, cb, w, rb, kb, form in terms:
        in_specs.append(pl.BlockSpec((tm, kb), lambda i, cb=cb: (i, cb)))
        if form == "nn":
            in_specs.append(pl.BlockSpec((kb, n), lambda i, rb=rb: (rb, 0)))
        else:
            in_specs.append(pl.BlockSpec((n, kb), lambda i, rb=rb: (0, rb)))
        args += [a, w]
    row = pl.BlockSpec((tm, n), lambda i: (i, 0))
    vec = pl.BlockSpec((1, n), lambda i: (0, 0))
    res = _pcall(body, out_shape=[_sds((s, n), F32), _sds((1, n), F32)], grid=(s // tm,),
                 in_specs=in_specs + [row, row, vec], out_specs=[row, vec],
                 args=args + [dxn, x, nw.reshape(1, n)], name=name, ride=ride)
    outs = res if ride is None else res[0]
    out = (outs[0], outs[1][0])
    return out if ride is None else (out, res[1])


def rope_tables(s):
    pos = jnp.arange(s, dtype=F32)
    inv = 1.0 / (ROPE_THETA ** (jnp.arange(0, ATT_HEAD_DIM, 2, dtype=F32) / ATT_HEAD_DIM))
    ang = pos[:, None] * inv[None, :]
    cos = jnp.tile(jnp.cos(ang), (1, 2 * LANES // ATT_HEAD_DIM))
    sin = jnp.tile(jnp.sin(ang), (1, 2 * LANES // ATT_HEAD_DIM))
    return cos, sin


def rope_apply(t, cos, sin, name, inverse=False, scale=1.0, out_dtype=BF16):
    s, n = t.shape
    tm = _row_tile(s, 512)
    half = ATT_HEAD_DIM // 2
    reps = n // LANES

    def body(t_ref, c_ref, s_ref, o_ref):
        tf = t_ref[...].astype(F32)
        c = jnp.tile(c_ref[...], (1, reps))
        sn = jnp.tile(s_ref[...], (1, reps))
        lane = lax.broadcasted_iota(jnp.int32, tf.shape, 1)
        first = (lane & (ATT_HEAD_DIM - 1)) < half
        rot = jnp.where(first, -pltpu.roll(tf, n - half, 1), pltpu.roll(tf, half, 1))
        sign = -1.0 if inverse else 1.0
        o_ref[...] = (scale * (tf * c + sign * rot * sn)).astype(out_dtype)

    tab = pl.BlockSpec((tm, LANES), lambda i: (i, 0))
    return _pcall(body, out_shape=[_sds((s, n), out_dtype)], grid=(s // tm,),
                  in_specs=[pl.BlockSpec((tm, n), lambda i: (i, 0)), tab, tab],
                  out_specs=[pl.BlockSpec((tm, n), lambda i: (i, 0))], args=[t, cos, sin], name=name)[0]


CONV_TILE = 256


def _shift_down(u, k):
    if k == 0:
        return u
    row = lax.broadcasted_iota(jnp.int32, u.shape, 0)
    return jnp.where(row >= k, pltpu.roll(u, k, 0), 0.0)


def _shift_up(u, k):
    if k == 0:
        return u
    s = u.shape[0]
    row = lax.broadcasted_iota(jnp.int32, u.shape, 0)
    return jnp.where(row < s - k, pltpu.roll(u, s - k, 0), 0.0)


def _conv_pre(u, w_ref, b_ref):
    pre = b_ref[...] + w_ref[CONV_WIDTH - 1:CONV_WIDTH, :] * u
    for k in range(CONV_WIDTH - 1):
        pre += w_ref[k:k + 1, :] * _shift_down(u, CONV_WIDTH - 1 - k)
    return pre


def conv_fwd(u, w, b, name, ride=None):
    s, c = u.shape

    def body(u_ref, w_ref, b_ref, o_ref):
        pre = _conv_pre(u_ref[...], w_ref, b_ref)
        o_ref[...] = pre * _sigmoid(pre)

    col = pl.BlockSpec((s, CONV_TILE), lambda j: (0, j))
    res = _pcall(body, out_shape=[_sds((s, c), F32)], grid=(c // CONV_TILE,),
                 in_specs=[col, pl.BlockSpec((CONV_WIDTH, CONV_TILE), lambda j: (0, j)),
                           pl.BlockSpec((1, CONV_TILE), lambda j: (0, j))],
                 out_specs=[col], args=[u, w, b.reshape(1, c)], name=name, ride=ride)
    return res[0] if ride is None else (res[0][0], res[1])


def conv_bwd(dxs, db_, dc_, u, w, b, name):
    s, c = u.shape
    n_x = dxs.shape[1] // CONV_TILE
    n_b = db_.shape[1] // CONV_TILE

    def body(dx_ref, dbb_ref, dcc_ref, u_ref, w_ref, b_ref, du_ref, dw_ref, dbias_ref):
        j = pl.program_id(0)
        dact = jnp.where(j < n_x, dx_ref[...], jnp.where(j < n_x + n_b, dbb_ref[...], dcc_ref[...]))
        uf = u_ref[...]
        pre = _conv_pre(uf, w_ref, b_ref)
        sg = _sigmoid(pre)
        dpre = dact * (sg * (1.0 + pre * (1.0 - sg)))
        du = w_ref[CONV_WIDTH - 1:CONV_WIDTH, :] * dpre
        for k in range(CONV_WIDTH - 1):
            du += w_ref[k:k + 1, :] * _shift_up(dpre, CONV_WIDTH - 1 - k)
        du_ref[...] = du
        dbias_ref[...] = jnp.sum(dpre, axis=0, keepdims=True)
        for k in range(CONV_WIDTH):
            dw_ref[k:k + 1, :] = jnp.sum(dpre * _shift_down(uf, CONV_WIDTH - 1 - k), axis=0, keepdims=True)

    col = pl.BlockSpec((s, CONV_TILE), lambda j: (0, j))
    wsp = pl.BlockSpec((CONV_WIDTH, CONV_TILE), lambda j: (0, j))
    bsp = pl.BlockSpec((1, CONV_TILE), lambda j: (0, j))
    du, dw, db = _pcall(
        body, out_shape=[_sds((s, c), F32), _sds((CONV_WIDTH, c), F32), _sds((1, c), F32)], grid=(c // CONV_TILE,),
        in_specs=[pl.BlockSpec((s, CONV_TILE), lambda j: (0, jnp.minimum(j, n_x - 1))),
                  pl.BlockSpec((s, CONV_TILE), lambda j: (0, jnp.clip(j - n_x, 0, n_b - 1))),
                  pl.BlockSpec((s, CONV_TILE), lambda j: (0, jnp.clip(j - n_x - n_b, 0, n_b - 1))),
                  col, wsp, bsp],
        out_specs=[col, wsp, bsp], args=[dxs, db_, dc_, u, w, b.reshape(1, c)], name=name)
    return du, dw, db[0]


def _lane_pick(mat, idx):
    lane = lax.broadcasted_iota(jnp.int32, mat.shape, 1)
    return jnp.sum(jnp.where(lane == idx, mat, 0.0), axis=1, keepdims=True)


def _sub_pick(mat, idx):
    sub = lax.broadcasted_iota(jnp.int32, mat.shape, 0)
    return jnp.sum(jnp.where(sub == idx, mat, 0.0), axis=0, keepdims=True)


def _expand_heads(cols):
    rows = cols[0].shape[0]
    left = lax.broadcasted_iota(jnp.int32, (rows, LANES), 1) < SSM_HEAD_DIM
    return jnp.concatenate(
        [jnp.where(left, cols[2 * p], cols[2 * p + 1]) for p in range(HEADS_PER_GROUP // 2)], axis=1)


def _dot_01(x, ones, ones_first, pieces):
    tot, rest = None, x
    for _ in range(pieces):
        piece = rest.astype(BF16)
        rest = rest - piece.astype(F32)
        part = _dot(ones, piece) if ones_first else _dot(piece, ones)
        tot = part if tot is None else tot + part
    return tot


def _heads_to_lanes(mat, g):
    jj = lax.broadcasted_iota(jnp.int32, (GROUP_DIM, LANES), 0)
    ll = lax.broadcasted_iota(jnp.int32, (GROUP_DIM, LANES), 1)
    sel = (ll == HEADS_PER_GROUP * g + (jj >> 6)).astype(BF16)
    return _dot_01(mat, sel, False, 3)


def _softplus(x):
    return jnp.maximum(x, 0.0) + jnp.log1p(jnp.exp(-jnp.abs(x)))


def _ssd_scalars(dt_ref, bias_ref, a_ref, dtall, csall, cst):
    dta = _softplus(dt_ref[...] + bias_ref[...])
    row = lax.broadcasted_iota(jnp.int32, (CHUNK, CHUNK), 0)
    col = lax.broadcasted_iota(jnp.int32, (CHUNK, CHUNK), 1)
    cs = _dot_01(dta * a_ref[...], (row >= col).astype(BF16), True, 3)
    dtall[...] = dta
    csall[...] = cs
    cst[...] = cs.T


def _decay_mat(cs_col, cs_row):
    row = lax.broadcasted_iota(jnp.int32, (CHUNK, CHUNK), 0)
    col = lax.broadcasted_iota(jnp.int32, (CHUNK, CHUNK), 1)
    return jnp.exp(jnp.where(row >= col, cs_col - cs_row, NEG))


def _head_mask(xpair, right):
    lane = lax.broadcasted_iota(jnp.int32, xpair.shape, 1)
    keep = (lane >= SSM_HEAD_DIM) if right else (lane < SSM_HEAD_DIM)
    return jnp.where(keep, xpair, 0.0)


def _chunk_cols(x_all, g):
    return [_lane_pick(x_all, HEADS_PER_GROUP * g + r) for r in range(HEADS_PER_GROUP)]


def _decay_col(cs_cols):
    return jnp.concatenate(
        [jnp.broadcast_to(jnp.exp(cc[CHUNK - 1:CHUNK, :]), (SSM_HEAD_DIM, 1)) for cc in cs_cols], axis=0)


def ssd_fwd(act, z, dtp, bias_p, a_p, d_p, normw, name, ride=None):
    s = act.shape[0]
    nc = s // CHUNK
    b_off = D_INNER // SSM_STATE
    c_off = b_off + SSM_GROUPS

    def body(xs_ref, b_ref, c_ref, z_ref, dt_ref, bias_ref, a_ref, d_ref, nw_ref,
             yn_ref, y_ref, st_ref, state, dtall, csall, cst):
        c = pl.program_id(0)
        g = pl.program_id(1)

        @pl.when(g == 0)
        def _():
            _ssd_scalars(dt_ref, bias_ref, a_ref, dtall, csall, cst)

        @pl.when(c == 0)
        def _():
            state[g] = jnp.zeros((GROUP_DIM, SSM_STATE), F32)

        cs_cols = _chunk_cols(csall[...], g)
        dt_cols = _chunk_cols(dtall[...], g)
        cs_rows = [_sub_pick(cst[...], HEADS_PER_GROUP * g + r) for r in range(HEADS_PER_GROUP)]
        d_cols = _chunk_cols(d_ref[...], g)
        cs_exp = _expand_heads(cs_cols)
        dt_exp = _expand_heads(dt_cols)
        d_exp = _expand_heads(d_cols)
        xs = xs_ref[...]
        bb = b_ref[...].astype(BF16)
        cb16 = c_ref[...].astype(BF16)
        xdt = xs * dt_exp
        s_prev = state[g]
        st_ref[0, 0] = s_prev
        y_off = _dot(cb16, s_prev.astype(BF16), NT) * jnp.exp(cs_exp)
        decay_st = jnp.exp(cs_exp[CHUNK - 1:CHUNK, :] - cs_exp)
        contrib = _dot((xdt * decay_st).astype(BF16), bb, TN)
        state[g] = _decay_col(cs_cols) * s_prev + contrib
        cbm = _dot(cb16, bb, NT)
        pairs = []
        for p in range(HEADS_PER_GROUP // 2):
            xpair = xdt[:, LANES * p:LANES * (p + 1)]
            m0 = (cbm * _decay_mat(cs_cols[2 * p], cs_rows[2 * p])).astype(BF16)
            m1 = (cbm * _decay_mat(cs_cols[2 * p + 1], cs_rows[2 * p + 1])).astype(BF16)
            pairs.append(_dot(m0, _head_mask(xpair, False).astype(BF16))
                         + _dot(m1, _head_mask(xpair, True).astype(BF16)))
        y = jnp.concatenate(pairs, axis=1) + y_off + xs * d_exp
        y_ref[...] = y
        zf = z_ref[...]
        yg = y * (zf * _sigmoid(zf))
        yn_ref[...] = _rms_fwd(yg, nw_ref[...]).astype(BF16)

    grp = pl.BlockSpec((CHUNK, GROUP_DIM), lambda c, g: (c, g))
    par = pl.BlockSpec((1, LANES), lambda c, g: (0, 0))
    return _pcall(
        body,
        out_shape=[_sds((s, D_INNER), BF16), _sds((s, D_INNER), F32),
                   _sds((nc, SSM_GROUPS, GROUP_DIM, SSM_STATE), F32)],
        grid=(nc, SSM_GROUPS),
        in_specs=[grp,
                  pl.BlockSpec((CHUNK, SSM_STATE), lambda c, g: (c, b_off + g)),
                  pl.BlockSpec((CHUNK, SSM_STATE), lambda c, g: (c, c_off + g)),
                  grp,
                  pl.BlockSpec((CHUNK, LANES), lambda c, g: (c, 0)),
                  par, par, par,
                  pl.BlockSpec((1, GROUP_DIM), lambda c, g: (0, g))],
        out_specs=[grp, grp, pl.BlockSpec((1, 1, GROUP_DIM, SSM_STATE), lambda c, g: (c, g, 0, 0))],
        scratch_shapes=[pltpu.VMEM((SSM_GROUPS, GROUP_DIM, SSM_STATE), F32),
                        pltpu.VMEM((CHUNK, LANES), F32), pltpu.VMEM((CHUNK, LANES), F32),
                        pltpu.VMEM((LANES, CHUNK), F32)],
        args=[act, act, act, z, dtp, bias_p, a_p, d_p, normw], name=name, ride=ride)


def ssd_bwd(dyn, act, z, y_pre, states, dtp, bias_p, a_p, d_p, normw, name, ride=None):
    s = act.shape[0]
    nc = s // CHUNK
    b_off = D_INNER // SSM_STATE
    c_off = b_off + SSM_GROUPS

    def body(dyn_ref, xs_ref, b_ref, c_ref, z_ref, y_ref, st_ref, dt_ref, bias_ref, a_ref, d_ref, nw_ref,
             dxs_ref, db_ref, dc_ref, dz_ref, ddt_ref, dnw_ref, dbias_ref, da_ref, dd_ref,
             dstate, dtall, csall, cst):
        c = pl.program_id(0)
        g = pl.program_id(1)

        @pl.when(g == 0)
        def _():
            _ssd_scalars(dt_ref, bias_ref, a_ref, dtall, csall, cst)
            ddt_ref[...] = jnp.zeros((CHUNK, LANES), F32)

        @pl.when(c == 0)
        def _():
            dstate[g] = jnp.zeros((GROUP_DIM, SSM_STATE), F32)

        @pl.when(jnp.logical_and(c == 0, g == 0))
        def _():
            dnw_ref[...] = jnp.zeros(dnw_ref.shape, F32)
            dbias_ref[...] = jnp.zeros((1, LANES), F32)
            da_ref[...] = jnp.zeros((1, LANES), F32)
            dd_ref[...] = jnp.zeros((1, LANES), F32)

        cs_cols = _chunk_cols(csall[...], g)
        dt_cols = _chunk_cols(dtall[...], g)
        cs_rows = [_sub_pick(cst[...], HEADS_PER_GROUP * g + r) for r in range(HEADS_PER_GROUP)]
        d_cols = _chunk_cols(d_ref[...], g)
        cs_exp = _expand_heads(cs_cols)
        dt_exp = _expand_heads(dt_cols)
        d_exp = _expand_heads(d_cols)
        xs = xs_ref[...]
        bb = b_ref[...].astype(BF16)
        cb16 = c_ref[...].astype(BF16)
        xdt = xs * dt_exp
        s_prev = st_ref[0, 0]
        s_prev16 = s_prev.astype(BF16)
        ds_next = dstate[g]
        ds16 = ds_next.astype(BF16)

        zf = z_ref[...]
        sz = _sigmoid(zf)
        silu_z = zf * sz
        y = y_ref[...]
        yg = y * silu_z
        dout = dyn_ref[...]
        dyg, dnw = _rms_bwd(dout, yg, nw_ref[...])
        dnw_ref[pl.ds(g, 1), :] += jnp.sum(dnw, axis=0, keepdims=True)
        dy = dyg * silu_z
        dz_ref[...] = dyg * y * (sz * (1.0 + zf * (1.0 - sz)))
        dd_ref[...] += jnp.sum(_heads_to_lanes(dy * xs, g), axis=0, keepdims=True)

        exp_cs = jnp.exp(cs_exp)
        decay_st = jnp.exp(cs_exp[CHUNK - 1:CHUNK, :] - cs_exp)
        cs_t = _dot(cb16, s_prev16, NT)
        dyo = dy * exp_cs
        dc_acc = _dot(dyo.astype(BF16), s_prev16, NN)
        g1 = _dot(bb, ds16, NT)
        xds = xdt * decay_st
        db_acc = _dot(xds.astype(BF16), ds16, NN)
        dxdt_off = g1 * decay_st
        t_exp = g1 * xds
        dcs_exp = dy * cs_t * exp_cs - t_exp
        decay_c = _decay_col(cs_cols)
        dstate[g] = decay_c * ds_next + _dot(dyo.astype(BF16), cb16, TN)
        dlast_col = jnp.sum(ds_next * s_prev, axis=1, keepdims=True) * decay_c
        jj = lax.broadcasted_iota(jnp.int32, (GROUP_DIM, LANES), 0)
        ll = lax.broadcasted_iota(jnp.int32, (GROUP_DIM, LANES), 1)
        sel = ll == HEADS_PER_GROUP * g + (jj >> 6)
        dlast = jnp.sum(jnp.where(sel, dlast_col, 0.0), axis=0, keepdims=True)
        t_all = _heads_to_lanes(t_exp, g)
        dlast += jnp.sum(t_all, axis=0, keepdims=True)
        dcs_all = _heads_to_lanes(dcs_exp, g)

        cbm = _dot(cb16, bb, NT)
        dcb = jnp.zeros((CHUNK, CHUNK), F32)
        dcs_rows = jnp.zeros((LANES, CHUNK), F32)
        lane_l = lax.broadcasted_iota(jnp.int32, (CHUNK, LANES), 1)
        sub_l = lax.broadcasted_iota(jnp.int32, (LANES, CHUNK), 0)
        dxdt_pairs = []
        for p in range(HEADS_PER_GROUP // 2):
            xpair16 = xdt[:, LANES * p:LANES * (p + 1)].astype(BF16)
            dypair = dy[:, LANES * p:LANES * (p + 1)]
            acc = None
            for r in (2 * p, 2 * p + 1):
                lm = _decay_mat(cs_cols[r], cs_rows[r])
                m = cbm * lm
                dyh = _head_mask(dypair, r % 2 == 1).astype(BF16)
                dm = _dot(dyh, xpair16, NT)
                dcb += dm * lm
                q = dm * m
                idx = HEADS_PER_GROUP * g + r
                dcs_all += jnp.where(lane_l == idx, jnp.sum(q, axis=1, keepdims=True), 0.0)
                dcs_rows -= jnp.where(sub_l == idx, jnp.sum(q, axis=0, keepdims=True), 0.0)
                part = _dot(m.astype(BF16), dyh, TN)
                acc = part if acc is None else acc + part
            dxdt_pairs.append(acc)
        dxdt = jnp.concatenate(dxdt_pairs, axis=1) + dxdt_off
        dcb16 = dcb.astype(BF16)
        dc_ref[...] = dc_acc + _dot(dcb16, bb, NN)
        db_ref[...] = db_acc + _dot(dcb16, cb16, TN)
        dxs_ref[...] = dxdt * dt_exp + dy * d_exp

        dcs_all += dcs_rows.T
        row = lax.broadcasted_iota(jnp.int32, (CHUNK, CHUNK), 0)
        col = lax.broadcasted_iota(jnp.int32, (CHUNK, CHUNK), 1)
        last_row = lax.broadcasted_iota(jnp.int32, (CHUNK, LANES), 0) == CHUNK - 1
        dcs_all += jnp.where(last_row, dlast, 0.0)
        da_all = _dot_01(dcs_all, (col >= row).astype(BF16), True, 3)
        dta = dtall[...]
        in_group = jnp.logical_and(lane_l >= HEADS_PER_GROUP * g, lane_l < HEADS_PER_GROUP * (g + 1))
        ddt = jnp.where(in_group, da_all * a_ref[...] + _heads_to_lanes(dxdt * xs, g), 0.0)
        da_ref[...] += jnp.sum(jnp.where(in_group, da_all * dta, 0.0), axis=0, keepdims=True)
        ddt_raw = ddt * _sigmoid(dt_ref[...] + bias_ref[...])
        ddt_ref[...] += ddt_raw
        dbias_ref[...] += jnp.sum(ddt_raw, axis=0, keepdims=True)

    rev = lambda c, g: (nc - 1 - c, g)
    grp = pl.BlockSpec((CHUNK, GROUP_DIM), rev)
    st = pl.BlockSpec((CHUNK, SSM_STATE), rev)
    par = pl.BlockSpec((1, LANES), lambda c, g: (0, 0))
    dtb = pl.BlockSpec((CHUNK, LANES), lambda c, g: (nc - 1 - c, 0))
    f = lambda shape: _sds(shape, F32)
    return _pcall(
        body,
        out_shape=[f((s, D_INNER)), f((s, SSM_GROUPS * SSM_STATE)), f((s, SSM_GROUPS * SSM_STATE)),
                   f((s, D_INNER)), f((s, LANES)), f((8, GROUP_DIM)), f((1, LANES)), f((1, LANES)), f((1, LANES))],
        grid=(nc, SSM_GROUPS),
        in_specs=[grp, grp,
                  pl.BlockSpec((CHUNK, SSM_STATE), lambda c, g: (nc - 1 - c, b_off + g)),
                  pl.BlockSpec((CHUNK, SSM_STATE), lambda c, g: (nc - 1 - c, c_off + g)),
                  grp, grp,
                  pl.BlockSpec((1, 1, GROUP_DIM, SSM_STATE), lambda c, g: (nc - 1 - c, g, 0, 0)),
                  dtb, par, par, par,
                  pl.BlockSpec((1, GROUP_DIM), lambda c, g: (0, g))],
        out_specs=[grp, st, st, grp, dtb, pl.BlockSpec((8, GROUP_DIM), lambda c, g: (0, 0)), par, par, par],
        scratch_shapes=[pltpu.VMEM((SSM_GROUPS, GROUP_DIM, SSM_STATE), F32),
                        pltpu.VMEM((CHUNK, LANES), F32), pltpu.VMEM((CHUNK, LANES), F32),
                        pltpu.VMEM((LANES, CHUNK), F32)],
        args=[dyn, act, act, act, z, y_pre, states, dtp, bias_p, a_p, d_p, normw], name=name, ride=ride)


def _attn_probs(q, kp, kc, sink, n):
    sp = _dot(q, kp, NT)
    sc = _dot(q, kc, NT)
    i = lax.broadcasted_iota(jnp.int32, sp.shape, 0) & (WINDOW - 1)
    j = lax.broadcasted_iota(jnp.int32, sp.shape, 1)
    sp = jnp.where(jnp.logical_and(j > i, n > 0), sp, NEG)
    sc = jnp.where(j <= i, sc, NEG)
    m = jnp.maximum(jnp.maximum(jnp.max(sp, axis=1, keepdims=True), jnp.max(sc, axis=1, keepdims=True)), sink)
    pp = jnp.exp(sp - m)
    pc = jnp.exp(sc - m)
    ps = jnp.exp(sink - m)
    inv = 1.0 / (jnp.sum(pp, axis=1, keepdims=True) + jnp.sum(pc, axis=1, keepdims=True) + ps)
    return pp * inv, pc * inv, ps * inv


def attn_fwd(qt, kt, vt, sink_rows, name, ride=None):
    s = qt.shape[1]
    nb = s // WINDOW
    rows = Q_PER_KV * WINDOW

    def body(q_ref, kp_ref, kc_ref, vp_ref, vc_ref, sk_ref, o_ref):
        n = pl.program_id(1)
        q = q_ref[...].reshape(rows, ATT_HEAD_DIM)
        pp, pc, _ = _attn_probs(q, kp_ref[0], kc_ref[0], sk_ref[0], n)
        o = _dot(pp.astype(BF16), vp_ref[0]) + _dot(pc.astype(BF16), vc_ref[0])
        o_ref[...] = o.reshape(Q_PER_KV, WINDOW, ATT_HEAD_DIM).astype(BF16)

    qsp = pl.BlockSpec((Q_PER_KV, WINDOW, ATT_HEAD_DIM), lambda h, n: (h, n, 0))
    prev = pl.BlockSpec((1, WINDOW, ATT_HEAD_DIM), lambda h, n: (h, jnp.maximum(n - 1, 0), 0))
    cur = pl.BlockSpec((1, WINDOW, ATT_HEAD_DIM), lambda h, n: (h, n, 0))
    return _pcall(body, out_shape=[_sds(qt.shape, BF16)], grid=(N_KV_HEADS, nb),
                  in_specs=[qsp, prev, cur, prev, cur, pl.BlockSpec((1, rows, 1), lambda h, n: (h, 0, 0))],
                  out_specs=[qsp], args=[qt, kt, kt, vt, vt, sink_rows], name=name, ride=ride)


def attn_bwd(qt, kt, vt, sink_rows, dot_, name, ride=None):
    s = qt.shape[1]
    nb = s // WINDOW
    rows = Q_PER_KV * WINDOW

    def body(q_ref, kp_ref, kc_ref, vp_ref, vc_ref, sk_ref, do_ref, dq_ref, dk_ref, dv_ref, ds_ref, kacc, vacc):
        n = pl.program_id(1)

        @pl.when(n < nb)
        def _():
            q = q_ref[...].reshape(rows, ATT_HEAD_DIM)
            do = do_ref[...].reshape(rows, ATT_HEAD_DIM)
            kp, kc, vp, vc = kp_ref[0], kc_ref[0], vp_ref[0], vc_ref[0]
            pp, pc, ps = _attn_probs(q, kp, kc, sk_ref[0], n)
            dpp = _dot(do, vp, NT)
            dpc = _dot(do, vc, NT)
            delta = jnp.sum(pp * dpp, axis=1, keepdims=True) + jnp.sum(pc * dpc, axis=1, keepdims=True)
            dsp = (pp * (dpp - delta)).astype(BF16)
            dsc = (pc * (dpc - delta)).astype(BF16)
            dq = _dot(dsp, kp) + _dot(dsc, kc)
            dq_ref[...] = dq.reshape(Q_PER_KV, WINDOW, ATT_HEAD_DIM)
            dk_prev = _dot(dsp, q, TN)
            dv_prev = _dot(pp.astype(BF16), do, TN)

            @pl.when(n == 0)
            def _():
                dk_ref[0] = dk_prev
                dv_ref[0] = dv_prev

            @pl.when(n > 0)
            def _():
                dk_ref[0] = kacc[...] + dk_prev
                dv_ref[0] = vacc[...] + dv_prev

            kacc[...] = _dot(dsc, q, TN)
            vacc[...] = _dot(pc.astype(BF16), do, TN)
            dsk = -ps * delta
            sub = lax.broadcasted_iota(jnp.int32, (8, LANES), 0)
            tile = jnp.zeros((8, LANES), F32)
            for h in range(Q_PER_KV):
                tile += jnp.where(sub == h, jnp.sum(dsk[h * WINDOW:(h + 1) * WINDOW, :], axis=0, keepdims=True), 0.0)
            ds_ref[0, 0] = tile

        @pl.when(n == nb)
        def _():
            dk_ref[0] = kacc[...]
            dv_ref[0] = vacc[...]
            ds_ref[0, 0] = jnp.zeros((8, LANES), F32)

    last = nb - 1
    qsp = pl.BlockSpec((Q_PER_KV, WINDOW, ATT_HEAD_DIM), lambda h, n: (h, jnp.minimum(n, last), 0))
    prev = pl.BlockSpec((1, WINDOW, ATT_HEAD_DIM), lambda h, n: (h, jnp.clip(n - 1, 0, last), 0))
    cur = pl.BlockSpec((1, WINDOW, ATT_HEAD_DIM), lambda h, n: (h, jnp.minimum(n, last), 0))
    dkv = pl.BlockSpec((1, WINDOW, ATT_HEAD_DIM), lambda h, n: (h, jnp.maximum(n - 1, 0), 0))
    f = lambda shape: _sds(shape, F32)
    return _pcall(
        body, out_shape=[f(qt.shape), f(kt.shape), f(vt.shape), f((N_KV_HEADS, nb + 1, 8, LANES))],
        grid=(N_KV_HEADS, nb + 1),
        in_specs=[qsp, prev, cur, prev, cur, pl.BlockSpec((1, rows, 1), lambda h, n: (h, 0, 0)), qsp],
        out_specs=[qsp, dkv, dkv, pl.BlockSpec((1, 1, 8, LANES), lambda h, n: (h, n, 0, 0))],
        scratch_shapes=[pltpu.VMEM((WINDOW, ATT_HEAD_DIM), F32), pltpu.VMEM((WINDOW, ATT_HEAD_DIM), F32)],
        args=[qt, kt, kt, vt, vt, sink_rows, dot_], name=name, ride=ride)


def loss_head(x, w, tgt, name):
    s, d = x.shape
    tm = _row_tile(s, 256)

    def body(x_ref, w_ref, t_ref, loss_ref, dx_ref, dw_ref):
        i = pl.program_id(0)
        xf = x_ref[...]
        wv = w_ref[...]
        r = lax.rsqrt(jnp.mean(xf * xf, axis=-1, keepdims=True) + EPS)
        xhat = xf * r
        e = xhat * wv - t_ref[...]
        part = 0.5 * jnp.sum(jnp.mean(e * e, axis=-1, keepdims=True), axis=0, keepdims=True)
        dy = e * (1.0 / d)
        dxhat = dy * wv
        dx_ref[...] = r * (dxhat - xhat * jnp.mean(dxhat * xhat, axis=-1, keepdims=True))
        col = jnp.sum(dy * xhat, axis=0, keepdims=True)

        @pl.when(i == 0)
        def _():
            loss_ref[...] = jnp.broadcast_to(part, (1, LANES))
            dw_ref[...] = col

        @pl.when(i > 0)
        def _():
            loss_ref[...] += jnp.broadcast_to(part, (1, LANES))
            dw_ref[...] += col

    row = pl.BlockSpec((tm, d), lambda i: (i, 0))
    vec = pl.BlockSpec((1, d), lambda i: (0, 0))
    return _pcall(body, out_shape=[_sds((1, LANES), F32), _sds((s, d), F32), _sds((1, d), F32)], grid=(s // tm,),
                  in_specs=[row, vec, row], out_specs=[pl.BlockSpec((1, LANES), lambda i: (0, 0)), row, vec],
                  args=[x, w.reshape(1, d), tgt], name=name)


def _tile_rows(r, c, max_elems=262144, mult=16):
    best = None
    for t in range(mult, r + 1, mult):
        if r % t == 0 and t * c <= max_elems:
            best = t
    return best or r


def add_pair(xhs, ps, c_idx, name):
    n = len(xhs)
    _, r, c = xhs[0].shape
    tr = _tile_rows(r, c)

    def body(c_ref, *refs):
        for x_ref, p_ref, o_ref in zip(refs[:n], refs[n:2 * n], refs[2 * n:]):
            o_ref[...] = (x_ref[0].astype(F32) + p_ref[...].astype(F32)).astype(BF16)

    blk = pl.BlockSpec((tr, c), lambda i, cr: (i, 0))
    return pl.pallas_call(
        body, out_shape=tuple([_sds((r, c), BF16)] * n),
        grid_spec=pltpu.PrefetchScalarGridSpec(
            num_scalar_prefetch=1, grid=(r // tr,),
            in_specs=[pl.BlockSpec((1, tr, c), lambda i, cr: (cr[0], i, 0))] * n + [blk] * n,
            out_specs=tuple([blk] * n)),
        name=name, compiler_params=_cp(1))(c_idx, *xhs, *ps)


def sum_chips(qs, owns, chip_idx, name):
    n = len(qs)
    _, r, c = qs[0].shape
    tr = _tile_rows(r, c)

    def body(k_ref, *refs):
        k = k_ref[0]
        for q_ref, own_ref, o_ref in zip(refs[:n], refs[n:2 * n], refs[2 * n:]):
            mine = own_ref[0].astype(F32)
            tot = None
            for j in range(N_CHIPS):
                term = jnp.where(k == j, mine, q_ref[j].astype(F32))
                tot = term if tot is None else tot + term
            o_ref[...] = tot

    return pl.pallas_call(
        body, out_shape=tuple([_sds((r, c), F32)] * n),
        grid_spec=pltpu.PrefetchScalarGridSpec(
            num_scalar_prefetch=1, grid=(r // tr,),
            in_specs=([pl.BlockSpec((N_CHIPS, tr, c), lambda i, kr: (0, i, 0))] * n
                      + [pl.BlockSpec((1, tr, c), lambda i, kr: (kr[0], i, 0))] * n),
            out_specs=tuple([pl.BlockSpec((tr, c), lambda i, kr: (i, 0))] * n)),
        name=name, compiler_params=_cp(1))(chip_idx, *qs, *owns)


def adamw(w, g, m, v, name, row0=0, into=None, ride=None):
    r, c = w.shape
    nr = g.shape[0]
    tr = _tile_rows(math.gcd(nr, row0) if row0 else nr, c, max_elems=131072, mult=8)
    base = row0 // tr
    c1 = 1.0 / (1.0 - ADAM_B1 ** ADAM_STEP)
    c2 = 1.0 / (1.0 - ADAM_B2 ** ADAM_STEP)

    def body(*refs):
        w_ref, g_ref, m_ref, v_ref = refs[:4]
        d_ref, mo_ref, vo_ref, go_ref = refs[-4:]
        gf = g_ref[...]
        mn = ADAM_B1 * m_ref[...] + (1.0 - ADAM_B1) * gf
        vn = ADAM_B2 * v_ref[...] + (1.0 - ADAM_B2) * (gf * gf)
        mo_ref[...] = mn
        vo_ref[...] = vn
        go_ref[...] = gf
        d_ref[...] = -ADAM_LR * ((mn * c1) / (jnp.sqrt(vn * c2) + ADAM_EPS) + ADAM_WD * w_ref[...])

    blk = pl.BlockSpec((tr, c), lambda i: (base + i, 0))
    in_specs = [blk, pl.BlockSpec((tr, c), lambda i: (i, 0)), blk, blk]
    args = [w, g, m, v]
    aliases = None
    if into is not None:
        in_specs += [ANY] * 4
        args += list(into)
        aliases = {4 + t: t for t in range(4)}
    out = _sds((r, c), F32)
    return _pcall(body, out_shape=[out] * 4, grid=(nr // tr,), in_specs=in_specs, out_specs=[blk] * 4,
                  args=args, name=name, ride=ride, aliases=aliases)


WEIGHTS = ['norm_w', 'ffn_w_gate', 'ffn_w_up', 'ffn_w_down', 'ssm_w_in', 'ssm_conv_w', 'ssm_conv_b', 'ssm_dt_bias',
           'ssm_a_log', 'ssm_d', 'ssm_norm_w', 'ssm_w_out', 'kv_norm_w', 'w_k', 'b_k', 'w_v', 'b_v', 'attn_w_q',
           'attn_b_q', 'attn_sinks', 'attn_w_o', 'attn_b_o', 'final_norm_w']
BIG = ['ffn_w_gate', 'ffn_w_up', 'ffn_w_down', 'ssm_w_in', 'ssm_w_out', 'w_k', 'w_v', 'attn_w_q', 'attn_w_o']
TRANSPOSED = ('ffn_w_gate', 'ffn_w_up', 'ssm_w_in')
SMALL = [n for n in WEIGHTS if n not in BIG]
SMALL_SHARDED = {'norm_w': 2, 'ssm_conv_w': 2, 'ssm_conv_b': 1, 'ssm_norm_w': 1}
ROW_ALIGN = 8 * LANES


def _pack_rows(parts):
    flat = jnp.concatenate([p.reshape(-1).astype(F32) for p in parts])
    pad = (-flat.size) % ROW_ALIGN
    return jnp.pad(flat, (0, pad)).reshape(-1, LANES)


def _unpack_rows(buf, shapes):
    flat = buf.reshape(-1)
    out, pos = [], 0
    for shp in shapes:
        size = math.prod(shp)
        out.append(flat[pos:pos + size].reshape(shp))
        pos += size
    return out


def _as2d(a):
    return a.reshape(-1, a.shape[-1])


def _heads_major(t, n_heads):
    s = t.shape[0]
    return t.reshape(s, n_heads, ATT_HEAD_DIM).transpose(1, 0, 2)


def _tokens_major(t):
    h, s, dh = t.shape
    return t.transpose(1, 0, 2).reshape(s, h * dh)


def _pad_lanes(v):
    return jnp.pad(v.reshape(1, -1), ((0, 0), (0, LANES - v.size)))


def _chips_first(t):
    return t.swapaxes(0, 1).reshape((-1,) + t.shape[3:])


def _parts_first(t, rows):
    return t.reshape((N_CHIPS, N_CORES, rows) + t.shape[1:]).swapaxes(0, 1)


def kernel(*args):
    names = (['x'] + WEIGHTS + ['loss_target'] + ['m_' + n for n in WEIGHTS] + ['v_' + n for n in WEIGHTS])
    a = dict(zip(names, args))
    for n in TRANSPOSED:
        for pre in ('', 'm_', 'v_'):
            a[pre + n] = a[pre + n].swapaxes(-1, -2)
    xi, yi, ci = lax.axis_index("x"), lax.axis_index("y"), lax.axis_index("c")
    chip = 2 * xi + yi
    south = ci == 0
    c_idx = jnp.reshape(ci, (1,)).astype(jnp.int32)
    chip_idx = jnp.reshape(chip, (1,)).astype(jnp.int32)
    x0 = a['x'][0]
    s = x0.shape[0]
    cos, sin = rope_tables(s)

    def own_slot(full, mine):
        return lax.dynamic_update_slice_in_dim(full, mine[:, None], chip, axis=1)

    def ffn_shard(l, i):
        return [a[n][l, i].astype(BF16).reshape(N_CORES, FF_PART, D_MODEL)
                for n in ('ffn_w_gate', 'ffn_w_up', 'ffn_w_down')]

    def own_slots(fulls, mines):
        return [own_slot(f, m) for f, m in zip(fulls, mines)]
    w_in_sh = jnp.pad(a['ssm_w_in'][0], ((0, IN_SHARD_PAD - IN_SHARD), (0, 0))).astype(BF16).reshape(
        N_CORES, IN_SHARD_PAD // 2, D_MODEL)
    w_out_sh = a['ssm_w_out'][0].astype(BF16).reshape(N_CORES, 256, D_MODEL)
    attn_sh = jnp.stack([a['attn_w_q'][0], a['attn_w_o'][0]]).astype(BF16)
    kv_sh = jnp.stack([a['w_k'], a['w_v']]).astype(BF16)
    small_names = list(SMALL_SHARDED)
    small_sh = _pack_rows([a[n] for n in small_names])
    small_sh = small_sh.reshape(N_CORES, small_sh.shape[0] // 2, LANES)

    sh00, sh01, sh10, sh11 = ffn_shard(0, 0), ffn_shard(0, 1), ffn_shard(1, 0), ffn_shard(1, 1)
    first = run_exchange(gather_chips(sh00 + [small_sh]), "gather_first")
    w00 = own_slots(first[:3], sh00)
    smalls = own_slot(first[3], small_sh)
    p = {}
    per_chip = [_unpack_rows(smalls[:, k], [a[n].shape for n in small_names]) for k in range(N_CHIPS)]
    for idx, n in enumerate(small_names):
        p[n] = jnp.concatenate([per_chip[k][idx] for k in range(N_CHIPS)], axis=SMALL_SHARDED[n])
    nw = p['norm_w']
    conv_w, conv_b, ssm_nw = p['ssm_conv_w'][0], p['ssm_conv_b'][0], p['ssm_norm_w'][0].reshape(1, D_INNER)

    h00 = rmsnorm_fwd(x0, nw[0, 0], "norm_in")
    (x1, h01, gu00), (w_in_g, kv_g) = ffn_fwd(h00, x0, *w00, [nw[0, 1]], "ffn_fwd_00",
                                              ride=gather_chips([w_in_sh, kv_sh]))
    w_in_t = _chips_first(own_slot(w_in_g, w_in_sh)).reshape(N_CHIPS, IN_SHARD_PAD, D_MODEL)[:, :IN_SHARD].reshape(
        IN_PROJ_DIM, D_MODEL)
    w_dt_t = jnp.pad(w_in_t[D_INNER + CONV_DIM:], ((0, LANES - SSM_HEADS), (0, 0)))
    kv_g = own_slot(kv_g, kv_sh)
    w_k, w_v = kv_g[0].reshape(D_MODEL, KV_DIM), kv_g[1].reshape(D_MODEL, KV_DIM)

    zz = mm_nt(h01, w_in_t, "ssm_in_z", n=D_INNER)
    xbc, (w_out_g,) = mm_nt(h01, w_in_t, "ssm_in_xbc", n=CONV_DIM, row0=D_INNER, ride=gather_chips([w_out_sh]))
    w_out = _chips_first(own_slot(w_out_g, w_out_sh))
    dtp = mm_nt(h01, w_dt_t, "ssm_in_dt")
    act, (wg01,) = conv_fwd(xbc, conv_w, conv_b, "ssm_conv", ride=gather_chips(sh01[:1]))
    bias_p = _pad_lanes(a['ssm_dt_bias'][0])
    a_p = _pad_lanes(-jnp.exp(a['ssm_a_log'][0]))
    d_p = _pad_lanes(a['ssm_d'][0])
    (yn, y_pre, states), (wu01, wd01) = ssd_fwd(act, zz, dtp, bias_p, a_p, d_p, ssm_nw, "ssd_fwd",
                                                ride=gather_chips(sh01[1:]))
    w01 = own_slots([wg01, wu01, wd01], sh01)
    (x2, h02), (wg10,) = mm_res(yn, w_out, x1, "ssm_out", norm_ws=[nw[0, 2]], ride=gather_chips(sh10[:1]))
    (x3, hkv, h10, gu01), (wu10, wd10) = ffn_fwd(h02, x2, *w01, [a['kv_norm_w'], nw[1, 0]], "ffn_fwd_01",
                                                 ride=gather_chips(sh10[1:]))
    w10 = own_slots([wg10, wu10, wd10], sh10)

    k_rot = rope_apply(mm_nn(hkv, w_k, "kv_k", bias=a['b_k']), cos, sin, "rope_k")
    v = mm_nn(hkv, w_v, "kv_v", bias=a['b_v'], out_dtype=BF16)
    kt = _heads_major(k_rot, N_KV_HEADS)
    vt = _heads_major(v, N_KV_HEADS)

    (x4, h11, gu10), (attn_g, wg11) = ffn_fwd(h10, x3, *w10, [nw[1, 1]], "ffn_fwd_10",
                                              ride=gather_chips([attn_sh, sh11[0]]))
    attn_g = own_slot(attn_g, attn_sh)
    w_q, w_o = attn_g[0].reshape(D_MODEL, D_MODEL), attn_g[1].reshape(D_MODEL, D_MODEL)
    scale = 1.0 / math.sqrt(ATT_HEAD_DIM)
    q_rot = rope_apply(mm_nn(h11, w_q, "attn_q", bias=a['attn_b_q'][0]), cos, sin, "rope_q", scale=scale)
    qt = _heads_major(q_rot, N_Q_HEADS)
    sink_rows = jnp.repeat(a['attn_sinks'][0].reshape(N_KV_HEADS, Q_PER_KV), WINDOW, axis=1).reshape(
        N_KV_HEADS, Q_PER_KV * WINDOW, 1)
    (ot,), (wu11, wd11) = attn_fwd(qt, kt, vt, sink_rows, "attn_fwd", ride=gather_chips(sh11[1:]))
    w11 = own_slots([wg11, wu11, wd11], sh11)
    o = _tokens_major(ot)
    x5, h12 = mm_res(o, w_o, x4, "attn_out", bias=a['attn_b_o'][0], norm_ws=[nw[1, 2]])
    x6, gu11 = ffn_fwd(h12, x5, *w11, [], "ffn_fwd_11")

    loss_v, dx6, d_final = loss_head(x6, a['final_norm_w'], a['loss_target'][0], "loss_head")
    loss = lax.psum(loss_v[0, 0], ("x", "y", "c"))
    g = {'final_norm_w': d_final[0]}

    def same_shape(xs, ys):
        runs = []
        for xv, yv in zip(xs, ys):
            if runs and runs[-1][0][0].shape == xv.shape:
                runs[-1][0].append(xv)
                runs[-1][1].append(yv)
            else:
                runs.append(([xv], [yv]))
        return runs

    def pre_reduce(grads, sib, tag):
        out = []
        for idx, (grp, sbs) in enumerate(same_shape(grads, list(sib))):
            ts = add_pair([gr.reshape(2, -1, gr.shape[-1]) for gr in grp], [_as2d(sb) for sb in sbs], c_idx,
                          "rs_add_%s_%d" % (tag, idx))
            out += [t.reshape(gr.shape[1:]) for t, gr in zip(ts, grp)]
        return out

    def chip_sum(landed, parts, tag):
        out = []
        for idx, (qs, owns) in enumerate(same_shape(list(landed), parts)):
            ts = sum_chips([q.reshape(N_CHIPS, -1, q.shape[-1]) for q in qs],
                           [own.reshape(N_CHIPS, -1, own.shape[-1]) for own in owns], chip_idx,
                           "rs_sum_%s_%d" % (tag, idx))
            out += [t.reshape(q.shape[1:]) for t, q in zip(ts, qs)]
        return out

    dnw = [[None] * 3 for _ in range(2)]
    sums = {}

    def trade(key):
        return swap_cores(sums[key], False)

    dx5, dnw12, *g11 = ffn_bwd(dx6, h12, x5, nw[1, 2], gu11, *w11, "ffn_bwd_11")
    dnw[1][2] = dnw12[0]
    (d_wo, g['attn_b_o']), sib11 = mm_tn(o, dx5, "attn_dwo", col_sum=True, ride=swap_cores(g11, True))
    t11 = pre_reduce(g11, sib11, "11")
    do = mm_nt(dx5, w_o, "attn_do", out_dtype=BF16)
    (dqt, dkt, dvt, dsink), land11 = attn_bwd(qt, kt, vt, sink_rows, _heads_major(do, N_Q_HEADS), "attn_bwd",
                                             ride=scatter_chips(t11))
    sums['11'] = chip_sum(land11, t11, "11")
    g['attn_sinks'] = jnp.sum(dsink[:, :, :Q_PER_KV, 0], axis=1).reshape(N_Q_HEADS)
    dq_pre = rope_apply(_tokens_major(dqt), cos, sin, "rope_dq", inverse=True, scale=scale, out_dtype=F32)
    d_wq, g['attn_b_q'] = mm_tn(h11, dq_pre, "attn_dwq", col_sum=True)
    g_attn = [jnp.stack([d_wq.reshape(N_CHIPS, 256, D_MODEL), d_wo.reshape(N_CHIPS, 256, D_MODEL)])]
    (dx4, dnw[1][1]), sib_attn = mm_rms_bwd([(dq_pre, 0, w_q, 0, D_MODEL, "nt")], dx5, x4, nw[1, 1], "attn_bwd_dh",
                                            ride=swap_cores(g_attn, True))
    t_attn = pre_reduce(g_attn, sib_attn, "attn")
    (dx3, dnw10, *g10), landed = ffn_bwd(dx4, h10, x3, nw[1, 0], gu10, *w10, "ffn_bwd_10",
                                         ride=join(scatter_chips(t_attn), trade('11')))
    dnw[1][0] = dnw10[0]
    sums['attn'] = chip_sum(landed[:1], t_attn, "attn")
    theirs = {'11': landed[1:]}
    dk_pre = rope_apply(_tokens_major(dkt), cos, sin, "rope_dk", inverse=True, out_dtype=F32)
    dv = _tokens_major(dvt)
    (d_wk, g['b_k']), sib10 = mm_tn(hkv, dk_pre, "kv_dwk", col_sum=True, ride=swap_cores(g10, True))
    t10 = pre_reduce(g10, sib10, "10")
    d_wv, g['b_v'] = mm_tn(hkv, dv, "kv_dwv", col_sum=True)
    g_kv = [jnp.stack([d_wk.reshape(N_CHIPS, 256, KV_DIM), d_wv.reshape(N_CHIPS, 256, KV_DIM)])]
    (dx3, g['kv_norm_w']), sib_kv = mm_rms_bwd(
        [(dk_pre, 0, w_k, 0, KV_DIM, "nt"), (dv, 0, w_v, 0, KV_DIM, "nt")], dx3, x3, a['kv_norm_w'], "kv_bwd_dh",
        ride=swap_cores(g_kv, True))
    t_kv = pre_reduce(g_kv, sib_kv, "kv")
    (dx2, dnw02, *g01), landed = ffn_bwd(dx3, h02, x2, nw[0, 2], gu01, *w01, "ffn_bwd_01",
                                         ride=join(scatter_chips(t10 + t_kv), trade('attn')))
    dnw[0][2] = dnw02[0]
    sums['10'] = chip_sum(landed[:3], t10, "10")
    sums['kv'] = chip_sum(landed[3:4], t_kv, "kv")
    theirs['attn'] = landed[4:]
    d_wout, sib01 = mm_tn(yn, dx2, "ssm_dwout", ride=swap_cores(g01, True))
    t01 = pre_reduce(g01, sib01, "01")
    dyn = mm_nt(dx2, w_out, "ssm_dyn")
    (dxs, db_, dc_, dz, ddt, d_ssm_nw, d_bias, d_a, d_d), landed = ssd_bwd(
        dyn, act, zz, y_pre, states, dtp, bias_p, a_p, d_p, ssm_nw, "ssd_bwd",
        ride=join(scatter_chips(t01), trade('10'), trade('kv')))
    sums['01'] = chip_sum(landed[:3], t01, "01")
    theirs['10'], theirs['kv'] = landed[3:6], landed[6:]
    g['ssm_norm_w'] = d_ssm_nw[:SSM_GROUPS].reshape(D_INNER)
    g['ssm_dt_bias'] = d_bias[0, :SSM_HEADS]
    g['ssm_a_log'] = d_a[0, :SSM_HEADS] * a_p[0, :SSM_HEADS]
    g['ssm_d'] = d_d[0, :SSM_HEADS]
    dxbc, g['ssm_conv_w'], g['ssm_conv_b'] = conv_bwd(dxs, db_, dc_, xbc, conv_w, conv_b, "ssm_conv_bwd")
    d_win = mm_tn(dz, h01, "ssm_dwz", rows=IN_PROJ_DIM)
    d_win = mm_tn(dxbc, h01, "ssm_dwxbc", into=d_win, rows=IN_PROJ_DIM, row0=D_INNER)
    d_win = mm_tn(ddt, h01, "ssm_dwdt", into=d_win, rows=IN_PROJ_DIM, row0=D_INNER + CONV_DIM, m_valid=SSM_HEADS)
    d_win = jnp.pad(d_win.reshape(N_CHIPS, IN_SHARD, D_MODEL), ((0, 0), (0, IN_SHARD_PAD - IN_SHARD), (0, 0)))
    g_ssm = [_parts_first(d_win.reshape(-1, D_MODEL), IN_SHARD_PAD // 2), _parts_first(d_wout, 256)]
    kb = 1024
    terms = ([(dz, j, w_in_t, j, kb, "nn") for j in range(D_INNER // kb)]
             + [(dxbc, j, w_in_t, D_INNER // kb + j, kb, "nn") for j in range(CONV_DIM // kb)]
             + [(ddt, 0, w_dt_t, 0, LANES, "nn")])
    (dx1, dnw[0][1]), sib_ssm = mm_rms_bwd(terms, dx2, x1, nw[0, 1], "ssm_bwd_dh", ride=swap_cores(g_ssm, True))
    t_ssm = pre_reduce(g_ssm, sib_ssm, "ssm")
    (grad_x, dnw00, *g00), landed = ffn_bwd(dx1, h00, x0, nw[0, 0], gu00, *w00, "ffn_bwd_00",
                                            ride=join(scatter_chips(t_ssm), trade('01')))
    dnw[0][0] = dnw00[0]
    sums['ssm'] = chip_sum(landed[:2], t_ssm, "ssm")
    theirs['01'] = landed[2:]
    landed = run_exchange(join(swap_cores(g00, True), trade('ssm')), "rs_swap_00")
    t00 = pre_reduce(g00, landed[:3], "00")
    theirs['ssm'] = landed[3:]

    def both(key):
        return [(jnp.where(south, m_, t_), jnp.where(south, t_, m_)) for m_, t_ in zip(sums[key], theirs[key])]

    delta, new_m, new_v, gw = {}, {}, {}, {}
    ffn_names = ('ffn_w_gate', 'ffn_w_up', 'ffn_w_down')
    rest = [both(key) for key in ('01', '10', '11')]
    partial, land00 = {}, []
    for t, n in enumerate(ffn_names):
        g_rows = jnp.concatenate([piece for blk in rest for piece in blk[t]], axis=0)
        partial[n], landed = adamw(_as2d(a[n]), g_rows, _as2d(a['m_' + n]), _as2d(a['v_' + n]), "adamw_rest_" + n,
                                   row0=FF_SHARD, ride=scatter_chips(t00[t:t + 1]))
        land00 += list(landed)
    sums['00'] = chip_sum(land00, t00, "00")
    theirs['00'] = run_exchange(trade('00'), "rs_trade_00")
    for n, pieces in zip(ffn_names, both('00')):
        outs = adamw(_as2d(a[n]), jnp.concatenate(pieces, axis=0), _as2d(a['m_' + n]), _as2d(a['v_' + n]),
                     "adamw_first_" + n, into=partial[n])
        delta[n], new_m[n], new_v[n], gw[n] = [z.reshape(a[n].shape) for z in outs]

    full = {key: both(key) for key in ('attn', 'kv', 'ssm')}
    lo, hi = full['attn'][0]
    gw['attn_w_q'], gw['attn_w_o'] = lo[None], hi[None]
    lo, hi = full['kv'][0]
    gw['w_k'], gw['w_v'] = lo, hi
    lo, hi = full['ssm'][0]
    gw['ssm_w_in'] = jnp.concatenate([lo, hi], axis=0)[:IN_SHARD][None]
    lo, hi = full['ssm'][1]
    gw['ssm_w_out'] = jnp.concatenate([lo, hi], axis=0)[None]

    g['norm_w'] = jnp.stack([jnp.stack(r) for r in dnw])
    red = all_reduce_small(_pack_rows([g[n] for n in SMALL]), "reduce_vectors")
    for n, t in zip(SMALL, _unpack_rows(red, [g[n].shape for n in SMALL])):
        if n in SMALL_SHARDED:
            ax = SMALL_SHARDED[n] - (a[n].ndim - t.ndim)
            width = a[n].shape[SMALL_SHARDED[n]]
            t = lax.dynamic_slice_in_dim(t, chip * width, width, axis=ax)
        gw[n] = t.reshape(a[n].shape)

    for n in BIG:
        if n in ffn_names:
            continue
        d, mo, vo, _ = adamw(_as2d(a[n]), _as2d(gw[n]), _as2d(a['m_' + n]), _as2d(a['v_' + n]), "adamw_" + n)
        delta[n], new_m[n], new_v[n] = d.reshape(a[n].shape), mo.reshape(a[n].shape), vo.reshape(a[n].shape)
    shapes = [a[n].shape for n in SMALL]
    packed = [_pack_rows([src[n] for n in SMALL]) for src in
              (a, gw, {n: a['m_' + n] for n in SMALL}, {n: a['v_' + n] for n in SMALL})]
    outs = adamw(*packed, "adamw_vectors")
    for dst, buf in zip((delta, new_m, new_v), outs):
        for n, t in zip(SMALL, _unpack_rows(buf, shapes)):
            dst[n] = t
    for n in TRANSPOSED:
        for dst in (gw, delta, new_m, new_v):
            dst[n] = dst[n].swapaxes(-1, -2)

    return (loss, grad_x[None], *[gw[n] for n in WEIGHTS], *[delta[n] for n in WEIGHTS],
            *[new_m[n] for n in WEIGHTS], *[new_v[n] for n in WEIGHTS])
```

```python
import math

import jax
import jax.numpy as jnp
from jax import lax
from jax.experimental import pallas as pl
from jax.experimental.pallas import tpu as pltpu

F32 = jnp.float32
BF16 = jnp.bfloat16

D_MODEL = 1024
D_INNER = 2048
SSM_HEADS = 32
SSM_GROUPS = 4
HEADS_PER_GROUP = SSM_HEADS // SSM_GROUPS
SSM_HEAD_DIM = 64
SSM_STATE = 128
GROUP_DIM = D_INNER // SSM_GROUPS
CONV_DIM = D_INNER + 2 * SSM_GROUPS * SSM_STATE
CONV_WIDTH = 4
CHUNK = 128
ATT_HEAD_DIM = 64
N_Q_HEADS = 16
N_KV_HEADS = 4
Q_PER_KV = N_Q_HEADS // N_KV_HEADS
KV_DIM = N_KV_HEADS * ATT_HEAD_DIM
WINDOW = 128
ROPE_THETA = 10000.0
D_FF = 2816
N_CHIPS = 4
N_CORES = 2
FF_SHARD = D_FF // N_CHIPS
FF_PART = FF_SHARD // N_CORES
IN_PROJ_DIM = D_INNER + CONV_DIM + SSM_HEADS
IN_SHARD = IN_PROJ_DIM // N_CHIPS
IN_SHARD_PAD = 1312
EPS = 1e-5
NEG = -1e30
LANES = 128
VMEM_LIMIT = 56 * 1024 * 1024

ADAM_LR = 0.001
ADAM_B1 = 0.9
ADAM_B2 = 0.999
ADAM_EPS = 1e-08
ADAM_WD = 0.01
ADAM_STEP = 10

NN = ((1,), (0,))
NT = ((1,), (1,))
TN = ((0,), (0,))
MESH = pl.DeviceIdType.MESH
ANY = pl.BlockSpec(memory_space=pl.ANY)


def _dot(a, b, dims=NN, precision=None):
    return lax.dot_general(a, b, (dims, ((), ())), preferred_element_type=F32, precision=precision)


def _cp(n_grid):
    return pltpu.CompilerParams(dimension_semantics=("arbitrary",) * n_grid, vmem_limit_bytes=VMEM_LIMIT)


def _sigmoid(x):
    return 1.0 / (1.0 + jnp.exp(-x))


def _rms_fwd(xf, w):
    r = lax.rsqrt(jnp.mean(xf * xf, axis=-1, keepdims=True) + EPS)
    return xf * r * w


def _rms_bwd(dh, xf, w):
    r = lax.rsqrt(jnp.mean(xf * xf, axis=-1, keepdims=True) + EPS)
    xhat = xf * r
    dxhat = dh * w
    dx = r * (dxhat - xhat * jnp.mean(dxhat * xhat, axis=-1, keepdims=True))
    return dx, dh * xhat


def _row_tile(s, pref):
    return pref if s % pref == 0 else s


def _col_tile(n):
    for t in (1024, 768, 512, 256, 128):
        if n % t == 0:
            return t
    return n


def _sds(shape, dtype):
    return jax.ShapeDtypeStruct(tuple(shape), dtype)


class Exchange:
    def __init__(self, ins, out_shapes, sems, start, finish):
        self.ins, self.out_shapes, self.sems, self.start, self.finish = ins, out_shapes, sems, start, finish


def _place():
    x, y, c = lax.axis_index("x"), lax.axis_index("y"), lax.axis_index("c")
    others = [(1 - x, y), (x, 1 - y), (1 - x, 1 - y)]
    return x, y, c, 2 * x + y, others


def _rc(src, dst, send_sem, recv_sem, dev):
    return pltpu.make_async_remote_copy(src_ref=src, dst_ref=dst, send_sem=send_sem, recv_sem=recv_sem,
                                        device_id=dev, device_id_type=MESH)


def gather_chips(arrs):
    n = len(arrs)

    def copies(ins, outs, sems):
        send, recv = sems
        x, y, c, k, others = _place()
        ici, land, fwd, fland = [], [], [], []
        for a in range(n):
            for j, (px, py) in enumerate(others):
                ici.append(_rc(ins[a].at[c], outs[a].at[c, k], send.at[a, j], recv.at[a, j], (px, py, c)))
                blk = outs[a].at[c, 2 * px + py]
                land.append(_rc(blk, blk, send.at[a, j], recv.at[a, j], (px, py, c)))
                fwd.append(_rc(blk, blk, send.at[a, 3 + j], recv.at[a, 3 + j], (x, y, 1 - c)))
                blk2 = outs[a].at[1 - c, 2 * px + py]
                fland.append(_rc(blk2, blk2, send.at[a, 3 + j], recv.at[a, 3 + j], (x, y, 1 - c)))
        return ici, land, fwd, fland

    def start(ins, outs, sems):
        for cp in copies(ins, outs, sems)[0]:
            cp.start()

    def finish(ins, outs, sems):
        ici, land, fwd, fland = copies(ins, outs, sems)
        for arrived, onward in zip(land, fwd):
            arrived.wait_recv()
            onward.start()
        for arrived in fland:
            arrived.wait_recv()
        for cp in ici + fwd:
            cp.wait_send()

    return Exchange(list(arrs), [_sds((2, N_CHIPS) + a.shape[1:], a.dtype) for a in arrs],
                    [pltpu.SemaphoreType.DMA((n, 6)), pltpu.SemaphoreType.DMA((n, 6))], start, finish)


def scatter_chips(arrs):
    n = len(arrs)

    def copies(ins, outs, sems):
        send, recv = sems
        x, y, c, k, others = _place()
        out, land = [], []
        for a in range(n):
            for j, (px, py) in enumerate(others):
                out.append(_rc(ins[a].at[2 * px + py], outs[a].at[k], send.at[a, j], recv.at[a, j], (px, py, c)))
                blk = outs[a].at[2 * px + py]
                land.append(_rc(blk, blk, send.at[a, j], recv.at[a, j], (px, py, c)))
        return out, land

    def start(ins, outs, sems):
        for cp in copies(ins, outs, sems)[0]:
            cp.start()

    def finish(ins, outs, sems):
        out, land = copies(ins, outs, sems)
        for arrived in land:
            arrived.wait_recv()
        for cp in out:
            cp.wait_send()

    return Exchange(list(arrs), [_sds(a.shape, a.dtype) for a in arrs],
                    [pltpu.SemaphoreType.DMA((n, 3)), pltpu.SemaphoreType.DMA((n, 3))], start, finish)


def swap_cores(arrs, pick_other):
    n = len(arrs)

    def copies(ins, outs, sems):
        send, recv = sems
        x, y, c, _, _ = _place()
        return [_rc(ins[a].at[1 - c] if pick_other else ins[a], outs[a], send.at[a], recv.at[a], (x, y, 1 - c))
                for a in range(n)]

    def start(ins, outs, sems):
        for cp in copies(ins, outs, sems):
            cp.start()

    def finish(ins, outs, sems):
        for cp in copies(ins, outs, sems):
            cp.wait()

    shapes = [_sds(a.shape[1:] if pick_other else a.shape, a.dtype) for a in arrs]
    return Exchange(list(arrs), shapes, [pltpu.SemaphoreType.DMA((n,)), pltpu.SemaphoreType.DMA((n,))],
                    start, finish)


def join(*parts):
    parts = [p for p in parts if p is not None]
    if not parts:
        return None

    def split(refs, counts):
        out, pos = [], 0
        for cnt in counts:
            out.append(refs[pos:pos + cnt])
            pos += cnt
        return out

    n_in = [len(p.ins) for p in parts]
    n_out = [len(p.out_shapes) for p in parts]
    n_sem = [len(p.sems) for p in parts]

    def run(which):
        def go(ins, outs, sems):
            for p, i, o, s in zip(parts, split(ins, n_in), split(outs, n_out), split(sems, n_sem)):
                getattr(p, which)(i, o, s)
        return go

    return Exchange([a for p in parts for a in p.ins], [s for p in parts for s in p.out_shapes],
                    [s for p in parts for s in p.sems], run("start"), run("finish"))


def _pcall(body, *, out_shape, grid, in_specs, out_specs, args, name, scratch_shapes=(), ride=None, aliases=None):
    out_shape, out_specs, in_specs = tuple(out_shape), tuple(out_specs), list(in_specs)
    aliases = aliases or {}
    if ride is None:
        return pl.pallas_call(body, out_shape=out_shape, grid=grid, in_specs=in_specs, out_specs=out_specs,
                              scratch_shapes=list(scratch_shapes), input_output_aliases=aliases, name=name,
                              compiler_params=_cp(len(grid)))(*args)
    n_in, n_out, n_sc = len(args), len(out_shape), len(scratch_shapes)
    n_xi, n_xo = len(ride.ins), len(ride.out_shapes)

    def wrapped(*refs):
        pos = [0]

        def take(cnt):
            got = refs[pos[0]:pos[0] + cnt]
            pos[0] += cnt
            return got

        c_in, x_in, c_out, x_out, c_sc = take(n_in), take(n_xi), take(n_out), take(n_xo), take(n_sc)
        sems = refs[pos[0]:]
        first, last = True, True
        for d, size in enumerate(grid):
            first = jnp.logical_and(first, pl.program_id(d) == 0)
            last = jnp.logical_and(last, pl.program_id(d) == size - 1)

        @pl.when(first)
        def _():
            ride.start(x_in, x_out, sems)

        body(*c_in, *c_out, *c_sc)

        @pl.when(last)
        def _():
            ride.finish(x_in, x_out, sems)

    res = pl.pallas_call(
        wrapped, out_shape=out_shape + tuple(ride.out_shapes), grid=grid,
        in_specs=in_specs + [ANY] * n_xi, out_specs=out_specs + (ANY,) * n_xo,
        scratch_shapes=list(scratch_shapes) + list(ride.sems), input_output_aliases=aliases, name=name,
        compiler_params=_cp(len(grid)))(*args, *ride.ins)
    return res[:n_out], res[n_out:]


def run_exchange(ex, name):
    n_xi, n_xo = len(ex.ins), len(ex.out_shapes)

    def body(*refs):
        ins, outs, sems = refs[:n_xi], refs[n_xi:n_xi + n_xo], refs[n_xi + n_xo:]
        ex.start(ins, outs, sems)
        ex.finish(ins, outs, sems)

    return pl.pallas_call(body, out_shape=tuple(ex.out_shapes), in_specs=[ANY] * n_xi, out_specs=(ANY,) * n_xo,
                          scratch_shapes=list(ex.sems), name=name)(*ex.ins)


def all_reduce_small(buf, name):
    r = buf.shape[0]
    n_dev = 8

    def body(in_ref, o_ref, land, send_sems, recv_sems):
        x, y, c, _, _ = _place()
        me = 4 * x + 2 * y + c
        land[me] = in_ref[...]
        sends = []
        for d in range(1, n_dev):
            peer = (x ^ (d >> 2), y ^ ((d >> 1) & 1), c ^ (d & 1))
            cp = _rc(in_ref, land.at[me], send_sems.at[d], recv_sems.at[d], peer)
            cp.start()
            sends.append(cp)
        for d in range(1, n_dev):
            blk = land.at[me ^ d]
            _rc(blk, blk, send_sems.at[d], recv_sems.at[d], (x, y, c)).wait_recv()
        for cp in sends:
            cp.wait_send()
        tot = land[0]
        for d in range(1, n_dev):
            tot = tot + land[d]
        o_ref[...] = tot

    vm = pl.BlockSpec(memory_space=pltpu.VMEM)
    return pl.pallas_call(
        body, out_shape=_sds(buf.shape, F32), in_specs=[vm], out_specs=vm,
        scratch_shapes=[pltpu.VMEM((n_dev, r, LANES), F32), pltpu.SemaphoreType.DMA((n_dev,)),
                        pltpu.SemaphoreType.DMA((n_dev,))],
        name=name)(buf)


def rmsnorm_fwd(x, w, name):
    s, d = x.shape
    tm = _row_tile(s, 512)

    def body(x_ref, w_ref, o_ref):
        o_ref[...] = _rms_fwd(x_ref[...], w_ref[...]).astype(BF16)

    return _pcall(body, out_shape=[_sds((s, d), BF16)], grid=(s // tm,),
                  in_specs=[pl.BlockSpec((tm, d), lambda i: (i, 0)), pl.BlockSpec((1, d), lambda i: (0, 0))],
                  out_specs=[pl.BlockSpec((tm, d), lambda i: (i, 0))], args=[x, w.reshape(1, d)], name=name)[0]


def _ffn_w_spec(chip_of, single=False):
    mode = dict(pipeline_mode=pl.Buffered(1)) if single else {}
    return pl.BlockSpec((N_CORES, 1, FF_PART, D_MODEL), lambda *ids: (0, chip_of(*ids), 0, 0), **mode)


def ffn_fwd(h, x, wg, wu, wd, norm_ws, name, ride=None):
    s, d = h.shape
    n_norm = len(norm_ws)
    tm = _row_tile(s, 1024)

    def body(*refs):
        h_ref, x_ref, wg_ref, wu_ref, wd_ref = refs[:5]
        nw_refs = refs[5:5 + n_norm]
        o_ref = refs[5 + n_norm]
        h_refs = refs[6 + n_norm:6 + 2 * n_norm]
        gu_ref, acc = refs[6 + 2 * n_norm], refs[7 + 2 * n_norm]
        k = pl.program_id(1)

        @pl.when(k == 0)
        def _():
            acc[...] = jnp.zeros(acc.shape, F32)

        hm = tm // 2
        for part in range(2):
            sub = pl.ds(part * hm, hm)
            hb = h_ref[sub, :]
            g = _dot(hb, wg_ref[...].reshape(FF_SHARD, d), NT)
            u = _dot(hb, wu_ref[...].reshape(FF_SHARD, d), NT)
            gu_ref[0, 0, sub, :] = g.astype(BF16)
            gu_ref[0, 1, sub, :] = u.astype(BF16)
            acc[sub, :] += _dot((g * _sigmoid(g) * u).astype(BF16), wd_ref[...].reshape(FF_SHARD, d))

        @pl.when(k == N_CHIPS - 1)
        def _():
            xn = x_ref[...] + 0.5 * acc[...]
            o_ref[...] = xn
            for nw_ref, hn_ref in zip(nw_refs, h_refs):
                hn_ref[...] = _rms_fwd(xn, nw_ref[...]).astype(BF16)

    row = pl.BlockSpec((tm, d), lambda i, k: (i, 0))
    vec = pl.BlockSpec((1, d), lambda i, k: (0, 0))
    wsp = _ffn_w_spec(lambda i, k: k)
    return _pcall(
        body, out_shape=[_sds((s, d), F32)] + [_sds((s, d), BF16)] * n_norm + [_sds((N_CHIPS, 2, s, FF_SHARD), BF16)],
        grid=(s // tm, N_CHIPS),
        in_specs=[row, row, wsp, wsp, wsp] + [vec] * n_norm,
        out_specs=[row] * (1 + n_norm) + [pl.BlockSpec((1, 2, tm, FF_SHARD), lambda i, k: (k, 0, i, 0))],
        scratch_shapes=[pltpu.VMEM((tm, d), F32)],
        args=[h, x, wg, wu, wd] + [nw.reshape(1, d) for nw in norm_ws], name=name, ride=ride)


def ffn_bwd(dxn, h, x_in, nw, gu, wg, wu, wd, name, ride=None):
    s, d = h.shape
    tm = _row_tile(s, 512)
    ni = s // tm
    last_e = N_CHIPS - 1

    def body(dxn_ref, h_ref, x_ref, nw_ref, gu_ref, wg_ref, wu_ref, wd_ref,
             dx_ref, dnw_ref, dwg_ref, dwu_ref, dwd_ref, dh, wacc):
        e = pl.program_id(0)
        i = pl.program_id(1)
        rows = pl.ds(pl.multiple_of(i * tm, tm), tm)

        @pl.when(i == 0)
        def _():
            wacc[...] = jnp.zeros(wacc.shape, F32)

        @pl.when(e == 0)
        def _():
            dh[rows, :] = jnp.zeros((tm, d), F32)

        hm = tm // 2
        for part in range(2):
            sub = pl.ds(part * hm, hm)
            dxb = dxn_ref[sub, :].astype(BF16)
            hb = h_ref[sub, :]
            g = gu_ref[0, 0, sub, :].astype(F32)
            u = gu_ref[0, 1, sub, :].astype(F32)
            drows = pl.ds(pl.multiple_of(i * tm + part * hm, hm), hm)
            sg = _sigmoid(g)
            silu = g * sg
            wacc[2] += _dot((0.5 * silu * u).astype(BF16), dxb, TN)
            da = 0.5 * _dot(dxb, wd_ref[...].reshape(FF_SHARD, d), NT)
            dg = (da * u * (sg * (1.0 + g * (1.0 - sg)))).astype(BF16)
            wacc[0] += _dot(dg, hb, TN)
            du = (da * silu).astype(BF16)
            dh[drows, :] += _dot(dg, wg_ref[...].reshape(FF_SHARD, d))
            wacc[1] += _dot(du, hb, TN)
            dh[drows, :] += _dot(du, wu_ref[...].reshape(FF_SHARD, d))

        @pl.when(i == ni - 1)
        def _():
            for t, dw_ref in enumerate((dwg_ref, dwu_ref, dwd_ref)):
                dw_ref[...] = wacc[t].astype(BF16).reshape(N_CORES, 1, FF_PART, d)

        @pl.when(e == last_e)
        def _():
            dx, dnw = _rms_bwd(dh[rows, :], x_ref[...], nw_ref[...])
            dx_ref[...] = dxn_ref[...] + dx
            col = jnp.sum(dnw, axis=0, keepdims=True)

            @pl.when(i == 0)
            def _():
                dnw_ref[...] = col

            @pl.when(i > 0)
            def _():
                dnw_ref[...] += col

    row = pl.BlockSpec((tm, d), lambda e, i: (i, 0))
    late = pl.BlockSpec((tm, d), lambda e, i: (jnp.where(e == last_e, i, 0), 0))
    vec = pl.BlockSpec((1, d), lambda e, i: (0, 0))
    wsp = _ffn_w_spec(lambda e, i: e, single=True)
    dwsp = _ffn_w_spec(lambda e, i: e, single=True)
    dw = _sds((N_CORES, N_CHIPS, FF_PART, d), BF16)
    return _pcall(
        body, out_shape=[_sds((s, d), F32), _sds((1, d), F32), dw, dw, dw],
        grid=(N_CHIPS, ni),
        in_specs=[row, row, late, vec, pl.BlockSpec((1, 2, tm, FF_SHARD), lambda e, i: (e, 0, i, 0)), wsp, wsp, wsp],
        out_specs=[late, vec, dwsp, dwsp, dwsp],
        scratch_shapes=[pltpu.VMEM((s, d), F32), pltpu.VMEM((3, FF_SHARD, d), F32)],
        args=[dxn, h, x_in, nw.reshape(1, d), gu, wg, wu, wd], name=name, ride=ride)


def mm_res(a, w, x, name, bias=None, norm_ws=(), ride=None):
    s, k = a.shape
    n = w.shape[1]
    tm = _row_tile(s, 256)
    has_bias = bias is not None
    n_norm = len(norm_ws)

    def body(*refs):
        a_ref, w_ref, x_ref = refs[:3]
        pos = 3
        t = _dot(a_ref[...], w_ref[...])
        if has_bias:
            t = t + refs[pos][...]
            pos += 1
        nw_refs = refs[pos:pos + n_norm]
        o_ref = refs[pos + n_norm]
        h_refs = refs[pos + n_norm + 1:]
        xn = x_ref[...] + t
        o_ref[...] = xn
        for nw_ref, h_ref in zip(nw_refs, h_refs):
            h_ref[...] = _rms_fwd(xn, nw_ref[...]).astype(BF16)

    row = pl.BlockSpec((tm, n), lambda i: (i, 0))
    vec = pl.BlockSpec((1, n), lambda i: (0, 0))
    in_specs = [pl.BlockSpec((tm, k), lambda i: (i, 0)), pl.BlockSpec((k, n), lambda i: (0, 0)), row]
    args = [a, w, x]
    if has_bias:
        in_specs.append(vec)
        args.append(bias.reshape(1, n))
    for nw in norm_ws:
        in_specs.append(vec)
        args.append(nw.reshape(1, n))
    return _pcall(body, out_shape=[_sds((s, n), F32)] + [_sds((s, n), BF16)] * n_norm, grid=(s // tm,),
                  in_specs=in_specs, out_specs=[row] * (1 + n_norm), args=args, name=name, ride=ride)


def mm_nn(a, w, name, bias=None, out_dtype=F32):
    s, k = a.shape
    n = w.shape[1]
    tm = _row_tile(s, 512)
    tn = _col_tile(n)
    has_bias = bias is not None

    def body(*refs):
        a_ref, w_ref = refs[:2]
        o_ref = refs[-1]
        t = _dot(a_ref[...], w_ref[...])
        if has_bias:
            t = t + refs[2][...]
        o_ref[...] = t.astype(out_dtype)

    in_specs = [pl.BlockSpec((tm, k), lambda j, i: (i, 0)), pl.BlockSpec((k, tn), lambda j, i: (0, j))]
    args = [a, w]
    if has_bias:
        in_specs.append(pl.BlockSpec((1, tn), lambda j, i: (0, j)))
        args.append(bias.reshape(1, n))
    return _pcall(body, out_shape=[_sds((s, n), out_dtype)], grid=(n // tn, s // tm), in_specs=in_specs,
                  out_specs=[pl.BlockSpec((tm, tn), lambda j, i: (i, j))], args=args, name=name)[0]


def mm_nt(a, w, name, n=None, row0=0, out_dtype=F32, ride=None):
    s, k = a.shape
    n = w.shape[0] if n is None else n
    tm = _row_tile(s, 512)
    tn = _col_tile(n)
    base = row0 // tn
    assert row0 % tn == 0

    def body(a_ref, w_ref, o_ref):
        o_ref[...] = _dot(a_ref[...].astype(BF16), w_ref[...], NT).astype(out_dtype)

    res = _pcall(body, out_shape=[_sds((s, n), out_dtype)], grid=(n // tn, s // tm),
                 in_specs=[pl.BlockSpec((tm, k), lambda j, i: (i, 0)), pl.BlockSpec((tn, k), lambda j, i: (base + j, 0))],
                 out_specs=[pl.BlockSpec((tm, tn), lambda j, i: (i, j))], args=[a, w], name=name, ride=ride)
    return res[0] if ride is None else (res[0][0], res[1])


def mm_tn(a, b, name, into=None, rows=None, row0=0, m_valid=None, col_sum=False, ride=None):
    s, m = a.shape
    n = b.shape[1]
    mv = m if m_valid is None else m_valid
    tm = _col_tile(m) if m_valid is None else mv
    tn = n if n <= 1024 else _col_tile(n)
    rows = mv if rows is None else rows
    assert row0 % tm == 0 and (m_valid is None or m == LANES)
    assert not col_sum or mv == tm
    base = row0 // tm
    ta = m if m_valid is not None else tm

    def body(*refs):
        a_ref, b_ref = refs[0], refs[1]
        o_ref = refs[-2] if col_sum else refs[-1]
        bf = b_ref[...]
        t = _dot(a_ref[...].astype(BF16), bf.astype(BF16), TN)
        o_ref[...] = t[:tm].astype(BF16)
        if col_sum:
            refs[-1][...] = jnp.sum(bf.astype(F32), axis=0, keepdims=True)

    in_specs = [pl.BlockSpec((s, ta), lambda i, j: (0, i)), pl.BlockSpec((s, tn), lambda i, j: (0, j))]
    args = [a, b]
    aliases = None
    if into is not None:
        in_specs.append(ANY)
        args.append(into)
        aliases = {2: 0}
    out_shape = [_sds((rows, n), BF16)]
    out_specs = [pl.BlockSpec((tm, tn), lambda i, j: (base + i, j))]
    if col_sum:
        out_shape.append(_sds((1, n), F32))
        out_specs.append(pl.BlockSpec((1, tn), lambda i, j: (0, j)))
    res = _pcall(body, out_shape=out_shape, grid=(mv // tm, n // tn), in_specs=in_specs, out_specs=out_specs,
                 args=args, name=name, ride=ride, aliases=aliases)
    outs = res if ride is None else res[0]
    out = (outs[0], outs[1][0]) if col_sum else outs[0]
    return out if ride is None else (out, res[1])


def mm_rms_bwd(terms, dxn, x, nw, name, ride=None):
    s, n = x.shape
    nt_ = len(terms)
    tm = _row_tile(s, 256)
    forms = [t[5] for t in terms]

    def body(*refs):
        dxn_ref, x_ref, nw_ref, dx_ref, dnw_ref = refs[2 * nt_:]
        i = pl.program_id(0)
        dh = None
        for t in range(nt_):
            part = _dot(refs[2 * t][...].astype(BF16), refs[2 * t + 1][...], NN if forms[t] == "nn" else NT)
            dh = part if dh is None else dh + part
        dx, dnw = _rms_bwd(dh, x_ref[...], nw_ref[...])
        dx_ref[...] = dxn_ref[...] + dx
        col = jnp.sum(dnw, axis=0, keepdims=True)

        @pl.when(i == 0)
        def _():
            dnw_ref[...] = col

        @pl.when(i > 0)
        def _():
            dnw_ref[...] += col

    in_specs, args = [], []
    for a, cb, w, rb, kb, form in terms:
        in_specs.append(pl.BlockSpec((tm, kb), lambda i, cb=cb: (i, cb)))
        if form == "nn":
            in_specs.append(pl.BlockSpec((kb, n), lambda i, rb=rb: (rb, 0)))
        else:
            in_specs.append(pl.BlockSpec((n, kb), lambda i, rb=rb: (0, rb)))
        args += [a, w]
    row = pl.BlockSpec((tm, n), lambda i: (i, 0))
    vec = pl.BlockSpec((1, n), lambda i: (0, 0))
    res = _pcall(body, out_shape=[_sds((s, n), F32), _sds((1, n), F32)], grid=(s // tm,),
                 in_specs=in_specs + [row, row, vec], out_specs=[row, vec],
                 args=args + [dxn, x, nw.reshape(1, n)], name=name, ride=ride)
    outs = res if ride is None else res[0]
    out = (outs[0], outs[1][0])
    return out if ride is None else (out, res[1])


def rope_tables(s):
    pos = jnp.arange(s, dtype=F32)
    inv = 1.0 / (ROPE_THETA ** (jnp.arange(0, ATT_HEAD_DIM, 2, dtype=F32) / ATT_HEAD_DIM))
    ang = pos[:, None] * inv[None, :]
    cos = jnp.tile(jnp.cos(ang), (1, 2 * LANES // ATT_HEAD_DIM))
    sin = jnp.tile(jnp.sin(ang), (1, 2 * LANES // ATT_HEAD_DIM))
    return cos, sin


def rope_apply(t, cos, sin, name, inverse=False, scale=1.0, out_dtype=BF16):
    s, n = t.shape
    tm = _row_tile(s, 512)
    half = ATT_HEAD_DIM // 2
    reps = n // LANES

    def body(t_ref, c_ref, s_ref, o_ref):
        tf = t_ref[...].astype(F32)
        c = jnp.tile(c_ref[...], (1, reps))
        sn = jnp.tile(s_ref[...], (1, reps))
        lane = lax.broadcasted_iota(jnp.int32, tf.shape, 1)
        first = (lane & (ATT_HEAD_DIM - 1)) < half
        rot = jnp.where(first, -pltpu.roll(tf, n - half, 1), pltpu.roll(tf, half, 1))
        sign = -1.0 if inverse else 1.0
        o_ref[...] = (scale * (tf * c + sign * rot * sn)).astype(out_dtype)

    tab = pl.BlockSpec((tm, LANES), lambda i: (i, 0))
    return _pcall(body, out_shape=[_sds((s, n), out_dtype)], grid=(s // tm,),
                  in_specs=[pl.BlockSpec((tm, n), lambda i: (i, 0)), tab, tab],
                  out_specs=[pl.BlockSpec((tm, n), lambda i: (i, 0))], args=[t, cos, sin], name=name)[0]


CONV_TILE = 256


def _shift_down(u, k):
    if k == 0:
        return u
    row = lax.broadcasted_iota(jnp.int32, u.shape, 0)
    return jnp.where(row >= k, pltpu.roll(u, k, 0), 0.0)


def _shift_up(u, k):
    if k == 0:
        return u
    s = u.shape[0]
    row = lax.broadcasted_iota(jnp.int32, u.shape, 0)
    return jnp.where(row < s - k, pltpu.roll(u, s - k, 0), 0.0)


def _conv_pre(u, w_ref, b_ref):
    pre = b_ref[...] + w_ref[CONV_WIDTH - 1:CONV_WIDTH, :] * u
    for k in range(CONV_WIDTH - 1):
        pre += w_ref[k:k + 1, :] * _shift_down(u, CONV_WIDTH - 1 - k)
    return pre


def conv_fwd(u, w, b, name, ride=None):
    s, c = u.shape

    def body(u_ref, w_ref, b_ref, o_ref):
        pre = _conv_pre(u_ref[...], w_ref, b_ref)
        o_ref[...] = pre * _sigmoid(pre)

    col = pl.BlockSpec((s, CONV_TILE), lambda j: (0, j))
    res = _pcall(body, out_shape=[_sds((s, c), F32)], grid=(c // CONV_TILE,),
                 in_specs=[col, pl.BlockSpec((CONV_WIDTH, CONV_TILE), lambda j: (0, j)),
                           pl.BlockSpec((1, CONV_TILE), lambda j: (0, j))],
                 out_specs=[col], args=[u, w, b.reshape(1, c)], name=name, ride=ride)
    return res[0] if ride is None else (res[0][0], res[1])


def conv_bwd(dxs, db_, dc_, u, w, b, name):
    s, c = u.shape
    n_x = dxs.shape[1] // CONV_TILE
    n_b = db_.shape[1] // CONV_TILE

    def body(dx_ref, dbb_ref, dcc_ref, u_ref, w_ref, b_ref, du_ref, dw_ref, dbias_ref):
        j = pl.program_id(0)
        dact = jnp.where(j < n_x, dx_ref[...], jnp.where(j < n_x + n_b, dbb_ref[...], dcc_ref[...]))
        uf = u_ref[...]
        pre = _conv_pre(uf, w_ref, b_ref)
        sg = _sigmoid(pre)
        dpre = dact * (sg * (1.0 + pre * (1.0 - sg)))
        du = w_ref[CONV_WIDTH - 1:CONV_WIDTH, :] * dpre
        for k in range(CONV_WIDTH - 1):
            du += w_ref[k:k + 1, :] * _shift_up(dpre, CONV_WIDTH - 1 - k)
        du_ref[...] = du
        dbias_ref[...] = jnp.sum(dpre, axis=0, keepdims=True)
        for k in range(CONV_WIDTH):
            dw_ref[k:k + 1, :] = jnp.sum(dpre * _shift_down(uf, CONV_WIDTH - 1 - k), axis=0, keepdims=True)

    col = pl.BlockSpec((s, CONV_TILE), lambda j: (0, j))
    wsp = pl.BlockSpec((CONV_WIDTH, CONV_TILE), lambda j: (0, j))
    bsp = pl.BlockSpec((1, CONV_TILE), lambda j: (0, j))
    du, dw, db = _pcall(
        body, out_shape=[_sds((s, c), F32), _sds((CONV_WIDTH, c), F32), _sds((1, c), F32)], grid=(c // CONV_TILE,),
        in_specs=[pl.BlockSpec((s, CONV_TILE), lambda j: (0, jnp.minimum(j, n_x - 1))),
                  pl.BlockSpec((s, CONV_TILE), lambda j: (0, jnp.clip(j - n_x, 0, n_b - 1))),
                  pl.BlockSpec((s, CONV_TILE), lambda j: (0, jnp.clip(j - n_x - n_b, 0, n_b - 1))),
                  col, wsp, bsp],
        out_specs=[col, wsp, bsp], args=[dxs, db_, dc_, u, w, b.reshape(1, c)], name=name)
    return du, dw, db[0]


def _lane_pick(mat, idx):
    lane = lax.broadcasted_iota(jnp.int32, mat.shape, 1)
    return jnp.sum(jnp.where(lane == idx, mat, 0.0), axis=1, keepdims=True)


def _sub_pick(mat, idx):
    sub = lax.broadcasted_iota(jnp.int32, mat.shape, 0)
    return jnp.sum(jnp.where(sub == idx, mat, 0.0), axis=0, keepdims=True)


def _expand_heads(cols):
    rows = cols[0].shape[0]
    left = lax.broadcasted_iota(jnp.int32, (rows, LANES), 1) < SSM_HEAD_DIM
    return jnp.concatenate(
        [jnp.where(left, cols[2 * p], cols[2 * p + 1]) for p in range(HEADS_PER_GROUP // 2)], axis=1)


def _dot_01(x, ones, ones_first, pieces):
    tot, rest = None, x
    for _ in range(pieces):
        piece = rest.astype(BF16)
        rest = rest - piece.astype(F32)
        part = _dot(ones, piece) if ones_first else _dot(piece, ones)
        tot = part if tot is None else tot + part
    return tot


def _heads_to_lanes(mat, g):
    jj = lax.broadcasted_iota(jnp.int32, (GROUP_DIM, LANES), 0)
    ll = lax.broadcasted_iota(jnp.int32, (GROUP_DIM, LANES), 1)
    sel = (ll == HEADS_PER_GROUP * g + (jj >> 6)).astype(BF16)
    return _dot_01(mat, sel, False, 3)


def _softplus(x):
    return jnp.maximum(x, 0.0) + jnp.log1p(jnp.exp(-jnp.abs(x)))


def _ssd_scalars(dt_ref, bias_ref, a_ref, dtall, csall, cst):
    dta = _softplus(dt_ref[...] + bias_ref[...])
    row = lax.broadcasted_iota(jnp.int32, (CHUNK, CHUNK), 0)
    col = lax.broadcasted_iota(jnp.int32, (CHUNK, CHUNK), 1)
    cs = _dot_01(dta * a_ref[...], (row >= col).astype(BF16), True, 3)
    dtall[...] = dta
    csall[...] = cs
    cst[...] = cs.T


def _decay_mat(cs_col, cs_row):
    row = lax.broadcasted_iota(jnp.int32, (CHUNK, CHUNK), 0)
    col = lax.broadcasted_iota(jnp.int32, (CHUNK, CHUNK), 1)
    return jnp.exp(jnp.where(row >= col, cs_col - cs_row, NEG))


def _head_mask(xpair, right):
    lane = lax.broadcasted_iota(jnp.int32, xpair.shape, 1)
    keep = (lane >= SSM_HEAD_DIM) if right else (lane < SSM_HEAD_DIM)
    return jnp.where(keep, xpair, 0.0)


def _chunk_cols(x_all, g):
    return [_lane_pick(x_all, HEADS_PER_GROUP * g + r) for r in range(HEADS_PER_GROUP)]


def _decay_col(cs_cols):
    return jnp.concatenate(
        [jnp.broadcast_to(jnp.exp(cc[CHUNK - 1:CHUNK, :]), (SSM_HEAD_DIM, 1)) for cc in cs_cols], axis=0)


def ssd_fwd(act, z, dtp, bias_p, a_p, d_p, normw, name, ride=None):
    s = act.shape[0]
    nc = s // CHUNK
    b_off = D_INNER // SSM_STATE
    c_off = b_off + SSM_GROUPS

    def body(xs_ref, b_ref, c_ref, z_ref, dt_ref, bias_ref, a_ref, d_ref, nw_ref,
             yn_ref, y_ref, st_ref, state, dtall, csall, cst):
        c = pl.program_id(0)
        g = pl.program_id(1)

        @pl.when(g == 0)
        def _():
            _ssd_scalars(dt_ref, bias_ref, a_ref, dtall, csall, cst)

        @pl.when(c == 0)
        def _():
            state[g] = jnp.zeros((GROUP_DIM, SSM_STATE), F32)

        cs_cols = _chunk_cols(csall[...], g)
        dt_cols = _chunk_cols(dtall[...], g)
        cs_rows = [_sub_pick(cst[...], HEADS_PER_GROUP * g + r) for r in range(HEADS_PER_GROUP)]
        d_cols = _chunk_cols(d_ref[...], g)
        cs_exp = _expand_heads(cs_cols)
        dt_exp = _expand_heads(dt_cols)
        d_exp = _expand_heads(d_cols)
        xs = xs_ref[...]
        bb = b_ref[...].astype(BF16)
        cb16 = c_ref[...].astype(BF16)
        xdt = xs * dt_exp
        s_prev = state[g]
        st_ref[0, 0] = s_prev
        y_off = _dot(cb16, s_prev.astype(BF16), NT) * jnp.exp(cs_exp)
        decay_st = jnp.exp(cs_exp[CHUNK - 1:CHUNK, :] - cs_exp)
        contrib = _dot((xdt * decay_st).astype(BF16), bb, TN)
        state[g] = _decay_col(cs_cols) * s_prev + contrib
        cbm = _dot(cb16, bb, NT)
        pairs = []
        for p in range(HEADS_PER_GROUP // 2):
            xpair = xdt[:, LANES * p:LANES * (p + 1)]
            m0 = (cbm * _decay_mat(cs_cols[2 * p], cs_rows[2 * p])).astype(BF16)
            m1 = (cbm * _decay_mat(cs_cols[2 * p + 1], cs_rows[2 * p + 1])).astype(BF16)
            pairs.append(_dot(m0, _head_mask(xpair, False).astype(BF16))
                         + _dot(m1, _head_mask(xpair, True).astype(BF16)))
        y = jnp.concatenate(pairs, axis=1) + y_off + xs * d_exp
        y_ref[...] = y
        zf = z_ref[...]
        yg = y * (zf * _sigmoid(zf))
        yn_ref[...] = _rms_fwd(yg, nw_ref[...]).astype(BF16)

    grp = pl.BlockSpec((CHUNK, GROUP_DIM), lambda c, g: (c, g))
    par = pl.BlockSpec((1, LANES), lambda c, g: (0, 0))
    return _pcall(
        body,
        out_shape=[_sds((s, D_INNER), BF16), _sds((s, D_INNER), F32),
                   _sds((nc, SSM_GROUPS, GROUP_DIM, SSM_STATE), F32)],
        grid=(nc, SSM_GROUPS),
        in_specs=[grp,
                  pl.BlockSpec((CHUNK, SSM_STATE), lambda c, g: (c, b_off + g)),
                  pl.BlockSpec((CHUNK, SSM_STATE), lambda c, g: (c, c_off + g)),
                  grp,
                  pl.BlockSpec((CHUNK, LANES), lambda c, g: (c, 0)),
                  par, par, par,
                  pl.BlockSpec((1, GROUP_DIM), lambda c, g: (0, g))],
        out_specs=[grp, grp, pl.BlockSpec((1, 1, GROUP_DIM, SSM_STATE), lambda c, g: (c, g, 0, 0))],
        scratch_shapes=[pltpu.VMEM((SSM_GROUPS, GROUP_DIM, SSM_STATE), F32),
                        pltpu.VMEM((CHUNK, LANES), F32), pltpu.VMEM((CHUNK, LANES), F32),
                        pltpu.VMEM((LANES, CHUNK), F32)],
        args=[act, act, act, z, dtp, bias_p, a_p, d_p, normw], name=name, ride=ride)


def ssd_bwd(dyn, act, z, y_pre, states, dtp, bias_p, a_p, d_p, normw, name, ride=None):
    s = act.shape[0]
    nc = s // CHUNK
    b_off = D_INNER // SSM_STATE
    c_off = b_off + SSM_GROUPS

    def body(dyn_ref, xs_ref, b_ref, c_ref, z_ref, y_ref, st_ref, dt_ref, bias_ref, a_ref, d_ref, nw_ref,
             dxs_ref, db_ref, dc_ref, dz_ref, ddt_ref, dnw_ref, dbias_ref, da_ref, dd_ref,
             dstate, dtall, csall, cst):
        c = pl.program_id(0)
        g = pl.program_id(1)

        @pl.when(g == 0)
        def _():
            _ssd_scalars(dt_ref, bias_ref, a_ref, dtall, csall, cst)
            ddt_ref[...] = jnp.zeros((CHUNK, LANES), F32)

        @pl.when(c == 0)
        def _():
            dstate[g] = jnp.zeros((GROUP_DIM, SSM_STATE), F32)

        @pl.when(jnp.logical_and(c == 0, g == 0))
        def _():
            dnw_ref[...] = jnp.zeros(dnw_ref.shape, F32)
            dbias_ref[...] = jnp.zeros((1, LANES), F32)
            da_ref[...] = jnp.zeros((1, LANES), F32)
            dd_ref[...] = jnp.zeros((1, LANES), F32)

        cs_cols = _chunk_cols(csall[...], g)
        dt_cols = _chunk_cols(dtall[...], g)
        cs_rows = [_sub_pick(cst[...], HEADS_PER_GROUP * g + r) for r in range(HEADS_PER_GROUP)]
        d_cols = _chunk_cols(d_ref[...], g)
        cs_exp = _expand_heads(cs_cols)
        dt_exp = _expand_heads(dt_cols)
        d_exp = _expand_heads(d_cols)
        xs = xs_ref[...]
        bb = b_ref[...].astype(BF16)
        cb16 = c_ref[...].astype(BF16)
        xdt = xs * dt_exp
        s_prev = st_ref[0, 0]
        s_prev16 = s_prev.astype(BF16)
        ds_next = dstate[g]
        ds16 = ds_next.astype(BF16)

        zf = z_ref[...]
        sz = _sigmoid(zf)
        silu_z = zf * sz
        y = y_ref[...]
        yg = y * silu_z
        dout = dyn_ref[...]
        dyg, dnw = _rms_bwd(dout, yg, nw_ref[...])
        dnw_ref[pl.ds(g, 1), :] += jnp.sum(dnw, axis=0, keepdims=True)
        dy = dyg * silu_z
        dz_ref[...] = dyg * y * (sz * (1.0 + zf * (1.0 - sz)))
        dd_ref[...] += jnp.sum(_heads_to_lanes(dy * xs, g), axis=0, keepdims=True)

        exp_cs = jnp.exp(cs_exp)
        decay_st = jnp.exp(cs_exp[CHUNK - 1:CHUNK, :] - cs_exp)
        cs_t = _dot(cb16, s_prev16, NT)
        dyo = dy * exp_cs
        dc_acc = _dot(dyo.astype(BF16), s_prev16, NN)
        g1 = _dot(bb, ds16, NT)
        xds = xdt * decay_st
        db_acc = _dot(xds.astype(BF16), ds16, NN)
        dxdt_off = g1 * decay_st
        t_exp = g1 * xds
        dcs_exp = dy * cs_t * exp_cs - t_exp
        decay_c = _decay_col(cs_cols)
        dstate[g] = decay_c * ds_next + _dot(dyo.astype(BF16), cb16, TN)
        dlast_col = jnp.sum(ds_next * s_prev, axis=1, keepdims=True) * decay_c
        jj = lax.broadcasted_iota(jnp.int32, (GROUP_DIM, LANES), 0)
        ll = lax.broadcasted_iota(jnp.int32, (GROUP_DIM, LANES), 1)
        sel = ll == HEADS_PER_GROUP * g + (jj >> 6)
        dlast = jnp.sum(jnp.where(sel, dlast_col, 0.0), axis=0, keepdims=True)
        t_all = _heads_to_lanes(t_exp, g)
        dlast += jnp.sum(t_all, axis=0, keepdims=True)
        dcs_all = _heads_to_lanes(dcs_exp, g)

        cbm = _dot(cb16, bb, NT)
        dcb = jnp.zeros((CHUNK, CHUNK), F32)
        dcs_rows = jnp.zeros((LANES, CHUNK), F32)
        lane_l = lax.broadcasted_iota(jnp.int32, (CHUNK, LANES), 1)
        sub_l = lax.broadcasted_iota(jnp.int32, (LANES, CHUNK), 0)
        dxdt_pairs = []
        for p in range(HEADS_PER_GROUP // 2):
            xpair16 = xdt[:, LANES * p:LANES * (p + 1)].astype(BF16)
            dypair = dy[:, LANES * p:LANES * (p + 1)]
            acc = None
            for r in (2 * p, 2 * p + 1):
                lm = _decay_mat(cs_cols[r], cs_rows[r])
                m = cbm * lm
                dyh = _head_mask(dypair, r % 2 == 1).astype(BF16)
                dm = _dot(dyh, xpair16, NT)
                dcb += dm * lm
                q = dm * m
                idx = HEADS_PER_GROUP * g + r
                dcs_all += jnp.where(lane_l == idx, jnp.sum(q, axis=1, keepdims=True), 0.0)
                dcs_rows -= jnp.where(sub_l == idx, jnp.sum(q, axis=0, keepdims=True), 0.0)
                part = _dot(m.astype(BF16), dyh, TN)
                acc = part if acc is None else acc + part
            dxdt_pairs.append(acc)
        dxdt = jnp.concatenate(dxdt_pairs, axis=1) + dxdt_off
        dcb16 = dcb.astype(BF16)
        dc_ref[...] = dc_acc + _dot(dcb16, bb, NN)
        db_ref[...] = db_acc + _dot(dcb16, cb16, TN)
        dxs_ref[...] = dxdt * dt_exp + dy * d_exp

        dcs_all += dcs_rows.T
        row = lax.broadcasted_iota(jnp.int32, (CHUNK, CHUNK), 0)
        col = lax.broadcasted_iota(jnp.int32, (CHUNK, CHUNK), 1)
        last_row = lax.broadcasted_iota(jnp.int32, (CHUNK, LANES), 0) == CHUNK - 1
        dcs_all += jnp.where(last_row, dlast, 0.0)
        da_all = _dot_01(dcs_all, (col >= row).astype(BF16), True, 3)
        dta = dtall[...]
        in_group = jnp.logical_and(lane_l >= HEADS_PER_GROUP * g, lane_l < HEADS_PER_GROUP * (g + 1))
        ddt = jnp.where(in_group, da_all * a_ref[...] + _heads_to_lanes(dxdt * xs, g), 0.0)
        da_ref[...] += jnp.sum(jnp.where(in_group, da_all * dta, 0.0), axis=0, keepdims=True)
        ddt_raw = ddt * _sigmoid(dt_ref[...] + bias_ref[...])
        ddt_ref[...] += ddt_raw
        dbias_ref[...] += jnp.sum(ddt_raw, axis=0, keepdims=True)

    rev = lambda c, g: (nc - 1 - c, g)
    grp = pl.BlockSpec((CHUNK, GROUP_DIM), rev)
    st = pl.BlockSpec((CHUNK, SSM_STATE), rev)
    par = pl.BlockSpec((1, LANES), lambda c, g: (0, 0))
    dtb = pl.BlockSpec((CHUNK, LANES), lambda c, g: (nc - 1 - c, 0))
    f = lambda shape: _sds(shape, F32)
    return _pcall(
        body,
        out_shape=[f((s, D_INNER)), f((s, SSM_GROUPS * SSM_STATE)), f((s, SSM_GROUPS * SSM_STATE)),
                   f((s, D_INNER)), f((s, LANES)), f((8, GROUP_DIM)), f((1, LANES)), f((1, LANES)), f((1, LANES))],
        grid=(nc, SSM_GROUPS),
        in_specs=[grp, grp,
                  pl.BlockSpec((CHUNK, SSM_STATE), lambda c, g: (nc - 1 - c, b_off + g)),
                  pl.BlockSpec((CHUNK, SSM_STATE), lambda c, g: (nc - 1 - c, c_off + g)),
                  grp, grp,
                  pl.BlockSpec((1, 1, GROUP_DIM, SSM_STATE), lambda c, g: (nc - 1 - c, g, 0, 0)),
                  dtb, par, par, par,
                  pl.BlockSpec((1, GROUP_DIM), lambda c, g: (0, g))],
        out_specs=[grp, st, st, grp, dtb, pl.BlockSpec((8, GROUP_DIM), lambda c, g: (0, 0)), par, par, par],
        scratch_shapes=[pltpu.VMEM((SSM_GROUPS, GROUP_DIM, SSM_STATE), F32),
                        pltpu.VMEM((CHUNK, LANES), F32), pltpu.VMEM((CHUNK, LANES), F32),
                        pltpu.VMEM((LANES, CHUNK), F32)],
        args=[dyn, act, act, act, z, y_pre, states, dtp, bias_p, a_p, d_p, normw], name=name, ride=ride)


def _attn_probs(q, kp, kc, sink, n):
    sp = _dot(q, kp, NT)
    sc = _dot(q, kc, NT)
    i = lax.broadcasted_iota(jnp.int32, sp.shape, 0) & (WINDOW - 1)
    j = lax.broadcasted_iota(jnp.int32, sp.shape, 1)
    sp = jnp.where(jnp.logical_and(j > i, n > 0), sp, NEG)
    sc = jnp.where(j <= i, sc, NEG)
    m = jnp.maximum(jnp.maximum(jnp.max(sp, axis=1, keepdims=True), jnp.max(sc, axis=1, keepdims=True)), sink)
    pp = jnp.exp(sp - m)
    pc = jnp.exp(sc - m)
    ps = jnp.exp(sink - m)
    inv = 1.0 / (jnp.sum(pp, axis=1, keepdims=True) + jnp.sum(pc, axis=1, keepdims=True) + ps)
    return pp * inv, pc * inv, ps * inv


def attn_fwd(qt, kt, vt, sink_rows, name, ride=None):
    s = qt.shape[1]
    nb = s // WINDOW
    rows = Q_PER_KV * WINDOW

    def body(q_ref, kp_ref, kc_ref, vp_ref, vc_ref, sk_ref, o_ref):
        n = pl.program_id(1)
        q = q_ref[...].reshape(rows, ATT_HEAD_DIM)
        pp, pc, _ = _attn_probs(q, kp_ref[0], kc_ref[0], sk_ref[0], n)
        o = _dot(pp.astype(BF16), vp_ref[0]) + _dot(pc.astype(BF16), vc_ref[0])
        o_ref[...] = o.reshape(Q_PER_KV, WINDOW, ATT_HEAD_DIM).astype(BF16)

    qsp = pl.BlockSpec((Q_PER_KV, WINDOW, ATT_HEAD_DIM), lambda h, n: (h, n, 0))
    prev = pl.BlockSpec((1, WINDOW, ATT_HEAD_DIM), lambda h, n: (h, jnp.maximum(n - 1, 0), 0))
    cur = pl.BlockSpec((1, WINDOW, ATT_HEAD_DIM), lambda h, n: (h, n, 0))
    return _pcall(body, out_shape=[_sds(qt.shape, BF16)], grid=(N_KV_HEADS, nb),
                  in_specs=[qsp, prev, cur, prev, cur, pl.BlockSpec((1, rows, 1), lambda h, n: (h, 0, 0))],
                  out_specs=[qsp], args=[qt, kt, kt, vt, vt, sink_rows], name=name, ride=ride)


def attn_bwd(qt, kt, vt, sink_rows, dot_, name, ride=None):
    s = qt.shape[1]
    nb = s // WINDOW
    rows = Q_PER_KV * WINDOW

    def body(q_ref, kp_ref, kc_ref, vp_ref, vc_ref, sk_ref, do_ref, dq_ref, dk_ref, dv_ref, ds_ref, kacc, vacc):
        n = pl.program_id(1)

        @pl.when(n < nb)
        def _():
            q = q_ref[...].reshape(rows, ATT_HEAD_DIM)
            do = do_ref[...].reshape(rows, ATT_HEAD_DIM)
            kp, kc, vp, vc = kp_ref[0], kc_ref[0], vp_ref[0], vc_ref[0]
            pp, pc, ps = _attn_probs(q, kp, kc, sk_ref[0], n)
            dpp = _dot(do, vp, NT)
            dpc = _dot(do, vc, NT)
            delta = jnp.sum(pp * dpp, axis=1, keepdims=True) + jnp.sum(pc * dpc, axis=1, keepdims=True)
            dsp = (pp * (dpp - delta)).astype(BF16)
            dsc = (pc * (dpc - delta)).astype(BF16)
            dq = _dot(dsp, kp) + _dot(dsc, kc)
            dq_ref[...] = dq.reshape(Q_PER_KV, WINDOW, ATT_HEAD_DIM)
            dk_prev = _dot(dsp, q, TN)
            dv_prev = _dot(pp.astype(BF16), do, TN)

            @pl.when(n == 0)
            def _():
                dk_ref[0] = dk_prev
                dv_ref[0] = dv_prev

            @pl.when(n > 0)
            def _():
                dk_ref[0] = kacc[...] + dk_prev
                dv_ref[0] = vacc[...] + dv_prev

            kacc[...] = _dot(dsc, q, TN)
            vacc[...] = _dot(pc.astype(BF16), do, TN)
            dsk = -ps * delta
            sub = lax.broadcasted_iota(jnp.int32, (8, LANES), 0)
            tile = jnp.zeros((8, LANES), F32)
            for h in range(Q_PER_KV):
                tile += jnp.where(sub == h, jnp.sum(dsk[h * WINDOW:(h + 1) * WINDOW, :], axis=0, keepdims=True), 0.0)
            ds_ref[0, 0] = tile

        @pl.when(n == nb)
        def _():
            dk_ref[0] = kacc[...]
            dv_ref[0] = vacc[...]
            ds_ref[0, 0] = jnp.zeros((8, LANES), F32)

    last = nb - 1
    qsp = pl.BlockSpec((Q_PER_KV, WINDOW, ATT_HEAD_DIM), lambda h, n: (h, jnp.minimum(n, last), 0))
    prev = pl.BlockSpec((1, WINDOW, ATT_HEAD_DIM), lambda h, n: (h, jnp.clip(n - 1, 0, last), 0))
    cur = pl.BlockSpec((1, WINDOW, ATT_HEAD_DIM), lambda h, n: (h, jnp.minimum(n, last), 0))
    dkv = pl.BlockSpec((1, WINDOW, ATT_HEAD_DIM), lambda h, n: (h, jnp.maximum(n - 1, 0), 0))
    f = lambda shape: _sds(shape, F32)
    return _pcall(
        body, out_shape=[f(qt.shape), f(kt.shape), f(vt.shape), f((N_KV_HEADS, nb + 1, 8, LANES))],
        grid=(N_KV_HEADS, nb + 1),
        in_specs=[qsp, prev, cur, prev, cur, pl.BlockSpec((1, rows, 1), lambda h, n: (h, 0, 0)), qsp],
        out_specs=[qsp, dkv, dkv, pl.BlockSpec((1, 1, 8, LANES), lambda h, n: (h, n, 0, 0))],
        scratch_shapes=[pltpu.VMEM((WINDOW, ATT_HEAD_DIM), F32), pltpu.VMEM((WINDOW, ATT_HEAD_DIM), F32)],
        args=[qt, kt, kt, vt, vt, sink_rows, dot_], name=name, ride=ride)


def loss_head(x, w, tgt, name):
    s, d = x.shape
    tm = _row_tile(s, 256)

    def body(x_ref, w_ref, t_ref, loss_ref, dx_ref, dw_ref):
        i = pl.program_id(0)
        xf = x_ref[...]
        wv = w_ref[...]
        r = lax.rsqrt(jnp.mean(xf * xf, axis=-1, keepdims=True) + EPS)
        xhat = xf * r
        e = xhat * wv - t_ref[...]
        part = 0.5 * jnp.sum(jnp.mean(e * e, axis=-1, keepdims=True), axis=0, keepdims=True)
        dy = e * (1.0 / d)
        dxhat = dy * wv
        dx_ref[...] = r * (dxhat - xhat * jnp.mean(dxhat * xhat, axis=-1, keepdims=True))
        col = jnp.sum(dy * xhat, axis=0, keepdims=True)

        @pl.when(i == 0)
        def _():
            loss_ref[...] = jnp.broadcast_to(part, (1, LANES))
            dw_ref[...] = col

        @pl.when(i > 0)
        def _():
            loss_ref[...] += jnp.broadcast_to(part, (1, LANES))
            dw_ref[...] += col

    row = pl.BlockSpec((tm, d), lambda i: (i, 0))
    vec = pl.BlockSpec((1, d), lambda i: (0, 0))
    return _pcall(body, out_shape=[_sds((1, LANES), F32), _sds((s, d), F32), _sds((1, d), F32)], grid=(s // tm,),
                  in_specs=[row, vec, row], out_specs=[pl.BlockSpec((1, LANES), lambda i: (0, 0)), row, vec],
                  args=[x, w.reshape(1, d), tgt], name=name)


def _tile_rows(r, c, max_elems=262144, mult=16):
    best = None
    for t in range(mult, r + 1, mult):
        if r % t == 0 and t * c <= max_elems:
            best = t
    return best or r


def add_pair(xhs, ps, c_idx, name):
    n = len(xhs)
    _, r, c = xhs[0].shape
    tr = _tile_rows(r, c)

    def body(c_ref, *refs):
        for x_ref, p_ref, o_ref in zip(refs[:n], refs[n:2 * n], refs[2 * n:]):
            o_ref[...] = (x_ref[0].astype(F32) + p_ref[...].astype(F32)).astype(BF16)

    blk = pl.BlockSpec((tr, c), lambda i, cr: (i, 0))
    return pl.pallas_call(
        body, out_shape=tuple([_sds((r, c), BF16)] * n),
        grid_spec=pltpu.PrefetchScalarGridSpec(
            num_scalar_prefetch=1, grid=(r // tr,),
            in_specs=[pl.BlockSpec((1, tr, c), lambda i, cr: (cr[0], i, 0))] * n + [blk] * n,
            out_specs=tuple([blk] * n)),
        name=name, compiler_params=_cp(1))(c_idx, *xhs, *ps)


def sum_chips(qs, owns, chip_idx, name):
    n = len(qs)
    _, r, c = qs[0].shape
    tr = _tile_rows(r, c)

    def body(k_ref, *refs):
        k = k_ref[0]
        for q_ref, own_ref, o_ref in zip(refs[:n], refs[n:2 * n], refs[2 * n:]):
            mine = own_ref[0].astype(F32)
            tot = None
            for j in range(N_CHIPS):
                term = jnp.where(k == j, mine, q_ref[j].astype(F32))
                tot = term if tot is None else tot + term
            o_ref[...] = tot

    return pl.pallas_call(
        body, out_shape=tuple([_sds((r, c), F32)] * n),
        grid_spec=pltpu.PrefetchScalarGridSpec(
            num_scalar_prefetch=1, grid=(r // tr,),
            in_specs=([pl.BlockSpec((N_CHIPS, tr, c), lambda i, kr: (0, i, 0))] * n
                      + [pl.BlockSpec((1, tr, c), lambda i, kr: (kr[0], i, 0))] * n),
            out_specs=tuple([pl.BlockSpec((tr, c), lambda i, kr: (i, 0))] * n)),
        name=name, compiler_params=_cp(1))(chip_idx, *qs, *owns)


def adamw(w, g, m, v, name):
    r, c = w.shape
    tr = _tile_rows(r, c, max_elems=131072, mult=8)
    c1 = 1.0 / (1.0 - ADAM_B1 ** ADAM_STEP)
    c2 = 1.0 / (1.0 - ADAM_B2 ** ADAM_STEP)

    def body(w_ref, g_ref, m_ref, v_ref, d_ref, mo_ref, vo_ref):
        gf = g_ref[...]
        mn = ADAM_B1 * m_ref[...] + (1.0 - ADAM_B1) * gf
        vn = ADAM_B2 * v_ref[...] + (1.0 - ADAM_B2) * (gf * gf)
        mo_ref[...] = mn
        vo_ref[...] = vn
        d_ref[...] = -ADAM_LR * ((mn * c1) / (jnp.sqrt(vn * c2) + ADAM_EPS) + ADAM_WD * w_ref[...])

    blk = pl.BlockSpec((tr, c), lambda i: (i, 0))
    out = _sds((r, c), F32)
    return _pcall(body, out_shape=[out, out, out], grid=(r // tr,), in_specs=[blk] * 4, out_specs=[blk] * 3,
                  args=[w, g, m, v], name=name)


WEIGHTS = ['norm_w', 'ffn_w_gate', 'ffn_w_up', 'ffn_w_down', 'ssm_w_in', 'ssm_conv_w', 'ssm_conv_b', 'ssm_dt_bias',
           'ssm_a_log', 'ssm_d', 'ssm_norm_w', 'ssm_w_out', 'kv_norm_w', 'w_k', 'b_k', 'w_v', 'b_v', 'attn_w_q',
           'attn_b_q', 'attn_sinks', 'attn_w_o', 'attn_b_o', 'final_norm_w']
BIG = ['ffn_w_gate', 'ffn_w_up', 'ffn_w_down', 'ssm_w_in', 'ssm_w_out', 'w_k', 'w_v', 'attn_w_q', 'attn_w_o']
TRANSPOSED = ('ffn_w_gate', 'ffn_w_up', 'ssm_w_in')
SMALL = [n for n in WEIGHTS if n not in BIG]
SMALL_SHARDED = {'norm_w': 2, 'ssm_conv_w': 2, 'ssm_conv_b': 1, 'ssm_norm_w': 1}
ROW_ALIGN = 8 * LANES


def _pack_rows(parts):
    flat = jnp.concatenate([p.reshape(-1).astype(F32) for p in parts])
    pad = (-flat.size) % ROW_ALIGN
    return jnp.pad(flat, (0, pad)).reshape(-1, LANES)


def _unpack_rows(buf, shapes):
    flat = buf.reshape(-1)
    out, pos = [], 0
    for shp in shapes:
        size = math.prod(shp)
        out.append(flat[pos:pos + size].reshape(shp))
        pos += size
    return out


def _as2d(a):
    return a.reshape(-1, a.shape[-1])


def _heads_major(t, n_heads):
    s = t.shape[0]
    return t.reshape(s, n_heads, ATT_HEAD_DIM).transpose(1, 0, 2)


def _tokens_major(t):
    h, s, dh = t.shape
    return t.transpose(1, 0, 2).reshape(s, h * dh)


def _pad_lanes(v):
    return jnp.pad(v.reshape(1, -1), ((0, 0), (0, LANES - v.size)))


def _chips_first(t):
    return t.swapaxes(0, 1).reshape((-1,) + t.shape[3:])


def _parts_first(t, rows):
    return t.reshape((N_CHIPS, N_CORES, rows) + t.shape[1:]).swapaxes(0, 1)


def kernel(*args):
    names = (['x'] + WEIGHTS + ['loss_target'] + ['m_' + n for n in WEIGHTS] + ['v_' + n for n in WEIGHTS])
    a = dict(zip(names, args))
    for n in TRANSPOSED:
        for pre in ('', 'm_', 'v_'):
            a[pre + n] = a[pre + n].swapaxes(-1, -2)
    xi, yi, ci = lax.axis_index("x"), lax.axis_index("y"), lax.axis_index("c")
    chip = 2 * xi + yi
    south = ci == 0
    c_idx = jnp.reshape(ci, (1,)).astype(jnp.int32)
    chip_idx = jnp.reshape(chip, (1,)).astype(jnp.int32)
    x0 = a['x'][0]
    s = x0.shape[0]
    cos, sin = rope_tables(s)

    def own_slot(full, mine):
        return lax.dynamic_update_slice_in_dim(full, mine[:, None], chip, axis=1)

    def ffn_shard(l, i):
        return [a[n][l, i].astype(BF16).reshape(N_CORES, FF_PART, D_MODEL)
                for n in ('ffn_w_gate', 'ffn_w_up', 'ffn_w_down')]

    def own_slots(fulls, mines):
        return [own_slot(f, m) for f, m in zip(fulls, mines)]
    w_in_sh = jnp.pad(a['ssm_w_in'][0], ((0, IN_SHARD_PAD - IN_SHARD), (0, 0))).astype(BF16).reshape(
        N_CORES, IN_SHARD_PAD // 2, D_MODEL)
    w_out_sh = a['ssm_w_out'][0].astype(BF16).reshape(N_CORES, 256, D_MODEL)
    attn_sh = jnp.stack([a['attn_w_q'][0], a['attn_w_o'][0]]).astype(BF16)
    kv_sh = jnp.stack([a['w_k'], a['w_v']]).astype(BF16)
    small_names = list(SMALL_SHARDED)
    small_sh = _pack_rows([a[n] for n in small_names])
    small_sh = small_sh.reshape(N_CORES, small_sh.shape[0] // 2, LANES)

    sh00, sh01, sh10, sh11 = ffn_shard(0, 0), ffn_shard(0, 1), ffn_shard(1, 0), ffn_shard(1, 1)
    first = run_exchange(gather_chips(sh00 + [small_sh]), "gather_first")
    w00 = own_slots(first[:3], sh00)
    smalls = own_slot(first[3], small_sh)
    p = {}
    per_chip = [_unpack_rows(smalls[:, k], [a[n].shape for n in small_names]) for k in range(N_CHIPS)]
    for idx, n in enumerate(small_names):
        p[n] = jnp.concatenate([per_chip[k][idx] for k in range(N_CHIPS)], axis=SMALL_SHARDED[n])
    nw = p['norm_w']
    conv_w, conv_b, ssm_nw = p['ssm_conv_w'][0], p['ssm_conv_b'][0], p['ssm_norm_w'][0].reshape(1, D_INNER)

    h00 = rmsnorm_fwd(x0, nw[0, 0], "norm_in")
    (x1, h01, gu00), (w_in_g, kv_g) = ffn_fwd(h00, x0, *w00, [nw[0, 1]], "ffn_fwd_00",
                                              ride=gather_chips([w_in_sh, kv_sh]))
    w_in_t = _chips_first(own_slot(w_in_g, w_in_sh)).reshape(N_CHIPS, IN_SHARD_PAD, D_MODEL)[:, :IN_SHARD].reshape(
        IN_PROJ_DIM, D_MODEL)
    w_dt_t = jnp.pad(w_in_t[D_INNER + CONV_DIM:], ((0, LANES - SSM_HEADS), (0, 0)))
    kv_g = own_slot(kv_g, kv_sh)
    w_k, w_v = kv_g[0].reshape(D_MODEL, KV_DIM), kv_g[1].reshape(D_MODEL, KV_DIM)

    zz = mm_nt(h01, w_in_t, "ssm_in_z", n=D_INNER)
    xbc, (w_out_g,) = mm_nt(h01, w_in_t, "ssm_in_xbc", n=CONV_DIM, row0=D_INNER, ride=gather_chips([w_out_sh]))
    w_out = _chips_first(own_slot(w_out_g, w_out_sh))
    dtp = mm_nt(h01, w_dt_t, "ssm_in_dt")
    act, (wg01,) = conv_fwd(xbc, conv_w, conv_b, "ssm_conv", ride=gather_chips(sh01[:1]))
    bias_p = _pad_lanes(a['ssm_dt_bias'][0])
    a_p = _pad_lanes(-jnp.exp(a['ssm_a_log'][0]))
    d_p = _pad_lanes(a['ssm_d'][0])
    (yn, y_pre, states), (wu01, wd01) = ssd_fwd(act, zz, dtp, bias_p, a_p, d_p, ssm_nw, "ssd_fwd",
                                                ride=gather_chips(sh01[1:]))
    w01 = own_slots([wg01, wu01, wd01], sh01)
    (x2, h02), (wg10,) = mm_res(yn, w_out, x1, "ssm_out", norm_ws=[nw[0, 2]], ride=gather_chips(sh10[:1]))
    (x3, hkv, h10, gu01), (wu10, wd10) = ffn_fwd(h02, x2, *w01, [a['kv_norm_w'], nw[1, 0]], "ffn_fwd_01",
                                                 ride=gather_chips(sh10[1:]))
    w10 = own_slots([wg10, wu10, wd10], sh10)

    k_rot = rope_apply(mm_nn(hkv, w_k, "kv_k", bias=a['b_k']), cos, sin, "rope_k")
    v = mm_nn(hkv, w_v, "kv_v", bias=a['b_v'], out_dtype=BF16)
    kt = _heads_major(k_rot, N_KV_HEADS)
    vt = _heads_major(v, N_KV_HEADS)

    (x4, h11, gu10), (attn_g, wg11) = ffn_fwd(h10, x3, *w10, [nw[1, 1]], "ffn_fwd_10",
                                              ride=gather_chips([attn_sh, sh11[0]]))
    attn_g = own_slot(attn_g, attn_sh)
    w_q, w_o = attn_g[0].reshape(D_MODEL, D_MODEL), attn_g[1].reshape(D_MODEL, D_MODEL)
    scale = 1.0 / math.sqrt(ATT_HEAD_DIM)
    q_rot = rope_apply(mm_nn(h11, w_q, "attn_q", bias=a['attn_b_q'][0]), cos, sin, "rope_q", scale=scale)
    qt = _heads_major(q_rot, N_Q_HEADS)
    sink_rows = jnp.repeat(a['attn_sinks'][0].reshape(N_KV_HEADS, Q_PER_KV), WINDOW, axis=1).reshape(
        N_KV_HEADS, Q_PER_KV * WINDOW, 1)
    (ot,), (wu11, wd11) = attn_fwd(qt, kt, vt, sink_rows, "attn_fwd", ride=gather_chips(sh11[1:]))
    w11 = own_slots([wg11, wu11, wd11], sh11)
    o = _tokens_major(ot)
    x5, h12 = mm_res(o, w_o, x4, "attn_out", bias=a['attn_b_o'][0], norm_ws=[nw[1, 2]])
    x6, gu11 = ffn_fwd(h12, x5, *w11, [], "ffn_fwd_11")

    loss_v, dx6, d_final = loss_head(x6, a['final_norm_w'], a['loss_target'][0], "loss_head")
    loss = lax.psum(loss_v[0, 0], ("x", "y", "c"))
    g = {'final_norm_w': d_final[0]}

    def same_shape(xs, ys):
        runs = []
        for xv, yv in zip(xs, ys):
            if runs and runs[-1][0][0].shape == xv.shape:
                runs[-1][0].append(xv)
                runs[-1][1].append(yv)
            else:
                runs.append(([xv], [yv]))
        return runs

    def pre_reduce(grads, sib, tag):
        out = []
        for idx, (grp, sbs) in enumerate(same_shape(grads, list(sib))):
            ts = add_pair([gr.reshape(2, -1, gr.shape[-1]) for gr in grp], [_as2d(sb) for sb in sbs], c_idx,
                          "rs_add_%s_%d" % (tag, idx))
            out += [t.reshape(gr.shape[1:]) for t, gr in zip(ts, grp)]
        return out

    def chip_sum(landed, parts, tag):
        out = []
        for idx, (qs, owns) in enumerate(same_shape(list(landed), parts)):
            ts = sum_chips([q.reshape(N_CHIPS, -1, q.shape[-1]) for q in qs],
                           [own.reshape(N_CHIPS, -1, own.shape[-1]) for own in owns], chip_idx,
                           "rs_sum_%s_%d" % (tag, idx))
            out += [t.reshape(q.shape[1:]) for t, q in zip(ts, qs)]
        return out

    dnw = [[None] * 3 for _ in range(2)]
    sums = {}

    def trade(key):
        return swap_cores(sums[key], False)

    dx5, dnw12, *g11 = ffn_bwd(dx6, h12, x5, nw[1, 2], gu11, *w11, "ffn_bwd_11")
    dnw[1][2] = dnw12[0]
    (d_wo, g['attn_b_o']), sib11 = mm_tn(o, dx5, "attn_dwo", col_sum=True, ride=swap_cores(g11, True))
    t11 = pre_reduce(g11, sib11, "11")
    do = mm_nt(dx5, w_o, "attn_do", out_dtype=BF16)
    (dqt, dkt, dvt, dsink), land11 = attn_bwd(qt, kt, vt, sink_rows, _heads_major(do, N_Q_HEADS), "attn_bwd",
                                             ride=scatter_chips(t11))
    sums['11'] = chip_sum(land11, t11, "11")
    g['attn_sinks'] = jnp.sum(dsink[:, :, :Q_PER_KV, 0], axis=1).reshape(N_Q_HEADS)
    dq_pre = rope_apply(_tokens_major(dqt), cos, sin, "rope_dq", inverse=True, scale=scale, out_dtype=F32)
    d_wq, g['attn_b_q'] = mm_tn(h11, dq_pre, "attn_dwq", col_sum=True)
    g_attn = [jnp.stack([d_wq.reshape(N_CHIPS, 256, D_MODEL), d_wo.reshape(N_CHIPS, 256, D_MODEL)])]
    (dx4, dnw[1][1]), sib_attn = mm_rms_bwd([(dq_pre, 0, w_q, 0, D_MODEL, "nt")], dx5, x4, nw[1, 1], "attn_bwd_dh",
                                            ride=swap_cores(g_attn, True))
    t_attn = pre_reduce(g_attn, sib_attn, "attn")
    (dx3, dnw10, *g10), landed = ffn_bwd(dx4, h10, x3, nw[1, 0], gu10, *w10, "ffn_bwd_10",
                                         ride=join(scatter_chips(t_attn), trade('11')))
    dnw[1][0] = dnw10[0]
    sums['attn'] = chip_sum(landed[:1], t_attn, "attn")
    theirs = {'11': landed[1:]}
    dk_pre = rope_apply(_tokens_major(dkt), cos, sin, "rope_dk", inverse=True, out_dtype=F32)
    dv = _tokens_major(dvt)
    (d_wk, g['b_k']), sib10 = mm_tn(hkv, dk_pre, "kv_dwk", col_sum=True, ride=swap_cores(g10, True))
    t10 = pre_reduce(g10, sib10, "10")
    d_wv, g['b_v'] = mm_tn(hkv, dv, "kv_dwv", col_sum=True)
    g_kv = [jnp.stack([d_wk.reshape(N_CHIPS, 256, KV_DIM), d_wv.reshape(N_CHIPS, 256, KV_DIM)])]
    (dx3, g['kv_norm_w']), sib_kv = mm_rms_bwd(
        [(dk_pre, 0, w_k, 0, KV_DIM, "nt"), (dv, 0, w_v, 0, KV_DIM, "nt")], dx3, x3, a['kv_norm_w'], "kv_bwd_dh",
        ride=swap_cores(g_kv, True))
    t_kv = pre_reduce(g_kv, sib_kv, "kv")
    (dx2, dnw02, *g01), landed = ffn_bwd(dx3, h02, x2, nw[0, 2], gu01, *w01, "ffn_bwd_01",
                                         ride=join(scatter_chips(t10 + t_kv), trade('attn')))
    dnw[0][2] = dnw02[0]
    sums['10'] = chip_sum(landed[:3], t10, "10")
    sums['kv'] = chip_sum(landed[3:4], t_kv, "kv")
    theirs['attn'] = landed[4:]
    d_wout, sib01 = mm_tn(yn, dx2, "ssm_dwout", ride=swap_cores(g01, True))
    t01 = pre_reduce(g01, sib01, "01")
    dyn = mm_nt(dx2, w_out, "ssm_dyn")
    (dxs, db_, dc_, dz, ddt, d_ssm_nw, d_bias, d_a, d_d), landed = ssd_bwd(
        dyn, act, zz, y_pre, states, dtp, bias_p, a_p, d_p, ssm_nw, "ssd_bwd",
        ride=join(scatter_chips(t01), trade('10'), trade('kv')))
    sums['01'] = chip_sum(landed[:3], t01, "01")
    theirs['10'], theirs['kv'] = landed[3:6], landed[6:]
    g['ssm_norm_w'] = d_ssm_nw[:SSM_GROUPS].reshape(D_INNER)
    g['ssm_dt_bias'] = d_bias[0, :SSM_HEADS]
    g['ssm_a_log'] = d_a[0, :SSM_HEADS] * a_p[0, :SSM_HEADS]
    g['ssm_d'] = d_d[0, :SSM_HEADS]
    dxbc, g['ssm_conv_w'], g['ssm_conv_b'] = conv_bwd(dxs, db_, dc_, xbc, conv_w, conv_b, "ssm_conv_bwd")
    d_win = mm_tn(dz, h01, "ssm_dwz", rows=IN_PROJ_DIM)
    d_win = mm_tn(dxbc, h01, "ssm_dwxbc", into=d_win, rows=IN_PROJ_DIM, row0=D_INNER)
    d_win = mm_tn(ddt, h01, "ssm_dwdt", into=d_win, rows=IN_PROJ_DIM, row0=D_INNER + CONV_DIM, m_valid=SSM_HEADS)
    d_win = jnp.pad(d_win.reshape(N_CHIPS, IN_SHARD, D_MODEL), ((0, 0), (0, IN_SHARD_PAD - IN_SHARD), (0, 0)))
    g_ssm = [_parts_first(d_win.reshape(-1, D_MODEL), IN_SHARD_PAD // 2), _parts_first(d_wout, 256)]
    kb = 1024
    terms = ([(dz, j, w_in_t, j, kb, "nn") for j in range(D_INNER // kb)]
             + [(dxbc, j, w_in_t, D_INNER // kb + j, kb, "nn") for j in range(CONV_DIM // kb)]
             + [(ddt, 0, w_dt_t, 0, LANES, "nn")])
    (dx1, dnw[0][1]), sib_ssm = mm_rms_bwd(terms, dx2, x1, nw[0, 1], "ssm_bwd_dh", ride=swap_cores(g_ssm, True))
    t_ssm = pre_reduce(g_ssm, sib_ssm, "ssm")
    (grad_x, dnw00, *g00), landed = ffn_bwd(dx1, h00, x0, nw[0, 0], gu00, *w00, "ffn_bwd_00",
                                            ride=join(scatter_chips(t_ssm), trade('01')))
    dnw[0][0] = dnw00[0]
    sums['ssm'] = chip_sum(landed[:2], t_ssm, "ssm")
    theirs['01'] = landed[2:]
    landed = run_exchange(join(swap_cores(g00, True), trade('ssm')), "rs_swap_00")
    t00 = pre_reduce(g00, landed[:3], "00")
    theirs['ssm'] = landed[3:]

    def both(key):
        return [(jnp.where(south, m_, t_), jnp.where(south, t_, m_)) for m_, t_ in zip(sums[key], theirs[key])]

    land00 = run_exchange(scatter_chips(t00), "rs_scatter_00")
    sums['00'] = chip_sum(land00, t00, "00")
    theirs['00'] = run_exchange(trade('00'), "rs_trade_00")

    delta, new_m, new_v, gw = {}, {}, {}, {}
    blocks = [both(key) for key in ('00', '01', '10', '11')]
    for t, n in enumerate(('ffn_w_gate', 'ffn_w_up', 'ffn_w_down')):
        gw[n] = jnp.concatenate([piece for blk in blocks for piece in blk[t]], axis=0).reshape(a[n].shape)
    full = {key: both(key) for key in ('attn', 'kv', 'ssm')}
    lo, hi = full['attn'][0]
    gw['attn_w_q'], gw['attn_w_o'] = lo[None], hi[None]
    lo, hi = full['kv'][0]
    gw['w_k'], gw['w_v'] = lo, hi
    lo, hi = full['ssm'][0]
    gw['ssm_w_in'] = jnp.concatenate([lo, hi], axis=0)[:IN_SHARD][None]
    lo, hi = full['ssm'][1]
    gw['ssm_w_out'] = jnp.concatenate([lo, hi], axis=0)[None]

    g['norm_w'] = jnp.stack([jnp.stack(r) for r in dnw])
    red = all_reduce_small(_pack_rows([g[n] for n in SMALL]), "reduce_vectors")
    for n, t in zip(SMALL, _unpack_rows(red, [g[n].shape for n in SMALL])):
        if n in SMALL_SHARDED:
            ax = SMALL_SHARDED[n] - (a[n].ndim - t.ndim)
            width = a[n].shape[SMALL_SHARDED[n]]
            t = lax.dynamic_slice_in_dim(t, chip * width, width, axis=ax)
        gw[n] = t.reshape(a[n].shape)

    for n in BIG:
        d, mo, vo = adamw(_as2d(a[n]), _as2d(gw[n]), _as2d(a['m_' + n]), _as2d(a['v_' + n]), "adamw_" + n)
        delta[n], new_m[n], new_v[n] = d.reshape(a[n].shape), mo.reshape(a[n].shape), vo.reshape(a[n].shape)
    shapes = [a[n].shape for n in SMALL]
    packed = [_pack_rows([src[n] for n in SMALL]) for src in
              (a, gw, {n: a['m_' + n] for n in SMALL}, {n: a['v_' + n] for n in SMALL})]
    outs = adamw(*packed, "adamw_vectors")
    for dst, buf in zip((delta, new_m, new_v), outs):
        for n, t in zip(SMALL, _unpack_rows(buf, shapes)):
            dst[n] = t
    for n in TRANSPOSED:
        for dst in (gw, delta, new_m, new_v):
            dst[n] = dst[n].swapaxes(-1, -2)

    return (loss, grad_x[None], *[gw[n] for n in WEIGHTS], *[delta[n] for n in WEIGHTS],
            *[new_m[n] for n in WEIGHTS], *[new_v[n] for n in WEIGHTS])
```

```python
import math

import jax
import jax.numpy as jnp
from jax import lax
from jax.experimental import pallas as pl
from jax.experimental.pallas import tpu as pltpu

F32 = jnp.float32
BF16 = jnp.bfloat16

D_MODEL = 1024
D_INNER = 2048
SSM_HEADS = 32
SSM_GROUPS = 4
HEADS_PER_GROUP = SSM_HEADS // SSM_GROUPS
SSM_HEAD_DIM = 64
SSM_STATE = 128
GROUP_DIM = D_INNER // SSM_GROUPS
CONV_DIM = D_INNER + 2 * SSM_GROUPS * SSM_STATE
CONV_WIDTH = 4
CHUNK = 128
ATT_HEAD_DIM = 64
N_Q_HEADS = 16
N_KV_HEADS = 4
Q_PER_KV = N_Q_HEADS // N_KV_HEADS
KV_DIM = N_KV_HEADS * ATT_HEAD_DIM
WINDOW = 128
ROPE_THETA = 10000.0
D_FF = 2816
N_CHIPS = 4
N_CORES = 2
FF_SHARD = D_FF // N_CHIPS
FF_PART = FF_SHARD // N_CORES
IN_PROJ_DIM = D_INNER + CONV_DIM + SSM_HEADS
IN_SHARD = IN_PROJ_DIM // N_CHIPS
IN_SHARD_PAD = 1312
EPS = 1e-5
NEG = -1e30
LANES = 128
VMEM_LIMIT = 56 * 1024 * 1024

ADAM_LR = 0.001
ADAM_B1 = 0.9
ADAM_B2 = 0.999
ADAM_EPS = 1e-08
ADAM_WD = 0.01
ADAM_STEP = 10

NN = ((1,), (0,))
NT = ((1,), (1,))
TN = ((0,), (0,))
MESH = pl.DeviceIdType.MESH
ANY = pl.BlockSpec(memory_space=pl.ANY)


def _dot(a, b, dims=NN, precision=None):
    return lax.dot_general(a, b, (dims, ((), ())), preferred_element_type=F32, precision=precision)


def _cp(n_grid):
    return pltpu.CompilerParams(dimension_semantics=("arbitrary",) * n_grid, vmem_limit_bytes=VMEM_LIMIT)


def _sigmoid(x):
    return 1.0 / (1.0 + jnp.exp(-x))


def _rms_fwd(xf, w):
    r = lax.rsqrt(jnp.mean(xf * xf, axis=-1, keepdims=True) + EPS)
    return xf * r * w


def _rms_bwd(dh, xf, w):
    r = lax.rsqrt(jnp.mean(xf * xf, axis=-1, keepdims=True) + EPS)
    xhat = xf * r
    dxhat = dh * w
    dx = r * (dxhat - xhat * jnp.mean(dxhat * xhat, axis=-1, keepdims=True))
    return dx, dh * xhat


def _row_tile(s, pref):
    return pref if s % pref == 0 else s


def _col_tile(n):
    for t in (1024, 768, 512, 256, 128):
        if n % t == 0:
            return t
    return n


def _sds(shape, dtype):
    return jax.ShapeDtypeStruct(tuple(shape), dtype)


class Exchange:
    def __init__(self, ins, out_shapes, sems, start, finish, inplace=False):
        self.ins, self.out_shapes, self.sems, self.start, self.finish = ins, out_shapes, sems, start, finish
        self.inplace = inplace


def _place():
    x, y, c = lax.axis_index("x"), lax.axis_index("y"), lax.axis_index("c")
    others = [(1 - x, y), (x, 1 - y), (1 - x, 1 - y)]
    return x, y, c, 2 * x + y, others


def _rc(src, dst, send_sem, recv_sem, dev):
    return pltpu.make_async_remote_copy(src_ref=src, dst_ref=dst, send_sem=send_sem, recv_sem=recv_sem,
                                        device_id=dev, device_id_type=MESH)


def gather_chips(arrs):
    n = len(arrs)

    def copies(ins, outs, sems):
        send, recv = sems
        x, y, c, k, others = _place()
        ici, land, fwd, fland = [], [], [], []
        for a in range(n):
            for j, (px, py) in enumerate(others):
                ici.append(_rc(ins[a].at[c], outs[a].at[c, k], send.at[a, j], recv.at[a, j], (px, py, c)))
                blk = outs[a].at[c, 2 * px + py]
                land.append(_rc(blk, blk, send.at[a, j], recv.at[a, j], (px, py, c)))
                fwd.append(_rc(blk, blk, send.at[a, 3 + j], recv.at[a, 3 + j], (x, y, 1 - c)))
                blk2 = outs[a].at[1 - c, 2 * px + py]
                fland.append(_rc(blk2, blk2, send.at[a, 3 + j], recv.at[a, 3 + j], (x, y, 1 - c)))
        return ici, land, fwd, fland

    def start(ins, outs, sems):
        for cp in copies(ins, outs, sems)[0]:
            cp.start()

    def finish(ins, outs, sems):
        ici, land, fwd, fland = copies(ins, outs, sems)
        for arrived, onward in zip(land, fwd):
            arrived.wait_recv()
            onward.start()
        for arrived in fland:
            arrived.wait_recv()
        for cp in ici + fwd:
            cp.wait_send()

    return Exchange(list(arrs), [_sds((2, N_CHIPS) + a.shape[1:], a.dtype) for a in arrs],
                    [pltpu.SemaphoreType.DMA((n, 6)), pltpu.SemaphoreType.DMA((n, 6))], start, finish)


def scatter_chips(arrs):
    n = len(arrs)

    def copies(ins, outs, sems):
        send, recv = sems
        x, y, c, k, others = _place()
        out, land = [], []
        for a in range(n):
            for j, (px, py) in enumerate(others):
                out.append(_rc(ins[a].at[2 * px + py], outs[a].at[k], send.at[a, j], recv.at[a, j], (px, py, c)))
                blk = outs[a].at[2 * px + py]
                land.append(_rc(blk, blk, send.at[a, j], recv.at[a, j], (px, py, c)))
        return out, land

    def start(ins, outs, sems):
        for cp in copies(ins, outs, sems)[0]:
            cp.start()

    def finish(ins, outs, sems):
        out, land = copies(ins, outs, sems)
        for arrived in land:
            arrived.wait_recv()
        for cp in out:
            cp.wait_send()

    return Exchange(list(arrs), [_sds(a.shape, a.dtype) for a in arrs],
                    [pltpu.SemaphoreType.DMA((n, 3)), pltpu.SemaphoreType.DMA((n, 3))], start, finish)


def swap_cores(arrs, pick_other):
    n = len(arrs)

    def copies(ins, outs, sems):
        send, recv = sems
        x, y, c, _, _ = _place()
        return [_rc(ins[a].at[1 - c] if pick_other else ins[a], outs[a], send.at[a], recv.at[a], (x, y, 1 - c))
                for a in range(n)]

    def start(ins, outs, sems):
        for cp in copies(ins, outs, sems):
            cp.start()

    def finish(ins, outs, sems):
        for cp in copies(ins, outs, sems):
            cp.wait()

    shapes = [_sds(a.shape[1:] if pick_other else a.shape, a.dtype) for a in arrs]
    return Exchange(list(arrs), shapes, [pltpu.SemaphoreType.DMA((n,)), pltpu.SemaphoreType.DMA((n,))],
                    start, finish)


def join(*parts):
    parts = [p for p in parts if p is not None]
    if not parts:
        return None

    def split(refs, counts):
        out, pos = [], 0
        for cnt in counts:
            out.append(refs[pos:pos + cnt])
            pos += cnt
        return out

    n_in = [len(p.ins) for p in parts]
    n_out = [len(p.out_shapes) for p in parts]
    n_sem = [len(p.sems) for p in parts]

    def run(which):
        def go(ins, outs, sems):
            for p, i, o, s in zip(parts, split(ins, n_in), split(outs, n_out), split(sems, n_sem)):
                getattr(p, which)(i, o, s)
        return go

    return Exchange([a for p in parts for a in p.ins], [s for p in parts for s in p.out_shapes],
                    [s for p in parts for s in p.sems], run("start"), run("finish"))


def _pcall(body, *, out_shape, grid, in_specs, out_specs, args, name, scratch_shapes=(), ride=None, aliases=None):
    out_shape, out_specs, in_specs = tuple(out_shape), tuple(out_specs), list(in_specs)
    aliases = aliases or {}
    if ride is None:
        return pl.pallas_call(body, out_shape=out_shape, grid=grid, in_specs=in_specs, out_specs=out_specs,
                              scratch_shapes=list(scratch_shapes), input_output_aliases=aliases, name=name,
                              compiler_params=_cp(len(grid)))(*args)
    n_in, n_out, n_sc = len(args), len(out_shape), len(scratch_shapes)
    n_xi, n_xo = len(ride.ins), len(ride.out_shapes)

    def wrapped(*refs):
        pos = [0]

        def take(cnt):
            got = refs[pos[0]:pos[0] + cnt]
            pos[0] += cnt
            return got

        c_in, x_in, c_out, x_out, c_sc = take(n_in), take(n_xi), take(n_out), take(n_xo), take(n_sc)
        sems = refs[pos[0]:]
        first, last = True, True
        for d, size in enumerate(grid):
            first = jnp.logical_and(first, pl.program_id(d) == 0)
            last = jnp.logical_and(last, pl.program_id(d) == size - 1)

        @pl.when(first)
        def _():
            ride.start(x_in, x_out, sems)

        body(*c_in, *c_out, *c_sc)

        @pl.when(last)
        def _():
            ride.finish(x_in, x_out, sems)

    if ride.inplace:
        aliases = {**aliases, **{n_in + t: n_out + t for t in range(n_xi)}}
    res = pl.pallas_call(
        wrapped, out_shape=out_shape + tuple(ride.out_shapes), grid=grid,
        in_specs=in_specs + [ANY] * n_xi, out_specs=out_specs + (ANY,) * n_xo,
        scratch_shapes=list(scratch_shapes) + list(ride.sems), input_output_aliases=aliases, name=name,
        compiler_params=_cp(len(grid)))(*args, *ride.ins)
    return res[:n_out], res[n_out:]


def run_exchange(ex, name):
    n_xi, n_xo = len(ex.ins), len(ex.out_shapes)

    def body(*refs):
        ins, outs, sems = refs[:n_xi], refs[n_xi:n_xi + n_xo], refs[n_xi + n_xo:]
        ex.start(ins, outs, sems)
        ex.finish(ins, outs, sems)

    aliases = {t: t for t in range(n_xi)} if ex.inplace else {}
    return pl.pallas_call(body, out_shape=tuple(ex.out_shapes), in_specs=[ANY] * n_xi, out_specs=(ANY,) * n_xo,
                          scratch_shapes=list(ex.sems), input_output_aliases=aliases, name=name)(*ex.ins)


HBM_SPEC = pl.BlockSpec(memory_space=pltpu.HBM)
SEM_SPEC = pl.BlockSpec(memory_space=pltpu.SEMAPHORE)
EFFECT = pltpu.SideEffectType.DATAFLOW_SIDE_EFFECTING


def gather_start(batches, name):
    flat = [a for batch in batches for a in batch]
    n, nb = len(flat), len(batches)
    lands = [lax.empty((2, N_CHIPS) + a.shape[1:], a.dtype) for a in flat]

    def body(*refs):
        srcs, dsts, sems, token = refs[:n], refs[n:2 * n], refs[2 * n:2 * n + 2 * nb], refs[-1]
        x, y, c, k, others = _place()
        pos = 0
        for b, batch in enumerate(batches):
            for a in range(len(batch)):
                for j, (px, py) in enumerate(others):
                    _rc(srcs[pos].at[c], dsts[pos].at[c, k], sems[2 * b].at[3 * a + j], sems[2 * b + 1].at[3 * a + j],
                        (px, py, c)).start()
                pos += 1
        token[...] = jnp.zeros(token.shape, token.dtype)

    sem_shapes = [pltpu.SemaphoreType.DMA((3 * len(batch),)) for batch in batches for _ in range(2)]
    thru = [pltpu.HBM(a.shape, a.dtype) for a in flat] + [pltpu.HBM(l.shape, l.dtype) for l in lands]
    res = pl.pallas_call(
        body, name=name, out_shape=tuple(sem_shapes + thru + [_sds((8, LANES), F32)]),
        in_specs=[HBM_SPEC] * (2 * n),
        out_specs=tuple([SEM_SPEC] * (2 * nb) + [HBM_SPEC] * (2 * n) + [pl.BlockSpec(memory_space=pltpu.VMEM)]),
        input_output_aliases={t: 2 * nb + t for t in range(2 * n)},
        compiler_params=pltpu.CompilerParams(has_side_effects=EFFECT),
    )(*[pltpu.with_memory_space_constraint(t, pltpu.HBM) for t in flat + lands])
    sems, srcs, dsts = res[:2 * nb], res[2 * nb:2 * nb + n], res[2 * nb + n:2 * nb + 2 * n]
    out, pos = [], 0
    for b, batch in enumerate(batches):
        out.append((sems[2 * b], sems[2 * b + 1], list(srcs[pos:pos + len(batch)]), list(dsts[pos:pos + len(batch)])))
        pos += len(batch)
    return out


def gather_arrive(handle, after, name):
    send, recv, srcs, dsts = handle
    n = len(srcs)

    def body(*refs):
        s_refs, d_refs, send_ref, recv_ref = refs[:n], refs[n:2 * n], refs[2 * n], refs[2 * n + 1]
        x, y, c, k, others = _place()
        for a in range(n):
            for j, (px, py) in enumerate(others):
                cp = _rc(s_refs[a].at[c], d_refs[a].at[c, 2 * px + py], send_ref.at[3 * a + j], recv_ref.at[3 * a + j],
                         (px, py, c))
                cp.wait_send()
                cp.wait_recv()

    res = pl.pallas_call(
        body, name=name, out_shape=tuple([pltpu.HBM(t.shape, t.dtype) for t in srcs + dsts]),
        in_specs=[HBM_SPEC] * (2 * n) + [SEM_SPEC, SEM_SPEC, ANY], out_specs=tuple([HBM_SPEC] * (2 * n)),
        input_output_aliases={t: t for t in range(2 * n)},
        compiler_params=pltpu.CompilerParams(has_side_effects=EFFECT),
    )(*srcs, *dsts, send, recv, after)
    return list(res[n:])


def forward_cores(bufs):
    n = len(bufs)

    def copies(outs, sems):
        send, recv = sems
        x, y, c, k, others = _place()
        onward, land = [], []
        for a in range(n):
            for j, (px, py) in enumerate(others):
                blk = outs[a].at[c, 2 * px + py]
                onward.append(_rc(blk, blk, send.at[a, j], recv.at[a, j], (x, y, 1 - c)))
                blk2 = outs[a].at[1 - c, 2 * px + py]
                land.append(_rc(blk2, blk2, send.at[a, j], recv.at[a, j], (x, y, 1 - c)))
        return onward, land

    def start(ins, outs, sems):
        for cp in copies(outs, sems)[0]:
            cp.start()

    def finish(ins, outs, sems):
        onward, land = copies(outs, sems)
        for arrived in land:
            arrived.wait_recv()
        for cp in onward:
            cp.wait_send()

    return Exchange(list(bufs), [_sds(b.shape, b.dtype) for b in bufs],
                    [pltpu.SemaphoreType.DMA((n, 3)), pltpu.SemaphoreType.DMA((n, 3))], start, finish, inplace=True)


def all_reduce_small(buf, name):
    r = buf.shape[0]
    n_dev = 8

    def body(in_ref, o_ref, land, send_sems, recv_sems):
        x, y, c, _, _ = _place()
        me = 4 * x + 2 * y + c
        land[me] = in_ref[...]
        sends = []
        for d in range(1, n_dev):
            peer = (x ^ (d >> 2), y ^ ((d >> 1) & 1), c ^ (d & 1))
            cp = _rc(in_ref, land.at[me], send_sems.at[d], recv_sems.at[d], peer)
            cp.start()
            sends.append(cp)
        for d in range(1, n_dev):
            blk = land.at[me ^ d]
            _rc(blk, blk, send_sems.at[d], recv_sems.at[d], (x, y, c)).wait_recv()
        for cp in sends:
            cp.wait_send()
        tot = land[0]
        for d in range(1, n_dev):
            tot = tot + land[d]
        o_ref[...] = tot

    vm = pl.BlockSpec(memory_space=pltpu.VMEM)
    return pl.pallas_call(
        body, out_shape=_sds(buf.shape, F32), in_specs=[vm], out_specs=vm,
        scratch_shapes=[pltpu.VMEM((n_dev, r, LANES), F32), pltpu.SemaphoreType.DMA((n_dev,)),
                        pltpu.SemaphoreType.DMA((n_dev,))],
        name=name)(buf)


def rmsnorm_fwd(x, w, name):
    s, d = x.shape
    tm = _row_tile(s, 512)

    def body(x_ref, w_ref, o_ref):
        o_ref[...] = _rms_fwd(x_ref[...], w_ref[...]).astype(BF16)

    return _pcall(body, out_shape=[_sds((s, d), BF16)], grid=(s // tm,),
                  in_specs=[pl.BlockSpec((tm, d), lambda i: (i, 0)), pl.BlockSpec((1, d), lambda i: (0, 0))],
                  out_specs=[pl.BlockSpec((tm, d), lambda i: (i, 0))], args=[x, w.reshape(1, d)], name=name)[0]


def _ffn_w_spec(chip_of, single=False):
    mode = dict(pipeline_mode=pl.Buffered(1)) if single else {}
    return pl.BlockSpec((N_CORES, 1, FF_PART, D_MODEL), lambda *ids: (0, chip_of(*ids), 0, 0), **mode)


def ffn_fwd(h, x, wg, wu, wd, norm_ws, name, ride=None):
    s, d = h.shape
    n_norm = len(norm_ws)
    tm = _row_tile(s, 1024)

    def body(*refs):
        h_ref, x_ref, wg_ref, wu_ref, wd_ref = refs[:5]
        nw_refs = refs[5:5 + n_norm]
        o_ref = refs[5 + n_norm]
        h_refs = refs[6 + n_norm:6 + 2 * n_norm]
        gu_ref, acc = refs[6 + 2 * n_norm], refs[7 + 2 * n_norm]
        k = pl.program_id(1)

        @pl.when(k == 0)
        def _():
            acc[...] = jnp.zeros(acc.shape, F32)

        hm = tm // 2
        for part in range(2):
            sub = pl.ds(part * hm, hm)
            hb = h_ref[sub, :]
            g = _dot(hb, wg_ref[...].reshape(FF_SHARD, d), NT)
            u = _dot(hb, wu_ref[...].reshape(FF_SHARD, d), NT)
            gu_ref[0, 0, sub, :] = g.astype(BF16)
            gu_ref[0, 1, sub, :] = u.astype(BF16)
            acc[sub, :] += _dot((g * _sigmoid(g) * u).astype(BF16), wd_ref[...].reshape(FF_SHARD, d))

        @pl.when(k == N_CHIPS - 1)
        def _():
            xn = x_ref[...] + 0.5 * acc[...]
            o_ref[...] = xn
            for nw_ref, hn_ref in zip(nw_refs, h_refs):
                hn_ref[...] = _rms_fwd(xn, nw_ref[...]).astype(BF16)

    row = pl.BlockSpec((tm, d), lambda i, k: (i, 0))
    vec = pl.BlockSpec((1, d), lambda i, k: (0, 0))
    wsp = _ffn_w_spec(lambda i, k: k)
    return _pcall(
        body, out_shape=[_sds((s, d), F32)] + [_sds((s, d), BF16)] * n_norm + [_sds((N_CHIPS, 2, s, FF_SHARD), BF16)],
        grid=(s // tm, N_CHIPS),
        in_specs=[row, row, wsp, wsp, wsp] + [vec] * n_norm,
        out_specs=[row] * (1 + n_norm) + [pl.BlockSpec((1, 2, tm, FF_SHARD), lambda i, k: (k, 0, i, 0))],
        scratch_shapes=[pltpu.VMEM((tm, d), F32)],
        args=[h, x, wg, wu, wd] + [nw.reshape(1, d) for nw in norm_ws], name=name, ride=ride)


def ffn_bwd(dxn, h, x_in, nw, gu, wg, wu, wd, name, ride=None):
    s, d = h.shape
    tm = _row_tile(s, 512)
    ni = s // tm
    last_e = N_CHIPS - 1

    def body(dxn_ref, h_ref, x_ref, nw_ref, gu_ref, wg_ref, wu_ref, wd_ref,
             dx_ref, dnw_ref, dwg_ref, dwu_ref, dwd_ref, dh, wacc):
        e = pl.program_id(0)
        i = pl.program_id(1)
        rows = pl.ds(pl.multiple_of(i * tm, tm), tm)

        @pl.when(i == 0)
        def _():
            wacc[...] = jnp.zeros(wacc.shape, F32)

        @pl.when(e == 0)
        def _():
            dh[rows, :] = jnp.zeros((tm, d), F32)

        hm = tm // 2
        for part in range(2):
            sub = pl.ds(part * hm, hm)
            dxb = dxn_ref[sub, :].astype(BF16)
            hb = h_ref[sub, :]
            g = gu_ref[0, 0, sub, :].astype(F32)
            u = gu_ref[0, 1, sub, :].astype(F32)
            drows = pl.ds(pl.multiple_of(i * tm + part * hm, hm), hm)
            sg = _sigmoid(g)
            silu = g * sg
            wacc[2] += _dot((0.5 * silu * u).astype(BF16), dxb, TN)
            da = 0.5 * _dot(dxb, wd_ref[...].reshape(FF_SHARD, d), NT)
            dg = (da * u * (sg * (1.0 + g * (1.0 - sg)))).astype(BF16)
            wacc[0] += _dot(dg, hb, TN)
            du = (da * silu).astype(BF16)
            dh[drows, :] += _dot(dg, wg_ref[...].reshape(FF_SHARD, d))
            wacc[1] += _dot(du, hb, TN)
            dh[drows, :] += _dot(du, wu_ref[...].reshape(FF_SHARD, d))

        @pl.when(i == ni - 1)
        def _():
            for t, dw_ref in enumerate((dwg_ref, dwu_ref, dwd_ref)):
                dw_ref[...] = wacc[t].astype(BF16).reshape(N_CORES, 1, FF_PART, d)

        @pl.when(e == last_e)
        def _():
            dx, dnw = _rms_bwd(dh[rows, :], x_ref[...], nw_ref[...])
            dx_ref[...] = dxn_ref[...] + dx
            col = jnp.sum(dnw, axis=0, keepdims=True)

            @pl.when(i == 0)
            def _():
                dnw_ref[...] = col

            @pl.when(i > 0)
            def _():
                dnw_ref[...] += col

    row = pl.BlockSpec((tm, d), lambda e, i: (i, 0))
    late = pl.BlockSpec((tm, d), lambda e, i: (jnp.where(e == last_e, i, 0), 0))
    vec = pl.BlockSpec((1, d), lambda e, i: (0, 0))
    wsp = _ffn_w_spec(lambda e, i: e, single=True)
    dwsp = _ffn_w_spec(lambda e, i: e, single=True)
    dw = _sds((N_CORES, N_CHIPS, FF_PART, d), BF16)
    return _pcall(
        body, out_shape=[_sds((s, d), F32), _sds((1, d), F32), dw, dw, dw],
        grid=(N_CHIPS, ni),
        in_specs=[row, row, late, vec, pl.BlockSpec((1, 2, tm, FF_SHARD), lambda e, i: (e, 0, i, 0)), wsp, wsp, wsp],
        out_specs=[late, vec, dwsp, dwsp, dwsp],
        scratch_shapes=[pltpu.VMEM((s, d), F32), pltpu.VMEM((3, FF_SHARD, d), F32)],
        args=[dxn, h, x_in, nw.reshape(1, d), gu, wg, wu, wd], name=name, ride=ride)


def mm_res(a, w, x, name, bias=None, norm_ws=(), ride=None):
    s, k = a.shape
    n = w.shape[1]
    tm = _row_tile(s, 256)
    has_bias = bias is not None
    n_norm = len(norm_ws)

    def body(*refs):
        a_ref, w_ref, x_ref = refs[:3]
        pos = 3
        t = _dot(a_ref[...], w_ref[...])
        if has_bias:
            t = t + refs[pos][...]
            pos += 1
        nw_refs = refs[pos:pos + n_norm]
        o_ref = refs[pos + n_norm]
        h_refs = refs[pos + n_norm + 1:]
        xn = x_ref[...] + t
        o_ref[...] = xn
        for nw_ref, h_ref in zip(nw_refs, h_refs):
            h_ref[...] = _rms_fwd(xn, nw_ref[...]).astype(BF16)

    row = pl.BlockSpec((tm, n), lambda i: (i, 0))
    vec = pl.BlockSpec((1, n), lambda i: (0, 0))
    in_specs = [pl.BlockSpec((tm, k), lambda i: (i, 0)), pl.BlockSpec((k, n), lambda i: (0, 0)), row]
    args = [a, w, x]
    if has_bias:
        in_specs.append(vec)
        args.append(bias.reshape(1, n))
    for nw in norm_ws:
        in_specs.append(vec)
        args.append(nw.reshape(1, n))
    return _pcall(body, out_shape=[_sds((s, n), F32)] + [_sds((s, n), BF16)] * n_norm, grid=(s // tm,),
                  in_specs=in_specs, out_specs=[row] * (1 + n_norm), args=args, name=name, ride=ride)


def mm_nn(a, w, name, bias=None, out_dtype=F32):
    s, k = a.shape
    n = w.shape[1]
    tm = _row_tile(s, 512)
    tn = _col_tile(n)
    has_bias = bias is not None

    def body(*refs):
        a_ref, w_ref = refs[:2]
        o_ref = refs[-1]
        t = _dot(a_ref[...], w_ref[...])
        if has_bias:
            t = t + refs[2][...]
        o_ref[...] = t.astype(out_dtype)

    in_specs = [pl.BlockSpec((tm, k), lambda j, i: (i, 0)), pl.BlockSpec((k, tn), lambda j, i: (0, j))]
    args = [a, w]
    if has_bias:
        in_specs.append(pl.BlockSpec((1, tn), lambda j, i: (0, j)))
        args.append(bias.reshape(1, n))
    return _pcall(body, out_shape=[_sds((s, n), out_dtype)], grid=(n // tn, s // tm), in_specs=in_specs,
                  out_specs=[pl.BlockSpec((tm, tn), lambda j, i: (i, j))], args=args, name=name)[0]


def mm_nt(a, w, name, n=None, row0=0, out_dtype=F32, ride=None):
    s, k = a.shape
    n = w.shape[0] if n is None else n
    tm = _row_tile(s, 512)
    tn = _col_tile(n)
    base = row0 // tn
    assert row0 % tn == 0

    def body(a_ref, w_ref, o_ref):
        o_ref[...] = _dot(a_ref[...].astype(BF16), w_ref[...], NT).astype(out_dtype)

    res = _pcall(body, out_shape=[_sds((s, n), out_dtype)], grid=(n // tn, s // tm),
                 in_specs=[pl.BlockSpec((tm, k), lambda j, i: (i, 0)), pl.BlockSpec((tn, k), lambda j, i: (base + j, 0))],
                 out_specs=[pl.BlockSpec((tm, tn), lambda j, i: (i, j))], args=[a, w], name=name, ride=ride)
    return res[0] if ride is None else (res[0][0], res[1])


def mm_tn(a, b, name, into=None, rows=None, row0=0, m_valid=None, col_sum=False, ride=None):
    s, m = a.shape
    n = b.shape[1]
    mv = m if m_valid is None else m_valid
    tm = _col_tile(m) if m_valid is None else mv
    tn = n if n <= 1024 else _col_tile(n)
    rows = mv if rows is None else rows
    assert row0 % tm == 0 and (m_valid is None or m == LANES)
    assert not col_sum or mv == tm
    base = row0 // tm
    ta = m if m_valid is not None else tm

    def body(*refs):
        a_ref, b_ref = refs[0], refs[1]
        o_ref = refs[-2] if col_sum else refs[-1]
        bf = b_ref[...]
        t = _dot(a_ref[...].astype(BF16), bf.astype(BF16), TN)
        o_ref[...] = t[:tm].astype(BF16)
        if col_sum:
            refs[-1][...] = jnp.sum(bf.astype(F32), axis=0, keepdims=True)

    in_specs = [pl.BlockSpec((s, ta), lambda i, j: (0, i)), pl.BlockSpec((s, tn), lambda i, j: (0, j))]
    args = [a, b]
    aliases = None
    if into is not None:
        in_specs.append(ANY)
        args.append(into)
        aliases = {2: 0}
    out_shape = [_sds((rows, n), BF16)]
    out_specs = [pl.BlockSpec((tm, tn), lambda i, j: (base + i, j))]
    if col_sum:
        out_shape.append(_sds((1, n), F32))
        out_specs.append(pl.BlockSpec((1, tn), lambda i, j: (0, j)))
    res = _pcall(body, out_shape=out_shape, grid=(mv // tm, n // tn), in_specs=in_specs, out_specs=out_specs,
                 args=args, name=name, ride=ride, aliases=aliases)
    outs = res if ride is None else res[0]
    out = (outs[0], outs[1][0]) if col_sum else outs[0]
    return out if ride is None else (out, res[1])


def mm_rms_bwd(terms, dxn, x, nw, name, ride=None):
    s, n = x.shape
    nt_ = len(terms)
    tm = _row_tile(s, 256)
    forms = [t[5] for t in terms]

    def body(*refs):
        dxn_ref, x_ref, nw_ref, dx_ref, dnw_ref = refs[2 * nt_:]
        i = pl.program_id(0)
        dh = None
        for t in range(nt_):
            part = _dot(refs[2 * t][...].astype(BF16), refs[2 * t + 1][...], NN if forms[t] == "nn" else NT)
            dh = part if dh is None else dh + part
        dx, dnw = _rms_bwd(dh, x_ref[...], nw_ref[...])
        dx_ref[...] = dxn_ref[...] + dx
        col = jnp.sum(dnw, axis=0, keepdims=True)

        @pl.when(i == 0)
        def _():
            dnw_ref[...] = col

        @pl.when(i > 0)
        def _():
            dnw_ref[...] += col

    in_specs, args = [], []
    for a, cb, w, rb, kb, form in terms:
        in_specs.append(pl.BlockSpec((tm, kb), lambda i, cb=cb: (i, cb)))
        if form == "nn":
            in_specs.append(pl.BlockSpec((kb, n), lambda i, rb=rb: (rb, 0)))
        else:
            in_specs.append(pl.BlockSpec((n, kb), lambda i, rb=rb: (0, rb)))
        args += [a, w]
    row = pl.BlockSpec((tm, n), lambda i: (i, 0))
    vec = pl.BlockSpec((1, n), lambda i: (0, 0))
    res = _pcall(body, out_shape=[_sds((s, n), F32), _sds((1, n), F32)], grid=(s // tm,),
                 in_specs=in_specs + [row, row, vec], out_specs=[row, vec],
                 args=args + [dxn, x, nw.reshape(1, n)], name=name, ride=ride)
    outs = res if ride is None else res[0]
    out = (outs[0], outs[1][0])
    return out if ride is None else (out, res[1])


def rope_tables(s):
    pos = jnp.arange(s, dtype=F32)
    inv = 1.0 / (ROPE_THETA ** (jnp.arange(0, ATT_HEAD_DIM, 2, dtype=F32) / ATT_HEAD_DIM))
    ang = pos[:, None] * inv[None, :]
    cos = jnp.tile(jnp.cos(ang), (1, 2 * LANES // ATT_HEAD_DIM))
    sin = jnp.tile(jnp.sin(ang), (1, 2 * LANES // ATT_HEAD_DIM))
    return cos, sin


def rope_apply(t, cos, sin, name, inverse=False, scale=1.0, out_dtype=BF16):
    s, n = t.shape
    tm = _row_tile(s, 512)
    half = ATT_HEAD_DIM // 2
    reps = n // LANES

    def body(t_ref, c_ref, s_ref, o_ref):
        tf = t_ref[...].astype(F32)
        c = jnp.tile(c_ref[...], (1, reps))
        sn = jnp.tile(s_ref[...], (1, reps))
        lane = lax.broadcasted_iota(jnp.int32, tf.shape, 1)
        first = (lane & (ATT_HEAD_DIM - 1)) < half
        rot = jnp.where(first, -pltpu.roll(tf, n - half, 1), pltpu.roll(tf, half, 1))
        sign = -1.0 if inverse else 1.0
        o_ref[...] = (scale * (tf * c + sign * rot * sn)).astype(out_dtype)

    tab = pl.BlockSpec((tm, LANES), lambda i: (i, 0))
    return _pcall(body, out_shape=[_sds((s, n), out_dtype)], grid=(s // tm,),
                  in_specs=[pl.BlockSpec((tm, n), lambda i: (i, 0)), tab, tab],
                  out_specs=[pl.BlockSpec((tm, n), lambda i: (i, 0))], args=[t, cos, sin], name=name)[0]


CONV_TILE = 256


def _shift_down(u, k):
    if k == 0:
        return u
    row = lax.broadcasted_iota(jnp.int32, u.shape, 0)
    return jnp.where(row >= k, pltpu.roll(u, k, 0), 0.0)


def _shift_up(u, k):
    if k == 0:
        return u
    s = u.shape[0]
    row = lax.broadcasted_iota(jnp.int32, u.shape, 0)
    return jnp.where(row < s - k, pltpu.roll(u, s - k, 0), 0.0)


def _conv_pre(u, w_ref, b_ref):
    pre = b_ref[...] + w_ref[CONV_WIDTH - 1:CONV_WIDTH, :] * u
    for k in range(CONV_WIDTH - 1):
        pre += w_ref[k:k + 1, :] * _shift_down(u, CONV_WIDTH - 1 - k)
    return pre


def conv_fwd(u, w, b, name, ride=None):
    s, c = u.shape

    def body(u_ref, w_ref, b_ref, o_ref):
        pre = _conv_pre(u_ref[...], w_ref, b_ref)
        o_ref[...] = pre * _sigmoid(pre)

    col = pl.BlockSpec((s, CONV_TILE), lambda j: (0, j))
    res = _pcall(body, out_shape=[_sds((s, c), F32)], grid=(c // CONV_TILE,),
                 in_specs=[col, pl.BlockSpec((CONV_WIDTH, CONV_TILE), lambda j: (0, j)),
                           pl.BlockSpec((1, CONV_TILE), lambda j: (0, j))],
                 out_specs=[col], args=[u, w, b.reshape(1, c)], name=name, ride=ride)
    return res[0] if ride is None else (res[0][0], res[1])


def conv_bwd(dxs, db_, dc_, u, w, b, name):
    s, c = u.shape
    n_x = dxs.shape[1] // CONV_TILE
    n_b = db_.shape[1] // CONV_TILE

    def body(dx_ref, dbb_ref, dcc_ref, u_ref, w_ref, b_ref, du_ref, dw_ref, dbias_ref):
        j = pl.program_id(0)
        dact = jnp.where(j < n_x, dx_ref[...], jnp.where(j < n_x + n_b, dbb_ref[...], dcc_ref[...]))
        uf = u_ref[...]
        pre = _conv_pre(uf, w_ref, b_ref)
        sg = _sigmoid(pre)
        dpre = dact * (sg * (1.0 + pre * (1.0 - sg)))
        du = w_ref[CONV_WIDTH - 1:CONV_WIDTH, :] * dpre
        for k in range(CONV_WIDTH - 1):
            du += w_ref[k:k + 1, :] * _shift_up(dpre, CONV_WIDTH - 1 - k)
        du_ref[...] = du
        dbias_ref[...] = jnp.sum(dpre, axis=0, keepdims=True)
        for k in range(CONV_WIDTH):
            dw_ref[k:k + 1, :] = jnp.sum(dpre * _shift_down(uf, CONV_WIDTH - 1 - k), axis=0, keepdims=True)

    col = pl.BlockSpec((s, CONV_TILE), lambda j: (0, j))
    wsp = pl.BlockSpec((CONV_WIDTH, CONV_TILE), lambda j: (0, j))
    bsp = pl.BlockSpec((1, CONV_TILE), lambda j: (0, j))
    du, dw, db = _pcall(
        body, out_shape=[_sds((s, c), F32), _sds((CONV_WIDTH, c), F32), _sds((1, c), F32)], grid=(c // CONV_TILE,),
        in_specs=[pl.BlockSpec((s, CONV_TILE), lambda j: (0, jnp.minimum(j, n_x - 1))),
                  pl.BlockSpec((s, CONV_TILE), lambda j: (0, jnp.clip(j - n_x, 0, n_b - 1))),
                  pl.BlockSpec((s, CONV_TILE), lambda j: (0, jnp.clip(j - n_x - n_b, 0, n_b - 1))),
                  col, wsp, bsp],
        out_specs=[col, wsp, bsp], args=[dxs, db_, dc_, u, w, b.reshape(1, c)], name=name)
    return du, dw, db[0]


def _lane_pick(mat, idx):
    lane = lax.broadcasted_iota(jnp.int32, mat.shape, 1)
    return jnp.sum(jnp.where(lane == idx, mat, 0.0), axis=1, keepdims=True)


def _sub_pick(mat, idx):
    sub = lax.broadcasted_iota(jnp.int32, mat.shape, 0)
    return jnp.sum(jnp.where(sub == idx, mat, 0.0), axis=0, keepdims=True)


def _expand_heads(cols):
    rows = cols[0].shape[0]
    left = lax.broadcasted_iota(jnp.int32, (rows, LANES), 1) < SSM_HEAD_DIM
    return jnp.concatenate(
        [jnp.where(left, cols[2 * p], cols[2 * p + 1]) for p in range(HEADS_PER_GROUP // 2)], axis=1)


def _dot_01(x, ones, ones_first, pieces):
    tot, rest = None, x
    for _ in range(pieces):
        piece = rest.astype(BF16)
        rest = rest - piece.astype(F32)
        part = _dot(ones, piece) if ones_first else _dot(piece, ones)
        tot = part if tot is None else tot + part
    return tot


def _heads_to_lanes(mat, g):
    jj = lax.broadcasted_iota(jnp.int32, (GROUP_DIM, LANES), 0)
    ll = lax.broadcasted_iota(jnp.int32, (GROUP_DIM, LANES), 1)
    sel = (ll == HEADS_PER_GROUP * g + (jj >> 6)).astype(BF16)
    return _dot_01(mat, sel, False, 3)


def _softplus(x):
    return jnp.maximum(x, 0.0) + jnp.log1p(jnp.exp(-jnp.abs(x)))


def _ssd_scalars(dt_ref, bias_ref, a_ref, dtall, csall, cst):
    dta = _softplus(dt_ref[...] + bias_ref[...])
    row = lax.broadcasted_iota(jnp.int32, (CHUNK, CHUNK), 0)
    col = lax.broadcasted_iota(jnp.int32, (CHUNK, CHUNK), 1)
    cs = _dot_01(dta * a_ref[...], (row >= col).astype(BF16), True, 3)
    dtall[...] = dta
    csall[...] = cs
    cst[...] = cs.T


def _decay_mat(cs_col, cs_row):
    row = lax.broadcasted_iota(jnp.int32, (CHUNK, CHUNK), 0)
    col = lax.broadcasted_iota(jnp.int32, (CHUNK, CHUNK), 1)
    return jnp.exp(jnp.where(row >= col, cs_col - cs_row, NEG))


def _head_mask(xpair, right):
    lane = lax.broadcasted_iota(jnp.int32, xpair.shape, 1)
    keep = (lane >= SSM_HEAD_DIM) if right else (lane < SSM_HEAD_DIM)
    return jnp.where(keep, xpair, 0.0)


def _chunk_cols(x_all, g):
    return [_lane_pick(x_all, HEADS_PER_GROUP * g + r) for r in range(HEADS_PER_GROUP)]


def _decay_col(cs_cols):
    return jnp.concatenate(
        [jnp.broadcast_to(jnp.exp(cc[CHUNK - 1:CHUNK, :]), (SSM_HEAD_DIM, 1)) for cc in cs_cols], axis=0)


def ssd_fwd(act, z, dtp, bias_p, a_p, d_p, normw, name, ride=None):
    s = act.shape[0]
    nc = s // CHUNK
    b_off = D_INNER // SSM_STATE
    c_off = b_off + SSM_GROUPS

    def body(xs_ref, b_ref, c_ref, z_ref, dt_ref, bias_ref, a_ref, d_ref, nw_ref,
             yn_ref, y_ref, st_ref, state, dtall, csall, cst):
        c = pl.program_id(0)
        g = pl.program_id(1)

        @pl.when(g == 0)
        def _():
            _ssd_scalars(dt_ref, bias_ref, a_ref, dtall, csall, cst)

        @pl.when(c == 0)
        def _():
            state[g] = jnp.zeros((GROUP_DIM, SSM_STATE), F32)

        cs_cols = _chunk_cols(csall[...], g)
        dt_cols = _chunk_cols(dtall[...], g)
        cs_rows = [_sub_pick(cst[...], HEADS_PER_GROUP * g + r) for r in range(HEADS_PER_GROUP)]
        d_cols = _chunk_cols(d_ref[...], g)
        cs_exp = _expand_heads(cs_cols)
        dt_exp = _expand_heads(dt_cols)
        d_exp = _expand_heads(d_cols)
        xs = xs_ref[...]
        bb = b_ref[...].astype(BF16)
        cb16 = c_ref[...].astype(BF16)
        xdt = xs * dt_exp
        s_prev = state[g]
        st_ref[0, 0] = s_prev
        y_off = _dot(cb16, s_prev.astype(BF16), NT) * jnp.exp(cs_exp)
        decay_st = jnp.exp(cs_exp[CHUNK - 1:CHUNK, :] - cs_exp)
        contrib = _dot((xdt * decay_st).astype(BF16), bb, TN)
        state[g] = _decay_col(cs_cols) * s_prev + contrib
        cbm = _dot(cb16, bb, NT)
        pairs = []
        for p in range(HEADS_PER_GROUP // 2):
            xpair = xdt[:, LANES * p:LANES * (p + 1)]
            m0 = (cbm * _decay_mat(cs_cols[2 * p], cs_rows[2 * p])).astype(BF16)
            m1 = (cbm * _decay_mat(cs_cols[2 * p + 1], cs_rows[2 * p + 1])).astype(BF16)
            pairs.append(_dot(m0, _head_mask(xpair, False).astype(BF16))
                         + _dot(m1, _head_mask(xpair, True).astype(BF16)))
        y = jnp.concatenate(pairs, axis=1) + y_off + xs * d_exp
        y_ref[...] = y
        zf = z_ref[...]
        yg = y * (zf * _sigmoid(zf))
        yn_ref[...] = _rms_fwd(yg, nw_ref[...]).astype(BF16)

    grp = pl.BlockSpec((CHUNK, GROUP_DIM), lambda c, g: (c, g))
    par = pl.BlockSpec((1, LANES), lambda c, g: (0, 0))
    return _pcall(
        body,
        out_shape=[_sds((s, D_INNER), BF16), _sds((s, D_INNER), F32),
                   _sds((nc, SSM_GROUPS, GROUP_DIM, SSM_STATE), F32)],
        grid=(nc, SSM_GROUPS),
        in_specs=[grp,
                  pl.BlockSpec((CHUNK, SSM_STATE), lambda c, g: (c, b_off + g)),
                  pl.BlockSpec((CHUNK, SSM_STATE), lambda c, g: (c, c_off + g)),
                  grp,
                  pl.BlockSpec((CHUNK, LANES), lambda c, g: (c, 0)),
                  par, par, par,
                  pl.BlockSpec((1, GROUP_DIM), lambda c, g: (0, g))],
        out_specs=[grp, grp, pl.BlockSpec((1, 1, GROUP_DIM, SSM_STATE), lambda c, g: (c, g, 0, 0))],
        scratch_shapes=[pltpu.VMEM((SSM_GROUPS, GROUP_DIM, SSM_STATE), F32),
                        pltpu.VMEM((CHUNK, LANES), F32), pltpu.VMEM((CHUNK, LANES), F32),
                        pltpu.VMEM((LANES, CHUNK), F32)],
        args=[act, act, act, z, dtp, bias_p, a_p, d_p, normw], name=name, ride=ride)


def ssd_bwd(dyn, act, z, y_pre, states, dtp, bias_p, a_p, d_p, normw, name, ride=None):
    s = act.shape[0]
    nc = s // CHUNK
    b_off = D_INNER // SSM_STATE
    c_off = b_off + SSM_GROUPS

    def body(dyn_ref, xs_ref, b_ref, c_ref, z_ref, y_ref, st_ref, dt_ref, bias_ref, a_ref, d_ref, nw_ref,
             dxs_ref, db_ref, dc_ref, dz_ref, ddt_ref, dnw_ref, dbias_ref, da_ref, dd_ref,
             dstate, dtall, csall, cst):
        c = pl.program_id(0)
        g = pl.program_id(1)

        @pl.when(g == 0)
        def _():
            _ssd_scalars(dt_ref, bias_ref, a_ref, dtall, csall, cst)
            ddt_ref[...] = jnp.zeros((CHUNK, LANES), F32)

        @pl.when(c == 0)
        def _():
            dstate[g] = jnp.zeros((GROUP_DIM, SSM_STATE), F32)

        @pl.when(jnp.logical_and(c == 0, g == 0))
        def _():
            dnw_ref[...] = jnp.zeros(dnw_ref.shape, F32)
            dbias_ref[...] = jnp.zeros((1, LANES), F32)
            da_ref[...] = jnp.zeros((1, LANES), F32)
            dd_ref[...] = jnp.zeros((1, LANES), F32)

        cs_cols = _chunk_cols(csall[...], g)
        dt_cols = _chunk_cols(dtall[...], g)
        cs_rows = [_sub_pick(cst[...], HEADS_PER_GROUP * g + r) for r in range(HEADS_PER_GROUP)]
        d_cols = _chunk_cols(d_ref[...], g)
        cs_exp = _expand_heads(cs_cols)
        dt_exp = _expand_heads(dt_cols)
        d_exp = _expand_heads(d_cols)
        xs = xs_ref[...]
        bb = b_ref[...].astype(BF16)
        cb16 = c_ref[...].astype(BF16)
        xdt = xs * dt_exp
        s_prev = st_ref[0, 0]
        s_prev16 = s_prev.astype(BF16)
        ds_next = dstate[g]
        ds16 = ds_next.astype(BF16)

        zf = z_ref[...]
        sz = _sigmoid(zf)
        silu_z = zf * sz
        y = y_ref[...]
        yg = y * silu_z
        dout = dyn_ref[...]
        dyg, dnw = _rms_bwd(dout, yg, nw_ref[...])
        dnw_ref[pl.ds(g, 1), :] += jnp.sum(dnw, axis=0, keepdims=True)
        dy = dyg * silu_z
        dz_ref[...] = dyg * y * (sz * (1.0 + zf * (1.0 - sz)))
        dd_ref[...] += jnp.sum(_heads_to_lanes(dy * xs, g), axis=0, keepdims=True)

        exp_cs = jnp.exp(cs_exp)
        decay_st = jnp.exp(cs_exp[CHUNK - 1:CHUNK, :] - cs_exp)
        cs_t = _dot(cb16, s_prev16, NT)
        dyo = dy * exp_cs
        dc_acc = _dot(dyo.astype(BF16), s_prev16, NN)
        g1 = _dot(bb, ds16, NT)
        xds = xdt * decay_st
        db_acc = _dot(xds.astype(BF16), ds16, NN)
        dxdt_off = g1 * decay_st
        t_exp = g1 * xds
        dcs_exp = dy * cs_t * exp_cs - t_exp
        decay_c = _decay_col(cs_cols)
        dstate[g] = decay_c * ds_next + _dot(dyo.astype(BF16), cb16, TN)
        dlast_col = jnp.sum(ds_next * s_prev, axis=1, keepdims=True) * decay_c
        jj = lax.broadcasted_iota(jnp.int32, (GROUP_DIM, LANES), 0)
        ll = lax.broadcasted_iota(jnp.int32, (GROUP_DIM, LANES), 1)
        sel = ll == HEADS_PER_GROUP * g + (jj >> 6)
        dlast = jnp.sum(jnp.where(sel, dlast_col, 0.0), axis=0, keepdims=True)
        t_all = _heads_to_lanes(t_exp, g)
        dlast += jnp.sum(t_all, axis=0, keepdims=True)
        dcs_all = _heads_to_lanes(dcs_exp, g)

        cbm = _dot(cb16, bb, NT)
        dcb = jnp.zeros((CHUNK, CHUNK), F32)
        dcs_rows = jnp.zeros((LANES, CHUNK), F32)
        lane_l = lax.broadcasted_iota(jnp.int32, (CHUNK, LANES), 1)
        sub_l = lax.broadcasted_iota(jnp.int32, (LANES, CHUNK), 0)
        dxdt_pairs = []
        for p in range(HEADS_PER_GROUP // 2):
            xpair16 = xdt[:, LANES * p:LANES * (p + 1)].astype(BF16)
            dypair = dy[:, LANES * p:LANES * (p + 1)]
            acc = None
            for r in (2 * p, 2 * p + 1):
                lm = _decay_mat(cs_cols[r], cs_rows[r])
                m = cbm * lm
                dyh = _head_mask(dypair, r % 2 == 1).astype(BF16)
                dm = _dot(dyh, xpair16, NT)
                dcb += dm * lm
                q = dm * m
                idx = HEADS_PER_GROUP * g + r
                dcs_all += jnp.where(lane_l == idx, jnp.sum(q, axis=1, keepdims=True), 0.0)
                dcs_rows -= jnp.where(sub_l == idx, jnp.sum(q, axis=0, keepdims=True), 0.0)
                part = _dot(m.astype(BF16), dyh, TN)
                acc = part if acc is None else acc + part
            dxdt_pairs.append(acc)
        dxdt = jnp.concatenate(dxdt_pairs, axis=1) + dxdt_off
        dcb16 = dcb.astype(BF16)
        dc_ref[...] = dc_acc + _dot(dcb16, bb, NN)
        db_ref[...] = db_acc + _dot(dcb16, cb16, TN)
        dxs_ref[...] = dxdt * dt_exp + dy * d_exp

        dcs_all += dcs_rows.T
        row = lax.broadcasted_iota(jnp.int32, (CHUNK, CHUNK), 0)
        col = lax.broadcasted_iota(jnp.int32, (CHUNK, CHUNK), 1)
        last_row = lax.broadcasted_iota(jnp.int32, (CHUNK, LANES), 0) == CHUNK - 1
        dcs_all += jnp.where(last_row, dlast, 0.0)
        da_all = _dot_01(dcs_all, (col >= row).astype(BF16), True, 3)
        dta = dtall[...]
        in_group = jnp.logical_and(lane_l >= HEADS_PER_GROUP * g, lane_l < HEADS_PER_GROUP * (g + 1))
        ddt = jnp.where(in_group, da_all * a_ref[...] + _heads_to_lanes(dxdt * xs, g), 0.0)
        da_ref[...] += jnp.sum(jnp.where(in_group, da_all * dta, 0.0), axis=0, keepdims=True)
        ddt_raw = ddt * _sigmoid(dt_ref[...] + bias_ref[...])
        ddt_ref[...] += ddt_raw
        dbias_ref[...] += jnp.sum(ddt_raw, axis=0, keepdims=True)

    rev = lambda c, g: (nc - 1 - c, g)
    grp = pl.BlockSpec((CHUNK, GROUP_DIM), rev)
    st = pl.BlockSpec((CHUNK, SSM_STATE), rev)
    par = pl.BlockSpec((1, LANES), lambda c, g: (0, 0))
    dtb = pl.BlockSpec((CHUNK, LANES), lambda c, g: (nc - 1 - c, 0))
    f = lambda shape: _sds(shape, F32)
    return _pcall(
        body,
        out_shape=[f((s, D_INNER)), f((s, SSM_GROUPS * SSM_STATE)), f((s, SSM_GROUPS * SSM_STATE)),
                   f((s, D_INNER)), f((s, LANES)), f((8, GROUP_DIM)), f((1, LANES)), f((1, LANES)), f((1, LANES))],
        grid=(nc, SSM_GROUPS),
        in_specs=[grp, grp,
                  pl.BlockSpec((CHUNK, SSM_STATE), lambda c, g: (nc - 1 - c, b_off + g)),
                  pl.BlockSpec((CHUNK, SSM_STATE), lambda c, g: (nc - 1 - c, c_off + g)),
                  grp, grp,
                  pl.BlockSpec((1, 1, GROUP_DIM, SSM_STATE), lambda c, g: (nc - 1 - c, g, 0, 0)),
                  dtb, par, par, par,
                  pl.BlockSpec((1, GROUP_DIM), lambda c, g: (0, g))],
        out_specs=[grp, st, st, grp, dtb, pl.BlockSpec((8, GROUP_DIM), lambda c, g: (0, 0)), par, par, par],
        scratch_shapes=[pltpu.VMEM((SSM_GROUPS, GROUP_DIM, SSM_STATE), F32),
                        pltpu.VMEM((CHUNK, LANES), F32), pltpu.VMEM((CHUNK, LANES), F32),
                        pltpu.VMEM((LANES, CHUNK), F32)],
        args=[dyn, act, act, act, z, y_pre, states, dtp, bias_p, a_p, d_p, normw], name=name, ride=ride)


def _attn_probs(q, kp, kc, sink, n):
    sp = _dot(q, kp, NT)
    sc = _dot(q, kc, NT)
    i = lax.broadcasted_iota(jnp.int32, sp.shape, 0) & (WINDOW - 1)
    j = lax.broadcasted_iota(jnp.int32, sp.shape, 1)
    sp = jnp.where(jnp.logical_and(j > i, n > 0), sp, NEG)
    sc = jnp.where(j <= i, sc, NEG)
    m = jnp.maximum(jnp.maximum(jnp.max(sp, axis=1, keepdims=True), jnp.max(sc, axis=1, keepdims=True)), sink)
    pp = jnp.exp(sp - m)
    pc = jnp.exp(sc - m)
    ps = jnp.exp(sink - m)
    inv = 1.0 / (jnp.sum(pp, axis=1, keepdims=True) + jnp.sum(pc, axis=1, keepdims=True) + ps)
    return pp * inv, pc * inv, ps * inv


def attn_fwd(qt, kt, vt, sink_rows, name, ride=None):
    s = qt.shape[1]
    nb = s // WINDOW
    rows = Q_PER_KV * WINDOW

    def body(q_ref, kp_ref, kc_ref, vp_ref, vc_ref, sk_ref, o_ref):
        n = pl.program_id(1)
        q = q_ref[...].reshape(rows, ATT_HEAD_DIM)
        pp, pc, _ = _attn_probs(q, kp_ref[0], kc_ref[0], sk_ref[0], n)
        o = _dot(pp.astype(BF16), vp_ref[0]) + _dot(pc.astype(BF16), vc_ref[0])
        o_ref[...] = o.reshape(Q_PER_KV, WINDOW, ATT_HEAD_DIM).astype(BF16)

    qsp = pl.BlockSpec((Q_PER_KV, WINDOW, ATT_HEAD_DIM), lambda h, n: (h, n, 0))
    prev = pl.BlockSpec((1, WINDOW, ATT_HEAD_DIM), lambda h, n: (h, jnp.maximum(n - 1, 0), 0))
    cur = pl.BlockSpec((1, WINDOW, ATT_HEAD_DIM), lambda h, n: (h, n, 0))
    return _pcall(body, out_shape=[_sds(qt.shape, BF16)], grid=(N_KV_HEADS, nb),
                  in_specs=[qsp, prev, cur, prev, cur, pl.BlockSpec((1, rows, 1), lambda h, n: (h, 0, 0))],
                  out_specs=[qsp], args=[qt, kt, kt, vt, vt, sink_rows], name=name, ride=ride)


def attn_bwd(qt, kt, vt, sink_rows, dot_, name, ride=None):
    s = qt.shape[1]
    nb = s // WINDOW
    rows = Q_PER_KV * WINDOW

    def body(q_ref, kp_ref, kc_ref, vp_ref, vc_ref, sk_ref, do_ref, dq_ref, dk_ref, dv_ref, ds_ref, kacc, vacc):
        n = pl.program_id(1)

        @pl.when(n < nb)
        def _():
            q = q_ref[...].reshape(rows, ATT_HEAD_DIM)
            do = do_ref[...].reshape(rows, ATT_HEAD_DIM)
            kp, kc, vp, vc = kp_ref[0], kc_ref[0], vp_ref[0], vc_ref[0]
            pp, pc, ps = _attn_probs(q, kp, kc, sk_ref[0], n)
            dpp = _dot(do, vp, NT)
            dpc = _dot(do, vc, NT)
            delta = jnp.sum(pp * dpp, axis=1, keepdims=True) + jnp.sum(pc * dpc, axis=1, keepdims=True)
            dsp = (pp * (dpp - delta)).astype(BF16)
            dsc = (pc * (dpc - delta)).astype(BF16)
            dq = _dot(dsp, kp) + _dot(dsc, kc)
            dq_ref[...] = dq.reshape(Q_PER_KV, WINDOW, ATT_HEAD_DIM)
            dk_prev = _dot(dsp, q, TN)
            dv_prev = _dot(pp.astype(BF16), do, TN)

            @pl.when(n == 0)
            def _():
                dk_ref[0] = dk_prev
                dv_ref[0] = dv_prev

            @pl.when(n > 0)
            def _():
                dk_ref[0] = kacc[...] + dk_prev
                dv_ref[0] = vacc[...] + dv_prev

            kacc[...] = _dot(dsc, q, TN)
            vacc[...] = _dot(pc.astype(BF16), do, TN)
            dsk = -ps * delta
            sub = lax.broadcasted_iota(jnp.int32, (8, LANES), 0)
            tile = jnp.zeros((8, LANES), F32)
            for h in range(Q_PER_KV):
                tile += jnp.where(sub == h, jnp.sum(dsk[h * WINDOW:(h + 1) * WINDOW, :], axis=0, keepdims=True), 0.0)
            ds_ref[0, 0] = tile

        @pl.when(n == nb)
        def _():
            dk_ref[0] = kacc[...]
            dv_ref[0] = vacc[...]
            ds_ref[0, 0] = jnp.zeros((8, LANES), F32)

    last = nb - 1
    qsp = pl.BlockSpec((Q_PER_KV, WINDOW, ATT_HEAD_DIM), lambda h, n: (h, jnp.minimum(n, last), 0))
    prev = pl.BlockSpec((1, WINDOW, ATT_HEAD_DIM), lambda h, n: (h, jnp.clip(n - 1, 0, last), 0))
    cur = pl.BlockSpec((1, WINDOW, ATT_HEAD_DIM), lambda h, n: (h, jnp.minimum(n, last), 0))
    dkv = pl.BlockSpec((1, WINDOW, ATT_HEAD_DIM), lambda h, n: (h, jnp.maximum(n - 1, 0), 0))
    f = lambda shape: _sds(shape, F32)
    return _pcall(
        body, out_shape=[f(qt.shape), f(kt.shape), f(vt.shape), f((N_KV_HEADS, nb + 1, 8, LANES))],
        grid=(N_KV_HEADS, nb + 1),
        in_specs=[qsp, prev, cur, prev, cur, pl.BlockSpec((1, rows, 1), lambda h, n: (h, 0, 0)), qsp],
        out_specs=[qsp, dkv, dkv, pl.BlockSpec((1, 1, 8, LANES), lambda h, n: (h, n, 0, 0))],
        scratch_shapes=[pltpu.VMEM((WINDOW, ATT_HEAD_DIM), F32), pltpu.VMEM((WINDOW, ATT_HEAD_DIM), F32)],
        args=[qt, kt, kt, vt, vt, sink_rows, dot_], name=name, ride=ride)


def loss_head(x, w, tgt, name):
    s, d = x.shape
    tm = _row_tile(s, 256)

    def body(x_ref, w_ref, t_ref, loss_ref, dx_ref, dw_ref):
        i = pl.program_id(0)
        xf = x_ref[...]
        wv = w_ref[...]
        r = lax.rsqrt(jnp.mean(xf * xf, axis=-1, keepdims=True) + EPS)
        xhat = xf * r
        e = xhat * wv - t_ref[...]
        part = 0.5 * jnp.sum(jnp.mean(e * e, axis=-1, keepdims=True), axis=0, keepdims=True)
        dy = e * (1.0 / d)
        dxhat = dy * wv
        dx_ref[...] = r * (dxhat - xhat * jnp.mean(dxhat * xhat, axis=-1, keepdims=True))
        col = jnp.sum(dy * xhat, axis=0, keepdims=True)

        @pl.when(i == 0)
        def _():
            loss_ref[...] = jnp.broadcast_to(part, (1, LANES))
            dw_ref[...] = col

        @pl.when(i > 0)
        def _():
            loss_ref[...] += jnp.broadcast_to(part, (1, LANES))
            dw_ref[...] += col

    row = pl.BlockSpec((tm, d), lambda i: (i, 0))
    vec = pl.BlockSpec((1, d), lambda i: (0, 0))
    return _pcall(body, out_shape=[_sds((1, LANES), F32), _sds((s, d), F32), _sds((1, d), F32)], grid=(s // tm,),
                  in_specs=[row, vec, row], out_specs=[pl.BlockSpec((1, LANES), lambda i: (0, 0)), row, vec],
                  args=[x, w.reshape(1, d), tgt], name=name)


def _tile_rows(r, c, max_elems=262144, mult=16):
    best = None
    for t in range(mult, r + 1, mult):
        if r % t == 0 and t * c <= max_elems:
            best = t
    return best or r


def add_pair(xhs, ps, c_idx, name):
    n = len(xhs)
    _, r, c = xhs[0].shape
    tr = _tile_rows(r, c)

    def body(c_ref, *refs):
        for x_ref, p_ref, o_ref in zip(refs[:n], refs[n:2 * n], refs[2 * n:]):
            o_ref[...] = (x_ref[0].astype(F32) + p_ref[...].astype(F32)).astype(BF16)

    blk = pl.BlockSpec((tr, c), lambda i, cr: (i, 0))
    return pl.pallas_call(
        body, out_shape=tuple([_sds((r, c), BF16)] * n),
        grid_spec=pltpu.PrefetchScalarGridSpec(
            num_scalar_prefetch=1, grid=(r // tr,),
            in_specs=[pl.BlockSpec((1, tr, c), lambda i, cr: (cr[0], i, 0))] * n + [blk] * n,
            out_specs=tuple([blk] * n)),
        name=name, compiler_params=_cp(1))(c_idx, *xhs, *ps)


def sum_chips(qs, owns, chip_idx, name):
    n = len(qs)
    _, r, c = qs[0].shape
    tr = _tile_rows(r, c)

    def body(k_ref, *refs):
        k = k_ref[0]
        for q_ref, own_ref, o_ref in zip(refs[:n], refs[n:2 * n], refs[2 * n:]):
            mine = own_ref[0].astype(F32)
            tot = None
            for j in range(N_CHIPS):
                term = jnp.where(k == j, mine, q_ref[j].astype(F32))
                tot = term if tot is None else tot + term
            o_ref[...] = tot

    return pl.pallas_call(
        body, out_shape=tuple([_sds((r, c), F32)] * n),
        grid_spec=pltpu.PrefetchScalarGridSpec(
            num_scalar_prefetch=1, grid=(r // tr,),
            in_specs=([pl.BlockSpec((N_CHIPS, tr, c), lambda i, kr: (0, i, 0))] * n
                      + [pl.BlockSpec((1, tr, c), lambda i, kr: (kr[0], i, 0))] * n),
            out_specs=tuple([pl.BlockSpec((tr, c), lambda i, kr: (i, 0))] * n)),
        name=name, compiler_params=_cp(1))(chip_idx, *qs, *owns)


def adamw(w, g, m, v, name):
    r, c = w.shape
    tr = _tile_rows(r, c, max_elems=131072, mult=8)
    c1 = 1.0 / (1.0 - ADAM_B1 ** ADAM_STEP)
    c2 = 1.0 / (1.0 - ADAM_B2 ** ADAM_STEP)

    def body(w_ref, g_ref, m_ref, v_ref, d_ref, mo_ref, vo_ref):
        gf = g_ref[...]
        mn = ADAM_B1 * m_ref[...] + (1.0 - ADAM_B1) * gf
        vn = ADAM_B2 * v_ref[...] + (1.0 - ADAM_B2) * (gf * gf)
        mo_ref[...] = mn
        vo_ref[...] = vn
        d_ref[...] = -ADAM_LR * ((mn * c1) / (jnp.sqrt(vn * c2) + ADAM_EPS) + ADAM_WD * w_ref[...])

    blk = pl.BlockSpec((tr, c), lambda i: (i, 0))
    out = _sds((r, c), F32)
    return _pcall(body, out_shape=[out, out, out], grid=(r // tr,), in_specs=[blk] * 4, out_specs=[blk] * 3,
                  args=[w, g, m, v], name=name)


WEIGHTS = ['norm_w', 'ffn_w_gate', 'ffn_w_up', 'ffn_w_down', 'ssm_w_in', 'ssm_conv_w', 'ssm_conv_b', 'ssm_dt_bias',
           'ssm_a_log', 'ssm_d', 'ssm_norm_w', 'ssm_w_out', 'kv_norm_w', 'w_k', 'b_k', 'w_v', 'b_v', 'attn_w_q',
           'attn_b_q', 'attn_sinks', 'attn_w_o', 'attn_b_o', 'final_norm_w']
BIG = ['ffn_w_gate', 'ffn_w_up', 'ffn_w_down', 'ssm_w_in', 'ssm_w_out', 'w_k', 'w_v', 'attn_w_q', 'attn_w_o']
TRANSPOSED = ('ffn_w_gate', 'ffn_w_up', 'ssm_w_in')
SMALL = [n for n in WEIGHTS if n not in BIG]
SMALL_SHARDED = {'norm_w': 2, 'ssm_conv_w': 2, 'ssm_conv_b': 1, 'ssm_norm_w': 1}
ROW_ALIGN = 8 * LANES


def _pack_rows(parts):
    flat = jnp.concatenate([p.reshape(-1).astype(F32) for p in parts])
    pad = (-flat.size) % ROW_ALIGN
    return jnp.pad(flat, (0, pad)).reshape(-1, LANES)


def _unpack_rows(buf, shapes):
    flat = buf.reshape(-1)
    out, pos = [], 0
    for shp in shapes:
        size = math.prod(shp)
        out.append(flat[pos:pos + size].reshape(shp))
        pos += size
    return out


def _as2d(a):
    return a.reshape(-1, a.shape[-1])


def _heads_major(t, n_heads):
    s = t.shape[0]
    return t.reshape(s, n_heads, ATT_HEAD_DIM).transpose(1, 0, 2)


def _tokens_major(t):
    h, s, dh = t.shape
    return t.transpose(1, 0, 2).reshape(s, h * dh)


def _pad_lanes(v):
    return jnp.pad(v.reshape(1, -1), ((0, 0), (0, LANES - v.size)))


def _chips_first(t):
    return t.swapaxes(0, 1).reshape((-1,) + t.shape[3:])


def _parts_first(t, rows):
    return t.reshape((N_CHIPS, N_CORES, rows) + t.shape[1:]).swapaxes(0, 1)


def kernel(*args):
    names = (['x'] + WEIGHTS + ['loss_target'] + ['m_' + n for n in WEIGHTS] + ['v_' + n for n in WEIGHTS])
    a = dict(zip(names, args))
    for n in TRANSPOSED:
        for pre in ('', 'm_', 'v_'):
            a[pre + n] = a[pre + n].swapaxes(-1, -2)
    xi, yi, ci = lax.axis_index("x"), lax.axis_index("y"), lax.axis_index("c")
    chip = 2 * xi + yi
    south = ci == 0
    c_idx = jnp.reshape(ci, (1,)).astype(jnp.int32)
    chip_idx = jnp.reshape(chip, (1,)).astype(jnp.int32)
    x0 = a['x'][0]
    s = x0.shape[0]
    cos, sin = rope_tables(s)

    def own_slot(full, mine):
        return lax.dynamic_update_slice_in_dim(full, mine[:, None], chip, axis=1)

    def ffn_shard(l, i):
        return [a[n][l, i].astype(BF16).reshape(N_CORES, FF_PART, D_MODEL)
                for n in ('ffn_w_gate', 'ffn_w_up', 'ffn_w_down')]

    def own_slots(fulls, mines):
        return [own_slot(f, m) for f, m in zip(fulls, mines)]
    w_in_sh = jnp.pad(a['ssm_w_in'][0], ((0, IN_SHARD_PAD - IN_SHARD), (0, 0))).astype(BF16).reshape(
        N_CORES, IN_SHARD_PAD // 2, D_MODEL)
    w_out_sh = a['ssm_w_out'][0].astype(BF16).reshape(N_CORES, 256, D_MODEL)
    attn_sh = jnp.stack([a['attn_w_q'][0], a['attn_w_o'][0]]).astype(BF16)
    kv_sh = jnp.stack([a['w_k'], a['w_v']]).astype(BF16)
    small_names = list(SMALL_SHARDED)
    small_sh = _pack_rows([a[n] for n in small_names])
    small_sh = small_sh.reshape(N_CORES, small_sh.shape[0] // 2, LANES)

    sh00, sh01, sh10, sh11 = ffn_shard(0, 0), ffn_shard(0, 1), ffn_shard(1, 0), ffn_shard(1, 1)
    in_flight = gather_start([sh00 + [small_sh], [w_in_sh, kv_sh], [w_out_sh], sh01, sh10, [attn_sh], sh11],
                             "gather_start")

    def arrive(idx, after, tag):
        return forward_cores(gather_arrive(in_flight[idx], after, "gather_arrive_" + tag))

    first = run_exchange(arrive(0, x0, "first"), "gather_hop_first")
    w00 = own_slots(first[:3], sh00)
    smalls = own_slot(first[3], small_sh)
    p = {}
    per_chip = [_unpack_rows(smalls[:, k], [a[n].shape for n in small_names]) for k in range(N_CHIPS)]
    for idx, n in enumerate(small_names):
        p[n] = jnp.concatenate([per_chip[k][idx] for k in range(N_CHIPS)], axis=SMALL_SHARDED[n])
    nw = p['norm_w']
    conv_w, conv_b, ssm_nw = p['ssm_conv_w'][0], p['ssm_conv_b'][0], p['ssm_norm_w'][0].reshape(1, D_INNER)

    h00 = rmsnorm_fwd(x0, nw[0, 0], "norm_in")
    x1, h01, gu00 = ffn_fwd(h00, x0, *w00, [nw[0, 1]], "ffn_fwd_00")
    w_in_g, kv_g = run_exchange(arrive(1, x1, "in"), "gather_hop_in")
    w_in_t = _chips_first(own_slot(w_in_g, w_in_sh)).reshape(N_CHIPS, IN_SHARD_PAD, D_MODEL)[:, :IN_SHARD].reshape(
        IN_PROJ_DIM, D_MODEL)
    w_dt_t = jnp.pad(w_in_t[D_INNER + CONV_DIM:], ((0, LANES - SSM_HEADS), (0, 0)))
    kv_g = own_slot(kv_g, kv_sh)
    w_k, w_v = kv_g[0].reshape(D_MODEL, KV_DIM), kv_g[1].reshape(D_MODEL, KV_DIM)

    zz = mm_nt(h01, w_in_t, "ssm_in_z", n=D_INNER)
    xbc = mm_nt(h01, w_in_t, "ssm_in_xbc", n=CONV_DIM, row0=D_INNER)
    dtp = mm_nt(h01, w_dt_t, "ssm_in_dt")
    act = conv_fwd(xbc, conv_w, conv_b, "ssm_conv")
    bias_p = _pad_lanes(a['ssm_dt_bias'][0])
    a_p = _pad_lanes(-jnp.exp(a['ssm_a_log'][0]))
    d_p = _pad_lanes(a['ssm_d'][0])
    (yn, y_pre, states), (w_out_g,) = ssd_fwd(act, zz, dtp, bias_p, a_p, d_p, ssm_nw, "ssd_fwd",
                                              ride=arrive(2, act, "out"))
    w_out = _chips_first(own_slot(w_out_g, w_out_sh))
    (x2, h02), w01 = mm_res(yn, w_out, x1, "ssm_out", norm_ws=[nw[0, 2]], ride=arrive(3, yn, "01"))
    w01 = own_slots(w01, sh01)
    x3, hkv, h10, gu01 = ffn_fwd(h02, x2, *w01, [a['kv_norm_w'], nw[1, 0]], "ffn_fwd_01")
    w10 = own_slots(run_exchange(arrive(4, x3, "10"), "gather_hop_10"), sh10)

    k_rot = rope_apply(mm_nn(hkv, w_k, "kv_k", bias=a['b_k']), cos, sin, "rope_k")
    v = mm_nn(hkv, w_v, "kv_v", bias=a['b_v'], out_dtype=BF16)
    kt = _heads_major(k_rot, N_KV_HEADS)
    vt = _heads_major(v, N_KV_HEADS)

    (x4, h11, gu10), (attn_g,) = ffn_fwd(h10, x3, *w10, [nw[1, 1]], "ffn_fwd_10", ride=arrive(5, v, "attn"))
    attn_g = own_slot(attn_g, attn_sh)
    w_q, w_o = attn_g[0].reshape(D_MODEL, D_MODEL), attn_g[1].reshape(D_MODEL, D_MODEL)
    scale = 1.0 / math.sqrt(ATT_HEAD_DIM)
    q_rot = rope_apply(mm_nn(h11, w_q, "attn_q", bias=a['attn_b_q'][0]), cos, sin, "rope_q", scale=scale)
    qt = _heads_major(q_rot, N_Q_HEADS)
    sink_rows = jnp.repeat(a['attn_sinks'][0].reshape(N_KV_HEADS, Q_PER_KV), WINDOW, axis=1).reshape(
        N_KV_HEADS, Q_PER_KV * WINDOW, 1)
    (ot,) = attn_fwd(qt, kt, vt, sink_rows, "attn_fwd")
    o = _tokens_major(ot)
    (x5, h12), w11 = mm_res(o, w_o, x4, "attn_out", bias=a['attn_b_o'][0], norm_ws=[nw[1, 2]],
                            ride=arrive(6, ot, "11"))
    w11 = own_slots(w11, sh11)
    x6, gu11 = ffn_fwd(h12, x5, *w11, [], "ffn_fwd_11")

    loss_v, dx6, d_final = loss_head(x6, a['final_norm_w'], a['loss_target'][0], "loss_head")
    loss = lax.psum(loss_v[0, 0], ("x", "y", "c"))
    g = {'final_norm_w': d_final[0]}

    def same_shape(xs, ys):
        runs = []
        for xv, yv in zip(xs, ys):
            if runs and runs[-1][0][0].shape == xv.shape:
                runs[-1][0].append(xv)
                runs[-1][1].append(yv)
            else:
                runs.append(([xv], [yv]))
        return runs

    def pre_reduce(grads, sib, tag):
        out = []
        for idx, (grp, sbs) in enumerate(same_shape(grads, list(sib))):
            ts = add_pair([gr.reshape(2, -1, gr.shape[-1]) for gr in grp], [_as2d(sb) for sb in sbs], c_idx,
                          "rs_add_%s_%d" % (tag, idx))
            out += [t.reshape(gr.shape[1:]) for t, gr in zip(ts, grp)]
        return out

    def chip_sum(landed, parts, tag):
        out = []
        for idx, (qs, owns) in enumerate(same_shape(list(landed), parts)):
            ts = sum_chips([q.reshape(N_CHIPS, -1, q.shape[-1]) for q in qs],
                           [own.reshape(N_CHIPS, -1, own.shape[-1]) for own in owns], chip_idx,
                           "rs_sum_%s_%d" % (tag, idx))
            out += [t.reshape(q.shape[1:]) for t, q in zip(ts, qs)]
        return out

    dnw = [[None] * 3 for _ in range(2)]
    sums = {}

    def trade(key):
        return swap_cores(sums[key], False)

    dx5, dnw12, *g11 = ffn_bwd(dx6, h12, x5, nw[1, 2], gu11, *w11, "ffn_bwd_11")
    dnw[1][2] = dnw12[0]
    (d_wo, g['attn_b_o']), sib11 = mm_tn(o, dx5, "attn_dwo", col_sum=True, ride=swap_cores(g11, True))
    t11 = pre_reduce(g11, sib11, "11")
    do = mm_nt(dx5, w_o, "attn_do", out_dtype=BF16)
    (dqt, dkt, dvt, dsink), land11 = attn_bwd(qt, kt, vt, sink_rows, _heads_major(do, N_Q_HEADS), "attn_bwd",
                                             ride=scatter_chips(t11))
    sums['11'] = chip_sum(land11, t11, "11")
    g['attn_sinks'] = jnp.sum(dsink[:, :, :Q_PER_KV, 0], axis=1).reshape(N_Q_HEADS)
    dq_pre = rope_apply(_tokens_major(dqt), cos, sin, "rope_dq", inverse=True, scale=scale, out_dtype=F32)
    d_wq, g['attn_b_q'] = mm_tn(h11, dq_pre, "attn_dwq", col_sum=True)
    g_attn = [jnp.stack([d_wq.reshape(N_CHIPS, 256, D_MODEL), d_wo.reshape(N_CHIPS, 256, D_MODEL)])]
    (dx4, dnw[1][1]), sib_attn = mm_rms_bwd([(dq_pre, 0, w_q, 0, D_MODEL, "nt")], dx5, x4, nw[1, 1], "attn_bwd_dh",
                                            ride=swap_cores(g_attn, True))
    t_attn = pre_reduce(g_attn, sib_attn, "attn")
    (dx3, dnw10, *g10), landed = ffn_bwd(dx4, h10, x3, nw[1, 0], gu10, *w10, "ffn_bwd_10",
                                         ride=join(scatter_chips(t_attn), trade('11')))
    dnw[1][0] = dnw10[0]
    sums['attn'] = chip_sum(landed[:1], t_attn, "attn")
    theirs = {'11': landed[1:]}
    dk_pre = rope_apply(_tokens_major(dkt), cos, sin, "rope_dk", inverse=True, out_dtype=F32)
    dv = _tokens_major(dvt)
    (d_wk, g['b_k']), sib10 = mm_tn(hkv, dk_pre, "kv_dwk", col_sum=True, ride=swap_cores(g10, True))
    t10 = pre_reduce(g10, sib10, "10")
    d_wv, g['b_v'] = mm_tn(hkv, dv, "kv_dwv", col_sum=True)
    g_kv = [jnp.stack([d_wk.reshape(N_CHIPS, 256, KV_DIM), d_wv.reshape(N_CHIPS, 256, KV_DIM)])]
    (dx3, g['kv_norm_w']), sib_kv = mm_rms_bwd(
        [(dk_pre, 0, w_k, 0, KV_DIM, "nt"), (dv, 0, w_v, 0, KV_DIM, "nt")], dx3, x3, a['kv_norm_w'], "kv_bwd_dh",
        ride=swap_cores(g_kv, True))
    t_kv = pre_reduce(g_kv, sib_kv, "kv")
    (dx2, dnw02, *g01), landed = ffn_bwd(dx3, h02, x2, nw[0, 2], gu01, *w01, "ffn_bwd_01",
                                         ride=join(scatter_chips(t10 + t_kv), trade('attn')))
    dnw[0][2] = dnw02[0]
    sums['10'] = chip_sum(landed[:3], t10, "10")
    sums['kv'] = chip_sum(landed[3:4], t_kv, "kv")
    theirs['attn'] = landed[4:]
    d_wout, sib01 = mm_tn(yn, dx2, "ssm_dwout", ride=swap_cores(g01, True))
    t01 = pre_reduce(g01, sib01, "01")
    dyn = mm_nt(dx2, w_out, "ssm_dyn")
    (dxs, db_, dc_, dz, ddt, d_ssm_nw, d_bias, d_a, d_d), landed = ssd_bwd(
        dyn, act, zz, y_pre, states, dtp, bias_p, a_p, d_p, ssm_nw, "ssd_bwd",
        ride=join(scatter_chips(t01), trade('10'), trade('kv')))
    sums['01'] = chip_sum(landed[:3], t01, "01")
    theirs['10'], theirs['kv'] = landed[3:6], landed[6:]
    g['ssm_norm_w'] = d_ssm_nw[:SSM_GROUPS].reshape(D_INNER)
    g['ssm_dt_bias'] = d_bias[0, :SSM_HEADS]
    g['ssm_a_log'] = d_a[0, :SSM_HEADS] * a_p[0, :SSM_HEADS]
    g['ssm_d'] = d_d[0, :SSM_HEADS]
    dxbc, g['ssm_conv_w'], g['ssm_conv_b'] = conv_bwd(dxs, db_, dc_, xbc, conv_w, conv_b, "ssm_conv_bwd")
    d_win = mm_tn(dz, h01, "ssm_dwz", rows=IN_PROJ_DIM)
    d_win = mm_tn(dxbc, h01, "ssm_dwxbc", into=d_win, rows=IN_PROJ_DIM, row0=D_INNER)
    d_win = mm_tn(ddt, h01, "ssm_dwdt", into=d_win, rows=IN_PROJ_DIM, row0=D_INNER + CONV_DIM, m_valid=SSM_HEADS)
    d_win = jnp.pad(d_win.reshape(N_CHIPS, IN_SHARD, D_MODEL), ((0, 0), (0, IN_SHARD_PAD - IN_SHARD), (0, 0)))
    g_ssm = [_parts_first(d_win.reshape(-1, D_MODEL), IN_SHARD_PAD // 2), _parts_first(d_wout, 256)]
    kb = 1024
    terms = ([(dz, j, w_in_t, j, kb, "nn") for j in range(D_INNER // kb)]
             + [(dxbc, j, w_in_t, D_INNER // kb + j, kb, "nn") for j in range(CONV_DIM // kb)]
             + [(ddt, 0, w_dt_t, 0, LANES, "nn")])
    (dx1, dnw[0][1]), sib_ssm = mm_rms_bwd(terms, dx2, x1, nw[0, 1], "ssm_bwd_dh", ride=swap_cores(g_ssm, True))
    t_ssm = pre_reduce(g_ssm, sib_ssm, "ssm")
    (grad_x, dnw00, *g00), landed = ffn_bwd(dx1, h00, x0, nw[0, 0], gu00, *w00, "ffn_bwd_00",
                                            ride=join(scatter_chips(t_ssm), trade('01')))
    dnw[0][0] = dnw00[0]
    sums['ssm'] = chip_sum(landed[:2], t_ssm, "ssm")
    theirs['01'] = landed[2:]
    landed = run_exchange(join(swap_cores(g00, True), trade('ssm')), "rs_swap_00")
    t00 = pre_reduce(g00, landed[:3], "00")
    theirs['ssm'] = landed[3:]

    def both(key):
        return [(jnp.where(south, m_, t_), jnp.where(south, t_, m_)) for m_, t_ in zip(sums[key], theirs[key])]

    land00 = run_exchange(scatter_chips(t00), "rs_scatter_00")
    sums['00'] = chip_sum(land00, t00, "00")
    theirs['00'] = run_exchange(trade('00'), "rs_trade_00")

    delta, new_m, new_v, gw = {}, {}, {}, {}
    blocks = [both(key) for key in ('00', '01', '10', '11')]
    for t, n in enumerate(('ffn_w_gate', 'ffn_w_up', 'ffn_w_down')):
        gw[n] = jnp.concatenate([piece for blk in blocks for piece in blk[t]], axis=0).reshape(a[n].shape)
    full = {key: both(key) for key in ('attn', 'kv', 'ssm')}
    lo, hi = full['attn'][0]
    gw['attn_w_q'], gw['attn_w_o'] = lo[None], hi[None]
    lo, hi = full['kv'][0]
    gw['w_k'], gw['w_v'] = lo, hi
    lo, hi = full['ssm'][0]
    gw['ssm_w_in'] = jnp.concatenate([lo, hi], axis=0)[:IN_SHARD][None]
    lo, hi = full['ssm'][1]
    gw['ssm_w_out'] = jnp.concatenate([lo, hi], axis=0)[None]

    g['norm_w'] = jnp.stack([jnp.stack(r) for r in dnw])
    red = all_reduce_small(_pack_rows([g[n] for n in SMALL]), "reduce_vectors")
    for n, t in zip(SMALL, _unpack_rows(red, [g[n].shape for n in SMALL])):
        if n in SMALL_SHARDED:
            ax = SMALL_SHARDED[n] - (a[n].ndim - t.ndim)
            width = a[n].shape[SMALL_SHARDED[n]]
            t = lax.dynamic_slice_in_dim(t, chip * width, width, axis=ax)
        gw[n] = t.reshape(a[n].shape)

    for n in BIG:
        d, mo, vo = adamw(_as2d(a[n]), _as2d(gw[n]), _as2d(a['m_' + n]), _as2d(a['v_' + n]), "adamw_" + n)
        delta[n], new_m[n], new_v[n] = d.reshape(a[n].shape), mo.reshape(a[n].shape), vo.reshape(a[n].shape)
    shapes = [a[n].shape for n in SMALL]
    packed = [_pack_rows([src[n] for n in SMALL]) for src in
              (a, gw, {n: a['m_' + n] for n in SMALL}, {n: a['v_' + n] for n in SMALL})]
    outs = adamw(*packed, "adamw_vectors")
    for dst, buf in zip((delta, new_m, new_v), outs):
        for n, t in zip(SMALL, _unpack_rows(buf, shapes)):
            dst[n] = t
    for n in TRANSPOSED:
        for dst in (gw, delta, new_m, new_v):
            dst[n] = dst[n].swapaxes(-1, -2)

    return (loss, grad_x[None], *[gw[n] for n in WEIGHTS], *[delta[n] for n in WEIGHTS],
            *[new_m[n] for n in WEIGHTS], *[new_v[n] for n in WEIGHTS])
```

```python
import math

import jax
import jax.numpy as jnp
from jax import lax
from jax.experimental import pallas as pl
from jax.experimental.pallas import tpu as pltpu

F32 = jnp.float32
BF16 = jnp.bfloat16

D_MODEL = 1024
D_INNER = 2048
SSM_HEADS = 32
SSM_GROUPS = 4
HEADS_PER_GROUP = SSM_HEADS // SSM_GROUPS
SSM_HEAD_DIM = 64
SSM_STATE = 128
GROUP_DIM = D_INNER // SSM_GROUPS
CONV_DIM = D_INNER + 2 * SSM_GROUPS * SSM_STATE
CONV_WIDTH = 4
CHUNK = 128
ATT_HEAD_DIM = 64
N_Q_HEADS = 16
N_KV_HEADS = 4
Q_PER_KV = N_Q_HEADS // N_KV_HEADS
KV_DIM = N_KV_HEADS * ATT_HEAD_DIM
WINDOW = 128
ROPE_THETA = 10000.0
D_FF = 2816
N_CHIPS = 4
N_CORES = 2
FF_SHARD = D_FF // N_CHIPS
FF_PART = FF_SHARD // N_CORES
IN_PROJ_DIM = D_INNER + CONV_DIM + SSM_HEADS
IN_SHARD = IN_PROJ_DIM // N_CHIPS
IN_SHARD_PAD = 1312
EPS = 1e-5
NEG = -1e30
LANES = 128
VMEM_LIMIT = 56 * 1024 * 1024

ADAM_LR = 0.001
ADAM_B1 = 0.9
ADAM_B2 = 0.999
ADAM_EPS = 1e-08
ADAM_WD = 0.01
ADAM_STEP = 10

NN = ((1,), (0,))
NT = ((1,), (1,))
TN = ((0,), (0,))
MESH = pl.DeviceIdType.MESH
ANY = pl.BlockSpec(memory_space=pl.ANY)


def _dot(a, b, dims=NN, precision=None):
    return lax.dot_general(a, b, (dims, ((), ())), preferred_element_type=F32, precision=precision)


def _cp(n_grid):
    return pltpu.CompilerParams(dimension_semantics=("arbitrary",) * n_grid, vmem_limit_bytes=VMEM_LIMIT)


def _sigmoid(x):
    return 1.0 / (1.0 + jnp.exp(-x))


def _rms_fwd(xf, w):
    r = lax.rsqrt(jnp.mean(xf * xf, axis=-1, keepdims=True) + EPS)
    return xf * r * w


def _rms_bwd(dh, xf, w):
    r = lax.rsqrt(jnp.mean(xf * xf, axis=-1, keepdims=True) + EPS)
    xhat = xf * r
    dxhat = dh * w
    dx = r * (dxhat - xhat * jnp.mean(dxhat * xhat, axis=-1, keepdims=True))
    return dx, dh * xhat


def _row_tile(s, pref):
    return pref if s % pref == 0 else s


def _col_tile(n):
    for t in (1024, 768, 512, 256, 128):
        if n % t == 0:
            return t
    return n


def _sds(shape, dtype):
    return jax.ShapeDtypeStruct(tuple(shape), dtype)


class Exchange:
    def __init__(self, ins, out_shapes, sems, start, finish, inplace=False):
        self.ins, self.out_shapes, self.sems, self.start, self.finish = ins, out_shapes, sems, start, finish
        self.inplace = inplace


def _place():
    x, y, c = lax.axis_index("x"), lax.axis_index("y"), lax.axis_index("c")
    others = [(1 - x, y), (x, 1 - y), (1 - x, 1 - y)]
    return x, y, c, 2 * x + y, others


def _rc(src, dst, send_sem, recv_sem, dev):
    return pltpu.make_async_remote_copy(src_ref=src, dst_ref=dst, send_sem=send_sem, recv_sem=recv_sem,
                                        device_id=dev, device_id_type=MESH)


def gather_chips(arrs):
    n = len(arrs)

    def copies(ins, outs, sems):
        send, recv = sems
        x, y, c, k, others = _place()
        ici, land, fwd, fland = [], [], [], []
        for a in range(n):
            for j, (px, py) in enumerate(others):
                ici.append(_rc(ins[a].at[c], outs[a].at[c, k], send.at[a, j], recv.at[a, j], (px, py, c)))
                blk = outs[a].at[c, 2 * px + py]
                land.append(_rc(blk, blk, send.at[a, j], recv.at[a, j], (px, py, c)))
                fwd.append(_rc(blk, blk, send.at[a, 3 + j], recv.at[a, 3 + j], (x, y, 1 - c)))
                blk2 = outs[a].at[1 - c, 2 * px + py]
                fland.append(_rc(blk2, blk2, send.at[a, 3 + j], recv.at[a, 3 + j], (x, y, 1 - c)))
        return ici, land, fwd, fland

    def start(ins, outs, sems):
        for cp in copies(ins, outs, sems)[0]:
            cp.start()

    def finish(ins, outs, sems):
        ici, land, fwd, fland = copies(ins, outs, sems)
        for arrived, onward in zip(land, fwd):
            arrived.wait_recv()
            onward.start()
        for arrived in fland:
            arrived.wait_recv()
        for cp in ici + fwd:
            cp.wait_send()

    return Exchange(list(arrs), [_sds((2, N_CHIPS) + a.shape[1:], a.dtype) for a in arrs],
                    [pltpu.SemaphoreType.DMA((n, 6)), pltpu.SemaphoreType.DMA((n, 6))], start, finish)


def scatter_chips(arrs):
    n = len(arrs)

    def copies(ins, outs, sems):
        send, recv = sems
        x, y, c, k, others = _place()
        out, land = [], []
        for a in range(n):
            for j, (px, py) in enumerate(others):
                out.append(_rc(ins[a].at[2 * px + py], outs[a].at[k], send.at[a, j], recv.at[a, j], (px, py, c)))
                blk = outs[a].at[2 * px + py]
                land.append(_rc(blk, blk, send.at[a, j], recv.at[a, j], (px, py, c)))
        return out, land

    def start(ins, outs, sems):
        for cp in copies(ins, outs, sems)[0]:
            cp.start()

    def finish(ins, outs, sems):
        out, land = copies(ins, outs, sems)
        for arrived in land:
            arrived.wait_recv()
        for cp in out:
            cp.wait_send()

    return Exchange(list(arrs), [_sds(a.shape, a.dtype) for a in arrs],
                    [pltpu.SemaphoreType.DMA((n, 3)), pltpu.SemaphoreType.DMA((n, 3))], start, finish)


def swap_cores(arrs, pick_other):
    n = len(arrs)

    def copies(ins, outs, sems):
        send, recv = sems
        x, y, c, _, _ = _place()
        return [_rc(ins[a].at[1 - c] if pick_other else ins[a], outs[a], send.at[a], recv.at[a], (x, y, 1 - c))
                for a in range(n)]

    def start(ins, outs, sems):
        for cp in copies(ins, outs, sems):
            cp.start()

    def finish(ins, outs, sems):
        for cp in copies(ins, outs, sems):
            cp.wait()

    shapes = [_sds(a.shape[1:] if pick_other else a.shape, a.dtype) for a in arrs]
    return Exchange(list(arrs), shapes, [pltpu.SemaphoreType.DMA((n,)), pltpu.SemaphoreType.DMA((n,))],
                    start, finish)


def join(*parts):
    parts = [p for p in parts if p is not None]
    if not parts:
        return None

    def split(refs, counts):
        out, pos = [], 0
        for cnt in counts:
            out.append(refs[pos:pos + cnt])
            pos += cnt
        return out

    n_in = [len(p.ins) for p in parts]
    n_out = [len(p.out_shapes) for p in parts]
    n_sem = [len(p.sems) for p in parts]

    def run(which):
        def go(ins, outs, sems):
            for p, i, o, s in zip(parts, split(ins, n_in), split(outs, n_out), split(sems, n_sem)):
                getattr(p, which)(i, o, s)
        return go

    return Exchange([a for p in parts for a in p.ins], [s for p in parts for s in p.out_shapes],
                    [s for p in parts for s in p.sems], run("start"), run("finish"))


def _pcall(body, *, out_shape, grid, in_specs, out_specs, args, name, scratch_shapes=(), ride=None, aliases=None):
    out_shape, out_specs, in_specs = tuple(out_shape), tuple(out_specs), list(in_specs)
    aliases = aliases or {}
    if ride is None:
        return pl.pallas_call(body, out_shape=out_shape, grid=grid, in_specs=in_specs, out_specs=out_specs,
                              scratch_shapes=list(scratch_shapes), input_output_aliases=aliases, name=name,
                              compiler_params=_cp(len(grid)))(*args)
    n_in, n_out, n_sc = len(args), len(out_shape), len(scratch_shapes)
    n_xi, n_xo = len(ride.ins), len(ride.out_shapes)

    def wrapped(*refs):
        pos = [0]

        def take(cnt):
            got = refs[pos[0]:pos[0] + cnt]
            pos[0] += cnt
            return got

        c_in, x_in, c_out, x_out, c_sc = take(n_in), take(n_xi), take(n_out), take(n_xo), take(n_sc)
        sems = refs[pos[0]:]
        first, last = True, True
        for d, size in enumerate(grid):
            first = jnp.logical_and(first, pl.program_id(d) == 0)
            last = jnp.logical_and(last, pl.program_id(d) == size - 1)

        @pl.when(first)
        def _():
            ride.start(x_in, x_out, sems)

        body(*c_in, *c_out, *c_sc)

        @pl.when(last)
        def _():
            ride.finish(x_in, x_out, sems)

    if ride.inplace:
        aliases = {**aliases, **{n_in + t: n_out + t for t in range(n_xi)}}
    res = pl.pallas_call(
        wrapped, out_shape=out_shape + tuple(ride.out_shapes), grid=grid,
        in_specs=in_specs + [ANY] * n_xi, out_specs=out_specs + (ANY,) * n_xo,
        scratch_shapes=list(scratch_shapes) + list(ride.sems), input_output_aliases=aliases, name=name,
        compiler_params=_cp(len(grid)))(*args, *ride.ins)
    return res[:n_out], res[n_out:]


def run_exchange(ex, name):
    n_xi, n_xo = len(ex.ins), len(ex.out_shapes)

    def body(*refs):
        ins, outs, sems = refs[:n_xi], refs[n_xi:n_xi + n_xo], refs[n_xi + n_xo:]
        ex.start(ins, outs, sems)
        ex.finish(ins, outs, sems)

    aliases = {t: t for t in range(n_xi)} if ex.inplace else {}
    return pl.pallas_call(body, out_shape=tuple(ex.out_shapes), in_specs=[ANY] * n_xi, out_specs=(ANY,) * n_xo,
                          scratch_shapes=list(ex.sems), input_output_aliases=aliases, name=name)(*ex.ins)


HBM_SPEC = pl.BlockSpec(memory_space=pltpu.HBM)
SEM_SPEC = pl.BlockSpec(memory_space=pltpu.SEMAPHORE)
EFFECT = pltpu.SideEffectType.DATAFLOW_SIDE_EFFECTING


def _route(kind, src, dst, c, k, peer):
    if kind == "gather":
        return src.at[c], dst.at[c, k], dst.at[c, peer]
    return src.at[peer], dst.at[k], dst.at[peer]


def split_start(batches, kind, name):
    flat = [a for batch in batches for a in batch]
    n, nb = len(flat), len(batches)
    lands = [lax.empty((2, N_CHIPS) + a.shape[1:] if kind == "gather" else a.shape, a.dtype) for a in flat]

    def body(*refs):
        srcs, dsts, sems, token = refs[:n], refs[n:2 * n], refs[2 * n:2 * n + 2 * nb], refs[-1]
        x, y, c, k, others = _place()
        pos = 0
        for b, batch in enumerate(batches):
            for a in range(len(batch)):
                for j, (px, py) in enumerate(others):
                    src, dst, _ = _route(kind, srcs[pos], dsts[pos], c, k, 2 * px + py)
                    _rc(src, dst, sems[2 * b].at[3 * a + j], sems[2 * b + 1].at[3 * a + j], (px, py, c)).start()
                pos += 1
        token[...] = jnp.zeros(token.shape, token.dtype)

    sem_shapes = [pltpu.SemaphoreType.DMA((3 * len(batch),)) for batch in batches for _ in range(2)]
    thru = [pltpu.HBM(a.shape, a.dtype) for a in flat] + [pltpu.HBM(l.shape, l.dtype) for l in lands]
    res = pl.pallas_call(
        body, name=name, out_shape=tuple(sem_shapes + thru + [_sds((8, LANES), F32)]),
        in_specs=[HBM_SPEC] * (2 * n),
        out_specs=tuple([SEM_SPEC] * (2 * nb) + [HBM_SPEC] * (2 * n) + [pl.BlockSpec(memory_space=pltpu.VMEM)]),
        input_output_aliases={t: 2 * nb + t for t in range(2 * n)},
        compiler_params=pltpu.CompilerParams(has_side_effects=EFFECT),
    )(*[pltpu.with_memory_space_constraint(t, pltpu.HBM) for t in flat + lands])
    sems, srcs, dsts = res[:2 * nb], res[2 * nb:2 * nb + n], res[2 * nb + n:2 * nb + 2 * n]
    out, pos = [], 0
    for b, batch in enumerate(batches):
        out.append((sems[2 * b], sems[2 * b + 1], list(srcs[pos:pos + len(batch)]), list(dsts[pos:pos + len(batch)])))
        pos += len(batch)
    return out, res[-1]


def split_arrive(handle, kind, after, name):
    send, recv, srcs, dsts = handle
    n = len(srcs)

    def body(*refs):
        s_refs, d_refs, send_ref, recv_ref = refs[:n], refs[n:2 * n], refs[2 * n], refs[2 * n + 1]
        x, y, c, k, others = _place()
        for a in range(n):
            for j, (px, py) in enumerate(others):
                src, _, landed = _route(kind, s_refs[a], d_refs[a], c, k, 2 * px + py)
                cp = _rc(src, landed, send_ref.at[3 * a + j], recv_ref.at[3 * a + j], (px, py, c))
                cp.wait_send()
                cp.wait_recv()

    res = pl.pallas_call(
        body, name=name, out_shape=tuple([pltpu.HBM(t.shape, t.dtype) for t in srcs + dsts]),
        in_specs=[HBM_SPEC] * (2 * n) + [SEM_SPEC, SEM_SPEC, ANY], out_specs=tuple([HBM_SPEC] * (2 * n)),
        input_output_aliases={t: t for t in range(2 * n)},
        compiler_params=pltpu.CompilerParams(has_side_effects=EFFECT),
    )(*srcs, *dsts, send, recv, after)
    return list(res[n:])


def forward_cores(bufs):
    n = len(bufs)

    def copies(outs, sems):
        send, recv = sems
        x, y, c, k, others = _place()
        onward, land = [], []
        for a in range(n):
            for j, (px, py) in enumerate(others):
                blk = outs[a].at[c, 2 * px + py]
                onward.append(_rc(blk, blk, send.at[a, j], recv.at[a, j], (x, y, 1 - c)))
                blk2 = outs[a].at[1 - c, 2 * px + py]
                land.append(_rc(blk2, blk2, send.at[a, j], recv.at[a, j], (x, y, 1 - c)))
        return onward, land

    def start(ins, outs, sems):
        for cp in copies(outs, sems)[0]:
            cp.start()

    def finish(ins, outs, sems):
        onward, land = copies(outs, sems)
        for arrived in land:
            arrived.wait_recv()
        for cp in onward:
            cp.wait_send()

    return Exchange(list(bufs), [_sds(b.shape, b.dtype) for b in bufs],
                    [pltpu.SemaphoreType.DMA((n, 3)), pltpu.SemaphoreType.DMA((n, 3))], start, finish, inplace=True)


def all_reduce_small(buf, name):
    r = buf.shape[0]
    n_dev = 8

    def body(in_ref, o_ref, land, send_sems, recv_sems):
        x, y, c, _, _ = _place()
        me = 4 * x + 2 * y + c
        land[me] = in_ref[...]
        sends = []
        for d in range(1, n_dev):
            peer = (x ^ (d >> 2), y ^ ((d >> 1) & 1), c ^ (d & 1))
            cp = _rc(in_ref, land.at[me], send_sems.at[d], recv_sems.at[d], peer)
            cp.start()
            sends.append(cp)
        for d in range(1, n_dev):
            blk = land.at[me ^ d]
            _rc(blk, blk, send_sems.at[d], recv_sems.at[d], (x, y, c)).wait_recv()
        for cp in sends:
            cp.wait_send()
        tot = land[0]
        for d in range(1, n_dev):
            tot = tot + land[d]
        o_ref[...] = tot

    vm = pl.BlockSpec(memory_space=pltpu.VMEM)
    return pl.pallas_call(
        body, out_shape=_sds(buf.shape, F32), in_specs=[vm], out_specs=vm,
        scratch_shapes=[pltpu.VMEM((n_dev, r, LANES), F32), pltpu.SemaphoreType.DMA((n_dev,)),
                        pltpu.SemaphoreType.DMA((n_dev,))],
        name=name)(buf)


def rmsnorm_fwd(x, w, name):
    s, d = x.shape
    tm = _row_tile(s, 512)

    def body(x_ref, w_ref, o_ref):
        o_ref[...] = _rms_fwd(x_ref[...], w_ref[...]).astype(BF16)

    return _pcall(body, out_shape=[_sds((s, d), BF16)], grid=(s // tm,),
                  in_specs=[pl.BlockSpec((tm, d), lambda i: (i, 0)), pl.BlockSpec((1, d), lambda i: (0, 0))],
                  out_specs=[pl.BlockSpec((tm, d), lambda i: (i, 0))], args=[x, w.reshape(1, d)], name=name)[0]


def _ffn_w_spec(chip_of, single=False):
    mode = dict(pipeline_mode=pl.Buffered(1)) if single else {}
    return pl.BlockSpec((N_CORES, 1, FF_PART, D_MODEL), lambda *ids: (0, chip_of(*ids), 0, 0), **mode)


def ffn_fwd(h, x, wg, wu, wd, norm_ws, name, ride=None):
    s, d = h.shape
    n_norm = len(norm_ws)
    tm = _row_tile(s, 1024)

    def body(*refs):
        h_ref, x_ref, wg_ref, wu_ref, wd_ref = refs[:5]
        nw_refs = refs[5:5 + n_norm]
        o_ref = refs[5 + n_norm]
        h_refs = refs[6 + n_norm:6 + 2 * n_norm]
        gu_ref, acc = refs[6 + 2 * n_norm], refs[7 + 2 * n_norm]
        k = pl.program_id(1)

        @pl.when(k == 0)
        def _():
            acc[...] = jnp.zeros(acc.shape, F32)

        hm = tm // 2
        for part in range(2):
            sub = pl.ds(part * hm, hm)
            hb = h_ref[sub, :]
            g = _dot(hb, wg_ref[...].reshape(FF_SHARD, d), NT)
            u = _dot(hb, wu_ref[...].reshape(FF_SHARD, d), NT)
            gu_ref[0, 0, sub, :] = g.astype(BF16)
            gu_ref[0, 1, sub, :] = u.astype(BF16)
            acc[sub, :] += _dot((g * _sigmoid(g) * u).astype(BF16), wd_ref[...].reshape(FF_SHARD, d))

        @pl.when(k == N_CHIPS - 1)
        def _():
            xn = x_ref[...] + 0.5 * acc[...]
            o_ref[...] = xn
            for nw_ref, hn_ref in zip(nw_refs, h_refs):
                hn_ref[...] = _rms_fwd(xn, nw_ref[...]).astype(BF16)

    row = pl.BlockSpec((tm, d), lambda i, k: (i, 0))
    vec = pl.BlockSpec((1, d), lambda i, k: (0, 0))
    wsp = _ffn_w_spec(lambda i, k: k)
    return _pcall(
        body, out_shape=[_sds((s, d), F32)] + [_sds((s, d), BF16)] * n_norm + [_sds((N_CHIPS, 2, s, FF_SHARD), BF16)],
        grid=(s // tm, N_CHIPS),
        in_specs=[row, row, wsp, wsp, wsp] + [vec] * n_norm,
        out_specs=[row] * (1 + n_norm) + [pl.BlockSpec((1, 2, tm, FF_SHARD), lambda i, k: (k, 0, i, 0))],
        scratch_shapes=[pltpu.VMEM((tm, d), F32)],
        args=[h, x, wg, wu, wd] + [nw.reshape(1, d) for nw in norm_ws], name=name, ride=ride)


def ffn_bwd(dxn, h, x_in, nw, gu, wg, wu, wd, name, ride=None):
    s, d = h.shape
    tm = _row_tile(s, 512)
    ni = s // tm
    last_e = N_CHIPS - 1

    def body(dxn_ref, h_ref, x_ref, nw_ref, gu_ref, wg_ref, wu_ref, wd_ref,
             dx_ref, dnw_ref, dwg_ref, dwu_ref, dwd_ref, dh, wacc):
        e = pl.program_id(0)
        i = pl.program_id(1)
        rows = pl.ds(pl.multiple_of(i * tm, tm), tm)

        @pl.when(i == 0)
        def _():
            wacc[...] = jnp.zeros(wacc.shape, F32)

        @pl.when(e == 0)
        def _():
            dh[rows, :] = jnp.zeros((tm, d), F32)

        hm = tm // 2
        for part in range(2):
            sub = pl.ds(part * hm, hm)
            dxb = dxn_ref[sub, :].astype(BF16)
            hb = h_ref[sub, :]
            g = gu_ref[0, 0, sub, :].astype(F32)
            u = gu_ref[0, 1, sub, :].astype(F32)
            drows = pl.ds(pl.multiple_of(i * tm + part * hm, hm), hm)
            sg = _sigmoid(g)
            silu = g * sg
            wacc[2] += _dot((0.5 * silu * u).astype(BF16), dxb, TN)
            da = 0.5 * _dot(dxb, wd_ref[...].reshape(FF_SHARD, d), NT)
            dg = (da * u * (sg * (1.0 + g * (1.0 - sg)))).astype(BF16)
            wacc[0] += _dot(dg, hb, TN)
            du = (da * silu).astype(BF16)
            dh[drows, :] += _dot(dg, wg_ref[...].reshape(FF_SHARD, d))
            wacc[1] += _dot(du, hb, TN)
            dh[drows, :] += _dot(du, wu_ref[...].reshape(FF_SHARD, d))

        @pl.when(i == ni - 1)
        def _():
            for t, dw_ref in enumerate((dwg_ref, dwu_ref, dwd_ref)):
                dw_ref[...] = wacc[t].astype(BF16).reshape(N_CORES, 1, FF_PART, d)

        @pl.when(e == last_e)
        def _():
            dx, dnw = _rms_bwd(dh[rows, :], x_ref[...], nw_ref[...])
            dx_ref[...] = dxn_ref[...] + dx
            col = jnp.sum(dnw, axis=0, keepdims=True)

            @pl.when(i == 0)
            def _():
                dnw_ref[...] = col

            @pl.when(i > 0)
            def _():
                dnw_ref[...] += col

    row = pl.BlockSpec((tm, d), lambda e, i: (i, 0))
    late = pl.BlockSpec((tm, d), lambda e, i: (jnp.where(e == last_e, i, 0), 0))
    vec = pl.BlockSpec((1, d), lambda e, i: (0, 0))
    wsp = _ffn_w_spec(lambda e, i: e, single=True)
    dwsp = _ffn_w_spec(lambda e, i: e, single=True)
    dw = _sds((N_CORES, N_CHIPS, FF_PART, d), BF16)
    return _pcall(
        body, out_shape=[_sds((s, d), F32), _sds((1, d), F32), dw, dw, dw],
        grid=(N_CHIPS, ni),
        in_specs=[row, row, late, vec, pl.BlockSpec((1, 2, tm, FF_SHARD), lambda e, i: (e, 0, i, 0)), wsp, wsp, wsp],
        out_specs=[late, vec, dwsp, dwsp, dwsp],
        scratch_shapes=[pltpu.VMEM((s, d), F32), pltpu.VMEM((3, FF_SHARD, d), F32)],
        args=[dxn, h, x_in, nw.reshape(1, d), gu, wg, wu, wd], name=name, ride=ride)


def mm_res(a, w, x, name, bias=None, norm_ws=(), ride=None):
    s, k = a.shape
    n = w.shape[1]
    tm = _row_tile(s, 256)
    has_bias = bias is not None
    n_norm = len(norm_ws)

    def body(*refs):
        a_ref, w_ref, x_ref = refs[:3]
        pos = 3
        t = _dot(a_ref[...], w_ref[...])
        if has_bias:
            t = t + refs[pos][...]
            pos += 1
        nw_refs = refs[pos:pos + n_norm]
        o_ref = refs[pos + n_norm]
        h_refs = refs[pos + n_norm + 1:]
        xn = x_ref[...] + t
        o_ref[...] = xn
        for nw_ref, h_ref in zip(nw_refs, h_refs):
            h_ref[...] = _rms_fwd(xn, nw_ref[...]).astype(BF16)

    row = pl.BlockSpec((tm, n), lambda i: (i, 0))
    vec = pl.BlockSpec((1, n), lambda i: (0, 0))
    in_specs = [pl.BlockSpec((tm, k), lambda i: (i, 0)), pl.BlockSpec((k, n), lambda i: (0, 0)), row]
    args = [a, w, x]
    if has_bias:
        in_specs.append(vec)
        args.append(bias.reshape(1, n))
    for nw in norm_ws:
        in_specs.append(vec)
        args.append(nw.reshape(1, n))
    return _pcall(body, out_shape=[_sds((s, n), F32)] + [_sds((s, n), BF16)] * n_norm, grid=(s // tm,),
                  in_specs=in_specs, out_specs=[row] * (1 + n_norm), args=args, name=name, ride=ride)


def mm_nn(a, w, name, bias=None, out_dtype=F32):
    s, k = a.shape
    n = w.shape[1]
    tm = _row_tile(s, 512)
    tn = _col_tile(n)
    has_bias = bias is not None

    def body(*refs):
        a_ref, w_ref = refs[:2]
        o_ref = refs[-1]
        t = _dot(a_ref[...], w_ref[...])
        if has_bias:
            t = t + refs[2][...]
        o_ref[...] = t.astype(out_dtype)

    in_specs = [pl.BlockSpec((tm, k), lambda j, i: (i, 0)), pl.BlockSpec((k, tn), lambda j, i: (0, j))]
    args = [a, w]
    if has_bias:
        in_specs.append(pl.BlockSpec((1, tn), lambda j, i: (0, j)))
        args.append(bias.reshape(1, n))
    return _pcall(body, out_shape=[_sds((s, n), out_dtype)], grid=(n // tn, s // tm), in_specs=in_specs,
                  out_specs=[pl.BlockSpec((tm, tn), lambda j, i: (i, j))], args=args, name=name)[0]


def mm_nt(a, w, name, n=None, row0=0, out_dtype=F32, ride=None):
    s, k = a.shape
    n = w.shape[0] if n is None else n
    tm = _row_tile(s, 512)
    tn = _col_tile(n)
    base = row0 // tn
    assert row0 % tn == 0

    def body(a_ref, w_ref, o_ref):
        o_ref[...] = _dot(a_ref[...].astype(BF16), w_ref[...], NT).astype(out_dtype)

    res = _pcall(body, out_shape=[_sds((s, n), out_dtype)], grid=(n // tn, s // tm),
                 in_specs=[pl.BlockSpec((tm, k), lambda j, i: (i, 0)), pl.BlockSpec((tn, k), lambda j, i: (base + j, 0))],
                 out_specs=[pl.BlockSpec((tm, tn), lambda j, i: (i, j))], args=[a, w], name=name, ride=ride)
    return res[0] if ride is None else (res[0][0], res[1])


def mm_tn(a, b, name, into=None, rows=None, row0=0, m_valid=None, col_sum=False, ride=None):
    s, m = a.shape
    n = b.shape[1]
    mv = m if m_valid is None else m_valid
    tm = _col_tile(m) if m_valid is None else mv
    tn = n if n <= 1024 else _col_tile(n)
    rows = mv if rows is None else rows
    assert row0 % tm == 0 and (m_valid is None or m == LANES)
    assert not col_sum or mv == tm
    base = row0 // tm
    ta = m if m_valid is not None else tm

    def body(*refs):
        a_ref, b_ref = refs[0], refs[1]
        o_ref = refs[-2] if col_sum else refs[-1]
        bf = b_ref[...]
        t = _dot(a_ref[...].astype(BF16), bf.astype(BF16), TN)
        o_ref[...] = t[:tm].astype(BF16)
        if col_sum:
            refs[-1][...] = jnp.sum(bf.astype(F32), axis=0, keepdims=True)

    in_specs = [pl.BlockSpec((s, ta), lambda i, j: (0, i)), pl.BlockSpec((s, tn), lambda i, j: (0, j))]
    args = [a, b]
    aliases = None
    if into is not None:
        in_specs.append(ANY)
        args.append(into)
        aliases = {2: 0}
    out_shape = [_sds((rows, n), BF16)]
    out_specs = [pl.BlockSpec((tm, tn), lambda i, j: (base + i, j))]
    if col_sum:
        out_shape.append(_sds((1, n), F32))
        out_specs.append(pl.BlockSpec((1, tn), lambda i, j: (0, j)))
    res = _pcall(body, out_shape=out_shape, grid=(mv // tm, n // tn), in_specs=in_specs, out_specs=out_specs,
                 args=args, name=name, ride=ride, aliases=aliases)
    outs = res if ride is None else res[0]
    out = (outs[0], outs[1][0]) if col_sum else outs[0]
    return out if ride is None else (out, res[1])


def mm_rms_bwd(terms, dxn, x, nw, name, ride=None):
    s, n = x.shape
    nt_ = len(terms)
    tm = _row_tile(s, 256)
    forms = [t[5] for t in terms]

    def body(*refs):
        dxn_ref, x_ref, nw_ref, dx_ref, dnw_ref = refs[2 * nt_:]
        i = pl.program_id(0)
        dh = None
        for t in range(nt_):
            part = _dot(refs[2 * t][...].astype(BF16), refs[2 * t + 1][...], NN if forms[t] == "nn" else NT)
            dh = part if dh is None else dh + part
        dx, dnw = _rms_bwd(dh, x_ref[...], nw_ref[...])
        dx_ref[...] = dxn_ref[...] + dx
        col = jnp.sum(dnw, axis=0, keepdims=True)

        @pl.when(i == 0)
        def _():
            dnw_ref[...] = col

        @pl.when(i > 0)
        def _():
            dnw_ref[...] += col

    in_specs, args = [], []
    for a, cb, w, rb, kb, form in terms:
        in_specs.append(pl.BlockSpec((tm, kb), lambda i, cb=cb: (i, cb)))
        if form == "nn":
            in_specs.append(pl.BlockSpec((kb, n), lambda i, rb=rb: (rb, 0)))
        else:
            in_specs.append(pl.BlockSpec((n, kb), lambda i, rb=rb: (0, rb)))
        args += [a, w]
    row = pl.BlockSpec((tm, n), lambda i: (i, 0))
    vec = pl.BlockSpec((1, n), lambda i: (0, 0))
    res = _pcall(body, out_shape=[_sds((s, n), F32), _sds((1, n), F32)], grid=(s // tm,),
                 in_specs=in_specs + [row, row, vec], out_specs=[row, vec],
                 args=args + [dxn, x, nw.reshape(1, n)], name=name, ride=ride)
    outs = res if ride is None else res[0]
    out = (outs[0], outs[1][0])
    return out if ride is None else (out, res[1])


def rope_tables(s):
    pos = jnp.arange(s, dtype=F32)
    inv = 1.0 / (ROPE_THETA ** (jnp.arange(0, ATT_HEAD_DIM, 2, dtype=F32) / ATT_HEAD_DIM))
    ang = pos[:, None] * inv[None, :]
    cos = jnp.tile(jnp.cos(ang), (1, 2 * LANES // ATT_HEAD_DIM))
    sin = jnp.tile(jnp.sin(ang), (1, 2 * LANES // ATT_HEAD_DIM))
    return cos, sin


def rope_apply(t, cos, sin, name, inverse=False, scale=1.0, out_dtype=BF16):
    s, n = t.shape
    tm = _row_tile(s, 512)
    half = ATT_HEAD_DIM // 2
    reps = n // LANES

    def body(t_ref, c_ref, s_ref, o_ref):
        tf = t_ref[...].astype(F32)
        c = jnp.tile(c_ref[...], (1, reps))
        sn = jnp.tile(s_ref[...], (1, reps))
        lane = lax.broadcasted_iota(jnp.int32, tf.shape, 1)
        first = (lane & (ATT_HEAD_DIM - 1)) < half
        rot = jnp.where(first, -pltpu.roll(tf, n - half, 1), pltpu.roll(tf, half, 1))
        sign = -1.0 if inverse else 1.0
        o_ref[...] = (scale * (tf * c + sign * rot * sn)).astype(out_dtype)

    tab = pl.BlockSpec((tm, LANES), lambda i: (i, 0))
    return _pcall(body, out_shape=[_sds((s, n), out_dtype)], grid=(s // tm,),
                  in_specs=[pl.BlockSpec((tm, n), lambda i: (i, 0)), tab, tab],
                  out_specs=[pl.BlockSpec((tm, n), lambda i: (i, 0))], args=[t, cos, sin], name=name)[0]


CONV_TILE = 256


def _shift_down(u, k):
    if k == 0:
        return u
    row = lax.broadcasted_iota(jnp.int32, u.shape, 0)
    return jnp.where(row >= k, pltpu.roll(u, k, 0), 0.0)


def _shift_up(u, k):
    if k == 0:
        return u
    s = u.shape[0]
    row = lax.broadcasted_iota(jnp.int32, u.shape, 0)
    return jnp.where(row < s - k, pltpu.roll(u, s - k, 0), 0.0)


def _conv_pre(u, w_ref, b_ref):
    pre = b_ref[...] + w_ref[CONV_WIDTH - 1:CONV_WIDTH, :] * u
    for k in range(CONV_WIDTH - 1):
        pre += w_ref[k:k + 1, :] * _shift_down(u, CONV_WIDTH - 1 - k)
    return pre


def conv_fwd(u, w, b, name, ride=None):
    s, c = u.shape

    def body(u_ref, w_ref, b_ref, o_ref):
        pre = _conv_pre(u_ref[...], w_ref, b_ref)
        o_ref[...] = pre * _sigmoid(pre)

    col = pl.BlockSpec((s, CONV_TILE), lambda j: (0, j))
    res = _pcall(body, out_shape=[_sds((s, c), F32)], grid=(c // CONV_TILE,),
                 in_specs=[col, pl.BlockSpec((CONV_WIDTH, CONV_TILE), lambda j: (0, j)),
                           pl.BlockSpec((1, CONV_TILE), lambda j: (0, j))],
                 out_specs=[col], args=[u, w, b.reshape(1, c)], name=name, ride=ride)
    return res[0] if ride is None else (res[0][0], res[1])


def conv_bwd(dxs, db_, dc_, u, w, b, name):
    s, c = u.shape
    n_x = dxs.shape[1] // CONV_TILE
    n_b = db_.shape[1] // CONV_TILE

    def body(dx_ref, dbb_ref, dcc_ref, u_ref, w_ref, b_ref, du_ref, dw_ref, dbias_ref):
        j = pl.program_id(0)
        dact = jnp.where(j < n_x, dx_ref[...], jnp.where(j < n_x + n_b, dbb_ref[...], dcc_ref[...]))
        uf = u_ref[...]
        pre = _conv_pre(uf, w_ref, b_ref)
        sg = _sigmoid(pre)
        dpre = dact * (sg * (1.0 + pre * (1.0 - sg)))
        du = w_ref[CONV_WIDTH - 1:CONV_WIDTH, :] * dpre
        for k in range(CONV_WIDTH - 1):
            du += w_ref[k:k + 1, :] * _shift_up(dpre, CONV_WIDTH - 1 - k)
        du_ref[...] = du
        dbias_ref[...] = jnp.sum(dpre, axis=0, keepdims=True)
        for k in range(CONV_WIDTH):
            dw_ref[k:k + 1, :] = jnp.sum(dpre * _shift_down(uf, CONV_WIDTH - 1 - k), axis=0, keepdims=True)

    col = pl.BlockSpec((s, CONV_TILE), lambda j: (0, j))
    wsp = pl.BlockSpec((CONV_WIDTH, CONV_TILE), lambda j: (0, j))
    bsp = pl.BlockSpec((1, CONV_TILE), lambda j: (0, j))
    du, dw, db = _pcall(
        body, out_shape=[_sds((s, c), F32), _sds((CONV_WIDTH, c), F32), _sds((1, c), F32)], grid=(c // CONV_TILE,),
        in_specs=[pl.BlockSpec((s, CONV_TILE), lambda j: (0, jnp.minimum(j, n_x - 1))),
                  pl.BlockSpec((s, CONV_TILE), lambda j: (0, jnp.clip(j - n_x, 0, n_b - 1))),
                  pl.BlockSpec((s, CONV_TILE), lambda j: (0, jnp.clip(j - n_x - n_b, 0, n_b - 1))),
                  col, wsp, bsp],
        out_specs=[col, wsp, bsp], args=[dxs, db_, dc_, u, w, b.reshape(1, c)], name=name)
    return du, dw, db[0]


def _lane_pick(mat, idx):
    lane = lax.broadcasted_iota(jnp.int32, mat.shape, 1)
    return jnp.sum(jnp.where(lane == idx, mat, 0.0), axis=1, keepdims=True)


def _sub_pick(mat, idx):
    sub = lax.broadcasted_iota(jnp.int32, mat.shape, 0)
    return jnp.sum(jnp.where(sub == idx, mat, 0.0), axis=0, keepdims=True)


def _expand_heads(cols):
    rows = cols[0].shape[0]
    left = lax.broadcasted_iota(jnp.int32, (rows, LANES), 1) < SSM_HEAD_DIM
    return jnp.concatenate(
        [jnp.where(left, cols[2 * p], cols[2 * p + 1]) for p in range(HEADS_PER_GROUP // 2)], axis=1)


def _dot_01(x, ones, ones_first, pieces):
    tot, rest = None, x
    for _ in range(pieces):
        piece = rest.astype(BF16)
        rest = rest - piece.astype(F32)
        part = _dot(ones, piece) if ones_first else _dot(piece, ones)
        tot = part if tot is None else tot + part
    return tot


def _heads_to_lanes(mat, g):
    jj = lax.broadcasted_iota(jnp.int32, (GROUP_DIM, LANES), 0)
    ll = lax.broadcasted_iota(jnp.int32, (GROUP_DIM, LANES), 1)
    sel = (ll == HEADS_PER_GROUP * g + (jj >> 6)).astype(BF16)
    return _dot_01(mat, sel, False, 3)


def _softplus(x):
    return jnp.maximum(x, 0.0) + jnp.log1p(jnp.exp(-jnp.abs(x)))


def _ssd_scalars(dt_ref, bias_ref, a_ref, dtall, csall, cst):
    dta = _softplus(dt_ref[...] + bias_ref[...])
    row = lax.broadcasted_iota(jnp.int32, (CHUNK, CHUNK), 0)
    col = lax.broadcasted_iota(jnp.int32, (CHUNK, CHUNK), 1)
    cs = _dot_01(dta * a_ref[...], (row >= col).astype(BF16), True, 3)
    dtall[...] = dta
    csall[...] = cs
    cst[...] = cs.T


def _decay_mat(cs_col, cs_row):
    row = lax.broadcasted_iota(jnp.int32, (CHUNK, CHUNK), 0)
    col = lax.broadcasted_iota(jnp.int32, (CHUNK, CHUNK), 1)
    return jnp.exp(jnp.where(row >= col, cs_col - cs_row, NEG))


def _head_mask(xpair, right):
    lane = lax.broadcasted_iota(jnp.int32, xpair.shape, 1)
    keep = (lane >= SSM_HEAD_DIM) if right else (lane < SSM_HEAD_DIM)
    return jnp.where(keep, xpair, 0.0)


def _chunk_cols(x_all, g):
    return [_lane_pick(x_all, HEADS_PER_GROUP * g + r) for r in range(HEADS_PER_GROUP)]


def _decay_col(cs_cols):
    return jnp.concatenate(
        [jnp.broadcast_to(jnp.exp(cc[CHUNK - 1:CHUNK, :]), (SSM_HEAD_DIM, 1)) for cc in cs_cols], axis=0)


def ssd_fwd(act, z, dtp, bias_p, a_p, d_p, normw, name, ride=None):
    s = act.shape[0]
    nc = s // CHUNK
    b_off = D_INNER // SSM_STATE
    c_off = b_off + SSM_GROUPS

    def body(xs_ref, b_ref, c_ref, z_ref, dt_ref, bias_ref, a_ref, d_ref, nw_ref,
             yn_ref, y_ref, st_ref, state, dtall, csall, cst):
        c = pl.program_id(0)
        g = pl.program_id(1)

        @pl.when(g == 0)
        def _():
            _ssd_scalars(dt_ref, bias_ref, a_ref, dtall, csall, cst)

        @pl.when(c == 0)
        def _():
            state[g] = jnp.zeros((GROUP_DIM, SSM_STATE), F32)

        cs_cols = _chunk_cols(csall[...], g)
        dt_cols = _chunk_cols(dtall[...], g)
        cs_rows = [_sub_pick(cst[...], HEADS_PER_GROUP * g + r) for r in range(HEADS_PER_GROUP)]
        d_cols = _chunk_cols(d_ref[...], g)
        cs_exp = _expand_heads(cs_cols)
        dt_exp = _expand_heads(dt_cols)
        d_exp = _expand_heads(d_cols)
        xs = xs_ref[...]
        bb = b_ref[...].astype(BF16)
        cb16 = c_ref[...].astype(BF16)
        xdt = xs * dt_exp
        s_prev = state[g]
        st_ref[0, 0] = s_prev
        y_off = _dot(cb16, s_prev.astype(BF16), NT) * jnp.exp(cs_exp)
        decay_st = jnp.exp(cs_exp[CHUNK - 1:CHUNK, :] - cs_exp)
        contrib = _dot((xdt * decay_st).astype(BF16), bb, TN)
        state[g] = _decay_col(cs_cols) * s_prev + contrib
        cbm = _dot(cb16, bb, NT)
        pairs = []
        for p in range(HEADS_PER_GROUP // 2):
            xpair = xdt[:, LANES * p:LANES * (p + 1)]
            m0 = (cbm * _decay_mat(cs_cols[2 * p], cs_rows[2 * p])).astype(BF16)
            m1 = (cbm * _decay_mat(cs_cols[2 * p + 1], cs_rows[2 * p + 1])).astype(BF16)
            pairs.append(_dot(m0, _head_mask(xpair, False).astype(BF16))
                         + _dot(m1, _head_mask(xpair, True).astype(BF16)))
        y = jnp.concatenate(pairs, axis=1) + y_off + xs * d_exp
        y_ref[...] = y
        zf = z_ref[...]
        yg = y * (zf * _sigmoid(zf))
        yn_ref[...] = _rms_fwd(yg, nw_ref[...]).astype(BF16)

    grp = pl.BlockSpec((CHUNK, GROUP_DIM), lambda c, g: (c, g))
    par = pl.BlockSpec((1, LANES), lambda c, g: (0, 0))
    return _pcall(
        body,
        out_shape=[_sds((s, D_INNER), BF16), _sds((s, D_INNER), F32),
                   _sds((nc, SSM_GROUPS, GROUP_DIM, SSM_STATE), F32)],
        grid=(nc, SSM_GROUPS),
        in_specs=[grp,
                  pl.BlockSpec((CHUNK, SSM_STATE), lambda c, g: (c, b_off + g)),
                  pl.BlockSpec((CHUNK, SSM_STATE), lambda c, g: (c, c_off + g)),
                  grp,
                  pl.BlockSpec((CHUNK, LANES), lambda c, g: (c, 0)),
                  par, par, par,
                  pl.BlockSpec((1, GROUP_DIM), lambda c, g: (0, g))],
        out_specs=[grp, grp, pl.BlockSpec((1, 1, GROUP_DIM, SSM_STATE), lambda c, g: (c, g, 0, 0))],
        scratch_shapes=[pltpu.VMEM((SSM_GROUPS, GROUP_DIM, SSM_STATE), F32),
                        pltpu.VMEM((CHUNK, LANES), F32), pltpu.VMEM((CHUNK, LANES), F32),
                        pltpu.VMEM((LANES, CHUNK), F32)],
        args=[act, act, act, z, dtp, bias_p, a_p, d_p, normw], name=name, ride=ride)


def ssd_bwd(dyn, act, z, y_pre, states, dtp, bias_p, a_p, d_p, normw, name, ride=None):
    s = act.shape[0]
    nc = s // CHUNK
    b_off = D_INNER // SSM_STATE
    c_off = b_off + SSM_GROUPS

    def body(dyn_ref, xs_ref, b_ref, c_ref, z_ref, y_ref, st_ref, dt_ref, bias_ref, a_ref, d_ref, nw_ref,
             dxs_ref, db_ref, dc_ref, dz_ref, ddt_ref, dnw_ref, dbias_ref, da_ref, dd_ref,
             dstate, dtall, csall, cst):
        c = pl.program_id(0)
        g = pl.program_id(1)

        @pl.when(g == 0)
        def _():
            _ssd_scalars(dt_ref, bias_ref, a_ref, dtall, csall, cst)
            ddt_ref[...] = jnp.zeros((CHUNK, LANES), F32)

        @pl.when(c == 0)
        def _():
            dstate[g] = jnp.zeros((GROUP_DIM, SSM_STATE), F32)

        @pl.when(jnp.logical_and(c == 0, g == 0))
        def _():
            dnw_ref[...] = jnp.zeros(dnw_ref.shape, F32)
            dbias_ref[...] = jnp.zeros((1, LANES), F32)
            da_ref[...] = jnp.zeros((1, LANES), F32)
            dd_ref[...] = jnp.zeros((1, LANES), F32)

        cs_cols = _chunk_cols(csall[...], g)
        dt_cols = _chunk_cols(dtall[...], g)
        cs_rows = [_sub_pick(cst[...], HEADS_PER_GROUP * g + r) for r in range(HEADS_PER_GROUP)]
        d_cols = _chunk_cols(d_ref[...], g)
        cs_exp = _expand_heads(cs_cols)
        dt_exp = _expand_heads(dt_cols)
        d_exp = _expand_heads(d_cols)
        xs = xs_ref[...]
        bb = b_ref[...].astype(BF16)
        cb16 = c_ref[...].astype(BF16)
        xdt = xs * dt_exp
        s_prev = st_ref[0, 0]
        s_prev16 = s_prev.astype(BF16)
        ds_next = dstate[g]
        ds16 = ds_next.astype(BF16)

        zf = z_ref[...]
        sz = _sigmoid(zf)
        silu_z = zf * sz
        y = y_ref[...]
        yg = y * silu_z
        dout = dyn_ref[...]
        dyg, dnw = _rms_bwd(dout, yg, nw_ref[...])
        dnw_ref[pl.ds(g, 1), :] += jnp.sum(dnw, axis=0, keepdims=True)
        dy = dyg * silu_z
        dz_ref[...] = dyg * y * (sz * (1.0 + zf * (1.0 - sz)))
        dd_ref[...] += jnp.sum(_heads_to_lanes(dy * xs, g), axis=0, keepdims=True)

        exp_cs = jnp.exp(cs_exp)
        decay_st = jnp.exp(cs_exp[CHUNK - 1:CHUNK, :] - cs_exp)
        cs_t = _dot(cb16, s_prev16, NT)
        dyo = dy * exp_cs
        dc_acc = _dot(dyo.astype(BF16), s_prev16, NN)
        g1 = _dot(bb, ds16, NT)
        xds = xdt * decay_st
        db_acc = _dot(xds.astype(BF16), ds16, NN)
        dxdt_off = g1 * decay_st
        t_exp = g1 * xds
        dcs_exp = dy * cs_t * exp_cs - t_exp
        decay_c = _decay_col(cs_cols)
        dstate[g] = decay_c * ds_next + _dot(dyo.astype(BF16), cb16, TN)
        dlast_col = jnp.sum(ds_next * s_prev, axis=1, keepdims=True) * decay_c
        jj = lax.broadcasted_iota(jnp.int32, (GROUP_DIM, LANES), 0)
        ll = lax.broadcasted_iota(jnp.int32, (GROUP_DIM, LANES), 1)
        sel = ll == HEADS_PER_GROUP * g + (jj >> 6)
        dlast = jnp.sum(jnp.where(sel, dlast_col, 0.0), axis=0, keepdims=True)
        t_all = _heads_to_lanes(t_exp, g)
        dlast += jnp.sum(t_all, axis=0, keepdims=True)
        dcs_all = _heads_to_lanes(dcs_exp, g)

        cbm = _dot(cb16, bb, NT)
        dcb = jnp.zeros((CHUNK, CHUNK), F32)
        dcs_rows = jnp.zeros((LANES, CHUNK), F32)
        lane_l = lax.broadcasted_iota(jnp.int32, (CHUNK, LANES), 1)
        sub_l = lax.broadcasted_iota(jnp.int32, (LANES, CHUNK), 0)
        dxdt_pairs = []
        for p in range(HEADS_PER_GROUP // 2):
            xpair16 = xdt[:, LANES * p:LANES * (p + 1)].astype(BF16)
            dypair = dy[:, LANES * p:LANES * (p + 1)]
            acc = None
            for r in (2 * p, 2 * p + 1):
                lm = _decay_mat(cs_cols[r], cs_rows[r])
                m = cbm * lm
                dyh = _head_mask(dypair, r % 2 == 1).astype(BF16)
                dm = _dot(dyh, xpair16, NT)
                dcb += dm * lm
                q = dm * m
                idx = HEADS_PER_GROUP * g + r
                dcs_all += jnp.where(lane_l == idx, jnp.sum(q, axis=1, keepdims=True), 0.0)
                dcs_rows -= jnp.where(sub_l == idx, jnp.sum(q, axis=0, keepdims=True), 0.0)
                part = _dot(m.astype(BF16), dyh, TN)
                acc = part if acc is None else acc + part
            dxdt_pairs.append(acc)
        dxdt = jnp.concatenate(dxdt_pairs, axis=1) + dxdt_off
        dcb16 = dcb.astype(BF16)
        dc_ref[...] = dc_acc + _dot(dcb16, bb, NN)
        db_ref[...] = db_acc + _dot(dcb16, cb16, TN)
        dxs_ref[...] = dxdt * dt_exp + dy * d_exp

        dcs_all += dcs_rows.T
        row = lax.broadcasted_iota(jnp.int32, (CHUNK, CHUNK), 0)
        col = lax.broadcasted_iota(jnp.int32, (CHUNK, CHUNK), 1)
        last_row = lax.broadcasted_iota(jnp.int32, (CHUNK, LANES), 0) == CHUNK - 1
        dcs_all += jnp.where(last_row, dlast, 0.0)
        da_all = _dot_01(dcs_all, (col >= row).astype(BF16), True, 3)
        dta = dtall[...]
        in_group = jnp.logical_and(lane_l >= HEADS_PER_GROUP * g, lane_l < HEADS_PER_GROUP * (g + 1))
        ddt = jnp.where(in_group, da_all * a_ref[...] + _heads_to_lanes(dxdt * xs, g), 0.0)
        da_ref[...] += jnp.sum(jnp.where(in_group, da_all * dta, 0.0), axis=0, keepdims=True)
        ddt_raw = ddt * _sigmoid(dt_ref[...] + bias_ref[...])
        ddt_ref[...] += ddt_raw
        dbias_ref[...] += jnp.sum(ddt_raw, axis=0, keepdims=True)

    rev = lambda c, g: (nc - 1 - c, g)
    grp = pl.BlockSpec((CHUNK, GROUP_DIM), rev)
    st = pl.BlockSpec((CHUNK, SSM_STATE), rev)
    par = pl.BlockSpec((1, LANES), lambda c, g: (0, 0))
    dtb = pl.BlockSpec((CHUNK, LANES), lambda c, g: (nc - 1 - c, 0))
    f = lambda shape: _sds(shape, F32)
    return _pcall(
        body,
        out_shape=[f((s, D_INNER)), f((s, SSM_GROUPS * SSM_STATE)), f((s, SSM_GROUPS * SSM_STATE)),
                   f((s, D_INNER)), f((s, LANES)), f((8, GROUP_DIM)), f((1, LANES)), f((1, LANES)), f((1, LANES))],
        grid=(nc, SSM_GROUPS),
        in_specs=[grp, grp,
                  pl.BlockSpec((CHUNK, SSM_STATE), lambda c, g: (nc - 1 - c, b_off + g)),
                  pl.BlockSpec((CHUNK, SSM_STATE), lambda c, g: (nc - 1 - c, c_off + g)),
                  grp, grp,
                  pl.BlockSpec((1, 1, GROUP_DIM, SSM_STATE), lambda c, g: (nc - 1 - c, g, 0, 0)),
                  dtb, par, par, par,
                  pl.BlockSpec((1, GROUP_DIM), lambda c, g: (0, g))],
        out_specs=[grp, st, st, grp, dtb, pl.BlockSpec((8, GROUP_DIM), lambda c, g: (0, 0)), par, par, par],
        scratch_shapes=[pltpu.VMEM((SSM_GROUPS, GROUP_DIM, SSM_STATE), F32),
                        pltpu.VMEM((CHUNK, LANES), F32), pltpu.VMEM((CHUNK, LANES), F32),
                        pltpu.VMEM((LANES, CHUNK), F32)],
        args=[dyn, act, act, act, z, y_pre, states, dtp, bias_p, a_p, d_p, normw], name=name, ride=ride)


def _attn_probs(q, kp, kc, sink, n):
    sp = _dot(q, kp, NT)
    sc = _dot(q, kc, NT)
    i = lax.broadcasted_iota(jnp.int32, sp.shape, 0) & (WINDOW - 1)
    j = lax.broadcasted_iota(jnp.int32, sp.shape, 1)
    sp = jnp.where(jnp.logical_and(j > i, n > 0), sp, NEG)
    sc = jnp.where(j <= i, sc, NEG)
    m = jnp.maximum(jnp.maximum(jnp.max(sp, axis=1, keepdims=True), jnp.max(sc, axis=1, keepdims=True)), sink)
    pp = jnp.exp(sp - m)
    pc = jnp.exp(sc - m)
    ps = jnp.exp(sink - m)
    inv = 1.0 / (jnp.sum(pp, axis=1, keepdims=True) + jnp.sum(pc, axis=1, keepdims=True) + ps)
    return pp * inv, pc * inv, ps * inv


def attn_fwd(qt, kt, vt, sink_rows, name, ride=None):
    s = qt.shape[1]
    nb = s // WINDOW
    rows = Q_PER_KV * WINDOW

    def body(q_ref, kp_ref, kc_ref, vp_ref, vc_ref, sk_ref, o_ref):
        n = pl.program_id(1)
        q = q_ref[...].reshape(rows, ATT_HEAD_DIM)
        pp, pc, _ = _attn_probs(q, kp_ref[0], kc_ref[0], sk_ref[0], n)
        o = _dot(pp.astype(BF16), vp_ref[0]) + _dot(pc.astype(BF16), vc_ref[0])
        o_ref[...] = o.reshape(Q_PER_KV, WINDOW, ATT_HEAD_DIM).astype(BF16)

    qsp = pl.BlockSpec((Q_PER_KV, WINDOW, ATT_HEAD_DIM), lambda h, n: (h, n, 0))
    prev = pl.BlockSpec((1, WINDOW, ATT_HEAD_DIM), lambda h, n: (h, jnp.maximum(n - 1, 0), 0))
    cur = pl.BlockSpec((1, WINDOW, ATT_HEAD_DIM), lambda h, n: (h, n, 0))
    return _pcall(body, out_shape=[_sds(qt.shape, BF16)], grid=(N_KV_HEADS, nb),
                  in_specs=[qsp, prev, cur, prev, cur, pl.BlockSpec((1, rows, 1), lambda h, n: (h, 0, 0))],
                  out_specs=[qsp], args=[qt, kt, kt, vt, vt, sink_rows], name=name, ride=ride)


def attn_bwd(qt, kt, vt, sink_rows, dot_, name, ride=None):
    s = qt.shape[1]
    nb = s // WINDOW
    rows = Q_PER_KV * WINDOW

    def body(q_ref, kp_ref, kc_ref, vp_ref, vc_ref, sk_ref, do_ref, dq_ref, dk_ref, dv_ref, ds_ref, kacc, vacc):
        n = pl.program_id(1)

        @pl.when(n < nb)
        def _():
            q = q_ref[...].reshape(rows, ATT_HEAD_DIM)
            do = do_ref[...].reshape(rows, ATT_HEAD_DIM)
            kp, kc, vp, vc = kp_ref[0], kc_ref[0], vp_ref[0], vc_ref[0]
            pp, pc, ps = _attn_probs(q, kp, kc, sk_ref[0], n)
            dpp = _dot(do, vp, NT)
            dpc = _dot(do, vc, NT)
            delta = jnp.sum(pp * dpp, axis=1, keepdims=True) + jnp.sum(pc * dpc, axis=1, keepdims=True)
            dsp = (pp * (dpp - delta)).astype(BF16)
            dsc = (pc * (dpc - delta)).astype(BF16)
            dq = _dot(dsp, kp) + _dot(dsc, kc)
            dq_ref[...] = dq.reshape(Q_PER_KV, WINDOW, ATT_HEAD_DIM)
            dk_prev = _dot(dsp, q, TN)
            dv_prev = _dot(pp.astype(BF16), do, TN)

            @pl.when(n == 0)
            def _():
                dk_ref[0] = dk_prev
                dv_ref[0] = dv_prev

            @pl.when(n > 0)
            def _():
                dk_ref[0] = kacc[...] + dk_prev
                dv_ref[0] = vacc[...] + dv_prev

            kacc[...] = _dot(dsc, q, TN)
            vacc[...] = _dot(pc.astype(BF16), do, TN)
            dsk = -ps * delta
            sub = lax.broadcasted_iota(jnp.int32, (8, LANES), 0)
            tile = jnp.zeros((8, LANES), F32)
            for h in range(Q_PER_KV):
                tile += jnp.where(sub == h, jnp.sum(dsk[h * WINDOW:(h + 1) * WINDOW, :], axis=0, keepdims=True), 0.0)
            ds_ref[0, 0] = tile

        @pl.when(n == nb)
        def _():
            dk_ref[0] = kacc[...]
            dv_ref[0] = vacc[...]
            ds_ref[0, 0] = jnp.zeros((8, LANES), F32)

    last = nb - 1
    qsp = pl.BlockSpec((Q_PER_KV, WINDOW, ATT_HEAD_DIM), lambda h, n: (h, jnp.minimum(n, last), 0))
    prev = pl.BlockSpec((1, WINDOW, ATT_HEAD_DIM), lambda h, n: (h, jnp.clip(n - 1, 0, last), 0))
    cur = pl.BlockSpec((1, WINDOW, ATT_HEAD_DIM), lambda h, n: (h, jnp.minimum(n, last), 0))
    dkv = pl.BlockSpec((1, WINDOW, ATT_HEAD_DIM), lambda h, n: (h, jnp.maximum(n - 1, 0), 0))
    f = lambda shape: _sds(shape, F32)
    return _pcall(
        body, out_shape=[f(qt.shape), f(kt.shape), f(vt.shape), f((N_KV_HEADS, nb + 1, 8, LANES))],
        grid=(N_KV_HEADS, nb + 1),
        in_specs=[qsp, prev, cur, prev, cur, pl.BlockSpec((1, rows, 1), lambda h, n: (h, 0, 0)), qsp],
        out_specs=[qsp, dkv, dkv, pl.BlockSpec((1, 1, 8, LANES), lambda h, n: (h, n, 0, 0))],
        scratch_shapes=[pltpu.VMEM((WINDOW, ATT_HEAD_DIM), F32), pltpu.VMEM((WINDOW, ATT_HEAD_DIM), F32)],
        args=[qt, kt, kt, vt, vt, sink_rows, dot_], name=name, ride=ride)


def loss_head(x, w, tgt, name):
    s, d = x.shape
    tm = _row_tile(s, 256)

    def body(x_ref, w_ref, t_ref, loss_ref, dx_ref, dw_ref):
        i = pl.program_id(0)
        xf = x_ref[...]
        wv = w_ref[...]
        r = lax.rsqrt(jnp.mean(xf * xf, axis=-1, keepdims=True) + EPS)
        xhat = xf * r
        e = xhat * wv - t_ref[...]
        part = 0.5 * jnp.sum(jnp.mean(e * e, axis=-1, keepdims=True), axis=0, keepdims=True)
        dy = e * (1.0 / d)
        dxhat = dy * wv
        dx_ref[...] = r * (dxhat - xhat * jnp.mean(dxhat * xhat, axis=-1, keepdims=True))
        col = jnp.sum(dy * xhat, axis=0, keepdims=True)

        @pl.when(i == 0)
        def _():
            loss_ref[...] = jnp.broadcast_to(part, (1, LANES))
            dw_ref[...] = col

        @pl.when(i > 0)
        def _():
            loss_ref[...] += jnp.broadcast_to(part, (1, LANES))
            dw_ref[...] += col

    row = pl.BlockSpec((tm, d), lambda i: (i, 0))
    vec = pl.BlockSpec((1, d), lambda i: (0, 0))
    return _pcall(body, out_shape=[_sds((1, LANES), F32), _sds((s, d), F32), _sds((1, d), F32)], grid=(s // tm,),
                  in_specs=[row, vec, row], out_specs=[pl.BlockSpec((1, LANES), lambda i: (0, 0)), row, vec],
                  args=[x, w.reshape(1, d), tgt], name=name)


def _tile_rows(r, c, max_elems=262144, mult=16):
    best = None
    for t in range(mult, r + 1, mult):
        if r % t == 0 and t * c <= max_elems:
            best = t
    return best or r


def add_pair(xhs, ps, c_idx, name):
    n = len(xhs)
    _, r, c = xhs[0].shape
    tr = _tile_rows(r, c)

    def body(c_ref, *refs):
        for x_ref, p_ref, o_ref in zip(refs[:n], refs[n:2 * n], refs[2 * n:]):
            o_ref[...] = (x_ref[0].astype(F32) + p_ref[...].astype(F32)).astype(BF16)

    blk = pl.BlockSpec((tr, c), lambda i, cr: (i, 0))
    return pl.pallas_call(
        body, out_shape=tuple([_sds((r, c), BF16)] * n),
        grid_spec=pltpu.PrefetchScalarGridSpec(
            num_scalar_prefetch=1, grid=(r // tr,),
            in_specs=[pl.BlockSpec((1, tr, c), lambda i, cr: (cr[0], i, 0))] * n + [blk] * n,
            out_specs=tuple([blk] * n)),
        name=name, compiler_params=_cp(1))(c_idx, *xhs, *ps)


def sum_chips(qs, owns, chip_idx, name):
    n = len(qs)
    _, r, c = qs[0].shape
    tr = _tile_rows(r, c)

    def body(k_ref, *refs):
        k = k_ref[0]
        for q_ref, own_ref, o_ref in zip(refs[:n], refs[n:2 * n], refs[2 * n:]):
            mine = own_ref[0].astype(F32)
            tot = None
            for j in range(N_CHIPS):
                term = jnp.where(k == j, mine, q_ref[j].astype(F32))
                tot = term if tot is None else tot + term
            o_ref[...] = tot

    return pl.pallas_call(
        body, out_shape=tuple([_sds((r, c), F32)] * n),
        grid_spec=pltpu.PrefetchScalarGridSpec(
            num_scalar_prefetch=1, grid=(r // tr,),
            in_specs=([pl.BlockSpec((N_CHIPS, tr, c), lambda i, kr: (0, i, 0))] * n
                      + [pl.BlockSpec((1, tr, c), lambda i, kr: (kr[0], i, 0))] * n),
            out_specs=tuple([pl.BlockSpec((tr, c), lambda i, kr: (i, 0))] * n)),
        name=name, compiler_params=_cp(1))(chip_idx, *qs, *owns)


def adamw(w, g, m, v, name):
    r, c = w.shape
    tr = _tile_rows(r, c, max_elems=131072, mult=8)
    c1 = 1.0 / (1.0 - ADAM_B1 ** ADAM_STEP)
    c2 = 1.0 / (1.0 - ADAM_B2 ** ADAM_STEP)

    def body(w_ref, g_ref, m_ref, v_ref, d_ref, mo_ref, vo_ref):
        gf = g_ref[...]
        mn = ADAM_B1 * m_ref[...] + (1.0 - ADAM_B1) * gf
        vn = ADAM_B2 * v_ref[...] + (1.0 - ADAM_B2) * (gf * gf)
        mo_ref[...] = mn
        vo_ref[...] = vn
        d_ref[...] = -ADAM_LR * ((mn * c1) / (jnp.sqrt(vn * c2) + ADAM_EPS) + ADAM_WD * w_ref[...])

    blk = pl.BlockSpec((tr, c), lambda i: (i, 0))
    out = _sds((r, c), F32)
    return _pcall(body, out_shape=[out, out, out], grid=(r // tr,), in_specs=[blk] * 4, out_specs=[blk] * 3,
                  args=[w, g, m, v], name=name)


WEIGHTS = ['norm_w', 'ffn_w_gate', 'ffn_w_up', 'ffn_w_down', 'ssm_w_in', 'ssm_conv_w', 'ssm_conv_b', 'ssm_dt_bias',
           'ssm_a_log', 'ssm_d', 'ssm_norm_w', 'ssm_w_out', 'kv_norm_w', 'w_k', 'b_k', 'w_v', 'b_v', 'attn_w_q',
           'attn_b_q', 'attn_sinks', 'attn_w_o', 'attn_b_o', 'final_norm_w']
BIG = ['ffn_w_gate', 'ffn_w_up', 'ffn_w_down', 'ssm_w_in', 'ssm_w_out', 'w_k', 'w_v', 'attn_w_q', 'attn_w_o']
TRANSPOSED = ('ffn_w_gate', 'ffn_w_up', 'ssm_w_in')
SMALL = [n for n in WEIGHTS if n not in BIG]
SMALL_SHARDED = {'norm_w': 2, 'ssm_conv_w': 2, 'ssm_conv_b': 1, 'ssm_norm_w': 1}
ROW_ALIGN = 8 * LANES


def _pack_rows(parts):
    flat = jnp.concatenate([p.reshape(-1).astype(F32) for p in parts])
    pad = (-flat.size) % ROW_ALIGN
    return jnp.pad(flat, (0, pad)).reshape(-1, LANES)


def _unpack_rows(buf, shapes):
    flat = buf.reshape(-1)
    out, pos = [], 0
    for shp in shapes:
        size = math.prod(shp)
        out.append(flat[pos:pos + size].reshape(shp))
        pos += size
    return out


def _as2d(a):
    return a.reshape(-1, a.shape[-1])


def _heads_major(t, n_heads):
    s = t.shape[0]
    return t.reshape(s, n_heads, ATT_HEAD_DIM).transpose(1, 0, 2)


def _tokens_major(t):
    h, s, dh = t.shape
    return t.transpose(1, 0, 2).reshape(s, h * dh)


def _pad_lanes(v):
    return jnp.pad(v.reshape(1, -1), ((0, 0), (0, LANES - v.size)))


def _chips_first(t):
    return t.swapaxes(0, 1).reshape((-1,) + t.shape[3:])


def _parts_first(t, rows):
    return t.reshape((N_CHIPS, N_CORES, rows) + t.shape[1:]).swapaxes(0, 1)


def kernel(*args):
    names = (['x'] + WEIGHTS + ['loss_target'] + ['m_' + n for n in WEIGHTS] + ['v_' + n for n in WEIGHTS])
    a = dict(zip(names, args))
    for n in TRANSPOSED:
        for pre in ('', 'm_', 'v_'):
            a[pre + n] = a[pre + n].swapaxes(-1, -2)
    xi, yi, ci = lax.axis_index("x"), lax.axis_index("y"), lax.axis_index("c")
    chip = 2 * xi + yi
    south = ci == 0
    c_idx = jnp.reshape(ci, (1,)).astype(jnp.int32)
    chip_idx = jnp.reshape(chip, (1,)).astype(jnp.int32)
    x0 = a['x'][0]
    s = x0.shape[0]
    cos, sin = rope_tables(s)

    def own_slot(full, mine):
        return lax.dynamic_update_slice_in_dim(full, mine[:, None], chip, axis=1)

    def ffn_shard(l, i, src):
        return [src[n][l, i].astype(BF16).reshape(N_CORES, FF_PART, D_MODEL)
                for n in ('ffn_w_gate', 'ffn_w_up', 'ffn_w_down')]

    def own_slots(fulls, mines):
        return [own_slot(f, m) for f, m in zip(fulls, mines)]
    small_names = list(SMALL_SHARDED)
    small_sh = _pack_rows([a[n] for n in small_names])
    small_sh = small_sh.reshape(N_CORES, small_sh.shape[0] // 2, LANES)
    sh00 = ffn_shard(0, 0, a)
    (first_flight,), started = split_start([sh00 + [small_sh]], "gather", "gather_start_first")
    held = lax.optimization_barrier((started, {n: a[n] for n in BIG}))[1]
    sh01, sh10, sh11 = ffn_shard(0, 1, held), ffn_shard(1, 0, held), ffn_shard(1, 1, held)
    w_in_sh = jnp.pad(held['ssm_w_in'][0], ((0, IN_SHARD_PAD - IN_SHARD), (0, 0))).astype(BF16).reshape(
        N_CORES, IN_SHARD_PAD // 2, D_MODEL)
    w_out_sh = held['ssm_w_out'][0].astype(BF16).reshape(N_CORES, 256, D_MODEL)
    attn_sh = jnp.stack([held['attn_w_q'][0], held['attn_w_o'][0]]).astype(BF16)
    kv_sh = jnp.stack([held['w_k'], held['w_v']]).astype(BF16)
    rest_flights, all_started = split_start([[w_in_sh, kv_sh], [w_out_sh], sh01, sh10, [attn_sh], sh11], "gather",
                                            "gather_start_rest")
    in_flight = [first_flight] + rest_flights

    def arrive(idx, after, tag):
        return forward_cores(split_arrive(in_flight[idx], "gather", after, "gather_arrive_" + tag))

    first = run_exchange(arrive(0, all_started, "first"), "gather_hop_first")
    w00 = own_slots(first[:3], sh00)
    smalls = own_slot(first[3], small_sh)
    p = {}
    per_chip = [_unpack_rows(smalls[:, k], [a[n].shape for n in small_names]) for k in range(N_CHIPS)]
    for idx, n in enumerate(small_names):
        p[n] = jnp.concatenate([per_chip[k][idx] for k in range(N_CHIPS)], axis=SMALL_SHARDED[n])
    nw = p['norm_w']
    conv_w, conv_b, ssm_nw = p['ssm_conv_w'][0], p['ssm_conv_b'][0], p['ssm_norm_w'][0].reshape(1, D_INNER)

    h00 = rmsnorm_fwd(x0, nw[0, 0], "norm_in")
    x1, h01, gu00 = ffn_fwd(h00, x0, *w00, [nw[0, 1]], "ffn_fwd_00")
    w_in_g, kv_g = run_exchange(arrive(1, x1, "in"), "gather_hop_in")
    w_in_t = _chips_first(own_slot(w_in_g, w_in_sh)).reshape(N_CHIPS, IN_SHARD_PAD, D_MODEL)[:, :IN_SHARD].reshape(
        IN_PROJ_DIM, D_MODEL)
    w_dt_t = jnp.pad(w_in_t[D_INNER + CONV_DIM:], ((0, LANES - SSM_HEADS), (0, 0)))
    kv_g = own_slot(kv_g, kv_sh)
    w_k, w_v = kv_g[0].reshape(D_MODEL, KV_DIM), kv_g[1].reshape(D_MODEL, KV_DIM)

    zz = mm_nt(h01, w_in_t, "ssm_in_z", n=D_INNER)
    xbc = mm_nt(h01, w_in_t, "ssm_in_xbc", n=CONV_DIM, row0=D_INNER)
    dtp = mm_nt(h01, w_dt_t, "ssm_in_dt")
    act = conv_fwd(xbc, conv_w, conv_b, "ssm_conv")
    bias_p = _pad_lanes(a['ssm_dt_bias'][0])
    a_p = _pad_lanes(-jnp.exp(a['ssm_a_log'][0]))
    d_p = _pad_lanes(a['ssm_d'][0])
    (yn, y_pre, states), (w_out_g,) = ssd_fwd(act, zz, dtp, bias_p, a_p, d_p, ssm_nw, "ssd_fwd",
                                              ride=arrive(2, act, "out"))
    w_out = _chips_first(own_slot(w_out_g, w_out_sh))
    (x2, h02), w01 = mm_res(yn, w_out, x1, "ssm_out", norm_ws=[nw[0, 2]], ride=arrive(3, yn, "01"))
    w01 = own_slots(w01, sh01)
    x3, hkv, h10, gu01 = ffn_fwd(h02, x2, *w01, [a['kv_norm_w'], nw[1, 0]], "ffn_fwd_01")
    w10 = own_slots(run_exchange(arrive(4, x3, "10"), "gather_hop_10"), sh10)

    k_rot = rope_apply(mm_nn(hkv, w_k, "kv_k", bias=a['b_k']), cos, sin, "rope_k")
    v = mm_nn(hkv, w_v, "kv_v", bias=a['b_v'], out_dtype=BF16)
    kt = _heads_major(k_rot, N_KV_HEADS)
    vt = _heads_major(v, N_KV_HEADS)

    (x4, h11, gu10), (attn_g,) = ffn_fwd(h10, x3, *w10, [nw[1, 1]], "ffn_fwd_10", ride=arrive(5, v, "attn"))
    attn_g = own_slot(attn_g, attn_sh)
    w_q, w_o = attn_g[0].reshape(D_MODEL, D_MODEL), attn_g[1].reshape(D_MODEL, D_MODEL)
    scale = 1.0 / math.sqrt(ATT_HEAD_DIM)
    q_rot = rope_apply(mm_nn(h11, w_q, "attn_q", bias=a['attn_b_q'][0]), cos, sin, "rope_q", scale=scale)
    qt = _heads_major(q_rot, N_Q_HEADS)
    sink_rows = jnp.repeat(a['attn_sinks'][0].reshape(N_KV_HEADS, Q_PER_KV), WINDOW, axis=1).reshape(
        N_KV_HEADS, Q_PER_KV * WINDOW, 1)
    (ot,) = attn_fwd(qt, kt, vt, sink_rows, "attn_fwd")
    o = _tokens_major(ot)
    (x5, h12), w11 = mm_res(o, w_o, x4, "attn_out", bias=a['attn_b_o'][0], norm_ws=[nw[1, 2]],
                            ride=arrive(6, ot, "11"))
    w11 = own_slots(w11, sh11)
    x6, gu11 = ffn_fwd(h12, x5, *w11, [], "ffn_fwd_11")

    loss_v, dx6, d_final = loss_head(x6, a['final_norm_w'], a['loss_target'][0], "loss_head")
    loss = lax.psum(loss_v[0, 0], ("x", "y", "c"))
    g = {'final_norm_w': d_final[0]}

    def same_shape(xs, ys):
        runs = []
        for xv, yv in zip(xs, ys):
            if runs and runs[-1][0][0].shape == xv.shape:
                runs[-1][0].append(xv)
                runs[-1][1].append(yv)
            else:
                runs.append(([xv], [yv]))
        return runs

    def pre_reduce(grads, sib, tag):
        out = []
        for idx, (grp, sbs) in enumerate(same_shape(grads, list(sib))):
            ts = add_pair([gr.reshape(2, -1, gr.shape[-1]) for gr in grp], [_as2d(sb) for sb in sbs], c_idx,
                          "rs_add_%s_%d" % (tag, idx))
            out += [t.reshape(gr.shape[1:]) for t, gr in zip(ts, grp)]
        return out

    def chip_sum(landed, parts, tag):
        out = []
        for idx, (qs, owns) in enumerate(same_shape(list(landed), parts)):
            ts = sum_chips([q.reshape(N_CHIPS, -1, q.shape[-1]) for q in qs],
                           [own.reshape(N_CHIPS, -1, own.shape[-1]) for own in owns], chip_idx,
                           "rs_sum_%s_%d" % (tag, idx))
            out += [t.reshape(q.shape[1:]) for t, q in zip(ts, qs)]
        return out

    dnw = [[None] * 3 for _ in range(2)]
    sums = {}

    def trade(key):
        return swap_cores(sums[key], False)

    dx5, dnw12, *g11 = ffn_bwd(dx6, h12, x5, nw[1, 2], gu11, *w11, "ffn_bwd_11")
    dnw[1][2] = dnw12[0]
    (d_wo, g['attn_b_o']), sib11 = mm_tn(o, dx5, "attn_dwo", col_sum=True, ride=swap_cores(g11, True))
    t11 = pre_reduce(g11, sib11, "11")
    do = mm_nt(dx5, w_o, "attn_do", out_dtype=BF16)
    (dqt, dkt, dvt, dsink), land11 = attn_bwd(qt, kt, vt, sink_rows, _heads_major(do, N_Q_HEADS), "attn_bwd",
                                             ride=scatter_chips(t11))
    sums['11'] = chip_sum(land11, t11, "11")
    g['attn_sinks'] = jnp.sum(dsink[:, :, :Q_PER_KV, 0], axis=1).reshape(N_Q_HEADS)
    dq_pre = rope_apply(_tokens_major(dqt), cos, sin, "rope_dq", inverse=True, scale=scale, out_dtype=F32)
    d_wq, g['attn_b_q'] = mm_tn(h11, dq_pre, "attn_dwq", col_sum=True)
    g_attn = [jnp.stack([d_wq.reshape(N_CHIPS, 256, D_MODEL), d_wo.reshape(N_CHIPS, 256, D_MODEL)])]
    (dx4, dnw[1][1]), sib_attn = mm_rms_bwd([(dq_pre, 0, w_q, 0, D_MODEL, "nt")], dx5, x4, nw[1, 1], "attn_bwd_dh",
                                            ride=swap_cores(g_attn, True))
    t_attn = pre_reduce(g_attn, sib_attn, "attn")
    (dx3, dnw10, *g10), landed = ffn_bwd(dx4, h10, x3, nw[1, 0], gu10, *w10, "ffn_bwd_10",
                                         ride=join(scatter_chips(t_attn), trade('11')))
    dnw[1][0] = dnw10[0]
    sums['attn'] = chip_sum(landed[:1], t_attn, "attn")
    theirs = {'11': landed[1:]}
    dk_pre = rope_apply(_tokens_major(dkt), cos, sin, "rope_dk", inverse=True, out_dtype=F32)
    dv = _tokens_major(dvt)
    (d_wk, g['b_k']), sib10 = mm_tn(hkv, dk_pre, "kv_dwk", col_sum=True, ride=swap_cores(g10, True))
    t10 = pre_reduce(g10, sib10, "10")
    d_wv, g['b_v'] = mm_tn(hkv, dv, "kv_dwv", col_sum=True)
    g_kv = [jnp.stack([d_wk.reshape(N_CHIPS, 256, KV_DIM), d_wv.reshape(N_CHIPS, 256, KV_DIM)])]
    (dx3, g['kv_norm_w']), sib_kv = mm_rms_bwd(
        [(dk_pre, 0, w_k, 0, KV_DIM, "nt"), (dv, 0, w_v, 0, KV_DIM, "nt")], dx3, x3, a['kv_norm_w'], "kv_bwd_dh",
        ride=swap_cores(g_kv, True))
    t_kv = pre_reduce(g_kv, sib_kv, "kv")
    (dx2, dnw02, *g01), landed = ffn_bwd(dx3, h02, x2, nw[0, 2], gu01, *w01, "ffn_bwd_01",
                                         ride=join(scatter_chips(t10 + t_kv), trade('attn')))
    dnw[0][2] = dnw02[0]
    sums['10'] = chip_sum(landed[:3], t10, "10")
    sums['kv'] = chip_sum(landed[3:4], t_kv, "kv")
    theirs['attn'] = landed[4:]
    d_wout, sib01 = mm_tn(yn, dx2, "ssm_dwout", ride=swap_cores(g01, True))
    t01 = pre_reduce(g01, sib01, "01")
    dyn = mm_nt(dx2, w_out, "ssm_dyn")
    (dxs, db_, dc_, dz, ddt, d_ssm_nw, d_bias, d_a, d_d), landed = ssd_bwd(
        dyn, act, zz, y_pre, states, dtp, bias_p, a_p, d_p, ssm_nw, "ssd_bwd",
        ride=join(scatter_chips(t01), trade('10'), trade('kv')))
    sums['01'] = chip_sum(landed[:3], t01, "01")
    theirs['10'], theirs['kv'] = landed[3:6], landed[6:]
    g['ssm_norm_w'] = d_ssm_nw[:SSM_GROUPS].reshape(D_INNER)
    g['ssm_dt_bias'] = d_bias[0, :SSM_HEADS]
    g['ssm_a_log'] = d_a[0, :SSM_HEADS] * a_p[0, :SSM_HEADS]
    g['ssm_d'] = d_d[0, :SSM_HEADS]
    dxbc, g['ssm_conv_w'], g['ssm_conv_b'] = conv_bwd(dxs, db_, dc_, xbc, conv_w, conv_b, "ssm_conv_bwd")
    d_win = mm_tn(dz, h01, "ssm_dwz", rows=IN_PROJ_DIM)
    d_win = mm_tn(dxbc, h01, "ssm_dwxbc", into=d_win, rows=IN_PROJ_DIM, row0=D_INNER)
    d_win = mm_tn(ddt, h01, "ssm_dwdt", into=d_win, rows=IN_PROJ_DIM, row0=D_INNER + CONV_DIM, m_valid=SSM_HEADS)
    d_win = jnp.pad(d_win.reshape(N_CHIPS, IN_SHARD, D_MODEL), ((0, 0), (0, IN_SHARD_PAD - IN_SHARD), (0, 0)))
    g_ssm = [_parts_first(d_win.reshape(-1, D_MODEL), IN_SHARD_PAD // 2), _parts_first(d_wout, 256)]
    kb = 1024
    terms = ([(dz, j, w_in_t, j, kb, "nn") for j in range(D_INNER // kb)]
             + [(dxbc, j, w_in_t, D_INNER // kb + j, kb, "nn") for j in range(CONV_DIM // kb)]
             + [(ddt, 0, w_dt_t, 0, LANES, "nn")])
    (dx1, dnw[0][1]), sib_ssm = mm_rms_bwd(terms, dx2, x1, nw[0, 1], "ssm_bwd_dh", ride=swap_cores(g_ssm, True))
    t_ssm = pre_reduce(g_ssm, sib_ssm, "ssm")
    (grad_x, dnw00, *g00), landed = ffn_bwd(dx1, h00, x0, nw[0, 0], gu00, *w00, "ffn_bwd_00",
                                            ride=join(scatter_chips(t_ssm), trade('01')))
    dnw[0][0] = dnw00[0]
    sums['ssm'] = chip_sum(landed[:2], t_ssm, "ssm")
    theirs['01'] = landed[2:]
    landed = run_exchange(join(swap_cores(g00, True), trade('ssm')), "rs_swap_00")
    t00 = pre_reduce(g00, landed[:3], "00")
    theirs['ssm'] = landed[3:]

    def both(key):
        return [(jnp.where(south, m_, t_), jnp.where(south, t_, m_)) for m_, t_ in zip(sums[key], theirs[key])]

    (flight00,), _ = split_start([t00], "scatter", "rs_scatter_00_start")

    delta, new_m, new_v, gw = {}, {}, {}, {}
    ffn_names = ('ffn_w_gate', 'ffn_w_up', 'ffn_w_down')
    full = {key: both(key) for key in ('attn', 'kv', 'ssm')}
    lo, hi = full['attn'][0]
    gw['attn_w_q'], gw['attn_w_o'] = lo[None], hi[None]
    lo, hi = full['kv'][0]
    gw['w_k'], gw['w_v'] = lo, hi
    lo, hi = full['ssm'][0]
    gw['ssm_w_in'] = jnp.concatenate([lo, hi], axis=0)[:IN_SHARD][None]
    lo, hi = full['ssm'][1]
    gw['ssm_w_out'] = jnp.concatenate([lo, hi], axis=0)[None]

    g['norm_w'] = jnp.stack([jnp.stack(r) for r in dnw])
    red = all_reduce_small(_pack_rows([g[n] for n in SMALL]), "reduce_vectors")
    for n, t in zip(SMALL, _unpack_rows(red, [g[n].shape for n in SMALL])):
        if n in SMALL_SHARDED:
            ax = SMALL_SHARDED[n] - (a[n].ndim - t.ndim)
            width = a[n].shape[SMALL_SHARDED[n]]
            t = lax.dynamic_slice_in_dim(t, chip * width, width, axis=ax)
        gw[n] = t.reshape(a[n].shape)

    def update(n):
        d, mo, vo = adamw(_as2d(a[n]), _as2d(gw[n]), _as2d(a['m_' + n]), _as2d(a['v_' + n]), "adamw_" + n)
        delta[n], new_m[n], new_v[n] = d.reshape(a[n].shape), mo.reshape(a[n].shape), vo.reshape(a[n].shape)

    for n in BIG:
        if n not in ffn_names:
            update(n)
    shapes = [a[n].shape for n in SMALL]
    packed = [_pack_rows([src[n] for n in SMALL]) for src in
              (a, gw, {n: a['m_' + n] for n in SMALL}, {n: a['v_' + n] for n in SMALL})]
    outs = adamw(*packed, "adamw_vectors")
    for dst, buf in zip((delta, new_m, new_v), outs):
        for n, t in zip(SMALL, _unpack_rows(buf, shapes)):
            dst[n] = t
    rest = [both(key) for key in ('01', '10', '11')]
    land00 = split_arrive(flight00, "scatter", outs[0], "rs_scatter_00_arrive")
    sums['00'] = chip_sum(land00, t00, "00")
    theirs['00'] = run_exchange(trade('00'), "rs_trade_00")
    blocks = [both('00')] + rest
    for t, n in enumerate(ffn_names):
        gw[n] = jnp.concatenate([piece for blk in blocks for piece in blk[t]], axis=0).reshape(a[n].shape)
        update(n)
    for n in TRANSPOSED:
        for dst in (gw, delta, new_m, new_v):
            dst[n] = dst[n].swapaxes(-1, -2)

    return (loss, grad_x[None], *[gw[n] for n in WEIGHTS], *[delta[n] for n in WEIGHTS],
            *[new_m[n] for n in WEIGHTS], *[new_v[n] for n in WEIGHTS])
```

```python
import math

import jax
import jax.numpy as jnp
from jax import lax
from jax.experimental import pallas as pl
from jax.experimental.pallas import tpu as pltpu

F32 = jnp.float32
BF16 = jnp.bfloat16

D_MODEL = 1024
D_INNER = 2048
SSM_HEADS = 32
SSM_GROUPS = 4
HEADS_PER_GROUP = SSM_HEADS // SSM_GROUPS
SSM_HEAD_DIM = 64
SSM_STATE = 128
GROUP_DIM = D_INNER // SSM_GROUPS
CONV_DIM = D_INNER + 2 * SSM_GROUPS * SSM_STATE
CONV_WIDTH = 4
CHUNK = 128
ATT_HEAD_DIM = 64
N_Q_HEADS = 16
N_KV_HEADS = 4
Q_PER_KV = N_Q_HEADS // N_KV_HEADS
KV_DIM = N_KV_HEADS * ATT_HEAD_DIM
WINDOW = 128
ROPE_THETA = 10000.0
D_FF = 2816
N_CHIPS = 4
N_CORES = 2
FF_SHARD = D_FF // N_CHIPS
FF_PART = FF_SHARD // N_CORES
IN_PROJ_DIM = D_INNER + CONV_DIM + SSM_HEADS
IN_SHARD = IN_PROJ_DIM // N_CHIPS
IN_SHARD_PAD = 1312
EPS = 1e-5
NEG = -1e30
LANES = 128
VMEM_LIMIT = 56 * 1024 * 1024

ADAM_LR = 0.001
ADAM_B1 = 0.9
ADAM_B2 = 0.999
ADAM_EPS = 1e-08
ADAM_WD = 0.01
ADAM_STEP = 10

NN = ((1,), (0,))
NT = ((1,), (1,))
TN = ((0,), (0,))
MESH = pl.DeviceIdType.MESH
ANY = pl.BlockSpec(memory_space=pl.ANY)


def _dot(a, b, dims=NN, precision=None):
    return lax.dot_general(a, b, (dims, ((), ())), preferred_element_type=F32, precision=precision)


def _cp(n_grid):
    return pltpu.CompilerParams(dimension_semantics=("arbitrary",) * n_grid, vmem_limit_bytes=VMEM_LIMIT)


def _sigmoid(x):
    return 1.0 / (1.0 + jnp.exp(-x))


def _rms_fwd(xf, w):
    r = lax.rsqrt(jnp.mean(xf * xf, axis=-1, keepdims=True) + EPS)
    return xf * r * w


def _rms_bwd(dh, xf, w):
    r = lax.rsqrt(jnp.mean(xf * xf, axis=-1, keepdims=True) + EPS)
    xhat = xf * r
    dxhat = dh * w
    dx = r * (dxhat - xhat * jnp.mean(dxhat * xhat, axis=-1, keepdims=True))
    return dx, dh * xhat


def _row_tile(s, pref):
    return pref if s % pref == 0 else s


def _col_tile(n):
    for t in (1024, 768, 512, 256, 128):
        if n % t == 0:
            return t
    return n


def _sds(shape, dtype):
    return jax.ShapeDtypeStruct(tuple(shape), dtype)


class Exchange:
    def __init__(self, ins, out_shapes, sems, start, finish, inplace=False):
        self.ins, self.out_shapes, self.sems, self.start, self.finish = ins, out_shapes, sems, start, finish
        self.inplace = inplace


def _place():
    x, y, c = lax.axis_index("x"), lax.axis_index("y"), lax.axis_index("c")
    others = [(1 - x, y), (x, 1 - y), (1 - x, 1 - y)]
    return x, y, c, 2 * x + y, others


def _rc(src, dst, send_sem, recv_sem, dev):
    return pltpu.make_async_remote_copy(src_ref=src, dst_ref=dst, send_sem=send_sem, recv_sem=recv_sem,
                                        device_id=dev, device_id_type=MESH)


def gather_chips(arrs):
    n = len(arrs)

    def copies(ins, outs, sems):
        send, recv = sems
        x, y, c, k, others = _place()
        ici, land, fwd, fland = [], [], [], []
        for a in range(n):
            for j, (px, py) in enumerate(others):
                ici.append(_rc(ins[a].at[c], outs[a].at[c, k], send.at[a, j], recv.at[a, j], (px, py, c)))
                blk = outs[a].at[c, 2 * px + py]
                land.append(_rc(blk, blk, send.at[a, j], recv.at[a, j], (px, py, c)))
                fwd.append(_rc(blk, blk, send.at[a, 3 + j], recv.at[a, 3 + j], (x, y, 1 - c)))
                blk2 = outs[a].at[1 - c, 2 * px + py]
                fland.append(_rc(blk2, blk2, send.at[a, 3 + j], recv.at[a, 3 + j], (x, y, 1 - c)))
        return ici, land, fwd, fland

    def start(ins, outs, sems):
        for cp in copies(ins, outs, sems)[0]:
            cp.start()

    def finish(ins, outs, sems):
        ici, land, fwd, fland = copies(ins, outs, sems)
        for arrived, onward in zip(land, fwd):
            arrived.wait_recv()
            onward.start()
        for arrived in fland:
            arrived.wait_recv()
        for cp in ici + fwd:
            cp.wait_send()

    return Exchange(list(arrs), [_sds((2, N_CHIPS) + a.shape[1:], a.dtype) for a in arrs],
                    [pltpu.SemaphoreType.DMA((n, 6)), pltpu.SemaphoreType.DMA((n, 6))], start, finish)


def scatter_chips(arrs):
    n = len(arrs)

    def copies(ins, outs, sems):
        send, recv = sems
        x, y, c, k, others = _place()
        out, land = [], []
        for a in range(n):
            for j, (px, py) in enumerate(others):
                out.append(_rc(ins[a].at[2 * px + py], outs[a].at[k], send.at[a, j], recv.at[a, j], (px, py, c)))
                blk = outs[a].at[2 * px + py]
                land.append(_rc(blk, blk, send.at[a, j], recv.at[a, j], (px, py, c)))
        return out, land

    def start(ins, outs, sems):
        for cp in copies(ins, outs, sems)[0]:
            cp.start()

    def finish(ins, outs, sems):
        out, land = copies(ins, outs, sems)
        for arrived in land:
            arrived.wait_recv()
        for cp in out:
            cp.wait_send()

    return Exchange(list(arrs), [_sds(a.shape, a.dtype) for a in arrs],
                    [pltpu.SemaphoreType.DMA((n, 3)), pltpu.SemaphoreType.DMA((n, 3))], start, finish)


def swap_cores(arrs, pick_other):
    n = len(arrs)

    def copies(ins, outs, sems):
        send, recv = sems
        x, y, c, _, _ = _place()
        return [_rc(ins[a].at[1 - c] if pick_other else ins[a], outs[a], send.at[a], recv.at[a], (x, y, 1 - c))
                for a in range(n)]

    def start(ins, outs, sems):
        for cp in copies(ins, outs, sems):
            cp.start()

    def finish(ins, outs, sems):
        for cp in copies(ins, outs, sems):
            cp.wait()

    shapes = [_sds(a.shape[1:] if pick_other else a.shape, a.dtype) for a in arrs]
    return Exchange(list(arrs), shapes, [pltpu.SemaphoreType.DMA((n,)), pltpu.SemaphoreType.DMA((n,))],
                    start, finish)


def join(*parts):
    parts = [p for p in parts if p is not None]
    if not parts:
        return None

    def split(refs, counts):
        out, pos = [], 0
        for cnt in counts:
            out.append(refs[pos:pos + cnt])
            pos += cnt
        return out

    n_in = [len(p.ins) for p in parts]
    n_out = [len(p.out_shapes) for p in parts]
    n_sem = [len(p.sems) for p in parts]

    def run(which):
        def go(ins, outs, sems):
            for p, i, o, s in zip(parts, split(ins, n_in), split(outs, n_out), split(sems, n_sem)):
                getattr(p, which)(i, o, s)
        return go

    return Exchange([a for p in parts for a in p.ins], [s for p in parts for s in p.out_shapes],
                    [s for p in parts for s in p.sems], run("start"), run("finish"))


def _pcall(body, *, out_shape, grid, in_specs, out_specs, args, name, scratch_shapes=(), ride=None, aliases=None):
    out_shape, out_specs, in_specs = tuple(out_shape), tuple(out_specs), list(in_specs)
    aliases = aliases or {}
    if ride is None:
        return pl.pallas_call(body, out_shape=out_shape, grid=grid, in_specs=in_specs, out_specs=out_specs,
                              scratch_shapes=list(scratch_shapes), input_output_aliases=aliases, name=name,
                              compiler_params=_cp(len(grid)))(*args)
    n_in, n_out, n_sc = len(args), len(out_shape), len(scratch_shapes)
    n_xi, n_xo = len(ride.ins), len(ride.out_shapes)

    def wrapped(*refs):
        pos = [0]

        def take(cnt):
            got = refs[pos[0]:pos[0] + cnt]
            pos[0] += cnt
            return got

        c_in, x_in, c_out, x_out, c_sc = take(n_in), take(n_xi), take(n_out), take(n_xo), take(n_sc)
        sems = refs[pos[0]:]
        first, last = True, True
        for d, size in enumerate(grid):
            first = jnp.logical_and(first, pl.program_id(d) == 0)
            last = jnp.logical_and(last, pl.program_id(d) == size - 1)

        @pl.when(first)
        def _():
            ride.start(x_in, x_out, sems)

        body(*c_in, *c_out, *c_sc)

        @pl.when(last)
        def _():
            ride.finish(x_in, x_out, sems)

    if ride.inplace:
        aliases = {**aliases, **{n_in + t: n_out + t for t in range(n_xi)}}
    res = pl.pallas_call(
        wrapped, out_shape=out_shape + tuple(ride.out_shapes), grid=grid,
        in_specs=in_specs + [ANY] * n_xi, out_specs=out_specs + (ANY,) * n_xo,
        scratch_shapes=list(scratch_shapes) + list(ride.sems), input_output_aliases=aliases, name=name,
        compiler_params=_cp(len(grid)))(*args, *ride.ins)
    return res[:n_out], res[n_out:]


def run_exchange(ex, name):
    n_xi, n_xo = len(ex.ins), len(ex.out_shapes)

    def body(*refs):
        ins, outs, sems = refs[:n_xi], refs[n_xi:n_xi + n_xo], refs[n_xi + n_xo:]
        ex.start(ins, outs, sems)
        ex.finish(ins, outs, sems)

    aliases = {t: t for t in range(n_xi)} if ex.inplace else {}
    return pl.pallas_call(body, out_shape=tuple(ex.out_shapes), in_specs=[ANY] * n_xi, out_specs=(ANY,) * n_xo,
                          scratch_shapes=list(ex.sems), input_output_aliases=aliases, name=name)(*ex.ins)


HBM_SPEC = pl.BlockSpec(memory_space=pltpu.HBM)
SEM_SPEC = pl.BlockSpec(memory_space=pltpu.SEMAPHORE)
EFFECT = pltpu.SideEffectType.DATAFLOW_SIDE_EFFECTING


def _route(kind, src, dst, c, k, peer):
    if kind == "gather":
        return src.at[c], dst.at[c, k], dst.at[c, peer]
    return src.at[peer], dst.at[k], dst.at[peer]


def split_start(batches, kind, name):
    flat = [a for batch in batches for a in batch]
    n, nb = len(flat), len(batches)
    lands = [lax.empty((2, N_CHIPS) + a.shape[1:] if kind == "gather" else a.shape, a.dtype) for a in flat]

    def body(*refs):
        srcs, dsts, sems, token = refs[:n], refs[n:2 * n], refs[2 * n:2 * n + 2 * nb], refs[-1]
        x, y, c, k, others = _place()
        pos = 0
        for b, batch in enumerate(batches):
            for a in range(len(batch)):
                for j, (px, py) in enumerate(others):
                    src, dst, _ = _route(kind, srcs[pos], dsts[pos], c, k, 2 * px + py)
                    _rc(src, dst, sems[2 * b].at[3 * a + j], sems[2 * b + 1].at[3 * a + j], (px, py, c)).start()
                pos += 1
        token[...] = jnp.zeros(token.shape, token.dtype)

    sem_shapes = [pltpu.SemaphoreType.DMA((3 * len(batch),)) for batch in batches for _ in range(2)]
    thru = [pltpu.HBM(a.shape, a.dtype) for a in flat] + [pltpu.HBM(l.shape, l.dtype) for l in lands]
    res = pl.pallas_call(
        body, name=name, out_shape=tuple(sem_shapes + thru + [_sds((8, LANES), F32)]),
        in_specs=[HBM_SPEC] * (2 * n),
        out_specs=tuple([SEM_SPEC] * (2 * nb) + [HBM_SPEC] * (2 * n) + [pl.BlockSpec(memory_space=pltpu.VMEM)]),
        input_output_aliases={t: 2 * nb + t for t in range(2 * n)},
        compiler_params=pltpu.CompilerParams(has_side_effects=EFFECT),
    )(*[pltpu.with_memory_space_constraint(t, pltpu.HBM) for t in flat + lands])
    sems, srcs, dsts = res[:2 * nb], res[2 * nb:2 * nb + n], res[2 * nb + n:2 * nb + 2 * n]
    out, pos = [], 0
    for b, batch in enumerate(batches):
        out.append((sems[2 * b], sems[2 * b + 1], list(srcs[pos:pos + len(batch)]), list(dsts[pos:pos + len(batch)])))
        pos += len(batch)
    return out, res[-1]


def split_arrive(handle, kind, after, name):
    send, recv, srcs, dsts = handle
    n = len(srcs)

    def body(*refs):
        s_refs, d_refs, send_ref, recv_ref = refs[:n], refs[n:2 * n], refs[2 * n], refs[2 * n + 1]
        x, y, c, k, others = _place()
        for a in range(n):
            for j, (px, py) in enumerate(others):
                src, _, landed = _route(kind, s_refs[a], d_refs[a], c, k, 2 * px + py)
                cp = _rc(src, landed, send_ref.at[3 * a + j], recv_ref.at[3 * a + j], (px, py, c))
                cp.wait_send()
                cp.wait_recv()

    res = pl.pallas_call(
        body, name=name, out_shape=tuple([pltpu.HBM(t.shape, t.dtype) for t in srcs + dsts]),
        in_specs=[HBM_SPEC] * (2 * n) + [SEM_SPEC, SEM_SPEC, ANY], out_specs=tuple([HBM_SPEC] * (2 * n)),
        input_output_aliases={t: t for t in range(2 * n)},
        compiler_params=pltpu.CompilerParams(has_side_effects=EFFECT),
    )(*srcs, *dsts, send, recv, after)
    return list(res[n:])


def forward_cores(bufs):
    n = len(bufs)

    def copies(outs, sems):
        send, recv = sems
        x, y, c, k, others = _place()
        onward, land = [], []
        for a in range(n):
            for j, (px, py) in enumerate(others):
                blk = outs[a].at[c, 2 * px + py]
                onward.append(_rc(blk, blk, send.at[a, j], recv.at[a, j], (x, y, 1 - c)))
                blk2 = outs[a].at[1 - c, 2 * px + py]
                land.append(_rc(blk2, blk2, send.at[a, j], recv.at[a, j], (x, y, 1 - c)))
        return onward, land

    def start(ins, outs, sems):
        for cp in copies(outs, sems)[0]:
            cp.start()

    def finish(ins, outs, sems):
        onward, land = copies(outs, sems)
        for arrived in land:
            arrived.wait_recv()
        for cp in onward:
            cp.wait_send()

    return Exchange(list(bufs), [_sds(b.shape, b.dtype) for b in bufs],
                    [pltpu.SemaphoreType.DMA((n, 3)), pltpu.SemaphoreType.DMA((n, 3))], start, finish, inplace=True)


def all_reduce_small(buf, name):
    r = buf.shape[0]
    n_dev = 8

    def body(in_ref, o_ref, land, send_sems, recv_sems):
        x, y, c, _, _ = _place()
        me = 4 * x + 2 * y + c
        land[me] = in_ref[...]
        sends = []
        for d in range(1, n_dev):
            peer = (x ^ (d >> 2), y ^ ((d >> 1) & 1), c ^ (d & 1))
            cp = _rc(in_ref, land.at[me], send_sems.at[d], recv_sems.at[d], peer)
            cp.start()
            sends.append(cp)
        for d in range(1, n_dev):
            blk = land.at[me ^ d]
            _rc(blk, blk, send_sems.at[d], recv_sems.at[d], (x, y, c)).wait_recv()
        for cp in sends:
            cp.wait_send()
        tot = land[0]
        for d in range(1, n_dev):
            tot = tot + land[d]
        o_ref[...] = tot

    vm = pl.BlockSpec(memory_space=pltpu.VMEM)
    return pl.pallas_call(
        body, out_shape=_sds(buf.shape, F32), in_specs=[vm], out_specs=vm,
        scratch_shapes=[pltpu.VMEM((n_dev, r, LANES), F32), pltpu.SemaphoreType.DMA((n_dev,)),
                        pltpu.SemaphoreType.DMA((n_dev,))],
        name=name)(buf)


def rmsnorm_fwd(x, w, name):
    s, d = x.shape
    tm = _row_tile(s, 512)

    def body(x_ref, w_ref, o_ref):
        o_ref[...] = _rms_fwd(x_ref[...], w_ref[...]).astype(BF16)

    return _pcall(body, out_shape=[_sds((s, d), BF16)], grid=(s // tm,),
                  in_specs=[pl.BlockSpec((tm, d), lambda i: (i, 0)), pl.BlockSpec((1, d), lambda i: (0, 0))],
                  out_specs=[pl.BlockSpec((tm, d), lambda i: (i, 0))], args=[x, w.reshape(1, d)], name=name)[0]


def _ffn_w_spec(chip_of, single=False):
    mode = dict(pipeline_mode=pl.Buffered(1)) if single else {}
    return pl.BlockSpec((N_CORES, 1, FF_PART, D_MODEL), lambda *ids: (0, chip_of(*ids), 0, 0), **mode)


def ffn_fwd(h, x, wg, wu, wd, norm_ws, name, ride=None):
    s, d = h.shape
    n_norm = len(norm_ws)
    tm = _row_tile(s, 1024)

    def body(*refs):
        h_ref, x_ref, wg_ref, wu_ref, wd_ref = refs[:5]
        nw_refs = refs[5:5 + n_norm]
        o_ref = refs[5 + n_norm]
        h_refs = refs[6 + n_norm:6 + 2 * n_norm]
        gu_ref, acc = refs[6 + 2 * n_norm], refs[7 + 2 * n_norm]
        k = pl.program_id(1)

        @pl.when(k == 0)
        def _():
            acc[...] = jnp.zeros(acc.shape, F32)

        hm = tm // 2
        for part in range(2):
            sub = pl.ds(part * hm, hm)
            hb = h_ref[sub, :]
            g = _dot(hb, wg_ref[...].reshape(FF_SHARD, d), NT)
            u = _dot(hb, wu_ref[...].reshape(FF_SHARD, d), NT)
            gu_ref[0, 0, sub, :] = g.astype(BF16)
            gu_ref[0, 1, sub, :] = u.astype(BF16)
            acc[sub, :] += _dot((g * _sigmoid(g) * u).astype(BF16), wd_ref[...].reshape(FF_SHARD, d))

        @pl.when(k == N_CHIPS - 1)
        def _():
            xn = x_ref[...] + 0.5 * acc[...]
            o_ref[...] = xn
            for nw_ref, hn_ref in zip(nw_refs, h_refs):
                hn_ref[...] = _rms_fwd(xn, nw_ref[...]).astype(BF16)

    row = pl.BlockSpec((tm, d), lambda i, k: (i, 0))
    vec = pl.BlockSpec((1, d), lambda i, k: (0, 0))
    wsp = _ffn_w_spec(lambda i, k: k)
    return _pcall(
        body, out_shape=[_sds((s, d), F32)] + [_sds((s, d), BF16)] * n_norm + [_sds((N_CHIPS, 2, s, FF_SHARD), BF16)],
        grid=(s // tm, N_CHIPS),
        in_specs=[row, row, wsp, wsp, wsp] + [vec] * n_norm,
        out_specs=[row] * (1 + n_norm) + [pl.BlockSpec((1, 2, tm, FF_SHARD), lambda i, k: (k, 0, i, 0))],
        scratch_shapes=[pltpu.VMEM((tm, d), F32)],
        args=[h, x, wg, wu, wd] + [nw.reshape(1, d) for nw in norm_ws], name=name, ride=ride)


def ffn_bwd(dxn, h, x_in, nw, gu, wg, wu, wd, name, ride=None):
    s, d = h.shape
    tm = _row_tile(s, 512)
    ni = s // tm
    last_e = N_CHIPS - 1

    def body(dxn_ref, h_ref, x_ref, nw_ref, gu_ref, wg_ref, wu_ref, wd_ref,
             dx_ref, dnw_ref, dwg_ref, dwu_ref, dwd_ref, dh, wacc):
        e = pl.program_id(0)
        i = pl.program_id(1)
        rows = pl.ds(pl.multiple_of(i * tm, tm), tm)

        @pl.when(i == 0)
        def _():
            wacc[...] = jnp.zeros(wacc.shape, F32)

        @pl.when(e == 0)
        def _():
            dh[rows, :] = jnp.zeros((tm, d), F32)

        hm = tm // 2
        for part in range(2):
            sub = pl.ds(part * hm, hm)
            dxb = dxn_ref[sub, :].astype(BF16)
            hb = h_ref[sub, :]
            g = gu_ref[0, 0, sub, :].astype(F32)
            u = gu_ref[0, 1, sub, :].astype(F32)
            drows = pl.ds(pl.multiple_of(i * tm + part * hm, hm), hm)
            sg = _sigmoid(g)
            silu = g * sg
            wacc[2] += _dot((0.5 * silu * u).astype(BF16), dxb, TN)
            da = 0.5 * _dot(dxb, wd_ref[...].reshape(FF_SHARD, d), NT)
            dg = (da * u * (sg * (1.0 + g * (1.0 - sg)))).astype(BF16)
            wacc[0] += _dot(dg, hb, TN)
            du = (da * silu).astype(BF16)
            dh[drows, :] += _dot(dg, wg_ref[...].reshape(FF_SHARD, d))
            wacc[1] += _dot(du, hb, TN)
            dh[drows, :] += _dot(du, wu_ref[...].reshape(FF_SHARD, d))

        @pl.when(i == ni - 1)
        def _():
            for t, dw_ref in enumerate((dwg_ref, dwu_ref, dwd_ref)):
                dw_ref[...] = wacc[t].astype(BF16).reshape(N_CORES, 1, FF_PART, d)

        @pl.when(e == last_e)
        def _():
            dx, dnw = _rms_bwd(dh[rows, :], x_ref[...], nw_ref[...])
            dx_ref[...] = dxn_ref[...] + dx
            col = jnp.sum(dnw, axis=0, keepdims=True)

            @pl.when(i == 0)
            def _():
                dnw_ref[...] = col

            @pl.when(i > 0)
            def _():
                dnw_ref[...] += col

    row = pl.BlockSpec((tm, d), lambda e, i: (i, 0))
    late = pl.BlockSpec((tm, d), lambda e, i: (jnp.where(e == last_e, i, 0), 0))
    vec = pl.BlockSpec((1, d), lambda e, i: (0, 0))
    wsp = _ffn_w_spec(lambda e, i: e, single=True)
    dwsp = _ffn_w_spec(lambda e, i: e, single=True)
    dw = _sds((N_CORES, N_CHIPS, FF_PART, d), BF16)
    return _pcall(
        body, out_shape=[_sds((s, d), F32), _sds((1, d), F32), dw, dw, dw],
        grid=(N_CHIPS, ni),
        in_specs=[row, row, late, vec, pl.BlockSpec((1, 2, tm, FF_SHARD), lambda e, i: (e, 0, i, 0)), wsp, wsp, wsp],
        out_specs=[late, vec, dwsp, dwsp, dwsp],
        scratch_shapes=[pltpu.VMEM((s, d), F32), pltpu.VMEM((3, FF_SHARD, d), F32)],
        args=[dxn, h, x_in, nw.reshape(1, d), gu, wg, wu, wd], name=name, ride=ride)


def mm_res(a, w, x, name, bias=None, norm_ws=(), ride=None):
    s, k = a.shape
    n = w.shape[1]
    tm = _row_tile(s, 256)
    has_bias = bias is not None
    n_norm = len(norm_ws)

    def body(*refs):
        a_ref, w_ref, x_ref = refs[:3]
        pos = 3
        t = _dot(a_ref[...], w_ref[...])
        if has_bias:
            t = t + refs[pos][...]
            pos += 1
        nw_refs = refs[pos:pos + n_norm]
        o_ref = refs[pos + n_norm]
        h_refs = refs[pos + n_norm + 1:]
        xn = x_ref[...] + t
        o_ref[...] = xn
        for nw_ref, h_ref in zip(nw_refs, h_refs):
            h_ref[...] = _rms_fwd(xn, nw_ref[...]).astype(BF16)

    row = pl.BlockSpec((tm, n), lambda i: (i, 0))
    vec = pl.BlockSpec((1, n), lambda i: (0, 0))
    in_specs = [pl.BlockSpec((tm, k), lambda i: (i, 0)), pl.BlockSpec((k, n), lambda i: (0, 0)), row]
    args = [a, w, x]
    if has_bias:
        in_specs.append(vec)
        args.append(bias.reshape(1, n))
    for nw in norm_ws:
        in_specs.append(vec)
        args.append(nw.reshape(1, n))
    return _pcall(body, out_shape=[_sds((s, n), F32)] + [_sds((s, n), BF16)] * n_norm, grid=(s // tm,),
                  in_specs=in_specs, out_specs=[row] * (1 + n_norm), args=args, name=name, ride=ride)


def mm_nn(a, w, name, bias=None, out_dtype=F32):
    s, k = a.shape
    n = w.shape[1]
    tm = _row_tile(s, 512)
    tn = _col_tile(n)
    has_bias = bias is not None

    def body(*refs):
        a_ref, w_ref = refs[:2]
        o_ref = refs[-1]
        t = _dot(a_ref[...], w_ref[...])
        if has_bias:
            t = t + refs[2][...]
        o_ref[...] = t.astype(out_dtype)

    in_specs = [pl.BlockSpec((tm, k), lambda j, i: (i, 0)), pl.BlockSpec((k, tn), lambda j, i: (0, j))]
    args = [a, w]
    if has_bias:
        in_specs.append(pl.BlockSpec((1, tn), lambda j, i: (0, j)))
        args.append(bias.reshape(1, n))
    return _pcall(body, out_shape=[_sds((s, n), out_dtype)], grid=(n // tn, s // tm), in_specs=in_specs,
                  out_specs=[pl.BlockSpec((tm, tn), lambda j, i: (i, j))], args=args, name=name)[0]


def mm_nt(a, w, name, n=None, row0=0, out_dtype=F32, ride=None):
    s, k = a.shape
    n = w.shape[0] if n is None else n
    tm = _row_tile(s, 512)
    tn = _col_tile(n)
    base = row0 // tn
    assert row0 % tn == 0

    def body(a_ref, w_ref, o_ref):
        o_ref[...] = _dot(a_ref[...].astype(BF16), w_ref[...], NT).astype(out_dtype)

    res = _pcall(body, out_shape=[_sds((s, n), out_dtype)], grid=(n // tn, s // tm),
                 in_specs=[pl.BlockSpec((tm, k), lambda j, i: (i, 0)), pl.BlockSpec((tn, k), lambda j, i: (base + j, 0))],
                 out_specs=[pl.BlockSpec((tm, tn), lambda j, i: (i, j))], args=[a, w], name=name, ride=ride)
    return res[0] if ride is None else (res[0][0], res[1])


def mm_tn(a, b, name, into=None, rows=None, row0=0, m_valid=None, col_sum=False, ride=None):
    s, m = a.shape
    n = b.shape[1]
    mv = m if m_valid is None else m_valid
    tm = _col_tile(m) if m_valid is None else mv
    tn = n if n <= 1024 else _col_tile(n)
    rows = mv if rows is None else rows
    assert row0 % tm == 0 and (m_valid is None or m == LANES)
    assert not col_sum or mv == tm
    base = row0 // tm
    ta = m if m_valid is not None else tm

    def body(*refs):
        a_ref, b_ref = refs[0], refs[1]
        o_ref = refs[-2] if col_sum else refs[-1]
        bf = b_ref[...]
        t = _dot(a_ref[...].astype(BF16), bf.astype(BF16), TN)
        o_ref[...] = t[:tm].astype(BF16)
        if col_sum:
            refs[-1][...] = jnp.sum(bf.astype(F32), axis=0, keepdims=True)

    in_specs = [pl.BlockSpec((s, ta), lambda i, j: (0, i)), pl.BlockSpec((s, tn), lambda i, j: (0, j))]
    args = [a, b]
    aliases = None
    if into is not None:
        in_specs.append(ANY)
        args.append(into)
        aliases = {2: 0}
    out_shape = [_sds((rows, n), BF16)]
    out_specs = [pl.BlockSpec((tm, tn), lambda i, j: (base + i, j))]
    if col_sum:
        out_shape.append(_sds((1, n), F32))
        out_specs.append(pl.BlockSpec((1, tn), lambda i, j: (0, j)))
    res = _pcall(body, out_shape=out_shape, grid=(mv // tm, n // tn), in_specs=in_specs, out_specs=out_specs,
                 args=args, name=name, ride=ride, aliases=aliases)
    outs = res if ride is None else res[0]
    out = (outs[0], outs[1][0]) if col_sum else outs[0]
    return out if ride is None else (out, res[1])


def mm_rms_bwd(terms, dxn, x, nw, name, ride=None):
    s, n = x.shape
    nt_ = len(terms)
    tm = _row_tile(s, 256)
    forms = [t[5] for t in terms]

    def body(*refs):
        dxn_ref, x_ref, nw_ref, dx_ref, dnw_ref = refs[2 * nt_:]
        i = pl.program_id(0)
        dh = None
        for t in range(nt_):
            part = _dot(refs[2 * t][...].astype(BF16), refs[2 * t + 1][...], NN if forms[t] == "nn" else NT)
            dh = part if dh is None else dh + part
        dx, dnw = _rms_bwd(dh, x_ref[...], nw_ref[...])
        dx_ref[...] = dxn_ref[...] + dx
        col = jnp.sum(dnw, axis=0, keepdims=True)

        @pl.when(i == 0)
        def _():
            dnw_ref[...] = col

        @pl.when(i > 0)
        def _():
            dnw_ref[...] += col

    in_specs, args = [], []
    for a, cb, w, rb, kb, form in terms:
        in_specs.append(pl.BlockSpec((tm, kb), lambda i, cb=cb: (i, cb)))
        if form == "nn":
            in_specs.append(pl.BlockSpec((kb, n), lambda i, rb=rb: (rb, 0)))
        else:
            in_specs.append(pl.BlockSpec((n, kb), lambda i, rb=rb: (0, rb)))
        args += [a, w]
    row = pl.BlockSpec((tm, n), lambda i: (i, 0))
    vec = pl.BlockSpec((1, n), lambda i: (0, 0))
    res = _pcall(body, out_shape=[_sds((s, n), F32), _sds((1, n), F32)], grid=(s // tm,),
                 in_specs=in_specs + [row, row, vec], out_specs=[row, vec],
                 args=args + [dxn, x, nw.reshape(1, n)], name=name, ride=ride)
    outs = res if ride is None else res[0]
    out = (outs[0], outs[1][0])
    return out if ride is None else (out, res[1])


def rope_tables(s):
    pos = jnp.arange(s, dtype=F32)
    inv = 1.0 / (ROPE_THETA ** (jnp.arange(0, ATT_HEAD_DIM, 2, dtype=F32) / ATT_HEAD_DIM))
    ang = pos[:, None] * inv[None, :]
    cos = jnp.tile(jnp.cos(ang), (1, 2 * LANES // ATT_HEAD_DIM))
    sin = jnp.tile(jnp.sin(ang), (1, 2 * LANES // ATT_HEAD_DIM))
    return cos, sin


def rope_apply(t, cos, sin, name, inverse=False, scale=1.0, out_dtype=BF16):
    s, n = t.shape
    tm = _row_tile(s, 512)
    half = ATT_HEAD_DIM // 2
    reps = n // LANES

    def body(t_ref, c_ref, s_ref, o_ref):
        tf = t_ref[...].astype(F32)
        c = jnp.tile(c_ref[...], (1, reps))
        sn = jnp.tile(s_ref[...], (1, reps))
        lane = lax.broadcasted_iota(jnp.int32, tf.shape, 1)
        first = (lane & (ATT_HEAD_DIM - 1)) < half
        rot = jnp.where(first, -pltpu.roll(tf, n - half, 1), pltpu.roll(tf, half, 1))
        sign = -1.0 if inverse else 1.0
        o_ref[...] = (scale * (tf * c + sign * rot * sn)).astype(out_dtype)

    tab = pl.BlockSpec((tm, LANES), lambda i: (i, 0))
    return _pcall(body, out_shape=[_sds((s, n), out_dtype)], grid=(s // tm,),
                  in_specs=[pl.BlockSpec((tm, n), lambda i: (i, 0)), tab, tab],
                  out_specs=[pl.BlockSpec((tm, n), lambda i: (i, 0))], args=[t, cos, sin], name=name)[0]


CONV_TILE = 256


def _shift_down(u, k):
    if k == 0:
        return u
    row = lax.broadcasted_iota(jnp.int32, u.shape, 0)
    return jnp.where(row >= k, pltpu.roll(u, k, 0), 0.0)


def _shift_up(u, k):
    if k == 0:
        return u
    s = u.shape[0]
    row = lax.broadcasted_iota(jnp.int32, u.shape, 0)
    return jnp.where(row < s - k, pltpu.roll(u, s - k, 0), 0.0)


def _conv_pre(u, w_ref, b_ref):
    pre = b_ref[...] + w_ref[CONV_WIDTH - 1:CONV_WIDTH, :] * u
    for k in range(CONV_WIDTH - 1):
        pre += w_ref[k:k + 1, :] * _shift_down(u, CONV_WIDTH - 1 - k)
    return pre


def conv_fwd(u, w, b, name, ride=None):
    s, c = u.shape

    def body(u_ref, w_ref, b_ref, o_ref):
        pre = _conv_pre(u_ref[...], w_ref, b_ref)
        o_ref[...] = pre * _sigmoid(pre)

    col = pl.BlockSpec((s, CONV_TILE), lambda j: (0, j))
    res = _pcall(body, out_shape=[_sds((s, c), F32)], grid=(c // CONV_TILE,),
                 in_specs=[col, pl.BlockSpec((CONV_WIDTH, CONV_TILE), lambda j: (0, j)),
                           pl.BlockSpec((1, CONV_TILE), lambda j: (0, j))],
                 out_specs=[col], args=[u, w, b.reshape(1, c)], name=name, ride=ride)
    return res[0] if ride is None else (res[0][0], res[1])


def conv_bwd(dxs, db_, dc_, u, w, b, name):
    s, c = u.shape
    n_x = dxs.shape[1] // CONV_TILE
    n_b = db_.shape[1] // CONV_TILE

    def body(dx_ref, dbb_ref, dcc_ref, u_ref, w_ref, b_ref, du_ref, dw_ref, dbias_ref):
        j = pl.program_id(0)
        dact = jnp.where(j < n_x, dx_ref[...], jnp.where(j < n_x + n_b, dbb_ref[...], dcc_ref[...]))
        uf = u_ref[...]
        pre = _conv_pre(uf, w_ref, b_ref)
        sg = _sigmoid(pre)
        dpre = dact * (sg * (1.0 + pre * (1.0 - sg)))
        du = w_ref[CONV_WIDTH - 1:CONV_WIDTH, :] * dpre
        for k in range(CONV_WIDTH - 1):
            du += w_ref[k:k + 1, :] * _shift_up(dpre, CONV_WIDTH - 1 - k)
        du_ref[...] = du
        dbias_ref[...] = jnp.sum(dpre, axis=0, keepdims=True)
        for k in range(CONV_WIDTH):
            dw_ref[k:k + 1, :] = jnp.sum(dpre * _shift_down(uf, CONV_WIDTH - 1 - k), axis=0, keepdims=True)

    col = pl.BlockSpec((s, CONV_TILE), lambda j: (0, j))
    wsp = pl.BlockSpec((CONV_WIDTH, CONV_TILE), lambda j: (0, j))
    bsp = pl.BlockSpec((1, CONV_TILE), lambda j: (0, j))
    du, dw, db = _pcall(
        body, out_shape=[_sds((s, c), F32), _sds((CONV_WIDTH, c), F32), _sds((1, c), F32)], grid=(c // CONV_TILE,),
        in_specs=[pl.BlockSpec((s, CONV_TILE), lambda j: (0, jnp.minimum(j, n_x - 1))),
                  pl.BlockSpec((s, CONV_TILE), lambda j: (0, jnp.clip(j - n_x, 0, n_b - 1))),
                  pl.BlockSpec((s, CONV_TILE), lambda j: (0, jnp.clip(j - n_x - n_b, 0, n_b - 1))),
                  col, wsp, bsp],
        out_specs=[col, wsp, bsp], args=[dxs, db_, dc_, u, w, b.reshape(1, c)], name=name)
    return du, dw, db[0]


def _lane_pick(mat, idx):
    lane = lax.broadcasted_iota(jnp.int32, mat.shape, 1)
    return jnp.sum(jnp.where(lane == idx, mat, 0.0), axis=1, keepdims=True)


def _sub_pick(mat, idx):
    sub = lax.broadcasted_iota(jnp.int32, mat.shape, 0)
    return jnp.sum(jnp.where(sub == idx, mat, 0.0), axis=0, keepdims=True)


def _expand_heads(cols):
    rows = cols[0].shape[0]
    left = lax.broadcasted_iota(jnp.int32, (rows, LANES), 1) < SSM_HEAD_DIM
    return jnp.concatenate(
        [jnp.where(left, cols[2 * p], cols[2 * p + 1]) for p in range(HEADS_PER_GROUP // 2)], axis=1)


def _dot_01(x, ones, ones_first, pieces):
    tot, rest = None, x
    for _ in range(pieces):
        piece = rest.astype(BF16)
        rest = rest - piece.astype(F32)
        part = _dot(ones, piece) if ones_first else _dot(piece, ones)
        tot = part if tot is None else tot + part
    return tot


def _heads_to_lanes(mat, g):
    jj = lax.broadcasted_iota(jnp.int32, (GROUP_DIM, LANES), 0)
    ll = lax.broadcasted_iota(jnp.int32, (GROUP_DIM, LANES), 1)
    sel = (ll == HEADS_PER_GROUP * g + (jj >> 6)).astype(BF16)
    return _dot_01(mat, sel, False, 3)


def _softplus(x):
    return jnp.maximum(x, 0.0) + jnp.log1p(jnp.exp(-jnp.abs(x)))


def _ssd_scalars(dt_ref, bias_ref, a_ref, dtall, csall, cst):
    dta = _softplus(dt_ref[...] + bias_ref[...])
    row = lax.broadcasted_iota(jnp.int32, (CHUNK, CHUNK), 0)
    col = lax.broadcasted_iota(jnp.int32, (CHUNK, CHUNK), 1)
    cs = _dot_01(dta * a_ref[...], (row >= col).astype(BF16), True, 3)
    dtall[...] = dta
    csall[...] = cs
    cst[...] = cs.T


def _decay_mat(cs_col, cs_row):
    row = lax.broadcasted_iota(jnp.int32, (CHUNK, CHUNK), 0)
    col = lax.broadcasted_iota(jnp.int32, (CHUNK, CHUNK), 1)
    return jnp.exp(jnp.where(row >= col, cs_col - cs_row, NEG))


def _head_mask(xpair, right):
    lane = lax.broadcasted_iota(jnp.int32, xpair.shape, 1)
    keep = (lane >= SSM_HEAD_DIM) if right else (lane < SSM_HEAD_DIM)
    return jnp.where(keep, xpair, 0.0)


def _chunk_cols(x_all, g):
    return [_lane_pick(x_all, HEADS_PER_GROUP * g + r) for r in range(HEADS_PER_GROUP)]


def _decay_col(cs_cols):
    return jnp.concatenate(
        [jnp.broadcast_to(jnp.exp(cc[CHUNK - 1:CHUNK, :]), (SSM_HEAD_DIM, 1)) for cc in cs_cols], axis=0)


def ssd_fwd(act, z, dtp, bias_p, a_p, d_p, normw, name, ride=None):
    s = act.shape[0]
    nc = s // CHUNK
    b_off = D_INNER // SSM_STATE
    c_off = b_off + SSM_GROUPS

    def body(xs_ref, b_ref, c_ref, z_ref, dt_ref, bias_ref, a_ref, d_ref, nw_ref,
             yn_ref, y_ref, st_ref, state, dtall, csall, cst):
        c = pl.program_id(0)
        g = pl.program_id(1)

        @pl.when(g == 0)
        def _():
            _ssd_scalars(dt_ref, bias_ref, a_ref, dtall, csall, cst)

        @pl.when(c == 0)
        def _():
            state[g] = jnp.zeros((GROUP_DIM, SSM_STATE), F32)

        cs_cols = _chunk_cols(csall[...], g)
        dt_cols = _chunk_cols(dtall[...], g)
        cs_rows = [_sub_pick(cst[...], HEADS_PER_GROUP * g + r) for r in range(HEADS_PER_GROUP)]
        d_cols = _chunk_cols(d_ref[...], g)
        cs_exp = _expand_heads(cs_cols)
        dt_exp = _expand_heads(dt_cols)
        d_exp = _expand_heads(d_cols)
        xs = xs_ref[...]
        bb = b_ref[...].astype(BF16)
        cb16 = c_ref[...].astype(BF16)
        xdt = xs * dt_exp
        s_prev = state[g]
        st_ref[0, 0] = s_prev
        y_off = _dot(cb16, s_prev.astype(BF16), NT) * jnp.exp(cs_exp)
        decay_st = jnp.exp(cs_exp[CHUNK - 1:CHUNK, :] - cs_exp)
        contrib = _dot((xdt * decay_st).astype(BF16), bb, TN)
        state[g] = _decay_col(cs_cols) * s_prev + contrib
        cbm = _dot(cb16, bb, NT)
        pairs = []
        for p in range(HEADS_PER_GROUP // 2):
            xpair = xdt[:, LANES * p:LANES * (p + 1)]
            m0 = (cbm * _decay_mat(cs_cols[2 * p], cs_rows[2 * p])).astype(BF16)
            m1 = (cbm * _decay_mat(cs_cols[2 * p + 1], cs_rows[2 * p + 1])).astype(BF16)
            pairs.append(_dot(m0, _head_mask(xpair, False).astype(BF16))
                         + _dot(m1, _head_mask(xpair, True).astype(BF16)))
        y = jnp.concatenate(pairs, axis=1) + y_off + xs * d_exp
        y_ref[...] = y
        zf = z_ref[...]
        yg = y * (zf * _sigmoid(zf))
        yn_ref[...] = _rms_fwd(yg, nw_ref[...]).astype(BF16)

    grp = pl.BlockSpec((CHUNK, GROUP_DIM), lambda c, g: (c, g))
    par = pl.BlockSpec((1, LANES), lambda c, g: (0, 0))
    return _pcall(
        body,
        out_shape=[_sds((s, D_INNER), BF16), _sds((s, D_INNER), F32),
                   _sds((nc, SSM_GROUPS, GROUP_DIM, SSM_STATE), F32)],
        grid=(nc, SSM_GROUPS),
        in_specs=[grp,
                  pl.BlockSpec((CHUNK, SSM_STATE), lambda c, g: (c, b_off + g)),
                  pl.BlockSpec((CHUNK, SSM_STATE), lambda c, g: (c, c_off + g)),
                  grp,
                  pl.BlockSpec((CHUNK, LANES), lambda c, g: (c, 0)),
                  par, par, par,
                  pl.BlockSpec((1, GROUP_DIM), lambda c, g: (0, g))],
        out_specs=[grp, grp, pl.BlockSpec((1, 1, GROUP_DIM, SSM_STATE), lambda c, g: (c, g, 0, 0))],
        scratch_shapes=[pltpu.VMEM((SSM_GROUPS, GROUP_DIM, SSM_STATE), F32),
                        pltpu.VMEM((CHUNK, LANES), F32), pltpu.VMEM((CHUNK, LANES), F32),
                        pltpu.VMEM((LANES, CHUNK), F32)],
        args=[act, act, act, z, dtp, bias_p, a_p, d_p, normw], name=name, ride=ride)


def ssd_bwd(dyn, act, z, y_pre, states, dtp, bias_p, a_p, d_p, normw, name, ride=None):
    s = act.shape[0]
    nc = s // CHUNK
    b_off = D_INNER // SSM_STATE
    c_off = b_off + SSM_GROUPS

    def body(dyn_ref, xs_ref, b_ref, c_ref, z_ref, y_ref, st_ref, dt_ref, bias_ref, a_ref, d_ref, nw_ref,
             dxs_ref, db_ref, dc_ref, dz_ref, ddt_ref, dnw_ref, dbias_ref, da_ref, dd_ref,
             dstate, dtall, csall, cst):
        c = pl.program_id(0)
        g = pl.program_id(1)

        @pl.when(g == 0)
        def _():
            _ssd_scalars(dt_ref, bias_ref, a_ref, dtall, csall, cst)
            ddt_ref[...] = jnp.zeros((CHUNK, LANES), F32)

        @pl.when(c == 0)
        def _():
            dstate[g] = jnp.zeros((GROUP_DIM, SSM_STATE), F32)

        @pl.when(jnp.logical_and(c == 0, g == 0))
        def _():
            dnw_ref[...] = jnp.zeros(dnw_ref.shape, F32)
            dbias_ref[...] = jnp.zeros((1, LANES), F32)
            da_ref[...] = jnp.zeros((1, LANES), F32)
            dd_ref[...] = jnp.zeros((1, LANES), F32)

        cs_cols = _chunk_cols(csall[...], g)
        dt_cols = _chunk_cols(dtall[...], g)
        cs_rows = [_sub_pick(cst[...], HEADS_PER_GROUP * g + r) for r in range(HEADS_PER_GROUP)]
        d_cols = _chunk_cols(d_ref[...], g)
        cs_exp = _expand_heads(cs_cols)
        dt_exp = _expand_heads(dt_cols)
        d_exp = _expand_heads(d_cols)
        xs = xs_ref[...]
        bb = b_ref[...].astype(BF16)
        cb16 = c_ref[...].astype(BF16)
        xdt = xs * dt_exp
        s_prev = st_ref[0, 0]
        s_prev16 = s_prev.astype(BF16)
        ds_next = dstate[g]
        ds16 = ds_next.astype(BF16)

        zf = z_ref[...]
        sz = _sigmoid(zf)
        silu_z = zf * sz
        y = y_ref[...]
        yg = y * silu_z
        dout = dyn_ref[...]
        dyg, dnw = _rms_bwd(dout, yg, nw_ref[...])
        dnw_ref[pl.ds(g, 1), :] += jnp.sum(dnw, axis=0, keepdims=True)
        dy = dyg * silu_z
        dz_ref[...] = dyg * y * (sz * (1.0 + zf * (1.0 - sz)))
        dd_ref[...] += jnp.sum(_heads_to_lanes(dy * xs, g), axis=0, keepdims=True)

        exp_cs = jnp.exp(cs_exp)
        decay_st = jnp.exp(cs_exp[CHUNK - 1:CHUNK, :] - cs_exp)
        cs_t = _dot(cb16, s_prev16, NT)
        dyo = dy * exp_cs
        dc_acc = _dot(dyo.astype(BF16), s_prev16, NN)
        g1 = _dot(bb, ds16, NT)
        xds = xdt * decay_st
        db_acc = _dot(xds.astype(BF16), ds16, NN)
        dxdt_off = g1 * decay_st
        t_exp = g1 * xds
        dcs_exp = dy * cs_t * exp_cs - t_exp
        decay_c = _decay_col(cs_cols)
        dstate[g] = decay_c * ds_next + _dot(dyo.astype(BF16), cb16, TN)
        dlast_col = jnp.sum(ds_next * s_prev, axis=1, keepdims=True) * decay_c
        jj = lax.broadcasted_iota(jnp.int32, (GROUP_DIM, LANES), 0)
        ll = lax.broadcasted_iota(jnp.int32, (GROUP_DIM, LANES), 1)
        sel = ll == HEADS_PER_GROUP * g + (jj >> 6)
        dlast = jnp.sum(jnp.where(sel, dlast_col, 0.0), axis=0, keepdims=True)
        t_all = _heads_to_lanes(t_exp, g)
        dlast += jnp.sum(t_all, axis=0, keepdims=True)
        dcs_all = _heads_to_lanes(dcs_exp, g)

        cbm = _dot(cb16, bb, NT)
        dcb = jnp.zeros((CHUNK, CHUNK), F32)
        dcs_rows = jnp.zeros((LANES, CHUNK), F32)
        lane_l = lax.broadcasted_iota(jnp.int32, (CHUNK, LANES), 1)
        sub_l = lax.broadcasted_iota(jnp.int32, (LANES, CHUNK), 0)
        dxdt_pairs = []
        for p in range(HEADS_PER_GROUP // 2):
            xpair16 = xdt[:, LANES * p:LANES * (p + 1)].astype(BF16)
            dypair = dy[:, LANES * p:LANES * (p + 1)]
            acc = None
            for r in (2 * p, 2 * p + 1):
                lm = _decay_mat(cs_cols[r], cs_rows[r])
                m = cbm * lm
                dyh = _head_mask(dypair, r % 2 == 1).astype(BF16)
                dm = _dot(dyh, xpair16, NT)
                dcb += dm * lm
                q = dm * m
                idx = HEADS_PER_GROUP * g + r
                dcs_all += jnp.where(lane_l == idx, jnp.sum(q, axis=1, keepdims=True), 0.0)
                dcs_rows -= jnp.where(sub_l == idx, jnp.sum(q, axis=0, keepdims=True), 0.0)
                part = _dot(m.astype(BF16), dyh, TN)
                acc = part if acc is None else acc + part
            dxdt_pairs.append(acc)
        dxdt = jnp.concatenate(dxdt_pairs, axis=1) + dxdt_off
        dcb16 = dcb.astype(BF16)
        dc_ref[...] = dc_acc + _dot(dcb16, bb, NN)
        db_ref[...] = db_acc + _dot(dcb16, cb16, TN)
        dxs_ref[...] = dxdt * dt_exp + dy * d_exp

        dcs_all += dcs_rows.T
        row = lax.broadcasted_iota(jnp.int32, (CHUNK, CHUNK), 0)
        col = lax.broadcasted_iota(jnp.int32, (CHUNK, CHUNK), 1)
        last_row = lax.broadcasted_iota(jnp.int32, (CHUNK, LANES), 0) == CHUNK - 1
        dcs_all += jnp.where(last_row, dlast, 0.0)
        da_all = _dot_01(dcs_all, (col >= row).astype(BF16), True, 3)
        dta = dtall[...]
        in_group = jnp.logical_and(lane_l >= HEADS_PER_GROUP * g, lane_l < HEADS_PER_GROUP * (g + 1))
        ddt = jnp.where(in_group, da_all * a_ref[...] + _heads_to_lanes(dxdt * xs, g), 0.0)
        da_ref[...] += jnp.sum(jnp.where(in_group, da_all * dta, 0.0), axis=0, keepdims=True)
        ddt_raw = ddt * _sigmoid(dt_ref[...] + bias_ref[...])
        ddt_ref[...] += ddt_raw
        dbias_ref[...] += jnp.sum(ddt_raw, axis=0, keepdims=True)

    rev = lambda c, g: (nc - 1 - c, g)
    grp = pl.BlockSpec((CHUNK, GROUP_DIM), rev)
    st = pl.BlockSpec((CHUNK, SSM_STATE), rev)
    par = pl.BlockSpec((1, LANES), lambda c, g: (0, 0))
    dtb = pl.BlockSpec((CHUNK, LANES), lambda c, g: (nc - 1 - c, 0))
    f = lambda shape: _sds(shape, F32)
    return _pcall(
        body,
        out_shape=[f((s, D_INNER)), f((s, SSM_GROUPS * SSM_STATE)), f((s, SSM_GROUPS * SSM_STATE)),
                   f((s, D_INNER)), f((s, LANES)), f((8, GROUP_DIM)), f((1, LANES)), f((1, LANES)), f((1, LANES))],
        grid=(nc, SSM_GROUPS),
        in_specs=[grp, grp,
                  pl.BlockSpec((CHUNK, SSM_STATE), lambda c, g: (nc - 1 - c, b_off + g)),
                  pl.BlockSpec((CHUNK, SSM_STATE), lambda c, g: (nc - 1 - c, c_off + g)),
                  grp, grp,
                  pl.BlockSpec((1, 1, GROUP_DIM, SSM_STATE), lambda c, g: (nc - 1 - c, g, 0, 0)),
                  dtb, par, par, par,
                  pl.BlockSpec((1, GROUP_DIM), lambda c, g: (0, g))],
        out_specs=[grp, st, st, grp, dtb, pl.BlockSpec((8, GROUP_DIM), lambda c, g: (0, 0)), par, par, par],
        scratch_shapes=[pltpu.VMEM((SSM_GROUPS, GROUP_DIM, SSM_STATE), F32),
                        pltpu.VMEM((CHUNK, LANES), F32), pltpu.VMEM((CHUNK, LANES), F32),
                        pltpu.VMEM((LANES, CHUNK), F32)],
        args=[dyn, act, act, act, z, y_pre, states, dtp, bias_p, a_p, d_p, normw], name=name, ride=ride)


def _attn_probs(q, kp, kc, sink, n):
    sp = _dot(q, kp, NT)
    sc = _dot(q, kc, NT)
    i = lax.broadcasted_iota(jnp.int32, sp.shape, 0) & (WINDOW - 1)
    j = lax.broadcasted_iota(jnp.int32, sp.shape, 1)
    sp = jnp.where(jnp.logical_and(j > i, n > 0), sp, NEG)
    sc = jnp.where(j <= i, sc, NEG)
    m = jnp.maximum(jnp.maximum(jnp.max(sp, axis=1, keepdims=True), jnp.max(sc, axis=1, keepdims=True)), sink)
    pp = jnp.exp(sp - m)
    pc = jnp.exp(sc - m)
    ps = jnp.exp(sink - m)
    inv = 1.0 / (jnp.sum(pp, axis=1, keepdims=True) + jnp.sum(pc, axis=1, keepdims=True) + ps)
    return pp * inv, pc * inv, ps * inv


def attn_fwd(qt, kt, vt, sink_rows, name, ride=None):
    s = qt.shape[1]
    nb = s // WINDOW
    rows = Q_PER_KV * WINDOW

    def body(q_ref, kp_ref, kc_ref, vp_ref, vc_ref, sk_ref, o_ref):
        n = pl.program_id(1)
        q = q_ref[...].reshape(rows, ATT_HEAD_DIM)
        pp, pc, _ = _attn_probs(q, kp_ref[0], kc_ref[0], sk_ref[0], n)
        o = _dot(pp.astype(BF16), vp_ref[0]) + _dot(pc.astype(BF16), vc_ref[0])
        o_ref[...] = o.reshape(Q_PER_KV, WINDOW, ATT_HEAD_DIM).astype(BF16)

    qsp = pl.BlockSpec((Q_PER_KV, WINDOW, ATT_HEAD_DIM), lambda h, n: (h, n, 0))
    prev = pl.BlockSpec((1, WINDOW, ATT_HEAD_DIM), lambda h, n: (h, jnp.maximum(n - 1, 0), 0))
    cur = pl.BlockSpec((1, WINDOW, ATT_HEAD_DIM), lambda h, n: (h, n, 0))
    return _pcall(body, out_shape=[_sds(qt.shape, BF16)], grid=(N_KV_HEADS, nb),
                  in_specs=[qsp, prev, cur, prev, cur, pl.BlockSpec((1, rows, 1), lambda h, n: (h, 0, 0))],
                  out_specs=[qsp], args=[qt, kt, kt, vt, vt, sink_rows], name=name, ride=ride)


def attn_bwd(qt, kt, vt, sink_rows, dot_, name, ride=None):
    s = qt.shape[1]
    nb = s // WINDOW
    rows = Q_PER_KV * WINDOW

    def body(q_ref, kp_ref, kc_ref, vp_ref, vc_ref, sk_ref, do_ref, dq_ref, dk_ref, dv_ref, ds_ref, kacc, vacc):
        n = pl.program_id(1)

        @pl.when(n < nb)
        def _():
            q = q_ref[...].reshape(rows, ATT_HEAD_DIM)
            do = do_ref[...].reshape(rows, ATT_HEAD_DIM)
            kp, kc, vp, vc = kp_ref[0], kc_ref[0], vp_ref[0], vc_ref[0]
            pp, pc, ps = _attn_probs(q, kp, kc, sk_ref[0], n)
            dpp = _dot(do, vp, NT)
            dpc = _dot(do, vc, NT)
            delta = jnp.sum(pp * dpp, axis=1, keepdims=True) + jnp.sum(pc * dpc, axis=1, keepdims=True)
            dsp = (pp * (dpp - delta)).astype(BF16)
            dsc = (pc * (dpc - delta)).astype(BF16)
            dq = _dot(dsp, kp) + _dot(dsc, kc)
            dq_ref[...] = dq.reshape(Q_PER_KV, WINDOW, ATT_HEAD_DIM)
            dk_prev = _dot(dsp, q, TN)
            dv_prev = _dot(pp.astype(BF16), do, TN)

            @pl.when(n == 0)
            def _():
                dk_ref[0] = dk_prev
                dv_ref[0] = dv_prev

            @pl.when(n > 0)
            def _():
                dk_ref[0] = kacc[...] + dk_prev
                dv_ref[0] = vacc[...] + dv_prev

            kacc[...] = _dot(dsc, q, TN)
            vacc[...] = _dot(pc.astype(BF16), do, TN)
            dsk = -ps * delta
            sub = lax.broadcasted_iota(jnp.int32, (8, LANES), 0)
            tile = jnp.zeros((8, LANES), F32)
            for h in range(Q_PER_KV):
                tile += jnp.where(sub == h, jnp.sum(dsk[h * WINDOW:(h + 1) * WINDOW, :], axis=0, keepdims=True), 0.0)
            ds_ref[0, 0] = tile

        @pl.when(n == nb)
        def _():
            dk_ref[0] = kacc[...]
            dv_ref[0] = vacc[...]
            ds_ref[0, 0] = jnp.zeros((8, LANES), F32)

    last = nb - 1
    qsp = pl.BlockSpec((Q_PER_KV, WINDOW, ATT_HEAD_DIM), lambda h, n: (h, jnp.minimum(n, last), 0))
    prev = pl.BlockSpec((1, WINDOW, ATT_HEAD_DIM), lambda h, n: (h, jnp.clip(n - 1, 0, last), 0))
    cur = pl.BlockSpec((1, WINDOW, ATT_HEAD_DIM), lambda h, n: (h, jnp.minimum(n, last), 0))
    dkv = pl.BlockSpec((1, WINDOW, ATT_HEAD_DIM), lambda h, n: (h, jnp.maximum(n - 1, 0), 0))
    f = lambda shape: _sds(shape, F32)
    return _pcall(
        body, out_shape=[f(qt.shape), f(kt.shape), f(vt.shape), f((N_KV_HEADS, nb + 1, 8, LANES))],
        grid=(N_KV_HEADS, nb + 1),
        in_specs=[qsp, prev, cur, prev, cur, pl.BlockSpec((1, rows, 1), lambda h, n: (h, 0, 0)), qsp],
        out_specs=[qsp, dkv, dkv, pl.BlockSpec((1, 1, 8, LANES), lambda h, n: (h, n, 0, 0))],
        scratch_shapes=[pltpu.VMEM((WINDOW, ATT_HEAD_DIM), F32), pltpu.VMEM((WINDOW, ATT_HEAD_DIM), F32)],
        args=[qt, kt, kt, vt, vt, sink_rows, dot_], name=name, ride=ride)


def loss_head(x, w, tgt, name):
    s, d = x.shape
    tm = _row_tile(s, 256)

    def body(x_ref, w_ref, t_ref, loss_ref, dx_ref, dw_ref):
        i = pl.program_id(0)
        xf = x_ref[...]
        wv = w_ref[...]
        r = lax.rsqrt(jnp.mean(xf * xf, axis=-1, keepdims=True) + EPS)
        xhat = xf * r
        e = xhat * wv - t_ref[...]
        part = 0.5 * jnp.sum(jnp.mean(e * e, axis=-1, keepdims=True), axis=0, keepdims=True)
        dy = e * (1.0 / d)
        dxhat = dy * wv
        dx_ref[...] = r * (dxhat - xhat * jnp.mean(dxhat * xhat, axis=-1, keepdims=True))
        col = jnp.sum(dy * xhat, axis=0, keepdims=True)

        @pl.when(i == 0)
        def _():
            loss_ref[...] = jnp.broadcast_to(part, (1, LANES))
            dw_ref[...] = col

        @pl.when(i > 0)
        def _():
            loss_ref[...] += jnp.broadcast_to(part, (1, LANES))
            dw_ref[...] += col

    row = pl.BlockSpec((tm, d), lambda i: (i, 0))
    vec = pl.BlockSpec((1, d), lambda i: (0, 0))
    return _pcall(body, out_shape=[_sds((1, LANES), F32), _sds((s, d), F32), _sds((1, d), F32)], grid=(s // tm,),
                  in_specs=[row, vec, row], out_specs=[pl.BlockSpec((1, LANES), lambda i: (0, 0)), row, vec],
                  args=[x, w.reshape(1, d), tgt], name=name)


def _tile_rows(r, c, max_elems=262144, mult=16):
    best = None
    for t in range(mult, r + 1, mult):
        if r % t == 0 and t * c <= max_elems:
            best = t
    return best or r


def add_pair(xhs, ps, c_idx, name):
    n = len(xhs)
    _, r, c = xhs[0].shape
    tr = _tile_rows(r, c)

    def body(c_ref, *refs):
        for x_ref, p_ref, o_ref in zip(refs[:n], refs[n:2 * n], refs[2 * n:]):
            o_ref[...] = (x_ref[0].astype(F32) + p_ref[...].astype(F32)).astype(BF16)

    blk = pl.BlockSpec((tr, c), lambda i, cr: (i, 0))
    return pl.pallas_call(
        body, out_shape=tuple([_sds((r, c), BF16)] * n),
        grid_spec=pltpu.PrefetchScalarGridSpec(
            num_scalar_prefetch=1, grid=(r // tr,),
            in_specs=[pl.BlockSpec((1, tr, c), lambda i, cr: (cr[0], i, 0))] * n + [blk] * n,
            out_specs=tuple([blk] * n)),
        name=name, compiler_params=_cp(1))(c_idx, *xhs, *ps)


def sum_chips(qs, owns, chip_idx, name):
    n = len(qs)
    _, r, c = qs[0].shape
    tr = _tile_rows(r, c)

    def body(k_ref, *refs):
        k = k_ref[0]
        for q_ref, own_ref, o_ref in zip(refs[:n], refs[n:2 * n], refs[2 * n:]):
            mine = own_ref[0].astype(F32)
            tot = None
            for j in range(N_CHIPS):
                term = jnp.where(k == j, mine, q_ref[j].astype(F32))
                tot = term if tot is None else tot + term
            o_ref[...] = tot

    return pl.pallas_call(
        body, out_shape=tuple([_sds((r, c), F32)] * n),
        grid_spec=pltpu.PrefetchScalarGridSpec(
            num_scalar_prefetch=1, grid=(r // tr,),
            in_specs=([pl.BlockSpec((N_CHIPS, tr, c), lambda i, kr: (0, i, 0))] * n
                      + [pl.BlockSpec((1, tr, c), lambda i, kr: (kr[0], i, 0))] * n),
            out_specs=tuple([pl.BlockSpec((tr, c), lambda i, kr: (i, 0))] * n)),
        name=name, compiler_params=_cp(1))(chip_idx, *qs, *owns)


def adamw(w, g, m, v, name):
    r, c = w.shape
    tr = _tile_rows(r, c, max_elems=131072, mult=8)
    c1 = 1.0 / (1.0 - ADAM_B1 ** ADAM_STEP)
    c2 = 1.0 / (1.0 - ADAM_B2 ** ADAM_STEP)

    def body(w_ref, g_ref, m_ref, v_ref, d_ref, mo_ref, vo_ref):
        gf = g_ref[...]
        mn = ADAM_B1 * m_ref[...] + (1.0 - ADAM_B1) * gf
        vn = ADAM_B2 * v_ref[...] + (1.0 - ADAM_B2) * (gf * gf)
        mo_ref[...] = mn
        vo_ref[...] = vn
        d_ref[...] = -ADAM_LR * ((mn * c1) / (jnp.sqrt(vn * c2) + ADAM_EPS) + ADAM_WD * w_ref[...])

    blk = pl.BlockSpec((tr, c), lambda i: (i, 0))
    out = _sds((r, c), F32)
    return _pcall(body, out_shape=[out, out, out], grid=(r // tr,), in_specs=[blk] * 4, out_specs=[blk] * 3,
                  args=[w, g, m, v], name=name)


WEIGHTS = ['norm_w', 'ffn_w_gate', 'ffn_w_up', 'ffn_w_down', 'ssm_w_in', 'ssm_conv_w', 'ssm_conv_b', 'ssm_dt_bias',
           'ssm_a_log', 'ssm_d', 'ssm_norm_w', 'ssm_w_out', 'kv_norm_w', 'w_k', 'b_k', 'w_v', 'b_v', 'attn_w_q',
           'attn_b_q', 'attn_sinks', 'attn_w_o', 'attn_b_o', 'final_norm_w']
BIG = ['ffn_w_gate', 'ffn_w_up', 'ffn_w_down', 'ssm_w_in', 'ssm_w_out', 'w_k', 'w_v', 'attn_w_q', 'attn_w_o']
TRANSPOSED = ('ffn_w_gate', 'ffn_w_up', 'ssm_w_in')
SMALL = [n for n in WEIGHTS if n not in BIG]
SMALL_SHARDED = {'norm_w': 2, 'ssm_conv_w': 2, 'ssm_conv_b': 1, 'ssm_norm_w': 1}
ROW_ALIGN = 8 * LANES


def _pack_rows(parts):
    flat = jnp.concatenate([p.reshape(-1).astype(F32) for p in parts])
    pad = (-flat.size) % ROW_ALIGN
    return jnp.pad(flat, (0, pad)).reshape(-1, LANES)


def _unpack_rows(buf, shapes):
    flat = buf.reshape(-1)
    out, pos = [], 0
    for shp in shapes:
        size = math.prod(shp)
        out.append(flat[pos:pos + size].reshape(shp))
        pos += size
    return out


def _as2d(a):
    return a.reshape(-1, a.shape[-1])


def _heads_major(t, n_heads):
    s = t.shape[0]
    return t.reshape(s, n_heads, ATT_HEAD_DIM).transpose(1, 0, 2)


def _tokens_major(t):
    h, s, dh = t.shape
    return t.transpose(1, 0, 2).reshape(s, h * dh)


def _pad_lanes(v):
    return jnp.pad(v.reshape(1, -1), ((0, 0), (0, LANES - v.size)))


def _chips_first(t):
    return t.swapaxes(0, 1).reshape((-1,) + t.shape[3:])


def _parts_first(t, rows):
    return t.reshape((N_CHIPS, N_CORES, rows) + t.shape[1:]).swapaxes(0, 1)


def kernel(*args):
    names = (['x'] + WEIGHTS + ['loss_target'] + ['m_' + n for n in WEIGHTS] + ['v_' + n for n in WEIGHTS])
    a = dict(zip(names, args))
    for n in TRANSPOSED:
        for pre in ('', 'm_', 'v_'):
            a[pre + n] = a[pre + n].swapaxes(-1, -2)
    xi, yi, ci = lax.axis_index("x"), lax.axis_index("y"), lax.axis_index("c")
    chip = 2 * xi + yi
    south = ci == 0
    c_idx = jnp.reshape(ci, (1,)).astype(jnp.int32)
    chip_idx = jnp.reshape(chip, (1,)).astype(jnp.int32)
    x0 = a['x'][0]
    s = x0.shape[0]
    cos, sin = rope_tables(s)

    def own_slot(full, mine):
        return lax.dynamic_update_slice_in_dim(full, mine[:, None], chip, axis=1)

    def ffn_shard(l, i, src):
        return [src[n][l, i].astype(BF16).reshape(N_CORES, FF_PART, D_MODEL)
                for n in ('ffn_w_gate', 'ffn_w_up', 'ffn_w_down')]

    def own_slots(fulls, mines):
        return [own_slot(f, m) for f, m in zip(fulls, mines)]
    small_names = list(SMALL_SHARDED)
    small_sh = _pack_rows([a[n] for n in small_names])
    small_sh = small_sh.reshape(N_CORES, small_sh.shape[0] // 2, LANES)
    sh00 = ffn_shard(0, 0, a)
    (first_flight,), started = split_start([sh00 + [small_sh]], "gather", "gather_start_first")
    held = lax.optimization_barrier((started, {n: a[n] for n in BIG}))[1]
    sh01, sh10, sh11 = ffn_shard(0, 1, held), ffn_shard(1, 0, held), ffn_shard(1, 1, held)
    w_in_sh = jnp.pad(held['ssm_w_in'][0], ((0, IN_SHARD_PAD - IN_SHARD), (0, 0))).astype(BF16).reshape(
        N_CORES, IN_SHARD_PAD // 2, D_MODEL)
    w_out_sh = held['ssm_w_out'][0].astype(BF16).reshape(N_CORES, 256, D_MODEL)
    attn_sh = jnp.stack([held['attn_w_q'][0], held['attn_w_o'][0]]).astype(BF16)
    kv_sh = jnp.stack([held['w_k'], held['w_v']]).astype(BF16)
    rest_flights, all_started = split_start([[w_in_sh, kv_sh], [w_out_sh], sh01, sh10, [attn_sh], sh11], "gather",
                                            "gather_start_rest")
    in_flight = [first_flight] + rest_flights

    def arrive(idx, after, tag):
        return forward_cores(split_arrive(in_flight[idx], "gather", after, "gather_arrive_" + tag))

    first = run_exchange(arrive(0, all_started, "first"), "gather_hop_first")
    w00 = own_slots(first[:3], sh00)
    smalls = own_slot(first[3], small_sh)
    p = {}
    per_chip = [_unpack_rows(smalls[:, k], [a[n].shape for n in small_names]) for k in range(N_CHIPS)]
    for idx, n in enumerate(small_names):
        p[n] = jnp.concatenate([per_chip[k][idx] for k in range(N_CHIPS)], axis=SMALL_SHARDED[n])
    nw = p['norm_w']
    conv_w, conv_b, ssm_nw = p['ssm_conv_w'][0], p['ssm_conv_b'][0], p['ssm_norm_w'][0].reshape(1, D_INNER)

    h00 = rmsnorm_fwd(x0, nw[0, 0], "norm_in")
    x1, h01, gu00 = ffn_fwd(h00, x0, *w00, [nw[0, 1]], "ffn_fwd_00")
    w_in_g, kv_g = run_exchange(arrive(1, x1, "in"), "gather_hop_in")
    w_in_t = _chips_first(own_slot(w_in_g, w_in_sh)).reshape(N_CHIPS, IN_SHARD_PAD, D_MODEL)[:, :IN_SHARD].reshape(
        IN_PROJ_DIM, D_MODEL)
    w_dt_t = jnp.pad(w_in_t[D_INNER + CONV_DIM:], ((0, LANES - SSM_HEADS), (0, 0)))
    kv_g = own_slot(kv_g, kv_sh)
    w_k, w_v = kv_g[0].reshape(D_MODEL, KV_DIM), kv_g[1].reshape(D_MODEL, KV_DIM)

    zz = mm_nt(h01, w_in_t, "ssm_in_z", n=D_INNER)
    xbc = mm_nt(h01, w_in_t, "ssm_in_xbc", n=CONV_DIM, row0=D_INNER)
    dtp = mm_nt(h01, w_dt_t, "ssm_in_dt")
    act = conv_fwd(xbc, conv_w, conv_b, "ssm_conv")
    bias_p = _pad_lanes(a['ssm_dt_bias'][0])
    a_p = _pad_lanes(-jnp.exp(a['ssm_a_log'][0]))
    d_p = _pad_lanes(a['ssm_d'][0])
    (yn, y_pre, states), (w_out_g,) = ssd_fwd(act, zz, dtp, bias_p, a_p, d_p, ssm_nw, "ssd_fwd",
                                              ride=arrive(2, act, "out"))
    w_out = _chips_first(own_slot(w_out_g, w_out_sh))
    (x2, h02), w01 = mm_res(yn, w_out, x1, "ssm_out", norm_ws=[nw[0, 2]], ride=arrive(3, yn, "01"))
    w01 = own_slots(w01, sh01)
    x3, hkv, h10, gu01 = ffn_fwd(h02, x2, *w01, [a['kv_norm_w'], nw[1, 0]], "ffn_fwd_01")
    w10 = own_slots(run_exchange(arrive(4, x3, "10"), "gather_hop_10"), sh10)

    k_rot = rope_apply(mm_nn(hkv, w_k, "kv_k", bias=a['b_k']), cos, sin, "rope_k")
    v = mm_nn(hkv, w_v, "kv_v", bias=a['b_v'], out_dtype=BF16)
    kt = _heads_major(k_rot, N_KV_HEADS)
    vt = _heads_major(v, N_KV_HEADS)

    (x4, h11, gu10), (attn_g,) = ffn_fwd(h10, x3, *w10, [nw[1, 1]], "ffn_fwd_10", ride=arrive(5, v, "attn"))
    attn_g = own_slot(attn_g, attn_sh)
    w_q, w_o = attn_g[0].reshape(D_MODEL, D_MODEL), attn_g[1].reshape(D_MODEL, D_MODEL)
    scale = 1.0 / math.sqrt(ATT_HEAD_DIM)
    q_rot = rope_apply(mm_nn(h11, w_q, "attn_q", bias=a['attn_b_q'][0]), cos, sin, "rope_q", scale=scale)
    qt = _heads_major(q_rot, N_Q_HEADS)
    sink_rows = jnp.repeat(a['attn_sinks'][0].reshape(N_KV_HEADS, Q_PER_KV), WINDOW, axis=1).reshape(
        N_KV_HEADS, Q_PER_KV * WINDOW, 1)
    (ot,) = attn_fwd(qt, kt, vt, sink_rows, "attn_fwd")
    o = _tokens_major(ot)
    (x5, h12), w11 = mm_res(o, w_o, x4, "attn_out", bias=a['attn_b_o'][0], norm_ws=[nw[1, 2]],
                            ride=arrive(6, ot, "11"))
    w11 = own_slots(w11, sh11)
    x6, gu11 = ffn_fwd(h12, x5, *w11, [], "ffn_fwd_11")

    loss_v, dx6, d_final = loss_head(x6, a['final_norm_w'], a['loss_target'][0], "loss_head")
    loss = lax.psum(loss_v[0, 0], ("x", "y", "c"))
    g = {'final_norm_w': d_final[0]}

    def same_shape(xs, ys):
        runs = []
        for xv, yv in zip(xs, ys):
            if runs and runs[-1][0][0].shape == xv.shape:
                runs[-1][0].append(xv)
                runs[-1][1].append(yv)
            else:
                runs.append(([xv], [yv]))
        return runs

    def pre_reduce(grads, sib, tag):
        out = []
        for idx, (grp, sbs) in enumerate(same_shape(grads, list(sib))):
            ts = add_pair([gr.reshape(2, -1, gr.shape[-1]) for gr in grp], [_as2d(sb) for sb in sbs], c_idx,
                          "rs_add_%s_%d" % (tag, idx))
            out += [t.reshape(gr.shape[1:]) for t, gr in zip(ts, grp)]
        return out

    def chip_sum(landed, parts, tag):
        out = []
        for idx, (qs, owns) in enumerate(same_shape(list(landed), parts)):
            ts = sum_chips([q.reshape(N_CHIPS, -1, q.shape[-1]) for q in qs],
                           [own.reshape(N_CHIPS, -1, own.shape[-1]) for own in owns], chip_idx,
                           "rs_sum_%s_%d" % (tag, idx))
            out += [t.reshape(q.shape[1:]) for t, q in zip(ts, qs)]
        return out

    dnw = [[None] * 3 for _ in range(2)]
    sums = {}

    def trade(key):
        return swap_cores(sums[key], False)

    dx5, dnw12, *g11 = ffn_bwd(dx6, h12, x5, nw[1, 2], gu11, *w11, "ffn_bwd_11")
    dnw[1][2] = dnw12[0]
    (d_wo, g['attn_b_o']), sib11 = mm_tn(o, dx5, "attn_dwo", col_sum=True, ride=swap_cores(g11, True))
    t11 = pre_reduce(g11, sib11, "11")
    do = mm_nt(dx5, w_o, "attn_do", out_dtype=BF16)
    (dqt, dkt, dvt, dsink), land11 = attn_bwd(qt, kt, vt, sink_rows, _heads_major(do, N_Q_HEADS), "attn_bwd",
                                             ride=scatter_chips(t11))
    sums['11'] = chip_sum(land11, t11, "11")
    g['attn_sinks'] = jnp.sum(dsink[:, :, :Q_PER_KV, 0], axis=1).reshape(N_Q_HEADS)
    dq_pre = rope_apply(_tokens_major(dqt), cos, sin, "rope_dq", inverse=True, scale=scale, out_dtype=F32)
    d_wq, g['attn_b_q'] = mm_tn(h11, dq_pre, "attn_dwq", col_sum=True)
    g_attn = [jnp.stack([d_wq.reshape(N_CHIPS, 256, D_MODEL), d_wo.reshape(N_CHIPS, 256, D_MODEL)])]
    (dx4, dnw[1][1]), sib_attn = mm_rms_bwd([(dq_pre, 0, w_q, 0, D_MODEL, "nt")], dx5, x4, nw[1, 1], "attn_bwd_dh",
                                            ride=swap_cores(g_attn, True))
    t_attn = pre_reduce(g_attn, sib_attn, "attn")
    (dx3, dnw10, *g10), landed = ffn_bwd(dx4, h10, x3, nw[1, 0], gu10, *w10, "ffn_bwd_10",
                                         ride=join(scatter_chips(t_attn), trade('11')))
    dnw[1][0] = dnw10[0]
    sums['attn'] = chip_sum(landed[:1], t_attn, "attn")
    theirs = {'11': landed[1:]}
    dk_pre = rope_apply(_tokens_major(dkt), cos, sin, "rope_dk", inverse=True, out_dtype=F32)
    dv = _tokens_major(dvt)
    (d_wk, g['b_k']), sib10 = mm_tn(hkv, dk_pre, "kv_dwk", col_sum=True, ride=swap_cores(g10, True))
    t10 = pre_reduce(g10, sib10, "10")
    d_wv, g['b_v'] = mm_tn(hkv, dv, "kv_dwv", col_sum=True)
    g_kv = [jnp.stack([d_wk.reshape(N_CHIPS, 256, KV_DIM), d_wv.reshape(N_CHIPS, 256, KV_DIM)])]
    (dx3, g['kv_norm_w']), sib_kv = mm_rms_bwd(
        [(dk_pre, 0, w_k, 0, KV_DIM, "nt"), (dv, 0, w_v, 0, KV_DIM, "nt")], dx3, x3, a['kv_norm_w'], "kv_bwd_dh",
        ride=swap_cores(g_kv, True))
    t_kv = pre_reduce(g_kv, sib_kv, "kv")
    (dx2, dnw02, *g01), landed = ffn_bwd(dx3, h02, x2, nw[0, 2], gu01, *w01, "ffn_bwd_01",
                                         ride=join(scatter_chips(t10 + t_kv), trade('attn')))
    dnw[0][2] = dnw02[0]
    sums['10'] = chip_sum(landed[:3], t10, "10")
    sums['kv'] = chip_sum(landed[3:4], t_kv, "kv")
    theirs['attn'] = landed[4:]
    d_wout, sib01 = mm_tn(yn, dx2, "ssm_dwout", ride=swap_cores(g01, True))
    t01 = pre_reduce(g01, sib01, "01")
    dyn = mm_nt(dx2, w_out, "ssm_dyn")
    (dxs, db_, dc_, dz, ddt, d_ssm_nw, d_bias, d_a, d_d), landed = ssd_bwd(
        dyn, act, zz, y_pre, states, dtp, bias_p, a_p, d_p, ssm_nw, "ssd_bwd",
        ride=join(scatter_chips(t01), trade('10'), trade('kv')))
    sums['01'] = chip_sum(landed[:3], t01, "01")
    theirs['10'], theirs['kv'] = landed[3:6], landed[6:]
    g['ssm_norm_w'] = d_ssm_nw[:SSM_GROUPS].reshape(D_INNER)
    g['ssm_dt_bias'] = d_bias[0, :SSM_HEADS]
    g['ssm_a_log'] = d_a[0, :SSM_HEADS] * a_p[0, :SSM_HEADS]
    g['ssm_d'] = d_d[0, :SSM_HEADS]
    dxbc, g['ssm_conv_w'], g['ssm_conv_b'] = conv_bwd(dxs, db_, dc_, xbc, conv_w, conv_b, "ssm_conv_bwd")
    d_win = mm_tn(dz, h01, "ssm_dwz", rows=IN_PROJ_DIM)
    d_win = mm_tn(dxbc, h01, "ssm_dwxbc", into=d_win, rows=IN_PROJ_DIM, row0=D_INNER)
    d_win = mm_tn(ddt, h01, "ssm_dwdt", into=d_win, rows=IN_PROJ_DIM, row0=D_INNER + CONV_DIM, m_valid=SSM_HEADS)
    d_win = jnp.pad(d_win.reshape(N_CHIPS, IN_SHARD, D_MODEL), ((0, 0), (0, IN_SHARD_PAD - IN_SHARD), (0, 0)))
    g_ssm = [_parts_first(d_win.reshape(-1, D_MODEL), IN_SHARD_PAD // 2), _parts_first(d_wout, 256)]
    kb = 1024
    terms = ([(dz, j, w_in_t, j, kb, "nn") for j in range(D_INNER // kb)]
             + [(dxbc, j, w_in_t, D_INNER // kb + j, kb, "nn") for j in range(CONV_DIM // kb)]
             + [(ddt, 0, w_dt_t, 0, LANES, "nn")])
    (dx1, dnw[0][1]), sib_ssm = mm_rms_bwd(terms, dx2, x1, nw[0, 1], "ssm_bwd_dh", ride=swap_cores(g_ssm, True))
    t_ssm = pre_reduce(g_ssm, sib_ssm, "ssm")
    (grad_x, dnw00, *g00), landed = ffn_bwd(dx1, h00, x0, nw[0, 0], gu00, *w00, "ffn_bwd_00",
                                            ride=join(scatter_chips(t_ssm), trade('01')))
    dnw[0][0] = dnw00[0]
    sums['ssm'] = chip_sum(landed[:2], t_ssm, "ssm")
    theirs['01'] = landed[2:]
    landed = run_exchange(join(swap_cores(g00, True), trade('ssm')), "rs_swap_00")
    t00 = pre_reduce(g00, landed[:3], "00")
    theirs['ssm'] = landed[3:]

    def both(key):
        return [(jnp.where(south, m_, t_), jnp.where(south, t_, m_)) for m_, t_ in zip(sums[key], theirs[key])]

    (flight00,), flying = split_start([t00], "scatter", "rs_scatter_00_start")

    def held(val):
        return lax.optimization_barrier((flying, val))[1]

    delta, new_m, new_v, gw = {}, {}, {}, {}
    ffn_names = ('ffn_w_gate', 'ffn_w_up', 'ffn_w_down')
    full = {key: both(key) for key in ('attn', 'kv', 'ssm')}
    lo, hi = full['attn'][0]
    gw['attn_w_q'], gw['attn_w_o'] = lo[None], hi[None]
    lo, hi = full['kv'][0]
    gw['w_k'], gw['w_v'] = lo, hi
    lo, hi = full['ssm'][0]
    gw['ssm_w_in'] = jnp.concatenate([lo, hi], axis=0)[:IN_SHARD][None]
    lo, hi = full['ssm'][1]
    gw['ssm_w_out'] = jnp.concatenate([lo, hi], axis=0)[None]

    g['norm_w'] = jnp.stack([jnp.stack(r) for r in dnw])
    red = all_reduce_small(held(_pack_rows([g[n] for n in SMALL])), "reduce_vectors")
    for n, t in zip(SMALL, _unpack_rows(red, [g[n].shape for n in SMALL])):
        if n in SMALL_SHARDED:
            ax = SMALL_SHARDED[n] - (a[n].ndim - t.ndim)
            width = a[n].shape[SMALL_SHARDED[n]]
            t = lax.dynamic_slice_in_dim(t, chip * width, width, axis=ax)
        gw[n] = t.reshape(a[n].shape)

    def update(n):
        d, mo, vo = adamw(_as2d(a[n]), held(_as2d(gw[n])), _as2d(a['m_' + n]), _as2d(a['v_' + n]), "adamw_" + n)
        delta[n], new_m[n], new_v[n] = d.reshape(a[n].shape), mo.reshape(a[n].shape), vo.reshape(a[n].shape)

    for n in BIG:
        if n not in ffn_names:
            update(n)
    shapes = [a[n].shape for n in SMALL]
    packed = [_pack_rows([src[n] for n in SMALL]) for src in
              (a, gw, {n: a['m_' + n] for n in SMALL}, {n: a['v_' + n] for n in SMALL})]
    outs = adamw(*packed, "adamw_vectors")
    for dst, buf in zip((delta, new_m, new_v), outs):
        for n, t in zip(SMALL, _unpack_rows(buf, shapes)):
            dst[n] = t
    rest = [both(key) for key in ('01', '10', '11')]
    land00 = split_arrive(flight00, "scatter", outs[0], "rs_scatter_00_arrive")
    sums['00'] = chip_sum(land00, t00, "00")
    theirs['00'] = run_exchange(trade('00'), "rs_trade_00")
    blocks = [both('00')] + rest
    for t, n in enumerate(ffn_names):
        gw[n] = jnp.concatenate([piece for blk in blocks for piece in blk[t]], axis=0).reshape(a[n].shape)
        update(n)
    for n in TRANSPOSED:
        for dst in (gw, delta, new_m, new_v):
            dst[n] = dst[n].swapaxes(-1, -2)

    return (loss, grad_x[None], *[gw[n] for n in WEIGHTS], *[delta[n] for n in WEIGHTS],
            *[new_m[n] for n in WEIGHTS], *[new_v[n] for n in WEIGHTS])
```

```python
import math

import jax
import jax.numpy as jnp
from jax import lax
from jax.experimental import pallas as pl
from jax.experimental.pallas import tpu as pltpu

F32 = jnp.float32
BF16 = jnp.bfloat16

D_MODEL = 1024
D_INNER = 2048
SSM_HEADS = 32
SSM_GROUPS = 4
HEADS_PER_GROUP = SSM_HEADS // SSM_GROUPS
SSM_HEAD_DIM = 64
SSM_STATE = 128
GROUP_DIM = D_INNER // SSM_GROUPS
CONV_DIM = D_INNER + 2 * SSM_GROUPS * SSM_STATE
CONV_WIDTH = 4
CHUNK = 128
ATT_HEAD_DIM = 64
N_Q_HEADS = 16
N_KV_HEADS = 4
Q_PER_KV = N_Q_HEADS // N_KV_HEADS
KV_DIM = N_KV_HEADS * ATT_HEAD_DIM
WINDOW = 128
ROPE_THETA = 10000.0
D_FF = 2816
N_CHIPS = 4
N_CORES = 2
FF_SHARD = D_FF // N_CHIPS
FF_PART = FF_SHARD // N_CORES
IN_PROJ_DIM = D_INNER + CONV_DIM + SSM_HEADS
IN_SHARD = IN_PROJ_DIM // N_CHIPS
IN_SHARD_PAD = 1312
EPS = 1e-5
NEG = -1e30
LANES = 128
VMEM_LIMIT = 56 * 1024 * 1024

ADAM_LR = 0.001
ADAM_B1 = 0.9
ADAM_B2 = 0.999
ADAM_EPS = 1e-08
ADAM_WD = 0.01
ADAM_STEP = 10

NN = ((1,), (0,))
NT = ((1,), (1,))
TN = ((0,), (0,))
MESH = pl.DeviceIdType.MESH
ANY = pl.BlockSpec(memory_space=pl.ANY)


def _dot(a, b, dims=NN, precision=None):
    return lax.dot_general(a, b, (dims, ((), ())), preferred_element_type=F32, precision=precision)


def _cp(n_grid):
    return pltpu.CompilerParams(dimension_semantics=("arbitrary",) * n_grid, vmem_limit_bytes=VMEM_LIMIT)


def _sigmoid(x):
    return 1.0 / (1.0 + jnp.exp(-x))


def _rms_fwd(xf, w):
    r = lax.rsqrt(jnp.mean(xf * xf, axis=-1, keepdims=True) + EPS)
    return xf * r * w


def _rms_bwd(dh, xf, w):
    r = lax.rsqrt(jnp.mean(xf * xf, axis=-1, keepdims=True) + EPS)
    xhat = xf * r
    dxhat = dh * w
    dx = r * (dxhat - xhat * jnp.mean(dxhat * xhat, axis=-1, keepdims=True))
    return dx, dh * xhat


def _row_tile(s, pref):
    return pref if s % pref == 0 else s


def _col_tile(n):
    for t in (1024, 768, 512, 256, 128):
        if n % t == 0:
            return t
    return n


def _sds(shape, dtype):
    return jax.ShapeDtypeStruct(tuple(shape), dtype)


class Exchange:
    def __init__(self, ins, out_shapes, sems, start, finish, inplace=False):
        self.ins, self.out_shapes, self.sems, self.start, self.finish = ins, out_shapes, sems, start, finish
        self.inplace = inplace


def _place():
    x, y, c = lax.axis_index("x"), lax.axis_index("y"), lax.axis_index("c")
    others = [(1 - x, y), (x, 1 - y), (1 - x, 1 - y)]
    return x, y, c, 2 * x + y, others


def _rc(src, dst, send_sem, recv_sem, dev):
    return pltpu.make_async_remote_copy(src_ref=src, dst_ref=dst, send_sem=send_sem, recv_sem=recv_sem,
                                        device_id=dev, device_id_type=MESH)


def gather_chips(arrs):
    n = len(arrs)

    def copies(ins, outs, sems):
        send, recv = sems
        x, y, c, k, others = _place()
        ici, land, fwd, fland = [], [], [], []
        for a in range(n):
            for j, (px, py) in enumerate(others):
                ici.append(_rc(ins[a].at[c], outs[a].at[c, k], send.at[a, j], recv.at[a, j], (px, py, c)))
                blk = outs[a].at[c, 2 * px + py]
                land.append(_rc(blk, blk, send.at[a, j], recv.at[a, j], (px, py, c)))
                fwd.append(_rc(blk, blk, send.at[a, 3 + j], recv.at[a, 3 + j], (x, y, 1 - c)))
                blk2 = outs[a].at[1 - c, 2 * px + py]
                fland.append(_rc(blk2, blk2, send.at[a, 3 + j], recv.at[a, 3 + j], (x, y, 1 - c)))
        return ici, land, fwd, fland

    def start(ins, outs, sems):
        for cp in copies(ins, outs, sems)[0]:
            cp.start()

    def finish(ins, outs, sems):
        ici, land, fwd, fland = copies(ins, outs, sems)
        for arrived, onward in zip(land, fwd):
            arrived.wait_recv()
            onward.start()
        for arrived in fland:
            arrived.wait_recv()
        for cp in ici + fwd:
            cp.wait_send()

    return Exchange(list(arrs), [_sds((2, N_CHIPS) + a.shape[1:], a.dtype) for a in arrs],
                    [pltpu.SemaphoreType.DMA((n, 6)), pltpu.SemaphoreType.DMA((n, 6))], start, finish)


def scatter_chips(arrs):
    n = len(arrs)

    def copies(ins, outs, sems):
        send, recv = sems
        x, y, c, k, others = _place()
        out, land = [], []
        for a in range(n):
            for j, (px, py) in enumerate(others):
                out.append(_rc(ins[a].at[2 * px + py], outs[a].at[k], send.at[a, j], recv.at[a, j], (px, py, c)))
                blk = outs[a].at[2 * px + py]
                land.append(_rc(blk, blk, send.at[a, j], recv.at[a, j], (px, py, c)))
        return out, land

    def start(ins, outs, sems):
        for cp in copies(ins, outs, sems)[0]:
            cp.start()

    def finish(ins, outs, sems):
        out, land = copies(ins, outs, sems)
        for arrived in land:
            arrived.wait_recv()
        for cp in out:
            cp.wait_send()

    return Exchange(list(arrs), [_sds(a.shape, a.dtype) for a in arrs],
                    [pltpu.SemaphoreType.DMA((n, 3)), pltpu.SemaphoreType.DMA((n, 3))], start, finish)


def swap_cores(arrs, pick_other):
    n = len(arrs)

    def copies(ins, outs, sems):
        send, recv = sems
        x, y, c, _, _ = _place()
        return [_rc(ins[a].at[1 - c] if pick_other else ins[a], outs[a], send.at[a], recv.at[a], (x, y, 1 - c))
                for a in range(n)]

    def start(ins, outs, sems):
        for cp in copies(ins, outs, sems):
            cp.start()

    def finish(ins, outs, sems):
        for cp in copies(ins, outs, sems):
            cp.wait()

    shapes = [_sds(a.shape[1:] if pick_other else a.shape, a.dtype) for a in arrs]
    return Exchange(list(arrs), shapes, [pltpu.SemaphoreType.DMA((n,)), pltpu.SemaphoreType.DMA((n,))],
                    start, finish)


def join(*parts):
    parts = [p for p in parts if p is not None]
    if not parts:
        return None

    def split(refs, counts):
        out, pos = [], 0
        for cnt in counts:
            out.append(refs[pos:pos + cnt])
            pos += cnt
        return out

    n_in = [len(p.ins) for p in parts]
    n_out = [len(p.out_shapes) for p in parts]
    n_sem = [len(p.sems) for p in parts]

    def run(which):
        def go(ins, outs, sems):
            for p, i, o, s in zip(parts, split(ins, n_in), split(outs, n_out), split(sems, n_sem)):
                getattr(p, which)(i, o, s)
        return go

    return Exchange([a for p in parts for a in p.ins], [s for p in parts for s in p.out_shapes],
                    [s for p in parts for s in p.sems], run("start"), run("finish"))


def _pcall(body, *, out_shape, grid, in_specs, out_specs, args, name, scratch_shapes=(), ride=None, aliases=None):
    out_shape, out_specs, in_specs = tuple(out_shape), tuple(out_specs), list(in_specs)
    aliases = aliases or {}
    if ride is None:
        return pl.pallas_call(body, out_shape=out_shape, grid=grid, in_specs=in_specs, out_specs=out_specs,
                              scratch_shapes=list(scratch_shapes), input_output_aliases=aliases, name=name,
                              compiler_params=_cp(len(grid)))(*args)
    n_in, n_out, n_sc = len(args), len(out_shape), len(scratch_shapes)
    n_xi, n_xo = len(ride.ins), len(ride.out_shapes)

    def wrapped(*refs):
        pos = [0]

        def take(cnt):
            got = refs[pos[0]:pos[0] + cnt]
            pos[0] += cnt
            return got

        c_in, x_in, c_out, x_out, c_sc = take(n_in), take(n_xi), take(n_out), take(n_xo), take(n_sc)
        sems = refs[pos[0]:]
        first, last = True, True
        for d, size in enumerate(grid):
            first = jnp.logical_and(first, pl.program_id(d) == 0)
            last = jnp.logical_and(last, pl.program_id(d) == size - 1)

        @pl.when(first)
        def _():
            ride.start(x_in, x_out, sems)

        body(*c_in, *c_out, *c_sc)

        @pl.when(last)
        def _():
            ride.finish(x_in, x_out, sems)

    if ride.inplace:
        aliases = {**aliases, **{n_in + t: n_out + t for t in range(n_xi)}}
    res = pl.pallas_call(
        wrapped, out_shape=out_shape + tuple(ride.out_shapes), grid=grid,
        in_specs=in_specs + [ANY] * n_xi, out_specs=out_specs + (ANY,) * n_xo,
        scratch_shapes=list(scratch_shapes) + list(ride.sems), input_output_aliases=aliases, name=name,
        compiler_params=_cp(len(grid)))(*args, *ride.ins)
    return res[:n_out], res[n_out:]


def run_exchange(ex, name):
    n_xi, n_xo = len(ex.ins), len(ex.out_shapes)

    def body(*refs):
        ins, outs, sems = refs[:n_xi], refs[n_xi:n_xi + n_xo], refs[n_xi + n_xo:]
        ex.start(ins, outs, sems)
        ex.finish(ins, outs, sems)

    aliases = {t: t for t in range(n_xi)} if ex.inplace else {}
    return pl.pallas_call(body, out_shape=tuple(ex.out_shapes), in_specs=[ANY] * n_xi, out_specs=(ANY,) * n_xo,
                          scratch_shapes=list(ex.sems), input_output_aliases=aliases, name=name)(*ex.ins)


HBM_SPEC = pl.BlockSpec(memory_space=pltpu.HBM)
SEM_SPEC = pl.BlockSpec(memory_space=pltpu.SEMAPHORE)
EFFECT = pltpu.SideEffectType.DATAFLOW_SIDE_EFFECTING


def _route(kind, src, dst, c, k, peer):
    if kind == "gather":
        return src.at[c], dst.at[c, k], dst.at[c, peer]
    return src.at[peer], dst.at[k], dst.at[peer]


def split_start(batches, kind, name):
    flat = [a for batch in batches for a in batch]
    n, nb = len(flat), len(batches)
    lands = [lax.empty((2, N_CHIPS) + a.shape[1:] if kind == "gather" else a.shape, a.dtype) for a in flat]

    def body(*refs):
        srcs, dsts, sems, token = refs[:n], refs[n:2 * n], refs[2 * n:2 * n + 2 * nb], refs[-1]
        x, y, c, k, others = _place()
        pos = 0
        for b, batch in enumerate(batches):
            for a in range(len(batch)):
                for j, (px, py) in enumerate(others):
                    src, dst, _ = _route(kind, srcs[pos], dsts[pos], c, k, 2 * px + py)
                    _rc(src, dst, sems[2 * b].at[3 * a + j], sems[2 * b + 1].at[3 * a + j], (px, py, c)).start()
                pos += 1
        token[...] = jnp.zeros(token.shape, token.dtype)

    sem_shapes = [pltpu.SemaphoreType.DMA((3 * len(batch),)) for batch in batches for _ in range(2)]
    thru = [pltpu.HBM(a.shape, a.dtype) for a in flat] + [pltpu.HBM(l.shape, l.dtype) for l in lands]
    res = pl.pallas_call(
        body, name=name, out_shape=tuple(sem_shapes + thru + [_sds((8, LANES), F32)]),
        in_specs=[HBM_SPEC] * (2 * n),
        out_specs=tuple([SEM_SPEC] * (2 * nb) + [HBM_SPEC] * (2 * n) + [pl.BlockSpec(memory_space=pltpu.VMEM)]),
        input_output_aliases={t: 2 * nb + t for t in range(2 * n)},
        compiler_params=pltpu.CompilerParams(has_side_effects=EFFECT),
    )(*[pltpu.with_memory_space_constraint(t, pltpu.HBM) for t in flat + lands])
    sems, srcs, dsts = res[:2 * nb], res[2 * nb:2 * nb + n], res[2 * nb + n:2 * nb + 2 * n]
    out, pos = [], 0
    for b, batch in enumerate(batches):
        out.append((sems[2 * b], sems[2 * b + 1], list(srcs[pos:pos + len(batch)]), list(dsts[pos:pos + len(batch)])))
        pos += len(batch)
    return out, res[-1]


def split_arrive(handle, kind, after, name):
    send, recv, srcs, dsts = handle
    n = len(srcs)

    def body(*refs):
        s_refs, d_refs, send_ref, recv_ref = refs[:n], refs[n:2 * n], refs[2 * n], refs[2 * n + 1]
        x, y, c, k, others = _place()
        for a in range(n):
            for j, (px, py) in enumerate(others):
                src, _, landed = _route(kind, s_refs[a], d_refs[a], c, k, 2 * px + py)
                cp = _rc(src, landed, send_ref.at[3 * a + j], recv_ref.at[3 * a + j], (px, py, c))
                cp.wait_send()
                cp.wait_recv()

    res = pl.pallas_call(
        body, name=name, out_shape=tuple([pltpu.HBM(t.shape, t.dtype) for t in srcs + dsts]),
        in_specs=[HBM_SPEC] * (2 * n) + [SEM_SPEC, SEM_SPEC, ANY], out_specs=tuple([HBM_SPEC] * (2 * n)),
        input_output_aliases={t: t for t in range(2 * n)},
        compiler_params=pltpu.CompilerParams(has_side_effects=EFFECT),
    )(*srcs, *dsts, send, recv, after)
    return list(res[n:])


def forward_cores(bufs):
    n = len(bufs)

    def copies(outs, sems):
        send, recv = sems
        x, y, c, k, others = _place()
        onward, land = [], []
        for a in range(n):
            for j, (px, py) in enumerate(others):
                blk = outs[a].at[c, 2 * px + py]
                onward.append(_rc(blk, blk, send.at[a, j], recv.at[a, j], (x, y, 1 - c)))
                blk2 = outs[a].at[1 - c, 2 * px + py]
                land.append(_rc(blk2, blk2, send.at[a, j], recv.at[a, j], (x, y, 1 - c)))
        return onward, land

    def start(ins, outs, sems):
        for cp in copies(outs, sems)[0]:
            cp.start()

    def finish(ins, outs, sems):
        onward, land = copies(outs, sems)
        for arrived in land:
            arrived.wait_recv()
        for cp in onward:
            cp.wait_send()

    return Exchange(list(bufs), [_sds(b.shape, b.dtype) for b in bufs],
                    [pltpu.SemaphoreType.DMA((n, 3)), pltpu.SemaphoreType.DMA((n, 3))], start, finish, inplace=True)


def all_reduce_small(buf, name):
    r = buf.shape[0]
    n_dev = 8

    def body(in_ref, o_ref, land, send_sems, recv_sems):
        x, y, c, _, _ = _place()
        me = 4 * x + 2 * y + c
        land[me] = in_ref[...]
        sends = []
        for d in range(1, n_dev):
            peer = (x ^ (d >> 2), y ^ ((d >> 1) & 1), c ^ (d & 1))
            cp = _rc(in_ref, land.at[me], send_sems.at[d], recv_sems.at[d], peer)
            cp.start()
            sends.append(cp)
        for d in range(1, n_dev):
            blk = land.at[me ^ d]
            _rc(blk, blk, send_sems.at[d], recv_sems.at[d], (x, y, c)).wait_recv()
        for cp in sends:
            cp.wait_send()
        tot = land[0]
        for d in range(1, n_dev):
            tot = tot + land[d]
        o_ref[...] = tot

    vm = pl.BlockSpec(memory_space=pltpu.VMEM)
    return pl.pallas_call(
        body, out_shape=_sds(buf.shape, F32), in_specs=[vm], out_specs=vm,
        scratch_shapes=[pltpu.VMEM((n_dev, r, LANES), F32), pltpu.SemaphoreType.DMA((n_dev,)),
                        pltpu.SemaphoreType.DMA((n_dev,))],
        name=name)(buf)


def rmsnorm_fwd(x, w, name):
    s, d = x.shape
    tm = _row_tile(s, 512)

    def body(x_ref, w_ref, o_ref):
        o_ref[...] = _rms_fwd(x_ref[...], w_ref[...]).astype(BF16)

    return _pcall(body, out_shape=[_sds((s, d), BF16)], grid=(s // tm,),
                  in_specs=[pl.BlockSpec((tm, d), lambda i: (i, 0)), pl.BlockSpec((1, d), lambda i: (0, 0))],
                  out_specs=[pl.BlockSpec((tm, d), lambda i: (i, 0))], args=[x, w.reshape(1, d)], name=name)[0]


def _ffn_w_spec(chip_of, single=False):
    mode = dict(pipeline_mode=pl.Buffered(1)) if single else {}
    return pl.BlockSpec((N_CORES, 1, FF_PART, D_MODEL), lambda *ids: (0, chip_of(*ids), 0, 0), **mode)


def ffn_fwd(h, x, wg, wu, wd, norm_ws, name, ride=None):
    s, d = h.shape
    n_norm = len(norm_ws)
    tm = _row_tile(s, 1024)

    def body(*refs):
        h_ref, x_ref, wg_ref, wu_ref, wd_ref = refs[:5]
        nw_refs = refs[5:5 + n_norm]
        o_ref = refs[5 + n_norm]
        h_refs = refs[6 + n_norm:6 + 2 * n_norm]
        gu_ref, acc = refs[6 + 2 * n_norm], refs[7 + 2 * n_norm]
        k = pl.program_id(1)

        @pl.when(k == 0)
        def _():
            acc[...] = jnp.zeros(acc.shape, F32)

        hm = tm // 2
        for part in range(2):
            sub = pl.ds(part * hm, hm)
            hb = h_ref[sub, :]
            g = _dot(hb, wg_ref[...].reshape(FF_SHARD, d), NT)
            u = _dot(hb, wu_ref[...].reshape(FF_SHARD, d), NT)
            gu_ref[0, 0, sub, :] = g.astype(BF16)
            gu_ref[0, 1, sub, :] = u.astype(BF16)
            acc[sub, :] += _dot((g * _sigmoid(g) * u).astype(BF16), wd_ref[...].reshape(FF_SHARD, d))

        @pl.when(k == N_CHIPS - 1)
        def _():
            xn = x_ref[...] + 0.5 * acc[...]
            o_ref[...] = xn
            for nw_ref, hn_ref in zip(nw_refs, h_refs):
                hn_ref[...] = _rms_fwd(xn, nw_ref[...]).astype(BF16)

    row = pl.BlockSpec((tm, d), lambda i, k: (i, 0))
    vec = pl.BlockSpec((1, d), lambda i, k: (0, 0))
    wsp = _ffn_w_spec(lambda i, k: k)
    return _pcall(
        body, out_shape=[_sds((s, d), F32)] + [_sds((s, d), BF16)] * n_norm + [_sds((N_CHIPS, 2, s, FF_SHARD), BF16)],
        grid=(s // tm, N_CHIPS),
        in_specs=[row, row, wsp, wsp, wsp] + [vec] * n_norm,
        out_specs=[row] * (1 + n_norm) + [pl.BlockSpec((1, 2, tm, FF_SHARD), lambda i, k: (k, 0, i, 0))],
        scratch_shapes=[pltpu.VMEM((tm, d), F32)],
        args=[h, x, wg, wu, wd] + [nw.reshape(1, d) for nw in norm_ws], name=name, ride=ride)


def ffn_bwd(dxn, h, x_in, nw, gu, wg, wu, wd, name, ride=None):
    s, d = h.shape
    tm = _row_tile(s, 512)
    ni = s // tm
    last_e = N_CHIPS - 1

    def body(dxn_ref, h_ref, x_ref, nw_ref, gu_ref, wg_ref, wu_ref, wd_ref,
             dx_ref, dnw_ref, dwg_ref, dwu_ref, dwd_ref, dh, wacc):
        e = pl.program_id(0)
        i = pl.program_id(1)
        rows = pl.ds(pl.multiple_of(i * tm, tm), tm)

        @pl.when(i == 0)
        def _():
            wacc[...] = jnp.zeros(wacc.shape, F32)

        @pl.when(e == 0)
        def _():
            dh[rows, :] = jnp.zeros((tm, d), F32)

        hm = tm // 2
        for part in range(2):
            sub = pl.ds(part * hm, hm)
            dxb = dxn_ref[sub, :].astype(BF16)
            hb = h_ref[sub, :]
            g = gu_ref[0, 0, sub, :].astype(F32)
            u = gu_ref[0, 1, sub, :].astype(F32)
            drows = pl.ds(pl.multiple_of(i * tm + part * hm, hm), hm)
            sg = _sigmoid(g)
            silu = g * sg
            wacc[2] += _dot((0.5 * silu * u).astype(BF16), dxb, TN)
            da = 0.5 * _dot(dxb, wd_ref[...].reshape(FF_SHARD, d), NT)
            dg = (da * u * (sg * (1.0 + g * (1.0 - sg)))).astype(BF16)
            wacc[0] += _dot(dg, hb, TN)
            du = (da * silu).astype(BF16)
            dh[drows, :] += _dot(dg, wg_ref[...].reshape(FF_SHARD, d))
            wacc[1] += _dot(du, hb, TN)
            dh[drows, :] += _dot(du, wu_ref[...].reshape(FF_SHARD, d))

        @pl.when(i == ni - 1)
        def _():
            for t, dw_ref in enumerate((dwg_ref, dwu_ref, dwd_ref)):
                dw_ref[...] = wacc[t].astype(BF16).reshape(N_CORES, 1, FF_PART, d)

        @pl.when(e == last_e)
        def _():
            dx, dnw = _rms_bwd(dh[rows, :], x_ref[...], nw_ref[...])
            dx_ref[...] = dxn_ref[...] + dx
            col = jnp.sum(dnw, axis=0, keepdims=True)

            @pl.when(i == 0)
            def _():
                dnw_ref[...] = col

            @pl.when(i > 0)
            def _():
                dnw_ref[...] += col

    row = pl.BlockSpec((tm, d), lambda e, i: (i, 0))
    late = pl.BlockSpec((tm, d), lambda e, i: (jnp.where(e == last_e, i, 0), 0))
    vec = pl.BlockSpec((1, d), lambda e, i: (0, 0))
    wsp = _ffn_w_spec(lambda e, i: e, single=True)
    dwsp = _ffn_w_spec(lambda e, i: e, single=True)
    dw = _sds((N_CORES, N_CHIPS, FF_PART, d), BF16)
    return _pcall(
        body, out_shape=[_sds((s, d), F32), _sds((1, d), F32), dw, dw, dw],
        grid=(N_CHIPS, ni),
        in_specs=[row, row, late, vec, pl.BlockSpec((1, 2, tm, FF_SHARD), lambda e, i: (e, 0, i, 0)), wsp, wsp, wsp],
        out_specs=[late, vec, dwsp, dwsp, dwsp],
        scratch_shapes=[pltpu.VMEM((s, d), F32), pltpu.VMEM((3, FF_SHARD, d), F32)],
        args=[dxn, h, x_in, nw.reshape(1, d), gu, wg, wu, wd], name=name, ride=ride)


def mm_res(a, w, x, name, bias=None, norm_ws=(), ride=None):
    s, k = a.shape
    n = w.shape[1]
    tm = _row_tile(s, 256)
    has_bias = bias is not None
    n_norm = len(norm_ws)

    def body(*refs):
        a_ref, w_ref, x_ref = refs[:3]
        pos = 3
        t = _dot(a_ref[...], w_ref[...])
        if has_bias:
            t = t + refs[pos][...]
            pos += 1
        nw_refs = refs[pos:pos + n_norm]
        o_ref = refs[pos + n_norm]
        h_refs = refs[pos + n_norm + 1:]
        xn = x_ref[...] + t
        o_ref[...] = xn
        for nw_ref, h_ref in zip(nw_refs, h_refs):
            h_ref[...] = _rms_fwd(xn, nw_ref[...]).astype(BF16)

    row = pl.BlockSpec((tm, n), lambda i: (i, 0))
    vec = pl.BlockSpec((1, n), lambda i: (0, 0))
    in_specs = [pl.BlockSpec((tm, k), lambda i: (i, 0)), pl.BlockSpec((k, n), lambda i: (0, 0)), row]
    args = [a, w, x]
    if has_bias:
        in_specs.append(vec)
        args.append(bias.reshape(1, n))
    for nw in norm_ws:
        in_specs.append(vec)
        args.append(nw.reshape(1, n))
    return _pcall(body, out_shape=[_sds((s, n), F32)] + [_sds((s, n), BF16)] * n_norm, grid=(s // tm,),
                  in_specs=in_specs, out_specs=[row] * (1 + n_norm), args=args, name=name, ride=ride)


def mm_nn(a, w, name, bias=None, out_dtype=F32):
    s, k = a.shape
    n = w.shape[1]
    tm = _row_tile(s, 512)
    tn = _col_tile(n)
    has_bias = bias is not None

    def body(*refs):
        a_ref, w_ref = refs[:2]
        o_ref = refs[-1]
        t = _dot(a_ref[...], w_ref[...])
        if has_bias:
            t = t + refs[2][...]
        o_ref[...] = t.astype(out_dtype)

    in_specs = [pl.BlockSpec((tm, k), lambda j, i: (i, 0)), pl.BlockSpec((k, tn), lambda j, i: (0, j))]
    args = [a, w]
    if has_bias:
        in_specs.append(pl.BlockSpec((1, tn), lambda j, i: (0, j)))
        args.append(bias.reshape(1, n))
    return _pcall(body, out_shape=[_sds((s, n), out_dtype)], grid=(n // tn, s // tm), in_specs=in_specs,
                  out_specs=[pl.BlockSpec((tm, tn), lambda j, i: (i, j))], args=args, name=name)[0]


def mm_nt(a, w, name, n=None, row0=0, out_dtype=F32, ride=None):
    s, k = a.shape
    n = w.shape[0] if n is None else n
    tm = _row_tile(s, 512)
    tn = _col_tile(n)
    base = row0 // tn
    assert row0 % tn == 0

    def body(a_ref, w_ref, o_ref):
        o_ref[...] = _dot(a_ref[...].astype(BF16), w_ref[...], NT).astype(out_dtype)

    res = _pcall(body, out_shape=[_sds((s, n), out_dtype)], grid=(n // tn, s // tm),
                 in_specs=[pl.BlockSpec((tm, k), lambda j, i: (i, 0)), pl.BlockSpec((tn, k), lambda j, i: (base + j, 0))],
                 out_specs=[pl.BlockSpec((tm, tn), lambda j, i: (i, j))], args=[a, w], name=name, ride=ride)
    return res[0] if ride is None else (res[0][0], res[1])


def mm_tn(a, b, name, into=None, rows=None, row0=0, m_valid=None, col_sum=False, ride=None):
    s, m = a.shape
    n = b.shape[1]
    mv = m if m_valid is None else m_valid
    tm = _col_tile(m) if m_valid is None else mv
    tn = n if n <= 1024 else _col_tile(n)
    rows = mv if rows is None else rows
    assert row0 % tm == 0 and (m_valid is None or m == LANES)
    assert not col_sum or mv == tm
    base = row0 // tm
    ta = m if m_valid is not None else tm

    def body(*refs):
        a_ref, b_ref = refs[0], refs[1]
        o_ref = refs[-2] if col_sum else refs[-1]
        bf = b_ref[...]
        t = _dot(a_ref[...].astype(BF16), bf.astype(BF16), TN)
        o_ref[...] = t[:tm].astype(BF16)
        if col_sum:
            refs[-1][...] = jnp.sum(bf.astype(F32), axis=0, keepdims=True)

    in_specs = [pl.BlockSpec((s, ta), lambda i, j: (0, i)), pl.BlockSpec((s, tn), lambda i, j: (0, j))]
    args = [a, b]
    aliases = None
    if into is not None:
        in_specs.append(ANY)
        args.append(into)
        aliases = {2: 0}
    out_shape = [_sds((rows, n), BF16)]
    out_specs = [pl.BlockSpec((tm, tn), lambda i, j: (base + i, j))]
    if col_sum:
        out_shape.append(_sds((1, n), F32))
        out_specs.append(pl.BlockSpec((1, tn), lambda i, j: (0, j)))
    res = _pcall(body, out_shape=out_shape, grid=(mv // tm, n // tn), in_specs=in_specs, out_specs=out_specs,
                 args=args, name=name, ride=ride, aliases=aliases)
    outs = res if ride is None else res[0]
    out = (outs[0], outs[1][0]) if col_sum else outs[0]
    return out if ride is None else (out, res[1])


def mm_rms_bwd(terms, dxn, x, nw, name, ride=None):
    s, n = x.shape
    nt_ = len(terms)
    tm = _row_tile(s, 256)
    forms = [t[5] for t in terms]

    def body(*refs):
        dxn_ref, x_ref, nw_ref, dx_ref, dnw_ref = refs[2 * nt_:]
        i = pl.program_id(0)
        dh = None
        for t in range(nt_):
            part = _dot(refs[2 * t][...].astype(BF16), refs[2 * t + 1][...], NN if forms[t] == "nn" else NT)
            dh = part if dh is None else dh + part
        dx, dnw = _rms_bwd(dh, x_ref[...], nw_ref[...])
        dx_ref[...] = dxn_ref[...] + dx
        col = jnp.sum(dnw, axis=0, keepdims=True)

        @pl.when(i == 0)
        def _():
            dnw_ref[...] = col

        @pl.when(i > 0)
        def _():
            dnw_ref[...] += col

    in_specs, args = [], []
    for a, cb, w, rb, kb, form in terms:
        in_specs.append(pl.BlockSpec((tm, kb), lambda i, cb=cb: (i, cb)))
        if form == "nn":
            in_specs.append(pl.BlockSpec((kb, n), lambda i, rb=rb: (rb, 0)))
        else:
            in_specs.append(pl.BlockSpec((n, kb), lambda i, rb=rb: (0, rb)))
        args += [a, w]
    row = pl.BlockSpec((tm, n), lambda i: (i, 0))
    vec = pl.BlockSpec((1, n), lambda i: (0, 0))
    res = _pcall(body, out_shape=[_sds((s, n), F32), _sds((1, n), F32)], grid=(s // tm,),
                 in_specs=in_specs + [row, row, vec], out_specs=[row, vec],
                 args=args + [dxn, x, nw.reshape(1, n)], name=name, ride=ride)
    outs = res if ride is None else res[0]
    out = (outs[0], outs[1][0])
    return out if ride is None else (out, res[1])


def rope_tables(s):
    pos = jnp.arange(s, dtype=F32)
    inv = 1.0 / (ROPE_THETA ** (jnp.arange(0, ATT_HEAD_DIM, 2, dtype=F32) / ATT_HEAD_DIM))
    ang = pos[:, None] * inv[None, :]
    cos = jnp.tile(jnp.cos(ang), (1, 2 * LANES // ATT_HEAD_DIM))
    sin = jnp.tile(jnp.sin(ang), (1, 2 * LANES // ATT_HEAD_DIM))
    return cos, sin


def rope_apply(t, cos, sin, name, inverse=False, scale=1.0, out_dtype=BF16):
    s, n = t.shape
    tm = _row_tile(s, 512)
    half = ATT_HEAD_DIM // 2
    reps = n // LANES

    def body(t_ref, c_ref, s_ref, o_ref):
        tf = t_ref[...].astype(F32)
        c = jnp.tile(c_ref[...], (1, reps))
        sn = jnp.tile(s_ref[...], (1, reps))
        lane = lax.broadcasted_iota(jnp.int32, tf.shape, 1)
        first = (lane & (ATT_HEAD_DIM - 1)) < half
        rot = jnp.where(first, -pltpu.roll(tf, n - half, 1), pltpu.roll(tf, half, 1))
        sign = -1.0 if inverse else 1.0
        o_ref[...] = (scale * (tf * c + sign * rot * sn)).astype(out_dtype)

    tab = pl.BlockSpec((tm, LANES), lambda i: (i, 0))
    return _pcall(body, out_shape=[_sds((s, n), out_dtype)], grid=(s // tm,),
                  in_specs=[pl.BlockSpec((tm, n), lambda i: (i, 0)), tab, tab],
                  out_specs=[pl.BlockSpec((tm, n), lambda i: (i, 0))], args=[t, cos, sin], name=name)[0]


CONV_TILE = 256


def _shift_down(u, k):
    if k == 0:
        return u
    row = lax.broadcasted_iota(jnp.int32, u.shape, 0)
    return jnp.where(row >= k, pltpu.roll(u, k, 0), 0.0)


def _shift_up(u, k):
    if k == 0:
        return u
    s = u.shape[0]
    row = lax.broadcasted_iota(jnp.int32, u.shape, 0)
    return jnp.where(row < s - k, pltpu.roll(u, s - k, 0), 0.0)


def _conv_taps(u):
    return [_shift_down(u, CONV_WIDTH - 1 - k) for k in range(CONV_WIDTH)]


def _conv_pre(taps, w_ref, b_ref):
    pre = b_ref[...] + w_ref[0:1, :] * taps[0]
    for k in range(1, CONV_WIDTH):
        pre += w_ref[k:k + 1, :] * taps[k]
    return pre


def conv_fwd(u, w, b, name, ride=None):
    s, c = u.shape

    def body(u_ref, w_ref, b_ref, o_ref):
        pre = _conv_pre(_conv_taps(u_ref[...]), w_ref, b_ref)
        o_ref[...] = pre * _sigmoid(pre)

    col = pl.BlockSpec((s, CONV_TILE), lambda j: (0, j))
    res = _pcall(body, out_shape=[_sds((s, c), F32)], grid=(c // CONV_TILE,),
                 in_specs=[col, pl.BlockSpec((CONV_WIDTH, CONV_TILE), lambda j: (0, j)),
                           pl.BlockSpec((1, CONV_TILE), lambda j: (0, j))],
                 out_specs=[col], args=[u, w, b.reshape(1, c)], name=name, ride=ride)
    return res[0] if ride is None else (res[0][0], res[1])


def conv_bwd(dxs, db_, dc_, u, w, b, name):
    s, c = u.shape
    n_x = dxs.shape[1] // CONV_TILE
    n_b = db_.shape[1] // CONV_TILE

    def body(dx_ref, dbb_ref, dcc_ref, u_ref, w_ref, b_ref, du_ref, dw_ref, dbias_ref):
        j = pl.program_id(0)
        dact = jnp.where(j < n_x, dx_ref[...], jnp.where(j < n_x + n_b, dbb_ref[...], dcc_ref[...]))
        taps = _conv_taps(u_ref[...])
        pre = _conv_pre(taps, w_ref, b_ref)
        sg = _sigmoid(pre)
        dpre = dact * (sg * (1.0 + pre * (1.0 - sg)))
        du = w_ref[CONV_WIDTH - 1:CONV_WIDTH, :] * dpre
        for k in range(CONV_WIDTH - 1):
            du += w_ref[k:k + 1, :] * _shift_up(dpre, CONV_WIDTH - 1 - k)
        du_ref[...] = du
        dbias_ref[...] = jnp.sum(dpre, axis=0, keepdims=True)
        for k in range(CONV_WIDTH):
            dw_ref[k:k + 1, :] = jnp.sum(dpre * taps[k], axis=0, keepdims=True)

    col = pl.BlockSpec((s, CONV_TILE), lambda j: (0, j))
    wsp = pl.BlockSpec((CONV_WIDTH, CONV_TILE), lambda j: (0, j))
    bsp = pl.BlockSpec((1, CONV_TILE), lambda j: (0, j))
    du, dw, db = _pcall(
        body, out_shape=[_sds((s, c), F32), _sds((CONV_WIDTH, c), F32), _sds((1, c), F32)], grid=(c // CONV_TILE,),
        in_specs=[pl.BlockSpec((s, CONV_TILE), lambda j: (0, jnp.minimum(j, n_x - 1))),
                  pl.BlockSpec((s, CONV_TILE), lambda j: (0, jnp.clip(j - n_x, 0, n_b - 1))),
                  pl.BlockSpec((s, CONV_TILE), lambda j: (0, jnp.clip(j - n_x - n_b, 0, n_b - 1))),
                  col, wsp, bsp],
        out_specs=[col, wsp, bsp], args=[dxs, db_, dc_, u, w, b.reshape(1, c)], name=name)
    return du, dw, db[0]


def _lane_pick(mat, idx):
    lane = lax.broadcasted_iota(jnp.int32, mat.shape, 1)
    return jnp.sum(jnp.where(lane == idx, mat, 0.0), axis=1, keepdims=True)


def _sub_pick(mat, idx):
    sub = lax.broadcasted_iota(jnp.int32, mat.shape, 0)
    return jnp.sum(jnp.where(sub == idx, mat, 0.0), axis=0, keepdims=True)


def _expand_heads(cols):
    rows = cols[0].shape[0]
    left = lax.broadcasted_iota(jnp.int32, (rows, LANES), 1) < SSM_HEAD_DIM
    return jnp.concatenate(
        [jnp.where(left, cols[2 * p], cols[2 * p + 1]) for p in range(HEADS_PER_GROUP // 2)], axis=1)


def _dot_01(x, ones, ones_first, pieces):
    tot, rest = None, x
    for _ in range(pieces):
        piece = rest.astype(BF16)
        rest = rest - piece.astype(F32)
        part = _dot(ones, piece) if ones_first else _dot(piece, ones)
        tot = part if tot is None else tot + part
    return tot


def _heads_to_lanes(mat, g):
    jj = lax.broadcasted_iota(jnp.int32, (GROUP_DIM, LANES), 0)
    ll = lax.broadcasted_iota(jnp.int32, (GROUP_DIM, LANES), 1)
    sel = (ll == HEADS_PER_GROUP * g + (jj >> 6)).astype(BF16)
    return _dot_01(mat, sel, False, 3)


def _softplus(x):
    return jnp.maximum(x, 0.0) + jnp.log1p(jnp.exp(-jnp.abs(x)))


def _ssd_scalars(dt_ref, bias_ref, a_ref, dtall, csall, cst):
    dta = _softplus(dt_ref[...] + bias_ref[...])
    row = lax.broadcasted_iota(jnp.int32, (CHUNK, CHUNK), 0)
    col = lax.broadcasted_iota(jnp.int32, (CHUNK, CHUNK), 1)
    cs = _dot_01(dta * a_ref[...], (row >= col).astype(BF16), True, 3)
    dtall[...] = dta
    csall[...] = cs
    cst[...] = cs.T


def _decay_mat(cs_col, cs_row):
    row = lax.broadcasted_iota(jnp.int32, (CHUNK, CHUNK), 0)
    col = lax.broadcasted_iota(jnp.int32, (CHUNK, CHUNK), 1)
    return jnp.exp(jnp.where(row >= col, cs_col - cs_row, NEG))


def _head_mask(xpair, right):
    lane = lax.broadcasted_iota(jnp.int32, xpair.shape, 1)
    keep = (lane >= SSM_HEAD_DIM) if right else (lane < SSM_HEAD_DIM)
    return jnp.where(keep, xpair, 0.0)


def _chunk_cols(x_all, g):
    return [_lane_pick(x_all, HEADS_PER_GROUP * g + r) for r in range(HEADS_PER_GROUP)]


def _decay_col(cs_cols):
    return jnp.concatenate(
        [jnp.broadcast_to(jnp.exp(cc[CHUNK - 1:CHUNK, :]), (SSM_HEAD_DIM, 1)) for cc in cs_cols], axis=0)


def ssd_fwd(act, z, dtp, bias_p, a_p, d_p, normw, name, ride=None):
    s = act.shape[0]
    nc = s // CHUNK
    b_off = D_INNER // SSM_STATE
    c_off = b_off + SSM_GROUPS

    def body(xs_ref, b_ref, c_ref, z_ref, dt_ref, bias_ref, a_ref, d_ref, nw_ref,
             yn_ref, y_ref, st_ref, state, dtall, csall, cst):
        c = pl.program_id(0)
        g = pl.program_id(1)

        @pl.when(g == 0)
        def _():
            _ssd_scalars(dt_ref, bias_ref, a_ref, dtall, csall, cst)

        @pl.when(c == 0)
        def _():
            state[g] = jnp.zeros((GROUP_DIM, SSM_STATE), F32)

        cs_cols = _chunk_cols(csall[...], g)
        dt_cols = _chunk_cols(dtall[...], g)
        cs_rows = [_sub_pick(cst[...], HEADS_PER_GROUP * g + r) for r in range(HEADS_PER_GROUP)]
        d_cols = _chunk_cols(d_ref[...], g)
        cs_exp = _expand_heads(cs_cols)
        dt_exp = _expand_heads(dt_cols)
        d_exp = _expand_heads(d_cols)
        xs = xs_ref[...]
        bb = b_ref[...].astype(BF16)
        cb16 = c_ref[...].astype(BF16)
        xdt = xs * dt_exp
        s_prev = state[g]
        st_ref[0, 0] = s_prev
        y_off = _dot(cb16, s_prev.astype(BF16), NT) * jnp.exp(cs_exp)
        decay_st = jnp.exp(cs_exp[CHUNK - 1:CHUNK, :] - cs_exp)
        contrib = _dot((xdt * decay_st).astype(BF16), bb, TN)
        state[g] = _decay_col(cs_cols) * s_prev + contrib
        cbm = _dot(cb16, bb, NT)
        pairs = []
        for p in range(HEADS_PER_GROUP // 2):
            xpair = xdt[:, LANES * p:LANES * (p + 1)]
            m0 = (cbm * _decay_mat(cs_cols[2 * p], cs_rows[2 * p])).astype(BF16)
            m1 = (cbm * _decay_mat(cs_cols[2 * p + 1], cs_rows[2 * p + 1])).astype(BF16)
            pairs.append(_dot(m0, _head_mask(xpair, False).astype(BF16))
                         + _dot(m1, _head_mask(xpair, True).astype(BF16)))
        y = jnp.concatenate(pairs, axis=1) + y_off + xs * d_exp
        y_ref[...] = y
        zf = z_ref[...]
        yg = y * (zf * _sigmoid(zf))
        yn_ref[...] = _rms_fwd(yg, nw_ref[...]).astype(BF16)

    grp = pl.BlockSpec((CHUNK, GROUP_DIM), lambda c, g: (c, g))
    par = pl.BlockSpec((1, LANES), lambda c, g: (0, 0))
    return _pcall(
        body,
        out_shape=[_sds((s, D_INNER), BF16), _sds((s, D_INNER), F32),
                   _sds((nc, SSM_GROUPS, GROUP_DIM, SSM_STATE), F32)],
        grid=(nc, SSM_GROUPS),
        in_specs=[grp,
                  pl.BlockSpec((CHUNK, SSM_STATE), lambda c, g: (c, b_off + g)),
                  pl.BlockSpec((CHUNK, SSM_STATE), lambda c, g: (c, c_off + g)),
                  grp,
                  pl.BlockSpec((CHUNK, LANES), lambda c, g: (c, 0)),
                  par, par, par,
                  pl.BlockSpec((1, GROUP_DIM), lambda c, g: (0, g))],
        out_specs=[grp, grp, pl.BlockSpec((1, 1, GROUP_DIM, SSM_STATE), lambda c, g: (c, g, 0, 0))],
        scratch_shapes=[pltpu.VMEM((SSM_GROUPS, GROUP_DIM, SSM_STATE), F32),
                        pltpu.VMEM((CHUNK, LANES), F32), pltpu.VMEM((CHUNK, LANES), F32),
                        pltpu.VMEM((LANES, CHUNK), F32)],
        args=[act, act, act, z, dtp, bias_p, a_p, d_p, normw], name=name, ride=ride)


def ssd_bwd(dyn, act, z, y_pre, states, dtp, bias_p, a_p, d_p, normw, name, ride=None):
    s = act.shape[0]
    nc = s // CHUNK
    b_off = D_INNER // SSM_STATE
    c_off = b_off + SSM_GROUPS

    def body(dyn_ref, xs_ref, b_ref, c_ref, z_ref, y_ref, st_ref, dt_ref, bias_ref, a_ref, d_ref, nw_ref,
             dxs_ref, db_ref, dc_ref, dz_ref, ddt_ref, dnw_ref, dbias_ref, da_ref, dd_ref,
             dstate, dtall, csall, cst):
        c = pl.program_id(0)
        g = pl.program_id(1)

        @pl.when(g == 0)
        def _():
            _ssd_scalars(dt_ref, bias_ref, a_ref, dtall, csall, cst)
            ddt_ref[...] = jnp.zeros((CHUNK, LANES), F32)

        @pl.when(c == 0)
        def _():
            dstate[g] = jnp.zeros((GROUP_DIM, SSM_STATE), F32)

        @pl.when(jnp.logical_and(c == 0, g == 0))
        def _():
            dnw_ref[...] = jnp.zeros(dnw_ref.shape, F32)
            dbias_ref[...] = jnp.zeros((1, LANES), F32)
            da_ref[...] = jnp.zeros((1, LANES), F32)
            dd_ref[...] = jnp.zeros((1, LANES), F32)

        cs_cols = _chunk_cols(csall[...], g)
        dt_cols = _chunk_cols(dtall[...], g)
        cs_rows = [_sub_pick(cst[...], HEADS_PER_GROUP * g + r) for r in range(HEADS_PER_GROUP)]
        d_cols = _chunk_cols(d_ref[...], g)
        cs_exp = _expand_heads(cs_cols)
        dt_exp = _expand_heads(dt_cols)
        d_exp = _expand_heads(d_cols)
        xs = xs_ref[...]
        bb = b_ref[...].astype(BF16)
        cb16 = c_ref[...].astype(BF16)
        xdt = xs * dt_exp
        s_prev = st_ref[0, 0]
        s_prev16 = s_prev.astype(BF16)
        ds_next = dstate[g]
        ds16 = ds_next.astype(BF16)

        zf = z_ref[...]
        sz = _sigmoid(zf)
        silu_z = zf * sz
        y = y_ref[...]
        yg = y * silu_z
        dout = dyn_ref[...]
        dyg, dnw = _rms_bwd(dout, yg, nw_ref[...])
        dnw_ref[pl.ds(g, 1), :] += jnp.sum(dnw, axis=0, keepdims=True)
        dy = dyg * silu_z
        dz_ref[...] = dyg * y * (sz * (1.0 + zf * (1.0 - sz)))
        dd_ref[...] += jnp.sum(_heads_to_lanes(dy * xs, g), axis=0, keepdims=True)

        exp_cs = jnp.exp(cs_exp)
        decay_st = jnp.exp(cs_exp[CHUNK - 1:CHUNK, :] - cs_exp)
        cs_t = _dot(cb16, s_prev16, NT)
        dyo = dy * exp_cs
        dc_acc = _dot(dyo.astype(BF16), s_prev16, NN)
        g1 = _dot(bb, ds16, NT)
        xds = xdt * decay_st
        db_acc = _dot(xds.astype(BF16), ds16, NN)
        dxdt_off = g1 * decay_st
        t_exp = g1 * xds
        dcs_exp = dy * cs_t * exp_cs - t_exp
        decay_c = _decay_col(cs_cols)
        dstate[g] = decay_c * ds_next + _dot(dyo.astype(BF16), cb16, TN)
        dlast_col = jnp.sum(ds_next * s_prev, axis=1, keepdims=True) * decay_c
        jj = lax.broadcasted_iota(jnp.int32, (GROUP_DIM, LANES), 0)
        ll = lax.broadcasted_iota(jnp.int32, (GROUP_DIM, LANES), 1)
        sel = ll == HEADS_PER_GROUP * g + (jj >> 6)
        dlast = jnp.sum(jnp.where(sel, dlast_col, 0.0), axis=0, keepdims=True)
        t_all = _heads_to_lanes(t_exp, g)
        dlast += jnp.sum(t_all, axis=0, keepdims=True)
        dcs_all = _heads_to_lanes(dcs_exp, g)

        cbm = _dot(cb16, bb, NT)
        dcb = jnp.zeros((CHUNK, CHUNK), F32)
        dcs_rows = jnp.zeros((LANES, CHUNK), F32)
        lane_l = lax.broadcasted_iota(jnp.int32, (CHUNK, LANES), 1)
        sub_l = lax.broadcasted_iota(jnp.int32, (LANES, CHUNK), 0)
        dxdt_pairs = []
        for p in range(HEADS_PER_GROUP // 2):
            xpair16 = xdt[:, LANES * p:LANES * (p + 1)].astype(BF16)
            dypair = dy[:, LANES * p:LANES * (p + 1)]
            acc = None
            for r in (2 * p, 2 * p + 1):
                lm = _decay_mat(cs_cols[r], cs_rows[r])
                m = cbm * lm
                dyh = _head_mask(dypair, r % 2 == 1).astype(BF16)
                dm = _dot(dyh, xpair16, NT)
                dcb += dm * lm
                q = dm * m
                idx = HEADS_PER_GROUP * g + r
                dcs_all += jnp.where(lane_l == idx, jnp.sum(q, axis=1, keepdims=True), 0.0)
                dcs_rows -= jnp.where(sub_l == idx, jnp.sum(q, axis=0, keepdims=True), 0.0)
                part = _dot(m.astype(BF16), dyh, TN)
                acc = part if acc is None else acc + part
            dxdt_pairs.append(acc)
        dxdt = jnp.concatenate(dxdt_pairs, axis=1) + dxdt_off
        dcb16 = dcb.astype(BF16)
        dc_ref[...] = dc_acc + _dot(dcb16, bb, NN)
        db_ref[...] = db_acc + _dot(dcb16, cb16, TN)
        dxs_ref[...] = dxdt * dt_exp + dy * d_exp

        dcs_all += dcs_rows.T
        row = lax.broadcasted_iota(jnp.int32, (CHUNK, CHUNK), 0)
        col = lax.broadcasted_iota(jnp.int32, (CHUNK, CHUNK), 1)
        last_row = lax.broadcasted_iota(jnp.int32, (CHUNK, LANES), 0) == CHUNK - 1
        dcs_all += jnp.where(last_row, dlast, 0.0)
        da_all = _dot_01(dcs_all, (col >= row).astype(BF16), True, 3)
        dta = dtall[...]
        in_group = jnp.logical_and(lane_l >= HEADS_PER_GROUP * g, lane_l < HEADS_PER_GROUP * (g + 1))
        ddt = jnp.where(in_group, da_all * a_ref[...] + _heads_to_lanes(dxdt * xs, g), 0.0)
        da_ref[...] += jnp.sum(jnp.where(in_group, da_all * dta, 0.0), axis=0, keepdims=True)
        ddt_raw = ddt * _sigmoid(dt_ref[...] + bias_ref[...])
        ddt_ref[...] += ddt_raw
        dbias_ref[...] += jnp.sum(ddt_raw, axis=0, keepdims=True)

    rev = lambda c, g: (nc - 1 - c, g)
    grp = pl.BlockSpec((CHUNK, GROUP_DIM), rev)
    st = pl.BlockSpec((CHUNK, SSM_STATE), rev)
    par = pl.BlockSpec((1, LANES), lambda c, g: (0, 0))
    dtb = pl.BlockSpec((CHUNK, LANES), lambda c, g: (nc - 1 - c, 0))
    f = lambda shape: _sds(shape, F32)
    return _pcall(
        body,
        out_shape=[f((s, D_INNER)), f((s, SSM_GROUPS * SSM_STATE)), f((s, SSM_GROUPS * SSM_STATE)),
                   f((s, D_INNER)), f((s, LANES)), f((8, GROUP_DIM)), f((1, LANES)), f((1, LANES)), f((1, LANES))],
        grid=(nc, SSM_GROUPS),
        in_specs=[grp, grp,
                  pl.BlockSpec((CHUNK, SSM_STATE), lambda c, g: (nc - 1 - c, b_off + g)),
                  pl.BlockSpec((CHUNK, SSM_STATE), lambda c, g: (nc - 1 - c, c_off + g)),
                  grp, grp,
                  pl.BlockSpec((1, 1, GROUP_DIM, SSM_STATE), lambda c, g: (nc - 1 - c, g, 0, 0)),
                  dtb, par, par, par,
                  pl.BlockSpec((1, GROUP_DIM), lambda c, g: (0, g))],
        out_specs=[grp, st, st, grp, dtb, pl.BlockSpec((8, GROUP_DIM), lambda c, g: (0, 0)), par, par, par],
        scratch_shapes=[pltpu.VMEM((SSM_GROUPS, GROUP_DIM, SSM_STATE), F32),
                        pltpu.VMEM((CHUNK, LANES), F32), pltpu.VMEM((CHUNK, LANES), F32),
                        pltpu.VMEM((LANES, CHUNK), F32)],
        args=[dyn, act, act, act, z, y_pre, states, dtp, bias_p, a_p, d_p, normw], name=name, ride=ride)


def _attn_probs(q, kp, kc, sink, n):
    sp = _dot(q, kp, NT)
    sc = _dot(q, kc, NT)
    i = lax.broadcasted_iota(jnp.int32, sp.shape, 0) & (WINDOW - 1)
    j = lax.broadcasted_iota(jnp.int32, sp.shape, 1)
    sp = jnp.where(jnp.logical_and(j > i, n > 0), sp, NEG)
    sc = jnp.where(j <= i, sc, NEG)
    m = jnp.maximum(jnp.maximum(jnp.max(sp, axis=1, keepdims=True), jnp.max(sc, axis=1, keepdims=True)), sink)
    pp = jnp.exp(sp - m)
    pc = jnp.exp(sc - m)
    ps = jnp.exp(sink - m)
    inv = 1.0 / (jnp.sum(pp, axis=1, keepdims=True) + jnp.sum(pc, axis=1, keepdims=True) + ps)
    return pp * inv, pc * inv, ps * inv


def attn_fwd(qt, kt, vt, sink_rows, name, ride=None):
    s = qt.shape[1]
    nb = s // WINDOW
    rows = Q_PER_KV * WINDOW

    def body(q_ref, kp_ref, kc_ref, vp_ref, vc_ref, sk_ref, o_ref):
        n = pl.program_id(1)
        q = q_ref[...].reshape(rows, ATT_HEAD_DIM)
        pp, pc, _ = _attn_probs(q, kp_ref[0], kc_ref[0], sk_ref[0], n)
        o = _dot(pp.astype(BF16), vp_ref[0]) + _dot(pc.astype(BF16), vc_ref[0])
        o_ref[...] = o.reshape(Q_PER_KV, WINDOW, ATT_HEAD_DIM).astype(BF16)

    qsp = pl.BlockSpec((Q_PER_KV, WINDOW, ATT_HEAD_DIM), lambda h, n: (h, n, 0))
    prev = pl.BlockSpec((1, WINDOW, ATT_HEAD_DIM), lambda h, n: (h, jnp.maximum(n - 1, 0), 0))
    cur = pl.BlockSpec((1, WINDOW, ATT_HEAD_DIM), lambda h, n: (h, n, 0))
    return _pcall(body, out_shape=[_sds(qt.shape, BF16)], grid=(N_KV_HEADS, nb),
                  in_specs=[qsp, prev, cur, prev, cur, pl.BlockSpec((1, rows, 1), lambda h, n: (h, 0, 0))],
                  out_specs=[qsp], args=[qt, kt, kt, vt, vt, sink_rows], name=name, ride=ride)


def attn_bwd(qt, kt, vt, sink_rows, dot_, name, ride=None):
    s = qt.shape[1]
    nb = s // WINDOW
    rows = Q_PER_KV * WINDOW

    def body(q_ref, kp_ref, kc_ref, vp_ref, vc_ref, sk_ref, do_ref, dq_ref, dk_ref, dv_ref, ds_ref, kacc, vacc):
        n = pl.program_id(1)

        @pl.when(n < nb)
        def _():
            q = q_ref[...].reshape(rows, ATT_HEAD_DIM)
            do = do_ref[...].reshape(rows, ATT_HEAD_DIM)
            kp, kc, vp, vc = kp_ref[0], kc_ref[0], vp_ref[0], vc_ref[0]
            pp, pc, ps = _attn_probs(q, kp, kc, sk_ref[0], n)
            dpp = _dot(do, vp, NT)
            dpc = _dot(do, vc, NT)
            delta = jnp.sum(pp * dpp, axis=1, keepdims=True) + jnp.sum(pc * dpc, axis=1, keepdims=True)
            dsp = (pp * (dpp - delta)).astype(BF16)
            dsc = (pc * (dpc - delta)).astype(BF16)
            dq = _dot(dsp, kp) + _dot(dsc, kc)
            dq_ref[...] = dq.reshape(Q_PER_KV, WINDOW, ATT_HEAD_DIM)
            dk_prev = _dot(dsp, q, TN)
            dv_prev = _dot(pp.astype(BF16), do, TN)

            @pl.when(n == 0)
            def _():
                dk_ref[0] = dk_prev
                dv_ref[0] = dv_prev

            @pl.when(n > 0)
            def _():
                dk_ref[0] = kacc[...] + dk_prev
                dv_ref[0] = vacc[...] + dv_prev

            kacc[...] = _dot(dsc, q, TN)
            vacc[...] = _dot(pc.astype(BF16), do, TN)
            dsk = -ps * delta
            sub = lax.broadcasted_iota(jnp.int32, (8, LANES), 0)
            tile = jnp.zeros((8, LANES), F32)
            for h in range(Q_PER_KV):
                tile += jnp.where(sub == h, jnp.sum(dsk[h * WINDOW:(h + 1) * WINDOW, :], axis=0, keepdims=True), 0.0)
            ds_ref[0, 0] = tile

        @pl.when(n == nb)
        def _():
            dk_ref[0] = kacc[...]
            dv_ref[0] = vacc[...]
            ds_ref[0, 0] = jnp.zeros((8, LANES), F32)

    last = nb - 1
    qsp = pl.BlockSpec((Q_PER_KV, WINDOW, ATT_HEAD_DIM), lambda h, n: (h, jnp.minimum(n, last), 0))
    prev = pl.BlockSpec((1, WINDOW, ATT_HEAD_DIM), lambda h, n: (h, jnp.clip(n - 1, 0, last), 0))
    cur = pl.BlockSpec((1, WINDOW, ATT_HEAD_DIM), lambda h, n: (h, jnp.minimum(n, last), 0))
    dkv = pl.BlockSpec((1, WINDOW, ATT_HEAD_DIM), lambda h, n: (h, jnp.maximum(n - 1, 0), 0))
    f = lambda shape: _sds(shape, F32)
    return _pcall(
        body, out_shape=[f(qt.shape), f(kt.shape), f(vt.shape), f((N_KV_HEADS, nb + 1, 8, LANES))],
        grid=(N_KV_HEADS, nb + 1),
        in_specs=[qsp, prev, cur, prev, cur, pl.BlockSpec((1, rows, 1), lambda h, n: (h, 0, 0)), qsp],
        out_specs=[qsp, dkv, dkv, pl.BlockSpec((1, 1, 8, LANES), lambda h, n: (h, n, 0, 0))],
        scratch_shapes=[pltpu.VMEM((WINDOW, ATT_HEAD_DIM), F32), pltpu.VMEM((WINDOW, ATT_HEAD_DIM), F32)],
        args=[qt, kt, kt, vt, vt, sink_rows, dot_], name=name, ride=ride)


def loss_head(x, w, tgt, name):
    s, d = x.shape
    tm = _row_tile(s, 256)

    def body(x_ref, w_ref, t_ref, loss_ref, dx_ref, dw_ref):
        i = pl.program_id(0)
        xf = x_ref[...]
        wv = w_ref[...]
        r = lax.rsqrt(jnp.mean(xf * xf, axis=-1, keepdims=True) + EPS)
        xhat = xf * r
        e = xhat * wv - t_ref[...]
        part = 0.5 * jnp.sum(jnp.mean(e * e, axis=-1, keepdims=True), axis=0, keepdims=True)
        dy = e * (1.0 / d)
        dxhat = dy * wv
        dx_ref[...] = r * (dxhat - xhat * jnp.mean(dxhat * xhat, axis=-1, keepdims=True))
        col = jnp.sum(dy * xhat, axis=0, keepdims=True)

        @pl.when(i == 0)
        def _():
            loss_ref[...] = jnp.broadcast_to(part, (1, LANES))
            dw_ref[...] = col

        @pl.when(i > 0)
        def _():
            loss_ref[...] += jnp.broadcast_to(part, (1, LANES))
            dw_ref[...] += col

    row = pl.BlockSpec((tm, d), lambda i: (i, 0))
    vec = pl.BlockSpec((1, d), lambda i: (0, 0))
    return _pcall(body, out_shape=[_sds((1, LANES), F32), _sds((s, d), F32), _sds((1, d), F32)], grid=(s // tm,),
                  in_specs=[row, vec, row], out_specs=[pl.BlockSpec((1, LANES), lambda i: (0, 0)), row, vec],
                  args=[x, w.reshape(1, d), tgt], name=name)


def _tile_rows(r, c, max_elems=262144, mult=16):
    best = None
    for t in range(mult, r + 1, mult):
        if r % t == 0 and t * c <= max_elems:
            best = t
    return best or r


def add_pair(xhs, ps, c_idx, name):
    n = len(xhs)
    _, r, c = xhs[0].shape
    tr = _tile_rows(r, c)

    def body(c_ref, *refs):
        for x_ref, p_ref, o_ref in zip(refs[:n], refs[n:2 * n], refs[2 * n:]):
            o_ref[...] = (x_ref[0].astype(F32) + p_ref[...].astype(F32)).astype(BF16)

    blk = pl.BlockSpec((tr, c), lambda i, cr: (i, 0))
    return pl.pallas_call(
        body, out_shape=tuple([_sds((r, c), BF16)] * n),
        grid_spec=pltpu.PrefetchScalarGridSpec(
            num_scalar_prefetch=1, grid=(r // tr,),
            in_specs=[pl.BlockSpec((1, tr, c), lambda i, cr: (cr[0], i, 0))] * n + [blk] * n,
            out_specs=tuple([blk] * n)),
        name=name, compiler_params=_cp(1))(c_idx, *xhs, *ps)


def sum_chips(qs, owns, chip_idx, name):
    n = len(qs)
    _, r, c = qs[0].shape
    tr = _tile_rows(r, c)

    def body(k_ref, *refs):
        k = k_ref[0]
        for q_ref, own_ref, o_ref in zip(refs[:n], refs[n:2 * n], refs[2 * n:]):
            mine = own_ref[0].astype(F32)
            tot = None
            for j in range(N_CHIPS):
                term = jnp.where(k == j, mine, q_ref[j].astype(F32))
                tot = term if tot is None else tot + term
            o_ref[...] = tot

    return pl.pallas_call(
        body, out_shape=tuple([_sds((r, c), F32)] * n),
        grid_spec=pltpu.PrefetchScalarGridSpec(
            num_scalar_prefetch=1, grid=(r // tr,),
            in_specs=([pl.BlockSpec((N_CHIPS, tr, c), lambda i, kr: (0, i, 0))] * n
                      + [pl.BlockSpec((1, tr, c), lambda i, kr: (kr[0], i, 0))] * n),
            out_specs=tuple([pl.BlockSpec((tr, c), lambda i, kr: (i, 0))] * n)),
        name=name, compiler_params=_cp(1))(chip_idx, *qs, *owns)


def adamw(w, g, m, v, name):
    r, c = w.shape
    tr = _tile_rows(r, c, max_elems=131072, mult=8)
    c1 = 1.0 / (1.0 - ADAM_B1 ** ADAM_STEP)
    c2 = 1.0 / (1.0 - ADAM_B2 ** ADAM_STEP)

    def body(w_ref, g_ref, m_ref, v_ref, d_ref, mo_ref, vo_ref):
        gf = g_ref[...]
        mn = ADAM_B1 * m_ref[...] + (1.0 - ADAM_B1) * gf
        vn = ADAM_B2 * v_ref[...] + (1.0 - ADAM_B2) * (gf * gf)
        mo_ref[...] = mn
        vo_ref[...] = vn
        d_ref[...] = -ADAM_LR * ((mn * c1) / (jnp.sqrt(vn * c2) + ADAM_EPS) + ADAM_WD * w_ref[...])

    blk = pl.BlockSpec((tr, c), lambda i: (i, 0))
    out = _sds((r, c), F32)
    return _pcall(body, out_shape=[out, out, out], grid=(r // tr,), in_specs=[blk] * 4, out_specs=[blk] * 3,
                  args=[w, g, m, v], name=name)


WEIGHTS = ['norm_w', 'ffn_w_gate', 'ffn_w_up', 'ffn_w_down', 'ssm_w_in', 'ssm_conv_w', 'ssm_conv_b', 'ssm_dt_bias',
           'ssm_a_log', 'ssm_d', 'ssm_norm_w', 'ssm_w_out', 'kv_norm_w', 'w_k', 'b_k', 'w_v', 'b_v', 'attn_w_q',
           'attn_b_q', 'attn_sinks', 'attn_w_o', 'attn_b_o', 'final_norm_w']
BIG = ['ffn_w_gate', 'ffn_w_up', 'ffn_w_down', 'ssm_w_in', 'ssm_w_out', 'w_k', 'w_v', 'attn_w_q', 'attn_w_o']
TRANSPOSED = ('ffn_w_gate', 'ffn_w_up', 'ssm_w_in')
SMALL = [n for n in WEIGHTS if n not in BIG]
SMALL_SHARDED = {'norm_w': 2, 'ssm_conv_w': 2, 'ssm_conv_b': 1, 'ssm_norm_w': 1}
ROW_ALIGN = 8 * LANES


def _pack_rows(parts):
    flat = jnp.concatenate([p.reshape(-1).astype(F32) for p in parts])
    pad = (-flat.size) % ROW_ALIGN
    return jnp.pad(flat, (0, pad)).reshape(-1, LANES)


def _unpack_rows(buf, shapes):
    flat = buf.reshape(-1)
    out, pos = [], 0
    for shp in shapes:
        size = math.prod(shp)
        out.append(flat[pos:pos + size].reshape(shp))
        pos += size
    return out


def _as2d(a):
    return a.reshape(-1, a.shape[-1])


def _heads_major(t, n_heads):
    s = t.shape[0]
    return t.reshape(s, n_heads, ATT_HEAD_DIM).transpose(1, 0, 2)


def _tokens_major(t):
    h, s, dh = t.shape
    return t.transpose(1, 0, 2).reshape(s, h * dh)


def _pad_lanes(v):
    return jnp.pad(v.reshape(1, -1), ((0, 0), (0, LANES - v.size)))


def _chips_first(t):
    return t.swapaxes(0, 1).reshape((-1,) + t.shape[3:])


def _parts_first(t, rows):
    return t.reshape((N_CHIPS, N_CORES, rows) + t.shape[1:]).swapaxes(0, 1)


def kernel(*args):
    names = (['x'] + WEIGHTS + ['loss_target'] + ['m_' + n for n in WEIGHTS] + ['v_' + n for n in WEIGHTS])
    a = dict(zip(names, args))
    for n in TRANSPOSED:
        for pre in ('', 'm_', 'v_'):
            a[pre + n] = a[pre + n].swapaxes(-1, -2)
    xi, yi, ci = lax.axis_index("x"), lax.axis_index("y"), lax.axis_index("c")
    chip = 2 * xi + yi
    south = ci == 0
    c_idx = jnp.reshape(ci, (1,)).astype(jnp.int32)
    chip_idx = jnp.reshape(chip, (1,)).astype(jnp.int32)
    x0 = a['x'][0]
    s = x0.shape[0]
    cos, sin = rope_tables(s)

    def own_slot(full, mine):
        return lax.dynamic_update_slice_in_dim(full, mine[:, None], chip, axis=1)

    def ffn_shard(l, i, src):
        return [src[n][l, i].astype(BF16).reshape(N_CORES, FF_PART, D_MODEL)
                for n in ('ffn_w_gate', 'ffn_w_up', 'ffn_w_down')]

    def own_slots(fulls, mines):
        return [own_slot(f, m) for f, m in zip(fulls, mines)]
    small_names = list(SMALL_SHARDED)
    small_sh = _pack_rows([a[n] for n in small_names])
    small_sh = small_sh.reshape(N_CORES, small_sh.shape[0] // 2, LANES)
    sh00 = ffn_shard(0, 0, a)
    (first_flight,), started = split_start([sh00 + [small_sh]], "gather", "gather_start_first")
    held = lax.optimization_barrier((started, {n: a[n] for n in BIG}))[1]
    sh01, sh10, sh11 = ffn_shard(0, 1, held), ffn_shard(1, 0, held), ffn_shard(1, 1, held)
    w_in_sh = jnp.pad(held['ssm_w_in'][0], ((0, IN_SHARD_PAD - IN_SHARD), (0, 0))).astype(BF16).reshape(
        N_CORES, IN_SHARD_PAD // 2, D_MODEL)
    w_out_sh = held['ssm_w_out'][0].astype(BF16).reshape(N_CORES, 256, D_MODEL)
    attn_sh = jnp.stack([held['attn_w_q'][0], held['attn_w_o'][0]]).astype(BF16)
    kv_sh = jnp.stack([held['w_k'], held['w_v']]).astype(BF16)
    rest_flights, all_started = split_start([[w_in_sh, kv_sh], [w_out_sh], sh01, sh10, [attn_sh], sh11], "gather",
                                            "gather_start_rest")
    in_flight = [first_flight] + rest_flights

    def arrive(idx, after, tag):
        return forward_cores(split_arrive(in_flight[idx], "gather", after, "gather_arrive_" + tag))

    first = run_exchange(arrive(0, all_started, "first"), "gather_hop_first")
    w00 = own_slots(first[:3], sh00)
    smalls = own_slot(first[3], small_sh)
    p = {}
    per_chip = [_unpack_rows(smalls[:, k], [a[n].shape for n in small_names]) for k in range(N_CHIPS)]
    for idx, n in enumerate(small_names):
        p[n] = jnp.concatenate([per_chip[k][idx] for k in range(N_CHIPS)], axis=SMALL_SHARDED[n])
    nw = p['norm_w']
    conv_w, conv_b, ssm_nw = p['ssm_conv_w'][0], p['ssm_conv_b'][0], p['ssm_norm_w'][0].reshape(1, D_INNER)

    h00 = rmsnorm_fwd(x0, nw[0, 0], "norm_in")
    x1, h01, gu00 = ffn_fwd(h00, x0, *w00, [nw[0, 1]], "ffn_fwd_00")
    w_in_g, kv_g = run_exchange(arrive(1, x1, "in"), "gather_hop_in")
    w_in_t = _chips_first(own_slot(w_in_g, w_in_sh)).reshape(N_CHIPS, IN_SHARD_PAD, D_MODEL)[:, :IN_SHARD].reshape(
        IN_PROJ_DIM, D_MODEL)
    w_dt_t = jnp.pad(w_in_t[D_INNER + CONV_DIM:], ((0, LANES - SSM_HEADS), (0, 0)))
    kv_g = own_slot(kv_g, kv_sh)
    w_k, w_v = kv_g[0].reshape(D_MODEL, KV_DIM), kv_g[1].reshape(D_MODEL, KV_DIM)

    zz = mm_nt(h01, w_in_t, "ssm_in_z", n=D_INNER)
    xbc = mm_nt(h01, w_in_t, "ssm_in_xbc", n=CONV_DIM, row0=D_INNER)
    dtp = mm_nt(h01, w_dt_t, "ssm_in_dt")
    act = conv_fwd(xbc, conv_w, conv_b, "ssm_conv")
    bias_p = _pad_lanes(a['ssm_dt_bias'][0])
    a_p = _pad_lanes(-jnp.exp(a['ssm_a_log'][0]))
    d_p = _pad_lanes(a['ssm_d'][0])
    (yn, y_pre, states), (w_out_g,) = ssd_fwd(act, zz, dtp, bias_p, a_p, d_p, ssm_nw, "ssd_fwd",
                                              ride=arrive(2, act, "out"))
    w_out = _chips_first(own_slot(w_out_g, w_out_sh))
    (x2, h02), w01 = mm_res(yn, w_out, x1, "ssm_out", norm_ws=[nw[0, 2]], ride=arrive(3, yn, "01"))
    w01 = own_slots(w01, sh01)
    x3, hkv, h10, gu01 = ffn_fwd(h02, x2, *w01, [a['kv_norm_w'], nw[1, 0]], "ffn_fwd_01")
    w10 = own_slots(run_exchange(arrive(4, x3, "10"), "gather_hop_10"), sh10)

    k_rot = rope_apply(mm_nn(hkv, w_k, "kv_k", bias=a['b_k']), cos, sin, "rope_k")
    v = mm_nn(hkv, w_v, "kv_v", bias=a['b_v'], out_dtype=BF16)
    kt = _heads_major(k_rot, N_KV_HEADS)
    vt = _heads_major(v, N_KV_HEADS)

    (x4, h11, gu10), (attn_g,) = ffn_fwd(h10, x3, *w10, [nw[1, 1]], "ffn_fwd_10", ride=arrive(5, v, "attn"))
    attn_g = own_slot(attn_g, attn_sh)
    w_q, w_o = attn_g[0].reshape(D_MODEL, D_MODEL), attn_g[1].reshape(D_MODEL, D_MODEL)
    scale = 1.0 / math.sqrt(ATT_HEAD_DIM)
    q_rot = rope_apply(mm_nn(h11, w_q, "attn_q", bias=a['attn_b_q'][0]), cos, sin, "rope_q", scale=scale)
    qt = _heads_major(q_rot, N_Q_HEADS)
    sink_rows = jnp.repeat(a['attn_sinks'][0].reshape(N_KV_HEADS, Q_PER_KV), WINDOW, axis=1).reshape(
        N_KV_HEADS, Q_PER_KV * WINDOW, 1)
    (ot,) = attn_fwd(qt, kt, vt, sink_rows, "attn_fwd")
    o = _tokens_major(ot)
    (x5, h12), w11 = mm_res(o, w_o, x4, "attn_out", bias=a['attn_b_o'][0], norm_ws=[nw[1, 2]],
                            ride=arrive(6, ot, "11"))
    w11 = own_slots(w11, sh11)
    x6, gu11 = ffn_fwd(h12, x5, *w11, [], "ffn_fwd_11")

    loss_v, dx6, d_final = loss_head(x6, a['final_norm_w'], a['loss_target'][0], "loss_head")
    loss = lax.psum(loss_v[0, 0], ("x", "y", "c"))
    g = {'final_norm_w': d_final[0]}

    def same_shape(xs, ys):
        runs = []
        for xv, yv in zip(xs, ys):
            if runs and runs[-1][0][0].shape == xv.shape:
                runs[-1][0].append(xv)
                runs[-1][1].append(yv)
            else:
                runs.append(([xv], [yv]))
        return runs

    def pre_reduce(grads, sib, tag):
        out = []
        for idx, (grp, sbs) in enumerate(same_shape(grads, list(sib))):
            ts = add_pair([gr.reshape(2, -1, gr.shape[-1]) for gr in grp], [_as2d(sb) for sb in sbs], c_idx,
                          "rs_add_%s_%d" % (tag, idx))
            out += [t.reshape(gr.shape[1:]) for t, gr in zip(ts, grp)]
        return out

    def chip_sum(landed, parts, tag):
        out = []
        for idx, (qs, owns) in enumerate(same_shape(list(landed), parts)):
            ts = sum_chips([q.reshape(N_CHIPS, -1, q.shape[-1]) for q in qs],
                           [own.reshape(N_CHIPS, -1, own.shape[-1]) for own in owns], chip_idx,
                           "rs_sum_%s_%d" % (tag, idx))
            out += [t.reshape(q.shape[1:]) for t, q in zip(ts, qs)]
        return out

    dnw = [[None] * 3 for _ in range(2)]
    sums = {}

    def trade(key):
        return swap_cores(sums[key], False)

    dx5, dnw12, *g11 = ffn_bwd(dx6, h12, x5, nw[1, 2], gu11, *w11, "ffn_bwd_11")
    dnw[1][2] = dnw12[0]
    (d_wo, g['attn_b_o']), sib11 = mm_tn(o, dx5, "attn_dwo", col_sum=True, ride=swap_cores(g11, True))
    t11 = pre_reduce(g11, sib11, "11")
    do = mm_nt(dx5, w_o, "attn_do", out_dtype=BF16)
    (dqt, dkt, dvt, dsink), land11 = attn_bwd(qt, kt, vt, sink_rows, _heads_major(do, N_Q_HEADS), "attn_bwd",
                                             ride=scatter_chips(t11))
    sums['11'] = chip_sum(land11, t11, "11")
    g['attn_sinks'] = jnp.sum(dsink[:, :, :Q_PER_KV, 0], axis=1).reshape(N_Q_HEADS)
    dq_pre = rope_apply(_tokens_major(dqt), cos, sin, "rope_dq", inverse=True, scale=scale, out_dtype=F32)
    d_wq, g['attn_b_q'] = mm_tn(h11, dq_pre, "attn_dwq", col_sum=True)
    g_attn = [jnp.stack([d_wq.reshape(N_CHIPS, 256, D_MODEL), d_wo.reshape(N_CHIPS, 256, D_MODEL)])]
    (dx4, dnw[1][1]), sib_attn = mm_rms_bwd([(dq_pre, 0, w_q, 0, D_MODEL, "nt")], dx5, x4, nw[1, 1], "attn_bwd_dh",
                                            ride=swap_cores(g_attn, True))
    t_attn = pre_reduce(g_attn, sib_attn, "attn")
    (dx3, dnw10, *g10), landed = ffn_bwd(dx4, h10, x3, nw[1, 0], gu10, *w10, "ffn_bwd_10",
                                         ride=join(scatter_chips(t_attn), trade('11')))
    dnw[1][0] = dnw10[0]
    sums['attn'] = chip_sum(landed[:1], t_attn, "attn")
    theirs = {'11': landed[1:]}
    dk_pre = rope_apply(_tokens_major(dkt), cos, sin, "rope_dk", inverse=True, out_dtype=F32)
    dv = _tokens_major(dvt)
    (d_wk, g['b_k']), sib10 = mm_tn(hkv, dk_pre, "kv_dwk", col_sum=True, ride=swap_cores(g10, True))
    t10 = pre_reduce(g10, sib10, "10")
    d_wv, g['b_v'] = mm_tn(hkv, dv, "kv_dwv", col_sum=True)
    g_kv = [jnp.stack([d_wk.reshape(N_CHIPS, 256, KV_DIM), d_wv.reshape(N_CHIPS, 256, KV_DIM)])]
    (dx3, g['kv_norm_w']), sib_kv = mm_rms_bwd(
        [(dk_pre, 0, w_k, 0, KV_DIM, "nt"), (dv, 0, w_v, 0, KV_DIM, "nt")], dx3, x3, a['kv_norm_w'], "kv_bwd_dh",
        ride=swap_cores(g_kv, True))
    t_kv = pre_reduce(g_kv, sib_kv, "kv")
    (dx2, dnw02, *g01), landed = ffn_bwd(dx3, h02, x2, nw[0, 2], gu01, *w01, "ffn_bwd_01",
                                         ride=join(scatter_chips(t10 + t_kv), trade('attn')))
    dnw[0][2] = dnw02[0]
    sums['10'] = chip_sum(landed[:3], t10, "10")
    sums['kv'] = chip_sum(landed[3:4], t_kv, "kv")
    theirs['attn'] = landed[4:]
    d_wout, sib01 = mm_tn(yn, dx2, "ssm_dwout", ride=swap_cores(g01, True))
    t01 = pre_reduce(g01, sib01, "01")
    dyn = mm_nt(dx2, w_out, "ssm_dyn")
    (dxs, db_, dc_, dz, ddt, d_ssm_nw, d_bias, d_a, d_d), landed = ssd_bwd(
        dyn, act, zz, y_pre, states, dtp, bias_p, a_p, d_p, ssm_nw, "ssd_bwd",
        ride=join(scatter_chips(t01), trade('10'), trade('kv')))
    sums['01'] = chip_sum(landed[:3], t01, "01")
    theirs['10'], theirs['kv'] = landed[3:6], landed[6:]
    g['ssm_norm_w'] = d_ssm_nw[:SSM_GROUPS].reshape(D_INNER)
    g['ssm_dt_bias'] = d_bias[0, :SSM_HEADS]
    g['ssm_a_log'] = d_a[0, :SSM_HEADS] * a_p[0, :SSM_HEADS]
    g['ssm_d'] = d_d[0, :SSM_HEADS]
    dxbc, g['ssm_conv_w'], g['ssm_conv_b'] = conv_bwd(dxs, db_, dc_, xbc, conv_w, conv_b, "ssm_conv_bwd")
    d_win = mm_tn(dz, h01, "ssm_dwz", rows=IN_PROJ_DIM)
    d_win = mm_tn(dxbc, h01, "ssm_dwxbc", into=d_win, rows=IN_PROJ_DIM, row0=D_INNER)
    d_win = mm_tn(ddt, h01, "ssm_dwdt", into=d_win, rows=IN_PROJ_DIM, row0=D_INNER + CONV_DIM, m_valid=SSM_HEADS)
    d_win = jnp.pad(d_win.reshape(N_CHIPS, IN_SHARD, D_MODEL), ((0, 0), (0, IN_SHARD_PAD - IN_SHARD), (0, 0)))
    g_ssm = [_parts_first(d_win.reshape(-1, D_MODEL), IN_SHARD_PAD // 2), _parts_first(d_wout, 256)]
    kb = 1024
    terms = ([(dz, j, w_in_t, j, kb, "nn") for j in range(D_INNER // kb)]
             + [(dxbc, j, w_in_t, D_INNER // kb + j, kb, "nn") for j in range(CONV_DIM // kb)]
             + [(ddt, 0, w_dt_t, 0, LANES, "nn")])
    (dx1, dnw[0][1]), sib_ssm = mm_rms_bwd(terms, dx2, x1, nw[0, 1], "ssm_bwd_dh", ride=swap_cores(g_ssm, True))
    t_ssm = pre_reduce(g_ssm, sib_ssm, "ssm")
    (grad_x, dnw00, *g00), landed = ffn_bwd(dx1, h00, x0, nw[0, 0], gu00, *w00, "ffn_bwd_00",
                                            ride=join(scatter_chips(t_ssm), trade('01')))
    dnw[0][0] = dnw00[0]
    sums['ssm'] = chip_sum(landed[:2], t_ssm, "ssm")
    theirs['01'] = landed[2:]
    landed = run_exchange(join(swap_cores(g00, True), trade('ssm')), "rs_swap_00")
    t00 = pre_reduce(g00, landed[:3], "00")
    theirs['ssm'] = landed[3:]

    def both(key):
        return [(jnp.where(south, m_, t_), jnp.where(south, t_, m_)) for m_, t_ in zip(sums[key], theirs[key])]

    g['norm_w'] = jnp.stack([jnp.stack(r) for r in dnw])
    red = all_reduce_small(_pack_rows([g[n] for n in SMALL]), "reduce_vectors")

    t00 = lax.optimization_barrier((red, t00))[1]
    (flight00,), flying = split_start([t00], "scatter", "rs_scatter_00_start")

    def held(val):
        return lax.optimization_barrier((flying, val))[1]

    delta, new_m, new_v, gw = {}, {}, {}, {}
    ffn_names = ('ffn_w_gate', 'ffn_w_up', 'ffn_w_down')
    full = {key: both(key) for key in ('attn', 'kv', 'ssm')}
    lo, hi = full['attn'][0]
    gw['attn_w_q'], gw['attn_w_o'] = lo[None], hi[None]
    lo, hi = full['kv'][0]
    gw['w_k'], gw['w_v'] = lo, hi
    lo, hi = full['ssm'][0]
    gw['ssm_w_in'] = jnp.concatenate([lo, hi], axis=0)[:IN_SHARD][None]
    lo, hi = full['ssm'][1]
    gw['ssm_w_out'] = jnp.concatenate([lo, hi], axis=0)[None]

    for n, t in zip(SMALL, _unpack_rows(red, [g[n].shape for n in SMALL])):
        if n in SMALL_SHARDED:
            ax = SMALL_SHARDED[n] - (a[n].ndim - t.ndim)
            width = a[n].shape[SMALL_SHARDED[n]]
            t = lax.dynamic_slice_in_dim(t, chip * width, width, axis=ax)
        gw[n] = t.reshape(a[n].shape)

    def update(n):
        d, mo, vo = adamw(_as2d(a[n]), held(_as2d(gw[n])), _as2d(a['m_' + n]), _as2d(a['v_' + n]), "adamw_" + n)
        delta[n], new_m[n], new_v[n] = d.reshape(a[n].shape), mo.reshape(a[n].shape), vo.reshape(a[n].shape)

    for n in BIG:
        if n not in ffn_names:
            update(n)
    shapes = [a[n].shape for n in SMALL]
    packed = [_pack_rows([src[n] for n in SMALL]) for src in
              (a, gw, {n: a['m_' + n] for n in SMALL}, {n: a['v_' + n] for n in SMALL})]
    outs = adamw(*packed, "adamw_vectors")
    for dst, buf in zip((delta, new_m, new_v), outs):
        for n, t in zip(SMALL, _unpack_rows(buf, shapes)):
            dst[n] = t
    for key in ('01', '10', '11'):
        sums[key] = held(list(sums[key]))
    rest = [both(key) for key in ('01', '10', '11')]
    land00 = split_arrive(flight00, "scatter", outs[0], "rs_scatter_00_arrive")
    sums['00'] = chip_sum(land00, t00, "00")
    theirs['00'] = run_exchange(trade('00'), "rs_trade_00")
    blocks = [both('00')] + rest
    for t, n in enumerate(ffn_names):
        gw[n] = jnp.concatenate([piece for blk in blocks for piece in blk[t]], axis=0).reshape(a[n].shape)
        update(n)
    for n in TRANSPOSED:
        for dst in (gw, delta, new_m, new_v):
            dst[n] = dst[n].swapaxes(-1, -2)

    return (loss, grad_x[None], *[gw[n] for n in WEIGHTS], *[delta[n] for n in WEIGHTS],
            *[new_m[n] for n in WEIGHTS], *[new_v[n] for n in WEIGHTS])
```

```python
import math

import jax
import jax.numpy as jnp
from jax import lax
from jax.experimental import pallas as pl
from jax.experimental.pallas import tpu as pltpu

F32 = jnp.float32
BF16 = jnp.bfloat16

D_MODEL = 1024
D_INNER = 2048
SSM_HEADS = 32
SSM_GROUPS = 4
HEADS_PER_GROUP = SSM_HEADS // SSM_GROUPS
SSM_HEAD_DIM = 64
SSM_STATE = 128
GROUP_DIM = D_INNER // SSM_GROUPS
CONV_DIM = D_INNER + 2 * SSM_GROUPS * SSM_STATE
CONV_WIDTH = 4
CHUNK = 128
ATT_HEAD_DIM = 64
N_Q_HEADS = 16
N_KV_HEADS = 4
Q_PER_KV = N_Q_HEADS // N_KV_HEADS
KV_DIM = N_KV_HEADS * ATT_HEAD_DIM
WINDOW = 128
ROPE_THETA = 10000.0
D_FF = 2816
N_CHIPS = 4
N_CORES = 2
FF_SHARD = D_FF // N_CHIPS
FF_PART = FF_SHARD // N_CORES
IN_PROJ_DIM = D_INNER + CONV_DIM + SSM_HEADS
IN_SHARD = IN_PROJ_DIM // N_CHIPS
IN_SHARD_PAD = 1312
EPS = 1e-5
NEG = -1e30
LANES = 128
VMEM_LIMIT = 56 * 1024 * 1024

ADAM_LR = 0.001
ADAM_B1 = 0.9
ADAM_B2 = 0.999
ADAM_EPS = 1e-08
ADAM_WD = 0.01
ADAM_STEP = 10

NN = ((1,), (0,))
NT = ((1,), (1,))
TN = ((0,), (0,))
MESH = pl.DeviceIdType.MESH
ANY = pl.BlockSpec(memory_space=pl.ANY)


def _dot(a, b, dims=NN, precision=None):
    return lax.dot_general(a, b, (dims, ((), ())), preferred_element_type=F32, precision=precision)


def _cp(n_grid):
    return pltpu.CompilerParams(dimension_semantics=("arbitrary",) * n_grid, vmem_limit_bytes=VMEM_LIMIT)


def _sigmoid(x):
    return 1.0 / (1.0 + jnp.exp(-x))


def _rms_fwd(xf, w):
    r = lax.rsqrt(jnp.mean(xf * xf, axis=-1, keepdims=True) + EPS)
    return xf * r * w


def _rms_bwd(dh, xf, w):
    r = lax.rsqrt(jnp.mean(xf * xf, axis=-1, keepdims=True) + EPS)
    xhat = xf * r
    dxhat = dh * w
    dx = r * (dxhat - xhat * jnp.mean(dxhat * xhat, axis=-1, keepdims=True))
    return dx, dh * xhat


def _row_tile(s, pref):
    return pref if s % pref == 0 else s


def _col_tile(n):
    for t in (1024, 768, 512, 256, 128):
        if n % t == 0:
            return t
    return n


def _sds(shape, dtype):
    return jax.ShapeDtypeStruct(tuple(shape), dtype)


class Exchange:
    def __init__(self, ins, out_shapes, sems, start, finish, inplace=False):
        self.ins, self.out_shapes, self.sems, self.start, self.finish = ins, out_shapes, sems, start, finish
        self.inplace = inplace


def _place():
    x, y, c = lax.axis_index("x"), lax.axis_index("y"), lax.axis_index("c")
    others = [(1 - x, y), (x, 1 - y), (1 - x, 1 - y)]
    return x, y, c, 2 * x + y, others


def _rc(src, dst, send_sem, recv_sem, dev):
    return pltpu.make_async_remote_copy(src_ref=src, dst_ref=dst, send_sem=send_sem, recv_sem=recv_sem,
                                        device_id=dev, device_id_type=MESH)


def gather_chips(arrs):
    n = len(arrs)

    def copies(ins, outs, sems):
        send, recv = sems
        x, y, c, k, others = _place()
        ici, land, fwd, fland = [], [], [], []
        for a in range(n):
            for j, (px, py) in enumerate(others):
                ici.append(_rc(ins[a].at[c], outs[a].at[c, k], send.at[a, j], recv.at[a, j], (px, py, c)))
                blk = outs[a].at[c, 2 * px + py]
                land.append(_rc(blk, blk, send.at[a, j], recv.at[a, j], (px, py, c)))
                fwd.append(_rc(blk, blk, send.at[a, 3 + j], recv.at[a, 3 + j], (x, y, 1 - c)))
                blk2 = outs[a].at[1 - c, 2 * px + py]
                fland.append(_rc(blk2, blk2, send.at[a, 3 + j], recv.at[a, 3 + j], (x, y, 1 - c)))
        return ici, land, fwd, fland

    def start(ins, outs, sems):
        for cp in copies(ins, outs, sems)[0]:
            cp.start()

    def finish(ins, outs, sems):
        ici, land, fwd, fland = copies(ins, outs, sems)
        for arrived, onward in zip(land, fwd):
            arrived.wait_recv()
            onward.start()
        for arrived in fland:
            arrived.wait_recv()
        for cp in ici + fwd:
            cp.wait_send()

    return Exchange(list(arrs), [_sds((2, N_CHIPS) + a.shape[1:], a.dtype) for a in arrs],
                    [pltpu.SemaphoreType.DMA((n, 6)), pltpu.SemaphoreType.DMA((n, 6))], start, finish)


def scatter_chips(arrs):
    n = len(arrs)

    def copies(ins, outs, sems):
        send, recv = sems
        x, y, c, k, others = _place()
        out, land = [], []
        for a in range(n):
            for j, (px, py) in enumerate(others):
                out.append(_rc(ins[a].at[2 * px + py], outs[a].at[k], send.at[a, j], recv.at[a, j], (px, py, c)))
                blk = outs[a].at[2 * px + py]
                land.append(_rc(blk, blk, send.at[a, j], recv.at[a, j], (px, py, c)))
        return out, land

    def start(ins, outs, sems):
        for cp in copies(ins, outs, sems)[0]:
            cp.start()

    def finish(ins, outs, sems):
        out, land = copies(ins, outs, sems)
        for arrived in land:
            arrived.wait_recv()
        for cp in out:
            cp.wait_send()

    return Exchange(list(arrs), [_sds(a.shape, a.dtype) for a in arrs],
                    [pltpu.SemaphoreType.DMA((n, 3)), pltpu.SemaphoreType.DMA((n, 3))], start, finish)


def swap_cores(arrs, pick_other):
    n = len(arrs)

    def copies(ins, outs, sems):
        send, recv = sems
        x, y, c, _, _ = _place()
        return [_rc(ins[a].at[1 - c] if pick_other else ins[a], outs[a], send.at[a], recv.at[a], (x, y, 1 - c))
                for a in range(n)]

    def start(ins, outs, sems):
        for cp in copies(ins, outs, sems):
            cp.start()

    def finish(ins, outs, sems):
        for cp in copies(ins, outs, sems):
            cp.wait()

    shapes = [_sds(a.shape[1:] if pick_other else a.shape, a.dtype) for a in arrs]
    return Exchange(list(arrs), shapes, [pltpu.SemaphoreType.DMA((n,)), pltpu.SemaphoreType.DMA((n,))],
                    start, finish)


def join(*parts):
    parts = [p for p in parts if p is not None]
    if not parts:
        return None

    def split(refs, counts):
        out, pos = [], 0
        for cnt in counts:
            out.append(refs[pos:pos + cnt])
            pos += cnt
        return out

    n_in = [len(p.ins) for p in parts]
    n_out = [len(p.out_shapes) for p in parts]
    n_sem = [len(p.sems) for p in parts]

    def run(which):
        def go(ins, outs, sems):
            for p, i, o, s in zip(parts, split(ins, n_in), split(outs, n_out), split(sems, n_sem)):
                getattr(p, which)(i, o, s)
        return go

    return Exchange([a for p in parts for a in p.ins], [s for p in parts for s in p.out_shapes],
                    [s for p in parts for s in p.sems], run("start"), run("finish"))


def _pcall(body, *, out_shape, grid, in_specs, out_specs, args, name, scratch_shapes=(), ride=None, aliases=None):
    out_shape, out_specs, in_specs = tuple(out_shape), tuple(out_specs), list(in_specs)
    aliases = aliases or {}
    if ride is None:
        return pl.pallas_call(body, out_shape=out_shape, grid=grid, in_specs=in_specs, out_specs=out_specs,
                              scratch_shapes=list(scratch_shapes), input_output_aliases=aliases, name=name,
                              compiler_params=_cp(len(grid)))(*args)
    n_in, n_out, n_sc = len(args), len(out_shape), len(scratch_shapes)
    n_xi, n_xo = len(ride.ins), len(ride.out_shapes)

    def wrapped(*refs):
        pos = [0]

        def take(cnt):
            got = refs[pos[0]:pos[0] + cnt]
            pos[0] += cnt
            return got

        c_in, x_in, c_out, x_out, c_sc = take(n_in), take(n_xi), take(n_out), take(n_xo), take(n_sc)
        sems = refs[pos[0]:]
        first, last = True, True
        for d, size in enumerate(grid):
            first = jnp.logical_and(first, pl.program_id(d) == 0)
            last = jnp.logical_and(last, pl.program_id(d) == size - 1)

        @pl.when(first)
        def _():
            ride.start(x_in, x_out, sems)

        body(*c_in, *c_out, *c_sc)

        @pl.when(last)
        def _():
            ride.finish(x_in, x_out, sems)

    if ride.inplace:
        aliases = {**aliases, **{n_in + t: n_out + t for t in range(n_xi)}}
    res = pl.pallas_call(
        wrapped, out_shape=out_shape + tuple(ride.out_shapes), grid=grid,
        in_specs=in_specs + [ANY] * n_xi, out_specs=out_specs + (ANY,) * n_xo,
        scratch_shapes=list(scratch_shapes) + list(ride.sems), input_output_aliases=aliases, name=name,
        compiler_params=_cp(len(grid)))(*args, *ride.ins)
    return res[:n_out], res[n_out:]


def run_exchange(ex, name):
    n_xi, n_xo = len(ex.ins), len(ex.out_shapes)

    def body(*refs):
        ins, outs, sems = refs[:n_xi], refs[n_xi:n_xi + n_xo], refs[n_xi + n_xo:]
        ex.start(ins, outs, sems)
        ex.finish(ins, outs, sems)

    aliases = {t: t for t in range(n_xi)} if ex.inplace else {}
    return pl.pallas_call(body, out_shape=tuple(ex.out_shapes), in_specs=[ANY] * n_xi, out_specs=(ANY,) * n_xo,
                          scratch_shapes=list(ex.sems), input_output_aliases=aliases, name=name)(*ex.ins)


HBM_SPEC = pl.BlockSpec(memory_space=pltpu.HBM)
SEM_SPEC = pl.BlockSpec(memory_space=pltpu.SEMAPHORE)
EFFECT = pltpu.SideEffectType.DATAFLOW_SIDE_EFFECTING


def _route(kind, src, dst, c, k, peer):
    if kind == "gather":
        return src.at[c], dst.at[c, k], dst.at[c, peer]
    return src.at[peer], dst.at[k], dst.at[peer]


def split_start(batches, kind, name):
    flat = [a for batch in batches for a in batch]
    n, nb = len(flat), len(batches)
    lands = [lax.empty((2, N_CHIPS) + a.shape[1:] if kind == "gather" else a.shape, a.dtype) for a in flat]

    def body(*refs):
        srcs, dsts, sems, token = refs[:n], refs[n:2 * n], refs[2 * n:2 * n + 2 * nb], refs[-1]
        x, y, c, k, others = _place()
        pos = 0
        for b, batch in enumerate(batches):
            for a in range(len(batch)):
                for j, (px, py) in enumerate(others):
                    src, dst, _ = _route(kind, srcs[pos], dsts[pos], c, k, 2 * px + py)
                    _rc(src, dst, sems[2 * b].at[3 * a + j], sems[2 * b + 1].at[3 * a + j], (px, py, c)).start()
                pos += 1
        token[...] = jnp.zeros(token.shape, token.dtype)

    sem_shapes = [pltpu.SemaphoreType.DMA((3 * len(batch),)) for batch in batches for _ in range(2)]
    thru = [pltpu.HBM(a.shape, a.dtype) for a in flat] + [pltpu.HBM(l.shape, l.dtype) for l in lands]
    res = pl.pallas_call(
        body, name=name, out_shape=tuple(sem_shapes + thru + [_sds((8, LANES), F32)]),
        in_specs=[HBM_SPEC] * (2 * n),
        out_specs=tuple([SEM_SPEC] * (2 * nb) + [HBM_SPEC] * (2 * n) + [pl.BlockSpec(memory_space=pltpu.VMEM)]),
        input_output_aliases={t: 2 * nb + t for t in range(2 * n)},
        compiler_params=pltpu.CompilerParams(has_side_effects=EFFECT),
    )(*[pltpu.with_memory_space_constraint(t, pltpu.HBM) for t in flat + lands])
    sems, srcs, dsts = res[:2 * nb], res[2 * nb:2 * nb + n], res[2 * nb + n:2 * nb + 2 * n]
    out, pos = [], 0
    for b, batch in enumerate(batches):
        out.append((sems[2 * b], sems[2 * b + 1], list(srcs[pos:pos + len(batch)]), list(dsts[pos:pos + len(batch)])))
        pos += len(batch)
    return out, res[-1]


def split_arrive(handle, kind, after, name):
    send, recv, srcs, dsts = handle
    n = len(srcs)

    def body(*refs):
        s_refs, d_refs, send_ref, recv_ref = refs[:n], refs[n:2 * n], refs[2 * n], refs[2 * n + 1]
        x, y, c, k, others = _place()
        for a in range(n):
            for j, (px, py) in enumerate(others):
                src, _, landed = _route(kind, s_refs[a], d_refs[a], c, k, 2 * px + py)
                cp = _rc(src, landed, send_ref.at[3 * a + j], recv_ref.at[3 * a + j], (px, py, c))
                cp.wait_send()
                cp.wait_recv()

    res = pl.pallas_call(
        body, name=name, out_shape=tuple([pltpu.HBM(t.shape, t.dtype) for t in srcs + dsts]),
        in_specs=[HBM_SPEC] * (2 * n) + [SEM_SPEC, SEM_SPEC, ANY], out_specs=tuple([HBM_SPEC] * (2 * n)),
        input_output_aliases={t: t for t in range(2 * n)},
        compiler_params=pltpu.CompilerParams(has_side_effects=EFFECT),
    )(*srcs, *dsts, send, recv, after)
    return list(res[n:])


def forward_cores(bufs):
    n = len(bufs)

    def copies(outs, sems):
        send, recv = sems
        x, y, c, k, others = _place()
        onward, land = [], []
        for a in range(n):
            for j, (px, py) in enumerate(others):
                blk = outs[a].at[c, 2 * px + py]
                onward.append(_rc(blk, blk, send.at[a, j], recv.at[a, j], (x, y, 1 - c)))
                blk2 = outs[a].at[1 - c, 2 * px + py]
                land.append(_rc(blk2, blk2, send.at[a, j], recv.at[a, j], (x, y, 1 - c)))
        return onward, land

    def start(ins, outs, sems):
        for cp in copies(outs, sems)[0]:
            cp.start()

    def finish(ins, outs, sems):
        onward, land = copies(outs, sems)
        for arrived in land:
            arrived.wait_recv()
        for cp in onward:
            cp.wait_send()

    return Exchange(list(bufs), [_sds(b.shape, b.dtype) for b in bufs],
                    [pltpu.SemaphoreType.DMA((n, 3)), pltpu.SemaphoreType.DMA((n, 3))], start, finish, inplace=True)


def all_reduce_small(buf, name):
    r = buf.shape[0]
    n_dev = 8

    def body(in_ref, o_ref, land, send_sems, recv_sems):
        x, y, c, _, _ = _place()
        me = 4 * x + 2 * y + c
        land[me] = in_ref[...]
        sends = []
        for d in range(1, n_dev):
            peer = (x ^ (d >> 2), y ^ ((d >> 1) & 1), c ^ (d & 1))
            cp = _rc(in_ref, land.at[me], send_sems.at[d], recv_sems.at[d], peer)
            cp.start()
            sends.append(cp)
        for d in range(1, n_dev):
            blk = land.at[me ^ d]
            _rc(blk, blk, send_sems.at[d], recv_sems.at[d], (x, y, c)).wait_recv()
        for cp in sends:
            cp.wait_send()
        tot = land[0]
        for d in range(1, n_dev):
            tot = tot + land[d]
        o_ref[...] = tot

    vm = pl.BlockSpec(memory_space=pltpu.VMEM)
    return pl.pallas_call(
        body, out_shape=_sds(buf.shape, F32), in_specs=[vm], out_specs=vm,
        scratch_shapes=[pltpu.VMEM((n_dev, r, LANES), F32), pltpu.SemaphoreType.DMA((n_dev,)),
                        pltpu.SemaphoreType.DMA((n_dev,))],
        name=name)(buf)


def rmsnorm_fwd(x, w, name):
    s, d = x.shape
    tm = _row_tile(s, 512)

    def body(x_ref, w_ref, o_ref):
        o_ref[...] = _rms_fwd(x_ref[...], w_ref[...]).astype(BF16)

    return _pcall(body, out_shape=[_sds((s, d), BF16)], grid=(s // tm,),
                  in_specs=[pl.BlockSpec((tm, d), lambda i: (i, 0)), pl.BlockSpec((1, d), lambda i: (0, 0))],
                  out_specs=[pl.BlockSpec((tm, d), lambda i: (i, 0))], args=[x, w.reshape(1, d)], name=name)[0]


def _ffn_w_spec(chip_of, single=False):
    mode = dict(pipeline_mode=pl.Buffered(1)) if single else {}
    return pl.BlockSpec((N_CORES, 1, FF_PART, D_MODEL), lambda *ids: (0, chip_of(*ids), 0, 0), **mode)


def ffn_fwd(h, x, wg, wu, wd, norm_ws, name, ride=None):
    s, d = h.shape
    n_norm = len(norm_ws)
    tm = _row_tile(s, 1024)

    def body(*refs):
        h_ref, x_ref, wg_ref, wu_ref, wd_ref = refs[:5]
        nw_refs = refs[5:5 + n_norm]
        o_ref = refs[5 + n_norm]
        h_refs = refs[6 + n_norm:6 + 2 * n_norm]
        gu_ref, acc = refs[6 + 2 * n_norm], refs[7 + 2 * n_norm]
        k = pl.program_id(1)

        @pl.when(k == 0)
        def _():
            acc[...] = jnp.zeros(acc.shape, F32)

        hm = tm // 2
        for part in range(2):
            sub = pl.ds(part * hm, hm)
            hb = h_ref[sub, :]
            g = _dot(hb, wg_ref[...].reshape(FF_SHARD, d), NT)
            u = _dot(hb, wu_ref[...].reshape(FF_SHARD, d), NT)
            gu_ref[0, 0, sub, :] = g.astype(BF16)
            gu_ref[0, 1, sub, :] = u.astype(BF16)
            acc[sub, :] += _dot((g * _sigmoid(g) * u).astype(BF16), wd_ref[...].reshape(FF_SHARD, d))

        @pl.when(k == N_CHIPS - 1)
        def _():
            xn = x_ref[...] + 0.5 * acc[...]
            o_ref[...] = xn
            for nw_ref, hn_ref in zip(nw_refs, h_refs):
                hn_ref[...] = _rms_fwd(xn, nw_ref[...]).astype(BF16)

    row = pl.BlockSpec((tm, d), lambda i, k: (i, 0))
    vec = pl.BlockSpec((1, d), lambda i, k: (0, 0))
    wsp = _ffn_w_spec(lambda i, k: k)
    return _pcall(
        body, out_shape=[_sds((s, d), F32)] + [_sds((s, d), BF16)] * n_norm + [_sds((N_CHIPS, 2, s, FF_SHARD), BF16)],
        grid=(s // tm, N_CHIPS),
        in_specs=[row, row, wsp, wsp, wsp] + [vec] * n_norm,
        out_specs=[row] * (1 + n_norm) + [pl.BlockSpec((1, 2, tm, FF_SHARD), lambda i, k: (k, 0, i, 0))],
        scratch_shapes=[pltpu.VMEM((tm, d), F32)],
        args=[h, x, wg, wu, wd] + [nw.reshape(1, d) for nw in norm_ws], name=name, ride=ride)


def ffn_bwd(dxn, h, x_in, nw, gu, wg, wu, wd, name, ride=None):
    s, d = h.shape
    tm = _row_tile(s, 512)
    ni = s // tm
    last_e = N_CHIPS - 1

    def body(dxn_ref, h_ref, x_ref, nw_ref, gu_ref, wg_ref, wu_ref, wd_ref,
             dx_ref, dnw_ref, dwg_ref, dwu_ref, dwd_ref, dh, wacc):
        e = pl.program_id(0)
        i = pl.program_id(1)
        rows = pl.ds(pl.multiple_of(i * tm, tm), tm)

        @pl.when(i == 0)
        def _():
            wacc[...] = jnp.zeros(wacc.shape, F32)

        @pl.when(e == 0)
        def _():
            dh[rows, :] = jnp.zeros((tm, d), F32)

        hm = tm // 2
        for part in range(2):
            sub = pl.ds(part * hm, hm)
            dxb = dxn_ref[sub, :].astype(BF16)
            hb = h_ref[sub, :]
            g = gu_ref[0, 0, sub, :].astype(F32)
            u = gu_ref[0, 1, sub, :].astype(F32)
            drows = pl.ds(pl.multiple_of(i * tm + part * hm, hm), hm)
            sg = _sigmoid(g)
            silu = g * sg
            wacc[2] += _dot((0.5 * silu * u).astype(BF16), dxb, TN)
            da = 0.5 * _dot(dxb, wd_ref[...].reshape(FF_SHARD, d), NT)
            dg = (da * u * (sg * (1.0 + g * (1.0 - sg)))).astype(BF16)
            wacc[0] += _dot(dg, hb, TN)
            du = (da * silu).astype(BF16)
            dh[drows, :] += _dot(dg, wg_ref[...].reshape(FF_SHARD, d))
            wacc[1] += _dot(du, hb, TN)
            dh[drows, :] += _dot(du, wu_ref[...].reshape(FF_SHARD, d))

        @pl.when(i == ni - 1)
        def _():
            for t, dw_ref in enumerate((dwg_ref, dwu_ref, dwd_ref)):
                dw_ref[...] = wacc[t].astype(BF16).reshape(N_CORES, 1, FF_PART, d)

        @pl.when(e == last_e)
        def _():
            dx, dnw = _rms_bwd(dh[rows, :], x_ref[...], nw_ref[...])
            dx_ref[...] = dxn_ref[...] + dx
            col = jnp.sum(dnw, axis=0, keepdims=True)

            @pl.when(i == 0)
            def _():
                dnw_ref[...] = col

            @pl.when(i > 0)
            def _():
                dnw_ref[...] += col

    row = pl.BlockSpec((tm, d), lambda e, i: (i, 0))
    late = pl.BlockSpec((tm, d), lambda e, i: (jnp.where(e == last_e, i, 0), 0))
    vec = pl.BlockSpec((1, d), lambda e, i: (0, 0))
    wsp = _ffn_w_spec(lambda e, i: e, single=True)
    dwsp = _ffn_w_spec(lambda e, i: e, single=True)
    dw = _sds((N_CORES, N_CHIPS, FF_PART, d), BF16)
    return _pcall(
        body, out_shape=[_sds((s, d), F32), _sds((1, d), F32), dw, dw, dw],
        grid=(N_CHIPS, ni),
        in_specs=[row, row, late, vec, pl.BlockSpec((1, 2, tm, FF_SHARD), lambda e, i: (e, 0, i, 0)), wsp, wsp, wsp],
        out_specs=[late, vec, dwsp, dwsp, dwsp],
        scratch_shapes=[pltpu.VMEM((s, d), F32), pltpu.VMEM((3, FF_SHARD, d), F32)],
        args=[dxn, h, x_in, nw.reshape(1, d), gu, wg, wu, wd], name=name, ride=ride)


def mm_res(a, w, x, name, bias=None, norm_ws=(), ride=None):
    s, k = a.shape
    n = w.shape[1]
    tm = _row_tile(s, 256)
    has_bias = bias is not None
    n_norm = len(norm_ws)

    def body(*refs):
        a_ref, w_ref, x_ref = refs[:3]
        pos = 3
        t = _dot(a_ref[...], w_ref[...])
        if has_bias:
            t = t + refs[pos][...]
            pos += 1
        nw_refs = refs[pos:pos + n_norm]
        o_ref = refs[pos + n_norm]
        h_refs = refs[pos + n_norm + 1:]
        xn = x_ref[...] + t
        o_ref[...] = xn
        for nw_ref, h_ref in zip(nw_refs, h_refs):
            h_ref[...] = _rms_fwd(xn, nw_ref[...]).astype(BF16)

    row = pl.BlockSpec((tm, n), lambda i: (i, 0))
    vec = pl.BlockSpec((1, n), lambda i: (0, 0))
    in_specs = [pl.BlockSpec((tm, k), lambda i: (i, 0)), pl.BlockSpec((k, n), lambda i: (0, 0)), row]
    args = [a, w, x]
    if has_bias:
        in_specs.append(vec)
        args.append(bias.reshape(1, n))
    for nw in norm_ws:
        in_specs.append(vec)
        args.append(nw.reshape(1, n))
    return _pcall(body, out_shape=[_sds((s, n), F32)] + [_sds((s, n), BF16)] * n_norm, grid=(s // tm,),
                  in_specs=in_specs, out_specs=[row] * (1 + n_norm), args=args, name=name, ride=ride)


def mm_nn(a, w, name, bias=None, out_dtype=F32):
    s, k = a.shape
    n = w.shape[1]
    tm = _row_tile(s, 512)
    tn = _col_tile(n)
    has_bias = bias is not None

    def body(*refs):
        a_ref, w_ref = refs[:2]
        o_ref = refs[-1]
        t = _dot(a_ref[...], w_ref[...])
        if has_bias:
            t = t + refs[2][...]
        o_ref[...] = t.astype(out_dtype)

    in_specs = [pl.BlockSpec((tm, k), lambda j, i: (i, 0)), pl.BlockSpec((k, tn), lambda j, i: (0, j))]
    args = [a, w]
    if has_bias:
        in_specs.append(pl.BlockSpec((1, tn), lambda j, i: (0, j)))
        args.append(bias.reshape(1, n))
    return _pcall(body, out_shape=[_sds((s, n), out_dtype)], grid=(n // tn, s // tm), in_specs=in_specs,
                  out_specs=[pl.BlockSpec((tm, tn), lambda j, i: (i, j))], args=args, name=name)[0]


def mm_nt(a, w, name, n=None, row0=0, out_dtype=F32, ride=None):
    s, k = a.shape
    n = w.shape[0] if n is None else n
    tm = _row_tile(s, 512)
    tn = _col_tile(n)
    base = row0 // tn
    assert row0 % tn == 0

    def body(a_ref, w_ref, o_ref):
        o_ref[...] = _dot(a_ref[...].astype(BF16), w_ref[...], NT).astype(out_dtype)

    res = _pcall(body, out_shape=[_sds((s, n), out_dtype)], grid=(n // tn, s // tm),
                 in_specs=[pl.BlockSpec((tm, k), lambda j, i: (i, 0)), pl.BlockSpec((tn, k), lambda j, i: (base + j, 0))],
                 out_specs=[pl.BlockSpec((tm, tn), lambda j, i: (i, j))], args=[a, w], name=name, ride=ride)
    return res[0] if ride is None else (res[0][0], res[1])


def mm_tn(a, b, name, into=None, rows=None, row0=0, m_valid=None, col_sum=False, ride=None):
    s, m = a.shape
    n = b.shape[1]
    mv = m if m_valid is None else m_valid
    tm = _col_tile(m) if m_valid is None else mv
    tn = n if n <= 1024 else _col_tile(n)
    rows = mv if rows is None else rows
    assert row0 % tm == 0 and (m_valid is None or m == LANES)
    assert not col_sum or mv == tm
    base = row0 // tm
    ta = m if m_valid is not None else tm

    def body(*refs):
        a_ref, b_ref = refs[0], refs[1]
        o_ref = refs[-2] if col_sum else refs[-1]
        bf = b_ref[...]
        t = _dot(a_ref[...].astype(BF16), bf.astype(BF16), TN)
        o_ref[...] = t[:tm].astype(BF16)
        if col_sum:
            refs[-1][...] = jnp.sum(bf.astype(F32), axis=0, keepdims=True)

    in_specs = [pl.BlockSpec((s, ta), lambda i, j: (0, i)), pl.BlockSpec((s, tn), lambda i, j: (0, j))]
    args = [a, b]
    aliases = None
    if into is not None:
        in_specs.append(ANY)
        args.append(into)
        aliases = {2: 0}
    out_shape = [_sds((rows, n), BF16)]
    out_specs = [pl.BlockSpec((tm, tn), lambda i, j: (base + i, j))]
    if col_sum:
        out_shape.append(_sds((1, n), F32))
        out_specs.append(pl.BlockSpec((1, tn), lambda i, j: (0, j)))
    res = _pcall(body, out_shape=out_shape, grid=(mv // tm, n // tn), in_specs=in_specs, out_specs=out_specs,
                 args=args, name=name, ride=ride, aliases=aliases)
    outs = res if ride is None else res[0]
    out = (outs[0], outs[1][0]) if col_sum else outs[0]
    return out if ride is None else (out, res[1])


def mm_rms_bwd(terms, dxn, x, nw, name, ride=None):
    s, n = x.shape
    nt_ = len(terms)
    tm = _row_tile(s, 256)
    forms = [t[5] for t in terms]

    def body(*refs):
        dxn_ref, x_ref, nw_ref, dx_ref, dnw_ref = refs[2 * nt_:]
        i = pl.program_id(0)
        dh = None
        for t in range(nt_):
            part = _dot(refs[2 * t][...].astype(BF16), refs[2 * t + 1][...], NN if forms[t] == "nn" else NT)
            dh = part if dh is None else dh + part
        dx, dnw = _rms_bwd(dh, x_ref[...], nw_ref[...])
        dx_ref[...] = dxn_ref[...] + dx
        col = jnp.sum(dnw, axis=0, keepdims=True)

        @pl.when(i == 0)
        def _():
            dnw_ref[...] = col

        @pl.when(i > 0)
        def _():
            dnw_ref[...] += col

    in_specs, args = [], []
    for a, cb, w, rb, kb, form in terms:
        in_specs.append(pl.BlockSpec((tm, kb), lambda i, cb=cb: (i, cb)))
        if form == "nn":
            in_specs.append(pl.BlockSpec((kb, n), lambda i, rb=rb: (rb, 0)))
        else:
            in_specs.append(pl.BlockSpec((n, kb), lambda i, rb=rb: (0, rb)))
        args += [a, w]
    row = pl.BlockSpec((tm, n), lambda i: (i, 0))
    vec = pl.BlockSpec((1, n), lambda i: (0, 0))
    res = _pcall(body, out_shape=[_sds((s, n), F32), _sds((1, n), F32)], grid=(s // tm,),
                 in_specs=in_specs + [row, row, vec], out_specs=[row, vec],
                 args=args + [dxn, x, nw.reshape(1, n)], name=name, ride=ride)
    outs = res if ride is None else res[0]
    out = (outs[0], outs[1][0])
    return out if ride is None else (out, res[1])


def rope_tables(s):
    pos = jnp.arange(s, dtype=F32)
    inv = 1.0 / (ROPE_THETA ** (jnp.arange(0, ATT_HEAD_DIM, 2, dtype=F32) / ATT_HEAD_DIM))
    ang = pos[:, None] * inv[None, :]
    cos = jnp.tile(jnp.cos(ang), (1, 2 * LANES // ATT_HEAD_DIM))
    sin = jnp.tile(jnp.sin(ang), (1, 2 * LANES // ATT_HEAD_DIM))
    return cos, sin


def rope_apply(t, cos, sin, name, inverse=False, scale=1.0, out_dtype=BF16):
    s, n = t.shape
    tm = _row_tile(s, 512)
    half = ATT_HEAD_DIM // 2
    reps = n // LANES

    def body(t_ref, c_ref, s_ref, o_ref):
        tf = t_ref[...].astype(F32)
        c = jnp.tile(c_ref[...], (1, reps))
        sn = jnp.tile(s_ref[...], (1, reps))
        lane = lax.broadcasted_iota(jnp.int32, tf.shape, 1)
        first = (lane & (ATT_HEAD_DIM - 1)) < half
        rot = jnp.where(first, -pltpu.roll(tf, n - half, 1), pltpu.roll(tf, half, 1))
        sign = -1.0 if inverse else 1.0
        o_ref[...] = (scale * (tf * c + sign * rot * sn)).astype(out_dtype)

    tab = pl.BlockSpec((tm, LANES), lambda i: (i, 0))
    return _pcall(body, out_shape=[_sds((s, n), out_dtype)], grid=(s // tm,),
                  in_specs=[pl.BlockSpec((tm, n), lambda i: (i, 0)), tab, tab],
                  out_specs=[pl.BlockSpec((tm, n), lambda i: (i, 0))], args=[t, cos, sin], name=name)[0]


CONV_TILE = 256


def _shift_down(u, k):
    if k == 0:
        return u
    row = lax.broadcasted_iota(jnp.int32, u.shape, 0)
    return jnp.where(row >= k, pltpu.roll(u, k, 0), 0.0)


def _shift_up(u, k):
    if k == 0:
        return u
    s = u.shape[0]
    row = lax.broadcasted_iota(jnp.int32, u.shape, 0)
    return jnp.where(row < s - k, pltpu.roll(u, s - k, 0), 0.0)


def _conv_taps(u):
    return [_shift_down(u, CONV_WIDTH - 1 - k) for k in range(CONV_WIDTH)]


def _conv_pre(taps, w_ref, b_ref):
    pre = b_ref[...] + w_ref[0:1, :] * taps[0]
    for k in range(1, CONV_WIDTH):
        pre += w_ref[k:k + 1, :] * taps[k]
    return pre


def conv_fwd(u, w, b, name, ride=None):
    s, c = u.shape

    def body(u_ref, w_ref, b_ref, o_ref):
        pre = _conv_pre(_conv_taps(u_ref[...]), w_ref, b_ref)
        o_ref[...] = pre * _sigmoid(pre)

    col = pl.BlockSpec((s, CONV_TILE), lambda j: (0, j))
    res = _pcall(body, out_shape=[_sds((s, c), F32)], grid=(c // CONV_TILE,),
                 in_specs=[col, pl.BlockSpec((CONV_WIDTH, CONV_TILE), lambda j: (0, j)),
                           pl.BlockSpec((1, CONV_TILE), lambda j: (0, j))],
                 out_specs=[col], args=[u, w, b.reshape(1, c)], name=name, ride=ride)
    return res[0] if ride is None else (res[0][0], res[1])


def conv_bwd(dxs, db_, dc_, u, w, b, name):
    s, c = u.shape
    n_x = dxs.shape[1] // CONV_TILE
    n_b = db_.shape[1] // CONV_TILE

    def body(dx_ref, dbb_ref, dcc_ref, u_ref, w_ref, b_ref, du_ref, dw_ref, dbias_ref):
        j = pl.program_id(0)
        dact = jnp.where(j < n_x, dx_ref[...], jnp.where(j < n_x + n_b, dbb_ref[...], dcc_ref[...]))
        taps = _conv_taps(u_ref[...])
        pre = _conv_pre(taps, w_ref, b_ref)
        sg = _sigmoid(pre)
        dpre = dact * (sg * (1.0 + pre * (1.0 - sg)))
        du = w_ref[CONV_WIDTH - 1:CONV_WIDTH, :] * dpre
        for k in range(CONV_WIDTH - 1):
            du += w_ref[k:k + 1, :] * _shift_up(dpre, CONV_WIDTH - 1 - k)
        du_ref[...] = du
        dbias_ref[...] = jnp.sum(dpre, axis=0, keepdims=True)
        for k in range(CONV_WIDTH):
            dw_ref[k:k + 1, :] = jnp.sum(dpre * taps[k], axis=0, keepdims=True)

    col = pl.BlockSpec((s, CONV_TILE), lambda j: (0, j))
    wsp = pl.BlockSpec((CONV_WIDTH, CONV_TILE), lambda j: (0, j))
    bsp = pl.BlockSpec((1, CONV_TILE), lambda j: (0, j))
    du, dw, db = _pcall(
        body, out_shape=[_sds((s, c), F32), _sds((CONV_WIDTH, c), F32), _sds((1, c), F32)], grid=(c // CONV_TILE,),
        in_specs=[pl.BlockSpec((s, CONV_TILE), lambda j: (0, jnp.minimum(j, n_x - 1))),
                  pl.BlockSpec((s, CONV_TILE), lambda j: (0, jnp.clip(j - n_x, 0, n_b - 1))),
                  pl.BlockSpec((s, CONV_TILE), lambda j: (0, jnp.clip(j - n_x - n_b, 0, n_b - 1))),
                  col, wsp, bsp],
        out_specs=[col, wsp, bsp], args=[dxs, db_, dc_, u, w, b.reshape(1, c)], name=name)
    return du, dw, db[0]


def _lane_pick(mat, idx):
    lane = lax.broadcasted_iota(jnp.int32, mat.shape, 1)
    return jnp.sum(jnp.where(lane == idx, mat, 0.0), axis=1, keepdims=True)


def _sub_pick(mat, idx):
    sub = lax.broadcasted_iota(jnp.int32, mat.shape, 0)
    return jnp.sum(jnp.where(sub == idx, mat, 0.0), axis=0, keepdims=True)


def _expand_heads(cols):
    rows = cols[0].shape[0]
    left = lax.broadcasted_iota(jnp.int32, (rows, LANES), 1) < SSM_HEAD_DIM
    return jnp.concatenate(
        [jnp.where(left, cols[2 * p], cols[2 * p + 1]) for p in range(HEADS_PER_GROUP // 2)], axis=1)


def _dot_01(x, ones, ones_first, pieces):
    tot, rest = None, x
    for _ in range(pieces):
        piece = rest.astype(BF16)
        rest = rest - piece.astype(F32)
        part = _dot(ones, piece) if ones_first else _dot(piece, ones)
        tot = part if tot is None else tot + part
    return tot


def _heads_to_lanes(mat, g):
    jj = lax.broadcasted_iota(jnp.int32, (GROUP_DIM, LANES), 0)
    ll = lax.broadcasted_iota(jnp.int32, (GROUP_DIM, LANES), 1)
    sel = (ll == HEADS_PER_GROUP * g + (jj >> 6)).astype(BF16)
    return _dot_01(mat, sel, False, 3)


def _softplus(x):
    return jnp.maximum(x, 0.0) + jnp.log1p(jnp.exp(-jnp.abs(x)))


def _ssd_scalars(dt_ref, bias_ref, a_ref, dtall, csall, cst):
    dta = _softplus(dt_ref[...] + bias_ref[...])
    row = lax.broadcasted_iota(jnp.int32, (CHUNK, CHUNK), 0)
    col = lax.broadcasted_iota(jnp.int32, (CHUNK, CHUNK), 1)
    cs = _dot_01(dta * a_ref[...], (row >= col).astype(BF16), True, 3)
    dtall[...] = dta
    csall[...] = cs
    cst[...] = cs.T


def _decay_mat(cs_col, cs_row):
    row = lax.broadcasted_iota(jnp.int32, (CHUNK, CHUNK), 0)
    col = lax.broadcasted_iota(jnp.int32, (CHUNK, CHUNK), 1)
    return jnp.exp(jnp.where(row >= col, cs_col - cs_row, NEG))


def _head_mask(xpair, right):
    lane = lax.broadcasted_iota(jnp.int32, xpair.shape, 1)
    keep = (lane >= SSM_HEAD_DIM) if right else (lane < SSM_HEAD_DIM)
    return jnp.where(keep, xpair, 0.0)


def _chunk_cols(x_all, g):
    return [_lane_pick(x_all, HEADS_PER_GROUP * g + r) for r in range(HEADS_PER_GROUP)]


def _decay_col(cs_cols):
    return jnp.concatenate(
        [jnp.broadcast_to(jnp.exp(cc[CHUNK - 1:CHUNK, :]), (SSM_HEAD_DIM, 1)) for cc in cs_cols], axis=0)


def ssd_fwd(act, z, dtp, bias_p, a_p, d_p, normw, name, ride=None):
    s = act.shape[0]
    nc = s // CHUNK
    b_off = D_INNER // SSM_STATE
    c_off = b_off + SSM_GROUPS

    def body(xs_ref, b_ref, c_ref, z_ref, dt_ref, bias_ref, a_ref, d_ref, nw_ref,
             yn_ref, y_ref, st_ref, state, dtall, csall, cst):
        c = pl.program_id(0)
        g = pl.program_id(1)

        @pl.when(g == 0)
        def _():
            _ssd_scalars(dt_ref, bias_ref, a_ref, dtall, csall, cst)

        @pl.when(c == 0)
        def _():
            state[g] = jnp.zeros((GROUP_DIM, SSM_STATE), F32)

        cs_cols = _chunk_cols(csall[...], g)
        dt_cols = _chunk_cols(dtall[...], g)
        cs_rows = [_sub_pick(cst[...], HEADS_PER_GROUP * g + r) for r in range(HEADS_PER_GROUP)]
        d_cols = _chunk_cols(d_ref[...], g)
        cs_exp = _expand_heads(cs_cols)
        dt_exp = _expand_heads(dt_cols)
        d_exp = _expand_heads(d_cols)
        xs = xs_ref[...]
        bb = b_ref[...].astype(BF16)
        cb16 = c_ref[...].astype(BF16)
        xdt = xs * dt_exp
        s_prev = state[g]
        st_ref[0, 0] = s_prev
        y_off = _dot(cb16, s_prev.astype(BF16), NT) * jnp.exp(cs_exp)
        decay_st = jnp.exp(cs_exp[CHUNK - 1:CHUNK, :] - cs_exp)
        contrib = _dot((xdt * decay_st).astype(BF16), bb, TN)
        state[g] = _decay_col(cs_cols) * s_prev + contrib
        cbm = _dot(cb16, bb, NT)
        pairs = []
        for p in range(HEADS_PER_GROUP // 2):
            xpair = xdt[:, LANES * p:LANES * (p + 1)]
            m0 = (cbm * _decay_mat(cs_cols[2 * p], cs_rows[2 * p])).astype(BF16)
            m1 = (cbm * _decay_mat(cs_cols[2 * p + 1], cs_rows[2 * p + 1])).astype(BF16)
            pairs.append(_dot(m0, _head_mask(xpair, False).astype(BF16))
                         + _dot(m1, _head_mask(xpair, True).astype(BF16)))
        y = jnp.concatenate(pairs, axis=1) + y_off + xs * d_exp
        y_ref[...] = y
        zf = z_ref[...]
        yg = y * (zf * _sigmoid(zf))
        yn_ref[...] = _rms_fwd(yg, nw_ref[...]).astype(BF16)

    grp = pl.BlockSpec((CHUNK, GROUP_DIM), lambda c, g: (c, g))
    par = pl.BlockSpec((1, LANES), lambda c, g: (0, 0))
    return _pcall(
        body,
        out_shape=[_sds((s, D_INNER), BF16), _sds((s, D_INNER), F32),
                   _sds((nc, SSM_GROUPS, GROUP_DIM, SSM_STATE), F32)],
        grid=(nc, SSM_GROUPS),
        in_specs=[grp,
                  pl.BlockSpec((CHUNK, SSM_STATE), lambda c, g: (c, b_off + g)),
                  pl.BlockSpec((CHUNK, SSM_STATE), lambda c, g: (c, c_off + g)),
                  grp,
                  pl.BlockSpec((CHUNK, LANES), lambda c, g: (c, 0)),
                  par, par, par,
                  pl.BlockSpec((1, GROUP_DIM), lambda c, g: (0, g))],
        out_specs=[grp, grp, pl.BlockSpec((1, 1, GROUP_DIM, SSM_STATE), lambda c, g: (c, g, 0, 0))],
        scratch_shapes=[pltpu.VMEM((SSM_GROUPS, GROUP_DIM, SSM_STATE), F32),
                        pltpu.VMEM((CHUNK, LANES), F32), pltpu.VMEM((CHUNK, LANES), F32),
                        pltpu.VMEM((LANES, CHUNK), F32)],
        args=[act, act, act, z, dtp, bias_p, a_p, d_p, normw], name=name, ride=ride)


def ssd_bwd(dyn, act, z, y_pre, states, dtp, bias_p, a_p, d_p, normw, name, ride=None):
    s = act.shape[0]
    nc = s // CHUNK
    b_off = D_INNER // SSM_STATE
    c_off = b_off + SSM_GROUPS

    def body(dyn_ref, xs_ref, b_ref, c_ref, z_ref, y_ref, st_ref, dt_ref, bias_ref, a_ref, d_ref, nw_ref,
             dxs_ref, db_ref, dc_ref, dz_ref, ddt_ref, dnw_ref, dbias_ref, da_ref, dd_ref,
             dstate, dtall, csall, cst):
        c = pl.program_id(0)
        g = pl.program_id(1)

        @pl.when(g == 0)
        def _():
            _ssd_scalars(dt_ref, bias_ref, a_ref, dtall, csall, cst)
            ddt_ref[...] = jnp.zeros((CHUNK, LANES), F32)

        @pl.when(c == 0)
        def _():
            dstate[g] = jnp.zeros((GROUP_DIM, SSM_STATE), F32)

        @pl.when(jnp.logical_and(c == 0, g == 0))
        def _():
            dnw_ref[...] = jnp.zeros(dnw_ref.shape, F32)
            dbias_ref[...] = jnp.zeros((1, LANES), F32)
            da_ref[...] = jnp.zeros((1, LANES), F32)
            dd_ref[...] = jnp.zeros((1, LANES), F32)

        cs_cols = _chunk_cols(csall[...], g)
        dt_cols = _chunk_cols(dtall[...], g)
        cs_rows = [_sub_pick(cst[...], HEADS_PER_GROUP * g + r) for r in range(HEADS_PER_GROUP)]
        d_cols = _chunk_cols(d_ref[...], g)
        cs_exp = _expand_heads(cs_cols)
        dt_exp = _expand_heads(dt_cols)
        d_exp = _expand_heads(d_cols)
        xs = xs_ref[...]
        bb = b_ref[...].astype(BF16)
        cb16 = c_ref[...].astype(BF16)
        xdt = xs * dt_exp
        s_prev = st_ref[0, 0]
        s_prev16 = s_prev.astype(BF16)
        ds_next = dstate[g]
        ds16 = ds_next.astype(BF16)

        zf = z_ref[...]
        sz = _sigmoid(zf)
        silu_z = zf * sz
        y = y_ref[...]
        yg = y * silu_z
        dout = dyn_ref[...]
        dyg, dnw = _rms_bwd(dout, yg, nw_ref[...])
        dnw_ref[pl.ds(g, 1), :] += jnp.sum(dnw, axis=0, keepdims=True)
        dy = dyg * silu_z
        dz_ref[...] = dyg * y * (sz * (1.0 + zf * (1.0 - sz)))
        dd_ref[...] += jnp.sum(_heads_to_lanes(dy * xs, g), axis=0, keepdims=True)

        exp_cs = jnp.exp(cs_exp)
        decay_st = jnp.exp(cs_exp[CHUNK - 1:CHUNK, :] - cs_exp)
        cs_t = _dot(cb16, s_prev16, NT)
        dyo = dy * exp_cs
        dc_acc = _dot(dyo.astype(BF16), s_prev16, NN)
        g1 = _dot(bb, ds16, NT)
        xds = xdt * decay_st
        db_acc = _dot(xds.astype(BF16), ds16, NN)
        dxdt_off = g1 * decay_st
        t_exp = g1 * xds
        dcs_exp = dy * cs_t * exp_cs - t_exp
        decay_c = _decay_col(cs_cols)
        dstate[g] = decay_c * ds_next + _dot(dyo.astype(BF16), cb16, TN)
        dlast_col = jnp.sum(ds_next * s_prev, axis=1, keepdims=True) * decay_c
        jj = lax.broadcasted_iota(jnp.int32, (GROUP_DIM, LANES), 0)
        ll = lax.broadcasted_iota(jnp.int32, (GROUP_DIM, LANES), 1)
        sel = ll == HEADS_PER_GROUP * g + (jj >> 6)
        dlast = jnp.sum(jnp.where(sel, dlast_col, 0.0), axis=0, keepdims=True)
        t_all = _heads_to_lanes(t_exp, g)
        dlast += jnp.sum(t_all, axis=0, keepdims=True)
        dcs_all = _heads_to_lanes(dcs_exp, g)

        cbm = _dot(cb16, bb, NT)
        dcb = jnp.zeros((CHUNK, CHUNK), F32)
        dcs_rows = jnp.zeros((LANES, CHUNK), F32)
        lane_l = lax.broadcasted_iota(jnp.int32, (CHUNK, LANES), 1)
        sub_l = lax.broadcasted_iota(jnp.int32, (LANES, CHUNK), 0)
        dxdt_pairs = []
        for p in range(HEADS_PER_GROUP // 2):
            xpair16 = xdt[:, LANES * p:LANES * (p + 1)].astype(BF16)
            dypair = dy[:, LANES * p:LANES * (p + 1)]
            acc = None
            for r in (2 * p, 2 * p + 1):
                lm = _decay_mat(cs_cols[r], cs_rows[r])
                m = cbm * lm
                dyh = _head_mask(dypair, r % 2 == 1).astype(BF16)
                dm = _dot(dyh, xpair16, NT)
                dcb += dm * lm
                q = dm * m
                idx = HEADS_PER_GROUP * g + r
                dcs_all += jnp.where(lane_l == idx, jnp.sum(q, axis=1, keepdims=True), 0.0)
                dcs_rows -= jnp.where(sub_l == idx, jnp.sum(q, axis=0, keepdims=True), 0.0)
                part = _dot(m.astype(BF16), dyh, TN)
                acc = part if acc is None else acc + part
            dxdt_pairs.append(acc)
        dxdt = jnp.concatenate(dxdt_pairs, axis=1) + dxdt_off
        dcb16 = dcb.astype(BF16)
        dc_ref[...] = dc_acc + _dot(dcb16, bb, NN)
        db_ref[...] = db_acc + _dot(dcb16, cb16, TN)
        dxs_ref[...] = dxdt * dt_exp + dy * d_exp

        dcs_all += dcs_rows.T
        row = lax.broadcasted_iota(jnp.int32, (CHUNK, CHUNK), 0)
        col = lax.broadcasted_iota(jnp.int32, (CHUNK, CHUNK), 1)
        last_row = lax.broadcasted_iota(jnp.int32, (CHUNK, LANES), 0) == CHUNK - 1
        dcs_all += jnp.where(last_row, dlast, 0.0)
        da_all = _dot_01(dcs_all, (col >= row).astype(BF16), True, 3)
        dta = dtall[...]
        in_group = jnp.logical_and(lane_l >= HEADS_PER_GROUP * g, lane_l < HEADS_PER_GROUP * (g + 1))
        ddt = jnp.where(in_group, da_all * a_ref[...] + _heads_to_lanes(dxdt * xs, g), 0.0)
        da_ref[...] += jnp.sum(jnp.where(in_group, da_all * dta, 0.0), axis=0, keepdims=True)
        ddt_raw = ddt * _sigmoid(dt_ref[...] + bias_ref[...])
        ddt_ref[...] += ddt_raw
        dbias_ref[...] += jnp.sum(ddt_raw, axis=0, keepdims=True)

    rev = lambda c, g: (nc - 1 - c, g)
    grp = pl.BlockSpec((CHUNK, GROUP_DIM), rev)
    st = pl.BlockSpec((CHUNK, SSM_STATE), rev)
    par = pl.BlockSpec((1, LANES), lambda c, g: (0, 0))
    dtb = pl.BlockSpec((CHUNK, LANES), lambda c, g: (nc - 1 - c, 0))
    f = lambda shape: _sds(shape, F32)
    return _pcall(
        body,
        out_shape=[f((s, D_INNER)), f((s, SSM_GROUPS * SSM_STATE)), f((s, SSM_GROUPS * SSM_STATE)),
                   f((s, D_INNER)), f((s, LANES)), f((8, GROUP_DIM)), f((1, LANES)), f((1, LANES)), f((1, LANES))],
        grid=(nc, SSM_GROUPS),
        in_specs=[grp, grp,
                  pl.BlockSpec((CHUNK, SSM_STATE), lambda c, g: (nc - 1 - c, b_off + g)),
                  pl.BlockSpec((CHUNK, SSM_STATE), lambda c, g: (nc - 1 - c, c_off + g)),
                  grp, grp,
                  pl.BlockSpec((1, 1, GROUP_DIM, SSM_STATE), lambda c, g: (nc - 1 - c, g, 0, 0)),
                  dtb, par, par, par,
                  pl.BlockSpec((1, GROUP_DIM), lambda c, g: (0, g))],
        out_specs=[grp, st, st, grp, dtb, pl.BlockSpec((8, GROUP_DIM), lambda c, g: (0, 0)), par, par, par],
        scratch_shapes=[pltpu.VMEM((SSM_GROUPS, GROUP_DIM, SSM_STATE), F32),
                        pltpu.VMEM((CHUNK, LANES), F32), pltpu.VMEM((CHUNK, LANES), F32),
                        pltpu.VMEM((LANES, CHUNK), F32)],
        args=[dyn, act, act, act, z, y_pre, states, dtp, bias_p, a_p, d_p, normw], name=name, ride=ride)


def _attn_probs(q, kp, kc, sink, n):
    sp = _dot(q, kp, NT)
    sc = _dot(q, kc, NT)
    i = lax.broadcasted_iota(jnp.int32, sp.shape, 0) & (WINDOW - 1)
    j = lax.broadcasted_iota(jnp.int32, sp.shape, 1)
    sp = jnp.where(jnp.logical_and(j > i, n > 0), sp, NEG)
    sc = jnp.where(j <= i, sc, NEG)
    m = jnp.maximum(jnp.maximum(jnp.max(sp, axis=1, keepdims=True), jnp.max(sc, axis=1, keepdims=True)), sink)
    pp = jnp.exp(sp - m)
    pc = jnp.exp(sc - m)
    ps = jnp.exp(sink - m)
    inv = 1.0 / (jnp.sum(pp, axis=1, keepdims=True) + jnp.sum(pc, axis=1, keepdims=True) + ps)
    return pp * inv, pc * inv, ps * inv


def attn_fwd(qt, kt, vt, sink_rows, name, ride=None):
    s = qt.shape[1]
    nb = s // WINDOW
    rows = Q_PER_KV * WINDOW

    def body(q_ref, kp_ref, kc_ref, vp_ref, vc_ref, sk_ref, o_ref):
        n = pl.program_id(1)
        q = q_ref[...].reshape(rows, ATT_HEAD_DIM)
        pp, pc, _ = _attn_probs(q, kp_ref[0], kc_ref[0], sk_ref[0], n)
        o = _dot(pp.astype(BF16), vp_ref[0]) + _dot(pc.astype(BF16), vc_ref[0])
        o_ref[...] = o.reshape(Q_PER_KV, WINDOW, ATT_HEAD_DIM).astype(BF16)

    qsp = pl.BlockSpec((Q_PER_KV, WINDOW, ATT_HEAD_DIM), lambda h, n: (h, n, 0))
    prev = pl.BlockSpec((1, WINDOW, ATT_HEAD_DIM), lambda h, n: (h, jnp.maximum(n - 1, 0), 0))
    cur = pl.BlockSpec((1, WINDOW, ATT_HEAD_DIM), lambda h, n: (h, n, 0))
    return _pcall(body, out_shape=[_sds(qt.shape, BF16)], grid=(N_KV_HEADS, nb),
                  in_specs=[qsp, prev, cur, prev, cur, pl.BlockSpec((1, rows, 1), lambda h, n: (h, 0, 0))],
                  out_specs=[qsp], args=[qt, kt, kt, vt, vt, sink_rows], name=name, ride=ride)


def attn_bwd(qt, kt, vt, sink_rows, dot_, name, ride=None):
    s = qt.shape[1]
    nb = s // WINDOW
    rows = Q_PER_KV * WINDOW

    def body(q_ref, kp_ref, kc_ref, vp_ref, vc_ref, sk_ref, do_ref, dq_ref, dk_ref, dv_ref, ds_ref, kacc, vacc):
        n = pl.program_id(1)

        @pl.when(n < nb)
        def _():
            q = q_ref[...].reshape(rows, ATT_HEAD_DIM)
            do = do_ref[...].reshape(rows, ATT_HEAD_DIM)
            kp, kc, vp, vc = kp_ref[0], kc_ref[0], vp_ref[0], vc_ref[0]
            pp, pc, ps = _attn_probs(q, kp, kc, sk_ref[0], n)
            dpp = _dot(do, vp, NT)
            dpc = _dot(do, vc, NT)
            delta = jnp.sum(pp * dpp, axis=1, keepdims=True) + jnp.sum(pc * dpc, axis=1, keepdims=True)
            dsp = (pp * (dpp - delta)).astype(BF16)
            dsc = (pc * (dpc - delta)).astype(BF16)
            dq = _dot(dsp, kp) + _dot(dsc, kc)
            dq_ref[...] = dq.reshape(Q_PER_KV, WINDOW, ATT_HEAD_DIM)
            dk_prev = _dot(dsp, q, TN)
            dv_prev = _dot(pp.astype(BF16), do, TN)

            @pl.when(n == 0)
            def _():
                dk_ref[0] = dk_prev
                dv_ref[0] = dv_prev

            @pl.when(n > 0)
            def _():
                dk_ref[0] = kacc[...] + dk_prev
                dv_ref[0] = vacc[...] + dv_prev

            kacc[...] = _dot(dsc, q, TN)
            vacc[...] = _dot(pc.astype(BF16), do, TN)
            dsk = -ps * delta
            sub = lax.broadcasted_iota(jnp.int32, (8, LANES), 0)
            tile = jnp.zeros((8, LANES), F32)
            for h in range(Q_PER_KV):
                tile += jnp.where(sub == h, jnp.sum(dsk[h * WINDOW:(h + 1) * WINDOW, :], axis=0, keepdims=True), 0.0)
            ds_ref[0, 0] = tile

        @pl.when(n == nb)
        def _():
            dk_ref[0] = kacc[...]
            dv_ref[0] = vacc[...]
            ds_ref[0, 0] = jnp.zeros((8, LANES), F32)

    last = nb - 1
    qsp = pl.BlockSpec((Q_PER_KV, WINDOW, ATT_HEAD_DIM), lambda h, n: (h, jnp.minimum(n, last), 0))
    prev = pl.BlockSpec((1, WINDOW, ATT_HEAD_DIM), lambda h, n: (h, jnp.clip(n - 1, 0, last), 0))
    cur = pl.BlockSpec((1, WINDOW, ATT_HEAD_DIM), lambda h, n: (h, jnp.minimum(n, last), 0))
    dkv = pl.BlockSpec((1, WINDOW, ATT_HEAD_DIM), lambda h, n: (h, jnp.maximum(n - 1, 0), 0))
    f = lambda shape: _sds(shape, F32)
    return _pcall(
        body, out_shape=[f(qt.shape), f(kt.shape), f(vt.shape), f((N_KV_HEADS, nb + 1, 8, LANES))],
        grid=(N_KV_HEADS, nb + 1),
        in_specs=[qsp, prev, cur, prev, cur, pl.BlockSpec((1, rows, 1), lambda h, n: (h, 0, 0)), qsp],
        out_specs=[qsp, dkv, dkv, pl.BlockSpec((1, 1, 8, LANES), lambda h, n: (h, n, 0, 0))],
        scratch_shapes=[pltpu.VMEM((WINDOW, ATT_HEAD_DIM), F32), pltpu.VMEM((WINDOW, ATT_HEAD_DIM), F32)],
        args=[qt, kt, kt, vt, vt, sink_rows, dot_], name=name, ride=ride)


def loss_head(x, w, tgt, name):
    s, d = x.shape
    tm = _row_tile(s, 256)

    def body(x_ref, w_ref, t_ref, loss_ref, dx_ref, dw_ref):
        i = pl.program_id(0)
        xf = x_ref[...]
        wv = w_ref[...]
        r = lax.rsqrt(jnp.mean(xf * xf, axis=-1, keepdims=True) + EPS)
        xhat = xf * r
        e = xhat * wv - t_ref[...]
        part = 0.5 * jnp.sum(jnp.mean(e * e, axis=-1, keepdims=True), axis=0, keepdims=True)
        dy = e * (1.0 / d)
        dxhat = dy * wv
        dx_ref[...] = r * (dxhat - xhat * jnp.mean(dxhat * xhat, axis=-1, keepdims=True))
        col = jnp.sum(dy * xhat, axis=0, keepdims=True)

        @pl.when(i == 0)
        def _():
            loss_ref[...] = jnp.broadcast_to(part, (1, LANES))
            dw_ref[...] = col

        @pl.when(i > 0)
        def _():
            loss_ref[...] += jnp.broadcast_to(part, (1, LANES))
            dw_ref[...] += col

    row = pl.BlockSpec((tm, d), lambda i: (i, 0))
    vec = pl.BlockSpec((1, d), lambda i: (0, 0))
    return _pcall(body, out_shape=[_sds((1, LANES), F32), _sds((s, d), F32), _sds((1, d), F32)], grid=(s // tm,),
                  in_specs=[row, vec, row], out_specs=[pl.BlockSpec((1, LANES), lambda i: (0, 0)), row, vec],
                  args=[x, w.reshape(1, d), tgt], name=name)


ELEMWISE_TILE = 720 * 1024


def _tile_rows(r, c, max_elems=262144, mult=16):
    best = None
    for t in range(mult, r + 1, mult):
        if r % t == 0 and t * c <= max_elems:
            best = t
    return best or r


def add_pair(xhs, ps, c_idx, name):
    n = len(xhs)
    _, r, c = xhs[0].shape
    tr = _tile_rows(r, c, max_elems=ELEMWISE_TILE)

    def body(c_ref, *refs):
        for x_ref, p_ref, o_ref in zip(refs[:n], refs[n:2 * n], refs[2 * n:]):
            o_ref[...] = (x_ref[0].astype(F32) + p_ref[...].astype(F32)).astype(BF16)

    blk = pl.BlockSpec((tr, c), lambda i, cr: (i, 0))
    return pl.pallas_call(
        body, out_shape=tuple([_sds((r, c), BF16)] * n),
        grid_spec=pltpu.PrefetchScalarGridSpec(
            num_scalar_prefetch=1, grid=(r // tr,),
            in_specs=[pl.BlockSpec((1, tr, c), lambda i, cr: (cr[0], i, 0))] * n + [blk] * n,
            out_specs=tuple([blk] * n)),
        name=name, compiler_params=_cp(1))(c_idx, *xhs, *ps)


def sum_chips(qs, owns, chip_idx, name):
    n = len(qs)
    _, r, c = qs[0].shape
    tr = _tile_rows(r, c, max_elems=ELEMWISE_TILE // max(1, n - 1))

    def body(k_ref, *refs):
        k = k_ref[0]
        for q_ref, own_ref, o_ref in zip(refs[:n], refs[n:2 * n], refs[2 * n:]):
            mine = own_ref[0].astype(F32)
            tot = None
            for j in range(N_CHIPS):
                term = jnp.where(k == j, mine, q_ref[j].astype(F32))
                tot = term if tot is None else tot + term
            o_ref[...] = tot

    return pl.pallas_call(
        body, out_shape=tuple([_sds((r, c), F32)] * n),
        grid_spec=pltpu.PrefetchScalarGridSpec(
            num_scalar_prefetch=1, grid=(r // tr,),
            in_specs=([pl.BlockSpec((N_CHIPS, tr, c), lambda i, kr: (0, i, 0))] * n
                      + [pl.BlockSpec((1, tr, c), lambda i, kr: (kr[0], i, 0))] * n),
            out_specs=tuple([pl.BlockSpec((tr, c), lambda i, kr: (i, 0))] * n)),
        name=name, compiler_params=_cp(1))(chip_idx, *qs, *owns)


def adamw(w, g, m, v, name):
    r, c = w.shape
    tr = _tile_rows(r, c, mult=8)
    c1 = 1.0 / (1.0 - ADAM_B1 ** ADAM_STEP)
    c2 = 1.0 / (1.0 - ADAM_B2 ** ADAM_STEP)

    def body(w_ref, g_ref, m_ref, v_ref, d_ref, mo_ref, vo_ref):
        gf = g_ref[...]
        mn = ADAM_B1 * m_ref[...] + (1.0 - ADAM_B1) * gf
        vn = ADAM_B2 * v_ref[...] + (1.0 - ADAM_B2) * (gf * gf)
        mo_ref[...] = mn
        vo_ref[...] = vn
        d_ref[...] = -ADAM_LR * ((mn * c1) / (jnp.sqrt(vn * c2) + ADAM_EPS) + ADAM_WD * w_ref[...])

    blk = pl.BlockSpec((tr, c), lambda i: (i, 0))
    out = _sds((r, c), F32)
    return _pcall(body, out_shape=[out, out, out], grid=(r // tr,), in_specs=[blk] * 4, out_specs=[blk] * 3,
                  args=[w, g, m, v], name=name)


WEIGHTS = ['norm_w', 'ffn_w_gate', 'ffn_w_up', 'ffn_w_down', 'ssm_w_in', 'ssm_conv_w', 'ssm_conv_b', 'ssm_dt_bias',
           'ssm_a_log', 'ssm_d', 'ssm_norm_w', 'ssm_w_out', 'kv_norm_w', 'w_k', 'b_k', 'w_v', 'b_v', 'attn_w_q',
           'attn_b_q', 'attn_sinks', 'attn_w_o', 'attn_b_o', 'final_norm_w']
BIG = ['ffn_w_gate', 'ffn_w_up', 'ffn_w_down', 'ssm_w_in', 'ssm_w_out', 'w_k', 'w_v', 'attn_w_q', 'attn_w_o']
TRANSPOSED = ('ffn_w_gate', 'ffn_w_up', 'ssm_w_in')
SMALL = [n for n in WEIGHTS if n not in BIG]
SMALL_SHARDED = {'norm_w': 2, 'ssm_conv_w': 2, 'ssm_conv_b': 1, 'ssm_norm_w': 1}
ROW_ALIGN = 8 * LANES


def _pack_rows(parts):
    flat = jnp.concatenate([p.reshape(-1).astype(F32) for p in parts])
    pad = (-flat.size) % ROW_ALIGN
    return jnp.pad(flat, (0, pad)).reshape(-1, LANES)


def _unpack_rows(buf, shapes):
    flat = buf.reshape(-1)
    out, pos = [], 0
    for shp in shapes:
        size = math.prod(shp)
        out.append(flat[pos:pos + size].reshape(shp))
        pos += size
    return out


def _as2d(a):
    return a.reshape(-1, a.shape[-1])


def _heads_major(t, n_heads):
    s = t.shape[0]
    return t.reshape(s, n_heads, ATT_HEAD_DIM).transpose(1, 0, 2)


def _tokens_major(t):
    h, s, dh = t.shape
    return t.transpose(1, 0, 2).reshape(s, h * dh)


def _pad_lanes(v):
    return jnp.pad(v.reshape(1, -1), ((0, 0), (0, LANES - v.size)))


def _chips_first(t):
    return t.swapaxes(0, 1).reshape((-1,) + t.shape[3:])


def _parts_first(t, rows):
    return t.reshape((N_CHIPS, N_CORES, rows) + t.shape[1:]).swapaxes(0, 1)


def kernel(*args):
    names = (['x'] + WEIGHTS + ['loss_target'] + ['m_' + n for n in WEIGHTS] + ['v_' + n for n in WEIGHTS])
    a = dict(zip(names, args))
    for n in TRANSPOSED:
        for pre in ('', 'm_', 'v_'):
            a[pre + n] = a[pre + n].swapaxes(-1, -2)
    xi, yi, ci = lax.axis_index("x"), lax.axis_index("y"), lax.axis_index("c")
    chip = 2 * xi + yi
    south = ci == 0
    c_idx = jnp.reshape(ci, (1,)).astype(jnp.int32)
    chip_idx = jnp.reshape(chip, (1,)).astype(jnp.int32)
    x0 = a['x'][0]
    s = x0.shape[0]
    cos, sin = rope_tables(s)

    def own_slot(full, mine):
        return lax.dynamic_update_slice_in_dim(full, mine[:, None], chip, axis=1)

    def ffn_shard(l, i, src):
        return [src[n][l, i].astype(BF16).reshape(N_CORES, FF_PART, D_MODEL)
                for n in ('ffn_w_gate', 'ffn_w_up', 'ffn_w_down')]

    def own_slots(fulls, mines):
        return [own_slot(f, m) for f, m in zip(fulls, mines)]
    small_names = list(SMALL_SHARDED)
    small_sh = _pack_rows([a[n] for n in small_names])
    small_sh = small_sh.reshape(N_CORES, small_sh.shape[0] // 2, LANES)
    sh00 = ffn_shard(0, 0, a)
    (first_flight,), started = split_start([sh00 + [small_sh]], "gather", "gather_start_first")
    held = lax.optimization_barrier((started, {n: a[n] for n in BIG}))[1]
    sh01, sh10, sh11 = ffn_shard(0, 1, held), ffn_shard(1, 0, held), ffn_shard(1, 1, held)
    w_in_sh = jnp.pad(held['ssm_w_in'][0], ((0, IN_SHARD_PAD - IN_SHARD), (0, 0))).astype(BF16).reshape(
        N_CORES, IN_SHARD_PAD // 2, D_MODEL)
    w_out_sh = held['ssm_w_out'][0].astype(BF16).reshape(N_CORES, 256, D_MODEL)
    attn_sh = jnp.stack([held['attn_w_q'][0], held['attn_w_o'][0]]).astype(BF16)
    kv_sh = jnp.stack([held['w_k'], held['w_v']]).astype(BF16)
    rest_flights, all_started = split_start([[w_in_sh, kv_sh], [w_out_sh], sh01, sh10, [attn_sh], sh11], "gather",
                                            "gather_start_rest")
    in_flight = [first_flight] + rest_flights

    def arrive(idx, after, tag):
        return forward_cores(split_arrive(in_flight[idx], "gather", after, "gather_arrive_" + tag))

    first = run_exchange(arrive(0, all_started, "first"), "gather_hop_first")
    w00 = own_slots(first[:3], sh00)
    smalls = own_slot(first[3], small_sh)
    p = {}
    per_chip = [_unpack_rows(smalls[:, k], [a[n].shape for n in small_names]) for k in range(N_CHIPS)]
    for idx, n in enumerate(small_names):
        p[n] = jnp.concatenate([per_chip[k][idx] for k in range(N_CHIPS)], axis=SMALL_SHARDED[n])
    nw = p['norm_w']
    conv_w, conv_b, ssm_nw = p['ssm_conv_w'][0], p['ssm_conv_b'][0], p['ssm_norm_w'][0].reshape(1, D_INNER)

    h00 = rmsnorm_fwd(x0, nw[0, 0], "norm_in")
    x1, h01, gu00 = ffn_fwd(h00, x0, *w00, [nw[0, 1]], "ffn_fwd_00")
    w_in_g, kv_g = run_exchange(arrive(1, x1, "in"), "gather_hop_in")
    w_in_t = _chips_first(own_slot(w_in_g, w_in_sh)).reshape(N_CHIPS, IN_SHARD_PAD, D_MODEL)[:, :IN_SHARD].reshape(
        IN_PROJ_DIM, D_MODEL)
    w_dt_t = jnp.pad(w_in_t[D_INNER + CONV_DIM:], ((0, LANES - SSM_HEADS), (0, 0)))
    kv_g = own_slot(kv_g, kv_sh)
    w_k, w_v = kv_g[0].reshape(D_MODEL, KV_DIM), kv_g[1].reshape(D_MODEL, KV_DIM)

    zz = mm_nt(h01, w_in_t, "ssm_in_z", n=D_INNER)
    xbc = mm_nt(h01, w_in_t, "ssm_in_xbc", n=CONV_DIM, row0=D_INNER)
    dtp = mm_nt(h01, w_dt_t, "ssm_in_dt")
    act = conv_fwd(xbc, conv_w, conv_b, "ssm_conv")
    bias_p = _pad_lanes(a['ssm_dt_bias'][0])
    a_p = _pad_lanes(-jnp.exp(a['ssm_a_log'][0]))
    d_p = _pad_lanes(a['ssm_d'][0])
    (yn, y_pre, states), (w_out_g,) = ssd_fwd(act, zz, dtp, bias_p, a_p, d_p, ssm_nw, "ssd_fwd",
                                              ride=arrive(2, act, "out"))
    w_out = _chips_first(own_slot(w_out_g, w_out_sh))
    (x2, h02), w01 = mm_res(yn, w_out, x1, "ssm_out", norm_ws=[nw[0, 2]], ride=arrive(3, yn, "01"))
    w01 = own_slots(w01, sh01)
    x3, hkv, h10, gu01 = ffn_fwd(h02, x2, *w01, [a['kv_norm_w'], nw[1, 0]], "ffn_fwd_01")
    w10 = own_slots(run_exchange(arrive(4, x3, "10"), "gather_hop_10"), sh10)

    k_rot = rope_apply(mm_nn(hkv, w_k, "kv_k", bias=a['b_k']), cos, sin, "rope_k")
    v = mm_nn(hkv, w_v, "kv_v", bias=a['b_v'], out_dtype=BF16)
    kt = _heads_major(k_rot, N_KV_HEADS)
    vt = _heads_major(v, N_KV_HEADS)

    (x4, h11, gu10), (attn_g,) = ffn_fwd(h10, x3, *w10, [nw[1, 1]], "ffn_fwd_10", ride=arrive(5, v, "attn"))
    attn_g = own_slot(attn_g, attn_sh)
    w_q, w_o = attn_g[0].reshape(D_MODEL, D_MODEL), attn_g[1].reshape(D_MODEL, D_MODEL)
    scale = 1.0 / math.sqrt(ATT_HEAD_DIM)
    q_rot = rope_apply(mm_nn(h11, w_q, "attn_q", bias=a['attn_b_q'][0]), cos, sin, "rope_q", scale=scale)
    qt = _heads_major(q_rot, N_Q_HEADS)
    sink_rows = jnp.repeat(a['attn_sinks'][0].reshape(N_KV_HEADS, Q_PER_KV), WINDOW, axis=1).reshape(
        N_KV_HEADS, Q_PER_KV * WINDOW, 1)
    (ot,) = attn_fwd(qt, kt, vt, sink_rows, "attn_fwd")
    o = _tokens_major(ot)
    (x5, h12), w11 = mm_res(o, w_o, x4, "attn_out", bias=a['attn_b_o'][0], norm_ws=[nw[1, 2]],
                            ride=arrive(6, ot, "11"))
    w11 = own_slots(w11, sh11)
    x6, gu11 = ffn_fwd(h12, x5, *w11, [], "ffn_fwd_11")

    loss_v, dx6, d_final = loss_head(x6, a['final_norm_w'], a['loss_target'][0], "loss_head")
    loss = lax.psum(loss_v[0, 0], ("x", "y", "c"))
    g = {'final_norm_w': d_final[0]}

    def same_shape(xs, ys):
        runs = []
        for xv, yv in zip(xs, ys):
            if runs and runs[-1][0][0].shape == xv.shape:
                runs[-1][0].append(xv)
                runs[-1][1].append(yv)
            else:
                runs.append(([xv], [yv]))
        return runs

    def pre_reduce(grads, sib, tag):
        out = []
        for idx, (grp, sbs) in enumerate(same_shape(grads, list(sib))):
            ts = add_pair([gr.reshape(2, -1, gr.shape[-1]) for gr in grp], [_as2d(sb) for sb in sbs], c_idx,
                          "rs_add_%s_%d" % (tag, idx))
            out += [t.reshape(gr.shape[1:]) for t, gr in zip(ts, grp)]
        return out

    def chip_sum(landed, parts, tag):
        out = []
        for idx, (qs, owns) in enumerate(same_shape(list(landed), parts)):
            ts = sum_chips([q.reshape(N_CHIPS, -1, q.shape[-1]) for q in qs],
                           [own.reshape(N_CHIPS, -1, own.shape[-1]) for own in owns], chip_idx,
                           "rs_sum_%s_%d" % (tag, idx))
            out += [t.reshape(q.shape[1:]) for t, q in zip(ts, qs)]
        return out

    dnw = [[None] * 3 for _ in range(2)]
    sums = {}

    def trade(key):
        return swap_cores(sums[key], False)

    dx5, dnw12, *g11 = ffn_bwd(dx6, h12, x5, nw[1, 2], gu11, *w11, "ffn_bwd_11")
    dnw[1][2] = dnw12[0]
    (d_wo, g['attn_b_o']), sib11 = mm_tn(o, dx5, "attn_dwo", col_sum=True, ride=swap_cores(g11, True))
    t11 = pre_reduce(g11, sib11, "11")
    do = mm_nt(dx5, w_o, "attn_do", out_dtype=BF16)
    (dqt, dkt, dvt, dsink), land11 = attn_bwd(qt, kt, vt, sink_rows, _heads_major(do, N_Q_HEADS), "attn_bwd",
                                             ride=scatter_chips(t11))
    sums['11'] = chip_sum(land11, t11, "11")
    g['attn_sinks'] = jnp.sum(dsink[:, :, :Q_PER_KV, 0], axis=1).reshape(N_Q_HEADS)
    dq_pre = rope_apply(_tokens_major(dqt), cos, sin, "rope_dq", inverse=True, scale=scale, out_dtype=F32)
    d_wq, g['attn_b_q'] = mm_tn(h11, dq_pre, "attn_dwq", col_sum=True)
    g_attn = [jnp.stack([d_wq.reshape(N_CHIPS, 256, D_MODEL), d_wo.reshape(N_CHIPS, 256, D_MODEL)])]
    (dx4, dnw[1][1]), sib_attn = mm_rms_bwd([(dq_pre, 0, w_q, 0, D_MODEL, "nt")], dx5, x4, nw[1, 1], "attn_bwd_dh",
                                            ride=swap_cores(g_attn, True))
    t_attn = pre_reduce(g_attn, sib_attn, "attn")
    (dx3, dnw10, *g10), landed = ffn_bwd(dx4, h10, x3, nw[1, 0], gu10, *w10, "ffn_bwd_10",
                                         ride=join(scatter_chips(t_attn), trade('11')))
    dnw[1][0] = dnw10[0]
    sums['attn'] = chip_sum(landed[:1], t_attn, "attn")
    theirs = {'11': landed[1:]}
    dk_pre = rope_apply(_tokens_major(dkt), cos, sin, "rope_dk", inverse=True, out_dtype=F32)
    dv = _tokens_major(dvt)
    (d_wk, g['b_k']), sib10 = mm_tn(hkv, dk_pre, "kv_dwk", col_sum=True, ride=swap_cores(g10, True))
    t10 = pre_reduce(g10, sib10, "10")
    d_wv, g['b_v'] = mm_tn(hkv, dv, "kv_dwv", col_sum=True)
    g_kv = [jnp.stack([d_wk.reshape(N_CHIPS, 256, KV_DIM), d_wv.reshape(N_CHIPS, 256, KV_DIM)])]
    (dx3, g['kv_norm_w']), sib_kv = mm_rms_bwd(
        [(dk_pre, 0, w_k, 0, KV_DIM, "nt"), (dv, 0, w_v, 0, KV_DIM, "nt")], dx3, x3, a['kv_norm_w'], "kv_bwd_dh",
        ride=swap_cores(g_kv, True))
    t_kv = pre_reduce(g_kv, sib_kv, "kv")
    (dx2, dnw02, *g01), landed = ffn_bwd(dx3, h02, x2, nw[0, 2], gu01, *w01, "ffn_bwd_01",
                                         ride=join(scatter_chips(t10 + t_kv), trade('attn')))
    dnw[0][2] = dnw02[0]
    sums['10'] = chip_sum(landed[:3], t10, "10")
    sums['kv'] = chip_sum(landed[3:4], t_kv, "kv")
    theirs['attn'] = landed[4:]
    d_wout, sib01 = mm_tn(yn, dx2, "ssm_dwout", ride=swap_cores(g01, True))
    t01 = pre_reduce(g01, sib01, "01")
    dyn = mm_nt(dx2, w_out, "ssm_dyn")
    (dxs, db_, dc_, dz, ddt, d_ssm_nw, d_bias, d_a, d_d), landed = ssd_bwd(
        dyn, act, zz, y_pre, states, dtp, bias_p, a_p, d_p, ssm_nw, "ssd_bwd",
        ride=join(scatter_chips(t01), trade('10'), trade('kv')))
    sums['01'] = chip_sum(landed[:3], t01, "01")
    theirs['10'], theirs['kv'] = landed[3:6], landed[6:]
    g['ssm_norm_w'] = d_ssm_nw[:SSM_GROUPS].reshape(D_INNER)
    g['ssm_dt_bias'] = d_bias[0, :SSM_HEADS]
    g['ssm_a_log'] = d_a[0, :SSM_HEADS] * a_p[0, :SSM_HEADS]
    g['ssm_d'] = d_d[0, :SSM_HEADS]
    dxbc, g['ssm_conv_w'], g['ssm_conv_b'] = conv_bwd(dxs, db_, dc_, xbc, conv_w, conv_b, "ssm_conv_bwd")
    d_win = mm_tn(dz, h01, "ssm_dwz", rows=IN_PROJ_DIM)
    d_win = mm_tn(dxbc, h01, "ssm_dwxbc", into=d_win, rows=IN_PROJ_DIM, row0=D_INNER)
    d_win = mm_tn(ddt, h01, "ssm_dwdt", into=d_win, rows=IN_PROJ_DIM, row0=D_INNER + CONV_DIM, m_valid=SSM_HEADS)
    d_win = jnp.pad(d_win.reshape(N_CHIPS, IN_SHARD, D_MODEL), ((0, 0), (0, IN_SHARD_PAD - IN_SHARD), (0, 0)))
    g_ssm = [_parts_first(d_win.reshape(-1, D_MODEL), IN_SHARD_PAD // 2), _parts_first(d_wout, 256)]
    kb = 1024
    terms = ([(dz, j, w_in_t, j, kb, "nn") for j in range(D_INNER // kb)]
             + [(dxbc, j, w_in_t, D_INNER // kb + j, kb, "nn") for j in range(CONV_DIM // kb)]
             + [(ddt, 0, w_dt_t, 0, LANES, "nn")])
    (dx1, dnw[0][1]), sib_ssm = mm_rms_bwd(terms, dx2, x1, nw[0, 1], "ssm_bwd_dh", ride=swap_cores(g_ssm, True))
    t_ssm = pre_reduce(g_ssm, sib_ssm, "ssm")
    (grad_x, dnw00, *g00), landed = ffn_bwd(dx1, h00, x0, nw[0, 0], gu00, *w00, "ffn_bwd_00",
                                            ride=join(scatter_chips(t_ssm), trade('01')))
    dnw[0][0] = dnw00[0]
    sums['ssm'] = chip_sum(landed[:2], t_ssm, "ssm")
    theirs['01'] = landed[2:]
    landed = run_exchange(join(swap_cores(g00, True), trade('ssm')), "rs_swap_00")
    t00 = pre_reduce(g00, landed[:3], "00")
    theirs['ssm'] = landed[3:]

    def both(key):
        return [(jnp.where(south, m_, t_), jnp.where(south, t_, m_)) for m_, t_ in zip(sums[key], theirs[key])]

    g['norm_w'] = jnp.stack([jnp.stack(r) for r in dnw])
    red = all_reduce_small(_pack_rows([g[n] for n in SMALL]), "reduce_vectors")

    t00 = lax.optimization_barrier((red, t00))[1]
    (flight00,), flying = split_start([t00], "scatter", "rs_scatter_00_start")

    def held(val):
        return lax.optimization_barrier((flying, val))[1]

    delta, new_m, new_v, gw = {}, {}, {}, {}
    ffn_names = ('ffn_w_gate', 'ffn_w_up', 'ffn_w_down')
    full = {key: both(key) for key in ('attn', 'kv', 'ssm')}
    lo, hi = full['attn'][0]
    gw['attn_w_q'], gw['attn_w_o'] = lo[None], hi[None]
    lo, hi = full['kv'][0]
    gw['w_k'], gw['w_v'] = lo, hi
    lo, hi = full['ssm'][0]
    gw['ssm_w_in'] = jnp.concatenate([lo, hi], axis=0)[:IN_SHARD][None]
    lo, hi = full['ssm'][1]
    gw['ssm_w_out'] = jnp.concatenate([lo, hi], axis=0)[None]

    for n, t in zip(SMALL, _unpack_rows(red, [g[n].shape for n in SMALL])):
        if n in SMALL_SHARDED:
            ax = SMALL_SHARDED[n] - (a[n].ndim - t.ndim)
            width = a[n].shape[SMALL_SHARDED[n]]
            t = lax.dynamic_slice_in_dim(t, chip * width, width, axis=ax)
        gw[n] = t.reshape(a[n].shape)

    def update(n):
        d, mo, vo = adamw(_as2d(a[n]), held(_as2d(gw[n])), _as2d(a['m_' + n]), _as2d(a['v_' + n]), "adamw_" + n)
        delta[n], new_m[n], new_v[n] = d.reshape(a[n].shape), mo.reshape(a[n].shape), vo.reshape(a[n].shape)

    for n in BIG:
        if n not in ffn_names:
            update(n)
    shapes = [a[n].shape for n in SMALL]
    packed = [_pack_rows([src[n] for n in SMALL]) for src in
              (a, gw, {n: a['m_' + n] for n in SMALL}, {n: a['v_' + n] for n in SMALL})]
    outs = adamw(*packed, "adamw_vectors")
    for dst, buf in zip((delta, new_m, new_v), outs):
        for n, t in zip(SMALL, _unpack_rows(buf, shapes)):
            dst[n] = t
    for key in ('01', '10', '11'):
        sums[key] = held(list(sums[key]))
    rest = [both(key) for key in ('01', '10', '11')]
    done = lax.optimization_barrier((outs[0], [delta[n] for n in BIG if n not in ffn_names], rest))[0]
    land00 = split_arrive(flight00, "scatter", done, "rs_scatter_00_arrive")
    sums['00'] = chip_sum(land00, t00, "00")
    theirs['00'] = run_exchange(trade('00'), "rs_trade_00")
    blocks = [both('00')] + rest
    for t, n in enumerate(ffn_names):
        gw[n] = jnp.concatenate([piece for blk in blocks for piece in blk[t]], axis=0).reshape(a[n].shape)
        update(n)
    for n in TRANSPOSED:
        for dst in (gw, delta, new_m, new_v):
            dst[n] = dst[n].swapaxes(-1, -2)

    return (loss, grad_x[None], *[gw[n] for n in WEIGHTS], *[delta[n] for n in WEIGHTS],
            *[new_m[n] for n in WEIGHTS], *[new_v[n] for n in WEIGHTS])
```

```python
import math

import jax
import jax.numpy as jnp
from jax import lax
from jax.experimental import pallas as pl
from jax.experimental.pallas import tpu as pltpu

F32 = jnp.float32
BF16 = jnp.bfloat16

D_MODEL = 1024
D_INNER = 2048
SSM_HEADS = 32
SSM_GROUPS = 4
HEADS_PER_GROUP = SSM_HEADS // SSM_GROUPS
SSM_HEAD_DIM = 64
SSM_STATE = 128
GROUP_DIM = D_INNER // SSM_GROUPS
CONV_DIM = D_INNER + 2 * SSM_GROUPS * SSM_STATE
CONV_WIDTH = 4
CHUNK = 128
ATT_HEAD_DIM = 64
N_Q_HEADS = 16
N_KV_HEADS = 4
Q_PER_KV = N_Q_HEADS // N_KV_HEADS
KV_DIM = N_KV_HEADS * ATT_HEAD_DIM
WINDOW = 128
ROPE_THETA = 10000.0
D_FF = 2816
N_CHIPS = 4
N_CORES = 2
FF_SHARD = D_FF // N_CHIPS
FF_PART = FF_SHARD // N_CORES
IN_PROJ_DIM = D_INNER + CONV_DIM + SSM_HEADS
IN_SHARD = IN_PROJ_DIM // N_CHIPS
IN_SHARD_PAD = 1312
EPS = 1e-5
NEG = -1e30
LANES = 128
VMEM_LIMIT = 56 * 1024 * 1024

ADAM_LR = 0.001
ADAM_B1 = 0.9
ADAM_B2 = 0.999
ADAM_EPS = 1e-08
ADAM_WD = 0.01
ADAM_STEP = 10

NN = ((1,), (0,))
NT = ((1,), (1,))
TN = ((0,), (0,))
MESH = pl.DeviceIdType.MESH
ANY = pl.BlockSpec(memory_space=pl.ANY)


def _dot(a, b, dims=NN, precision=None):
    return lax.dot_general(a, b, (dims, ((), ())), preferred_element_type=F32, precision=precision)


def _cp(n_grid):
    return pltpu.CompilerParams(dimension_semantics=("arbitrary",) * n_grid, vmem_limit_bytes=VMEM_LIMIT)


def _sigmoid(x):
    return 1.0 / (1.0 + jnp.exp(-x))


def _rms_fwd(xf, w):
    r = lax.rsqrt(jnp.mean(xf * xf, axis=-1, keepdims=True) + EPS)
    return xf * r * w


def _rms_bwd(dh, xf, w):
    r = lax.rsqrt(jnp.mean(xf * xf, axis=-1, keepdims=True) + EPS)
    xhat = xf * r
    dxhat = dh * w
    dx = r * (dxhat - xhat * jnp.mean(dxhat * xhat, axis=-1, keepdims=True))
    return dx, dh * xhat


def _row_tile(s, pref):
    return pref if s % pref == 0 else s


def _col_tile(n):
    for t in (1024, 768, 512, 256, 128):
        if n % t == 0:
            return t
    return n


def _sds(shape, dtype):
    return jax.ShapeDtypeStruct(tuple(shape), dtype)


class Exchange:
    def __init__(self, ins, out_shapes, sems, start, finish, inplace=False):
        self.ins, self.out_shapes, self.sems, self.start, self.finish = ins, out_shapes, sems, start, finish
        self.inplace = inplace


def _place():
    x, y, c = lax.axis_index("x"), lax.axis_index("y"), lax.axis_index("c")
    others = [(1 - x, y), (x, 1 - y), (1 - x, 1 - y)]
    return x, y, c, 2 * x + y, others


def _rc(src, dst, send_sem, recv_sem, dev):
    return pltpu.make_async_remote_copy(src_ref=src, dst_ref=dst, send_sem=send_sem, recv_sem=recv_sem,
                                        device_id=dev, device_id_type=MESH)


def gather_chips(arrs):
    n = len(arrs)

    def copies(ins, outs, sems):
        send, recv = sems
        x, y, c, k, others = _place()
        ici, land, fwd, fland = [], [], [], []
        for a in range(n):
            for j, (px, py) in enumerate(others):
                ici.append(_rc(ins[a].at[c], outs[a].at[c, k], send.at[a, j], recv.at[a, j], (px, py, c)))
                blk = outs[a].at[c, 2 * px + py]
                land.append(_rc(blk, blk, send.at[a, j], recv.at[a, j], (px, py, c)))
                fwd.append(_rc(blk, blk, send.at[a, 3 + j], recv.at[a, 3 + j], (x, y, 1 - c)))
                blk2 = outs[a].at[1 - c, 2 * px + py]
                fland.append(_rc(blk2, blk2, send.at[a, 3 + j], recv.at[a, 3 + j], (x, y, 1 - c)))
        return ici, land, fwd, fland

    def start(ins, outs, sems):
        for cp in copies(ins, outs, sems)[0]:
            cp.start()

    def finish(ins, outs, sems):
        ici, land, fwd, fland = copies(ins, outs, sems)
        for arrived, onward in zip(land, fwd):
            arrived.wait_recv()
            onward.start()
        for arrived in fland:
            arrived.wait_recv()
        for cp in ici + fwd:
            cp.wait_send()

    return Exchange(list(arrs), [_sds((2, N_CHIPS) + a.shape[1:], a.dtype) for a in arrs],
                    [pltpu.SemaphoreType.DMA((n, 6)), pltpu.SemaphoreType.DMA((n, 6))], start, finish)


def scatter_chips(arrs):
    n = len(arrs)

    def copies(ins, outs, sems):
        send, recv = sems
        x, y, c, k, others = _place()
        out, land = [], []
        for a in range(n):
            for j, (px, py) in enumerate(others):
                out.append(_rc(ins[a].at[2 * px + py], outs[a].at[k], send.at[a, j], recv.at[a, j], (px, py, c)))
                blk = outs[a].at[2 * px + py]
                land.append(_rc(blk, blk, send.at[a, j], recv.at[a, j], (px, py, c)))
        return out, land

    def start(ins, outs, sems):
        for cp in copies(ins, outs, sems)[0]:
            cp.start()

    def finish(ins, outs, sems):
        out, land = copies(ins, outs, sems)
        for arrived in land:
            arrived.wait_recv()
        for cp in out:
            cp.wait_send()

    return Exchange(list(arrs), [_sds(a.shape, a.dtype) for a in arrs],
                    [pltpu.SemaphoreType.DMA((n, 3)), pltpu.SemaphoreType.DMA((n, 3))], start, finish)


def swap_cores(arrs, pick_other):
    n = len(arrs)

    def copies(ins, outs, sems):
        send, recv = sems
        x, y, c, _, _ = _place()
        return [_rc(ins[a].at[1 - c] if pick_other else ins[a], outs[a], send.at[a], recv.at[a], (x, y, 1 - c))
                for a in range(n)]

    def start(ins, outs, sems):
        for cp in copies(ins, outs, sems):
            cp.start()

    def finish(ins, outs, sems):
        for cp in copies(ins, outs, sems):
            cp.wait()

    shapes = [_sds(a.shape[1:] if pick_other else a.shape, a.dtype) for a in arrs]
    return Exchange(list(arrs), shapes, [pltpu.SemaphoreType.DMA((n,)), pltpu.SemaphoreType.DMA((n,))],
                    start, finish)


def join(*parts):
    parts = [p for p in parts if p is not None]
    if not parts:
        return None

    def split(refs, counts):
        out, pos = [], 0
        for cnt in counts:
            out.append(refs[pos:pos + cnt])
            pos += cnt
        return out

    n_in = [len(p.ins) for p in parts]
    n_out = [len(p.out_shapes) for p in parts]
    n_sem = [len(p.sems) for p in parts]

    def run(which):
        def go(ins, outs, sems):
            for p, i, o, s in zip(parts, split(ins, n_in), split(outs, n_out), split(sems, n_sem)):
                getattr(p, which)(i, o, s)
        return go

    return Exchange([a for p in parts for a in p.ins], [s for p in parts for s in p.out_shapes],
                    [s for p in parts for s in p.sems], run("start"), run("finish"))


def _pcall(body, *, out_shape, grid, in_specs, out_specs, args, name, scratch_shapes=(), ride=None, aliases=None):
    out_shape, out_specs, in_specs = tuple(out_shape), tuple(out_specs), list(in_specs)
    aliases = aliases or {}
    if ride is None:
        return pl.pallas_call(body, out_shape=out_shape, grid=grid, in_specs=in_specs, out_specs=out_specs,
                              scratch_shapes=list(scratch_shapes), input_output_aliases=aliases, name=name,
                              compiler_params=_cp(len(grid)))(*args)
    n_in, n_out, n_sc = len(args), len(out_shape), len(scratch_shapes)
    n_xi, n_xo = len(ride.ins), len(ride.out_shapes)

    def wrapped(*refs):
        pos = [0]

        def take(cnt):
            got = refs[pos[0]:pos[0] + cnt]
            pos[0] += cnt
            return got

        c_in, x_in, c_out, x_out, c_sc = take(n_in), take(n_xi), take(n_out), take(n_xo), take(n_sc)
        sems = refs[pos[0]:]
        first, last = True, True
        for d, size in enumerate(grid):
            first = jnp.logical_and(first, pl.program_id(d) == 0)
            last = jnp.logical_and(last, pl.program_id(d) == size - 1)

        @pl.when(first)
        def _():
            ride.start(x_in, x_out, sems)

        body(*c_in, *c_out, *c_sc)

        @pl.when(last)
        def _():
            ride.finish(x_in, x_out, sems)

    if ride.inplace:
        aliases = {**aliases, **{n_in + t: n_out + t for t in range(n_xi)}}
    res = pl.pallas_call(
        wrapped, out_shape=out_shape + tuple(ride.out_shapes), grid=grid,
        in_specs=in_specs + [ANY] * n_xi, out_specs=out_specs + (ANY,) * n_xo,
        scratch_shapes=list(scratch_shapes) + list(ride.sems), input_output_aliases=aliases, name=name,
        compiler_params=_cp(len(grid)))(*args, *ride.ins)
    return res[:n_out], res[n_out:]


def run_exchange(ex, name):
    n_xi, n_xo = len(ex.ins), len(ex.out_shapes)

    def body(*refs):
        ins, outs, sems = refs[:n_xi], refs[n_xi:n_xi + n_xo], refs[n_xi + n_xo:]
        ex.start(ins, outs, sems)
        ex.finish(ins, outs, sems)

    aliases = {t: t for t in range(n_xi)} if ex.inplace else {}
    return pl.pallas_call(body, out_shape=tuple(ex.out_shapes), in_specs=[ANY] * n_xi, out_specs=(ANY,) * n_xo,
                          scratch_shapes=list(ex.sems), input_output_aliases=aliases, name=name)(*ex.ins)


HBM_SPEC = pl.BlockSpec(memory_space=pltpu.HBM)
SEM_SPEC = pl.BlockSpec(memory_space=pltpu.SEMAPHORE)
EFFECT = pltpu.SideEffectType.DATAFLOW_SIDE_EFFECTING


def _route(kind, src, dst, c, k, peer):
    if kind == "gather":
        return src.at[c], dst.at[c, k], dst.at[c, peer]
    return src.at[peer], dst.at[k], dst.at[peer]


def split_start(batches, kind, name):
    flat = [a for batch in batches for a in batch]
    n, nb = len(flat), len(batches)
    lands = [lax.empty((2, N_CHIPS) + a.shape[1:] if kind == "gather" else a.shape, a.dtype) for a in flat]

    def body(*refs):
        srcs, dsts, sems, token = refs[:n], refs[n:2 * n], refs[2 * n:2 * n + 2 * nb], refs[-1]
        x, y, c, k, others = _place()
        pos = 0
        for b, batch in enumerate(batches):
            for a in range(len(batch)):
                for j, (px, py) in enumerate(others):
                    src, dst, _ = _route(kind, srcs[pos], dsts[pos], c, k, 2 * px + py)
                    _rc(src, dst, sems[2 * b].at[3 * a + j], sems[2 * b + 1].at[3 * a + j], (px, py, c)).start()
                pos += 1
        token[...] = jnp.zeros(token.shape, token.dtype)

    sem_shapes = [pltpu.SemaphoreType.DMA((3 * len(batch),)) for batch in batches for _ in range(2)]
    thru = [pltpu.HBM(a.shape, a.dtype) for a in flat] + [pltpu.HBM(l.shape, l.dtype) for l in lands]
    res = pl.pallas_call(
        body, name=name, out_shape=tuple(sem_shapes + thru + [_sds((8, LANES), F32)]),
        in_specs=[HBM_SPEC] * (2 * n),
        out_specs=tuple([SEM_SPEC] * (2 * nb) + [HBM_SPEC] * (2 * n) + [pl.BlockSpec(memory_space=pltpu.VMEM)]),
        input_output_aliases={t: 2 * nb + t for t in range(2 * n)},
        compiler_params=pltpu.CompilerParams(has_side_effects=EFFECT),
    )(*[pltpu.with_memory_space_constraint(t, pltpu.HBM) for t in flat + lands])
    sems, srcs, dsts = res[:2 * nb], res[2 * nb:2 * nb + n], res[2 * nb + n:2 * nb + 2 * n]
    out, pos = [], 0
    for b, batch in enumerate(batches):
        out.append((sems[2 * b], sems[2 * b + 1], list(srcs[pos:pos + len(batch)]), list(dsts[pos:pos + len(batch)])))
        pos += len(batch)
    return out, res[-1]


def split_arrive(handle, kind, after, name):
    send, recv, srcs, dsts = handle
    n = len(srcs)

    def body(*refs):
        s_refs, d_refs, send_ref, recv_ref = refs[:n], refs[n:2 * n], refs[2 * n], refs[2 * n + 1]
        x, y, c, k, others = _place()
        for a in range(n):
            for j, (px, py) in enumerate(others):
                src, _, landed = _route(kind, s_refs[a], d_refs[a], c, k, 2 * px + py)
                cp = _rc(src, landed, send_ref.at[3 * a + j], recv_ref.at[3 * a + j], (px, py, c))
                cp.wait_send()
                cp.wait_recv()

    res = pl.pallas_call(
        body, name=name, out_shape=tuple([pltpu.HBM(t.shape, t.dtype) for t in srcs + dsts]),
        in_specs=[HBM_SPEC] * (2 * n) + [SEM_SPEC, SEM_SPEC, ANY], out_specs=tuple([HBM_SPEC] * (2 * n)),
        input_output_aliases={t: t for t in range(2 * n)},
        compiler_params=pltpu.CompilerParams(has_side_effects=EFFECT),
    )(*srcs, *dsts, send, recv, after)
    return list(res[n:])


def forward_cores(bufs):
    n = len(bufs)

    def copies(outs, sems):
        send, recv = sems
        x, y, c, k, others = _place()
        onward, land = [], []
        for a in range(n):
            for j, (px, py) in enumerate(others):
                blk = outs[a].at[c, 2 * px + py]
                onward.append(_rc(blk, blk, send.at[a, j], recv.at[a, j], (x, y, 1 - c)))
                blk2 = outs[a].at[1 - c, 2 * px + py]
                land.append(_rc(blk2, blk2, send.at[a, j], recv.at[a, j], (x, y, 1 - c)))
        return onward, land

    def start(ins, outs, sems):
        for cp in copies(outs, sems)[0]:
            cp.start()

    def finish(ins, outs, sems):
        onward, land = copies(outs, sems)
        for arrived in land:
            arrived.wait_recv()
        for cp in onward:
            cp.wait_send()

    return Exchange(list(bufs), [_sds(b.shape, b.dtype) for b in bufs],
                    [pltpu.SemaphoreType.DMA((n, 3)), pltpu.SemaphoreType.DMA((n, 3))], start, finish, inplace=True)


def all_reduce_small(buf, name):
    r = buf.shape[0]
    n_dev = 8

    def body(in_ref, o_ref, land, send_sems, recv_sems):
        x, y, c, _, _ = _place()
        me = 4 * x + 2 * y + c
        land[me] = in_ref[...]
        sends = []
        for d in range(1, n_dev):
            peer = (x ^ (d >> 2), y ^ ((d >> 1) & 1), c ^ (d & 1))
            cp = _rc(in_ref, land.at[me], send_sems.at[d], recv_sems.at[d], peer)
            cp.start()
            sends.append(cp)
        for d in range(1, n_dev):
            blk = land.at[me ^ d]
            _rc(blk, blk, send_sems.at[d], recv_sems.at[d], (x, y, c)).wait_recv()
        for cp in sends:
            cp.wait_send()
        tot = land[0]
        for d in range(1, n_dev):
            tot = tot + land[d]
        o_ref[...] = tot

    vm = pl.BlockSpec(memory_space=pltpu.VMEM)
    return pl.pallas_call(
        body, out_shape=_sds(buf.shape, F32), in_specs=[vm], out_specs=vm,
        scratch_shapes=[pltpu.VMEM((n_dev, r, LANES), F32), pltpu.SemaphoreType.DMA((n_dev,)),
                        pltpu.SemaphoreType.DMA((n_dev,))],
        name=name)(buf)


def rmsnorm_fwd(x, w, name):
    s, d = x.shape
    tm = _row_tile(s, 512)

    def body(x_ref, w_ref, o_ref):
        o_ref[...] = _rms_fwd(x_ref[...], w_ref[...]).astype(BF16)

    return _pcall(body, out_shape=[_sds((s, d), BF16)], grid=(s // tm,),
                  in_specs=[pl.BlockSpec((tm, d), lambda i: (i, 0)), pl.BlockSpec((1, d), lambda i: (0, 0))],
                  out_specs=[pl.BlockSpec((tm, d), lambda i: (i, 0))], args=[x, w.reshape(1, d)], name=name)[0]


def _ffn_w_spec(chip_of, single=False):
    mode = dict(pipeline_mode=pl.Buffered(1)) if single else {}
    return pl.BlockSpec((N_CORES, 1, FF_PART, D_MODEL), lambda *ids: (0, chip_of(*ids), 0, 0), **mode)


def ffn_fwd(h, x, wg, wu, wd, norm_ws, name, ride=None):
    s, d = h.shape
    n_norm = len(norm_ws)
    tm = _row_tile(s, 1024)

    def body(*refs):
        h_ref, x_ref, wg_ref, wu_ref, wd_ref = refs[:5]
        nw_refs = refs[5:5 + n_norm]
        o_ref = refs[5 + n_norm]
        h_refs = refs[6 + n_norm:6 + 2 * n_norm]
        gu_ref, acc = refs[6 + 2 * n_norm], refs[7 + 2 * n_norm]
        k = pl.program_id(1)

        @pl.when(k == 0)
        def _():
            acc[...] = jnp.zeros(acc.shape, F32)

        hm = tm // 2
        for part in range(2):
            sub = pl.ds(part * hm, hm)
            hb = h_ref[sub, :]
            g = _dot(hb, wg_ref[...].reshape(FF_SHARD, d), NT)
            u = _dot(hb, wu_ref[...].reshape(FF_SHARD, d), NT)
            gu_ref[0, 0, sub, :] = g.astype(BF16)
            gu_ref[0, 1, sub, :] = u.astype(BF16)
            acc[sub, :] += _dot((g * _sigmoid(g) * u).astype(BF16), wd_ref[...].reshape(FF_SHARD, d))

        @pl.when(k == N_CHIPS - 1)
        def _():
            xn = x_ref[...] + 0.5 * acc[...]
            o_ref[...] = xn
            for nw_ref, hn_ref in zip(nw_refs, h_refs):
                hn_ref[...] = _rms_fwd(xn, nw_ref[...]).astype(BF16)

    row = pl.BlockSpec((tm, d), lambda i, k: (i, 0))
    vec = pl.BlockSpec((1, d), lambda i, k: (0, 0))
    wsp = _ffn_w_spec(lambda i, k: k)
    return _pcall(
        body, out_shape=[_sds((s, d), F32)] + [_sds((s, d), BF16)] * n_norm + [_sds((N_CHIPS, 2, s, FF_SHARD), BF16)],
        grid=(s // tm, N_CHIPS),
        in_specs=[row, row, wsp, wsp, wsp] + [vec] * n_norm,
        out_specs=[row] * (1 + n_norm) + [pl.BlockSpec((1, 2, tm, FF_SHARD), lambda i, k: (k, 0, i, 0))],
        scratch_shapes=[pltpu.VMEM((tm, d), F32)],
        args=[h, x, wg, wu, wd] + [nw.reshape(1, d) for nw in norm_ws], name=name, ride=ride)


def ffn_bwd(dxn, h, x_in, nw, gu, wg, wu, wd, name, ride=None):
    s, d = h.shape
    tm = _row_tile(s, 512)
    ni = s // tm
    last_e = N_CHIPS - 1

    def body(dxn_ref, h_ref, x_ref, nw_ref, gu_ref, wg_ref, wu_ref, wd_ref,
             dx_ref, dnw_ref, dwg_ref, dwu_ref, dwd_ref, dh, wacc):
        e = pl.program_id(0)
        i = pl.program_id(1)
        rows = pl.ds(pl.multiple_of(i * tm, tm), tm)

        @pl.when(i == 0)
        def _():
            wacc[...] = jnp.zeros(wacc.shape, F32)

        @pl.when(e == 0)
        def _():
            dh[rows, :] = jnp.zeros((tm, d), F32)

        hm = tm // 2
        for part in range(2):
            sub = pl.ds(part * hm, hm)
            dxb = dxn_ref[sub, :].astype(BF16)
            hb = h_ref[sub, :]
            g = gu_ref[0, 0, sub, :].astype(F32)
            u = gu_ref[0, 1, sub, :].astype(F32)
            drows = pl.ds(pl.multiple_of(i * tm + part * hm, hm), hm)
            sg = _sigmoid(g)
            silu = g * sg
            wacc[2] += _dot((0.5 * silu * u).astype(BF16), dxb, TN)
            da = 0.5 * _dot(dxb, wd_ref[...].reshape(FF_SHARD, d), NT)
            dg = (da * u * (sg * (1.0 + g * (1.0 - sg)))).astype(BF16)
            wacc[0] += _dot(dg, hb, TN)
            du = (da * silu).astype(BF16)
            dh[drows, :] += _dot(dg, wg_ref[...].reshape(FF_SHARD, d))
            wacc[1] += _dot(du, hb, TN)
            dh[drows, :] += _dot(du, wu_ref[...].reshape(FF_SHARD, d))

        @pl.when(i == ni - 1)
        def _():
            for t, dw_ref in enumerate((dwg_ref, dwu_ref, dwd_ref)):
                dw_ref[...] = wacc[t].astype(BF16).reshape(N_CORES, 1, FF_PART, d)

        @pl.when(e == last_e)
        def _():
            dx, dnw = _rms_bwd(dh[rows, :], x_ref[...], nw_ref[...])
            dx_ref[...] = dxn_ref[...] + dx
            col = jnp.sum(dnw, axis=0, keepdims=True)

            @pl.when(i == 0)
            def _():
                dnw_ref[...] = col

            @pl.when(i > 0)
            def _():
                dnw_ref[...] += col

    row = pl.BlockSpec((tm, d), lambda e, i: (i, 0))
    late = pl.BlockSpec((tm, d), lambda e, i: (jnp.where(e == last_e, i, 0), 0))
    vec = pl.BlockSpec((1, d), lambda e, i: (0, 0))
    wsp = _ffn_w_spec(lambda e, i: e, single=True)
    dwsp = _ffn_w_spec(lambda e, i: e, single=True)
    dw = _sds((N_CORES, N_CHIPS, FF_PART, d), BF16)
    return _pcall(
        body, out_shape=[_sds((s, d), F32), _sds((1, d), F32), dw, dw, dw],
        grid=(N_CHIPS, ni),
        in_specs=[row, row, late, vec, pl.BlockSpec((1, 2, tm, FF_SHARD), lambda e, i: (e, 0, i, 0)), wsp, wsp, wsp],
        out_specs=[late, vec, dwsp, dwsp, dwsp],
        scratch_shapes=[pltpu.VMEM((s, d), F32), pltpu.VMEM((3, FF_SHARD, d), F32)],
        args=[dxn, h, x_in, nw.reshape(1, d), gu, wg, wu, wd], name=name, ride=ride)


def mm_res(a, w, x, name, bias=None, norm_ws=(), ride=None):
    s, k = a.shape
    n = w.shape[1]
    tm = _row_tile(s, 256)
    has_bias = bias is not None
    n_norm = len(norm_ws)

    def body(*refs):
        a_ref, w_ref, x_ref = refs[:3]
        pos = 3
        t = _dot(a_ref[...], w_ref[...])
        if has_bias:
            t = t + refs[pos][...]
            pos += 1
        nw_refs = refs[pos:pos + n_norm]
        o_ref = refs[pos + n_norm]
        h_refs = refs[pos + n_norm + 1:]
        xn = x_ref[...] + t
        o_ref[...] = xn
        for nw_ref, h_ref in zip(nw_refs, h_refs):
            h_ref[...] = _rms_fwd(xn, nw_ref[...]).astype(BF16)

    row = pl.BlockSpec((tm, n), lambda i: (i, 0))
    vec = pl.BlockSpec((1, n), lambda i: (0, 0))
    in_specs = [pl.BlockSpec((tm, k), lambda i: (i, 0)), pl.BlockSpec((k, n), lambda i: (0, 0)), row]
    args = [a, w, x]
    if has_bias:
        in_specs.append(vec)
        args.append(bias.reshape(1, n))
    for nw in norm_ws:
        in_specs.append(vec)
        args.append(nw.reshape(1, n))
    return _pcall(body, out_shape=[_sds((s, n), F32)] + [_sds((s, n), BF16)] * n_norm, grid=(s // tm,),
                  in_specs=in_specs, out_specs=[row] * (1 + n_norm), args=args, name=name, ride=ride)


def mm_nn(a, w, name, bias=None, out_dtype=F32):
    s, k = a.shape
    n = w.shape[1]
    tm = _row_tile(s, 512)
    tn = _col_tile(n)
    has_bias = bias is not None

    def body(*refs):
        a_ref, w_ref = refs[:2]
        o_ref = refs[-1]
        t = _dot(a_ref[...], w_ref[...])
        if has_bias:
            t = t + refs[2][...]
        o_ref[...] = t.astype(out_dtype)

    in_specs = [pl.BlockSpec((tm, k), lambda j, i: (i, 0)), pl.BlockSpec((k, tn), lambda j, i: (0, j))]
    args = [a, w]
    if has_bias:
        in_specs.append(pl.BlockSpec((1, tn), lambda j, i: (0, j)))
        args.append(bias.reshape(1, n))
    return _pcall(body, out_shape=[_sds((s, n), out_dtype)], grid=(n // tn, s // tm), in_specs=in_specs,
                  out_specs=[pl.BlockSpec((tm, tn), lambda j, i: (i, j))], args=args, name=name)[0]


def mm_nt(a, w, name, n=None, row0=0, out_dtype=F32, ride=None):
    s, k = a.shape
    n = w.shape[0] if n is None else n
    tm = _row_tile(s, 512)
    tn = _col_tile(n)
    base = row0 // tn
    assert row0 % tn == 0

    def body(a_ref, w_ref, o_ref):
        o_ref[...] = _dot(a_ref[...].astype(BF16), w_ref[...], NT).astype(out_dtype)

    res = _pcall(body, out_shape=[_sds((s, n), out_dtype)], grid=(n // tn, s // tm),
                 in_specs=[pl.BlockSpec((tm, k), lambda j, i: (i, 0)), pl.BlockSpec((tn, k), lambda j, i: (base + j, 0))],
                 out_specs=[pl.BlockSpec((tm, tn), lambda j, i: (i, j))], args=[a, w], name=name, ride=ride)
    return res[0] if ride is None else (res[0][0], res[1])


def mm_tn(a, b, name, into=None, rows=None, row0=0, m_valid=None, col_sum=False, ride=None):
    s, m = a.shape
    n = b.shape[1]
    mv = m if m_valid is None else m_valid
    tm = _col_tile(m) if m_valid is None else mv
    tn = n if n <= 1024 else _col_tile(n)
    rows = mv if rows is None else rows
    assert row0 % tm == 0 and (m_valid is None or m == LANES)
    assert not col_sum or mv == tm
    base = row0 // tm
    ta = m if m_valid is not None else tm

    def body(*refs):
        a_ref, b_ref = refs[0], refs[1]
        o_ref = refs[-2] if col_sum else refs[-1]
        bf = b_ref[...]
        t = _dot(a_ref[...].astype(BF16), bf.astype(BF16), TN)
        o_ref[...] = t[:tm].astype(BF16)
        if col_sum:
            refs[-1][...] = jnp.sum(bf.astype(F32), axis=0, keepdims=True)

    in_specs = [pl.BlockSpec((s, ta), lambda i, j: (0, i)), pl.BlockSpec((s, tn), lambda i, j: (0, j))]
    args = [a, b]
    aliases = None
    if into is not None:
        in_specs.append(ANY)
        args.append(into)
        aliases = {2: 0}
    out_shape = [_sds((rows, n), BF16)]
    out_specs = [pl.BlockSpec((tm, tn), lambda i, j: (base + i, j))]
    if col_sum:
        out_shape.append(_sds((1, n), F32))
        out_specs.append(pl.BlockSpec((1, tn), lambda i, j: (0, j)))
    res = _pcall(body, out_shape=out_shape, grid=(mv // tm, n // tn), in_specs=in_specs, out_specs=out_specs,
                 args=args, name=name, ride=ride, aliases=aliases)
    outs = res if ride is None else res[0]
    out = (outs[0], outs[1][0]) if col_sum else outs[0]
    return out if ride is None else (out, res[1])


def mm_rms_bwd(terms, dxn, x, nw, name, ride=None):
    s, n = x.shape
    nt_ = len(terms)
    tm = _row_tile(s, 256)
    forms = [t[5] for t in terms]

    def body(*refs):
        dxn_ref, x_ref, nw_ref, dx_ref, dnw_ref = refs[2 * nt_:]
        i = pl.program_id(0)
        dh = None
        for t in range(nt_):
            part = _dot(refs[2 * t][...].astype(BF16), refs[2 * t + 1][...], NN if forms[t] == "nn" else NT)
            dh = part if dh is None else dh + part
        dx, dnw = _rms_bwd(dh, x_ref[...], nw_ref[...])
        dx_ref[...] = dxn_ref[...] + dx
        col = jnp.sum(dnw, axis=0, keepdims=True)

        @pl.when(i == 0)
        def _():
            dnw_ref[...] = col

        @pl.when(i > 0)
        def _():
            dnw_ref[...] += col

    in_specs, args = [], []
    for a, cb, w, rb, kb, form in terms:
        in_specs.append(pl.BlockSpec((tm, kb), lambda i, cb=cb: (i, cb)))
        if form == "nn":
            in_specs.append(pl.BlockSpec((kb, n), lambda i, rb=rb: (rb, 0)))
        else:
            in_specs.append(pl.BlockSpec((n, kb), lambda i, rb=rb: (0, rb)))
        args += [a, w]
    row = pl.BlockSpec((tm, n), lambda i: (i, 0))
    vec = pl.BlockSpec((1, n), lambda i: (0, 0))
    res = _pcall(body, out_shape=[_sds((s, n), F32), _sds((1, n), F32)], grid=(s // tm,),
                 in_specs=in_specs + [row, row, vec], out_specs=[row, vec],
                 args=args + [dxn, x, nw.reshape(1, n)], name=name, ride=ride)
    outs = res if ride is None else res[0]
    out = (outs[0], outs[1][0])
    return out if ride is None else (out, res[1])


def rope_tables(s):
    pos = jnp.arange(s, dtype=F32)
    inv = 1.0 / (ROPE_THETA ** (jnp.arange(0, ATT_HEAD_DIM, 2, dtype=F32) / ATT_HEAD_DIM))
    ang = pos[:, None] * inv[None, :]
    cos = jnp.tile(jnp.cos(ang), (1, 2 * LANES // ATT_HEAD_DIM))
    sin = jnp.tile(jnp.sin(ang), (1, 2 * LANES // ATT_HEAD_DIM))
    return cos, sin


def rope_apply(t, cos, sin, name, inverse=False, scale=1.0, out_dtype=BF16):
    s, n = t.shape
    tm = _row_tile(s, 512)
    half = ATT_HEAD_DIM // 2
    reps = n // LANES

    def body(t_ref, c_ref, s_ref, o_ref):
        tf = t_ref[...].astype(F32)
        c = jnp.tile(c_ref[...], (1, reps))
        sn = jnp.tile(s_ref[...], (1, reps))
        lane = lax.broadcasted_iota(jnp.int32, tf.shape, 1)
        first = (lane & (ATT_HEAD_DIM - 1)) < half
        rot = jnp.where(first, -pltpu.roll(tf, n - half, 1), pltpu.roll(tf, half, 1))
        sign = -1.0 if inverse else 1.0
        o_ref[...] = (scale * (tf * c + sign * rot * sn)).astype(out_dtype)

    tab = pl.BlockSpec((tm, LANES), lambda i: (i, 0))
    return _pcall(body, out_shape=[_sds((s, n), out_dtype)], grid=(s // tm,),
                  in_specs=[pl.BlockSpec((tm, n), lambda i: (i, 0)), tab, tab],
                  out_specs=[pl.BlockSpec((tm, n), lambda i: (i, 0))], args=[t, cos, sin], name=name)[0]


CONV_TILE = 256


def _shift_down(u, k):
    if k == 0:
        return u
    row = lax.broadcasted_iota(jnp.int32, u.shape, 0)
    return jnp.where(row >= k, pltpu.roll(u, k, 0), 0.0)


def _shift_up(u, k):
    if k == 0:
        return u
    s = u.shape[0]
    row = lax.broadcasted_iota(jnp.int32, u.shape, 0)
    return jnp.where(row < s - k, pltpu.roll(u, s - k, 0), 0.0)


def _conv_taps(u):
    return [_shift_down(u, CONV_WIDTH - 1 - k) for k in range(CONV_WIDTH)]


def _conv_pre(taps, w_ref, b_ref):
    pre = b_ref[...] + w_ref[0:1, :] * taps[0]
    for k in range(1, CONV_WIDTH):
        pre += w_ref[k:k + 1, :] * taps[k]
    return pre


def conv_fwd(u, w, b, name, ride=None):
    s, c = u.shape

    def body(u_ref, w_ref, b_ref, o_ref):
        pre = _conv_pre(_conv_taps(u_ref[...]), w_ref, b_ref)
        o_ref[...] = pre * _sigmoid(pre)

    col = pl.BlockSpec((s, CONV_TILE), lambda j: (0, j))
    res = _pcall(body, out_shape=[_sds((s, c), F32)], grid=(c // CONV_TILE,),
                 in_specs=[col, pl.BlockSpec((CONV_WIDTH, CONV_TILE), lambda j: (0, j)),
                           pl.BlockSpec((1, CONV_TILE), lambda j: (0, j))],
                 out_specs=[col], args=[u, w, b.reshape(1, c)], name=name, ride=ride)
    return res[0] if ride is None else (res[0][0], res[1])


def conv_bwd(dxs, db_, dc_, u, w, b, name):
    s, c = u.shape
    n_x = dxs.shape[1] // CONV_TILE
    n_b = db_.shape[1] // CONV_TILE

    def body(dx_ref, dbb_ref, dcc_ref, u_ref, w_ref, b_ref, du_ref, dw_ref, dbias_ref):
        j = pl.program_id(0)
        dact = jnp.where(j < n_x, dx_ref[...], jnp.where(j < n_x + n_b, dbb_ref[...], dcc_ref[...]))
        taps = _conv_taps(u_ref[...])
        pre = _conv_pre(taps, w_ref, b_ref)
        sg = _sigmoid(pre)
        dpre = dact * (sg * (1.0 + pre * (1.0 - sg)))
        du = w_ref[CONV_WIDTH - 1:CONV_WIDTH, :] * dpre
        for k in range(CONV_WIDTH - 1):
            du += w_ref[k:k + 1, :] * _shift_up(dpre, CONV_WIDTH - 1 - k)
        du_ref[...] = du
        dbias_ref[...] = jnp.sum(dpre, axis=0, keepdims=True)
        for k in range(CONV_WIDTH):
            dw_ref[k:k + 1, :] = jnp.sum(dpre * taps[k], axis=0, keepdims=True)

    col = pl.BlockSpec((s, CONV_TILE), lambda j: (0, j))
    wsp = pl.BlockSpec((CONV_WIDTH, CONV_TILE), lambda j: (0, j))
    bsp = pl.BlockSpec((1, CONV_TILE), lambda j: (0, j))
    du, dw, db = _pcall(
        body, out_shape=[_sds((s, c), F32), _sds((CONV_WIDTH, c), F32), _sds((1, c), F32)], grid=(c // CONV_TILE,),
        in_specs=[pl.BlockSpec((s, CONV_TILE), lambda j: (0, jnp.minimum(j, n_x - 1))),
                  pl.BlockSpec((s, CONV_TILE), lambda j: (0, jnp.clip(j - n_x, 0, n_b - 1))),
                  pl.BlockSpec((s, CONV_TILE), lambda j: (0, jnp.clip(j - n_x - n_b, 0, n_b - 1))),
                  col, wsp, bsp],
        out_specs=[col, wsp, bsp], args=[dxs, db_, dc_, u, w, b.reshape(1, c)], name=name)
    return du, dw, db[0]


def _lane_pick(mat, idx):
    lane = lax.broadcasted_iota(jnp.int32, mat.shape, 1)
    return jnp.sum(jnp.where(lane == idx, mat, 0.0), axis=1, keepdims=True)


def _sub_pick(mat, idx):
    sub = lax.broadcasted_iota(jnp.int32, mat.shape, 0)
    return jnp.sum(jnp.where(sub == idx, mat, 0.0), axis=0, keepdims=True)


def _expand_heads(cols):
    rows = cols[0].shape[0]
    left = lax.broadcasted_iota(jnp.int32, (rows, LANES), 1) < SSM_HEAD_DIM
    return jnp.concatenate(
        [jnp.where(left, cols[2 * p], cols[2 * p + 1]) for p in range(HEADS_PER_GROUP // 2)], axis=1)


def _dot_01(x, ones, ones_first, pieces):
    tot, rest = None, x
    for _ in range(pieces):
        piece = rest.astype(BF16)
        rest = rest - piece.astype(F32)
        part = _dot(ones, piece) if ones_first else _dot(piece, ones)
        tot = part if tot is None else tot + part
    return tot


def _heads_to_lanes(mat, g):
    jj = lax.broadcasted_iota(jnp.int32, (GROUP_DIM, LANES), 0)
    ll = lax.broadcasted_iota(jnp.int32, (GROUP_DIM, LANES), 1)
    sel = (ll == HEADS_PER_GROUP * g + (jj >> 6)).astype(BF16)
    return _dot_01(mat, sel, False, 3)


def _softplus(x):
    return jnp.maximum(x, 0.0) + jnp.log1p(jnp.exp(-jnp.abs(x)))


def _ssd_scalars(dt_ref, bias_ref, a_ref, dtall, csall, cst):
    dta = _softplus(dt_ref[...] + bias_ref[...])
    row = lax.broadcasted_iota(jnp.int32, (CHUNK, CHUNK), 0)
    col = lax.broadcasted_iota(jnp.int32, (CHUNK, CHUNK), 1)
    cs = _dot_01(dta * a_ref[...], (row >= col).astype(BF16), True, 3)
    dtall[...] = dta
    csall[...] = cs
    cst[...] = cs.T


def _decay_mat(cs_col, cs_row):
    row = lax.broadcasted_iota(jnp.int32, (CHUNK, CHUNK), 0)
    col = lax.broadcasted_iota(jnp.int32, (CHUNK, CHUNK), 1)
    return jnp.exp(jnp.where(row >= col, cs_col - cs_row, NEG))


def _head_mask(xpair, right):
    lane = lax.broadcasted_iota(jnp.int32, xpair.shape, 1)
    keep = (lane >= SSM_HEAD_DIM) if right else (lane < SSM_HEAD_DIM)
    return jnp.where(keep, xpair, 0.0)


def _chunk_cols(x_all, g):
    return [_lane_pick(x_all, HEADS_PER_GROUP * g + r) for r in range(HEADS_PER_GROUP)]


def _decay_col(cs_cols):
    return jnp.concatenate(
        [jnp.broadcast_to(jnp.exp(cc[CHUNK - 1:CHUNK, :]), (SSM_HEAD_DIM, 1)) for cc in cs_cols], axis=0)


def ssd_fwd(act, z, dtp, bias_p, a_p, d_p, normw, name, ride=None):
    s = act.shape[0]
    nc = s // CHUNK

    def body(xs_all, b_all, c_all, z_all, dt_ref, bias_ref, a_ref, d_ref, nw_all,
             yn_all, y_all, st_all, state, dtall, csall, cst):
        _ssd_scalars(dt_ref, bias_ref, a_ref, dtall, csall, cst)

        @pl.when(pl.program_id(0) == 0)
        def _():
            state[...] = jnp.zeros(state.shape, F32)

        for g in range(SSM_GROUPS):
            wide = pl.ds(g * GROUP_DIM, GROUP_DIM)
            narrow = pl.ds(g * SSM_STATE, SSM_STATE)
            group(g, xs_all.at[:, wide], b_all.at[:, narrow], c_all.at[:, narrow], z_all.at[:, wide], d_ref,
                  nw_all.at[:, wide], yn_all.at[:, wide], y_all.at[:, wide], st_all.at[:, pl.ds(g, 1)],
                  state, dtall, csall, cst)

    def group(g, xs_ref, b_ref, c_ref, z_ref, d_ref, nw_ref, yn_ref, y_ref, st_ref, state, dtall, csall, cst):
        cs_cols = _chunk_cols(csall[...], g)
        dt_cols = _chunk_cols(dtall[...], g)
        cs_rows = [_sub_pick(cst[...], HEADS_PER_GROUP * g + r) for r in range(HEADS_PER_GROUP)]
        d_cols = _chunk_cols(d_ref[...], g)
        cs_exp = _expand_heads(cs_cols)
        dt_exp = _expand_heads(dt_cols)
        d_exp = _expand_heads(d_cols)
        xs = xs_ref[...]
        bb = b_ref[...].astype(BF16)
        cb16 = c_ref[...].astype(BF16)
        xdt = xs * dt_exp
        s_prev = state[g]
        st_ref[0, 0] = s_prev
        y_off = _dot(cb16, s_prev.astype(BF16), NT) * jnp.exp(cs_exp)
        decay_st = jnp.exp(cs_exp[CHUNK - 1:CHUNK, :] - cs_exp)
        contrib = _dot((xdt * decay_st).astype(BF16), bb, TN)
        state[g] = _decay_col(cs_cols) * s_prev + contrib
        cbm = _dot(cb16, bb, NT)
        pairs = []
        for p in range(HEADS_PER_GROUP // 2):
            xpair = xdt[:, LANES * p:LANES * (p + 1)]
            m0 = (cbm * _decay_mat(cs_cols[2 * p], cs_rows[2 * p])).astype(BF16)
            m1 = (cbm * _decay_mat(cs_cols[2 * p + 1], cs_rows[2 * p + 1])).astype(BF16)
            pairs.append(_dot(m0, _head_mask(xpair, False).astype(BF16))
                         + _dot(m1, _head_mask(xpair, True).astype(BF16)))
        y = jnp.concatenate(pairs, axis=1) + y_off + xs * d_exp
        y_ref[...] = y
        zf = z_ref[...]
        yg = y * (zf * _sigmoid(zf))
        yn_ref[...] = _rms_fwd(yg, nw_ref[...]).astype(BF16)

    gn = SSM_GROUPS * SSM_STATE
    wide = pl.BlockSpec((CHUNK, D_INNER), lambda c: (c, 0))
    par = pl.BlockSpec((1, LANES), lambda c: (0, 0))
    return _pcall(
        body,
        out_shape=[_sds((s, D_INNER), BF16), _sds((s, D_INNER), F32),
                   _sds((nc, SSM_GROUPS, GROUP_DIM, SSM_STATE), F32)],
        grid=(nc,),
        in_specs=[wide,
                  pl.BlockSpec((CHUNK, gn), lambda c: (c, D_INNER // gn)),
                  pl.BlockSpec((CHUNK, gn), lambda c: (c, D_INNER // gn + 1)),
                  wide,
                  pl.BlockSpec((CHUNK, LANES), lambda c: (c, 0)),
                  par, par, par,
                  pl.BlockSpec((1, D_INNER), lambda c: (0, 0))],
        out_specs=[wide, wide, pl.BlockSpec((1, SSM_GROUPS, GROUP_DIM, SSM_STATE), lambda c: (c, 0, 0, 0))],
        scratch_shapes=[pltpu.VMEM((SSM_GROUPS, GROUP_DIM, SSM_STATE), F32),
                        pltpu.VMEM((CHUNK, LANES), F32), pltpu.VMEM((CHUNK, LANES), F32),
                        pltpu.VMEM((LANES, CHUNK), F32)],
        args=[act, act, act, z, dtp, bias_p, a_p, d_p, normw], name=name, ride=ride)


def ssd_bwd(dyn, act, z, y_pre, states, dtp, bias_p, a_p, d_p, normw, name, ride=None):
    s = act.shape[0]
    nc = s // CHUNK

    def body(dyn_all, xs_all, b_all, c_all, z_all, y_all, st_all, dt_ref, bias_ref, a_ref, d_ref, nw_all,
             dxs_all, db_all, dc_all, dz_all, ddt_ref, dnw_ref, dbias_ref, da_ref, dd_ref,
             dstate, dtall, csall, cst):
        _ssd_scalars(dt_ref, bias_ref, a_ref, dtall, csall, cst)
        ddt_ref[...] = jnp.zeros((CHUNK, LANES), F32)

        @pl.when(pl.program_id(0) == 0)
        def _():
            dstate[...] = jnp.zeros(dstate.shape, F32)
            dnw_ref[...] = jnp.zeros(dnw_ref.shape, F32)
            dbias_ref[...] = jnp.zeros((1, LANES), F32)
            da_ref[...] = jnp.zeros((1, LANES), F32)
            dd_ref[...] = jnp.zeros((1, LANES), F32)

        for g in range(SSM_GROUPS):
            wide = pl.ds(g * GROUP_DIM, GROUP_DIM)
            narrow = pl.ds(g * SSM_STATE, SSM_STATE)
            group(g, dyn_all.at[:, wide], xs_all.at[:, wide], b_all.at[:, narrow], c_all.at[:, narrow],
                  z_all.at[:, wide], y_all.at[:, wide], st_all.at[:, pl.ds(g, 1)], dt_ref, bias_ref, a_ref, d_ref,
                  nw_all.at[:, wide], dxs_all.at[:, wide], db_all.at[:, narrow], dc_all.at[:, narrow],
                  dz_all.at[:, wide], ddt_ref, dnw_ref, dbias_ref, da_ref, dd_ref, dstate, dtall, csall, cst)

    def group(g, dyn_ref, xs_ref, b_ref, c_ref, z_ref, y_ref, st_ref, dt_ref, bias_ref, a_ref, d_ref, nw_ref,
              dxs_ref, db_ref, dc_ref, dz_ref, ddt_ref, dnw_ref, dbias_ref, da_ref, dd_ref,
              dstate, dtall, csall, cst):
        cs_cols = _chunk_cols(csall[...], g)
        dt_cols = _chunk_cols(dtall[...], g)
        cs_rows = [_sub_pick(cst[...], HEADS_PER_GROUP * g + r) for r in range(HEADS_PER_GROUP)]
        d_cols = _chunk_cols(d_ref[...], g)
        cs_exp = _expand_heads(cs_cols)
        dt_exp = _expand_heads(dt_cols)
        d_exp = _expand_heads(d_cols)
        xs = xs_ref[...]
        bb = b_ref[...].astype(BF16)
        cb16 = c_ref[...].astype(BF16)
        xdt = xs * dt_exp
        s_prev = st_ref[0, 0]
        s_prev16 = s_prev.astype(BF16)
        ds_next = dstate[g]
        ds16 = ds_next.astype(BF16)

        zf = z_ref[...]
        sz = _sigmoid(zf)
        silu_z = zf * sz
        y = y_ref[...]
        yg = y * silu_z
        dout = dyn_ref[...]
        dyg, dnw = _rms_bwd(dout, yg, nw_ref[...])
        dnw_ref[pl.ds(g, 1), :] += jnp.sum(dnw, axis=0, keepdims=True)
        dy = dyg * silu_z
        dz_ref[...] = dyg * y * (sz * (1.0 + zf * (1.0 - sz)))
        dd_ref[...] += jnp.sum(_heads_to_lanes(dy * xs, g), axis=0, keepdims=True)

        exp_cs = jnp.exp(cs_exp)
        decay_st = jnp.exp(cs_exp[CHUNK - 1:CHUNK, :] - cs_exp)
        cs_t = _dot(cb16, s_prev16, NT)
        dyo = dy * exp_cs
        dc_acc = _dot(dyo.astype(BF16), s_prev16, NN)
        g1 = _dot(bb, ds16, NT)
        xds = xdt * decay_st
        db_acc = _dot(xds.astype(BF16), ds16, NN)
        dxdt_off = g1 * decay_st
        t_exp = g1 * xds
        dcs_exp = dy * cs_t * exp_cs - t_exp
        decay_c = _decay_col(cs_cols)
        dstate[g] = decay_c * ds_next + _dot(dyo.astype(BF16), cb16, TN)
        dlast_col = jnp.sum(ds_next * s_prev, axis=1, keepdims=True) * decay_c
        jj = lax.broadcasted_iota(jnp.int32, (GROUP_DIM, LANES), 0)
        ll = lax.broadcasted_iota(jnp.int32, (GROUP_DIM, LANES), 1)
        sel = ll == HEADS_PER_GROUP * g + (jj >> 6)
        dlast = jnp.sum(jnp.where(sel, dlast_col, 0.0), axis=0, keepdims=True)
        t_all = _heads_to_lanes(t_exp, g)
        dlast += jnp.sum(t_all, axis=0, keepdims=True)
        dcs_all = _heads_to_lanes(dcs_exp, g)

        cbm = _dot(cb16, bb, NT)
        dcb = jnp.zeros((CHUNK, CHUNK), F32)
        dcs_rows = jnp.zeros((LANES, CHUNK), F32)
        lane_l = lax.broadcasted_iota(jnp.int32, (CHUNK, LANES), 1)
        sub_l = lax.broadcasted_iota(jnp.int32, (LANES, CHUNK), 0)
        dxdt_pairs = []
        for p in range(HEADS_PER_GROUP // 2):
            xpair16 = xdt[:, LANES * p:LANES * (p + 1)].astype(BF16)
            dypair = dy[:, LANES * p:LANES * (p + 1)]
            acc = None
            for r in (2 * p, 2 * p + 1):
                lm = _decay_mat(cs_cols[r], cs_rows[r])
                m = cbm * lm
                dyh = _head_mask(dypair, r % 2 == 1).astype(BF16)
                dm = _dot(dyh, xpair16, NT)
                dcb += dm * lm
                q = dm * m
                idx = HEADS_PER_GROUP * g + r
                dcs_all += jnp.where(lane_l == idx, jnp.sum(q, axis=1, keepdims=True), 0.0)
                dcs_rows -= jnp.where(sub_l == idx, jnp.sum(q, axis=0, keepdims=True), 0.0)
                part = _dot(m.astype(BF16), dyh, TN)
                acc = part if acc is None else acc + part
            dxdt_pairs.append(acc)
        dxdt = jnp.concatenate(dxdt_pairs, axis=1) + dxdt_off
        dcb16 = dcb.astype(BF16)
        dc_ref[...] = dc_acc + _dot(dcb16, bb, NN)
        db_ref[...] = db_acc + _dot(dcb16, cb16, TN)
        dxs_ref[...] = dxdt * dt_exp + dy * d_exp

        dcs_all += dcs_rows.T
        row = lax.broadcasted_iota(jnp.int32, (CHUNK, CHUNK), 0)
        col = lax.broadcasted_iota(jnp.int32, (CHUNK, CHUNK), 1)
        last_row = lax.broadcasted_iota(jnp.int32, (CHUNK, LANES), 0) == CHUNK - 1
        dcs_all += jnp.where(last_row, dlast, 0.0)
        da_all = _dot_01(dcs_all, (col >= row).astype(BF16), True, 3)
        dta = dtall[...]
        in_group = jnp.logical_and(lane_l >= HEADS_PER_GROUP * g, lane_l < HEADS_PER_GROUP * (g + 1))
        ddt = jnp.where(in_group, da_all * a_ref[...] + _heads_to_lanes(dxdt * xs, g), 0.0)
        da_ref[...] += jnp.sum(jnp.where(in_group, da_all * dta, 0.0), axis=0, keepdims=True)
        ddt_raw = ddt * _sigmoid(dt_ref[...] + bias_ref[...])
        ddt_ref[...] += ddt_raw
        dbias_ref[...] += jnp.sum(ddt_raw, axis=0, keepdims=True)

    gn = SSM_GROUPS * SSM_STATE
    wide = pl.BlockSpec((CHUNK, D_INNER), lambda c: (nc - 1 - c, 0))
    st = pl.BlockSpec((CHUNK, gn), lambda c: (nc - 1 - c, 0))
    par = pl.BlockSpec((1, LANES), lambda c: (0, 0))
    dtb = pl.BlockSpec((CHUNK, LANES), lambda c: (nc - 1 - c, 0))
    f = lambda shape: _sds(shape, F32)
    return _pcall(
        body,
        out_shape=[f((s, D_INNER)), f((s, gn)), f((s, gn)),
                   f((s, D_INNER)), f((s, LANES)), f((8, GROUP_DIM)), f((1, LANES)), f((1, LANES)), f((1, LANES))],
        grid=(nc,),
        in_specs=[wide, wide,
                  pl.BlockSpec((CHUNK, gn), lambda c: (nc - 1 - c, D_INNER // gn)),
                  pl.BlockSpec((CHUNK, gn), lambda c: (nc - 1 - c, D_INNER // gn + 1)),
                  wide, wide,
                  pl.BlockSpec((1, SSM_GROUPS, GROUP_DIM, SSM_STATE), lambda c: (nc - 1 - c, 0, 0, 0)),
                  dtb, par, par, par,
                  pl.BlockSpec((1, D_INNER), lambda c: (0, 0))],
        out_specs=[wide, st, st, wide, dtb, pl.BlockSpec((8, GROUP_DIM), lambda c: (0, 0)), par, par, par],
        scratch_shapes=[pltpu.VMEM((SSM_GROUPS, GROUP_DIM, SSM_STATE), F32),
                        pltpu.VMEM((CHUNK, LANES), F32), pltpu.VMEM((CHUNK, LANES), F32),
                        pltpu.VMEM((LANES, CHUNK), F32)],
        args=[dyn, act, act, act, z, y_pre, states, dtp, bias_p, a_p, d_p, normw], name=name, ride=ride)


def _attn_probs(q, kp, kc, sink, n):
    sp = _dot(q, kp, NT)
    sc = _dot(q, kc, NT)
    i = lax.broadcasted_iota(jnp.int32, sp.shape, 0) & (WINDOW - 1)
    j = lax.broadcasted_iota(jnp.int32, sp.shape, 1)
    sp = jnp.where(jnp.logical_and(j > i, n > 0), sp, NEG)
    sc = jnp.where(j <= i, sc, NEG)
    m = jnp.maximum(jnp.maximum(jnp.max(sp, axis=1, keepdims=True), jnp.max(sc, axis=1, keepdims=True)), sink)
    pp = jnp.exp(sp - m)
    pc = jnp.exp(sc - m)
    ps = jnp.exp(sink - m)
    inv = 1.0 / (jnp.sum(pp, axis=1, keepdims=True) + jnp.sum(pc, axis=1, keepdims=True) + ps)
    return pp * inv, pc * inv, ps * inv


def attn_fwd(qt, kt, vt, sink_rows, name, ride=None):
    s = qt.shape[1]
    nb = s // WINDOW
    rows = Q_PER_KV * WINDOW

    def body(q_ref, kp_ref, kc_ref, vp_ref, vc_ref, sk_ref, o_ref):
        n = pl.program_id(0)
        for h in range(N_KV_HEADS):
            heads = pl.ds(h * Q_PER_KV, Q_PER_KV)
            q = q_ref[heads].reshape(rows, ATT_HEAD_DIM)
            pp, pc, _ = _attn_probs(q, kp_ref[h], kc_ref[h], sk_ref[h], n)
            o = _dot(pp.astype(BF16), vp_ref[h]) + _dot(pc.astype(BF16), vc_ref[h])
            o_ref[heads] = o.reshape(Q_PER_KV, WINDOW, ATT_HEAD_DIM).astype(BF16)

    qsp = pl.BlockSpec((N_Q_HEADS, WINDOW, ATT_HEAD_DIM), lambda n: (0, n, 0))
    prev = pl.BlockSpec((N_KV_HEADS, WINDOW, ATT_HEAD_DIM), lambda n: (0, jnp.maximum(n - 1, 0), 0))
    cur = pl.BlockSpec((N_KV_HEADS, WINDOW, ATT_HEAD_DIM), lambda n: (0, n, 0))
    return _pcall(body, out_shape=[_sds(qt.shape, BF16)], grid=(nb,),
                  in_specs=[qsp, prev, cur, prev, cur, pl.BlockSpec((N_KV_HEADS, rows, 1), lambda n: (0, 0, 0))],
                  out_specs=[qsp], args=[qt, kt, kt, vt, vt, sink_rows], name=name, ride=ride)


def attn_bwd(qt, kt, vt, sink_rows, dot_, name, ride=None):
    s = qt.shape[1]
    nb = s // WINDOW
    rows = Q_PER_KV * WINDOW

    def body(q_ref, kp_ref, kc_ref, vp_ref, vc_ref, sk_ref, do_ref, dq_ref, dk_ref, dv_ref, ds_ref, kacc, vacc):
        n = pl.program_id(0)

        @pl.when(n == 0)
        def _():
            kacc[...] = jnp.zeros(kacc.shape, F32)
            vacc[...] = jnp.zeros(vacc.shape, F32)

        @pl.when(n < nb)
        def _():
            for h in range(N_KV_HEADS):
                heads = pl.ds(h * Q_PER_KV, Q_PER_KV)
                q = q_ref[heads].reshape(rows, ATT_HEAD_DIM)
                do = do_ref[heads].reshape(rows, ATT_HEAD_DIM)
                kp, kc, vp, vc = kp_ref[h], kc_ref[h], vp_ref[h], vc_ref[h]
                pp, pc, ps = _attn_probs(q, kp, kc, sk_ref[h], n)
                dpp = _dot(do, vp, NT)
                dpc = _dot(do, vc, NT)
                delta = jnp.sum(pp * dpp, axis=1, keepdims=True) + jnp.sum(pc * dpc, axis=1, keepdims=True)
                dsp = (pp * (dpp - delta)).astype(BF16)
                dsc = (pc * (dpc - delta)).astype(BF16)
                dq = _dot(dsp, kp) + _dot(dsc, kc)
                dq_ref[heads] = dq.reshape(Q_PER_KV, WINDOW, ATT_HEAD_DIM)
                dk_ref[h] = kacc[h] + _dot(dsp, q, TN)
                dv_ref[h] = vacc[h] + _dot(pp.astype(BF16), do, TN)
                kacc[h] = _dot(dsc, q, TN)
                vacc[h] = _dot(pc.astype(BF16), do, TN)
                dsk = -ps * delta
                sub = lax.broadcasted_iota(jnp.int32, (8, LANES), 0)
                tile = jnp.zeros((8, LANES), F32)
                for j in range(Q_PER_KV):
                    tile += jnp.where(sub == j, jnp.sum(dsk[j * WINDOW:(j + 1) * WINDOW, :], axis=0, keepdims=True),
                                      0.0)
                ds_ref[h, 0] = tile

        @pl.when(n == nb)
        def _():
            dk_ref[...] = kacc[...]
            dv_ref[...] = vacc[...]
            ds_ref[...] = jnp.zeros(ds_ref.shape, F32)

    last = nb - 1
    qsp = pl.BlockSpec((N_Q_HEADS, WINDOW, ATT_HEAD_DIM), lambda n: (0, jnp.minimum(n, last), 0))
    prev = pl.BlockSpec((N_KV_HEADS, WINDOW, ATT_HEAD_DIM), lambda n: (0, jnp.clip(n - 1, 0, last), 0))
    cur = pl.BlockSpec((N_KV_HEADS, WINDOW, ATT_HEAD_DIM), lambda n: (0, jnp.minimum(n, last), 0))
    dkv = pl.BlockSpec((N_KV_HEADS, WINDOW, ATT_HEAD_DIM), lambda n: (0, jnp.maximum(n - 1, 0), 0))
    f = lambda shape: _sds(shape, F32)
    acc = pltpu.VMEM((N_KV_HEADS, WINDOW, ATT_HEAD_DIM), F32)
    return _pcall(
        body, out_shape=[f(qt.shape), f(kt.shape), f(vt.shape), f((N_KV_HEADS, nb + 1, 8, LANES))],
        grid=(nb + 1,),
        in_specs=[qsp, prev, cur, prev, cur, pl.BlockSpec((N_KV_HEADS, rows, 1), lambda n: (0, 0, 0)), qsp],
        out_specs=[qsp, dkv, dkv, pl.BlockSpec((N_KV_HEADS, 1, 8, LANES), lambda n: (0, n, 0, 0))],
        scratch_shapes=[acc, acc], args=[qt, kt, kt, vt, vt, sink_rows, dot_], name=name, ride=ride)


def loss_head(x, w, tgt, name):
    s, d = x.shape
    tm = _row_tile(s, 256)

    def body(x_ref, w_ref, t_ref, loss_ref, dx_ref, dw_ref):
        i = pl.program_id(0)
        xf = x_ref[...]
        wv = w_ref[...]
        r = lax.rsqrt(jnp.mean(xf * xf, axis=-1, keepdims=True) + EPS)
        xhat = xf * r
        e = xhat * wv - t_ref[...]
        part = 0.5 * jnp.sum(jnp.mean(e * e, axis=-1, keepdims=True), axis=0, keepdims=True)
        dy = e * (1.0 / d)
        dxhat = dy * wv
        dx_ref[...] = r * (dxhat - xhat * jnp.mean(dxhat * xhat, axis=-1, keepdims=True))
        col = jnp.sum(dy * xhat, axis=0, keepdims=True)

        @pl.when(i == 0)
        def _():
            loss_ref[...] = jnp.broadcast_to(part, (1, LANES))
            dw_ref[...] = col

        @pl.when(i > 0)
        def _():
            loss_ref[...] += jnp.broadcast_to(part, (1, LANES))
            dw_ref[...] += col

    row = pl.BlockSpec((tm, d), lambda i: (i, 0))
    vec = pl.BlockSpec((1, d), lambda i: (0, 0))
    return _pcall(body, out_shape=[_sds((1, LANES), F32), _sds((s, d), F32), _sds((1, d), F32)], grid=(s // tm,),
                  in_specs=[row, vec, row], out_specs=[pl.BlockSpec((1, LANES), lambda i: (0, 0)), row, vec],
                  args=[x, w.reshape(1, d), tgt], name=name)


ELEMWISE_TILE = 720 * 1024


def _tile_rows(r, c, max_elems=262144, mult=16):
    best = None
    for t in range(mult, r + 1, mult):
        if r % t == 0 and t * c <= max_elems:
            best = t
    return best or r


def add_pair(xhs, ps, c_idx, name):
    n = len(xhs)
    _, r, c = xhs[0].shape
    tr = _tile_rows(r, c, max_elems=ELEMWISE_TILE)

    def body(c_ref, *refs):
        for x_ref, p_ref, o_ref in zip(refs[:n], refs[n:2 * n], refs[2 * n:]):
            o_ref[...] = (x_ref[0].astype(F32) + p_ref[...].astype(F32)).astype(BF16)

    blk = pl.BlockSpec((tr, c), lambda i, cr: (i, 0))
    return pl.pallas_call(
        body, out_shape=tuple([_sds((r, c), BF16)] * n),
        grid_spec=pltpu.PrefetchScalarGridSpec(
            num_scalar_prefetch=1, grid=(r // tr,),
            in_specs=[pl.BlockSpec((1, tr, c), lambda i, cr: (cr[0], i, 0))] * n + [blk] * n,
            out_specs=tuple([blk] * n)),
        name=name, compiler_params=_cp(1))(c_idx, *xhs, *ps)


def sum_chips(qs, owns, chip_idx, name):
    n = len(qs)
    _, r, c = qs[0].shape
    tr = _tile_rows(r, c, max_elems=ELEMWISE_TILE // max(1, n - 1))

    def body(k_ref, *refs):
        k = k_ref[0]
        for q_ref, own_ref, o_ref in zip(refs[:n], refs[n:2 * n], refs[2 * n:]):
            mine = own_ref[0].astype(F32)
            tot = None
            for j in range(N_CHIPS):
                term = jnp.where(k == j, mine, q_ref[j].astype(F32))
                tot = term if tot is None else tot + term
            o_ref[...] = tot

    return pl.pallas_call(
        body, out_shape=tuple([_sds((r, c), F32)] * n),
        grid_spec=pltpu.PrefetchScalarGridSpec(
            num_scalar_prefetch=1, grid=(r // tr,),
            in_specs=([pl.BlockSpec((N_CHIPS, tr, c), lambda i, kr: (0, i, 0))] * n
                      + [pl.BlockSpec((1, tr, c), lambda i, kr: (kr[0], i, 0))] * n),
            out_specs=tuple([pl.BlockSpec((tr, c), lambda i, kr: (i, 0))] * n)),
        name=name, compiler_params=_cp(1))(chip_idx, *qs, *owns)


def adamw(w, g, m, v, name):
    r, c = w.shape
    tr = _tile_rows(r, c, mult=8)
    c1 = 1.0 / (1.0 - ADAM_B1 ** ADAM_STEP)
    c2 = 1.0 / (1.0 - ADAM_B2 ** ADAM_STEP)

    def body(w_ref, g_ref, m_ref, v_ref, d_ref, mo_ref, vo_ref):
        gf = g_ref[...]
        mn = ADAM_B1 * m_ref[...] + (1.0 - ADAM_B1) * gf
        vn = ADAM_B2 * v_ref[...] + (1.0 - ADAM_B2) * (gf * gf)
        mo_ref[...] = mn
        vo_ref[...] = vn
        d_ref[...] = -ADAM_LR * ((mn * c1) / (jnp.sqrt(vn * c2) + ADAM_EPS) + ADAM_WD * w_ref[...])

    blk = pl.BlockSpec((tr, c), lambda i: (i, 0))
    out = _sds((r, c), F32)
    return _pcall(body, out_shape=[out, out, out], grid=(r // tr,), in_specs=[blk] * 4, out_specs=[blk] * 3,
                  args=[w, g, m, v], name=name)


WEIGHTS = ['norm_w', 'ffn_w_gate', 'ffn_w_up', 'ffn_w_down', 'ssm_w_in', 'ssm_conv_w', 'ssm_conv_b', 'ssm_dt_bias',
           'ssm_a_log', 'ssm_d', 'ssm_norm_w', 'ssm_w_out', 'kv_norm_w', 'w_k', 'b_k', 'w_v', 'b_v', 'attn_w_q',
           'attn_b_q', 'attn_sinks', 'attn_w_o', 'attn_b_o', 'final_norm_w']
BIG = ['ffn_w_gate', 'ffn_w_up', 'ffn_w_down', 'ssm_w_in', 'ssm_w_out', 'w_k', 'w_v', 'attn_w_q', 'attn_w_o']
TRANSPOSED = ('ffn_w_gate', 'ffn_w_up', 'ssm_w_in')
SMALL = [n for n in WEIGHTS if n not in BIG]
SMALL_SHARDED = {'norm_w': 2, 'ssm_conv_w': 2, 'ssm_conv_b': 1, 'ssm_norm_w': 1}
ROW_ALIGN = 8 * LANES


def _pack_rows(parts):
    flat = jnp.concatenate([p.reshape(-1).astype(F32) for p in parts])
    pad = (-flat.size) % ROW_ALIGN
    return jnp.pad(flat, (0, pad)).reshape(-1, LANES)


def _unpack_rows(buf, shapes):
    flat = buf.reshape(-1)
    out, pos = [], 0
    for shp in shapes:
        size = math.prod(shp)
        out.append(flat[pos:pos + size].reshape(shp))
        pos += size
    return out


def _as2d(a):
    return a.reshape(-1, a.shape[-1])


def _heads_major(t, n_heads):
    s = t.shape[0]
    return t.reshape(s, n_heads, ATT_HEAD_DIM).transpose(1, 0, 2)


def _tokens_major(t):
    h, s, dh = t.shape
    return t.transpose(1, 0, 2).reshape(s, h * dh)


def _pad_lanes(v):
    return jnp.pad(v.reshape(1, -1), ((0, 0), (0, LANES - v.size)))


def _chips_first(t):
    return t.swapaxes(0, 1).reshape((-1,) + t.shape[3:])


def _parts_first(t, rows):
    return t.reshape((N_CHIPS, N_CORES, rows) + t.shape[1:]).swapaxes(0, 1)


def kernel(*args):
    names = (['x'] + WEIGHTS + ['loss_target'] + ['m_' + n for n in WEIGHTS] + ['v_' + n for n in WEIGHTS])
    a = dict(zip(names, args))
    for n in TRANSPOSED:
        for pre in ('', 'm_', 'v_'):
            a[pre + n] = a[pre + n].swapaxes(-1, -2)
    xi, yi, ci = lax.axis_index("x"), lax.axis_index("y"), lax.axis_index("c")
    chip = 2 * xi + yi
    south = ci == 0
    c_idx = jnp.reshape(ci, (1,)).astype(jnp.int32)
    chip_idx = jnp.reshape(chip, (1,)).astype(jnp.int32)
    x0 = a['x'][0]
    s = x0.shape[0]
    cos, sin = rope_tables(s)

    def own_slot(full, mine):
        return lax.dynamic_update_slice_in_dim(full, mine[:, None], chip, axis=1)

    def ffn_shard(l, i, src):
        return [src[n][l, i].astype(BF16).reshape(N_CORES, FF_PART, D_MODEL)
                for n in ('ffn_w_gate', 'ffn_w_up', 'ffn_w_down')]

    def own_slots(fulls, mines):
        return [own_slot(f, m) for f, m in zip(fulls, mines)]
    small_names = list(SMALL_SHARDED)
    small_sh = _pack_rows([a[n] for n in small_names])
    small_sh = small_sh.reshape(N_CORES, small_sh.shape[0] // 2, LANES)
    sh00 = ffn_shard(0, 0, a)
    (first_flight,), started = split_start([sh00 + [small_sh]], "gather", "gather_start_first")
    held = lax.optimization_barrier((started, {n: a[n] for n in BIG}))[1]
    sh01, sh10, sh11 = ffn_shard(0, 1, held), ffn_shard(1, 0, held), ffn_shard(1, 1, held)
    w_in_sh = jnp.pad(held['ssm_w_in'][0], ((0, IN_SHARD_PAD - IN_SHARD), (0, 0))).astype(BF16).reshape(
        N_CORES, IN_SHARD_PAD // 2, D_MODEL)
    w_out_sh = held['ssm_w_out'][0].astype(BF16).reshape(N_CORES, 256, D_MODEL)
    attn_sh = jnp.stack([held['attn_w_q'][0], held['attn_w_o'][0]]).astype(BF16)
    kv_sh = jnp.stack([held['w_k'], held['w_v']]).astype(BF16)
    rest_flights, all_started = split_start([[w_in_sh, kv_sh], [w_out_sh], sh01, sh10, [attn_sh], sh11], "gather",
                                            "gather_start_rest")
    in_flight = [first_flight] + rest_flights

    def arrive(idx, after, tag):
        return forward_cores(split_arrive(in_flight[idx], "gather", after, "gather_arrive_" + tag))

    first = run_exchange(arrive(0, all_started, "first"), "gather_hop_first")
    w00 = own_slots(first[:3], sh00)
    smalls = own_slot(first[3], small_sh)
    p = {}
    per_chip = [_unpack_rows(smalls[:, k], [a[n].shape for n in small_names]) for k in range(N_CHIPS)]
    for idx, n in enumerate(small_names):
        p[n] = jnp.concatenate([per_chip[k][idx] for k in range(N_CHIPS)], axis=SMALL_SHARDED[n])
    nw = p['norm_w']
    conv_w, conv_b, ssm_nw = p['ssm_conv_w'][0], p['ssm_conv_b'][0], p['ssm_norm_w'][0].reshape(1, D_INNER)

    h00 = rmsnorm_fwd(x0, nw[0, 0], "norm_in")
    x1, h01, gu00 = ffn_fwd(h00, x0, *w00, [nw[0, 1]], "ffn_fwd_00")
    w_in_g, kv_g = run_exchange(arrive(1, x1, "in"), "gather_hop_in")
    w_in_t = _chips_first(own_slot(w_in_g, w_in_sh)).reshape(N_CHIPS, IN_SHARD_PAD, D_MODEL)[:, :IN_SHARD].reshape(
        IN_PROJ_DIM, D_MODEL)
    w_dt_t = jnp.pad(w_in_t[D_INNER + CONV_DIM:], ((0, LANES - SSM_HEADS), (0, 0)))
    kv_g = own_slot(kv_g, kv_sh)
    w_k, w_v = kv_g[0].reshape(D_MODEL, KV_DIM), kv_g[1].reshape(D_MODEL, KV_DIM)

    zz = mm_nt(h01, w_in_t, "ssm_in_z", n=D_INNER)
    xbc = mm_nt(h01, w_in_t, "ssm_in_xbc", n=CONV_DIM, row0=D_INNER)
    dtp = mm_nt(h01, w_dt_t, "ssm_in_dt")
    act = conv_fwd(xbc, conv_w, conv_b, "ssm_conv")
    bias_p = _pad_lanes(a['ssm_dt_bias'][0])
    a_p = _pad_lanes(-jnp.exp(a['ssm_a_log'][0]))
    d_p = _pad_lanes(a['ssm_d'][0])
    (yn, y_pre, states), (w_out_g,) = ssd_fwd(act, zz, dtp, bias_p, a_p, d_p, ssm_nw, "ssd_fwd",
                                              ride=arrive(2, act, "out"))
    w_out = _chips_first(own_slot(w_out_g, w_out_sh))
    (x2, h02), w01 = mm_res(yn, w_out, x1, "ssm_out", norm_ws=[nw[0, 2]], ride=arrive(3, yn, "01"))
    w01 = own_slots(w01, sh01)
    x3, hkv, h10, gu01 = ffn_fwd(h02, x2, *w01, [a['kv_norm_w'], nw[1, 0]], "ffn_fwd_01")
    w10 = own_slots(run_exchange(arrive(4, x3, "10"), "gather_hop_10"), sh10)

    k_rot = rope_apply(mm_nn(hkv, w_k, "kv_k", bias=a['b_k']), cos, sin, "rope_k")
    v = mm_nn(hkv, w_v, "kv_v", bias=a['b_v'], out_dtype=BF16)
    kt = _heads_major(k_rot, N_KV_HEADS)
    vt = _heads_major(v, N_KV_HEADS)

    (x4, h11, gu10), (attn_g,) = ffn_fwd(h10, x3, *w10, [nw[1, 1]], "ffn_fwd_10", ride=arrive(5, v, "attn"))
    attn_g = own_slot(attn_g, attn_sh)
    w_q, w_o = attn_g[0].reshape(D_MODEL, D_MODEL), attn_g[1].reshape(D_MODEL, D_MODEL)
    scale = 1.0 / math.sqrt(ATT_HEAD_DIM)
    q_rot = rope_apply(mm_nn(h11, w_q, "attn_q", bias=a['attn_b_q'][0]), cos, sin, "rope_q", scale=scale)
    qt = _heads_major(q_rot, N_Q_HEADS)
    sink_rows = jnp.repeat(a['attn_sinks'][0].reshape(N_KV_HEADS, Q_PER_KV), WINDOW, axis=1).reshape(
        N_KV_HEADS, Q_PER_KV * WINDOW, 1)
    (ot,) = attn_fwd(qt, kt, vt, sink_rows, "attn_fwd")
    o = _tokens_major(ot)
    (x5, h12), w11 = mm_res(o, w_o, x4, "attn_out", bias=a['attn_b_o'][0], norm_ws=[nw[1, 2]],
                            ride=arrive(6, ot, "11"))
    w11 = own_slots(w11, sh11)
    x6, gu11 = ffn_fwd(h12, x5, *w11, [], "ffn_fwd_11")

    loss_v, dx6, d_final = loss_head(x6, a['final_norm_w'], a['loss_target'][0], "loss_head")
    loss = lax.psum(loss_v[0, 0], ("x", "y", "c"))
    g = {'final_norm_w': d_final[0]}

    def same_shape(xs, ys):
        runs = []
        for xv, yv in zip(xs, ys):
            if runs and runs[-1][0][0].shape == xv.shape:
                runs[-1][0].append(xv)
                runs[-1][1].append(yv)
            else:
                runs.append(([xv], [yv]))
        return runs

    def pre_reduce(grads, sib, tag):
        out = []
        for idx, (grp, sbs) in enumerate(same_shape(grads, list(sib))):
            ts = add_pair([gr.reshape(2, -1, gr.shape[-1]) for gr in grp], [_as2d(sb) for sb in sbs], c_idx,
                          "rs_add_%s_%d" % (tag, idx))
            out += [t.reshape(gr.shape[1:]) for t, gr in zip(ts, grp)]
        return out

    def chip_sum(landed, parts, tag):
        out = []
        for idx, (qs, owns) in enumerate(same_shape(list(landed), parts)):
            ts = sum_chips([q.reshape(N_CHIPS, -1, q.shape[-1]) for q in qs],
                           [own.reshape(N_CHIPS, -1, own.shape[-1]) for own in owns], chip_idx,
                           "rs_sum_%s_%d" % (tag, idx))
            out += [t.reshape(q.shape[1:]) for t, q in zip(ts, qs)]
        return out

    dnw = [[None] * 3 for _ in range(2)]
    sums = {}

    def trade(key):
        return swap_cores(sums[key], False)

    dx5, dnw12, *g11 = ffn_bwd(dx6, h12, x5, nw[1, 2], gu11, *w11, "ffn_bwd_11")
    dnw[1][2] = dnw12[0]
    (d_wo, g['attn_b_o']), sib11 = mm_tn(o, dx5, "attn_dwo", col_sum=True, ride=swap_cores(g11, True))
    t11 = pre_reduce(g11, sib11, "11")
    do = mm_nt(dx5, w_o, "attn_do", out_dtype=BF16)
    (dqt, dkt, dvt, dsink), land11 = attn_bwd(qt, kt, vt, sink_rows, _heads_major(do, N_Q_HEADS), "attn_bwd",
                                             ride=scatter_chips(t11))
    sums['11'] = chip_sum(land11, t11, "11")
    g['attn_sinks'] = jnp.sum(dsink[:, :, :Q_PER_KV, 0], axis=1).reshape(N_Q_HEADS)
    dq_pre = rope_apply(_tokens_major(dqt), cos, sin, "rope_dq", inverse=True, scale=scale, out_dtype=F32)
    d_wq, g['attn_b_q'] = mm_tn(h11, dq_pre, "attn_dwq", col_sum=True)
    g_attn = [jnp.stack([d_wq.reshape(N_CHIPS, 256, D_MODEL), d_wo.reshape(N_CHIPS, 256, D_MODEL)])]
    (dx4, dnw[1][1]), sib_attn = mm_rms_bwd([(dq_pre, 0, w_q, 0, D_MODEL, "nt")], dx5, x4, nw[1, 1], "attn_bwd_dh",
                                            ride=swap_cores(g_attn, True))
    t_attn = pre_reduce(g_attn, sib_attn, "attn")
    (dx3, dnw10, *g10), landed = ffn_bwd(dx4, h10, x3, nw[1, 0], gu10, *w10, "ffn_bwd_10",
                                         ride=join(scatter_chips(t_attn), trade('11')))
    dnw[1][0] = dnw10[0]
    sums['attn'] = chip_sum(landed[:1], t_attn, "attn")
    theirs = {'11': landed[1:]}
    dk_pre = rope_apply(_tokens_major(dkt), cos, sin, "rope_dk", inverse=True, out_dtype=F32)
    dv = _tokens_major(dvt)
    (d_wk, g['b_k']), sib10 = mm_tn(hkv, dk_pre, "kv_dwk", col_sum=True, ride=swap_cores(g10, True))
    t10 = pre_reduce(g10, sib10, "10")
    d_wv, g['b_v'] = mm_tn(hkv, dv, "kv_dwv", col_sum=True)
    g_kv = [jnp.stack([d_wk.reshape(N_CHIPS, 256, KV_DIM), d_wv.reshape(N_CHIPS, 256, KV_DIM)])]
    (dx3, g['kv_norm_w']), sib_kv = mm_rms_bwd(
        [(dk_pre, 0, w_k, 0, KV_DIM, "nt"), (dv, 0, w_v, 0, KV_DIM, "nt")], dx3, x3, a['kv_norm_w'], "kv_bwd_dh",
        ride=swap_cores(g_kv, True))
    t_kv = pre_reduce(g_kv, sib_kv, "kv")
    (dx2, dnw02, *g01), landed = ffn_bwd(dx3, h02, x2, nw[0, 2], gu01, *w01, "ffn_bwd_01",
                                         ride=join(scatter_chips(t10 + t_kv), trade('attn')))
    dnw[0][2] = dnw02[0]
    sums['10'] = chip_sum(landed[:3], t10, "10")
    sums['kv'] = chip_sum(landed[3:4], t_kv, "kv")
    theirs['attn'] = landed[4:]
    d_wout, sib01 = mm_tn(yn, dx2, "ssm_dwout", ride=swap_cores(g01, True))
    t01 = pre_reduce(g01, sib01, "01")
    dyn = mm_nt(dx2, w_out, "ssm_dyn")
    (dxs, db_, dc_, dz, ddt, d_ssm_nw, d_bias, d_a, d_d), landed = ssd_bwd(
        dyn, act, zz, y_pre, states, dtp, bias_p, a_p, d_p, ssm_nw, "ssd_bwd",
        ride=join(scatter_chips(t01), trade('10'), trade('kv')))
    sums['01'] = chip_sum(landed[:3], t01, "01")
    theirs['10'], theirs['kv'] = landed[3:6], landed[6:]
    g['ssm_norm_w'] = d_ssm_nw[:SSM_GROUPS].reshape(D_INNER)
    g['ssm_dt_bias'] = d_bias[0, :SSM_HEADS]
    g['ssm_a_log'] = d_a[0, :SSM_HEADS] * a_p[0, :SSM_HEADS]
    g['ssm_d'] = d_d[0, :SSM_HEADS]
    dxbc, g['ssm_conv_w'], g['ssm_conv_b'] = conv_bwd(dxs, db_, dc_, xbc, conv_w, conv_b, "ssm_conv_bwd")
    d_win = mm_tn(dz, h01, "ssm_dwz", rows=IN_PROJ_DIM)
    d_win = mm_tn(dxbc, h01, "ssm_dwxbc", into=d_win, rows=IN_PROJ_DIM, row0=D_INNER)
    d_win = mm_tn(ddt, h01, "ssm_dwdt", into=d_win, rows=IN_PROJ_DIM, row0=D_INNER + CONV_DIM, m_valid=SSM_HEADS)
    d_win = jnp.pad(d_win.reshape(N_CHIPS, IN_SHARD, D_MODEL), ((0, 0), (0, IN_SHARD_PAD - IN_SHARD), (0, 0)))
    g_ssm = [_parts_first(d_win.reshape(-1, D_MODEL), IN_SHARD_PAD // 2), _parts_first(d_wout, 256)]
    kb = 1024
    terms = ([(dz, j, w_in_t, j, kb, "nn") for j in range(D_INNER // kb)]
             + [(dxbc, j, w_in_t, D_INNER // kb + j, kb, "nn") for j in range(CONV_DIM // kb)]
             + [(ddt, 0, w_dt_t, 0, LANES, "nn")])
    (dx1, dnw[0][1]), sib_ssm = mm_rms_bwd(terms, dx2, x1, nw[0, 1], "ssm_bwd_dh", ride=swap_cores(g_ssm, True))
    t_ssm = pre_reduce(g_ssm, sib_ssm, "ssm")
    (grad_x, dnw00, *g00), landed = ffn_bwd(dx1, h00, x0, nw[0, 0], gu00, *w00, "ffn_bwd_00",
                                            ride=join(scatter_chips(t_ssm), trade('01')))
    dnw[0][0] = dnw00[0]
    sums['ssm'] = chip_sum(landed[:2], t_ssm, "ssm")
    theirs['01'] = landed[2:]
    landed = run_exchange(join(swap_cores(g00, True), trade('ssm')), "rs_swap_00")
    t00 = pre_reduce(g00, landed[:3], "00")
    theirs['ssm'] = landed[3:]

    def both(key):
        return [(jnp.where(south, m_, t_), jnp.where(south, t_, m_)) for m_, t_ in zip(sums[key], theirs[key])]

    g['norm_w'] = jnp.stack([jnp.stack(r) for r in dnw])
    red = all_reduce_small(_pack_rows([g[n] for n in SMALL]), "reduce_vectors")

    t00 = lax.optimization_barrier((red, t00))[1]
    (flight00,), flying = split_start([t00], "scatter", "rs_scatter_00_start")

    def held(val):
        return lax.optimization_barrier((flying, val))[1]

    delta, new_m, new_v, gw = {}, {}, {}, {}
    ffn_names = ('ffn_w_gate', 'ffn_w_up', 'ffn_w_down')
    full = {key: both(key) for key in ('attn', 'kv', 'ssm')}
    lo, hi = full['attn'][0]
    gw['attn_w_q'], gw['attn_w_o'] = lo[None], hi[None]
    lo, hi = full['kv'][0]
    gw['w_k'], gw['w_v'] = lo, hi
    lo, hi = full['ssm'][0]
    gw['ssm_w_in'] = jnp.concatenate([lo, hi], axis=0)[:IN_SHARD][None]
    lo, hi = full['ssm'][1]
    gw['ssm_w_out'] = jnp.concatenate([lo, hi], axis=0)[None]

    for n, t in zip(SMALL, _unpack_rows(red, [g[n].shape for n in SMALL])):
        if n in SMALL_SHARDED:
            ax = SMALL_SHARDED[n] - (a[n].ndim - t.ndim)
            width = a[n].shape[SMALL_SHARDED[n]]
            t = lax.dynamic_slice_in_dim(t, chip * width, width, axis=ax)
        gw[n] = t.reshape(a[n].shape)

    def update(n):
        d, mo, vo = adamw(_as2d(a[n]), held(_as2d(gw[n])), _as2d(a['m_' + n]), _as2d(a['v_' + n]), "adamw_" + n)
        delta[n], new_m[n], new_v[n] = d.reshape(a[n].shape), mo.reshape(a[n].shape), vo.reshape(a[n].shape)

    for n in BIG:
        if n not in ffn_names:
            update(n)
    shapes = [a[n].shape for n in SMALL]
    packed = [_pack_rows([src[n] for n in SMALL]) for src in
              (a, gw, {n: a['m_' + n] for n in SMALL}, {n: a['v_' + n] for n in SMALL})]
    outs = adamw(*packed, "adamw_vectors")
    for dst, buf in zip((delta, new_m, new_v), outs):
        for n, t in zip(SMALL, _unpack_rows(buf, shapes)):
            dst[n] = t
    for key in ('01', '10', '11'):
        sums[key] = held(list(sums[key]))
    rest = [both(key) for key in ('01', '10', '11')]
    done = lax.optimization_barrier((outs[0], [delta[n] for n in BIG if n not in ffn_names], rest))[0]
    land00 = split_arrive(flight00, "scatter", done, "rs_scatter_00_arrive")
    sums['00'] = chip_sum(land00, t00, "00")
    theirs['00'] = run_exchange(trade('00'), "rs_trade_00")
    blocks = [both('00')] + rest
    for t, n in enumerate(ffn_names):
        gw[n] = jnp.concatenate([piece for blk in blocks for piece in blk[t]], axis=0).reshape(a[n].shape)
        update(n)
    for n in TRANSPOSED:
        for dst in (gw, delta, new_m, new_v):
            dst[n] = dst[n].swapaxes(-1, -2)

    return (loss, grad_x[None], *[gw[n] for n in WEIGHTS], *[delta[n] for n in WEIGHTS],
            *[new_m[n] for n in WEIGHTS], *[new_v[n] for n in WEIGHTS])
```

```python
import math

import jax
import jax.numpy as jnp
from jax import lax
from jax.experimental import pallas as pl
from jax.experimental.pallas import tpu as pltpu

F32 = jnp.float32
BF16 = jnp.bfloat16

D_MODEL = 1024
D_INNER = 2048
SSM_HEADS = 32
SSM_GROUPS = 4
HEADS_PER_GROUP = SSM_HEADS // SSM_GROUPS
SSM_HEAD_DIM = 64
SSM_STATE = 128
GROUP_DIM = D_INNER // SSM_GROUPS
CONV_DIM = D_INNER + 2 * SSM_GROUPS * SSM_STATE
CONV_WIDTH = 4
CHUNK = 128
ATT_HEAD_DIM = 64
N_Q_HEADS = 16
N_KV_HEADS = 4
Q_PER_KV = N_Q_HEADS // N_KV_HEADS
KV_DIM = N_KV_HEADS * ATT_HEAD_DIM
WINDOW = 128
ROPE_THETA = 10000.0
D_FF = 2816
N_CHIPS = 4
N_CORES = 2
FF_SHARD = D_FF // N_CHIPS
FF_PART = FF_SHARD // N_CORES
IN_PROJ_DIM = D_INNER + CONV_DIM + SSM_HEADS
IN_SHARD = IN_PROJ_DIM // N_CHIPS
IN_SHARD_PAD = 1312
EPS = 1e-5
NEG = -1e30
LANES = 128
VMEM_LIMIT = 56 * 1024 * 1024

ADAM_LR = 0.001
ADAM_B1 = 0.9
ADAM_B2 = 0.999
ADAM_EPS = 1e-08
ADAM_WD = 0.01
ADAM_STEP = 10

NN = ((1,), (0,))
NT = ((1,), (1,))
TN = ((0,), (0,))
MESH = pl.DeviceIdType.MESH
ANY = pl.BlockSpec(memory_space=pl.ANY)


def _dot(a, b, dims=NN, precision=None):
    return lax.dot_general(a, b, (dims, ((), ())), preferred_element_type=F32, precision=precision)


def _cp(n_grid):
    return pltpu.CompilerParams(dimension_semantics=("arbitrary",) * n_grid, vmem_limit_bytes=VMEM_LIMIT)


def _sigmoid(x):
    return 1.0 / (1.0 + jnp.exp(-x))


def _rms_fwd(xf, w):
    r = lax.rsqrt(jnp.mean(xf * xf, axis=-1, keepdims=True) + EPS)
    return xf * r * w


def _rms_bwd(dh, xf, w):
    r = lax.rsqrt(jnp.mean(xf * xf, axis=-1, keepdims=True) + EPS)
    xhat = xf * r
    dxhat = dh * w
    dx = r * (dxhat - xhat * jnp.mean(dxhat * xhat, axis=-1, keepdims=True))
    return dx, dh * xhat


def _row_tile(s, pref):
    return pref if s % pref == 0 else s


def _col_tile(n):
    for t in (1024, 768, 512, 256, 128):
        if n % t == 0:
            return t
    return n


def _sds(shape, dtype):
    return jax.ShapeDtypeStruct(tuple(shape), dtype)


class Exchange:
    def __init__(self, ins, out_shapes, sems, start, finish, inplace=False):
        self.ins, self.out_shapes, self.sems, self.start, self.finish = ins, out_shapes, sems, start, finish
        self.inplace = inplace


def _place():
    x, y, c = lax.axis_index("x"), lax.axis_index("y"), lax.axis_index("c")
    others = [(1 - x, y), (x, 1 - y), (1 - x, 1 - y)]
    return x, y, c, 2 * x + y, others


def _rc(src, dst, send_sem, recv_sem, dev):
    return pltpu.make_async_remote_copy(src_ref=src, dst_ref=dst, send_sem=send_sem, recv_sem=recv_sem,
                                        device_id=dev, device_id_type=MESH)


def swap_cores(arrs, pick_other):
    n = len(arrs)

    def copies(ins, outs, sems):
        send, recv = sems
        x, y, c, _, _ = _place()
        return [_rc(ins[a].at[1 - c] if pick_other else ins[a], outs[a], send.at[a], recv.at[a], (x, y, 1 - c))
                for a in range(n)]

    def start(ins, outs, sems):
        for cp in copies(ins, outs, sems):
            cp.start()

    def finish(ins, outs, sems):
        for cp in copies(ins, outs, sems):
            cp.wait()

    shapes = [_sds(a.shape[1:] if pick_other else a.shape, a.dtype) for a in arrs]
    return Exchange(list(arrs), shapes, [pltpu.SemaphoreType.DMA((n,)), pltpu.SemaphoreType.DMA((n,))],
                    start, finish)


def join(*parts):
    parts = [p for p in parts if p is not None]
    if not parts:
        return None

    def split(refs, counts):
        out, pos = [], 0
        for cnt in counts:
            out.append(refs[pos:pos + cnt])
            pos += cnt
        return out

    n_in = [len(p.ins) for p in parts]
    n_out = [len(p.out_shapes) for p in parts]
    n_sem = [len(p.sems) for p in parts]

    def run(which):
        def go(ins, outs, sems):
            for p, i, o, s in zip(parts, split(ins, n_in), split(outs, n_out), split(sems, n_sem)):
                getattr(p, which)(i, o, s)
        return go

    return Exchange([a for p in parts for a in p.ins], [s for p in parts for s in p.out_shapes],
                    [s for p in parts for s in p.sems], run("start"), run("finish"))


def _pcall(body, *, out_shape, grid, in_specs, out_specs, args, name, scratch_shapes=(), ride=None, aliases=None):
    out_shape, out_specs, in_specs = tuple(out_shape), tuple(out_specs), list(in_specs)
    aliases = aliases or {}
    if ride is None:
        return pl.pallas_call(body, out_shape=out_shape, grid=grid, in_specs=in_specs, out_specs=out_specs,
                              scratch_shapes=list(scratch_shapes), input_output_aliases=aliases, name=name,
                              compiler_params=_cp(len(grid)))(*args)
    n_in, n_out, n_sc = len(args), len(out_shape), len(scratch_shapes)
    n_xi, n_xo = len(ride.ins), len(ride.out_shapes)

    def wrapped(*refs):
        pos = [0]

        def take(cnt):
            got = refs[pos[0]:pos[0] + cnt]
            pos[0] += cnt
            return got

        c_in, x_in, c_out, x_out, c_sc = take(n_in), take(n_xi), take(n_out), take(n_xo), take(n_sc)
        sems = refs[pos[0]:]
        first, last = True, True
        for d, size in enumerate(grid):
            first = jnp.logical_and(first, pl.program_id(d) == 0)
            last = jnp.logical_and(last, pl.program_id(d) == size - 1)

        @pl.when(first)
        def _():
            ride.start(x_in, x_out, sems)

        body(*c_in, *c_out, *c_sc)

        @pl.when(last)
        def _():
            ride.finish(x_in, x_out, sems)

    if ride.inplace:
        aliases = {**aliases, **{n_in + t: n_out + t for t in range(n_xi)}}
    res = pl.pallas_call(
        wrapped, out_shape=out_shape + tuple(ride.out_shapes), grid=grid,
        in_specs=in_specs + [ANY] * n_xi, out_specs=out_specs + (ANY,) * n_xo,
        scratch_shapes=list(scratch_shapes) + list(ride.sems), input_output_aliases=aliases, name=name,
        compiler_params=_cp(len(grid)))(*args, *ride.ins)
    return res[:n_out], res[n_out:]


def run_exchange(ex, name):
    n_xi, n_xo = len(ex.ins), len(ex.out_shapes)

    def body(*refs):
        ins, outs, sems = refs[:n_xi], refs[n_xi:n_xi + n_xo], refs[n_xi + n_xo:]
        ex.start(ins, outs, sems)
        ex.finish(ins, outs, sems)

    aliases = {t: t for t in range(n_xi)} if ex.inplace else {}
    return pl.pallas_call(body, out_shape=tuple(ex.out_shapes), in_specs=[ANY] * n_xi, out_specs=(ANY,) * n_xo,
                          scratch_shapes=list(ex.sems), input_output_aliases=aliases, name=name)(*ex.ins)


HBM_SPEC = pl.BlockSpec(memory_space=pltpu.HBM)
SEM_SPEC = pl.BlockSpec(memory_space=pltpu.SEMAPHORE)
EFFECT = pltpu.SideEffectType.DATAFLOW_SIDE_EFFECTING


def _route(kind, src, dst, c, k, peer):
    if kind == "gather":
        return src.at[c], dst.at[c, k], dst.at[c, peer]
    return src.at[peer], dst.at[k], dst.at[peer]


def split_start(batches, kind, name):
    flat = [a for batch in batches for a in batch]
    n, nb = len(flat), len(batches)
    lands = [lax.empty((2, N_CHIPS) + a.shape[1:] if kind == "gather" else a.shape, a.dtype) for a in flat]

    def body(*refs):
        srcs, dsts, sems, token = refs[:n], refs[n:2 * n], refs[2 * n:2 * n + 2 * nb], refs[-1]
        x, y, c, k, others = _place()
        pos = 0
        for b, batch in enumerate(batches):
            for a in range(len(batch)):
                for j, (px, py) in enumerate(others):
                    src, dst, _ = _route(kind, srcs[pos], dsts[pos], c, k, 2 * px + py)
                    _rc(src, dst, sems[2 * b].at[3 * a + j], sems[2 * b + 1].at[3 * a + j], (px, py, c)).start()
                pos += 1
        token[...] = jnp.zeros(token.shape, token.dtype)

    sem_shapes = [pltpu.SemaphoreType.DMA((3 * len(batch),)) for batch in batches for _ in range(2)]
    thru = [pltpu.HBM(a.shape, a.dtype) for a in flat] + [pltpu.HBM(l.shape, l.dtype) for l in lands]
    res = pl.pallas_call(
        body, name=name, out_shape=tuple(sem_shapes + thru + [_sds((8, LANES), F32)]),
        in_specs=[HBM_SPEC] * (2 * n),
        out_specs=tuple([SEM_SPEC] * (2 * nb) + [HBM_SPEC] * (2 * n) + [pl.BlockSpec(memory_space=pltpu.VMEM)]),
        input_output_aliases={t: 2 * nb + t for t in range(2 * n)},
        compiler_params=pltpu.CompilerParams(has_side_effects=EFFECT),
    )(*[pltpu.with_memory_space_constraint(t, pltpu.HBM) for t in flat + lands])
    sems, srcs, dsts = res[:2 * nb], res[2 * nb:2 * nb + n], res[2 * nb + n:2 * nb + 2 * n]
    out, pos = [], 0
    for b, batch in enumerate(batches):
        out.append((sems[2 * b], sems[2 * b + 1], list(srcs[pos:pos + len(batch)]), list(dsts[pos:pos + len(batch)])))
        pos += len(batch)
    return out, res[-1]


def split_arrive(handle, kind, after, name):
    send, recv, srcs, dsts = handle
    n = len(srcs)

    def body(*refs):
        s_refs, d_refs, send_ref, recv_ref = refs[:n], refs[n:2 * n], refs[2 * n], refs[2 * n + 1]
        x, y, c, k, others = _place()
        for a in range(n):
            for j, (px, py) in enumerate(others):
                src, _, landed = _route(kind, s_refs[a], d_refs[a], c, k, 2 * px + py)
                cp = _rc(src, landed, send_ref.at[3 * a + j], recv_ref.at[3 * a + j], (px, py, c))
                cp.wait_send()
                cp.wait_recv()

    res = pl.pallas_call(
        body, name=name, out_shape=tuple([pltpu.HBM(t.shape, t.dtype) for t in srcs + dsts]),
        in_specs=[HBM_SPEC] * (2 * n) + [SEM_SPEC, SEM_SPEC, ANY], out_specs=tuple([HBM_SPEC] * (2 * n)),
        input_output_aliases={t: t for t in range(2 * n)},
        compiler_params=pltpu.CompilerParams(has_side_effects=EFFECT),
    )(*srcs, *dsts, send, recv, after)
    return list(res[n:])


def forward_cores(bufs):
    n = len(bufs)

    def copies(outs, sems):
        send, recv = sems
        x, y, c, k, others = _place()
        onward, land = [], []
        for a in range(n):
            for j, (px, py) in enumerate(others):
                blk = outs[a].at[c, 2 * px + py]
                onward.append(_rc(blk, blk, send.at[a, j], recv.at[a, j], (x, y, 1 - c)))
                blk2 = outs[a].at[1 - c, 2 * px + py]
                land.append(_rc(blk2, blk2, send.at[a, j], recv.at[a, j], (x, y, 1 - c)))
        return onward, land

    def start(ins, outs, sems):
        for cp in copies(outs, sems)[0]:
            cp.start()

    def finish(ins, outs, sems):
        onward, land = copies(outs, sems)
        for arrived in land:
            arrived.wait_recv()
        for cp in onward:
            cp.wait_send()

    return Exchange(list(bufs), [_sds(b.shape, b.dtype) for b in bufs],
                    [pltpu.SemaphoreType.DMA((n, 3)), pltpu.SemaphoreType.DMA((n, 3))], start, finish, inplace=True)


def all_reduce_small(buf, name):
    r = buf.shape[0]
    n_dev = 8

    def body(in_ref, o_ref, land, send_sems, recv_sems):
        x, y, c, _, _ = _place()
        me = 4 * x + 2 * y + c
        land[me] = in_ref[...]
        sends = []
        for d in range(1, n_dev):
            peer = (x ^ (d >> 2), y ^ ((d >> 1) & 1), c ^ (d & 1))
            cp = _rc(in_ref, land.at[me], send_sems.at[d], recv_sems.at[d], peer)
            cp.start()
            sends.append(cp)
        for d in range(1, n_dev):
            blk = land.at[me ^ d]
            _rc(blk, blk, send_sems.at[d], recv_sems.at[d], (x, y, c)).wait_recv()
        for cp in sends:
            cp.wait_send()
        tot = land[0]
        for d in range(1, n_dev):
            tot = tot + land[d]
        o_ref[...] = tot

    vm = pl.BlockSpec(memory_space=pltpu.VMEM)
    return pl.pallas_call(
        body, out_shape=_sds(buf.shape, F32), in_specs=[vm], out_specs=vm,
        scratch_shapes=[pltpu.VMEM((n_dev, r, LANES), F32), pltpu.SemaphoreType.DMA((n_dev,)),
                        pltpu.SemaphoreType.DMA((n_dev,))],
        name=name)(buf)


def rmsnorm_fwd(x, w, name):
    s, d = x.shape
    tm = _row_tile(s, 512)

    def body(x_ref, w_ref, o_ref):
        o_ref[...] = _rms_fwd(x_ref[...], w_ref[...]).astype(BF16)

    return _pcall(body, out_shape=[_sds((s, d), BF16)], grid=(s // tm,),
                  in_specs=[pl.BlockSpec((tm, d), lambda i: (i, 0)), pl.BlockSpec((1, d), lambda i: (0, 0))],
                  out_specs=[pl.BlockSpec((tm, d), lambda i: (i, 0))], args=[x, w.reshape(1, d)], name=name)[0]


def _ffn_w_spec(chip_of, single=False):
    mode = dict(pipeline_mode=pl.Buffered(1)) if single else {}
    return pl.BlockSpec((N_CORES, 1, FF_PART, D_MODEL), lambda *ids: (0, chip_of(*ids), 0, 0), **mode)


def ffn_fwd(h, x, wg, wu, wd, norm_ws, name, ride=None):
    s, d = h.shape
    n_norm = len(norm_ws)
    tm = _row_tile(s, 1024)

    def body(*refs):
        h_ref, x_ref, wg_ref, wu_ref, wd_ref = refs[:5]
        nw_refs = refs[5:5 + n_norm]
        o_ref = refs[5 + n_norm]
        h_refs = refs[6 + n_norm:6 + 2 * n_norm]
        gu_ref, acc = refs[6 + 2 * n_norm], refs[7 + 2 * n_norm]
        k = pl.program_id(1)

        @pl.when(k == 0)
        def _():
            acc[...] = jnp.zeros(acc.shape, F32)

        hm = tm // 2
        for part in range(2):
            sub = pl.ds(part * hm, hm)
            hb = h_ref[sub, :]
            g = _dot(hb, wg_ref[...].reshape(FF_SHARD, d), NT)
            u = _dot(hb, wu_ref[...].reshape(FF_SHARD, d), NT)
            gu_ref[0, 0, sub, :] = g.astype(BF16)
            gu_ref[0, 1, sub, :] = u.astype(BF16)
            acc[sub, :] += _dot((g * _sigmoid(g) * u).astype(BF16), wd_ref[...].reshape(FF_SHARD, d))

        @pl.when(k == N_CHIPS - 1)
        def _():
            xn = x_ref[...] + 0.5 * acc[...]
            o_ref[...] = xn
            for nw_ref, hn_ref in zip(nw_refs, h_refs):
                hn_ref[...] = _rms_fwd(xn, nw_ref[...]).astype(BF16)

    row = pl.BlockSpec((tm, d), lambda i, k: (i, 0))
    vec = pl.BlockSpec((1, d), lambda i, k: (0, 0))
    wsp = _ffn_w_spec(lambda i, k: k)
    return _pcall(
        body, out_shape=[_sds((s, d), F32)] + [_sds((s, d), BF16)] * n_norm + [_sds((N_CHIPS, 2, s, FF_SHARD), BF16)],
        grid=(s // tm, N_CHIPS),
        in_specs=[row, row, wsp, wsp, wsp] + [vec] * n_norm,
        out_specs=[row] * (1 + n_norm) + [pl.BlockSpec((1, 2, tm, FF_SHARD), lambda i, k: (k, 0, i, 0))],
        scratch_shapes=[pltpu.VMEM((tm, d), F32)],
        args=[h, x, wg, wu, wd] + [nw.reshape(1, d) for nw in norm_ws], name=name, ride=ride)


def ffn_bwd(dxn, h, x_in, nw, gu, wg, wu, wd, name, ride=None):
    s, d = h.shape
    tm = _row_tile(s, 512)
    ni = s // tm
    last_e = N_CHIPS - 1

    def body(dxn_ref, h_ref, x_ref, nw_ref, gu_ref, wg_ref, wu_ref, wd_ref,
             dx_ref, dnw_ref, dwg_ref, dwu_ref, dwd_ref, dh, wacc):
        e = pl.program_id(0)
        i = pl.program_id(1)
        rows = pl.ds(pl.multiple_of(i * tm, tm), tm)

        @pl.when(i == 0)
        def _():
            wacc[...] = jnp.zeros(wacc.shape, F32)

        @pl.when(e == 0)
        def _():
            dh[rows, :] = jnp.zeros((tm, d), F32)

        hm = tm // 2
        for part in range(2):
            sub = pl.ds(part * hm, hm)
            dxb = dxn_ref[sub, :].astype(BF16)
            hb = h_ref[sub, :]
            g = gu_ref[0, 0, sub, :].astype(F32)
            u = gu_ref[0, 1, sub, :].astype(F32)
            drows = pl.ds(pl.multiple_of(i * tm + part * hm, hm), hm)
            sg = _sigmoid(g)
            silu = g * sg
            wacc[2] += _dot((0.5 * silu * u).astype(BF16), dxb, TN)
            da = 0.5 * _dot(dxb, wd_ref[...].reshape(FF_SHARD, d), NT)
            dg = (da * u * (sg * (1.0 + g * (1.0 - sg)))).astype(BF16)
            wacc[0] += _dot(dg, hb, TN)
            du = (da * silu).astype(BF16)
            dh[drows, :] += _dot(dg, wg_ref[...].reshape(FF_SHARD, d))
            wacc[1] += _dot(du, hb, TN)
            dh[drows, :] += _dot(du, wu_ref[...].reshape(FF_SHARD, d))

        @pl.when(i == ni - 1)
        def _():
            for t, dw_ref in enumerate((dwg_ref, dwu_ref, dwd_ref)):
                dw_ref[...] = wacc[t].astype(BF16).reshape(N_CORES, 1, FF_PART, d)

        @pl.when(e == last_e)
        def _():
            dx, dnw = _rms_bwd(dh[rows, :], x_ref[...], nw_ref[...])
            dx_ref[...] = dxn_ref[...] + dx
            col = jnp.sum(dnw, axis=0, keepdims=True)

            @pl.when(i == 0)
            def _():
                dnw_ref[...] = col

            @pl.when(i > 0)
            def _():
                dnw_ref[...] += col

    row = pl.BlockSpec((tm, d), lambda e, i: (i, 0))
    late = pl.BlockSpec((tm, d), lambda e, i: (jnp.where(e == last_e, i, 0), 0))
    vec = pl.BlockSpec((1, d), lambda e, i: (0, 0))
    wsp = _ffn_w_spec(lambda e, i: e, single=True)
    dwsp = _ffn_w_spec(lambda e, i: e, single=True)
    dw = _sds((N_CORES, N_CHIPS, FF_PART, d), BF16)
    return _pcall(
        body, out_shape=[_sds((s, d), F32), _sds((1, d), F32), dw, dw, dw],
        grid=(N_CHIPS, ni),
        in_specs=[row, row, late, vec, pl.BlockSpec((1, 2, tm, FF_SHARD), lambda e, i: (e, 0, i, 0)), wsp, wsp, wsp],
        out_specs=[late, vec, dwsp, dwsp, dwsp],
        scratch_shapes=[pltpu.VMEM((s, d), F32), pltpu.VMEM((3, FF_SHARD, d), F32)],
        args=[dxn, h, x_in, nw.reshape(1, d), gu, wg, wu, wd], name=name, ride=ride)


def mm_res(a, w, x, name, bias=None, norm_ws=(), ride=None):
    s, k = a.shape
    n = w.shape[1]
    tm = _row_tile(s, 256)
    has_bias = bias is not None
    n_norm = len(norm_ws)

    def body(*refs):
        a_ref, w_ref, x_ref = refs[:3]
        pos = 3
        t = _dot(a_ref[...], w_ref[...])
        if has_bias:
            t = t + refs[pos][...]
            pos += 1
        nw_refs = refs[pos:pos + n_norm]
        o_ref = refs[pos + n_norm]
        h_refs = refs[pos + n_norm + 1:]
        xn = x_ref[...] + t
        o_ref[...] = xn
        for nw_ref, h_ref in zip(nw_refs, h_refs):
            h_ref[...] = _rms_fwd(xn, nw_ref[...]).astype(BF16)

    row = pl.BlockSpec((tm, n), lambda i: (i, 0))
    vec = pl.BlockSpec((1, n), lambda i: (0, 0))
    in_specs = [pl.BlockSpec((tm, k), lambda i: (i, 0)), pl.BlockSpec((k, n), lambda i: (0, 0)), row]
    args = [a, w, x]
    if has_bias:
        in_specs.append(vec)
        args.append(bias.reshape(1, n))
    for nw in norm_ws:
        in_specs.append(vec)
        args.append(nw.reshape(1, n))
    return _pcall(body, out_shape=[_sds((s, n), F32)] + [_sds((s, n), BF16)] * n_norm, grid=(s // tm,),
                  in_specs=in_specs, out_specs=[row] * (1 + n_norm), args=args, name=name, ride=ride)


def mm_nn(a, w, name, bias=None, out_dtype=F32):
    s, k = a.shape
    n = w.shape[1]
    tm = _row_tile(s, 512)
    tn = _col_tile(n)
    has_bias = bias is not None

    def body(*refs):
        a_ref, w_ref = refs[:2]
        o_ref = refs[-1]
        t = _dot(a_ref[...], w_ref[...])
        if has_bias:
            t = t + refs[2][...]
        o_ref[...] = t.astype(out_dtype)

    in_specs = [pl.BlockSpec((tm, k), lambda j, i: (i, 0)), pl.BlockSpec((k, tn), lambda j, i: (0, j))]
    args = [a, w]
    if has_bias:
        in_specs.append(pl.BlockSpec((1, tn), lambda j, i: (0, j)))
        args.append(bias.reshape(1, n))
    return _pcall(body, out_shape=[_sds((s, n), out_dtype)], grid=(n // tn, s // tm), in_specs=in_specs,
                  out_specs=[pl.BlockSpec((tm, tn), lambda j, i: (i, j))], args=args, name=name)[0]


def mm_nt(a, w, name, n=None, row0=0, out_dtype=F32, ride=None):
    s, k = a.shape
    n = w.shape[0] if n is None else n
    tm = _row_tile(s, 512)
    tn = _col_tile(n)
    base = row0 // tn
    assert row0 % tn == 0

    def body(a_ref, w_ref, o_ref):
        o_ref[...] = _dot(a_ref[...].astype(BF16), w_ref[...], NT).astype(out_dtype)

    res = _pcall(body, out_shape=[_sds((s, n), out_dtype)], grid=(n // tn, s // tm),
                 in_specs=[pl.BlockSpec((tm, k), lambda j, i: (i, 0)), pl.BlockSpec((tn, k), lambda j, i: (base + j, 0))],
                 out_specs=[pl.BlockSpec((tm, tn), lambda j, i: (i, j))], args=[a, w], name=name, ride=ride)
    return res[0] if ride is None else (res[0][0], res[1])


def mm_tn(a, b, name, into=None, rows=None, row0=0, m_valid=None, col_sum=False, ride=None):
    s, m = a.shape
    n = b.shape[1]
    mv = m if m_valid is None else m_valid
    tm = _col_tile(m) if m_valid is None else mv
    tn = n if n <= 1024 else _col_tile(n)
    rows = mv if rows is None else rows
    assert row0 % tm == 0 and (m_valid is None or m == LANES)
    assert not col_sum or mv == tm
    base = row0 // tm
    ta = m if m_valid is not None else tm

    def body(*refs):
        a_ref, b_ref = refs[0], refs[1]
        o_ref = refs[-2] if col_sum else refs[-1]
        bf = b_ref[...]
        t = _dot(a_ref[...].astype(BF16), bf.astype(BF16), TN)
        o_ref[...] = t[:tm].astype(BF16)
        if col_sum:
            refs[-1][...] = jnp.sum(bf.astype(F32), axis=0, keepdims=True)

    in_specs = [pl.BlockSpec((s, ta), lambda i, j: (0, i)), pl.BlockSpec((s, tn), lambda i, j: (0, j))]
    args = [a, b]
    aliases = None
    if into is not None:
        in_specs.append(ANY)
        args.append(into)
        aliases = {2: 0}
    out_shape = [_sds((rows, n), BF16)]
    out_specs = [pl.BlockSpec((tm, tn), lambda i, j: (base + i, j))]
    if col_sum:
        out_shape.append(_sds((1, n), F32))
        out_specs.append(pl.BlockSpec((1, tn), lambda i, j: (0, j)))
    res = _pcall(body, out_shape=out_shape, grid=(mv // tm, n // tn), in_specs=in_specs, out_specs=out_specs,
                 args=args, name=name, ride=ride, aliases=aliases)
    outs = res if ride is None else res[0]
    out = (outs[0], outs[1][0]) if col_sum else outs[0]
    return out if ride is None else (out, res[1])


def mm_rms_bwd(terms, dxn, x, nw, name, ride=None):
    s, n = x.shape
    nt_ = len(terms)
    tm = _row_tile(s, 256)
    forms = [t[5] for t in terms]

    def body(*refs):
        dxn_ref, x_ref, nw_ref, dx_ref, dnw_ref = refs[2 * nt_:]
        i = pl.program_id(0)
        dh = None
        for t in range(nt_):
            part = _dot(refs[2 * t][...].astype(BF16), refs[2 * t + 1][...], NN if forms[t] == "nn" else NT)
            dh = part if dh is None else dh + part
        dx, dnw = _rms_bwd(dh, x_ref[...], nw_ref[...])
        dx_ref[...] = dxn_ref[...] + dx
        col = jnp.sum(dnw, axis=0, keepdims=True)

        @pl.when(i == 0)
        def _():
            dnw_ref[...] = col

        @pl.when(i > 0)
        def _():
            dnw_ref[...] += col

    in_specs, args = [], []
    for a, cb, w, rb, kb, form in terms:
        in_specs.append(pl.BlockSpec((tm, kb), lambda i, cb=cb: (i, cb)))
        if form == "nn":
            in_specs.append(pl.BlockSpec((kb, n), lambda i, rb=rb: (rb, 0)))
        else:
            in_specs.append(pl.BlockSpec((n, kb), lambda i, rb=rb: (0, rb)))
        args += [a, w]
    row = pl.BlockSpec((tm, n), lambda i: (i, 0))
    vec = pl.BlockSpec((1, n), lambda i: (0, 0))
    res = _pcall(body, out_shape=[_sds((s, n), F32), _sds((1, n), F32)], grid=(s // tm,),
                 in_specs=in_specs + [row, row, vec], out_specs=[row, vec],
                 args=args + [dxn, x, nw.reshape(1, n)], name=name, ride=ride)
    outs = res if ride is None else res[0]
    out = (outs[0], outs[1][0])
    return out if ride is None else (out, res[1])


def rope_tables(s):
    pos = jnp.arange(s, dtype=F32)
    inv = 1.0 / (ROPE_THETA ** (jnp.arange(0, ATT_HEAD_DIM, 2, dtype=F32) / ATT_HEAD_DIM))
    ang = pos[:, None] * inv[None, :]
    cos = jnp.tile(jnp.cos(ang), (1, 2 * LANES // ATT_HEAD_DIM))
    sin = jnp.tile(jnp.sin(ang), (1, 2 * LANES // ATT_HEAD_DIM))
    return cos, sin


def rope_apply(t, cos, sin, name, inverse=False, scale=1.0, out_dtype=BF16):
    s, n = t.shape
    tm = _row_tile(s, 512)
    half = ATT_HEAD_DIM // 2
    reps = n // LANES

    def body(t_ref, c_ref, s_ref, o_ref):
        tf = t_ref[...].astype(F32)
        c = jnp.tile(c_ref[...], (1, reps))
        sn = jnp.tile(s_ref[...], (1, reps))
        lane = lax.broadcasted_iota(jnp.int32, tf.shape, 1)
        first = (lane & (ATT_HEAD_DIM - 1)) < half
        rot = jnp.where(first, -pltpu.roll(tf, n - half, 1), pltpu.roll(tf, half, 1))
        sign = -1.0 if inverse else 1.0
        o_ref[...] = (scale * (tf * c + sign * rot * sn)).astype(out_dtype)

    tab = pl.BlockSpec((tm, LANES), lambda i: (i, 0))
    return _pcall(body, out_shape=[_sds((s, n), out_dtype)], grid=(s // tm,),
                  in_specs=[pl.BlockSpec((tm, n), lambda i: (i, 0)), tab, tab],
                  out_specs=[pl.BlockSpec((tm, n), lambda i: (i, 0))], args=[t, cos, sin], name=name)[0]


CONV_TILE = 256


def _shift_down(u, k):
    if k == 0:
        return u
    row = lax.broadcasted_iota(jnp.int32, u.shape, 0)
    return jnp.where(row >= k, pltpu.roll(u, k, 0), 0.0)


def _shift_up(u, k):
    if k == 0:
        return u
    s = u.shape[0]
    row = lax.broadcasted_iota(jnp.int32, u.shape, 0)
    return jnp.where(row < s - k, pltpu.roll(u, s - k, 0), 0.0)


def _conv_taps(u):
    return [_shift_down(u, CONV_WIDTH - 1 - k) for k in range(CONV_WIDTH)]


def _conv_pre(taps, w_ref, b_ref):
    pre = b_ref[...] + w_ref[0:1, :] * taps[0]
    for k in range(1, CONV_WIDTH):
        pre += w_ref[k:k + 1, :] * taps[k]
    return pre


def conv_fwd(u, w, b, name, ride=None):
    s, c = u.shape

    def body(u_ref, w_ref, b_ref, o_ref):
        pre = _conv_pre(_conv_taps(u_ref[...]), w_ref, b_ref)
        o_ref[...] = pre * _sigmoid(pre)

    col = pl.BlockSpec((s, CONV_TILE), lambda j: (0, j))
    res = _pcall(body, out_shape=[_sds((s, c), F32)], grid=(c // CONV_TILE,),
                 in_specs=[col, pl.BlockSpec((CONV_WIDTH, CONV_TILE), lambda j: (0, j)),
                           pl.BlockSpec((1, CONV_TILE), lambda j: (0, j))],
                 out_specs=[col], args=[u, w, b.reshape(1, c)], name=name, ride=ride)
    return res[0] if ride is None else (res[0][0], res[1])


def conv_bwd(dxs, db_, dc_, u, w, b, name):
    s, c = u.shape
    n_x = dxs.shape[1] // CONV_TILE
    n_b = db_.shape[1] // CONV_TILE

    def body(dx_ref, dbb_ref, dcc_ref, u_ref, w_ref, b_ref, du_ref, dw_ref, dbias_ref):
        j = pl.program_id(0)
        dact = jnp.where(j < n_x, dx_ref[...], jnp.where(j < n_x + n_b, dbb_ref[...], dcc_ref[...]))
        taps = _conv_taps(u_ref[...])
        pre = _conv_pre(taps, w_ref, b_ref)
        sg = _sigmoid(pre)
        dpre = dact * (sg * (1.0 + pre * (1.0 - sg)))
        du = w_ref[CONV_WIDTH - 1:CONV_WIDTH, :] * dpre
        for k in range(CONV_WIDTH - 1):
            du += w_ref[k:k + 1, :] * _shift_up(dpre, CONV_WIDTH - 1 - k)
        du_ref[...] = du
        dbias_ref[...] = jnp.sum(dpre, axis=0, keepdims=True)
        for k in range(CONV_WIDTH):
            dw_ref[k:k + 1, :] = jnp.sum(dpre * taps[k], axis=0, keepdims=True)

    col = pl.BlockSpec((s, CONV_TILE), lambda j: (0, j))
    wsp = pl.BlockSpec((CONV_WIDTH, CONV_TILE), lambda j: (0, j))
    bsp = pl.BlockSpec((1, CONV_TILE), lambda j: (0, j))
    du, dw, db = _pcall(
        body, out_shape=[_sds((s, c), F32), _sds((CONV_WIDTH, c), F32), _sds((1, c), F32)], grid=(c // CONV_TILE,),
        in_specs=[pl.BlockSpec((s, CONV_TILE), lambda j: (0, jnp.minimum(j, n_x - 1))),
                  pl.BlockSpec((s, CONV_TILE), lambda j: (0, jnp.clip(j - n_x, 0, n_b - 1))),
                  pl.BlockSpec((s, CONV_TILE), lambda j: (0, jnp.clip(j - n_x - n_b, 0, n_b - 1))),
                  col, wsp, bsp],
        out_specs=[col, wsp, bsp], args=[dxs, db_, dc_, u, w, b.reshape(1, c)], name=name)
    return du, dw, db[0]


def _lane_pick(mat, idx):
    lane = lax.broadcasted_iota(jnp.int32, mat.shape, 1)
    return jnp.sum(jnp.where(lane == idx, mat, 0.0), axis=1, keepdims=True)


def _sub_pick(mat, idx):
    sub = lax.broadcasted_iota(jnp.int32, mat.shape, 0)
    return jnp.sum(jnp.where(sub == idx, mat, 0.0), axis=0, keepdims=True)


def _expand_heads(cols):
    rows = cols[0].shape[0]
    left = lax.broadcasted_iota(jnp.int32, (rows, LANES), 1) < SSM_HEAD_DIM
    return jnp.concatenate(
        [jnp.where(left, cols[2 * p], cols[2 * p + 1]) for p in range(HEADS_PER_GROUP // 2)], axis=1)


def _dot_01(x, ones, ones_first, pieces):
    tot, rest = None, x
    for _ in range(pieces):
        piece = rest.astype(BF16)
        rest = rest - piece.astype(F32)
        part = _dot(ones, piece) if ones_first else _dot(piece, ones)
        tot = part if tot is None else tot + part
    return tot


def _heads_to_lanes(mat, g):
    jj = lax.broadcasted_iota(jnp.int32, (GROUP_DIM, LANES), 0)
    ll = lax.broadcasted_iota(jnp.int32, (GROUP_DIM, LANES), 1)
    sel = (ll == HEADS_PER_GROUP * g + (jj >> 6)).astype(BF16)
    return _dot_01(mat, sel, False, 3)


def _softplus(x):
    return jnp.maximum(x, 0.0) + jnp.log1p(jnp.exp(-jnp.abs(x)))


def _ssd_scalars(dt_ref, bias_ref, a_ref, dtall, csall, cst):
    dta = _softplus(dt_ref[...] + bias_ref[...])
    row = lax.broadcasted_iota(jnp.int32, (CHUNK, CHUNK), 0)
    col = lax.broadcasted_iota(jnp.int32, (CHUNK, CHUNK), 1)
    cs = _dot_01(dta * a_ref[...], (row >= col).astype(BF16), True, 3)
    dtall[...] = dta
    csall[...] = cs
    cst[...] = cs.T


def _decay_mat(cs_col, cs_row):
    row = lax.broadcasted_iota(jnp.int32, (CHUNK, CHUNK), 0)
    col = lax.broadcasted_iota(jnp.int32, (CHUNK, CHUNK), 1)
    return jnp.exp(jnp.where(row >= col, cs_col - cs_row, NEG))


def _head_mask(xpair, right):
    lane = lax.broadcasted_iota(jnp.int32, xpair.shape, 1)
    keep = (lane >= SSM_HEAD_DIM) if right else (lane < SSM_HEAD_DIM)
    return jnp.where(keep, xpair, 0.0)


def _chunk_cols(x_all, g):
    return [_lane_pick(x_all, HEADS_PER_GROUP * g + r) for r in range(HEADS_PER_GROUP)]


def _decay_col(cs_cols):
    return jnp.concatenate(
        [jnp.broadcast_to(jnp.exp(cc[CHUNK - 1:CHUNK, :]), (SSM_HEAD_DIM, 1)) for cc in cs_cols], axis=0)


def ssd_fwd(act, z, dtp, bias_p, a_p, d_p, normw, name, ride=None):
    s = act.shape[0]
    nc = s // CHUNK

    def body(xs_all, b_all, c_all, z_all, dt_ref, bias_ref, a_ref, d_ref, nw_all,
             yn_all, y_all, st_all, state, dtall, csall, cst):
        _ssd_scalars(dt_ref, bias_ref, a_ref, dtall, csall, cst)

        @pl.when(pl.program_id(0) == 0)
        def _():
            state[...] = jnp.zeros(state.shape, F32)

        for g in range(SSM_GROUPS):
            wide = pl.ds(g * GROUP_DIM, GROUP_DIM)
            narrow = pl.ds(g * SSM_STATE, SSM_STATE)
            group(g, xs_all.at[:, wide], b_all.at[:, narrow], c_all.at[:, narrow], z_all.at[:, wide], d_ref,
                  nw_all.at[:, wide], yn_all.at[:, wide], y_all.at[:, wide], st_all.at[:, pl.ds(g, 1)],
                  state, dtall, csall, cst)

    def group(g, xs_ref, b_ref, c_ref, z_ref, d_ref, nw_ref, yn_ref, y_ref, st_ref, state, dtall, csall, cst):
        cs_cols = _chunk_cols(csall[...], g)
        dt_cols = _chunk_cols(dtall[...], g)
        cs_rows = [_sub_pick(cst[...], HEADS_PER_GROUP * g + r) for r in range(HEADS_PER_GROUP)]
        d_cols = _chunk_cols(d_ref[...], g)
        cs_exp = _expand_heads(cs_cols)
        dt_exp = _expand_heads(dt_cols)
        d_exp = _expand_heads(d_cols)
        xs = xs_ref[...]
        bb = b_ref[...].astype(BF16)
        cb16 = c_ref[...].astype(BF16)
        xdt = xs * dt_exp
        s_prev = state[g]
        st_ref[0, 0] = s_prev
        y_off = _dot(cb16, s_prev.astype(BF16), NT) * jnp.exp(cs_exp)
        decay_st = jnp.exp(cs_exp[CHUNK - 1:CHUNK, :] - cs_exp)
        contrib = _dot((xdt * decay_st).astype(BF16), bb, TN)
        state[g] = _decay_col(cs_cols) * s_prev + contrib
        cbm = _dot(cb16, bb, NT)
        pairs = []
        for p in range(HEADS_PER_GROUP // 2):
            xpair = xdt[:, LANES * p:LANES * (p + 1)]
            m0 = (cbm * _decay_mat(cs_cols[2 * p], cs_rows[2 * p])).astype(BF16)
            m1 = (cbm * _decay_mat(cs_cols[2 * p + 1], cs_rows[2 * p + 1])).astype(BF16)
            pairs.append(_dot(m0, _head_mask(xpair, False).astype(BF16))
                         + _dot(m1, _head_mask(xpair, True).astype(BF16)))
        y = jnp.concatenate(pairs, axis=1) + y_off + xs * d_exp
        y_ref[...] = y
        zf = z_ref[...]
        yg = y * (zf * _sigmoid(zf))
        yn_ref[...] = _rms_fwd(yg, nw_ref[...]).astype(BF16)

    gn = SSM_GROUPS * SSM_STATE
    wide = pl.BlockSpec((CHUNK, D_INNER), lambda c: (c, 0))
    par = pl.BlockSpec((1, LANES), lambda c: (0, 0))
    return _pcall(
        body,
        out_shape=[_sds((s, D_INNER), BF16), _sds((s, D_INNER), F32),
                   _sds((nc, SSM_GROUPS, GROUP_DIM, SSM_STATE), F32)],
        grid=(nc,),
        in_specs=[wide,
                  pl.BlockSpec((CHUNK, gn), lambda c: (c, D_INNER // gn)),
                  pl.BlockSpec((CHUNK, gn), lambda c: (c, D_INNER // gn + 1)),
                  wide,
                  pl.BlockSpec((CHUNK, LANES), lambda c: (c, 0)),
                  par, par, par,
                  pl.BlockSpec((1, D_INNER), lambda c: (0, 0))],
        out_specs=[wide, wide, pl.BlockSpec((1, SSM_GROUPS, GROUP_DIM, SSM_STATE), lambda c: (c, 0, 0, 0))],
        scratch_shapes=[pltpu.VMEM((SSM_GROUPS, GROUP_DIM, SSM_STATE), F32),
                        pltpu.VMEM((CHUNK, LANES), F32), pltpu.VMEM((CHUNK, LANES), F32),
                        pltpu.VMEM((LANES, CHUNK), F32)],
        args=[act, act, act, z, dtp, bias_p, a_p, d_p, normw], name=name, ride=ride)


def ssd_bwd(dyn, act, z, y_pre, states, dtp, bias_p, a_p, d_p, normw, name, ride=None):
    s = act.shape[0]
    nc = s // CHUNK

    def body(dyn_all, xs_all, b_all, c_all, z_all, y_all, st_all, dt_ref, bias_ref, a_ref, d_ref, nw_all,
             dxs_all, db_all, dc_all, dz_all, ddt_ref, dnw_ref, dbias_ref, da_ref, dd_ref,
             dstate, dtall, csall, cst):
        _ssd_scalars(dt_ref, bias_ref, a_ref, dtall, csall, cst)
        ddt_ref[...] = jnp.zeros((CHUNK, LANES), F32)

        @pl.when(pl.program_id(0) == 0)
        def _():
            dstate[...] = jnp.zeros(dstate.shape, F32)
            dnw_ref[...] = jnp.zeros(dnw_ref.shape, F32)
            dbias_ref[...] = jnp.zeros((1, LANES), F32)
            da_ref[...] = jnp.zeros((1, LANES), F32)
            dd_ref[...] = jnp.zeros((1, LANES), F32)

        for g in range(SSM_GROUPS):
            wide = pl.ds(g * GROUP_DIM, GROUP_DIM)
            narrow = pl.ds(g * SSM_STATE, SSM_STATE)
            group(g, dyn_all.at[:, wide], xs_all.at[:, wide], b_all.at[:, narrow], c_all.at[:, narrow],
                  z_all.at[:, wide], y_all.at[:, wide], st_all.at[:, pl.ds(g, 1)], dt_ref, bias_ref, a_ref, d_ref,
                  nw_all.at[:, wide], dxs_all.at[:, wide], db_all.at[:, narrow], dc_all.at[:, narrow],
                  dz_all.at[:, wide], ddt_ref, dnw_ref, dbias_ref, da_ref, dd_ref, dstate, dtall, csall, cst)

    def group(g, dyn_ref, xs_ref, b_ref, c_ref, z_ref, y_ref, st_ref, dt_ref, bias_ref, a_ref, d_ref, nw_ref,
              dxs_ref, db_ref, dc_ref, dz_ref, ddt_ref, dnw_ref, dbias_ref, da_ref, dd_ref,
              dstate, dtall, csall, cst):
        cs_cols = _chunk_cols(csall[...], g)
        dt_cols = _chunk_cols(dtall[...], g)
        cs_rows = [_sub_pick(cst[...], HEADS_PER_GROUP * g + r) for r in range(HEADS_PER_GROUP)]
        d_cols = _chunk_cols(d_ref[...], g)
        cs_exp = _expand_heads(cs_cols)
        dt_exp = _expand_heads(dt_cols)
        d_exp = _expand_heads(d_cols)
        xs = xs_ref[...]
        bb = b_ref[...].astype(BF16)
        cb16 = c_ref[...].astype(BF16)
        xdt = xs * dt_exp
        s_prev = st_ref[0, 0]
        s_prev16 = s_prev.astype(BF16)
        ds_next = dstate[g]
        ds16 = ds_next.astype(BF16)

        zf = z_ref[...]
        sz = _sigmoid(zf)
        silu_z = zf * sz
        y = y_ref[...]
        yg = y * silu_z
        dout = dyn_ref[...]
        dyg, dnw = _rms_bwd(dout, yg, nw_ref[...])
        dnw_ref[pl.ds(g, 1), :] += jnp.sum(dnw, axis=0, keepdims=True)
        dy = dyg * silu_z
        dz_ref[...] = dyg * y * (sz * (1.0 + zf * (1.0 - sz)))
        dd_ref[...] += jnp.sum(_heads_to_lanes(dy * xs, g), axis=0, keepdims=True)

        exp_cs = jnp.exp(cs_exp)
        decay_st = jnp.exp(cs_exp[CHUNK - 1:CHUNK, :] - cs_exp)
        cs_t = _dot(cb16, s_prev16, NT)
        dyo = dy * exp_cs
        dc_acc = _dot(dyo.astype(BF16), s_prev16, NN)
        g1 = _dot(bb, ds16, NT)
        xds = xdt * decay_st
        db_acc = _dot(xds.astype(BF16), ds16, NN)
        dxdt_off = g1 * decay_st
        t_exp = g1 * xds
        dcs_exp = dy * cs_t * exp_cs - t_exp
        decay_c = _decay_col(cs_cols)
        dstate[g] = decay_c * ds_next + _dot(dyo.astype(BF16), cb16, TN)
        dlast_col = jnp.sum(ds_next * s_prev, axis=1, keepdims=True) * decay_c
        jj = lax.broadcasted_iota(jnp.int32, (GROUP_DIM, LANES), 0)
        ll = lax.broadcasted_iota(jnp.int32, (GROUP_DIM, LANES), 1)
        sel = ll == HEADS_PER_GROUP * g + (jj >> 6)
        dlast = jnp.sum(jnp.where(sel, dlast_col, 0.0), axis=0, keepdims=True)
        t_all = _heads_to_lanes(t_exp, g)
        dlast += jnp.sum(t_all, axis=0, keepdims=True)
        dcs_all = _heads_to_lanes(dcs_exp, g)

        cbm = _dot(cb16, bb, NT)
        dcb = jnp.zeros((CHUNK, CHUNK), F32)
        dcs_rows = jnp.zeros((LANES, CHUNK), F32)
        lane_l = lax.broadcasted_iota(jnp.int32, (CHUNK, LANES), 1)
        sub_l = lax.broadcasted_iota(jnp.int32, (LANES, CHUNK), 0)
        dxdt_pairs = []
        for p in range(HEADS_PER_GROUP // 2):
            xpair16 = xdt[:, LANES * p:LANES * (p + 1)].astype(BF16)
            dypair = dy[:, LANES * p:LANES * (p + 1)]
            acc = None
            for r in (2 * p, 2 * p + 1):
                lm = _decay_mat(cs_cols[r], cs_rows[r])
                m = cbm * lm
                dyh = _head_mask(dypair, r % 2 == 1).astype(BF16)
                dm = _dot(dyh, xpair16, NT)
                dcb += dm * lm
                q = dm * m
                idx = HEADS_PER_GROUP * g + r
                dcs_all += jnp.where(lane_l == idx, jnp.sum(q, axis=1, keepdims=True), 0.0)
                dcs_rows -= jnp.where(sub_l == idx, jnp.sum(q, axis=0, keepdims=True), 0.0)
                part = _dot(m.astype(BF16), dyh, TN)
                acc = part if acc is None else acc + part
            dxdt_pairs.append(acc)
        dxdt = jnp.concatenate(dxdt_pairs, axis=1) + dxdt_off
        dcb16 = dcb.astype(BF16)
        dc_ref[...] = dc_acc + _dot(dcb16, bb, NN)
        db_ref[...] = db_acc + _dot(dcb16, cb16, TN)
        dxs_ref[...] = dxdt * dt_exp + dy * d_exp

        dcs_all += dcs_rows.T
        row = lax.broadcasted_iota(jnp.int32, (CHUNK, CHUNK), 0)
        col = lax.broadcasted_iota(jnp.int32, (CHUNK, CHUNK), 1)
        last_row = lax.broadcasted_iota(jnp.int32, (CHUNK, LANES), 0) == CHUNK - 1
        dcs_all += jnp.where(last_row, dlast, 0.0)
        da_all = _dot_01(dcs_all, (col >= row).astype(BF16), True, 3)
        dta = dtall[...]
        in_group = jnp.logical_and(lane_l >= HEADS_PER_GROUP * g, lane_l < HEADS_PER_GROUP * (g + 1))
        ddt = jnp.where(in_group, da_all * a_ref[...] + _heads_to_lanes(dxdt * xs, g), 0.0)
        da_ref[...] += jnp.sum(jnp.where(in_group, da_all * dta, 0.0), axis=0, keepdims=True)
        ddt_raw = ddt * _sigmoid(dt_ref[...] + bias_ref[...])
        ddt_ref[...] += ddt_raw
        dbias_ref[...] += jnp.sum(ddt_raw, axis=0, keepdims=True)

    gn = SSM_GROUPS * SSM_STATE
    wide = pl.BlockSpec((CHUNK, D_INNER), lambda c: (nc - 1 - c, 0))
    st = pl.BlockSpec((CHUNK, gn), lambda c: (nc - 1 - c, 0))
    par = pl.BlockSpec((1, LANES), lambda c: (0, 0))
    dtb = pl.BlockSpec((CHUNK, LANES), lambda c: (nc - 1 - c, 0))
    f = lambda shape: _sds(shape, F32)
    return _pcall(
        body,
        out_shape=[f((s, D_INNER)), f((s, gn)), f((s, gn)),
                   f((s, D_INNER)), f((s, LANES)), f((8, GROUP_DIM)), f((1, LANES)), f((1, LANES)), f((1, LANES))],
        grid=(nc,),
        in_specs=[wide, wide,
                  pl.BlockSpec((CHUNK, gn), lambda c: (nc - 1 - c, D_INNER // gn)),
                  pl.BlockSpec((CHUNK, gn), lambda c: (nc - 1 - c, D_INNER // gn + 1)),
                  wide, wide,
                  pl.BlockSpec((1, SSM_GROUPS, GROUP_DIM, SSM_STATE), lambda c: (nc - 1 - c, 0, 0, 0)),
                  dtb, par, par, par,
                  pl.BlockSpec((1, D_INNER), lambda c: (0, 0))],
        out_specs=[wide, st, st, wide, dtb, pl.BlockSpec((8, GROUP_DIM), lambda c: (0, 0)), par, par, par],
        scratch_shapes=[pltpu.VMEM((SSM_GROUPS, GROUP_DIM, SSM_STATE), F32),
                        pltpu.VMEM((CHUNK, LANES), F32), pltpu.VMEM((CHUNK, LANES), F32),
                        pltpu.VMEM((LANES, CHUNK), F32)],
        args=[dyn, act, act, act, z, y_pre, states, dtp, bias_p, a_p, d_p, normw], name=name, ride=ride)


def _attn_probs(q, kp, kc, sink, n):
    sp = _dot(q, kp, NT)
    sc = _dot(q, kc, NT)
    i = lax.broadcasted_iota(jnp.int32, sp.shape, 0) & (WINDOW - 1)
    j = lax.broadcasted_iota(jnp.int32, sp.shape, 1)
    sp = jnp.where(jnp.logical_and(j > i, n > 0), sp, NEG)
    sc = jnp.where(j <= i, sc, NEG)
    m = jnp.maximum(jnp.maximum(jnp.max(sp, axis=1, keepdims=True), jnp.max(sc, axis=1, keepdims=True)), sink)
    pp = jnp.exp(sp - m)
    pc = jnp.exp(sc - m)
    ps = jnp.exp(sink - m)
    inv = 1.0 / (jnp.sum(pp, axis=1, keepdims=True) + jnp.sum(pc, axis=1, keepdims=True) + ps)
    return pp * inv, pc * inv, ps * inv


def attn_fwd(qt, kt, vt, sink_rows, name, ride=None):
    s = qt.shape[1]
    nb = s // WINDOW
    rows = Q_PER_KV * WINDOW

    def body(q_ref, kp_ref, kc_ref, vp_ref, vc_ref, sk_ref, o_ref):
        n = pl.program_id(0)
        for h in range(N_KV_HEADS):
            heads = pl.ds(h * Q_PER_KV, Q_PER_KV)
            q = q_ref[heads].reshape(rows, ATT_HEAD_DIM)
            pp, pc, _ = _attn_probs(q, kp_ref[h], kc_ref[h], sk_ref[h], n)
            o = _dot(pp.astype(BF16), vp_ref[h]) + _dot(pc.astype(BF16), vc_ref[h])
            o_ref[heads] = o.reshape(Q_PER_KV, WINDOW, ATT_HEAD_DIM).astype(BF16)

    qsp = pl.BlockSpec((N_Q_HEADS, WINDOW, ATT_HEAD_DIM), lambda n: (0, n, 0))
    prev = pl.BlockSpec((N_KV_HEADS, WINDOW, ATT_HEAD_DIM), lambda n: (0, jnp.maximum(n - 1, 0), 0))
    cur = pl.BlockSpec((N_KV_HEADS, WINDOW, ATT_HEAD_DIM), lambda n: (0, n, 0))
    return _pcall(body, out_shape=[_sds(qt.shape, BF16)], grid=(nb,),
                  in_specs=[qsp, prev, cur, prev, cur, pl.BlockSpec((N_KV_HEADS, rows, 1), lambda n: (0, 0, 0))],
                  out_specs=[qsp], args=[qt, kt, kt, vt, vt, sink_rows], name=name, ride=ride)


def attn_bwd(qt, kt, vt, sink_rows, dot_, name, ride=None):
    s = qt.shape[1]
    nb = s // WINDOW
    rows = Q_PER_KV * WINDOW

    def body(q_ref, kp_ref, kc_ref, vp_ref, vc_ref, sk_ref, do_ref, dq_ref, dk_ref, dv_ref, ds_ref, kacc, vacc):
        n = pl.program_id(0)

        @pl.when(n == 0)
        def _():
            kacc[...] = jnp.zeros(kacc.shape, F32)
            vacc[...] = jnp.zeros(vacc.shape, F32)

        @pl.when(n < nb)
        def _():
            for h in range(N_KV_HEADS):
                heads = pl.ds(h * Q_PER_KV, Q_PER_KV)
                q = q_ref[heads].reshape(rows, ATT_HEAD_DIM)
                do = do_ref[heads].reshape(rows, ATT_HEAD_DIM)
                kp, kc, vp, vc = kp_ref[h], kc_ref[h], vp_ref[h], vc_ref[h]
                pp, pc, ps = _attn_probs(q, kp, kc, sk_ref[h], n)
                dpp = _dot(do, vp, NT)
                dpc = _dot(do, vc, NT)
                delta = jnp.sum(pp * dpp, axis=1, keepdims=True) + jnp.sum(pc * dpc, axis=1, keepdims=True)
                dsp = (pp * (dpp - delta)).astype(BF16)
                dsc = (pc * (dpc - delta)).astype(BF16)
                dq = _dot(dsp, kp) + _dot(dsc, kc)
                dq_ref[heads] = dq.reshape(Q_PER_KV, WINDOW, ATT_HEAD_DIM)
                dk_ref[h] = kacc[h] + _dot(dsp, q, TN)
                dv_ref[h] = vacc[h] + _dot(pp.astype(BF16), do, TN)
                kacc[h] = _dot(dsc, q, TN)
                vacc[h] = _dot(pc.astype(BF16), do, TN)
                dsk = -ps * delta
                sub = lax.broadcasted_iota(jnp.int32, (8, LANES), 0)
                tile = jnp.zeros((8, LANES), F32)
                for j in range(Q_PER_KV):
                    tile += jnp.where(sub == j, jnp.sum(dsk[j * WINDOW:(j + 1) * WINDOW, :], axis=0, keepdims=True),
                                      0.0)
                ds_ref[h, 0] = tile

        @pl.when(n == nb)
        def _():
            dk_ref[...] = kacc[...]
            dv_ref[...] = vacc[...]
            ds_ref[...] = jnp.zeros(ds_ref.shape, F32)

    last = nb - 1
    qsp = pl.BlockSpec((N_Q_HEADS, WINDOW, ATT_HEAD_DIM), lambda n: (0, jnp.minimum(n, last), 0))
    prev = pl.BlockSpec((N_KV_HEADS, WINDOW, ATT_HEAD_DIM), lambda n: (0, jnp.clip(n - 1, 0, last), 0))
    cur = pl.BlockSpec((N_KV_HEADS, WINDOW, ATT_HEAD_DIM), lambda n: (0, jnp.minimum(n, last), 0))
    dkv = pl.BlockSpec((N_KV_HEADS, WINDOW, ATT_HEAD_DIM), lambda n: (0, jnp.maximum(n - 1, 0), 0))
    f = lambda shape: _sds(shape, F32)
    acc = pltpu.VMEM((N_KV_HEADS, WINDOW, ATT_HEAD_DIM), F32)
    return _pcall(
        body, out_shape=[f(qt.shape), f(kt.shape), f(vt.shape), f((N_KV_HEADS, nb + 1, 8, LANES))],
        grid=(nb + 1,),
        in_specs=[qsp, prev, cur, prev, cur, pl.BlockSpec((N_KV_HEADS, rows, 1), lambda n: (0, 0, 0)), qsp],
        out_specs=[qsp, dkv, dkv, pl.BlockSpec((N_KV_HEADS, 1, 8, LANES), lambda n: (0, n, 0, 0))],
        scratch_shapes=[acc, acc], args=[qt, kt, kt, vt, vt, sink_rows, dot_], name=name, ride=ride)


def loss_head(x, w, tgt, name):
    s, d = x.shape
    tm = _row_tile(s, 256)

    def body(x_ref, w_ref, t_ref, loss_ref, dx_ref, dw_ref):
        i = pl.program_id(0)
        xf = x_ref[...]
        wv = w_ref[...]
        r = lax.rsqrt(jnp.mean(xf * xf, axis=-1, keepdims=True) + EPS)
        xhat = xf * r
        e = xhat * wv - t_ref[...]
        part = 0.5 * jnp.sum(jnp.mean(e * e, axis=-1, keepdims=True), axis=0, keepdims=True)
        dy = e * (1.0 / d)
        dxhat = dy * wv
        dx_ref[...] = r * (dxhat - xhat * jnp.mean(dxhat * xhat, axis=-1, keepdims=True))
        col = jnp.sum(dy * xhat, axis=0, keepdims=True)

        @pl.when(i == 0)
        def _():
            loss_ref[...] = jnp.broadcast_to(part, (1, LANES))
            dw_ref[...] = col

        @pl.when(i > 0)
        def _():
            loss_ref[...] += jnp.broadcast_to(part, (1, LANES))
            dw_ref[...] += col

    row = pl.BlockSpec((tm, d), lambda i: (i, 0))
    vec = pl.BlockSpec((1, d), lambda i: (0, 0))
    return _pcall(body, out_shape=[_sds((1, LANES), F32), _sds((s, d), F32), _sds((1, d), F32)], grid=(s // tm,),
                  in_specs=[row, vec, row], out_specs=[pl.BlockSpec((1, LANES), lambda i: (0, 0)), row, vec],
                  args=[x, w.reshape(1, d), tgt], name=name)


ELEMWISE_TILE = 720 * 1024


def _tile_rows(r, c, max_elems=262144, mult=16):
    best = None
    for t in range(mult, r + 1, mult):
        if r % t == 0 and t * c <= max_elems:
            best = t
    return best or r


def add_pair(xhs, ps, c_idx, name):
    n = len(xhs)
    _, r, c = xhs[0].shape
    tr = _tile_rows(r, c, max_elems=ELEMWISE_TILE)

    def body(c_ref, *refs):
        for x_ref, p_ref, o_ref in zip(refs[:n], refs[n:2 * n], refs[2 * n:]):
            o_ref[...] = (x_ref[0].astype(F32) + p_ref[...].astype(F32)).astype(BF16)

    blk = pl.BlockSpec((tr, c), lambda i, cr: (i, 0))
    return pl.pallas_call(
        body, out_shape=tuple([_sds((r, c), BF16)] * n),
        grid_spec=pltpu.PrefetchScalarGridSpec(
            num_scalar_prefetch=1, grid=(r // tr,),
            in_specs=[pl.BlockSpec((1, tr, c), lambda i, cr: (cr[0], i, 0))] * n + [blk] * n,
            out_specs=tuple([blk] * n)),
        name=name, compiler_params=_cp(1))(c_idx, *xhs, *ps)


def sum_chips(qs, owns, chip_idx, name):
    n = len(qs)
    _, r, c = qs[0].shape
    tr = _tile_rows(r, c, max_elems=ELEMWISE_TILE // max(1, n - 1))

    def body(k_ref, *refs):
        k = k_ref[0]
        for q_ref, own_ref, o_ref in zip(refs[:n], refs[n:2 * n], refs[2 * n:]):
            mine = own_ref[0].astype(F32)
            tot = None
            for j in range(N_CHIPS):
                term = jnp.where(k == j, mine, q_ref[j].astype(F32))
                tot = term if tot is None else tot + term
            o_ref[...] = tot

    return pl.pallas_call(
        body, out_shape=tuple([_sds((r, c), F32)] * n),
        grid_spec=pltpu.PrefetchScalarGridSpec(
            num_scalar_prefetch=1, grid=(r // tr,),
            in_specs=([pl.BlockSpec((N_CHIPS, tr, c), lambda i, kr: (0, i, 0))] * n
                      + [pl.BlockSpec((1, tr, c), lambda i, kr: (kr[0], i, 0))] * n),
            out_specs=tuple([pl.BlockSpec((tr, c), lambda i, kr: (i, 0))] * n)),
        name=name, compiler_params=_cp(1))(chip_idx, *qs, *owns)


def adamw(w, g, m, v, name):
    r, c = w.shape
    tr = _tile_rows(r, c, mult=8)
    c1 = 1.0 / (1.0 - ADAM_B1 ** ADAM_STEP)
    c2 = 1.0 / (1.0 - ADAM_B2 ** ADAM_STEP)

    def body(w_ref, g_ref, m_ref, v_ref, d_ref, mo_ref, vo_ref):
        gf = g_ref[...]
        mn = ADAM_B1 * m_ref[...] + (1.0 - ADAM_B1) * gf
        vn = ADAM_B2 * v_ref[...] + (1.0 - ADAM_B2) * (gf * gf)
        mo_ref[...] = mn
        vo_ref[...] = vn
        d_ref[...] = -ADAM_LR * ((mn * c1) / (jnp.sqrt(vn * c2) + ADAM_EPS) + ADAM_WD * w_ref[...])

    blk = pl.BlockSpec((tr, c), lambda i: (i, 0))
    out = _sds((r, c), F32)
    return _pcall(body, out_shape=[out, out, out], grid=(r // tr,), in_specs=[blk] * 4, out_specs=[blk] * 3,
                  args=[w, g, m, v], name=name)


WEIGHTS = ['norm_w', 'ffn_w_gate', 'ffn_w_up', 'ffn_w_down', 'ssm_w_in', 'ssm_conv_w', 'ssm_conv_b', 'ssm_dt_bias',
           'ssm_a_log', 'ssm_d', 'ssm_norm_w', 'ssm_w_out', 'kv_norm_w', 'w_k', 'b_k', 'w_v', 'b_v', 'attn_w_q',
           'attn_b_q', 'attn_sinks', 'attn_w_o', 'attn_b_o', 'final_norm_w']
BIG = ['ffn_w_gate', 'ffn_w_up', 'ffn_w_down', 'ssm_w_in', 'ssm_w_out', 'w_k', 'w_v', 'attn_w_q', 'attn_w_o']
TRANSPOSED = ('ffn_w_gate', 'ffn_w_up', 'ssm_w_in')
SMALL = [n for n in WEIGHTS if n not in BIG]
SMALL_SHARDED = {'norm_w': 2, 'ssm_conv_w': 2, 'ssm_conv_b': 1, 'ssm_norm_w': 1}
ROW_ALIGN = 8 * LANES


def _pack_rows(parts):
    flat = jnp.concatenate([p.reshape(-1).astype(F32) for p in parts])
    pad = (-flat.size) % ROW_ALIGN
    return jnp.pad(flat, (0, pad)).reshape(-1, LANES)


def _unpack_rows(buf, shapes):
    flat = buf.reshape(-1)
    out, pos = [], 0
    for shp in shapes:
        size = math.prod(shp)
        out.append(flat[pos:pos + size].reshape(shp))
        pos += size
    return out


def _as2d(a):
    return a.reshape(-1, a.shape[-1])


def _heads_major(t, n_heads):
    s = t.shape[0]
    return t.reshape(s, n_heads, ATT_HEAD_DIM).transpose(1, 0, 2)


def _tokens_major(t):
    h, s, dh = t.shape
    return t.transpose(1, 0, 2).reshape(s, h * dh)


def _pad_lanes(v):
    return jnp.pad(v.reshape(1, -1), ((0, 0), (0, LANES - v.size)))


def _chips_first(t):
    return t.swapaxes(0, 1).reshape((-1,) + t.shape[3:])


def _parts_first(t, rows):
    return t.reshape((N_CHIPS, N_CORES, rows) + t.shape[1:]).swapaxes(0, 1)


def kernel(*args):
    names = (['x'] + WEIGHTS + ['loss_target'] + ['m_' + n for n in WEIGHTS] + ['v_' + n for n in WEIGHTS])
    a = dict(zip(names, args))
    for n in TRANSPOSED:
        for pre in ('', 'm_', 'v_'):
            a[pre + n] = a[pre + n].swapaxes(-1, -2)
    xi, yi, ci = lax.axis_index("x"), lax.axis_index("y"), lax.axis_index("c")
    chip = 2 * xi + yi
    south = ci == 0
    c_idx = jnp.reshape(ci, (1,)).astype(jnp.int32)
    chip_idx = jnp.reshape(chip, (1,)).astype(jnp.int32)
    x0 = a['x'][0]
    s = x0.shape[0]
    cos, sin = rope_tables(s)

    def own_slot(full, mine):
        return lax.dynamic_update_slice_in_dim(full, mine[:, None], chip, axis=1)

    def ffn_shard(l, i, src):
        return [src[n][l, i].astype(BF16).reshape(N_CORES, FF_PART, D_MODEL)
                for n in ('ffn_w_gate', 'ffn_w_up', 'ffn_w_down')]

    def own_slots(fulls, mines):
        return [own_slot(f, m) for f, m in zip(fulls, mines)]
    small_names = list(SMALL_SHARDED)
    small_sh = _pack_rows([a[n] for n in small_names])
    small_sh = small_sh.reshape(N_CORES, small_sh.shape[0] // 2, LANES)
    sh00 = ffn_shard(0, 0, a)
    (first_flight,), started = split_start([sh00 + [small_sh]], "gather", "gather_start_first")
    held = lax.optimization_barrier((started, {n: a[n] for n in BIG}))[1]
    sh01, sh10, sh11 = ffn_shard(0, 1, held), ffn_shard(1, 0, held), ffn_shard(1, 1, held)
    w_in_sh = jnp.pad(held['ssm_w_in'][0], ((0, IN_SHARD_PAD - IN_SHARD), (0, 0))).astype(BF16).reshape(
        N_CORES, IN_SHARD_PAD // 2, D_MODEL)
    w_out_sh = held['ssm_w_out'][0].astype(BF16).reshape(N_CORES, 256, D_MODEL)
    attn_sh = jnp.stack([held['attn_w_q'][0], held['attn_w_o'][0]]).astype(BF16)
    kv_sh = jnp.stack([held['w_k'], held['w_v']]).astype(BF16)
    rest_flights, all_started = split_start([[w_in_sh, kv_sh], [w_out_sh], sh01, sh10, [attn_sh], sh11], "gather",
                                            "gather_start_rest")
    in_flight = [first_flight] + rest_flights

    def arrive(idx, after, tag):
        return forward_cores(split_arrive(in_flight[idx], "gather", after, "gather_arrive_" + tag))

    first = run_exchange(arrive(0, all_started, "first"), "gather_hop_first")
    w00 = own_slots(first[:3], sh00)
    smalls = own_slot(first[3], small_sh)
    p = {}
    per_chip = [_unpack_rows(smalls[:, k], [a[n].shape for n in small_names]) for k in range(N_CHIPS)]
    for idx, n in enumerate(small_names):
        p[n] = jnp.concatenate([per_chip[k][idx] for k in range(N_CHIPS)], axis=SMALL_SHARDED[n])
    nw = p['norm_w']
    conv_w, conv_b, ssm_nw = p['ssm_conv_w'][0], p['ssm_conv_b'][0], p['ssm_norm_w'][0].reshape(1, D_INNER)

    h00 = rmsnorm_fwd(x0, nw[0, 0], "norm_in")
    x1, h01, gu00 = ffn_fwd(h00, x0, *w00, [nw[0, 1]], "ffn_fwd_00")
    w_in_g, kv_g = run_exchange(arrive(1, x1, "in"), "gather_hop_in")
    w_in_t = _chips_first(own_slot(w_in_g, w_in_sh)).reshape(N_CHIPS, IN_SHARD_PAD, D_MODEL)[:, :IN_SHARD].reshape(
        IN_PROJ_DIM, D_MODEL)
    w_dt_t = jnp.pad(w_in_t[D_INNER + CONV_DIM:], ((0, LANES - SSM_HEADS), (0, 0)))
    kv_g = own_slot(kv_g, kv_sh)
    w_k, w_v = kv_g[0].reshape(D_MODEL, KV_DIM), kv_g[1].reshape(D_MODEL, KV_DIM)

    zz = mm_nt(h01, w_in_t, "ssm_in_z", n=D_INNER)
    xbc = mm_nt(h01, w_in_t, "ssm_in_xbc", n=CONV_DIM, row0=D_INNER)
    dtp = mm_nt(h01, w_dt_t, "ssm_in_dt")
    act = conv_fwd(xbc, conv_w, conv_b, "ssm_conv")
    bias_p = _pad_lanes(a['ssm_dt_bias'][0])
    a_p = _pad_lanes(-jnp.exp(a['ssm_a_log'][0]))
    d_p = _pad_lanes(a['ssm_d'][0])
    (yn, y_pre, states), (w_out_g,) = ssd_fwd(act, zz, dtp, bias_p, a_p, d_p, ssm_nw, "ssd_fwd",
                                              ride=arrive(2, act, "out"))
    w_out = _chips_first(own_slot(w_out_g, w_out_sh))
    (x2, h02), w01 = mm_res(yn, w_out, x1, "ssm_out", norm_ws=[nw[0, 2]], ride=arrive(3, yn, "01"))
    w01 = own_slots(w01, sh01)
    x3, hkv, h10, gu01 = ffn_fwd(h02, x2, *w01, [a['kv_norm_w'], nw[1, 0]], "ffn_fwd_01")
    w10 = own_slots(run_exchange(arrive(4, x3, "10"), "gather_hop_10"), sh10)

    k_rot = rope_apply(mm_nn(hkv, w_k, "kv_k", bias=a['b_k']), cos, sin, "rope_k")
    v = mm_nn(hkv, w_v, "kv_v", bias=a['b_v'], out_dtype=BF16)
    kt = _heads_major(k_rot, N_KV_HEADS)
    vt = _heads_major(v, N_KV_HEADS)

    (x4, h11, gu10), (attn_g,) = ffn_fwd(h10, x3, *w10, [nw[1, 1]], "ffn_fwd_10", ride=arrive(5, v, "attn"))
    attn_g = own_slot(attn_g, attn_sh)
    w_q, w_o = attn_g[0].reshape(D_MODEL, D_MODEL), attn_g[1].reshape(D_MODEL, D_MODEL)
    scale = 1.0 / math.sqrt(ATT_HEAD_DIM)
    q_rot = rope_apply(mm_nn(h11, w_q, "attn_q", bias=a['attn_b_q'][0]), cos, sin, "rope_q", scale=scale)
    qt = _heads_major(q_rot, N_Q_HEADS)
    sink_rows = jnp.repeat(a['attn_sinks'][0].reshape(N_KV_HEADS, Q_PER_KV), WINDOW, axis=1).reshape(
        N_KV_HEADS, Q_PER_KV * WINDOW, 1)
    (ot,) = attn_fwd(qt, kt, vt, sink_rows, "attn_fwd")
    o = _tokens_major(ot)
    (x5, h12), w11 = mm_res(o, w_o, x4, "attn_out", bias=a['attn_b_o'][0], norm_ws=[nw[1, 2]],
                            ride=arrive(6, ot, "11"))
    w11 = own_slots(w11, sh11)
    x6, gu11 = ffn_fwd(h12, x5, *w11, [], "ffn_fwd_11")

    loss_v, dx6, d_final = loss_head(x6, a['final_norm_w'], a['loss_target'][0], "loss_head")
    loss = lax.psum(loss_v[0, 0], ("x", "y", "c"))
    g = {'final_norm_w': d_final[0]}

    def same_shape(xs, ys):
        runs = []
        for xv, yv in zip(xs, ys):
            if runs and runs[-1][0][0].shape == xv.shape:
                runs[-1][0].append(xv)
                runs[-1][1].append(yv)
            else:
                runs.append(([xv], [yv]))
        return runs

    def pre_reduce(grads, sib, tag):
        out = []
        for idx, (grp, sbs) in enumerate(same_shape(grads, list(sib))):
            ts = add_pair([gr.reshape(2, -1, gr.shape[-1]) for gr in grp], [_as2d(sb) for sb in sbs], c_idx,
                          "rs_add_%s_%d" % (tag, idx))
            out += [t.reshape(gr.shape[1:]) for t, gr in zip(ts, grp)]
        return out

    def chip_sum(landed, parts, tag):
        out = []
        for idx, (qs, owns) in enumerate(same_shape(list(landed), parts)):
            ts = sum_chips([q.reshape(N_CHIPS, -1, q.shape[-1]) for q in qs],
                           [own.reshape(N_CHIPS, -1, own.shape[-1]) for own in owns], chip_idx,
                           "rs_sum_%s_%d" % (tag, idx))
            out += [t.reshape(q.shape[1:]) for t, q in zip(ts, qs)]
        return out

    dnw = [[None] * 3 for _ in range(2)]
    sums = {}

    def trade(key):
        return swap_cores(sums[key], False)

    def fly(parts, tag, then):
        (flight,), started_ = split_start([parts], "scatter", "rs_scatter_%s_start" % tag)
        return flight, lax.optimization_barrier((started_, then))[1]

    def land(flight, parts, after, tag):
        return chip_sum(split_arrive(flight, "scatter", after, "rs_scatter_%s_arrive" % tag), parts, tag)

    dx5, dnw12, *g11 = ffn_bwd(dx6, h12, x5, nw[1, 2], gu11, *w11, "ffn_bwd_11")
    dnw[1][2] = dnw12[0]
    (d_wo, g['attn_b_o']), sib11 = mm_tn(o, dx5, "attn_dwo", col_sum=True, ride=swap_cores(g11, True))
    t11 = pre_reduce(g11, sib11, "11")
    f11, dx5 = fly(t11, "11", dx5)
    do = mm_nt(dx5, w_o, "attn_do", out_dtype=BF16)
    dqt, dkt, dvt, dsink = attn_bwd(qt, kt, vt, sink_rows, _heads_major(do, N_Q_HEADS), "attn_bwd")
    g['attn_sinks'] = jnp.sum(dsink[:, :, :Q_PER_KV, 0], axis=1).reshape(N_Q_HEADS)
    dq_pre = rope_apply(_tokens_major(dqt), cos, sin, "rope_dq", inverse=True, scale=scale, out_dtype=F32)
    d_wq, g['attn_b_q'] = mm_tn(h11, dq_pre, "attn_dwq", col_sum=True)
    g_attn = [jnp.stack([d_wq.reshape(N_CHIPS, 256, D_MODEL), d_wo.reshape(N_CHIPS, 256, D_MODEL)])]
    (dx4, dnw[1][1]), sib_attn = mm_rms_bwd([(dq_pre, 0, w_q, 0, D_MODEL, "nt")], dx5, x4, nw[1, 1], "attn_bwd_dh",
                                            ride=swap_cores(g_attn, True))
    t_attn = pre_reduce(g_attn, sib_attn, "attn")
    f_attn, dx4 = fly(t_attn, "attn", dx4)
    dx3, dnw10, *g10 = ffn_bwd(dx4, h10, x3, nw[1, 0], gu10, *w10, "ffn_bwd_10")
    dnw[1][0] = dnw10[0]
    sums['11'] = land(f11, t11, dx3, "11")
    sums['attn'] = land(f_attn, t_attn, dx3, "attn")
    dk_pre = rope_apply(_tokens_major(dkt), cos, sin, "rope_dk", inverse=True, out_dtype=F32)
    dv = _tokens_major(dvt)
    (d_wk, g['b_k']), sib10 = mm_tn(hkv, dk_pre, "kv_dwk", col_sum=True, ride=swap_cores(g10, True))
    t10 = pre_reduce(g10, sib10, "10")
    f10, dv = fly(t10, "10", dv)
    d_wv, g['b_v'] = mm_tn(hkv, dv, "kv_dwv", col_sum=True)
    g_kv = [jnp.stack([d_wk.reshape(N_CHIPS, 256, KV_DIM), d_wv.reshape(N_CHIPS, 256, KV_DIM)])]
    (dx3, g['kv_norm_w']), sib_kv = mm_rms_bwd(
        [(dk_pre, 0, w_k, 0, KV_DIM, "nt"), (dv, 0, w_v, 0, KV_DIM, "nt")], dx3, x3, a['kv_norm_w'], "kv_bwd_dh",
        ride=swap_cores(g_kv, True))
    t_kv = pre_reduce(g_kv, sib_kv, "kv")
    f_kv, dx3 = fly(t_kv, "kv", dx3)
    (dx2, dnw02, *g01), landed = ffn_bwd(dx3, h02, x2, nw[0, 2], gu01, *w01, "ffn_bwd_01",
                                         ride=join(trade('11'), trade('attn')))
    dnw[0][2] = dnw02[0]
    theirs = {'11': landed[:3], 'attn': landed[3:]}
    sums['10'] = land(f10, t10, dx2, "10")
    sums['kv'] = land(f_kv, t_kv, dx2, "kv")
    d_wout, sib01 = mm_tn(yn, dx2, "ssm_dwout", ride=swap_cores(g01, True))
    t01 = pre_reduce(g01, sib01, "01")
    f01, dx2 = fly(t01, "01", dx2)
    dyn = mm_nt(dx2, w_out, "ssm_dyn")
    (dxs, db_, dc_, dz, ddt, d_ssm_nw, d_bias, d_a, d_d), landed = ssd_bwd(
        dyn, act, zz, y_pre, states, dtp, bias_p, a_p, d_p, ssm_nw, "ssd_bwd",
        ride=join(trade('10'), trade('kv')))
    theirs['10'], theirs['kv'] = landed[:3], landed[3:]
    sums['01'] = land(f01, t01, dz, "01")
    g['ssm_norm_w'] = d_ssm_nw[:SSM_GROUPS].reshape(D_INNER)
    g['ssm_dt_bias'] = d_bias[0, :SSM_HEADS]
    g['ssm_a_log'] = d_a[0, :SSM_HEADS] * a_p[0, :SSM_HEADS]
    g['ssm_d'] = d_d[0, :SSM_HEADS]
    dxbc, g['ssm_conv_w'], g['ssm_conv_b'] = conv_bwd(dxs, db_, dc_, xbc, conv_w, conv_b, "ssm_conv_bwd")
    d_win = mm_tn(dz, h01, "ssm_dwz", rows=IN_PROJ_DIM)
    d_win = mm_tn(dxbc, h01, "ssm_dwxbc", into=d_win, rows=IN_PROJ_DIM, row0=D_INNER)
    d_win = mm_tn(ddt, h01, "ssm_dwdt", into=d_win, rows=IN_PROJ_DIM, row0=D_INNER + CONV_DIM, m_valid=SSM_HEADS)
    d_win = jnp.pad(d_win.reshape(N_CHIPS, IN_SHARD, D_MODEL), ((0, 0), (0, IN_SHARD_PAD - IN_SHARD), (0, 0)))
    g_ssm = [_parts_first(d_win.reshape(-1, D_MODEL), IN_SHARD_PAD // 2), _parts_first(d_wout, 256)]
    kb = 1024
    terms = ([(dz, j, w_in_t, j, kb, "nn") for j in range(D_INNER // kb)]
             + [(dxbc, j, w_in_t, D_INNER // kb + j, kb, "nn") for j in range(CONV_DIM // kb)]
             + [(ddt, 0, w_dt_t, 0, LANES, "nn")])
    (dx1, dnw[0][1]), sib_ssm = mm_rms_bwd(terms, dx2, x1, nw[0, 1], "ssm_bwd_dh", ride=swap_cores(g_ssm, True))
    t_ssm = pre_reduce(g_ssm, sib_ssm, "ssm")
    f_ssm, dx1 = fly(t_ssm, "ssm", dx1)
    (grad_x, dnw00, *g00), landed = ffn_bwd(dx1, h00, x0, nw[0, 0], gu00, *w00, "ffn_bwd_00", ride=trade('01'))
    dnw[0][0] = dnw00[0]
    theirs['01'] = landed
    sums['ssm'] = land(f_ssm, t_ssm, grad_x, "ssm")
    landed = run_exchange(join(swap_cores(g00, True), trade('ssm')), "rs_swap_00")
    t00 = pre_reduce(g00, landed[:3], "00")
    theirs['ssm'] = landed[3:]

    def both(key):
        return [(jnp.where(south, m_, t_), jnp.where(south, t_, m_)) for m_, t_ in zip(sums[key], theirs[key])]

    g['norm_w'] = jnp.stack([jnp.stack(r) for r in dnw])
    red = all_reduce_small(_pack_rows([g[n] for n in SMALL]), "reduce_vectors")

    t00 = lax.optimization_barrier((red, t00))[1]
    (flight00,), flying = split_start([t00], "scatter", "rs_scatter_00_start")

    def held(val):
        return lax.optimization_barrier((flying, val))[1]

    delta, new_m, new_v, gw = {}, {}, {}, {}
    ffn_names = ('ffn_w_gate', 'ffn_w_up', 'ffn_w_down')
    full = {key: both(key) for key in ('attn', 'kv', 'ssm')}
    lo, hi = full['attn'][0]
    gw['attn_w_q'], gw['attn_w_o'] = lo[None], hi[None]
    lo, hi = full['kv'][0]
    gw['w_k'], gw['w_v'] = lo, hi
    lo, hi = full['ssm'][0]
    gw['ssm_w_in'] = jnp.concatenate([lo, hi], axis=0)[:IN_SHARD][None]
    lo, hi = full['ssm'][1]
    gw['ssm_w_out'] = jnp.concatenate([lo, hi], axis=0)[None]

    for n, t in zip(SMALL, _unpack_rows(red, [g[n].shape for n in SMALL])):
        if n in SMALL_SHARDED:
            ax = SMALL_SHARDED[n] - (a[n].ndim - t.ndim)
            width = a[n].shape[SMALL_SHARDED[n]]
            t = lax.dynamic_slice_in_dim(t, chip * width, width, axis=ax)
        gw[n] = t.reshape(a[n].shape)

    def update(n):
        d, mo, vo = adamw(_as2d(a[n]), held(_as2d(gw[n])), _as2d(a['m_' + n]), _as2d(a['v_' + n]), "adamw_" + n)
        delta[n], new_m[n], new_v[n] = d.reshape(a[n].shape), mo.reshape(a[n].shape), vo.reshape(a[n].shape)

    for n in BIG:
        if n not in ffn_names:
            update(n)
    shapes = [a[n].shape for n in SMALL]
    packed = [_pack_rows([src[n] for n in SMALL]) for src in
              (a, gw, {n: a['m_' + n] for n in SMALL}, {n: a['v_' + n] for n in SMALL})]
    outs = adamw(*packed, "adamw_vectors")
    for dst, buf in zip((delta, new_m, new_v), outs):
        for n, t in zip(SMALL, _unpack_rows(buf, shapes)):
            dst[n] = t
    for key in ('01', '10', '11'):
        sums[key] = held(list(sums[key]))
    rest = [both(key) for key in ('01', '10', '11')]
    done = lax.optimization_barrier((outs[0], [delta[n] for n in BIG if n not in ffn_names], rest))[0]
    land00 = split_arrive(flight00, "scatter", done, "rs_scatter_00_arrive")
    sums['00'] = chip_sum(land00, t00, "00")
    theirs['00'] = run_exchange(trade('00'), "rs_trade_00")
    blocks = [both('00')] + rest
    for t, n in enumerate(ffn_names):
        gw[n] = jnp.concatenate([piece for blk in blocks for piece in blk[t]], axis=0).reshape(a[n].shape)
        update(n)
    for n in TRANSPOSED:
        for dst in (gw, delta, new_m, new_v):
            dst[n] = dst[n].swapaxes(-1, -2)

    return (loss, grad_x[None], *[gw[n] for n in WEIGHTS], *[delta[n] for n in WEIGHTS],
            *[new_m[n] for n in WEIGHTS], *[new_v[n] for n in WEIGHTS])
```

```python
import math

import jax
import jax.numpy as jnp
from jax import lax
from jax.experimental import pallas as pl
from jax.experimental.pallas import tpu as pltpu

F32 = jnp.float32
BF16 = jnp.bfloat16

D_MODEL = 1024
D_INNER = 2048
SSM_HEADS = 32
SSM_GROUPS = 4
HEADS_PER_GROUP = SSM_HEADS // SSM_GROUPS
SSM_HEAD_DIM = 64
SSM_STATE = 128
GROUP_DIM = D_INNER // SSM_GROUPS
CONV_DIM = D_INNER + 2 * SSM_GROUPS * SSM_STATE
CONV_WIDTH = 4
CHUNK = 128
ATT_HEAD_DIM = 64
N_Q_HEADS = 16
N_KV_HEADS = 4
Q_PER_KV = N_Q_HEADS // N_KV_HEADS
KV_DIM = N_KV_HEADS * ATT_HEAD_DIM
WINDOW = 128
ROPE_THETA = 10000.0
D_FF = 2816
N_CHIPS = 4
N_CORES = 2
FF_SHARD = D_FF // N_CHIPS
FF_PART = FF_SHARD // N_CORES
IN_PROJ_DIM = D_INNER + CONV_DIM + SSM_HEADS
IN_SHARD = IN_PROJ_DIM // N_CHIPS
IN_SHARD_PAD = 1312
EPS = 1e-5
NEG = -1e30
LANES = 128
VMEM_LIMIT = 56 * 1024 * 1024

ADAM_LR = 0.001
ADAM_B1 = 0.9
ADAM_B2 = 0.999
ADAM_EPS = 1e-08
ADAM_WD = 0.01
ADAM_STEP = 10

NN = ((1,), (0,))
NT = ((1,), (1,))
TN = ((0,), (0,))
MESH = pl.DeviceIdType.MESH
ANY = pl.BlockSpec(memory_space=pl.ANY)


def _dot(a, b, dims=NN, precision=None):
    return lax.dot_general(a, b, (dims, ((), ())), preferred_element_type=F32, precision=precision)


def _cp(n_grid):
    return pltpu.CompilerParams(dimension_semantics=("arbitrary",) * n_grid, vmem_limit_bytes=VMEM_LIMIT)


def _sigmoid(x):
    return 1.0 / (1.0 + jnp.exp(-x))


def _rms_fwd(xf, w):
    r = lax.rsqrt(jnp.mean(xf * xf, axis=-1, keepdims=True) + EPS)
    return xf * r * w


def _rms_bwd(dh, xf, w):
    r = lax.rsqrt(jnp.mean(xf * xf, axis=-1, keepdims=True) + EPS)
    xhat = xf * r
    dxhat = dh * w
    dx = r * (dxhat - xhat * jnp.mean(dxhat * xhat, axis=-1, keepdims=True))
    return dx, dh * xhat


def _row_tile(s, pref):
    return pref if s % pref == 0 else s


def _col_tile(n):
    for t in (1024, 768, 512, 256, 128):
        if n % t == 0:
            return t
    return n


def _sds(shape, dtype):
    return jax.ShapeDtypeStruct(tuple(shape), dtype)


class Exchange:
    def __init__(self, ins, out_shapes, sems, start, finish, inplace=False):
        self.ins, self.out_shapes, self.sems, self.start, self.finish = ins, out_shapes, sems, start, finish
        self.inplace = inplace


def _place():
    x, y, c = lax.axis_index("x"), lax.axis_index("y"), lax.axis_index("c")
    others = [(1 - x, y), (x, 1 - y), (1 - x, 1 - y)]
    return x, y, c, 2 * x + y, others


def _rc(src, dst, send_sem, recv_sem, dev):
    return pltpu.make_async_remote_copy(src_ref=src, dst_ref=dst, send_sem=send_sem, recv_sem=recv_sem,
                                        device_id=dev, device_id_type=MESH)


def scatter_chips(arrs):
    n = len(arrs)

    def copies(ins, outs, sems):
        send, recv = sems
        x, y, c, k, others = _place()
        out, land = [], []
        for a in range(n):
            for j, (px, py) in enumerate(others):
                out.append(_rc(ins[a].at[2 * px + py], outs[a].at[k], send.at[a, j], recv.at[a, j], (px, py, c)))
                blk = outs[a].at[2 * px + py]
                land.append(_rc(blk, blk, send.at[a, j], recv.at[a, j], (px, py, c)))
        return out, land

    def start(ins, outs, sems):
        for cp in copies(ins, outs, sems)[0]:
            cp.start()

    def finish(ins, outs, sems):
        out, land = copies(ins, outs, sems)
        for arrived in land:
            arrived.wait_recv()
        for cp in out:
            cp.wait_send()

    return Exchange(list(arrs), [_sds(a.shape, a.dtype) for a in arrs],
                    [pltpu.SemaphoreType.DMA((n, 3)), pltpu.SemaphoreType.DMA((n, 3))], start, finish)


def swap_cores(arrs, pick_other):
    n = len(arrs)

    def copies(ins, outs, sems):
        send, recv = sems
        x, y, c, _, _ = _place()
        return [_rc(ins[a].at[1 - c] if pick_other else ins[a], outs[a], send.at[a], recv.at[a], (x, y, 1 - c))
                for a in range(n)]

    def start(ins, outs, sems):
        for cp in copies(ins, outs, sems):
            cp.start()

    def finish(ins, outs, sems):
        for cp in copies(ins, outs, sems):
            cp.wait()

    shapes = [_sds(a.shape[1:] if pick_other else a.shape, a.dtype) for a in arrs]
    return Exchange(list(arrs), shapes, [pltpu.SemaphoreType.DMA((n,)), pltpu.SemaphoreType.DMA((n,))],
                    start, finish)


def join(*parts):
    parts = [p for p in parts if p is not None]
    if not parts:
        return None

    def split(refs, counts):
        out, pos = [], 0
        for cnt in counts:
            out.append(refs[pos:pos + cnt])
            pos += cnt
        return out

    n_in = [len(p.ins) for p in parts]
    n_out = [len(p.out_shapes) for p in parts]
    n_sem = [len(p.sems) for p in parts]

    def run(which):
        def go(ins, outs, sems):
            for p, i, o, s in zip(parts, split(ins, n_in), split(outs, n_out), split(sems, n_sem)):
                getattr(p, which)(i, o, s)
        return go

    return Exchange([a for p in parts for a in p.ins], [s for p in parts for s in p.out_shapes],
                    [s for p in parts for s in p.sems], run("start"), run("finish"))


def _pcall(body, *, out_shape, grid, in_specs, out_specs, args, name, scratch_shapes=(), ride=None, aliases=None):
    out_shape, out_specs, in_specs = tuple(out_shape), tuple(out_specs), list(in_specs)
    aliases = aliases or {}
    if ride is None:
        return pl.pallas_call(body, out_shape=out_shape, grid=grid, in_specs=in_specs, out_specs=out_specs,
                              scratch_shapes=list(scratch_shapes), input_output_aliases=aliases, name=name,
                              compiler_params=_cp(len(grid)))(*args)
    n_in, n_out, n_sc = len(args), len(out_shape), len(scratch_shapes)
    n_xi, n_xo = len(ride.ins), len(ride.out_shapes)

    def wrapped(*refs):
        pos = [0]

        def take(cnt):
            got = refs[pos[0]:pos[0] + cnt]
            pos[0] += cnt
            return got

        c_in, x_in, c_out, x_out, c_sc = take(n_in), take(n_xi), take(n_out), take(n_xo), take(n_sc)
        sems = refs[pos[0]:]
        first, last = True, True
        for d, size in enumerate(grid):
            first = jnp.logical_and(first, pl.program_id(d) == 0)
            last = jnp.logical_and(last, pl.program_id(d) == size - 1)

        @pl.when(first)
        def _():
            ride.start(x_in, x_out, sems)

        body(*c_in, *c_out, *c_sc)

        @pl.when(last)
        def _():
            ride.finish(x_in, x_out, sems)

    if ride.inplace:
        aliases = {**aliases, **{n_in + t: n_out + t for t in range(n_xi)}}
    res = pl.pallas_call(
        wrapped, out_shape=out_shape + tuple(ride.out_shapes), grid=grid,
        in_specs=in_specs + [ANY] * n_xi, out_specs=out_specs + (ANY,) * n_xo,
        scratch_shapes=list(scratch_shapes) + list(ride.sems), input_output_aliases=aliases, name=name,
        compiler_params=_cp(len(grid)))(*args, *ride.ins)
    return res[:n_out], res[n_out:]


def run_exchange(ex, name):
    n_xi, n_xo = len(ex.ins), len(ex.out_shapes)

    def body(*refs):
        ins, outs, sems = refs[:n_xi], refs[n_xi:n_xi + n_xo], refs[n_xi + n_xo:]
        ex.start(ins, outs, sems)
        ex.finish(ins, outs, sems)

    aliases = {t: t for t in range(n_xi)} if ex.inplace else {}
    return pl.pallas_call(body, out_shape=tuple(ex.out_shapes), in_specs=[ANY] * n_xi, out_specs=(ANY,) * n_xo,
                          scratch_shapes=list(ex.sems), input_output_aliases=aliases, name=name)(*ex.ins)


HBM_SPEC = pl.BlockSpec(memory_space=pltpu.HBM)
SEM_SPEC = pl.BlockSpec(memory_space=pltpu.SEMAPHORE)
EFFECT = pltpu.SideEffectType.DATAFLOW_SIDE_EFFECTING


def _route(kind, src, dst, c, k, peer):
    if kind == "gather":
        return src.at[c], dst.at[c, k], dst.at[c, peer]
    return src.at[peer], dst.at[k], dst.at[peer]


def split_start(batches, kind, name):
    flat = [a for batch in batches for a in batch]
    n, nb = len(flat), len(batches)
    lands = [lax.empty((2, N_CHIPS) + a.shape[1:] if kind == "gather" else a.shape, a.dtype) for a in flat]

    def body(*refs):
        srcs, dsts, sems, token = refs[:n], refs[n:2 * n], refs[2 * n:2 * n + 2 * nb], refs[-1]
        x, y, c, k, others = _place()
        pos = 0
        for b, batch in enumerate(batches):
            for a in range(len(batch)):
                for j, (px, py) in enumerate(others):
                    src, dst, _ = _route(kind, srcs[pos], dsts[pos], c, k, 2 * px + py)
                    _rc(src, dst, sems[2 * b].at[3 * a + j], sems[2 * b + 1].at[3 * a + j], (px, py, c)).start()
                pos += 1
        token[...] = jnp.zeros(token.shape, token.dtype)

    sem_shapes = [pltpu.SemaphoreType.DMA((3 * len(batch),)) for batch in batches for _ in range(2)]
    thru = [pltpu.HBM(a.shape, a.dtype) for a in flat] + [pltpu.HBM(l.shape, l.dtype) for l in lands]
    res = pl.pallas_call(
        body, name=name, out_shape=tuple(sem_shapes + thru + [_sds((8, LANES), F32)]),
        in_specs=[HBM_SPEC] * (2 * n),
        out_specs=tuple([SEM_SPEC] * (2 * nb) + [HBM_SPEC] * (2 * n) + [pl.BlockSpec(memory_space=pltpu.VMEM)]),
        input_output_aliases={t: 2 * nb + t for t in range(2 * n)},
        compiler_params=pltpu.CompilerParams(has_side_effects=EFFECT),
    )(*[pltpu.with_memory_space_constraint(t, pltpu.HBM) for t in flat + lands])
    sems, srcs, dsts = res[:2 * nb], res[2 * nb:2 * nb + n], res[2 * nb + n:2 * nb + 2 * n]
    out, pos = [], 0
    for b, batch in enumerate(batches):
        out.append((sems[2 * b], sems[2 * b + 1], list(srcs[pos:pos + len(batch)]), list(dsts[pos:pos + len(batch)])))
        pos += len(batch)
    return out, res[-1]


def split_arrive(handle, kind, after, name):
    send, recv, srcs, dsts = handle
    n = len(srcs)

    def body(*refs):
        s_refs, d_refs, send_ref, recv_ref = refs[:n], refs[n:2 * n], refs[2 * n], refs[2 * n + 1]
        x, y, c, k, others = _place()
        for a in range(n):
            for j, (px, py) in enumerate(others):
                src, _, landed = _route(kind, s_refs[a], d_refs[a], c, k, 2 * px + py)
                cp = _rc(src, landed, send_ref.at[3 * a + j], recv_ref.at[3 * a + j], (px, py, c))
                cp.wait_send()
                cp.wait_recv()

    res = pl.pallas_call(
        body, name=name, out_shape=tuple([pltpu.HBM(t.shape, t.dtype) for t in srcs + dsts]),
        in_specs=[HBM_SPEC] * (2 * n) + [SEM_SPEC, SEM_SPEC, ANY], out_specs=tuple([HBM_SPEC] * (2 * n)),
        input_output_aliases={t: t for t in range(2 * n)},
        compiler_params=pltpu.CompilerParams(has_side_effects=EFFECT),
    )(*srcs, *dsts, send, recv, after)
    return list(res[n:])


def forward_cores(bufs):
    n = len(bufs)

    def copies(outs, sems):
        send, recv = sems
        x, y, c, k, others = _place()
        onward, land = [], []
        for a in range(n):
            for j, (px, py) in enumerate(others):
                blk = outs[a].at[c, 2 * px + py]
                onward.append(_rc(blk, blk, send.at[a, j], recv.at[a, j], (x, y, 1 - c)))
                blk2 = outs[a].at[1 - c, 2 * px + py]
                land.append(_rc(blk2, blk2, send.at[a, j], recv.at[a, j], (x, y, 1 - c)))
        return onward, land

    def start(ins, outs, sems):
        for cp in copies(outs, sems)[0]:
            cp.start()

    def finish(ins, outs, sems):
        onward, land = copies(outs, sems)
        for arrived in land:
            arrived.wait_recv()
        for cp in onward:
            cp.wait_send()

    return Exchange(list(bufs), [_sds(b.shape, b.dtype) for b in bufs],
                    [pltpu.SemaphoreType.DMA((n, 3)), pltpu.SemaphoreType.DMA((n, 3))], start, finish, inplace=True)


def all_reduce_small(buf, name):
    r = buf.shape[0]
    n_dev = 8

    def body(in_ref, o_ref, land, send_sems, recv_sems):
        x, y, c, _, _ = _place()
        me = 4 * x + 2 * y + c
        land[me] = in_ref[...]
        sends = []
        for d in range(1, n_dev):
            peer = (x ^ (d >> 2), y ^ ((d >> 1) & 1), c ^ (d & 1))
            cp = _rc(in_ref, land.at[me], send_sems.at[d], recv_sems.at[d], peer)
            cp.start()
            sends.append(cp)
        for d in range(1, n_dev):
            blk = land.at[me ^ d]
            _rc(blk, blk, send_sems.at[d], recv_sems.at[d], (x, y, c)).wait_recv()
        for cp in sends:
            cp.wait_send()
        tot = land[0]
        for d in range(1, n_dev):
            tot = tot + land[d]
        o_ref[...] = tot

    vm = pl.BlockSpec(memory_space=pltpu.VMEM)
    return pl.pallas_call(
        body, out_shape=_sds(buf.shape, F32), in_specs=[vm], out_specs=vm,
        scratch_shapes=[pltpu.VMEM((n_dev, r, LANES), F32), pltpu.SemaphoreType.DMA((n_dev,)),
                        pltpu.SemaphoreType.DMA((n_dev,))],
        name=name)(buf)


def rmsnorm_fwd(x, w, name):
    s, d = x.shape
    tm = _row_tile(s, 512)

    def body(x_ref, w_ref, o_ref):
        o_ref[...] = _rms_fwd(x_ref[...], w_ref[...]).astype(BF16)

    return _pcall(body, out_shape=[_sds((s, d), BF16)], grid=(s // tm,),
                  in_specs=[pl.BlockSpec((tm, d), lambda i: (i, 0)), pl.BlockSpec((1, d), lambda i: (0, 0))],
                  out_specs=[pl.BlockSpec((tm, d), lambda i: (i, 0))], args=[x, w.reshape(1, d)], name=name)[0]


def _ffn_w_spec(chip_of, single=False):
    mode = dict(pipeline_mode=pl.Buffered(1)) if single else {}
    return pl.BlockSpec((N_CORES, 1, FF_PART, D_MODEL), lambda *ids: (0, chip_of(*ids), 0, 0), **mode)


def ffn_fwd(h, x, wg, wu, wd, norm_ws, name, ride=None):
    s, d = h.shape
    n_norm = len(norm_ws)
    tm = _row_tile(s, 1024)

    def body(*refs):
        h_ref, x_ref, wg_ref, wu_ref, wd_ref = refs[:5]
        nw_refs = refs[5:5 + n_norm]
        o_ref = refs[5 + n_norm]
        h_refs = refs[6 + n_norm:6 + 2 * n_norm]
        gu_ref, acc = refs[6 + 2 * n_norm], refs[7 + 2 * n_norm]
        k = pl.program_id(1)

        @pl.when(k == 0)
        def _():
            acc[...] = jnp.zeros(acc.shape, F32)

        hm = tm // 2
        for part in range(2):
            sub = pl.ds(part * hm, hm)
            hb = h_ref[sub, :]
            g = _dot(hb, wg_ref[...].reshape(FF_SHARD, d), NT)
            u = _dot(hb, wu_ref[...].reshape(FF_SHARD, d), NT)
            gu_ref[0, 0, sub, :] = g.astype(BF16)
            gu_ref[0, 1, sub, :] = u.astype(BF16)
            acc[sub, :] += _dot((g * _sigmoid(g) * u).astype(BF16), wd_ref[...].reshape(FF_SHARD, d))

        @pl.when(k == N_CHIPS - 1)
        def _():
            xn = x_ref[...] + 0.5 * acc[...]
            o_ref[...] = xn
            for nw_ref, hn_ref in zip(nw_refs, h_refs):
                hn_ref[...] = _rms_fwd(xn, nw_ref[...]).astype(BF16)

    row = pl.BlockSpec((tm, d), lambda i, k: (i, 0))
    vec = pl.BlockSpec((1, d), lambda i, k: (0, 0))
    wsp = _ffn_w_spec(lambda i, k: k)
    return _pcall(
        body, out_shape=[_sds((s, d), F32)] + [_sds((s, d), BF16)] * n_norm + [_sds((N_CHIPS, 2, s, FF_SHARD), BF16)],
        grid=(s // tm, N_CHIPS),
        in_specs=[row, row, wsp, wsp, wsp] + [vec] * n_norm,
        out_specs=[row] * (1 + n_norm) + [pl.BlockSpec((1, 2, tm, FF_SHARD), lambda i, k: (k, 0, i, 0))],
        scratch_shapes=[pltpu.VMEM((tm, d), F32)],
        args=[h, x, wg, wu, wd] + [nw.reshape(1, d) for nw in norm_ws], name=name, ride=ride)


def ffn_bwd(dxn, h, x_in, nw, gu, wg, wu, wd, name, ride=None):
    s, d = h.shape
    tm = _row_tile(s, 512)
    ni = s // tm
    last_e = N_CHIPS - 1

    def body(dxn_ref, h_ref, x_ref, nw_ref, gu_ref, wg_ref, wu_ref, wd_ref,
             dx_ref, dnw_ref, dwg_ref, dwu_ref, dwd_ref, dh, wacc):
        e = pl.program_id(0)
        i = pl.program_id(1)
        rows = pl.ds(pl.multiple_of(i * tm, tm), tm)

        @pl.when(i == 0)
        def _():
            wacc[...] = jnp.zeros(wacc.shape, F32)

        @pl.when(e == 0)
        def _():
            dh[rows, :] = jnp.zeros((tm, d), F32)

        hm = tm // 2
        for part in range(2):
            sub = pl.ds(part * hm, hm)
            dxb = dxn_ref[sub, :].astype(BF16)
            hb = h_ref[sub, :]
            g = gu_ref[0, 0, sub, :].astype(F32)
            u = gu_ref[0, 1, sub, :].astype(F32)
            drows = pl.ds(pl.multiple_of(i * tm + part * hm, hm), hm)
            sg = _sigmoid(g)
            silu = g * sg
            wacc[2] += _dot((0.5 * silu * u).astype(BF16), dxb, TN)
            da = 0.5 * _dot(dxb, wd_ref[...].reshape(FF_SHARD, d), NT)
            dg = (da * u * (sg * (1.0 + g * (1.0 - sg)))).astype(BF16)
            wacc[0] += _dot(dg, hb, TN)
            du = (da * silu).astype(BF16)
            dh[drows, :] += _dot(dg, wg_ref[...].reshape(FF_SHARD, d))
            wacc[1] += _dot(du, hb, TN)
            dh[drows, :] += _dot(du, wu_ref[...].reshape(FF_SHARD, d))

        @pl.when(i == ni - 1)
        def _():
            for t, dw_ref in enumerate((dwg_ref, dwu_ref, dwd_ref)):
                dw_ref[...] = wacc[t].astype(BF16).reshape(N_CORES, 1, FF_PART, d)

        @pl.when(e == last_e)
        def _():
            dx, dnw = _rms_bwd(dh[rows, :], x_ref[...], nw_ref[...])
            dx_ref[...] = dxn_ref[...] + dx
            col = jnp.sum(dnw, axis=0, keepdims=True)

            @pl.when(i == 0)
            def _():
                dnw_ref[...] = col

            @pl.when(i > 0)
            def _():
                dnw_ref[...] += col

    row = pl.BlockSpec((tm, d), lambda e, i: (i, 0))
    late = pl.BlockSpec((tm, d), lambda e, i: (jnp.where(e == last_e, i, 0), 0))
    vec = pl.BlockSpec((1, d), lambda e, i: (0, 0))
    wsp = _ffn_w_spec(lambda e, i: e, single=True)
    dwsp = _ffn_w_spec(lambda e, i: e, single=True)
    dw = _sds((N_CORES, N_CHIPS, FF_PART, d), BF16)
    return _pcall(
        body, out_shape=[_sds((s, d), F32), _sds((1, d), F32), dw, dw, dw],
        grid=(N_CHIPS, ni),
        in_specs=[row, row, late, vec, pl.BlockSpec((1, 2, tm, FF_SHARD), lambda e, i: (e, 0, i, 0)), wsp, wsp, wsp],
        out_specs=[late, vec, dwsp, dwsp, dwsp],
        scratch_shapes=[pltpu.VMEM((s, d), F32), pltpu.VMEM((3, FF_SHARD, d), F32)],
        args=[dxn, h, x_in, nw.reshape(1, d), gu, wg, wu, wd], name=name, ride=ride)


def mm_res(a, w, x, name, bias=None, norm_ws=(), ride=None):
    s, k = a.shape
    n = w.shape[1]
    tm = _row_tile(s, 256)
    has_bias = bias is not None
    n_norm = len(norm_ws)

    def body(*refs):
        a_ref, w_ref, x_ref = refs[:3]
        pos = 3
        t = _dot(a_ref[...], w_ref[...])
        if has_bias:
            t = t + refs[pos][...]
            pos += 1
        nw_refs = refs[pos:pos + n_norm]
        o_ref = refs[pos + n_norm]
        h_refs = refs[pos + n_norm + 1:]
        xn = x_ref[...] + t
        o_ref[...] = xn
        for nw_ref, h_ref in zip(nw_refs, h_refs):
            h_ref[...] = _rms_fwd(xn, nw_ref[...]).astype(BF16)

    row = pl.BlockSpec((tm, n), lambda i: (i, 0))
    vec = pl.BlockSpec((1, n), lambda i: (0, 0))
    in_specs = [pl.BlockSpec((tm, k), lambda i: (i, 0)), pl.BlockSpec((k, n), lambda i: (0, 0)), row]
    args = [a, w, x]
    if has_bias:
        in_specs.append(vec)
        args.append(bias.reshape(1, n))
    for nw in norm_ws:
        in_specs.append(vec)
        args.append(nw.reshape(1, n))
    return _pcall(body, out_shape=[_sds((s, n), F32)] + [_sds((s, n), BF16)] * n_norm, grid=(s // tm,),
                  in_specs=in_specs, out_specs=[row] * (1 + n_norm), args=args, name=name, ride=ride)


def mm_nn(a, w, name, bias=None, out_dtype=F32):
    s, k = a.shape
    n = w.shape[1]
    tm = _row_tile(s, 512)
    tn = _col_tile(n)
    has_bias = bias is not None

    def body(*refs):
        a_ref, w_ref = refs[:2]
        o_ref = refs[-1]
        t = _dot(a_ref[...], w_ref[...])
        if has_bias:
            t = t + refs[2][...]
        o_ref[...] = t.astype(out_dtype)

    in_specs = [pl.BlockSpec((tm, k), lambda j, i: (i, 0)), pl.BlockSpec((k, tn), lambda j, i: (0, j))]
    args = [a, w]
    if has_bias:
        in_specs.append(pl.BlockSpec((1, tn), lambda j, i: (0, j)))
        args.append(bias.reshape(1, n))
    return _pcall(body, out_shape=[_sds((s, n), out_dtype)], grid=(n // tn, s // tm), in_specs=in_specs,
                  out_specs=[pl.BlockSpec((tm, tn), lambda j, i: (i, j))], args=args, name=name)[0]


def mm_nt(a, w, name, n=None, row0=0, out_dtype=F32, ride=None):
    s, k = a.shape
    n = w.shape[0] if n is None else n
    tm = _row_tile(s, 512)
    tn = _col_tile(n)
    base = row0 // tn
    assert row0 % tn == 0

    def body(a_ref, w_ref, o_ref):
        o_ref[...] = _dot(a_ref[...].astype(BF16), w_ref[...], NT).astype(out_dtype)

    res = _pcall(body, out_shape=[_sds((s, n), out_dtype)], grid=(n // tn, s // tm),
                 in_specs=[pl.BlockSpec((tm, k), lambda j, i: (i, 0)), pl.BlockSpec((tn, k), lambda j, i: (base + j, 0))],
                 out_specs=[pl.BlockSpec((tm, tn), lambda j, i: (i, j))], args=[a, w], name=name, ride=ride)
    return res[0] if ride is None else (res[0][0], res[1])


def mm_tn(a, b, name, into=None, rows=None, row0=0, m_valid=None, col_sum=False, ride=None):
    s, m = a.shape
    n = b.shape[1]
    mv = m if m_valid is None else m_valid
    tm = _col_tile(m) if m_valid is None else mv
    tn = n if n <= 1024 else _col_tile(n)
    rows = mv if rows is None else rows
    assert row0 % tm == 0 and (m_valid is None or m == LANES)
    assert not col_sum or mv == tm
    base = row0 // tm
    ta = m if m_valid is not None else tm

    def body(*refs):
        a_ref, b_ref = refs[0], refs[1]
        o_ref = refs[-2] if col_sum else refs[-1]
        bf = b_ref[...]
        t = _dot(a_ref[...].astype(BF16), bf.astype(BF16), TN)
        o_ref[...] = t[:tm].astype(BF16)
        if col_sum:
            refs[-1][...] = jnp.sum(bf.astype(F32), axis=0, keepdims=True)

    in_specs = [pl.BlockSpec((s, ta), lambda i, j: (0, i)), pl.BlockSpec((s, tn), lambda i, j: (0, j))]
    args = [a, b]
    aliases = None
    if into is not None:
        in_specs.append(ANY)
        args.append(into)
        aliases = {2: 0}
    out_shape = [_sds((rows, n), BF16)]
    out_specs = [pl.BlockSpec((tm, tn), lambda i, j: (base + i, j))]
    if col_sum:
        out_shape.append(_sds((1, n), F32))
        out_specs.append(pl.BlockSpec((1, tn), lambda i, j: (0, j)))
    res = _pcall(body, out_shape=out_shape, grid=(mv // tm, n // tn), in_specs=in_specs, out_specs=out_specs,
                 args=args, name=name, ride=ride, aliases=aliases)
    outs = res if ride is None else res[0]
    out = (outs[0], outs[1][0]) if col_sum else outs[0]
    return out if ride is None else (out, res[1])


def mm_rms_bwd(terms, dxn, x, nw, name, ride=None):
    s, n = x.shape
    nt_ = len(terms)
    tm = _row_tile(s, 256)
    forms = [t[5] for t in terms]

    def body(*refs):
        dxn_ref, x_ref, nw_ref, dx_ref, dnw_ref = refs[2 * nt_:]
        i = pl.program_id(0)
        dh = None
        for t in range(nt_):
            part = _dot(refs[2 * t][...].astype(BF16), refs[2 * t + 1][...], NN if forms[t] == "nn" else NT)
            dh = part if dh is None else dh + part
        dx, dnw = _rms_bwd(dh, x_ref[...], nw_ref[...])
        dx_ref[...] = dxn_ref[...] + dx
        col = jnp.sum(dnw, axis=0, keepdims=True)

        @pl.when(i == 0)
        def _():
            dnw_ref[...] = col

        @pl.when(i > 0)
        def _():
            dnw_ref[...] += col

    in_specs, args = [], []
    for a, cb, w, rb, kb, form in terms:
        in_specs.append(pl.BlockSpec((tm, kb), lambda i, cb=cb: (i, cb)))
        if form == "nn":
            in_specs.append(pl.BlockSpec((kb, n), lambda i, rb=rb: (rb, 0)))
        else:
            in_specs.append(pl.BlockSpec((n, kb), lambda i, rb=rb: (0, rb)))
        args += [a, w]
    row = pl.BlockSpec((tm, n), lambda i: (i, 0))
    vec = pl.BlockSpec((1, n), lambda i: (0, 0))
    res = _pcall(body, out_shape=[_sds((s, n), F32), _sds((1, n), F32)], grid=(s // tm,),
                 in_specs=in_specs + [row, row, vec], out_specs=[row, vec],
                 args=args + [dxn, x, nw.reshape(1, n)], name=name, ride=ride)
    outs = res if ride is None else res[0]
    out = (outs[0], outs[1][0])
    return out if ride is None else (out, res[1])


def rope_tables(s):
    pos = jnp.arange(s, dtype=F32)
    inv = 1.0 / (ROPE_THETA ** (jnp.arange(0, ATT_HEAD_DIM, 2, dtype=F32) / ATT_HEAD_DIM))
    ang = pos[:, None] * inv[None, :]
    cos = jnp.tile(jnp.cos(ang), (1, 2 * LANES // ATT_HEAD_DIM))
    sin = jnp.tile(jnp.sin(ang), (1, 2 * LANES // ATT_HEAD_DIM))
    return cos, sin


def rope_apply(t, cos, sin, name, inverse=False, scale=1.0, out_dtype=BF16):
    s, n = t.shape
    tm = _row_tile(s, 512)
    half = ATT_HEAD_DIM // 2
    reps = n // LANES

    def body(t_ref, c_ref, s_ref, o_ref):
        tf = t_ref[...].astype(F32)
        c = jnp.tile(c_ref[...], (1, reps))
        sn = jnp.tile(s_ref[...], (1, reps))
        lane = lax.broadcasted_iota(jnp.int32, tf.shape, 1)
        first = (lane & (ATT_HEAD_DIM - 1)) < half
        rot = jnp.where(first, -pltpu.roll(tf, n - half, 1), pltpu.roll(tf, half, 1))
        sign = -1.0 if inverse else 1.0
        o_ref[...] = (scale * (tf * c + sign * rot * sn)).astype(out_dtype)

    tab = pl.BlockSpec((tm, LANES), lambda i: (i, 0))
    return _pcall(body, out_shape=[_sds((s, n), out_dtype)], grid=(s // tm,),
                  in_specs=[pl.BlockSpec((tm, n), lambda i: (i, 0)), tab, tab],
                  out_specs=[pl.BlockSpec((tm, n), lambda i: (i, 0))], args=[t, cos, sin], name=name)[0]


CONV_TILE = 256


def _shift_down(u, k):
    if k == 0:
        return u
    row = lax.broadcasted_iota(jnp.int32, u.shape, 0)
    return jnp.where(row >= k, pltpu.roll(u, k, 0), 0.0)


def _shift_up(u, k):
    if k == 0:
        return u
    s = u.shape[0]
    row = lax.broadcasted_iota(jnp.int32, u.shape, 0)
    return jnp.where(row < s - k, pltpu.roll(u, s - k, 0), 0.0)


def _conv_taps(u):
    return [_shift_down(u, CONV_WIDTH - 1 - k) for k in range(CONV_WIDTH)]


def _conv_pre(taps, w_ref, b_ref):
    pre = b_ref[...] + w_ref[0:1, :] * taps[0]
    for k in range(1, CONV_WIDTH):
        pre += w_ref[k:k + 1, :] * taps[k]
    return pre


def conv_fwd(u, w, b, name, ride=None):
    s, c = u.shape

    def body(u_ref, w_ref, b_ref, o_ref):
        pre = _conv_pre(_conv_taps(u_ref[...]), w_ref, b_ref)
        o_ref[...] = pre * _sigmoid(pre)

    col = pl.BlockSpec((s, CONV_TILE), lambda j: (0, j))
    res = _pcall(body, out_shape=[_sds((s, c), F32)], grid=(c // CONV_TILE,),
                 in_specs=[col, pl.BlockSpec((CONV_WIDTH, CONV_TILE), lambda j: (0, j)),
                           pl.BlockSpec((1, CONV_TILE), lambda j: (0, j))],
                 out_specs=[col], args=[u, w, b.reshape(1, c)], name=name, ride=ride)
    return res[0] if ride is None else (res[0][0], res[1])


def conv_bwd(dxs, db_, dc_, u, w, b, name, ride=None):
    s, c = u.shape
    n_x = dxs.shape[1] // CONV_TILE
    n_b = db_.shape[1] // CONV_TILE

    def body(dx_ref, dbb_ref, dcc_ref, u_ref, w_ref, b_ref, du_ref, dw_ref, dbias_ref):
        j = pl.program_id(0)
        dact = jnp.where(j < n_x, dx_ref[...], jnp.where(j < n_x + n_b, dbb_ref[...], dcc_ref[...]))
        taps = _conv_taps(u_ref[...])
        pre = _conv_pre(taps, w_ref, b_ref)
        sg = _sigmoid(pre)
        dpre = dact * (sg * (1.0 + pre * (1.0 - sg)))
        du = w_ref[CONV_WIDTH - 1:CONV_WIDTH, :] * dpre
        for k in range(CONV_WIDTH - 1):
            du += w_ref[k:k + 1, :] * _shift_up(dpre, CONV_WIDTH - 1 - k)
        du_ref[...] = du
        dbias_ref[...] = jnp.sum(dpre, axis=0, keepdims=True)
        for k in range(CONV_WIDTH):
            dw_ref[k:k + 1, :] = jnp.sum(dpre * taps[k], axis=0, keepdims=True)

    col = pl.BlockSpec((s, CONV_TILE), lambda j: (0, j))
    wsp = pl.BlockSpec((CONV_WIDTH, CONV_TILE), lambda j: (0, j))
    bsp = pl.BlockSpec((1, CONV_TILE), lambda j: (0, j))
    res = _pcall(
        body, out_shape=[_sds((s, c), F32), _sds((CONV_WIDTH, c), F32), _sds((1, c), F32)], grid=(c // CONV_TILE,),
        in_specs=[pl.BlockSpec((s, CONV_TILE), lambda j: (0, jnp.minimum(j, n_x - 1))),
                  pl.BlockSpec((s, CONV_TILE), lambda j: (0, jnp.clip(j - n_x, 0, n_b - 1))),
                  pl.BlockSpec((s, CONV_TILE), lambda j: (0, jnp.clip(j - n_x - n_b, 0, n_b - 1))),
                  col, wsp, bsp],
        out_specs=[col, wsp, bsp], args=[dxs, db_, dc_, u, w, b.reshape(1, c)], name=name, ride=ride)
    (du, dw, db), rode = res if ride is not None else (res, None)
    return (du, dw, db[0]) if ride is None else ((du, dw, db[0]), rode)


def _lane_pick(mat, idx):
    lane = lax.broadcasted_iota(jnp.int32, mat.shape, 1)
    return jnp.sum(jnp.where(lane == idx, mat, 0.0), axis=1, keepdims=True)


def _sub_pick(mat, idx):
    sub = lax.broadcasted_iota(jnp.int32, mat.shape, 0)
    return jnp.sum(jnp.where(sub == idx, mat, 0.0), axis=0, keepdims=True)


def _expand_heads(cols):
    rows = cols[0].shape[0]
    left = lax.broadcasted_iota(jnp.int32, (rows, LANES), 1) < SSM_HEAD_DIM
    return jnp.concatenate(
        [jnp.where(left, cols[2 * p], cols[2 * p + 1]) for p in range(HEADS_PER_GROUP // 2)], axis=1)


def _dot_01(x, ones, ones_first, pieces):
    tot, rest = None, x
    for _ in range(pieces):
        piece = rest.astype(BF16)
        rest = rest - piece.astype(F32)
        part = _dot(ones, piece) if ones_first else _dot(piece, ones)
        tot = part if tot is None else tot + part
    return tot


def _heads_to_lanes(mat, g):
    jj = lax.broadcasted_iota(jnp.int32, (GROUP_DIM, LANES), 0)
    ll = lax.broadcasted_iota(jnp.int32, (GROUP_DIM, LANES), 1)
    sel = (ll == HEADS_PER_GROUP * g + (jj >> 6)).astype(BF16)
    return _dot_01(mat, sel, False, 3)


def _softplus(x):
    return jnp.maximum(x, 0.0) + jnp.log1p(jnp.exp(-jnp.abs(x)))


def _ssd_scalars(dt_ref, bias_ref, a_ref, dtall, csall, cst):
    dta = _softplus(dt_ref[...] + bias_ref[...])
    row = lax.broadcasted_iota(jnp.int32, (CHUNK, CHUNK), 0)
    col = lax.broadcasted_iota(jnp.int32, (CHUNK, CHUNK), 1)
    cs = _dot_01(dta * a_ref[...], (row >= col).astype(BF16), True, 3)
    dtall[...] = dta
    csall[...] = cs
    cst[...] = cs.T


def _decay_mat(cs_col, cs_row):
    row = lax.broadcasted_iota(jnp.int32, (CHUNK, CHUNK), 0)
    col = lax.broadcasted_iota(jnp.int32, (CHUNK, CHUNK), 1)
    return jnp.exp(jnp.where(row >= col, cs_col - cs_row, NEG))


def _head_mask(xpair, right):
    lane = lax.broadcasted_iota(jnp.int32, xpair.shape, 1)
    keep = (lane >= SSM_HEAD_DIM) if right else (lane < SSM_HEAD_DIM)
    return jnp.where(keep, xpair, 0.0)


def _chunk_cols(x_all, g):
    return [_lane_pick(x_all, HEADS_PER_GROUP * g + r) for r in range(HEADS_PER_GROUP)]


def _decay_col(cs_cols):
    return jnp.concatenate(
        [jnp.broadcast_to(jnp.exp(cc[CHUNK - 1:CHUNK, :]), (SSM_HEAD_DIM, 1)) for cc in cs_cols], axis=0)


def ssd_fwd(act, z, dtp, bias_p, a_p, d_p, normw, name, ride=None):
    s = act.shape[0]
    nc = s // CHUNK

    def body(xs_all, b_all, c_all, z_all, dt_ref, bias_ref, a_ref, d_ref, nw_all,
             yn_all, y_all, st_all, state, dtall, csall, cst):
        _ssd_scalars(dt_ref, bias_ref, a_ref, dtall, csall, cst)

        @pl.when(pl.program_id(0) == 0)
        def _():
            state[...] = jnp.zeros(state.shape, F32)

        for g in range(SSM_GROUPS):
            wide = pl.ds(g * GROUP_DIM, GROUP_DIM)
            narrow = pl.ds(g * SSM_STATE, SSM_STATE)
            group(g, xs_all.at[:, wide], b_all.at[:, narrow], c_all.at[:, narrow], z_all.at[:, wide], d_ref,
                  nw_all.at[:, wide], yn_all.at[:, wide], y_all.at[:, wide], st_all.at[:, pl.ds(g, 1)],
                  state, dtall, csall, cst)

    def group(g, xs_ref, b_ref, c_ref, z_ref, d_ref, nw_ref, yn_ref, y_ref, st_ref, state, dtall, csall, cst):
        cs_cols = _chunk_cols(csall[...], g)
        dt_cols = _chunk_cols(dtall[...], g)
        cs_rows = [_sub_pick(cst[...], HEADS_PER_GROUP * g + r) for r in range(HEADS_PER_GROUP)]
        d_cols = _chunk_cols(d_ref[...], g)
        cs_exp = _expand_heads(cs_cols)
        dt_exp = _expand_heads(dt_cols)
        d_exp = _expand_heads(d_cols)
        xs = xs_ref[...]
        bb = b_ref[...].astype(BF16)
        cb16 = c_ref[...].astype(BF16)
        xdt = xs * dt_exp
        s_prev = state[g]
        st_ref[0, 0] = s_prev
        y_off = _dot(cb16, s_prev.astype(BF16), NT) * jnp.exp(cs_exp)
        decay_st = jnp.exp(cs_exp[CHUNK - 1:CHUNK, :] - cs_exp)
        contrib = _dot((xdt * decay_st).astype(BF16), bb, TN)
        state[g] = _decay_col(cs_cols) * s_prev + contrib
        cbm = _dot(cb16, bb, NT)
        pairs = []
        for p in range(HEADS_PER_GROUP // 2):
            xpair = xdt[:, LANES * p:LANES * (p + 1)]
            m0 = (cbm * _decay_mat(cs_cols[2 * p], cs_rows[2 * p])).astype(BF16)
            m1 = (cbm * _decay_mat(cs_cols[2 * p + 1], cs_rows[2 * p + 1])).astype(BF16)
            pairs.append(_dot(m0, _head_mask(xpair, False).astype(BF16))
                         + _dot(m1, _head_mask(xpair, True).astype(BF16)))
        y = jnp.concatenate(pairs, axis=1) + y_off + xs * d_exp
        y_ref[...] = y
        zf = z_ref[...]
        yg = y * (zf * _sigmoid(zf))
        yn_ref[...] = _rms_fwd(yg, nw_ref[...]).astype(BF16)

    gn = SSM_GROUPS * SSM_STATE
    wide = pl.BlockSpec((CHUNK, D_INNER), lambda c: (c, 0))
    par = pl.BlockSpec((1, LANES), lambda c: (0, 0))
    return _pcall(
        body,
        out_shape=[_sds((s, D_INNER), BF16), _sds((s, D_INNER), F32),
                   _sds((nc, SSM_GROUPS, GROUP_DIM, SSM_STATE), F32)],
        grid=(nc,),
        in_specs=[wide,
                  pl.BlockSpec((CHUNK, gn), lambda c: (c, D_INNER // gn)),
                  pl.BlockSpec((CHUNK, gn), lambda c: (c, D_INNER // gn + 1)),
                  wide,
                  pl.BlockSpec((CHUNK, LANES), lambda c: (c, 0)),
                  par, par, par,
                  pl.BlockSpec((1, D_INNER), lambda c: (0, 0))],
        out_specs=[wide, wide, pl.BlockSpec((1, SSM_GROUPS, GROUP_DIM, SSM_STATE), lambda c: (c, 0, 0, 0))],
        scratch_shapes=[pltpu.VMEM((SSM_GROUPS, GROUP_DIM, SSM_STATE), F32),
                        pltpu.VMEM((CHUNK, LANES), F32), pltpu.VMEM((CHUNK, LANES), F32),
                        pltpu.VMEM((LANES, CHUNK), F32)],
        args=[act, act, act, z, dtp, bias_p, a_p, d_p, normw], name=name, ride=ride)


def ssd_bwd(dyn, act, z, y_pre, states, dtp, bias_p, a_p, d_p, normw, name, ride=None):
    s = act.shape[0]
    nc = s // CHUNK

    def body(dyn_all, xs_all, b_all, c_all, z_all, y_all, st_all, dt_ref, bias_ref, a_ref, d_ref, nw_all,
             dxs_all, db_all, dc_all, dz_all, ddt_ref, dnw_ref, dbias_ref, da_ref, dd_ref,
             dstate, dtall, csall, cst):
        _ssd_scalars(dt_ref, bias_ref, a_ref, dtall, csall, cst)
        ddt_ref[...] = jnp.zeros((CHUNK, LANES), F32)

        @pl.when(pl.program_id(0) == 0)
        def _():
            dstate[...] = jnp.zeros(dstate.shape, F32)
            dnw_ref[...] = jnp.zeros(dnw_ref.shape, F32)
            dbias_ref[...] = jnp.zeros((1, LANES), F32)
            da_ref[...] = jnp.zeros((1, LANES), F32)
            dd_ref[...] = jnp.zeros((1, LANES), F32)

        for g in range(SSM_GROUPS):
            wide = pl.ds(g * GROUP_DIM, GROUP_DIM)
            narrow = pl.ds(g * SSM_STATE, SSM_STATE)
            group(g, dyn_all.at[:, wide], xs_all.at[:, wide], b_all.at[:, narrow], c_all.at[:, narrow],
                  z_all.at[:, wide], y_all.at[:, wide], st_all.at[:, pl.ds(g, 1)], dt_ref, bias_ref, a_ref, d_ref,
                  nw_all.at[:, wide], dxs_all.at[:, wide], db_all.at[:, narrow], dc_all.at[:, narrow],
                  dz_all.at[:, wide], ddt_ref, dnw_ref, dbias_ref, da_ref, dd_ref, dstate, dtall, csall, cst)

    def group(g, dyn_ref, xs_ref, b_ref, c_ref, z_ref, y_ref, st_ref, dt_ref, bias_ref, a_ref, d_ref, nw_ref,
              dxs_ref, db_ref, dc_ref, dz_ref, ddt_ref, dnw_ref, dbias_ref, da_ref, dd_ref,
              dstate, dtall, csall, cst):
        cs_cols = _chunk_cols(csall[...], g)
        dt_cols = _chunk_cols(dtall[...], g)
        cs_rows = [_sub_pick(cst[...], HEADS_PER_GROUP * g + r) for r in range(HEADS_PER_GROUP)]
        d_cols = _chunk_cols(d_ref[...], g)
        cs_exp = _expand_heads(cs_cols)
        dt_exp = _expand_heads(dt_cols)
        d_exp = _expand_heads(d_cols)
        xs = xs_ref[...]
        bb = b_ref[...].astype(BF16)
        cb16 = c_ref[...].astype(BF16)
        xdt = xs * dt_exp
        s_prev = st_ref[0, 0]
        s_prev16 = s_prev.astype(BF16)
        ds_next = dstate[g]
        ds16 = ds_next.astype(BF16)

        zf = z_ref[...]
        sz = _sigmoid(zf)
        silu_z = zf * sz
        y = y_ref[...]
        yg = y * silu_z
        dout = dyn_ref[...]
        dyg, dnw = _rms_bwd(dout, yg, nw_ref[...])
        dnw_ref[pl.ds(g, 1), :] += jnp.sum(dnw, axis=0, keepdims=True)
        dy = dyg * silu_z
        dz_ref[...] = dyg * y * (sz * (1.0 + zf * (1.0 - sz)))
        dd_ref[...] += jnp.sum(_heads_to_lanes(dy * xs, g), axis=0, keepdims=True)

        exp_cs = jnp.exp(cs_exp)
        decay_st = jnp.exp(cs_exp[CHUNK - 1:CHUNK, :] - cs_exp)
        cs_t = _dot(cb16, s_prev16, NT)
        dyo = dy * exp_cs
        dc_acc = _dot(dyo.astype(BF16), s_prev16, NN)
        g1 = _dot(bb, ds16, NT)
        xds = xdt * decay_st
        db_acc = _dot(xds.astype(BF16), ds16, NN)
        dxdt_off = g1 * decay_st
        t_exp = g1 * xds
        dcs_exp = dy * cs_t * exp_cs - t_exp
        decay_c = _decay_col(cs_cols)
        dstate[g] = decay_c * ds_next + _dot(dyo.astype(BF16), cb16, TN)
        dlast_col = jnp.sum(ds_next * s_prev, axis=1, keepdims=True) * decay_c
        jj = lax.broadcasted_iota(jnp.int32, (GROUP_DIM, LANES), 0)
        ll = lax.broadcasted_iota(jnp.int32, (GROUP_DIM, LANES), 1)
        sel = ll == HEADS_PER_GROUP * g + (jj >> 6)
        dlast = jnp.sum(jnp.where(sel, dlast_col, 0.0), axis=0, keepdims=True)
        t_all = _heads_to_lanes(t_exp, g)
        dlast += jnp.sum(t_all, axis=0, keepdims=True)
        dcs_all = _heads_to_lanes(dcs_exp, g)

        cbm = _dot(cb16, bb, NT)
        dcb = jnp.zeros((CHUNK, CHUNK), F32)
        dcs_rows = jnp.zeros((LANES, CHUNK), F32)
        lane_l = lax.broadcasted_iota(jnp.int32, (CHUNK, LANES), 1)
        sub_l = lax.broadcasted_iota(jnp.int32, (LANES, CHUNK), 0)
        dxdt_pairs = []
        for p in range(HEADS_PER_GROUP // 2):
            xpair16 = xdt[:, LANES * p:LANES * (p + 1)].astype(BF16)
            dypair = dy[:, LANES * p:LANES * (p + 1)]
            acc = None
            for r in (2 * p, 2 * p + 1):
                lm = _decay_mat(cs_cols[r], cs_rows[r])
                m = cbm * lm
                dyh = _head_mask(dypair, r % 2 == 1).astype(BF16)
                dm = _dot(dyh, xpair16, NT)
                dcb += dm * lm
                q = dm * m
                idx = HEADS_PER_GROUP * g + r
                dcs_all += jnp.where(lane_l == idx, jnp.sum(q, axis=1, keepdims=True), 0.0)
                dcs_rows -= jnp.where(sub_l == idx, jnp.sum(q, axis=0, keepdims=True), 0.0)
                part = _dot(m.astype(BF16), dyh, TN)
                acc = part if acc is None else acc + part
            dxdt_pairs.append(acc)
        dxdt = jnp.concatenate(dxdt_pairs, axis=1) + dxdt_off
        dcb16 = dcb.astype(BF16)
        dc_ref[...] = dc_acc + _dot(dcb16, bb, NN)
        db_ref[...] = db_acc + _dot(dcb16, cb16, TN)
        dxs_ref[...] = dxdt * dt_exp + dy * d_exp

        dcs_all += dcs_rows.T
        row = lax.broadcasted_iota(jnp.int32, (CHUNK, CHUNK), 0)
        col = lax.broadcasted_iota(jnp.int32, (CHUNK, CHUNK), 1)
        last_row = lax.broadcasted_iota(jnp.int32, (CHUNK, LANES), 0) == CHUNK - 1
        dcs_all += jnp.where(last_row, dlast, 0.0)
        da_all = _dot_01(dcs_all, (col >= row).astype(BF16), True, 3)
        dta = dtall[...]
        in_group = jnp.logical_and(lane_l >= HEADS_PER_GROUP * g, lane_l < HEADS_PER_GROUP * (g + 1))
        ddt = jnp.where(in_group, da_all * a_ref[...] + _heads_to_lanes(dxdt * xs, g), 0.0)
        da_ref[...] += jnp.sum(jnp.where(in_group, da_all * dta, 0.0), axis=0, keepdims=True)
        ddt_raw = ddt * _sigmoid(dt_ref[...] + bias_ref[...])
        ddt_ref[...] += ddt_raw
        dbias_ref[...] += jnp.sum(ddt_raw, axis=0, keepdims=True)

    gn = SSM_GROUPS * SSM_STATE
    wide = pl.BlockSpec((CHUNK, D_INNER), lambda c: (nc - 1 - c, 0))
    st = pl.BlockSpec((CHUNK, gn), lambda c: (nc - 1 - c, 0))
    par = pl.BlockSpec((1, LANES), lambda c: (0, 0))
    dtb = pl.BlockSpec((CHUNK, LANES), lambda c: (nc - 1 - c, 0))
    f = lambda shape: _sds(shape, F32)
    return _pcall(
        body,
        out_shape=[f((s, D_INNER)), f((s, gn)), f((s, gn)),
                   f((s, D_INNER)), f((s, LANES)), f((8, GROUP_DIM)), f((1, LANES)), f((1, LANES)), f((1, LANES))],
        grid=(nc,),
        in_specs=[wide, wide,
                  pl.BlockSpec((CHUNK, gn), lambda c: (nc - 1 - c, D_INNER // gn)),
                  pl.BlockSpec((CHUNK, gn), lambda c: (nc - 1 - c, D_INNER // gn + 1)),
                  wide, wide,
                  pl.BlockSpec((1, SSM_GROUPS, GROUP_DIM, SSM_STATE), lambda c: (nc - 1 - c, 0, 0, 0)),
                  dtb, par, par, par,
                  pl.BlockSpec((1, D_INNER), lambda c: (0, 0))],
        out_specs=[wide, st, st, wide, dtb, pl.BlockSpec((8, GROUP_DIM), lambda c: (0, 0)), par, par, par],
        scratch_shapes=[pltpu.VMEM((SSM_GROUPS, GROUP_DIM, SSM_STATE), F32),
                        pltpu.VMEM((CHUNK, LANES), F32), pltpu.VMEM((CHUNK, LANES), F32),
                        pltpu.VMEM((LANES, CHUNK), F32)],
        args=[dyn, act, act, act, z, y_pre, states, dtp, bias_p, a_p, d_p, normw], name=name, ride=ride)


def _attn_probs(q, kp, kc, sink, n):
    sp = _dot(q, kp, NT)
    sc = _dot(q, kc, NT)
    i = lax.broadcasted_iota(jnp.int32, sp.shape, 0) & (WINDOW - 1)
    j = lax.broadcasted_iota(jnp.int32, sp.shape, 1)
    sp = jnp.where(jnp.logical_and(j > i, n > 0), sp, NEG)
    sc = jnp.where(j <= i, sc, NEG)
    m = jnp.maximum(jnp.maximum(jnp.max(sp, axis=1, keepdims=True), jnp.max(sc, axis=1, keepdims=True)), sink)
    pp = jnp.exp(sp - m)
    pc = jnp.exp(sc - m)
    ps = jnp.exp(sink - m)
    inv = 1.0 / (jnp.sum(pp, axis=1, keepdims=True) + jnp.sum(pc, axis=1, keepdims=True) + ps)
    return pp * inv, pc * inv, ps * inv


def attn_fwd(qt, kt, vt, sink_rows, name, ride=None):
    s = qt.shape[1]
    nb = s // WINDOW
    rows = Q_PER_KV * WINDOW

    def body(q_ref, kp_ref, kc_ref, vp_ref, vc_ref, sk_ref, o_ref):
        n = pl.program_id(0)
        for h in range(N_KV_HEADS):
            heads = pl.ds(h * Q_PER_KV, Q_PER_KV)
            q = q_ref[heads].reshape(rows, ATT_HEAD_DIM)
            pp, pc, _ = _attn_probs(q, kp_ref[h], kc_ref[h], sk_ref[h], n)
            o = _dot(pp.astype(BF16), vp_ref[h]) + _dot(pc.astype(BF16), vc_ref[h])
            o_ref[heads] = o.reshape(Q_PER_KV, WINDOW, ATT_HEAD_DIM).astype(BF16)

    qsp = pl.BlockSpec((N_Q_HEADS, WINDOW, ATT_HEAD_DIM), lambda n: (0, n, 0))
    prev = pl.BlockSpec((N_KV_HEADS, WINDOW, ATT_HEAD_DIM), lambda n: (0, jnp.maximum(n - 1, 0), 0))
    cur = pl.BlockSpec((N_KV_HEADS, WINDOW, ATT_HEAD_DIM), lambda n: (0, n, 0))
    return _pcall(body, out_shape=[_sds(qt.shape, BF16)], grid=(nb,),
                  in_specs=[qsp, prev, cur, prev, cur, pl.BlockSpec((N_KV_HEADS, rows, 1), lambda n: (0, 0, 0))],
                  out_specs=[qsp], args=[qt, kt, kt, vt, vt, sink_rows], name=name, ride=ride)


def attn_bwd(qt, kt, vt, sink_rows, dot_, name, ride=None):
    s = qt.shape[1]
    nb = s // WINDOW
    rows = Q_PER_KV * WINDOW

    def body(q_ref, kp_ref, kc_ref, vp_ref, vc_ref, sk_ref, do_ref, dq_ref, dk_ref, dv_ref, ds_ref, kacc, vacc):
        n = pl.program_id(0)

        @pl.when(n == 0)
        def _():
            kacc[...] = jnp.zeros(kacc.shape, F32)
            vacc[...] = jnp.zeros(vacc.shape, F32)

        @pl.when(n < nb)
        def _():
            for h in range(N_KV_HEADS):
                heads = pl.ds(h * Q_PER_KV, Q_PER_KV)
                q = q_ref[heads].reshape(rows, ATT_HEAD_DIM)
                do = do_ref[heads].reshape(rows, ATT_HEAD_DIM)
                kp, kc, vp, vc = kp_ref[h], kc_ref[h], vp_ref[h], vc_ref[h]
                pp, pc, ps = _attn_probs(q, kp, kc, sk_ref[h], n)
                dpp = _dot(do, vp, NT)
                dpc = _dot(do, vc, NT)
                delta = jnp.sum(pp * dpp, axis=1, keepdims=True) + jnp.sum(pc * dpc, axis=1, keepdims=True)
                dsp = (pp * (dpp - delta)).astype(BF16)
                dsc = (pc * (dpc - delta)).astype(BF16)
                dq = _dot(dsp, kp) + _dot(dsc, kc)
                dq_ref[heads] = dq.reshape(Q_PER_KV, WINDOW, ATT_HEAD_DIM)
                dk_ref[h] = kacc[h] + _dot(dsp, q, TN)
                dv_ref[h] = vacc[h] + _dot(pp.astype(BF16), do, TN)
                kacc[h] = _dot(dsc, q, TN)
                vacc[h] = _dot(pc.astype(BF16), do, TN)
                dsk = -ps * delta
                sub = lax.broadcasted_iota(jnp.int32, (8, LANES), 0)
                tile = jnp.zeros((8, LANES), F32)
                for j in range(Q_PER_KV):
                    tile += jnp.where(sub == j, jnp.sum(dsk[j * WINDOW:(j + 1) * WINDOW, :], axis=0, keepdims=True),
                                      0.0)
                ds_ref[h, 0] = tile

        @pl.when(n == nb)
        def _():
            dk_ref[...] = kacc[...]
            dv_ref[...] = vacc[...]
            ds_ref[...] = jnp.zeros(ds_ref.shape, F32)

    last = nb - 1
    qsp = pl.BlockSpec((N_Q_HEADS, WINDOW, ATT_HEAD_DIM), lambda n: (0, jnp.minimum(n, last), 0))
    prev = pl.BlockSpec((N_KV_HEADS, WINDOW, ATT_HEAD_DIM), lambda n: (0, jnp.clip(n - 1, 0, last), 0))
    cur = pl.BlockSpec((N_KV_HEADS, WINDOW, ATT_HEAD_DIM), lambda n: (0, jnp.minimum(n, last), 0))
    dkv = pl.BlockSpec((N_KV_HEADS, WINDOW, ATT_HEAD_DIM), lambda n: (0, jnp.maximum(n - 1, 0), 0))
    f = lambda shape: _sds(shape, F32)
    acc = pltpu.VMEM((N_KV_HEADS, WINDOW, ATT_HEAD_DIM), F32)
    return _pcall(
        body, out_shape=[f(qt.shape), f(kt.shape), f(vt.shape), f((N_KV_HEADS, nb + 1, 8, LANES))],
        grid=(nb + 1,),
        in_specs=[qsp, prev, cur, prev, cur, pl.BlockSpec((N_KV_HEADS, rows, 1), lambda n: (0, 0, 0)), qsp],
        out_specs=[qsp, dkv, dkv, pl.BlockSpec((N_KV_HEADS, 1, 8, LANES), lambda n: (0, n, 0, 0))],
        scratch_shapes=[acc, acc], args=[qt, kt, kt, vt, vt, sink_rows, dot_], name=name, ride=ride)


def loss_head(x, w, tgt, name):
    s, d = x.shape
    tm = _row_tile(s, 256)

    def body(x_ref, w_ref, t_ref, loss_ref, dx_ref, dw_ref):
        i = pl.program_id(0)
        xf = x_ref[...]
        wv = w_ref[...]
        r = lax.rsqrt(jnp.mean(xf * xf, axis=-1, keepdims=True) + EPS)
        xhat = xf * r
        e = xhat * wv - t_ref[...]
        part = 0.5 * jnp.sum(jnp.mean(e * e, axis=-1, keepdims=True), axis=0, keepdims=True)
        dy = e * (1.0 / d)
        dxhat = dy * wv
        dx_ref[...] = r * (dxhat - xhat * jnp.mean(dxhat * xhat, axis=-1, keepdims=True))
        col = jnp.sum(dy * xhat, axis=0, keepdims=True)

        @pl.when(i == 0)
        def _():
            loss_ref[...] = jnp.broadcast_to(part, (1, LANES))
            dw_ref[...] = col

        @pl.when(i > 0)
        def _():
            loss_ref[...] += jnp.broadcast_to(part, (1, LANES))
            dw_ref[...] += col

    row = pl.BlockSpec((tm, d), lambda i: (i, 0))
    vec = pl.BlockSpec((1, d), lambda i: (0, 0))
    return _pcall(body, out_shape=[_sds((1, LANES), F32), _sds((s, d), F32), _sds((1, d), F32)], grid=(s // tm,),
                  in_specs=[row, vec, row], out_specs=[pl.BlockSpec((1, LANES), lambda i: (0, 0)), row, vec],
                  args=[x, w.reshape(1, d), tgt], name=name)


ELEMWISE_TILE = 720 * 1024


def _tile_rows(r, c, max_elems=262144, mult=16):
    best = None
    for t in range(mult, r + 1, mult):
        if r % t == 0 and t * c <= max_elems:
            best = t
    return best or r


def add_pair(xhs, ps, c_idx, name):
    n = len(xhs)
    _, r, c = xhs[0].shape
    tr = _tile_rows(r, c, max_elems=ELEMWISE_TILE)

    def body(c_ref, *refs):
        for x_ref, p_ref, o_ref in zip(refs[:n], refs[n:2 * n], refs[2 * n:]):
            o_ref[...] = (x_ref[0].astype(F32) + p_ref[...].astype(F32)).astype(BF16)

    blk = pl.BlockSpec((tr, c), lambda i, cr: (i, 0))
    return pl.pallas_call(
        body, out_shape=tuple([_sds((r, c), BF16)] * n),
        grid_spec=pltpu.PrefetchScalarGridSpec(
            num_scalar_prefetch=1, grid=(r // tr,),
            in_specs=[pl.BlockSpec((1, tr, c), lambda i, cr: (cr[0], i, 0))] * n + [blk] * n,
            out_specs=tuple([blk] * n)),
        name=name, compiler_params=_cp(1))(c_idx, *xhs, *ps)


def sum_chips(qs, owns, chip_idx, name):
    n = len(qs)
    _, r, c = qs[0].shape
    tr = _tile_rows(r, c, max_elems=ELEMWISE_TILE // max(1, n - 1))

    def body(k_ref, *refs):
        k = k_ref[0]
        for q_ref, own_ref, o_ref in zip(refs[:n], refs[n:2 * n], refs[2 * n:]):
            mine = own_ref[0].astype(F32)
            tot = None
            for j in range(N_CHIPS):
                term = jnp.where(k == j, mine, q_ref[j].astype(F32))
                tot = term if tot is None else tot + term
            o_ref[...] = tot

    return pl.pallas_call(
        body, out_shape=tuple([_sds((r, c), F32)] * n),
        grid_spec=pltpu.PrefetchScalarGridSpec(
            num_scalar_prefetch=1, grid=(r // tr,),
            in_specs=([pl.BlockSpec((N_CHIPS, tr, c), lambda i, kr: (0, i, 0))] * n
                      + [pl.BlockSpec((1, tr, c), lambda i, kr: (kr[0], i, 0))] * n),
            out_specs=tuple([pl.BlockSpec((tr, c), lambda i, kr: (i, 0))] * n)),
        name=name, compiler_params=_cp(1))(chip_idx, *qs, *owns)


def adamw(w, g, m, v, name):
    r, c = w.shape
    tr = _tile_rows(r, c, mult=8)
    c1 = 1.0 / (1.0 - ADAM_B1 ** ADAM_STEP)
    c2 = 1.0 / (1.0 - ADAM_B2 ** ADAM_STEP)

    def body(w_ref, g_ref, m_ref, v_ref, d_ref, mo_ref, vo_ref):
        gf = g_ref[...]
        mn = ADAM_B1 * m_ref[...] + (1.0 - ADAM_B1) * gf
        vn = ADAM_B2 * v_ref[...] + (1.0 - ADAM_B2) * (gf * gf)
        mo_ref[...] = mn
        vo_ref[...] = vn
        d_ref[...] = -ADAM_LR * ((mn * c1) / (jnp.sqrt(vn * c2) + ADAM_EPS) + ADAM_WD * w_ref[...])

    blk = pl.BlockSpec((tr, c), lambda i: (i, 0))
    out = _sds((r, c), F32)
    return _pcall(body, out_shape=[out, out, out], grid=(r // tr,), in_specs=[blk] * 4, out_specs=[blk] * 3,
                  args=[w, g, m, v], name=name)


WEIGHTS = ['norm_w', 'ffn_w_gate', 'ffn_w_up', 'ffn_w_down', 'ssm_w_in', 'ssm_conv_w', 'ssm_conv_b', 'ssm_dt_bias',
           'ssm_a_log', 'ssm_d', 'ssm_norm_w', 'ssm_w_out', 'kv_norm_w', 'w_k', 'b_k', 'w_v', 'b_v', 'attn_w_q',
           'attn_b_q', 'attn_sinks', 'attn_w_o', 'attn_b_o', 'final_norm_w']
BIG = ['ffn_w_gate', 'ffn_w_up', 'ffn_w_down', 'ssm_w_in', 'ssm_w_out', 'w_k', 'w_v', 'attn_w_q', 'attn_w_o']
TRANSPOSED = ('ffn_w_gate', 'ffn_w_up', 'ssm_w_in')
SMALL = [n for n in WEIGHTS if n not in BIG]
SMALL_SHARDED = {'norm_w': 2, 'ssm_conv_w': 2, 'ssm_conv_b': 1, 'ssm_norm_w': 1}
ROW_ALIGN = 8 * LANES


def _pack_rows(parts):
    flat = jnp.concatenate([p.reshape(-1).astype(F32) for p in parts])
    pad = (-flat.size) % ROW_ALIGN
    return jnp.pad(flat, (0, pad)).reshape(-1, LANES)


def _unpack_rows(buf, shapes):
    flat = buf.reshape(-1)
    out, pos = [], 0
    for shp in shapes:
        size = math.prod(shp)
        out.append(flat[pos:pos + size].reshape(shp))
        pos += size
    return out


def _as2d(a):
    return a.reshape(-1, a.shape[-1])


def _heads_major(t, n_heads):
    s = t.shape[0]
    return t.reshape(s, n_heads, ATT_HEAD_DIM).transpose(1, 0, 2)


def _tokens_major(t):
    h, s, dh = t.shape
    return t.transpose(1, 0, 2).reshape(s, h * dh)


def _pad_lanes(v):
    return jnp.pad(v.reshape(1, -1), ((0, 0), (0, LANES - v.size)))


def _chips_first(t):
    return t.swapaxes(0, 1).reshape((-1,) + t.shape[3:])


def _parts_first(t, rows):
    return t.reshape((N_CHIPS, N_CORES, rows) + t.shape[1:]).swapaxes(0, 1)


def kernel(*args):
    names = (['x'] + WEIGHTS + ['loss_target'] + ['m_' + n for n in WEIGHTS] + ['v_' + n for n in WEIGHTS])
    a = dict(zip(names, args))
    for n in TRANSPOSED:
        for pre in ('', 'm_', 'v_'):
            a[pre + n] = a[pre + n].swapaxes(-1, -2)
    xi, yi, ci = lax.axis_index("x"), lax.axis_index("y"), lax.axis_index("c")
    chip = 2 * xi + yi
    south = ci == 0
    c_idx = jnp.reshape(ci, (1,)).astype(jnp.int32)
    chip_idx = jnp.reshape(chip, (1,)).astype(jnp.int32)
    x0 = a['x'][0]
    s = x0.shape[0]
    cos, sin = rope_tables(s)

    def own_slot(full, mine):
        return lax.dynamic_update_slice_in_dim(full, mine[:, None], chip, axis=1)

    def ffn_shard(l, i, src):
        return [src[n][l, i].astype(BF16).reshape(N_CORES, FF_PART, D_MODEL)
                for n in ('ffn_w_gate', 'ffn_w_up', 'ffn_w_down')]

    def own_slots(fulls, mines):
        return [own_slot(f, m) for f, m in zip(fulls, mines)]
    small_names = list(SMALL_SHARDED)
    small_sh = _pack_rows([a[n] for n in small_names])
    small_sh = small_sh.reshape(N_CORES, small_sh.shape[0] // 2, LANES)
    sh00 = ffn_shard(0, 0, a)
    (first_flight,), started = split_start([sh00 + [small_sh]], "gather", "gather_start_first")
    held = lax.optimization_barrier((started, {n: a[n] for n in BIG}))[1]
    sh01, sh10, sh11 = ffn_shard(0, 1, held), ffn_shard(1, 0, held), ffn_shard(1, 1, held)
    w_in_sh = jnp.pad(held['ssm_w_in'][0], ((0, IN_SHARD_PAD - IN_SHARD), (0, 0))).astype(BF16).reshape(
        N_CORES, IN_SHARD_PAD // 2, D_MODEL)
    w_out_sh = held['ssm_w_out'][0].astype(BF16).reshape(N_CORES, 256, D_MODEL)
    attn_sh = jnp.stack([held['attn_w_q'][0], held['attn_w_o'][0]]).astype(BF16)
    kv_sh = jnp.stack([held['w_k'], held['w_v']]).astype(BF16)
    rest_flights, all_started = split_start([[w_in_sh, kv_sh], [w_out_sh], sh01, sh10, [attn_sh], sh11], "gather",
                                            "gather_start_rest")
    in_flight = [first_flight] + rest_flights

    def arrive(idx, after, tag):
        return forward_cores(split_arrive(in_flight[idx], "gather", after, "gather_arrive_" + tag))

    first = run_exchange(arrive(0, all_started, "first"), "gather_hop_first")
    w00 = own_slots(first[:3], sh00)
    smalls = own_slot(first[3], small_sh)
    p = {}
    per_chip = [_unpack_rows(smalls[:, k], [a[n].shape for n in small_names]) for k in range(N_CHIPS)]
    for idx, n in enumerate(small_names):
        p[n] = jnp.concatenate([per_chip[k][idx] for k in range(N_CHIPS)], axis=SMALL_SHARDED[n])
    nw = p['norm_w']
    conv_w, conv_b, ssm_nw = p['ssm_conv_w'][0], p['ssm_conv_b'][0], p['ssm_norm_w'][0].reshape(1, D_INNER)

    h00 = rmsnorm_fwd(x0, nw[0, 0], "norm_in")
    x1, h01, gu00 = ffn_fwd(h00, x0, *w00, [nw[0, 1]], "ffn_fwd_00")
    w_in_g, kv_g = run_exchange(arrive(1, x1, "in"), "gather_hop_in")
    w_in_t = _chips_first(own_slot(w_in_g, w_in_sh)).reshape(N_CHIPS, IN_SHARD_PAD, D_MODEL)[:, :IN_SHARD].reshape(
        IN_PROJ_DIM, D_MODEL)
    w_dt_t = jnp.pad(w_in_t[D_INNER + CONV_DIM:], ((0, LANES - SSM_HEADS), (0, 0)))
    kv_g = own_slot(kv_g, kv_sh)
    w_k, w_v = kv_g[0].reshape(D_MODEL, KV_DIM), kv_g[1].reshape(D_MODEL, KV_DIM)

    zz = mm_nt(h01, w_in_t, "ssm_in_z", n=D_INNER)
    xbc = mm_nt(h01, w_in_t, "ssm_in_xbc", n=CONV_DIM, row0=D_INNER)
    dtp = mm_nt(h01, w_dt_t, "ssm_in_dt")
    act = conv_fwd(xbc, conv_w, conv_b, "ssm_conv")
    bias_p = _pad_lanes(a['ssm_dt_bias'][0])
    a_p = _pad_lanes(-jnp.exp(a['ssm_a_log'][0]))
    d_p = _pad_lanes(a['ssm_d'][0])
    (yn, y_pre, states), (w_out_g,) = ssd_fwd(act, zz, dtp, bias_p, a_p, d_p, ssm_nw, "ssd_fwd",
                                              ride=arrive(2, act, "out"))
    w_out = _chips_first(own_slot(w_out_g, w_out_sh))
    (x2, h02), w01 = mm_res(yn, w_out, x1, "ssm_out", norm_ws=[nw[0, 2]], ride=arrive(3, yn, "01"))
    w01 = own_slots(w01, sh01)
    x3, hkv, h10, gu01 = ffn_fwd(h02, x2, *w01, [a['kv_norm_w'], nw[1, 0]], "ffn_fwd_01")
    w10 = own_slots(run_exchange(arrive(4, x3, "10"), "gather_hop_10"), sh10)

    k_rot = rope_apply(mm_nn(hkv, w_k, "kv_k", bias=a['b_k']), cos, sin, "rope_k")
    v = mm_nn(hkv, w_v, "kv_v", bias=a['b_v'], out_dtype=BF16)
    kt = _heads_major(k_rot, N_KV_HEADS)
    vt = _heads_major(v, N_KV_HEADS)

    (x4, h11, gu10), (attn_g,) = ffn_fwd(h10, x3, *w10, [nw[1, 1]], "ffn_fwd_10", ride=arrive(5, v, "attn"))
    attn_g = own_slot(attn_g, attn_sh)
    w_q, w_o = attn_g[0].reshape(D_MODEL, D_MODEL), attn_g[1].reshape(D_MODEL, D_MODEL)
    scale = 1.0 / math.sqrt(ATT_HEAD_DIM)
    q_rot = rope_apply(mm_nn(h11, w_q, "attn_q", bias=a['attn_b_q'][0]), cos, sin, "rope_q", scale=scale)
    qt = _heads_major(q_rot, N_Q_HEADS)
    sink_rows = jnp.repeat(a['attn_sinks'][0].reshape(N_KV_HEADS, Q_PER_KV), WINDOW, axis=1).reshape(
        N_KV_HEADS, Q_PER_KV * WINDOW, 1)
    (ot,) = attn_fwd(qt, kt, vt, sink_rows, "attn_fwd")
    o = _tokens_major(ot)
    (x5, h12), w11 = mm_res(o, w_o, x4, "attn_out", bias=a['attn_b_o'][0], norm_ws=[nw[1, 2]],
                            ride=arrive(6, ot, "11"))
    w11 = own_slots(w11, sh11)
    x6, gu11 = ffn_fwd(h12, x5, *w11, [], "ffn_fwd_11")

    loss_v, dx6, d_final = loss_head(x6, a['final_norm_w'], a['loss_target'][0], "loss_head")
    loss = lax.psum(loss_v[0, 0], ("x", "y", "c"))
    g = {'final_norm_w': d_final[0]}

    def same_shape(xs, ys):
        runs = []
        for xv, yv in zip(xs, ys):
            if runs and runs[-1][0][0].shape == xv.shape:
                runs[-1][0].append(xv)
                runs[-1][1].append(yv)
            else:
                runs.append(([xv], [yv]))
        return runs

    def pre_reduce(grads, sib, tag):
        out = []
        for idx, (grp, sbs) in enumerate(same_shape(grads, list(sib))):
            ts = add_pair([gr.reshape(2, -1, gr.shape[-1]) for gr in grp], [_as2d(sb) for sb in sbs], c_idx,
                          "rs_add_%s_%d" % (tag, idx))
            out += [t.reshape(gr.shape[1:]) for t, gr in zip(ts, grp)]
        return out

    def chip_sum(landed, parts, tag):
        out = []
        for idx, (qs, owns) in enumerate(same_shape(list(landed), parts)):
            ts = sum_chips([q.reshape(N_CHIPS, -1, q.shape[-1]) for q in qs],
                           [own.reshape(N_CHIPS, -1, own.shape[-1]) for own in owns], chip_idx,
                           "rs_sum_%s_%d" % (tag, idx))
            out += [t.reshape(q.shape[1:]) for t, q in zip(ts, qs)]
        return out

    dnw = [[None] * 3 for _ in range(2)]
    sums = {}

    def trade(key):
        return swap_cores(sums[key], False)

    dx5, dnw12, *g11 = ffn_bwd(dx6, h12, x5, nw[1, 2], gu11, *w11, "ffn_bwd_11")
    dnw[1][2] = dnw12[0]
    (d_wo, g['attn_b_o']), sib11 = mm_tn(o, dx5, "attn_dwo", col_sum=True, ride=swap_cores(g11, True))
    t11 = pre_reduce(g11, sib11, "11")
    do = mm_nt(dx5, w_o, "attn_do", out_dtype=BF16)
    (dqt, dkt, dvt, dsink), land11 = attn_bwd(qt, kt, vt, sink_rows, _heads_major(do, N_Q_HEADS), "attn_bwd",
                                             ride=scatter_chips(t11[:2]))
    g['attn_sinks'] = jnp.sum(dsink[:, :, :Q_PER_KV, 0], axis=1).reshape(N_Q_HEADS)
    dq_pre = rope_apply(_tokens_major(dqt), cos, sin, "rope_dq", inverse=True, scale=scale, out_dtype=F32)
    d_wq, g['attn_b_q'] = mm_tn(h11, dq_pre, "attn_dwq", col_sum=True)
    g_attn = [jnp.stack([d_wq.reshape(N_CHIPS, 256, D_MODEL), d_wo.reshape(N_CHIPS, 256, D_MODEL)])]
    (dx4, dnw[1][1]), sib_attn = mm_rms_bwd([(dq_pre, 0, w_q, 0, D_MODEL, "nt")], dx5, x4, nw[1, 1], "attn_bwd_dh",
                                            ride=swap_cores(g_attn, True))
    t_attn = pre_reduce(g_attn, sib_attn, "attn")
    (dx3, dnw10, *g10), landed = ffn_bwd(dx4, h10, x3, nw[1, 0], gu10, *w10, "ffn_bwd_10",
                                         ride=scatter_chips(t_attn + t11[2:]))
    dnw[1][0] = dnw10[0]
    sums['attn'] = chip_sum(landed[:1], t_attn, "attn")
    sums['11'] = chip_sum(list(land11) + list(landed[1:]), t11, "11")
    dk_pre = rope_apply(_tokens_major(dkt), cos, sin, "rope_dk", inverse=True, out_dtype=F32)
    dv = _tokens_major(dvt)
    (d_wk, g['b_k']), sib10 = mm_tn(hkv, dk_pre, "kv_dwk", col_sum=True, ride=swap_cores(g10, True))
    t10 = pre_reduce(g10, sib10, "10")
    d_wv, g['b_v'] = mm_tn(hkv, dv, "kv_dwv", col_sum=True)
    g_kv = [jnp.stack([d_wk.reshape(N_CHIPS, 256, KV_DIM), d_wv.reshape(N_CHIPS, 256, KV_DIM)])]
    (dx3, g['kv_norm_w']), sib_kv = mm_rms_bwd(
        [(dk_pre, 0, w_k, 0, KV_DIM, "nt"), (dv, 0, w_v, 0, KV_DIM, "nt")], dx3, x3, a['kv_norm_w'], "kv_bwd_dh",
        ride=swap_cores(g_kv, True))
    t_kv = pre_reduce(g_kv, sib_kv, "kv")
    (dx2, dnw02, *g01), landed = ffn_bwd(dx3, h02, x2, nw[0, 2], gu01, *w01, "ffn_bwd_01",
                                         ride=join(scatter_chips(t10 + t_kv), trade('11'), trade('attn')))
    dnw[0][2] = dnw02[0]
    sums['10'] = chip_sum(landed[:3], t10, "10")
    sums['kv'] = chip_sum(landed[3:4], t_kv, "kv")
    theirs = {'11': landed[4:7], 'attn': landed[7:]}
    d_wout, sib01 = mm_tn(yn, dx2, "ssm_dwout", ride=swap_cores(g01, True))
    t01 = pre_reduce(g01, sib01, "01")
    dyn = mm_nt(dx2, w_out, "ssm_dyn")
    (dxs, db_, dc_, dz, ddt, d_ssm_nw, d_bias, d_a, d_d), landed = ssd_bwd(
        dyn, act, zz, y_pre, states, dtp, bias_p, a_p, d_p, ssm_nw, "ssd_bwd",
        ride=join(scatter_chips(t01[:2]), trade('10'), trade('kv')))
    land01 = list(landed[:2])
    theirs['10'], theirs['kv'] = landed[2:5], landed[5:]
    g['ssm_norm_w'] = d_ssm_nw[:SSM_GROUPS].reshape(D_INNER)
    g['ssm_dt_bias'] = d_bias[0, :SSM_HEADS]
    g['ssm_a_log'] = d_a[0, :SSM_HEADS] * a_p[0, :SSM_HEADS]
    g['ssm_d'] = d_d[0, :SSM_HEADS]
    (dxbc, g['ssm_conv_w'], g['ssm_conv_b']), landed = conv_bwd(dxs, db_, dc_, xbc, conv_w, conv_b, "ssm_conv_bwd",
                                                                ride=scatter_chips(t01[2:]))
    sums['01'] = chip_sum(land01 + list(landed), t01, "01")
    d_win = mm_tn(dz, h01, "ssm_dwz", rows=IN_PROJ_DIM)
    d_win = mm_tn(dxbc, h01, "ssm_dwxbc", into=d_win, rows=IN_PROJ_DIM, row0=D_INNER)
    d_win = mm_tn(ddt, h01, "ssm_dwdt", into=d_win, rows=IN_PROJ_DIM, row0=D_INNER + CONV_DIM, m_valid=SSM_HEADS)
    d_win = jnp.pad(d_win.reshape(N_CHIPS, IN_SHARD, D_MODEL), ((0, 0), (0, IN_SHARD_PAD - IN_SHARD), (0, 0)))
    g_ssm = [_parts_first(d_win.reshape(-1, D_MODEL), IN_SHARD_PAD // 2), _parts_first(d_wout, 256)]
    kb = 1024
    terms = ([(dz, j, w_in_t, j, kb, "nn") for j in range(D_INNER // kb)]
             + [(dxbc, j, w_in_t, D_INNER // kb + j, kb, "nn") for j in range(CONV_DIM // kb)]
             + [(ddt, 0, w_dt_t, 0, LANES, "nn")])
    (dx1, dnw[0][1]), sib_ssm = mm_rms_bwd(terms, dx2, x1, nw[0, 1], "ssm_bwd_dh", ride=swap_cores(g_ssm, True))
    t_ssm = pre_reduce(g_ssm, sib_ssm, "ssm")
    (grad_x, dnw00, *g00), landed = ffn_bwd(dx1, h00, x0, nw[0, 0], gu00, *w00, "ffn_bwd_00",
                                            ride=join(scatter_chips(t_ssm), trade('01')))
    dnw[0][0] = dnw00[0]
    sums['ssm'] = chip_sum(landed[:2], t_ssm, "ssm")
    theirs['01'] = landed[2:]
    landed = run_exchange(join(swap_cores(g00, True), trade('ssm')), "rs_swap_00")
    t00 = pre_reduce(g00, landed[:3], "00")
    theirs['ssm'] = landed[3:]

    def both(key):
        return [(jnp.where(south, m_, t_), jnp.where(south, t_, m_)) for m_, t_ in zip(sums[key], theirs[key])]

    g['norm_w'] = jnp.stack([jnp.stack(r) for r in dnw])
    red = all_reduce_small(_pack_rows([g[n] for n in SMALL]), "reduce_vectors")

    t00 = lax.optimization_barrier((red, t00))[1]
    (flight00,), flying = split_start([t00], "scatter", "rs_scatter_00_start")

    def held(val):
        return lax.optimization_barrier((flying, val))[1]

    delta, new_m, new_v, gw = {}, {}, {}, {}
    ffn_names = ('ffn_w_gate', 'ffn_w_up', 'ffn_w_down')
    full = {key: both(key) for key in ('attn', 'kv', 'ssm')}
    lo, hi = full['attn'][0]
    gw['attn_w_q'], gw['attn_w_o'] = lo[None], hi[None]
    lo, hi = full['kv'][0]
    gw['w_k'], gw['w_v'] = lo, hi
    lo, hi = full['ssm'][0]
    gw['ssm_w_in'] = jnp.concatenate([lo, hi], axis=0)[:IN_SHARD][None]
    lo, hi = full['ssm'][1]
    gw['ssm_w_out'] = jnp.concatenate([lo, hi], axis=0)[None]

    for n, t in zip(SMALL, _unpack_rows(red, [g[n].shape for n in SMALL])):
        if n in SMALL_SHARDED:
            ax = SMALL_SHARDED[n] - (a[n].ndim - t.ndim)
            width = a[n].shape[SMALL_SHARDED[n]]
            t = lax.dynamic_slice_in_dim(t, chip * width, width, axis=ax)
        gw[n] = t.reshape(a[n].shape)

    def update(n):
        d, mo, vo = adamw(_as2d(a[n]), held(_as2d(gw[n])), _as2d(a['m_' + n]), _as2d(a['v_' + n]), "adamw_" + n)
        delta[n], new_m[n], new_v[n] = d.reshape(a[n].shape), mo.reshape(a[n].shape), vo.reshape(a[n].shape)

    for n in BIG:
        if n not in ffn_names:
            update(n)
    shapes = [a[n].shape for n in SMALL]
    packed = [_pack_rows([src[n] for n in SMALL]) for src in
              (a, gw, {n: a['m_' + n] for n in SMALL}, {n: a['v_' + n] for n in SMALL})]
    outs = adamw(*packed, "adamw_vectors")
    for dst, buf in zip((delta, new_m, new_v), outs):
        for n, t in zip(SMALL, _unpack_rows(buf, shapes)):
            dst[n] = t
    for key in ('01', '10', '11'):
        sums[key] = held(list(sums[key]))
    rest = [both(key) for key in ('01', '10', '11')]
    done = lax.optimization_barrier((outs[0], [delta[n] for n in BIG if n not in ffn_names], rest))[0]
    land00 = split_arrive(flight00, "scatter", done, "rs_scatter_00_arrive")
    sums['00'] = chip_sum(land00, t00, "00")
    theirs['00'] = run_exchange(trade('00'), "rs_trade_00")
    blocks = [both('00')] + rest
    for t, n in enumerate(ffn_names):
        gw[n] = jnp.concatenate([piece for blk in blocks for piece in blk[t]], axis=0).reshape(a[n].shape)
        update(n)
    for n in TRANSPOSED:
        for dst in (gw, delta, new_m, new_v):
            dst[n] = dst[n].swapaxes(-1, -2)

    return (loss, grad_x[None], *[gw[n] for n in WEIGHTS], *[delta[n] for n in WEIGHTS],
            *[new_m[n] for n in WEIGHTS], *[new_v[n] for n in WEIGHTS])
```

```python
import math

import jax
import jax.numpy as jnp
from jax import lax
from jax.experimental import pallas as pl
from jax.experimental.pallas import tpu as pltpu

F32 = jnp.float32
BF16 = jnp.bfloat16

D_MODEL = 1024
D_INNER = 2048
SSM_HEADS = 32
SSM_GROUPS = 4
HEADS_PER_GROUP = SSM_HEADS // SSM_GROUPS
SSM_HEAD_DIM = 64
SSM_STATE = 128
GROUP_DIM = D_INNER // SSM_GROUPS
CONV_DIM = D_INNER + 2 * SSM_GROUPS * SSM_STATE
CONV_WIDTH = 4
CHUNK = 128
ATT_HEAD_DIM = 64
N_Q_HEADS = 16
N_KV_HEADS = 4
Q_PER_KV = N_Q_HEADS // N_KV_HEADS
KV_DIM = N_KV_HEADS * ATT_HEAD_DIM
WINDOW = 128
ROPE_THETA = 10000.0
D_FF = 2816
N_CHIPS = 4
N_CORES = 2
FF_SHARD = D_FF // N_CHIPS
FF_PART = FF_SHARD // N_CORES
IN_PROJ_DIM = D_INNER + CONV_DIM + SSM_HEADS
IN_SHARD = IN_PROJ_DIM // N_CHIPS
IN_SHARD_PAD = 1312
EPS = 1e-5
NEG = -1e30
LANES = 128
VMEM_LIMIT = 56 * 1024 * 1024

ADAM_LR = 0.001
ADAM_B1 = 0.9
ADAM_B2 = 0.999
ADAM_EPS = 1e-08
ADAM_WD = 0.01
ADAM_STEP = 10

NN = ((1,), (0,))
NT = ((1,), (1,))
TN = ((0,), (0,))
MESH = pl.DeviceIdType.MESH
ANY = pl.BlockSpec(memory_space=pl.ANY)


def _dot(a, b, dims=NN, precision=None):
    return lax.dot_general(a, b, (dims, ((), ())), preferred_element_type=F32, precision=precision)


def _cp(n_grid):
    return pltpu.CompilerParams(dimension_semantics=("arbitrary",) * n_grid, vmem_limit_bytes=VMEM_LIMIT)


def _sigmoid(x):
    return 1.0 / (1.0 + jnp.exp(-x))


def _rms_fwd(xf, w):
    r = lax.rsqrt(jnp.mean(xf * xf, axis=-1, keepdims=True) + EPS)
    return xf * r * w


def _rms_bwd(dh, xf, w):
    r = lax.rsqrt(jnp.mean(xf * xf, axis=-1, keepdims=True) + EPS)
    xhat = xf * r
    dxhat = dh * w
    dx = r * (dxhat - xhat * jnp.mean(dxhat * xhat, axis=-1, keepdims=True))
    return dx, dh * xhat


def _row_tile(s, pref):
    return pref if s % pref == 0 else s


def _col_tile(n):
    for t in (1024, 768, 512, 256, 128):
        if n % t == 0:
            return t
    return n


def _sds(shape, dtype):
    return jax.ShapeDtypeStruct(tuple(shape), dtype)


class Exchange:
    def __init__(self, ins, out_shapes, sems, start, finish, inplace=False):
        self.ins, self.out_shapes, self.sems, self.start, self.finish = ins, out_shapes, sems, start, finish
        self.inplace = inplace


def _place():
    x, y, c = lax.axis_index("x"), lax.axis_index("y"), lax.axis_index("c")
    others = [(1 - x, y), (x, 1 - y), (1 - x, 1 - y)]
    return x, y, c, 2 * x + y, others


def _rc(src, dst, send_sem, recv_sem, dev):
    return pltpu.make_async_remote_copy(src_ref=src, dst_ref=dst, send_sem=send_sem, recv_sem=recv_sem,
                                        device_id=dev, device_id_type=MESH)


def scatter_chips(arrs):
    n = len(arrs)

    def copies(ins, outs, sems):
        send, recv = sems
        x, y, c, k, others = _place()
        out, land = [], []
        for a in range(n):
            for j, (px, py) in enumerate(others):
                out.append(_rc(ins[a].at[2 * px + py], outs[a].at[k], send.at[a, j], recv.at[a, j], (px, py, c)))
                blk = outs[a].at[2 * px + py]
                land.append(_rc(blk, blk, send.at[a, j], recv.at[a, j], (px, py, c)))
        return out, land

    def start(ins, outs, sems):
        for cp in copies(ins, outs, sems)[0]:
            cp.start()

    def finish(ins, outs, sems):
        out, land = copies(ins, outs, sems)
        for arrived in land:
            arrived.wait_recv()
        for cp in out:
            cp.wait_send()

    return Exchange(list(arrs), [_sds(a.shape, a.dtype) for a in arrs],
                    [pltpu.SemaphoreType.DMA((n, 3)), pltpu.SemaphoreType.DMA((n, 3))], start, finish)


def swap_cores(arrs, pick_other):
    n = len(arrs)

    def copies(ins, outs, sems):
        send, recv = sems
        x, y, c, _, _ = _place()
        return [_rc(ins[a].at[1 - c] if pick_other else ins[a], outs[a], send.at[a], recv.at[a], (x, y, 1 - c))
                for a in range(n)]

    def start(ins, outs, sems):
        for cp in copies(ins, outs, sems):
            cp.start()

    def finish(ins, outs, sems):
        for cp in copies(ins, outs, sems):
            cp.wait()

    shapes = [_sds(a.shape[1:] if pick_other else a.shape, a.dtype) for a in arrs]
    return Exchange(list(arrs), shapes, [pltpu.SemaphoreType.DMA((n,)), pltpu.SemaphoreType.DMA((n,))],
                    start, finish)


def join(*parts):
    parts = [p for p in parts if p is not None]
    if not parts:
        return None

    def split(refs, counts):
        out, pos = [], 0
        for cnt in counts:
            out.append(refs[pos:pos + cnt])
            pos += cnt
        return out

    n_in = [len(p.ins) for p in parts]
    n_out = [len(p.out_shapes) for p in parts]
    n_sem = [len(p.sems) for p in parts]

    def run(which):
        def go(ins, outs, sems):
            for p, i, o, s in zip(parts, split(ins, n_in), split(outs, n_out), split(sems, n_sem)):
                getattr(p, which)(i, o, s)
        return go

    return Exchange([a for p in parts for a in p.ins], [s for p in parts for s in p.out_shapes],
                    [s for p in parts for s in p.sems], run("start"), run("finish"))


def _pcall(body, *, out_shape, grid, in_specs, out_specs, args, name, scratch_shapes=(), ride=None, aliases=None):
    out_shape, out_specs, in_specs = tuple(out_shape), tuple(out_specs), list(in_specs)
    aliases = aliases or {}
    if ride is None:
        return pl.pallas_call(body, out_shape=out_shape, grid=grid, in_specs=in_specs, out_specs=out_specs,
                              scratch_shapes=list(scratch_shapes), input_output_aliases=aliases, name=name,
                              compiler_params=_cp(len(grid)))(*args)
    n_in, n_out, n_sc = len(args), len(out_shape), len(scratch_shapes)
    n_xi, n_xo = len(ride.ins), len(ride.out_shapes)

    def wrapped(*refs):
        pos = [0]

        def take(cnt):
            got = refs[pos[0]:pos[0] + cnt]
            pos[0] += cnt
            return got

        c_in, x_in, c_out, x_out, c_sc = take(n_in), take(n_xi), take(n_out), take(n_xo), take(n_sc)
        sems = refs[pos[0]:]
        first, last = True, True
        for d, size in enumerate(grid):
            first = jnp.logical_and(first, pl.program_id(d) == 0)
            last = jnp.logical_and(last, pl.program_id(d) == size - 1)

        @pl.when(first)
        def _():
            ride.start(x_in, x_out, sems)

        body(*c_in, *c_out, *c_sc)

        @pl.when(last)
        def _():
            ride.finish(x_in, x_out, sems)

    if ride.inplace:
        aliases = {**aliases, **{n_in + t: n_out + t for t in range(n_xi)}}
    res = pl.pallas_call(
        wrapped, out_shape=out_shape + tuple(ride.out_shapes), grid=grid,
        in_specs=in_specs + [ANY] * n_xi, out_specs=out_specs + (ANY,) * n_xo,
        scratch_shapes=list(scratch_shapes) + list(ride.sems), input_output_aliases=aliases, name=name,
        compiler_params=_cp(len(grid)))(*args, *ride.ins)
    return res[:n_out], res[n_out:]


def run_exchange(ex, name):
    n_xi, n_xo = len(ex.ins), len(ex.out_shapes)

    def body(*refs):
        ins, outs, sems = refs[:n_xi], refs[n_xi:n_xi + n_xo], refs[n_xi + n_xo:]
        ex.start(ins, outs, sems)
        ex.finish(ins, outs, sems)

    aliases = {t: t for t in range(n_xi)} if ex.inplace else {}
    return pl.pallas_call(body, out_shape=tuple(ex.out_shapes), in_specs=[ANY] * n_xi, out_specs=(ANY,) * n_xo,
                          scratch_shapes=list(ex.sems), input_output_aliases=aliases, name=name)(*ex.ins)


HBM_SPEC = pl.BlockSpec(memory_space=pltpu.HBM)
SEM_SPEC = pl.BlockSpec(memory_space=pltpu.SEMAPHORE)
EFFECT = pltpu.SideEffectType.DATAFLOW_SIDE_EFFECTING


def _route(kind, src, dst, c, k, peer):
    if kind == "gather":
        return src.at[c], dst.at[c, k], dst.at[c, peer]
    return src.at[peer], dst.at[k], dst.at[peer]


def split_start(batches, kind, name):
    flat = [a for batch in batches for a in batch]
    n, nb = len(flat), len(batches)
    lands = [lax.empty((2, N_CHIPS) + a.shape[1:] if kind == "gather" else a.shape, a.dtype) for a in flat]

    def body(*refs):
        srcs, dsts, sems, token = refs[:n], refs[n:2 * n], refs[2 * n:2 * n + 2 * nb], refs[-1]
        x, y, c, k, others = _place()
        pos = 0
        for b, batch in enumerate(batches):
            for a in range(len(batch)):
                for j, (px, py) in enumerate(others):
                    src, dst, _ = _route(kind, srcs[pos], dsts[pos], c, k, 2 * px + py)
                    _rc(src, dst, sems[2 * b].at[3 * a + j], sems[2 * b + 1].at[3 * a + j], (px, py, c)).start()
                pos += 1
        token[...] = jnp.zeros(token.shape, token.dtype)

    sem_shapes = [pltpu.SemaphoreType.DMA((3 * len(batch),)) for batch in batches for _ in range(2)]
    thru = [pltpu.HBM(a.shape, a.dtype) for a in flat] + [pltpu.HBM(l.shape, l.dtype) for l in lands]
    res = pl.pallas_call(
        body, name=name, out_shape=tuple(sem_shapes + thru + [_sds((8, LANES), F32)]),
        in_specs=[HBM_SPEC] * (2 * n),
        out_specs=tuple([SEM_SPEC] * (2 * nb) + [HBM_SPEC] * (2 * n) + [pl.BlockSpec(memory_space=pltpu.VMEM)]),
        input_output_aliases={t: 2 * nb + t for t in range(2 * n)},
        compiler_params=pltpu.CompilerParams(has_side_effects=EFFECT),
    )(*[pltpu.with_memory_space_constraint(t, pltpu.HBM) for t in flat + lands])
    sems, srcs, dsts = res[:2 * nb], res[2 * nb:2 * nb + n], res[2 * nb + n:2 * nb + 2 * n]
    out, pos = [], 0
    for b, batch in enumerate(batches):
        out.append((sems[2 * b], sems[2 * b + 1], list(srcs[pos:pos + len(batch)]), list(dsts[pos:pos + len(batch)])))
        pos += len(batch)
    return out, res[-1]


def split_arrive(handle, kind, after, name):
    send, recv, srcs, dsts = handle
    n = len(srcs)

    def body(*refs):
        s_refs, d_refs, send_ref, recv_ref = refs[:n], refs[n:2 * n], refs[2 * n], refs[2 * n + 1]
        x, y, c, k, others = _place()
        for a in range(n):
            for j, (px, py) in enumerate(others):
                src, _, landed = _route(kind, s_refs[a], d_refs[a], c, k, 2 * px + py)
                cp = _rc(src, landed, send_ref.at[3 * a + j], recv_ref.at[3 * a + j], (px, py, c))
                cp.wait_send()
                cp.wait_recv()

    res = pl.pallas_call(
        body, name=name, out_shape=tuple([pltpu.HBM(t.shape, t.dtype) for t in srcs + dsts]),
        in_specs=[HBM_SPEC] * (2 * n) + [SEM_SPEC, SEM_SPEC, ANY], out_specs=tuple([HBM_SPEC] * (2 * n)),
        input_output_aliases={t: t for t in range(2 * n)},
        compiler_params=pltpu.CompilerParams(has_side_effects=EFFECT),
    )(*srcs, *dsts, send, recv, after)
    return list(res[n:])


def forward_cores(bufs):
    n = len(bufs)

    def copies(outs, sems):
        send, recv = sems
        x, y, c, k, others = _place()
        onward, land = [], []
        for a in range(n):
            for j, (px, py) in enumerate(others):
                blk = outs[a].at[c, 2 * px + py]
                onward.append(_rc(blk, blk, send.at[a, j], recv.at[a, j], (x, y, 1 - c)))
                blk2 = outs[a].at[1 - c, 2 * px + py]
                land.append(_rc(blk2, blk2, send.at[a, j], recv.at[a, j], (x, y, 1 - c)))
        return onward, land

    def start(ins, outs, sems):
        for cp in copies(outs, sems)[0]:
            cp.start()

    def finish(ins, outs, sems):
        onward, land = copies(outs, sems)
        for arrived in land:
            arrived.wait_recv()
        for cp in onward:
            cp.wait_send()

    return Exchange(list(bufs), [_sds(b.shape, b.dtype) for b in bufs],
                    [pltpu.SemaphoreType.DMA((n, 3)), pltpu.SemaphoreType.DMA((n, 3))], start, finish, inplace=True)


def all_reduce_small(buf, name):
    r = buf.shape[0]
    n_dev = 8

    def body(in_ref, o_ref, land, send_sems, recv_sems):
        x, y, c, _, _ = _place()
        me = 4 * x + 2 * y + c
        land[me] = in_ref[...]
        sends = []
        for d in range(1, n_dev):
            peer = (x ^ (d >> 2), y ^ ((d >> 1) & 1), c ^ (d & 1))
            cp = _rc(in_ref, land.at[me], send_sems.at[d], recv_sems.at[d], peer)
            cp.start()
            sends.append(cp)
        for d in range(1, n_dev):
            blk = land.at[me ^ d]
            _rc(blk, blk, send_sems.at[d], recv_sems.at[d], (x, y, c)).wait_recv()
        for cp in sends:
            cp.wait_send()
        tot = land[0]
        for d in range(1, n_dev):
            tot = tot + land[d]
        o_ref[...] = tot

    vm = pl.BlockSpec(memory_space=pltpu.VMEM)
    return pl.pallas_call(
        body, out_shape=_sds(buf.shape, F32), in_specs=[vm], out_specs=vm,
        scratch_shapes=[pltpu.VMEM((n_dev, r, LANES), F32), pltpu.SemaphoreType.DMA((n_dev,)),
                        pltpu.SemaphoreType.DMA((n_dev,))],
        name=name)(buf)


def rmsnorm_fwd(x, w, name):
    s, d = x.shape
    tm = _row_tile(s, 512)

    def body(x_ref, w_ref, o_ref):
        o_ref[...] = _rms_fwd(x_ref[...], w_ref[...]).astype(BF16)

    return _pcall(body, out_shape=[_sds((s, d), BF16)], grid=(s // tm,),
                  in_specs=[pl.BlockSpec((tm, d), lambda i: (i, 0)), pl.BlockSpec((1, d), lambda i: (0, 0))],
                  out_specs=[pl.BlockSpec((tm, d), lambda i: (i, 0))], args=[x, w.reshape(1, d)], name=name)[0]


def _ffn_w_spec(chip_of, single=False):
    mode = dict(pipeline_mode=pl.Buffered(1)) if single else {}
    return pl.BlockSpec((N_CORES, 1, FF_PART, D_MODEL), lambda *ids: (0, chip_of(*ids), 0, 0), **mode)


def ffn_fwd(h, x, wg, wu, wd, norm_ws, name, ride=None):
    s, d = h.shape
    n_norm = len(norm_ws)
    tm = _row_tile(s, 1024)

    def body(*refs):
        h_ref, x_ref, wg_ref, wu_ref, wd_ref = refs[:5]
        nw_refs = refs[5:5 + n_norm]
        o_ref = refs[5 + n_norm]
        h_refs = refs[6 + n_norm:6 + 2 * n_norm]
        gu_ref, acc = refs[6 + 2 * n_norm], refs[7 + 2 * n_norm]
        k = pl.program_id(1)

        @pl.when(k == 0)
        def _():
            acc[...] = jnp.zeros(acc.shape, F32)

        hm = tm // 2
        for part in range(2):
            sub = pl.ds(part * hm, hm)
            hb = h_ref[sub, :]
            g = _dot(hb, wg_ref[...].reshape(FF_SHARD, d), NT)
            u = _dot(hb, wu_ref[...].reshape(FF_SHARD, d), NT)
            gu_ref[0, 0, sub, :] = g.astype(BF16)
            gu_ref[0, 1, sub, :] = u.astype(BF16)
            acc[sub, :] += _dot((g * _sigmoid(g) * u).astype(BF16), wd_ref[...].reshape(FF_SHARD, d))

        @pl.when(k == N_CHIPS - 1)
        def _():
            xn = x_ref[...] + 0.5 * acc[...]
            o_ref[...] = xn
            for nw_ref, hn_ref in zip(nw_refs, h_refs):
                hn_ref[...] = _rms_fwd(xn, nw_ref[...]).astype(BF16)

    row = pl.BlockSpec((tm, d), lambda i, k: (i, 0))
    vec = pl.BlockSpec((1, d), lambda i, k: (0, 0))
    wsp = _ffn_w_spec(lambda i, k: k)
    return _pcall(
        body, out_shape=[_sds((s, d), F32)] + [_sds((s, d), BF16)] * n_norm + [_sds((N_CHIPS, 2, s, FF_SHARD), BF16)],
        grid=(s // tm, N_CHIPS),
        in_specs=[row, row, wsp, wsp, wsp] + [vec] * n_norm,
        out_specs=[row] * (1 + n_norm) + [pl.BlockSpec((1, 2, tm, FF_SHARD), lambda i, k: (k, 0, i, 0))],
        scratch_shapes=[pltpu.VMEM((tm, d), F32)],
        args=[h, x, wg, wu, wd] + [nw.reshape(1, d) for nw in norm_ws], name=name, ride=ride)


def ffn_bwd(dxn, h, x_in, nw, gu, wg, wu, wd, name, ride=None):
    s, d = h.shape
    tm = _row_tile(s, 512)
    ni = s // tm
    last_e = N_CHIPS - 1

    def body(dxn_ref, h_ref, x_ref, nw_ref, gu_ref, wg_ref, wu_ref, wd_ref,
             dx_ref, dnw_ref, dwg_ref, dwu_ref, dwd_ref, dh, wacc):
        e = pl.program_id(0)
        i = pl.program_id(1)
        rows = pl.ds(pl.multiple_of(i * tm, tm), tm)

        @pl.when(i == 0)
        def _():
            wacc[...] = jnp.zeros(wacc.shape, F32)

        @pl.when(e == 0)
        def _():
            dh[rows, :] = jnp.zeros((tm, d), F32)

        hm = tm // 2
        for part in range(2):
            sub = pl.ds(part * hm, hm)
            dxb = dxn_ref[sub, :].astype(BF16)
            hb = h_ref[sub, :]
            g = gu_ref[0, 0, sub, :].astype(F32)
            u = gu_ref[0, 1, sub, :].astype(F32)
            drows = pl.ds(pl.multiple_of(i * tm + part * hm, hm), hm)
            sg = _sigmoid(g)
            silu = g * sg
            wacc[2] += _dot((0.5 * silu * u).astype(BF16), dxb, TN)
            da = 0.5 * _dot(dxb, wd_ref[...].reshape(FF_SHARD, d), NT)
            dg = (da * u * (sg * (1.0 + g * (1.0 - sg)))).astype(BF16)
            wacc[0] += _dot(dg, hb, TN)
            du = (da * silu).astype(BF16)
            dh[drows, :] += _dot(dg, wg_ref[...].reshape(FF_SHARD, d))
            wacc[1] += _dot(du, hb, TN)
            dh[drows, :] += _dot(du, wu_ref[...].reshape(FF_SHARD, d))

        @pl.when(i == ni - 1)
        def _():
            for t, dw_ref in enumerate((dwg_ref, dwu_ref, dwd_ref)):
                dw_ref[...] = wacc[t].astype(BF16).reshape(N_CORES, 1, FF_PART, d)

        @pl.when(e == last_e)
        def _():
            dx, dnw = _rms_bwd(dh[rows, :], x_ref[...], nw_ref[...])
            dx_ref[...] = dxn_ref[...] + dx
            col = jnp.sum(dnw, axis=0, keepdims=True)

            @pl.when(i == 0)
            def _():
                dnw_ref[...] = col

            @pl.when(i > 0)
            def _():
                dnw_ref[...] += col

    row = pl.BlockSpec((tm, d), lambda e, i: (i, 0))
    late = pl.BlockSpec((tm, d), lambda e, i: (jnp.where(e == last_e, i, 0), 0))
    vec = pl.BlockSpec((1, d), lambda e, i: (0, 0))
    wsp = _ffn_w_spec(lambda e, i: e, single=True)
    dwsp = _ffn_w_spec(lambda e, i: e, single=True)
    dw = _sds((N_CORES, N_CHIPS, FF_PART, d), BF16)
    return _pcall(
        body, out_shape=[_sds((s, d), F32), _sds((1, d), F32), dw, dw, dw],
        grid=(N_CHIPS, ni),
        in_specs=[row, row, late, vec, pl.BlockSpec((1, 2, tm, FF_SHARD), lambda e, i: (e, 0, i, 0)), wsp, wsp, wsp],
        out_specs=[late, vec, dwsp, dwsp, dwsp],
        scratch_shapes=[pltpu.VMEM((s, d), F32), pltpu.VMEM((3, FF_SHARD, d), F32)],
        args=[dxn, h, x_in, nw.reshape(1, d), gu, wg, wu, wd], name=name, ride=ride)


def mm_res(a, w, x, name, bias=None, norm_ws=(), ride=None):
    s, k = a.shape
    n = w.shape[1]
    tm = _row_tile(s, 512)
    has_bias = bias is not None
    n_norm = len(norm_ws)

    def body(*refs):
        a_ref, w_ref, x_ref = refs[:3]
        pos = 3
        t = _dot(a_ref[...], w_ref[...])
        if has_bias:
            t = t + refs[pos][...]
            pos += 1
        nw_refs = refs[pos:pos + n_norm]
        o_ref = refs[pos + n_norm]
        h_refs = refs[pos + n_norm + 1:]
        xn = x_ref[...] + t
        o_ref[...] = xn
        for nw_ref, h_ref in zip(nw_refs, h_refs):
            h_ref[...] = _rms_fwd(xn, nw_ref[...]).astype(BF16)

    row = pl.BlockSpec((tm, n), lambda i: (i, 0))
    vec = pl.BlockSpec((1, n), lambda i: (0, 0))
    in_specs = [pl.BlockSpec((tm, k), lambda i: (i, 0)), pl.BlockSpec((k, n), lambda i: (0, 0)), row]
    args = [a, w, x]
    if has_bias:
        in_specs.append(vec)
        args.append(bias.reshape(1, n))
    for nw in norm_ws:
        in_specs.append(vec)
        args.append(nw.reshape(1, n))
    return _pcall(body, out_shape=[_sds((s, n), F32)] + [_sds((s, n), BF16)] * n_norm, grid=(s // tm,),
                  in_specs=in_specs, out_specs=[row] * (1 + n_norm), args=args, name=name, ride=ride)


def mm_nn(a, w, name, bias=None, out_dtype=F32):
    s, k = a.shape
    n = w.shape[1]
    tm = _row_tile(s, 512)
    tn = _col_tile(n)
    has_bias = bias is not None

    def body(*refs):
        a_ref, w_ref = refs[:2]
        o_ref = refs[-1]
        t = _dot(a_ref[...], w_ref[...])
        if has_bias:
            t = t + refs[2][...]
        o_ref[...] = t.astype(out_dtype)

    in_specs = [pl.BlockSpec((tm, k), lambda j, i: (i, 0)), pl.BlockSpec((k, tn), lambda j, i: (0, j))]
    args = [a, w]
    if has_bias:
        in_specs.append(pl.BlockSpec((1, tn), lambda j, i: (0, j)))
        args.append(bias.reshape(1, n))
    return _pcall(body, out_shape=[_sds((s, n), out_dtype)], grid=(n // tn, s // tm), in_specs=in_specs,
                  out_specs=[pl.BlockSpec((tm, tn), lambda j, i: (i, j))], args=args, name=name)[0]


def mm_nt(a, w, name, n=None, row0=0, out_dtype=F32, ride=None):
    s, k = a.shape
    n = w.shape[0] if n is None else n
    tm = _row_tile(s, 512)
    tn = _col_tile(n)
    base = row0 // tn
    assert row0 % tn == 0

    def body(a_ref, w_ref, o_ref):
        o_ref[...] = _dot(a_ref[...].astype(BF16), w_ref[...], NT).astype(out_dtype)

    res = _pcall(body, out_shape=[_sds((s, n), out_dtype)], grid=(n // tn, s // tm),
                 in_specs=[pl.BlockSpec((tm, k), lambda j, i: (i, 0)), pl.BlockSpec((tn, k), lambda j, i: (base + j, 0))],
                 out_specs=[pl.BlockSpec((tm, tn), lambda j, i: (i, j))], args=[a, w], name=name, ride=ride)
    return res[0] if ride is None else (res[0][0], res[1])


def mm_tn(a, b, name, into=None, rows=None, row0=0, m_valid=None, col_sum=False, ride=None):
    s, m = a.shape
    n = b.shape[1]
    mv = m if m_valid is None else m_valid
    tm = _col_tile(m) if m_valid is None else mv
    tn = n if n <= 1024 else _col_tile(n)
    rows = mv if rows is None else rows
    assert row0 % tm == 0 and (m_valid is None or m == LANES)
    assert not col_sum or mv == tm
    base = row0 // tm
    ta = m if m_valid is not None else tm

    def body(*refs):
        a_ref, b_ref = refs[0], refs[1]
        o_ref = refs[-2] if col_sum else refs[-1]
        bf = b_ref[...]
        t = _dot(a_ref[...].astype(BF16), bf.astype(BF16), TN)
        o_ref[...] = t[:tm].astype(BF16)
        if col_sum:
            refs[-1][...] = jnp.sum(bf.astype(F32), axis=0, keepdims=True)

    in_specs = [pl.BlockSpec((s, ta), lambda i, j: (0, i)), pl.BlockSpec((s, tn), lambda i, j: (0, j))]
    args = [a, b]
    aliases = None
    if into is not None:
        in_specs.append(ANY)
        args.append(into)
        aliases = {2: 0}
    out_shape = [_sds((rows, n), BF16)]
    out_specs = [pl.BlockSpec((tm, tn), lambda i, j: (base + i, j))]
    if col_sum:
        out_shape.append(_sds((1, n), F32))
        out_specs.append(pl.BlockSpec((1, tn), lambda i, j: (0, j)))
    res = _pcall(body, out_shape=out_shape, grid=(mv // tm, n // tn), in_specs=in_specs, out_specs=out_specs,
                 args=args, name=name, ride=ride, aliases=aliases)
    outs = res if ride is None else res[0]
    out = (outs[0], outs[1][0]) if col_sum else outs[0]
    return out if ride is None else (out, res[1])


def mm_rms_bwd(terms, dxn, x, nw, name, ride=None):
    s, n = x.shape
    nt_ = len(terms)
    tm = _row_tile(s, 512 if nt_ <= 2 else 256)
    forms = [t[5] for t in terms]

    def body(*refs):
        dxn_ref, x_ref, nw_ref, dx_ref, dnw_ref = refs[2 * nt_:]
        i = pl.program_id(0)
        dh = None
        for t in range(nt_):
            part = _dot(refs[2 * t][...].astype(BF16), refs[2 * t + 1][...], NN if forms[t] == "nn" else NT)
            dh = part if dh is None else dh + part
        dx, dnw = _rms_bwd(dh, x_ref[...], nw_ref[...])
        dx_ref[...] = dxn_ref[...] + dx
        col = jnp.sum(dnw, axis=0, keepdims=True)

        @pl.when(i == 0)
        def _():
            dnw_ref[...] = col

        @pl.when(i > 0)
        def _():
            dnw_ref[...] += col

    in_specs, args = [], []
    for a, cb, w, rb, kb, form in terms:
        in_specs.append(pl.BlockSpec((tm, kb), lambda i, cb=cb: (i, cb)))
        if form == "nn":
            in_specs.append(pl.BlockSpec((kb, n), lambda i, rb=rb: (rb, 0)))
        else:
            in_specs.append(pl.BlockSpec((n, kb), lambda i, rb=rb: (0, rb)))
        args += [a, w]
    row = pl.BlockSpec((tm, n), lambda i: (i, 0))
    vec = pl.BlockSpec((1, n), lambda i: (0, 0))
    res = _pcall(body, out_shape=[_sds((s, n), F32), _sds((1, n), F32)], grid=(s // tm,),
                 in_specs=in_specs + [row, row, vec], out_specs=[row, vec],
                 args=args + [dxn, x, nw.reshape(1, n)], name=name, ride=ride)
    outs = res if ride is None else res[0]
    out = (outs[0], outs[1][0])
    return out if ride is None else (out, res[1])


def rope_tables(s):
    pos = jnp.arange(s, dtype=F32)
    inv = 1.0 / (ROPE_THETA ** (jnp.arange(0, ATT_HEAD_DIM, 2, dtype=F32) / ATT_HEAD_DIM))
    ang = pos[:, None] * inv[None, :]
    cos = jnp.tile(jnp.cos(ang), (1, 2 * LANES // ATT_HEAD_DIM))
    sin = jnp.tile(jnp.sin(ang), (1, 2 * LANES // ATT_HEAD_DIM))
    return cos, sin


def rope_apply(t, cos, sin, name, inverse=False, scale=1.0, out_dtype=BF16):
    s, n = t.shape
    tm = _row_tile(s, 512)
    half = ATT_HEAD_DIM // 2
    reps = n // LANES

    def body(t_ref, c_ref, s_ref, o_ref):
        tf = t_ref[...].astype(F32)
        c = jnp.tile(c_ref[...], (1, reps))
        sn = jnp.tile(s_ref[...], (1, reps))
        lane = lax.broadcasted_iota(jnp.int32, tf.shape, 1)
        first = (lane & (ATT_HEAD_DIM - 1)) < half
        rot = jnp.where(first, -pltpu.roll(tf, n - half, 1), pltpu.roll(tf, half, 1))
        sign = -1.0 if inverse else 1.0
        o_ref[...] = (scale * (tf * c + sign * rot * sn)).astype(out_dtype)

    tab = pl.BlockSpec((tm, LANES), lambda i: (i, 0))
    return _pcall(body, out_shape=[_sds((s, n), out_dtype)], grid=(s // tm,),
                  in_specs=[pl.BlockSpec((tm, n), lambda i: (i, 0)), tab, tab],
                  out_specs=[pl.BlockSpec((tm, n), lambda i: (i, 0))], args=[t, cos, sin], name=name)[0]


CONV_TILE = 256


def _shift_down(u, k):
    if k == 0:
        return u
    row = lax.broadcasted_iota(jnp.int32, u.shape, 0)
    return jnp.where(row >= k, pltpu.roll(u, k, 0), 0.0)


def _shift_up(u, k):
    if k == 0:
        return u
    s = u.shape[0]
    row = lax.broadcasted_iota(jnp.int32, u.shape, 0)
    return jnp.where(row < s - k, pltpu.roll(u, s - k, 0), 0.0)


def _conv_taps(u):
    return [_shift_down(u, CONV_WIDTH - 1 - k) for k in range(CONV_WIDTH)]


def _conv_pre(taps, w_ref, b_ref):
    pre = b_ref[...] + w_ref[0:1, :] * taps[0]
    for k in range(1, CONV_WIDTH):
        pre += w_ref[k:k + 1, :] * taps[k]
    return pre


def conv_fwd(u, w, b, name, ride=None):
    s, c = u.shape

    def body(u_ref, w_ref, b_ref, o_ref):
        pre = _conv_pre(_conv_taps(u_ref[...]), w_ref, b_ref)
        o_ref[...] = pre * _sigmoid(pre)

    col = pl.BlockSpec((s, CONV_TILE), lambda j: (0, j))
    res = _pcall(body, out_shape=[_sds((s, c), F32)], grid=(c // CONV_TILE,),
                 in_specs=[col, pl.BlockSpec((CONV_WIDTH, CONV_TILE), lambda j: (0, j)),
                           pl.BlockSpec((1, CONV_TILE), lambda j: (0, j))],
                 out_specs=[col], args=[u, w, b.reshape(1, c)], name=name, ride=ride)
    return res[0] if ride is None else (res[0][0], res[1])


def conv_bwd(dxs, db_, dc_, u, w, b, name, ride=None):
    s, c = u.shape
    n_x = dxs.shape[1] // CONV_TILE
    n_b = db_.shape[1] // CONV_TILE

    def body(dx_ref, dbb_ref, dcc_ref, u_ref, w_ref, b_ref, du_ref, dw_ref, dbias_ref):
        j = pl.program_id(0)
        dact = jnp.where(j < n_x, dx_ref[...], jnp.where(j < n_x + n_b, dbb_ref[...], dcc_ref[...]))
        taps = _conv_taps(u_ref[...])
        pre = _conv_pre(taps, w_ref, b_ref)
        sg = _sigmoid(pre)
        dpre = dact * (sg * (1.0 + pre * (1.0 - sg)))
        du = w_ref[CONV_WIDTH - 1:CONV_WIDTH, :] * dpre
        for k in range(CONV_WIDTH - 1):
            du += w_ref[k:k + 1, :] * _shift_up(dpre, CONV_WIDTH - 1 - k)
        du_ref[...] = du
        dbias_ref[...] = jnp.sum(dpre, axis=0, keepdims=True)
        for k in range(CONV_WIDTH):
            dw_ref[k:k + 1, :] = jnp.sum(dpre * taps[k], axis=0, keepdims=True)

    col = pl.BlockSpec((s, CONV_TILE), lambda j: (0, j))
    wsp = pl.BlockSpec((CONV_WIDTH, CONV_TILE), lambda j: (0, j))
    bsp = pl.BlockSpec((1, CONV_TILE), lambda j: (0, j))
    res = _pcall(
        body, out_shape=[_sds((s, c), F32), _sds((CONV_WIDTH, c), F32), _sds((1, c), F32)], grid=(c // CONV_TILE,),
        in_specs=[pl.BlockSpec((s, CONV_TILE), lambda j: (0, jnp.minimum(j, n_x - 1))),
                  pl.BlockSpec((s, CONV_TILE), lambda j: (0, jnp.clip(j - n_x, 0, n_b - 1))),
                  pl.BlockSpec((s, CONV_TILE), lambda j: (0, jnp.clip(j - n_x - n_b, 0, n_b - 1))),
                  col, wsp, bsp],
        out_specs=[col, wsp, bsp], args=[dxs, db_, dc_, u, w, b.reshape(1, c)], name=name, ride=ride)
    (du, dw, db), rode = res if ride is not None else (res, None)
    return (du, dw, db[0]) if ride is None else ((du, dw, db[0]), rode)


def _lane_pick(mat, idx):
    lane = lax.broadcasted_iota(jnp.int32, mat.shape, 1)
    return jnp.sum(jnp.where(lane == idx, mat, 0.0), axis=1, keepdims=True)


def _sub_pick(mat, idx):
    sub = lax.broadcasted_iota(jnp.int32, mat.shape, 0)
    return jnp.sum(jnp.where(sub == idx, mat, 0.0), axis=0, keepdims=True)


def _expand_heads(cols):
    rows = cols[0].shape[0]
    left = lax.broadcasted_iota(jnp.int32, (rows, LANES), 1) < SSM_HEAD_DIM
    return jnp.concatenate(
        [jnp.where(left, cols[2 * p], cols[2 * p + 1]) for p in range(HEADS_PER_GROUP // 2)], axis=1)


def _dot_01(x, ones, ones_first, pieces):
    tot, rest = None, x
    for _ in range(pieces):
        piece = rest.astype(BF16)
        rest = rest - piece.astype(F32)
        part = _dot(ones, piece) if ones_first else _dot(piece, ones)
        tot = part if tot is None else tot + part
    return tot


def _heads_to_lanes(mat, g):
    jj = lax.broadcasted_iota(jnp.int32, (GROUP_DIM, LANES), 0)
    ll = lax.broadcasted_iota(jnp.int32, (GROUP_DIM, LANES), 1)
    sel = (ll == HEADS_PER_GROUP * g + (jj >> 6)).astype(BF16)
    return _dot_01(mat, sel, False, 3)


def _softplus(x):
    return jnp.maximum(x, 0.0) + jnp.log1p(jnp.exp(-jnp.abs(x)))


def _ssd_scalars(dt_ref, bias_ref, a_ref, dtall, csall, cst):
    dta = _softplus(dt_ref[...] + bias_ref[...])
    row = lax.broadcasted_iota(jnp.int32, (CHUNK, CHUNK), 0)
    col = lax.broadcasted_iota(jnp.int32, (CHUNK, CHUNK), 1)
    cs = _dot_01(dta * a_ref[...], (row >= col).astype(BF16), True, 3)
    dtall[...] = dta
    csall[...] = cs
    cst[...] = cs.T


def _decay_mat(cs_col, cs_row):
    row = lax.broadcasted_iota(jnp.int32, (CHUNK, CHUNK), 0)
    col = lax.broadcasted_iota(jnp.int32, (CHUNK, CHUNK), 1)
    return jnp.exp(jnp.where(row >= col, cs_col - cs_row, NEG))


def _head_mask(xpair, right):
    lane = lax.broadcasted_iota(jnp.int32, xpair.shape, 1)
    keep = (lane >= SSM_HEAD_DIM) if right else (lane < SSM_HEAD_DIM)
    return jnp.where(keep, xpair, 0.0)


def _chunk_cols(x_all, g):
    return [_lane_pick(x_all, HEADS_PER_GROUP * g + r) for r in range(HEADS_PER_GROUP)]


def _decay_col(cs_cols):
    return jnp.concatenate(
        [jnp.broadcast_to(jnp.exp(cc[CHUNK - 1:CHUNK, :]), (SSM_HEAD_DIM, 1)) for cc in cs_cols], axis=0)


def ssd_fwd(act, z, dtp, bias_p, a_p, d_p, normw, name, ride=None):
    s = act.shape[0]
    nc = s // CHUNK

    def body(xs_all, b_all, c_all, z_all, dt_ref, bias_ref, a_ref, d_ref, nw_all,
             yn_all, y_all, st_all, state, dtall, csall, cst):
        _ssd_scalars(dt_ref, bias_ref, a_ref, dtall, csall, cst)

        @pl.when(pl.program_id(0) == 0)
        def _():
            state[...] = jnp.zeros(state.shape, F32)

        for g in range(SSM_GROUPS):
            wide = pl.ds(g * GROUP_DIM, GROUP_DIM)
            narrow = pl.ds(g * SSM_STATE, SSM_STATE)
            group(g, xs_all.at[:, wide], b_all.at[:, narrow], c_all.at[:, narrow], z_all.at[:, wide], d_ref,
                  nw_all.at[:, wide], yn_all.at[:, wide], y_all.at[:, wide], st_all.at[:, pl.ds(g, 1)],
                  state, dtall, csall, cst)

    def group(g, xs_ref, b_ref, c_ref, z_ref, d_ref, nw_ref, yn_ref, y_ref, st_ref, state, dtall, csall, cst):
        cs_cols = _chunk_cols(csall[...], g)
        dt_cols = _chunk_cols(dtall[...], g)
        cs_rows = [_sub_pick(cst[...], HEADS_PER_GROUP * g + r) for r in range(HEADS_PER_GROUP)]
        d_cols = _chunk_cols(d_ref[...], g)
        cs_exp = _expand_heads(cs_cols)
        dt_exp = _expand_heads(dt_cols)
        d_exp = _expand_heads(d_cols)
        xs = xs_ref[...]
        bb = b_ref[...].astype(BF16)
        cb16 = c_ref[...].astype(BF16)
        xdt = xs * dt_exp
        s_prev = state[g]
        st_ref[0, 0] = s_prev
        y_off = _dot(cb16, s_prev.astype(BF16), NT) * jnp.exp(cs_exp)
        decay_st = jnp.exp(cs_exp[CHUNK - 1:CHUNK, :] - cs_exp)
        contrib = _dot((xdt * decay_st).astype(BF16), bb, TN)
        state[g] = _decay_col(cs_cols) * s_prev + contrib
        cbm = _dot(cb16, bb, NT)
        pairs = []
        for p in range(HEADS_PER_GROUP // 2):
            xpair = xdt[:, LANES * p:LANES * (p + 1)]
            m0 = (cbm * _decay_mat(cs_cols[2 * p], cs_rows[2 * p])).astype(BF16)
            m1 = (cbm * _decay_mat(cs_cols[2 * p + 1], cs_rows[2 * p + 1])).astype(BF16)
            pairs.append(_dot(m0, _head_mask(xpair, False).astype(BF16))
                         + _dot(m1, _head_mask(xpair, True).astype(BF16)))
        y = jnp.concatenate(pairs, axis=1) + y_off + xs * d_exp
        y_ref[...] = y
        zf = z_ref[...]
        yg = y * (zf * _sigmoid(zf))
        yn_ref[...] = _rms_fwd(yg, nw_ref[...]).astype(BF16)

    gn = SSM_GROUPS * SSM_STATE
    wide = pl.BlockSpec((CHUNK, D_INNER), lambda c: (c, 0))
    par = pl.BlockSpec((1, LANES), lambda c: (0, 0))
    return _pcall(
        body,
        out_shape=[_sds((s, D_INNER), BF16), _sds((s, D_INNER), F32),
                   _sds((nc, SSM_GROUPS, GROUP_DIM, SSM_STATE), F32)],
        grid=(nc,),
        in_specs=[wide,
                  pl.BlockSpec((CHUNK, gn), lambda c: (c, D_INNER // gn)),
                  pl.BlockSpec((CHUNK, gn), lambda c: (c, D_INNER // gn + 1)),
                  wide,
                  pl.BlockSpec((CHUNK, LANES), lambda c: (c, 0)),
                  par, par, par,
                  pl.BlockSpec((1, D_INNER), lambda c: (0, 0))],
        out_specs=[wide, wide, pl.BlockSpec((1, SSM_GROUPS, GROUP_DIM, SSM_STATE), lambda c: (c, 0, 0, 0))],
        scratch_shapes=[pltpu.VMEM((SSM_GROUPS, GROUP_DIM, SSM_STATE), F32),
                        pltpu.VMEM((CHUNK, LANES), F32), pltpu.VMEM((CHUNK, LANES), F32),
                        pltpu.VMEM((LANES, CHUNK), F32)],
        args=[act, act, act, z, dtp, bias_p, a_p, d_p, normw], name=name, ride=ride)


def ssd_bwd(dyn, act, z, y_pre, states, dtp, bias_p, a_p, d_p, normw, name, ride=None):
    s = act.shape[0]
    nc = s // CHUNK

    def body(dyn_all, xs_all, b_all, c_all, z_all, y_all, st_all, dt_ref, bias_ref, a_ref, d_ref, nw_all,
             dxs_all, db_all, dc_all, dz_all, ddt_ref, dnw_ref, dbias_ref, da_ref, dd_ref,
             dstate, dtall, csall, cst):
        _ssd_scalars(dt_ref, bias_ref, a_ref, dtall, csall, cst)
        ddt_ref[...] = jnp.zeros((CHUNK, LANES), F32)

        @pl.when(pl.program_id(0) == 0)
        def _():
            dstate[...] = jnp.zeros(dstate.shape, F32)
            dnw_ref[...] = jnp.zeros(dnw_ref.shape, F32)
            dbias_ref[...] = jnp.zeros((1, LANES), F32)
            da_ref[...] = jnp.zeros((1, LANES), F32)
            dd_ref[...] = jnp.zeros((1, LANES), F32)

        for g in range(SSM_GROUPS):
            wide = pl.ds(g * GROUP_DIM, GROUP_DIM)
            narrow = pl.ds(g * SSM_STATE, SSM_STATE)
            group(g, dyn_all.at[:, wide], xs_all.at[:, wide], b_all.at[:, narrow], c_all.at[:, narrow],
                  z_all.at[:, wide], y_all.at[:, wide], st_all.at[:, pl.ds(g, 1)], dt_ref, bias_ref, a_ref, d_ref,
                  nw_all.at[:, wide], dxs_all.at[:, wide], db_all.at[:, narrow], dc_all.at[:, narrow],
                  dz_all.at[:, wide], ddt_ref, dnw_ref, dbias_ref, da_ref, dd_ref, dstate, dtall, csall, cst)

    def group(g, dyn_ref, xs_ref, b_ref, c_ref, z_ref, y_ref, st_ref, dt_ref, bias_ref, a_ref, d_ref, nw_ref,
              dxs_ref, db_ref, dc_ref, dz_ref, ddt_ref, dnw_ref, dbias_ref, da_ref, dd_ref,
              dstate, dtall, csall, cst):
        cs_cols = _chunk_cols(csall[...], g)
        dt_cols = _chunk_cols(dtall[...], g)
        cs_rows = [_sub_pick(cst[...], HEADS_PER_GROUP * g + r) for r in range(HEADS_PER_GROUP)]
        d_cols = _chunk_cols(d_ref[...], g)
        cs_exp = _expand_heads(cs_cols)
        dt_exp = _expand_heads(dt_cols)
        d_exp = _expand_heads(d_cols)
        xs = xs_ref[...]
        bb = b_ref[...].astype(BF16)
        cb16 = c_ref[...].astype(BF16)
        xdt = xs * dt_exp
        s_prev = st_ref[0, 0]
        s_prev16 = s_prev.astype(BF16)
        ds_next = dstate[g]
        ds16 = ds_next.astype(BF16)

        zf = z_ref[...]
        sz = _sigmoid(zf)
        silu_z = zf * sz
        y = y_ref[...]
        yg = y * silu_z
        dout = dyn_ref[...]
        dyg, dnw = _rms_bwd(dout, yg, nw_ref[...])
        dnw_ref[pl.ds(g, 1), :] += jnp.sum(dnw, axis=0, keepdims=True)
        dy = dyg * silu_z
        dz_ref[...] = dyg * y * (sz * (1.0 + zf * (1.0 - sz)))
        dd_ref[...] += jnp.sum(_heads_to_lanes(dy * xs, g), axis=0, keepdims=True)

        exp_cs = jnp.exp(cs_exp)
        decay_st = jnp.exp(cs_exp[CHUNK - 1:CHUNK, :] - cs_exp)
        cs_t = _dot(cb16, s_prev16, NT)
        dyo = dy * exp_cs
        dc_acc = _dot(dyo.astype(BF16), s_prev16, NN)
        g1 = _dot(bb, ds16, NT)
        xds = xdt * decay_st
        db_acc = _dot(xds.astype(BF16), ds16, NN)
        dxdt_off = g1 * decay_st
        t_exp = g1 * xds
        dcs_exp = dy * cs_t * exp_cs - t_exp
        decay_c = _decay_col(cs_cols)
        dstate[g] = decay_c * ds_next + _dot(dyo.astype(BF16), cb16, TN)
        dlast_col = jnp.sum(ds_next * s_prev, axis=1, keepdims=True) * decay_c
        jj = lax.broadcasted_iota(jnp.int32, (GROUP_DIM, LANES), 0)
        ll = lax.broadcasted_iota(jnp.int32, (GROUP_DIM, LANES), 1)
        sel = ll == HEADS_PER_GROUP * g + (jj >> 6)
        dlast = jnp.sum(jnp.where(sel, dlast_col, 0.0), axis=0, keepdims=True)
        t_all = _heads_to_lanes(t_exp, g)
        dlast += jnp.sum(t_all, axis=0, keepdims=True)
        dcs_all = _heads_to_lanes(dcs_exp, g)

        cbm = _dot(cb16, bb, NT)
        dcb = jnp.zeros((CHUNK, CHUNK), F32)
        dcs_rows = jnp.zeros((LANES, CHUNK), F32)
        lane_l = lax.broadcasted_iota(jnp.int32, (CHUNK, LANES), 1)
        sub_l = lax.broadcasted_iota(jnp.int32, (LANES, CHUNK), 0)
        dxdt_pairs = []
        for p in range(HEADS_PER_GROUP // 2):
            xpair16 = xdt[:, LANES * p:LANES * (p + 1)].astype(BF16)
            dypair = dy[:, LANES * p:LANES * (p + 1)]
            acc = None
            for r in (2 * p, 2 * p + 1):
                lm = _decay_mat(cs_cols[r], cs_rows[r])
                m = cbm * lm
                dyh = _head_mask(dypair, r % 2 == 1).astype(BF16)
                dm = _dot(dyh, xpair16, NT)
                dcb += dm * lm
                q = dm * m
                idx = HEADS_PER_GROUP * g + r
                dcs_all += jnp.where(lane_l == idx, jnp.sum(q, axis=1, keepdims=True), 0.0)
                dcs_rows -= jnp.where(sub_l == idx, jnp.sum(q, axis=0, keepdims=True), 0.0)
                part = _dot(m.astype(BF16), dyh, TN)
                acc = part if acc is None else acc + part
            dxdt_pairs.append(acc)
        dxdt = jnp.concatenate(dxdt_pairs, axis=1) + dxdt_off
        dcb16 = dcb.astype(BF16)
        dc_ref[...] = dc_acc + _dot(dcb16, bb, NN)
        db_ref[...] = db_acc + _dot(dcb16, cb16, TN)
        dxs_ref[...] = dxdt * dt_exp + dy * d_exp

        dcs_all += dcs_rows.T
        row = lax.broadcasted_iota(jnp.int32, (CHUNK, CHUNK), 0)
        col = lax.broadcasted_iota(jnp.int32, (CHUNK, CHUNK), 1)
        last_row = lax.broadcasted_iota(jnp.int32, (CHUNK, LANES), 0) == CHUNK - 1
        dcs_all += jnp.where(last_row, dlast, 0.0)
        da_all = _dot_01(dcs_all, (col >= row).astype(BF16), True, 3)
        dta = dtall[...]
        in_group = jnp.logical_and(lane_l >= HEADS_PER_GROUP * g, lane_l < HEADS_PER_GROUP * (g + 1))
        ddt = jnp.where(in_group, da_all * a_ref[...] + _heads_to_lanes(dxdt * xs, g), 0.0)
        da_ref[...] += jnp.sum(jnp.where(in_group, da_all * dta, 0.0), axis=0, keepdims=True)
        ddt_raw = ddt * _sigmoid(dt_ref[...] + bias_ref[...])
        ddt_ref[...] += ddt_raw
        dbias_ref[...] += jnp.sum(ddt_raw, axis=0, keepdims=True)

    gn = SSM_GROUPS * SSM_STATE
    wide = pl.BlockSpec((CHUNK, D_INNER), lambda c: (nc - 1 - c, 0))
    st = pl.BlockSpec((CHUNK, gn), lambda c: (nc - 1 - c, 0))
    par = pl.BlockSpec((1, LANES), lambda c: (0, 0))
    dtb = pl.BlockSpec((CHUNK, LANES), lambda c: (nc - 1 - c, 0))
    f = lambda shape: _sds(shape, F32)
    return _pcall(
        body,
        out_shape=[f((s, D_INNER)), f((s, gn)), f((s, gn)),
                   f((s, D_INNER)), f((s, LANES)), f((8, GROUP_DIM)), f((1, LANES)), f((1, LANES)), f((1, LANES))],
        grid=(nc,),
        in_specs=[wide, wide,
                  pl.BlockSpec((CHUNK, gn), lambda c: (nc - 1 - c, D_INNER // gn)),
                  pl.BlockSpec((CHUNK, gn), lambda c: (nc - 1 - c, D_INNER // gn + 1)),
                  wide, wide,
                  pl.BlockSpec((1, SSM_GROUPS, GROUP_DIM, SSM_STATE), lambda c: (nc - 1 - c, 0, 0, 0)),
                  dtb, par, par, par,
                  pl.BlockSpec((1, D_INNER), lambda c: (0, 0))],
        out_specs=[wide, st, st, wide, dtb, pl.BlockSpec((8, GROUP_DIM), lambda c: (0, 0)), par, par, par],
        scratch_shapes=[pltpu.VMEM((SSM_GROUPS, GROUP_DIM, SSM_STATE), F32),
                        pltpu.VMEM((CHUNK, LANES), F32), pltpu.VMEM((CHUNK, LANES), F32),
                        pltpu.VMEM((LANES, CHUNK), F32)],
        args=[dyn, act, act, act, z, y_pre, states, dtp, bias_p, a_p, d_p, normw], name=name, ride=ride)


def _attn_probs(q, kp, kc, sink, n):
    sp = _dot(q, kp, NT)
    sc = _dot(q, kc, NT)
    i = lax.broadcasted_iota(jnp.int32, sp.shape, 0) & (WINDOW - 1)
    j = lax.broadcasted_iota(jnp.int32, sp.shape, 1)
    sp = jnp.where(jnp.logical_and(j > i, n > 0), sp, NEG)
    sc = jnp.where(j <= i, sc, NEG)
    m = jnp.maximum(jnp.maximum(jnp.max(sp, axis=1, keepdims=True), jnp.max(sc, axis=1, keepdims=True)), sink)
    pp = jnp.exp(sp - m)
    pc = jnp.exp(sc - m)
    ps = jnp.exp(sink - m)
    inv = 1.0 / (jnp.sum(pp, axis=1, keepdims=True) + jnp.sum(pc, axis=1, keepdims=True) + ps)
    return pp * inv, pc * inv, ps * inv


def attn_fwd(qt, kt, vt, sink_rows, name, ride=None):
    s = qt.shape[1]
    nb = s // WINDOW
    rows = Q_PER_KV * WINDOW

    def body(q_ref, kp_ref, kc_ref, vp_ref, vc_ref, sk_ref, o_ref):
        n = pl.program_id(0)
        for h in range(N_KV_HEADS):
            heads = pl.ds(h * Q_PER_KV, Q_PER_KV)
            q = q_ref[heads].reshape(rows, ATT_HEAD_DIM)
            pp, pc, _ = _attn_probs(q, kp_ref[h], kc_ref[h], sk_ref[h], n)
            o = _dot(pp.astype(BF16), vp_ref[h]) + _dot(pc.astype(BF16), vc_ref[h])
            o_ref[heads] = o.reshape(Q_PER_KV, WINDOW, ATT_HEAD_DIM).astype(BF16)

    qsp = pl.BlockSpec((N_Q_HEADS, WINDOW, ATT_HEAD_DIM), lambda n: (0, n, 0))
    prev = pl.BlockSpec((N_KV_HEADS, WINDOW, ATT_HEAD_DIM), lambda n: (0, jnp.maximum(n - 1, 0), 0))
    cur = pl.BlockSpec((N_KV_HEADS, WINDOW, ATT_HEAD_DIM), lambda n: (0, n, 0))
    return _pcall(body, out_shape=[_sds(qt.shape, BF16)], grid=(nb,),
                  in_specs=[qsp, prev, cur, prev, cur, pl.BlockSpec((N_KV_HEADS, rows, 1), lambda n: (0, 0, 0))],
                  out_specs=[qsp], args=[qt, kt, kt, vt, vt, sink_rows], name=name, ride=ride)


def attn_bwd(qt, kt, vt, sink_rows, dot_, name, ride=None):
    s = qt.shape[1]
    nb = s // WINDOW
    rows = Q_PER_KV * WINDOW

    def body(q_ref, kp_ref, kc_ref, vp_ref, vc_ref, sk_ref, do_ref, dq_ref, dk_ref, dv_ref, ds_ref, kacc, vacc):
        n = pl.program_id(0)

        @pl.when(n == 0)
        def _():
            kacc[...] = jnp.zeros(kacc.shape, F32)
            vacc[...] = jnp.zeros(vacc.shape, F32)

        @pl.when(n < nb)
        def _():
            for h in range(N_KV_HEADS):
                heads = pl.ds(h * Q_PER_KV, Q_PER_KV)
                q = q_ref[heads].reshape(rows, ATT_HEAD_DIM)
                do = do_ref[heads].reshape(rows, ATT_HEAD_DIM)
                kp, kc, vp, vc = kp_ref[h], kc_ref[h], vp_ref[h], vc_ref[h]
                pp, pc, ps = _attn_probs(q, kp, kc, sk_ref[h], n)
                dpp = _dot(do, vp, NT)
                dpc = _dot(do, vc, NT)
                delta = jnp.sum(pp * dpp, axis=1, keepdims=True) + jnp.sum(pc * dpc, axis=1, keepdims=True)
                dsp = (pp * (dpp - delta)).astype(BF16)
                dsc = (pc * (dpc - delta)).astype(BF16)
                dq = _dot(dsp, kp) + _dot(dsc, kc)
                dq_ref[heads] = dq.reshape(Q_PER_KV, WINDOW, ATT_HEAD_DIM)
                dk_ref[h] = kacc[h] + _dot(dsp, q, TN)
                dv_ref[h] = vacc[h] + _dot(pp.astype(BF16), do, TN)
                kacc[h] = _dot(dsc, q, TN)
                vacc[h] = _dot(pc.astype(BF16), do, TN)
                dsk = -ps * delta
                sub = lax.broadcasted_iota(jnp.int32, (8, LANES), 0)
                tile = jnp.zeros((8, LANES), F32)
                for j in range(Q_PER_KV):
                    tile += jnp.where(sub == j, jnp.sum(dsk[j * WINDOW:(j + 1) * WINDOW, :], axis=0, keepdims=True),
                                      0.0)
                ds_ref[h, 0] = tile

        @pl.when(n == nb)
        def _():
            dk_ref[...] = kacc[...]
            dv_ref[...] = vacc[...]
            ds_ref[...] = jnp.zeros(ds_ref.shape, F32)

    last = nb - 1
    qsp = pl.BlockSpec((N_Q_HEADS, WINDOW, ATT_HEAD_DIM), lambda n: (0, jnp.minimum(n, last), 0))
    prev = pl.BlockSpec((N_KV_HEADS, WINDOW, ATT_HEAD_DIM), lambda n: (0, jnp.clip(n - 1, 0, last), 0))
    cur = pl.BlockSpec((N_KV_HEADS, WINDOW, ATT_HEAD_DIM), lambda n: (0, jnp.minimum(n, last), 0))
    dkv = pl.BlockSpec((N_KV_HEADS, WINDOW, ATT_HEAD_DIM), lambda n: (0, jnp.maximum(n - 1, 0), 0))
    f = lambda shape: _sds(shape, F32)
    acc = pltpu.VMEM((N_KV_HEADS, WINDOW, ATT_HEAD_DIM), F32)
    return _pcall(
        body, out_shape=[f(qt.shape), f(kt.shape), f(vt.shape), f((N_KV_HEADS, nb + 1, 8, LANES))],
        grid=(nb + 1,),
        in_specs=[qsp, prev, cur, prev, cur, pl.BlockSpec((N_KV_HEADS, rows, 1), lambda n: (0, 0, 0)), qsp],
        out_specs=[qsp, dkv, dkv, pl.BlockSpec((N_KV_HEADS, 1, 8, LANES), lambda n: (0, n, 0, 0))],
        scratch_shapes=[acc, acc], args=[qt, kt, kt, vt, vt, sink_rows, dot_], name=name, ride=ride)


def loss_head(x, w, tgt, name):
    s, d = x.shape
    tm = _row_tile(s, 256)

    def body(x_ref, w_ref, t_ref, loss_ref, dx_ref, dw_ref):
        i = pl.program_id(0)
        xf = x_ref[...]
        wv = w_ref[...]
        r = lax.rsqrt(jnp.mean(xf * xf, axis=-1, keepdims=True) + EPS)
        xhat = xf * r
        e = xhat * wv - t_ref[...]
        part = 0.5 * jnp.sum(jnp.mean(e * e, axis=-1, keepdims=True), axis=0, keepdims=True)
        dy = e * (1.0 / d)
        dxhat = dy * wv
        dx_ref[...] = r * (dxhat - xhat * jnp.mean(dxhat * xhat, axis=-1, keepdims=True))
        col = jnp.sum(dy * xhat, axis=0, keepdims=True)

        @pl.when(i == 0)
        def _():
            loss_ref[...] = jnp.broadcast_to(part, (1, LANES))
            dw_ref[...] = col

        @pl.when(i > 0)
        def _():
            loss_ref[...] += jnp.broadcast_to(part, (1, LANES))
            dw_ref[...] += col

    row = pl.BlockSpec((tm, d), lambda i: (i, 0))
    vec = pl.BlockSpec((1, d), lambda i: (0, 0))
    return _pcall(body, out_shape=[_sds((1, LANES), F32), _sds((s, d), F32), _sds((1, d), F32)], grid=(s // tm,),
                  in_specs=[row, vec, row], out_specs=[pl.BlockSpec((1, LANES), lambda i: (0, 0)), row, vec],
                  args=[x, w.reshape(1, d), tgt], name=name)


ELEMWISE_TILE = 720 * 1024


def _tile_rows(r, c, max_elems=262144, mult=16):
    best = None
    for t in range(mult, r + 1, mult):
        if r % t == 0 and t * c <= max_elems:
            best = t
    return best or r


def add_pair(xhs, ps, c_idx, name):
    n = len(xhs)
    _, r, c = xhs[0].shape
    tr = _tile_rows(r, c, max_elems=ELEMWISE_TILE)

    def body(c_ref, *refs):
        for x_ref, p_ref, o_ref in zip(refs[:n], refs[n:2 * n], refs[2 * n:]):
            o_ref[...] = (x_ref[0].astype(F32) + p_ref[...].astype(F32)).astype(BF16)

    blk = pl.BlockSpec((tr, c), lambda i, cr: (i, 0))
    return pl.pallas_call(
        body, out_shape=tuple([_sds((r, c), BF16)] * n),
        grid_spec=pltpu.PrefetchScalarGridSpec(
            num_scalar_prefetch=1, grid=(r // tr,),
            in_specs=[pl.BlockSpec((1, tr, c), lambda i, cr: (cr[0], i, 0))] * n + [blk] * n,
            out_specs=tuple([blk] * n)),
        name=name, compiler_params=_cp(1))(c_idx, *xhs, *ps)


def sum_chips(qs, owns, chip_idx, name):
    n = len(qs)
    _, r, c = qs[0].shape
    tr = _tile_rows(r, c, max_elems=ELEMWISE_TILE // max(1, n - 1))

    def body(k_ref, *refs):
        k = k_ref[0]
        for q_ref, own_ref, o_ref in zip(refs[:n], refs[n:2 * n], refs[2 * n:]):
            mine = own_ref[0].astype(F32)
            tot = None
            for j in range(N_CHIPS):
                term = jnp.where(k == j, mine, q_ref[j].astype(F32))
                tot = term if tot is None else tot + term
            o_ref[...] = tot

    return pl.pallas_call(
        body, out_shape=tuple([_sds((r, c), F32)] * n),
        grid_spec=pltpu.PrefetchScalarGridSpec(
            num_scalar_prefetch=1, grid=(r // tr,),
            in_specs=([pl.BlockSpec((N_CHIPS, tr, c), lambda i, kr: (0, i, 0))] * n
                      + [pl.BlockSpec((1, tr, c), lambda i, kr: (kr[0], i, 0))] * n),
            out_specs=tuple([pl.BlockSpec((tr, c), lambda i, kr: (i, 0))] * n)),
        name=name, compiler_params=_cp(1))(chip_idx, *qs, *owns)


def adamw(w, g, m, v, name):
    r, c = w.shape
    tr = _tile_rows(r, c, mult=8)
    c1 = 1.0 / (1.0 - ADAM_B1 ** ADAM_STEP)
    c2 = 1.0 / (1.0 - ADAM_B2 ** ADAM_STEP)

    def body(w_ref, g_ref, m_ref, v_ref, d_ref, mo_ref, vo_ref):
        gf = g_ref[...]
        mn = ADAM_B1 * m_ref[...] + (1.0 - ADAM_B1) * gf
        vn = ADAM_B2 * v_ref[...] + (1.0 - ADAM_B2) * (gf * gf)
        mo_ref[...] = mn
        vo_ref[...] = vn
        d_ref[...] = -ADAM_LR * ((mn * c1) / (jnp.sqrt(vn * c2) + ADAM_EPS) + ADAM_WD * w_ref[...])

    blk = pl.BlockSpec((tr, c), lambda i: (i, 0))
    out = _sds((r, c), F32)
    return _pcall(body, out_shape=[out, out, out], grid=(r // tr,), in_specs=[blk] * 4, out_specs=[blk] * 3,
                  args=[w, g, m, v], name=name)


WEIGHTS = ['norm_w', 'ffn_w_gate', 'ffn_w_up', 'ffn_w_down', 'ssm_w_in', 'ssm_conv_w', 'ssm_conv_b', 'ssm_dt_bias',
           'ssm_a_log', 'ssm_d', 'ssm_norm_w', 'ssm_w_out', 'kv_norm_w', 'w_k', 'b_k', 'w_v', 'b_v', 'attn_w_q',
           'attn_b_q', 'attn_sinks', 'attn_w_o', 'attn_b_o', 'final_norm_w']
BIG = ['ffn_w_gate', 'ffn_w_up', 'ffn_w_down', 'ssm_w_in', 'ssm_w_out', 'w_k', 'w_v', 'attn_w_q', 'attn_w_o']
TRANSPOSED = ('ffn_w_gate', 'ffn_w_up', 'ssm_w_in')
SMALL = [n for n in WEIGHTS if n not in BIG]
SMALL_SHARDED = {'norm_w': 2, 'ssm_conv_w': 2, 'ssm_conv_b': 1, 'ssm_norm_w': 1}
ROW_ALIGN = 8 * LANES


def _pack_rows(parts):
    flat = jnp.concatenate([p.reshape(-1).astype(F32) for p in parts])
    pad = (-flat.size) % ROW_ALIGN
    return jnp.pad(flat, (0, pad)).reshape(-1, LANES)


def _unpack_rows(buf, shapes):
    flat = buf.reshape(-1)
    out, pos = [], 0
    for shp in shapes:
        size = math.prod(shp)
        out.append(flat[pos:pos + size].reshape(shp))
        pos += size
    return out


def _as2d(a):
    return a.reshape(-1, a.shape[-1])


def _heads_major(t, n_heads):
    s = t.shape[0]
    return t.reshape(s, n_heads, ATT_HEAD_DIM).transpose(1, 0, 2)


def _tokens_major(t):
    h, s, dh = t.shape
    return t.transpose(1, 0, 2).reshape(s, h * dh)


def _pad_lanes(v):
    return jnp.pad(v.reshape(1, -1), ((0, 0), (0, LANES - v.size)))


def _chips_first(t):
    return t.swapaxes(0, 1).reshape((-1,) + t.shape[3:])


def _parts_first(t, rows):
    return t.reshape((N_CHIPS, N_CORES, rows) + t.shape[1:]).swapaxes(0, 1)


def kernel(*args):
    names = (['x'] + WEIGHTS + ['loss_target'] + ['m_' + n for n in WEIGHTS] + ['v_' + n for n in WEIGHTS])
    a = dict(zip(names, args))
    for n in TRANSPOSED:
        for pre in ('', 'm_', 'v_'):
            a[pre + n] = a[pre + n].swapaxes(-1, -2)
    xi, yi, ci = lax.axis_index("x"), lax.axis_index("y"), lax.axis_index("c")
    chip = 2 * xi + yi
    south = ci == 0
    c_idx = jnp.reshape(ci, (1,)).astype(jnp.int32)
    chip_idx = jnp.reshape(chip, (1,)).astype(jnp.int32)
    x0 = a['x'][0]
    s = x0.shape[0]
    cos, sin = rope_tables(s)

    def own_slot(full, mine):
        return lax.dynamic_update_slice_in_dim(full, mine[:, None], chip, axis=1)

    def ffn_shard(l, i):
        return [a[n][l, i].astype(BF16).reshape(N_CORES, FF_PART, D_MODEL)
                for n in ('ffn_w_gate', 'ffn_w_up', 'ffn_w_down')]

    def own_slots(fulls, mines):
        return [own_slot(f, m) for f, m in zip(fulls, mines)]
    small_names = list(SMALL_SHARDED)
    small_sh = _pack_rows([a[n] for n in small_names])
    small_sh = small_sh.reshape(N_CORES, small_sh.shape[0] // 2, LANES)
    sh00, sh01, sh10, sh11 = ffn_shard(0, 0), ffn_shard(0, 1), ffn_shard(1, 0), ffn_shard(1, 1)
    w_in_sh = jnp.pad(a['ssm_w_in'][0], ((0, IN_SHARD_PAD - IN_SHARD), (0, 0))).astype(BF16).reshape(
        N_CORES, IN_SHARD_PAD // 2, D_MODEL)
    w_out_sh = a['ssm_w_out'][0].astype(BF16).reshape(N_CORES, 256, D_MODEL)
    attn_sh = jnp.stack([a['attn_w_q'][0], a['attn_w_o'][0]]).astype(BF16)
    kv_sh = jnp.stack([a['w_k'], a['w_v']]).astype(BF16)
    in_flight, all_started = split_start(
        [sh00 + [small_sh], [w_in_sh, kv_sh], [w_out_sh], sh01, sh10, [attn_sh], sh11], "gather", "gather_start")

    def arrive(idx, after, tag):
        return forward_cores(split_arrive(in_flight[idx], "gather", after, "gather_arrive_" + tag))

    first = run_exchange(arrive(0, all_started, "first"), "gather_hop_first")
    w00 = own_slots(first[:3], sh00)
    smalls = own_slot(first[3], small_sh)
    p = {}
    per_chip = [_unpack_rows(smalls[:, k], [a[n].shape for n in small_names]) for k in range(N_CHIPS)]
    for idx, n in enumerate(small_names):
        p[n] = jnp.concatenate([per_chip[k][idx] for k in range(N_CHIPS)], axis=SMALL_SHARDED[n])
    nw = p['norm_w']
    conv_w, conv_b, ssm_nw = p['ssm_conv_w'][0], p['ssm_conv_b'][0], p['ssm_norm_w'][0].reshape(1, D_INNER)

    h00 = rmsnorm_fwd(x0, nw[0, 0], "norm_in")
    x1, h01, gu00 = ffn_fwd(h00, x0, *w00, [nw[0, 1]], "ffn_fwd_00")
    w_in_g, kv_g = run_exchange(arrive(1, x1, "in"), "gather_hop_in")
    w_in_t = _chips_first(own_slot(w_in_g, w_in_sh)).reshape(N_CHIPS, IN_SHARD_PAD, D_MODEL)[:, :IN_SHARD].reshape(
        IN_PROJ_DIM, D_MODEL)
    w_dt_t = jnp.pad(w_in_t[D_INNER + CONV_DIM:], ((0, LANES - SSM_HEADS), (0, 0)))
    kv_g = own_slot(kv_g, kv_sh)
    w_k, w_v = kv_g[0].reshape(D_MODEL, KV_DIM), kv_g[1].reshape(D_MODEL, KV_DIM)

    zz = mm_nt(h01, w_in_t, "ssm_in_z", n=D_INNER)
    xbc = mm_nt(h01, w_in_t, "ssm_in_xbc", n=CONV_DIM, row0=D_INNER)
    dtp = mm_nt(h01, w_dt_t, "ssm_in_dt")
    act = conv_fwd(xbc, conv_w, conv_b, "ssm_conv")
    bias_p = _pad_lanes(a['ssm_dt_bias'][0])
    a_p = _pad_lanes(-jnp.exp(a['ssm_a_log'][0]))
    d_p = _pad_lanes(a['ssm_d'][0])
    (yn, y_pre, states), (w_out_g,) = ssd_fwd(act, zz, dtp, bias_p, a_p, d_p, ssm_nw, "ssd_fwd",
                                              ride=arrive(2, act, "out"))
    w_out = _chips_first(own_slot(w_out_g, w_out_sh))
    (x2, h02), w01 = mm_res(yn, w_out, x1, "ssm_out", norm_ws=[nw[0, 2]], ride=arrive(3, yn, "01"))
    w01 = own_slots(w01, sh01)
    x3, hkv, h10, gu01 = ffn_fwd(h02, x2, *w01, [a['kv_norm_w'], nw[1, 0]], "ffn_fwd_01")
    w10 = own_slots(run_exchange(arrive(4, x3, "10"), "gather_hop_10"), sh10)

    k_rot = rope_apply(mm_nn(hkv, w_k, "kv_k", bias=a['b_k']), cos, sin, "rope_k")
    v = mm_nn(hkv, w_v, "kv_v", bias=a['b_v'], out_dtype=BF16)
    kt = _heads_major(k_rot, N_KV_HEADS)
    vt = _heads_major(v, N_KV_HEADS)

    (x4, h11, gu10), (attn_g,) = ffn_fwd(h10, x3, *w10, [nw[1, 1]], "ffn_fwd_10", ride=arrive(5, v, "attn"))
    attn_g = own_slot(attn_g, attn_sh)
    w_q, w_o = attn_g[0].reshape(D_MODEL, D_MODEL), attn_g[1].reshape(D_MODEL, D_MODEL)
    scale = 1.0 / math.sqrt(ATT_HEAD_DIM)
    q_rot = rope_apply(mm_nn(h11, w_q, "attn_q", bias=a['attn_b_q'][0]), cos, sin, "rope_q", scale=scale)
    qt = _heads_major(q_rot, N_Q_HEADS)
    sink_rows = jnp.repeat(a['attn_sinks'][0].reshape(N_KV_HEADS, Q_PER_KV), WINDOW, axis=1).reshape(
        N_KV_HEADS, Q_PER_KV * WINDOW, 1)
    (ot,) = attn_fwd(qt, kt, vt, sink_rows, "attn_fwd")
    o = _tokens_major(ot)
    (x5, h12), w11 = mm_res(o, w_o, x4, "attn_out", bias=a['attn_b_o'][0], norm_ws=[nw[1, 2]],
                            ride=arrive(6, ot, "11"))
    w11 = own_slots(w11, sh11)
    x6, gu11 = ffn_fwd(h12, x5, *w11, [], "ffn_fwd_11")

    loss_v, dx6, d_final = loss_head(x6, a['final_norm_w'], a['loss_target'][0], "loss_head")
    loss = lax.psum(loss_v[0, 0], ("x", "y", "c"))
    g = {'final_norm_w': d_final[0]}

    def same_shape(xs, ys):
        runs = []
        for xv, yv in zip(xs, ys):
            if runs and runs[-1][0][0].shape == xv.shape:
                runs[-1][0].append(xv)
                runs[-1][1].append(yv)
            else:
                runs.append(([xv], [yv]))
        return runs

    def pre_reduce(grads, sib, tag):
        out = []
        for idx, (grp, sbs) in enumerate(same_shape(grads, list(sib))):
            ts = add_pair([gr.reshape(2, -1, gr.shape[-1]) for gr in grp], [_as2d(sb) for sb in sbs], c_idx,
                          "rs_add_%s_%d" % (tag, idx))
            out += [t.reshape(gr.shape[1:]) for t, gr in zip(ts, grp)]
        return out

    def chip_sum(landed, parts, tag):
        out = []
        for idx, (qs, owns) in enumerate(same_shape(list(landed), parts)):
            ts = sum_chips([q.reshape(N_CHIPS, -1, q.shape[-1]) for q in qs],
                           [own.reshape(N_CHIPS, -1, own.shape[-1]) for own in owns], chip_idx,
                           "rs_sum_%s_%d" % (tag, idx))
            out += [t.reshape(q.shape[1:]) for t, q in zip(ts, qs)]
        return out

    dnw = [[None] * 3 for _ in range(2)]
    sums = {}

    def trade(key):
        return swap_cores(sums[key], False)

    dx5, dnw12, *g11 = ffn_bwd(dx6, h12, x5, nw[1, 2], gu11, *w11, "ffn_bwd_11")
    dnw[1][2] = dnw12[0]
    (d_wo, g['attn_b_o']), sib11 = mm_tn(o, dx5, "attn_dwo", col_sum=True, ride=swap_cores(g11, True))
    t11 = pre_reduce(g11, sib11, "11")
    do = mm_nt(dx5, w_o, "attn_do", out_dtype=BF16)
    (dqt, dkt, dvt, dsink), land11 = attn_bwd(qt, kt, vt, sink_rows, _heads_major(do, N_Q_HEADS), "attn_bwd",
                                             ride=scatter_chips(t11[:2]))
    g['attn_sinks'] = jnp.sum(dsink[:, :, :Q_PER_KV, 0], axis=1).reshape(N_Q_HEADS)
    dq_pre = rope_apply(_tokens_major(dqt), cos, sin, "rope_dq", inverse=True, scale=scale, out_dtype=F32)
    d_wq, g['attn_b_q'] = mm_tn(h11, dq_pre, "attn_dwq", col_sum=True)
    g_attn = [jnp.stack([d_wq.reshape(N_CHIPS, 256, D_MODEL), d_wo.reshape(N_CHIPS, 256, D_MODEL)])]
    (dx4, dnw[1][1]), sib_attn = mm_rms_bwd([(dq_pre, 0, w_q, 0, D_MODEL, "nt")], dx5, x4, nw[1, 1], "attn_bwd_dh",
                                            ride=swap_cores(g_attn, True))
    t_attn = pre_reduce(g_attn, sib_attn, "attn")
    (dx3, dnw10, *g10), landed = ffn_bwd(dx4, h10, x3, nw[1, 0], gu10, *w10, "ffn_bwd_10",
                                         ride=scatter_chips(t_attn + t11[2:]))
    dnw[1][0] = dnw10[0]
    sums['attn'] = chip_sum(landed[:1], t_attn, "attn")
    sums['11'] = chip_sum(list(land11) + list(landed[1:]), t11, "11")
    dk_pre = rope_apply(_tokens_major(dkt), cos, sin, "rope_dk", inverse=True, out_dtype=F32)
    dv = _tokens_major(dvt)
    (d_wk, g['b_k']), sib10 = mm_tn(hkv, dk_pre, "kv_dwk", col_sum=True, ride=swap_cores(g10, True))
    t10 = pre_reduce(g10, sib10, "10")
    d_wv, g['b_v'] = mm_tn(hkv, dv, "kv_dwv", col_sum=True)
    g_kv = [jnp.stack([d_wk.reshape(N_CHIPS, 256, KV_DIM), d_wv.reshape(N_CHIPS, 256, KV_DIM)])]
    (dx3, g['kv_norm_w']), sib_kv = mm_rms_bwd(
        [(dk_pre, 0, w_k, 0, KV_DIM, "nt"), (dv, 0, w_v, 0, KV_DIM, "nt")], dx3, x3, a['kv_norm_w'], "kv_bwd_dh",
        ride=swap_cores(g_kv, True))
    t_kv = pre_reduce(g_kv, sib_kv, "kv")
    (dx2, dnw02, *g01), landed = ffn_bwd(dx3, h02, x2, nw[0, 2], gu01, *w01, "ffn_bwd_01",
                                         ride=join(scatter_chips(t10 + t_kv), trade('11'), trade('attn')))
    dnw[0][2] = dnw02[0]
    sums['10'] = chip_sum(landed[:3], t10, "10")
    sums['kv'] = chip_sum(landed[3:4], t_kv, "kv")
    theirs = {'11': landed[4:7], 'attn': landed[7:]}
    d_wout, sib01 = mm_tn(yn, dx2, "ssm_dwout", ride=swap_cores(g01, True))
    t01 = pre_reduce(g01, sib01, "01")
    dyn = mm_nt(dx2, w_out, "ssm_dyn")
    (dxs, db_, dc_, dz, ddt, d_ssm_nw, d_bias, d_a, d_d), landed = ssd_bwd(
        dyn, act, zz, y_pre, states, dtp, bias_p, a_p, d_p, ssm_nw, "ssd_bwd",
        ride=join(scatter_chips(t01[:2]), trade('10'), trade('kv')))
    land01 = list(landed[:2])
    theirs['10'], theirs['kv'] = landed[2:5], landed[5:]
    g['ssm_norm_w'] = d_ssm_nw[:SSM_GROUPS].reshape(D_INNER)
    g['ssm_dt_bias'] = d_bias[0, :SSM_HEADS]
    g['ssm_a_log'] = d_a[0, :SSM_HEADS] * a_p[0, :SSM_HEADS]
    g['ssm_d'] = d_d[0, :SSM_HEADS]
    (dxbc, g['ssm_conv_w'], g['ssm_conv_b']), landed = conv_bwd(dxs, db_, dc_, xbc, conv_w, conv_b, "ssm_conv_bwd",
                                                                ride=scatter_chips(t01[2:]))
    sums['01'] = chip_sum(land01 + list(landed), t01, "01")
    d_win = mm_tn(dz, h01, "ssm_dwz", rows=IN_PROJ_DIM)
    d_win = mm_tn(dxbc, h01, "ssm_dwxbc", into=d_win, rows=IN_PROJ_DIM, row0=D_INNER)
    d_win = mm_tn(ddt, h01, "ssm_dwdt", into=d_win, rows=IN_PROJ_DIM, row0=D_INNER + CONV_DIM, m_valid=SSM_HEADS)
    d_win = jnp.pad(d_win.reshape(N_CHIPS, IN_SHARD, D_MODEL), ((0, 0), (0, IN_SHARD_PAD - IN_SHARD), (0, 0)))
    g_ssm = [_parts_first(d_win.reshape(-1, D_MODEL), IN_SHARD_PAD // 2), _parts_first(d_wout, 256)]
    kb = 1024
    terms = ([(dz, j, w_in_t, j, kb, "nn") for j in range(D_INNER // kb)]
             + [(dxbc, j, w_in_t, D_INNER // kb + j, kb, "nn") for j in range(CONV_DIM // kb)]
             + [(ddt, 0, w_dt_t, 0, LANES, "nn")])
    (dx1, dnw[0][1]), sib_ssm = mm_rms_bwd(terms, dx2, x1, nw[0, 1], "ssm_bwd_dh", ride=swap_cores(g_ssm, True))
    t_ssm = pre_reduce(g_ssm, sib_ssm, "ssm")
    (grad_x, dnw00, *g00), landed = ffn_bwd(dx1, h00, x0, nw[0, 0], gu00, *w00, "ffn_bwd_00",
                                            ride=join(scatter_chips(t_ssm), trade('01')))
    dnw[0][0] = dnw00[0]
    sums['ssm'] = chip_sum(landed[:2], t_ssm, "ssm")
    theirs['01'] = landed[2:]
    landed = run_exchange(join(swap_cores(g00, True), trade('ssm')), "rs_swap_00")
    t00 = pre_reduce(g00, landed[:3], "00")
    theirs['ssm'] = landed[3:]

    def both(key):
        return [(jnp.where(south, m_, t_), jnp.where(south, t_, m_)) for m_, t_ in zip(sums[key], theirs[key])]

    g['norm_w'] = jnp.stack([jnp.stack(r) for r in dnw])
    red = all_reduce_small(_pack_rows([g[n] for n in SMALL]), "reduce_vectors")

    t00 = lax.optimization_barrier((red, t00))[1]
    (flight00,), flying = split_start([t00], "scatter", "rs_scatter_00_start")

    def held(val):
        return lax.optimization_barrier((flying, val))[1]

    delta, new_m, new_v, gw = {}, {}, {}, {}
    ffn_names = ('ffn_w_gate', 'ffn_w_up', 'ffn_w_down')
    full = {key: both(key) for key in ('attn', 'kv', 'ssm')}
    lo, hi = full['attn'][0]
    gw['attn_w_q'], gw['attn_w_o'] = lo[None], hi[None]
    lo, hi = full['kv'][0]
    gw['w_k'], gw['w_v'] = lo, hi
    lo, hi = full['ssm'][0]
    gw['ssm_w_in'] = jnp.concatenate([lo, hi], axis=0)[:IN_SHARD][None]
    lo, hi = full['ssm'][1]
    gw['ssm_w_out'] = jnp.concatenate([lo, hi], axis=0)[None]

    for n, t in zip(SMALL, _unpack_rows(red, [g[n].shape for n in SMALL])):
        if n in SMALL_SHARDED:
            ax = SMALL_SHARDED[n] - (a[n].ndim - t.ndim)
            width = a[n].shape[SMALL_SHARDED[n]]
            t = lax.dynamic_slice_in_dim(t, chip * width, width, axis=ax)
        gw[n] = t.reshape(a[n].shape)

    def update(n):
        d, mo, vo = adamw(_as2d(a[n]), held(_as2d(gw[n])), _as2d(a['m_' + n]), _as2d(a['v_' + n]), "adamw_" + n)
        delta[n], new_m[n], new_v[n] = d.reshape(a[n].shape), mo.reshape(a[n].shape), vo.reshape(a[n].shape)

    for n in BIG:
        if n not in ffn_names:
            update(n)
    shapes = [a[n].shape for n in SMALL]
    packed = [_pack_rows([src[n] for n in SMALL]) for src in
              (a, gw, {n: a['m_' + n] for n in SMALL}, {n: a['v_' + n] for n in SMALL})]
    outs = adamw(*packed, "adamw_vectors")
    for dst, buf in zip((delta, new_m, new_v), outs):
        for n, t in zip(SMALL, _unpack_rows(buf, shapes)):
            dst[n] = t
    for key in ('01', '10', '11'):
        sums[key] = held(list(sums[key]))
    rest = [both(key) for key in ('01', '10', '11')]
    done = lax.optimization_barrier((outs[0], [delta[n] for n in BIG if n not in ffn_names], rest))[0]
    land00 = split_arrive(flight00, "scatter", done, "rs_scatter_00_arrive")
    sums['00'] = chip_sum(land00, t00, "00")
    theirs['00'] = run_exchange(trade('00'), "rs_trade_00")
    blocks = [both('00')] + rest
    for t, n in enumerate(ffn_names):
        gw[n] = jnp.concatenate([piece for blk in blocks for piece in blk[t]], axis=0).reshape(a[n].shape)
        update(n)
    for n in TRANSPOSED:
        for dst in (gw, delta, new_m, new_v):
            dst[n] = dst[n].swapaxes(-1, -2)

    return (loss, grad_x[None], *[gw[n] for n in WEIGHTS], *[delta[n] for n in WEIGHTS],
            *[new_m[n] for n in WEIGHTS], *[new_v[n] for n in WEIGHTS])
```

```python
import math

import jax
import jax.numpy as jnp
from jax import lax
from jax.experimental import pallas as pl
from jax.experimental.pallas import tpu as pltpu

F32 = jnp.float32
BF16 = jnp.bfloat16

D_MODEL = 1024
D_INNER = 2048
SSM_HEADS = 32
SSM_GROUPS = 4
HEADS_PER_GROUP = SSM_HEADS // SSM_GROUPS
SSM_HEAD_DIM = 64
SSM_STATE = 128
GROUP_DIM = D_INNER // SSM_GROUPS
CONV_DIM = D_INNER + 2 * SSM_GROUPS * SSM_STATE
CONV_WIDTH = 4
CHUNK = 128
ATT_HEAD_DIM = 64
N_Q_HEADS = 16
N_KV_HEADS = 4
Q_PER_KV = N_Q_HEADS // N_KV_HEADS
KV_DIM = N_KV_HEADS * ATT_HEAD_DIM
WINDOW = 128
ROPE_THETA = 10000.0
D_FF = 2816
N_CHIPS = 4
N_CORES = 2
FF_SHARD = D_FF // N_CHIPS
FF_PART = FF_SHARD // N_CORES
IN_PROJ_DIM = D_INNER + CONV_DIM + SSM_HEADS
IN_SHARD = IN_PROJ_DIM // N_CHIPS
IN_SHARD_PAD = 1312
EPS = 1e-5
NEG = -1e30
LANES = 128
VMEM_LIMIT = 56 * 1024 * 1024

ADAM_LR = 0.001
ADAM_B1 = 0.9
ADAM_B2 = 0.999
ADAM_EPS = 1e-08
ADAM_WD = 0.01
ADAM_STEP = 10

NN = ((1,), (0,))
NT = ((1,), (1,))
TN = ((0,), (0,))
MESH = pl.DeviceIdType.MESH
ANY = pl.BlockSpec(memory_space=pl.ANY)


def _dot(a, b, dims=NN, precision=None):
    return lax.dot_general(a, b, (dims, ((), ())), preferred_element_type=F32, precision=precision)


def _cp(n_grid):
    return pltpu.CompilerParams(dimension_semantics=("arbitrary",) * n_grid, vmem_limit_bytes=VMEM_LIMIT)


def _sigmoid(x):
    return 1.0 / (1.0 + jnp.exp(-x))


def _rms_fwd(xf, w):
    r = lax.rsqrt(jnp.mean(xf * xf, axis=-1, keepdims=True) + EPS)
    return xf * r * w


def _rms_bwd(dh, xf, w):
    r = lax.rsqrt(jnp.mean(xf * xf, axis=-1, keepdims=True) + EPS)
    xhat = xf * r
    dxhat = dh * w
    dx = r * (dxhat - xhat * jnp.mean(dxhat * xhat, axis=-1, keepdims=True))
    return dx, dh * xhat


def _row_tile(s, pref):
    return pref if s % pref == 0 else s


def _col_tile(n):
    for t in (1024, 768, 512, 256, 128):
        if n % t == 0:
            return t
    return n


def _sds(shape, dtype):
    return jax.ShapeDtypeStruct(tuple(shape), dtype)


class Exchange:
    def __init__(self, ins, out_shapes, sems, start, finish, inplace=False):
        self.ins, self.out_shapes, self.sems, self.start, self.finish = ins, out_shapes, sems, start, finish
        self.inplace = inplace


def _place():
    x, y, c = lax.axis_index("x"), lax.axis_index("y"), lax.axis_index("c")
    others = [(1 - x, y), (x, 1 - y), (1 - x, 1 - y)]
    return x, y, c, 2 * x + y, others


def _rc(src, dst, send_sem, recv_sem, dev):
    return pltpu.make_async_remote_copy(src_ref=src, dst_ref=dst, send_sem=send_sem, recv_sem=recv_sem,
                                        device_id=dev, device_id_type=MESH)


def scatter_chips(arrs):
    n = len(arrs)

    def copies(ins, outs, sems):
        send, recv = sems
        x, y, c, k, others = _place()
        out, land = [], []
        for a in range(n):
            for j, (px, py) in enumerate(others):
                out.append(_rc(ins[a].at[2 * px + py], outs[a].at[k], send.at[a, j], recv.at[a, j], (px, py, c)))
                blk = outs[a].at[2 * px + py]
                land.append(_rc(blk, blk, send.at[a, j], recv.at[a, j], (px, py, c)))
        return out, land

    def start(ins, outs, sems):
        for cp in copies(ins, outs, sems)[0]:
            cp.start()

    def finish(ins, outs, sems):
        out, land = copies(ins, outs, sems)
        for arrived in land:
            arrived.wait_recv()
        for cp in out:
            cp.wait_send()

    return Exchange(list(arrs), [_sds(a.shape, a.dtype) for a in arrs],
                    [pltpu.SemaphoreType.DMA((n, 3)), pltpu.SemaphoreType.DMA((n, 3))], start, finish)


def swap_cores(arrs, pick_other):
    n = len(arrs)

    def copies(ins, outs, sems):
        send, recv = sems
        x, y, c, _, _ = _place()
        return [_rc(ins[a].at[1 - c] if pick_other else ins[a], outs[a], send.at[a], recv.at[a], (x, y, 1 - c))
                for a in range(n)]

    def start(ins, outs, sems):
        for cp in copies(ins, outs, sems):
            cp.start()

    def finish(ins, outs, sems):
        for cp in copies(ins, outs, sems):
            cp.wait()

    shapes = [_sds(a.shape[1:] if pick_other else a.shape, a.dtype) for a in arrs]
    return Exchange(list(arrs), shapes, [pltpu.SemaphoreType.DMA((n,)), pltpu.SemaphoreType.DMA((n,))],
                    start, finish)


def join(*parts):
    parts = [p for p in parts if p is not None]
    if not parts:
        return None

    def split(refs, counts):
        out, pos = [], 0
        for cnt in counts:
            out.append(refs[pos:pos + cnt])
            pos += cnt
        return out

    n_in = [len(p.ins) for p in parts]
    n_out = [len(p.out_shapes) for p in parts]
    n_sem = [len(p.sems) for p in parts]

    def run(which):
        def go(ins, outs, sems):
            for p, i, o, s in zip(parts, split(ins, n_in), split(outs, n_out), split(sems, n_sem)):
                getattr(p, which)(i, o, s)
        return go

    return Exchange([a for p in parts for a in p.ins], [s for p in parts for s in p.out_shapes],
                    [s for p in parts for s in p.sems], run("start"), run("finish"))


def _pcall(body, *, out_shape, grid, in_specs, out_specs, args, name, scratch_shapes=(), ride=None, aliases=None):
    out_shape, out_specs, in_specs = tuple(out_shape), tuple(out_specs), list(in_specs)
    aliases = aliases or {}
    if ride is None:
        return pl.pallas_call(body, out_shape=out_shape, grid=grid, in_specs=in_specs, out_specs=out_specs,
                              scratch_shapes=list(scratch_shapes), input_output_aliases=aliases, name=name,
                              compiler_params=_cp(len(grid)))(*args)
    n_in, n_out, n_sc = len(args), len(out_shape), len(scratch_shapes)
    n_xi, n_xo = len(ride.ins), len(ride.out_shapes)

    def wrapped(*refs):
        pos = [0]

        def take(cnt):
            got = refs[pos[0]:pos[0] + cnt]
            pos[0] += cnt
            return got

        c_in, x_in, c_out, x_out, c_sc = take(n_in), take(n_xi), take(n_out), take(n_xo), take(n_sc)
        sems = refs[pos[0]:]
        first, last = True, True
        for d, size in enumerate(grid):
            first = jnp.logical_and(first, pl.program_id(d) == 0)
            last = jnp.logical_and(last, pl.program_id(d) == size - 1)

        @pl.when(first)
        def _():
            ride.start(x_in, x_out, sems)

        body(*c_in, *c_out, *c_sc)

        @pl.when(last)
        def _():
            ride.finish(x_in, x_out, sems)

    if ride.inplace:
        aliases = {**aliases, **{n_in + t: n_out + t for t in range(n_xi)}}
    res = pl.pallas_call(
        wrapped, out_shape=out_shape + tuple(ride.out_shapes), grid=grid,
        in_specs=in_specs + [ANY] * n_xi, out_specs=out_specs + (ANY,) * n_xo,
        scratch_shapes=list(scratch_shapes) + list(ride.sems), input_output_aliases=aliases, name=name,
        compiler_params=_cp(len(grid)))(*args, *ride.ins)
    return res[:n_out], res[n_out:]


def run_exchange(ex, name):
    n_xi, n_xo = len(ex.ins), len(ex.out_shapes)

    def body(*refs):
        ins, outs, sems = refs[:n_xi], refs[n_xi:n_xi + n_xo], refs[n_xi + n_xo:]
        ex.start(ins, outs, sems)
        ex.finish(ins, outs, sems)

    aliases = {t: t for t in range(n_xi)} if ex.inplace else {}
    return pl.pallas_call(body, out_shape=tuple(ex.out_shapes), in_specs=[ANY] * n_xi, out_specs=(ANY,) * n_xo,
                          scratch_shapes=list(ex.sems), input_output_aliases=aliases, name=name)(*ex.ins)


HBM_SPEC = pl.BlockSpec(memory_space=pltpu.HBM)
SEM_SPEC = pl.BlockSpec(memory_space=pltpu.SEMAPHORE)
EFFECT = pltpu.SideEffectType.DATAFLOW_SIDE_EFFECTING


def _route(kind, src, dst, c, k, peer):
    if kind == "gather":
        return src.at[c], dst.at[c, k], dst.at[c, peer]
    return src.at[peer], dst.at[k], dst.at[peer]


def split_start(batches, kind, name):
    flat = [a for batch in batches for a in batch]
    n, nb = len(flat), len(batches)
    lands = [lax.empty((2, N_CHIPS) + a.shape[1:] if kind == "gather" else a.shape, a.dtype) for a in flat]

    def body(*refs):
        srcs, dsts, sems, token = refs[:n], refs[n:2 * n], refs[2 * n:2 * n + 2 * nb], refs[-1]
        x, y, c, k, others = _place()
        pos = 0
        for b, batch in enumerate(batches):
            for a in range(len(batch)):
                for j, (px, py) in enumerate(others):
                    src, dst, _ = _route(kind, srcs[pos], dsts[pos], c, k, 2 * px + py)
                    _rc(src, dst, sems[2 * b].at[3 * a + j], sems[2 * b + 1].at[3 * a + j], (px, py, c)).start()
                pos += 1
        token[...] = jnp.zeros(token.shape, token.dtype)

    sem_shapes = [pltpu.SemaphoreType.DMA((3 * len(batch),)) for batch in batches for _ in range(2)]
    thru = [pltpu.HBM(a.shape, a.dtype) for a in flat] + [pltpu.HBM(l.shape, l.dtype) for l in lands]
    res = pl.pallas_call(
        body, name=name, out_shape=tuple(sem_shapes + thru + [_sds((8, LANES), F32)]),
        in_specs=[HBM_SPEC] * (2 * n),
        out_specs=tuple([SEM_SPEC] * (2 * nb) + [HBM_SPEC] * (2 * n) + [pl.BlockSpec(memory_space=pltpu.VMEM)]),
        input_output_aliases={t: 2 * nb + t for t in range(2 * n)},
        compiler_params=pltpu.CompilerParams(has_side_effects=EFFECT),
    )(*[pltpu.with_memory_space_constraint(t, pltpu.HBM) for t in flat + lands])
    sems, srcs, dsts = res[:2 * nb], res[2 * nb:2 * nb + n], res[2 * nb + n:2 * nb + 2 * n]
    out, pos = [], 0
    for b, batch in enumerate(batches):
        out.append((sems[2 * b], sems[2 * b + 1], list(srcs[pos:pos + len(batch)]), list(dsts[pos:pos + len(batch)])))
        pos += len(batch)
    return out, res[-1]


def split_arrive(handle, kind, after, name):
    send, recv, srcs, dsts = handle
    n = len(srcs)

    def body(*refs):
        s_refs, d_refs, send_ref, recv_ref = refs[:n], refs[n:2 * n], refs[2 * n], refs[2 * n + 1]
        x, y, c, k, others = _place()
        for a in range(n):
            for j, (px, py) in enumerate(others):
                src, _, landed = _route(kind, s_refs[a], d_refs[a], c, k, 2 * px + py)
                cp = _rc(src, landed, send_ref.at[3 * a + j], recv_ref.at[3 * a + j], (px, py, c))
                cp.wait_send()
                cp.wait_recv()

    res = pl.pallas_call(
        body, name=name, out_shape=tuple([pltpu.HBM(t.shape, t.dtype) for t in srcs + dsts]),
        in_specs=[HBM_SPEC] * (2 * n) + [SEM_SPEC, SEM_SPEC, ANY], out_specs=tuple([HBM_SPEC] * (2 * n)),
        input_output_aliases={t: t for t in range(2 * n)},
        compiler_params=pltpu.CompilerParams(has_side_effects=EFFECT),
    )(*srcs, *dsts, send, recv, after)
    return list(res[n:])


def forward_cores(bufs):
    n = len(bufs)

    def copies(outs, sems):
        send, recv = sems
        x, y, c, k, others = _place()
        onward, land = [], []
        for a in range(n):
            for j, (px, py) in enumerate(others):
                blk = outs[a].at[c, 2 * px + py]
                onward.append(_rc(blk, blk, send.at[a, j], recv.at[a, j], (x, y, 1 - c)))
                blk2 = outs[a].at[1 - c, 2 * px + py]
                land.append(_rc(blk2, blk2, send.at[a, j], recv.at[a, j], (x, y, 1 - c)))
        return onward, land

    def start(ins, outs, sems):
        for cp in copies(outs, sems)[0]:
            cp.start()

    def finish(ins, outs, sems):
        onward, land = copies(outs, sems)
        for arrived in land:
            arrived.wait_recv()
        for cp in onward:
            cp.wait_send()

    return Exchange(list(bufs), [_sds(b.shape, b.dtype) for b in bufs],
                    [pltpu.SemaphoreType.DMA((n, 3)), pltpu.SemaphoreType.DMA((n, 3))], start, finish, inplace=True)


def all_reduce_small(buf, name):
    r = buf.shape[0]
    n_dev = 8

    def body(in_ref, o_ref, land, send_sems, recv_sems):
        x, y, c, _, _ = _place()
        me = 4 * x + 2 * y + c
        land[me] = in_ref[...]
        sends = []
        for d in range(1, n_dev):
            peer = (x ^ (d >> 2), y ^ ((d >> 1) & 1), c ^ (d & 1))
            cp = _rc(in_ref, land.at[me], send_sems.at[d], recv_sems.at[d], peer)
            cp.start()
            sends.append(cp)
        for d in range(1, n_dev):
            blk = land.at[me ^ d]
            _rc(blk, blk, send_sems.at[d], recv_sems.at[d], (x, y, c)).wait_recv()
        for cp in sends:
            cp.wait_send()
        tot = land[0]
        for d in range(1, n_dev):
            tot = tot + land[d]
        o_ref[...] = tot

    vm = pl.BlockSpec(memory_space=pltpu.VMEM)
    return pl.pallas_call(
        body, out_shape=_sds(buf.shape, F32), in_specs=[vm], out_specs=vm,
        scratch_shapes=[pltpu.VMEM((n_dev, r, LANES), F32), pltpu.SemaphoreType.DMA((n_dev,)),
                        pltpu.SemaphoreType.DMA((n_dev,))],
        name=name)(buf)


def rmsnorm_fwd(x, w, name):
    s, d = x.shape
    tm = _row_tile(s, 512)

    def body(x_ref, w_ref, o_ref):
        o_ref[...] = _rms_fwd(x_ref[...], w_ref[...]).astype(BF16)

    return _pcall(body, out_shape=[_sds((s, d), BF16)], grid=(s // tm,),
                  in_specs=[pl.BlockSpec((tm, d), lambda i: (i, 0)), pl.BlockSpec((1, d), lambda i: (0, 0))],
                  out_specs=[pl.BlockSpec((tm, d), lambda i: (i, 0))], args=[x, w.reshape(1, d)], name=name)[0]


def _ffn_w_spec(chip_of, single=False):
    mode = dict(pipeline_mode=pl.Buffered(1)) if single else {}
    return pl.BlockSpec((N_CORES, 1, FF_PART, D_MODEL), lambda *ids: (0, chip_of(*ids), 0, 0), **mode)


def ffn_fwd(h, x, wg, wu, wd, norm_ws, name, ride=None):
    s, d = h.shape
    n_norm = len(norm_ws)
    tm = _row_tile(s, 1024)

    def body(*refs):
        h_ref, x_ref, wg_ref, wu_ref, wd_ref = refs[:5]
        nw_refs = refs[5:5 + n_norm]
        o_ref = refs[5 + n_norm]
        h_refs = refs[6 + n_norm:6 + 2 * n_norm]
        gu_ref, acc = refs[6 + 2 * n_norm], refs[7 + 2 * n_norm]
        k = pl.program_id(1)

        @pl.when(k == 0)
        def _():
            acc[...] = jnp.zeros(acc.shape, F32)

        hm = tm // 2
        for part in range(2):
            sub = pl.ds(part * hm, hm)
            hb = h_ref[sub, :]
            g = _dot(hb, wg_ref[...].reshape(FF_SHARD, d), NT)
            u = _dot(hb, wu_ref[...].reshape(FF_SHARD, d), NT)
            gu_ref[0, 0, sub, :] = g.astype(BF16)
            gu_ref[0, 1, sub, :] = u.astype(BF16)
            acc[sub, :] += _dot((g * _sigmoid(g) * u).astype(BF16), wd_ref[...].reshape(FF_SHARD, d))

        @pl.when(k == N_CHIPS - 1)
        def _():
            xn = x_ref[...] + 0.5 * acc[...]
            o_ref[...] = xn
            for nw_ref, hn_ref in zip(nw_refs, h_refs):
                hn_ref[...] = _rms_fwd(xn, nw_ref[...]).astype(BF16)

    row = pl.BlockSpec((tm, d), lambda i, k: (i, 0))
    vec = pl.BlockSpec((1, d), lambda i, k: (0, 0))
    wsp = _ffn_w_spec(lambda i, k: k)
    return _pcall(
        body, out_shape=[_sds((s, d), F32)] + [_sds((s, d), BF16)] * n_norm + [_sds((N_CHIPS, 2, s, FF_SHARD), BF16)],
        grid=(s // tm, N_CHIPS),
        in_specs=[row, row, wsp, wsp, wsp] + [vec] * n_norm,
        out_specs=[row] * (1 + n_norm) + [pl.BlockSpec((1, 2, tm, FF_SHARD), lambda i, k: (k, 0, i, 0))],
        scratch_shapes=[pltpu.VMEM((tm, d), F32)],
        args=[h, x, wg, wu, wd] + [nw.reshape(1, d) for nw in norm_ws], name=name, ride=ride)


def ffn_bwd(dxn, h, x_in, nw, gu, wg, wu, wd, name, ride=None):
    s, d = h.shape
    tm = _row_tile(s, 512)
    ni = s // tm
    last_e = N_CHIPS - 1

    def body(dxn_ref, h_ref, x_ref, nw_ref, gu_ref, wg_ref, wu_ref, wd_ref,
             dx_ref, dnw_ref, dwg_ref, dwu_ref, dwd_ref, dh, wacc):
        e = pl.program_id(0)
        i = pl.program_id(1)
        rows = pl.ds(pl.multiple_of(i * tm, tm), tm)

        @pl.when(i == 0)
        def _():
            wacc[...] = jnp.zeros(wacc.shape, F32)

        @pl.when(e == 0)
        def _():
            dh[rows, :] = jnp.zeros((tm, d), F32)

        hm = tm // 2
        for part in range(2):
            sub = pl.ds(part * hm, hm)
            dxb = dxn_ref[sub, :].astype(BF16)
            hb = h_ref[sub, :]
            g = gu_ref[0, 0, sub, :].astype(F32)
            u = gu_ref[0, 1, sub, :].astype(F32)
            drows = pl.ds(pl.multiple_of(i * tm + part * hm, hm), hm)
            sg = _sigmoid(g)
            silu = g * sg
            wacc[2] += _dot((0.5 * silu * u).astype(BF16), dxb, TN)
            da = 0.5 * _dot(dxb, wd_ref[...].reshape(FF_SHARD, d), NT)
            dg = (da * u * (sg * (1.0 + g * (1.0 - sg)))).astype(BF16)
            wacc[0] += _dot(dg, hb, TN)
            du = (da * silu).astype(BF16)
            dh[drows, :] += _dot(dg, wg_ref[...].reshape(FF_SHARD, d))
            wacc[1] += _dot(du, hb, TN)
            dh[drows, :] += _dot(du, wu_ref[...].reshape(FF_SHARD, d))

        @pl.when(i == ni - 1)
        def _():
            for t, dw_ref in enumerate((dwg_ref, dwu_ref, dwd_ref)):
                dw_ref[...] = wacc[t].astype(BF16).reshape(N_CORES, 1, FF_PART, d)

        @pl.when(e == last_e)
        def _():
            dx, dnw = _rms_bwd(dh[rows, :], x_ref[...], nw_ref[...])
            dx_ref[...] = dxn_ref[...] + dx
            col = jnp.sum(dnw, axis=0, keepdims=True)

            @pl.when(i == 0)
            def _():
                dnw_ref[...] = col

            @pl.when(i > 0)
            def _():
                dnw_ref[...] += col

    row = pl.BlockSpec((tm, d), lambda e, i: (i, 0))
    late = pl.BlockSpec((tm, d), lambda e, i: (jnp.where(e == last_e, i, 0), 0))
    vec = pl.BlockSpec((1, d), lambda e, i: (0, 0))
    wsp = _ffn_w_spec(lambda e, i: e, single=True)
    dwsp = _ffn_w_spec(lambda e, i: e, single=True)
    dw = _sds((N_CORES, N_CHIPS, FF_PART, d), BF16)
    return _pcall(
        body, out_shape=[_sds((s, d), F32), _sds((1, d), F32), dw, dw, dw],
        grid=(N_CHIPS, ni),
        in_specs=[row, row, late, vec, pl.BlockSpec((1, 2, tm, FF_SHARD), lambda e, i: (e, 0, i, 0)), wsp, wsp, wsp],
        out_specs=[late, vec, dwsp, dwsp, dwsp],
        scratch_shapes=[pltpu.VMEM((s, d), F32), pltpu.VMEM((3, FF_SHARD, d), F32)],
        args=[dxn, h, x_in, nw.reshape(1, d), gu, wg, wu, wd], name=name, ride=ride)


def mm_res(a, w, x, name, bias=None, norm_ws=(), ride=None):
    s, k = a.shape
    n = w.shape[1]
    tm = _row_tile(s, 512)
    has_bias = bias is not None
    n_norm = len(norm_ws)

    def body(*refs):
        a_ref, w_ref, x_ref = refs[:3]
        pos = 3
        t = _dot(a_ref[...], w_ref[...])
        if has_bias:
            t = t + refs[pos][...]
            pos += 1
        nw_refs = refs[pos:pos + n_norm]
        o_ref = refs[pos + n_norm]
        h_refs = refs[pos + n_norm + 1:]
        xn = x_ref[...] + t
        o_ref[...] = xn
        for nw_ref, h_ref in zip(nw_refs, h_refs):
            h_ref[...] = _rms_fwd(xn, nw_ref[...]).astype(BF16)

    row = pl.BlockSpec((tm, n), lambda i: (i, 0))
    vec = pl.BlockSpec((1, n), lambda i: (0, 0))
    in_specs = [pl.BlockSpec((tm, k), lambda i: (i, 0)), pl.BlockSpec((k, n), lambda i: (0, 0)), row]
    args = [a, w, x]
    if has_bias:
        in_specs.append(vec)
        args.append(bias.reshape(1, n))
    for nw in norm_ws:
        in_specs.append(vec)
        args.append(nw.reshape(1, n))
    return _pcall(body, out_shape=[_sds((s, n), F32)] + [_sds((s, n), BF16)] * n_norm, grid=(s // tm,),
                  in_specs=in_specs, out_specs=[row] * (1 + n_norm), args=args, name=name, ride=ride)


def mm_nn(a, w, name, bias=None, out_dtype=F32):
    s, k = a.shape
    n = w.shape[1]
    tm = _row_tile(s, 512)
    tn = _col_tile(n)
    has_bias = bias is not None

    def body(*refs):
        a_ref, w_ref = refs[:2]
        o_ref = refs[-1]
        t = _dot(a_ref[...], w_ref[...])
        if has_bias:
            t = t + refs[2][...]
        o_ref[...] = t.astype(out_dtype)

    in_specs = [pl.BlockSpec((tm, k), lambda j, i: (i, 0)), pl.BlockSpec((k, tn), lambda j, i: (0, j))]
    args = [a, w]
    if has_bias:
        in_specs.append(pl.BlockSpec((1, tn), lambda j, i: (0, j)))
        args.append(bias.reshape(1, n))
    return _pcall(body, out_shape=[_sds((s, n), out_dtype)], grid=(n // tn, s // tm), in_specs=in_specs,
                  out_specs=[pl.BlockSpec((tm, tn), lambda j, i: (i, j))], args=args, name=name)[0]


def mm_nt(a, w, name, n=None, row0=0, out_dtype=F32, ride=None):
    s, k = a.shape
    n = w.shape[0] if n is None else n
    tm = _row_tile(s, 512)
    tn = _col_tile(n)
    base = row0 // tn
    assert row0 % tn == 0

    def body(a_ref, w_ref, o_ref):
        o_ref[...] = _dot(a_ref[...].astype(BF16), w_ref[...], NT).astype(out_dtype)

    res = _pcall(body, out_shape=[_sds((s, n), out_dtype)], grid=(n // tn, s // tm),
                 in_specs=[pl.BlockSpec((tm, k), lambda j, i: (i, 0)), pl.BlockSpec((tn, k), lambda j, i: (base + j, 0))],
                 out_specs=[pl.BlockSpec((tm, tn), lambda j, i: (i, j))], args=[a, w], name=name, ride=ride)
    return res[0] if ride is None else (res[0][0], res[1])


def mm_tn(a, b, name, into=None, rows=None, row0=0, m_valid=None, col_sum=False, ride=None):
    s, m = a.shape
    n = b.shape[1]
    mv = m if m_valid is None else m_valid
    tm = _col_tile(m) if m_valid is None else mv
    tn = 512 if n % 512 == 0 else n
    rows = mv if rows is None else rows
    assert row0 % tm == 0 and (m_valid is None or m == LANES)
    assert not col_sum or mv == tm
    base = row0 // tm
    ta = m if m_valid is not None else tm

    def body(*refs):
        a_ref, b_ref = refs[0], refs[1]
        o_ref = refs[-2] if col_sum else refs[-1]
        bf = b_ref[...]
        t = _dot(a_ref[...].astype(BF16), bf.astype(BF16), TN)
        o_ref[...] = t[:tm].astype(BF16)
        if col_sum:
            refs[-1][...] = jnp.sum(bf.astype(F32), axis=0, keepdims=True)

    in_specs = [pl.BlockSpec((s, ta), lambda i, j: (0, i)), pl.BlockSpec((s, tn), lambda i, j: (0, j))]
    args = [a, b]
    aliases = None
    if into is not None:
        in_specs.append(ANY)
        args.append(into)
        aliases = {2: 0}
    out_shape = [_sds((rows, n), BF16)]
    out_specs = [pl.BlockSpec((tm, tn), lambda i, j: (base + i, j))]
    if col_sum:
        out_shape.append(_sds((1, n), F32))
        out_specs.append(pl.BlockSpec((1, tn), lambda i, j: (0, j)))
    res = _pcall(body, out_shape=out_shape, grid=(mv // tm, n // tn), in_specs=in_specs, out_specs=out_specs,
                 args=args, name=name, ride=ride, aliases=aliases)
    outs = res if ride is None else res[0]
    out = (outs[0], outs[1][0]) if col_sum else outs[0]
    return out if ride is None else (out, res[1])


def mm_rms_bwd(terms, dxn, x, nw, name, ride=None):
    s, n = x.shape
    nt_ = len(terms)
    tm = _row_tile(s, 512 if nt_ <= 2 else 256)
    forms = [t[5] for t in terms]

    def body(*refs):
        dxn_ref, x_ref, nw_ref, dx_ref, dnw_ref = refs[2 * nt_:]
        i = pl.program_id(0)
        dh = None
        for t in range(nt_):
            part = _dot(refs[2 * t][...].astype(BF16), refs[2 * t + 1][...], NN if forms[t] == "nn" else NT)
            dh = part if dh is None else dh + part
        dx, dnw = _rms_bwd(dh, x_ref[...], nw_ref[...])
        dx_ref[...] = dxn_ref[...] + dx
        col = jnp.sum(dnw, axis=0, keepdims=True)

        @pl.when(i == 0)
        def _():
            dnw_ref[...] = col

        @pl.when(i > 0)
        def _():
            dnw_ref[...] += col

    in_specs, args = [], []
    for a, cb, w, rb, kb, form in terms:
        in_specs.append(pl.BlockSpec((tm, kb), lambda i, cb=cb: (i, cb)))
        if form == "nn":
            in_specs.append(pl.BlockSpec((kb, n), lambda i, rb=rb: (rb, 0)))
        else:
            in_specs.append(pl.BlockSpec((n, kb), lambda i, rb=rb: (0, rb)))
        args += [a, w]
    row = pl.BlockSpec((tm, n), lambda i: (i, 0))
    vec = pl.BlockSpec((1, n), lambda i: (0, 0))
    res = _pcall(body, out_shape=[_sds((s, n), F32), _sds((1, n), F32)], grid=(s // tm,),
                 in_specs=in_specs + [row, row, vec], out_specs=[row, vec],
                 args=args + [dxn, x, nw.reshape(1, n)], name=name, ride=ride)
    outs = res if ride is None else res[0]
    out = (outs[0], outs[1][0])
    return out if ride is None else (out, res[1])


def rope_tables(s):
    pos = jnp.arange(s, dtype=F32)
    inv = 1.0 / (ROPE_THETA ** (jnp.arange(0, ATT_HEAD_DIM, 2, dtype=F32) / ATT_HEAD_DIM))
    ang = pos[:, None] * inv[None, :]
    cos = jnp.tile(jnp.cos(ang), (1, 2 * LANES // ATT_HEAD_DIM))
    sin = jnp.tile(jnp.sin(ang), (1, 2 * LANES // ATT_HEAD_DIM))
    return cos, sin


def rope_apply(t, cos, sin, name, inverse=False, scale=1.0, out_dtype=BF16):
    s, n = t.shape
    tm = _row_tile(s, 512)
    half = ATT_HEAD_DIM // 2
    reps = n // LANES

    def body(t_ref, c_ref, s_ref, o_ref):
        tf = t_ref[...].astype(F32)
        c = jnp.tile(c_ref[...], (1, reps))
        sn = jnp.tile(s_ref[...], (1, reps))
        lane = lax.broadcasted_iota(jnp.int32, tf.shape, 1)
        first = (lane & (ATT_HEAD_DIM - 1)) < half
        rot = jnp.where(first, -pltpu.roll(tf, n - half, 1), pltpu.roll(tf, half, 1))
        sign = -1.0 if inverse else 1.0
        o_ref[...] = (scale * (tf * c + sign * rot * sn)).astype(out_dtype)

    tab = pl.BlockSpec((tm, LANES), lambda i: (i, 0))
    return _pcall(body, out_shape=[_sds((s, n), out_dtype)], grid=(s // tm,),
                  in_specs=[pl.BlockSpec((tm, n), lambda i: (i, 0)), tab, tab],
                  out_specs=[pl.BlockSpec((tm, n), lambda i: (i, 0))], args=[t, cos, sin], name=name)[0]


CONV_TILE = 256


def _shift_down(u, k):
    if k == 0:
        return u
    row = lax.broadcasted_iota(jnp.int32, u.shape, 0)
    return jnp.where(row >= k, pltpu.roll(u, k, 0), 0.0)


def _shift_up(u, k):
    if k == 0:
        return u
    s = u.shape[0]
    row = lax.broadcasted_iota(jnp.int32, u.shape, 0)
    return jnp.where(row < s - k, pltpu.roll(u, s - k, 0), 0.0)


def _conv_taps(u):
    return [_shift_down(u, CONV_WIDTH - 1 - k) for k in range(CONV_WIDTH)]


def _conv_pre(taps, w_ref, b_ref):
    pre = b_ref[...] + w_ref[0:1, :] * taps[0]
    for k in range(1, CONV_WIDTH):
        pre += w_ref[k:k + 1, :] * taps[k]
    return pre


def conv_fwd(u, w, b, name, ride=None):
    s, c = u.shape

    def body(u_ref, w_ref, b_ref, o_ref):
        pre = _conv_pre(_conv_taps(u_ref[...]), w_ref, b_ref)
        o_ref[...] = pre * _sigmoid(pre)

    col = pl.BlockSpec((s, CONV_TILE), lambda j: (0, j))
    res = _pcall(body, out_shape=[_sds((s, c), F32)], grid=(c // CONV_TILE,),
                 in_specs=[col, pl.BlockSpec((CONV_WIDTH, CONV_TILE), lambda j: (0, j)),
                           pl.BlockSpec((1, CONV_TILE), lambda j: (0, j))],
                 out_specs=[col], args=[u, w, b.reshape(1, c)], name=name, ride=ride)
    return res[0] if ride is None else (res[0][0], res[1])


def conv_bwd(dxs, db_, dc_, u, w, b, name, ride=None):
    s, c = u.shape
    n_x = dxs.shape[1] // CONV_TILE
    n_b = db_.shape[1] // CONV_TILE

    def body(dx_ref, dbb_ref, dcc_ref, u_ref, w_ref, b_ref, du_ref, dw_ref, dbias_ref):
        j = pl.program_id(0)
        dact = jnp.where(j < n_x, dx_ref[...], jnp.where(j < n_x + n_b, dbb_ref[...], dcc_ref[...]))
        taps = _conv_taps(u_ref[...])
        pre = _conv_pre(taps, w_ref, b_ref)
        sg = _sigmoid(pre)
        dpre = dact * (sg * (1.0 + pre * (1.0 - sg)))
        du = w_ref[CONV_WIDTH - 1:CONV_WIDTH, :] * dpre
        for k in range(CONV_WIDTH - 1):
            du += w_ref[k:k + 1, :] * _shift_up(dpre, CONV_WIDTH - 1 - k)
        du_ref[...] = du
        dbias_ref[...] = jnp.sum(dpre, axis=0, keepdims=True)
        for k in range(CONV_WIDTH):
            dw_ref[k:k + 1, :] = jnp.sum(dpre * taps[k], axis=0, keepdims=True)

    col = pl.BlockSpec((s, CONV_TILE), lambda j: (0, j))
    wsp = pl.BlockSpec((CONV_WIDTH, CONV_TILE), lambda j: (0, j))
    bsp = pl.BlockSpec((1, CONV_TILE), lambda j: (0, j))
    res = _pcall(
        body, out_shape=[_sds((s, c), F32), _sds((CONV_WIDTH, c), F32), _sds((1, c), F32)], grid=(c // CONV_TILE,),
        in_specs=[pl.BlockSpec((s, CONV_TILE), lambda j: (0, jnp.minimum(j, n_x - 1))),
                  pl.BlockSpec((s, CONV_TILE), lambda j: (0, jnp.clip(j - n_x, 0, n_b - 1))),
                  pl.BlockSpec((s, CONV_TILE), lambda j: (0, jnp.clip(j - n_x - n_b, 0, n_b - 1))),
                  col, wsp, bsp],
        out_specs=[col, wsp, bsp], args=[dxs, db_, dc_, u, w, b.reshape(1, c)], name=name, ride=ride)
    (du, dw, db), rode = res if ride is not None else (res, None)
    return (du, dw, db[0]) if ride is None else ((du, dw, db[0]), rode)


def _lane_pick(mat, idx):
    lane = lax.broadcasted_iota(jnp.int32, mat.shape, 1)
    return jnp.sum(jnp.where(lane == idx, mat, 0.0), axis=1, keepdims=True)


def _sub_pick(mat, idx):
    sub = lax.broadcasted_iota(jnp.int32, mat.shape, 0)
    return jnp.sum(jnp.where(sub == idx, mat, 0.0), axis=0, keepdims=True)


def _expand_heads(cols):
    rows = cols[0].shape[0]
    left = lax.broadcasted_iota(jnp.int32, (rows, LANES), 1) < SSM_HEAD_DIM
    return jnp.concatenate(
        [jnp.where(left, cols[2 * p], cols[2 * p + 1]) for p in range(HEADS_PER_GROUP // 2)], axis=1)


def _dot_01(x, ones, ones_first, pieces):
    tot, rest = None, x
    for _ in range(pieces):
        piece = rest.astype(BF16)
        rest = rest - piece.astype(F32)
        part = _dot(ones, piece) if ones_first else _dot(piece, ones)
        tot = part if tot is None else tot + part
    return tot


def _heads_to_lanes(mat, g):
    jj = lax.broadcasted_iota(jnp.int32, (GROUP_DIM, LANES), 0)
    ll = lax.broadcasted_iota(jnp.int32, (GROUP_DIM, LANES), 1)
    sel = (ll == HEADS_PER_GROUP * g + (jj >> 6)).astype(BF16)
    return _dot_01(mat, sel, False, 3)


def _softplus(x):
    return jnp.maximum(x, 0.0) + jnp.log1p(jnp.exp(-jnp.abs(x)))


def _ssd_scalars(dt_ref, bias_ref, a_ref, dtall, csall, cst):
    dta = _softplus(dt_ref[...] + bias_ref[...])
    row = lax.broadcasted_iota(jnp.int32, (CHUNK, CHUNK), 0)
    col = lax.broadcasted_iota(jnp.int32, (CHUNK, CHUNK), 1)
    cs = _dot_01(dta * a_ref[...], (row >= col).astype(BF16), True, 3)
    dtall[...] = dta
    csall[...] = cs
    cst[...] = cs.T


def _decay_mat(cs_col, cs_row):
    row = lax.broadcasted_iota(jnp.int32, (CHUNK, CHUNK), 0)
    col = lax.broadcasted_iota(jnp.int32, (CHUNK, CHUNK), 1)
    return jnp.exp(jnp.where(row >= col, cs_col - cs_row, NEG))


def _head_mask(xpair, right):
    lane = lax.broadcasted_iota(jnp.int32, xpair.shape, 1)
    keep = (lane >= SSM_HEAD_DIM) if right else (lane < SSM_HEAD_DIM)
    return jnp.where(keep, xpair, 0.0)


def _chunk_cols(x_all, g):
    return [_lane_pick(x_all, HEADS_PER_GROUP * g + r) for r in range(HEADS_PER_GROUP)]


def _decay_col(cs_cols):
    return jnp.concatenate(
        [jnp.broadcast_to(jnp.exp(cc[CHUNK - 1:CHUNK, :]), (SSM_HEAD_DIM, 1)) for cc in cs_cols], axis=0)


def ssd_fwd(act, z, dtp, bias_p, a_p, d_p, normw, name, ride=None):
    s = act.shape[0]
    nc = s // CHUNK

    def body(xs_all, b_all, c_all, z_all, dt_ref, bias_ref, a_ref, d_ref, nw_all,
             yn_all, y_all, st_all, state, dtall, csall, cst):
        _ssd_scalars(dt_ref, bias_ref, a_ref, dtall, csall, cst)

        @pl.when(pl.program_id(0) == 0)
        def _():
            state[...] = jnp.zeros(state.shape, F32)

        for g in range(SSM_GROUPS):
            wide = pl.ds(g * GROUP_DIM, GROUP_DIM)
            narrow = pl.ds(g * SSM_STATE, SSM_STATE)
            group(g, xs_all.at[:, wide], b_all.at[:, narrow], c_all.at[:, narrow], z_all.at[:, wide], d_ref,
                  nw_all.at[:, wide], yn_all.at[:, wide], y_all.at[:, wide], st_all.at[:, pl.ds(g, 1)],
                  state, dtall, csall, cst)

    def group(g, xs_ref, b_ref, c_ref, z_ref, d_ref, nw_ref, yn_ref, y_ref, st_ref, state, dtall, csall, cst):
        cs_cols = _chunk_cols(csall[...], g)
        dt_cols = _chunk_cols(dtall[...], g)
        cs_rows = [_sub_pick(cst[...], HEADS_PER_GROUP * g + r) for r in range(HEADS_PER_GROUP)]
        d_cols = _chunk_cols(d_ref[...], g)
        cs_exp = _expand_heads(cs_cols)
        dt_exp = _expand_heads(dt_cols)
        d_exp = _expand_heads(d_cols)
        xs = xs_ref[...]
        bb = b_ref[...].astype(BF16)
        cb16 = c_ref[...].astype(BF16)
        xdt = xs * dt_exp
        s_prev = state[g]
        st_ref[0, 0] = s_prev
        y_off = _dot(cb16, s_prev.astype(BF16), NT) * jnp.exp(cs_exp)
        decay_st = jnp.exp(cs_exp[CHUNK - 1:CHUNK, :] - cs_exp)
        contrib = _dot((xdt * decay_st).astype(BF16), bb, TN)
        state[g] = _decay_col(cs_cols) * s_prev + contrib
        cbm = _dot(cb16, bb, NT)
        pairs = []
        for p in range(HEADS_PER_GROUP // 2):
            xpair = xdt[:, LANES * p:LANES * (p + 1)]
            m0 = (cbm * _decay_mat(cs_cols[2 * p], cs_rows[2 * p])).astype(BF16)
            m1 = (cbm * _decay_mat(cs_cols[2 * p + 1], cs_rows[2 * p + 1])).astype(BF16)
            pairs.append(_dot(m0, _head_mask(xpair, False).astype(BF16))
                         + _dot(m1, _head_mask(xpair, True).astype(BF16)))
        y = jnp.concatenate(pairs, axis=1) + y_off + xs * d_exp
        y_ref[...] = y
        zf = z_ref[...]
        yg = y * (zf * _sigmoid(zf))
        yn_ref[...] = _rms_fwd(yg, nw_ref[...]).astype(BF16)

    gn = SSM_GROUPS * SSM_STATE
    wide = pl.BlockSpec((CHUNK, D_INNER), lambda c: (c, 0))
    par = pl.BlockSpec((1, LANES), lambda c: (0, 0))
    return _pcall(
        body,
        out_shape=[_sds((s, D_INNER), BF16), _sds((s, D_INNER), F32),
                   _sds((nc, SSM_GROUPS, GROUP_DIM, SSM_STATE), F32)],
        grid=(nc,),
        in_specs=[wide,
                  pl.BlockSpec((CHUNK, gn), lambda c: (c, D_INNER // gn)),
                  pl.BlockSpec((CHUNK, gn), lambda c: (c, D_INNER // gn + 1)),
                  wide,
                  pl.BlockSpec((CHUNK, LANES), lambda c: (c, 0)),
                  par, par, par,
                  pl.BlockSpec((1, D_INNER), lambda c: (0, 0))],
        out_specs=[wide, wide, pl.BlockSpec((1, SSM_GROUPS, GROUP_DIM, SSM_STATE), lambda c: (c, 0, 0, 0))],
        scratch_shapes=[pltpu.VMEM((SSM_GROUPS, GROUP_DIM, SSM_STATE), F32),
                        pltpu.VMEM((CHUNK, LANES), F32), pltpu.VMEM((CHUNK, LANES), F32),
                        pltpu.VMEM((LANES, CHUNK), F32)],
        args=[act, act, act, z, dtp, bias_p, a_p, d_p, normw], name=name, ride=ride)


def ssd_bwd(dyn, act, z, y_pre, states, dtp, bias_p, a_p, d_p, normw, name, ride=None):
    s = act.shape[0]
    nc = s // CHUNK

    def body(dyn_all, xs_all, b_all, c_all, z_all, y_all, st_all, dt_ref, bias_ref, a_ref, d_ref, nw_all,
             dxs_all, db_all, dc_all, dz_all, ddt_ref, dnw_ref, dbias_ref, da_ref, dd_ref,
             dstate, dtall, csall, cst):
        _ssd_scalars(dt_ref, bias_ref, a_ref, dtall, csall, cst)
        ddt_ref[...] = jnp.zeros((CHUNK, LANES), F32)

        @pl.when(pl.program_id(0) == 0)
        def _():
            dstate[...] = jnp.zeros(dstate.shape, F32)
            dnw_ref[...] = jnp.zeros(dnw_ref.shape, F32)
            dbias_ref[...] = jnp.zeros((1, LANES), F32)
            da_ref[...] = jnp.zeros((1, LANES), F32)
            dd_ref[...] = jnp.zeros((1, LANES), F32)

        for g in range(SSM_GROUPS):
            wide = pl.ds(g * GROUP_DIM, GROUP_DIM)
            narrow = pl.ds(g * SSM_STATE, SSM_STATE)
            group(g, dyn_all.at[:, wide], xs_all.at[:, wide], b_all.at[:, narrow], c_all.at[:, narrow],
                  z_all.at[:, wide], y_all.at[:, wide], st_all.at[:, pl.ds(g, 1)], dt_ref, bias_ref, a_ref, d_ref,
                  nw_all.at[:, wide], dxs_all.at[:, wide], db_all.at[:, narrow], dc_all.at[:, narrow],
                  dz_all.at[:, wide], ddt_ref, dnw_ref, dbias_ref, da_ref, dd_ref, dstate, dtall, csall, cst)

    def group(g, dyn_ref, xs_ref, b_ref, c_ref, z_ref, y_ref, st_ref, dt_ref, bias_ref, a_ref, d_ref, nw_ref,
              dxs_ref, db_ref, dc_ref, dz_ref, ddt_ref, dnw_ref, dbias_ref, da_ref, dd_ref,
              dstate, dtall, csall, cst):
        cs_cols = _chunk_cols(csall[...], g)
        dt_cols = _chunk_cols(dtall[...], g)
        cs_rows = [_sub_pick(cst[...], HEADS_PER_GROUP * g + r) for r in range(HEADS_PER_GROUP)]
        d_cols = _chunk_cols(d_ref[...], g)
        cs_exp = _expand_heads(cs_cols)
        dt_exp = _expand_heads(dt_cols)
        d_exp = _expand_heads(d_cols)
        xs = xs_ref[...]
        bb = b_ref[...].astype(BF16)
        cb16 = c_ref[...].astype(BF16)
        xdt = xs * dt_exp
        s_prev = st_ref[0, 0]
        s_prev16 = s_prev.astype(BF16)
        ds_next = dstate[g]
        ds16 = ds_next.astype(BF16)

        zf = z_ref[...]
        sz = _sigmoid(zf)
        silu_z = zf * sz
        y = y_ref[...]
        yg = y * silu_z
        dout = dyn_ref[...]
        dyg, dnw = _rms_bwd(dout, yg, nw_ref[...])
        dnw_ref[pl.ds(g, 1), :] += jnp.sum(dnw, axis=0, keepdims=True)
        dy = dyg * silu_z
        dz_ref[...] = dyg * y * (sz * (1.0 + zf * (1.0 - sz)))
        dd_ref[...] += jnp.sum(_heads_to_lanes(dy * xs, g), axis=0, keepdims=True)

        exp_cs = jnp.exp(cs_exp)
        decay_st = jnp.exp(cs_exp[CHUNK - 1:CHUNK, :] - cs_exp)
        cs_t = _dot(cb16, s_prev16, NT)
        dyo = dy * exp_cs
        dc_acc = _dot(dyo.astype(BF16), s_prev16, NN)
        g1 = _dot(bb, ds16, NT)
        xds = xdt * decay_st
        db_acc = _dot(xds.astype(BF16), ds16, NN)
        dxdt_off = g1 * decay_st
        t_exp = g1 * xds
        dcs_exp = dy * cs_t * exp_cs - t_exp
        decay_c = _decay_col(cs_cols)
        dstate[g] = decay_c * ds_next + _dot(dyo.astype(BF16), cb16, TN)
        dlast_col = jnp.sum(ds_next * s_prev, axis=1, keepdims=True) * decay_c
        jj = lax.broadcasted_iota(jnp.int32, (GROUP_DIM, LANES), 0)
        ll = lax.broadcasted_iota(jnp.int32, (GROUP_DIM, LANES), 1)
        sel = ll == HEADS_PER_GROUP * g + (jj >> 6)
        dlast = jnp.sum(jnp.where(sel, dlast_col, 0.0), axis=0, keepdims=True)
        t_all = _heads_to_lanes(t_exp, g)
        dlast += jnp.sum(t_all, axis=0, keepdims=True)
        dcs_all = _heads_to_lanes(dcs_exp, g)

        cbm = _dot(cb16, bb, NT)
        dcb = jnp.zeros((CHUNK, CHUNK), F32)
        dcs_rows = jnp.zeros((LANES, CHUNK), F32)
        lane_l = lax.broadcasted_iota(jnp.int32, (CHUNK, LANES), 1)
        sub_l = lax.broadcasted_iota(jnp.int32, (LANES, CHUNK), 0)
        dxdt_pairs = []
        for p in range(HEADS_PER_GROUP // 2):
            xpair16 = xdt[:, LANES * p:LANES * (p + 1)].astype(BF16)
            dypair = dy[:, LANES * p:LANES * (p + 1)]
            acc = None
            for r in (2 * p, 2 * p + 1):
                lm = _decay_mat(cs_cols[r], cs_rows[r])
                m = cbm * lm
                dyh = _head_mask(dypair, r % 2 == 1).astype(BF16)
                dm = _dot(dyh, xpair16, NT)
                dcb += dm * lm
                q = dm * m
                idx = HEADS_PER_GROUP * g + r
                dcs_all += jnp.where(lane_l == idx, jnp.sum(q, axis=1, keepdims=True), 0.0)
                dcs_rows -= jnp.where(sub_l == idx, jnp.sum(q, axis=0, keepdims=True), 0.0)
                part = _dot(m.astype(BF16), dyh, TN)
                acc = part if acc is None else acc + part
            dxdt_pairs.append(acc)
        dxdt = jnp.concatenate(dxdt_pairs, axis=1) + dxdt_off
        dcb16 = dcb.astype(BF16)
        dc_ref[...] = dc_acc + _dot(dcb16, bb, NN)
        db_ref[...] = db_acc + _dot(dcb16, cb16, TN)
        dxs_ref[...] = dxdt * dt_exp + dy * d_exp

        dcs_all += dcs_rows.T
        row = lax.broadcasted_iota(jnp.int32, (CHUNK, CHUNK), 0)
        col = lax.broadcasted_iota(jnp.int32, (CHUNK, CHUNK), 1)
        last_row = lax.broadcasted_iota(jnp.int32, (CHUNK, LANES), 0) == CHUNK - 1
        dcs_all += jnp.where(last_row, dlast, 0.0)
        da_all = _dot_01(dcs_all, (col >= row).astype(BF16), True, 3)
        dta = dtall[...]
        in_group = jnp.logical_and(lane_l >= HEADS_PER_GROUP * g, lane_l < HEADS_PER_GROUP * (g + 1))
        ddt = jnp.where(in_group, da_all * a_ref[...] + _heads_to_lanes(dxdt * xs, g), 0.0)
        da_ref[...] += jnp.sum(jnp.where(in_group, da_all * dta, 0.0), axis=0, keepdims=True)
        ddt_raw = ddt * _sigmoid(dt_ref[...] + bias_ref[...])
        ddt_ref[...] += ddt_raw
        dbias_ref[...] += jnp.sum(ddt_raw, axis=0, keepdims=True)

    gn = SSM_GROUPS * SSM_STATE
    wide = pl.BlockSpec((CHUNK, D_INNER), lambda c: (nc - 1 - c, 0))
    st = pl.BlockSpec((CHUNK, gn), lambda c: (nc - 1 - c, 0))
    par = pl.BlockSpec((1, LANES), lambda c: (0, 0))
    dtb = pl.BlockSpec((CHUNK, LANES), lambda c: (nc - 1 - c, 0))
    f = lambda shape: _sds(shape, F32)
    return _pcall(
        body,
        out_shape=[f((s, D_INNER)), f((s, gn)), f((s, gn)),
                   f((s, D_INNER)), f((s, LANES)), f((8, GROUP_DIM)), f((1, LANES)), f((1, LANES)), f((1, LANES))],
        grid=(nc,),
        in_specs=[wide, wide,
                  pl.BlockSpec((CHUNK, gn), lambda c: (nc - 1 - c, D_INNER // gn)),
                  pl.BlockSpec((CHUNK, gn), lambda c: (nc - 1 - c, D_INNER // gn + 1)),
                  wide, wide,
                  pl.BlockSpec((1, SSM_GROUPS, GROUP_DIM, SSM_STATE), lambda c: (nc - 1 - c, 0, 0, 0)),
                  dtb, par, par, par,
                  pl.BlockSpec((1, D_INNER), lambda c: (0, 0))],
        out_specs=[wide, st, st, wide, dtb, pl.BlockSpec((8, GROUP_DIM), lambda c: (0, 0)), par, par, par],
        scratch_shapes=[pltpu.VMEM((SSM_GROUPS, GROUP_DIM, SSM_STATE), F32),
                        pltpu.VMEM((CHUNK, LANES), F32), pltpu.VMEM((CHUNK, LANES), F32),
                        pltpu.VMEM((LANES, CHUNK), F32)],
        args=[dyn, act, act, act, z, y_pre, states, dtp, bias_p, a_p, d_p, normw], name=name, ride=ride)


def _attn_probs(q, kp, kc, sink, n):
    sp = _dot(q, kp, NT)
    sc = _dot(q, kc, NT)
    i = lax.broadcasted_iota(jnp.int32, sp.shape, 0) & (WINDOW - 1)
    j = lax.broadcasted_iota(jnp.int32, sp.shape, 1)
    sp = jnp.where(jnp.logical_and(j > i, n > 0), sp, NEG)
    sc = jnp.where(j <= i, sc, NEG)
    m = jnp.maximum(jnp.maximum(jnp.max(sp, axis=1, keepdims=True), jnp.max(sc, axis=1, keepdims=True)), sink)
    pp = jnp.exp(sp - m)
    pc = jnp.exp(sc - m)
    ps = jnp.exp(sink - m)
    inv = 1.0 / (jnp.sum(pp, axis=1, keepdims=True) + jnp.sum(pc, axis=1, keepdims=True) + ps)
    return pp * inv, pc * inv, ps * inv


def attn_fwd(qt, kt, vt, sink_rows, name, ride=None):
    s = qt.shape[1]
    nb = s // WINDOW
    rows = Q_PER_KV * WINDOW

    def body(q_ref, kp_ref, kc_ref, vp_ref, vc_ref, sk_ref, o_ref):
        n = pl.program_id(0)
        for h in range(N_KV_HEADS):
            heads = pl.ds(h * Q_PER_KV, Q_PER_KV)
            q = q_ref[heads].reshape(rows, ATT_HEAD_DIM)
            pp, pc, _ = _attn_probs(q, kp_ref[h], kc_ref[h], sk_ref[h], n)
            o = _dot(pp.astype(BF16), vp_ref[h]) + _dot(pc.astype(BF16), vc_ref[h])
            o_ref[heads] = o.reshape(Q_PER_KV, WINDOW, ATT_HEAD_DIM).astype(BF16)

    qsp = pl.BlockSpec((N_Q_HEADS, WINDOW, ATT_HEAD_DIM), lambda n: (0, n, 0))
    prev = pl.BlockSpec((N_KV_HEADS, WINDOW, ATT_HEAD_DIM), lambda n: (0, jnp.maximum(n - 1, 0), 0))
    cur = pl.BlockSpec((N_KV_HEADS, WINDOW, ATT_HEAD_DIM), lambda n: (0, n, 0))
    return _pcall(body, out_shape=[_sds(qt.shape, BF16)], grid=(nb,),
                  in_specs=[qsp, prev, cur, prev, cur, pl.BlockSpec((N_KV_HEADS, rows, 1), lambda n: (0, 0, 0))],
                  out_specs=[qsp], args=[qt, kt, kt, vt, vt, sink_rows], name=name, ride=ride)


def attn_bwd(qt, kt, vt, sink_rows, dot_, name, ride=None):
    s = qt.shape[1]
    nb = s // WINDOW
    rows = Q_PER_KV * WINDOW

    def body(q_ref, kp_ref, kc_ref, vp_ref, vc_ref, sk_ref, do_ref, dq_ref, dk_ref, dv_ref, ds_ref, kacc, vacc):
        n = pl.program_id(0)

        @pl.when(n == 0)
        def _():
            kacc[...] = jnp.zeros(kacc.shape, F32)
            vacc[...] = jnp.zeros(vacc.shape, F32)

        @pl.when(n < nb)
        def _():
            for h in range(N_KV_HEADS):
                heads = pl.ds(h * Q_PER_KV, Q_PER_KV)
                q = q_ref[heads].reshape(rows, ATT_HEAD_DIM)
                do = do_ref[heads].reshape(rows, ATT_HEAD_DIM)
                kp, kc, vp, vc = kp_ref[h], kc_ref[h], vp_ref[h], vc_ref[h]
                pp, pc, ps = _attn_probs(q, kp, kc, sk_ref[h], n)
                dpp = _dot(do, vp, NT)
                dpc = _dot(do, vc, NT)
                delta = jnp.sum(pp * dpp, axis=1, keepdims=True) + jnp.sum(pc * dpc, axis=1, keepdims=True)
                dsp = (pp * (dpp - delta)).astype(BF16)
                dsc = (pc * (dpc - delta)).astype(BF16)
                dq = _dot(dsp, kp) + _dot(dsc, kc)
                dq_ref[heads] = dq.reshape(Q_PER_KV, WINDOW, ATT_HEAD_DIM)
                dk_ref[h] = kacc[h] + _dot(dsp, q, TN)
                dv_ref[h] = vacc[h] + _dot(pp.astype(BF16), do, TN)
                kacc[h] = _dot(dsc, q, TN)
                vacc[h] = _dot(pc.astype(BF16), do, TN)
                dsk = -ps * delta
                sub = lax.broadcasted_iota(jnp.int32, (8, LANES), 0)
                tile = jnp.zeros((8, LANES), F32)
                for j in range(Q_PER_KV):
                    tile += jnp.where(sub == j, jnp.sum(dsk[j * WINDOW:(j + 1) * WINDOW, :], axis=0, keepdims=True),
                                      0.0)
                ds_ref[h, 0] = tile

        @pl.when(n == nb)
        def _():
            dk_ref[...] = kacc[...]
            dv_ref[...] = vacc[...]
            ds_ref[...] = jnp.zeros(ds_ref.shape, F32)

    last = nb - 1
    qsp = pl.BlockSpec((N_Q_HEADS, WINDOW, ATT_HEAD_DIM), lambda n: (0, jnp.minimum(n, last), 0))
    prev = pl.BlockSpec((N_KV_HEADS, WINDOW, ATT_HEAD_DIM), lambda n: (0, jnp.clip(n - 1, 0, last), 0))
    cur = pl.BlockSpec((N_KV_HEADS, WINDOW, ATT_HEAD_DIM), lambda n: (0, jnp.minimum(n, last), 0))
    dkv = pl.BlockSpec((N_KV_HEADS, WINDOW, ATT_HEAD_DIM), lambda n: (0, jnp.maximum(n - 1, 0), 0))
    f = lambda shape: _sds(shape, F32)
    acc = pltpu.VMEM((N_KV_HEADS, WINDOW, ATT_HEAD_DIM), F32)
    return _pcall(
        body, out_shape=[f(qt.shape), f(kt.shape), f(vt.shape), f((N_KV_HEADS, nb + 1, 8, LANES))],
        grid=(nb + 1,),
        in_specs=[qsp, prev, cur, prev, cur, pl.BlockSpec((N_KV_HEADS, rows, 1), lambda n: (0, 0, 0)), qsp],
        out_specs=[qsp, dkv, dkv, pl.BlockSpec((N_KV_HEADS, 1, 8, LANES), lambda n: (0, n, 0, 0))],
        scratch_shapes=[acc, acc], args=[qt, kt, kt, vt, vt, sink_rows, dot_], name=name, ride=ride)


def loss_head(x, w, tgt, name):
    s, d = x.shape
    tm = _row_tile(s, 256)

    def body(x_ref, w_ref, t_ref, loss_ref, dx_ref, dw_ref):
        i = pl.program_id(0)
        xf = x_ref[...]
        wv = w_ref[...]
        r = lax.rsqrt(jnp.mean(xf * xf, axis=-1, keepdims=True) + EPS)
        xhat = xf * r
        e = xhat * wv - t_ref[...]
        part = 0.5 * jnp.sum(jnp.mean(e * e, axis=-1, keepdims=True), axis=0, keepdims=True)
        dy = e * (1.0 / d)
        dxhat = dy * wv
        dx_ref[...] = r * (dxhat - xhat * jnp.mean(dxhat * xhat, axis=-1, keepdims=True))
        col = jnp.sum(dy * xhat, axis=0, keepdims=True)

        @pl.when(i == 0)
        def _():
            loss_ref[...] = jnp.broadcast_to(part, (1, LANES))
            dw_ref[...] = col

        @pl.when(i > 0)
        def _():
            loss_ref[...] += jnp.broadcast_to(part, (1, LANES))
            dw_ref[...] += col

    row = pl.BlockSpec((tm, d), lambda i: (i, 0))
    vec = pl.BlockSpec((1, d), lambda i: (0, 0))
    return _pcall(body, out_shape=[_sds((1, LANES), F32), _sds((s, d), F32), _sds((1, d), F32)], grid=(s // tm,),
                  in_specs=[row, vec, row], out_specs=[pl.BlockSpec((1, LANES), lambda i: (0, 0)), row, vec],
                  args=[x, w.reshape(1, d), tgt], name=name)


ELEMWISE_TILE = 720 * 1024


def _tile_rows(r, c, max_elems=262144, mult=16):
    best = None
    for t in range(mult, r + 1, mult):
        if r % t == 0 and t * c <= max_elems:
            best = t
    return best or r


def add_pair(xhs, ps, c_idx, name):
    n = len(xhs)
    _, r, c = xhs[0].shape
    tr = _tile_rows(r, c, max_elems=ELEMWISE_TILE)

    def body(c_ref, *refs):
        for x_ref, p_ref, o_ref in zip(refs[:n], refs[n:2 * n], refs[2 * n:]):
            o_ref[...] = (x_ref[0].astype(F32) + p_ref[...].astype(F32)).astype(BF16)

    blk = pl.BlockSpec((tr, c), lambda i, cr: (i, 0))
    return pl.pallas_call(
        body, out_shape=tuple([_sds((r, c), BF16)] * n),
        grid_spec=pltpu.PrefetchScalarGridSpec(
            num_scalar_prefetch=1, grid=(r // tr,),
            in_specs=[pl.BlockSpec((1, tr, c), lambda i, cr: (cr[0], i, 0))] * n + [blk] * n,
            out_specs=tuple([blk] * n)),
        name=name, compiler_params=_cp(1))(c_idx, *xhs, *ps)


def sum_chips(qs, owns, chip_idx, name):
    n = len(qs)
    _, r, c = qs[0].shape
    tr = _tile_rows(r, c, max_elems=ELEMWISE_TILE // max(1, n - 1))

    def body(k_ref, *refs):
        k = k_ref[0]
        for q_ref, own_ref, o_ref in zip(refs[:n], refs[n:2 * n], refs[2 * n:]):
            mine = own_ref[0].astype(F32)
            tot = None
            for j in range(N_CHIPS):
                term = jnp.where(k == j, mine, q_ref[j].astype(F32))
                tot = term if tot is None else tot + term
            o_ref[...] = tot

    return pl.pallas_call(
        body, out_shape=tuple([_sds((r, c), F32)] * n),
        grid_spec=pltpu.PrefetchScalarGridSpec(
            num_scalar_prefetch=1, grid=(r // tr,),
            in_specs=([pl.BlockSpec((N_CHIPS, tr, c), lambda i, kr: (0, i, 0))] * n
                      + [pl.BlockSpec((1, tr, c), lambda i, kr: (kr[0], i, 0))] * n),
            out_specs=tuple([pl.BlockSpec((tr, c), lambda i, kr: (i, 0))] * n)),
        name=name, compiler_params=_cp(1))(chip_idx, *qs, *owns)


def adamw(w, g, m, v, name):
    r, c = w.shape
    tr = _tile_rows(r, c, mult=8)
    c1 = 1.0 / (1.0 - ADAM_B1 ** ADAM_STEP)
    c2 = 1.0 / (1.0 - ADAM_B2 ** ADAM_STEP)

    def body(w_ref, g_ref, m_ref, v_ref, d_ref, mo_ref, vo_ref):
        gf = g_ref[...]
        mn = ADAM_B1 * m_ref[...] + (1.0 - ADAM_B1) * gf
        vn = ADAM_B2 * v_ref[...] + (1.0 - ADAM_B2) * (gf * gf)
        mo_ref[...] = mn
        vo_ref[...] = vn
        d_ref[...] = -ADAM_LR * ((mn * c1) / (jnp.sqrt(vn * c2) + ADAM_EPS) + ADAM_WD * w_ref[...])

    blk = pl.BlockSpec((tr, c), lambda i: (i, 0))
    out = _sds((r, c), F32)
    return _pcall(body, out_shape=[out, out, out], grid=(r // tr,), in_specs=[blk] * 4, out_specs=[blk] * 3,
                  args=[w, g, m, v], name=name)


WEIGHTS = ['norm_w', 'ffn_w_gate', 'ffn_w_up', 'ffn_w_down', 'ssm_w_in', 'ssm_conv_w', 'ssm_conv_b', 'ssm_dt_bias',
           'ssm_a_log', 'ssm_d', 'ssm_norm_w', 'ssm_w_out', 'kv_norm_w', 'w_k', 'b_k', 'w_v', 'b_v', 'attn_w_q',
           'attn_b_q', 'attn_sinks', 'attn_w_o', 'attn_b_o', 'final_norm_w']
BIG = ['ffn_w_gate', 'ffn_w_up', 'ffn_w_down', 'ssm_w_in', 'ssm_w_out', 'w_k', 'w_v', 'attn_w_q', 'attn_w_o']
TRANSPOSED = ('ffn_w_gate', 'ffn_w_up', 'ssm_w_in')
SMALL = [n for n in WEIGHTS if n not in BIG]
SMALL_SHARDED = {'norm_w': 2, 'ssm_conv_w': 2, 'ssm_conv_b': 1, 'ssm_norm_w': 1}
ROW_ALIGN = 8 * LANES


def _pack_rows(parts):
    flat = jnp.concatenate([p.reshape(-1).astype(F32) for p in parts])
    pad = (-flat.size) % ROW_ALIGN
    return jnp.pad(flat, (0, pad)).reshape(-1, LANES)


def _unpack_rows(buf, shapes):
    flat = buf.reshape(-1)
    out, pos = [], 0
    for shp in shapes:
        size = math.prod(shp)
        out.append(flat[pos:pos + size].reshape(shp))
        pos += size
    return out


def _as2d(a):
    return a.reshape(-1, a.shape[-1])


def _heads_major(t, n_heads):
    s = t.shape[0]
    return t.reshape(s, n_heads, ATT_HEAD_DIM).transpose(1, 0, 2)


def _tokens_major(t):
    h, s, dh = t.shape
    return t.transpose(1, 0, 2).reshape(s, h * dh)


def _pad_lanes(v):
    return jnp.pad(v.reshape(1, -1), ((0, 0), (0, LANES - v.size)))


def _chips_first(t):
    return t.swapaxes(0, 1).reshape((-1,) + t.shape[3:])


def _parts_first(t, rows):
    return t.reshape((N_CHIPS, N_CORES, rows) + t.shape[1:]).swapaxes(0, 1)


def kernel(*args):
    names = (['x'] + WEIGHTS + ['loss_target'] + ['m_' + n for n in WEIGHTS] + ['v_' + n for n in WEIGHTS])
    a = dict(zip(names, args))
    for n in TRANSPOSED:
        for pre in ('', 'm_', 'v_'):
            a[pre + n] = a[pre + n].swapaxes(-1, -2)
    xi, yi, ci = lax.axis_index("x"), lax.axis_index("y"), lax.axis_index("c")
    chip = 2 * xi + yi
    south = ci == 0
    c_idx = jnp.reshape(ci, (1,)).astype(jnp.int32)
    chip_idx = jnp.reshape(chip, (1,)).astype(jnp.int32)
    x0 = a['x'][0]
    s = x0.shape[0]
    cos, sin = rope_tables(s)

    def own_slot(full, mine):
        return lax.dynamic_update_slice_in_dim(full, mine[:, None], chip, axis=1)

    def ffn_shard(l, i):
        return [a[n][l, i].astype(BF16).reshape(N_CORES, FF_PART, D_MODEL)
                for n in ('ffn_w_gate', 'ffn_w_up', 'ffn_w_down')]

    def own_slots(fulls, mines):
        return [own_slot(f, m) for f, m in zip(fulls, mines)]
    small_names = list(SMALL_SHARDED)
    small_sh = _pack_rows([a[n] for n in small_names])
    small_sh = small_sh.reshape(N_CORES, small_sh.shape[0] // 2, LANES)
    sh00, sh01, sh10, sh11 = ffn_shard(0, 0), ffn_shard(0, 1), ffn_shard(1, 0), ffn_shard(1, 1)
    w_in_sh = jnp.pad(a['ssm_w_in'][0], ((0, IN_SHARD_PAD - IN_SHARD), (0, 0))).astype(BF16).reshape(
        N_CORES, IN_SHARD_PAD // 2, D_MODEL)
    w_out_sh = a['ssm_w_out'][0].astype(BF16).reshape(N_CORES, 256, D_MODEL)
    attn_sh = jnp.stack([a['attn_w_q'][0], a['attn_w_o'][0]]).astype(BF16)
    kv_sh = jnp.stack([a['w_k'], a['w_v']]).astype(BF16)
    in_flight, all_started = split_start(
        [sh00 + [small_sh], [w_in_sh, kv_sh], [w_out_sh], sh01, sh10, [attn_sh], sh11], "gather", "gather_start")

    def arrive(idx, after, tag):
        return forward_cores(split_arrive(in_flight[idx], "gather", after, "gather_arrive_" + tag))

    first = run_exchange(arrive(0, all_started, "first"), "gather_hop_first")
    w00 = own_slots(first[:3], sh00)
    smalls = own_slot(first[3], small_sh)
    p = {}
    per_chip = [_unpack_rows(smalls[:, k], [a[n].shape for n in small_names]) for k in range(N_CHIPS)]
    for idx, n in enumerate(small_names):
        p[n] = jnp.concatenate([per_chip[k][idx] for k in range(N_CHIPS)], axis=SMALL_SHARDED[n])
    nw = p['norm_w']
    conv_w, conv_b, ssm_nw = p['ssm_conv_w'][0], p['ssm_conv_b'][0], p['ssm_norm_w'][0].reshape(1, D_INNER)

    h00 = rmsnorm_fwd(x0, nw[0, 0], "norm_in")
    x1, h01, gu00 = ffn_fwd(h00, x0, *w00, [nw[0, 1]], "ffn_fwd_00")
    w_in_g, kv_g = run_exchange(arrive(1, x1, "in"), "gather_hop_in")
    w_in_t = _chips_first(own_slot(w_in_g, w_in_sh)).reshape(N_CHIPS, IN_SHARD_PAD, D_MODEL)[:, :IN_SHARD].reshape(
        IN_PROJ_DIM, D_MODEL)
    w_dt_t = jnp.pad(w_in_t[D_INNER + CONV_DIM:], ((0, LANES - SSM_HEADS), (0, 0)))
    kv_g = own_slot(kv_g, kv_sh)
    w_k, w_v = kv_g[0].reshape(D_MODEL, KV_DIM), kv_g[1].reshape(D_MODEL, KV_DIM)

    zz = mm_nt(h01, w_in_t, "ssm_in_z", n=D_INNER)
    xbc = mm_nt(h01, w_in_t, "ssm_in_xbc", n=CONV_DIM, row0=D_INNER)
    dtp = mm_nt(h01, w_dt_t, "ssm_in_dt")
    act = conv_fwd(xbc, conv_w, conv_b, "ssm_conv")
    bias_p = _pad_lanes(a['ssm_dt_bias'][0])
    a_p = _pad_lanes(-jnp.exp(a['ssm_a_log'][0]))
    d_p = _pad_lanes(a['ssm_d'][0])
    (yn, y_pre, states), (w_out_g,) = ssd_fwd(act, zz, dtp, bias_p, a_p, d_p, ssm_nw, "ssd_fwd",
                                              ride=arrive(2, act, "out"))
    w_out = _chips_first(own_slot(w_out_g, w_out_sh))
    (x2, h02), w01 = mm_res(yn, w_out, x1, "ssm_out", norm_ws=[nw[0, 2]], ride=arrive(3, yn, "01"))
    w01 = own_slots(w01, sh01)
    x3, hkv, h10, gu01 = ffn_fwd(h02, x2, *w01, [a['kv_norm_w'], nw[1, 0]], "ffn_fwd_01")
    w10 = own_slots(run_exchange(arrive(4, x3, "10"), "gather_hop_10"), sh10)

    k_rot = rope_apply(mm_nn(hkv, w_k, "kv_k", bias=a['b_k']), cos, sin, "rope_k")
    v = mm_nn(hkv, w_v, "kv_v", bias=a['b_v'], out_dtype=BF16)
    kt = _heads_major(k_rot, N_KV_HEADS)
    vt = _heads_major(v, N_KV_HEADS)

    (x4, h11, gu10), (attn_g,) = ffn_fwd(h10, x3, *w10, [nw[1, 1]], "ffn_fwd_10", ride=arrive(5, v, "attn"))
    attn_g = own_slot(attn_g, attn_sh)
    w_q, w_o = attn_g[0].reshape(D_MODEL, D_MODEL), attn_g[1].reshape(D_MODEL, D_MODEL)
    scale = 1.0 / math.sqrt(ATT_HEAD_DIM)
    q_rot = rope_apply(mm_nn(h11, w_q, "attn_q", bias=a['attn_b_q'][0]), cos, sin, "rope_q", scale=scale)
    qt = _heads_major(q_rot, N_Q_HEADS)
    sink_rows = jnp.repeat(a['attn_sinks'][0].reshape(N_KV_HEADS, Q_PER_KV), WINDOW, axis=1).reshape(
        N_KV_HEADS, Q_PER_KV * WINDOW, 1)
    (ot,) = attn_fwd(qt, kt, vt, sink_rows, "attn_fwd")
    o = _tokens_major(ot)
    (x5, h12), w11 = mm_res(o, w_o, x4, "attn_out", bias=a['attn_b_o'][0], norm_ws=[nw[1, 2]],
                            ride=arrive(6, ot, "11"))
    w11 = own_slots(w11, sh11)
    x6, gu11 = ffn_fwd(h12, x5, *w11, [], "ffn_fwd_11")

    loss_v, dx6, d_final = loss_head(x6, a['final_norm_w'], a['loss_target'][0], "loss_head")
    g = {'final_norm_w': d_final[0]}

    def same_shape(xs, ys):
        runs = []
        for xv, yv in zip(xs, ys):
            if runs and runs[-1][0][0].shape == xv.shape:
                runs[-1][0].append(xv)
                runs[-1][1].append(yv)
            else:
                runs.append(([xv], [yv]))
        return runs

    def pre_reduce(grads, sib, tag):
        out = []
        for idx, (grp, sbs) in enumerate(same_shape(grads, list(sib))):
            ts = add_pair([gr.reshape(2, -1, gr.shape[-1]) for gr in grp], [_as2d(sb) for sb in sbs], c_idx,
                          "rs_add_%s_%d" % (tag, idx))
            out += [t.reshape(gr.shape[1:]) for t, gr in zip(ts, grp)]
        return out

    def chip_sum(landed, parts, tag):
        out = []
        for idx, (qs, owns) in enumerate(same_shape(list(landed), parts)):
            ts = sum_chips([q.reshape(N_CHIPS, -1, q.shape[-1]) for q in qs],
                           [own.reshape(N_CHIPS, -1, own.shape[-1]) for own in owns], chip_idx,
                           "rs_sum_%s_%d" % (tag, idx))
            out += [t.reshape(q.shape[1:]) for t, q in zip(ts, qs)]
        return out

    dnw = [[None] * 3 for _ in range(2)]
    sums = {}

    def trade(key):
        return swap_cores(sums[key], False)

    dx5, dnw12, *g11 = ffn_bwd(dx6, h12, x5, nw[1, 2], gu11, *w11, "ffn_bwd_11")
    dnw[1][2] = dnw12[0]
    (d_wo, g['attn_b_o']), sib11 = mm_tn(o, dx5, "attn_dwo", col_sum=True, ride=swap_cores(g11, True))
    t11 = pre_reduce(g11, sib11, "11")
    do = mm_nt(dx5, w_o, "attn_do", out_dtype=BF16)
    (dqt, dkt, dvt, dsink), land11 = attn_bwd(qt, kt, vt, sink_rows, _heads_major(do, N_Q_HEADS), "attn_bwd",
                                             ride=scatter_chips(t11[:2]))
    g['attn_sinks'] = jnp.sum(dsink[:, :, :Q_PER_KV, 0], axis=1).reshape(N_Q_HEADS)
    dq_pre = rope_apply(_tokens_major(dqt), cos, sin, "rope_dq", inverse=True, scale=scale, out_dtype=F32)
    d_wq, g['attn_b_q'] = mm_tn(h11, dq_pre, "attn_dwq", col_sum=True)
    g_attn = [jnp.stack([d_wq.reshape(N_CHIPS, 256, D_MODEL), d_wo.reshape(N_CHIPS, 256, D_MODEL)])]
    (dx4, dnw[1][1]), sib_attn = mm_rms_bwd([(dq_pre, 0, w_q, 0, D_MODEL, "nt")], dx5, x4, nw[1, 1], "attn_bwd_dh",
                                            ride=swap_cores(g_attn, True))
    t_attn = pre_reduce(g_attn, sib_attn, "attn")
    (dx3, dnw10, *g10), landed = ffn_bwd(dx4, h10, x3, nw[1, 0], gu10, *w10, "ffn_bwd_10",
                                         ride=scatter_chips(t_attn + t11[2:]))
    dnw[1][0] = dnw10[0]
    sums['attn'] = chip_sum(landed[:1], t_attn, "attn")
    sums['11'] = chip_sum(list(land11) + list(landed[1:]), t11, "11")
    dk_pre = rope_apply(_tokens_major(dkt), cos, sin, "rope_dk", inverse=True, out_dtype=F32)
    dv = _tokens_major(dvt)
    (d_wk, g['b_k']), sib10 = mm_tn(hkv, dk_pre, "kv_dwk", col_sum=True, ride=swap_cores(g10, True))
    t10 = pre_reduce(g10, sib10, "10")
    d_wv, g['b_v'] = mm_tn(hkv, dv, "kv_dwv", col_sum=True)
    g_kv = [jnp.stack([d_wk.reshape(N_CHIPS, 256, KV_DIM), d_wv.reshape(N_CHIPS, 256, KV_DIM)])]
    (dx3, g['kv_norm_w']), sib_kv = mm_rms_bwd(
        [(dk_pre, 0, w_k, 0, KV_DIM, "nt"), (dv, 0, w_v, 0, KV_DIM, "nt")], dx3, x3, a['kv_norm_w'], "kv_bwd_dh",
        ride=swap_cores(g_kv, True))
    t_kv = pre_reduce(g_kv, sib_kv, "kv")
    (dx2, dnw02, *g01), landed = ffn_bwd(dx3, h02, x2, nw[0, 2], gu01, *w01, "ffn_bwd_01",
                                         ride=join(scatter_chips(t10 + t_kv), trade('11'), trade('attn')))
    dnw[0][2] = dnw02[0]
    sums['10'] = chip_sum(landed[:3], t10, "10")
    sums['kv'] = chip_sum(landed[3:4], t_kv, "kv")
    theirs = {'11': landed[4:7], 'attn': landed[7:]}
    d_wout, sib01 = mm_tn(yn, dx2, "ssm_dwout", ride=swap_cores(g01, True))
    t01 = pre_reduce(g01, sib01, "01")
    dyn = mm_nt(dx2, w_out, "ssm_dyn")
    (dxs, db_, dc_, dz, ddt, d_ssm_nw, d_bias, d_a, d_d), landed = ssd_bwd(
        dyn, act, zz, y_pre, states, dtp, bias_p, a_p, d_p, ssm_nw, "ssd_bwd",
        ride=join(scatter_chips(t01[:2]), trade('10'), trade('kv')))
    land01 = list(landed[:2])
    theirs['10'], theirs['kv'] = landed[2:5], landed[5:]
    g['ssm_norm_w'] = d_ssm_nw[:SSM_GROUPS].reshape(D_INNER)
    g['ssm_dt_bias'] = d_bias[0, :SSM_HEADS]
    g['ssm_a_log'] = d_a[0, :SSM_HEADS] * a_p[0, :SSM_HEADS]
    g['ssm_d'] = d_d[0, :SSM_HEADS]
    (dxbc, g['ssm_conv_w'], g['ssm_conv_b']), landed = conv_bwd(dxs, db_, dc_, xbc, conv_w, conv_b, "ssm_conv_bwd",
                                                                ride=scatter_chips(t01[2:]))
    sums['01'] = chip_sum(land01 + list(landed), t01, "01")
    d_win = mm_tn(dz, h01, "ssm_dwz", rows=IN_PROJ_DIM)
    d_win = mm_tn(dxbc, h01, "ssm_dwxbc", into=d_win, rows=IN_PROJ_DIM, row0=D_INNER)
    d_win = mm_tn(ddt, h01, "ssm_dwdt", into=d_win, rows=IN_PROJ_DIM, row0=D_INNER + CONV_DIM, m_valid=SSM_HEADS)
    d_win = jnp.pad(d_win.reshape(N_CHIPS, IN_SHARD, D_MODEL), ((0, 0), (0, IN_SHARD_PAD - IN_SHARD), (0, 0)))
    g_ssm = [_parts_first(d_win.reshape(-1, D_MODEL), IN_SHARD_PAD // 2), _parts_first(d_wout, 256)]
    kb = 1024
    terms = ([(dz, j, w_in_t, j, kb, "nn") for j in range(D_INNER // kb)]
             + [(dxbc, j, w_in_t, D_INNER // kb + j, kb, "nn") for j in range(CONV_DIM // kb)]
             + [(ddt, 0, w_dt_t, 0, LANES, "nn")])
    (dx1, dnw[0][1]), sib_ssm = mm_rms_bwd(terms, dx2, x1, nw[0, 1], "ssm_bwd_dh", ride=swap_cores(g_ssm, True))
    t_ssm = pre_reduce(g_ssm, sib_ssm, "ssm")
    (grad_x, dnw00, *g00), landed = ffn_bwd(dx1, h00, x0, nw[0, 0], gu00, *w00, "ffn_bwd_00",
                                            ride=join(scatter_chips(t_ssm), trade('01')))
    dnw[0][0] = dnw00[0]
    sums['ssm'] = chip_sum(landed[:2], t_ssm, "ssm")
    theirs['01'] = landed[2:]
    landed = run_exchange(join(swap_cores(g00, True), trade('ssm')), "rs_swap_00")
    t00 = pre_reduce(g00, landed[:3], "00")
    theirs['ssm'] = landed[3:]

    def both(key):
        return [(jnp.where(south, m_, t_), jnp.where(south, t_, m_)) for m_, t_ in zip(sums[key], theirs[key])]

    g['norm_w'] = jnp.stack([jnp.stack(r) for r in dnw])
    red = all_reduce_small(_pack_rows([g[n] for n in SMALL] + [loss_v[0, :1]]), "reduce_vectors")

    t00 = lax.optimization_barrier((red, t00))[1]
    (flight00,), flying = split_start([t00], "scatter", "rs_scatter_00_start")

    def held(val):
        return lax.optimization_barrier((flying, val))[1]

    delta, new_m, new_v, gw = {}, {}, {}, {}
    ffn_names = ('ffn_w_gate', 'ffn_w_up', 'ffn_w_down')
    full = {key: both(key) for key in ('attn', 'kv', 'ssm')}
    lo, hi = full['attn'][0]
    gw['attn_w_q'], gw['attn_w_o'] = lo[None], hi[None]
    lo, hi = full['kv'][0]
    gw['w_k'], gw['w_v'] = lo, hi
    lo, hi = full['ssm'][0]
    gw['ssm_w_in'] = jnp.concatenate([lo, hi], axis=0)[:IN_SHARD][None]
    lo, hi = full['ssm'][1]
    gw['ssm_w_out'] = jnp.concatenate([lo, hi], axis=0)[None]

    *small_sums, loss = _unpack_rows(red, [g[n].shape for n in SMALL] + [()])
    for n, t in zip(SMALL, small_sums):
        if n in SMALL_SHARDED:
            ax = SMALL_SHARDED[n] - (a[n].ndim - t.ndim)
            width = a[n].shape[SMALL_SHARDED[n]]
            t = lax.dynamic_slice_in_dim(t, chip * width, width, axis=ax)
        gw[n] = t.reshape(a[n].shape)

    def update(n):
        d, mo, vo = adamw(_as2d(a[n]), held(_as2d(gw[n])), _as2d(a['m_' + n]), _as2d(a['v_' + n]), "adamw_" + n)
        delta[n], new_m[n], new_v[n] = d.reshape(a[n].shape), mo.reshape(a[n].shape), vo.reshape(a[n].shape)

    for n in BIG:
        if n not in ffn_names:
            update(n)
    shapes = [a[n].shape for n in SMALL]
    packed = [_pack_rows([src[n] for n in SMALL]) for src in
              (a, gw, {n: a['m_' + n] for n in SMALL}, {n: a['v_' + n] for n in SMALL})]
    outs = adamw(*packed, "adamw_vectors")
    for dst, buf in zip((delta, new_m, new_v), outs):
        for n, t in zip(SMALL, _unpack_rows(buf, shapes)):
            dst[n] = t
    for key in ('01', '10', '11'):
        sums[key] = held(list(sums[key]))
    rest = [both(key) for key in ('01', '10', '11')]
    done = lax.optimization_barrier((outs[0], [delta[n] for n in BIG if n not in ffn_names], rest))[0]
    land00 = split_arrive(flight00, "scatter", done, "rs_scatter_00_arrive")
    sums['00'] = chip_sum(land00, t00, "00")
    theirs['00'] = run_exchange(trade('00'), "rs_trade_00")
    blocks = [both('00')] + rest
    for t, n in enumerate(ffn_names):
        gw[n] = jnp.concatenate([piece for blk in blocks for piece in blk[t]], axis=0).reshape(a[n].shape)
        update(n)
    for n in TRANSPOSED:
        for dst in (gw, delta, new_m, new_v):
            dst[n] = dst[n].swapaxes(-1, -2)

    return (loss, grad_x[None], *[gw[n] for n in WEIGHTS], *[delta[n] for n in WEIGHTS],
            *[new_m[n] for n in WEIGHTS], *[new_v[n] for n in WEIGHTS])
```

```python
import math

import jax
import jax.numpy as jnp
from jax import lax
from jax.experimental import pallas as pl
from jax.experimental.pallas import tpu as pltpu

F32 = jnp.float32
BF16 = jnp.bfloat16

D_MODEL = 1024
D_INNER = 2048
SSM_HEADS = 32
SSM_GROUPS = 4
HEADS_PER_GROUP = SSM_HEADS // SSM_GROUPS
SSM_HEAD_DIM = 64
SSM_STATE = 128
GROUP_DIM = D_INNER // SSM_GROUPS
CONV_DIM = D_INNER + 2 * SSM_GROUPS * SSM_STATE
CONV_WIDTH = 4
CHUNK = 128
ATT_HEAD_DIM = 64
N_Q_HEADS = 16
N_KV_HEADS = 4
Q_PER_KV = N_Q_HEADS // N_KV_HEADS
KV_DIM = N_KV_HEADS * ATT_HEAD_DIM
WINDOW = 128
ROPE_THETA = 10000.0
D_FF = 2816
N_CHIPS = 4
N_CORES = 2
FF_SHARD = D_FF // N_CHIPS
FF_PART = FF_SHARD // N_CORES
IN_PROJ_DIM = D_INNER + CONV_DIM + SSM_HEADS
IN_SHARD = IN_PROJ_DIM // N_CHIPS
IN_SHARD_PAD = 1312
EPS = 1e-5
NEG = -1e30
LANES = 128
VMEM_LIMIT = 60 * 1024 * 1024

ADAM_LR = 0.001
ADAM_B1 = 0.9
ADAM_B2 = 0.999
ADAM_EPS = 1e-08
ADAM_WD = 0.01
ADAM_STEP = 10

NN = ((1,), (0,))
NT = ((1,), (1,))
TN = ((0,), (0,))
MESH = pl.DeviceIdType.MESH
ANY = pl.BlockSpec(memory_space=pl.ANY)


def _dot(a, b, dims=NN, precision=None):
    return lax.dot_general(a, b, (dims, ((), ())), preferred_element_type=F32, precision=precision)


def _cp(n_grid):
    return pltpu.CompilerParams(dimension_semantics=("arbitrary",) * n_grid, vmem_limit_bytes=VMEM_LIMIT)


def _sigmoid(x):
    return 1.0 / (1.0 + jnp.exp(-x))


def _rms_fwd(xf, w):
    r = lax.rsqrt(jnp.mean(xf * xf, axis=-1, keepdims=True) + EPS)
    return xf * r * w


def _rms_bwd(dh, xf, w):
    r = lax.rsqrt(jnp.mean(xf * xf, axis=-1, keepdims=True) + EPS)
    xhat = xf * r
    dxhat = dh * w
    dx = r * (dxhat - xhat * jnp.mean(dxhat * xhat, axis=-1, keepdims=True))
    return dx, dh * xhat


def _row_tile(s, pref):
    return pref if s % pref == 0 else s


def _col_tile(n):
    for t in (1024, 768, 512, 256, 128):
        if n % t == 0:
            return t
    return n


def _sds(shape, dtype):
    return jax.ShapeDtypeStruct(tuple(shape), dtype)


class Exchange:
    def __init__(self, ins, out_shapes, sems, start, finish, inplace=False):
        self.ins, self.out_shapes, self.sems, self.start, self.finish = ins, out_shapes, sems, start, finish
        self.inplace = inplace


def _place():
    x, y, c = lax.axis_index("x"), lax.axis_index("y"), lax.axis_index("c")
    others = [(1 - x, y), (x, 1 - y), (1 - x, 1 - y)]
    return x, y, c, 2 * x + y, others


def _rc(src, dst, send_sem, recv_sem, dev):
    return pltpu.make_async_remote_copy(src_ref=src, dst_ref=dst, send_sem=send_sem, recv_sem=recv_sem,
                                        device_id=dev, device_id_type=MESH)


def scatter_chips(arrs):
    n = len(arrs)

    def copies(ins, outs, sems):
        send, recv = sems
        x, y, c, k, others = _place()
        out, land = [], []
        for a in range(n):
            for j, (px, py) in enumerate(others):
                out.append(_rc(ins[a].at[2 * px + py], outs[a].at[k], send.at[a, j], recv.at[a, j], (px, py, c)))
                blk = outs[a].at[2 * px + py]
                land.append(_rc(blk, blk, send.at[a, j], recv.at[a, j], (px, py, c)))
        return out, land

    def start(ins, outs, sems):
        for cp in copies(ins, outs, sems)[0]:
            cp.start()

    def finish(ins, outs, sems):
        out, land = copies(ins, outs, sems)
        for arrived in land:
            arrived.wait_recv()
        for cp in out:
            cp.wait_send()

    return Exchange(list(arrs), [_sds(a.shape, a.dtype) for a in arrs],
                    [pltpu.SemaphoreType.DMA((n, 3)), pltpu.SemaphoreType.DMA((n, 3))], start, finish)


def swap_cores(arrs, pick_other):
    n = len(arrs)

    def copies(ins, outs, sems):
        send, recv = sems
        x, y, c, _, _ = _place()
        return [_rc(ins[a].at[1 - c] if pick_other else ins[a], outs[a], send.at[a], recv.at[a], (x, y, 1 - c))
                for a in range(n)]

    def start(ins, outs, sems):
        for cp in copies(ins, outs, sems):
            cp.start()

    def finish(ins, outs, sems):
        for cp in copies(ins, outs, sems):
            cp.wait()

    shapes = [_sds(a.shape[1:] if pick_other else a.shape, a.dtype) for a in arrs]
    return Exchange(list(arrs), shapes, [pltpu.SemaphoreType.DMA((n,)), pltpu.SemaphoreType.DMA((n,))],
                    start, finish)


def join(*parts):
    parts = [p for p in parts if p is not None]
    if not parts:
        return None

    def split(refs, counts):
        out, pos = [], 0
        for cnt in counts:
            out.append(refs[pos:pos + cnt])
            pos += cnt
        return out

    n_in = [len(p.ins) for p in parts]
    n_out = [len(p.out_shapes) for p in parts]
    n_sem = [len(p.sems) for p in parts]

    def run(which):
        def go(ins, outs, sems):
            for p, i, o, s in zip(parts, split(ins, n_in), split(outs, n_out), split(sems, n_sem)):
                getattr(p, which)(i, o, s)
        return go

    return Exchange([a for p in parts for a in p.ins], [s for p in parts for s in p.out_shapes],
                    [s for p in parts for s in p.sems], run("start"), run("finish"))


def _pcall(body, *, out_shape, grid, in_specs, out_specs, args, name, scratch_shapes=(), ride=None, aliases=None):
    out_shape, out_specs, in_specs = tuple(out_shape), tuple(out_specs), list(in_specs)
    aliases = aliases or {}
    if ride is None:
        return pl.pallas_call(body, out_shape=out_shape, grid=grid, in_specs=in_specs, out_specs=out_specs,
                              scratch_shapes=list(scratch_shapes), input_output_aliases=aliases, name=name,
                              compiler_params=_cp(len(grid)))(*args)
    n_in, n_out, n_sc = len(args), len(out_shape), len(scratch_shapes)
    n_xi, n_xo = len(ride.ins), len(ride.out_shapes)

    def wrapped(*refs):
        pos = [0]

        def take(cnt):
            got = refs[pos[0]:pos[0] + cnt]
            pos[0] += cnt
            return got

        c_in, x_in, c_out, x_out, c_sc = take(n_in), take(n_xi), take(n_out), take(n_xo), take(n_sc)
        sems = refs[pos[0]:]
        first, last = True, True
        for d, size in enumerate(grid):
            first = jnp.logical_and(first, pl.program_id(d) == 0)
            last = jnp.logical_and(last, pl.program_id(d) == size - 1)

        @pl.when(first)
        def _():
            ride.start(x_in, x_out, sems)

        body(*c_in, *c_out, *c_sc)

        @pl.when(last)
        def _():
            ride.finish(x_in, x_out, sems)

    if ride.inplace:
        aliases = {**aliases, **{n_in + t: n_out + t for t in range(n_xi)}}
    res = pl.pallas_call(
        wrapped, out_shape=out_shape + tuple(ride.out_shapes), grid=grid,
        in_specs=in_specs + [ANY] * n_xi, out_specs=out_specs + (ANY,) * n_xo,
        scratch_shapes=list(scratch_shapes) + list(ride.sems), input_output_aliases=aliases, name=name,
        compiler_params=_cp(len(grid)))(*args, *ride.ins)
    return res[:n_out], res[n_out:]


def run_exchange(ex, name):
    n_xi, n_xo = len(ex.ins), len(ex.out_shapes)

    def body(*refs):
        ins, outs, sems = refs[:n_xi], refs[n_xi:n_xi + n_xo], refs[n_xi + n_xo:]
        ex.start(ins, outs, sems)
        ex.finish(ins, outs, sems)

    aliases = {t: t for t in range(n_xi)} if ex.inplace else {}
    return pl.pallas_call(body, out_shape=tuple(ex.out_shapes), in_specs=[ANY] * n_xi, out_specs=(ANY,) * n_xo,
                          scratch_shapes=list(ex.sems), input_output_aliases=aliases, name=name)(*ex.ins)


HBM_SPEC = pl.BlockSpec(memory_space=pltpu.HBM)
SEM_SPEC = pl.BlockSpec(memory_space=pltpu.SEMAPHORE)
EFFECT = pltpu.SideEffectType.DATAFLOW_SIDE_EFFECTING


def _route(kind, src, dst, c, k, peer):
    if kind == "gather":
        return src.at[c], dst.at[c, k], dst.at[c, peer]
    return src.at[peer], dst.at[k], dst.at[peer]


def split_start(batches, kind, name):
    flat = [a for batch in batches for a in batch]
    n, nb = len(flat), len(batches)
    lands = [lax.empty((2, N_CHIPS) + a.shape[1:] if kind == "gather" else a.shape, a.dtype) for a in flat]

    def body(*refs):
        srcs, dsts, sems, token = refs[:n], refs[n:2 * n], refs[2 * n:2 * n + 2 * nb], refs[-1]
        x, y, c, k, others = _place()
        pos = 0
        for b, batch in enumerate(batches):
            for a in range(len(batch)):
                for j, (px, py) in enumerate(others):
                    src, dst, _ = _route(kind, srcs[pos], dsts[pos], c, k, 2 * px + py)
                    _rc(src, dst, sems[2 * b].at[3 * a + j], sems[2 * b + 1].at[3 * a + j], (px, py, c)).start()
                pos += 1
        token[...] = jnp.zeros(token.shape, token.dtype)

    sem_shapes = [pltpu.SemaphoreType.DMA((3 * len(batch),)) for batch in batches for _ in range(2)]
    thru = [pltpu.HBM(a.shape, a.dtype) for a in flat] + [pltpu.HBM(l.shape, l.dtype) for l in lands]
    res = pl.pallas_call(
        body, name=name, out_shape=tuple(sem_shapes + thru + [_sds((8, LANES), F32)]),
        in_specs=[HBM_SPEC] * (2 * n),
        out_specs=tuple([SEM_SPEC] * (2 * nb) + [HBM_SPEC] * (2 * n) + [pl.BlockSpec(memory_space=pltpu.VMEM)]),
        input_output_aliases={t: 2 * nb + t for t in range(2 * n)},
        compiler_params=pltpu.CompilerParams(has_side_effects=EFFECT),
    )(*[pltpu.with_memory_space_constraint(t, pltpu.HBM) for t in flat + lands])
    sems, srcs, dsts = res[:2 * nb], res[2 * nb:2 * nb + n], res[2 * nb + n:2 * nb + 2 * n]
    out, pos = [], 0
    for b, batch in enumerate(batches):
        out.append((sems[2 * b], sems[2 * b + 1], list(srcs[pos:pos + len(batch)]), list(dsts[pos:pos + len(batch)])))
        pos += len(batch)
    return out, res[-1]


def split_arrive(handle, kind, after, name):
    send, recv, srcs, dsts = handle
    n = len(srcs)

    def body(*refs):
        s_refs, d_refs, send_ref, recv_ref = refs[:n], refs[n:2 * n], refs[2 * n], refs[2 * n + 1]
        x, y, c, k, others = _place()
        for a in range(n):
            for j, (px, py) in enumerate(others):
                src, _, landed = _route(kind, s_refs[a], d_refs[a], c, k, 2 * px + py)
                cp = _rc(src, landed, send_ref.at[3 * a + j], recv_ref.at[3 * a + j], (px, py, c))
                cp.wait_send()
                cp.wait_recv()

    res = pl.pallas_call(
        body, name=name, out_shape=tuple([pltpu.HBM(t.shape, t.dtype) for t in srcs + dsts]),
        in_specs=[HBM_SPEC] * (2 * n) + [SEM_SPEC, SEM_SPEC, ANY], out_specs=tuple([HBM_SPEC] * (2 * n)),
        input_output_aliases={t: t for t in range(2 * n)},
        compiler_params=pltpu.CompilerParams(has_side_effects=EFFECT),
    )(*srcs, *dsts, send, recv, after)
    return list(res[n:])


def forward_cores(bufs):
    n = len(bufs)

    def copies(outs, sems):
        send, recv = sems
        x, y, c, k, others = _place()
        onward, land = [], []
        for a in range(n):
            for j, (px, py) in enumerate(others):
                blk = outs[a].at[c, 2 * px + py]
                onward.append(_rc(blk, blk, send.at[a, j], recv.at[a, j], (x, y, 1 - c)))
                blk2 = outs[a].at[1 - c, 2 * px + py]
                land.append(_rc(blk2, blk2, send.at[a, j], recv.at[a, j], (x, y, 1 - c)))
        return onward, land

    def start(ins, outs, sems):
        for cp in copies(outs, sems)[0]:
            cp.start()

    def finish(ins, outs, sems):
        onward, land = copies(outs, sems)
        for arrived in land:
            arrived.wait_recv()
        for cp in onward:
            cp.wait_send()

    return Exchange(list(bufs), [_sds(b.shape, b.dtype) for b in bufs],
                    [pltpu.SemaphoreType.DMA((n, 3)), pltpu.SemaphoreType.DMA((n, 3))], start, finish, inplace=True)


def all_reduce_small(buf, name):
    r = buf.shape[0]
    n_dev = 8

    def body(in_ref, o_ref, land, send_sems, recv_sems):
        x, y, c, _, _ = _place()
        me = 4 * x + 2 * y + c
        land[me] = in_ref[...]
        sends = []
        for d in range(1, n_dev):
            peer = (x ^ (d >> 2), y ^ ((d >> 1) & 1), c ^ (d & 1))
            cp = _rc(in_ref, land.at[me], send_sems.at[d], recv_sems.at[d], peer)
            cp.start()
            sends.append(cp)
        for d in range(1, n_dev):
            blk = land.at[me ^ d]
            _rc(blk, blk, send_sems.at[d], recv_sems.at[d], (x, y, c)).wait_recv()
        for cp in sends:
            cp.wait_send()
        tot = land[0]
        for d in range(1, n_dev):
            tot = tot + land[d]
        o_ref[...] = tot

    vm = pl.BlockSpec(memory_space=pltpu.VMEM)
    return pl.pallas_call(
        body, out_shape=_sds(buf.shape, F32), in_specs=[vm], out_specs=vm,
        scratch_shapes=[pltpu.VMEM((n_dev, r, LANES), F32), pltpu.SemaphoreType.DMA((n_dev,)),
                        pltpu.SemaphoreType.DMA((n_dev,))],
        name=name)(buf)


def rmsnorm_fwd(x, w, name):
    s, d = x.shape
    tm = _row_tile(s, 512)

    def body(x_ref, w_ref, o_ref):
        o_ref[...] = _rms_fwd(x_ref[...], w_ref[...]).astype(BF16)

    return _pcall(body, out_shape=[_sds((s, d), BF16)], grid=(s // tm,),
                  in_specs=[pl.BlockSpec((tm, d), lambda i: (i, 0)), pl.BlockSpec((1, d), lambda i: (0, 0))],
                  out_specs=[pl.BlockSpec((tm, d), lambda i: (i, 0))], args=[x, w.reshape(1, d)], name=name)[0]


def _ffn_w_spec(chip_of, single=False):
    mode = dict(pipeline_mode=pl.Buffered(1)) if single else {}
    return pl.BlockSpec((N_CORES, 1, FF_PART, D_MODEL), lambda *ids: (0, chip_of(*ids), 0, 0), **mode)


def ffn_fwd(h, x, wg, wu, wd, norm_ws, name, ride=None):
    s, d = h.shape
    n_norm = len(norm_ws)
    tm = _row_tile(s, 1024)

    def body(*refs):
        h_ref, x_ref, wg_ref, wu_ref, wd_ref = refs[:5]
        nw_refs = refs[5:5 + n_norm]
        o_ref = refs[5 + n_norm]
        h_refs = refs[6 + n_norm:6 + 2 * n_norm]
        gu_ref, acc = refs[6 + 2 * n_norm], refs[7 + 2 * n_norm]
        k = pl.program_id(1)

        @pl.when(k == 0)
        def _():
            acc[...] = jnp.zeros(acc.shape, F32)

        hm = tm // 2
        for part in range(2):
            sub = pl.ds(part * hm, hm)
            hb = h_ref[sub, :]
            g = _dot(hb, wg_ref[...].reshape(FF_SHARD, d), NT)
            u = _dot(hb, wu_ref[...].reshape(FF_SHARD, d), NT)
            gu_ref[0, 0, sub, :] = g.astype(BF16)
            gu_ref[0, 1, sub, :] = u.astype(BF16)
            acc[sub, :] += _dot((g * _sigmoid(g) * u).astype(BF16), wd_ref[...].reshape(FF_SHARD, d))

        @pl.when(k == N_CHIPS - 1)
        def _():
            xn = x_ref[...] + 0.5 * acc[...]
            o_ref[...] = xn
            for nw_ref, hn_ref in zip(nw_refs, h_refs):
                hn_ref[...] = _rms_fwd(xn, nw_ref[...]).astype(BF16)

    row = pl.BlockSpec((tm, d), lambda i, k: (i, 0))
    vec = pl.BlockSpec((1, d), lambda i, k: (0, 0))
    wsp = _ffn_w_spec(lambda i, k: k)
    return _pcall(
        body, out_shape=[_sds((s, d), F32)] + [_sds((s, d), BF16)] * n_norm + [_sds((N_CHIPS, 2, s, FF_SHARD), BF16)],
        grid=(s // tm, N_CHIPS),
        in_specs=[row, row, wsp, wsp, wsp] + [vec] * n_norm,
        out_specs=[row] * (1 + n_norm) + [pl.BlockSpec((1, 2, tm, FF_SHARD), lambda i, k: (k, 0, i, 0))],
        scratch_shapes=[pltpu.VMEM((tm, d), F32)],
        args=[h, x, wg, wu, wd] + [nw.reshape(1, d) for nw in norm_ws], name=name, ride=ride)


def ffn_bwd(dxn, h, x_in, nw, gu, wg, wu, wd, name, ride=None):
    s, d = h.shape
    tm = _row_tile(s, 512)
    ni = s // tm
    last_e = N_CHIPS - 1

    def body(dxn_ref, h_ref, x_ref, nw_ref, gu_ref, wg_ref, wu_ref, wd_ref,
             dx_ref, dnw_ref, dwg_ref, dwu_ref, dwd_ref, dh, wacc):
        e = pl.program_id(0)
        i = pl.program_id(1)
        rows = pl.ds(pl.multiple_of(i * tm, tm), tm)

        @pl.when(i == 0)
        def _():
            wacc[...] = jnp.zeros(wacc.shape, F32)

        @pl.when(e == 0)
        def _():
            dh[rows, :] = jnp.zeros((tm, d), F32)

        hm = tm // 2
        for part in range(2):
            sub = pl.ds(part * hm, hm)
            dxb = dxn_ref[sub, :].astype(BF16)
            hb = h_ref[sub, :]
            g = gu_ref[0, 0, sub, :].astype(F32)
            u = gu_ref[0, 1, sub, :].astype(F32)
            drows = pl.ds(pl.multiple_of(i * tm + part * hm, hm), hm)
            sg = _sigmoid(g)
            silu = g * sg
            wacc[2] += _dot((0.5 * silu * u).astype(BF16), dxb, TN)
            da = 0.5 * _dot(dxb, wd_ref[...].reshape(FF_SHARD, d), NT)
            dg = (da * u * (sg * (1.0 + g * (1.0 - sg)))).astype(BF16)
            wacc[0] += _dot(dg, hb, TN)
            du = (da * silu).astype(BF16)
            dh[drows, :] += _dot(dg, wg_ref[...].reshape(FF_SHARD, d))
            wacc[1] += _dot(du, hb, TN)
            dh[drows, :] += _dot(du, wu_ref[...].reshape(FF_SHARD, d))

        @pl.when(i == ni - 1)
        def _():
            for t, dw_ref in enumerate((dwg_ref, dwu_ref, dwd_ref)):
                dw_ref[...] = wacc[t].astype(BF16).reshape(N_CORES, 1, FF_PART, d)

        @pl.when(e == last_e)
        def _():
            dx, dnw = _rms_bwd(dh[rows, :], x_ref[...], nw_ref[...])
            dx_ref[...] = dxn_ref[...] + dx
            col = jnp.sum(dnw, axis=0, keepdims=True)

            @pl.when(i == 0)
            def _():
                dnw_ref[...] = col

            @pl.when(i > 0)
            def _():
                dnw_ref[...] += col

    row = pl.BlockSpec((tm, d), lambda e, i: (i, 0))
    late = pl.BlockSpec((tm, d), lambda e, i: (jnp.where(e == last_e, i, 0), 0))
    vec = pl.BlockSpec((1, d), lambda e, i: (0, 0))
    wsp = _ffn_w_spec(lambda e, i: e)
    dwsp = _ffn_w_spec(lambda e, i: e, single=True)
    dw = _sds((N_CORES, N_CHIPS, FF_PART, d), BF16)
    return _pcall(
        body, out_shape=[_sds((s, d), F32), _sds((1, d), F32), dw, dw, dw],
        grid=(N_CHIPS, ni),
        in_specs=[row, row, late, vec, pl.BlockSpec((1, 2, tm, FF_SHARD), lambda e, i: (e, 0, i, 0)), wsp, wsp, wsp],
        out_specs=[late, vec, dwsp, dwsp, dwsp],
        scratch_shapes=[pltpu.VMEM((s, d), F32), pltpu.VMEM((3, FF_SHARD, d), F32)],
        args=[dxn, h, x_in, nw.reshape(1, d), gu, wg, wu, wd], name=name, ride=ride)


def mm_res(a, w, x, name, bias=None, norm_ws=(), ride=None):
    s, k = a.shape
    n = w.shape[1]
    tm = _row_tile(s, 512)
    has_bias = bias is not None
    n_norm = len(norm_ws)

    def body(*refs):
        a_ref, w_ref, x_ref = refs[:3]
        pos = 3
        t = _dot(a_ref[...], w_ref[...])
        if has_bias:
            t = t + refs[pos][...]
            pos += 1
        nw_refs = refs[pos:pos + n_norm]
        o_ref = refs[pos + n_norm]
        h_refs = refs[pos + n_norm + 1:]
        xn = x_ref[...] + t
        o_ref[...] = xn
        for nw_ref, h_ref in zip(nw_refs, h_refs):
            h_ref[...] = _rms_fwd(xn, nw_ref[...]).astype(BF16)

    row = pl.BlockSpec((tm, n), lambda i: (i, 0))
    vec = pl.BlockSpec((1, n), lambda i: (0, 0))
    in_specs = [pl.BlockSpec((tm, k), lambda i: (i, 0)), pl.BlockSpec((k, n), lambda i: (0, 0)), row]
    args = [a, w, x]
    if has_bias:
        in_specs.append(vec)
        args.append(bias.reshape(1, n))
    for nw in norm_ws:
        in_specs.append(vec)
        args.append(nw.reshape(1, n))
    return _pcall(body, out_shape=[_sds((s, n), F32)] + [_sds((s, n), BF16)] * n_norm, grid=(s // tm,),
                  in_specs=in_specs, out_specs=[row] * (1 + n_norm), args=args, name=name, ride=ride)


def mm_nn(a, w, name, bias=None, out_dtype=F32):
    s, k = a.shape
    n = w.shape[1]
    tm = _row_tile(s, 512)
    tn = _col_tile(n)
    has_bias = bias is not None

    def body(*refs):
        a_ref, w_ref = refs[:2]
        o_ref = refs[-1]
        t = _dot(a_ref[...], w_ref[...])
        if has_bias:
            t = t + refs[2][...]
        o_ref[...] = t.astype(out_dtype)

    in_specs = [pl.BlockSpec((tm, k), lambda j, i: (i, 0)), pl.BlockSpec((k, tn), lambda j, i: (0, j))]
    args = [a, w]
    if has_bias:
        in_specs.append(pl.BlockSpec((1, tn), lambda j, i: (0, j)))
        args.append(bias.reshape(1, n))
    return _pcall(body, out_shape=[_sds((s, n), out_dtype)], grid=(n // tn, s // tm), in_specs=in_specs,
                  out_specs=[pl.BlockSpec((tm, tn), lambda j, i: (i, j))], args=args, name=name)[0]


def mm_nt(a, w, name, n=None, row0=0, out_dtype=F32, ride=None):
    s, k = a.shape
    n = w.shape[0] if n is None else n
    tm = _row_tile(s, 512)
    tn = _col_tile(n)
    base = row0 // tn
    assert row0 % tn == 0

    def body(a_ref, w_ref, o_ref):
        o_ref[...] = _dot(a_ref[...].astype(BF16), w_ref[...], NT).astype(out_dtype)

    res = _pcall(body, out_shape=[_sds((s, n), out_dtype)], grid=(n // tn, s // tm),
                 in_specs=[pl.BlockSpec((tm, k), lambda j, i: (i, 0)), pl.BlockSpec((tn, k), lambda j, i: (base + j, 0))],
                 out_specs=[pl.BlockSpec((tm, tn), lambda j, i: (i, j))], args=[a, w], name=name, ride=ride)
    return res[0] if ride is None else (res[0][0], res[1])


def mm_tn(a, b, name, into=None, rows=None, row0=0, m_valid=None, col_sum=False, ride=None):
    s, m = a.shape
    n = b.shape[1]
    mv = m if m_valid is None else m_valid
    tm = _col_tile(m) if m_valid is None else mv
    tn = 512 if n % 512 == 0 else n
    rows = mv if rows is None else rows
    assert row0 % tm == 0 and (m_valid is None or m == LANES)
    assert not col_sum or mv == tm
    base = row0 // tm
    ta = m if m_valid is not None else tm

    def body(*refs):
        a_ref, b_ref = refs[0], refs[1]
        o_ref = refs[-2] if col_sum else refs[-1]
        bf = b_ref[...]
        t = _dot(a_ref[...].astype(BF16), bf.astype(BF16), TN)
        o_ref[...] = t[:tm].astype(BF16)
        if col_sum:
            refs[-1][...] = jnp.sum(bf.astype(F32), axis=0, keepdims=True)

    in_specs = [pl.BlockSpec((s, ta), lambda i, j: (0, i)), pl.BlockSpec((s, tn), lambda i, j: (0, j))]
    args = [a, b]
    aliases = None
    if into is not None:
        in_specs.append(ANY)
        args.append(into)
        aliases = {2: 0}
    out_shape = [_sds((rows, n), BF16)]
    out_specs = [pl.BlockSpec((tm, tn), lambda i, j: (base + i, j))]
    if col_sum:
        out_shape.append(_sds((1, n), F32))
        out_specs.append(pl.BlockSpec((1, tn), lambda i, j: (0, j)))
    res = _pcall(body, out_shape=out_shape, grid=(mv // tm, n // tn), in_specs=in_specs, out_specs=out_specs,
                 args=args, name=name, ride=ride, aliases=aliases)
    outs = res if ride is None else res[0]
    out = (outs[0], outs[1][0]) if col_sum else outs[0]
    return out if ride is None else (out, res[1])


def mm_rms_bwd(terms, dxn, x, nw, name, ride=None):
    s, n = x.shape
    nt_ = len(terms)
    tm = _row_tile(s, 512 if nt_ <= 2 else 256)
    forms = [t[5] for t in terms]

    def body(*refs):
        dxn_ref, x_ref, nw_ref, dx_ref, dnw_ref = refs[2 * nt_:]
        i = pl.program_id(0)
        dh = None
        for t in range(nt_):
            part = _dot(refs[2 * t][...].astype(BF16), refs[2 * t + 1][...], NN if forms[t] == "nn" else NT)
            dh = part if dh is None else dh + part
        dx, dnw = _rms_bwd(dh, x_ref[...], nw_ref[...])
        dx_ref[...] = dxn_ref[...] + dx
        col = jnp.sum(dnw, axis=0, keepdims=True)

        @pl.when(i == 0)
        def _():
            dnw_ref[...] = col

        @pl.when(i > 0)
        def _():
            dnw_ref[...] += col

    in_specs, args = [], []
    for a, cb, w, rb, kb, form in terms:
        in_specs.append(pl.BlockSpec((tm, kb), lambda i, cb=cb: (i, cb)))
        if form == "nn":
            in_specs.append(pl.BlockSpec((kb, n), lambda i, rb=rb: (rb, 0)))
        else:
            in_specs.append(pl.BlockSpec((n, kb), lambda i, rb=rb: (0, rb)))
        args += [a, w]
    row = pl.BlockSpec((tm, n), lambda i: (i, 0))
    vec = pl.BlockSpec((1, n), lambda i: (0, 0))
    res = _pcall(body, out_shape=[_sds((s, n), F32), _sds((1, n), F32)], grid=(s // tm,),
                 in_specs=in_specs + [row, row, vec], out_specs=[row, vec],
                 args=args + [dxn, x, nw.reshape(1, n)], name=name, ride=ride)
    outs = res if ride is None else res[0]
    out = (outs[0], outs[1][0])
    return out if ride is None else (out, res[1])


def rope_tables(s):
    pos = jnp.arange(s, dtype=F32)
    inv = 1.0 / (ROPE_THETA ** (jnp.arange(0, ATT_HEAD_DIM, 2, dtype=F32) / ATT_HEAD_DIM))
    ang = pos[:, None] * inv[None, :]
    cos = jnp.tile(jnp.cos(ang), (1, 2 * LANES // ATT_HEAD_DIM))
    sin = jnp.tile(jnp.sin(ang), (1, 2 * LANES // ATT_HEAD_DIM))
    return cos, sin


def rope_apply(t, cos, sin, name, inverse=False, scale=1.0, out_dtype=BF16):
    s, n = t.shape
    tm = _row_tile(s, 512)
    half = ATT_HEAD_DIM // 2
    reps = n // LANES

    def body(t_ref, c_ref, s_ref, o_ref):
        tf = t_ref[...].astype(F32)
        c = jnp.tile(c_ref[...], (1, reps))
        sn = jnp.tile(s_ref[...], (1, reps))
        lane = lax.broadcasted_iota(jnp.int32, tf.shape, 1)
        first = (lane & (ATT_HEAD_DIM - 1)) < half
        rot = jnp.where(first, -pltpu.roll(tf, n - half, 1), pltpu.roll(tf, half, 1))
        sign = -1.0 if inverse else 1.0
        o_ref[...] = (scale * (tf * c + sign * rot * sn)).astype(out_dtype)

    tab = pl.BlockSpec((tm, LANES), lambda i: (i, 0))
    return _pcall(body, out_shape=[_sds((s, n), out_dtype)], grid=(s // tm,),
                  in_specs=[pl.BlockSpec((tm, n), lambda i: (i, 0)), tab, tab],
                  out_specs=[pl.BlockSpec((tm, n), lambda i: (i, 0))], args=[t, cos, sin], name=name)[0]


CONV_TILE = 256


def _shift_down(u, k):
    if k == 0:
        return u
    row = lax.broadcasted_iota(jnp.int32, u.shape, 0)
    return jnp.where(row >= k, pltpu.roll(u, k, 0), 0.0)


def _shift_up(u, k):
    if k == 0:
        return u
    s = u.shape[0]
    row = lax.broadcasted_iota(jnp.int32, u.shape, 0)
    return jnp.where(row < s - k, pltpu.roll(u, s - k, 0), 0.0)


def _conv_taps(u):
    return [_shift_down(u, CONV_WIDTH - 1 - k) for k in range(CONV_WIDTH)]


def _conv_pre(taps, w_ref, b_ref):
    pre = b_ref[...] + w_ref[0:1, :] * taps[0]
    for k in range(1, CONV_WIDTH):
        pre += w_ref[k:k + 1, :] * taps[k]
    return pre


def conv_fwd(u, w, b, name, ride=None):
    s, c = u.shape

    def body(u_ref, w_ref, b_ref, o_ref):
        pre = _conv_pre(_conv_taps(u_ref[...]), w_ref, b_ref)
        o_ref[...] = pre * _sigmoid(pre)

    col = pl.BlockSpec((s, CONV_TILE), lambda j: (0, j))
    res = _pcall(body, out_shape=[_sds((s, c), F32)], grid=(c // CONV_TILE,),
                 in_specs=[col, pl.BlockSpec((CONV_WIDTH, CONV_TILE), lambda j: (0, j)),
                           pl.BlockSpec((1, CONV_TILE), lambda j: (0, j))],
                 out_specs=[col], args=[u, w, b.reshape(1, c)], name=name, ride=ride)
    return res[0] if ride is None else (res[0][0], res[1])


def conv_bwd(dxs, db_, dc_, u, w, b, name, ride=None):
    s, c = u.shape
    n_x = dxs.shape[1] // CONV_TILE
    n_b = db_.shape[1] // CONV_TILE

    def body(dx_ref, dbb_ref, dcc_ref, u_ref, w_ref, b_ref, du_ref, dw_ref, dbias_ref):
        j = pl.program_id(0)
        dact = jnp.where(j < n_x, dx_ref[...], jnp.where(j < n_x + n_b, dbb_ref[...], dcc_ref[...]))
        taps = _conv_taps(u_ref[...])
        pre = _conv_pre(taps, w_ref, b_ref)
        sg = _sigmoid(pre)
        dpre = dact * (sg * (1.0 + pre * (1.0 - sg)))
        du = w_ref[CONV_WIDTH - 1:CONV_WIDTH, :] * dpre
        for k in range(CONV_WIDTH - 1):
            du += w_ref[k:k + 1, :] * _shift_up(dpre, CONV_WIDTH - 1 - k)
        du_ref[...] = du
        dbias_ref[...] = jnp.sum(dpre, axis=0, keepdims=True)
        for k in range(CONV_WIDTH):
            dw_ref[k:k + 1, :] = jnp.sum(dpre * taps[k], axis=0, keepdims=True)

    col = pl.BlockSpec((s, CONV_TILE), lambda j: (0, j))
    wsp = pl.BlockSpec((CONV_WIDTH, CONV_TILE), lambda j: (0, j))
    bsp = pl.BlockSpec((1, CONV_TILE), lambda j: (0, j))
    res = _pcall(
        body, out_shape=[_sds((s, c), F32), _sds((CONV_WIDTH, c), F32), _sds((1, c), F32)], grid=(c // CONV_TILE,),
        in_specs=[pl.BlockSpec((s, CONV_TILE), lambda j: (0, jnp.minimum(j, n_x - 1))),
                  pl.BlockSpec((s, CONV_TILE), lambda j: (0, jnp.clip(j - n_x, 0, n_b - 1))),
                  pl.BlockSpec((s, CONV_TILE), lambda j: (0, jnp.clip(j - n_x - n_b, 0, n_b - 1))),
                  col, wsp, bsp],
        out_specs=[col, wsp, bsp], args=[dxs, db_, dc_, u, w, b.reshape(1, c)], name=name, ride=ride)
    (du, dw, db), rode = res if ride is not None else (res, None)
    return (du, dw, db[0]) if ride is None else ((du, dw, db[0]), rode)


def _lane_pick(mat, idx):
    lane = lax.broadcasted_iota(jnp.int32, mat.shape, 1)
    return jnp.sum(jnp.where(lane == idx, mat, 0.0), axis=1, keepdims=True)


def _sub_pick(mat, idx):
    sub = lax.broadcasted_iota(jnp.int32, mat.shape, 0)
    return jnp.sum(jnp.where(sub == idx, mat, 0.0), axis=0, keepdims=True)


def _expand_heads(cols):
    rows = cols[0].shape[0]
    left = lax.broadcasted_iota(jnp.int32, (rows, LANES), 1) < SSM_HEAD_DIM
    return jnp.concatenate(
        [jnp.where(left, cols[2 * p], cols[2 * p + 1]) for p in range(HEADS_PER_GROUP // 2)], axis=1)


def _dot_01(x, ones, ones_first, pieces):
    tot, rest = None, x
    for _ in range(pieces):
        piece = rest.astype(BF16)
        rest = rest - piece.astype(F32)
        part = _dot(ones, piece) if ones_first else _dot(piece, ones)
        tot = part if tot is None else tot + part
    return tot


def _heads_to_lanes(mat, g):
    jj = lax.broadcasted_iota(jnp.int32, (GROUP_DIM, LANES), 0)
    ll = lax.broadcasted_iota(jnp.int32, (GROUP_DIM, LANES), 1)
    sel = (ll == HEADS_PER_GROUP * g + (jj >> 6)).astype(BF16)
    return _dot_01(mat, sel, False, 3)


def _softplus(x):
    return jnp.maximum(x, 0.0) + jnp.log1p(jnp.exp(-jnp.abs(x)))


def _ssd_scalars(dt_ref, bias_ref, a_ref, dtall, csall, cst):
    dta = _softplus(dt_ref[...] + bias_ref[...])
    row = lax.broadcasted_iota(jnp.int32, (CHUNK, CHUNK), 0)
    col = lax.broadcasted_iota(jnp.int32, (CHUNK, CHUNK), 1)
    cs = _dot_01(dta * a_ref[...], (row >= col).astype(BF16), True, 3)
    dtall[...] = dta
    csall[...] = cs
    cst[...] = cs.T


def _decay_mat(cs_col, cs_row):
    row = lax.broadcasted_iota(jnp.int32, (CHUNK, CHUNK), 0)
    col = lax.broadcasted_iota(jnp.int32, (CHUNK, CHUNK), 1)
    return jnp.exp(jnp.where(row >= col, cs_col - cs_row, NEG))


def _head_mask(xpair, right):
    lane = lax.broadcasted_iota(jnp.int32, xpair.shape, 1)
    keep = (lane >= SSM_HEAD_DIM) if right else (lane < SSM_HEAD_DIM)
    return jnp.where(keep, xpair, 0.0)


def _chunk_cols(x_all, g):
    return [_lane_pick(x_all, HEADS_PER_GROUP * g + r) for r in range(HEADS_PER_GROUP)]


def _decay_col(cs_cols):
    return jnp.concatenate(
        [jnp.broadcast_to(jnp.exp(cc[CHUNK - 1:CHUNK, :]), (SSM_HEAD_DIM, 1)) for cc in cs_cols], axis=0)


def ssd_fwd(act, z, dtp, bias_p, a_p, d_p, normw, name, ride=None):
    s = act.shape[0]
    nc = s // CHUNK

    def body(xs_all, b_all, c_all, z_all, dt_ref, bias_ref, a_ref, d_ref, nw_all,
             yn_all, y_all, st_all, state, dtall, csall, cst):
        _ssd_scalars(dt_ref, bias_ref, a_ref, dtall, csall, cst)

        @pl.when(pl.program_id(0) == 0)
        def _():
            state[...] = jnp.zeros(state.shape, F32)

        for g in range(SSM_GROUPS):
            wide = pl.ds(g * GROUP_DIM, GROUP_DIM)
            narrow = pl.ds(g * SSM_STATE, SSM_STATE)
            group(g, xs_all.at[:, wide], b_all.at[:, narrow], c_all.at[:, narrow], z_all.at[:, wide], d_ref,
                  nw_all.at[:, wide], yn_all.at[:, wide], y_all.at[:, wide], st_all.at[:, pl.ds(g, 1)],
                  state, dtall, csall, cst)

    def group(g, xs_ref, b_ref, c_ref, z_ref, d_ref, nw_ref, yn_ref, y_ref, st_ref, state, dtall, csall, cst):
        cs_cols = _chunk_cols(csall[...], g)
        dt_cols = _chunk_cols(dtall[...], g)
        cs_rows = [_sub_pick(cst[...], HEADS_PER_GROUP * g + r) for r in range(HEADS_PER_GROUP)]
        d_cols = _chunk_cols(d_ref[...], g)
        cs_exp = _expand_heads(cs_cols)
        dt_exp = _expand_heads(dt_cols)
        d_exp = _expand_heads(d_cols)
        xs = xs_ref[...]
        bb = b_ref[...].astype(BF16)
        cb16 = c_ref[...].astype(BF16)
        xdt = xs * dt_exp
        s_prev = state[g]
        st_ref[0, 0] = s_prev
        y_off = _dot(cb16, s_prev.astype(BF16), NT) * jnp.exp(cs_exp)
        decay_st = jnp.exp(cs_exp[CHUNK - 1:CHUNK, :] - cs_exp)
        contrib = _dot((xdt * decay_st).astype(BF16), bb, TN)
        state[g] = _decay_col(cs_cols) * s_prev + contrib
        cbm = _dot(cb16, bb, NT)
        pairs = []
        for p in range(HEADS_PER_GROUP // 2):
            xpair = xdt[:, LANES * p:LANES * (p + 1)]
            m0 = (cbm * _decay_mat(cs_cols[2 * p], cs_rows[2 * p])).astype(BF16)
            m1 = (cbm * _decay_mat(cs_cols[2 * p + 1], cs_rows[2 * p + 1])).astype(BF16)
            pairs.append(_dot(m0, _head_mask(xpair, False).astype(BF16))
                         + _dot(m1, _head_mask(xpair, True).astype(BF16)))
        y = jnp.concatenate(pairs, axis=1) + y_off + xs * d_exp
        y_ref[...] = y
        zf = z_ref[...]
        yg = y * (zf * _sigmoid(zf))
        yn_ref[...] = _rms_fwd(yg, nw_ref[...]).astype(BF16)

    gn = SSM_GROUPS * SSM_STATE
    wide = pl.BlockSpec((CHUNK, D_INNER), lambda c: (c, 0))
    par = pl.BlockSpec((1, LANES), lambda c: (0, 0))
    return _pcall(
        body,
        out_shape=[_sds((s, D_INNER), BF16), _sds((s, D_INNER), F32),
                   _sds((nc, SSM_GROUPS, GROUP_DIM, SSM_STATE), F32)],
        grid=(nc,),
        in_specs=[wide,
                  pl.BlockSpec((CHUNK, gn), lambda c: (c, D_INNER // gn)),
                  pl.BlockSpec((CHUNK, gn), lambda c: (c, D_INNER // gn + 1)),
                  wide,
                  pl.BlockSpec((CHUNK, LANES), lambda c: (c, 0)),
                  par, par, par,
                  pl.BlockSpec((1, D_INNER), lambda c: (0, 0))],
        out_specs=[wide, wide, pl.BlockSpec((1, SSM_GROUPS, GROUP_DIM, SSM_STATE), lambda c: (c, 0, 0, 0))],
        scratch_shapes=[pltpu.VMEM((SSM_GROUPS, GROUP_DIM, SSM_STATE), F32),
                        pltpu.VMEM((CHUNK, LANES), F32), pltpu.VMEM((CHUNK, LANES), F32),
                        pltpu.VMEM((LANES, CHUNK), F32)],
        args=[act, act, act, z, dtp, bias_p, a_p, d_p, normw], name=name, ride=ride)


def ssd_bwd(dyn, act, z, y_pre, states, dtp, bias_p, a_p, d_p, normw, name, ride=None):
    s = act.shape[0]
    nc = s // CHUNK

    def body(dyn_all, xs_all, b_all, c_all, z_all, y_all, st_all, dt_ref, bias_ref, a_ref, d_ref, nw_all,
             dxs_all, db_all, dc_all, dz_all, ddt_ref, dnw_ref, dbias_ref, da_ref, dd_ref,
             dstate, dtall, csall, cst):
        _ssd_scalars(dt_ref, bias_ref, a_ref, dtall, csall, cst)
        ddt_ref[...] = jnp.zeros((CHUNK, LANES), F32)

        @pl.when(pl.program_id(0) == 0)
        def _():
            dstate[...] = jnp.zeros(dstate.shape, F32)
            dnw_ref[...] = jnp.zeros(dnw_ref.shape, F32)
            dbias_ref[...] = jnp.zeros((1, LANES), F32)
            da_ref[...] = jnp.zeros((1, LANES), F32)
            dd_ref[...] = jnp.zeros((1, LANES), F32)

        for g in range(SSM_GROUPS):
            wide = pl.ds(g * GROUP_DIM, GROUP_DIM)
            narrow = pl.ds(g * SSM_STATE, SSM_STATE)
            group(g, dyn_all.at[:, wide], xs_all.at[:, wide], b_all.at[:, narrow], c_all.at[:, narrow],
                  z_all.at[:, wide], y_all.at[:, wide], st_all.at[:, pl.ds(g, 1)], dt_ref, bias_ref, a_ref, d_ref,
                  nw_all.at[:, wide], dxs_all.at[:, wide], db_all.at[:, narrow], dc_all.at[:, narrow],
                  dz_all.at[:, wide], ddt_ref, dnw_ref, dbias_ref, da_ref, dd_ref, dstate, dtall, csall, cst)

    def group(g, dyn_ref, xs_ref, b_ref, c_ref, z_ref, y_ref, st_ref, dt_ref, bias_ref, a_ref, d_ref, nw_ref,
              dxs_ref, db_ref, dc_ref, dz_ref, ddt_ref, dnw_ref, dbias_ref, da_ref, dd_ref,
              dstate, dtall, csall, cst):
        cs_cols = _chunk_cols(csall[...], g)
        dt_cols = _chunk_cols(dtall[...], g)
        cs_rows = [_sub_pick(cst[...], HEADS_PER_GROUP * g + r) for r in range(HEADS_PER_GROUP)]
        d_cols = _chunk_cols(d_ref[...], g)
        cs_exp = _expand_heads(cs_cols)
        dt_exp = _expand_heads(dt_cols)
        d_exp = _expand_heads(d_cols)
        xs = xs_ref[...]
        bb = b_ref[...].astype(BF16)
        cb16 = c_ref[...].astype(BF16)
        xdt = xs * dt_exp
        s_prev = st_ref[0, 0]
        s_prev16 = s_prev.astype(BF16)
        ds_next = dstate[g]
        ds16 = ds_next.astype(BF16)

        zf = z_ref[...]
        sz = _sigmoid(zf)
        silu_z = zf * sz
        y = y_ref[...]
        yg = y * silu_z
        dout = dyn_ref[...]
        dyg, dnw = _rms_bwd(dout, yg, nw_ref[...])
        dnw_ref[pl.ds(g, 1), :] += jnp.sum(dnw, axis=0, keepdims=True)
        dy = dyg * silu_z
        dz_ref[...] = dyg * y * (sz * (1.0 + zf * (1.0 - sz)))
        dd_ref[...] += jnp.sum(_heads_to_lanes(dy * xs, g), axis=0, keepdims=True)

        exp_cs = jnp.exp(cs_exp)
        decay_st = jnp.exp(cs_exp[CHUNK - 1:CHUNK, :] - cs_exp)
        cs_t = _dot(cb16, s_prev16, NT)
        dyo = dy * exp_cs
        dc_acc = _dot(dyo.astype(BF16), s_prev16, NN)
        g1 = _dot(bb, ds16, NT)
        xds = xdt * decay_st
        db_acc = _dot(xds.astype(BF16), ds16, NN)
        dxdt_off = g1 * decay_st
        t_exp = g1 * xds
        dcs_exp = dy * cs_t * exp_cs - t_exp
        decay_c = _decay_col(cs_cols)
        dstate[g] = decay_c * ds_next + _dot(dyo.astype(BF16), cb16, TN)
        dlast_col = jnp.sum(ds_next * s_prev, axis=1, keepdims=True) * decay_c
        jj = lax.broadcasted_iota(jnp.int32, (GROUP_DIM, LANES), 0)
        ll = lax.broadcasted_iota(jnp.int32, (GROUP_DIM, LANES), 1)
        sel = ll == HEADS_PER_GROUP * g + (jj >> 6)
        dlast = jnp.sum(jnp.where(sel, dlast_col, 0.0), axis=0, keepdims=True)
        t_all = _heads_to_lanes(t_exp, g)
        dlast += jnp.sum(t_all, axis=0, keepdims=True)
        dcs_all = _heads_to_lanes(dcs_exp, g)

        cbm = _dot(cb16, bb, NT)
        dcb = jnp.zeros((CHUNK, CHUNK), F32)
        dcs_rows = jnp.zeros((LANES, CHUNK), F32)
        lane_l = lax.broadcasted_iota(jnp.int32, (CHUNK, LANES), 1)
        sub_l = lax.broadcasted_iota(jnp.int32, (LANES, CHUNK), 0)
        dxdt_pairs = []
        for p in range(HEADS_PER_GROUP // 2):
            xpair16 = xdt[:, LANES * p:LANES * (p + 1)].astype(BF16)
            dypair = dy[:, LANES * p:LANES * (p + 1)]
            acc = None
            for r in (2 * p, 2 * p + 1):
                lm = _decay_mat(cs_cols[r], cs_rows[r])
                m = cbm * lm
                dyh = _head_mask(dypair, r % 2 == 1).astype(BF16)
                dm = _dot(dyh, xpair16, NT)
                dcb += dm * lm
                q = dm * m
                idx = HEADS_PER_GROUP * g + r
                dcs_all += jnp.where(lane_l == idx, jnp.sum(q, axis=1, keepdims=True), 0.0)
                dcs_rows -= jnp.where(sub_l == idx, jnp.sum(q, axis=0, keepdims=True), 0.0)
                part = _dot(m.astype(BF16), dyh, TN)
                acc = part if acc is None else acc + part
            dxdt_pairs.append(acc)
        dxdt = jnp.concatenate(dxdt_pairs, axis=1) + dxdt_off
        dcb16 = dcb.astype(BF16)
        dc_ref[...] = dc_acc + _dot(dcb16, bb, NN)
        db_ref[...] = db_acc + _dot(dcb16, cb16, TN)
        dxs_ref[...] = dxdt * dt_exp + dy * d_exp

        dcs_all += dcs_rows.T
        row = lax.broadcasted_iota(jnp.int32, (CHUNK, CHUNK), 0)
        col = lax.broadcasted_iota(jnp.int32, (CHUNK, CHUNK), 1)
        last_row = lax.broadcasted_iota(jnp.int32, (CHUNK, LANES), 0) == CHUNK - 1
        dcs_all += jnp.where(last_row, dlast, 0.0)
        da_all = _dot_01(dcs_all, (col >= row).astype(BF16), True, 3)
        dta = dtall[...]
        in_group = jnp.logical_and(lane_l >= HEADS_PER_GROUP * g, lane_l < HEADS_PER_GROUP * (g + 1))
        ddt = jnp.where(in_group, da_all * a_ref[...] + _heads_to_lanes(dxdt * xs, g), 0.0)
        da_ref[...] += jnp.sum(jnp.where(in_group, da_all * dta, 0.0), axis=0, keepdims=True)
        ddt_raw = ddt * _sigmoid(dt_ref[...] + bias_ref[...])
        ddt_ref[...] += ddt_raw
        dbias_ref[...] += jnp.sum(ddt_raw, axis=0, keepdims=True)

    gn = SSM_GROUPS * SSM_STATE
    wide = pl.BlockSpec((CHUNK, D_INNER), lambda c: (nc - 1 - c, 0))
    st = pl.BlockSpec((CHUNK, gn), lambda c: (nc - 1 - c, 0))
    par = pl.BlockSpec((1, LANES), lambda c: (0, 0))
    dtb = pl.BlockSpec((CHUNK, LANES), lambda c: (nc - 1 - c, 0))
    f = lambda shape: _sds(shape, F32)
    return _pcall(
        body,
        out_shape=[f((s, D_INNER)), f((s, gn)), f((s, gn)),
                   f((s, D_INNER)), f((s, LANES)), f((8, GROUP_DIM)), f((1, LANES)), f((1, LANES)), f((1, LANES))],
        grid=(nc,),
        in_specs=[wide, wide,
                  pl.BlockSpec((CHUNK, gn), lambda c: (nc - 1 - c, D_INNER // gn)),
                  pl.BlockSpec((CHUNK, gn), lambda c: (nc - 1 - c, D_INNER // gn + 1)),
                  wide, wide,
                  pl.BlockSpec((1, SSM_GROUPS, GROUP_DIM, SSM_STATE), lambda c: (nc - 1 - c, 0, 0, 0)),
                  dtb, par, par, par,
                  pl.BlockSpec((1, D_INNER), lambda c: (0, 0))],
        out_specs=[wide, st, st, wide, dtb, pl.BlockSpec((8, GROUP_DIM), lambda c: (0, 0)), par, par, par],
        scratch_shapes=[pltpu.VMEM((SSM_GROUPS, GROUP_DIM, SSM_STATE), F32),
                        pltpu.VMEM((CHUNK, LANES), F32), pltpu.VMEM((CHUNK, LANES), F32),
                        pltpu.VMEM((LANES, CHUNK), F32)],
        args=[dyn, act, act, act, z, y_pre, states, dtp, bias_p, a_p, d_p, normw], name=name, ride=ride)


def _attn_probs(q, kp, kc, sink, n):
    sp = _dot(q, kp, NT)
    sc = _dot(q, kc, NT)
    i = lax.broadcasted_iota(jnp.int32, sp.shape, 0) & (WINDOW - 1)
    j = lax.broadcasted_iota(jnp.int32, sp.shape, 1)
    sp = jnp.where(jnp.logical_and(j > i, n > 0), sp, NEG)
    sc = jnp.where(j <= i, sc, NEG)
    m = jnp.maximum(jnp.maximum(jnp.max(sp, axis=1, keepdims=True), jnp.max(sc, axis=1, keepdims=True)), sink)
    pp = jnp.exp(sp - m)
    pc = jnp.exp(sc - m)
    ps = jnp.exp(sink - m)
    inv = 1.0 / (jnp.sum(pp, axis=1, keepdims=True) + jnp.sum(pc, axis=1, keepdims=True) + ps)
    return pp * inv, pc * inv, ps * inv


def attn_fwd(qt, kt, vt, sink_rows, name, ride=None):
    s = qt.shape[1]
    nb = s // WINDOW
    rows = Q_PER_KV * WINDOW

    def body(q_ref, kp_ref, kc_ref, vp_ref, vc_ref, sk_ref, o_ref):
        n = pl.program_id(0)
        for h in range(N_KV_HEADS):
            heads = pl.ds(h * Q_PER_KV, Q_PER_KV)
            q = q_ref[heads].reshape(rows, ATT_HEAD_DIM)
            pp, pc, _ = _attn_probs(q, kp_ref[h], kc_ref[h], sk_ref[h], n)
            o = _dot(pp.astype(BF16), vp_ref[h]) + _dot(pc.astype(BF16), vc_ref[h])
            o_ref[heads] = o.reshape(Q_PER_KV, WINDOW, ATT_HEAD_DIM).astype(BF16)

    qsp = pl.BlockSpec((N_Q_HEADS, WINDOW, ATT_HEAD_DIM), lambda n: (0, n, 0))
    prev = pl.BlockSpec((N_KV_HEADS, WINDOW, ATT_HEAD_DIM), lambda n: (0, jnp.maximum(n - 1, 0), 0))
    cur = pl.BlockSpec((N_KV_HEADS, WINDOW, ATT_HEAD_DIM), lambda n: (0, n, 0))
    return _pcall(body, out_shape=[_sds(qt.shape, BF16)], grid=(nb,),
                  in_specs=[qsp, prev, cur, prev, cur, pl.BlockSpec((N_KV_HEADS, rows, 1), lambda n: (0, 0, 0))],
                  out_specs=[qsp], args=[qt, kt, kt, vt, vt, sink_rows], name=name, ride=ride)


def attn_bwd(qt, kt, vt, sink_rows, dot_, name, ride=None):
    s = qt.shape[1]
    nb = s // WINDOW
    rows = Q_PER_KV * WINDOW

    def body(q_ref, kp_ref, kc_ref, vp_ref, vc_ref, sk_ref, do_ref, dq_ref, dk_ref, dv_ref, ds_ref, kacc, vacc):
        n = pl.program_id(0)

        @pl.when(n == 0)
        def _():
            kacc[...] = jnp.zeros(kacc.shape, F32)
            vacc[...] = jnp.zeros(vacc.shape, F32)

        @pl.when(n < nb)
        def _():
            for h in range(N_KV_HEADS):
                heads = pl.ds(h * Q_PER_KV, Q_PER_KV)
                q = q_ref[heads].reshape(rows, ATT_HEAD_DIM)
                do = do_ref[heads].reshape(rows, ATT_HEAD_DIM)
                kp, kc, vp, vc = kp_ref[h], kc_ref[h], vp_ref[h], vc_ref[h]
                pp, pc, ps = _attn_probs(q, kp, kc, sk_ref[h], n)
                dpp = _dot(do, vp, NT)
                dpc = _dot(do, vc, NT)
                delta = jnp.sum(pp * dpp, axis=1, keepdims=True) + jnp.sum(pc * dpc, axis=1, keepdims=True)
                dsp = (pp * (dpp - delta)).astype(BF16)
                dsc = (pc * (dpc - delta)).astype(BF16)
                dq = _dot(dsp, kp) + _dot(dsc, kc)
                dq_ref[heads] = dq.reshape(Q_PER_KV, WINDOW, ATT_HEAD_DIM)
                dk_ref[h] = kacc[h] + _dot(dsp, q, TN)
                dv_ref[h] = vacc[h] + _dot(pp.astype(BF16), do, TN)
                kacc[h] = _dot(dsc, q, TN)
                vacc[h] = _dot(pc.astype(BF16), do, TN)
                dsk = -ps * delta
                sub = lax.broadcasted_iota(jnp.int32, (8, LANES), 0)
                tile = jnp.zeros((8, LANES), F32)
                for j in range(Q_PER_KV):
                    tile += jnp.where(sub == j, jnp.sum(dsk[j * WINDOW:(j + 1) * WINDOW, :], axis=0, keepdims=True),
                                      0.0)
                ds_ref[h, 0] = tile

        @pl.when(n == nb)
        def _():
            dk_ref[...] = kacc[...]
            dv_ref[...] = vacc[...]
            ds_ref[...] = jnp.zeros(ds_ref.shape, F32)

    last = nb - 1
    qsp = pl.BlockSpec((N_Q_HEADS, WINDOW, ATT_HEAD_DIM), lambda n: (0, jnp.minimum(n, last), 0))
    prev = pl.BlockSpec((N_KV_HEADS, WINDOW, ATT_HEAD_DIM), lambda n: (0, jnp.clip(n - 1, 0, last), 0))
    cur = pl.BlockSpec((N_KV_HEADS, WINDOW, ATT_HEAD_DIM), lambda n: (0, jnp.minimum(n, last), 0))
    dkv = pl.BlockSpec((N_KV_HEADS, WINDOW, ATT_HEAD_DIM), lambda n: (0, jnp.maximum(n - 1, 0), 0))
    f = lambda shape: _sds(shape, F32)
    acc = pltpu.VMEM((N_KV_HEADS, WINDOW, ATT_HEAD_DIM), F32)
    return _pcall(
        body, out_shape=[f(qt.shape), f(kt.shape), f(vt.shape), f((N_KV_HEADS, nb + 1, 8, LANES))],
        grid=(nb + 1,),
        in_specs=[qsp, prev, cur, prev, cur, pl.BlockSpec((N_KV_HEADS, rows, 1), lambda n: (0, 0, 0)), qsp],
        out_specs=[qsp, dkv, dkv, pl.BlockSpec((N_KV_HEADS, 1, 8, LANES), lambda n: (0, n, 0, 0))],
        scratch_shapes=[acc, acc], args=[qt, kt, kt, vt, vt, sink_rows, dot_], name=name, ride=ride)


def loss_head(x, w, tgt, name):
    s, d = x.shape
    tm = _row_tile(s, 256)

    def body(x_ref, w_ref, t_ref, loss_ref, dx_ref, dw_ref):
        i = pl.program_id(0)
        xf = x_ref[...]
        wv = w_ref[...]
        r = lax.rsqrt(jnp.mean(xf * xf, axis=-1, keepdims=True) + EPS)
        xhat = xf * r
        e = xhat * wv - t_ref[...]
        part = 0.5 * jnp.sum(jnp.mean(e * e, axis=-1, keepdims=True), axis=0, keepdims=True)
        dy = e * (1.0 / d)
        dxhat = dy * wv
        dx_ref[...] = r * (dxhat - xhat * jnp.mean(dxhat * xhat, axis=-1, keepdims=True))
        col = jnp.sum(dy * xhat, axis=0, keepdims=True)

        @pl.when(i == 0)
        def _():
            loss_ref[...] = jnp.broadcast_to(part, (1, LANES))
            dw_ref[...] = col

        @pl.when(i > 0)
        def _():
            loss_ref[...] += jnp.broadcast_to(part, (1, LANES))
            dw_ref[...] += col

    row = pl.BlockSpec((tm, d), lambda i: (i, 0))
    vec = pl.BlockSpec((1, d), lambda i: (0, 0))
    return _pcall(body, out_shape=[_sds((1, LANES), F32), _sds((s, d), F32), _sds((1, d), F32)], grid=(s // tm,),
                  in_specs=[row, vec, row], out_specs=[pl.BlockSpec((1, LANES), lambda i: (0, 0)), row, vec],
                  args=[x, w.reshape(1, d), tgt], name=name)


ELEMWISE_TILE = 720 * 1024


def _tile_rows(r, c, max_elems=262144, mult=16):
    best = None
    for t in range(mult, r + 1, mult):
        if r % t == 0 and t * c <= max_elems:
            best = t
    return best or r


def add_pair(xhs, ps, c_idx, name):
    n = len(xhs)
    _, r, c = xhs[0].shape
    tr = _tile_rows(r, c, max_elems=ELEMWISE_TILE)

    def body(c_ref, *refs):
        for x_ref, p_ref, o_ref in zip(refs[:n], refs[n:2 * n], refs[2 * n:]):
            o_ref[...] = (x_ref[0].astype(F32) + p_ref[...].astype(F32)).astype(BF16)

    blk = pl.BlockSpec((tr, c), lambda i, cr: (i, 0))
    return pl.pallas_call(
        body, out_shape=tuple([_sds((r, c), BF16)] * n),
        grid_spec=pltpu.PrefetchScalarGridSpec(
            num_scalar_prefetch=1, grid=(r // tr,),
            in_specs=[pl.BlockSpec((1, tr, c), lambda i, cr: (cr[0], i, 0))] * n + [blk] * n,
            out_specs=tuple([blk] * n)),
        name=name, compiler_params=_cp(1))(c_idx, *xhs, *ps)


def sum_chips(qs, owns, chip_idx, name):
    n = len(qs)
    _, r, c = qs[0].shape
    tr = _tile_rows(r, c, max_elems=ELEMWISE_TILE // max(1, n - 1))

    def body(k_ref, *refs):
        k = k_ref[0]
        for q_ref, own_ref, o_ref in zip(refs[:n], refs[n:2 * n], refs[2 * n:]):
            mine = own_ref[0].astype(F32)
            tot = None
            for j in range(N_CHIPS):
                term = jnp.where(k == j, mine, q_ref[j].astype(F32))
                tot = term if tot is None else tot + term
            o_ref[...] = tot

    return pl.pallas_call(
        body, out_shape=tuple([_sds((r, c), F32)] * n),
        grid_spec=pltpu.PrefetchScalarGridSpec(
            num_scalar_prefetch=1, grid=(r // tr,),
            in_specs=([pl.BlockSpec((N_CHIPS, tr, c), lambda i, kr: (0, i, 0))] * n
                      + [pl.BlockSpec((1, tr, c), lambda i, kr: (kr[0], i, 0))] * n),
            out_specs=tuple([pl.BlockSpec((tr, c), lambda i, kr: (i, 0))] * n)),
        name=name, compiler_params=_cp(1))(chip_idx, *qs, *owns)


def adamw(w, g, m, v, name):
    r, c = w.shape
    tr = _tile_rows(r, c, mult=8)
    c1 = 1.0 / (1.0 - ADAM_B1 ** ADAM_STEP)
    c2 = 1.0 / (1.0 - ADAM_B2 ** ADAM_STEP)

    def body(w_ref, g_ref, m_ref, v_ref, d_ref, mo_ref, vo_ref):
        gf = g_ref[...]
        mn = ADAM_B1 * m_ref[...] + (1.0 - ADAM_B1) * gf
        vn = ADAM_B2 * v_ref[...] + (1.0 - ADAM_B2) * (gf * gf)
        mo_ref[...] = mn
        vo_ref[...] = vn
        d_ref[...] = -ADAM_LR * ((mn * c1) / (jnp.sqrt(vn * c2) + ADAM_EPS) + ADAM_WD * w_ref[...])

    blk = pl.BlockSpec((tr, c), lambda i: (i, 0))
    out = _sds((r, c), F32)
    return _pcall(body, out_shape=[out, out, out], grid=(r // tr,), in_specs=[blk] * 4, out_specs=[blk] * 3,
                  args=[w, g, m, v], name=name)


WEIGHTS = ['norm_w', 'ffn_w_gate', 'ffn_w_up', 'ffn_w_down', 'ssm_w_in', 'ssm_conv_w', 'ssm_conv_b', 'ssm_dt_bias',
           'ssm_a_log', 'ssm_d', 'ssm_norm_w', 'ssm_w_out', 'kv_norm_w', 'w_k', 'b_k', 'w_v', 'b_v', 'attn_w_q',
           'attn_b_q', 'attn_sinks', 'attn_w_o', 'attn_b_o', 'final_norm_w']
BIG = ['ffn_w_gate', 'ffn_w_up', 'ffn_w_down', 'ssm_w_in', 'ssm_w_out', 'w_k', 'w_v', 'attn_w_q', 'attn_w_o']
TRANSPOSED = ('ffn_w_gate', 'ffn_w_up', 'ssm_w_in')
SMALL = [n for n in WEIGHTS if n not in BIG]
SMALL_SHARDED = {'norm_w': 2, 'ssm_conv_w': 2, 'ssm_conv_b': 1, 'ssm_norm_w': 1}
ROW_ALIGN = 8 * LANES


def _pack_rows(parts):
    flat = jnp.concatenate([p.reshape(-1).astype(F32) for p in parts])
    pad = (-flat.size) % ROW_ALIGN
    return jnp.pad(flat, (0, pad)).reshape(-1, LANES)


def _unpack_rows(buf, shapes):
    flat = buf.reshape(-1)
    out, pos = [], 0
    for shp in shapes:
        size = math.prod(shp)
        out.append(flat[pos:pos + size].reshape(shp))
        pos += size
    return out


def _as2d(a):
    return a.reshape(-1, a.shape[-1])


def _heads_major(t, n_heads):
    s = t.shape[0]
    return t.reshape(s, n_heads, ATT_HEAD_DIM).transpose(1, 0, 2)


def _tokens_major(t):
    h, s, dh = t.shape
    return t.transpose(1, 0, 2).reshape(s, h * dh)


def _pad_lanes(v):
    return jnp.pad(v.reshape(1, -1), ((0, 0), (0, LANES - v.size)))


def _chips_first(t):
    return t.swapaxes(0, 1).reshape((-1,) + t.shape[3:])


def _parts_first(t, rows):
    return t.reshape((N_CHIPS, N_CORES, rows) + t.shape[1:]).swapaxes(0, 1)


def kernel(*args):
    names = (['x'] + WEIGHTS + ['loss_target'] + ['m_' + n for n in WEIGHTS] + ['v_' + n for n in WEIGHTS])
    a = dict(zip(names, args))
    for n in TRANSPOSED:
        for pre in ('', 'm_', 'v_'):
            a[pre + n] = a[pre + n].swapaxes(-1, -2)
    xi, yi, ci = lax.axis_index("x"), lax.axis_index("y"), lax.axis_index("c")
    chip = 2 * xi + yi
    south = ci == 0
    c_idx = jnp.reshape(ci, (1,)).astype(jnp.int32)
    chip_idx = jnp.reshape(chip, (1,)).astype(jnp.int32)
    x0 = a['x'][0]
    s = x0.shape[0]
    cos, sin = rope_tables(s)

    def own_slot(full, mine):
        return lax.dynamic_update_slice_in_dim(full, mine[:, None], chip, axis=1)

    def ffn_shard(l, i):
        return [a[n][l, i].astype(BF16).reshape(N_CORES, FF_PART, D_MODEL)
                for n in ('ffn_w_gate', 'ffn_w_up', 'ffn_w_down')]

    def own_slots(fulls, mines):
        return [own_slot(f, m) for f, m in zip(fulls, mines)]
    small_names = list(SMALL_SHARDED)
    small_sh = _pack_rows([a[n] for n in small_names])
    small_sh = small_sh.reshape(N_CORES, small_sh.shape[0] // 2, LANES)
    sh00, sh01, sh10, sh11 = ffn_shard(0, 0), ffn_shard(0, 1), ffn_shard(1, 0), ffn_shard(1, 1)
    w_in_sh = jnp.pad(a['ssm_w_in'][0], ((0, IN_SHARD_PAD - IN_SHARD), (0, 0))).astype(BF16).reshape(
        N_CORES, IN_SHARD_PAD // 2, D_MODEL)
    w_out_sh = a['ssm_w_out'][0].astype(BF16).reshape(N_CORES, 256, D_MODEL)
    attn_sh = jnp.stack([a['attn_w_q'][0], a['attn_w_o'][0]]).astype(BF16)
    kv_sh = jnp.stack([a['w_k'], a['w_v']]).astype(BF16)
    in_flight, all_started = split_start(
        [sh00 + [small_sh], [w_in_sh, kv_sh], [w_out_sh], sh01, sh10, [attn_sh], sh11], "gather", "gather_start")

    def arrive(idx, after, tag):
        return forward_cores(split_arrive(in_flight[idx], "gather", after, "gather_arrive_" + tag))

    first = run_exchange(arrive(0, all_started, "first"), "gather_hop_first")
    w00 = own_slots(first[:3], sh00)
    smalls = own_slot(first[3], small_sh)
    p = {}
    per_chip = [_unpack_rows(smalls[:, k], [a[n].shape for n in small_names]) for k in range(N_CHIPS)]
    for idx, n in enumerate(small_names):
        p[n] = jnp.concatenate([per_chip[k][idx] for k in range(N_CHIPS)], axis=SMALL_SHARDED[n])
    nw = p['norm_w']
    conv_w, conv_b, ssm_nw = p['ssm_conv_w'][0], p['ssm_conv_b'][0], p['ssm_norm_w'][0].reshape(1, D_INNER)

    h00 = rmsnorm_fwd(x0, nw[0, 0], "norm_in")
    x1, h01, gu00 = ffn_fwd(h00, x0, *w00, [nw[0, 1]], "ffn_fwd_00")
    w_in_g, kv_g = run_exchange(arrive(1, x1, "in"), "gather_hop_in")
    w_in_t = _chips_first(own_slot(w_in_g, w_in_sh)).reshape(N_CHIPS, IN_SHARD_PAD, D_MODEL)[:, :IN_SHARD].reshape(
        IN_PROJ_DIM, D_MODEL)
    w_dt_t = jnp.pad(w_in_t[D_INNER + CONV_DIM:], ((0, LANES - SSM_HEADS), (0, 0)))
    kv_g = own_slot(kv_g, kv_sh)
    w_k, w_v = kv_g[0].reshape(D_MODEL, KV_DIM), kv_g[1].reshape(D_MODEL, KV_DIM)

    zz = mm_nt(h01, w_in_t, "ssm_in_z", n=D_INNER)
    xbc = mm_nt(h01, w_in_t, "ssm_in_xbc", n=CONV_DIM, row0=D_INNER)
    dtp = mm_nt(h01, w_dt_t, "ssm_in_dt")
    act = conv_fwd(xbc, conv_w, conv_b, "ssm_conv")
    bias_p = _pad_lanes(a['ssm_dt_bias'][0])
    a_p = _pad_lanes(-jnp.exp(a['ssm_a_log'][0]))
    d_p = _pad_lanes(a['ssm_d'][0])
    (yn, y_pre, states), (w_out_g,) = ssd_fwd(act, zz, dtp, bias_p, a_p, d_p, ssm_nw, "ssd_fwd",
                                              ride=arrive(2, act, "out"))
    w_out = _chips_first(own_slot(w_out_g, w_out_sh))
    (x2, h02), w01 = mm_res(yn, w_out, x1, "ssm_out", norm_ws=[nw[0, 2]], ride=arrive(3, yn, "01"))
    w01 = own_slots(w01, sh01)
    x3, hkv, h10, gu01 = ffn_fwd(h02, x2, *w01, [a['kv_norm_w'], nw[1, 0]], "ffn_fwd_01")
    w10 = own_slots(run_exchange(arrive(4, x3, "10"), "gather_hop_10"), sh10)

    k_rot = rope_apply(mm_nn(hkv, w_k, "kv_k", bias=a['b_k']), cos, sin, "rope_k")
    v = mm_nn(hkv, w_v, "kv_v", bias=a['b_v'], out_dtype=BF16)
    kt = _heads_major(k_rot, N_KV_HEADS)
    vt = _heads_major(v, N_KV_HEADS)

    (x4, h11, gu10), (attn_g,) = ffn_fwd(h10, x3, *w10, [nw[1, 1]], "ffn_fwd_10", ride=arrive(5, v, "attn"))
    attn_g = own_slot(attn_g, attn_sh)
    w_q, w_o = attn_g[0].reshape(D_MODEL, D_MODEL), attn_g[1].reshape(D_MODEL, D_MODEL)
    scale = 1.0 / math.sqrt(ATT_HEAD_DIM)
    q_rot = rope_apply(mm_nn(h11, w_q, "attn_q", bias=a['attn_b_q'][0]), cos, sin, "rope_q", scale=scale)
    qt = _heads_major(q_rot, N_Q_HEADS)
    sink_rows = jnp.repeat(a['attn_sinks'][0].reshape(N_KV_HEADS, Q_PER_KV), WINDOW, axis=1).reshape(
        N_KV_HEADS, Q_PER_KV * WINDOW, 1)
    (ot,) = attn_fwd(qt, kt, vt, sink_rows, "attn_fwd")
    o = _tokens_major(ot)
    (x5, h12), w11 = mm_res(o, w_o, x4, "attn_out", bias=a['attn_b_o'][0], norm_ws=[nw[1, 2]],
                            ride=arrive(6, ot, "11"))
    w11 = own_slots(w11, sh11)
    x6, gu11 = ffn_fwd(h12, x5, *w11, [], "ffn_fwd_11")

    loss_v, dx6, d_final = loss_head(x6, a['final_norm_w'], a['loss_target'][0], "loss_head")
    g = {'final_norm_w': d_final[0]}

    def same_shape(xs, ys):
        runs = []
        for xv, yv in zip(xs, ys):
            if runs and runs[-1][0][0].shape == xv.shape:
                runs[-1][0].append(xv)
                runs[-1][1].append(yv)
            else:
                runs.append(([xv], [yv]))
        return runs

    def pre_reduce(grads, sib, tag):
        out = []
        for idx, (grp, sbs) in enumerate(same_shape(grads, list(sib))):
            ts = add_pair([gr.reshape(2, -1, gr.shape[-1]) for gr in grp], [_as2d(sb) for sb in sbs], c_idx,
                          "rs_add_%s_%d" % (tag, idx))
            out += [t.reshape(gr.shape[1:]) for t, gr in zip(ts, grp)]
        return out

    def chip_sum(landed, parts, tag):
        out = []
        for idx, (qs, owns) in enumerate(same_shape(list(landed), parts)):
            ts = sum_chips([q.reshape(N_CHIPS, -1, q.shape[-1]) for q in qs],
                           [own.reshape(N_CHIPS, -1, own.shape[-1]) for own in owns], chip_idx,
                           "rs_sum_%s_%d" % (tag, idx))
            out += [t.reshape(q.shape[1:]) for t, q in zip(ts, qs)]
        return out

    dnw = [[None] * 3 for _ in range(2)]
    sums = {}

    def trade(key):
        return swap_cores(sums[key], False)

    dx5, dnw12, *g11 = ffn_bwd(dx6, h12, x5, nw[1, 2], gu11, *w11, "ffn_bwd_11")
    dnw[1][2] = dnw12[0]
    (d_wo, g['attn_b_o']), sib11 = mm_tn(o, dx5, "attn_dwo", col_sum=True, ride=swap_cores(g11, True))
    t11 = pre_reduce(g11, sib11, "11")
    do = mm_nt(dx5, w_o, "attn_do", out_dtype=BF16)
    (dqt, dkt, dvt, dsink), land11 = attn_bwd(qt, kt, vt, sink_rows, _heads_major(do, N_Q_HEADS), "attn_bwd",
                                             ride=scatter_chips(t11[:2]))
    g['attn_sinks'] = jnp.sum(dsink[:, :, :Q_PER_KV, 0], axis=1).reshape(N_Q_HEADS)
    dq_pre = rope_apply(_tokens_major(dqt), cos, sin, "rope_dq", inverse=True, scale=scale, out_dtype=F32)
    d_wq, g['attn_b_q'] = mm_tn(h11, dq_pre, "attn_dwq", col_sum=True)
    g_attn = [jnp.stack([d_wq.reshape(N_CHIPS, 256, D_MODEL), d_wo.reshape(N_CHIPS, 256, D_MODEL)])]
    (dx4, dnw[1][1]), sib_attn = mm_rms_bwd([(dq_pre, 0, w_q, 0, D_MODEL, "nt")], dx5, x4, nw[1, 1], "attn_bwd_dh",
                                            ride=swap_cores(g_attn, True))
    t_attn = pre_reduce(g_attn, sib_attn, "attn")
    (dx3, dnw10, *g10), landed = ffn_bwd(dx4, h10, x3, nw[1, 0], gu10, *w10, "ffn_bwd_10",
                                         ride=scatter_chips(t_attn + t11[2:]))
    dnw[1][0] = dnw10[0]
    sums['attn'] = chip_sum(landed[:1], t_attn, "attn")
    sums['11'] = chip_sum(list(land11) + list(landed[1:]), t11, "11")
    dk_pre = rope_apply(_tokens_major(dkt), cos, sin, "rope_dk", inverse=True, out_dtype=F32)
    dv = _tokens_major(dvt)
    (d_wk, g['b_k']), sib10 = mm_tn(hkv, dk_pre, "kv_dwk", col_sum=True, ride=swap_cores(g10, True))
    t10 = pre_reduce(g10, sib10, "10")
    d_wv, g['b_v'] = mm_tn(hkv, dv, "kv_dwv", col_sum=True)
    g_kv = [jnp.stack([d_wk.reshape(N_CHIPS, 256, KV_DIM), d_wv.reshape(N_CHIPS, 256, KV_DIM)])]
    (dx3, g['kv_norm_w']), sib_kv = mm_rms_bwd(
        [(dk_pre, 0, w_k, 0, KV_DIM, "nt"), (dv, 0, w_v, 0, KV_DIM, "nt")], dx3, x3, a['kv_norm_w'], "kv_bwd_dh",
        ride=swap_cores(g_kv, True))
    t_kv = pre_reduce(g_kv, sib_kv, "kv")
    (dx2, dnw02, *g01), landed = ffn_bwd(dx3, h02, x2, nw[0, 2], gu01, *w01, "ffn_bwd_01",
                                         ride=join(scatter_chips(t10 + t_kv), trade('11'), trade('attn')))
    dnw[0][2] = dnw02[0]
    sums['10'] = chip_sum(landed[:3], t10, "10")
    sums['kv'] = chip_sum(landed[3:4], t_kv, "kv")
    theirs = {'11': landed[4:7], 'attn': landed[7:]}
    d_wout, sib01 = mm_tn(yn, dx2, "ssm_dwout", ride=swap_cores(g01, True))
    t01 = pre_reduce(g01, sib01, "01")
    dyn = mm_nt(dx2, w_out, "ssm_dyn")
    (dxs, db_, dc_, dz, ddt, d_ssm_nw, d_bias, d_a, d_d), landed = ssd_bwd(
        dyn, act, zz, y_pre, states, dtp, bias_p, a_p, d_p, ssm_nw, "ssd_bwd",
        ride=join(scatter_chips(t01[:2]), trade('10'), trade('kv')))
    land01 = list(landed[:2])
    theirs['10'], theirs['kv'] = landed[2:5], landed[5:]
    g['ssm_norm_w'] = d_ssm_nw[:SSM_GROUPS].reshape(D_INNER)
    g['ssm_dt_bias'] = d_bias[0, :SSM_HEADS]
    g['ssm_a_log'] = d_a[0, :SSM_HEADS] * a_p[0, :SSM_HEADS]
    g['ssm_d'] = d_d[0, :SSM_HEADS]
    (dxbc, g['ssm_conv_w'], g['ssm_conv_b']), landed = conv_bwd(dxs, db_, dc_, xbc, conv_w, conv_b, "ssm_conv_bwd",
                                                                ride=scatter_chips(t01[2:]))
    sums['01'] = chip_sum(land01 + list(landed), t01, "01")
    d_win = mm_tn(dz, h01, "ssm_dwz", rows=IN_PROJ_DIM)
    d_win = mm_tn(dxbc, h01, "ssm_dwxbc", into=d_win, rows=IN_PROJ_DIM, row0=D_INNER)
    d_win = mm_tn(ddt, h01, "ssm_dwdt", into=d_win, rows=IN_PROJ_DIM, row0=D_INNER + CONV_DIM, m_valid=SSM_HEADS)
    d_win = jnp.pad(d_win.reshape(N_CHIPS, IN_SHARD, D_MODEL), ((0, 0), (0, IN_SHARD_PAD - IN_SHARD), (0, 0)))
    g_ssm = [_parts_first(d_win.reshape(-1, D_MODEL), IN_SHARD_PAD // 2), _parts_first(d_wout, 256)]
    kb = 1024
    terms = ([(dz, j, w_in_t, j, kb, "nn") for j in range(D_INNER // kb)]
             + [(dxbc, j, w_in_t, D_INNER // kb + j, kb, "nn") for j in range(CONV_DIM // kb)]
             + [(ddt, 0, w_dt_t, 0, LANES, "nn")])
    (dx1, dnw[0][1]), sib_ssm = mm_rms_bwd(terms, dx2, x1, nw[0, 1], "ssm_bwd_dh", ride=swap_cores(g_ssm, True))
    t_ssm = pre_reduce(g_ssm, sib_ssm, "ssm")
    (grad_x, dnw00, *g00), landed = ffn_bwd(dx1, h00, x0, nw[0, 0], gu00, *w00, "ffn_bwd_00",
                                            ride=join(scatter_chips(t_ssm), trade('01')))
    dnw[0][0] = dnw00[0]
    sums['ssm'] = chip_sum(landed[:2], t_ssm, "ssm")
    theirs['01'] = landed[2:]
    landed = run_exchange(join(swap_cores(g00, True), trade('ssm')), "rs_swap_00")
    t00 = pre_reduce(g00, landed[:3], "00")
    theirs['ssm'] = landed[3:]

    def both(key):
        return [(jnp.where(south, m_, t_), jnp.where(south, t_, m_)) for m_, t_ in zip(sums[key], theirs[key])]

    g['norm_w'] = jnp.stack([jnp.stack(r) for r in dnw])
    red = all_reduce_small(_pack_rows([g[n] for n in SMALL] + [loss_v[0, :1]]), "reduce_vectors")

    t00 = lax.optimization_barrier((red, t00))[1]
    (flight00,), flying = split_start([t00], "scatter", "rs_scatter_00_start")

    def held(val):
        return lax.optimization_barrier((flying, val))[1]

    delta, new_m, new_v, gw = {}, {}, {}, {}
    ffn_names = ('ffn_w_gate', 'ffn_w_up', 'ffn_w_down')
    full = {key: both(key) for key in ('attn', 'kv', 'ssm')}
    lo, hi = full['attn'][0]
    gw['attn_w_q'], gw['attn_w_o'] = lo[None], hi[None]
    lo, hi = full['kv'][0]
    gw['w_k'], gw['w_v'] = lo, hi
    lo, hi = full['ssm'][0]
    gw['ssm_w_in'] = jnp.concatenate([lo, hi], axis=0)[:IN_SHARD][None]
    lo, hi = full['ssm'][1]
    gw['ssm_w_out'] = jnp.concatenate([lo, hi], axis=0)[None]

    *small_sums, loss = _unpack_rows(red, [g[n].shape for n in SMALL] + [()])
    for n, t in zip(SMALL, small_sums):
        if n in SMALL_SHARDED:
            ax = SMALL_SHARDED[n] - (a[n].ndim - t.ndim)
            width = a[n].shape[SMALL_SHARDED[n]]
            t = lax.dynamic_slice_in_dim(t, chip * width, width, axis=ax)
        gw[n] = t.reshape(a[n].shape)

    def update(n):
        d, mo, vo = adamw(_as2d(a[n]), held(_as2d(gw[n])), _as2d(a['m_' + n]), _as2d(a['v_' + n]), "adamw_" + n)
        delta[n], new_m[n], new_v[n] = d.reshape(a[n].shape), mo.reshape(a[n].shape), vo.reshape(a[n].shape)

    for n in BIG:
        if n not in ffn_names:
            update(n)
    shapes = [a[n].shape for n in SMALL]
    packed = [_pack_rows([src[n] for n in SMALL]) for src in
              (a, gw, {n: a['m_' + n] for n in SMALL}, {n: a['v_' + n] for n in SMALL})]
    outs = adamw(*packed, "adamw_vectors")
    for dst, buf in zip((delta, new_m, new_v), outs):
        for n, t in zip(SMALL, _unpack_rows(buf, shapes)):
            dst[n] = t
    for key in ('01', '10', '11'):
        sums[key] = held(list(sums[key]))
    rest = [both(key) for key in ('01', '10', '11')]
    done = lax.optimization_barrier((outs[0], [delta[n] for n in BIG if n not in ffn_names], rest))[0]
    land00 = split_arrive(flight00, "scatter", done, "rs_scatter_00_arrive")
    sums['00'] = chip_sum(land00, t00, "00")
    theirs['00'] = run_exchange(trade('00'), "rs_trade_00")
    blocks = [both('00')] + rest
    for t, n in enumerate(ffn_names):
        gw[n] = jnp.concatenate([piece for blk in blocks for piece in blk[t]], axis=0).reshape(a[n].shape)
        update(n)
    for n in TRANSPOSED:
        for dst in (gw, delta, new_m, new_v):
            dst[n] = dst[n].swapaxes(-1, -2)

    return (loss, grad_x[None], *[gw[n] for n in WEIGHTS], *[delta[n] for n in WEIGHTS],
            *[new_m[n] for n in WEIGHTS], *[new_v[n] for n in WEIGHTS])
```

```python
import math

import jax
import jax.numpy as jnp
from jax import lax
from jax.experimental import pallas as pl
from jax.experimental.pallas import tpu as pltpu

F32 = jnp.float32
BF16 = jnp.bfloat16

D_MODEL = 1024
D_INNER = 2048
SSM_HEADS = 32
SSM_GROUPS = 4
HEADS_PER_GROUP = SSM_HEADS // SSM_GROUPS
SSM_HEAD_DIM = 64
SSM_STATE = 128
GROUP_DIM = D_INNER // SSM_GROUPS
CONV_DIM = D_INNER + 2 * SSM_GROUPS * SSM_STATE
CONV_WIDTH = 4
CHUNK = 128
ATT_HEAD_DIM = 64
N_Q_HEADS = 16
N_KV_HEADS = 4
Q_PER_KV = N_Q_HEADS // N_KV_HEADS
KV_DIM = N_KV_HEADS * ATT_HEAD_DIM
WINDOW = 128
ROPE_THETA = 10000.0
D_FF = 2816
N_CHIPS = 4
N_CORES = 2
FF_SHARD = D_FF // N_CHIPS
FF_PART = FF_SHARD // N_CORES
IN_PROJ_DIM = D_INNER + CONV_DIM + SSM_HEADS
IN_SHARD = IN_PROJ_DIM // N_CHIPS
IN_SHARD_PAD = 1312
EPS = 1e-5
NEG = -1e30
LANES = 128
VMEM_LIMIT = 60 * 1024 * 1024

ADAM_LR = 0.001
ADAM_B1 = 0.9
ADAM_B2 = 0.999
ADAM_EPS = 1e-08
ADAM_WD = 0.01
ADAM_STEP = 10

NN = ((1,), (0,))
NT = ((1,), (1,))
TN = ((0,), (0,))
MESH = pl.DeviceIdType.MESH
ANY = pl.BlockSpec(memory_space=pl.ANY)


def _dot(a, b, dims=NN, precision=None):
    return lax.dot_general(a, b, (dims, ((), ())), preferred_element_type=F32, precision=precision)


def _cp(n_grid):
    return pltpu.CompilerParams(dimension_semantics=("arbitrary",) * n_grid, vmem_limit_bytes=VMEM_LIMIT)


def _sigmoid(x):
    return 1.0 / (1.0 + jnp.exp(-x))


def _rms_fwd(xf, w):
    r = lax.rsqrt(jnp.mean(xf * xf, axis=-1, keepdims=True) + EPS)
    return xf * r * w


def _rms_bwd(dh, xf, w):
    r = lax.rsqrt(jnp.mean(xf * xf, axis=-1, keepdims=True) + EPS)
    xhat = xf * r
    dxhat = dh * w
    dx = r * (dxhat - xhat * jnp.mean(dxhat * xhat, axis=-1, keepdims=True))
    return dx, dh * xhat


def _row_tile(s, pref):
    return pref if s % pref == 0 else s


def _col_tile(n):
    for t in (1024, 768, 512, 256, 128):
        if n % t == 0:
            return t
    return n


def _sds(shape, dtype):
    return jax.ShapeDtypeStruct(tuple(shape), dtype)


class Exchange:
    def __init__(self, ins, out_shapes, sems, start, finish, inplace=False):
        self.ins, self.out_shapes, self.sems, self.start, self.finish = ins, out_shapes, sems, start, finish
        self.inplace = inplace


def _place():
    x, y, c = lax.axis_index("x"), lax.axis_index("y"), lax.axis_index("c")
    others = [(1 - x, y), (x, 1 - y), (1 - x, 1 - y)]
    return x, y, c, 2 * x + y, others


def _rc(src, dst, send_sem, recv_sem, dev):
    return pltpu.make_async_remote_copy(src_ref=src, dst_ref=dst, send_sem=send_sem, recv_sem=recv_sem,
                                        device_id=dev, device_id_type=MESH)


def scatter_chips(arrs):
    n = len(arrs)

    def copies(ins, outs, sems):
        send, recv = sems
        x, y, c, k, others = _place()
        out, land = [], []
        for a in range(n):
            for j, (px, py) in enumerate(others):
                out.append(_rc(ins[a].at[2 * px + py], outs[a].at[k], send.at[a, j], recv.at[a, j], (px, py, c)))
                blk = outs[a].at[2 * px + py]
                land.append(_rc(blk, blk, send.at[a, j], recv.at[a, j], (px, py, c)))
        return out, land

    def start(ins, outs, sems):
        for cp in copies(ins, outs, sems)[0]:
            cp.start()

    def finish(ins, outs, sems):
        out, land = copies(ins, outs, sems)
        for arrived in land:
            arrived.wait_recv()
        for cp in out:
            cp.wait_send()

    return Exchange(list(arrs), [_sds(a.shape, a.dtype) for a in arrs],
                    [pltpu.SemaphoreType.DMA((n, 3)), pltpu.SemaphoreType.DMA((n, 3))], start, finish)


def swap_cores(arrs, pick_other):
    n = len(arrs)

    def copies(ins, outs, sems):
        send, recv = sems
        x, y, c, _, _ = _place()
        return [_rc(ins[a].at[1 - c] if pick_other else ins[a], outs[a], send.at[a], recv.at[a], (x, y, 1 - c))
                for a in range(n)]

    def start(ins, outs, sems):
        for cp in copies(ins, outs, sems):
            cp.start()

    def finish(ins, outs, sems):
        for cp in copies(ins, outs, sems):
            cp.wait()

    shapes = [_sds(a.shape[1:] if pick_other else a.shape, a.dtype) for a in arrs]
    return Exchange(list(arrs), shapes, [pltpu.SemaphoreType.DMA((n,)), pltpu.SemaphoreType.DMA((n,))],
                    start, finish)


def join(*parts):
    parts = [p for p in parts if p is not None]
    if not parts:
        return None

    def split(refs, counts):
        out, pos = [], 0
        for cnt in counts:
            out.append(refs[pos:pos + cnt])
            pos += cnt
        return out

    n_in = [len(p.ins) for p in parts]
    n_out = [len(p.out_shapes) for p in parts]
    n_sem = [len(p.sems) for p in parts]

    def run(which):
        def go(ins, outs, sems):
            for p, i, o, s in zip(parts, split(ins, n_in), split(outs, n_out), split(sems, n_sem)):
                getattr(p, which)(i, o, s)
        return go

    return Exchange([a for p in parts for a in p.ins], [s for p in parts for s in p.out_shapes],
                    [s for p in parts for s in p.sems], run("start"), run("finish"))


def _pcall(body, *, out_shape, grid, in_specs, out_specs, args, name, scratch_shapes=(), ride=None, aliases=None):
    out_shape, out_specs, in_specs = tuple(out_shape), tuple(out_specs), list(in_specs)
    aliases = aliases or {}
    if ride is None:
        return pl.pallas_call(body, out_shape=out_shape, grid=grid, in_specs=in_specs, out_specs=out_specs,
                              scratch_shapes=list(scratch_shapes), input_output_aliases=aliases, name=name,
                              compiler_params=_cp(len(grid)))(*args)
    n_in, n_out, n_sc = len(args), len(out_shape), len(scratch_shapes)
    n_xi, n_xo = len(ride.ins), len(ride.out_shapes)

    def wrapped(*refs):
        pos = [0]

        def take(cnt):
            got = refs[pos[0]:pos[0] + cnt]
            pos[0] += cnt
            return got

        c_in, x_in, c_out, x_out, c_sc = take(n_in), take(n_xi), take(n_out), take(n_xo), take(n_sc)
        sems = refs[pos[0]:]
        first, last = True, True
        for d, size in enumerate(grid):
            first = jnp.logical_and(first, pl.program_id(d) == 0)
            last = jnp.logical_and(last, pl.program_id(d) == size - 1)

        @pl.when(first)
        def _():
            ride.start(x_in, x_out, sems)

        body(*c_in, *c_out, *c_sc)

        @pl.when(last)
        def _():
            ride.finish(x_in, x_out, sems)

    if ride.inplace:
        aliases = {**aliases, **{n_in + t: n_out + t for t in range(n_xi)}}
    res = pl.pallas_call(
        wrapped, out_shape=out_shape + tuple(ride.out_shapes), grid=grid,
        in_specs=in_specs + [ANY] * n_xi, out_specs=out_specs + (ANY,) * n_xo,
        scratch_shapes=list(scratch_shapes) + list(ride.sems), input_output_aliases=aliases, name=name,
        compiler_params=_cp(len(grid)))(*args, *ride.ins)
    return res[:n_out], res[n_out:]


def run_exchange(ex, name):
    n_xi, n_xo = len(ex.ins), len(ex.out_shapes)

    def body(*refs):
        ins, outs, sems = refs[:n_xi], refs[n_xi:n_xi + n_xo], refs[n_xi + n_xo:]
        ex.start(ins, outs, sems)
        ex.finish(ins, outs, sems)

    aliases = {t: t for t in range(n_xi)} if ex.inplace else {}
    return pl.pallas_call(body, out_shape=tuple(ex.out_shapes), in_specs=[ANY] * n_xi, out_specs=(ANY,) * n_xo,
                          scratch_shapes=list(ex.sems), input_output_aliases=aliases, name=name)(*ex.ins)


HBM_SPEC = pl.BlockSpec(memory_space=pltpu.HBM)
SEM_SPEC = pl.BlockSpec(memory_space=pltpu.SEMAPHORE)
EFFECT = pltpu.SideEffectType.DATAFLOW_SIDE_EFFECTING


def _route(kind, src, dst, c, k, peer):
    if kind == "gather":
        return src.at[c], dst.at[c, k], dst.at[c, peer]
    return src.at[peer], dst.at[k], dst.at[peer]


def split_start(batches, kind, name):
    flat = [a for batch in batches for a in batch]
    n, nb = len(flat), len(batches)
    lands = [lax.empty((2, N_CHIPS) + a.shape[1:] if kind == "gather" else a.shape, a.dtype) for a in flat]

    def body(*refs):
        srcs, dsts, sems, token = refs[:n], refs[n:2 * n], refs[2 * n:2 * n + 2 * nb], refs[-1]
        x, y, c, k, others = _place()
        pos = 0
        for b, batch in enumerate(batches):
            for a in range(len(batch)):
                for j, (px, py) in enumerate(others):
                    src, dst, _ = _route(kind, srcs[pos], dsts[pos], c, k, 2 * px + py)
                    _rc(src, dst, sems[2 * b].at[3 * a + j], sems[2 * b + 1].at[3 * a + j], (px, py, c)).start()
                pos += 1
        token[...] = jnp.zeros(token.shape, token.dtype)

    sem_shapes = [pltpu.SemaphoreType.DMA((3 * len(batch),)) for batch in batches for _ in range(2)]
    thru = [pltpu.HBM(a.shape, a.dtype) for a in flat] + [pltpu.HBM(l.shape, l.dtype) for l in lands]
    res = pl.pallas_call(
        body, name=name, out_shape=tuple(sem_shapes + thru + [_sds((8, LANES), F32)]),
        in_specs=[HBM_SPEC] * (2 * n),
        out_specs=tuple([SEM_SPEC] * (2 * nb) + [HBM_SPEC] * (2 * n) + [pl.BlockSpec(memory_space=pltpu.VMEM)]),
        input_output_aliases={t: 2 * nb + t for t in range(2 * n)},
        compiler_params=pltpu.CompilerParams(has_side_effects=EFFECT),
    )(*[pltpu.with_memory_space_constraint(t, pltpu.HBM) for t in flat + lands])
    sems, srcs, dsts = res[:2 * nb], res[2 * nb:2 * nb + n], res[2 * nb + n:2 * nb + 2 * n]
    out, pos = [], 0
    for b, batch in enumerate(batches):
        out.append((sems[2 * b], sems[2 * b + 1], list(srcs[pos:pos + len(batch)]), list(dsts[pos:pos + len(batch)])))
        pos += len(batch)
    return out, res[-1]


def split_arrive(handle, kind, after, name):
    send, recv, srcs, dsts = handle
    n = len(srcs)

    def body(*refs):
        s_refs, d_refs, send_ref, recv_ref = refs[:n], refs[n:2 * n], refs[2 * n], refs[2 * n + 1]
        x, y, c, k, others = _place()
        for a in range(n):
            for j, (px, py) in enumerate(others):
                src, _, landed = _route(kind, s_refs[a], d_refs[a], c, k, 2 * px + py)
                cp = _rc(src, landed, send_ref.at[3 * a + j], recv_ref.at[3 * a + j], (px, py, c))
                cp.wait_send()
                cp.wait_recv()

    res = pl.pallas_call(
        body, name=name, out_shape=tuple([pltpu.HBM(t.shape, t.dtype) for t in srcs + dsts]),
        in_specs=[HBM_SPEC] * (2 * n) + [SEM_SPEC, SEM_SPEC, ANY], out_specs=tuple([HBM_SPEC] * (2 * n)),
        input_output_aliases={t: t for t in range(2 * n)},
        compiler_params=pltpu.CompilerParams(has_side_effects=EFFECT),
    )(*srcs, *dsts, send, recv, after)
    return list(res[n:])


def forward_cores(bufs):
    n = len(bufs)

    def copies(outs, sems):
        send, recv = sems
        x, y, c, k, others = _place()
        onward, land = [], []
        for a in range(n):
            for j, (px, py) in enumerate(others):
                blk = outs[a].at[c, 2 * px + py]
                onward.append(_rc(blk, blk, send.at[a, j], recv.at[a, j], (x, y, 1 - c)))
                blk2 = outs[a].at[1 - c, 2 * px + py]
                land.append(_rc(blk2, blk2, send.at[a, j], recv.at[a, j], (x, y, 1 - c)))
        return onward, land

    def start(ins, outs, sems):
        for cp in copies(outs, sems)[0]:
            cp.start()

    def finish(ins, outs, sems):
        onward, land = copies(outs, sems)
        for arrived in land:
            arrived.wait_recv()
        for cp in onward:
            cp.wait_send()

    return Exchange(list(bufs), [_sds(b.shape, b.dtype) for b in bufs],
                    [pltpu.SemaphoreType.DMA((n, 3)), pltpu.SemaphoreType.DMA((n, 3))], start, finish, inplace=True)


def all_reduce_small(buf, name):
    r = buf.shape[0]
    n_dev = 8

    def body(in_ref, o_ref, land, send_sems, recv_sems):
        x, y, c, _, _ = _place()
        me = 4 * x + 2 * y + c
        land[me] = in_ref[...]
        sends = []
        for d in range(1, n_dev):
            peer = (x ^ (d >> 2), y ^ ((d >> 1) & 1), c ^ (d & 1))
            cp = _rc(in_ref, land.at[me], send_sems.at[d], recv_sems.at[d], peer)
            cp.start()
            sends.append(cp)
        for d in range(1, n_dev):
            blk = land.at[me ^ d]
            _rc(blk, blk, send_sems.at[d], recv_sems.at[d], (x, y, c)).wait_recv()
        for cp in sends:
            cp.wait_send()
        tot = land[0]
        for d in range(1, n_dev):
            tot = tot + land[d]
        o_ref[...] = tot

    vm = pl.BlockSpec(memory_space=pltpu.VMEM)
    return pl.pallas_call(
        body, out_shape=_sds(buf.shape, F32), in_specs=[vm], out_specs=vm,
        scratch_shapes=[pltpu.VMEM((n_dev, r, LANES), F32), pltpu.SemaphoreType.DMA((n_dev,)),
                        pltpu.SemaphoreType.DMA((n_dev,))],
        name=name)(buf)


def rmsnorm_fwd(x, w, name):
    s, d = x.shape
    tm = _row_tile(s, 512)

    def body(x_ref, w_ref, o_ref):
        o_ref[...] = _rms_fwd(x_ref[...], w_ref[...]).astype(BF16)

    return _pcall(body, out_shape=[_sds((s, d), BF16)], grid=(s // tm,),
                  in_specs=[pl.BlockSpec((tm, d), lambda i: (i, 0)), pl.BlockSpec((1, d), lambda i: (0, 0))],
                  out_specs=[pl.BlockSpec((tm, d), lambda i: (i, 0))], args=[x, w.reshape(1, d)], name=name)[0]


def _ffn_w_spec(chip_of, single=False):
    mode = dict(pipeline_mode=pl.Buffered(1)) if single else {}
    return pl.BlockSpec((N_CORES, 1, FF_PART, D_MODEL), lambda *ids: (0, chip_of(*ids), 0, 0), **mode)


def ffn_fwd(h, x, wg, wu, wd, norm_ws, name, ride=None):
    s, d = h.shape
    n_norm = len(norm_ws)
    tm = _row_tile(s, 1024)

    def body(*refs):
        h_ref, x_ref, wg_ref, wu_ref, wd_ref = refs[:5]
        nw_refs = refs[5:5 + n_norm]
        o_ref = refs[5 + n_norm]
        h_refs = refs[6 + n_norm:6 + 2 * n_norm]
        gu_ref, acc = refs[6 + 2 * n_norm], refs[7 + 2 * n_norm]
        k = pl.program_id(1)

        @pl.when(k == 0)
        def _():
            acc[...] = jnp.zeros(acc.shape, F32)

        hm = tm // 2
        for part in range(2):
            sub = pl.ds(part * hm, hm)
            hb = h_ref[sub, :]
            g = _dot(hb, wg_ref[...].reshape(FF_SHARD, d), NT)
            u = _dot(hb, wu_ref[...].reshape(FF_SHARD, d), NT)
            gu_ref[0, 0, sub, :] = g.astype(BF16)
            gu_ref[0, 1, sub, :] = u.astype(BF16)
            acc[sub, :] += _dot((g * _sigmoid(g) * u).astype(BF16), wd_ref[...].reshape(FF_SHARD, d))

        @pl.when(k == N_CHIPS - 1)
        def _():
            xn = x_ref[...] + 0.5 * acc[...]
            o_ref[...] = xn
            for nw_ref, hn_ref in zip(nw_refs, h_refs):
                hn_ref[...] = _rms_fwd(xn, nw_ref[...]).astype(BF16)

    row = pl.BlockSpec((tm, d), lambda i, k: (i, 0))
    vec = pl.BlockSpec((1, d), lambda i, k: (0, 0))
    wsp = _ffn_w_spec(lambda i, k: k)
    return _pcall(
        body, out_shape=[_sds((s, d), F32)] + [_sds((s, d), BF16)] * n_norm + [_sds((N_CHIPS, 2, s, FF_SHARD), BF16)],
        grid=(s // tm, N_CHIPS),
        in_specs=[row, row, wsp, wsp, wsp] + [vec] * n_norm,
        out_specs=[row] * (1 + n_norm) + [pl.BlockSpec((1, 2, tm, FF_SHARD), lambda i, k: (k, 0, i, 0))],
        scratch_shapes=[pltpu.VMEM((tm, d), F32)],
        args=[h, x, wg, wu, wd] + [nw.reshape(1, d) for nw in norm_ws], name=name, ride=ride)


def ffn_bwd(dxn, h, x_in, nw, gu, wg, wu, wd, name, ride=None):
    s, d = h.shape
    tm = _row_tile(s, 512)
    ni = s // tm
    last_e = N_CHIPS - 1

    def body(dxn_ref, h_ref, x_ref, nw_ref, gu_ref, wg_ref, wu_ref, wd_ref,
             dx_ref, dnw_ref, dwg_ref, dwu_ref, dwd_ref, dh, wacc):
        e = pl.program_id(0)
        i = pl.program_id(1)
        rows = pl.ds(pl.multiple_of(i * tm, tm), tm)

        @pl.when(i == 0)
        def _():
            wacc[...] = jnp.zeros(wacc.shape, F32)

        @pl.when(e == 0)
        def _():
            dh[rows, :] = jnp.zeros((tm, d), F32)

        hm = tm // 2
        for part in range(2):
            sub = pl.ds(part * hm, hm)
            dxb = dxn_ref[sub, :].astype(BF16)
            hb = h_ref[sub, :]
            g = gu_ref[0, 0, sub, :].astype(F32)
            u = gu_ref[0, 1, sub, :].astype(F32)
            drows = pl.ds(pl.multiple_of(i * tm + part * hm, hm), hm)
            sg = _sigmoid(g)
            silu = g * sg
            wacc[2] += _dot((0.5 * silu * u).astype(BF16), dxb, TN)
            da = 0.5 * _dot(dxb, wd_ref[...].reshape(FF_SHARD, d), NT)
            dg = (da * u * (sg * (1.0 + g * (1.0 - sg)))).astype(BF16)
            wacc[0] += _dot(dg, hb, TN)
            du = (da * silu).astype(BF16)
            dh[drows, :] += _dot(dg, wg_ref[...].reshape(FF_SHARD, d))
            wacc[1] += _dot(du, hb, TN)
            dh[drows, :] += _dot(du, wu_ref[...].reshape(FF_SHARD, d))

        @pl.when(i == ni - 1)
        def _():
            for t, dw_ref in enumerate((dwg_ref, dwu_ref, dwd_ref)):
                dw_ref[...] = wacc[t].astype(BF16).reshape(N_CORES, 1, FF_PART, d)

        @pl.when(e == last_e)
        def _():
            dx, dnw = _rms_bwd(dh[rows, :], x_ref[...], nw_ref[...])
            dx_ref[...] = dxn_ref[...] + dx
            col = jnp.sum(dnw, axis=0, keepdims=True)

            @pl.when(i == 0)
            def _():
                dnw_ref[...] = col

            @pl.when(i > 0)
            def _():
                dnw_ref[...] += col

    row = pl.BlockSpec((tm, d), lambda e, i: (i, 0))
    late = pl.BlockSpec((tm, d), lambda e, i: (jnp.where(e == last_e, i, 0), 0))
    vec = pl.BlockSpec((1, d), lambda e, i: (0, 0))
    wsp = _ffn_w_spec(lambda e, i: e)
    dwsp = _ffn_w_spec(lambda e, i: e)
    dw = _sds((N_CORES, N_CHIPS, FF_PART, d), BF16)
    return _pcall(
        body, out_shape=[_sds((s, d), F32), _sds((1, d), F32), dw, dw, dw],
        grid=(N_CHIPS, ni),
        in_specs=[row, row, late, vec, pl.BlockSpec((1, 2, tm, FF_SHARD), lambda e, i: (e, 0, i, 0)), wsp, wsp, wsp],
        out_specs=[late, vec, dwsp, dwsp, dwsp],
        scratch_shapes=[pltpu.VMEM((s, d), F32), pltpu.VMEM((3, FF_SHARD, d), F32)],
        args=[dxn, h, x_in, nw.reshape(1, d), gu, wg, wu, wd], name=name, ride=ride)


def mm_res(a, w, x, name, bias=None, norm_ws=(), ride=None):
    s, k = a.shape
    n = w.shape[1]
    tm = _row_tile(s, 512)
    has_bias = bias is not None
    n_norm = len(norm_ws)

    def body(*refs):
        a_ref, w_ref, x_ref = refs[:3]
        pos = 3
        t = _dot(a_ref[...], w_ref[...])
        if has_bias:
            t = t + refs[pos][...]
            pos += 1
        nw_refs = refs[pos:pos + n_norm]
        o_ref = refs[pos + n_norm]
        h_refs = refs[pos + n_norm + 1:]
        xn = x_ref[...] + t
        o_ref[...] = xn
        for nw_ref, h_ref in zip(nw_refs, h_refs):
            h_ref[...] = _rms_fwd(xn, nw_ref[...]).astype(BF16)

    row = pl.BlockSpec((tm, n), lambda i: (i, 0))
    vec = pl.BlockSpec((1, n), lambda i: (0, 0))
    in_specs = [pl.BlockSpec((tm, k), lambda i: (i, 0)), pl.BlockSpec((k, n), lambda i: (0, 0)), row]
    args = [a, w, x]
    if has_bias:
        in_specs.append(vec)
        args.append(bias.reshape(1, n))
    for nw in norm_ws:
        in_specs.append(vec)
        args.append(nw.reshape(1, n))
    return _pcall(body, out_shape=[_sds((s, n), F32)] + [_sds((s, n), BF16)] * n_norm, grid=(s // tm,),
                  in_specs=in_specs, out_specs=[row] * (1 + n_norm), args=args, name=name, ride=ride)


def mm_nn(a, w, name, bias=None, out_dtype=F32):
    s, k = a.shape
    n = w.shape[1]
    tm = _row_tile(s, 512)
    tn = _col_tile(n)
    has_bias = bias is not None

    def body(*refs):
        a_ref, w_ref = refs[:2]
        o_ref = refs[-1]
        t = _dot(a_ref[...], w_ref[...])
        if has_bias:
            t = t + refs[2][...]
        o_ref[...] = t.astype(out_dtype)

    in_specs = [pl.BlockSpec((tm, k), lambda j, i: (i, 0)), pl.BlockSpec((k, tn), lambda j, i: (0, j))]
    args = [a, w]
    if has_bias:
        in_specs.append(pl.BlockSpec((1, tn), lambda j, i: (0, j)))
        args.append(bias.reshape(1, n))
    return _pcall(body, out_shape=[_sds((s, n), out_dtype)], grid=(n // tn, s // tm), in_specs=in_specs,
                  out_specs=[pl.BlockSpec((tm, tn), lambda j, i: (i, j))], args=args, name=name)[0]


def mm_nt(a, w, name, n=None, row0=0, out_dtype=F32, ride=None):
    s, k = a.shape
    n = w.shape[0] if n is None else n
    tm = _row_tile(s, 512)
    tn = _col_tile(n)
    base = row0 // tn
    assert row0 % tn == 0

    def body(a_ref, w_ref, o_ref):
        o_ref[...] = _dot(a_ref[...].astype(BF16), w_ref[...], NT).astype(out_dtype)

    res = _pcall(body, out_shape=[_sds((s, n), out_dtype)], grid=(n // tn, s // tm),
                 in_specs=[pl.BlockSpec((tm, k), lambda j, i: (i, 0)), pl.BlockSpec((tn, k), lambda j, i: (base + j, 0))],
                 out_specs=[pl.BlockSpec((tm, tn), lambda j, i: (i, j))], args=[a, w], name=name, ride=ride)
    return res[0] if ride is None else (res[0][0], res[1])


def mm_tn(a, b, name, into=None, rows=None, row0=0, m_valid=None, col_sum=False, ride=None):
    s, m = a.shape
    n = b.shape[1]
    mv = m if m_valid is None else m_valid
    tm = _col_tile(m) if m_valid is None else mv
    tn = 512 if n % 512 == 0 else n
    rows = mv if rows is None else rows
    assert row0 % tm == 0 and (m_valid is None or m == LANES)
    assert not col_sum or mv == tm
    base = row0 // tm
    ta = m if m_valid is not None else tm

    def body(*refs):
        a_ref, b_ref = refs[0], refs[1]
        o_ref = refs[-2] if col_sum else refs[-1]
        bf = b_ref[...]
        t = _dot(a_ref[...].astype(BF16), bf.astype(BF16), TN)
        o_ref[...] = t[:tm].astype(BF16)
        if col_sum:
            refs[-1][...] = jnp.sum(bf.astype(F32), axis=0, keepdims=True)

    in_specs = [pl.BlockSpec((s, ta), lambda i, j: (0, i)), pl.BlockSpec((s, tn), lambda i, j: (0, j))]
    args = [a, b]
    aliases = None
    if into is not None:
        in_specs.append(ANY)
        args.append(into)
        aliases = {2: 0}
    out_shape = [_sds((rows, n), BF16)]
    out_specs = [pl.BlockSpec((tm, tn), lambda i, j: (base + i, j))]
    if col_sum:
        out_shape.append(_sds((1, n), F32))
        out_specs.append(pl.BlockSpec((1, tn), lambda i, j: (0, j)))
    res = _pcall(body, out_shape=out_shape, grid=(mv // tm, n // tn), in_specs=in_specs, out_specs=out_specs,
                 args=args, name=name, ride=ride, aliases=aliases)
    outs = res if ride is None else res[0]
    out = (outs[0], outs[1][0]) if col_sum else outs[0]
    return out if ride is None else (out, res[1])


def mm_rms_bwd(terms, dxn, x, nw, name, ride=None):
    s, n = x.shape
    nt_ = len(terms)
    tm = _row_tile(s, 512 if nt_ <= 2 else 256)
    forms = [t[5] for t in terms]

    def body(*refs):
        dxn_ref, x_ref, nw_ref, dx_ref, dnw_ref = refs[2 * nt_:]
        i = pl.program_id(0)
        dh = None
        for t in range(nt_):
            part = _dot(refs[2 * t][...].astype(BF16), refs[2 * t + 1][...], NN if forms[t] == "nn" else NT)
            dh = part if dh is None else dh + part
        dx, dnw = _rms_bwd(dh, x_ref[...], nw_ref[...])
        dx_ref[...] = dxn_ref[...] + dx
        col = jnp.sum(dnw, axis=0, keepdims=True)

        @pl.when(i == 0)
        def _():
            dnw_ref[...] = col

        @pl.when(i > 0)
        def _():
            dnw_ref[...] += col

    in_specs, args = [], []
    for a, cb, w, rb, kb, form in terms:
        in_specs.append(pl.BlockSpec((tm, kb), lambda i, cb=cb: (i, cb)))
        if form == "nn":
            in_specs.append(pl.BlockSpec((kb, n), lambda i, rb=rb: (rb, 0)))
        else:
            in_specs.append(pl.BlockSpec((n, kb), lambda i, rb=rb: (0, rb)))
        args += [a, w]
    row = pl.BlockSpec((tm, n), lambda i: (i, 0))
    vec = pl.BlockSpec((1, n), lambda i: (0, 0))
    res = _pcall(body, out_shape=[_sds((s, n), F32), _sds((1, n), F32)], grid=(s // tm,),
                 in_specs=in_specs + [row, row, vec], out_specs=[row, vec],
                 args=args + [dxn, x, nw.reshape(1, n)], name=name, ride=ride)
    outs = res if ride is None else res[0]
    out = (outs[0], outs[1][0])
    return out if ride is None else (out, res[1])


def rope_tables(s):
    pos = jnp.arange(s, dtype=F32)
    inv = 1.0 / (ROPE_THETA ** (jnp.arange(0, ATT_HEAD_DIM, 2, dtype=F32) / ATT_HEAD_DIM))
    ang = pos[:, None] * inv[None, :]
    cos = jnp.tile(jnp.cos(ang), (1, 2 * LANES // ATT_HEAD_DIM))
    sin = jnp.tile(jnp.sin(ang), (1, 2 * LANES // ATT_HEAD_DIM))
    return cos, sin


def rope_apply(t, cos, sin, name, inverse=False, scale=1.0, out_dtype=BF16):
    s, n = t.shape
    tm = _row_tile(s, 512)
    half = ATT_HEAD_DIM // 2
    reps = n // LANES

    def body(t_ref, c_ref, s_ref, o_ref):
        tf = t_ref[...].astype(F32)
        c = jnp.tile(c_ref[...], (1, reps))
        sn = jnp.tile(s_ref[...], (1, reps))
        lane = lax.broadcasted_iota(jnp.int32, tf.shape, 1)
        first = (lane & (ATT_HEAD_DIM - 1)) < half
        rot = jnp.where(first, -pltpu.roll(tf, n - half, 1), pltpu.roll(tf, half, 1))
        sign = -1.0 if inverse else 1.0
        o_ref[...] = (scale * (tf * c + sign * rot * sn)).astype(out_dtype)

    tab = pl.BlockSpec((tm, LANES), lambda i: (i, 0))
    return _pcall(body, out_shape=[_sds((s, n), out_dtype)], grid=(s // tm,),
                  in_specs=[pl.BlockSpec((tm, n), lambda i: (i, 0)), tab, tab],
                  out_specs=[pl.BlockSpec((tm, n), lambda i: (i, 0))], args=[t, cos, sin], name=name)[0]


CONV_TILE = 256


def _shift_down(u, k):
    if k == 0:
        return u
    row = lax.broadcasted_iota(jnp.int32, u.shape, 0)
    return jnp.where(row >= k, pltpu.roll(u, k, 0), 0.0)


def _shift_up(u, k):
    if k == 0:
        return u
    s = u.shape[0]
    row = lax.broadcasted_iota(jnp.int32, u.shape, 0)
    return jnp.where(row < s - k, pltpu.roll(u, s - k, 0), 0.0)


def _conv_taps(u):
    return [_shift_down(u, CONV_WIDTH - 1 - k) for k in range(CONV_WIDTH)]


def _conv_pre(taps, w_ref, b_ref):
    pre = b_ref[...] + w_ref[0:1, :] * taps[0]
    for k in range(1, CONV_WIDTH):
        pre += w_ref[k:k + 1, :] * taps[k]
    return pre


def conv_fwd(u, w, b, name, ride=None):
    s, c = u.shape

    def body(u_ref, w_ref, b_ref, o_ref):
        pre = _conv_pre(_conv_taps(u_ref[...]), w_ref, b_ref)
        o_ref[...] = pre * _sigmoid(pre)

    col = pl.BlockSpec((s, CONV_TILE), lambda j: (0, j))
    res = _pcall(body, out_shape=[_sds((s, c), F32)], grid=(c // CONV_TILE,),
                 in_specs=[col, pl.BlockSpec((CONV_WIDTH, CONV_TILE), lambda j: (0, j)),
                           pl.BlockSpec((1, CONV_TILE), lambda j: (0, j))],
                 out_specs=[col], args=[u, w, b.reshape(1, c)], name=name, ride=ride)
    return res[0] if ride is None else (res[0][0], res[1])


def conv_bwd(dxs, db_, dc_, u, w, b, name, ride=None):
    s, c = u.shape
    n_x = dxs.shape[1] // CONV_TILE
    n_b = db_.shape[1] // CONV_TILE

    def body(dx_ref, dbb_ref, dcc_ref, u_ref, w_ref, b_ref, du_ref, dw_ref, dbias_ref):
        j = pl.program_id(0)
        dact = jnp.where(j < n_x, dx_ref[...], jnp.where(j < n_x + n_b, dbb_ref[...], dcc_ref[...]))
        taps = _conv_taps(u_ref[...])
        pre = _conv_pre(taps, w_ref, b_ref)
        sg = _sigmoid(pre)
        dpre = dact * (sg * (1.0 + pre * (1.0 - sg)))
        du = w_ref[CONV_WIDTH - 1:CONV_WIDTH, :] * dpre
        for k in range(CONV_WIDTH - 1):
            du += w_ref[k:k + 1, :] * _shift_up(dpre, CONV_WIDTH - 1 - k)
        du_ref[...] = du
        dbias_ref[...] = jnp.sum(dpre, axis=0, keepdims=True)
        for k in range(CONV_WIDTH):
            dw_ref[k:k + 1, :] = jnp.sum(dpre * taps[k], axis=0, keepdims=True)

    col = pl.BlockSpec((s, CONV_TILE), lambda j: (0, j))
    wsp = pl.BlockSpec((CONV_WIDTH, CONV_TILE), lambda j: (0, j))
    bsp = pl.BlockSpec((1, CONV_TILE), lambda j: (0, j))
    res = _pcall(
        body, out_shape=[_sds((s, c), F32), _sds((CONV_WIDTH, c), F32), _sds((1, c), F32)], grid=(c // CONV_TILE,),
        in_specs=[pl.BlockSpec((s, CONV_TILE), lambda j: (0, jnp.minimum(j, n_x - 1))),
                  pl.BlockSpec((s, CONV_TILE), lambda j: (0, jnp.clip(j - n_x, 0, n_b - 1))),
                  pl.BlockSpec((s, CONV_TILE), lambda j: (0, jnp.clip(j - n_x - n_b, 0, n_b - 1))),
                  col, wsp, bsp],
        out_specs=[col, wsp, bsp], args=[dxs, db_, dc_, u, w, b.reshape(1, c)], name=name, ride=ride)
    (du, dw, db), rode = res if ride is not None else (res, None)
    return (du, dw, db[0]) if ride is None else ((du, dw, db[0]), rode)


def _lane_pick(mat, idx):
    lane = lax.broadcasted_iota(jnp.int32, mat.shape, 1)
    return jnp.sum(jnp.where(lane == idx, mat, 0.0), axis=1, keepdims=True)


def _sub_pick(mat, idx):
    sub = lax.broadcasted_iota(jnp.int32, mat.shape, 0)
    return jnp.sum(jnp.where(sub == idx, mat, 0.0), axis=0, keepdims=True)


def _expand_heads(cols):
    rows = cols[0].shape[0]
    left = lax.broadcasted_iota(jnp.int32, (rows, LANES), 1) < SSM_HEAD_DIM
    return jnp.concatenate(
        [jnp.where(left, cols[2 * p], cols[2 * p + 1]) for p in range(HEADS_PER_GROUP // 2)], axis=1)


def _dot_01(x, ones, ones_first, pieces):
    tot, rest = None, x
    for _ in range(pieces):
        piece = rest.astype(BF16)
        rest = rest - piece.astype(F32)
        part = _dot(ones, piece) if ones_first else _dot(piece, ones)
        tot = part if tot is None else tot + part
    return tot


def _heads_to_lanes(mat, g):
    jj = lax.broadcasted_iota(jnp.int32, (GROUP_DIM, LANES), 0)
    ll = lax.broadcasted_iota(jnp.int32, (GROUP_DIM, LANES), 1)
    sel = (ll == HEADS_PER_GROUP * g + (jj >> 6)).astype(BF16)
    return _dot_01(mat, sel, False, 3)


def _softplus(x):
    return jnp.maximum(x, 0.0) + jnp.log1p(jnp.exp(-jnp.abs(x)))


def _ssd_scalars(dt_ref, bias_ref, a_ref, dtall, csall, cst):
    dta = _softplus(dt_ref[...] + bias_ref[...])
    row = lax.broadcasted_iota(jnp.int32, (CHUNK, CHUNK), 0)
    col = lax.broadcasted_iota(jnp.int32, (CHUNK, CHUNK), 1)
    cs = _dot_01(dta * a_ref[...], (row >= col).astype(BF16), True, 3)
    dtall[...] = dta
    csall[...] = cs
    cst[...] = cs.T


def _decay_mat(cs_col, cs_row):
    row = lax.broadcasted_iota(jnp.int32, (CHUNK, CHUNK), 0)
    col = lax.broadcasted_iota(jnp.int32, (CHUNK, CHUNK), 1)
    return jnp.exp(jnp.where(row >= col, cs_col - cs_row, NEG))


def _head_mask(xpair, right):
    lane = lax.broadcasted_iota(jnp.int32, xpair.shape, 1)
    keep = (lane >= SSM_HEAD_DIM) if right else (lane < SSM_HEAD_DIM)
    return jnp.where(keep, xpair, 0.0)


def _chunk_cols(x_all, g):
    return [_lane_pick(x_all, HEADS_PER_GROUP * g + r) for r in range(HEADS_PER_GROUP)]


def _decay_col(cs_cols):
    return jnp.concatenate(
        [jnp.broadcast_to(jnp.exp(cc[CHUNK - 1:CHUNK, :]), (SSM_HEAD_DIM, 1)) for cc in cs_cols], axis=0)


def ssd_fwd(act, z, dtp, bias_p, a_p, d_p, normw, name, ride=None):
    s = act.shape[0]
    nc = s // CHUNK

    def body(xs_all, b_all, c_all, z_all, dt_ref, bias_ref, a_ref, d_ref, nw_all,
             yn_all, y_all, st_all, state, dtall, csall, cst):
        _ssd_scalars(dt_ref, bias_ref, a_ref, dtall, csall, cst)

        @pl.when(pl.program_id(0) == 0)
        def _():
            state[...] = jnp.zeros(state.shape, F32)

        for g in range(SSM_GROUPS):
            wide = pl.ds(g * GROUP_DIM, GROUP_DIM)
            narrow = pl.ds(g * SSM_STATE, SSM_STATE)
            group(g, xs_all.at[:, wide], b_all.at[:, narrow], c_all.at[:, narrow], z_all.at[:, wide], d_ref,
                  nw_all.at[:, wide], yn_all.at[:, wide], y_all.at[:, wide], st_all.at[:, pl.ds(g, 1)],
                  state, dtall, csall, cst)

    def group(g, xs_ref, b_ref, c_ref, z_ref, d_ref, nw_ref, yn_ref, y_ref, st_ref, state, dtall, csall, cst):
        cs_cols = _chunk_cols(csall[...], g)
        dt_cols = _chunk_cols(dtall[...], g)
        cs_rows = [_sub_pick(cst[...], HEADS_PER_GROUP * g + r) for r in range(HEADS_PER_GROUP)]
        d_cols = _chunk_cols(d_ref[...], g)
        cs_exp = _expand_heads(cs_cols)
        dt_exp = _expand_heads(dt_cols)
        d_exp = _expand_heads(d_cols)
        xs = xs_ref[...]
        bb = b_ref[...].astype(BF16)
        cb16 = c_ref[...].astype(BF16)
        xdt = xs * dt_exp
        s_prev = state[g]
        st_ref[0, 0] = s_prev
        y_off = _dot(cb16, s_prev.astype(BF16), NT) * jnp.exp(cs_exp)
        decay_st = jnp.exp(cs_exp[CHUNK - 1:CHUNK, :] - cs_exp)
        contrib = _dot((xdt * decay_st).astype(BF16), bb, TN)
        state[g] = _decay_col(cs_cols) * s_prev + contrib
        cbm = _dot(cb16, bb, NT)
        pairs = []
        for p in range(HEADS_PER_GROUP // 2):
            xpair = xdt[:, LANES * p:LANES * (p + 1)]
            m0 = (cbm * _decay_mat(cs_cols[2 * p], cs_rows[2 * p])).astype(BF16)
            m1 = (cbm * _decay_mat(cs_cols[2 * p + 1], cs_rows[2 * p + 1])).astype(BF16)
            pairs.append(_dot(m0, _head_mask(xpair, False).astype(BF16))
                         + _dot(m1, _head_mask(xpair, True).astype(BF16)))
        y = jnp.concatenate(pairs, axis=1) + y_off + xs * d_exp
        y_ref[...] = y
        zf = z_ref[...]
        yg = y * (zf * _sigmoid(zf))
        yn_ref[...] = _rms_fwd(yg, nw_ref[...]).astype(BF16)

    gn = SSM_GROUPS * SSM_STATE
    wide = pl.BlockSpec((CHUNK, D_INNER), lambda c: (c, 0))
    par = pl.BlockSpec((1, LANES), lambda c: (0, 0))
    return _pcall(
        body,
        out_shape=[_sds((s, D_INNER), BF16), _sds((s, D_INNER), F32),
                   _sds((nc, SSM_GROUPS, GROUP_DIM, SSM_STATE), F32)],
        grid=(nc,),
        in_specs=[wide,
                  pl.BlockSpec((CHUNK, gn), lambda c: (c, D_INNER // gn)),
                  pl.BlockSpec((CHUNK, gn), lambda c: (c, D_INNER // gn + 1)),
                  wide,
                  pl.BlockSpec((CHUNK, LANES), lambda c: (c, 0)),
                  par, par, par,
                  pl.BlockSpec((1, D_INNER), lambda c: (0, 0))],
        out_specs=[wide, wide, pl.BlockSpec((1, SSM_GROUPS, GROUP_DIM, SSM_STATE), lambda c: (c, 0, 0, 0))],
        scratch_shapes=[pltpu.VMEM((SSM_GROUPS, GROUP_DIM, SSM_STATE), F32),
                        pltpu.VMEM((CHUNK, LANES), F32), pltpu.VMEM((CHUNK, LANES), F32),
                        pltpu.VMEM((LANES, CHUNK), F32)],
        args=[act, act, act, z, dtp, bias_p, a_p, d_p, normw], name=name, ride=ride)


def ssd_bwd(dyn, act, z, y_pre, states, dtp, bias_p, a_p, d_p, normw, name, ride=None):
    s = act.shape[0]
    nc = s // CHUNK

    def body(dyn_all, xs_all, b_all, c_all, z_all, y_all, st_all, dt_ref, bias_ref, a_ref, d_ref, nw_all,
             dxs_all, db_all, dc_all, dz_all, ddt_ref, dnw_ref, dbias_ref, da_ref, dd_ref,
             dstate, dtall, csall, cst):
        _ssd_scalars(dt_ref, bias_ref, a_ref, dtall, csall, cst)
        ddt_ref[...] = jnp.zeros((CHUNK, LANES), F32)

        @pl.when(pl.program_id(0) == 0)
        def _():
            dstate[...] = jnp.zeros(dstate.shape, F32)
            dnw_ref[...] = jnp.zeros(dnw_ref.shape, F32)
            dbias_ref[...] = jnp.zeros((1, LANES), F32)
            da_ref[...] = jnp.zeros((1, LANES), F32)
            dd_ref[...] = jnp.zeros((1, LANES), F32)

        for g in range(SSM_GROUPS):
            wide = pl.ds(g * GROUP_DIM, GROUP_DIM)
            narrow = pl.ds(g * SSM_STATE, SSM_STATE)
            group(g, dyn_all.at[:, wide], xs_all.at[:, wide], b_all.at[:, narrow], c_all.at[:, narrow],
                  z_all.at[:, wide], y_all.at[:, wide], st_all.at[:, pl.ds(g, 1)], dt_ref, bias_ref, a_ref, d_ref,
                  nw_all.at[:, wide], dxs_all.at[:, wide], db_all.at[:, narrow], dc_all.at[:, narrow],
                  dz_all.at[:, wide], ddt_ref, dnw_ref, dbias_ref, da_ref, dd_ref, dstate, dtall, csall, cst)

    def group(g, dyn_ref, xs_ref, b_ref, c_ref, z_ref, y_ref, st_ref, dt_ref, bias_ref, a_ref, d_ref, nw_ref,
              dxs_ref, db_ref, dc_ref, dz_ref, ddt_ref, dnw_ref, dbias_ref, da_ref, dd_ref,
              dstate, dtall, csall, cst):
        cs_cols = _chunk_cols(csall[...], g)
        dt_cols = _chunk_cols(dtall[...], g)
        cs_rows = [_sub_pick(cst[...], HEADS_PER_GROUP * g + r) for r in range(HEADS_PER_GROUP)]
        d_cols = _chunk_cols(d_ref[...], g)
        cs_exp = _expand_heads(cs_cols)
        dt_exp = _expand_heads(dt_cols)
        d_exp = _expand_heads(d_cols)
        xs = xs_ref[...]
        bb = b_ref[...].astype(BF16)
        cb16 = c_ref[...].astype(BF16)
        xdt = xs * dt_exp
        s_prev = st_ref[0, 0]
        s_prev16 = s_prev.astype(BF16)
        ds_next = dstate[g]
        ds16 = ds_next.astype(BF16)

        zf = z_ref[...]
        sz = _sigmoid(zf)
        silu_z = zf * sz
        y = y_ref[...]
        yg = y * silu_z
        dout = dyn_ref[...]
        dyg, dnw = _rms_bwd(dout, yg, nw_ref[...])
        dnw_ref[pl.ds(g, 1), :] += jnp.sum(dnw, axis=0, keepdims=True)
        dy = dyg * silu_z
        dz_ref[...] = dyg * y * (sz * (1.0 + zf * (1.0 - sz)))
        dd_ref[...] += jnp.sum(_heads_to_lanes(dy * xs, g), axis=0, keepdims=True)

        exp_cs = jnp.exp(cs_exp)
        decay_st = jnp.exp(cs_exp[CHUNK - 1:CHUNK, :] - cs_exp)
        cs_t = _dot(cb16, s_prev16, NT)
        dyo = dy * exp_cs
        dc_acc = _dot(dyo.astype(BF16), s_prev16, NN)
        g1 = _dot(bb, ds16, NT)
        xds = xdt * decay_st
        db_acc = _dot(xds.astype(BF16), ds16, NN)
        dxdt_off = g1 * decay_st
        t_exp = g1 * xds
        dcs_exp = dy * cs_t * exp_cs - t_exp
        decay_c = _decay_col(cs_cols)
        dstate[g] = decay_c * ds_next + _dot(dyo.astype(BF16), cb16, TN)
        dlast_col = jnp.sum(ds_next * s_prev, axis=1, keepdims=True) * decay_c
        jj = lax.broadcasted_iota(jnp.int32, (GROUP_DIM, LANES), 0)
        ll = lax.broadcasted_iota(jnp.int32, (GROUP_DIM, LANES), 1)
        sel = ll == HEADS_PER_GROUP * g + (jj >> 6)
        dlast = jnp.sum(jnp.where(sel, dlast_col, 0.0), axis=0, keepdims=True)
        t_all = _heads_to_lanes(t_exp, g)
        dlast += jnp.sum(t_all, axis=0, keepdims=True)
        dcs_all = _heads_to_lanes(dcs_exp, g)

        cbm = _dot(cb16, bb, NT)
        dcb = jnp.zeros((CHUNK, CHUNK), F32)
        dcs_rows = jnp.zeros((LANES, CHUNK), F32)
        lane_l = lax.broadcasted_iota(jnp.int32, (CHUNK, LANES), 1)
        sub_l = lax.broadcasted_iota(jnp.int32, (LANES, CHUNK), 0)
        dxdt_pairs = []
        for p in range(HEADS_PER_GROUP // 2):
            xpair16 = xdt[:, LANES * p:LANES * (p + 1)].astype(BF16)
            dypair = dy[:, LANES * p:LANES * (p + 1)]
            acc = None
            for r in (2 * p, 2 * p + 1):
                lm = _decay_mat(cs_cols[r], cs_rows[r])
                m = cbm * lm
                dyh = _head_mask(dypair, r % 2 == 1).astype(BF16)
                dm = _dot(dyh, xpair16, NT)
                dcb += dm * lm
                q = dm * m
                idx = HEADS_PER_GROUP * g + r
                dcs_all += jnp.where(lane_l == idx, jnp.sum(q, axis=1, keepdims=True), 0.0)
                dcs_rows -= jnp.where(sub_l == idx, jnp.sum(q, axis=0, keepdims=True), 0.0)
                part = _dot(m.astype(BF16), dyh, TN)
                acc = part if acc is None else acc + part
            dxdt_pairs.append(acc)
        dxdt = jnp.concatenate(dxdt_pairs, axis=1) + dxdt_off
        dcb16 = dcb.astype(BF16)
        dc_ref[...] = dc_acc + _dot(dcb16, bb, NN)
        db_ref[...] = db_acc + _dot(dcb16, cb16, TN)
        dxs_ref[...] = dxdt * dt_exp + dy * d_exp

        dcs_all += dcs_rows.T
        row = lax.broadcasted_iota(jnp.int32, (CHUNK, CHUNK), 0)
        col = lax.broadcasted_iota(jnp.int32, (CHUNK, CHUNK), 1)
        last_row = lax.broadcasted_iota(jnp.int32, (CHUNK, LANES), 0) == CHUNK - 1
        dcs_all += jnp.where(last_row, dlast, 0.0)
        da_all = _dot_01(dcs_all, (col >= row).astype(BF16), True, 3)
        dta = dtall[...]
        in_group = jnp.logical_and(lane_l >= HEADS_PER_GROUP * g, lane_l < HEADS_PER_GROUP * (g + 1))
        ddt = jnp.where(in_group, da_all * a_ref[...] + _heads_to_lanes(dxdt * xs, g), 0.0)
        da_ref[...] += jnp.sum(jnp.where(in_group, da_all * dta, 0.0), axis=0, keepdims=True)
        ddt_raw = ddt * _sigmoid(dt_ref[...] + bias_ref[...])
        ddt_ref[...] += ddt_raw
        dbias_ref[...] += jnp.sum(ddt_raw, axis=0, keepdims=True)

    gn = SSM_GROUPS * SSM_STATE
    wide = pl.BlockSpec((CHUNK, D_INNER), lambda c: (nc - 1 - c, 0))
    st = pl.BlockSpec((CHUNK, gn), lambda c: (nc - 1 - c, 0))
    par = pl.BlockSpec((1, LANES), lambda c: (0, 0))
    dtb = pl.BlockSpec((CHUNK, LANES), lambda c: (nc - 1 - c, 0))
    f = lambda shape: _sds(shape, F32)
    return _pcall(
        body,
        out_shape=[f((s, D_INNER)), f((s, gn)), f((s, gn)),
                   f((s, D_INNER)), f((s, LANES)), f((8, GROUP_DIM)), f((1, LANES)), f((1, LANES)), f((1, LANES))],
        grid=(nc,),
        in_specs=[wide, wide,
                  pl.BlockSpec((CHUNK, gn), lambda c: (nc - 1 - c, D_INNER // gn)),
                  pl.BlockSpec((CHUNK, gn), lambda c: (nc - 1 - c, D_INNER // gn + 1)),
                  wide, wide,
                  pl.BlockSpec((1, SSM_GROUPS, GROUP_DIM, SSM_STATE), lambda c: (nc - 1 - c, 0, 0, 0)),
                  dtb, par, par, par,
                  pl.BlockSpec((1, D_INNER), lambda c: (0, 0))],
        out_specs=[wide, st, st, wide, dtb, pl.BlockSpec((8, GROUP_DIM), lambda c: (0, 0)), par, par, par],
        scratch_shapes=[pltpu.VMEM((SSM_GROUPS, GROUP_DIM, SSM_STATE), F32),
                        pltpu.VMEM((CHUNK, LANES), F32), pltpu.VMEM((CHUNK, LANES), F32),
                        pltpu.VMEM((LANES, CHUNK), F32)],
        args=[dyn, act, act, act, z, y_pre, states, dtp, bias_p, a_p, d_p, normw], name=name, ride=ride)


def _attn_probs(q, kp, kc, sink, n):
    sp = _dot(q, kp, NT)
    sc = _dot(q, kc, NT)
    i = lax.broadcasted_iota(jnp.int32, sp.shape, 0) & (WINDOW - 1)
    j = lax.broadcasted_iota(jnp.int32, sp.shape, 1)
    sp = jnp.where(jnp.logical_and(j > i, n > 0), sp, NEG)
    sc = jnp.where(j <= i, sc, NEG)
    m = jnp.maximum(jnp.maximum(jnp.max(sp, axis=1, keepdims=True), jnp.max(sc, axis=1, keepdims=True)), sink)
    pp = jnp.exp(sp - m)
    pc = jnp.exp(sc - m)
    ps = jnp.exp(sink - m)
    inv = 1.0 / (jnp.sum(pp, axis=1, keepdims=True) + jnp.sum(pc, axis=1, keepdims=True) + ps)
    return pp * inv, pc * inv, ps * inv


def attn_fwd(qt, kt, vt, sink_rows, name, ride=None):
    s = qt.shape[1]
    nb = s // WINDOW
    rows = Q_PER_KV * WINDOW

    def body(q_ref, kp_ref, kc_ref, vp_ref, vc_ref, sk_ref, o_ref):
        n = pl.program_id(0)
        for h in range(N_KV_HEADS):
            heads = pl.ds(h * Q_PER_KV, Q_PER_KV)
            q = q_ref[heads].reshape(rows, ATT_HEAD_DIM)
            pp, pc, _ = _attn_probs(q, kp_ref[h], kc_ref[h], sk_ref[h], n)
            o = _dot(pp.astype(BF16), vp_ref[h]) + _dot(pc.astype(BF16), vc_ref[h])
            o_ref[heads] = o.reshape(Q_PER_KV, WINDOW, ATT_HEAD_DIM).astype(BF16)

    qsp = pl.BlockSpec((N_Q_HEADS, WINDOW, ATT_HEAD_DIM), lambda n: (0, n, 0))
    prev = pl.BlockSpec((N_KV_HEADS, WINDOW, ATT_HEAD_DIM), lambda n: (0, jnp.maximum(n - 1, 0), 0))
    cur = pl.BlockSpec((N_KV_HEADS, WINDOW, ATT_HEAD_DIM), lambda n: (0, n, 0))
    return _pcall(body, out_shape=[_sds(qt.shape, BF16)], grid=(nb,),
                  in_specs=[qsp, prev, cur, prev, cur, pl.BlockSpec((N_KV_HEADS, rows, 1), lambda n: (0, 0, 0))],
                  out_specs=[qsp], args=[qt, kt, kt, vt, vt, sink_rows], name=name, ride=ride)


def attn_bwd(qt, kt, vt, sink_rows, dot_, name, ride=None):
    s = qt.shape[1]
    nb = s // WINDOW
    rows = Q_PER_KV * WINDOW

    def body(q_ref, kp_ref, kc_ref, vp_ref, vc_ref, sk_ref, do_ref, dq_ref, dk_ref, dv_ref, ds_ref, kacc, vacc):
        n = pl.program_id(0)

        @pl.when(n == 0)
        def _():
            kacc[...] = jnp.zeros(kacc.shape, F32)
            vacc[...] = jnp.zeros(vacc.shape, F32)

        @pl.when(n < nb)
        def _():
            for h in range(N_KV_HEADS):
                heads = pl.ds(h * Q_PER_KV, Q_PER_KV)
                q = q_ref[heads].reshape(rows, ATT_HEAD_DIM)
                do = do_ref[heads].reshape(rows, ATT_HEAD_DIM)
                kp, kc, vp, vc = kp_ref[h], kc_ref[h], vp_ref[h], vc_ref[h]
                pp, pc, ps = _attn_probs(q, kp, kc, sk_ref[h], n)
                dpp = _dot(do, vp, NT)
                dpc = _dot(do, vc, NT)
                delta = jnp.sum(pp * dpp, axis=1, keepdims=True) + jnp.sum(pc * dpc, axis=1, keepdims=True)
                dsp = (pp * (dpp - delta)).astype(BF16)
                dsc = (pc * (dpc - delta)).astype(BF16)
                dq = _dot(dsp, kp) + _dot(dsc, kc)
                dq_ref[heads] = dq.reshape(Q_PER_KV, WINDOW, ATT_HEAD_DIM)
                dk_ref[h] = kacc[h] + _dot(dsp, q, TN)
                dv_ref[h] = vacc[h] + _dot(pp.astype(BF16), do, TN)
                kacc[h] = _dot(dsc, q, TN)
                vacc[h] = _dot(pc.astype(BF16), do, TN)
                dsk = -ps * delta
                sub = lax.broadcasted_iota(jnp.int32, (8, LANES), 0)
                tile = jnp.zeros((8, LANES), F32)
                for j in range(Q_PER_KV):
                    tile += jnp.where(sub == j, jnp.sum(dsk[j * WINDOW:(j + 1) * WINDOW, :], axis=0, keepdims=True),
                                      0.0)
                ds_ref[h, 0] = tile

        @pl.when(n == nb)
        def _():
            dk_ref[...] = kacc[...]
            dv_ref[...] = vacc[...]
            ds_ref[...] = jnp.zeros(ds_ref.shape, F32)

    last = nb - 1
    qsp = pl.BlockSpec((N_Q_HEADS, WINDOW, ATT_HEAD_DIM), lambda n: (0, jnp.minimum(n, last), 0))
    prev = pl.BlockSpec((N_KV_HEADS, WINDOW, ATT_HEAD_DIM), lambda n: (0, jnp.clip(n - 1, 0, last), 0))
    cur = pl.BlockSpec((N_KV_HEADS, WINDOW, ATT_HEAD_DIM), lambda n: (0, jnp.minimum(n, last), 0))
    dkv = pl.BlockSpec((N_KV_HEADS, WINDOW, ATT_HEAD_DIM), lambda n: (0, jnp.maximum(n - 1, 0), 0))
    f = lambda shape: _sds(shape, F32)
    acc = pltpu.VMEM((N_KV_HEADS, WINDOW, ATT_HEAD_DIM), F32)
    return _pcall(
        body, out_shape=[f(qt.shape), f(kt.shape), f(vt.shape), f((N_KV_HEADS, nb + 1, 8, LANES))],
        grid=(nb + 1,),
        in_specs=[qsp, prev, cur, prev, cur, pl.BlockSpec((N_KV_HEADS, rows, 1), lambda n: (0, 0, 0)), qsp],
        out_specs=[qsp, dkv, dkv, pl.BlockSpec((N_KV_HEADS, 1, 8, LANES), lambda n: (0, n, 0, 0))],
        scratch_shapes=[acc, acc], args=[qt, kt, kt, vt, vt, sink_rows, dot_], name=name, ride=ride)


def loss_head(x, w, tgt, name):
    s, d = x.shape
    tm = _row_tile(s, 256)

    def body(x_ref, w_ref, t_ref, loss_ref, dx_ref, dw_ref):
        i = pl.program_id(0)
        xf = x_ref[...]
        wv = w_ref[...]
        r = lax.rsqrt(jnp.mean(xf * xf, axis=-1, keepdims=True) + EPS)
        xhat = xf * r
        e = xhat * wv - t_ref[...]
        part = 0.5 * jnp.sum(jnp.mean(e * e, axis=-1, keepdims=True), axis=0, keepdims=True)
        dy = e * (1.0 / d)
        dxhat = dy * wv
        dx_ref[...] = r * (dxhat - xhat * jnp.mean(dxhat * xhat, axis=-1, keepdims=True))
        col = jnp.sum(dy * xhat, axis=0, keepdims=True)

        @pl.when(i == 0)
        def _():
            loss_ref[...] = jnp.broadcast_to(part, (1, LANES))
            dw_ref[...] = col

        @pl.when(i > 0)
        def _():
            loss_ref[...] += jnp.broadcast_to(part, (1, LANES))
            dw_ref[...] += col

    row = pl.BlockSpec((tm, d), lambda i: (i, 0))
    vec = pl.BlockSpec((1, d), lambda i: (0, 0))
    return _pcall(body, out_shape=[_sds((1, LANES), F32), _sds((s, d), F32), _sds((1, d), F32)], grid=(s // tm,),
                  in_specs=[row, vec, row], out_specs=[pl.BlockSpec((1, LANES), lambda i: (0, 0)), row, vec],
                  args=[x, w.reshape(1, d), tgt], name=name)


ELEMWISE_TILE = 720 * 1024


def _tile_rows(r, c, max_elems=262144, mult=16):
    best = None
    for t in range(mult, r + 1, mult):
        if r % t == 0 and t * c <= max_elems:
            best = t
    return best or r


def add_pair(xhs, ps, c_idx, name):
    n = len(xhs)
    _, r, c = xhs[0].shape
    tr = _tile_rows(r, c, max_elems=ELEMWISE_TILE)

    def body(c_ref, *refs):
        for x_ref, p_ref, o_ref in zip(refs[:n], refs[n:2 * n], refs[2 * n:]):
            o_ref[...] = (x_ref[0].astype(F32) + p_ref[...].astype(F32)).astype(BF16)

    blk = pl.BlockSpec((tr, c), lambda i, cr: (i, 0))
    return pl.pallas_call(
        body, out_shape=tuple([_sds((r, c), BF16)] * n),
        grid_spec=pltpu.PrefetchScalarGridSpec(
            num_scalar_prefetch=1, grid=(r // tr,),
            in_specs=[pl.BlockSpec((1, tr, c), lambda i, cr: (cr[0], i, 0))] * n + [blk] * n,
            out_specs=tuple([blk] * n)),
        name=name, compiler_params=_cp(1))(c_idx, *xhs, *ps)


def sum_chips(qs, owns, chip_idx, name):
    n = len(qs)
    _, r, c = qs[0].shape
    tr = _tile_rows(r, c, max_elems=ELEMWISE_TILE // max(1, n - 1))

    def body(k_ref, *refs):
        k = k_ref[0]
        for q_ref, own_ref, o_ref in zip(refs[:n], refs[n:2 * n], refs[2 * n:]):
            mine = own_ref[0].astype(F32)
            tot = None
            for j in range(N_CHIPS):
                term = jnp.where(k == j, mine, q_ref[j].astype(F32))
                tot = term if tot is None else tot + term
            o_ref[...] = tot

    return pl.pallas_call(
        body, out_shape=tuple([_sds((r, c), F32)] * n),
        grid_spec=pltpu.PrefetchScalarGridSpec(
            num_scalar_prefetch=1, grid=(r // tr,),
            in_specs=([pl.BlockSpec((N_CHIPS, tr, c), lambda i, kr: (0, i, 0))] * n
                      + [pl.BlockSpec((1, tr, c), lambda i, kr: (kr[0], i, 0))] * n),
            out_specs=tuple([pl.BlockSpec((tr, c), lambda i, kr: (i, 0))] * n)),
        name=name, compiler_params=_cp(1))(chip_idx, *qs, *owns)


def adamw(w, g, m, v, name):
    r, c = w.shape
    tr = _tile_rows(r, c, mult=8)
    c1 = 1.0 / (1.0 - ADAM_B1 ** ADAM_STEP)
    c2 = 1.0 / (1.0 - ADAM_B2 ** ADAM_STEP)

    def body(w_ref, g_ref, m_ref, v_ref, d_ref, mo_ref, vo_ref):
        gf = g_ref[...]
        mn = ADAM_B1 * m_ref[...] + (1.0 - ADAM_B1) * gf
        vn = ADAM_B2 * v_ref[...] + (1.0 - ADAM_B2) * (gf * gf)
        mo_ref[...] = mn
        vo_ref[...] = vn
        d_ref[...] = -ADAM_LR * ((mn * c1) / (jnp.sqrt(vn * c2) + ADAM_EPS) + ADAM_WD * w_ref[...])

    blk = pl.BlockSpec((tr, c), lambda i: (i, 0))
    out = _sds((r, c), F32)
    return _pcall(body, out_shape=[out, out, out], grid=(r // tr,), in_specs=[blk] * 4, out_specs=[blk] * 3,
                  args=[w, g, m, v], name=name)


WEIGHTS = ['norm_w', 'ffn_w_gate', 'ffn_w_up', 'ffn_w_down', 'ssm_w_in', 'ssm_conv_w', 'ssm_conv_b', 'ssm_dt_bias',
           'ssm_a_log', 'ssm_d', 'ssm_norm_w', 'ssm_w_out', 'kv_norm_w', 'w_k', 'b_k', 'w_v', 'b_v', 'attn_w_q',
           'attn_b_q', 'attn_sinks', 'attn_w_o', 'attn_b_o', 'final_norm_w']
BIG = ['ffn_w_gate', 'ffn_w_up', 'ffn_w_down', 'ssm_w_in', 'ssm_w_out', 'w_k', 'w_v', 'attn_w_q', 'attn_w_o']
TRANSPOSED = ('ffn_w_gate', 'ffn_w_up', 'ssm_w_in')
SMALL = [n for n in WEIGHTS if n not in BIG]
SMALL_SHARDED = {'norm_w': 2, 'ssm_conv_w': 2, 'ssm_conv_b': 1, 'ssm_norm_w': 1}
ROW_ALIGN = 8 * LANES


def _pack_rows(parts):
    flat = jnp.concatenate([p.reshape(-1).astype(F32) for p in parts])
    pad = (-flat.size) % ROW_ALIGN
    return jnp.pad(flat, (0, pad)).reshape(-1, LANES)


def _unpack_rows(buf, shapes):
    flat = buf.reshape(-1)
    out, pos = [], 0
    for shp in shapes:
        size = math.prod(shp)
        out.append(flat[pos:pos + size].reshape(shp))
        pos += size
    return out


def _as2d(a):
    return a.reshape(-1, a.shape[-1])


def _heads_major(t, n_heads):
    s = t.shape[0]
    return t.reshape(s, n_heads, ATT_HEAD_DIM).transpose(1, 0, 2)


def _tokens_major(t):
    h, s, dh = t.shape
    return t.transpose(1, 0, 2).reshape(s, h * dh)


def _pad_lanes(v):
    return jnp.pad(v.reshape(1, -1), ((0, 0), (0, LANES - v.size)))


def _chips_first(t):
    return t.swapaxes(0, 1).reshape((-1,) + t.shape[3:])


def _parts_first(t, rows):
    return t.reshape((N_CHIPS, N_CORES, rows) + t.shape[1:]).swapaxes(0, 1)


def kernel(*args):
    names = (['x'] + WEIGHTS + ['loss_target'] + ['m_' + n for n in WEIGHTS] + ['v_' + n for n in WEIGHTS])
    a = dict(zip(names, args))
    for n in TRANSPOSED:
        for pre in ('', 'm_', 'v_'):
            a[pre + n] = a[pre + n].swapaxes(-1, -2)
    xi, yi, ci = lax.axis_index("x"), lax.axis_index("y"), lax.axis_index("c")
    chip = 2 * xi + yi
    south = ci == 0
    c_idx = jnp.reshape(ci, (1,)).astype(jnp.int32)
    chip_idx = jnp.reshape(chip, (1,)).astype(jnp.int32)
    x0 = a['x'][0]
    s = x0.shape[0]
    cos, sin = rope_tables(s)

    def own_slot(full, mine):
        return lax.dynamic_update_slice_in_dim(full, mine[:, None], chip, axis=1)

    def ffn_shard(l, i):
        return [a[n][l, i].astype(BF16).reshape(N_CORES, FF_PART, D_MODEL)
                for n in ('ffn_w_gate', 'ffn_w_up', 'ffn_w_down')]

    def own_slots(fulls, mines):
        return [own_slot(f, m) for f, m in zip(fulls, mines)]
    small_names = list(SMALL_SHARDED)
    small_sh = _pack_rows([a[n] for n in small_names])
    small_sh = small_sh.reshape(N_CORES, small_sh.shape[0] // 2, LANES)
    sh00, sh01, sh10, sh11 = ffn_shard(0, 0), ffn_shard(0, 1), ffn_shard(1, 0), ffn_shard(1, 1)
    w_in_sh = jnp.pad(a['ssm_w_in'][0], ((0, IN_SHARD_PAD - IN_SHARD), (0, 0))).astype(BF16).reshape(
        N_CORES, IN_SHARD_PAD // 2, D_MODEL)
    w_out_sh = a['ssm_w_out'][0].astype(BF16).reshape(N_CORES, 256, D_MODEL)
    attn_sh = jnp.stack([a['attn_w_q'][0], a['attn_w_o'][0]]).astype(BF16)
    kv_sh = jnp.stack([a['w_k'], a['w_v']]).astype(BF16)
    in_flight, all_started = split_start(
        [sh00 + [small_sh], [w_in_sh, kv_sh], [w_out_sh], sh01, sh10, [attn_sh], sh11], "gather", "gather_start")

    def arrive(idx, after, tag):
        return forward_cores(split_arrive(in_flight[idx], "gather", after, "gather_arrive_" + tag))

    first = run_exchange(arrive(0, all_started, "first"), "gather_hop_first")
    w00 = own_slots(first[:3], sh00)
    smalls = own_slot(first[3], small_sh)
    p = {}
    per_chip = [_unpack_rows(smalls[:, k], [a[n].shape for n in small_names]) for k in range(N_CHIPS)]
    for idx, n in enumerate(small_names):
        p[n] = jnp.concatenate([per_chip[k][idx] for k in range(N_CHIPS)], axis=SMALL_SHARDED[n])
    nw = p['norm_w']
    conv_w, conv_b, ssm_nw = p['ssm_conv_w'][0], p['ssm_conv_b'][0], p['ssm_norm_w'][0].reshape(1, D_INNER)

    h00 = rmsnorm_fwd(x0, nw[0, 0], "norm_in")
    x1, h01, gu00 = ffn_fwd(h00, x0, *w00, [nw[0, 1]], "ffn_fwd_00")
    w_in_g, kv_g = run_exchange(arrive(1, x1, "in"), "gather_hop_in")
    w_in_t = _chips_first(own_slot(w_in_g, w_in_sh)).reshape(N_CHIPS, IN_SHARD_PAD, D_MODEL)[:, :IN_SHARD].reshape(
        IN_PROJ_DIM, D_MODEL)
    w_dt_t = jnp.pad(w_in_t[D_INNER + CONV_DIM:], ((0, LANES - SSM_HEADS), (0, 0)))
    kv_g = own_slot(kv_g, kv_sh)
    w_k, w_v = kv_g[0].reshape(D_MODEL, KV_DIM), kv_g[1].reshape(D_MODEL, KV_DIM)

    zz = mm_nt(h01, w_in_t, "ssm_in_z", n=D_INNER)
    xbc = mm_nt(h01, w_in_t, "ssm_in_xbc", n=CONV_DIM, row0=D_INNER)
    dtp = mm_nt(h01, w_dt_t, "ssm_in_dt")
    act = conv_fwd(xbc, conv_w, conv_b, "ssm_conv")
    bias_p = _pad_lanes(a['ssm_dt_bias'][0])
    a_p = _pad_lanes(-jnp.exp(a['ssm_a_log'][0]))
    d_p = _pad_lanes(a['ssm_d'][0])
    (yn, y_pre, states), (w_out_g,) = ssd_fwd(act, zz, dtp, bias_p, a_p, d_p, ssm_nw, "ssd_fwd",
                                              ride=arrive(2, act, "out"))
    w_out = _chips_first(own_slot(w_out_g, w_out_sh))
    (x2, h02), w01 = mm_res(yn, w_out, x1, "ssm_out", norm_ws=[nw[0, 2]], ride=arrive(3, yn, "01"))
    w01 = own_slots(w01, sh01)
    x3, hkv, h10, gu01 = ffn_fwd(h02, x2, *w01, [a['kv_norm_w'], nw[1, 0]], "ffn_fwd_01")
    w10 = own_slots(run_exchange(arrive(4, x3, "10"), "gather_hop_10"), sh10)

    k_rot = rope_apply(mm_nn(hkv, w_k, "kv_k", bias=a['b_k']), cos, sin, "rope_k")
    v = mm_nn(hkv, w_v, "kv_v", bias=a['b_v'], out_dtype=BF16)
    kt = _heads_major(k_rot, N_KV_HEADS)
    vt = _heads_major(v, N_KV_HEADS)

    (x4, h11, gu10), (attn_g,) = ffn_fwd(h10, x3, *w10, [nw[1, 1]], "ffn_fwd_10", ride=arrive(5, v, "attn"))
    attn_g = own_slot(attn_g, attn_sh)
    w_q, w_o = attn_g[0].reshape(D_MODEL, D_MODEL), attn_g[1].reshape(D_MODEL, D_MODEL)
    scale = 1.0 / math.sqrt(ATT_HEAD_DIM)
    q_rot = rope_apply(mm_nn(h11, w_q, "attn_q", bias=a['attn_b_q'][0]), cos, sin, "rope_q", scale=scale)
    qt = _heads_major(q_rot, N_Q_HEADS)
    sink_rows = jnp.repeat(a['attn_sinks'][0].reshape(N_KV_HEADS, Q_PER_KV), WINDOW, axis=1).reshape(
        N_KV_HEADS, Q_PER_KV * WINDOW, 1)
    (ot,) = attn_fwd(qt, kt, vt, sink_rows, "attn_fwd")
    o = _tokens_major(ot)
    (x5, h12), w11 = mm_res(o, w_o, x4, "attn_out", bias=a['attn_b_o'][0], norm_ws=[nw[1, 2]],
                            ride=arrive(6, ot, "11"))
    w11 = own_slots(w11, sh11)
    x6, gu11 = ffn_fwd(h12, x5, *w11, [], "ffn_fwd_11")

    loss_v, dx6, d_final = loss_head(x6, a['final_norm_w'], a['loss_target'][0], "loss_head")
    g = {'final_norm_w': d_final[0]}

    def same_shape(xs, ys):
        runs = []
        for xv, yv in zip(xs, ys):
            if runs and runs[-1][0][0].shape == xv.shape:
                runs[-1][0].append(xv)
                runs[-1][1].append(yv)
            else:
                runs.append(([xv], [yv]))
        return runs

    def pre_reduce(grads, sib, tag):
        out = []
        for idx, (grp, sbs) in enumerate(same_shape(grads, list(sib))):
            ts = add_pair([gr.reshape(2, -1, gr.shape[-1]) for gr in grp], [_as2d(sb) for sb in sbs], c_idx,
                          "rs_add_%s_%d" % (tag, idx))
            out += [t.reshape(gr.shape[1:]) for t, gr in zip(ts, grp)]
        return out

    def chip_sum(landed, parts, tag):
        out = []
        for idx, (qs, owns) in enumerate(same_shape(list(landed), parts)):
            ts = sum_chips([q.reshape(N_CHIPS, -1, q.shape[-1]) for q in qs],
                           [own.reshape(N_CHIPS, -1, own.shape[-1]) for own in owns], chip_idx,
                           "rs_sum_%s_%d" % (tag, idx))
            out += [t.reshape(q.shape[1:]) for t, q in zip(ts, qs)]
        return out

    dnw = [[None] * 3 for _ in range(2)]
    sums = {}

    def trade(key):
        return swap_cores(sums[key], False)

    dx5, dnw12, *g11 = ffn_bwd(dx6, h12, x5, nw[1, 2], gu11, *w11, "ffn_bwd_11")
    dnw[1][2] = dnw12[0]
    (d_wo, g['attn_b_o']), sib11 = mm_tn(o, dx5, "attn_dwo", col_sum=True, ride=swap_cores(g11, True))
    t11 = pre_reduce(g11, sib11, "11")
    do = mm_nt(dx5, w_o, "attn_do", out_dtype=BF16)
    (dqt, dkt, dvt, dsink), land11 = attn_bwd(qt, kt, vt, sink_rows, _heads_major(do, N_Q_HEADS), "attn_bwd",
                                             ride=scatter_chips(t11[:2]))
    g['attn_sinks'] = jnp.sum(dsink[:, :, :Q_PER_KV, 0], axis=1).reshape(N_Q_HEADS)
    dq_pre = rope_apply(_tokens_major(dqt), cos, sin, "rope_dq", inverse=True, scale=scale, out_dtype=F32)
    d_wq, g['attn_b_q'] = mm_tn(h11, dq_pre, "attn_dwq", col_sum=True)
    g_attn = [jnp.stack([d_wq.reshape(N_CHIPS, 256, D_MODEL), d_wo.reshape(N_CHIPS, 256, D_MODEL)])]
    (dx4, dnw[1][1]), sib_attn = mm_rms_bwd([(dq_pre, 0, w_q, 0, D_MODEL, "nt")], dx5, x4, nw[1, 1], "attn_bwd_dh",
                                            ride=swap_cores(g_attn, True))
    t_attn = pre_reduce(g_attn, sib_attn, "attn")
    (dx3, dnw10, *g10), landed = ffn_bwd(dx4, h10, x3, nw[1, 0], gu10, *w10, "ffn_bwd_10",
                                         ride=scatter_chips(t_attn + t11[2:]))
    dnw[1][0] = dnw10[0]
    sums['attn'] = chip_sum(landed[:1], t_attn, "attn")
    sums['11'] = chip_sum(list(land11) + list(landed[1:]), t11, "11")
    dk_pre = rope_apply(_tokens_major(dkt), cos, sin, "rope_dk", inverse=True, out_dtype=F32)
    dv = _tokens_major(dvt)
    (d_wk, g['b_k']), sib10 = mm_tn(hkv, dk_pre, "kv_dwk", col_sum=True, ride=swap_cores(g10, True))
    t10 = pre_reduce(g10, sib10, "10")
    d_wv, g['b_v'] = mm_tn(hkv, dv, "kv_dwv", col_sum=True)
    g_kv = [jnp.stack([d_wk.reshape(N_CHIPS, 256, KV_DIM), d_wv.reshape(N_CHIPS, 256, KV_DIM)])]
    (dx3, g['kv_norm_w']), sib_kv = mm_rms_bwd(
        [(dk_pre, 0, w_k, 0, KV_DIM, "nt"), (dv, 0, w_v, 0, KV_DIM, "nt")], dx3, x3, a['kv_norm_w'], "kv_bwd_dh",
        ride=swap_cores(g_kv, True))
    t_kv = pre_reduce(g_kv, sib_kv, "kv")
    (dx2, dnw02, *g01), landed = ffn_bwd(dx3, h02, x2, nw[0, 2], gu01, *w01, "ffn_bwd_01",
                                         ride=join(scatter_chips(t10 + t_kv), trade('11'), trade('attn')))
    dnw[0][2] = dnw02[0]
    sums['10'] = chip_sum(landed[:3], t10, "10")
    sums['kv'] = chip_sum(landed[3:4], t_kv, "kv")
    theirs = {'11': landed[4:7], 'attn': landed[7:]}
    d_wout, sib01 = mm_tn(yn, dx2, "ssm_dwout", ride=swap_cores(g01, True))
    t01 = pre_reduce(g01, sib01, "01")
    dyn = mm_nt(dx2, w_out, "ssm_dyn")
    (dxs, db_, dc_, dz, ddt, d_ssm_nw, d_bias, d_a, d_d), landed = ssd_bwd(
        dyn, act, zz, y_pre, states, dtp, bias_p, a_p, d_p, ssm_nw, "ssd_bwd",
        ride=join(scatter_chips(t01[:2]), trade('10'), trade('kv')))
    land01 = list(landed[:2])
    theirs['10'], theirs['kv'] = landed[2:5], landed[5:]
    g['ssm_norm_w'] = d_ssm_nw[:SSM_GROUPS].reshape(D_INNER)
    g['ssm_dt_bias'] = d_bias[0, :SSM_HEADS]
    g['ssm_a_log'] = d_a[0, :SSM_HEADS] * a_p[0, :SSM_HEADS]
    g['ssm_d'] = d_d[0, :SSM_HEADS]
    (dxbc, g['ssm_conv_w'], g['ssm_conv_b']), landed = conv_bwd(dxs, db_, dc_, xbc, conv_w, conv_b, "ssm_conv_bwd",
                                                                ride=scatter_chips(t01[2:]))
    sums['01'] = chip_sum(land01 + list(landed), t01, "01")
    d_win = mm_tn(dz, h01, "ssm_dwz", rows=IN_PROJ_DIM)
    d_win = mm_tn(dxbc, h01, "ssm_dwxbc", into=d_win, rows=IN_PROJ_DIM, row0=D_INNER)
    d_win = mm_tn(ddt, h01, "ssm_dwdt", into=d_win, rows=IN_PROJ_DIM, row0=D_INNER + CONV_DIM, m_valid=SSM_HEADS)
    d_win = jnp.pad(d_win.reshape(N_CHIPS, IN_SHARD, D_MODEL), ((0, 0), (0, IN_SHARD_PAD - IN_SHARD), (0, 0)))
    g_ssm = [_parts_first(d_win.reshape(-1, D_MODEL), IN_SHARD_PAD // 2), _parts_first(d_wout, 256)]
    kb = 1024
    terms = ([(dz, j, w_in_t, j, kb, "nn") for j in range(D_INNER // kb)]
             + [(dxbc, j, w_in_t, D_INNER // kb + j, kb, "nn") for j in range(CONV_DIM // kb)]
             + [(ddt, 0, w_dt_t, 0, LANES, "nn")])
    (dx1, dnw[0][1]), sib_ssm = mm_rms_bwd(terms, dx2, x1, nw[0, 1], "ssm_bwd_dh", ride=swap_cores(g_ssm, True))
    t_ssm = pre_reduce(g_ssm, sib_ssm, "ssm")
    (grad_x, dnw00, *g00), landed = ffn_bwd(dx1, h00, x0, nw[0, 0], gu00, *w00, "ffn_bwd_00",
                                            ride=join(scatter_chips(t_ssm), trade('01')))
    dnw[0][0] = dnw00[0]
    sums['ssm'] = chip_sum(landed[:2], t_ssm, "ssm")
    theirs['01'] = landed[2:]
    landed = run_exchange(join(swap_cores(g00, True), trade('ssm')), "rs_swap_00")
    t00 = pre_reduce(g00, landed[:3], "00")
    theirs['ssm'] = landed[3:]

    def both(key):
        return [(jnp.where(south, m_, t_), jnp.where(south, t_, m_)) for m_, t_ in zip(sums[key], theirs[key])]

    g['norm_w'] = jnp.stack([jnp.stack(r) for r in dnw])
    red = all_reduce_small(_pack_rows([g[n] for n in SMALL] + [loss_v[0, :1]]), "reduce_vectors")

    t00 = lax.optimization_barrier((red, t00))[1]
    (flight00,), flying = split_start([t00], "scatter", "rs_scatter_00_start")

    def held(val):
        return lax.optimization_barrier((flying, val))[1]

    delta, new_m, new_v, gw = {}, {}, {}, {}
    ffn_names = ('ffn_w_gate', 'ffn_w_up', 'ffn_w_down')
    full = {key: both(key) for key in ('attn', 'kv', 'ssm')}
    lo, hi = full['attn'][0]
    gw['attn_w_q'], gw['attn_w_o'] = lo[None], hi[None]
    lo, hi = full['kv'][0]
    gw['w_k'], gw['w_v'] = lo, hi
    lo, hi = full['ssm'][0]
    gw['ssm_w_in'] = jnp.concatenate([lo, hi], axis=0)[:IN_SHARD][None]
    lo, hi = full['ssm'][1]
    gw['ssm_w_out'] = jnp.concatenate([lo, hi], axis=0)[None]

    *small_sums, loss = _unpack_rows(red, [g[n].shape for n in SMALL] + [()])
    for n, t in zip(SMALL, small_sums):
        if n in SMALL_SHARDED:
            ax = SMALL_SHARDED[n] - (a[n].ndim - t.ndim)
            width = a[n].shape[SMALL_SHARDED[n]]
            t = lax.dynamic_slice_in_dim(t, chip * width, width, axis=ax)
        gw[n] = t.reshape(a[n].shape)

    def update(n):
        d, mo, vo = adamw(_as2d(a[n]), held(_as2d(gw[n])), _as2d(a['m_' + n]), _as2d(a['v_' + n]), "adamw_" + n)
        delta[n], new_m[n], new_v[n] = d.reshape(a[n].shape), mo.reshape(a[n].shape), vo.reshape(a[n].shape)

    for n in BIG:
        if n not in ffn_names:
            update(n)
    shapes = [a[n].shape for n in SMALL]
    packed = [_pack_rows([src[n] for n in SMALL]) for src in
              (a, gw, {n: a['m_' + n] for n in SMALL}, {n: a['v_' + n] for n in SMALL})]
    outs = adamw(*packed, "adamw_vectors")
    for dst, buf in zip((delta, new_m, new_v), outs):
        for n, t in zip(SMALL, _unpack_rows(buf, shapes)):
            dst[n] = t
    for key in ('01', '10', '11'):
        sums[key] = held(list(sums[key]))
    rest = [both(key) for key in ('01', '10', '11')]
    done = lax.optimization_barrier((outs[0], [delta[n] for n in BIG if n not in ffn_names], rest))[0]
    land00 = split_arrive(flight00, "scatter", done, "rs_scatter_00_arrive")
    sums['00'] = chip_sum(land00, t00, "00")
    theirs['00'] = run_exchange(trade('00'), "rs_trade_00")
    blocks = [both('00')] + rest
    for t, n in enumerate(ffn_names):
        gw[n] = jnp.concatenate([piece for blk in blocks for piece in blk[t]], axis=0).reshape(a[n].shape)
        update(n)
    for n in TRANSPOSED:
        for dst in (gw, delta, new_m, new_v):
            dst[n] = dst[n].swapaxes(-1, -2)

    return (loss, grad_x[None], *[gw[n] for n in WEIGHTS], *[delta[n] for n in WEIGHTS],
            *[new_m[n] for n in WEIGHTS], *[new_v[n] for n in WEIGHTS])
```

```python
import math

import jax
import jax.numpy as jnp
from jax import lax
from jax.experimental import pallas as pl
from jax.experimental.pallas import tpu as pltpu

F32 = jnp.float32
BF16 = jnp.bfloat16

D_MODEL = 1024
D_INNER = 2048
SSM_HEADS = 32
SSM_GROUPS = 4
HEADS_PER_GROUP = SSM_HEADS // SSM_GROUPS
SSM_HEAD_DIM = 64
SSM_STATE = 128
GROUP_DIM = D_INNER // SSM_GROUPS
CONV_DIM = D_INNER + 2 * SSM_GROUPS * SSM_STATE
CONV_WIDTH = 4
CHUNK = 128
ATT_HEAD_DIM = 64
N_Q_HEADS = 16
N_KV_HEADS = 4
Q_PER_KV = N_Q_HEADS // N_KV_HEADS
KV_DIM = N_KV_HEADS * ATT_HEAD_DIM
WINDOW = 128
ROPE_THETA = 10000.0
D_FF = 2816
N_CHIPS = 4
N_CORES = 2
FF_SHARD = D_FF // N_CHIPS
FF_PART = FF_SHARD // N_CORES
IN_PROJ_DIM = D_INNER + CONV_DIM + SSM_HEADS
IN_SHARD = IN_PROJ_DIM // N_CHIPS
IN_SHARD_PAD = 1312
EPS = 1e-5
NEG = -1e30
LANES = 128
VMEM_LIMIT = 60 * 1024 * 1024

ADAM_LR = 0.001
ADAM_B1 = 0.9
ADAM_B2 = 0.999
ADAM_EPS = 1e-08
ADAM_WD = 0.01
ADAM_STEP = 10

NN = ((1,), (0,))
NT = ((1,), (1,))
TN = ((0,), (0,))
MESH = pl.DeviceIdType.MESH
ANY = pl.BlockSpec(memory_space=pl.ANY)


def _dot(a, b, dims=NN, precision=None):
    return lax.dot_general(a, b, (dims, ((), ())), preferred_element_type=F32, precision=precision)


def _cp(n_grid):
    return pltpu.CompilerParams(dimension_semantics=("arbitrary",) * n_grid, vmem_limit_bytes=VMEM_LIMIT)


def _sigmoid(x):
    return 1.0 / (1.0 + jnp.exp(-x))


def _rms_fwd(xf, w):
    r = lax.rsqrt(jnp.mean(xf * xf, axis=-1, keepdims=True) + EPS)
    return xf * r * w


def _rms_bwd(dh, xf, w):
    r = lax.rsqrt(jnp.mean(xf * xf, axis=-1, keepdims=True) + EPS)
    xhat = xf * r
    dxhat = dh * w
    dx = r * (dxhat - xhat * jnp.mean(dxhat * xhat, axis=-1, keepdims=True))
    return dx, dh * xhat


def _row_tile(s, pref):
    return pref if s % pref == 0 else s


def _col_tile(n):
    for t in (1024, 768, 512, 256, 128):
        if n % t == 0:
            return t
    return n


def _sds(shape, dtype):
    return jax.ShapeDtypeStruct(tuple(shape), dtype)


class Exchange:
    def __init__(self, ins, out_shapes, sems, start, finish, inplace=False):
        self.ins, self.out_shapes, self.sems, self.start, self.finish = ins, out_shapes, sems, start, finish
        self.inplace = inplace


def _place():
    x, y, c = lax.axis_index("x"), lax.axis_index("y"), lax.axis_index("c")
    others = [(1 - x, y), (x, 1 - y), (1 - x, 1 - y)]
    return x, y, c, 2 * x + y, others


def _rc(src, dst, send_sem, recv_sem, dev):
    return pltpu.make_async_remote_copy(src_ref=src, dst_ref=dst, send_sem=send_sem, recv_sem=recv_sem,
                                        device_id=dev, device_id_type=MESH)


def scatter_chips(arrs):
    n = len(arrs)

    def copies(ins, outs, sems):
        send, recv = sems
        x, y, c, k, others = _place()
        out, land = [], []
        for a in range(n):
            for j, (px, py) in enumerate(others):
                out.append(_rc(ins[a].at[2 * px + py], outs[a].at[k], send.at[a, j], recv.at[a, j], (px, py, c)))
                blk = outs[a].at[2 * px + py]
                land.append(_rc(blk, blk, send.at[a, j], recv.at[a, j], (px, py, c)))
        return out, land

    def start(ins, outs, sems):
        for cp in copies(ins, outs, sems)[0]:
            cp.start()

    def finish(ins, outs, sems):
        out, land = copies(ins, outs, sems)
        for arrived in land:
            arrived.wait_recv()
        for cp in out:
            cp.wait_send()

    return Exchange(list(arrs), [_sds(a.shape, a.dtype) for a in arrs],
                    [pltpu.SemaphoreType.DMA((n, 3)), pltpu.SemaphoreType.DMA((n, 3))], start, finish)


def swap_cores(arrs, pick_other):
    n = len(arrs)

    def copies(ins, outs, sems):
        send, recv = sems
        x, y, c, _, _ = _place()
        return [_rc(ins[a].at[1 - c] if pick_other else ins[a], outs[a], send.at[a], recv.at[a], (x, y, 1 - c))
                for a in range(n)]

    def start(ins, outs, sems):
        for cp in copies(ins, outs, sems):
            cp.start()

    def finish(ins, outs, sems):
        for cp in copies(ins, outs, sems):
            cp.wait()

    shapes = [_sds(a.shape[1:] if pick_other else a.shape, a.dtype) for a in arrs]
    return Exchange(list(arrs), shapes, [pltpu.SemaphoreType.DMA((n,)), pltpu.SemaphoreType.DMA((n,))],
                    start, finish)


def join(*parts):
    parts = [p for p in parts if p is not None]
    if not parts:
        return None

    def split(refs, counts):
        out, pos = [], 0
        for cnt in counts:
            out.append(refs[pos:pos + cnt])
            pos += cnt
        return out

    n_in = [len(p.ins) for p in parts]
    n_out = [len(p.out_shapes) for p in parts]
    n_sem = [len(p.sems) for p in parts]

    def run(which):
        def go(ins, outs, sems):
            for p, i, o, s in zip(parts, split(ins, n_in), split(outs, n_out), split(sems, n_sem)):
                getattr(p, which)(i, o, s)
        return go

    return Exchange([a for p in parts for a in p.ins], [s for p in parts for s in p.out_shapes],
                    [s for p in parts for s in p.sems], run("start"), run("finish"))


def _pcall(body, *, out_shape, grid, in_specs, out_specs, args, name, scratch_shapes=(), ride=None, aliases=None):
    out_shape, out_specs, in_specs = tuple(out_shape), tuple(out_specs), list(in_specs)
    aliases = aliases or {}
    if ride is None:
        return pl.pallas_call(body, out_shape=out_shape, grid=grid, in_specs=in_specs, out_specs=out_specs,
                              scratch_shapes=list(scratch_shapes), input_output_aliases=aliases, name=name,
                              compiler_params=_cp(len(grid)))(*args)
    n_in, n_out, n_sc = len(args), len(out_shape), len(scratch_shapes)
    n_xi, n_xo = len(ride.ins), len(ride.out_shapes)

    def wrapped(*refs):
        pos = [0]

        def take(cnt):
            got = refs[pos[0]:pos[0] + cnt]
            pos[0] += cnt
            return got

        c_in, x_in, c_out, x_out, c_sc = take(n_in), take(n_xi), take(n_out), take(n_xo), take(n_sc)
        sems = refs[pos[0]:]
        first, last = True, True
        for d, size in enumerate(grid):
            first = jnp.logical_and(first, pl.program_id(d) == 0)
            last = jnp.logical_and(last, pl.program_id(d) == size - 1)

        @pl.when(first)
        def _():
            ride.start(x_in, x_out, sems)

        body(*c_in, *c_out, *c_sc)

        @pl.when(last)
        def _():
            ride.finish(x_in, x_out, sems)

    if ride.inplace:
        aliases = {**aliases, **{n_in + t: n_out + t for t in range(n_xi)}}
    res = pl.pallas_call(
        wrapped, out_shape=out_shape + tuple(ride.out_shapes), grid=grid,
        in_specs=in_specs + [ANY] * n_xi, out_specs=out_specs + (ANY,) * n_xo,
        scratch_shapes=list(scratch_shapes) + list(ride.sems), input_output_aliases=aliases, name=name,
        compiler_params=_cp(len(grid)))(*args, *ride.ins)
    return res[:n_out], res[n_out:]


def run_exchange(ex, name):
    n_xi, n_xo = len(ex.ins), len(ex.out_shapes)

    def body(*refs):
        ins, outs, sems = refs[:n_xi], refs[n_xi:n_xi + n_xo], refs[n_xi + n_xo:]
        ex.start(ins, outs, sems)
        ex.finish(ins, outs, sems)

    aliases = {t: t for t in range(n_xi)} if ex.inplace else {}
    return pl.pallas_call(body, out_shape=tuple(ex.out_shapes), in_specs=[ANY] * n_xi, out_specs=(ANY,) * n_xo,
                          scratch_shapes=list(ex.sems), input_output_aliases=aliases, name=name)(*ex.ins)


HBM_SPEC = pl.BlockSpec(memory_space=pltpu.HBM)
SEM_SPEC = pl.BlockSpec(memory_space=pltpu.SEMAPHORE)
EFFECT = pltpu.SideEffectType.DATAFLOW_SIDE_EFFECTING


def _route(kind, src, dst, c, k, peer):
    if kind == "gather":
        return src.at[c], dst.at[c, k], dst.at[c, peer]
    return src.at[peer], dst.at[k], dst.at[peer]


def split_start(batches, kind, name):
    flat = [a for batch in batches for a in batch]
    n, nb = len(flat), len(batches)
    lands = [lax.empty((2, N_CHIPS) + a.shape[1:] if kind == "gather" else a.shape, a.dtype) for a in flat]

    def body(*refs):
        srcs, dsts, sems, token = refs[:n], refs[n:2 * n], refs[2 * n:2 * n + 2 * nb], refs[-1]
        x, y, c, k, others = _place()
        pos = 0
        for b, batch in enumerate(batches):
            for a in range(len(batch)):
                for j, (px, py) in enumerate(others):
                    src, dst, _ = _route(kind, srcs[pos], dsts[pos], c, k, 2 * px + py)
                    _rc(src, dst, sems[2 * b].at[3 * a + j], sems[2 * b + 1].at[3 * a + j], (px, py, c)).start()
                pos += 1
        token[...] = jnp.zeros(token.shape, token.dtype)

    sem_shapes = [pltpu.SemaphoreType.DMA((3 * len(batch),)) for batch in batches for _ in range(2)]
    thru = [pltpu.HBM(a.shape, a.dtype) for a in flat] + [pltpu.HBM(l.shape, l.dtype) for l in lands]
    res = pl.pallas_call(
        body, name=name, out_shape=tuple(sem_shapes + thru + [_sds((8, LANES), F32)]),
        in_specs=[HBM_SPEC] * (2 * n),
        out_specs=tuple([SEM_SPEC] * (2 * nb) + [HBM_SPEC] * (2 * n) + [pl.BlockSpec(memory_space=pltpu.VMEM)]),
        input_output_aliases={t: 2 * nb + t for t in range(2 * n)},
        compiler_params=pltpu.CompilerParams(has_side_effects=EFFECT),
    )(*[pltpu.with_memory_space_constraint(t, pltpu.HBM) for t in flat + lands])
    sems, srcs, dsts = res[:2 * nb], res[2 * nb:2 * nb + n], res[2 * nb + n:2 * nb + 2 * n]
    out, pos = [], 0
    for b, batch in enumerate(batches):
        out.append((sems[2 * b], sems[2 * b + 1], list(srcs[pos:pos + len(batch)]), list(dsts[pos:pos + len(batch)])))
        pos += len(batch)
    return out, res[-1]


def split_arrive(handle, kind, after, name):
    send, recv, srcs, dsts = handle
    n = len(srcs)

    def body(*refs):
        s_refs, d_refs, send_ref, recv_ref = refs[:n], refs[n:2 * n], refs[2 * n], refs[2 * n + 1]
        x, y, c, k, others = _place()
        for a in range(n):
            for j, (px, py) in enumerate(others):
                src, _, landed = _route(kind, s_refs[a], d_refs[a], c, k, 2 * px + py)
                cp = _rc(src, landed, send_ref.at[3 * a + j], recv_ref.at[3 * a + j], (px, py, c))
                cp.wait_send()
                cp.wait_recv()

    res = pl.pallas_call(
        body, name=name, out_shape=tuple([pltpu.HBM(t.shape, t.dtype) for t in srcs + dsts]),
        in_specs=[HBM_SPEC] * (2 * n) + [SEM_SPEC, SEM_SPEC, ANY], out_specs=tuple([HBM_SPEC] * (2 * n)),
        input_output_aliases={t: t for t in range(2 * n)},
        compiler_params=pltpu.CompilerParams(has_side_effects=EFFECT),
    )(*srcs, *dsts, send, recv, after)
    return list(res[n:])


def forward_cores(bufs):
    n = len(bufs)

    def copies(outs, sems):
        send, recv = sems
        x, y, c, k, others = _place()
        onward, land = [], []
        for a in range(n):
            for j, (px, py) in enumerate(others):
                blk = outs[a].at[c, 2 * px + py]
                onward.append(_rc(blk, blk, send.at[a, j], recv.at[a, j], (x, y, 1 - c)))
                blk2 = outs[a].at[1 - c, 2 * px + py]
                land.append(_rc(blk2, blk2, send.at[a, j], recv.at[a, j], (x, y, 1 - c)))
        return onward, land

    def start(ins, outs, sems):
        for cp in copies(outs, sems)[0]:
            cp.start()

    def finish(ins, outs, sems):
        onward, land = copies(outs, sems)
        for arrived in land:
            arrived.wait_recv()
        for cp in onward:
            cp.wait_send()

    return Exchange(list(bufs), [_sds(b.shape, b.dtype) for b in bufs],
                    [pltpu.SemaphoreType.DMA((n, 3)), pltpu.SemaphoreType.DMA((n, 3))], start, finish, inplace=True)


def all_reduce_small(buf, name):
    r = buf.shape[0]
    n_dev = 8

    def body(in_ref, o_ref, land, send_sems, recv_sems):
        x, y, c, _, _ = _place()
        me = 4 * x + 2 * y + c
        land[me] = in_ref[...]
        sends = []
        for d in range(1, n_dev):
            peer = (x ^ (d >> 2), y ^ ((d >> 1) & 1), c ^ (d & 1))
            cp = _rc(in_ref, land.at[me], send_sems.at[d], recv_sems.at[d], peer)
            cp.start()
            sends.append(cp)
        for d in range(1, n_dev):
            blk = land.at[me ^ d]
            _rc(blk, blk, send_sems.at[d], recv_sems.at[d], (x, y, c)).wait_recv()
        for cp in sends:
            cp.wait_send()
        tot = land[0]
        for d in range(1, n_dev):
            tot = tot + land[d]
        o_ref[...] = tot

    vm = pl.BlockSpec(memory_space=pltpu.VMEM)
    return pl.pallas_call(
        body, out_shape=_sds(buf.shape, F32), in_specs=[vm], out_specs=vm,
        scratch_shapes=[pltpu.VMEM((n_dev, r, LANES), F32), pltpu.SemaphoreType.DMA((n_dev,)),
                        pltpu.SemaphoreType.DMA((n_dev,))],
        name=name)(buf)


def rmsnorm_fwd(x, w, name):
    s, d = x.shape
    tm = _row_tile(s, 512)

    def body(x_ref, w_ref, o_ref):
        o_ref[...] = _rms_fwd(x_ref[...], w_ref[...]).astype(BF16)

    return _pcall(body, out_shape=[_sds((s, d), BF16)], grid=(s // tm,),
                  in_specs=[pl.BlockSpec((tm, d), lambda i: (i, 0)), pl.BlockSpec((1, d), lambda i: (0, 0))],
                  out_specs=[pl.BlockSpec((tm, d), lambda i: (i, 0))], args=[x, w.reshape(1, d)], name=name)[0]


def _ffn_w_spec(chip_of):
    return pl.BlockSpec((N_CORES, 1, FF_PART, D_MODEL), lambda *ids: (0, chip_of(*ids), 0, 0))


def ffn_fwd(h, x, wg, wu, wd, norm_ws, name, ride=None):
    s, d = h.shape
    n_norm = len(norm_ws)
    tm = _row_tile(s, 1024)

    def body(*refs):
        h_ref, x_ref, wg_ref, wu_ref, wd_ref = refs[:5]
        nw_refs = refs[5:5 + n_norm]
        o_ref = refs[5 + n_norm]
        h_refs = refs[6 + n_norm:6 + 2 * n_norm]
        gu_ref, acc = refs[6 + 2 * n_norm], refs[7 + 2 * n_norm]
        k = pl.program_id(1)

        @pl.when(k == 0)
        def _():
            acc[...] = jnp.zeros(acc.shape, F32)

        hm = tm // 2
        for part in range(2):
            sub = pl.ds(part * hm, hm)
            hb = h_ref[sub, :]
            g = _dot(hb, wg_ref[...].reshape(FF_SHARD, d), NT)
            u = _dot(hb, wu_ref[...].reshape(FF_SHARD, d), NT)
            gu_ref[0, 0, sub, :] = g.astype(BF16)
            gu_ref[0, 1, sub, :] = u.astype(BF16)
            acc[sub, :] += _dot((g * _sigmoid(g) * u).astype(BF16), wd_ref[...].reshape(FF_SHARD, d))

        @pl.when(k == N_CHIPS - 1)
        def _():
            xn = x_ref[...] + 0.5 * acc[...]
            o_ref[...] = xn
            for nw_ref, hn_ref in zip(nw_refs, h_refs):
                hn_ref[...] = _rms_fwd(xn, nw_ref[...]).astype(BF16)

    row = pl.BlockSpec((tm, d), lambda i, k: (i, 0))
    vec = pl.BlockSpec((1, d), lambda i, k: (0, 0))
    wsp = _ffn_w_spec(lambda i, k: k)
    return _pcall(
        body, out_shape=[_sds((s, d), F32)] + [_sds((s, d), BF16)] * n_norm + [_sds((N_CHIPS, 2, s, FF_SHARD), BF16)],
        grid=(s // tm, N_CHIPS),
        in_specs=[row, row, wsp, wsp, wsp] + [vec] * n_norm,
        out_specs=[row] * (1 + n_norm) + [pl.BlockSpec((1, 2, tm, FF_SHARD), lambda i, k: (k, 0, i, 0))],
        scratch_shapes=[pltpu.VMEM((tm, d), F32)],
        args=[h, x, wg, wu, wd] + [nw.reshape(1, d) for nw in norm_ws], name=name, ride=ride)


def ffn_bwd(dxn, h, x_in, nw, gu, wg, wu, wd, name, ride=None):
    s, d = h.shape
    tm = _row_tile(s, 512)
    ni = s // tm
    last_e = N_CHIPS - 1

    def body(dxn_ref, h_ref, x_ref, nw_ref, gu_ref, wg_ref, wu_ref, wd_ref,
             dx_ref, dnw_ref, dwg_ref, dwu_ref, dwd_ref, dh, wacc):
        e = pl.program_id(0)
        i = pl.program_id(1)
        rows = pl.ds(pl.multiple_of(i * tm, tm), tm)

        @pl.when(i == 0)
        def _():
            wacc[...] = jnp.zeros(wacc.shape, F32)

        @pl.when(e == 0)
        def _():
            dh[rows, :] = jnp.zeros((tm, d), F32)

        hm = tm // 2
        for part in range(2):
            sub = pl.ds(part * hm, hm)
            dxb = dxn_ref[sub, :].astype(BF16)
            hb = h_ref[sub, :]
            g = gu_ref[0, 0, sub, :].astype(F32)
            u = gu_ref[0, 1, sub, :].astype(F32)
            drows = pl.ds(pl.multiple_of(i * tm + part * hm, hm), hm)
            sg = _sigmoid(g)
            silu = g * sg
            wacc[2] += _dot((0.5 * silu * u).astype(BF16), dxb, TN)
            da = 0.5 * _dot(dxb, wd_ref[...].reshape(FF_SHARD, d), NT)
            dg = (da * u * (sg * (1.0 + g * (1.0 - sg)))).astype(BF16)
            wacc[0] += _dot(dg, hb, TN)
            du = (da * silu).astype(BF16)
            dh[drows, :] += _dot(dg, wg_ref[...].reshape(FF_SHARD, d))
            wacc[1] += _dot(du, hb, TN)
            dh[drows, :] += _dot(du, wu_ref[...].reshape(FF_SHARD, d))

        @pl.when(i == ni - 1)
        def _():
            for t, dw_ref in enumerate((dwg_ref, dwu_ref, dwd_ref)):
                dw_ref[...] = wacc[t].astype(BF16).reshape(N_CORES, 1, FF_PART, d)

        @pl.when(e == last_e)
        def _():
            dx, dnw = _rms_bwd(dh[rows, :], x_ref[...], nw_ref[...])
            dx_ref[...] = dxn_ref[...] + dx
            col = jnp.sum(dnw, axis=0, keepdims=True)

            @pl.when(i == 0)
            def _():
                dnw_ref[...] = col

            @pl.when(i > 0)
            def _():
                dnw_ref[...] += col

    row = pl.BlockSpec((tm, d), lambda e, i: (i, 0))
    late = pl.BlockSpec((tm, d), lambda e, i: (jnp.where(e == last_e, i, 0), 0))
    vec = pl.BlockSpec((1, d), lambda e, i: (0, 0))
    wsp = _ffn_w_spec(lambda e, i: e)
    dw = _sds((N_CORES, N_CHIPS, FF_PART, d), BF16)
    return _pcall(
        body, out_shape=[_sds((s, d), F32), _sds((1, d), F32), dw, dw, dw],
        grid=(N_CHIPS, ni),
        in_specs=[row, row, late, vec, pl.BlockSpec((1, 2, tm, FF_SHARD), lambda e, i: (e, 0, i, 0)), wsp, wsp, wsp],
        out_specs=[late, vec, wsp, wsp, wsp],
        scratch_shapes=[pltpu.VMEM((s, d), F32), pltpu.VMEM((3, FF_SHARD, d), F32)],
        args=[dxn, h, x_in, nw.reshape(1, d), gu, wg, wu, wd], name=name, ride=ride)


def mm_res(a, w, x, name, bias=None, norm_ws=(), ride=None):
    s, k = a.shape
    n = w.shape[1]
    tm = _row_tile(s, 512)
    has_bias = bias is not None
    n_norm = len(norm_ws)

    def body(*refs):
        a_ref, w_ref, x_ref = refs[:3]
        pos = 3
        t = _dot(a_ref[...], w_ref[...])
        if has_bias:
            t = t + refs[pos][...]
            pos += 1
        nw_refs = refs[pos:pos + n_norm]
        o_ref = refs[pos + n_norm]
        h_refs = refs[pos + n_norm + 1:]
        xn = x_ref[...] + t
        o_ref[...] = xn
        for nw_ref, h_ref in zip(nw_refs, h_refs):
            h_ref[...] = _rms_fwd(xn, nw_ref[...]).astype(BF16)

    row = pl.BlockSpec((tm, n), lambda i: (i, 0))
    vec = pl.BlockSpec((1, n), lambda i: (0, 0))
    in_specs = [pl.BlockSpec((tm, k), lambda i: (i, 0)), pl.BlockSpec((k, n), lambda i: (0, 0)), row]
    args = [a, w, x]
    if has_bias:
        in_specs.append(vec)
        args.append(bias.reshape(1, n))
    for nw in norm_ws:
        in_specs.append(vec)
        args.append(nw.reshape(1, n))
    return _pcall(body, out_shape=[_sds((s, n), F32)] + [_sds((s, n), BF16)] * n_norm, grid=(s // tm,),
                  in_specs=in_specs, out_specs=[row] * (1 + n_norm), args=args, name=name, ride=ride)


def mm_nn(a, w, name, bias=None, out_dtype=F32):
    s, k = a.shape
    n = w.shape[1]
    tm = _row_tile(s, 512)
    tn = _col_tile(n)
    has_bias = bias is not None

    def body(*refs):
        a_ref, w_ref = refs[:2]
        o_ref = refs[-1]
        t = _dot(a_ref[...], w_ref[...])
        if has_bias:
            t = t + refs[2][...]
        o_ref[...] = t.astype(out_dtype)

    in_specs = [pl.BlockSpec((tm, k), lambda j, i: (i, 0)), pl.BlockSpec((k, tn), lambda j, i: (0, j))]
    args = [a, w]
    if has_bias:
        in_specs.append(pl.BlockSpec((1, tn), lambda j, i: (0, j)))
        args.append(bias.reshape(1, n))
    return _pcall(body, out_shape=[_sds((s, n), out_dtype)], grid=(n // tn, s // tm), in_specs=in_specs,
                  out_specs=[pl.BlockSpec((tm, tn), lambda j, i: (i, j))], args=args, name=name)[0]


def mm_nt(a, w, name, n=None, row0=0, out_dtype=F32, ride=None):
    s, k = a.shape
    n = w.shape[0] if n is None else n
    tm = _row_tile(s, 512)
    tn = _col_tile(n)
    base = row0 // tn
    assert row0 % tn == 0

    def body(a_ref, w_ref, o_ref):
        o_ref[...] = _dot(a_ref[...].astype(BF16), w_ref[...], NT).astype(out_dtype)

    res = _pcall(body, out_shape=[_sds((s, n), out_dtype)], grid=(n // tn, s // tm),
                 in_specs=[pl.BlockSpec((tm, k), lambda j, i: (i, 0)), pl.BlockSpec((tn, k), lambda j, i: (base + j, 0))],
                 out_specs=[pl.BlockSpec((tm, tn), lambda j, i: (i, j))], args=[a, w], name=name, ride=ride)
    return res[0] if ride is None else (res[0][0], res[1])


def mm_tn(a, b, name, into=None, rows=None, row0=0, m_valid=None, col_sum=False, ride=None):
    s, m = a.shape
    n = b.shape[1]
    mv = m if m_valid is None else m_valid
    tm = _col_tile(m) if m_valid is None else mv
    tn = 512 if n % 512 == 0 else n
    rows = mv if rows is None else rows
    assert row0 % tm == 0 and (m_valid is None or m == LANES)
    assert not col_sum or mv == tm
    base = row0 // tm
    ta = m if m_valid is not None else tm

    def body(*refs):
        a_ref, b_ref = refs[0], refs[1]
        o_ref = refs[-2] if col_sum else refs[-1]
        bf = b_ref[...]
        t = _dot(a_ref[...].astype(BF16), bf.astype(BF16), TN)
        o_ref[...] = t[:tm].astype(BF16)
        if col_sum:
            refs[-1][...] = jnp.sum(bf.astype(F32), axis=0, keepdims=True)

    in_specs = [pl.BlockSpec((s, ta), lambda i, j: (0, i)), pl.BlockSpec((s, tn), lambda i, j: (0, j))]
    args = [a, b]
    aliases = None
    if into is not None:
        in_specs.append(ANY)
        args.append(into)
        aliases = {2: 0}
    out_shape = [_sds((rows, n), BF16)]
    out_specs = [pl.BlockSpec((tm, tn), lambda i, j: (base + i, j))]
    if col_sum:
        out_shape.append(_sds((1, n), F32))
        out_specs.append(pl.BlockSpec((1, tn), lambda i, j: (0, j)))
    res = _pcall(body, out_shape=out_shape, grid=(mv // tm, n // tn), in_specs=in_specs, out_specs=out_specs,
                 args=args, name=name, ride=ride, aliases=aliases)
    outs = res if ride is None else res[0]
    out = (outs[0], outs[1][0]) if col_sum else outs[0]
    return out if ride is None else (out, res[1])


def mm_rms_bwd(terms, dxn, x, nw, name, ride=None):
    s, n = x.shape
    nt_ = len(terms)
    tm = _row_tile(s, 512)
    forms = [t[5] for t in terms]

    def body(*refs):
        dxn_ref, x_ref, nw_ref, dx_ref, dnw_ref = refs[2 * nt_:]
        i = pl.program_id(0)
        dh = None
        for t in range(nt_):
            part = _dot(refs[2 * t][...].astype(BF16), refs[2 * t + 1][...], NN if forms[t] == "nn" else NT)
            dh = part if dh is None else dh + part
        dx, dnw = _rms_bwd(dh, x_ref[...], nw_ref[...])
        dx_ref[...] = dxn_ref[...] + dx
        col = jnp.sum(dnw, axis=0, keepdims=True)

        @pl.when(i == 0)
        def _():
            dnw_ref[...] = col

        @pl.when(i > 0)
        def _():
            dnw_ref[...] += col

    in_specs, args = [], []
    for a, cb, w, rb, kb, form in terms:
        in_specs.append(pl.BlockSpec((tm, kb), lambda i, cb=cb: (i, cb)))
        if form == "nn":
            in_specs.append(pl.BlockSpec((kb, n), lambda i, rb=rb: (rb, 0)))
        else:
            in_specs.append(pl.BlockSpec((n, kb), lambda i, rb=rb: (0, rb)))
        args += [a, w]
    row = pl.BlockSpec((tm, n), lambda i: (i, 0))
    vec = pl.BlockSpec((1, n), lambda i: (0, 0))
    res = _pcall(body, out_shape=[_sds((s, n), F32), _sds((1, n), F32)], grid=(s // tm,),
                 in_specs=in_specs + [row, row, vec], out_specs=[row, vec],
                 args=args + [dxn, x, nw.reshape(1, n)], name=name, ride=ride)
    outs = res if ride is None else res[0]
    out = (outs[0], outs[1][0])
    return out if ride is None else (out, res[1])


def rope_tables(s):
    pos = jnp.arange(s, dtype=F32)
    inv = 1.0 / (ROPE_THETA ** (jnp.arange(0, ATT_HEAD_DIM, 2, dtype=F32) / ATT_HEAD_DIM))
    ang = pos[:, None] * inv[None, :]
    cos = jnp.tile(jnp.cos(ang), (1, 2 * LANES // ATT_HEAD_DIM))
    sin = jnp.tile(jnp.sin(ang), (1, 2 * LANES // ATT_HEAD_DIM))
    return cos, sin


def rope_apply(t, cos, sin, name, inverse=False, scale=1.0, out_dtype=BF16):
    s, n = t.shape
    tm = _row_tile(s, 512)
    half = ATT_HEAD_DIM // 2
    reps = n // LANES

    def body(t_ref, c_ref, s_ref, o_ref):
        tf = t_ref[...].astype(F32)
        c = jnp.tile(c_ref[...], (1, reps))
        sn = jnp.tile(s_ref[...], (1, reps))
        lane = lax.broadcasted_iota(jnp.int32, tf.shape, 1)
        first = (lane & (ATT_HEAD_DIM - 1)) < half
        rot = jnp.where(first, -pltpu.roll(tf, n - half, 1), pltpu.roll(tf, half, 1))
        sign = -1.0 if inverse else 1.0
        o_ref[...] = (scale * (tf * c + sign * rot * sn)).astype(out_dtype)

    tab = pl.BlockSpec((tm, LANES), lambda i: (i, 0))
    return _pcall(body, out_shape=[_sds((s, n), out_dtype)], grid=(s // tm,),
                  in_specs=[pl.BlockSpec((tm, n), lambda i: (i, 0)), tab, tab],
                  out_specs=[pl.BlockSpec((tm, n), lambda i: (i, 0))], args=[t, cos, sin], name=name)[0]


CONV_TILE = 256


def _shift_down(u, k):
    if k == 0:
        return u
    row = lax.broadcasted_iota(jnp.int32, u.shape, 0)
    return jnp.where(row >= k, pltpu.roll(u, k, 0), 0.0)


def _shift_up(u, k):
    if k == 0:
        return u
    s = u.shape[0]
    row = lax.broadcasted_iota(jnp.int32, u.shape, 0)
    return jnp.where(row < s - k, pltpu.roll(u, s - k, 0), 0.0)


def _conv_taps(u):
    return [_shift_down(u, CONV_WIDTH - 1 - k) for k in range(CONV_WIDTH)]


def _conv_pre(taps, w_ref, b_ref):
    pre = b_ref[...] + w_ref[0:1, :] * taps[0]
    for k in range(1, CONV_WIDTH):
        pre += w_ref[k:k + 1, :] * taps[k]
    return pre


def conv_fwd(u, w, b, name, ride=None):
    s, c = u.shape

    def body(u_ref, w_ref, b_ref, o_ref):
        pre = _conv_pre(_conv_taps(u_ref[...]), w_ref, b_ref)
        o_ref[...] = pre * _sigmoid(pre)

    col = pl.BlockSpec((s, CONV_TILE), lambda j: (0, j))
    res = _pcall(body, out_shape=[_sds((s, c), F32)], grid=(c // CONV_TILE,),
                 in_specs=[col, pl.BlockSpec((CONV_WIDTH, CONV_TILE), lambda j: (0, j)),
                           pl.BlockSpec((1, CONV_TILE), lambda j: (0, j))],
                 out_specs=[col], args=[u, w, b.reshape(1, c)], name=name, ride=ride)
    return res[0] if ride is None else (res[0][0], res[1])


def conv_bwd(dxs, db_, dc_, u, w, b, name, ride=None):
    s, c = u.shape
    n_x = dxs.shape[1] // CONV_TILE
    n_b = db_.shape[1] // CONV_TILE

    def body(dx_ref, dbb_ref, dcc_ref, u_ref, w_ref, b_ref, du_ref, dw_ref, dbias_ref):
        j = pl.program_id(0)
        dact = jnp.where(j < n_x, dx_ref[...], jnp.where(j < n_x + n_b, dbb_ref[...], dcc_ref[...]))
        taps = _conv_taps(u_ref[...])
        pre = _conv_pre(taps, w_ref, b_ref)
        sg = _sigmoid(pre)
        dpre = dact * (sg * (1.0 + pre * (1.0 - sg)))
        du = w_ref[CONV_WIDTH - 1:CONV_WIDTH, :] * dpre
        for k in range(CONV_WIDTH - 1):
            du += w_ref[k:k + 1, :] * _shift_up(dpre, CONV_WIDTH - 1 - k)
        du_ref[...] = du
        dbias_ref[...] = jnp.sum(dpre, axis=0, keepdims=True)
        for k in range(CONV_WIDTH):
            dw_ref[k:k + 1, :] = jnp.sum(dpre * taps[k], axis=0, keepdims=True)

    col = pl.BlockSpec((s, CONV_TILE), lambda j: (0, j))
    wsp = pl.BlockSpec((CONV_WIDTH, CONV_TILE), lambda j: (0, j))
    bsp = pl.BlockSpec((1, CONV_TILE), lambda j: (0, j))
    res = _pcall(
        body, out_shape=[_sds((s, c), F32), _sds((CONV_WIDTH, c), F32), _sds((1, c), F32)], grid=(c // CONV_TILE,),
        in_specs=[pl.BlockSpec((s, CONV_TILE), lambda j: (0, jnp.minimum(j, n_x - 1))),
                  pl.BlockSpec((s, CONV_TILE), lambda j: (0, jnp.clip(j - n_x, 0, n_b - 1))),
                  pl.BlockSpec((s, CONV_TILE), lambda j: (0, jnp.clip(j - n_x - n_b, 0, n_b - 1))),
                  col, wsp, bsp],
        out_specs=[col, wsp, bsp], args=[dxs, db_, dc_, u, w, b.reshape(1, c)], name=name, ride=ride)
    (du, dw, db), rode = res if ride is not None else (res, None)
    return (du, dw, db[0]) if ride is None else ((du, dw, db[0]), rode)


def _lane_pick(mat, idx):
    lane = lax.broadcasted_iota(jnp.int32, mat.shape, 1)
    return jnp.sum(jnp.where(lane == idx, mat, 0.0), axis=1, keepdims=True)


def _sub_pick(mat, idx):
    sub = lax.broadcasted_iota(jnp.int32, mat.shape, 0)
    return jnp.sum(jnp.where(sub == idx, mat, 0.0), axis=0, keepdims=True)


def _expand_heads(cols):
    rows = cols[0].shape[0]
    left = lax.broadcasted_iota(jnp.int32, (rows, LANES), 1) < SSM_HEAD_DIM
    return jnp.concatenate(
        [jnp.where(left, cols[2 * p], cols[2 * p + 1]) for p in range(HEADS_PER_GROUP // 2)], axis=1)


def _dot_01(x, ones, ones_first, pieces):
    tot, rest = None, x
    for _ in range(pieces):
        piece = rest.astype(BF16)
        rest = rest - piece.astype(F32)
        part = _dot(ones, piece) if ones_first else _dot(piece, ones)
        tot = part if tot is None else tot + part
    return tot


def _heads_to_lanes(mat, g):
    jj = lax.broadcasted_iota(jnp.int32, (GROUP_DIM, LANES), 0)
    ll = lax.broadcasted_iota(jnp.int32, (GROUP_DIM, LANES), 1)
    sel = (ll == HEADS_PER_GROUP * g + (jj >> 6)).astype(BF16)
    return _dot_01(mat, sel, False, 3)


def _softplus(x):
    return jnp.maximum(x, 0.0) + jnp.log1p(jnp.exp(-jnp.abs(x)))


def _ssd_scalars(dt_ref, bias_ref, a_ref, dtall, csall, cst):
    dta = _softplus(dt_ref[...] + bias_ref[...])
    row = lax.broadcasted_iota(jnp.int32, (CHUNK, CHUNK), 0)
    col = lax.broadcasted_iota(jnp.int32, (CHUNK, CHUNK), 1)
    cs = _dot_01(dta * a_ref[...], (row >= col).astype(BF16), True, 3)
    dtall[...] = dta
    csall[...] = cs
    cst[...] = cs.T


def _decay_mat(cs_col, cs_row):
    row = lax.broadcasted_iota(jnp.int32, (CHUNK, CHUNK), 0)
    col = lax.broadcasted_iota(jnp.int32, (CHUNK, CHUNK), 1)
    return jnp.exp(jnp.where(row >= col, cs_col - cs_row, NEG))


def _head_mask(xpair, right):
    lane = lax.broadcasted_iota(jnp.int32, xpair.shape, 1)
    keep = (lane >= SSM_HEAD_DIM) if right else (lane < SSM_HEAD_DIM)
    return jnp.where(keep, xpair, 0.0)


def _chunk_cols(x_all, g):
    return [_lane_pick(x_all, HEADS_PER_GROUP * g + r) for r in range(HEADS_PER_GROUP)]


def _decay_col(cs_cols):
    return jnp.concatenate(
        [jnp.broadcast_to(jnp.exp(cc[CHUNK - 1:CHUNK, :]), (SSM_HEAD_DIM, 1)) for cc in cs_cols], axis=0)


def ssd_fwd(act, z, dtp, bias_p, a_p, d_p, normw, name, ride=None):
    s = act.shape[0]
    nc = s // CHUNK

    def body(xs_all, b_all, c_all, z_all, dt_ref, bias_ref, a_ref, d_ref, nw_all,
             yn_all, y_all, st_all, state, dtall, csall, cst):
        _ssd_scalars(dt_ref, bias_ref, a_ref, dtall, csall, cst)

        @pl.when(pl.program_id(0) == 0)
        def _():
            state[...] = jnp.zeros(state.shape, F32)

        for g in range(SSM_GROUPS):
            wide = pl.ds(g * GROUP_DIM, GROUP_DIM)
            narrow = pl.ds(g * SSM_STATE, SSM_STATE)
            group(g, xs_all.at[:, wide], b_all.at[:, narrow], c_all.at[:, narrow], z_all.at[:, wide], d_ref,
                  nw_all.at[:, wide], yn_all.at[:, wide], y_all.at[:, wide], st_all.at[:, pl.ds(g, 1)],
                  state, dtall, csall, cst)

    def group(g, xs_ref, b_ref, c_ref, z_ref, d_ref, nw_ref, yn_ref, y_ref, st_ref, state, dtall, csall, cst):
        cs_cols = _chunk_cols(csall[...], g)
        dt_cols = _chunk_cols(dtall[...], g)
        cs_rows = [_sub_pick(cst[...], HEADS_PER_GROUP * g + r) for r in range(HEADS_PER_GROUP)]
        d_cols = _chunk_cols(d_ref[...], g)
        cs_exp = _expand_heads(cs_cols)
        dt_exp = _expand_heads(dt_cols)
        d_exp = _expand_heads(d_cols)
        xs = xs_ref[...]
        bb = b_ref[...].astype(BF16)
        cb16 = c_ref[...].astype(BF16)
        xdt = xs * dt_exp
        s_prev = state[g]
        st_ref[0, 0] = s_prev
        y_off = _dot(cb16, s_prev.astype(BF16), NT) * jnp.exp(cs_exp)
        decay_st = jnp.exp(cs_exp[CHUNK - 1:CHUNK, :] - cs_exp)
        contrib = _dot((xdt * decay_st).astype(BF16), bb, TN)
        state[g] = _decay_col(cs_cols) * s_prev + contrib
        cbm = _dot(cb16, bb, NT)
        pairs = []
        for p in range(HEADS_PER_GROUP // 2):
            xpair = xdt[:, LANES * p:LANES * (p + 1)]
            m0 = (cbm * _decay_mat(cs_cols[2 * p], cs_rows[2 * p])).astype(BF16)
            m1 = (cbm * _decay_mat(cs_cols[2 * p + 1], cs_rows[2 * p + 1])).astype(BF16)
            pairs.append(_dot(m0, _head_mask(xpair, False).astype(BF16))
                         + _dot(m1, _head_mask(xpair, True).astype(BF16)))
        y = jnp.concatenate(pairs, axis=1) + y_off + xs * d_exp
        y_ref[...] = y
        zf = z_ref[...]
        yg = y * (zf * _sigmoid(zf))
        yn_ref[...] = _rms_fwd(yg, nw_ref[...]).astype(BF16)

    gn = SSM_GROUPS * SSM_STATE
    wide = pl.BlockSpec((CHUNK, D_INNER), lambda c: (c, 0))
    par = pl.BlockSpec((1, LANES), lambda c: (0, 0))
    return _pcall(
        body,
        out_shape=[_sds((s, D_INNER), BF16), _sds((s, D_INNER), F32),
                   _sds((nc, SSM_GROUPS, GROUP_DIM, SSM_STATE), F32)],
        grid=(nc,),
        in_specs=[wide,
                  pl.BlockSpec((CHUNK, gn), lambda c: (c, D_INNER // gn)),
                  pl.BlockSpec((CHUNK, gn), lambda c: (c, D_INNER // gn + 1)),
                  wide,
                  pl.BlockSpec((CHUNK, LANES), lambda c: (c, 0)),
                  par, par, par,
                  pl.BlockSpec((1, D_INNER), lambda c: (0, 0))],
        out_specs=[wide, wide, pl.BlockSpec((1, SSM_GROUPS, GROUP_DIM, SSM_STATE), lambda c: (c, 0, 0, 0))],
        scratch_shapes=[pltpu.VMEM((SSM_GROUPS, GROUP_DIM, SSM_STATE), F32),
                        pltpu.VMEM((CHUNK, LANES), F32), pltpu.VMEM((CHUNK, LANES), F32),
                        pltpu.VMEM((LANES, CHUNK), F32)],
        args=[act, act, act, z, dtp, bias_p, a_p, d_p, normw], name=name, ride=ride)


def ssd_bwd(dyn, act, z, y_pre, states, dtp, bias_p, a_p, d_p, normw, name, ride=None):
    s = act.shape[0]
    nc = s // CHUNK

    def body(dyn_all, xs_all, b_all, c_all, z_all, y_all, st_all, dt_ref, bias_ref, a_ref, d_ref, nw_all,
             dxs_all, db_all, dc_all, dz_all, ddt_ref, dnw_ref, dbias_ref, da_ref, dd_ref,
             dstate, dtall, csall, cst):
        _ssd_scalars(dt_ref, bias_ref, a_ref, dtall, csall, cst)
        ddt_ref[...] = jnp.zeros((CHUNK, LANES), F32)

        @pl.when(pl.program_id(0) == 0)
        def _():
            dstate[...] = jnp.zeros(dstate.shape, F32)
            dnw_ref[...] = jnp.zeros(dnw_ref.shape, F32)
            dbias_ref[...] = jnp.zeros((1, LANES), F32)
            da_ref[...] = jnp.zeros((1, LANES), F32)
            dd_ref[...] = jnp.zeros((1, LANES), F32)

        for g in range(SSM_GROUPS):
            wide = pl.ds(g * GROUP_DIM, GROUP_DIM)
            narrow = pl.ds(g * SSM_STATE, SSM_STATE)
            group(g, dyn_all.at[:, wide], xs_all.at[:, wide], b_all.at[:, narrow], c_all.at[:, narrow],
                  z_all.at[:, wide], y_all.at[:, wide], st_all.at[:, pl.ds(g, 1)], dt_ref, bias_ref, a_ref, d_ref,
                  nw_all.at[:, wide], dxs_all.at[:, wide], db_all.at[:, narrow], dc_all.at[:, narrow],
                  dz_all.at[:, wide], ddt_ref, dnw_ref, dbias_ref, da_ref, dd_ref, dstate, dtall, csall, cst)

    def group(g, dyn_ref, xs_ref, b_ref, c_ref, z_ref, y_ref, st_ref, dt_ref, bias_ref, a_ref, d_ref, nw_ref,
              dxs_ref, db_ref, dc_ref, dz_ref, ddt_ref, dnw_ref, dbias_ref, da_ref, dd_ref,
              dstate, dtall, csall, cst):
        cs_cols = _chunk_cols(csall[...], g)
        dt_cols = _chunk_cols(dtall[...], g)
        cs_rows = [_sub_pick(cst[...], HEADS_PER_GROUP * g + r) for r in range(HEADS_PER_GROUP)]
        d_cols = _chunk_cols(d_ref[...], g)
        cs_exp = _expand_heads(cs_cols)
        dt_exp = _expand_heads(dt_cols)
        d_exp = _expand_heads(d_cols)
        xs = xs_ref[...]
        bb = b_ref[...].astype(BF16)
        cb16 = c_ref[...].astype(BF16)
        xdt = xs * dt_exp
        s_prev = st_ref[0, 0]
        s_prev16 = s_prev.astype(BF16)
        ds_next = dstate[g]
        ds16 = ds_next.astype(BF16)

        zf = z_ref[...]
        sz = _sigmoid(zf)
        silu_z = zf * sz
        y = y_ref[...]
        yg = y * silu_z
        dout = dyn_ref[...]
        dyg, dnw = _rms_bwd(dout, yg, nw_ref[...])
        dnw_ref[pl.ds(g, 1), :] += jnp.sum(dnw, axis=0, keepdims=True)
        dy = dyg * silu_z
        dz_ref[...] = dyg * y * (sz * (1.0 + zf * (1.0 - sz)))
        dd_ref[...] += jnp.sum(_heads_to_lanes(dy * xs, g), axis=0, keepdims=True)

        exp_cs = jnp.exp(cs_exp)
        decay_st = jnp.exp(cs_exp[CHUNK - 1:CHUNK, :] - cs_exp)
        cs_t = _dot(cb16, s_prev16, NT)
        dyo = dy * exp_cs
        dc_acc = _dot(dyo.astype(BF16), s_prev16, NN)
        g1 = _dot(bb, ds16, NT)
        xds = xdt * decay_st
        db_acc = _dot(xds.astype(BF16), ds16, NN)
        dxdt_off = g1 * decay_st
        t_exp = g1 * xds
        dcs_exp = dy * cs_t * exp_cs - t_exp
        decay_c = _decay_col(cs_cols)
        dstate[g] = decay_c * ds_next + _dot(dyo.astype(BF16), cb16, TN)
        dlast_col = jnp.sum(ds_next * s_prev, axis=1, keepdims=True) * decay_c
        jj = lax.broadcasted_iota(jnp.int32, (GROUP_DIM, LANES), 0)
        ll = lax.broadcasted_iota(jnp.int32, (GROUP_DIM, LANES), 1)
        sel = ll == HEADS_PER_GROUP * g + (jj >> 6)
        dlast = jnp.sum(jnp.where(sel, dlast_col, 0.0), axis=0, keepdims=True)
        t_all = _heads_to_lanes(t_exp, g)
        dlast += jnp.sum(t_all, axis=0, keepdims=True)
        dcs_all = _heads_to_lanes(dcs_exp, g)

        cbm = _dot(cb16, bb, NT)
        dcb = jnp.zeros((CHUNK, CHUNK), F32)
        dcs_rows = jnp.zeros((LANES, CHUNK), F32)
        lane_l = lax.broadcasted_iota(jnp.int32, (CHUNK, LANES), 1)
        sub_l = lax.broadcasted_iota(jnp.int32, (LANES, CHUNK), 0)
        dxdt_pairs = []
        for p in range(HEADS_PER_GROUP // 2):
            xpair16 = xdt[:, LANES * p:LANES * (p + 1)].astype(BF16)
            dypair = dy[:, LANES * p:LANES * (p + 1)]
            acc = None
            for r in (2 * p, 2 * p + 1):
                lm = _decay_mat(cs_cols[r], cs_rows[r])
                m = cbm * lm
                dyh = _head_mask(dypair, r % 2 == 1).astype(BF16)
                dm = _dot(dyh, xpair16, NT)
                dcb += dm * lm
                q = dm * m
                idx = HEADS_PER_GROUP * g + r
                dcs_all += jnp.where(lane_l == idx, jnp.sum(q, axis=1, keepdims=True), 0.0)
                dcs_rows -= jnp.where(sub_l == idx, jnp.sum(q, axis=0, keepdims=True), 0.0)
                part = _dot(m.astype(BF16), dyh, TN)
                acc = part if acc is None else acc + part
            dxdt_pairs.append(acc)
        dxdt = jnp.concatenate(dxdt_pairs, axis=1) + dxdt_off
        dcb16 = dcb.astype(BF16)
        dc_ref[...] = dc_acc + _dot(dcb16, bb, NN)
        db_ref[...] = db_acc + _dot(dcb16, cb16, TN)
        dxs_ref[...] = dxdt * dt_exp + dy * d_exp

        dcs_all += dcs_rows.T
        row = lax.broadcasted_iota(jnp.int32, (CHUNK, CHUNK), 0)
        col = lax.broadcasted_iota(jnp.int32, (CHUNK, CHUNK), 1)
        last_row = lax.broadcasted_iota(jnp.int32, (CHUNK, LANES), 0) == CHUNK - 1
        dcs_all += jnp.where(last_row, dlast, 0.0)
        da_all = _dot_01(dcs_all, (col >= row).astype(BF16), True, 3)
        dta = dtall[...]
        in_group = jnp.logical_and(lane_l >= HEADS_PER_GROUP * g, lane_l < HEADS_PER_GROUP * (g + 1))
        ddt = jnp.where(in_group, da_all * a_ref[...] + _heads_to_lanes(dxdt * xs, g), 0.0)
        da_ref[...] += jnp.sum(jnp.where(in_group, da_all * dta, 0.0), axis=0, keepdims=True)
        ddt_raw = ddt * _sigmoid(dt_ref[...] + bias_ref[...])
        ddt_ref[...] += ddt_raw
        dbias_ref[...] += jnp.sum(ddt_raw, axis=0, keepdims=True)

    gn = SSM_GROUPS * SSM_STATE
    wide = pl.BlockSpec((CHUNK, D_INNER), lambda c: (nc - 1 - c, 0))
    st = pl.BlockSpec((CHUNK, gn), lambda c: (nc - 1 - c, 0))
    par = pl.BlockSpec((1, LANES), lambda c: (0, 0))
    dtb = pl.BlockSpec((CHUNK, LANES), lambda c: (nc - 1 - c, 0))
    f = lambda shape: _sds(shape, F32)
    return _pcall(
        body,
        out_shape=[f((s, D_INNER)), f((s, gn)), f((s, gn)),
                   f((s, D_INNER)), f((s, LANES)), f((8, GROUP_DIM)), f((1, LANES)), f((1, LANES)), f((1, LANES))],
        grid=(nc,),
        in_specs=[wide, wide,
                  pl.BlockSpec((CHUNK, gn), lambda c: (nc - 1 - c, D_INNER // gn)),
                  pl.BlockSpec((CHUNK, gn), lambda c: (nc - 1 - c, D_INNER // gn + 1)),
                  wide, wide,
                  pl.BlockSpec((1, SSM_GROUPS, GROUP_DIM, SSM_STATE), lambda c: (nc - 1 - c, 0, 0, 0)),
                  dtb, par, par, par,
                  pl.BlockSpec((1, D_INNER), lambda c: (0, 0))],
        out_specs=[wide, st, st, wide, dtb, pl.BlockSpec((8, GROUP_DIM), lambda c: (0, 0)), par, par, par],
        scratch_shapes=[pltpu.VMEM((SSM_GROUPS, GROUP_DIM, SSM_STATE), F32),
                        pltpu.VMEM((CHUNK, LANES), F32), pltpu.VMEM((CHUNK, LANES), F32),
                        pltpu.VMEM((LANES, CHUNK), F32)],
        args=[dyn, act, act, act, z, y_pre, states, dtp, bias_p, a_p, d_p, normw], name=name, ride=ride)


def _attn_probs(q, kp, kc, sink, n):
    sp = _dot(q, kp, NT)
    sc = _dot(q, kc, NT)
    i = lax.broadcasted_iota(jnp.int32, sp.shape, 0) & (WINDOW - 1)
    j = lax.broadcasted_iota(jnp.int32, sp.shape, 1)
    sp = jnp.where(jnp.logical_and(j > i, n > 0), sp, NEG)
    sc = jnp.where(j <= i, sc, NEG)
    m = jnp.maximum(jnp.maximum(jnp.max(sp, axis=1, keepdims=True), jnp.max(sc, axis=1, keepdims=True)), sink)
    pp = jnp.exp(sp - m)
    pc = jnp.exp(sc - m)
    ps = jnp.exp(sink - m)
    inv = 1.0 / (jnp.sum(pp, axis=1, keepdims=True) + jnp.sum(pc, axis=1, keepdims=True) + ps)
    return pp * inv, pc * inv, ps * inv


def attn_fwd(qt, kt, vt, sink_rows, name, ride=None):
    s = qt.shape[1]
    nb = s // WINDOW
    rows = Q_PER_KV * WINDOW

    def body(q_ref, kp_ref, kc_ref, vp_ref, vc_ref, sk_ref, o_ref):
        n = pl.program_id(0)
        for h in range(N_KV_HEADS):
            heads = pl.ds(h * Q_PER_KV, Q_PER_KV)
            q = q_ref[heads].reshape(rows, ATT_HEAD_DIM)
            pp, pc, _ = _attn_probs(q, kp_ref[h], kc_ref[h], sk_ref[h], n)
            o = _dot(pp.astype(BF16), vp_ref[h]) + _dot(pc.astype(BF16), vc_ref[h])
            o_ref[heads] = o.reshape(Q_PER_KV, WINDOW, ATT_HEAD_DIM).astype(BF16)

    qsp = pl.BlockSpec((N_Q_HEADS, WINDOW, ATT_HEAD_DIM), lambda n: (0, n, 0))
    prev = pl.BlockSpec((N_KV_HEADS, WINDOW, ATT_HEAD_DIM), lambda n: (0, jnp.maximum(n - 1, 0), 0))
    cur = pl.BlockSpec((N_KV_HEADS, WINDOW, ATT_HEAD_DIM), lambda n: (0, n, 0))
    return _pcall(body, out_shape=[_sds(qt.shape, BF16)], grid=(nb,),
                  in_specs=[qsp, prev, cur, prev, cur, pl.BlockSpec((N_KV_HEADS, rows, 1), lambda n: (0, 0, 0))],
                  out_specs=[qsp], args=[qt, kt, kt, vt, vt, sink_rows], name=name, ride=ride)


def attn_bwd(qt, kt, vt, sink_rows, dot_, name, ride=None):
    s = qt.shape[1]
    nb = s // WINDOW
    rows = Q_PER_KV * WINDOW

    def body(q_ref, kp_ref, kc_ref, vp_ref, vc_ref, sk_ref, do_ref, dq_ref, dk_ref, dv_ref, ds_ref, kacc, vacc):
        n = pl.program_id(0)

        @pl.when(n == 0)
        def _():
            kacc[...] = jnp.zeros(kacc.shape, F32)
            vacc[...] = jnp.zeros(vacc.shape, F32)

        @pl.when(n < nb)
        def _():
            for h in range(N_KV_HEADS):
                heads = pl.ds(h * Q_PER_KV, Q_PER_KV)
                q = q_ref[heads].reshape(rows, ATT_HEAD_DIM)
                do = do_ref[heads].reshape(rows, ATT_HEAD_DIM)
                kp, kc, vp, vc = kp_ref[h], kc_ref[h], vp_ref[h], vc_ref[h]
                pp, pc, ps = _attn_probs(q, kp, kc, sk_ref[h], n)
                dpp = _dot(do, vp, NT)
                dpc = _dot(do, vc, NT)
                delta = jnp.sum(pp * dpp, axis=1, keepdims=True) + jnp.sum(pc * dpc, axis=1, keepdims=True)
                dsp = (pp * (dpp - delta)).astype(BF16)
                dsc = (pc * (dpc - delta)).astype(BF16)
                dq = _dot(dsp, kp) + _dot(dsc, kc)
                dq_ref[heads] = dq.reshape(Q_PER_KV, WINDOW, ATT_HEAD_DIM)
                dk_ref[h] = kacc[h] + _dot(dsp, q, TN)
                dv_ref[h] = vacc[h] + _dot(pp.astype(BF16), do, TN)
                kacc[h] = _dot(dsc, q, TN)
                vacc[h] = _dot(pc.astype(BF16), do, TN)
                dsk = -ps * delta
                sub = lax.broadcasted_iota(jnp.int32, (8, LANES), 0)
                tile = jnp.zeros((8, LANES), F32)
                for j in range(Q_PER_KV):
                    tile += jnp.where(sub == j, jnp.sum(dsk[j * WINDOW:(j + 1) * WINDOW, :], axis=0, keepdims=True),
                                      0.0)
                ds_ref[h, 0] = tile

        @pl.when(n == nb)
        def _():
            dk_ref[...] = kacc[...]
            dv_ref[...] = vacc[...]
            ds_ref[...] = jnp.zeros(ds_ref.shape, F32)

    last = nb - 1
    qsp = pl.BlockSpec((N_Q_HEADS, WINDOW, ATT_HEAD_DIM), lambda n: (0, jnp.minimum(n, last), 0))
    prev = pl.BlockSpec((N_KV_HEADS, WINDOW, ATT_HEAD_DIM), lambda n: (0, jnp.clip(n - 1, 0, last), 0))
    cur = pl.BlockSpec((N_KV_HEADS, WINDOW, ATT_HEAD_DIM), lambda n: (0, jnp.minimum(n, last), 0))
    dkv = pl.BlockSpec((N_KV_HEADS, WINDOW, ATT_HEAD_DIM), lambda n: (0, jnp.maximum(n - 1, 0), 0))
    f = lambda shape: _sds(shape, F32)
    acc = pltpu.VMEM((N_KV_HEADS, WINDOW, ATT_HEAD_DIM), F32)
    return _pcall(
        body, out_shape=[f(qt.shape), f(kt.shape), f(vt.shape), f((N_KV_HEADS, nb + 1, 8, LANES))],
        grid=(nb + 1,),
        in_specs=[qsp, prev, cur, prev, cur, pl.BlockSpec((N_KV_HEADS, rows, 1), lambda n: (0, 0, 0)), qsp],
        out_specs=[qsp, dkv, dkv, pl.BlockSpec((N_KV_HEADS, 1, 8, LANES), lambda n: (0, n, 0, 0))],
        scratch_shapes=[acc, acc], args=[qt, kt, kt, vt, vt, sink_rows, dot_], name=name, ride=ride)


def loss_head(x, w, tgt, name):
    s, d = x.shape
    tm = _row_tile(s, 256)

    def body(x_ref, w_ref, t_ref, loss_ref, dx_ref, dw_ref):
        i = pl.program_id(0)
        xf = x_ref[...]
        wv = w_ref[...]
        r = lax.rsqrt(jnp.mean(xf * xf, axis=-1, keepdims=True) + EPS)
        xhat = xf * r
        e = xhat * wv - t_ref[...]
        part = 0.5 * jnp.sum(jnp.mean(e * e, axis=-1, keepdims=True), axis=0, keepdims=True)
        dy = e * (1.0 / d)
        dxhat = dy * wv
        dx_ref[...] = r * (dxhat - xhat * jnp.mean(dxhat * xhat, axis=-1, keepdims=True))
        col = jnp.sum(dy * xhat, axis=0, keepdims=True)

        @pl.when(i == 0)
        def _():
            loss_ref[...] = jnp.broadcast_to(part, (1, LANES))
            dw_ref[...] = col

        @pl.when(i > 0)
        def _():
            loss_ref[...] += jnp.broadcast_to(part, (1, LANES))
            dw_ref[...] += col

    row = pl.BlockSpec((tm, d), lambda i: (i, 0))
    vec = pl.BlockSpec((1, d), lambda i: (0, 0))
    return _pcall(body, out_shape=[_sds((1, LANES), F32), _sds((s, d), F32), _sds((1, d), F32)], grid=(s // tm,),
                  in_specs=[row, vec, row], out_specs=[pl.BlockSpec((1, LANES), lambda i: (0, 0)), row, vec],
                  args=[x, w.reshape(1, d), tgt], name=name)


ELEMWISE_TILE = 720 * 1024


def _tile_rows(r, c, max_elems=262144, mult=16):
    best = None
    for t in range(mult, r + 1, mult):
        if r % t == 0 and t * c <= max_elems:
            best = t
    return best or r


def add_pair(xhs, ps, c_idx, name):
    n = len(xhs)
    _, r, c = xhs[0].shape
    tr = _tile_rows(r, c, max_elems=ELEMWISE_TILE)

    def body(c_ref, *refs):
        for x_ref, p_ref, o_ref in zip(refs[:n], refs[n:2 * n], refs[2 * n:]):
            o_ref[...] = (x_ref[0].astype(F32) + p_ref[...].astype(F32)).astype(BF16)

    blk = pl.BlockSpec((tr, c), lambda i, cr: (i, 0))
    return pl.pallas_call(
        body, out_shape=tuple([_sds((r, c), BF16)] * n),
        grid_spec=pltpu.PrefetchScalarGridSpec(
            num_scalar_prefetch=1, grid=(r // tr,),
            in_specs=[pl.BlockSpec((1, tr, c), lambda i, cr: (cr[0], i, 0))] * n + [blk] * n,
            out_specs=tuple([blk] * n)),
        name=name, compiler_params=_cp(1))(c_idx, *xhs, *ps)


def sum_chips(qs, owns, chip_idx, name):
    n = len(qs)
    _, r, c = qs[0].shape
    tr = _tile_rows(r, c, max_elems=ELEMWISE_TILE // max(1, n - 1))

    def body(k_ref, *refs):
        k = k_ref[0]
        for q_ref, own_ref, o_ref in zip(refs[:n], refs[n:2 * n], refs[2 * n:]):
            mine = own_ref[0].astype(F32)
            tot = None
            for j in range(N_CHIPS):
                term = jnp.where(k == j, mine, q_ref[j].astype(F32))
                tot = term if tot is None else tot + term
            o_ref[...] = tot

    return pl.pallas_call(
        body, out_shape=tuple([_sds((r, c), F32)] * n),
        grid_spec=pltpu.PrefetchScalarGridSpec(
            num_scalar_prefetch=1, grid=(r // tr,),
            in_specs=([pl.BlockSpec((N_CHIPS, tr, c), lambda i, kr: (0, i, 0))] * n
                      + [pl.BlockSpec((1, tr, c), lambda i, kr: (kr[0], i, 0))] * n),
            out_specs=tuple([pl.BlockSpec((tr, c), lambda i, kr: (i, 0))] * n)),
        name=name, compiler_params=_cp(1))(chip_idx, *qs, *owns)


def adamw(w, g, m, v, name):
    r, c = w.shape
    tr = _tile_rows(r, c, mult=8)
    c1 = 1.0 / (1.0 - ADAM_B1 ** ADAM_STEP)
    c2 = 1.0 / (1.0 - ADAM_B2 ** ADAM_STEP)

    def body(w_ref, g_ref, m_ref, v_ref, d_ref, mo_ref, vo_ref):
        gf = g_ref[...]
        mn = ADAM_B1 * m_ref[...] + (1.0 - ADAM_B1) * gf
        vn = ADAM_B2 * v_ref[...] + (1.0 - ADAM_B2) * (gf * gf)
        mo_ref[...] = mn
        vo_ref[...] = vn
        d_ref[...] = -ADAM_LR * ((mn * c1) / (jnp.sqrt(vn * c2) + ADAM_EPS) + ADAM_WD * w_ref[...])

    blk = pl.BlockSpec((tr, c), lambda i: (i, 0))
    out = _sds((r, c), F32)
    return _pcall(body, out_shape=[out, out, out], grid=(r // tr,), in_specs=[blk] * 4, out_specs=[blk] * 3,
                  args=[w, g, m, v], name=name)


WEIGHTS = ['norm_w', 'ffn_w_gate', 'ffn_w_up', 'ffn_w_down', 'ssm_w_in', 'ssm_conv_w', 'ssm_conv_b', 'ssm_dt_bias',
           'ssm_a_log', 'ssm_d', 'ssm_norm_w', 'ssm_w_out', 'kv_norm_w', 'w_k', 'b_k', 'w_v', 'b_v', 'attn_w_q',
           'attn_b_q', 'attn_sinks', 'attn_w_o', 'attn_b_o', 'final_norm_w']
BIG = ['ffn_w_gate', 'ffn_w_up', 'ffn_w_down', 'ssm_w_in', 'ssm_w_out', 'w_k', 'w_v', 'attn_w_q', 'attn_w_o']
TRANSPOSED = ('ffn_w_gate', 'ffn_w_up', 'ssm_w_in')
SMALL = [n for n in WEIGHTS if n not in BIG]
SMALL_SHARDED = {'norm_w': 2, 'ssm_conv_w': 2, 'ssm_conv_b': 1, 'ssm_norm_w': 1}
ROW_ALIGN = 8 * LANES


def _pack_rows(parts):
    flat = jnp.concatenate([p.reshape(-1).astype(F32) for p in parts])
    pad = (-flat.size) % ROW_ALIGN
    return jnp.pad(flat, (0, pad)).reshape(-1, LANES)


def _unpack_rows(buf, shapes):
    flat = buf.reshape(-1)
    out, pos = [], 0
    for shp in shapes:
        size = math.prod(shp)
        out.append(flat[pos:pos + size].reshape(shp))
        pos += size
    return out


def _as2d(a):
    return a.reshape(-1, a.shape[-1])


def _heads_major(t, n_heads):
    s = t.shape[0]
    return t.reshape(s, n_heads, ATT_HEAD_DIM).transpose(1, 0, 2)


def _tokens_major(t):
    h, s, dh = t.shape
    return t.transpose(1, 0, 2).reshape(s, h * dh)


def _pad_lanes(v):
    return jnp.pad(v.reshape(1, -1), ((0, 0), (0, LANES - v.size)))


def _chips_first(t):
    return t.swapaxes(0, 1).reshape((-1,) + t.shape[3:])


def _parts_first(t, rows):
    return t.reshape((N_CHIPS, N_CORES, rows) + t.shape[1:]).swapaxes(0, 1)


def kernel(*args):
    names = (['x'] + WEIGHTS + ['loss_target'] + ['m_' + n for n in WEIGHTS] + ['v_' + n for n in WEIGHTS])
    a = dict(zip(names, args))
    for n in TRANSPOSED:
        for pre in ('', 'm_', 'v_'):
            a[pre + n] = a[pre + n].swapaxes(-1, -2)
    xi, yi, ci = lax.axis_index("x"), lax.axis_index("y"), lax.axis_index("c")
    chip = 2 * xi + yi
    south = ci == 0
    c_idx = jnp.reshape(ci, (1,)).astype(jnp.int32)
    chip_idx = jnp.reshape(chip, (1,)).astype(jnp.int32)
    x0 = a['x'][0]
    s = x0.shape[0]
    cos, sin = rope_tables(s)

    def own_slot(full, mine):
        return lax.dynamic_update_slice_in_dim(full, mine[:, None], chip, axis=1)

    def ffn_shard(l, i):
        return [a[n][l, i].astype(BF16).reshape(N_CORES, FF_PART, D_MODEL)
                for n in ('ffn_w_gate', 'ffn_w_up', 'ffn_w_down')]

    def own_slots(fulls, mines):
        return [own_slot(f, m) for f, m in zip(fulls, mines)]
    small_names = list(SMALL_SHARDED)
    small_sh = _pack_rows([a[n] for n in small_names])
    small_sh = small_sh.reshape(N_CORES, small_sh.shape[0] // 2, LANES)
    sh00, sh01, sh10, sh11 = ffn_shard(0, 0), ffn_shard(0, 1), ffn_shard(1, 0), ffn_shard(1, 1)
    w_in_sh = jnp.pad(a['ssm_w_in'][0], ((0, IN_SHARD_PAD - IN_SHARD), (0, 0))).astype(BF16).reshape(
        N_CORES, IN_SHARD_PAD // 2, D_MODEL)
    w_out_sh = a['ssm_w_out'][0].astype(BF16).reshape(N_CORES, 256, D_MODEL)
    attn_sh = jnp.stack([a['attn_w_q'][0], a['attn_w_o'][0]]).astype(BF16)
    kv_sh = jnp.stack([a['w_k'], a['w_v']]).astype(BF16)
    in_flight, all_started = split_start(
        [sh00 + [small_sh], [w_in_sh, kv_sh], [w_out_sh], sh01, sh10, [attn_sh], sh11], "gather", "gather_start")

    def arrive(idx, after, tag):
        return forward_cores(split_arrive(in_flight[idx], "gather", after, "gather_arrive_" + tag))

    first = run_exchange(arrive(0, all_started, "first"), "gather_hop_first")
    w00 = own_slots(first[:3], sh00)
    smalls = own_slot(first[3], small_sh)
    p = {}
    per_chip = [_unpack_rows(smalls[:, k], [a[n].shape for n in small_names]) for k in range(N_CHIPS)]
    for idx, n in enumerate(small_names):
        p[n] = jnp.concatenate([per_chip[k][idx] for k in range(N_CHIPS)], axis=SMALL_SHARDED[n])
    nw = p['norm_w']
    conv_w, conv_b, ssm_nw = p['ssm_conv_w'][0], p['ssm_conv_b'][0], p['ssm_norm_w'][0].reshape(1, D_INNER)

    h00 = rmsnorm_fwd(x0, nw[0, 0], "norm_in")
    x1, h01, gu00 = ffn_fwd(h00, x0, *w00, [nw[0, 1]], "ffn_fwd_00")
    w_in_g, kv_g = run_exchange(arrive(1, x1, "in"), "gather_hop_in")
    w_in_t = _chips_first(own_slot(w_in_g, w_in_sh)).reshape(N_CHIPS, IN_SHARD_PAD, D_MODEL)[:, :IN_SHARD].reshape(
        IN_PROJ_DIM, D_MODEL)
    w_dt_t = jnp.pad(w_in_t[D_INNER + CONV_DIM:], ((0, LANES - SSM_HEADS), (0, 0)))
    kv_g = own_slot(kv_g, kv_sh)
    w_k, w_v = kv_g[0].reshape(D_MODEL, KV_DIM), kv_g[1].reshape(D_MODEL, KV_DIM)

    zz = mm_nt(h01, w_in_t, "ssm_in_z", n=D_INNER)
    xbc = mm_nt(h01, w_in_t, "ssm_in_xbc", n=CONV_DIM, row0=D_INNER)
    dtp = mm_nt(h01, w_dt_t, "ssm_in_dt")
    act = conv_fwd(xbc, conv_w, conv_b, "ssm_conv")
    bias_p = _pad_lanes(a['ssm_dt_bias'][0])
    a_p = _pad_lanes(-jnp.exp(a['ssm_a_log'][0]))
    d_p = _pad_lanes(a['ssm_d'][0])
    (yn, y_pre, states), (w_out_g,) = ssd_fwd(act, zz, dtp, bias_p, a_p, d_p, ssm_nw, "ssd_fwd",
                                              ride=arrive(2, act, "out"))
    w_out = _chips_first(own_slot(w_out_g, w_out_sh))
    (x2, h02), w01 = mm_res(yn, w_out, x1, "ssm_out", norm_ws=[nw[0, 2]], ride=arrive(3, yn, "01"))
    w01 = own_slots(w01, sh01)
    x3, hkv, h10, gu01 = ffn_fwd(h02, x2, *w01, [a['kv_norm_w'], nw[1, 0]], "ffn_fwd_01")
    w10 = own_slots(run_exchange(arrive(4, x3, "10"), "gather_hop_10"), sh10)

    k_rot = rope_apply(mm_nn(hkv, w_k, "kv_k", bias=a['b_k']), cos, sin, "rope_k")
    v = mm_nn(hkv, w_v, "kv_v", bias=a['b_v'], out_dtype=BF16)
    kt = _heads_major(k_rot, N_KV_HEADS)
    vt = _heads_major(v, N_KV_HEADS)

    (x4, h11, gu10), (attn_g,) = ffn_fwd(h10, x3, *w10, [nw[1, 1]], "ffn_fwd_10", ride=arrive(5, v, "attn"))
    attn_g = own_slot(attn_g, attn_sh)
    w_q, w_o = attn_g[0].reshape(D_MODEL, D_MODEL), attn_g[1].reshape(D_MODEL, D_MODEL)
    scale = 1.0 / math.sqrt(ATT_HEAD_DIM)
    q_rot = rope_apply(mm_nn(h11, w_q, "attn_q", bias=a['attn_b_q'][0]), cos, sin, "rope_q", scale=scale)
    qt = _heads_major(q_rot, N_Q_HEADS)
    sink_rows = jnp.repeat(a['attn_sinks'][0].reshape(N_KV_HEADS, Q_PER_KV), WINDOW, axis=1).reshape(
        N_KV_HEADS, Q_PER_KV * WINDOW, 1)
    (ot,) = attn_fwd(qt, kt, vt, sink_rows, "attn_fwd")
    o = _tokens_major(ot)
    (x5, h12), w11 = mm_res(o, w_o, x4, "attn_out", bias=a['attn_b_o'][0], norm_ws=[nw[1, 2]],
                            ride=arrive(6, ot, "11"))
    w11 = own_slots(w11, sh11)
    x6, gu11 = ffn_fwd(h12, x5, *w11, [], "ffn_fwd_11")

    loss_v, dx6, d_final = loss_head(x6, a['final_norm_w'], a['loss_target'][0], "loss_head")
    g = {'final_norm_w': d_final[0]}

    def same_shape(xs, ys):
        runs = []
        for xv, yv in zip(xs, ys):
            if runs and runs[-1][0][0].shape == xv.shape:
                runs[-1][0].append(xv)
                runs[-1][1].append(yv)
            else:
                runs.append(([xv], [yv]))
        return runs

    def pre_reduce(grads, sib, tag):
        out = []
        for idx, (grp, sbs) in enumerate(same_shape(grads, list(sib))):
            ts = add_pair([gr.reshape(2, -1, gr.shape[-1]) for gr in grp], [_as2d(sb) for sb in sbs], c_idx,
                          "rs_add_%s_%d" % (tag, idx))
            out += [t.reshape(gr.shape[1:]) for t, gr in zip(ts, grp)]
        return out

    def chip_sum(landed, parts, tag):
        out = []
        for idx, (qs, owns) in enumerate(same_shape(list(landed), parts)):
            ts = sum_chips([q.reshape(N_CHIPS, -1, q.shape[-1]) for q in qs],
                           [own.reshape(N_CHIPS, -1, own.shape[-1]) for own in owns], chip_idx,
                           "rs_sum_%s_%d" % (tag, idx))
            out += [t.reshape(q.shape[1:]) for t, q in zip(ts, qs)]
        return out

    dnw = [[None] * 3 for _ in range(2)]
    sums = {}

    def trade(key):
        return swap_cores(sums[key], False)

    dx5, dnw12, *g11 = ffn_bwd(dx6, h12, x5, nw[1, 2], gu11, *w11, "ffn_bwd_11")
    dnw[1][2] = dnw12[0]
    (d_wo, g['attn_b_o']), sib11 = mm_tn(o, dx5, "attn_dwo", col_sum=True, ride=swap_cores(g11, True))
    t11 = pre_reduce(g11, sib11, "11")
    do = mm_nt(dx5, w_o, "attn_do", out_dtype=BF16)
    (dqt, dkt, dvt, dsink), land11 = attn_bwd(qt, kt, vt, sink_rows, _heads_major(do, N_Q_HEADS), "attn_bwd",
                                             ride=scatter_chips(t11[:2]))
    g['attn_sinks'] = jnp.sum(dsink[:, :, :Q_PER_KV, 0], axis=1).reshape(N_Q_HEADS)
    dq_pre = rope_apply(_tokens_major(dqt), cos, sin, "rope_dq", inverse=True, scale=scale, out_dtype=F32)
    d_wq, g['attn_b_q'] = mm_tn(h11, dq_pre, "attn_dwq", col_sum=True)
    g_attn = [jnp.stack([d_wq.reshape(N_CHIPS, 256, D_MODEL), d_wo.reshape(N_CHIPS, 256, D_MODEL)])]
    (dx4, dnw[1][1]), sib_attn = mm_rms_bwd([(dq_pre, 0, w_q, 0, D_MODEL, "nt")], dx5, x4, nw[1, 1], "attn_bwd_dh",
                                            ride=swap_cores(g_attn, True))
    t_attn = pre_reduce(g_attn, sib_attn, "attn")
    (dx3, dnw10, *g10), landed = ffn_bwd(dx4, h10, x3, nw[1, 0], gu10, *w10, "ffn_bwd_10",
                                         ride=scatter_chips(t_attn + t11[2:]))
    dnw[1][0] = dnw10[0]
    sums['attn'] = chip_sum(landed[:1], t_attn, "attn")
    sums['11'] = chip_sum(list(land11) + list(landed[1:]), t11, "11")
    dk_pre = rope_apply(_tokens_major(dkt), cos, sin, "rope_dk", inverse=True, out_dtype=F32)
    dv = _tokens_major(dvt)
    (d_wk, g['b_k']), sib10 = mm_tn(hkv, dk_pre, "kv_dwk", col_sum=True, ride=swap_cores(g10, True))
    t10 = pre_reduce(g10, sib10, "10")
    d_wv, g['b_v'] = mm_tn(hkv, dv, "kv_dwv", col_sum=True)
    g_kv = [jnp.stack([d_wk.reshape(N_CHIPS, 256, KV_DIM), d_wv.reshape(N_CHIPS, 256, KV_DIM)])]
    (dx3, g['kv_norm_w']), sib_kv = mm_rms_bwd(
        [(dk_pre, 0, w_k, 0, KV_DIM, "nt"), (dv, 0, w_v, 0, KV_DIM, "nt")], dx3, x3, a['kv_norm_w'], "kv_bwd_dh",
        ride=swap_cores(g_kv, True))
    t_kv = pre_reduce(g_kv, sib_kv, "kv")
    (dx2, dnw02, *g01), landed = ffn_bwd(dx3, h02, x2, nw[0, 2], gu01, *w01, "ffn_bwd_01",
                                         ride=join(scatter_chips(t10 + t_kv), trade('11'), trade('attn')))
    dnw[0][2] = dnw02[0]
    sums['10'] = chip_sum(landed[:3], t10, "10")
    sums['kv'] = chip_sum(landed[3:4], t_kv, "kv")
    theirs = {'11': landed[4:7], 'attn': landed[7:]}
    d_wout, sib01 = mm_tn(yn, dx2, "ssm_dwout", ride=swap_cores(g01, True))
    t01 = pre_reduce(g01, sib01, "01")
    dyn = mm_nt(dx2, w_out, "ssm_dyn")
    (dxs, db_, dc_, dz, ddt, d_ssm_nw, d_bias, d_a, d_d), landed = ssd_bwd(
        dyn, act, zz, y_pre, states, dtp, bias_p, a_p, d_p, ssm_nw, "ssd_bwd",
        ride=join(scatter_chips(t01[:2]), trade('10'), trade('kv')))
    land01 = list(landed[:2])
    theirs['10'], theirs['kv'] = landed[2:5], landed[5:]
    g['ssm_norm_w'] = d_ssm_nw[:SSM_GROUPS].reshape(D_INNER)
    g['ssm_dt_bias'] = d_bias[0, :SSM_HEADS]
    g['ssm_a_log'] = d_a[0, :SSM_HEADS] * a_p[0, :SSM_HEADS]
    g['ssm_d'] = d_d[0, :SSM_HEADS]
    (dxbc, g['ssm_conv_w'], g['ssm_conv_b']), landed = conv_bwd(dxs, db_, dc_, xbc, conv_w, conv_b, "ssm_conv_bwd",
                                                                ride=scatter_chips(t01[2:]))
    sums['01'] = chip_sum(land01 + list(landed), t01, "01")
    d_win = mm_tn(dz, h01, "ssm_dwz", rows=IN_PROJ_DIM)
    d_win = mm_tn(dxbc, h01, "ssm_dwxbc", into=d_win, rows=IN_PROJ_DIM, row0=D_INNER)
    d_win = mm_tn(ddt, h01, "ssm_dwdt", into=d_win, rows=IN_PROJ_DIM, row0=D_INNER + CONV_DIM, m_valid=SSM_HEADS)
    d_win = jnp.pad(d_win.reshape(N_CHIPS, IN_SHARD, D_MODEL), ((0, 0), (0, IN_SHARD_PAD - IN_SHARD), (0, 0)))
    g_ssm = [_parts_first(d_win.reshape(-1, D_MODEL), IN_SHARD_PAD // 2), _parts_first(d_wout, 256)]
    kb = 1024
    terms = ([(dz, j, w_in_t, j, kb, "nn") for j in range(D_INNER // kb)]
             + [(dxbc, j, w_in_t, D_INNER // kb + j, kb, "nn") for j in range(CONV_DIM // kb)]
             + [(ddt, 0, w_dt_t, 0, LANES, "nn")])
    (dx1, dnw[0][1]), sib_ssm = mm_rms_bwd(terms, dx2, x1, nw[0, 1], "ssm_bwd_dh", ride=swap_cores(g_ssm, True))
    t_ssm = pre_reduce(g_ssm, sib_ssm, "ssm")
    (grad_x, dnw00, *g00), landed = ffn_bwd(dx1, h00, x0, nw[0, 0], gu00, *w00, "ffn_bwd_00",
                                            ride=join(scatter_chips(t_ssm), trade('01')))
    dnw[0][0] = dnw00[0]
    sums['ssm'] = chip_sum(landed[:2], t_ssm, "ssm")
    theirs['01'] = landed[2:]
    landed = run_exchange(join(swap_cores(g00, True), trade('ssm')), "rs_swap_00")
    t00 = pre_reduce(g00, landed[:3], "00")
    theirs['ssm'] = landed[3:]

    def both(key):
        return [(jnp.where(south, m_, t_), jnp.where(south, t_, m_)) for m_, t_ in zip(sums[key], theirs[key])]

    g['norm_w'] = jnp.stack([jnp.stack(r) for r in dnw])
    red = all_reduce_small(_pack_rows([g[n] for n in SMALL] + [loss_v[0, :1]]), "reduce_vectors")

    t00 = lax.optimization_barrier((red, t00))[1]
    (flight00,), flying = split_start([t00], "scatter", "rs_scatter_00_start")

    def held(val):
        return lax.optimization_barrier((flying, val))[1]

    delta, new_m, new_v, gw = {}, {}, {}, {}
    ffn_names = ('ffn_w_gate', 'ffn_w_up', 'ffn_w_down')
    full = {key: both(key) for key in ('attn', 'kv', 'ssm')}
    lo, hi = full['attn'][0]
    gw['attn_w_q'], gw['attn_w_o'] = lo[None], hi[None]
    lo, hi = full['kv'][0]
    gw['w_k'], gw['w_v'] = lo, hi
    lo, hi = full['ssm'][0]
    gw['ssm_w_in'] = jnp.concatenate([lo, hi], axis=0)[:IN_SHARD][None]
    lo, hi = full['ssm'][1]
    gw['ssm_w_out'] = jnp.concatenate([lo, hi], axis=0)[None]

    *small_sums, loss = _unpack_rows(red, [g[n].shape for n in SMALL] + [()])
    for n, t in zip(SMALL, small_sums):
        if n in SMALL_SHARDED:
            ax = SMALL_SHARDED[n] - (a[n].ndim - t.ndim)
            width = a[n].shape[SMALL_SHARDED[n]]
            t = lax.dynamic_slice_in_dim(t, chip * width, width, axis=ax)
        gw[n] = t.reshape(a[n].shape)

    def update(n):
        d, mo, vo = adamw(_as2d(a[n]), held(_as2d(gw[n])), _as2d(a['m_' + n]), _as2d(a['v_' + n]), "adamw_" + n)
        delta[n], new_m[n], new_v[n] = d.reshape(a[n].shape), mo.reshape(a[n].shape), vo.reshape(a[n].shape)

    for n in BIG:
        if n not in ffn_names:
            update(n)
    shapes = [a[n].shape for n in SMALL]
    packed = [_pack_rows([src[n] for n in SMALL]) for src in
              (a, gw, {n: a['m_' + n] for n in SMALL}, {n: a['v_' + n] for n in SMALL})]
    outs = adamw(*packed, "adamw_vectors")
    for dst, buf in zip((delta, new_m, new_v), outs):
        for n, t in zip(SMALL, _unpack_rows(buf, shapes)):
            dst[n] = t
    for key in ('01', '10', '11'):
        sums[key] = held(list(sums[key]))
    rest = [both(key) for key in ('01', '10', '11')]
    done = lax.optimization_barrier((outs[0], [delta[n] for n in BIG if n not in ffn_names], rest))[0]
    land00 = split_arrive(flight00, "scatter", done, "rs_scatter_00_arrive")
    sums['00'] = chip_sum(land00, t00, "00")
    theirs['00'] = run_exchange(trade('00'), "rs_trade_00")
    blocks = [both('00')] + rest
    for t, n in enumerate(ffn_names):
        gw[n] = jnp.concatenate([piece for blk in blocks for piece in blk[t]], axis=0).reshape(a[n].shape)
        update(n)
    for n in TRANSPOSED:
        for dst in (gw, delta, new_m, new_v):
            dst[n] = dst[n].swapaxes(-1, -2)

    return (loss, grad_x[None], *[gw[n] for n in WEIGHTS], *[delta[n] for n in WEIGHTS],
            *[new_m[n] for n in WEIGHTS], *[new_v[n] for n in WEIGHTS])
```

```python
import math

import jax
import jax.numpy as jnp
from jax import lax
from jax.experimental import pallas as pl
from jax.experimental.pallas import tpu as pltpu

F32 = jnp.float32
BF16 = jnp.bfloat16

D_MODEL = 1024
D_INNER = 2048
SSM_HEADS = 32
SSM_GROUPS = 4
HEADS_PER_GROUP = SSM_HEADS // SSM_GROUPS
SSM_HEAD_DIM = 64
SSM_STATE = 128
GROUP_DIM = D_INNER // SSM_GROUPS
CONV_DIM = D_INNER + 2 * SSM_GROUPS * SSM_STATE
CONV_WIDTH = 4
CHUNK = 128
ATT_HEAD_DIM = 64
N_Q_HEADS = 16
N_KV_HEADS = 4
Q_PER_KV = N_Q_HEADS // N_KV_HEADS
KV_DIM = N_KV_HEADS * ATT_HEAD_DIM
WINDOW = 128
ROPE_THETA = 10000.0
D_FF = 2816
N_CHIPS = 4
N_CORES = 2
FF_SHARD = D_FF // N_CHIPS
FF_PART = FF_SHARD // N_CORES
IN_PROJ_DIM = D_INNER + CONV_DIM + SSM_HEADS
IN_SHARD = IN_PROJ_DIM // N_CHIPS
IN_SHARD_PAD = 1312
EPS = 1e-5
NEG = -1e30
LANES = 128
VMEM_LIMIT = 60 * 1024 * 1024

ADAM_LR = 0.001
ADAM_B1 = 0.9
ADAM_B2 = 0.999
ADAM_EPS = 1e-08
ADAM_WD = 0.01
ADAM_STEP = 10

NN = ((1,), (0,))
NT = ((1,), (1,))
TN = ((0,), (0,))
MESH = pl.DeviceIdType.MESH
ANY = pl.BlockSpec(memory_space=pl.ANY)


def _dot(a, b, dims=NN, precision=None):
    return lax.dot_general(a, b, (dims, ((), ())), preferred_element_type=F32, precision=precision)


def _cp(n_grid):
    return pltpu.CompilerParams(dimension_semantics=("arbitrary",) * n_grid, vmem_limit_bytes=VMEM_LIMIT)


def _sigmoid(x):
    return 1.0 / (1.0 + jnp.exp(-x))


def _rms_fwd(xf, w):
    r = lax.rsqrt(jnp.mean(xf * xf, axis=-1, keepdims=True) + EPS)
    return xf * r * w


def _rms_bwd(dh, xf, w):
    r = lax.rsqrt(jnp.mean(xf * xf, axis=-1, keepdims=True) + EPS)
    xhat = xf * r
    dxhat = dh * w
    dx = r * (dxhat - xhat * jnp.mean(dxhat * xhat, axis=-1, keepdims=True))
    return dx, dh * xhat


def _row_tile(s, pref):
    return pref if s % pref == 0 else s


def _col_tile(n):
    for t in (1024, 768, 512, 256, 128):
        if n % t == 0:
            return t
    return n


def _sds(shape, dtype):
    return jax.ShapeDtypeStruct(tuple(shape), dtype)


class Exchange:
    def __init__(self, ins, out_shapes, sems, start, finish, inplace=False):
        self.ins, self.out_shapes, self.sems, self.start, self.finish = ins, out_shapes, sems, start, finish
        self.inplace = inplace


def _place():
    x, y, c = lax.axis_index("x"), lax.axis_index("y"), lax.axis_index("c")
    others = [(1 - x, y), (x, 1 - y), (1 - x, 1 - y)]
    return x, y, c, 2 * x + y, others


def _rc(src, dst, send_sem, recv_sem, dev):
    return pltpu.make_async_remote_copy(src_ref=src, dst_ref=dst, send_sem=send_sem, recv_sem=recv_sem,
                                        device_id=dev, device_id_type=MESH)


def scatter_chips(arrs):
    n = len(arrs)

    def copies(ins, outs, sems):
        send, recv = sems
        x, y, c, k, others = _place()
        out, land = [], []
        for a in range(n):
            for j, (px, py) in enumerate(others):
                out.append(_rc(ins[a].at[2 * px + py], outs[a].at[k], send.at[a, j], recv.at[a, j], (px, py, c)))
                blk = outs[a].at[2 * px + py]
                land.append(_rc(blk, blk, send.at[a, j], recv.at[a, j], (px, py, c)))
        return out, land

    def start(ins, outs, sems):
        for cp in copies(ins, outs, sems)[0]:
            cp.start()

    def finish(ins, outs, sems):
        out, land = copies(ins, outs, sems)
        for arrived in land:
            arrived.wait_recv()
        for cp in out:
            cp.wait_send()

    return Exchange(list(arrs), [_sds(a.shape, a.dtype) for a in arrs],
                    [pltpu.SemaphoreType.DMA((n, 3)), pltpu.SemaphoreType.DMA((n, 3))], start, finish)


def swap_cores(arrs, pick_other):
    n = len(arrs)

    def copies(ins, outs, sems):
        send, recv = sems
        x, y, c, _, _ = _place()
        return [_rc(ins[a].at[1 - c] if pick_other else ins[a], outs[a], send.at[a], recv.at[a], (x, y, 1 - c))
                for a in range(n)]

    def start(ins, outs, sems):
        for cp in copies(ins, outs, sems):
            cp.start()

    def finish(ins, outs, sems):
        for cp in copies(ins, outs, sems):
            cp.wait()

    shapes = [_sds(a.shape[1:] if pick_other else a.shape, a.dtype) for a in arrs]
    return Exchange(list(arrs), shapes, [pltpu.SemaphoreType.DMA((n,)), pltpu.SemaphoreType.DMA((n,))],
                    start, finish)


def join(*parts):
    parts = [p for p in parts if p is not None]
    if not parts:
        return None

    def split(refs, counts):
        out, pos = [], 0
        for cnt in counts:
            out.append(refs[pos:pos + cnt])
            pos += cnt
        return out

    n_in = [len(p.ins) for p in parts]
    n_out = [len(p.out_shapes) for p in parts]
    n_sem = [len(p.sems) for p in parts]

    def run(which):
        def go(ins, outs, sems):
            for p, i, o, s in zip(parts, split(ins, n_in), split(outs, n_out), split(sems, n_sem)):
                getattr(p, which)(i, o, s)
        return go

    return Exchange([a for p in parts for a in p.ins], [s for p in parts for s in p.out_shapes],
                    [s for p in parts for s in p.sems], run("start"), run("finish"))


def _pcall(body, *, out_shape, grid, in_specs, out_specs, args, name, scratch_shapes=(), ride=None, aliases=None):
    out_shape, out_specs, in_specs = tuple(out_shape), tuple(out_specs), list(in_specs)
    aliases = aliases or {}
    if ride is None:
        return pl.pallas_call(body, out_shape=out_shape, grid=grid, in_specs=in_specs, out_specs=out_specs,
                              scratch_shapes=list(scratch_shapes), input_output_aliases=aliases, name=name,
                              compiler_params=_cp(len(grid)))(*args)
    n_in, n_out, n_sc = len(args), len(out_shape), len(scratch_shapes)
    n_xi, n_xo = len(ride.ins), len(ride.out_shapes)

    def wrapped(*refs):
        pos = [0]

        def take(cnt):
            got = refs[pos[0]:pos[0] + cnt]
            pos[0] += cnt
            return got

        c_in, x_in, c_out, x_out, c_sc = take(n_in), take(n_xi), take(n_out), take(n_xo), take(n_sc)
        sems = refs[pos[0]:]
        first, last = True, True
        for d, size in enumerate(grid):
            first = jnp.logical_and(first, pl.program_id(d) == 0)
            last = jnp.logical_and(last, pl.program_id(d) == size - 1)

        @pl.when(first)
        def _():
            ride.start(x_in, x_out, sems)

        body(*c_in, *c_out, *c_sc)

        @pl.when(last)
        def _():
            ride.finish(x_in, x_out, sems)

    if ride.inplace:
        aliases = {**aliases, **{n_in + t: n_out + t for t in range(n_xi)}}
    res = pl.pallas_call(
        wrapped, out_shape=out_shape + tuple(ride.out_shapes), grid=grid,
        in_specs=in_specs + [ANY] * n_xi, out_specs=out_specs + (ANY,) * n_xo,
        scratch_shapes=list(scratch_shapes) + list(ride.sems), input_output_aliases=aliases, name=name,
        compiler_params=_cp(len(grid)))(*args, *ride.ins)
    return res[:n_out], res[n_out:]


def run_exchange(ex, name):
    n_xi, n_xo = len(ex.ins), len(ex.out_shapes)

    def body(*refs):
        ins, outs, sems = refs[:n_xi], refs[n_xi:n_xi + n_xo], refs[n_xi + n_xo:]
        ex.start(ins, outs, sems)
        ex.finish(ins, outs, sems)

    aliases = {t: t for t in range(n_xi)} if ex.inplace else {}
    return pl.pallas_call(body, out_shape=tuple(ex.out_shapes), in_specs=[ANY] * n_xi, out_specs=(ANY,) * n_xo,
                          scratch_shapes=list(ex.sems), input_output_aliases=aliases, name=name)(*ex.ins)


HBM_SPEC = pl.BlockSpec(memory_space=pltpu.HBM)
SEM_SPEC = pl.BlockSpec(memory_space=pltpu.SEMAPHORE)
EFFECT = pltpu.SideEffectType.DATAFLOW_SIDE_EFFECTING


def _route(kind, src, dst, c, k, peer):
    if kind == "gather":
        return src.at[c], dst.at[c, k], dst.at[c, peer]
    return src.at[peer], dst.at[k], dst.at[peer]


def split_start(batches, kind, name):
    flat = [a for batch in batches for a in batch]
    n, nb = len(flat), len(batches)
    lands = [lax.empty((2, N_CHIPS) + a.shape[1:] if kind == "gather" else a.shape, a.dtype) for a in flat]

    def body(*refs):
        srcs, dsts, sems, token = refs[:n], refs[n:2 * n], refs[2 * n:2 * n + 2 * nb], refs[-1]
        x, y, c, k, others = _place()
        pos = 0
        for b, batch in enumerate(batches):
            for a in range(len(batch)):
                for j, (px, py) in enumerate(others):
                    src, dst, _ = _route(kind, srcs[pos], dsts[pos], c, k, 2 * px + py)
                    _rc(src, dst, sems[2 * b].at[3 * a + j], sems[2 * b + 1].at[3 * a + j], (px, py, c)).start()
                pos += 1
        token[...] = jnp.zeros(token.shape, token.dtype)

    sem_shapes = [pltpu.SemaphoreType.DMA((3 * len(batch),)) for batch in batches for _ in range(2)]
    thru = [pltpu.HBM(a.shape, a.dtype) for a in flat] + [pltpu.HBM(l.shape, l.dtype) for l in lands]
    res = pl.pallas_call(
        body, name=name, out_shape=tuple(sem_shapes + thru + [_sds((8, LANES), F32)]),
        in_specs=[HBM_SPEC] * (2 * n),
        out_specs=tuple([SEM_SPEC] * (2 * nb) + [HBM_SPEC] * (2 * n) + [pl.BlockSpec(memory_space=pltpu.VMEM)]),
        input_output_aliases={t: 2 * nb + t for t in range(2 * n)},
        compiler_params=pltpu.CompilerParams(has_side_effects=EFFECT),
    )(*[pltpu.with_memory_space_constraint(t, pltpu.HBM) for t in flat + lands])
    sems, srcs, dsts = res[:2 * nb], res[2 * nb:2 * nb + n], res[2 * nb + n:2 * nb + 2 * n]
    out, pos = [], 0
    for b, batch in enumerate(batches):
        out.append((sems[2 * b], sems[2 * b + 1], list(srcs[pos:pos + len(batch)]), list(dsts[pos:pos + len(batch)])))
        pos += len(batch)
    return out, res[-1]


def split_arrive(handle, kind, after, name):
    send, recv, srcs, dsts = handle
    n = len(srcs)

    def body(*refs):
        s_refs, d_refs, send_ref, recv_ref = refs[:n], refs[n:2 * n], refs[2 * n], refs[2 * n + 1]
        x, y, c, k, others = _place()
        for a in range(n):
            for j, (px, py) in enumerate(others):
                src, _, landed = _route(kind, s_refs[a], d_refs[a], c, k, 2 * px + py)
                cp = _rc(src, landed, send_ref.at[3 * a + j], recv_ref.at[3 * a + j], (px, py, c))
                cp.wait_send()
                cp.wait_recv()

    res = pl.pallas_call(
        body, name=name, out_shape=tuple([pltpu.HBM(t.shape, t.dtype) for t in srcs + dsts]),
        in_specs=[HBM_SPEC] * (2 * n) + [SEM_SPEC, SEM_SPEC, ANY], out_specs=tuple([HBM_SPEC] * (2 * n)),
        input_output_aliases={t: t for t in range(2 * n)},
        compiler_params=pltpu.CompilerParams(has_side_effects=EFFECT),
    )(*srcs, *dsts, send, recv, after)
    return list(res[n:])


def forward_cores(bufs):
    n = len(bufs)

    def copies(outs, sems):
        send, recv = sems
        x, y, c, k, others = _place()
        onward, land = [], []
        for a in range(n):
            for j, (px, py) in enumerate(others):
                blk = outs[a].at[c, 2 * px + py]
                onward.append(_rc(blk, blk, send.at[a, j], recv.at[a, j], (x, y, 1 - c)))
                blk2 = outs[a].at[1 - c, 2 * px + py]
                land.append(_rc(blk2, blk2, send.at[a, j], recv.at[a, j], (x, y, 1 - c)))
        return onward, land

    def start(ins, outs, sems):
        for cp in copies(outs, sems)[0]:
            cp.start()

    def finish(ins, outs, sems):
        onward, land = copies(outs, sems)
        for arrived in land:
            arrived.wait_recv()
        for cp in onward:
            cp.wait_send()

    return Exchange(list(bufs), [_sds(b.shape, b.dtype) for b in bufs],
                    [pltpu.SemaphoreType.DMA((n, 3)), pltpu.SemaphoreType.DMA((n, 3))], start, finish, inplace=True)


def all_reduce_small(buf, name):
    r = buf.shape[0]
    n_dev = 8

    def body(in_ref, o_ref, land, send_sems, recv_sems):
        x, y, c, _, _ = _place()
        me = 4 * x + 2 * y + c
        land[me] = in_ref[...]
        sends = []
        for d in range(1, n_dev):
            peer = (x ^ (d >> 2), y ^ ((d >> 1) & 1), c ^ (d & 1))
            cp = _rc(in_ref, land.at[me], send_sems.at[d], recv_sems.at[d], peer)
            cp.start()
            sends.append(cp)
        for d in range(1, n_dev):
            blk = land.at[me ^ d]
            _rc(blk, blk, send_sems.at[d], recv_sems.at[d], (x, y, c)).wait_recv()
        for cp in sends:
            cp.wait_send()
        tot = land[0]
        for d in range(1, n_dev):
            tot = tot + land[d]
        o_ref[...] = tot

    vm = pl.BlockSpec(memory_space=pltpu.VMEM)
    return pl.pallas_call(
        body, out_shape=_sds(buf.shape, F32), in_specs=[vm], out_specs=vm,
        scratch_shapes=[pltpu.VMEM((n_dev, r, LANES), F32), pltpu.SemaphoreType.DMA((n_dev,)),
                        pltpu.SemaphoreType.DMA((n_dev,))],
        name=name)(buf)


def rmsnorm_fwd(x, w, name):
    s, d = x.shape
    tm = _row_tile(s, 512)

    def body(x_ref, w_ref, o_ref):
        o_ref[...] = _rms_fwd(x_ref[...], w_ref[...]).astype(BF16)

    return _pcall(body, out_shape=[_sds((s, d), BF16)], grid=(s // tm,),
                  in_specs=[pl.BlockSpec((tm, d), lambda i: (i, 0)), pl.BlockSpec((1, d), lambda i: (0, 0))],
                  out_specs=[pl.BlockSpec((tm, d), lambda i: (i, 0))], args=[x, w.reshape(1, d)], name=name)[0]


def _ffn_w_spec(chip_of):
    return pl.BlockSpec((N_CORES, 1, FF_PART, D_MODEL), lambda *ids: (0, chip_of(*ids), 0, 0))


def ffn_fwd(h, x, wg, wu, wd, norm_ws, name, ride=None):
    s, d = h.shape
    n_norm = len(norm_ws)
    tm = _row_tile(s, 1024)

    def body(*refs):
        h_ref, x_ref, wg_ref, wu_ref, wd_ref = refs[:5]
        nw_refs = refs[5:5 + n_norm]
        o_ref = refs[5 + n_norm]
        h_refs = refs[6 + n_norm:6 + 2 * n_norm]
        gu_ref, acc = refs[6 + 2 * n_norm], refs[7 + 2 * n_norm]
        k = pl.program_id(1)

        @pl.when(k == 0)
        def _():
            acc[...] = jnp.zeros(acc.shape, F32)

        hm = tm // 2
        for part in range(2):
            sub = pl.ds(part * hm, hm)
            hb = h_ref[sub, :]
            g = _dot(hb, wg_ref[...].reshape(FF_SHARD, d), NT)
            u = _dot(hb, wu_ref[...].reshape(FF_SHARD, d), NT)
            gu_ref[0, 0, sub, :] = g.astype(BF16)
            gu_ref[0, 1, sub, :] = u.astype(BF16)
            acc[sub, :] += _dot((g * _sigmoid(g) * u).astype(BF16), wd_ref[...].reshape(FF_SHARD, d))

        @pl.when(k == N_CHIPS - 1)
        def _():
            xn = x_ref[...] + 0.5 * acc[...]
            o_ref[...] = xn
            for nw_ref, hn_ref in zip(nw_refs, h_refs):
                hn_ref[...] = _rms_fwd(xn, nw_ref[...]).astype(BF16)

    row = pl.BlockSpec((tm, d), lambda i, k: (i, 0))
    vec = pl.BlockSpec((1, d), lambda i, k: (0, 0))
    wsp = _ffn_w_spec(lambda i, k: k)
    return _pcall(
        body, out_shape=[_sds((s, d), F32)] + [_sds((s, d), BF16)] * n_norm + [_sds((N_CHIPS, 2, s, FF_SHARD), BF16)],
        grid=(s // tm, N_CHIPS),
        in_specs=[row, row, wsp, wsp, wsp] + [vec] * n_norm,
        out_specs=[row] * (1 + n_norm) + [pl.BlockSpec((1, 2, tm, FF_SHARD), lambda i, k: (k, 0, i, 0))],
        scratch_shapes=[pltpu.VMEM((tm, d), F32)],
        args=[h, x, wg, wu, wd] + [nw.reshape(1, d) for nw in norm_ws], name=name, ride=ride)


def ffn_bwd(dxn, h, x_in, nw, gu, wg, wu, wd, name, ride=None):
    s, d = h.shape
    tm = _row_tile(s, 512)
    ni = s // tm
    last_e = N_CHIPS - 1

    def body(dxn_ref, h_ref, x_ref, nw_ref, gu_ref, wg_ref, wu_ref, wd_ref,
             dx_ref, dnw_ref, dwg_ref, dwu_ref, dwd_ref, dh, wacc):
        e = pl.program_id(0)
        i = pl.program_id(1)
        rows = pl.ds(pl.multiple_of(i * tm, tm), tm)

        @pl.when(i == 0)
        def _():
            wacc[...] = jnp.zeros(wacc.shape, F32)

        @pl.when(e == 0)
        def _():
            dh[rows, :] = jnp.zeros((tm, d), F32)

        hm = tm // 2
        for part in range(2):
            sub = pl.ds(part * hm, hm)
            dxb = dxn_ref[sub, :].astype(BF16)
            hb = h_ref[sub, :]
            g = gu_ref[0, 0, sub, :].astype(F32)
            u = gu_ref[0, 1, sub, :].astype(F32)
            drows = pl.ds(pl.multiple_of(i * tm + part * hm, hm), hm)
            sg = _sigmoid(g)
            silu = g * sg
            wacc[2] += _dot((0.5 * silu * u).astype(BF16), dxb, TN)
            da = 0.5 * _dot(dxb, wd_ref[...].reshape(FF_SHARD, d), NT)
            dg = (da * u * (sg * (1.0 + g * (1.0 - sg)))).astype(BF16)
            wacc[0] += _dot(dg, hb, TN)
            du = (da * silu).astype(BF16)
            dh[drows, :] += _dot(dg, wg_ref[...].reshape(FF_SHARD, d))
            wacc[1] += _dot(du, hb, TN)
            dh[drows, :] += _dot(du, wu_ref[...].reshape(FF_SHARD, d))

        @pl.when(i == ni - 1)
        def _():
            for t, dw_ref in enumerate((dwg_ref, dwu_ref, dwd_ref)):
                dw_ref[...] = wacc[t].astype(BF16).reshape(N_CORES, 1, FF_PART, d)

        @pl.when(e == last_e)
        def _():
            dx, dnw = _rms_bwd(dh[rows, :], x_ref[...], nw_ref[...])
            dx_ref[...] = dxn_ref[...] + dx
            col = jnp.sum(dnw, axis=0, keepdims=True)

            @pl.when(i == 0)
            def _():
                dnw_ref[...] = col

            @pl.when(i > 0)
            def _():
                dnw_ref[...] += col

    row = pl.BlockSpec((tm, d), lambda e, i: (i, 0))
    late = pl.BlockSpec((tm, d), lambda e, i: (jnp.where(e == last_e, i, 0), 0))
    vec = pl.BlockSpec((1, d), lambda e, i: (0, 0))
    wsp = _ffn_w_spec(lambda e, i: e)
    dw = _sds((N_CORES, N_CHIPS, FF_PART, d), BF16)
    return _pcall(
        body, out_shape=[_sds((s, d), F32), _sds((1, d), F32), dw, dw, dw],
        grid=(N_CHIPS, ni),
        in_specs=[row, row, late, vec, pl.BlockSpec((1, 2, tm, FF_SHARD), lambda e, i: (e, 0, i, 0)), wsp, wsp, wsp],
        out_specs=[late, vec, wsp, wsp, wsp],
        scratch_shapes=[pltpu.VMEM((s, d), F32), pltpu.VMEM((3, FF_SHARD, d), F32)],
        args=[dxn, h, x_in, nw.reshape(1, d), gu, wg, wu, wd], name=name, ride=ride)


def mm_res(a, w, x, name, bias=None, norm_ws=(), ride=None):
    s, k = a.shape
    n = w.shape[1]
    tm = _row_tile(s, 512)
    has_bias = bias is not None
    n_norm = len(norm_ws)

    def body(*refs):
        a_ref, w_ref, x_ref = refs[:3]
        pos = 3
        t = _dot(a_ref[...], w_ref[...])
        if has_bias:
            t = t + refs[pos][...]
            pos += 1
        nw_refs = refs[pos:pos + n_norm]
        o_ref = refs[pos + n_norm]
        h_refs = refs[pos + n_norm + 1:]
        xn = x_ref[...] + t
        o_ref[...] = xn
        for nw_ref, h_ref in zip(nw_refs, h_refs):
            h_ref[...] = _rms_fwd(xn, nw_ref[...]).astype(BF16)

    row = pl.BlockSpec((tm, n), lambda i: (i, 0))
    vec = pl.BlockSpec((1, n), lambda i: (0, 0))
    in_specs = [pl.BlockSpec((tm, k), lambda i: (i, 0)), pl.BlockSpec((k, n), lambda i: (0, 0)), row]
    args = [a, w, x]
    if has_bias:
        in_specs.append(vec)
        args.append(bias.reshape(1, n))
    for nw in norm_ws:
        in_specs.append(vec)
        args.append(nw.reshape(1, n))
    return _pcall(body, out_shape=[_sds((s, n), F32)] + [_sds((s, n), BF16)] * n_norm, grid=(s // tm,),
                  in_specs=in_specs, out_specs=[row] * (1 + n_norm), args=args, name=name, ride=ride)


def mm_nn(a, w, name, bias=None, out_dtype=F32, rope=None):
    s, k = a.shape
    n = w.shape[1]
    tm = _row_tile(s, 512)
    tn = _col_tile(n)
    has_bias = bias is not None
    half = ATT_HEAD_DIM // 2

    def body(*refs):
        a_ref, w_ref = refs[:2]
        o_ref = refs[-1]
        t = _dot(a_ref[...], w_ref[...])
        if has_bias:
            t = t + refs[2][...]
        if rope is not None:
            c = jnp.tile(refs[-3][...], (1, tn // LANES))
            sn = jnp.tile(refs[-2][...], (1, tn // LANES))
            lane = lax.broadcasted_iota(jnp.int32, t.shape, 1)
            first = (lane & (ATT_HEAD_DIM - 1)) < half
            rot = jnp.where(first, -pltpu.roll(t, tn - half, 1), pltpu.roll(t, half, 1))
            t = rope[2] * (t * c + rot * sn)
        o_ref[...] = t.astype(out_dtype)

    in_specs = [pl.BlockSpec((tm, k), lambda j, i: (i, 0)), pl.BlockSpec((k, tn), lambda j, i: (0, j))]
    args = [a, w]
    if has_bias:
        in_specs.append(pl.BlockSpec((1, tn), lambda j, i: (0, j)))
        args.append(bias.reshape(1, n))
    if rope is not None:
        in_specs += [pl.BlockSpec((tm, LANES), lambda j, i: (i, 0))] * 2
        args += [rope[0], rope[1]]
    return _pcall(body, out_shape=[_sds((s, n), out_dtype)], grid=(n // tn, s // tm), in_specs=in_specs,
                  out_specs=[pl.BlockSpec((tm, tn), lambda j, i: (i, j))], args=args, name=name)[0]


def mm_nt(a, w, name, n=None, row0=0, out_dtype=F32, ride=None):
    s, k = a.shape
    n = w.shape[0] if n is None else n
    tm = _row_tile(s, 512)
    tn = _col_tile(n)
    base = row0 // tn
    assert row0 % tn == 0

    def body(a_ref, w_ref, o_ref):
        o_ref[...] = _dot(a_ref[...].astype(BF16), w_ref[...], NT).astype(out_dtype)

    res = _pcall(body, out_shape=[_sds((s, n), out_dtype)], grid=(n // tn, s // tm),
                 in_specs=[pl.BlockSpec((tm, k), lambda j, i: (i, 0)), pl.BlockSpec((tn, k), lambda j, i: (base + j, 0))],
                 out_specs=[pl.BlockSpec((tm, tn), lambda j, i: (i, j))], args=[a, w], name=name, ride=ride)
    return res[0] if ride is None else (res[0][0], res[1])


def mm_tn(a, b, name, into=None, rows=None, row0=0, m_valid=None, col_sum=False, ride=None):
    s, m = a.shape
    n = b.shape[1]
    mv = m if m_valid is None else m_valid
    tm = _col_tile(m) if m_valid is None else mv
    tn = 512 if n % 512 == 0 else n
    rows = mv if rows is None else rows
    assert row0 % tm == 0 and (m_valid is None or m == LANES)
    assert not col_sum or mv == tm
    base = row0 // tm
    ta = m if m_valid is not None else tm

    def body(*refs):
        a_ref, b_ref = refs[0], refs[1]
        o_ref = refs[-2] if col_sum else refs[-1]
        bf = b_ref[...]
        t = _dot(a_ref[...].astype(BF16), bf.astype(BF16), TN)
        o_ref[...] = t[:tm].astype(BF16)
        if col_sum:
            refs[-1][...] = jnp.sum(bf.astype(F32), axis=0, keepdims=True)

    in_specs = [pl.BlockSpec((s, ta), lambda i, j: (0, i)), pl.BlockSpec((s, tn), lambda i, j: (0, j))]
    args = [a, b]
    aliases = None
    if into is not None:
        in_specs.append(ANY)
        args.append(into)
        aliases = {2: 0}
    out_shape = [_sds((rows, n), BF16)]
    out_specs = [pl.BlockSpec((tm, tn), lambda i, j: (base + i, j))]
    if col_sum:
        out_shape.append(_sds((1, n), F32))
        out_specs.append(pl.BlockSpec((1, tn), lambda i, j: (0, j)))
    res = _pcall(body, out_shape=out_shape, grid=(mv // tm, n // tn), in_specs=in_specs, out_specs=out_specs,
                 args=args, name=name, ride=ride, aliases=aliases)
    outs = res if ride is None else res[0]
    out = (outs[0], outs[1][0]) if col_sum else outs[0]
    return out if ride is None else (out, res[1])


def mm_rms_bwd(terms, dxn, x, nw, name, ride=None):
    s, n = x.shape
    nt_ = len(terms)
    tm = _row_tile(s, 512 if nt_ <= 2 else 256)
    forms = [t[5] for t in terms]

    def body(*refs):
        dxn_ref, x_ref, nw_ref, dx_ref, dnw_ref = refs[2 * nt_:]
        i = pl.program_id(0)
        dh = None
        for t in range(nt_):
            part = _dot(refs[2 * t][...].astype(BF16), refs[2 * t + 1][...], NN if forms[t] == "nn" else NT)
            dh = part if dh is None else dh + part
        dx, dnw = _rms_bwd(dh, x_ref[...], nw_ref[...])
        dx_ref[...] = dxn_ref[...] + dx
        col = jnp.sum(dnw, axis=0, keepdims=True)

        @pl.when(i == 0)
        def _():
            dnw_ref[...] = col

        @pl.when(i > 0)
        def _():
            dnw_ref[...] += col

    in_specs, args = [], []
    for a, cb, w, rb, kb, form in terms:
        in_specs.append(pl.BlockSpec((tm, kb), lambda i, cb=cb: (i, cb)))
        if form == "nn":
            in_specs.append(pl.BlockSpec((kb, n), lambda i, rb=rb: (rb, 0)))
        else:
            in_specs.append(pl.BlockSpec((n, kb), lambda i, rb=rb: (0, rb)))
        args += [a, w]
    row = pl.BlockSpec((tm, n), lambda i: (i, 0))
    vec = pl.BlockSpec((1, n), lambda i: (0, 0))
    res = _pcall(body, out_shape=[_sds((s, n), F32), _sds((1, n), F32)], grid=(s // tm,),
                 in_specs=in_specs + [row, row, vec], out_specs=[row, vec],
                 args=args + [dxn, x, nw.reshape(1, n)], name=name, ride=ride)
    outs = res if ride is None else res[0]
    out = (outs[0], outs[1][0])
    return out if ride is None else (out, res[1])


def rope_tables(s):
    pos = jnp.arange(s, dtype=F32)
    inv = 1.0 / (ROPE_THETA ** (jnp.arange(0, ATT_HEAD_DIM, 2, dtype=F32) / ATT_HEAD_DIM))
    ang = pos[:, None] * inv[None, :]
    cos = jnp.tile(jnp.cos(ang), (1, 2 * LANES // ATT_HEAD_DIM))
    sin = jnp.tile(jnp.sin(ang), (1, 2 * LANES // ATT_HEAD_DIM))
    return cos, sin


def rope_apply(t, cos, sin, name, inverse=False, scale=1.0, out_dtype=BF16):
    s, n = t.shape
    tm = _row_tile(s, 512)
    half = ATT_HEAD_DIM // 2
    reps = n // LANES

    def body(t_ref, c_ref, s_ref, o_ref):
        tf = t_ref[...].astype(F32)
        c = jnp.tile(c_ref[...], (1, reps))
        sn = jnp.tile(s_ref[...], (1, reps))
        lane = lax.broadcasted_iota(jnp.int32, tf.shape, 1)
        first = (lane & (ATT_HEAD_DIM - 1)) < half
        rot = jnp.where(first, -pltpu.roll(tf, n - half, 1), pltpu.roll(tf, half, 1))
        sign = -1.0 if inverse else 1.0
        o_ref[...] = (scale * (tf * c + sign * rot * sn)).astype(out_dtype)

    tab = pl.BlockSpec((tm, LANES), lambda i: (i, 0))
    return _pcall(body, out_shape=[_sds((s, n), out_dtype)], grid=(s // tm,),
                  in_specs=[pl.BlockSpec((tm, n), lambda i: (i, 0)), tab, tab],
                  out_specs=[pl.BlockSpec((tm, n), lambda i: (i, 0))], args=[t, cos, sin], name=name)[0]


CONV_TILE = 256


def _shift_down(u, k):
    if k == 0:
        return u
    row = lax.broadcasted_iota(jnp.int32, u.shape, 0)
    return jnp.where(row >= k, pltpu.roll(u, k, 0), 0.0)


def _shift_up(u, k):
    if k == 0:
        return u
    s = u.shape[0]
    row = lax.broadcasted_iota(jnp.int32, u.shape, 0)
    return jnp.where(row < s - k, pltpu.roll(u, s - k, 0), 0.0)


def _conv_taps(u):
    return [_shift_down(u, CONV_WIDTH - 1 - k) for k in range(CONV_WIDTH)]


def _conv_pre(taps, w_ref, b_ref):
    pre = b_ref[...] + w_ref[0:1, :] * taps[0]
    for k in range(1, CONV_WIDTH):
        pre += w_ref[k:k + 1, :] * taps[k]
    return pre


def conv_fwd(u, w, b, name, ride=None):
    s, c = u.shape

    def body(u_ref, w_ref, b_ref, o_ref):
        pre = _conv_pre(_conv_taps(u_ref[...]), w_ref, b_ref)
        o_ref[...] = pre * _sigmoid(pre)

    col = pl.BlockSpec((s, CONV_TILE), lambda j: (0, j))
    res = _pcall(body, out_shape=[_sds((s, c), F32)], grid=(c // CONV_TILE,),
                 in_specs=[col, pl.BlockSpec((CONV_WIDTH, CONV_TILE), lambda j: (0, j)),
                           pl.BlockSpec((1, CONV_TILE), lambda j: (0, j))],
                 out_specs=[col], args=[u, w, b.reshape(1, c)], name=name, ride=ride)
    return res[0] if ride is None else (res[0][0], res[1])


def conv_bwd(dxs, db_, dc_, u, w, b, name, ride=None):
    s, c = u.shape
    n_x = dxs.shape[1] // CONV_TILE
    n_b = db_.shape[1] // CONV_TILE

    def body(dx_ref, dbb_ref, dcc_ref, u_ref, w_ref, b_ref, du_ref, dw_ref, dbias_ref):
        j = pl.program_id(0)
        dact = jnp.where(j < n_x, dx_ref[...], jnp.where(j < n_x + n_b, dbb_ref[...], dcc_ref[...]))
        taps = _conv_taps(u_ref[...])
        pre = _conv_pre(taps, w_ref, b_ref)
        sg = _sigmoid(pre)
        dpre = dact * (sg * (1.0 + pre * (1.0 - sg)))
        du = w_ref[CONV_WIDTH - 1:CONV_WIDTH, :] * dpre
        for k in range(CONV_WIDTH - 1):
            du += w_ref[k:k + 1, :] * _shift_up(dpre, CONV_WIDTH - 1 - k)
        du_ref[...] = du
        dbias_ref[...] = jnp.sum(dpre, axis=0, keepdims=True)
        for k in range(CONV_WIDTH):
            dw_ref[k:k + 1, :] = jnp.sum(dpre * taps[k], axis=0, keepdims=True)

    col = pl.BlockSpec((s, CONV_TILE), lambda j: (0, j))
    wsp = pl.BlockSpec((CONV_WIDTH, CONV_TILE), lambda j: (0, j))
    bsp = pl.BlockSpec((1, CONV_TILE), lambda j: (0, j))
    res = _pcall(
        body, out_shape=[_sds((s, c), F32), _sds((CONV_WIDTH, c), F32), _sds((1, c), F32)], grid=(c // CONV_TILE,),
        in_specs=[pl.BlockSpec((s, CONV_TILE), lambda j: (0, jnp.minimum(j, n_x - 1))),
                  pl.BlockSpec((s, CONV_TILE), lambda j: (0, jnp.clip(j - n_x, 0, n_b - 1))),
                  pl.BlockSpec((s, CONV_TILE), lambda j: (0, jnp.clip(j - n_x - n_b, 0, n_b - 1))),
                  col, wsp, bsp],
        out_specs=[col, wsp, bsp], args=[dxs, db_, dc_, u, w, b.reshape(1, c)], name=name, ride=ride)
    (du, dw, db), rode = res if ride is not None else (res, None)
    return (du, dw, db[0]) if ride is None else ((du, dw, db[0]), rode)


def _lane_pick(mat, idx):
    lane = lax.broadcasted_iota(jnp.int32, mat.shape, 1)
    return jnp.sum(jnp.where(lane == idx, mat, 0.0), axis=1, keepdims=True)


def _sub_pick(mat, idx):
    sub = lax.broadcasted_iota(jnp.int32, mat.shape, 0)
    return jnp.sum(jnp.where(sub == idx, mat, 0.0), axis=0, keepdims=True)


def _expand_heads(cols):
    rows = cols[0].shape[0]
    left = lax.broadcasted_iota(jnp.int32, (rows, LANES), 1) < SSM_HEAD_DIM
    return jnp.concatenate(
        [jnp.where(left, cols[2 * p], cols[2 * p + 1]) for p in range(HEADS_PER_GROUP // 2)], axis=1)


def _dot_01(x, ones, ones_first, pieces):
    tot, rest = None, x
    for _ in range(pieces):
        piece = rest.astype(BF16)
        rest = rest - piece.astype(F32)
        part = _dot(ones, piece) if ones_first else _dot(piece, ones)
        tot = part if tot is None else tot + part
    return tot


def _heads_to_lanes(mat, g):
    jj = lax.broadcasted_iota(jnp.int32, (GROUP_DIM, LANES), 0)
    ll = lax.broadcasted_iota(jnp.int32, (GROUP_DIM, LANES), 1)
    sel = (ll == HEADS_PER_GROUP * g + (jj >> 6)).astype(BF16)
    return _dot_01(mat, sel, False, 3)


def _softplus(x):
    return jnp.maximum(x, 0.0) + jnp.log1p(jnp.exp(-jnp.abs(x)))


def _ssd_scalars(dt_ref, bias_ref, a_ref, dtall, csall, cst):
    dta = _softplus(dt_ref[...] + bias_ref[...])
    row = lax.broadcasted_iota(jnp.int32, (CHUNK, CHUNK), 0)
    col = lax.broadcasted_iota(jnp.int32, (CHUNK, CHUNK), 1)
    cs = _dot_01(dta * a_ref[...], (row >= col).astype(BF16), True, 3)
    dtall[...] = dta
    csall[...] = cs
    cst[...] = cs.T


def _decay_mat(cs_col, cs_row):
    row = lax.broadcasted_iota(jnp.int32, (CHUNK, CHUNK), 0)
    col = lax.broadcasted_iota(jnp.int32, (CHUNK, CHUNK), 1)
    return jnp.exp(jnp.where(row >= col, cs_col - cs_row, NEG))


def _head_mask(xpair, right):
    lane = lax.broadcasted_iota(jnp.int32, xpair.shape, 1)
    keep = (lane >= SSM_HEAD_DIM) if right else (lane < SSM_HEAD_DIM)
    return jnp.where(keep, xpair, 0.0)


def _chunk_cols(x_all, g):
    return [_lane_pick(x_all, HEADS_PER_GROUP * g + r) for r in range(HEADS_PER_GROUP)]


def _decay_col(cs_cols):
    return jnp.concatenate(
        [jnp.broadcast_to(jnp.exp(cc[CHUNK - 1:CHUNK, :]), (SSM_HEAD_DIM, 1)) for cc in cs_cols], axis=0)


def ssd_fwd(act, z, dtp, bias_p, a_p, d_p, normw, name, ride=None):
    s = act.shape[0]
    nc = s // CHUNK

    def body(xs_all, b_all, c_all, z_all, dt_ref, bias_ref, a_ref, d_ref, nw_all,
             yn_all, y_all, st_all, state, dtall, csall, cst):
        _ssd_scalars(dt_ref, bias_ref, a_ref, dtall, csall, cst)

        @pl.when(pl.program_id(0) == 0)
        def _():
            state[...] = jnp.zeros(state.shape, F32)

        for g in range(SSM_GROUPS):
            wide = pl.ds(g * GROUP_DIM, GROUP_DIM)
            narrow = pl.ds(g * SSM_STATE, SSM_STATE)
            group(g, xs_all.at[:, wide], b_all.at[:, narrow], c_all.at[:, narrow], z_all.at[:, wide], d_ref,
                  nw_all.at[:, wide], yn_all.at[:, wide], y_all.at[:, wide], st_all.at[:, pl.ds(g, 1)],
                  state, dtall, csall, cst)

    def group(g, xs_ref, b_ref, c_ref, z_ref, d_ref, nw_ref, yn_ref, y_ref, st_ref, state, dtall, csall, cst):
        cs_cols = _chunk_cols(csall[...], g)
        dt_cols = _chunk_cols(dtall[...], g)
        cs_rows = [_sub_pick(cst[...], HEADS_PER_GROUP * g + r) for r in range(HEADS_PER_GROUP)]
        d_cols = _chunk_cols(d_ref[...], g)
        cs_exp = _expand_heads(cs_cols)
        dt_exp = _expand_heads(dt_cols)
        d_exp = _expand_heads(d_cols)
        xs = xs_ref[...]
        bb = b_ref[...].astype(BF16)
        cb16 = c_ref[...].astype(BF16)
        xdt = xs * dt_exp
        s_prev = state[g]
        st_ref[0, 0] = s_prev
        y_off = _dot(cb16, s_prev.astype(BF16), NT) * jnp.exp(cs_exp)
        decay_st = jnp.exp(cs_exp[CHUNK - 1:CHUNK, :] - cs_exp)
        contrib = _dot((xdt * decay_st).astype(BF16), bb, TN)
        state[g] = _decay_col(cs_cols) * s_prev + contrib
        cbm = _dot(cb16, bb, NT)
        pairs = []
        for p in range(HEADS_PER_GROUP // 2):
            xpair = xdt[:, LANES * p:LANES * (p + 1)]
            m0 = (cbm * _decay_mat(cs_cols[2 * p], cs_rows[2 * p])).astype(BF16)
            m1 = (cbm * _decay_mat(cs_cols[2 * p + 1], cs_rows[2 * p + 1])).astype(BF16)
            pairs.append(_dot(m0, _head_mask(xpair, False).astype(BF16))
                         + _dot(m1, _head_mask(xpair, True).astype(BF16)))
        y = jnp.concatenate(pairs, axis=1) + y_off + xs * d_exp
        y_ref[...] = y
        zf = z_ref[...]
        yg = y * (zf * _sigmoid(zf))
        yn_ref[...] = _rms_fwd(yg, nw_ref[...]).astype(BF16)

    gn = SSM_GROUPS * SSM_STATE
    wide = pl.BlockSpec((CHUNK, D_INNER), lambda c: (c, 0))
    par = pl.BlockSpec((1, LANES), lambda c: (0, 0))
    return _pcall(
        body,
        out_shape=[_sds((s, D_INNER), BF16), _sds((s, D_INNER), F32),
                   _sds((nc, SSM_GROUPS, GROUP_DIM, SSM_STATE), F32)],
        grid=(nc,),
        in_specs=[wide,
                  pl.BlockSpec((CHUNK, gn), lambda c: (c, D_INNER // gn)),
                  pl.BlockSpec((CHUNK, gn), lambda c: (c, D_INNER // gn + 1)),
                  wide,
                  pl.BlockSpec((CHUNK, LANES), lambda c: (c, 0)),
                  par, par, par,
                  pl.BlockSpec((1, D_INNER), lambda c: (0, 0))],
        out_specs=[wide, wide, pl.BlockSpec((1, SSM_GROUPS, GROUP_DIM, SSM_STATE), lambda c: (c, 0, 0, 0))],
        scratch_shapes=[pltpu.VMEM((SSM_GROUPS, GROUP_DIM, SSM_STATE), F32),
                        pltpu.VMEM((CHUNK, LANES), F32), pltpu.VMEM((CHUNK, LANES), F32),
                        pltpu.VMEM((LANES, CHUNK), F32)],
        args=[act, act, act, z, dtp, bias_p, a_p, d_p, normw], name=name, ride=ride)


def ssd_bwd(dyn, act, z, y_pre, states, dtp, bias_p, a_p, d_p, normw, name, ride=None):
    s = act.shape[0]
    nc = s // CHUNK

    def body(dyn_all, xs_all, b_all, c_all, z_all, y_all, st_all, dt_ref, bias_ref, a_ref, d_ref, nw_all,
             dxs_all, db_all, dc_all, dz_all, ddt_ref, dnw_ref, dbias_ref, da_ref, dd_ref,
             dstate, dtall, csall, cst):
        _ssd_scalars(dt_ref, bias_ref, a_ref, dtall, csall, cst)
        ddt_ref[...] = jnp.zeros((CHUNK, LANES), F32)

        @pl.when(pl.program_id(0) == 0)
        def _():
            dstate[...] = jnp.zeros(dstate.shape, F32)
            dnw_ref[...] = jnp.zeros(dnw_ref.shape, F32)
            dbias_ref[...] = jnp.zeros((1, LANES), F32)
            da_ref[...] = jnp.zeros((1, LANES), F32)
            dd_ref[...] = jnp.zeros((1, LANES), F32)

        for g in range(SSM_GROUPS):
            wide = pl.ds(g * GROUP_DIM, GROUP_DIM)
            narrow = pl.ds(g * SSM_STATE, SSM_STATE)
            group(g, dyn_all.at[:, wide], xs_all.at[:, wide], b_all.at[:, narrow], c_all.at[:, narrow],
                  z_all.at[:, wide], y_all.at[:, wide], st_all.at[:, pl.ds(g, 1)], dt_ref, bias_ref, a_ref, d_ref,
                  nw_all.at[:, wide], dxs_all.at[:, wide], db_all.at[:, narrow], dc_all.at[:, narrow],
                  dz_all.at[:, wide], ddt_ref, dnw_ref, dbias_ref, da_ref, dd_ref, dstate, dtall, csall, cst)

    def group(g, dyn_ref, xs_ref, b_ref, c_ref, z_ref, y_ref, st_ref, dt_ref, bias_ref, a_ref, d_ref, nw_ref,
              dxs_ref, db_ref, dc_ref, dz_ref, ddt_ref, dnw_ref, dbias_ref, da_ref, dd_ref,
              dstate, dtall, csall, cst):
        cs_cols = _chunk_cols(csall[...], g)
        dt_cols = _chunk_cols(dtall[...], g)
        cs_rows = [_sub_pick(cst[...], HEADS_PER_GROUP * g + r) for r in range(HEADS_PER_GROUP)]
        d_cols = _chunk_cols(d_ref[...], g)
        cs_exp = _expand_heads(cs_cols)
        dt_exp = _expand_heads(dt_cols)
        d_exp = _expand_heads(d_cols)
        xs = xs_ref[...]
        bb = b_ref[...].astype(BF16)
        cb16 = c_ref[...].astype(BF16)
        xdt = xs * dt_exp
        s_prev = st_ref[0, 0]
        s_prev16 = s_prev.astype(BF16)
        ds_next = dstate[g]
        ds16 = ds_next.astype(BF16)

        zf = z_ref[...]
        sz = _sigmoid(zf)
        silu_z = zf * sz
        y = y_ref[...]
        yg = y * silu_z
        dout = dyn_ref[...]
        dyg, dnw = _rms_bwd(dout, yg, nw_ref[...])
        dnw_ref[pl.ds(g, 1), :] += jnp.sum(dnw, axis=0, keepdims=True)
        dy = dyg * silu_z
        dz_ref[...] = dyg * y * (sz * (1.0 + zf * (1.0 - sz)))
        dd_ref[...] += jnp.sum(_heads_to_lanes(dy * xs, g), axis=0, keepdims=True)

        exp_cs = jnp.exp(cs_exp)
        decay_st = jnp.exp(cs_exp[CHUNK - 1:CHUNK, :] - cs_exp)
        cs_t = _dot(cb16, s_prev16, NT)
        dyo = dy * exp_cs
        dc_acc = _dot(dyo.astype(BF16), s_prev16, NN)
        g1 = _dot(bb, ds16, NT)
        xds = xdt * decay_st
        db_acc = _dot(xds.astype(BF16), ds16, NN)
        dxdt_off = g1 * decay_st
        t_exp = g1 * xds
        dcs_exp = dy * cs_t * exp_cs - t_exp
        decay_c = _decay_col(cs_cols)
        dstate[g] = decay_c * ds_next + _dot(dyo.astype(BF16), cb16, TN)
        dlast_col = jnp.sum(ds_next * s_prev, axis=1, keepdims=True) * decay_c
        jj = lax.broadcasted_iota(jnp.int32, (GROUP_DIM, LANES), 0)
        ll = lax.broadcasted_iota(jnp.int32, (GROUP_DIM, LANES), 1)
        sel = ll == HEADS_PER_GROUP * g + (jj >> 6)
        dlast = jnp.sum(jnp.where(sel, dlast_col, 0.0), axis=0, keepdims=True)
        t_all = _heads_to_lanes(t_exp, g)
        dlast += jnp.sum(t_all, axis=0, keepdims=True)
        dcs_all = _heads_to_lanes(dcs_exp, g)

        cbm = _dot(cb16, bb, NT)
        dcb = jnp.zeros((CHUNK, CHUNK), F32)
        dcs_rows = jnp.zeros((LANES, CHUNK), F32)
        lane_l = lax.broadcasted_iota(jnp.int32, (CHUNK, LANES), 1)
        sub_l = lax.broadcasted_iota(jnp.int32, (LANES, CHUNK), 0)
        dxdt_pairs = []
        for p in range(HEADS_PER_GROUP // 2):
            xpair16 = xdt[:, LANES * p:LANES * (p + 1)].astype(BF16)
            dypair = dy[:, LANES * p:LANES * (p + 1)]
            acc = None
            for r in (2 * p, 2 * p + 1):
                lm = _decay_mat(cs_cols[r], cs_rows[r])
                m = cbm * lm
                dyh = _head_mask(dypair, r % 2 == 1).astype(BF16)
                dm = _dot(dyh, xpair16, NT)
                dcb += dm * lm
                q = dm * m
                idx = HEADS_PER_GROUP * g + r
                dcs_all += jnp.where(lane_l == idx, jnp.sum(q, axis=1, keepdims=True), 0.0)
                dcs_rows -= jnp.where(sub_l == idx, jnp.sum(q, axis=0, keepdims=True), 0.0)
                part = _dot(m.astype(BF16), dyh, TN)
                acc = part if acc is None else acc + part
            dxdt_pairs.append(acc)
        dxdt = jnp.concatenate(dxdt_pairs, axis=1) + dxdt_off
        dcb16 = dcb.astype(BF16)
        dc_ref[...] = dc_acc + _dot(dcb16, bb, NN)
        db_ref[...] = db_acc + _dot(dcb16, cb16, TN)
        dxs_ref[...] = dxdt * dt_exp + dy * d_exp

        dcs_all += dcs_rows.T
        row = lax.broadcasted_iota(jnp.int32, (CHUNK, CHUNK), 0)
        col = lax.broadcasted_iota(jnp.int32, (CHUNK, CHUNK), 1)
        last_row = lax.broadcasted_iota(jnp.int32, (CHUNK, LANES), 0) == CHUNK - 1
        dcs_all += jnp.where(last_row, dlast, 0.0)
        da_all = _dot_01(dcs_all, (col >= row).astype(BF16), True, 3)
        dta = dtall[...]
        in_group = jnp.logical_and(lane_l >= HEADS_PER_GROUP * g, lane_l < HEADS_PER_GROUP * (g + 1))
        ddt = jnp.where(in_group, da_all * a_ref[...] + _heads_to_lanes(dxdt * xs, g), 0.0)
        da_ref[...] += jnp.sum(jnp.where(in_group, da_all * dta, 0.0), axis=0, keepdims=True)
        ddt_raw = ddt * _sigmoid(dt_ref[...] + bias_ref[...])
        ddt_ref[...] += ddt_raw
        dbias_ref[...] += jnp.sum(ddt_raw, axis=0, keepdims=True)

    gn = SSM_GROUPS * SSM_STATE
    wide = pl.BlockSpec((CHUNK, D_INNER), lambda c: (nc - 1 - c, 0))
    st = pl.BlockSpec((CHUNK, gn), lambda c: (nc - 1 - c, 0))
    par = pl.BlockSpec((1, LANES), lambda c: (0, 0))
    dtb = pl.BlockSpec((CHUNK, LANES), lambda c: (nc - 1 - c, 0))
    f = lambda shape: _sds(shape, F32)
    return _pcall(
        body,
        out_shape=[f((s, D_INNER)), f((s, gn)), f((s, gn)),
                   f((s, D_INNER)), f((s, LANES)), f((8, GROUP_DIM)), f((1, LANES)), f((1, LANES)), f((1, LANES))],
        grid=(nc,),
        in_specs=[wide, wide,
                  pl.BlockSpec((CHUNK, gn), lambda c: (nc - 1 - c, D_INNER // gn)),
                  pl.BlockSpec((CHUNK, gn), lambda c: (nc - 1 - c, D_INNER // gn + 1)),
                  wide, wide,
                  pl.BlockSpec((1, SSM_GROUPS, GROUP_DIM, SSM_STATE), lambda c: (nc - 1 - c, 0, 0, 0)),
                  dtb, par, par, par,
                  pl.BlockSpec((1, D_INNER), lambda c: (0, 0))],
        out_specs=[wide, st, st, wide, dtb, pl.BlockSpec((8, GROUP_DIM), lambda c: (0, 0)), par, par, par],
        scratch_shapes=[pltpu.VMEM((SSM_GROUPS, GROUP_DIM, SSM_STATE), F32),
                        pltpu.VMEM((CHUNK, LANES), F32), pltpu.VMEM((CHUNK, LANES), F32),
                        pltpu.VMEM((LANES, CHUNK), F32)],
        args=[dyn, act, act, act, z, y_pre, states, dtp, bias_p, a_p, d_p, normw], name=name, ride=ride)


def _attn_probs(q, kp, kc, sink, n):
    sp = _dot(q, kp, NT)
    sc = _dot(q, kc, NT)
    i = lax.broadcasted_iota(jnp.int32, sp.shape, 0) & (WINDOW - 1)
    j = lax.broadcasted_iota(jnp.int32, sp.shape, 1)
    sp = jnp.where(jnp.logical_and(j > i, n > 0), sp, NEG)
    sc = jnp.where(j <= i, sc, NEG)
    m = jnp.maximum(jnp.maximum(jnp.max(sp, axis=1, keepdims=True), jnp.max(sc, axis=1, keepdims=True)), sink)
    pp = jnp.exp(sp - m)
    pc = jnp.exp(sc - m)
    ps = jnp.exp(sink - m)
    inv = 1.0 / (jnp.sum(pp, axis=1, keepdims=True) + jnp.sum(pc, axis=1, keepdims=True) + ps)
    return pp * inv, pc * inv, ps * inv


def attn_fwd(qt, kt, vt, sink_rows, name, ride=None):
    s = qt.shape[1]
    nb = s // WINDOW
    rows = Q_PER_KV * WINDOW

    def body(q_ref, kp_ref, kc_ref, vp_ref, vc_ref, sk_ref, o_ref):
        n = pl.program_id(0)
        for h in range(N_KV_HEADS):
            heads = pl.ds(h * Q_PER_KV, Q_PER_KV)
            q = q_ref[heads].reshape(rows, ATT_HEAD_DIM)
            pp, pc, _ = _attn_probs(q, kp_ref[h], kc_ref[h], sk_ref[h], n)
            o = _dot(pp.astype(BF16), vp_ref[h]) + _dot(pc.astype(BF16), vc_ref[h])
            o_ref[heads] = o.reshape(Q_PER_KV, WINDOW, ATT_HEAD_DIM).astype(BF16)

    qsp = pl.BlockSpec((N_Q_HEADS, WINDOW, ATT_HEAD_DIM), lambda n: (0, n, 0))
    prev = pl.BlockSpec((N_KV_HEADS, WINDOW, ATT_HEAD_DIM), lambda n: (0, jnp.maximum(n - 1, 0), 0))
    cur = pl.BlockSpec((N_KV_HEADS, WINDOW, ATT_HEAD_DIM), lambda n: (0, n, 0))
    return _pcall(body, out_shape=[_sds(qt.shape, BF16)], grid=(nb,),
                  in_specs=[qsp, prev, cur, prev, cur, pl.BlockSpec((N_KV_HEADS, rows, 1), lambda n: (0, 0, 0))],
                  out_specs=[qsp], args=[qt, kt, kt, vt, vt, sink_rows], name=name, ride=ride)


def attn_bwd(qt, kt, vt, sink_rows, dot_, name, ride=None):
    s = qt.shape[1]
    nb = s // WINDOW
    rows = Q_PER_KV * WINDOW

    def body(q_ref, kp_ref, kc_ref, vp_ref, vc_ref, sk_ref, do_ref, dq_ref, dk_ref, dv_ref, ds_ref, kacc, vacc):
        n = pl.program_id(0)

        @pl.when(n == 0)
        def _():
            kacc[...] = jnp.zeros(kacc.shape, F32)
            vacc[...] = jnp.zeros(vacc.shape, F32)

        @pl.when(n < nb)
        def _():
            for h in range(N_KV_HEADS):
                heads = pl.ds(h * Q_PER_KV, Q_PER_KV)
                q = q_ref[heads].reshape(rows, ATT_HEAD_DIM)
                do = do_ref[heads].reshape(rows, ATT_HEAD_DIM)
                kp, kc, vp, vc = kp_ref[h], kc_ref[h], vp_ref[h], vc_ref[h]
                pp, pc, ps = _attn_probs(q, kp, kc, sk_ref[h], n)
                dpp = _dot(do, vp, NT)
                dpc = _dot(do, vc, NT)
                delta = jnp.sum(pp * dpp, axis=1, keepdims=True) + jnp.sum(pc * dpc, axis=1, keepdims=True)
                dsp = (pp * (dpp - delta)).astype(BF16)
                dsc = (pc * (dpc - delta)).astype(BF16)
                dq = _dot(dsp, kp) + _dot(dsc, kc)
                dq_ref[heads] = dq.reshape(Q_PER_KV, WINDOW, ATT_HEAD_DIM)
                dk_ref[h] = kacc[h] + _dot(dsp, q, TN)
                dv_ref[h] = vacc[h] + _dot(pp.astype(BF16), do, TN)
                kacc[h] = _dot(dsc, q, TN)
                vacc[h] = _dot(pc.astype(BF16), do, TN)
                dsk = -ps * delta
                sub = lax.broadcasted_iota(jnp.int32, (8, LANES), 0)
                tile = jnp.zeros((8, LANES), F32)
                for j in range(Q_PER_KV):
                    tile += jnp.where(sub == j, jnp.sum(dsk[j * WINDOW:(j + 1) * WINDOW, :], axis=0, keepdims=True),
                                      0.0)
                ds_ref[h, 0] = tile

        @pl.when(n == nb)
        def _():
            dk_ref[...] = kacc[...]
            dv_ref[...] = vacc[...]
            ds_ref[...] = jnp.zeros(ds_ref.shape, F32)

    last = nb - 1
    qsp = pl.BlockSpec((N_Q_HEADS, WINDOW, ATT_HEAD_DIM), lambda n: (0, jnp.minimum(n, last), 0))
    prev = pl.BlockSpec((N_KV_HEADS, WINDOW, ATT_HEAD_DIM), lambda n: (0, jnp.clip(n - 1, 0, last), 0))
    cur = pl.BlockSpec((N_KV_HEADS, WINDOW, ATT_HEAD_DIM), lambda n: (0, jnp.minimum(n, last), 0))
    dkv = pl.BlockSpec((N_KV_HEADS, WINDOW, ATT_HEAD_DIM), lambda n: (0, jnp.maximum(n - 1, 0), 0))
    f = lambda shape: _sds(shape, F32)
    acc = pltpu.VMEM((N_KV_HEADS, WINDOW, ATT_HEAD_DIM), F32)
    return _pcall(
        body, out_shape=[f(qt.shape), f(kt.shape), f(vt.shape), f((N_KV_HEADS, nb + 1, 8, LANES))],
        grid=(nb + 1,),
        in_specs=[qsp, prev, cur, prev, cur, pl.BlockSpec((N_KV_HEADS, rows, 1), lambda n: (0, 0, 0)), qsp],
        out_specs=[qsp, dkv, dkv, pl.BlockSpec((N_KV_HEADS, 1, 8, LANES), lambda n: (0, n, 0, 0))],
        scratch_shapes=[acc, acc], args=[qt, kt, kt, vt, vt, sink_rows, dot_], name=name, ride=ride)


def loss_head(x, w, tgt, name):
    s, d = x.shape
    tm = _row_tile(s, 256)

    def body(x_ref, w_ref, t_ref, loss_ref, dx_ref, dw_ref):
        i = pl.program_id(0)
        xf = x_ref[...]
        wv = w_ref[...]
        r = lax.rsqrt(jnp.mean(xf * xf, axis=-1, keepdims=True) + EPS)
        xhat = xf * r
        e = xhat * wv - t_ref[...]
        part = 0.5 * jnp.sum(jnp.mean(e * e, axis=-1, keepdims=True), axis=0, keepdims=True)
        dy = e * (1.0 / d)
        dxhat = dy * wv
        dx_ref[...] = r * (dxhat - xhat * jnp.mean(dxhat * xhat, axis=-1, keepdims=True))
        col = jnp.sum(dy * xhat, axis=0, keepdims=True)

        @pl.when(i == 0)
        def _():
            loss_ref[...] = jnp.broadcast_to(part, (1, LANES))
            dw_ref[...] = col

        @pl.when(i > 0)
        def _():
            loss_ref[...] += jnp.broadcast_to(part, (1, LANES))
            dw_ref[...] += col

    row = pl.BlockSpec((tm, d), lambda i: (i, 0))
    vec = pl.BlockSpec((1, d), lambda i: (0, 0))
    return _pcall(body, out_shape=[_sds((1, LANES), F32), _sds((s, d), F32), _sds((1, d), F32)], grid=(s // tm,),
                  in_specs=[row, vec, row], out_specs=[pl.BlockSpec((1, LANES), lambda i: (0, 0)), row, vec],
                  args=[x, w.reshape(1, d), tgt], name=name)


ELEMWISE_TILE = 720 * 1024


def _tile_rows(r, c, max_elems=262144, mult=16):
    best = None
    for t in range(mult, r + 1, mult):
        if r % t == 0 and t * c <= max_elems:
            best = t
    return best or r


def add_pair(xhs, ps, c_idx, name):
    n = len(xhs)
    _, r, c = xhs[0].shape
    tr = _tile_rows(r, c, max_elems=ELEMWISE_TILE)

    def body(c_ref, *refs):
        for x_ref, p_ref, o_ref in zip(refs[:n], refs[n:2 * n], refs[2 * n:]):
            o_ref[...] = (x_ref[0].astype(F32) + p_ref[...].astype(F32)).astype(BF16)

    blk = pl.BlockSpec((tr, c), lambda i, cr: (i, 0))
    return pl.pallas_call(
        body, out_shape=tuple([_sds((r, c), BF16)] * n),
        grid_spec=pltpu.PrefetchScalarGridSpec(
            num_scalar_prefetch=1, grid=(r // tr,),
            in_specs=[pl.BlockSpec((1, tr, c), lambda i, cr: (cr[0], i, 0))] * n + [blk] * n,
            out_specs=tuple([blk] * n)),
        name=name, compiler_params=_cp(1))(c_idx, *xhs, *ps)


def sum_chips(qs, owns, chip_idx, name):
    n = len(qs)
    _, r, c = qs[0].shape
    tr = _tile_rows(r, c, max_elems=ELEMWISE_TILE // max(1, n - 1))

    def body(k_ref, *refs):
        k = k_ref[0]
        for q_ref, own_ref, o_ref in zip(refs[:n], refs[n:2 * n], refs[2 * n:]):
            mine = own_ref[0].astype(F32)
            tot = None
            for j in range(N_CHIPS):
                term = jnp.where(k == j, mine, q_ref[j].astype(F32))
                tot = term if tot is None else tot + term
            o_ref[...] = tot

    return pl.pallas_call(
        body, out_shape=tuple([_sds((r, c), F32)] * n),
        grid_spec=pltpu.PrefetchScalarGridSpec(
            num_scalar_prefetch=1, grid=(r // tr,),
            in_specs=([pl.BlockSpec((N_CHIPS, tr, c), lambda i, kr: (0, i, 0))] * n
                      + [pl.BlockSpec((1, tr, c), lambda i, kr: (kr[0], i, 0))] * n),
            out_specs=tuple([pl.BlockSpec((tr, c), lambda i, kr: (i, 0))] * n)),
        name=name, compiler_params=_cp(1))(chip_idx, *qs, *owns)


def adamw(w, g, m, v, name):
    r, c = w.shape
    tr = _tile_rows(r, c, mult=8)
    c1 = 1.0 / (1.0 - ADAM_B1 ** ADAM_STEP)
    c2 = 1.0 / (1.0 - ADAM_B2 ** ADAM_STEP)

    def body(w_ref, g_ref, m_ref, v_ref, d_ref, mo_ref, vo_ref):
        gf = g_ref[...]
        mn = ADAM_B1 * m_ref[...] + (1.0 - ADAM_B1) * gf
        vn = ADAM_B2 * v_ref[...] + (1.0 - ADAM_B2) * (gf * gf)
        mo_ref[...] = mn
        vo_ref[...] = vn
        d_ref[...] = -ADAM_LR * ((mn * c1) / (jnp.sqrt(vn * c2) + ADAM_EPS) + ADAM_WD * w_ref[...])

    blk = pl.BlockSpec((tr, c), lambda i: (i, 0))
    out = _sds((r, c), F32)
    return _pcall(body, out_shape=[out, out, out], grid=(r // tr,), in_specs=[blk] * 4, out_specs=[blk] * 3,
                  args=[w, g, m, v], name=name)


WEIGHTS = ['norm_w', 'ffn_w_gate', 'ffn_w_up', 'ffn_w_down', 'ssm_w_in', 'ssm_conv_w', 'ssm_conv_b', 'ssm_dt_bias',
           'ssm_a_log', 'ssm_d', 'ssm_norm_w', 'ssm_w_out', 'kv_norm_w', 'w_k', 'b_k', 'w_v', 'b_v', 'attn_w_q',
           'attn_b_q', 'attn_sinks', 'attn_w_o', 'attn_b_o', 'final_norm_w']
BIG = ['ffn_w_gate', 'ffn_w_up', 'ffn_w_down', 'ssm_w_in', 'ssm_w_out', 'w_k', 'w_v', 'attn_w_q', 'attn_w_o']
TRANSPOSED = ('ffn_w_gate', 'ffn_w_up', 'ssm_w_in')
SMALL = [n for n in WEIGHTS if n not in BIG]
SMALL_SHARDED = {'norm_w': 2, 'ssm_conv_w': 2, 'ssm_conv_b': 1, 'ssm_norm_w': 1}
ROW_ALIGN = 8 * LANES


def _pack_rows(parts):
    flat = jnp.concatenate([p.reshape(-1).astype(F32) for p in parts])
    pad = (-flat.size) % ROW_ALIGN
    return jnp.pad(flat, (0, pad)).reshape(-1, LANES)


def _unpack_rows(buf, shapes):
    flat = buf.reshape(-1)
    out, pos = [], 0
    for shp in shapes:
        size = math.prod(shp)
        out.append(flat[pos:pos + size].reshape(shp))
        pos += size
    return out


def _as2d(a):
    return a.reshape(-1, a.shape[-1])


def _heads_major(t, n_heads):
    s = t.shape[0]
    return t.reshape(s, n_heads, ATT_HEAD_DIM).transpose(1, 0, 2)


def _tokens_major(t):
    h, s, dh = t.shape
    return t.transpose(1, 0, 2).reshape(s, h * dh)


def _pad_lanes(v):
    return jnp.pad(v.reshape(1, -1), ((0, 0), (0, LANES - v.size)))


def _chips_first(t):
    return t.swapaxes(0, 1).reshape((-1,) + t.shape[3:])


def _parts_first(t, rows):
    return t.reshape((N_CHIPS, N_CORES, rows) + t.shape[1:]).swapaxes(0, 1)


def kernel(*args):
    names = (['x'] + WEIGHTS + ['loss_target'] + ['m_' + n for n in WEIGHTS] + ['v_' + n for n in WEIGHTS])
    a = dict(zip(names, args))
    for n in TRANSPOSED:
        for pre in ('', 'm_', 'v_'):
            a[pre + n] = a[pre + n].swapaxes(-1, -2)
    xi, yi, ci = lax.axis_index("x"), lax.axis_index("y"), lax.axis_index("c")
    chip = 2 * xi + yi
    south = ci == 0
    c_idx = jnp.reshape(ci, (1,)).astype(jnp.int32)
    chip_idx = jnp.reshape(chip, (1,)).astype(jnp.int32)
    x0 = a['x'][0]
    s = x0.shape[0]
    cos, sin = rope_tables(s)

    def own_slot(full, mine):
        return lax.dynamic_update_slice_in_dim(full, mine[:, None], chip, axis=1)

    def ffn_shard(l, i):
        return [a[n][l, i].astype(BF16).reshape(N_CORES, FF_PART, D_MODEL)
                for n in ('ffn_w_gate', 'ffn_w_up', 'ffn_w_down')]

    def own_slots(fulls, mines):
        return [own_slot(f, m) for f, m in zip(fulls, mines)]
    small_names = list(SMALL_SHARDED)
    small_sh = _pack_rows([a[n] for n in small_names])
    small_sh = small_sh.reshape(N_CORES, small_sh.shape[0] // 2, LANES)
    sh00, sh01, sh10, sh11 = ffn_shard(0, 0), ffn_shard(0, 1), ffn_shard(1, 0), ffn_shard(1, 1)
    w_in_sh = jnp.pad(a['ssm_w_in'][0], ((0, IN_SHARD_PAD - IN_SHARD), (0, 0))).astype(BF16).reshape(
        N_CORES, IN_SHARD_PAD // 2, D_MODEL)
    w_out_sh = a['ssm_w_out'][0].astype(BF16).reshape(N_CORES, 256, D_MODEL)
    attn_sh = jnp.stack([a['attn_w_q'][0], a['attn_w_o'][0]]).astype(BF16)
    kv_sh = jnp.stack([a['w_k'], a['w_v']]).astype(BF16)
    in_flight, all_started = split_start(
        [sh00 + [small_sh], [w_in_sh, kv_sh], [w_out_sh], sh01, sh10, [attn_sh], sh11], "gather", "gather_start")

    def arrive(idx, after, tag):
        return forward_cores(split_arrive(in_flight[idx], "gather", after, "gather_arrive_" + tag))

    first = run_exchange(arrive(0, all_started, "first"), "gather_hop_first")
    w00 = own_slots(first[:3], sh00)
    smalls = own_slot(first[3], small_sh)
    p = {}
    per_chip = [_unpack_rows(smalls[:, k], [a[n].shape for n in small_names]) for k in range(N_CHIPS)]
    for idx, n in enumerate(small_names):
        p[n] = jnp.concatenate([per_chip[k][idx] for k in range(N_CHIPS)], axis=SMALL_SHARDED[n])
    nw = p['norm_w']
    conv_w, conv_b, ssm_nw = p['ssm_conv_w'][0], p['ssm_conv_b'][0], p['ssm_norm_w'][0].reshape(1, D_INNER)

    h00 = rmsnorm_fwd(x0, nw[0, 0], "norm_in")
    x1, h01, gu00 = ffn_fwd(h00, x0, *w00, [nw[0, 1]], "ffn_fwd_00")
    w_in_g, kv_g = run_exchange(arrive(1, x1, "in"), "gather_hop_in")
    w_in_t = _chips_first(own_slot(w_in_g, w_in_sh)).reshape(N_CHIPS, IN_SHARD_PAD, D_MODEL)[:, :IN_SHARD].reshape(
        IN_PROJ_DIM, D_MODEL)
    w_dt_t = jnp.pad(w_in_t[D_INNER + CONV_DIM:], ((0, LANES - SSM_HEADS), (0, 0)))
    kv_g = own_slot(kv_g, kv_sh)
    w_k, w_v = kv_g[0].reshape(D_MODEL, KV_DIM), kv_g[1].reshape(D_MODEL, KV_DIM)

    zz = mm_nt(h01, w_in_t, "ssm_in_z", n=D_INNER)
    xbc = mm_nt(h01, w_in_t, "ssm_in_xbc", n=CONV_DIM, row0=D_INNER)
    dtp = mm_nt(h01, w_dt_t, "ssm_in_dt")
    act = conv_fwd(xbc, conv_w, conv_b, "ssm_conv")
    bias_p = _pad_lanes(a['ssm_dt_bias'][0])
    a_p = _pad_lanes(-jnp.exp(a['ssm_a_log'][0]))
    d_p = _pad_lanes(a['ssm_d'][0])
    (yn, y_pre, states), (w_out_g,) = ssd_fwd(act, zz, dtp, bias_p, a_p, d_p, ssm_nw, "ssd_fwd",
                                              ride=arrive(2, act, "out"))
    w_out = _chips_first(own_slot(w_out_g, w_out_sh))
    (x2, h02), w01 = mm_res(yn, w_out, x1, "ssm_out", norm_ws=[nw[0, 2]], ride=arrive(3, yn, "01"))
    w01 = own_slots(w01, sh01)
    x3, hkv, h10, gu01 = ffn_fwd(h02, x2, *w01, [a['kv_norm_w'], nw[1, 0]], "ffn_fwd_01")
    w10 = own_slots(run_exchange(arrive(4, x3, "10"), "gather_hop_10"), sh10)

    k_rot = mm_nn(hkv, w_k, "kv_k", bias=a['b_k'], out_dtype=BF16, rope=(cos, sin, 1.0))
    v = mm_nn(hkv, w_v, "kv_v", bias=a['b_v'], out_dtype=BF16)
    kt = _heads_major(k_rot, N_KV_HEADS)
    vt = _heads_major(v, N_KV_HEADS)

    (x4, h11, gu10), (attn_g,) = ffn_fwd(h10, x3, *w10, [nw[1, 1]], "ffn_fwd_10", ride=arrive(5, v, "attn"))
    attn_g = own_slot(attn_g, attn_sh)
    w_q, w_o = attn_g[0].reshape(D_MODEL, D_MODEL), attn_g[1].reshape(D_MODEL, D_MODEL)
    scale = 1.0 / math.sqrt(ATT_HEAD_DIM)
    q_rot = mm_nn(h11, w_q, "attn_q", bias=a['attn_b_q'][0], out_dtype=BF16, rope=(cos, sin, scale))
    qt = _heads_major(q_rot, N_Q_HEADS)
    sink_rows = jnp.repeat(a['attn_sinks'][0].reshape(N_KV_HEADS, Q_PER_KV), WINDOW, axis=1).reshape(
        N_KV_HEADS, Q_PER_KV * WINDOW, 1)
    (ot,) = attn_fwd(qt, kt, vt, sink_rows, "attn_fwd")
    o = _tokens_major(ot)
    (x5, h12), w11 = mm_res(o, w_o, x4, "attn_out", bias=a['attn_b_o'][0], norm_ws=[nw[1, 2]],
                            ride=arrive(6, ot, "11"))
    w11 = own_slots(w11, sh11)
    x6, gu11 = ffn_fwd(h12, x5, *w11, [], "ffn_fwd_11")

    loss_v, dx6, d_final = loss_head(x6, a['final_norm_w'], a['loss_target'][0], "loss_head")
    g = {'final_norm_w': d_final[0]}

    def same_shape(xs, ys):
        runs = []
        for xv, yv in zip(xs, ys):
            if runs and runs[-1][0][0].shape == xv.shape:
                runs[-1][0].append(xv)
                runs[-1][1].append(yv)
            else:
                runs.append(([xv], [yv]))
        return runs

    def pre_reduce(grads, sib, tag):
        out = []
        for idx, (grp, sbs) in enumerate(same_shape(grads, list(sib))):
            ts = add_pair([gr.reshape(2, -1, gr.shape[-1]) for gr in grp], [_as2d(sb) for sb in sbs], c_idx,
                          "rs_add_%s_%d" % (tag, idx))
            out += [t.reshape(gr.shape[1:]) for t, gr in zip(ts, grp)]
        return out

    def chip_sum(landed, parts, tag):
        out = []
        for idx, (qs, owns) in enumerate(same_shape(list(landed), parts)):
            ts = sum_chips([q.reshape(N_CHIPS, -1, q.shape[-1]) for q in qs],
                           [own.reshape(N_CHIPS, -1, own.shape[-1]) for own in owns], chip_idx,
                           "rs_sum_%s_%d" % (tag, idx))
            out += [t.reshape(q.shape[1:]) for t, q in zip(ts, qs)]
        return out

    dnw = [[None] * 3 for _ in range(2)]
    sums = {}

    def trade(key):
        return swap_cores(sums[key], False)

    dx5, dnw12, *g11 = ffn_bwd(dx6, h12, x5, nw[1, 2], gu11, *w11, "ffn_bwd_11")
    dnw[1][2] = dnw12[0]
    (d_wo, g['attn_b_o']), sib11 = mm_tn(o, dx5, "attn_dwo", col_sum=True, ride=swap_cores(g11, True))
    t11 = pre_reduce(g11, sib11, "11")
    do = mm_nt(dx5, w_o, "attn_do", out_dtype=BF16)
    (dqt, dkt, dvt, dsink), land11 = attn_bwd(qt, kt, vt, sink_rows, _heads_major(do, N_Q_HEADS), "attn_bwd",
                                             ride=scatter_chips(t11[:2]))
    g['attn_sinks'] = jnp.sum(dsink[:, :, :Q_PER_KV, 0], axis=1).reshape(N_Q_HEADS)
    dq_pre = rope_apply(_tokens_major(dqt), cos, sin, "rope_dq", inverse=True, scale=scale, out_dtype=F32)
    d_wq, g['attn_b_q'] = mm_tn(h11, dq_pre, "attn_dwq", col_sum=True)
    g_attn = [jnp.stack([d_wq.reshape(N_CHIPS, 256, D_MODEL), d_wo.reshape(N_CHIPS, 256, D_MODEL)])]
    (dx4, dnw[1][1]), sib_attn = mm_rms_bwd([(dq_pre, 0, w_q, 0, D_MODEL, "nt")], dx5, x4, nw[1, 1], "attn_bwd_dh",
                                            ride=swap_cores(g_attn, True))
    t_attn = pre_reduce(g_attn, sib_attn, "attn")
    (dx3, dnw10, *g10), landed = ffn_bwd(dx4, h10, x3, nw[1, 0], gu10, *w10, "ffn_bwd_10",
                                         ride=scatter_chips(t_attn + t11[2:]))
    dnw[1][0] = dnw10[0]
    sums['attn'] = chip_sum(landed[:1], t_attn, "attn")
    sums['11'] = chip_sum(list(land11) + list(landed[1:]), t11, "11")
    dk_pre = rope_apply(_tokens_major(dkt), cos, sin, "rope_dk", inverse=True, out_dtype=F32)
    dv = _tokens_major(dvt)
    (d_wk, g['b_k']), sib10 = mm_tn(hkv, dk_pre, "kv_dwk", col_sum=True, ride=swap_cores(g10, True))
    t10 = pre_reduce(g10, sib10, "10")
    d_wv, g['b_v'] = mm_tn(hkv, dv, "kv_dwv", col_sum=True)
    g_kv = [jnp.stack([d_wk.reshape(N_CHIPS, 256, KV_DIM), d_wv.reshape(N_CHIPS, 256, KV_DIM)])]
    (dx3, g['kv_norm_w']), sib_kv = mm_rms_bwd(
        [(dk_pre, 0, w_k, 0, KV_DIM, "nt"), (dv, 0, w_v, 0, KV_DIM, "nt")], dx3, x3, a['kv_norm_w'], "kv_bwd_dh",
        ride=swap_cores(g_kv, True))
    t_kv = pre_reduce(g_kv, sib_kv, "kv")
    (dx2, dnw02, *g01), landed = ffn_bwd(dx3, h02, x2, nw[0, 2], gu01, *w01, "ffn_bwd_01",
                                         ride=join(scatter_chips(t10 + t_kv), trade('11'), trade('attn')))
    dnw[0][2] = dnw02[0]
    sums['10'] = chip_sum(landed[:3], t10, "10")
    sums['kv'] = chip_sum(landed[3:4], t_kv, "kv")
    theirs = {'11': landed[4:7], 'attn': landed[7:]}
    d_wout, sib01 = mm_tn(yn, dx2, "ssm_dwout", ride=swap_cores(g01, True))
    t01 = pre_reduce(g01, sib01, "01")
    dyn = mm_nt(dx2, w_out, "ssm_dyn")
    (dxs, db_, dc_, dz, ddt, d_ssm_nw, d_bias, d_a, d_d), landed = ssd_bwd(
        dyn, act, zz, y_pre, states, dtp, bias_p, a_p, d_p, ssm_nw, "ssd_bwd",
        ride=join(scatter_chips(t01[:2]), trade('10'), trade('kv')))
    land01 = list(landed[:2])
    theirs['10'], theirs['kv'] = landed[2:5], landed[5:]
    g['ssm_norm_w'] = d_ssm_nw[:SSM_GROUPS].reshape(D_INNER)
    g['ssm_dt_bias'] = d_bias[0, :SSM_HEADS]
    g['ssm_a_log'] = d_a[0, :SSM_HEADS] * a_p[0, :SSM_HEADS]
    g['ssm_d'] = d_d[0, :SSM_HEADS]
    (dxbc, g['ssm_conv_w'], g['ssm_conv_b']), landed = conv_bwd(dxs, db_, dc_, xbc, conv_w, conv_b, "ssm_conv_bwd",
                                                                ride=scatter_chips(t01[2:]))
    sums['01'] = chip_sum(land01 + list(landed), t01, "01")
    d_win = mm_tn(dz, h01, "ssm_dwz", rows=IN_PROJ_DIM)
    d_win = mm_tn(dxbc, h01, "ssm_dwxbc", into=d_win, rows=IN_PROJ_DIM, row0=D_INNER)
    d_win = mm_tn(ddt, h01, "ssm_dwdt", into=d_win, rows=IN_PROJ_DIM, row0=D_INNER + CONV_DIM, m_valid=SSM_HEADS)
    d_win = jnp.pad(d_win.reshape(N_CHIPS, IN_SHARD, D_MODEL), ((0, 0), (0, IN_SHARD_PAD - IN_SHARD), (0, 0)))
    g_ssm = [_parts_first(d_win.reshape(-1, D_MODEL), IN_SHARD_PAD // 2), _parts_first(d_wout, 256)]
    kb = 1024
    terms = ([(dz, j, w_in_t, j, kb, "nn") for j in range(D_INNER // kb)]
             + [(dxbc, j, w_in_t, D_INNER // kb + j, kb, "nn") for j in range(CONV_DIM // kb)]
             + [(ddt, 0, w_dt_t, 0, LANES, "nn")])
    (dx1, dnw[0][1]), sib_ssm = mm_rms_bwd(terms, dx2, x1, nw[0, 1], "ssm_bwd_dh", ride=swap_cores(g_ssm, True))
    t_ssm = pre_reduce(g_ssm, sib_ssm, "ssm")
    (grad_x, dnw00, *g00), landed = ffn_bwd(dx1, h00, x0, nw[0, 0], gu00, *w00, "ffn_bwd_00",
                                            ride=join(scatter_chips(t_ssm), trade('01')))
    dnw[0][0] = dnw00[0]
    sums['ssm'] = chip_sum(landed[:2], t_ssm, "ssm")
    theirs['01'] = landed[2:]
    landed = run_exchange(join(swap_cores(g00, True), trade('ssm')), "rs_swap_00")
    t00 = pre_reduce(g00, landed[:3], "00")
    theirs['ssm'] = landed[3:]

    def both(key):
        return [(jnp.where(south, m_, t_), jnp.where(south, t_, m_)) for m_, t_ in zip(sums[key], theirs[key])]

    g['norm_w'] = jnp.stack([jnp.stack(r) for r in dnw])
    red = all_reduce_small(_pack_rows([g[n] for n in SMALL] + [loss_v[0, :1]]), "reduce_vectors")

    t00 = lax.optimization_barrier((red, t00))[1]
    (flight00,), flying = split_start([t00], "scatter", "rs_scatter_00_start")

    def held(val):
        return lax.optimization_barrier((flying, val))[1]

    delta, new_m, new_v, gw = {}, {}, {}, {}
    ffn_names = ('ffn_w_gate', 'ffn_w_up', 'ffn_w_down')
    full = {key: both(key) for key in ('attn', 'kv', 'ssm')}
    lo, hi = full['attn'][0]
    gw['attn_w_q'], gw['attn_w_o'] = lo[None], hi[None]
    lo, hi = full['kv'][0]
    gw['w_k'], gw['w_v'] = lo, hi
    lo, hi = full['ssm'][0]
    gw['ssm_w_in'] = jnp.concatenate([lo, hi], axis=0)[:IN_SHARD][None]
    lo, hi = full['ssm'][1]
    gw['ssm_w_out'] = jnp.concatenate([lo, hi], axis=0)[None]

    *small_sums, loss = _unpack_rows(red, [g[n].shape for n in SMALL] + [()])
    for n, t in zip(SMALL, small_sums):
        if n in SMALL_SHARDED:
            ax = SMALL_SHARDED[n] - (a[n].ndim - t.ndim)
            width = a[n].shape[SMALL_SHARDED[n]]
            t = lax.dynamic_slice_in_dim(t, chip * width, width, axis=ax)
        gw[n] = t.reshape(a[n].shape)

    def update(n):
        d, mo, vo = adamw(_as2d(a[n]), held(_as2d(gw[n])), _as2d(a['m_' + n]), _as2d(a['v_' + n]), "adamw_" + n)
        delta[n], new_m[n], new_v[n] = d.reshape(a[n].shape), mo.reshape(a[n].shape), vo.reshape(a[n].shape)

    for n in BIG:
        if n not in ffn_names:
            update(n)
    shapes = [a[n].shape for n in SMALL]
    packed = [_pack_rows([src[n] for n in SMALL]) for src in
              (a, gw, {n: a['m_' + n] for n in SMALL}, {n: a['v_' + n] for n in SMALL})]
    outs = adamw(*packed, "adamw_vectors")
    for dst, buf in zip((delta, new_m, new_v), outs):
        for n, t in zip(SMALL, _unpack_rows(buf, shapes)):
            dst[n] = t
    for key in ('01', '10', '11'):
        sums[key] = held(list(sums[key]))
    rest = [both(key) for key in ('01', '10', '11')]
    done = lax.optimization_barrier((outs[0], [delta[n] for n in BIG if n not in ffn_names], rest))[0]
    land00 = split_arrive(flight00, "scatter", done, "rs_scatter_00_arrive")
    sums['00'] = chip_sum(land00, t00, "00")
    theirs['00'] = run_exchange(trade('00'), "rs_trade_00")
    blocks = [both('00')] + rest
    for t, n in enumerate(ffn_names):
        gw[n] = jnp.concatenate([piece for blk in blocks for piece in blk[t]], axis=0).reshape(a[n].shape)
        update(n)
    for n in TRANSPOSED:
        for dst in (gw, delta, new_m, new_v):
            dst[n] = dst[n].swapaxes(-1, -2)

    return (loss, grad_x[None], *[gw[n] for n in WEIGHTS], *[delta[n] for n in WEIGHTS],
            *[new_m[n] for n in WEIGHTS], *[new_v[n] for n in WEIGHTS])
```
